```python
import math
import jax, jax.numpy as jnp
from jax import lax
import numpy as np

D_MODEL = 1024
BATCH = 16
SEQ = 2048
DEPTH = 2

HEAD_DIM = 64
N_SB_HEADS = 4
N_FOX_HEADS = 4
N_DIL_HEADS = 4
N_LRU_BLOCKS = 4
LRU_BLOCK = HEAD_DIM
LRU_WIDTH = N_LRU_BLOCKS * LRU_BLOCK
SB_W = N_SB_HEADS * HEAD_DIM
FOX_W = N_FOX_HEADS * HEAD_DIM
DIL_W = N_DIL_HEADS * HEAD_DIM
MIX_WIDTH = SB_W + FOX_W + DIL_W + LRU_WIDTH
N_IN = 3 * SB_W + 3 * FOX_W + N_FOX_HEADS + 3 * DIL_W + 2 * LRU_WIDTH
BLOCK = 128
DILATED_PATTERNS = ((128, 1), (512, 4), (2048, 16))
LRU_CONV_WIDTH = 4
LRU_C = 8.0
FFN_CONV_WIDTH = 3
D_FF = 2816
N_MEM = 256
N_CROSS_HEADS = 4
CROSS_W = N_CROSS_HEADS * HEAD_DIM
NUM_BUCKETS = 32
MAX_DISTANCE = 2048
EPS = 1e-6

kernel_name = "hymba_style_sb_fox_dilated_rglru_trunk"


def rms_norm(x, g):
    x32 = x.astype(jnp.float32)
    y = x32 * lax.rsqrt(jnp.mean(x32 * x32, axis=-1, keepdims=True) + EPS)
    return (y * g.astype(jnp.float32)).astype(x.dtype)


def causal_dwconv(x, w, b):
    k_w = w.shape[0]
    s = x.shape[1]
    xp = jnp.pad(x, ((0, 0), (k_w - 1, 0), (0, 0)))
    y = b
    for j in range(k_w):
        y = y + w[j] * xp[:, j:j + s]
    return y


def t5_bucket(dist):
    n = jnp.maximum(dist, 0)
    max_exact = NUM_BUCKETS // 2
    nf = jnp.maximum(n, 1).astype(jnp.float32)
    large = max_exact + (jnp.log(nf / max_exact) / math.log(MAX_DISTANCE / max_exact)
                         * (NUM_BUCKETS - max_exact)).astype(jnp.int32)
    large = jnp.minimum(large, NUM_BUCKETS - 1)
    return jnp.where(n < max_exact, n, large)


def split_cols(t, sizes):
    out, start = [], 0
    for n in sizes:
        out.append(t[..., start:start + n])
        start += n
    return out


def split_qkv(t, n_heads):
    b, s, _ = t.shape
    t = t.reshape(b, s, 3, n_heads, HEAD_DIM).transpose(2, 0, 3, 1, 4)
    return t[0] * HEAD_DIM ** -0.5, t[1], t[2]


def merge_heads(o):
    b, h, s, d = o.shape
    return o.transpose(0, 2, 1, 3).reshape(b, s, h * d)


def stick_breaking_attention(q, k, v):
    s_len = q.shape[2]
    outs = []
    for i in range(s_len // BLOCK):
        q0, q1 = i * BLOCK, (i + 1) * BLOCK
        z = jnp.einsum('bhqd,bhkd->bhqk', q[:, :, q0:q1], k[:, :, :q1]).astype(jnp.float32)
        strict = jnp.arange(q1)[None, :] < jnp.arange(q0, q1)[:, None]
        log_keep = jnp.where(strict, jax.nn.log_sigmoid(-z), 0.0)
        log_keep_after = lax.cumsum(log_keep, axis=3, reverse=True) - log_keep
        att = jnp.where(strict, jnp.exp(jax.nn.log_sigmoid(z) + log_keep_after), 0.0)
        outs.append(jnp.einsum('bhqk,bhkd->bhqd', att, v[:, :, :q1].astype(jnp.float32)))
    return jnp.concatenate(outs, axis=2)


def forgetting_attention(q, k, v, f_logit):
    s_len = q.shape[2]
    log_f = jax.nn.log_sigmoid(f_logit.astype(jnp.float32))
    cum = jnp.cumsum(log_f, axis=1).transpose(0, 2, 1)
    outs = []
    for i in range(s_len // BLOCK):
        q0, q1 = i * BLOCK, (i + 1) * BLOCK
        z = jnp.einsum('bhqd,bhkd->bhqk', q[:, :, q0:q1], k[:, :, :q1]).astype(jnp.float32)
        z = z + cum[:, :, q0:q1, None] - cum[:, :, None, :q1]
        causal = jnp.arange(q1)[None, :] <= jnp.arange(q0, q1)[:, None]
        p = jax.nn.softmax(jnp.where(causal, z, -jnp.inf), axis=-1)
        outs.append(jnp.einsum('bhqk,bhkd->bhqd', p, v[:, :, :q1].astype(jnp.float32)))
    return jnp.concatenate(outs, axis=2)


def dilated_branch(q, k, v, rel_bias, dil, steps):
    b, h, s, hd = q.shape
    L = s // dil
    qb_len = math.gcd(BLOCK, L)
    nb = L // qb_len

    def to_cls(t):
        return t.reshape(b, h, L, dil, hd).transpose(0, 1, 3, 2, 4)

    qc = to_cls(q).reshape(b, h, dil, nb, qb_len, hd)
    pad = ((0, 0), (0, 0), (0, 0), (steps, 0), (0, 0))
    kc = jnp.pad(to_cls(k), pad)
    vc = jnp.pad(to_cls(v), pad)
    idx = (jnp.arange(nb) * qb_len)[:, None] + jnp.arange(qb_len + steps)[None, :]
    kb = kc[:, :, :, idx]
    vb = vc[:, :, :, idx]
    sc = jnp.einsum('bhrnqd,bhrnkd->bhrnqk', qc, kb).astype(jnp.float32)
    qpos = (jnp.arange(nb) * qb_len)[:, None] + jnp.arange(qb_len)[None, :]
    kpos = idx - steps
    delta = qpos[:, :, None] - kpos[:, None, :]
    valid = (delta >= 0) & (delta <= steps) & (kpos[:, None, :] >= 0)
    bias = jnp.moveaxis(rel_bias.astype(jnp.float32)[t5_bucket(delta * dil)], -1, 0)[:, None]
    sc = jnp.where(valid, sc + bias, -jnp.inf)
    m = jnp.max(sc, axis=-1, keepdims=True)
    p = jnp.exp(sc - m)
    l = jnp.sum(p, axis=-1)
    o = jnp.einsum('bhrnqk,bhrnkd->bhrnqd', p, vb.astype(jnp.float32)) / l[..., None]
    lse = m[..., 0] + jnp.log(l)
    o = o.reshape(b, h, dil, L, hd).transpose(0, 1, 3, 2, 4).reshape(b, h, s, hd)
    lse = lse.reshape(b, h, dil, L).transpose(0, 1, 3, 2).reshape(b, h, s)
    return o, lse


def dilated_attention(q, k, v, rel_bias):
    outs, lses = [], []
    for window, dil in DILATED_PATTERNS:
        o, lse = dilated_branch(q, k, v, rel_bias, dil, window // dil)
        outs.append(o)
        lses.append(lse)
    wts = jax.nn.softmax(jnp.stack(lses), axis=0)
    return jnp.sum(wts[..., None] * jnp.stack(outs), axis=0)


def rg_lru_branch(x_br, gate_br, conv_w, conv_b, w_a, b_a, w_x, b_x, lam):
    b, s, c = x_br.shape
    f32 = jnp.float32
    xc = causal_dwconv(x_br.astype(f32), conv_w.astype(f32), conv_b.astype(f32))
    xg = xc.reshape(b, s, N_LRU_BLOCKS, LRU_BLOCK)
    r = jax.nn.sigmoid(jnp.einsum('bsgi,gij->bsgj', xg, w_a.astype(f32)).reshape(b, s, c) + b_a.astype(f32))
    i_gate = jax.nn.sigmoid(jnp.einsum('bsgi,gij->bsgj', xg, w_x.astype(f32)).reshape(b, s, c) + b_x.astype(f32))
    log_a = -LRU_C * r * jax.nn.softplus(-lam.astype(f32))
    a = jnp.exp(log_a)
    u = jnp.sqrt(-jnp.expm1(2.0 * log_a)) * (i_gate * xc)

    def combine(e1, e2):
        a1, b1 = e1
        a2, b2 = e2
        return a1 * a2, a2 * b1 + b2

    _, h = lax.associative_scan(combine, (a, u), axis=1)
    return h * jax.nn.gelu(gate_br.astype(f32), approximate=False)


def memory_cross_attention(h, mem_n, w_cq, w_ck, w_cv, w_co):
    b, s, _ = h.shape
    m = mem_n.shape[1]
    q = (h @ w_cq).reshape(b, s, N_CROSS_HEADS, HEAD_DIM).transpose(0, 2, 1, 3) * HEAD_DIM ** -0.5
    k = (mem_n @ w_ck).reshape(b, m, N_CROSS_HEADS, HEAD_DIM).transpose(0, 2, 1, 3)
    v = (mem_n @ w_cv).reshape(b, m, N_CROSS_HEADS, HEAD_DIM).transpose(0, 2, 1, 3)
    p = jax.nn.softmax(jnp.einsum('bhqd,bhkd->bhqk', q, k).astype(jnp.float32), axis=-1)
    o = jnp.einsum('bhqk,bhkd->bhqd', p, v.astype(jnp.float32))
    return merge_heads(o).astype(h.dtype) @ w_co


def _fwd_setup_inputs(seed: int = 0) -> dict:
    key = jax.random.key(seed)
    ks = iter(jax.random.split(key, 40))
    f32 = jnp.float32
    L = DEPTH

    def normal(shape, scale):
        return scale * jax.random.normal(next(ks), shape, f32)

    def gain(shape):
        return 1.0 + normal(shape, 0.02)

    u = jax.random.uniform(next(ks), (L, LRU_WIDTH), f32, 0.9, 0.999)
    a0 = u ** (1.0 / LRU_C)
    lru_lambda = jnp.log(a0) - jnp.log1p(-a0)
    return {
        "x": normal((BATCH, SEQ, D_MODEL), 1.0),
        "mem": normal((BATCH, N_MEM, D_MODEL), 1.0),
        "norm_mix_g": gain((L, D_MODEL)),
        "w_in": normal((L, D_MODEL, N_IN), D_MODEL ** -0.5),
        "b_forget": jax.random.uniform(next(ks), (L, N_FOX_HEADS), f32, 1.0, 4.0),
        "lru_conv_w": normal((L, LRU_CONV_WIDTH, LRU_WIDTH), LRU_CONV_WIDTH ** -0.5),
        "lru_conv_b": normal((L, LRU_WIDTH), 0.01),
        "lru_w_a": normal((L, N_LRU_BLOCKS, LRU_BLOCK, LRU_BLOCK), LRU_BLOCK ** -0.5),
        "lru_b_a": normal((L, LRU_WIDTH), 0.01),
        "lru_w_x": normal((L, N_LRU_BLOCKS, LRU_BLOCK, LRU_BLOCK), LRU_BLOCK ** -0.5),
        "lru_b_x": normal((L, LRU_WIDTH), 0.01),
        "lru_lambda": lru_lambda,
        "w_out": normal((L, MIX_WIDTH, D_MODEL), MIX_WIDTH ** -0.5),
        "norm_cross_g": gain((L, D_MODEL)),
        "norm_mem_g": gain((L, D_MODEL)),
        "w_cq": normal((L, D_MODEL, CROSS_W), D_MODEL ** -0.5),
        "w_ck": normal((L, D_MODEL, CROSS_W), D_MODEL ** -0.5),
        "w_cv": normal((L, D_MODEL, CROSS_W), D_MODEL ** -0.5),
        "w_co": normal((L, CROSS_W, D_MODEL), CROSS_W ** -0.5),
        "norm_ffn_g": gain((L, D_MODEL)),
        "w_up": normal((L, D_MODEL, 2 * D_FF), D_MODEL ** -0.5),
        "ffn_conv_w": normal((L, FFN_CONV_WIDTH, 2 * D_FF), FFN_CONV_WIDTH ** -0.5),
        "ffn_conv_b": normal((L, 2 * D_FF), 0.01),
        "w_down": normal((L, D_FF, D_MODEL), D_FF ** -0.5),
        "rel_bias": normal((NUM_BUCKETS, N_DIL_HEADS), 0.5),
        "final_norm_g": gain((D_MODEL,)),
    }


def _fwd_reference(x, mem, norm_mix_g, w_in, b_forget, lru_conv_w, lru_conv_b, lru_w_a, lru_b_a,
              lru_w_x, lru_b_x, lru_lambda, w_out, norm_cross_g, norm_mem_g, w_cq, w_ck, w_cv,
              w_co, norm_ffn_g, w_up, ffn_conv_w, ffn_conv_b, w_down, rel_bias, final_norm_g):
    col_sizes = [3 * SB_W, 3 * FOX_W, N_FOX_HEADS, 3 * DIL_W, LRU_WIDTH, LRU_WIDTH]
    for l in range(DEPTH):
        h = rms_norm(x, norm_mix_g[l])
        proj = h @ w_in[l]
        sb_qkv, fox_qkv, fox_f, dil_qkv, lru_x, lru_gate = split_cols(proj, col_sizes)
        o_sb = stick_breaking_attention(*split_qkv(sb_qkv, N_SB_HEADS))
        o_fox = forgetting_attention(*split_qkv(fox_qkv, N_FOX_HEADS), fox_f + b_forget[l])
        o_dil = dilated_attention(*split_qkv(dil_qkv, N_DIL_HEADS), rel_bias)
        o_lru = rg_lru_branch(lru_x, lru_gate, lru_conv_w[l], lru_conv_b[l], lru_w_a[l],
                              lru_b_a[l], lru_w_x[l], lru_b_x[l], lru_lambda[l])
        mixed = jnp.concatenate([merge_heads(o_sb), merge_heads(o_fox), merge_heads(o_dil), o_lru],
                                axis=-1).astype(x.dtype)
        x = x + mixed @ w_out[l]
        x = x + memory_cross_attention(rms_norm(x, norm_cross_g[l]), rms_norm(mem, norm_mem_g[l]),
                                       w_cq[l], w_ck[l], w_cv[l], w_co[l])
        hf = rms_norm(x, norm_ffn_g[l]) @ w_up[l]
        hf = causal_dwconv(hf, ffn_conv_w[l], ffn_conv_b[l])
        up, gate = hf[..., :D_FF], hf[..., D_FF:]
        x = x + (jax.nn.gelu(gate, approximate=False) * up).astype(x.dtype) @ w_down[l]
    return rms_norm(x, final_norm_g)


import jax as _jax
import jax.numpy as _jnp

TWIN_FORMAT = 'train_step'
FWD_PARAMS = ['x', 'mem', 'norm_mix_g', 'w_in', 'b_forget', 'lru_conv_w', 'lru_conv_b', 'lru_w_a', 'lru_b_a', 'lru_w_x', 'lru_b_x', 'lru_lambda', 'w_out', 'norm_cross_g', 'norm_mem_g', 'w_cq', 'w_ck', 'w_cv', 'w_co', 'norm_ffn_g', 'w_up', 'ffn_conv_w', 'ffn_conv_b', 'w_down', 'rel_bias', 'final_norm_g']
TWIN_WEIGHTS = ['norm_mix_g', 'w_in', 'b_forget', 'lru_conv_w', 'lru_conv_b', 'lru_w_a', 'lru_b_a', 'lru_w_x', 'lru_b_x', 'lru_lambda', 'w_out', 'norm_cross_g', 'norm_mem_g', 'w_cq', 'w_ck', 'w_cv', 'w_co', 'norm_ffn_g', 'w_up', 'ffn_conv_w', 'ffn_conv_b', 'w_down', 'rel_bias', 'final_norm_g']
TWIN_DIFF_INPUT = 'x'
TWIN_INPUTS = ['x', 'mem', 'norm_mix_g', 'w_in', 'b_forget', 'lru_conv_w', 'lru_conv_b', 'lru_w_a', 'lru_b_a', 'lru_w_x', 'lru_b_x', 'lru_lambda', 'w_out', 'norm_cross_g', 'norm_mem_g', 'w_cq', 'w_ck', 'w_cv', 'w_co', 'norm_ffn_g', 'w_up', 'ffn_conv_w', 'ffn_conv_b', 'w_down', 'rel_bias', 'final_norm_g', 'loss_target', 'm_norm_mix_g', 'm_w_in', 'm_b_forget', 'm_lru_conv_w', 'm_lru_conv_b', 'm_lru_w_a', 'm_lru_b_a', 'm_lru_w_x', 'm_lru_b_x', 'm_lru_lambda', 'm_w_out', 'm_norm_cross_g', 'm_norm_mem_g', 'm_w_cq', 'm_w_ck', 'm_w_cv', 'm_w_co', 'm_norm_ffn_g', 'm_w_up', 'm_ffn_conv_w', 'm_ffn_conv_b', 'm_w_down', 'm_rel_bias', 'm_final_norm_g', 'v_norm_mix_g', 'v_w_in', 'v_b_forget', 'v_lru_conv_w', 'v_lru_conv_b', 'v_lru_w_a', 'v_lru_b_a', 'v_lru_w_x', 'v_lru_b_x', 'v_lru_lambda', 'v_w_out', 'v_norm_cross_g', 'v_norm_mem_g', 'v_w_cq', 'v_w_ck', 'v_w_cv', 'v_w_co', 'v_norm_ffn_g', 'v_w_up', 'v_ffn_conv_w', 'v_ffn_conv_b', 'v_w_down', 'v_rel_bias', 'v_final_norm_g']
TWIN_OUTPUTS = ['loss', 'grad_x', 'grad_norm_mix_g', 'grad_w_in', 'grad_b_forget', 'grad_lru_conv_w', 'grad_lru_conv_b', 'grad_lru_w_a', 'grad_lru_b_a', 'grad_lru_w_x', 'grad_lru_b_x', 'grad_lru_lambda', 'grad_w_out', 'grad_norm_cross_g', 'grad_norm_mem_g', 'grad_w_cq', 'grad_w_ck', 'grad_w_cv', 'grad_w_co', 'grad_norm_ffn_g', 'grad_w_up', 'grad_ffn_conv_w', 'grad_ffn_conv_b', 'grad_w_down', 'grad_rel_bias', 'grad_final_norm_g', 'delta_norm_mix_g', 'delta_w_in', 'delta_b_forget', 'delta_lru_conv_w', 'delta_lru_conv_b', 'delta_lru_w_a', 'delta_lru_b_a', 'delta_lru_w_x', 'delta_lru_b_x', 'delta_lru_lambda', 'delta_w_out', 'delta_norm_cross_g', 'delta_norm_mem_g', 'delta_w_cq', 'delta_w_ck', 'delta_w_cv', 'delta_w_co', 'delta_norm_ffn_g', 'delta_w_up', 'delta_ffn_conv_w', 'delta_ffn_conv_b', 'delta_w_down', 'delta_rel_bias', 'delta_final_norm_g', 'new_m_norm_mix_g', 'new_m_w_in', 'new_m_b_forget', 'new_m_lru_conv_w', 'new_m_lru_conv_b', 'new_m_lru_w_a', 'new_m_lru_b_a', 'new_m_lru_w_x', 'new_m_lru_b_x', 'new_m_lru_lambda', 'new_m_w_out', 'new_m_norm_cross_g', 'new_m_norm_mem_g', 'new_m_w_cq', 'new_m_w_ck', 'new_m_w_cv', 'new_m_w_co', 'new_m_norm_ffn_g', 'new_m_w_up', 'new_m_ffn_conv_w', 'new_m_ffn_conv_b', 'new_m_w_down', 'new_m_rel_bias', 'new_m_final_norm_g', 'new_v_norm_mix_g', 'new_v_w_in', 'new_v_b_forget', 'new_v_lru_conv_w', 'new_v_lru_conv_b', 'new_v_lru_w_a', 'new_v_lru_b_a', 'new_v_lru_w_x', 'new_v_lru_b_x', 'new_v_lru_lambda', 'new_v_w_out', 'new_v_norm_cross_g', 'new_v_norm_mem_g', 'new_v_w_cq', 'new_v_w_ck', 'new_v_w_cv', 'new_v_w_co', 'new_v_norm_ffn_g', 'new_v_w_up', 'new_v_ffn_conv_w', 'new_v_ffn_conv_b', 'new_v_w_down', 'new_v_rel_bias', 'new_v_final_norm_g']
TWIN_LEAF_KINDS = {'loss': 'loss', 'grad_x': 'grad_x', 'grad_norm_mix_g': 'grad_w', 'grad_w_in': 'grad_w', 'grad_b_forget': 'grad_w', 'grad_lru_conv_w': 'grad_w', 'grad_lru_conv_b': 'grad_w', 'grad_lru_w_a': 'grad_w', 'grad_lru_b_a': 'grad_w', 'grad_lru_w_x': 'grad_w', 'grad_lru_b_x': 'grad_w', 'grad_lru_lambda': 'grad_w', 'grad_w_out': 'grad_w', 'grad_norm_cross_g': 'grad_w', 'grad_norm_mem_g': 'grad_w', 'grad_w_cq': 'grad_w', 'grad_w_ck': 'grad_w', 'grad_w_cv': 'grad_w', 'grad_w_co': 'grad_w', 'grad_norm_ffn_g': 'grad_w', 'grad_w_up': 'grad_w', 'grad_ffn_conv_w': 'grad_w', 'grad_ffn_conv_b': 'grad_w', 'grad_w_down': 'grad_w', 'grad_rel_bias': 'grad_w', 'grad_final_norm_g': 'grad_w', 'delta_norm_mix_g': 'delta_w', 'delta_w_in': 'delta_w', 'delta_b_forget': 'delta_w', 'delta_lru_conv_w': 'delta_w', 'delta_lru_conv_b': 'delta_w', 'delta_lru_w_a': 'delta_w', 'delta_lru_b_a': 'delta_w', 'delta_lru_w_x': 'delta_w', 'delta_lru_b_x': 'delta_w', 'delta_lru_lambda': 'delta_w', 'delta_w_out': 'delta_w', 'delta_norm_cross_g': 'delta_w', 'delta_norm_mem_g': 'delta_w', 'delta_w_cq': 'delta_w', 'delta_w_ck': 'delta_w', 'delta_w_cv': 'delta_w', 'delta_w_co': 'delta_w', 'delta_norm_ffn_g': 'delta_w', 'delta_w_up': 'delta_w', 'delta_ffn_conv_w': 'delta_w', 'delta_ffn_conv_b': 'delta_w', 'delta_w_down': 'delta_w', 'delta_rel_bias': 'delta_w', 'delta_final_norm_g': 'delta_w', 'new_m_norm_mix_g': 'new_m', 'new_m_w_in': 'new_m', 'new_m_b_forget': 'new_m', 'new_m_lru_conv_w': 'new_m', 'new_m_lru_conv_b': 'new_m', 'new_m_lru_w_a': 'new_m', 'new_m_lru_b_a': 'new_m', 'new_m_lru_w_x': 'new_m', 'new_m_lru_b_x': 'new_m', 'new_m_lru_lambda': 'new_m', 'new_m_w_out': 'new_m', 'new_m_norm_cross_g': 'new_m', 'new_m_norm_mem_g': 'new_m', 'new_m_w_cq': 'new_m', 'new_m_w_ck': 'new_m', 'new_m_w_cv': 'new_m', 'new_m_w_co': 'new_m', 'new_m_norm_ffn_g': 'new_m', 'new_m_w_up': 'new_m', 'new_m_ffn_conv_w': 'new_m', 'new_m_ffn_conv_b': 'new_m', 'new_m_w_down': 'new_m', 'new_m_rel_bias': 'new_m', 'new_m_final_norm_g': 'new_m', 'new_v_norm_mix_g': 'new_v', 'new_v_w_in': 'new_v', 'new_v_b_forget': 'new_v', 'new_v_lru_conv_w': 'new_v', 'new_v_lru_conv_b': 'new_v', 'new_v_lru_w_a': 'new_v', 'new_v_lru_b_a': 'new_v', 'new_v_lru_w_x': 'new_v', 'new_v_lru_b_x': 'new_v', 'new_v_lru_lambda': 'new_v', 'new_v_w_out': 'new_v', 'new_v_norm_cross_g': 'new_v', 'new_v_norm_mem_g': 'new_v', 'new_v_w_cq': 'new_v', 'new_v_w_ck': 'new_v', 'new_v_w_cv': 'new_v', 'new_v_w_co': 'new_v', 'new_v_norm_ffn_g': 'new_v', 'new_v_w_up': 'new_v', 'new_v_ffn_conv_w': 'new_v', 'new_v_ffn_conv_b': 'new_v', 'new_v_w_down': 'new_v', 'new_v_rel_bias': 'new_v', 'new_v_final_norm_g': 'new_v'}


def _forward(args):
    return _fwd_reference(*[args[k] for k in FWD_PARAMS])


def _output_shape():
    out = _jax.eval_shape(lambda: _forward(_fwd_setup_inputs(0)))
    return out.shape, out.dtype

N_MICROBATCH = 1
ADAM_LR = 0.001
ADAM_B1 = 0.9
ADAM_B2 = 0.999
ADAM_EPS = 1e-08
ADAM_WD = 0.01
ADAM_STEP = 10
PER_EXAMPLE_BATCH_AXIS = {'x': 0, 'mem': 0, 'loss_target': 0}
SHARED_INPUTS = []
_WEIGHT_DTYPES = {'norm_mix_g': _jnp.float32, 'w_in': _jnp.float32, 'b_forget': _jnp.float32, 'lru_conv_w': _jnp.float32, 'lru_conv_b': _jnp.float32, 'lru_w_a': _jnp.float32, 'lru_b_a': _jnp.float32, 'lru_w_x': _jnp.float32, 'lru_b_x': _jnp.float32, 'lru_lambda': _jnp.float32, 'w_out': _jnp.float32, 'norm_cross_g': _jnp.float32, 'norm_mem_g': _jnp.float32, 'w_cq': _jnp.float32, 'w_ck': _jnp.float32, 'w_cv': _jnp.float32, 'w_co': _jnp.float32, 'norm_ffn_g': _jnp.float32, 'w_up': _jnp.float32, 'ffn_conv_w': _jnp.float32, 'ffn_conv_b': _jnp.float32, 'w_down': _jnp.float32, 'rel_bias': _jnp.float32, 'final_norm_g': _jnp.float32}
MOMENT_SCALE = {'norm_mix_g': 1.022867e-01, 'w_in': 6.135688e-02, 'b_forget': 2.955066e-01, 'lru_conv_w': 1.038729e-01, 'lru_conv_b': 1.145046e+00, 'lru_w_a': 3.924040e-02, 'lru_b_a': 3.069439e-02, 'lru_w_x': 6.754256e-02, 'lru_b_x': 4.193486e-02, 'lru_lambda': 5.554843e-02, 'w_out': 7.923533e-02, 'norm_cross_g': 1.904768e-02, 'norm_mem_g': 3.162616e-02, 'w_cq': 3.837831e-02, 'w_ck': 3.897824e-02, 'w_cv': 4.448182e-02, 'w_co': 2.240606e-02, 'norm_ffn_g': 1.360993e-01, 'w_up': 5.665401e-02, 'ffn_conv_w': 5.596644e-02, 'ffn_conv_b': 6.049407e-02, 'w_down': 9.246962e-02, 'rel_bias': 6.686498e-02, 'final_norm_g': 3.197567e+01}


def _to_microbatches(a, axis):
    t = _jnp.moveaxis(a, axis, 0)
    t = t.reshape((N_MICROBATCH, t.shape[0] // N_MICROBATCH) + t.shape[1:])
    return _jnp.moveaxis(t, 1, axis + 1)


def setup_inputs(seed: int = 0) -> dict:
    inp = _fwd_setup_inputs(seed)
    key = _jax.random.fold_in(_jax.random.key(seed), 7919)
    shape, _ = _output_shape()
    out = dict(inp)
    out["loss_target"] = _jax.random.normal(_jax.random.fold_in(key, 0), shape, _jnp.float32)
    for i, name in enumerate(TWIN_WEIGHTS):
        w = inp[name].astype(_jnp.float32)
        if MOMENT_SCALE is None:
            s = _jnp.sqrt(_jnp.mean(_jnp.square(w)) + 1e-30)
        else:
            s = MOMENT_SCALE[name]
        km, kv = _jax.random.split(_jax.random.fold_in(key, i + 1))
        out[name] = w
        out["m_" + name] = s * _jax.random.normal(km, w.shape, _jnp.float32)
        out["v_" + name] = (s * s) * _jax.random.uniform(kv, w.shape, _jnp.float32, 0.5, 1.5)
    if N_MICROBATCH > 1:
        for name, axis in PER_EXAMPLE_BATCH_AXIS.items():
            out[name] = _to_microbatches(out[name], axis)
    return {'x': out['x'], 'mem': out['mem'], 'norm_mix_g': out['norm_mix_g'], 'w_in': out['w_in'], 'b_forget': out['b_forget'], 'lru_conv_w': out['lru_conv_w'], 'lru_conv_b': out['lru_conv_b'], 'lru_w_a': out['lru_w_a'], 'lru_b_a': out['lru_b_a'], 'lru_w_x': out['lru_w_x'], 'lru_b_x': out['lru_b_x'], 'lru_lambda': out['lru_lambda'], 'w_out': out['w_out'], 'norm_cross_g': out['norm_cross_g'], 'norm_mem_g': out['norm_mem_g'], 'w_cq': out['w_cq'], 'w_ck': out['w_ck'], 'w_cv': out['w_cv'], 'w_co': out['w_co'], 'norm_ffn_g': out['norm_ffn_g'], 'w_up': out['w_up'], 'ffn_conv_w': out['ffn_conv_w'], 'ffn_conv_b': out['ffn_conv_b'], 'w_down': out['w_down'], 'rel_bias': out['rel_bias'], 'final_norm_g': out['final_norm_g'], 'loss_target': out['loss_target'], 'm_norm_mix_g': out['m_norm_mix_g'], 'm_w_in': out['m_w_in'], 'm_b_forget': out['m_b_forget'], 'm_lru_conv_w': out['m_lru_conv_w'], 'm_lru_conv_b': out['m_lru_conv_b'], 'm_lru_w_a': out['m_lru_w_a'], 'm_lru_b_a': out['m_lru_b_a'], 'm_lru_w_x': out['m_lru_w_x'], 'm_lru_b_x': out['m_lru_b_x'], 'm_lru_lambda': out['m_lru_lambda'], 'm_w_out': out['m_w_out'], 'm_norm_cross_g': out['m_norm_cross_g'], 'm_norm_mem_g': out['m_norm_mem_g'], 'm_w_cq': out['m_w_cq'], 'm_w_ck': out['m_w_ck'], 'm_w_cv': out['m_w_cv'], 'm_w_co': out['m_w_co'], 'm_norm_ffn_g': out['m_norm_ffn_g'], 'm_w_up': out['m_w_up'], 'm_ffn_conv_w': out['m_ffn_conv_w'], 'm_ffn_conv_b': out['m_ffn_conv_b'], 'm_w_down': out['m_w_down'], 'm_rel_bias': out['m_rel_bias'], 'm_final_norm_g': out['m_final_norm_g'], 'v_norm_mix_g': out['v_norm_mix_g'], 'v_w_in': out['v_w_in'], 'v_b_forget': out['v_b_forget'], 'v_lru_conv_w': out['v_lru_conv_w'], 'v_lru_conv_b': out['v_lru_conv_b'], 'v_lru_w_a': out['v_lru_w_a'], 'v_lru_b_a': out['v_lru_b_a'], 'v_lru_w_x': out['v_lru_w_x'], 'v_lru_b_x': out['v_lru_b_x'], 'v_lru_lambda': out['v_lru_lambda'], 'v_w_out': out['v_w_out'], 'v_norm_cross_g': out['v_norm_cross_g'], 'v_norm_mem_g': out['v_norm_mem_g'], 'v_w_cq': out['v_w_cq'], 'v_w_ck': out['v_w_ck'], 'v_w_cv': out['v_w_cv'], 'v_w_co': out['v_w_co'], 'v_norm_ffn_g': out['v_norm_ffn_g'], 'v_w_up': out['v_w_up'], 'v_ffn_conv_w': out['v_ffn_conv_w'], 'v_ffn_conv_b': out['v_ffn_conv_b'], 'v_w_down': out['v_w_down'], 'v_rel_bias': out['v_rel_bias'], 'v_final_norm_g': out['v_final_norm_g']}


def _loss(weights, diff, rest, loss_target):
    with _jax.named_scope("forward"):
        args = {**rest, TWIN_DIFF_INPUT: diff, **{k: w.astype(_WEIGHT_DTYPES[k]) for k, w in weights.items()}}
        y = _forward(args)
    with _jax.named_scope("loss_head"):
        err = _jnp.square(y.astype(_jnp.float32) - loss_target)
        return 0.5 * _jnp.sum(_jnp.mean(err, axis=-1)) if err.ndim else 0.5 * err


def _adamw(w, g, m, v):
    m = ADAM_B1 * m + (1.0 - ADAM_B1) * g
    v = ADAM_B2 * v + (1.0 - ADAM_B2) * _jnp.square(g)
    m_hat = m / (1.0 - ADAM_B1 ** ADAM_STEP)
    v_hat = v / (1.0 - ADAM_B2 ** ADAM_STEP)
    delta = -ADAM_LR * (m_hat / (_jnp.sqrt(v_hat) + ADAM_EPS) + ADAM_WD * w)
    return delta, m, v


def reference(x, mem, norm_mix_g, w_in, b_forget, lru_conv_w, lru_conv_b, lru_w_a, lru_b_a, lru_w_x, lru_b_x, lru_lambda, w_out, norm_cross_g, norm_mem_g, w_cq, w_ck, w_cv, w_co, norm_ffn_g, w_up, ffn_conv_w, ffn_conv_b, w_down, rel_bias, final_norm_g, loss_target, m_norm_mix_g, m_w_in, m_b_forget, m_lru_conv_w, m_lru_conv_b, m_lru_w_a, m_lru_b_a, m_lru_w_x, m_lru_b_x, m_lru_lambda, m_w_out, m_norm_cross_g, m_norm_mem_g, m_w_cq, m_w_ck, m_w_cv, m_w_co, m_norm_ffn_g, m_w_up, m_ffn_conv_w, m_ffn_conv_b, m_w_down, m_rel_bias, m_final_norm_g, v_norm_mix_g, v_w_in, v_b_forget, v_lru_conv_w, v_lru_conv_b, v_lru_w_a, v_lru_b_a, v_lru_w_x, v_lru_b_x, v_lru_lambda, v_w_out, v_norm_cross_g, v_norm_mem_g, v_w_cq, v_w_ck, v_w_cv, v_w_co, v_norm_ffn_g, v_w_up, v_ffn_conv_w, v_ffn_conv_b, v_w_down, v_rel_bias, v_final_norm_g):
    given = dict(x=x, mem=mem, norm_mix_g=norm_mix_g, w_in=w_in, b_forget=b_forget, lru_conv_w=lru_conv_w, lru_conv_b=lru_conv_b, lru_w_a=lru_w_a, lru_b_a=lru_b_a, lru_w_x=lru_w_x, lru_b_x=lru_b_x, lru_lambda=lru_lambda, w_out=w_out, norm_cross_g=norm_cross_g, norm_mem_g=norm_mem_g, w_cq=w_cq, w_ck=w_ck, w_cv=w_cv, w_co=w_co, norm_ffn_g=norm_ffn_g, w_up=w_up, ffn_conv_w=ffn_conv_w, ffn_conv_b=ffn_conv_b, w_down=w_down, rel_bias=rel_bias, final_norm_g=final_norm_g, loss_target=loss_target, m_norm_mix_g=m_norm_mix_g, m_w_in=m_w_in, m_b_forget=m_b_forget, m_lru_conv_w=m_lru_conv_w, m_lru_conv_b=m_lru_conv_b, m_lru_w_a=m_lru_w_a, m_lru_b_a=m_lru_b_a, m_lru_w_x=m_lru_w_x, m_lru_b_x=m_lru_b_x, m_lru_lambda=m_lru_lambda, m_w_out=m_w_out, m_norm_cross_g=m_norm_cross_g, m_norm_mem_g=m_norm_mem_g, m_w_cq=m_w_cq, m_w_ck=m_w_ck, m_w_cv=m_w_cv, m_w_co=m_w_co, m_norm_ffn_g=m_norm_ffn_g, m_w_up=m_w_up, m_ffn_conv_w=m_ffn_conv_w, m_ffn_conv_b=m_ffn_conv_b, m_w_down=m_w_down, m_rel_bias=m_rel_bias, m_final_norm_g=m_final_norm_g, v_norm_mix_g=v_norm_mix_g, v_w_in=v_w_in, v_b_forget=v_b_forget, v_lru_conv_w=v_lru_conv_w, v_lru_conv_b=v_lru_conv_b, v_lru_w_a=v_lru_w_a, v_lru_b_a=v_lru_b_a, v_lru_w_x=v_lru_w_x, v_lru_b_x=v_lru_b_x, v_lru_lambda=v_lru_lambda, v_w_out=v_w_out, v_norm_cross_g=v_norm_cross_g, v_norm_mem_g=v_norm_mem_g, v_w_cq=v_w_cq, v_w_ck=v_w_ck, v_w_cv=v_w_cv, v_w_co=v_w_co, v_norm_ffn_g=v_norm_ffn_g, v_w_up=v_w_up, v_ffn_conv_w=v_ffn_conv_w, v_ffn_conv_b=v_ffn_conv_b, v_w_down=v_w_down, v_rel_bias=v_rel_bias, v_final_norm_g=v_final_norm_g)
    weights = {n: given[n] for n in TWIN_WEIGHTS}
    shared = {n: given[n] for n in SHARED_INPUTS}
    per_example = {n: given[n] for n in ['x', 'mem']}
    grad_fn = _jax.value_and_grad(_loss, argnums=(0, 1))

    def one_microbatch(ex, loss_target):
        ex = dict(ex)
        diff = ex.pop(TWIN_DIFF_INPUT)
        return grad_fn(weights, diff, {**shared, **ex}, loss_target)

    if N_MICROBATCH == 1:
        loss, (grad_w, grad_x) = one_microbatch(per_example, given["loss_target"])
    else:
        def body(carry, xs):
            loss_sum, grad_sum = carry
            l_k, (gw_k, gx_k) = one_microbatch(xs[0], xs[1])
            with _jax.named_scope("update"):
                return (loss_sum + l_k, _jax.tree.map(_jnp.add, grad_sum, gw_k)), gx_k

        init = (_jnp.zeros((), _jnp.float32), _jax.tree.map(_jnp.zeros_like, weights))
        (loss, grad_w), grad_x = _jax.lax.scan(body, init, (per_example, given["loss_target"]))
    with _jax.named_scope("update"):
        delta_w, new_m, new_v = {}, {}, {}
        for n in TWIN_WEIGHTS:
            delta_w[n], new_m[n], new_v[n] = _adamw(weights[n], grad_w[n], given["m_" + n], given["v_" + n])
    return (loss, grad_x, *[grad_w[n] for n in TWIN_WEIGHTS], *[delta_w[n] for n in TWIN_WEIGHTS],
            *[new_m[n] for n in TWIN_WEIGHTS], *[new_v[n] for n in TWIN_WEIGHTS])
```

```python
import functools
import math

import numpy as np
import jax
import jax.numpy as jnp
from jax import lax
from jax.experimental import pallas as pl
from jax.experimental.pallas import tpu as pltpu

F32 = jnp.float32
BF16 = jnp.bfloat16
MESH = pl.DeviceIdType.MESH

N_DEV = 8
D_MODEL = 1024
SEQ = 2048
DEPTH = 2
HEAD_DIM = 64
N_HEADS = 4
GROUP_W = N_HEADS * HEAD_DIM
D_FF = 2816
N_MEM = 256
NUM_BUCKETS = 32
MAX_DISTANCE = 2048
BLOCK = 128
DILATIONS = (1, 4, 16)
EPS = 1e-6
LRU_C = 8.0
Q_SCALE = HEAD_DIM ** -0.5
AUX_W = 640
LRU_HALF_W = 128
LRU_HALVES = GROUP_W // LRU_HALF_W
ADAM_LR, ADAM_B1, ADAM_B2, ADAM_EPS, ADAM_WD, ADAM_STEP = 0.001, 0.9, 0.999, 1e-08, 0.01, 10

VMEM_LIMIT_V7X = 48 * 1024 * 1024


def _params(*sem):
    return pltpu.CompilerParams(dimension_semantics=sem if sem else None, vmem_limit_bytes=VMEM_LIMIT_V7X)


def _pick(n, cands):
    for c in cands:
        if n % c == 0:
            return c
    return n


def matmul(a, b, *, name, trans_a=False, trans_b=False, out_dtype=F32, residual=None):
    (K, M) = a.shape if trans_a else a.shape[::-1]
    (N, Kb) = b.shape if trans_b else b.shape[::-1]
    assert K == Kb, (a.shape, b.shape)
    tm = _pick(M, (512, 256, 128))
    tn = _pick(N, (768, 512, 640, 256, 128))
    tk = _pick(K, (1024, 512, 256, 128))
    nk = K // tk
    a_spec = pl.BlockSpec((tk, tm), lambda i, j, k: (k, i)) if trans_a else pl.BlockSpec((tm, tk), lambda i, j, k: (i, k))
    b_spec = pl.BlockSpec((tn, tk), lambda i, j, k: (j, k)) if trans_b else pl.BlockSpec((tk, tn), lambda i, j, k: (k, j))
    o_spec = pl.BlockSpec((tm, tn), lambda i, j, k: (i, j))
    dims = (((0 if trans_a else 1,), (1 if trans_b else 0,)), ((), ()))
    has_res = residual is not None

    def body(*refs):
        a_ref, b_ref = refs[0], refs[1]
        r_ref = refs[2] if has_res else None
        o_ref, acc_ref = refs[-2], refs[-1]
        k = pl.program_id(2)
        part = lax.dot_general(a_ref[...].astype(BF16), b_ref[...].astype(BF16), dims, preferred_element_type=F32)

        @pl.when(k == 0)
        def _():
            acc_ref[...] = part

        @pl.when(k > 0)
        def _():
            acc_ref[...] += part

        @pl.when(k == nk - 1)
        def _():
            r = acc_ref[...]
            if has_res:
                r = r + r_ref[...].astype(F32)
            o_ref[...] = r.astype(out_dtype)

    ops = (a, b) + ((residual,) if has_res else ())
    return pl.pallas_call(
        body, name=name, grid=(M // tm, N // tn, nk),
        in_specs=[a_spec, b_spec] + ([o_spec] if has_res else []),
        out_specs=o_spec, out_shape=jax.ShapeDtypeStruct((M, N), out_dtype),
        scratch_shapes=[pltpu.VMEM((tm, tn), F32)],
        compiler_params=_params("parallel", "parallel", "arbitrary"),
    )(*ops)


def rmsnorm_fwd(x, g, *, name):
    R, D = x.shape
    tr = _pick(R, (512, 256))

    def body(x_ref, g_ref, o_ref):
        xv = x_ref[...]
        r = lax.rsqrt(jnp.mean(xv * xv, axis=-1, keepdims=True) + EPS)
        o_ref[...] = (xv * r * g_ref[...]).astype(BF16)

    return pl.pallas_call(
        body, name=name, grid=(R // tr,),
        in_specs=[pl.BlockSpec((tr, D), lambda i: (i, 0)), pl.BlockSpec((1, D), lambda i: (0, 0))],
        out_specs=pl.BlockSpec((tr, D), lambda i: (i, 0)), out_shape=jax.ShapeDtypeStruct((R, D), BF16),
        compiler_params=_params("parallel"),
    )(x, g)


def rmsnorm_bwd(x, g, dh, dres, *, name):
    R, D = x.shape
    tr = _pick(R, (512, 256))
    has_res = dres is not None

    def body(*refs):
        x_ref, g_ref, dh_ref = refs[:3]
        dx_ref, dg_ref = refs[-2], refs[-1]
        xv = x_ref[...]
        r = lax.rsqrt(jnp.mean(xv * xv, axis=-1, keepdims=True) + EPS)
        n = xv * r
        dhv = dh_ref[...]
        dn = dhv * g_ref[...]
        dx = r * (dn - n * jnp.mean(dn * n, axis=-1, keepdims=True))
        if has_res:
            dx = dx + refs[3][...]
        dx_ref[...] = dx
        part = jnp.sum(dhv * n, axis=0, keepdims=True)

        @pl.when(pl.program_id(0) == 0)
        def _():
            dg_ref[...] = part

        @pl.when(pl.program_id(0) > 0)
        def _():
            dg_ref[...] += part

    row = pl.BlockSpec((tr, D), lambda i: (i, 0))
    vec = pl.BlockSpec((1, D), lambda i: (0, 0))
    ops = (x, g, dh) + ((dres,) if has_res else ())
    return pl.pallas_call(
        body, name=name, grid=(R // tr,),
        in_specs=[row, vec, row] + ([row] if has_res else []),
        out_specs=[row, vec],
        out_shape=[jax.ShapeDtypeStruct((R, D), F32), jax.ShapeDtypeStruct((1, D), F32)],
        compiler_params=_params("arbitrary"),
    )(*ops)


_SQRT_HALF = 0.7071067811865476
_INV_SQRT_2PI = 0.3989422804014327


def _erf(x):
    ax = jnp.abs(x)
    t = 1.0 / (1.0 + 0.3275911 * ax)
    poly = t * (0.254829592 + t * (-0.284496736 + t * (1.421413741 + t * (-1.453152027 + t * 1.061405429))))
    y = 1.0 - poly * jnp.exp(-ax * ax)
    return jnp.where(x < 0, -y, y)


def _gelu_cdf(x):
    return 0.5 * (1.0 + _erf(x * _SQRT_HALF))


def _gelu_and_grad(x):
    cdf = _gelu_cdf(x)
    return x * cdf, cdf + x * _INV_SQRT_2PI * jnp.exp(-0.5 * x * x)


def _shift_down(main, halo, first, shifts):
    halo = jnp.where(first, 0.0, halo)
    ext = jnp.concatenate([halo, main], axis=0)
    return [pltpu.roll(ext, s, 0)[8:] for s in shifts]


def _conv3(main, halo, first, w, b):
    m1, m2 = _shift_down(main, halo, first, (1, 2))
    return ((b + w[0:1] * m2) + w[1:2] * m1) + w[2:3] * main, m1, m2


def glu_fwd(hu, hg, wu, wg, bu, bg, *, name):
    T, F = hu.shape
    tm, tf = 256, _pick(F, (256, 128))
    hb = tm // 8
    blocks_per_example = SEQ // tm

    def body(hu_ref, hg_ref, hau_ref, hag_ref, wu_ref, wg_ref, bu_ref, bg_ref, o_ref):
        first = pl.program_id(0) % blocks_per_example == 0
        up, _, _ = _conv3(hu_ref[...], hau_ref[...], first, wu_ref[...], bu_ref[...])
        gate, _, _ = _conv3(hg_ref[...], hag_ref[...], first, wg_ref[...], bg_ref[...])
        o_ref[...] = (gate * _gelu_cdf(gate) * up).astype(BF16)

    main = pl.BlockSpec((tm, tf), lambda i, j: (i, j))
    halo = pl.BlockSpec((8, tf), lambda i, j: (jnp.maximum(i * hb - 1, 0), j))
    w3 = pl.BlockSpec((3, tf), lambda i, j: (0, j))
    b1 = pl.BlockSpec((1, tf), lambda i, j: (0, j))
    return pl.pallas_call(
        body, name=name, grid=(T // tm, F // tf),
        in_specs=[main, main, halo, halo, w3, w3, b1, b1],
        out_specs=main, out_shape=jax.ShapeDtypeStruct((T, F), BF16),
        compiler_params=_params("parallel", "parallel"),
    )(hu, hg, hu, hg, wu, wg, bu, bg)


def glu_bwd(hu, hg, dact, wu, wg, bu, bg, *, name):
    T, F = hu.shape
    tm, tf = 256, _pick(F, (256, 128))
    hb = tm // 8
    blocks_per_example = SEQ // tm

    def body(hu_ref, hg_ref, hau_ref, hag_ref, da_ref, wu_ref, wg_ref, bu_ref, bg_ref,
             du_ref, dg_ref, dwu_ref, dwg_ref, dbu_ref, dbg_ref):
        i = pl.program_id(1)
        first = i % blocks_per_example == 0
        xu, xg = hu_ref[...], hg_ref[...]
        up, u1, u2 = _conv3(xu, hau_ref[...], first, wu_ref[...], bu_ref[...])
        gate, g1, g2 = _conv3(xg, hag_ref[...], first, wg_ref[...], bg_ref[...])
        act, dact_dgate = _gelu_and_grad(gate)
        da = da_ref[...]
        dup = da * act
        dgate = da * up * dact_dgate
        du_ref[...] = dup
        dg_ref[...] = dgate

        def sums(d, x0, x1, x2):
            s = lambda v: jnp.sum(v, axis=0, keepdims=True)
            return jnp.concatenate([s(d * x2), s(d * x1), s(d * x0)], axis=0), s(d)

        pwu, pbu = sums(dup, xu, u1, u2)
        pwg, pbg = sums(dgate, xg, g1, g2)

        @pl.when(i == 0)
        def _():
            dwu_ref[...] = pwu
            dwg_ref[...] = pwg
            dbu_ref[...] = pbu
            dbg_ref[...] = pbg

        @pl.when(i > 0)
        def _():
            dwu_ref[...] += pwu
            dwg_ref[...] += pwg
            dbu_ref[...] += pbu
            dbg_ref[...] += pbg

    main = pl.BlockSpec((tm, tf), lambda j, i: (i, j))
    halo = pl.BlockSpec((8, tf), lambda j, i: (jnp.maximum(i * hb - 1, 0), j))
    w3 = pl.BlockSpec((3, tf), lambda j, i: (0, j))
    b1 = pl.BlockSpec((1, tf), lambda j, i: (0, j))
    sd = jax.ShapeDtypeStruct
    return pl.pallas_call(
        body, name=name, grid=(F // tf, T // tm),
        in_specs=[main, main, halo, halo, main, w3, w3, b1, b1],
        out_specs=[main, main, w3, w3, b1, b1],
        out_shape=[sd((T, F), F32), sd((T, F), F32), sd((3, F), F32), sd((3, F), F32), sd((1, F), F32), sd((1, F), F32)],
        compiler_params=_params("parallel", "arbitrary"),
    )(hu, hg, hu, hg, dact, wu, wg, bu, bg)


def conv3_transpose(d, w, *, name):
    T, F = d.shape
    tm, tf = 256, _pick(F, (256, 128))
    hb = tm // 8
    blocks_per_example = SEQ // tm
    n_halo_blocks = T // 8

    def body(d_ref, ha_ref, w_ref, o_ref):
        last = pl.program_id(0) % blocks_per_example == blocks_per_example - 1
        main = d_ref[...]
        halo = jnp.where(last, 0.0, ha_ref[...])
        ext = jnp.concatenate([main, halo], axis=0)
        n = tm + 8
        p1 = pltpu.roll(ext, n - 1, 0)[:tm]
        p2 = pltpu.roll(ext, n - 2, 0)[:tm]
        w = w_ref[...]
        o_ref[...] = (w[2:3] * main + w[1:2] * p1 + w[0:1] * p2).astype(BF16)

    main = pl.BlockSpec((tm, tf), lambda i, j: (i, j))
    halo = pl.BlockSpec((8, tf), lambda i, j: (jnp.minimum((i + 1) * hb, n_halo_blocks - 1), j))
    return pl.pallas_call(
        body, name=name, grid=(T // tm, F // tf),
        in_specs=[main, halo, pl.BlockSpec((3, tf), lambda i, j: (0, j))],
        out_specs=main, out_shape=jax.ShapeDtypeStruct((T, F), BF16),
        compiler_params=_params("parallel", "parallel"),
    )(d, d, w)


def loss_head(x, g, target, *, name):
    T, D = x.shape
    tr = 256

    def body(x_ref, g_ref, t_ref, loss_ref, dx_ref, dg_ref):
        xv = x_ref[...]
        gv = g_ref[...]
        r = lax.rsqrt(jnp.mean(xv * xv, axis=-1, keepdims=True) + EPS)
        n = xv * r
        err = n * gv - t_ref[...]
        part_loss = jnp.zeros((1, 128), F32) + 0.5 * jnp.sum(jnp.mean(err * err, axis=-1, keepdims=True))
        dy = err * (1.0 / D)
        dn = dy * gv
        dx_ref[...] = r * (dn - n * jnp.mean(dn * n, axis=-1, keepdims=True))
        part_g = jnp.sum(dy * n, axis=0, keepdims=True)

        @pl.when(pl.program_id(0) == 0)
        def _():
            loss_ref[...] = part_loss
            dg_ref[...] = part_g

        @pl.when(pl.program_id(0) > 0)
        def _():
            loss_ref[...] += part_loss
            dg_ref[...] += part_g

    row = pl.BlockSpec((tr, D), lambda i: (i, 0))
    vec = pl.BlockSpec((1, D), lambda i: (0, 0))
    sd = jax.ShapeDtypeStruct
    return pl.pallas_call(
        body, name=name, grid=(T // tr,),
        in_specs=[row, vec, row],
        out_specs=[pl.BlockSpec((1, 128), lambda i: (0, 0)), row, vec],
        out_shape=[sd((1, 128), F32), sd((T, D), F32), sd((1, D), F32)],
        compiler_params=_params("arbitrary"),
    )(x, g, target)


def adamw(w, g, m, v, *, name):
    R, C = w.shape
    tr = _pick(R, (256, 128, 64, 32, 16, 8))

    def body(w_ref, g_ref, m_ref, v_ref, d_ref, nm_ref, nv_ref):
        gv = g_ref[...]
        mn = ADAM_B1 * m_ref[...] + (1.0 - ADAM_B1) * gv
        vn = ADAM_B2 * v_ref[...] + (1.0 - ADAM_B2) * (gv * gv)
        m_hat = mn / (1.0 - ADAM_B1 ** ADAM_STEP)
        v_hat = vn / (1.0 - ADAM_B2 ** ADAM_STEP)
        d_ref[...] = -ADAM_LR * (m_hat / (jnp.sqrt(v_hat) + ADAM_EPS) + ADAM_WD * w_ref[...])
        nm_ref[...] = mn
        nv_ref[...] = vn

    blk = pl.BlockSpec((tr, C), lambda i: (i, 0))
    sd = jax.ShapeDtypeStruct((R, C), F32)
    return pl.pallas_call(
        body, name=name, grid=(R // tr,), in_specs=[blk] * 4, out_specs=[blk] * 3, out_shape=[sd] * 3,
        compiler_params=_params("parallel"),
    )(w, g, m, v)


def _softplus(x):
    return jnp.maximum(x, 0.0) + jnp.log(1.0 + jnp.exp(-jnp.abs(x)))


def _lru_gates(x, cw, cb, wa, ba, wx, bx, lam):
    S = x.shape[0]
    row = lax.broadcasted_iota(jnp.int32, (S, 1), 0)

    def back(s):
        return jnp.where(row >= s, pltpu.roll(x, s, 0), 0.0)

    xc = (((cb + cw[0:1] * back(3)) + cw[1:2] * back(2)) + cw[2:3] * back(1)) + cw[3:4] * x
    xb = xc.astype(BF16)
    r = jax.nn.sigmoid(jnp.dot(xb, wa, preferred_element_type=F32) + ba)
    ig = jax.nn.sigmoid(jnp.dot(xb, wx, preferred_element_type=F32) + bx)
    sp = _softplus(-lam)
    la = -LRU_C * r * sp
    a = jnp.exp(la)
    y = 2.0 * la
    one_minus_a2 = jnp.where(y > -0.05, -y * (1.0 + y * (0.5 + y * (1.0 / 6.0 + y * (1.0 / 24.0)))), 1.0 - jnp.exp(y))
    mm = jnp.sqrt(one_minus_a2)
    return xc, xb, r, ig, sp, a, mm


def lru_fwd(aux, cw, cb, wa, ba, wx, bx, lam, *, name):
    T = aux.shape[0]
    S, C = SEQ, LRU_HALF_W

    def body(x_ref, g_ref, cw_ref, cb_ref, wa_ref, ba_ref, wx_ref, bx_ref, lam_ref, o_ref, h_ref, a_s, u_s):
        xc, _, r, ig, sp, a, mm = _lru_gates(x_ref[...], cw_ref[...], cb_ref[...], wa_ref[...], ba_ref[...],
                                             wx_ref[...], bx_ref[...], lam_ref[...])
        a_s[...] = a
        u_s[...] = mm * (ig * xc)

        def group(i, h):
            base = pl.multiple_of(i * 8, 8)
            a8 = a_s[pl.ds(base, 8), :]
            u8 = u_s[pl.ds(base, 8), :]
            for rr in range(8):
                h = a8[rr:rr + 1] * h + u8[rr:rr + 1]
                h_ref[pl.ds(base + rr, 1), :] = h
            return h

        lax.fori_loop(0, S // 8, group, jnp.zeros((1, C), F32))
        gate = g_ref[...]
        o_ref[...] = (h_ref[...] * (gate * _gelu_cdf(gate))).astype(BF16)

    blk = lambda col: pl.BlockSpec((S, C), lambda c, b: (b, col + c))
    par = lambda rows: pl.BlockSpec((rows, C), lambda c, b: (0, c))
    sq = pl.BlockSpec((None, C, C), lambda c, b: (c, 0, 0))
    sd = jax.ShapeDtypeStruct
    W = LRU_HALVES * C
    return pl.pallas_call(
        body, name=name, grid=(LRU_HALVES, T // S),
        in_specs=[blk(0), blk(LRU_HALVES), par(4), par(1), sq, par(1), sq, par(1), par(1)],
        out_specs=[blk(0), blk(0)], out_shape=[sd((T, W), BF16), sd((T, W), F32)],
        scratch_shapes=[pltpu.VMEM((S, C), F32), pltpu.VMEM((S, C), F32)],
        compiler_params=_params("parallel", "parallel"),
    )(aux, aux, cw, cb, wa, ba, wx, bx, lam)


def lru_bwd(aux, h, dmixed, cw, cb, wa, ba, wx, bx, lam, *, name):
    T = aux.shape[0]
    S, C = SEQ, LRU_HALF_W

    def body(x_ref, g_ref, h_ref, do_ref, cw_ref, cb_ref, wa_ref, ba_ref, wx_ref, bx_ref, lam_ref,
             dx_ref, dgate_ref, dcw_ref, dcb_ref, dwa_ref, dba_ref, dwx_ref, dbx_ref, dlam_ref, a_s, d_s):
        x = x_ref[...]
        cw = cw_ref[...]
        lam = lam_ref[...]
        xc, xb, r, ig, sp, a, mm = _lru_gates(x, cw, cb_ref[...], wa_ref[...], ba_ref[...], wx_ref[...], bx_ref[...], lam)
        gate = g_ref[...]
        gl, dgl = _gelu_and_grad(gate)
        dout = do_ref[...]
        hv = h_ref[...]
        dgate_ref[...] = dout * hv * dgl
        a_s[...] = a
        d_s[...] = dout * gl

        def group(i, c):
            base = pl.multiple_of((S // 8 - 1 - i) * 8, 8)
            a8 = a_s[pl.ds(base, 8), :]
            d8 = d_s[pl.ds(base, 8), :]
            for rr in range(7, -1, -1):
                d = d8[rr:rr + 1] + c
                d_s[pl.ds(base + rr, 1), :] = d
                c = a8[rr:rr + 1] * d
            return c

        lax.fori_loop(0, S // 8, group, jnp.zeros((1, C), F32))
        row = lax.broadcasted_iota(jnp.int32, (S, 1), 0)
        dht = d_s[...]
        h_prev = jnp.where(row >= 1, pltpu.roll(hv, 1, 0), 0.0)
        da = dht * h_prev
        gx = ig * xc
        dmm = dht * gx
        dig = dht * mm * xc
        dxc = dht * mm * ig
        dla = da * a - dmm * (a * a) / mm
        dr = dla * (-LRU_C * sp)
        dsp = jnp.sum(dla * (-LRU_C * r), axis=0, keepdims=True)
        dlam = dsp * (-jax.nn.sigmoid(-lam))
        dpa = dr * r * (1.0 - r)
        dpx = dig * ig * (1.0 - ig)
        dpa_b, dpx_b = dpa.astype(BF16), dpx.astype(BF16)
        nt = (((1,), (1,)), ((), ()))
        tn = (((0,), (0,)), ((), ()))
        dxc = dxc + lax.dot_general(dpa_b, wa_ref[...], nt, preferred_element_type=F32) \
                  + lax.dot_general(dpx_b, wx_ref[...], nt, preferred_element_type=F32)
        dwa = lax.dot_general(xb, dpa_b, tn, preferred_element_type=F32)
        dwx = lax.dot_general(xb, dpx_b, tn, preferred_element_type=F32)

        def fwd(v, s):
            return jnp.where(row < S - s, pltpu.roll(v, S - s, 0), 0.0)

        def back(v, s):
            return jnp.where(row >= s, pltpu.roll(v, s, 0), 0.0)

        dx_ref[...] = cw[3:4] * dxc + cw[2:3] * fwd(dxc, 1) + cw[1:2] * fwd(dxc, 2) + cw[0:1] * fwd(dxc, 3)
        s0 = lambda v: jnp.sum(v, axis=0, keepdims=True)
        dcw = jnp.concatenate([s0(dxc * back(x, 3)), s0(dxc * back(x, 2)), s0(dxc * back(x, 1)), s0(dxc * x)], axis=0)
        parts = ((dcw_ref, dcw), (dcb_ref, s0(dxc)), (dwa_ref, dwa), (dba_ref, s0(dpa)), (dwx_ref, dwx),
                 (dbx_ref, s0(dpx)), (dlam_ref, dlam))

        @pl.when(pl.program_id(1) == 0)
        def _():
            for ref, val in parts:
                ref[...] = val

        @pl.when(pl.program_id(1) > 0)
        def _():
            for ref, val in parts:
                ref[...] += val

    blk = lambda col: pl.BlockSpec((S, C), lambda c, b: (b, col + c))
    par = lambda rows: pl.BlockSpec((rows, C), lambda c, b: (0, c))
    sq = pl.BlockSpec((None, C, C), lambda c, b: (c, 0, 0))
    sd = jax.ShapeDtypeStruct
    W = LRU_HALVES * C
    vec = sd((1, W), F32)
    return pl.pallas_call(
        body, name=name, grid=(LRU_HALVES, T // S),
        in_specs=[blk(0), blk(LRU_HALVES), blk(0), blk(3 * LRU_HALVES), par(4), par(1), sq, par(1), sq, par(1), par(1)],
        out_specs=[blk(0), blk(0), par(4), par(1), sq, par(1), sq, par(1), par(1)],
        out_shape=[sd((T, W), F32), sd((T, W), F32), sd((4, W), F32), vec, sd((LRU_HALVES, C, C), F32), vec,
                   sd((LRU_HALVES, C, C), F32), vec, vec],
        scratch_shapes=[pltpu.VMEM((S, C), F32), pltpu.VMEM((S, C), F32)],
        compiler_params=_params("parallel", "arbitrary"),
    )(aux, aux, h, dmixed, cw, cb, wa, ba, wx, bx, lam)


_NT = (((1,), (1,)), ((), ()))
_TN = (((0,), (0,)), ((), ()))


def _dot(a, b, dims=None):
    if dims is None:
        return jnp.dot(a, b, preferred_element_type=F32)
    return lax.dot_general(a, b, dims, preferred_element_type=F32)


def _hs(h):
    return slice(h * HEAD_DIM, (h + 1) * HEAD_DIM)


def cross_fwd(q, kv, *, name):
    T = q.shape[0]
    tq = 512

    def body(q_ref, kv_ref, o_ref):
        for h in range(N_HEADS):
            qh = q_ref[:, _hs(h)] * Q_SCALE
            k = kv_ref[:, _hs(h)]
            v = kv_ref[:, GROUP_W + h * HEAD_DIM:GROUP_W + (h + 1) * HEAD_DIM]
            s = _dot(qh, k, _NT)
            p = jnp.exp(s - jnp.max(s, axis=-1, keepdims=True))
            p = p / jnp.sum(p, axis=-1, keepdims=True)
            o_ref[:, _hs(h)] = _dot(p.astype(BF16), v).astype(BF16)

    per = SEQ // tq
    return pl.pallas_call(
        body, name=name, grid=(T // tq,),
        in_specs=[pl.BlockSpec((tq, GROUP_W), lambda i: (i, 0)), pl.BlockSpec((N_MEM, 2 * GROUP_W), lambda i: (i // per, 0))],
        out_specs=pl.BlockSpec((tq, GROUP_W), lambda i: (i, 0)), out_shape=jax.ShapeDtypeStruct((T, GROUP_W), BF16),
        compiler_params=_params("parallel"),
    )(q, kv)


def cross_bwd(q, kv, do, *, name):
    T = q.shape[0]
    tq = 512
    per = SEQ // tq

    def body(q_ref, kv_ref, do_ref, dq_ref, dkv_ref):
        first = pl.program_id(0) % per == 0
        for h in range(N_HEADS):
            vs = slice(GROUP_W + h * HEAD_DIM, GROUP_W + (h + 1) * HEAD_DIM)
            qh = q_ref[:, _hs(h)] * Q_SCALE
            k = kv_ref[:, _hs(h)]
            v = kv_ref[:, vs]
            doh = do_ref[:, _hs(h)].astype(BF16)
            s = _dot(qh, k, _NT)
            p = jnp.exp(s - jnp.max(s, axis=-1, keepdims=True))
            p = p / jnp.sum(p, axis=-1, keepdims=True)
            dp = _dot(doh, v, _NT)
            ds = (p * (dp - jnp.sum(p * dp, axis=-1, keepdims=True))).astype(BF16)
            dq_ref[:, _hs(h)] = (_dot(ds, k) * Q_SCALE).astype(BF16)
            dk = _dot(ds, qh, _TN)
            dv = _dot(p.astype(BF16), doh, _TN)

            @pl.when(first)
            def _():
                dkv_ref[:, _hs(h)] = dk
                dkv_ref[:, vs] = dv

            @pl.when(jnp.logical_not(first))
            def _():
                dkv_ref[:, _hs(h)] += dk
                dkv_ref[:, vs] += dv

    qb = pl.BlockSpec((tq, GROUP_W), lambda i: (i, 0))
    kvb = pl.BlockSpec((N_MEM, 2 * GROUP_W), lambda i: (i // per, 0))
    sd = jax.ShapeDtypeStruct
    return pl.pallas_call(
        body, name=name, grid=(T // tq,),
        in_specs=[qb, kvb, qb], out_specs=[qb, kvb],
        out_shape=[sd((T, GROUP_W), BF16), sd(kv.shape, F32)],
        compiler_params=_params("arbitrary"),
    )(q, kv, do)


NB = SEQ // BLOCK
NEG = -1e30


def _split_dot(x, tri):
    hi = x.astype(BF16)
    lo = (x - hi.astype(F32)).astype(BF16)
    return _dot(hi, tri) + _dot(lo, tri)


def _blk(i):
    return pl.ds(pl.multiple_of(i * BLOCK, BLOCK), BLOCK)


def _iotas():
    row = lax.broadcasted_iota(jnp.int32, (BLOCK, BLOCK), 0)
    col = lax.broadcasted_iota(jnp.int32, (BLOCK, BLOCK), 1)
    return row, col


def _sb_scores(q, k, diag, strict, later, csum):
    z = _dot(q, k, _NT)
    mask = jnp.logical_or(jnp.logical_not(diag), strict)
    lk = jnp.where(mask, -_softplus(z), 0.0)
    lka = _split_dot(lk, later) + csum
    att = jnp.where(mask, jnp.exp(z + lk + lka), 0.0)
    sg = jnp.where(mask, jnp.exp(z + lk), 0.0)
    return att, sg, lk


def _qkv_specs(first_col):
    return [pl.BlockSpec((SEQ, GROUP_W), lambda b, c=first_col + j: (b, c)) for j in range(3)]


def sb_fwd(qkv, *, name):
    T = qkv.shape[0]

    def body(q_ref, k_ref, v_ref, o_ref):
        row, col = _iotas()
        strict = col < row
        later = (row > col).astype(BF16)
        for h in range(N_HEADS):
            hs = _hs(h)

            def qblock(i, _):
                q = q_ref[_blk(i), hs] * Q_SCALE

                def kblock(jj, carry):
                    acc, csum = carry
                    j = i - jj
                    att, _, lk = _sb_scores(q, k_ref[_blk(j), hs], jj == 0, strict, later, csum)
                    acc = acc + _dot(att.astype(BF16), v_ref[_blk(j), hs])
                    return acc, csum + jnp.sum(lk, axis=1, keepdims=True)

                acc, _ = lax.fori_loop(0, i + 1, kblock, (jnp.zeros((BLOCK, HEAD_DIM), F32), jnp.zeros((BLOCK, 1), F32)))
                o_ref[_blk(i), hs] = acc.astype(BF16)
                return 0

            lax.fori_loop(0, NB, qblock, 0)

    return pl.pallas_call(
        body, name=name, grid=(T // SEQ,), in_specs=_qkv_specs(0),
        out_specs=pl.BlockSpec((SEQ, GROUP_W), lambda b: (b, 0)), out_shape=jax.ShapeDtypeStruct((T, GROUP_W), BF16),
        compiler_params=_params("parallel"),
    )(qkv, qkv, qkv)


def sb_bwd(qkv, dmixed, *, name):
    T = qkv.shape[0]

    def body(q_ref, k_ref, v_ref, do_ref, dq_ref, dk_ref, dv_ref, att_s, sg_s):
        row, col = _iotas()
        strict = col < row
        later = (row > col).astype(BF16)
        earlier = (row < col).astype(BF16)
        dk_ref[...] = jnp.zeros_like(dk_ref)
        dv_ref[...] = jnp.zeros_like(dv_ref)
        for h in range(N_HEADS):
            hs = _hs(h)

            def qblock(i, _):
                q = q_ref[_blk(i), hs] * Q_SCALE
                do = do_ref[_blk(i), hs].astype(BF16)

                def recompute(jj, csum):
                    j = i - jj
                    att, sg, lk = _sb_scores(q, k_ref[_blk(j), hs], jj == 0, strict, later, csum)
                    att_s[j] = att
                    sg_s[j] = sg
                    return csum + jnp.sum(lk, axis=1, keepdims=True)

                lax.fori_loop(0, i + 1, recompute, jnp.zeros((BLOCK, 1), F32))

                def kblock(j, carry):
                    dq, pre = carry
                    att = att_s[j]
                    ds = _dot(do, v_ref[_blk(j), hs], _NT) * att
                    dlk = ds + _split_dot(ds, earlier) + pre
                    dz = (ds - dlk * sg_s[j]).astype(BF16)
                    dq = dq + _dot(dz, k_ref[_blk(j), hs])
                    dk_ref[_blk(j), hs] += _dot(dz, q, _TN)
                    dv_ref[_blk(j), hs] += _dot(att.astype(BF16), do, _TN)
                    return dq, pre + jnp.sum(ds, axis=1, keepdims=True)

                dq, _ = lax.fori_loop(0, i + 1, kblock, (jnp.zeros((BLOCK, HEAD_DIM), F32), jnp.zeros((BLOCK, 1), F32)))
                dq_ref[_blk(i), hs] = dq * Q_SCALE
                return 0

            lax.fori_loop(0, NB, qblock, 0)

    out = pl.BlockSpec((SEQ, GROUP_W), lambda b: (b, 0))
    sd = jax.ShapeDtypeStruct((T, GROUP_W), F32)
    return pl.pallas_call(
        body, name=name, grid=(T // SEQ,), in_specs=_qkv_specs(0) + [out],
        out_specs=[out] * 3, out_shape=[sd] * 3,
        scratch_shapes=[pltpu.VMEM((NB, BLOCK, BLOCK), F32), pltpu.VMEM((NB, BLOCK, BLOCK), F32)],
        compiler_params=_params("parallel"),
    )(qkv, qkv, qkv, dmixed)


LANES = 128
CUM_BLK = 256


def col_to_row(c):
    b = c.shape[0] // SEQ
    return c.reshape(b, SEQ, LANES)[:, :, :8].transpose(0, 2, 1).reshape(b * 8, SEQ)


def row_to_col(r):
    b = r.shape[0] // 8
    c = r.reshape(b, 8, SEQ).transpose(0, 2, 1)
    return jnp.pad(c, ((0, 0), (0, 0), (0, LANES - 8))).reshape(b * SEQ, LANES)


def fox_prep(aux, bf, *, name):
    T = aux.shape[0]

    def body(f_ref, b_ref, o_ref):
        row = lax.broadcasted_iota(jnp.int32, (CUM_BLK, CUM_BLK), 0)
        col = lax.broadcasted_iota(jnp.int32, (CUM_BLK, CUM_BLK), 1)
        upto = (col <= row).astype(BF16)
        carry = jnp.zeros((1, LANES), F32)
        for n in range(SEQ // CUM_BLK):
            rows = slice(n * CUM_BLK, (n + 1) * CUM_BLK)
            logf = -_softplus(-(f_ref[rows, :] + b_ref[...]))
            hi = logf.astype(BF16)
            lo = (logf - hi.astype(F32)).astype(BF16)
            cum = _dot(upto, hi) + _dot(upto, lo) + carry
            o_ref[rows, :] = cum
            carry = cum[CUM_BLK - 1:CUM_BLK]

    return pl.pallas_call(
        body, name=name, grid=(T // SEQ,),
        in_specs=[pl.BlockSpec((SEQ, LANES), lambda b: (b, 4)), pl.BlockSpec((1, LANES), lambda b: (0, 0))],
        out_specs=pl.BlockSpec((SEQ, LANES), lambda b: (b, 0)), out_shape=jax.ShapeDtypeStruct((T, LANES), F32),
        compiler_params=_params("parallel"),
    )(aux, bf)


def fox_prep_bwd(aux, bf, dcum, *, name):
    T = aux.shape[0]

    def body(f_ref, b_ref, d_ref, df_ref, db_ref):
        row = lax.broadcasted_iota(jnp.int32, (CUM_BLK, CUM_BLK), 0)
        col = lax.broadcasted_iota(jnp.int32, (CUM_BLK, CUM_BLK), 1)
        onward = (col >= row).astype(BF16)
        carry = jnp.zeros((1, LANES), F32)
        tot = jnp.zeros((1, LANES), F32)
        for n in range(SEQ // CUM_BLK - 1, -1, -1):
            rows = slice(n * CUM_BLK, (n + 1) * CUM_BLK)
            d = d_ref[rows, :]
            hi = d.astype(BF16)
            lo = (d - hi.astype(F32)).astype(BF16)
            dlogf = _dot(onward, hi) + _dot(onward, lo) + carry
            carry = dlogf[0:1]
            df = dlogf * jax.nn.sigmoid(-(f_ref[rows, :] + b_ref[...]))
            df_ref[rows, :] = df
            tot = tot + jnp.sum(df, axis=0, keepdims=True)

        @pl.when(pl.program_id(0) == 0)
        def _():
            db_ref[...] = tot

        @pl.when(pl.program_id(0) > 0)
        def _():
            db_ref[...] += tot

    blk = pl.BlockSpec((SEQ, LANES), lambda b: (b, 0))
    vec = pl.BlockSpec((1, LANES), lambda b: (0, 0))
    sd = jax.ShapeDtypeStruct
    return pl.pallas_call(
        body, name=name, grid=(T // SEQ,),
        in_specs=[pl.BlockSpec((SEQ, LANES), lambda b: (b, 4)), vec, blk],
        out_specs=[blk, vec], out_shape=[sd((T, LANES), F32), sd((1, LANES), F32)],
        compiler_params=_params("arbitrary"),
    )(aux, bf, dcum)


def _fox_logits(q, k, cq, ck, diag, causal):
    z = _dot(q, k, _NT) + cq - ck
    return jnp.where(jnp.logical_or(jnp.logical_not(diag), causal), z, NEG)


def fox_fwd(qkv, cumc, cumr, *, name):
    T = qkv.shape[0]

    def body(q_ref, k_ref, v_ref, cc_ref, cr_ref, o_ref, lse_ref, z_s):
        row, col = _iotas()
        causal = col <= row
        lse_ref[...] = jnp.zeros_like(lse_ref)
        for h in range(N_HEADS):
            hs = _hs(h)

            def qblock(i, _):
                q = q_ref[_blk(i), hs] * Q_SCALE
                cq = cc_ref[_blk(i), h:h + 1]

                def logits(j, m):
                    z = _fox_logits(q, k_ref[_blk(j), hs], cq, cr_ref[h:h + 1, _blk(j)], j == i, causal)
                    z_s[j] = z
                    return jnp.maximum(m, jnp.max(z, axis=1, keepdims=True))

                m = lax.fori_loop(0, i + 1, logits, jnp.full((BLOCK, 1), NEG, F32))

                def values(j, carry):
                    acc, l = carry
                    p = jnp.exp(z_s[j] - m)
                    return acc + _dot(p.astype(BF16), v_ref[_blk(j), hs]), l + jnp.sum(p, axis=1, keepdims=True)

                acc, l = lax.fori_loop(0, i + 1, values, (jnp.zeros((BLOCK, HEAD_DIM), F32), jnp.zeros((BLOCK, 1), F32)))
                o_ref[_blk(i), hs] = (acc / l).astype(BF16)
                lse_ref[_blk(i), h:h + 1] = m + jnp.log(l)
                return 0

            lax.fori_loop(0, NB, qblock, 0)

    out = pl.BlockSpec((SEQ, GROUP_W), lambda b: (b, 0))
    colb = pl.BlockSpec((SEQ, LANES), lambda b: (b, 0))
    sd = jax.ShapeDtypeStruct
    return pl.pallas_call(
        body, name=name, grid=(T // SEQ,),
        in_specs=_qkv_specs(3) + [colb, pl.BlockSpec((8, SEQ), lambda b: (b, 0))],
        out_specs=[out, colb], out_shape=[sd((T, GROUP_W), BF16), sd((T, LANES), F32)],
        scratch_shapes=[pltpu.VMEM((NB, BLOCK, BLOCK), F32)],
        compiler_params=_params("parallel"),
    )(qkv, qkv, qkv, cumc, cumr)


def fox_bwd(qkv, cumc, cumr, lse, dmixed, *, name):
    T = qkv.shape[0]

    def body(q_ref, k_ref, v_ref, cc_ref, cr_ref, lse_ref, do_ref, dq_ref, dk_ref, dv_ref, dcc_ref, dcr_ref, p_s, dp_s):
        row, col = _iotas()
        causal = col <= row
        dk_ref[...] = jnp.zeros_like(dk_ref)
        dv_ref[...] = jnp.zeros_like(dv_ref)
        dcc_ref[...] = jnp.zeros_like(dcc_ref)
        dcr_ref[...] = jnp.zeros_like(dcr_ref)
        for h in range(N_HEADS):
            hs = _hs(h)

            def qblock(i, _):
                q = q_ref[_blk(i), hs] * Q_SCALE
                do = do_ref[_blk(i), hs].astype(BF16)
                cq = cc_ref[_blk(i), h:h + 1]
                lse_q = lse_ref[_blk(i), h:h + 1]

                def probs(j, delta):
                    z = _fox_logits(q, k_ref[_blk(j), hs], cq, cr_ref[h:h + 1, _blk(j)], j == i, causal)
                    p = jnp.exp(z - lse_q)
                    dp = _dot(do, v_ref[_blk(j), hs], _NT)
                    p_s[j] = p
                    dp_s[j] = dp
                    return delta + jnp.sum(p * dp, axis=1, keepdims=True)

                delta = lax.fori_loop(0, i + 1, probs, jnp.zeros((BLOCK, 1), F32))

                def kblock(j, carry):
                    dq, dcq = carry
                    p = p_s[j]
                    ds = p * (dp_s[j] - delta)
                    dsb = ds.astype(BF16)
                    dq = dq + _dot(dsb, k_ref[_blk(j), hs])
                    dk_ref[_blk(j), hs] += _dot(dsb, q, _TN)
                    dv_ref[_blk(j), hs] += _dot(p.astype(BF16), do, _TN)
                    dcr_ref[h:h + 1, _blk(j)] -= jnp.sum(ds, axis=0, keepdims=True)
                    return dq, dcq + jnp.sum(ds, axis=1, keepdims=True)

                dq, dcq = lax.fori_loop(0, i + 1, kblock, (jnp.zeros((BLOCK, HEAD_DIM), F32), jnp.zeros((BLOCK, 1), F32)))
                dq_ref[_blk(i), hs] = dq * Q_SCALE
                dcc_ref[_blk(i), h:h + 1] = dcq
                return 0

            lax.fori_loop(0, NB, qblock, 0)

    out = pl.BlockSpec((SEQ, GROUP_W), lambda b: (b, 0))
    colb = pl.BlockSpec((SEQ, LANES), lambda b: (b, 0))
    rowb = pl.BlockSpec((8, SEQ), lambda b: (b, 0))
    sd = jax.ShapeDtypeStruct
    big = sd((T, GROUP_W), F32)
    return pl.pallas_call(
        body, name=name, grid=(T // SEQ,),
        in_specs=_qkv_specs(3) + [colb, rowb, colb, pl.BlockSpec((SEQ, GROUP_W), lambda b: (b, 1))],
        out_specs=[out, out, out, colb, rowb],
        out_shape=[big, big, big, sd((T, LANES), F32), sd((T // SEQ * 8, SEQ), F32)],
        scratch_shapes=[pltpu.VMEM((NB, BLOCK, BLOCK), F32), pltpu.VMEM((NB, BLOCK, BLOCK), F32)],
        compiler_params=_params("parallel"),
    )(qkv, qkv, qkv, cumc, cumr, lse, dmixed)


BAND = 2 * BLOCK


def _t5_bucket_np(dist):
    n = np.maximum(dist, 0)
    max_exact = NUM_BUCKETS // 2
    nf = np.maximum(n, 1).astype(np.float32)
    large = max_exact + (np.log(nf / np.float32(max_exact)) / np.float32(math.log(MAX_DISTANCE / max_exact))
                         * np.float32(NUM_BUCKETS - max_exact)).astype(np.int32)
    large = np.minimum(large, NUM_BUCKETS - 1)
    return np.where(n < max_exact, n, large).astype(np.int32)


def _band_buckets():
    qi = np.arange(BLOCK)[:, None]
    ki = np.arange(BAND)[None, :]
    delta = np.clip(qi - ki + BLOCK, 0, BLOCK)
    return np.stack([_t5_bucket_np(delta * d) for d in DILATIONS])


def to_classes(a, d):
    if d == 1:
        return a
    T, C = a.shape
    return a.reshape(T // SEQ, SEQ // d, d, C).transpose(0, 2, 1, 3).reshape(T, C)


def from_classes(a, d):
    if d == 1:
        return a
    T, C = a.shape
    return a.reshape(T // SEQ, d, SEQ // d, C).transpose(0, 2, 1, 3).reshape(T, C)


def relbias_expand(rel, *, name):
    buckets = jnp.asarray(_band_buckets())
    n_pat = len(DILATIONS)

    def body(rel_ref, bk_ref, o_ref):
        for p in range(n_pat):
            bk = bk_ref[p]
            for h in range(N_HEADS):
                acc = jnp.zeros((BLOCK, BAND), F32)
                for b in range(NUM_BUCKETS):
                    acc = jnp.where(bk == b, rel_ref[b, h], acc)
                o_ref[p * N_HEADS + h] = acc

    return pl.pallas_call(
        body, name=name,
        in_specs=[pl.BlockSpec(memory_space=pltpu.SMEM), pl.BlockSpec(memory_space=pltpu.VMEM)],
        out_specs=pl.BlockSpec(memory_space=pltpu.VMEM),
        out_shape=jax.ShapeDtypeStruct((n_pat * N_HEADS, BLOCK, BAND), F32),
        compiler_params=_params(),
    )(rel, buckets)


def relbias_reduce(ds_all, *, name):
    buckets = jnp.asarray(_band_buckets())
    n_pat = len(DILATIONS)

    def body(ds_ref, bk_ref, o_ref):
        for b in range(NUM_BUCKETS):
            for h in range(N_HEADS):
                tot = jnp.float32(0.0)
                for p in range(n_pat):
                    tot = tot + jnp.sum(jnp.where(bk_ref[p] == b, ds_ref[p * N_HEADS + h], 0.0))
                o_ref[b, h] = tot

    return pl.pallas_call(
        body, name=name,
        in_specs=[pl.BlockSpec(memory_space=pltpu.VMEM), pl.BlockSpec(memory_space=pltpu.VMEM)],
        out_specs=pl.BlockSpec(memory_space=pltpu.SMEM),
        out_shape=jax.ShapeDtypeStruct((NUM_BUCKETS, N_HEADS), F32),
        compiler_params=_params(),
    )(ds_all, buckets)


def _band_valid(first):
    qi = lax.broadcasted_iota(jnp.int32, (BLOCK, BAND), 0)
    ki = lax.broadcasted_iota(jnp.int32, (BLOCK, BAND), 1)
    inside = jnp.logical_and(ki >= qi, ki <= qi + BLOCK)
    return jnp.logical_and(inside, jnp.logical_or(jnp.logical_not(first), ki >= BLOCK))


def _band_specs(pattern):
    cur = lambda c: pl.BlockSpec((BLOCK, GROUP_W), lambda g: (g, c))
    prev = lambda c: pl.BlockSpec((BLOCK, GROUP_W), lambda g: (jnp.maximum(g - 1, 0), c))
    bias = pl.BlockSpec((N_HEADS, BLOCK, BAND), lambda g: (pattern, 0, 0))
    return [cur(0), prev(1), cur(1), prev(2), cur(2), bias]


def band_fwd(qd, bias, pattern, *, name):
    T = qd.shape[0]
    seq_blocks = SEQ // DILATIONS[pattern] // BLOCK

    def body(q_ref, kp_ref, kc_ref, vp_ref, vc_ref, b_ref, o_ref, lse_ref):
        valid = _band_valid(pl.program_id(0) % seq_blocks == 0)
        lse_ref[...] = jnp.zeros_like(lse_ref)
        for h in range(N_HEADS):
            hs = _hs(h)
            q = q_ref[:, hs] * Q_SCALE
            s = jnp.concatenate([_dot(q, kp_ref[:, hs], _NT), _dot(q, kc_ref[:, hs], _NT)], axis=1)
            sc = jnp.where(valid, s + b_ref[h], NEG)
            m = jnp.max(sc, axis=1, keepdims=True)
            p = jnp.exp(sc - m)
            l = jnp.sum(p, axis=1, keepdims=True)
            pb = p.astype(BF16)
            o_ref[:, hs] = (_dot(pb[:, :BLOCK], vp_ref[:, hs]) + _dot(pb[:, BLOCK:], vc_ref[:, hs])) / l
            lse_ref[:, h:h + 1] = m + jnp.log(l)

    sd = jax.ShapeDtypeStruct
    return pl.pallas_call(
        body, name=name, grid=(T // BLOCK,), in_specs=_band_specs(pattern),
        out_specs=[pl.BlockSpec((BLOCK, GROUP_W), lambda g: (g, 0)), pl.BlockSpec((BLOCK, LANES), lambda g: (g, 0))],
        out_shape=[sd((T, GROUP_W), F32), sd((T, LANES), F32)],
        compiler_params=_params("parallel"),
    )(qd, qd, qd, qd, qd, bias)


def band_bwd(qd, bias, lse, do, dlse, pattern, *, name):
    T = qd.shape[0]
    seq_blocks = SEQ // DILATIONS[pattern] // BLOCK

    def body(q_ref, kp_ref, kc_ref, vp_ref, vc_ref, b_ref, lse_ref, do_ref, dlse_ref,
             dq_ref, dkc_ref, dkp_ref, dvc_ref, dvp_ref, ds_ref):
        g = pl.program_id(0)
        valid = _band_valid(g % seq_blocks == 0)
        for h in range(N_HEADS):
            hs = _hs(h)
            q = q_ref[:, hs] * Q_SCALE
            do_h = do_ref[:, hs].astype(BF16)
            s = jnp.concatenate([_dot(q, kp_ref[:, hs], _NT), _dot(q, kc_ref[:, hs], _NT)], axis=1)
            p = jnp.where(valid, jnp.exp(s + b_ref[h] - lse_ref[:, h:h + 1]), 0.0)
            dp = jnp.concatenate([_dot(do_h, vp_ref[:, hs], _NT), _dot(do_h, vc_ref[:, hs], _NT)], axis=1)
            ds = p * (dp - jnp.sum(p * dp, axis=1, keepdims=True) + dlse_ref[:, h:h + 1])
            dsb, pb = ds.astype(BF16), p.astype(BF16)
            dq_ref[:, hs] = (_dot(dsb[:, :BLOCK], kp_ref[:, hs]) + _dot(dsb[:, BLOCK:], kc_ref[:, hs])) * Q_SCALE
            dkp_ref[:, hs] = _dot(dsb[:, :BLOCK], q, _TN)
            dkc_ref[:, hs] = _dot(dsb[:, BLOCK:], q, _TN)
            dvp_ref[:, hs] = _dot(pb[:, :BLOCK], do_h, _TN)
            dvc_ref[:, hs] = _dot(pb[:, BLOCK:], do_h, _TN)

            @pl.when(g == 0)
            def _():
                ds_ref[h] = ds

            @pl.when(g > 0)
            def _():
                ds_ref[h] += ds

    big = pl.BlockSpec((BLOCK, GROUP_W), lambda g: (g, 0))
    colb = pl.BlockSpec((BLOCK, LANES), lambda g: (g, 0))
    sd = jax.ShapeDtypeStruct
    return pl.pallas_call(
        body, name=name, grid=(T // BLOCK,), in_specs=_band_specs(pattern) + [colb, big, colb],
        out_specs=[big] * 5 + [pl.BlockSpec((N_HEADS, BLOCK, BAND), lambda g: (0, 0, 0))],
        out_shape=[sd((T, GROUP_W), F32)] * 5 + [sd((N_HEADS, BLOCK, BAND), F32)],
        compiler_params=_params("arbitrary"),
    )(qd, qd, qd, qd, qd, bias, lse, do, dlse)


def shift_add(cur, prev, *, name):
    T = cur.shape[0]
    nb = T // BLOCK

    def body(c_ref, p_ref, o_ref):
        keep = (pl.program_id(0) < nb - 1).astype(F32)
        o_ref[...] = c_ref[...] + keep * p_ref[...]

    return pl.pallas_call(
        body, name=name, grid=(nb,),
        in_specs=[pl.BlockSpec((BLOCK, GROUP_W), lambda g: (g, 0)), pl.BlockSpec((BLOCK, GROUP_W), lambda g: (jnp.minimum(g + 1, nb - 1), 0))],
        out_specs=pl.BlockSpec((BLOCK, GROUP_W), lambda g: (g, 0)), out_shape=jax.ShapeDtypeStruct((T, GROUP_W), F32),
        compiler_params=_params("parallel"),
    )(cur, prev)


def _pattern_weights(lse_refs, h):
    ls = [r[:, h:h + 1] for r in lse_refs]
    mx = functools.reduce(jnp.maximum, ls)
    es = [jnp.exp(l - mx) for l in ls]
    tot = functools.reduce(lambda a, b: a + b, es)
    return [e / tot for e in es]


def dil_combine_fwd(outs, *, name):
    T = outs[0][0].shape[0]
    n = len(outs)
    tm = 512

    def body(*refs):
        o_refs, l_refs, out_ref = refs[:n], refs[n:2 * n], refs[2 * n]
        for h in range(N_HEADS):
            w = _pattern_weights(l_refs, h)
            acc = w[0] * o_refs[0][:, _hs(h)]
            for p in range(1, n):
                acc = acc + w[p] * o_refs[p][:, _hs(h)]
            out_ref[:, _hs(h)] = acc.astype(BF16)

    big = pl.BlockSpec((tm, GROUP_W), lambda i: (i, 0))
    colb = pl.BlockSpec((tm, LANES), lambda i: (i, 0))
    return pl.pallas_call(
        body, name=name, grid=(T // tm,), in_specs=[big] * n + [colb] * n,
        out_specs=big, out_shape=jax.ShapeDtypeStruct((T, GROUP_W), BF16),
        compiler_params=_params("parallel"),
    )(*[o for o, _ in outs], *[l for _, l in outs])


def dil_combine_bwd(outs, dmixed, *, name):
    T = outs[0][0].shape[0]
    n = len(outs)
    tm = 512

    def body(*refs):
        o_refs, l_refs, do_ref = refs[:n], refs[n:2 * n], refs[2 * n]
        do_refs, dl_refs = refs[2 * n + 1:3 * n + 1], refs[3 * n + 1:]
        for r in dl_refs:
            r[...] = jnp.zeros_like(r)
        for h in range(N_HEADS):
            w = _pattern_weights(l_refs, h)
            do = do_ref[:, _hs(h)]
            dw = [jnp.sum(do * o_refs[p][:, _hs(h)], axis=1, keepdims=True) for p in range(n)]
            mean = functools.reduce(lambda a, b: a + b, [w[p] * dw[p] for p in range(n)])
            for p in range(n):
                do_refs[p][:, _hs(h)] = w[p] * do
                dl_refs[p][:, h:h + 1] = w[p] * (dw[p] - mean)

    big = pl.BlockSpec((tm, GROUP_W), lambda i: (i, 0))
    colb = pl.BlockSpec((tm, LANES), lambda i: (i, 0))
    sd = jax.ShapeDtypeStruct
    res = pl.pallas_call(
        body, name=name, grid=(T // tm,),
        in_specs=[big] * n + [colb] * n + [pl.BlockSpec((tm, GROUP_W), lambda i: (i, 2))],
        out_specs=[big] * n + [colb] * n, out_shape=[sd((T, GROUP_W), F32)] * n + [sd((T, LANES), F32)] * n,
        compiler_params=_params("parallel"),
    )(*[o for o, _ in outs], *[l for _, l in outs], dmixed)
    return list(zip(res[:n], res[n:]))


def dilated_fwd(qkv, bias, tag):
    qd = qkv[:, 6 * GROUP_W:]
    outs = []
    for p, d in enumerate(DILATIONS):
        o, l = band_fwd(to_classes(qd, d), bias, p, name=f"{tag}_band_fwd{p}")
        outs.append((from_classes(o, d), from_classes(l, d)))
    return outs


def dilated_bwd(qkv, bias, outs, dmixed, tag):
    qd = qkv[:, 6 * GROUP_W:]
    grads = dil_combine_bwd(outs, dmixed, name=f"{tag}_combine_bwd")
    parts, ds_all = [], []
    for p, d in enumerate(DILATIONS):
        (_, lse), (do, dlse) = outs[p], grads[p]
        dq, dkc, dkp, dvc, dvp, ds = band_bwd(to_classes(qd, d), bias, to_classes(lse, d), to_classes(do, d),
                                              to_classes(dlse, d), p, name=f"{tag}_band_bwd{p}")
        dk = shift_add(dkc, dkp, name=f"{tag}_dk{p}")
        dv = shift_add(dvc, dvp, name=f"{tag}_dv{p}")
        parts.append(from_classes(jnp.concatenate([dq, dk, dv], axis=1), d))
        ds_all.append(ds)
    return sum_cast(parts, BF16, name=f"{tag}_dqkv_sum"), jnp.concatenate(ds_all, axis=0)


def sum_cast(arrs, dtype, *, name):
    R, C = arrs[0].shape
    tr = _pick(R, (512, 256, 128, 8))
    n = len(arrs)

    def body(*refs):
        acc = refs[0][...].astype(F32)
        for r in refs[1:n]:
            acc = acc + r[...].astype(F32)
        refs[n][...] = acc.astype(dtype)

    blk = pl.BlockSpec((tr, C), lambda i: (i, 0))
    return pl.pallas_call(
        body, name=name, grid=(R // tr,), in_specs=[blk] * n, out_specs=blk, out_shape=jax.ShapeDtypeStruct((R, C), dtype),
        compiler_params=_params("parallel"),
    )(*arrs)


GRAD_WIRE = BF16


def _block_diag_halves(w):
    z = jnp.zeros((HEAD_DIM, HEAD_DIM), w.dtype)
    half = lambda a, b: jnp.concatenate([jnp.concatenate([a, z], axis=1), jnp.concatenate([z, b], axis=1)], axis=0)
    return jnp.stack([half(w[0], w[1]), half(w[2], w[3])]).astype(BF16)


def _diag_blocks(d):
    h = HEAD_DIM
    return jnp.stack([d[0, :h, :h], d[0, h:, h:], d[1, :h, :h], d[1, h:, h:]])


def layer_fwd(x, mem2d, W, P, bias, tag):
    s = {}
    s["x"] = x
    h1 = rmsnorm_fwd(x, P["norm_mix_g"], name=f"{tag}_norm_mix")
    qkv = matmul(h1, W["qkv"], out_dtype=BF16, name=f"{tag}_qkv")
    aux = matmul(h1, W["aux"], name=f"{tag}_aux")
    o_sb = sb_fwd(qkv, name=f"{tag}_sb_fwd")
    cumc = fox_prep(aux, P["bf"], name=f"{tag}_fox_prep")
    cumr = col_to_row(cumc)
    o_fox, lse_fox = fox_fwd(qkv, cumc, cumr, name=f"{tag}_fox_fwd")
    dil = dilated_fwd(qkv, bias, tag)
    o_dil = dil_combine_fwd(dil, name=f"{tag}_dil_combine")
    o_lru, h_lru = lru_fwd(aux, P["lru_conv_w"], P["lru_conv_b"], P["wa"], P["lru_b_a"], P["wx"], P["lru_b_x"],
                           P["lru_lambda"], name=f"{tag}_lru_fwd")
    mixed = jnp.concatenate([o_sb, o_fox, o_dil, o_lru], axis=1)
    x1 = matmul(mixed, W["out"], residual=x, name=f"{tag}_out")
    hq = rmsnorm_fwd(x1, P["norm_cross_g"], name=f"{tag}_norm_cross")
    qc = matmul(hq, W["cq"], out_dtype=BF16, name=f"{tag}_cq")
    memn = rmsnorm_fwd(mem2d, P["norm_mem_g"], name=f"{tag}_norm_mem")
    kv = matmul(memn, W["ckv"], out_dtype=BF16, name=f"{tag}_ckv")
    oc = cross_fwd(qc, kv, name=f"{tag}_cross_fwd")
    x2 = matmul(oc, W["coT"], trans_b=True, residual=x1, name=f"{tag}_co")
    h2 = rmsnorm_fwd(x2, P["norm_ffn_g"], name=f"{tag}_norm_ffn")
    hu = matmul(h2, W["up_u"], trans_b=True, name=f"{tag}_up_u")
    hg = matmul(h2, W["up_g"], trans_b=True, name=f"{tag}_up_g")
    act = glu_fwd(hu, hg, P["wu"], P["wg"], P["bu"], P["bg"], name=f"{tag}_glu_fwd")
    x3 = matmul(act, W["down"], residual=x2, name=f"{tag}_down")
    s.update(h1=h1, qkv=qkv, aux=aux, cumc=cumc, cumr=cumr, lse_fox=lse_fox, dil=dil, h_lru=h_lru, mixed=mixed,
             x1=x1, hq=hq, qc=qc, memn=memn, kv=kv, oc=oc, x2=x2, h2=h2, hu=hu, hg=hg, act=act)
    return x3, s


def layer_bwd(dx3, mem2d, W, P, bias, s, tag):
    mm = functools.partial(matmul, out_dtype=GRAD_WIRE, trans_a=True)
    gW, gP = {}, {}
    dact = matmul(dx3, W["down"], trans_b=True, name=f"{tag}_d_act")
    gW["down"] = mm(s["act"], dx3, name=f"{tag}_g_down")
    dcu, dcg, dwu, dwg, dbu, dbg = glu_bwd(s["hu"], s["hg"], dact, P["wu"], P["wg"], P["bu"], P["bg"], name=f"{tag}_glu_bwd")
    gP["ffn_conv_w"] = jnp.concatenate([dwu, dwg], axis=1)
    gP["ffn_conv_b"] = jnp.concatenate([dbu, dbg], axis=1)
    dhu = conv3_transpose(dcu, P["wu"], name=f"{tag}_convT_u")
    dhg = conv3_transpose(dcg, P["wg"], name=f"{tag}_convT_g")
    dh2 = matmul(dhu, W["up_u"], name=f"{tag}_d_h2u")
    dh2 = matmul(dhg, W["up_g"], residual=dh2, name=f"{tag}_d_h2g")
    gW["up_u"] = mm(dhu, s["h2"], name=f"{tag}_g_up_u")
    gW["up_g"] = mm(dhg, s["h2"], name=f"{tag}_g_up_g")
    dx2, gP["norm_ffn_g"] = rmsnorm_bwd(s["x2"], P["norm_ffn_g"], dh2, dx3, name=f"{tag}_norm_ffn_bwd")
    doc = matmul(dx2, W["coT"], name=f"{tag}_d_oc")
    gW["coT"] = mm(dx2, s["oc"], name=f"{tag}_g_co")
    dqc, dkv = cross_bwd(s["qc"], s["kv"], doc, name=f"{tag}_cross_bwd")
    dhq = matmul(dqc, W["cq"], trans_b=True, name=f"{tag}_d_hq")
    gW["cq"] = mm(s["hq"], dqc, name=f"{tag}_g_cq")
    dmemn = matmul(dkv, W["ckv"], trans_b=True, name=f"{tag}_d_memn")
    gW["ckv"] = mm(s["memn"], dkv, name=f"{tag}_g_ckv")
    _, gP["norm_mem_g"] = rmsnorm_bwd(mem2d, P["norm_mem_g"], dmemn, None, name=f"{tag}_norm_mem_bwd")
    dx1, gP["norm_cross_g"] = rmsnorm_bwd(s["x1"], P["norm_cross_g"], dhq, dx2, name=f"{tag}_norm_cross_bwd")
    dmixed = matmul(dx1, W["out"], trans_b=True, name=f"{tag}_d_mixed")
    gW["out"] = mm(s["mixed"], dx1, name=f"{tag}_g_out")
    qkv, aux = s["qkv"], s["aux"]
    d_sb = sb_bwd(qkv, dmixed, name=f"{tag}_sb_bwd")
    dfq, dfk, dfv, dcc, dcr = fox_bwd(qkv, s["cumc"], s["cumr"], s["lse_fox"], dmixed, name=f"{tag}_fox_bwd")
    dcum = sum_cast([dcc, row_to_col(dcr)], F32, name=f"{tag}_dcum")
    df, dbf = fox_prep_bwd(aux, P["bf"], dcum, name=f"{tag}_fox_prep_bwd")
    gP["b_forget"] = dbf[0, :N_HEADS]
    d_dil, ds_band = dilated_bwd(qkv, bias, s["dil"], dmixed, tag)
    dlx, dlg, dcw, dcb, dwa, dba, dwx, dbx, dlam = lru_bwd(
        aux, s["h_lru"], dmixed, P["lru_conv_w"], P["lru_conv_b"], P["wa"], P["lru_b_a"], P["wx"], P["lru_b_x"],
        P["lru_lambda"], name=f"{tag}_lru_bwd")
    gP.update(lru_conv_w=dcw, lru_conv_b=dcb, lru_w_a=_diag_blocks(dwa), lru_b_a=dba, lru_w_x=_diag_blocks(dwx),
              lru_b_x=dbx, lru_lambda=dlam)
    dqkv = jnp.concatenate([d.astype(BF16) for d in d_sb] + [dfq.astype(BF16), dfk.astype(BF16), dfv.astype(BF16), d_dil], axis=1)
    daux = jnp.concatenate([dlx, dlg, df], axis=1)
    dh1 = matmul(dqkv, W["qkv"], trans_b=True, name=f"{tag}_d_h1a")
    dh1 = matmul(daux, W["aux"], trans_b=True, residual=dh1, name=f"{tag}_d_h1b")
    gW["qkv"] = mm(s["h1"], dqkv, name=f"{tag}_g_qkv")
    gW["aux"] = mm(s["h1"], daux, name=f"{tag}_g_aux")
    dx, gP["norm_mix_g"] = rmsnorm_bwd(s["x"], P["norm_mix_g"], dh1, dx1, name=f"{tag}_norm_mix_bwd")
    return dx, gW, gP, ds_band


def local_step(x, mem, target, Ws, Ps, rel_bias, final_norm_g):
    B = x.shape[0]
    x2d = x.reshape(B * SEQ, D_MODEL)
    mem2d = mem.reshape(B * N_MEM, D_MODEL)
    bias = relbias_expand(rel_bias, name="relbias_expand")
    saved = []
    h = x2d
    for l in range(DEPTH):
        h, s = layer_fwd(h, mem2d, Ws[l], Ps[l], bias, f"l{l}")
        saved.append(s)
    loss, dh, d_final = loss_head(h, final_norm_g, target.reshape(B * SEQ, D_MODEL), name="loss_head")
    gWs, gPs, ds_bands = [None] * DEPTH, [None] * DEPTH, []
    for l in range(DEPTH - 1, -1, -1):
        dh, gWs[l], gPs[l], ds = layer_bwd(dh, mem2d, Ws[l], Ps[l], bias, saved[l], f"l{l}")
        ds_bands.append(ds)
    d_rel = relbias_reduce(sum_cast([d.reshape(-1, BAND) for d in ds_bands], F32, name="ds_band_sum").reshape(-1, BLOCK, BAND),
                           name="relbias_reduce")
    return loss, dh.reshape(B, SEQ, D_MODEL), gWs, gPs, d_rel, d_final


def small_params(p, l):
    row = lambda name: p[name][l].reshape(1, -1)
    ffn_w, ffn_b = p["ffn_conv_w"][l], row("ffn_conv_b")
    return dict(
        norm_mix_g=row("norm_mix_g"), norm_cross_g=row("norm_cross_g"), norm_mem_g=row("norm_mem_g"), norm_ffn_g=row("norm_ffn_g"),
        bf=jnp.pad(row("b_forget"), ((0, 0), (0, LANES - N_HEADS))),
        lru_conv_w=p["lru_conv_w"][l], lru_conv_b=row("lru_conv_b"), wa=_block_diag_halves(p["lru_w_a"][l]), lru_b_a=row("lru_b_a"),
        wx=_block_diag_halves(p["lru_w_x"][l]), lru_b_x=row("lru_b_x"), lru_lambda=row("lru_lambda"),
        wu=ffn_w[:, :D_FF], wg=ffn_w[:, D_FF:], bu=ffn_b[:, :D_FF], bg=ffn_b[:, D_FF:])


def canonical_weights(w_in, w_out, w_cq, w_ck, w_cv, w_co, w_up, w_down):
    sb_fox, fox_f, rest = w_in[:, :6 * GROUP_W], w_in[:, 6 * GROUP_W:6 * GROUP_W + N_HEADS], w_in[:, 6 * GROUP_W + N_HEADS:]
    dil, lru = rest[:, :3 * GROUP_W], rest[:, 3 * GROUP_W:]
    pad = jnp.zeros((w_in.shape[0], AUX_W - 2 * GROUP_W - N_HEADS), w_in.dtype)
    return dict(qkv=jnp.concatenate([sb_fox, dil], axis=1), aux=jnp.concatenate([lru, fox_f, pad], axis=1), out=w_out,
                cq=w_cq, ckv=jnp.concatenate([w_ck, w_cv], axis=1), coT=w_co.T, upT=w_up.T, down=w_down)


def native_grads(g):
    qkv, aux = g["qkv"], g["aux"]
    w_in = jnp.concatenate([qkv[:, :6 * GROUP_W], aux[:, 2 * GROUP_W:2 * GROUP_W + N_HEADS], qkv[:, 6 * GROUP_W:],
                            aux[:, :2 * GROUP_W]], axis=1)
    return (w_in, g["out"], g["cq"], g["ckv"][:, :GROUP_W], g["ckv"][:, GROUP_W:], g["coT"].T, g["upT"].T, g["down"])


ANY = pl.BlockSpec(memory_space=pl.ANY)
VMEM_SPEC = pl.BlockSpec(memory_space=pltpu.VMEM)


def _place():
    x, y, c = lax.axis_index("x"), lax.axis_index("y"), lax.axis_index("c")
    other_chips = [(1 - x, y), (x, 1 - y), (1 - x, 1 - y)]
    return x, y, c, other_chips


def _gather_body(x_ref, out_ref, send_sems, recv_sems, local_sem):
    x, y, c, chips = _place()
    me, sibling = (x, y, c), (x, y, 1 - c)

    def slot(px, py, pc):
        return out_ref.at[4 * px + 2 * py + pc]

    def copy(k, block, to, src=None):
        return pltpu.make_async_remote_copy(
            src_ref=slot(*block) if src is None else src, dst_ref=slot(*block),
            send_sem=send_sems.at[k], recv_sem=recv_sems.at[k], device_id=to, device_id_type=MESH)

    mine = pltpu.make_async_copy(x_ref, slot(*me), local_sem)
    mine.start()
    first = [copy(0, me, sibling, src=x_ref)]
    first += [copy(1 + j, me, (*chip, c), src=x_ref) for j, chip in enumerate(chips)]
    for cp in first:
        cp.start()
    passed = [copy(4 + j, (*chip, c), sibling) for j, chip in enumerate(chips)]
    for j, chip in enumerate(chips):
        copy(1 + j, (*chip, c), me).wait_recv()
        passed[j].start()
    copy(0, sibling, me).wait_recv()
    for j, chip in enumerate(chips):
        copy(4 + j, (*chip, 1 - c), me).wait_recv()
    for cp in first + passed:
        cp.wait_send()
    mine.wait()


_GATHER_SEMS = [pltpu.SemaphoreType.DMA((7,)), pltpu.SemaphoreType.DMA((7,)), pltpu.SemaphoreType.DMA]


def allgather_hbm(shard, *, name):
    def body(x_ref, out_ref, send_sems, recv_sems, local_sem):
        _gather_body(x_ref, out_ref, send_sems, recv_sems, local_sem)

    return pl.pallas_call(
        body, name=name, in_specs=[ANY], out_specs=ANY,
        out_shape=jax.ShapeDtypeStruct((N_DEV,) + shard.shape, shard.dtype), scratch_shapes=_GATHER_SEMS,
    )(shard)


def allgather_small(x, *, name, reduce=False):
    def body(x_ref, out_ref, *rest):
        _gather_body(x_ref, out_ref, *rest[-3:])
        if reduce:
            acc = out_ref[0]
            for d in range(1, N_DEV):
                acc = acc + out_ref[d]
            rest[0][...] = acc

    sd = jax.ShapeDtypeStruct
    return pl.pallas_call(
        body, name=name, in_specs=[VMEM_SPEC], out_specs=[VMEM_SPEC, VMEM_SPEC] if reduce else VMEM_SPEC,
        out_shape=[sd((N_DEV,) + x.shape, x.dtype), sd(x.shape, x.dtype)] if reduce else sd((N_DEV,) + x.shape, x.dtype),
        scratch_shapes=_GATHER_SEMS, compiler_params=pltpu.CompilerParams(vmem_limit_bytes=VMEM_LIMIT_V7X),
    )(x)


N_CHIPS = 4


def pair_exchange(g, *, name):
    _, R, C = g.shape

    def body(g_ref, own_ref, recv_ref, send_sems, recv_sems, local_sems):
        x, y, c, _ = _place()
        sibling = (x, y, 1 - c)
        local = [pltpu.make_async_copy(g_ref.at[2 * q + c], own_ref.at[q], local_sems.at[q]) for q in range(N_CHIPS)]
        remote = [pltpu.make_async_remote_copy(
            src_ref=g_ref.at[2 * q + (1 - c)], dst_ref=recv_ref.at[q], send_sem=send_sems.at[q], recv_sem=recv_sems.at[q],
            device_id=sibling, device_id_type=MESH) for q in range(N_CHIPS)]
        for cp in remote + local:
            cp.start()
        for cp in remote:
            cp.wait_recv()
        for cp in remote:
            cp.wait_send()
        for cp in local:
            cp.wait()

    sd = jax.ShapeDtypeStruct((N_CHIPS, R, C), g.dtype)
    return pl.pallas_call(
        body, name=name, in_specs=[ANY], out_specs=[ANY, ANY], out_shape=[sd, sd],
        scratch_shapes=[pltpu.SemaphoreType.DMA((N_CHIPS,))] * 3,
    )(g)


def chip_exchange(s, *, name):
    _, R, C = s.shape

    def body(s_ref, o0, o1, o2, o3, send_sems, recv_sems, local_sem):
        x, y, c, chips = _place()
        outs = (o0, o1, o2)
        mine = pltpu.make_async_copy(s_ref.at[2 * x + y], o3, local_sem)
        mine.start()
        copies = [pltpu.make_async_remote_copy(
            src_ref=s_ref.at[2 * cx + cy], dst_ref=outs[j], send_sem=send_sems.at[j], recv_sem=recv_sems.at[j],
            device_id=(cx, cy, c), device_id_type=MESH) for j, (cx, cy) in enumerate(chips)]
        for cp in copies:
            cp.start()
        for cp in copies:
            cp.wait_recv()
        for cp in copies:
            cp.wait_send()
        mine.wait()

    sd = jax.ShapeDtypeStruct((R, C), s.dtype)
    return pl.pallas_call(
        body, name=name, in_specs=[ANY], out_specs=[ANY] * 4, out_shape=[sd] * 4,
        scratch_shapes=[pltpu.SemaphoreType.DMA((3,)), pltpu.SemaphoreType.DMA((3,)), pltpu.SemaphoreType.DMA],
    )(s)


WEIGHTS = ("norm_mix_g", "w_in", "b_forget", "lru_conv_w", "lru_conv_b", "lru_w_a", "lru_b_a", "lru_w_x", "lru_b_x", "lru_lambda",
           "w_out", "norm_cross_g", "norm_mem_g", "w_cq", "w_ck", "w_cv", "w_co", "norm_ffn_g", "w_up", "ffn_conv_w", "ffn_conv_b",
           "w_down", "rel_bias", "final_norm_g")
LARGE = ("w_in", "w_out", "w_cq", "w_ck", "w_cv", "w_co", "w_up", "w_down")
COLUMN_SPLIT_SMALL = ("lru_conv_w", "ffn_conv_w")
PACK = (("qkv", 128, 2304), ("aux", 128, 640), ("out", 128, 1024), ("cq", 128, 256), ("ckv", 128, 512), ("coT", 128, 256),
        ("upT", 704, 1024), ("down", 352, 1024))
PACK_W = 1024


def _pack_rows(parts):
    return jnp.concatenate([p.reshape(-1, PACK_W) for p in parts], axis=0)


def _pad_rows(flat, mult=8 * LANES):
    n = flat.shape[0]
    return jnp.pad(flat, (0, (-n) % mult)).reshape(-1, LANES)


def kernel(x, mem, norm_mix_g, w_in, b_forget, lru_conv_w, lru_conv_b, lru_w_a, lru_b_a, lru_w_x, lru_b_x, lru_lambda, w_out, norm_cross_g, norm_mem_g, w_cq, w_ck, w_cv, w_co, norm_ffn_g, w_up, ffn_conv_w, ffn_conv_b, w_down, rel_bias, final_norm_g, loss_target, m_norm_mix_g, m_w_in, m_b_forget, m_lru_conv_w, m_lru_conv_b, m_lru_w_a, m_lru_b_a, m_lru_w_x, m_lru_b_x, m_lru_lambda, m_w_out, m_norm_cross_g, m_norm_mem_g, m_w_cq, m_w_ck, m_w_cv, m_w_co, m_norm_ffn_g, m_w_up, m_ffn_conv_w, m_ffn_conv_b, m_w_down, m_rel_bias, m_final_norm_g, v_norm_mix_g, v_w_in, v_b_forget, v_lru_conv_w, v_lru_conv_b, v_lru_w_a, v_lru_b_a, v_lru_w_x, v_lru_b_x, v_lru_lambda, v_w_out, v_norm_cross_g, v_norm_mem_g, v_w_cq, v_w_ck, v_w_cv, v_w_co, v_norm_ffn_g, v_w_up, v_ffn_conv_w, v_ffn_conv_b, v_w_down, v_rel_bias, v_final_norm_g):
    w = dict(norm_mix_g=norm_mix_g, w_in=w_in, b_forget=b_forget, lru_conv_w=lru_conv_w, lru_conv_b=lru_conv_b, lru_w_a=lru_w_a,
             lru_b_a=lru_b_a, lru_w_x=lru_w_x, lru_b_x=lru_b_x, lru_lambda=lru_lambda, w_out=w_out, norm_cross_g=norm_cross_g,
             norm_mem_g=norm_mem_g, w_cq=w_cq, w_ck=w_ck, w_cv=w_cv, w_co=w_co, norm_ffn_g=norm_ffn_g, w_up=w_up,
             ffn_conv_w=ffn_conv_w, ffn_conv_b=ffn_conv_b, w_down=w_down, rel_bias=rel_bias, final_norm_g=final_norm_g)
    m = dict(norm_mix_g=m_norm_mix_g, w_in=m_w_in, b_forget=m_b_forget, lru_conv_w=m_lru_conv_w, lru_conv_b=m_lru_conv_b,
             lru_w_a=m_lru_w_a, lru_b_a=m_lru_b_a, lru_w_x=m_lru_w_x, lru_b_x=m_lru_b_x, lru_lambda=m_lru_lambda, w_out=m_w_out,
             norm_cross_g=m_norm_cross_g, norm_mem_g=m_norm_mem_g, w_cq=m_w_cq, w_ck=m_w_ck, w_cv=m_w_cv, w_co=m_w_co,
             norm_ffn_g=m_norm_ffn_g, w_up=m_w_up, ffn_conv_w=m_ffn_conv_w, ffn_conv_b=m_ffn_conv_b, w_down=m_w_down,
             rel_bias=m_rel_bias, final_norm_g=m_final_norm_g)
    v = dict(norm_mix_g=v_norm_mix_g, w_in=v_w_in, b_forget=v_b_forget, lru_conv_w=v_lru_conv_w, lru_conv_b=v_lru_conv_b,
             lru_w_a=v_lru_w_a, lru_b_a=v_lru_b_a, lru_w_x=v_lru_w_x, lru_b_x=v_lru_b_x, lru_lambda=v_lru_lambda, w_out=v_w_out,
             norm_cross_g=v_norm_cross_g, norm_mem_g=v_norm_mem_g, w_cq=v_w_cq, w_ck=v_w_ck, w_cv=v_w_cv, w_co=v_w_co,
             norm_ffn_g=v_norm_ffn_g, w_up=v_w_up, ffn_conv_w=v_ffn_conv_w, ffn_conv_b=v_ffn_conv_b, w_down=v_w_down,
             rel_bias=v_rel_bias, final_norm_g=v_final_norm_g)
    me = 4 * lax.axis_index("x") + 2 * lax.axis_index("y") + lax.axis_index("c")

    conv_shard = jnp.concatenate([w[n].reshape(-1) for n in COLUMN_SPLIT_SMALL])
    conv_all = allgather_small(_pad_rows(conv_shard), name="gather_conv").reshape(N_DEV, -1)
    full = dict(w)
    off = 0
    for n in COLUMN_SPLIT_SMALL:
        d, k, c = w[n].shape
        blocks = conv_all[:, off:off + d * k * c].reshape(N_DEV, d, k, c)
        full[n] = blocks.transpose(1, 2, 0, 3).reshape(d, k, N_DEV * c)
        off += d * k * c

    shard_parts = []
    for l in range(DEPTH):
        canon = canonical_weights(*[w[n][l] for n in LARGE])
        shard_parts += [canon[k].astype(BF16) for k, _, _ in PACK]
    packed = allgather_hbm(_pack_rows(shard_parts), name="gather_weights")
    Ws, row = [], 0
    for l in range(DEPTH):
        W = {}
        for k, r, c in PACK:
            n_rows = r * c // PACK_W
            W[k] = packed[:, row:row + n_rows].reshape(N_DEV * r, c)
            row += n_rows
        upT = W.pop("upT")
        W["up_u"], W["up_g"] = upT[:D_FF], upT[D_FF:]
        Ws.append(W)
    Ps = [small_params(full, l) for l in range(DEPTH)]

    loss, grad_x, gWs, gPs, d_rel, d_final = local_step(x, mem, loss_target, Ws, Ps, rel_bias, final_norm_g.reshape(1, -1))

    grad_parts = []
    for l in range(DEPTH):
        g = dict(gWs[l])
        g["upT"] = jnp.concatenate([g.pop("up_u"), g.pop("up_g")], axis=0)
        grad_parts += [g[k].reshape(N_DEV, r * c // PACK_W, PACK_W) for k, r, c in PACK]
    g_all = jnp.concatenate(grad_parts, axis=1)
    rows = g_all.shape[1]
    own, got = pair_exchange(g_all, name="grads_pair_exchange")
    pair = sum_cast([own.reshape(-1, PACK_W), got.reshape(-1, PACK_W)], GRAD_WIRE, name="grads_pair_sum").reshape(N_CHIPS, rows, PACK_W)
    from_x, from_y, from_xy, mine = chip_exchange(pair, name="grads_chip_exchange")
    g_shard = sum_cast([mine, from_x, from_y, from_xy], F32, name="grads_chip_sum")
    grads, row = {}, 0
    per_layer = []
    for l in range(DEPTH):
        g = {}
        for k, r, c in PACK:
            n_rows = r * c // PACK_W
            g[k] = g_shard[row:row + n_rows].reshape(r, c)
            row += n_rows
        per_layer.append(native_grads(g))
    for i, n in enumerate(LARGE):
        grads[n] = jnp.stack([per_layer[l][i] for l in range(DEPTH)])

    small_names = [n for n in WEIGHTS if n not in LARGE and n not in ("rel_bias", "final_norm_g")]
    pieces = [gPs[l][n].reshape(-1) for n in small_names for l in range(DEPTH)] + [d_rel.reshape(-1), d_final.reshape(-1), loss[0, :1]]
    sizes = [p.shape[0] for p in pieces]
    _, total = allgather_small(_pad_rows(jnp.concatenate(pieces)), name="allreduce_small", reduce=True)
    total = total.reshape(-1)
    off, it = 0, iter(sizes)
    for n in small_names:
        per = []
        for l in range(DEPTH):
            sz = next(it)
            per.append(total[off:off + sz])
            off += sz
        full_shape = (DEPTH,) + full[n].shape[1:]
        gfull = jnp.stack(per).reshape(full_shape)
        if n in COLUMN_SPLIT_SMALL:
            c = w[n].shape[-1]
            gfull = lax.dynamic_slice_in_dim(gfull, me * c, c, axis=gfull.ndim - 1)
        grads[n] = gfull
    grads["rel_bias"] = total[off:off + rel_bias.size].reshape(rel_bias.shape)
    off += rel_bias.size
    grads["final_norm_g"] = total[off:off + D_MODEL]
    off += D_MODEL
    loss_out = total[off]

    delta, new_m, new_v = {}, {}, {}
    for n in LARGE:
        shape = w[n].shape
        two_d = lambda a: a.reshape(-1, shape[-1])
        d_, m_, v_ = adamw(two_d(w[n]), two_d(grads[n]), two_d(m[n]), two_d(v[n]), name=f"adamw_{n}")
        delta[n], new_m[n], new_v[n] = d_.reshape(shape), m_.reshape(shape), v_.reshape(shape)
    small_all = [n for n in WEIGHTS if n not in LARGE]
    flat = lambda src: _pad_rows(jnp.concatenate([src[n].reshape(-1) for n in small_all]))
    d_, m_, v_ = adamw(flat(w), flat(grads), flat(m), flat(v), name="adamw_small")
    off = 0
    for n in small_all:
        sz, shape = w[n].size, w[n].shape
        delta[n], new_m[n], new_v[n] = (a.reshape(-1)[off:off + sz].reshape(shape) for a in (d_, m_, v_))
        off += sz

    return (loss_out, grad_x, *[grads[n] for n in WEIGHTS], *[delta[n] for n in WEIGHTS], *[new_m[n] for n in WEIGHTS],
            *[new_v[n] for n in WEIGHTS])
```

```python
import functools
import math

import numpy as np
import jax
import jax.numpy as jnp
from jax import lax
from jax.experimental import pallas as pl
from jax.experimental.pallas import tpu as pltpu

F32 = jnp.float32
BF16 = jnp.bfloat16
MESH = pl.DeviceIdType.MESH

N_DEV = 8
D_MODEL = 1024
SEQ = 2048
DEPTH = 2
HEAD_DIM = 64
N_HEADS = 4
GROUP_W = N_HEADS * HEAD_DIM
D_FF = 2816
N_MEM = 256
NUM_BUCKETS = 32
MAX_DISTANCE = 2048
BLOCK = 128
DILATIONS = (1, 4, 16)
EPS = 1e-6
LRU_C = 8.0
Q_SCALE = HEAD_DIM ** -0.5
AUX_W = 640
LRU_HALF_W = 128
LRU_HALVES = GROUP_W // LRU_HALF_W
ADAM_LR, ADAM_B1, ADAM_B2, ADAM_EPS, ADAM_WD, ADAM_STEP = 0.001, 0.9, 0.999, 1e-08, 0.01, 10

VMEM_LIMIT_V7X = 48 * 1024 * 1024


def _params(*sem):
    return pltpu.CompilerParams(dimension_semantics=sem if sem else None, vmem_limit_bytes=VMEM_LIMIT_V7X)


def _pick(n, cands):
    for c in cands:
        if n % c == 0:
            return c
    return n


def _largest_tile(n, cap, align):
    best = None
    for t in range(align, min(n, cap) + 1, align):
        if n % t == 0:
            best = t
    return n if best is None else best


def matmul(a, b, *, name, trans_a=False, trans_b=False, out_dtype=F32, residual=None):
    (K, M) = a.shape if trans_a else a.shape[::-1]
    (N, Kb) = b.shape if trans_b else b.shape[::-1]
    assert K == Kb, (a.shape, b.shape)
    tm = _largest_tile(M, 512, 128)
    tn = _largest_tile(N, 1408, 128)
    tk = _largest_tile(K, 2816, 128)
    nk = K // tk
    a_spec = pl.BlockSpec((tk, tm), lambda i, j, k: (k, i)) if trans_a else pl.BlockSpec((tm, tk), lambda i, j, k: (i, k))
    b_spec = pl.BlockSpec((tn, tk), lambda i, j, k: (j, k)) if trans_b else pl.BlockSpec((tk, tn), lambda i, j, k: (k, j))
    o_spec = pl.BlockSpec((tm, tn), lambda i, j, k: (i, j))
    dims = (((0 if trans_a else 1,), (1 if trans_b else 0,)), ((), ()))
    has_res = residual is not None

    def body(*refs):
        a_ref, b_ref = refs[0], refs[1]
        r_ref = refs[2] if has_res else None
        part = lax.dot_general(a_ref[...].astype(BF16), b_ref[...].astype(BF16), dims, preferred_element_type=F32)
        if nk == 1:
            if has_res:
                part = part + r_ref[...].astype(F32)
            refs[-1][...] = part.astype(out_dtype)
            return
        o_ref, acc_ref = refs[-2], refs[-1]
        k = pl.program_id(2)

        @pl.when(k == 0)
        def _():
            acc_ref[...] = part

        @pl.when(k > 0)
        def _():
            acc_ref[...] += part

        @pl.when(k == nk - 1)
        def _():
            r = acc_ref[...]
            if has_res:
                r = r + r_ref[...].astype(F32)
            o_ref[...] = r.astype(out_dtype)

    ops = (a, b) + ((residual,) if has_res else ())
    return pl.pallas_call(
        body, name=name, grid=(M // tm, N // tn, nk),
        in_specs=[a_spec, b_spec] + ([o_spec] if has_res else []),
        out_specs=o_spec, out_shape=jax.ShapeDtypeStruct((M, N), out_dtype),
        scratch_shapes=[pltpu.VMEM((tm, tn), F32)] if nk > 1 else [],
        compiler_params=_params("parallel", "parallel", "arbitrary"),
    )(*ops)


def rmsnorm_fwd(x, g, *, name):
    R, D = x.shape
    tr = _pick(R, (512, 256))

    def body(x_ref, g_ref, o_ref):
        xv = x_ref[...]
        r = lax.rsqrt(jnp.mean(xv * xv, axis=-1, keepdims=True) + EPS)
        o_ref[...] = (xv * r * g_ref[...]).astype(BF16)

    return pl.pallas_call(
        body, name=name, grid=(R // tr,),
        in_specs=[pl.BlockSpec((tr, D), lambda i: (i, 0)), pl.BlockSpec((1, D), lambda i: (0, 0))],
        out_specs=pl.BlockSpec((tr, D), lambda i: (i, 0)), out_shape=jax.ShapeDtypeStruct((R, D), BF16),
        compiler_params=_params("parallel"),
    )(x, g)


def rmsnorm_bwd(x, g, dh, dres, *, name):
    R, D = x.shape
    tr = _pick(R, (512, 256))
    has_res = dres is not None

    def body(*refs):
        x_ref, g_ref, dh_ref = refs[:3]
        dx_ref, dg_ref = refs[-2], refs[-1]
        xv = x_ref[...]
        r = lax.rsqrt(jnp.mean(xv * xv, axis=-1, keepdims=True) + EPS)
        n = xv * r
        dhv = dh_ref[...]
        dn = dhv * g_ref[...]
        dx = r * (dn - n * jnp.mean(dn * n, axis=-1, keepdims=True))
        if has_res:
            dx = dx + refs[3][...]
        dx_ref[...] = dx
        part = jnp.sum(dhv * n, axis=0, keepdims=True)

        @pl.when(pl.program_id(0) == 0)
        def _():
            dg_ref[...] = part

        @pl.when(pl.program_id(0) > 0)
        def _():
            dg_ref[...] += part

    row = pl.BlockSpec((tr, D), lambda i: (i, 0))
    vec = pl.BlockSpec((1, D), lambda i: (0, 0))
    ops = (x, g, dh) + ((dres,) if has_res else ())
    return pl.pallas_call(
        body, name=name, grid=(R // tr,),
        in_specs=[row, vec, row] + ([row] if has_res else []),
        out_specs=[row, vec],
        out_shape=[jax.ShapeDtypeStruct((R, D), F32), jax.ShapeDtypeStruct((1, D), F32)],
        compiler_params=_params("arbitrary"),
    )(*ops)


_SQRT_HALF = 0.7071067811865476
_INV_SQRT_2PI = 0.3989422804014327


def _erf(x):
    ax = jnp.abs(x)
    t = 1.0 / (1.0 + 0.3275911 * ax)
    poly = t * (0.254829592 + t * (-0.284496736 + t * (1.421413741 + t * (-1.453152027 + t * 1.061405429))))
    y = 1.0 - poly * jnp.exp(-ax * ax)
    return jnp.where(x < 0, -y, y)


def _gelu_cdf(x):
    return 0.5 * (1.0 + _erf(x * _SQRT_HALF))


def _gelu_and_grad(x):
    cdf = _gelu_cdf(x)
    return x * cdf, cdf + x * _INV_SQRT_2PI * jnp.exp(-0.5 * x * x)


def _shift_down(main, halo, first, shifts):
    halo = jnp.where(first, 0.0, halo)
    ext = jnp.concatenate([halo, main], axis=0)
    return [pltpu.roll(ext, s, 0)[8:] for s in shifts]


def _conv3(main, halo, first, w, b):
    m1, m2 = _shift_down(main, halo, first, (1, 2))
    return ((b + w[0:1] * m2) + w[1:2] * m1) + w[2:3] * main, m1, m2


def glu_fwd(hu, hg, wu, wg, bu, bg, *, name):
    T, F = hu.shape
    tm, tf = 256, _pick(F, (256, 128))
    hb = tm // 8
    blocks_per_example = SEQ // tm

    def body(hu_ref, hg_ref, hau_ref, hag_ref, wu_ref, wg_ref, bu_ref, bg_ref, o_ref):
        first = pl.program_id(0) % blocks_per_example == 0
        up, _, _ = _conv3(hu_ref[...], hau_ref[...], first, wu_ref[...], bu_ref[...])
        gate, _, _ = _conv3(hg_ref[...], hag_ref[...], first, wg_ref[...], bg_ref[...])
        o_ref[...] = (gate * _gelu_cdf(gate) * up).astype(BF16)

    main = pl.BlockSpec((tm, tf), lambda i, j: (i, j))
    halo = pl.BlockSpec((8, tf), lambda i, j: (jnp.maximum(i * hb - 1, 0), j))
    w3 = pl.BlockSpec((3, tf), lambda i, j: (0, j))
    b1 = pl.BlockSpec((1, tf), lambda i, j: (0, j))
    return pl.pallas_call(
        body, name=name, grid=(T // tm, F // tf),
        in_specs=[main, main, halo, halo, w3, w3, b1, b1],
        out_specs=main, out_shape=jax.ShapeDtypeStruct((T, F), BF16),
        compiler_params=_params("parallel", "parallel"),
    )(hu, hg, hu, hg, wu, wg, bu, bg)


def glu_bwd(hu, hg, dact, wu, wg, bu, bg, *, name):
    T, F = hu.shape
    tm, tf = 256, _pick(F, (256, 128))
    hb = tm // 8
    blocks_per_example = SEQ // tm

    def body(hu_ref, hg_ref, hau_ref, hag_ref, da_ref, wu_ref, wg_ref, bu_ref, bg_ref,
             du_ref, dg_ref, dwu_ref, dwg_ref, dbu_ref, dbg_ref):
        i = pl.program_id(1)
        first = i % blocks_per_example == 0
        xu, xg = hu_ref[...], hg_ref[...]
        up, u1, u2 = _conv3(xu, hau_ref[...], first, wu_ref[...], bu_ref[...])
        gate, g1, g2 = _conv3(xg, hag_ref[...], first, wg_ref[...], bg_ref[...])
        act, dact_dgate = _gelu_and_grad(gate)
        da = da_ref[...]
        dup = da * act
        dgate = da * up * dact_dgate
        du_ref[...] = dup
        dg_ref[...] = dgate

        def sums(d, x0, x1, x2):
            s = lambda v: jnp.sum(v, axis=0, keepdims=True)
            return jnp.concatenate([s(d * x2), s(d * x1), s(d * x0)], axis=0), s(d)

        pwu, pbu = sums(dup, xu, u1, u2)
        pwg, pbg = sums(dgate, xg, g1, g2)

        @pl.when(i == 0)
        def _():
            dwu_ref[...] = pwu
            dwg_ref[...] = pwg
            dbu_ref[...] = pbu
            dbg_ref[...] = pbg

        @pl.when(i > 0)
        def _():
            dwu_ref[...] += pwu
            dwg_ref[...] += pwg
            dbu_ref[...] += pbu
            dbg_ref[...] += pbg

    main = pl.BlockSpec((tm, tf), lambda j, i: (i, j))
    halo = pl.BlockSpec((8, tf), lambda j, i: (jnp.maximum(i * hb - 1, 0), j))
    w3 = pl.BlockSpec((3, tf), lambda j, i: (0, j))
    b1 = pl.BlockSpec((1, tf), lambda j, i: (0, j))
    sd = jax.ShapeDtypeStruct
    return pl.pallas_call(
        body, name=name, grid=(F // tf, T // tm),
        in_specs=[main, main, halo, halo, main, w3, w3, b1, b1],
        out_specs=[main, main, w3, w3, b1, b1],
        out_shape=[sd((T, F), F32), sd((T, F), F32), sd((3, F), F32), sd((3, F), F32), sd((1, F), F32), sd((1, F), F32)],
        compiler_params=_params("parallel", "arbitrary"),
    )(hu, hg, hu, hg, dact, wu, wg, bu, bg)


def conv3_transpose(d, w, *, name):
    T, F = d.shape
    tm, tf = 256, _pick(F, (256, 128))
    hb = tm // 8
    blocks_per_example = SEQ // tm
    n_halo_blocks = T // 8

    def body(d_ref, ha_ref, w_ref, o_ref):
        last = pl.program_id(0) % blocks_per_example == blocks_per_example - 1
        main = d_ref[...]
        halo = jnp.where(last, 0.0, ha_ref[...])
        ext = jnp.concatenate([main, halo], axis=0)
        n = tm + 8
        p1 = pltpu.roll(ext, n - 1, 0)[:tm]
        p2 = pltpu.roll(ext, n - 2, 0)[:tm]
        w = w_ref[...]
        o_ref[...] = (w[2:3] * main + w[1:2] * p1 + w[0:1] * p2).astype(BF16)

    main = pl.BlockSpec((tm, tf), lambda i, j: (i, j))
    halo = pl.BlockSpec((8, tf), lambda i, j: (jnp.minimum((i + 1) * hb, n_halo_blocks - 1), j))
    return pl.pallas_call(
        body, name=name, grid=(T // tm, F // tf),
        in_specs=[main, halo, pl.BlockSpec((3, tf), lambda i, j: (0, j))],
        out_specs=main, out_shape=jax.ShapeDtypeStruct((T, F), BF16),
        compiler_params=_params("parallel", "parallel"),
    )(d, d, w)


def loss_head(x, g, target, *, name):
    T, D = x.shape
    tr = 256

    def body(x_ref, g_ref, t_ref, loss_ref, dx_ref, dg_ref):
        xv = x_ref[...]
        gv = g_ref[...]
        r = lax.rsqrt(jnp.mean(xv * xv, axis=-1, keepdims=True) + EPS)
        n = xv * r
        err = n * gv - t_ref[...]
        part_loss = jnp.zeros((1, 128), F32) + 0.5 * jnp.sum(jnp.mean(err * err, axis=-1, keepdims=True))
        dy = err * (1.0 / D)
        dn = dy * gv
        dx_ref[...] = r * (dn - n * jnp.mean(dn * n, axis=-1, keepdims=True))
        part_g = jnp.sum(dy * n, axis=0, keepdims=True)

        @pl.when(pl.program_id(0) == 0)
        def _():
            loss_ref[...] = part_loss
            dg_ref[...] = part_g

        @pl.when(pl.program_id(0) > 0)
        def _():
            loss_ref[...] += part_loss
            dg_ref[...] += part_g

    row = pl.BlockSpec((tr, D), lambda i: (i, 0))
    vec = pl.BlockSpec((1, D), lambda i: (0, 0))
    sd = jax.ShapeDtypeStruct
    return pl.pallas_call(
        body, name=name, grid=(T // tr,),
        in_specs=[row, vec, row],
        out_specs=[pl.BlockSpec((1, 128), lambda i: (0, 0)), row, vec],
        out_shape=[sd((1, 128), F32), sd((T, D), F32), sd((1, D), F32)],
        compiler_params=_params("arbitrary"),
    )(x, g, target)


def adamw(w, g, m, v, *, name):
    R, C = w.shape
    tr = _pick(R, (256, 128, 64, 32, 16, 8))

    def body(w_ref, g_ref, m_ref, v_ref, d_ref, nm_ref, nv_ref):
        gv = g_ref[...]
        mn = ADAM_B1 * m_ref[...] + (1.0 - ADAM_B1) * gv
        vn = ADAM_B2 * v_ref[...] + (1.0 - ADAM_B2) * (gv * gv)
        m_hat = mn / (1.0 - ADAM_B1 ** ADAM_STEP)
        v_hat = vn / (1.0 - ADAM_B2 ** ADAM_STEP)
        d_ref[...] = -ADAM_LR * (m_hat / (jnp.sqrt(v_hat) + ADAM_EPS) + ADAM_WD * w_ref[...])
        nm_ref[...] = mn
        nv_ref[...] = vn

    blk = pl.BlockSpec((tr, C), lambda i: (i, 0))
    sd = jax.ShapeDtypeStruct((R, C), F32)
    return pl.pallas_call(
        body, name=name, grid=(R // tr,), in_specs=[blk] * 4, out_specs=[blk] * 3, out_shape=[sd] * 3,
        compiler_params=_params("parallel"),
    )(w, g, m, v)


def _softplus(x):
    return jnp.maximum(x, 0.0) + jnp.log(1.0 + jnp.exp(-jnp.abs(x)))


def _lru_gates(x, cw, cb, wa, ba, wx, bx, lam):
    S = x.shape[0]
    row = lax.broadcasted_iota(jnp.int32, (S, 1), 0)

    def back(s):
        return jnp.where(row >= s, pltpu.roll(x, s, 0), 0.0)

    xc = (((cb + cw[0:1] * back(3)) + cw[1:2] * back(2)) + cw[2:3] * back(1)) + cw[3:4] * x
    xb = xc.astype(BF16)
    r = jax.nn.sigmoid(jnp.dot(xb, wa, preferred_element_type=F32) + ba)
    ig = jax.nn.sigmoid(jnp.dot(xb, wx, preferred_element_type=F32) + bx)
    sp = _softplus(-lam)
    la = -LRU_C * r * sp
    a = jnp.exp(la)
    y = 2.0 * la
    one_minus_a2 = jnp.where(y > -0.05, -y * (1.0 + y * (0.5 + y * (1.0 / 6.0 + y * (1.0 / 24.0)))), 1.0 - jnp.exp(y))
    mm = jnp.sqrt(one_minus_a2)
    return xc, xb, r, ig, sp, a, mm


def lru_fwd(aux, cw, cb, wa, ba, wx, bx, lam, *, name):
    T = aux.shape[0]
    S, C = SEQ, LRU_HALF_W

    def body(x_ref, g_ref, cw_ref, cb_ref, wa_ref, ba_ref, wx_ref, bx_ref, lam_ref, o_ref, h_ref, a_s, u_s):
        xc, _, r, ig, sp, a, mm = _lru_gates(x_ref[...], cw_ref[...], cb_ref[...], wa_ref[...], ba_ref[...],
                                             wx_ref[...], bx_ref[...], lam_ref[...])
        a_s[...] = a
        u_s[...] = mm * (ig * xc)

        def group(i, h):
            base = pl.multiple_of(i * 8, 8)
            a8 = a_s[pl.ds(base, 8), :]
            u8 = u_s[pl.ds(base, 8), :]
            for rr in range(8):
                h = a8[rr:rr + 1] * h + u8[rr:rr + 1]
                h_ref[pl.ds(base + rr, 1), :] = h
            return h

        lax.fori_loop(0, S // 8, group, jnp.zeros((1, C), F32))
        gate = g_ref[...]
        o_ref[...] = (h_ref[...] * (gate * _gelu_cdf(gate))).astype(BF16)

    blk = lambda col: pl.BlockSpec((S, C), lambda c, b: (b, col + c))
    par = lambda rows: pl.BlockSpec((rows, C), lambda c, b: (0, c))
    sq = pl.BlockSpec((None, C, C), lambda c, b: (c, 0, 0))
    sd = jax.ShapeDtypeStruct
    W = LRU_HALVES * C
    return pl.pallas_call(
        body, name=name, grid=(LRU_HALVES, T // S),
        in_specs=[blk(0), blk(LRU_HALVES), par(4), par(1), sq, par(1), sq, par(1), par(1)],
        out_specs=[blk(0), blk(0)], out_shape=[sd((T, W), BF16), sd((T, W), F32)],
        scratch_shapes=[pltpu.VMEM((S, C), F32), pltpu.VMEM((S, C), F32)],
        compiler_params=_params("parallel", "parallel"),
    )(aux, aux, cw, cb, wa, ba, wx, bx, lam)


def lru_bwd(aux, h, dmixed, cw, cb, wa, ba, wx, bx, lam, *, name):
    T = aux.shape[0]
    S, C = SEQ, LRU_HALF_W

    def body(x_ref, g_ref, h_ref, do_ref, cw_ref, cb_ref, wa_ref, ba_ref, wx_ref, bx_ref, lam_ref,
             dx_ref, dgate_ref, dcw_ref, dcb_ref, dwa_ref, dba_ref, dwx_ref, dbx_ref, dlam_ref, a_s, d_s):
        x = x_ref[...]
        cw = cw_ref[...]
        lam = lam_ref[...]
        xc, xb, r, ig, sp, a, mm = _lru_gates(x, cw, cb_ref[...], wa_ref[...], ba_ref[...], wx_ref[...], bx_ref[...], lam)
        gate = g_ref[...]
        gl, dgl = _gelu_and_grad(gate)
        dout = do_ref[...]
        hv = h_ref[...]
        dgate_ref[...] = dout * hv * dgl
        a_s[...] = a
        d_s[...] = dout * gl

        def group(i, c):
            base = pl.multiple_of((S // 8 - 1 - i) * 8, 8)
            a8 = a_s[pl.ds(base, 8), :]
            d8 = d_s[pl.ds(base, 8), :]
            for rr in range(7, -1, -1):
                d = d8[rr:rr + 1] + c
                d_s[pl.ds(base + rr, 1), :] = d
                c = a8[rr:rr + 1] * d
            return c

        lax.fori_loop(0, S // 8, group, jnp.zeros((1, C), F32))
        row = lax.broadcasted_iota(jnp.int32, (S, 1), 0)
        dht = d_s[...]
        h_prev = jnp.where(row >= 1, pltpu.roll(hv, 1, 0), 0.0)
        da = dht * h_prev
        gx = ig * xc
        dmm = dht * gx
        dig = dht * mm * xc
        dxc = dht * mm * ig
        dla = da * a - dmm * (a * a) / mm
        dr = dla * (-LRU_C * sp)
        dsp = jnp.sum(dla * (-LRU_C * r), axis=0, keepdims=True)
        dlam = dsp * (-jax.nn.sigmoid(-lam))
        dpa = dr * r * (1.0 - r)
        dpx = dig * ig * (1.0 - ig)
        dpa_b, dpx_b = dpa.astype(BF16), dpx.astype(BF16)
        nt = (((1,), (1,)), ((), ()))
        tn = (((0,), (0,)), ((), ()))
        dxc = dxc + lax.dot_general(dpa_b, wa_ref[...], nt, preferred_element_type=F32) \
                  + lax.dot_general(dpx_b, wx_ref[...], nt, preferred_element_type=F32)
        dwa = lax.dot_general(xb, dpa_b, tn, preferred_element_type=F32)
        dwx = lax.dot_general(xb, dpx_b, tn, preferred_element_type=F32)

        def fwd(v, s):
            return jnp.where(row < S - s, pltpu.roll(v, S - s, 0), 0.0)

        def back(v, s):
            return jnp.where(row >= s, pltpu.roll(v, s, 0), 0.0)

        dx_ref[...] = cw[3:4] * dxc + cw[2:3] * fwd(dxc, 1) + cw[1:2] * fwd(dxc, 2) + cw[0:1] * fwd(dxc, 3)
        s0 = lambda v: jnp.sum(v, axis=0, keepdims=True)
        dcw = jnp.concatenate([s0(dxc * back(x, 3)), s0(dxc * back(x, 2)), s0(dxc * back(x, 1)), s0(dxc * x)], axis=0)
        parts = ((dcw_ref, dcw), (dcb_ref, s0(dxc)), (dwa_ref, dwa), (dba_ref, s0(dpa)), (dwx_ref, dwx),
                 (dbx_ref, s0(dpx)), (dlam_ref, dlam))

        @pl.when(pl.program_id(1) == 0)
        def _():
            for ref, val in parts:
                ref[...] = val

        @pl.when(pl.program_id(1) > 0)
        def _():
            for ref, val in parts:
                ref[...] += val

    blk = lambda col: pl.BlockSpec((S, C), lambda c, b: (b, col + c))
    par = lambda rows: pl.BlockSpec((rows, C), lambda c, b: (0, c))
    sq = pl.BlockSpec((None, C, C), lambda c, b: (c, 0, 0))
    sd = jax.ShapeDtypeStruct
    W = LRU_HALVES * C
    vec = sd((1, W), F32)
    return pl.pallas_call(
        body, name=name, grid=(LRU_HALVES, T // S),
        in_specs=[blk(0), blk(LRU_HALVES), blk(0), blk(3 * LRU_HALVES), par(4), par(1), sq, par(1), sq, par(1), par(1)],
        out_specs=[blk(0), blk(0), par(4), par(1), sq, par(1), sq, par(1), par(1)],
        out_shape=[sd((T, W), F32), sd((T, W), F32), sd((4, W), F32), vec, sd((LRU_HALVES, C, C), F32), vec,
                   sd((LRU_HALVES, C, C), F32), vec, vec],
        scratch_shapes=[pltpu.VMEM((S, C), F32), pltpu.VMEM((S, C), F32)],
        compiler_params=_params("parallel", "arbitrary"),
    )(aux, aux, h, dmixed, cw, cb, wa, ba, wx, bx, lam)


_NT = (((1,), (1,)), ((), ()))
_TN = (((0,), (0,)), ((), ()))


def _dot(a, b, dims=None):
    if dims is None:
        return jnp.dot(a, b, preferred_element_type=F32)
    return lax.dot_general(a, b, dims, preferred_element_type=F32)


def _hs(h):
    return slice(h * HEAD_DIM, (h + 1) * HEAD_DIM)


def cross_fwd(q, kv, *, name):
    T = q.shape[0]
    tq = 512

    def body(q_ref, kv_ref, o_ref):
        for h in range(N_HEADS):
            qh = q_ref[:, _hs(h)] * Q_SCALE
            k = kv_ref[:, _hs(h)]
            v = kv_ref[:, GROUP_W + h * HEAD_DIM:GROUP_W + (h + 1) * HEAD_DIM]
            s = _dot(qh, k, _NT)
            p = jnp.exp(s - jnp.max(s, axis=-1, keepdims=True))
            p = p / jnp.sum(p, axis=-1, keepdims=True)
            o_ref[:, _hs(h)] = _dot(p.astype(BF16), v).astype(BF16)

    per = SEQ // tq
    return pl.pallas_call(
        body, name=name, grid=(T // tq,),
        in_specs=[pl.BlockSpec((tq, GROUP_W), lambda i: (i, 0)), pl.BlockSpec((N_MEM, 2 * GROUP_W), lambda i: (i // per, 0))],
        out_specs=pl.BlockSpec((tq, GROUP_W), lambda i: (i, 0)), out_shape=jax.ShapeDtypeStruct((T, GROUP_W), BF16),
        compiler_params=_params("parallel"),
    )(q, kv)


def cross_bwd(q, kv, do, *, name):
    T = q.shape[0]
    tq = 512
    per = SEQ // tq

    def body(q_ref, kv_ref, do_ref, dq_ref, dkv_ref):
        first = pl.program_id(0) % per == 0
        for h in range(N_HEADS):
            vs = slice(GROUP_W + h * HEAD_DIM, GROUP_W + (h + 1) * HEAD_DIM)
            qh = q_ref[:, _hs(h)] * Q_SCALE
            k = kv_ref[:, _hs(h)]
            v = kv_ref[:, vs]
            doh = do_ref[:, _hs(h)].astype(BF16)
            s = _dot(qh, k, _NT)
            p = jnp.exp(s - jnp.max(s, axis=-1, keepdims=True))
            p = p / jnp.sum(p, axis=-1, keepdims=True)
            dp = _dot(doh, v, _NT)
            ds = (p * (dp - jnp.sum(p * dp, axis=-1, keepdims=True))).astype(BF16)
            dq_ref[:, _hs(h)] = (_dot(ds, k) * Q_SCALE).astype(BF16)
            dk = _dot(ds, qh, _TN)
            dv = _dot(p.astype(BF16), doh, _TN)

            @pl.when(first)
            def _():
                dkv_ref[:, _hs(h)] = dk
                dkv_ref[:, vs] = dv

            @pl.when(jnp.logical_not(first))
            def _():
                dkv_ref[:, _hs(h)] += dk
                dkv_ref[:, vs] += dv

    qb = pl.BlockSpec((tq, GROUP_W), lambda i: (i, 0))
    kvb = pl.BlockSpec((N_MEM, 2 * GROUP_W), lambda i: (i // per, 0))
    sd = jax.ShapeDtypeStruct
    return pl.pallas_call(
        body, name=name, grid=(T // tq,),
        in_specs=[qb, kvb, qb], out_specs=[qb, kvb],
        out_shape=[sd((T, GROUP_W), BF16), sd(kv.shape, F32)],
        compiler_params=_params("arbitrary"),
    )(q, kv, do)


NB = SEQ // BLOCK
NEG = -1e30


def _split_dot(x, tri):
    hi = x.astype(BF16)
    lo = (x - hi.astype(F32)).astype(BF16)
    return _dot(hi, tri) + _dot(lo, tri)


def _blk(i):
    return pl.ds(pl.multiple_of(i * BLOCK, BLOCK), BLOCK)


def _iotas():
    row = lax.broadcasted_iota(jnp.int32, (BLOCK, BLOCK), 0)
    col = lax.broadcasted_iota(jnp.int32, (BLOCK, BLOCK), 1)
    return row, col


def _sb_scores(q, k, mask, later, csum, want_sigmoid=False):
    z = _dot(q, k, _NT)
    lk = -_softplus(z)
    if mask is not None:
        lk = jnp.where(mask, lk, 0.0)
    lka = _split_dot(lk, later) + csum
    att = jnp.exp(z + lk + lka)
    sg = jnp.exp(z + lk) if want_sigmoid else None
    if mask is not None:
        att = jnp.where(mask, att, 0.0)
        sg = jnp.where(mask, sg, 0.0) if want_sigmoid else None
    return att, sg, lk


def _rowsum(v):
    return jnp.sum(v, axis=1, keepdims=True)


HEADS = tuple(range(N_HEADS))


def _qkv_specs(first_col):
    return [pl.BlockSpec((SEQ, GROUP_W), lambda b, c=first_col + j: (b, c)) for j in range(3)]


def sb_fwd(qkv, *, name):
    T = qkv.shape[0]

    def body(q_ref, k_ref, v_ref, o_ref):
        row, col = _iotas()
        strict = col < row
        later = (row > col).astype(BF16)

        def qblock(i, _):
            qs = [q_ref[_blk(i), _hs(h)] * Q_SCALE for h in HEADS]

            def kblock(j, mask, carry):
                out = []
                for h in HEADS:
                    acc, csum = carry[h]
                    att, _, lk = _sb_scores(qs[h], k_ref[_blk(j), _hs(h)], mask, later, csum)
                    out.append((acc + _dot(att.astype(BF16), v_ref[_blk(j), _hs(h)]), csum + _rowsum(lk)))
                return tuple(out)

            zero = (jnp.zeros((BLOCK, HEAD_DIM), F32), jnp.zeros((BLOCK, 1), F32))
            carry = kblock(i, strict, (zero,) * N_HEADS)
            carry = lax.fori_loop(0, i, lambda jj, c: kblock(i - 1 - jj, None, c), carry)
            for h in HEADS:
                o_ref[_blk(i), _hs(h)] = carry[h][0].astype(BF16)
            return 0

        lax.fori_loop(0, NB, qblock, 0)

    return pl.pallas_call(
        body, name=name, grid=(T // SEQ,), in_specs=_qkv_specs(0),
        out_specs=pl.BlockSpec((SEQ, GROUP_W), lambda b: (b, 0)), out_shape=jax.ShapeDtypeStruct((T, GROUP_W), BF16),
        compiler_params=_params("parallel"),
    )(qkv, qkv, qkv)


def sb_bwd(qkv, dmixed, *, name):
    T = qkv.shape[0]

    def body(q_ref, k_ref, v_ref, do_ref, dq_ref, dk_ref, dv_ref, att_s, sg_s):
        row, col = _iotas()
        strict = col < row
        later = (row > col).astype(BF16)
        earlier = (row < col).astype(BF16)
        dk_ref[...] = jnp.zeros_like(dk_ref)
        dv_ref[...] = jnp.zeros_like(dv_ref)

        def qblock(i, _):
            qs = [q_ref[_blk(i), _hs(h)] * Q_SCALE for h in HEADS]
            dos = [do_ref[_blk(i), _hs(h)].astype(BF16) for h in HEADS]

            def recompute(j, mask, csums):
                out = []
                for h in HEADS:
                    att, sg, lk = _sb_scores(qs[h], k_ref[_blk(j), _hs(h)], mask, later, csums[h], want_sigmoid=True)
                    att_s[h, j] = att
                    sg_s[h, j] = sg
                    out.append(csums[h] + _rowsum(lk))
                return tuple(out)

            csums = recompute(i, strict, (jnp.zeros((BLOCK, 1), F32),) * N_HEADS)
            lax.fori_loop(0, i, lambda jj, c: recompute(i - 1 - jj, None, c), csums)

            def kblock(j, carry):
                out = []
                for h in HEADS:
                    dq, pre = carry[h]
                    att = att_s[h, j]
                    ds = _dot(dos[h], v_ref[_blk(j), _hs(h)], _NT) * att
                    dlk = ds + _split_dot(ds, earlier) + pre
                    dz = (ds - dlk * sg_s[h, j]).astype(BF16)
                    dk_ref[_blk(j), _hs(h)] += _dot(dz, qs[h], _TN)
                    dv_ref[_blk(j), _hs(h)] += _dot(att.astype(BF16), dos[h], _TN)
                    out.append((dq + _dot(dz, k_ref[_blk(j), _hs(h)]), pre + _rowsum(ds)))
                return tuple(out)

            zero = (jnp.zeros((BLOCK, HEAD_DIM), F32), jnp.zeros((BLOCK, 1), F32))
            res = lax.fori_loop(0, i + 1, kblock, (zero,) * N_HEADS)
            for h in HEADS:
                dq_ref[_blk(i), _hs(h)] = res[h][0] * Q_SCALE
            return 0

        lax.fori_loop(0, NB, qblock, 0)

    out = pl.BlockSpec((SEQ, GROUP_W), lambda b: (b, 0))
    sd = jax.ShapeDtypeStruct((T, GROUP_W), F32)
    return pl.pallas_call(
        body, name=name, grid=(T // SEQ,), in_specs=_qkv_specs(0) + [out],
        out_specs=[out] * 3, out_shape=[sd] * 3,
        scratch_shapes=[pltpu.VMEM((N_HEADS, NB, BLOCK, BLOCK), F32), pltpu.VMEM((N_HEADS, NB, BLOCK, BLOCK), F32)],
        compiler_params=_params("parallel"),
    )(qkv, qkv, qkv, dmixed)


LANES = 128
CUM_BLK = 256


def col_to_row(c):
    b = c.shape[0] // SEQ
    return c.reshape(b, SEQ, LANES)[:, :, :8].transpose(0, 2, 1).reshape(b * 8, SEQ)


def row_to_col(r):
    b = r.shape[0] // 8
    c = r.reshape(b, 8, SEQ).transpose(0, 2, 1)
    return jnp.pad(c, ((0, 0), (0, 0), (0, LANES - 8))).reshape(b * SEQ, LANES)


def fox_prep(aux, bf, *, name):
    T = aux.shape[0]

    def body(f_ref, b_ref, o_ref):
        row = lax.broadcasted_iota(jnp.int32, (CUM_BLK, CUM_BLK), 0)
        col = lax.broadcasted_iota(jnp.int32, (CUM_BLK, CUM_BLK), 1)
        upto = (col <= row).astype(BF16)
        carry = jnp.zeros((1, LANES), F32)
        for n in range(SEQ // CUM_BLK):
            rows = slice(n * CUM_BLK, (n + 1) * CUM_BLK)
            logf = -_softplus(-(f_ref[rows, :] + b_ref[...]))
            hi = logf.astype(BF16)
            lo = (logf - hi.astype(F32)).astype(BF16)
            cum = _dot(upto, hi) + _dot(upto, lo) + carry
            o_ref[rows, :] = cum
            carry = cum[CUM_BLK - 1:CUM_BLK]

    return pl.pallas_call(
        body, name=name, grid=(T // SEQ,),
        in_specs=[pl.BlockSpec((SEQ, LANES), lambda b: (b, 4)), pl.BlockSpec((1, LANES), lambda b: (0, 0))],
        out_specs=pl.BlockSpec((SEQ, LANES), lambda b: (b, 0)), out_shape=jax.ShapeDtypeStruct((T, LANES), F32),
        compiler_params=_params("parallel"),
    )(aux, bf)


def fox_prep_bwd(aux, bf, dcum, *, name):
    T = aux.shape[0]

    def body(f_ref, b_ref, d_ref, df_ref, db_ref):
        row = lax.broadcasted_iota(jnp.int32, (CUM_BLK, CUM_BLK), 0)
        col = lax.broadcasted_iota(jnp.int32, (CUM_BLK, CUM_BLK), 1)
        onward = (col >= row).astype(BF16)
        carry = jnp.zeros((1, LANES), F32)
        tot = jnp.zeros((1, LANES), F32)
        for n in range(SEQ // CUM_BLK - 1, -1, -1):
            rows = slice(n * CUM_BLK, (n + 1) * CUM_BLK)
            d = d_ref[rows, :]
            hi = d.astype(BF16)
            lo = (d - hi.astype(F32)).astype(BF16)
            dlogf = _dot(onward, hi) + _dot(onward, lo) + carry
            carry = dlogf[0:1]
            df = dlogf * jax.nn.sigmoid(-(f_ref[rows, :] + b_ref[...]))
            df_ref[rows, :] = df
            tot = tot + jnp.sum(df, axis=0, keepdims=True)

        @pl.when(pl.program_id(0) == 0)
        def _():
            db_ref[...] = tot

        @pl.when(pl.program_id(0) > 0)
        def _():
            db_ref[...] += tot

    blk = pl.BlockSpec((SEQ, LANES), lambda b: (b, 0))
    vec = pl.BlockSpec((1, LANES), lambda b: (0, 0))
    sd = jax.ShapeDtypeStruct
    return pl.pallas_call(
        body, name=name, grid=(T // SEQ,),
        in_specs=[pl.BlockSpec((SEQ, LANES), lambda b: (b, 4)), vec, blk],
        out_specs=[blk, vec], out_shape=[sd((T, LANES), F32), sd((1, LANES), F32)],
        compiler_params=_params("arbitrary"),
    )(aux, bf, dcum)


def _fox_logits(q, k, cq, ck, mask):
    z = _dot(q, k, _NT) + cq - ck
    return z if mask is None else jnp.where(mask, z, NEG)


def fox_fwd(qkv, cumc, cumr, *, name):
    T = qkv.shape[0]

    def body(q_ref, k_ref, v_ref, cc_ref, cr_ref, o_ref, lse_ref, z_s):
        row, col = _iotas()
        causal = col <= row
        lse_ref[...] = jnp.zeros_like(lse_ref)

        def qblock(i, _):
            qs = [q_ref[_blk(i), _hs(h)] * Q_SCALE for h in HEADS]
            cqs = [cc_ref[_blk(i), h:h + 1] for h in HEADS]

            def logits(j, mask, ms):
                out = []
                for h in HEADS:
                    z = _fox_logits(qs[h], k_ref[_blk(j), _hs(h)], cqs[h], cr_ref[h:h + 1, _blk(j)], mask)
                    z_s[h, j] = z
                    out.append(jnp.maximum(ms[h], jnp.max(z, axis=1, keepdims=True)))
                return tuple(out)

            ms = logits(i, causal, (jnp.full((BLOCK, 1), NEG, F32),) * N_HEADS)
            ms = lax.fori_loop(0, i, lambda j, c: logits(j, None, c), ms)

            def values(j, carry):
                out = []
                for h in HEADS:
                    acc, l = carry[h]
                    p = jnp.exp(z_s[h, j] - ms[h])
                    out.append((acc + _dot(p.astype(BF16), v_ref[_blk(j), _hs(h)]), l + _rowsum(p)))
                return tuple(out)

            zero = (jnp.zeros((BLOCK, HEAD_DIM), F32), jnp.zeros((BLOCK, 1), F32))
            res = lax.fori_loop(0, i + 1, values, (zero,) * N_HEADS)
            for h in HEADS:
                acc, l = res[h]
                o_ref[_blk(i), _hs(h)] = (acc / l).astype(BF16)
                lse_ref[_blk(i), h:h + 1] = ms[h] + jnp.log(l)
            return 0

        lax.fori_loop(0, NB, qblock, 0)

    out = pl.BlockSpec((SEQ, GROUP_W), lambda b: (b, 0))
    colb = pl.BlockSpec((SEQ, LANES), lambda b: (b, 0))
    sd = jax.ShapeDtypeStruct
    return pl.pallas_call(
        body, name=name, grid=(T // SEQ,),
        in_specs=_qkv_specs(3) + [colb, pl.BlockSpec((8, SEQ), lambda b: (b, 0))],
        out_specs=[out, colb], out_shape=[sd((T, GROUP_W), BF16), sd((T, LANES), F32)],
        scratch_shapes=[pltpu.VMEM((N_HEADS, NB, BLOCK, BLOCK), F32)],
        compiler_params=_params("parallel"),
    )(qkv, qkv, qkv, cumc, cumr)


def fox_bwd(qkv, cumc, cumr, lse, dmixed, *, name):
    T = qkv.shape[0]

    def body(q_ref, k_ref, v_ref, cc_ref, cr_ref, lse_ref, do_ref, dq_ref, dk_ref, dv_ref, dcc_ref, dcr_ref, p_s, dp_s):
        row, col = _iotas()
        causal = col <= row
        dk_ref[...] = jnp.zeros_like(dk_ref)
        dv_ref[...] = jnp.zeros_like(dv_ref)
        dcc_ref[...] = jnp.zeros_like(dcc_ref)
        dcr_ref[...] = jnp.zeros_like(dcr_ref)

        def qblock(i, _):
            qs = [q_ref[_blk(i), _hs(h)] * Q_SCALE for h in HEADS]
            dos = [do_ref[_blk(i), _hs(h)].astype(BF16) for h in HEADS]
            cqs = [cc_ref[_blk(i), h:h + 1] for h in HEADS]
            lses = [lse_ref[_blk(i), h:h + 1] for h in HEADS]

            def probs(j, mask, deltas):
                out = []
                for h in HEADS:
                    z = _fox_logits(qs[h], k_ref[_blk(j), _hs(h)], cqs[h], cr_ref[h:h + 1, _blk(j)], mask)
                    p = jnp.exp(z - lses[h])
                    dp = _dot(dos[h], v_ref[_blk(j), _hs(h)], _NT)
                    p_s[h, j] = p
                    dp_s[h, j] = dp
                    out.append(deltas[h] + _rowsum(p * dp))
                return tuple(out)

            deltas = probs(i, causal, (jnp.zeros((BLOCK, 1), F32),) * N_HEADS)
            deltas = lax.fori_loop(0, i, lambda j, c: probs(j, None, c), deltas)

            def kblock(j, carry):
                out = []
                for h in HEADS:
                    dq, dcq = carry[h]
                    p = p_s[h, j]
                    ds = p * (dp_s[h, j] - deltas[h])
                    dsb = ds.astype(BF16)
                    dk_ref[_blk(j), _hs(h)] += _dot(dsb, qs[h], _TN)
                    dv_ref[_blk(j), _hs(h)] += _dot(p.astype(BF16), dos[h], _TN)
                    dcr_ref[h:h + 1, _blk(j)] -= jnp.sum(ds, axis=0, keepdims=True)
                    out.append((dq + _dot(dsb, k_ref[_blk(j), _hs(h)]), dcq + _rowsum(ds)))
                return tuple(out)

            zero = (jnp.zeros((BLOCK, HEAD_DIM), F32), jnp.zeros((BLOCK, 1), F32))
            res = lax.fori_loop(0, i + 1, kblock, (zero,) * N_HEADS)
            for h in HEADS:
                dq_ref[_blk(i), _hs(h)] = res[h][0] * Q_SCALE
                dcc_ref[_blk(i), h:h + 1] = res[h][1]
            return 0

        lax.fori_loop(0, NB, qblock, 0)

    out = pl.BlockSpec((SEQ, GROUP_W), lambda b: (b, 0))
    colb = pl.BlockSpec((SEQ, LANES), lambda b: (b, 0))
    rowb = pl.BlockSpec((8, SEQ), lambda b: (b, 0))
    sd = jax.ShapeDtypeStruct
    big = sd((T, GROUP_W), F32)
    return pl.pallas_call(
        body, name=name, grid=(T // SEQ,),
        in_specs=_qkv_specs(3) + [colb, rowb, colb, pl.BlockSpec((SEQ, GROUP_W), lambda b: (b, 1))],
        out_specs=[out, out, out, colb, rowb],
        out_shape=[big, big, big, sd((T, LANES), F32), sd((T // SEQ * 8, SEQ), F32)],
        scratch_shapes=[pltpu.VMEM((N_HEADS, NB, BLOCK, BLOCK), F32), pltpu.VMEM((N_HEADS, NB, BLOCK, BLOCK), F32)],
        compiler_params=_params("parallel"),
    )(qkv, qkv, qkv, cumc, cumr, lse, dmixed)


BAND = 2 * BLOCK


def _t5_bucket_np(dist):
    n = np.maximum(dist, 0)
    max_exact = NUM_BUCKETS // 2
    nf = np.maximum(n, 1).astype(np.float32)
    large = max_exact + (np.log(nf / np.float32(max_exact)) / np.float32(math.log(MAX_DISTANCE / max_exact))
                         * np.float32(NUM_BUCKETS - max_exact)).astype(np.int32)
    large = np.minimum(large, NUM_BUCKETS - 1)
    return np.where(n < max_exact, n, large).astype(np.int32)


def _band_buckets():
    qi = np.arange(BLOCK)[:, None]
    ki = np.arange(BAND)[None, :]
    delta = np.clip(qi - ki + BLOCK, 0, BLOCK)
    return np.stack([_t5_bucket_np(delta * d) for d in DILATIONS])


def to_classes(a, d):
    if d == 1:
        return a
    T, C = a.shape
    return a.reshape(T // SEQ, SEQ // d, d, C).transpose(0, 2, 1, 3).reshape(T, C)


def from_classes(a, d):
    if d == 1:
        return a
    T, C = a.shape
    return a.reshape(T // SEQ, d, SEQ // d, C).transpose(0, 2, 1, 3).reshape(T, C)


def relbias_expand(rel, *, name):
    buckets = jnp.asarray(_band_buckets())
    n_pat = len(DILATIONS)

    def body(rel_ref, bk_ref, o_ref):
        for p in range(n_pat):
            bk = bk_ref[p]
            for h in range(N_HEADS):
                acc = jnp.zeros((BLOCK, BAND), F32)
                for b in range(NUM_BUCKETS):
                    acc = jnp.where(bk == b, rel_ref[b, h], acc)
                o_ref[p * N_HEADS + h] = acc

    return pl.pallas_call(
        body, name=name,
        in_specs=[pl.BlockSpec(memory_space=pltpu.SMEM), pl.BlockSpec(memory_space=pltpu.VMEM)],
        out_specs=pl.BlockSpec(memory_space=pltpu.VMEM),
        out_shape=jax.ShapeDtypeStruct((n_pat * N_HEADS, BLOCK, BAND), F32),
        compiler_params=_params(),
    )(rel, buckets)


def relbias_reduce(ds_all, *, name):
    buckets = jnp.asarray(_band_buckets())
    n_pat = len(DILATIONS)

    def body(ds_ref, bk_ref, o_ref):
        for b in range(NUM_BUCKETS):
            for h in range(N_HEADS):
                tot = jnp.float32(0.0)
                for p in range(n_pat):
                    tot = tot + jnp.sum(jnp.where(bk_ref[p] == b, ds_ref[p * N_HEADS + h], 0.0))
                o_ref[b, h] = tot

    return pl.pallas_call(
        body, name=name,
        in_specs=[pl.BlockSpec(memory_space=pltpu.VMEM), pl.BlockSpec(memory_space=pltpu.VMEM)],
        out_specs=pl.BlockSpec(memory_space=pltpu.SMEM),
        out_shape=jax.ShapeDtypeStruct((NUM_BUCKETS, N_HEADS), F32),
        compiler_params=_params(),
    )(ds_all, buckets)


def _band_valid(first):
    qi = lax.broadcasted_iota(jnp.int32, (BLOCK, BAND), 0)
    ki = lax.broadcasted_iota(jnp.int32, (BLOCK, BAND), 1)
    inside = jnp.logical_and(ki >= qi, ki <= qi + BLOCK)
    return jnp.logical_and(inside, jnp.logical_or(jnp.logical_not(first), ki >= BLOCK))


def _band_specs(pattern):
    cur = lambda c: pl.BlockSpec((BLOCK, GROUP_W), lambda g: (g, c))
    prev = lambda c: pl.BlockSpec((BLOCK, GROUP_W), lambda g: (jnp.maximum(g - 1, 0), c))
    bias = pl.BlockSpec((N_HEADS, BLOCK, BAND), lambda g: (pattern, 0, 0))
    return [cur(0), prev(1), cur(1), prev(2), cur(2), bias]


def band_fwd(qd, bias, pattern, *, name):
    T = qd.shape[0]
    seq_blocks = SEQ // DILATIONS[pattern] // BLOCK

    def body(q_ref, kp_ref, kc_ref, vp_ref, vc_ref, b_ref, o_ref, lse_ref):
        valid = _band_valid(pl.program_id(0) % seq_blocks == 0)
        lse_ref[...] = jnp.zeros_like(lse_ref)
        for h in range(N_HEADS):
            hs = _hs(h)
            q = q_ref[:, hs] * Q_SCALE
            s = jnp.concatenate([_dot(q, kp_ref[:, hs], _NT), _dot(q, kc_ref[:, hs], _NT)], axis=1)
            sc = jnp.where(valid, s + b_ref[h], NEG)
            m = jnp.max(sc, axis=1, keepdims=True)
            p = jnp.exp(sc - m)
            l = jnp.sum(p, axis=1, keepdims=True)
            pb = p.astype(BF16)
            o_ref[:, hs] = (_dot(pb[:, :BLOCK], vp_ref[:, hs]) + _dot(pb[:, BLOCK:], vc_ref[:, hs])) / l
            lse_ref[:, h:h + 1] = m + jnp.log(l)

    sd = jax.ShapeDtypeStruct
    return pl.pallas_call(
        body, name=name, grid=(T // BLOCK,), in_specs=_band_specs(pattern),
        out_specs=[pl.BlockSpec((BLOCK, GROUP_W), lambda g: (g, 0)), pl.BlockSpec((BLOCK, LANES), lambda g: (g, 0))],
        out_shape=[sd((T, GROUP_W), F32), sd((T, LANES), F32)],
        compiler_params=_params("parallel"),
    )(qd, qd, qd, qd, qd, bias)


def band_bwd(qd, bias, lse, do, dlse, pattern, *, name):
    T = qd.shape[0]
    seq_blocks = SEQ // DILATIONS[pattern] // BLOCK

    def body(q_ref, kp_ref, kc_ref, vp_ref, vc_ref, b_ref, lse_ref, do_ref, dlse_ref,
             dq_ref, dkc_ref, dkp_ref, dvc_ref, dvp_ref, ds_ref):
        g = pl.program_id(0)
        valid = _band_valid(g % seq_blocks == 0)
        for h in range(N_HEADS):
            hs = _hs(h)
            q = q_ref[:, hs] * Q_SCALE
            do_h = do_ref[:, hs].astype(BF16)
            s = jnp.concatenate([_dot(q, kp_ref[:, hs], _NT), _dot(q, kc_ref[:, hs], _NT)], axis=1)
            p = jnp.where(valid, jnp.exp(s + b_ref[h] - lse_ref[:, h:h + 1]), 0.0)
            dp = jnp.concatenate([_dot(do_h, vp_ref[:, hs], _NT), _dot(do_h, vc_ref[:, hs], _NT)], axis=1)
            ds = p * (dp - jnp.sum(p * dp, axis=1, keepdims=True) + dlse_ref[:, h:h + 1])
            dsb, pb = ds.astype(BF16), p.astype(BF16)
            dq_ref[:, hs] = (_dot(dsb[:, :BLOCK], kp_ref[:, hs]) + _dot(dsb[:, BLOCK:], kc_ref[:, hs])) * Q_SCALE
            dkp_ref[:, hs] = _dot(dsb[:, :BLOCK], q, _TN)
            dkc_ref[:, hs] = _dot(dsb[:, BLOCK:], q, _TN)
            dvp_ref[:, hs] = _dot(pb[:, :BLOCK], do_h, _TN)
            dvc_ref[:, hs] = _dot(pb[:, BLOCK:], do_h, _TN)

            @pl.when(g == 0)
            def _():
                ds_ref[h] = ds

            @pl.when(g > 0)
            def _():
                ds_ref[h] += ds

    big = pl.BlockSpec((BLOCK, GROUP_W), lambda g: (g, 0))
    colb = pl.BlockSpec((BLOCK, LANES), lambda g: (g, 0))
    sd = jax.ShapeDtypeStruct
    return pl.pallas_call(
        body, name=name, grid=(T // BLOCK,), in_specs=_band_specs(pattern) + [colb, big, colb],
        out_specs=[big] * 5 + [pl.BlockSpec((N_HEADS, BLOCK, BAND), lambda g: (0, 0, 0))],
        out_shape=[sd((T, GROUP_W), F32)] * 5 + [sd((N_HEADS, BLOCK, BAND), F32)],
        compiler_params=_params("arbitrary"),
    )(qd, qd, qd, qd, qd, bias, lse, do, dlse)


def shift_add(cur, prev, *, name):
    T = cur.shape[0]
    nb = T // BLOCK

    def body(c_ref, p_ref, o_ref):
        keep = (pl.program_id(0) < nb - 1).astype(F32)
        o_ref[...] = c_ref[...] + keep * p_ref[...]

    return pl.pallas_call(
        body, name=name, grid=(nb,),
        in_specs=[pl.BlockSpec((BLOCK, GROUP_W), lambda g: (g, 0)), pl.BlockSpec((BLOCK, GROUP_W), lambda g: (jnp.minimum(g + 1, nb - 1), 0))],
        out_specs=pl.BlockSpec((BLOCK, GROUP_W), lambda g: (g, 0)), out_shape=jax.ShapeDtypeStruct((T, GROUP_W), F32),
        compiler_params=_params("parallel"),
    )(cur, prev)


def _pattern_weights(lse_refs, h):
    ls = [r[:, h:h + 1] for r in lse_refs]
    mx = functools.reduce(jnp.maximum, ls)
    es = [jnp.exp(l - mx) for l in ls]
    tot = functools.reduce(lambda a, b: a + b, es)
    return [e / tot for e in es]


def dil_combine_fwd(outs, *, name):
    T = outs[0][0].shape[0]
    n = len(outs)
    tm = 512

    def body(*refs):
        o_refs, l_refs, out_ref = refs[:n], refs[n:2 * n], refs[2 * n]
        for h in range(N_HEADS):
            w = _pattern_weights(l_refs, h)
            acc = w[0] * o_refs[0][:, _hs(h)]
            for p in range(1, n):
                acc = acc + w[p] * o_refs[p][:, _hs(h)]
            out_ref[:, _hs(h)] = acc.astype(BF16)

    big = pl.BlockSpec((tm, GROUP_W), lambda i: (i, 0))
    colb = pl.BlockSpec((tm, LANES), lambda i: (i, 0))
    return pl.pallas_call(
        body, name=name, grid=(T // tm,), in_specs=[big] * n + [colb] * n,
        out_specs=big, out_shape=jax.ShapeDtypeStruct((T, GROUP_W), BF16),
        compiler_params=_params("parallel"),
    )(*[o for o, _ in outs], *[l for _, l in outs])


def dil_combine_bwd(outs, dmixed, *, name):
    T = outs[0][0].shape[0]
    n = len(outs)
    tm = 512

    def body(*refs):
        o_refs, l_refs, do_ref = refs[:n], refs[n:2 * n], refs[2 * n]
        do_refs, dl_refs = refs[2 * n + 1:3 * n + 1], refs[3 * n + 1:]
        for r in dl_refs:
            r[...] = jnp.zeros_like(r)
        for h in range(N_HEADS):
            w = _pattern_weights(l_refs, h)
            do = do_ref[:, _hs(h)]
            dw = [jnp.sum(do * o_refs[p][:, _hs(h)], axis=1, keepdims=True) for p in range(n)]
            mean = functools.reduce(lambda a, b: a + b, [w[p] * dw[p] for p in range(n)])
            for p in range(n):
                do_refs[p][:, _hs(h)] = w[p] * do
                dl_refs[p][:, h:h + 1] = w[p] * (dw[p] - mean)

    big = pl.BlockSpec((tm, GROUP_W), lambda i: (i, 0))
    colb = pl.BlockSpec((tm, LANES), lambda i: (i, 0))
    sd = jax.ShapeDtypeStruct
    res = pl.pallas_call(
        body, name=name, grid=(T // tm,),
        in_specs=[big] * n + [colb] * n + [pl.BlockSpec((tm, GROUP_W), lambda i: (i, 2))],
        out_specs=[big] * n + [colb] * n, out_shape=[sd((T, GROUP_W), F32)] * n + [sd((T, LANES), F32)] * n,
        compiler_params=_params("parallel"),
    )(*[o for o, _ in outs], *[l for _, l in outs], dmixed)
    return list(zip(res[:n], res[n:]))


def dilated_fwd(qkv, bias, tag):
    qd = qkv[:, 6 * GROUP_W:]
    outs = []
    for p, d in enumerate(DILATIONS):
        o, l = band_fwd(to_classes(qd, d), bias, p, name=f"{tag}_band_fwd{p}")
        outs.append((from_classes(o, d), from_classes(l, d)))
    return outs


def dilated_bwd(qkv, bias, outs, dmixed, tag):
    qd = qkv[:, 6 * GROUP_W:]
    grads = dil_combine_bwd(outs, dmixed, name=f"{tag}_combine_bwd")
    parts, ds_all = [], []
    for p, d in enumerate(DILATIONS):
        (_, lse), (do, dlse) = outs[p], grads[p]
        dq, dkc, dkp, dvc, dvp, ds = band_bwd(to_classes(qd, d), bias, to_classes(lse, d), to_classes(do, d),
                                              to_classes(dlse, d), p, name=f"{tag}_band_bwd{p}")
        dk = shift_add(dkc, dkp, name=f"{tag}_dk{p}")
        dv = shift_add(dvc, dvp, name=f"{tag}_dv{p}")
        parts.append(from_classes(jnp.concatenate([dq, dk, dv], axis=1), d))
        ds_all.append(ds)
    return sum_cast(parts, BF16, name=f"{tag}_dqkv_sum"), jnp.concatenate(ds_all, axis=0)


def sum_cast(arrs, dtype, *, name):
    R, C = arrs[0].shape
    tr = _largest_tile(R, 512, 16)
    n = len(arrs)

    def body(*refs):
        acc = refs[0][...].astype(F32)
        for r in refs[1:n]:
            acc = acc + r[...].astype(F32)
        refs[n][...] = acc.astype(dtype)

    blk = pl.BlockSpec((tr, C), lambda i: (i, 0))
    return pl.pallas_call(
        body, name=name, grid=(R // tr,), in_specs=[blk] * n, out_specs=blk, out_shape=jax.ShapeDtypeStruct((R, C), dtype),
        compiler_params=_params("parallel"),
    )(*arrs)


GRAD_WIRE = BF16


def _block_diag_halves(w):
    z = jnp.zeros((HEAD_DIM, HEAD_DIM), w.dtype)
    half = lambda a, b: jnp.concatenate([jnp.concatenate([a, z], axis=1), jnp.concatenate([z, b], axis=1)], axis=0)
    return jnp.stack([half(w[0], w[1]), half(w[2], w[3])]).astype(BF16)


def _diag_blocks(d):
    h = HEAD_DIM
    return jnp.stack([d[0, :h, :h], d[0, h:, h:], d[1, :h, :h], d[1, h:, h:]])


def layer_fwd(x, mem2d, W, P, bias, tag):
    s = {}
    s["x"] = x
    h1 = rmsnorm_fwd(x, P["norm_mix_g"], name=f"{tag}_norm_mix")
    qkv = matmul(h1, W["qkv"], out_dtype=BF16, name=f"{tag}_qkv")
    aux = matmul(h1, W["aux"], name=f"{tag}_aux")
    o_sb = sb_fwd(qkv, name=f"{tag}_sb_fwd")
    cumc = fox_prep(aux, P["bf"], name=f"{tag}_fox_prep")
    cumr = col_to_row(cumc)
    o_fox, lse_fox = fox_fwd(qkv, cumc, cumr, name=f"{tag}_fox_fwd")
    dil = dilated_fwd(qkv, bias, tag)
    o_dil = dil_combine_fwd(dil, name=f"{tag}_dil_combine")
    o_lru, h_lru = lru_fwd(aux, P["lru_conv_w"], P["lru_conv_b"], P["wa"], P["lru_b_a"], P["wx"], P["lru_b_x"],
                           P["lru_lambda"], name=f"{tag}_lru_fwd")
    mixed = jnp.concatenate([o_sb, o_fox, o_dil, o_lru], axis=1)
    x1 = matmul(mixed, W["out"], residual=x, name=f"{tag}_out")
    hq = rmsnorm_fwd(x1, P["norm_cross_g"], name=f"{tag}_norm_cross")
    qc = matmul(hq, W["cq"], out_dtype=BF16, name=f"{tag}_cq")
    memn = rmsnorm_fwd(mem2d, P["norm_mem_g"], name=f"{tag}_norm_mem")
    kv = matmul(memn, W["ckv"], out_dtype=BF16, name=f"{tag}_ckv")
    oc = cross_fwd(qc, kv, name=f"{tag}_cross_fwd")
    x2 = matmul(oc, W["coT"], trans_b=True, residual=x1, name=f"{tag}_co")
    h2 = rmsnorm_fwd(x2, P["norm_ffn_g"], name=f"{tag}_norm_ffn")
    hu = matmul(h2, W["up_u"], trans_b=True, name=f"{tag}_up_u")
    hg = matmul(h2, W["up_g"], trans_b=True, name=f"{tag}_up_g")
    act = glu_fwd(hu, hg, P["wu"], P["wg"], P["bu"], P["bg"], name=f"{tag}_glu_fwd")
    x3 = matmul(act, W["down"], residual=x2, name=f"{tag}_down")
    s.update(h1=h1, qkv=qkv, aux=aux, cumc=cumc, cumr=cumr, lse_fox=lse_fox, dil=dil, h_lru=h_lru, mixed=mixed,
             x1=x1, hq=hq, qc=qc, memn=memn, kv=kv, oc=oc, x2=x2, h2=h2, hu=hu, hg=hg, act=act)
    return x3, s


def layer_bwd(dx3, mem2d, W, P, bias, s, tag):
    mm = functools.partial(matmul, out_dtype=GRAD_WIRE, trans_a=True)
    gW, gP = {}, {}
    dact = matmul(dx3, W["down"], trans_b=True, name=f"{tag}_d_act")
    gW["down"] = mm(s["act"], dx3, name=f"{tag}_g_down")
    dcu, dcg, dwu, dwg, dbu, dbg = glu_bwd(s["hu"], s["hg"], dact, P["wu"], P["wg"], P["bu"], P["bg"], name=f"{tag}_glu_bwd")
    gP["ffn_conv_w"] = jnp.concatenate([dwu, dwg], axis=1)
    gP["ffn_conv_b"] = jnp.concatenate([dbu, dbg], axis=1)
    dhu = conv3_transpose(dcu, P["wu"], name=f"{tag}_convT_u")
    dhg = conv3_transpose(dcg, P["wg"], name=f"{tag}_convT_g")
    dh2 = matmul(dhu, W["up_u"], name=f"{tag}_d_h2u")
    dh2 = matmul(dhg, W["up_g"], residual=dh2, name=f"{tag}_d_h2g")
    gW["up_u"] = mm(dhu, s["h2"], name=f"{tag}_g_up_u")
    gW["up_g"] = mm(dhg, s["h2"], name=f"{tag}_g_up_g")
    dx2, gP["norm_ffn_g"] = rmsnorm_bwd(s["x2"], P["norm_ffn_g"], dh2, dx3, name=f"{tag}_norm_ffn_bwd")
    doc = matmul(dx2, W["coT"], name=f"{tag}_d_oc")
    gW["coT"] = mm(dx2, s["oc"], name=f"{tag}_g_co")
    dqc, dkv = cross_bwd(s["qc"], s["kv"], doc, name=f"{tag}_cross_bwd")
    dhq = matmul(dqc, W["cq"], trans_b=True, name=f"{tag}_d_hq")
    gW["cq"] = mm(s["hq"], dqc, name=f"{tag}_g_cq")
    dmemn = matmul(dkv, W["ckv"], trans_b=True, name=f"{tag}_d_memn")
    gW["ckv"] = mm(s["memn"], dkv, name=f"{tag}_g_ckv")
    _, gP["norm_mem_g"] = rmsnorm_bwd(mem2d, P["norm_mem_g"], dmemn, None, name=f"{tag}_norm_mem_bwd")
    dx1, gP["norm_cross_g"] = rmsnorm_bwd(s["x1"], P["norm_cross_g"], dhq, dx2, name=f"{tag}_norm_cross_bwd")
    dmixed = matmul(dx1, W["out"], trans_b=True, name=f"{tag}_d_mixed")
    gW["out"] = mm(s["mixed"], dx1, name=f"{tag}_g_out")
    qkv, aux = s["qkv"], s["aux"]
    d_sb = sb_bwd(qkv, dmixed, name=f"{tag}_sb_bwd")
    dfq, dfk, dfv, dcc, dcr = fox_bwd(qkv, s["cumc"], s["cumr"], s["lse_fox"], dmixed, name=f"{tag}_fox_bwd")
    dcum = sum_cast([dcc, row_to_col(dcr)], F32, name=f"{tag}_dcum")
    df, dbf = fox_prep_bwd(aux, P["bf"], dcum, name=f"{tag}_fox_prep_bwd")
    gP["b_forget"] = dbf[0, :N_HEADS]
    d_dil, ds_band = dilated_bwd(qkv, bias, s["dil"], dmixed, tag)
    dlx, dlg, dcw, dcb, dwa, dba, dwx, dbx, dlam = lru_bwd(
        aux, s["h_lru"], dmixed, P["lru_conv_w"], P["lru_conv_b"], P["wa"], P["lru_b_a"], P["wx"], P["lru_b_x"],
        P["lru_lambda"], name=f"{tag}_lru_bwd")
    gP.update(lru_conv_w=dcw, lru_conv_b=dcb, lru_w_a=_diag_blocks(dwa), lru_b_a=dba, lru_w_x=_diag_blocks(dwx),
              lru_b_x=dbx, lru_lambda=dlam)
    dqkv = jnp.concatenate([d.astype(BF16) for d in d_sb] + [dfq.astype(BF16), dfk.astype(BF16), dfv.astype(BF16), d_dil], axis=1)
    daux = jnp.concatenate([dlx, dlg, df], axis=1)
    dh1 = matmul(dqkv, W["qkv"], trans_b=True, name=f"{tag}_d_h1a")
    dh1 = matmul(daux, W["aux"], trans_b=True, residual=dh1, name=f"{tag}_d_h1b")
    gW["qkv"] = mm(s["h1"], dqkv, name=f"{tag}_g_qkv")
    gW["aux"] = mm(s["h1"], daux, name=f"{tag}_g_aux")
    dx, gP["norm_mix_g"] = rmsnorm_bwd(s["x"], P["norm_mix_g"], dh1, dx1, name=f"{tag}_norm_mix_bwd")
    return dx, gW, gP, ds_band


def local_step(x, mem, target, Ws, Ps, rel_bias, final_norm_g):
    B = x.shape[0]
    x2d = x.reshape(B * SEQ, D_MODEL)
    mem2d = mem.reshape(B * N_MEM, D_MODEL)
    bias = relbias_expand(rel_bias, name="relbias_expand")
    saved = []
    h = x2d
    for l in range(DEPTH):
        h, s = layer_fwd(h, mem2d, Ws[l], Ps[l], bias, f"l{l}")
        saved.append(s)
    loss, dh, d_final = loss_head(h, final_norm_g, target.reshape(B * SEQ, D_MODEL), name="loss_head")
    gWs, gPs, ds_bands = [None] * DEPTH, [None] * DEPTH, []
    for l in range(DEPTH - 1, -1, -1):
        dh, gWs[l], gPs[l], ds = layer_bwd(dh, mem2d, Ws[l], Ps[l], bias, saved[l], f"l{l}")
        ds_bands.append(ds)
    d_rel = relbias_reduce(sum_cast([d.reshape(-1, BAND) for d in ds_bands], F32, name="ds_band_sum").reshape(-1, BLOCK, BAND),
                           name="relbias_reduce")
    return loss, dh.reshape(B, SEQ, D_MODEL), gWs, gPs, d_rel, d_final


def small_params(p, l):
    row = lambda name: p[name][l].reshape(1, -1)
    ffn_w, ffn_b = p["ffn_conv_w"][l], row("ffn_conv_b")
    return dict(
        norm_mix_g=row("norm_mix_g"), norm_cross_g=row("norm_cross_g"), norm_mem_g=row("norm_mem_g"), norm_ffn_g=row("norm_ffn_g"),
        bf=jnp.pad(row("b_forget"), ((0, 0), (0, LANES - N_HEADS))),
        lru_conv_w=p["lru_conv_w"][l], lru_conv_b=row("lru_conv_b"), wa=_block_diag_halves(p["lru_w_a"][l]), lru_b_a=row("lru_b_a"),
        wx=_block_diag_halves(p["lru_w_x"][l]), lru_b_x=row("lru_b_x"), lru_lambda=row("lru_lambda"),
        wu=ffn_w[:, :D_FF], wg=ffn_w[:, D_FF:], bu=ffn_b[:, :D_FF], bg=ffn_b[:, D_FF:])


def canonical_weights(w_in, w_out, w_cq, w_ck, w_cv, w_co, w_up, w_down):
    sb_fox, fox_f, rest = w_in[:, :6 * GROUP_W], w_in[:, 6 * GROUP_W:6 * GROUP_W + N_HEADS], w_in[:, 6 * GROUP_W + N_HEADS:]
    dil, lru = rest[:, :3 * GROUP_W], rest[:, 3 * GROUP_W:]
    pad = jnp.zeros((w_in.shape[0], AUX_W - 2 * GROUP_W - N_HEADS), w_in.dtype)
    return dict(qkv=jnp.concatenate([sb_fox, dil], axis=1), aux=jnp.concatenate([lru, fox_f, pad], axis=1), out=w_out,
                cq=w_cq, ckv=jnp.concatenate([w_ck, w_cv], axis=1), coT=w_co.T, upT=w_up.T, down=w_down)


def native_grads(g):
    qkv, aux = g["qkv"], g["aux"]
    w_in = jnp.concatenate([qkv[:, :6 * GROUP_W], aux[:, 2 * GROUP_W:2 * GROUP_W + N_HEADS], qkv[:, 6 * GROUP_W:],
                            aux[:, :2 * GROUP_W]], axis=1)
    return (w_in, g["out"], g["cq"], g["ckv"][:, :GROUP_W], g["ckv"][:, GROUP_W:], g["coT"].T, g["upT"].T, g["down"])


ANY = pl.BlockSpec(memory_space=pl.ANY)
VMEM_SPEC = pl.BlockSpec(memory_space=pltpu.VMEM)


def _place():
    x, y, c = lax.axis_index("x"), lax.axis_index("y"), lax.axis_index("c")
    other_chips = [(1 - x, y), (x, 1 - y), (1 - x, 1 - y)]
    return x, y, c, other_chips


def _gather_body(x_ref, out_ref, send_sems, recv_sems, local_sem):
    x, y, c, chips = _place()
    me, sibling = (x, y, c), (x, y, 1 - c)

    def slot(px, py, pc):
        return out_ref.at[4 * px + 2 * py + pc]

    def copy(k, block, to, src=None):
        return pltpu.make_async_remote_copy(
            src_ref=slot(*block) if src is None else src, dst_ref=slot(*block),
            send_sem=send_sems.at[k], recv_sem=recv_sems.at[k], device_id=to, device_id_type=MESH)

    if local_sem is not None:
        mine = pltpu.make_async_copy(x_ref, slot(*me), local_sem)
        mine.start()
    first = [copy(0, me, sibling, src=x_ref)]
    first += [copy(1 + j, me, (*chip, c), src=x_ref) for j, chip in enumerate(chips)]
    for cp in first:
        cp.start()
    passed = [copy(4 + j, (*chip, c), sibling) for j, chip in enumerate(chips)]
    for j, chip in enumerate(chips):
        copy(1 + j, (*chip, c), me).wait_recv()
        passed[j].start()
    copy(0, sibling, me).wait_recv()
    for j, chip in enumerate(chips):
        copy(4 + j, (*chip, 1 - c), me).wait_recv()
    for cp in first + passed:
        cp.wait_send()
    if local_sem is not None:
        mine.wait()


_GATHER_SEMS = [pltpu.SemaphoreType.DMA((7,)), pltpu.SemaphoreType.DMA((7,)), pltpu.SemaphoreType.DMA]


def allgather_hbm(shard, me, *, name):
    def body(x_ref, out_ref, send_sems, recv_sems):
        _gather_body(x_ref, out_ref, send_sems, recv_sems, None)

    others = pl.pallas_call(
        body, name=name, in_specs=[ANY], out_specs=ANY,
        out_shape=jax.ShapeDtypeStruct((N_DEV,) + shard.shape, shard.dtype), scratch_shapes=_GATHER_SEMS[:2],
    )(shard)
    return lax.dynamic_update_slice(others, shard[None], (me, 0, 0))


def allgather_small(x, *, name, reduce=False):
    def body(x_ref, out_ref, *rest):
        _gather_body(x_ref, out_ref, *rest[-3:])
        if reduce:
            acc = out_ref[0]
            for d in range(1, N_DEV):
                acc = acc + out_ref[d]
            rest[0][...] = acc

    sd = jax.ShapeDtypeStruct
    return pl.pallas_call(
        body, name=name, in_specs=[VMEM_SPEC], out_specs=[VMEM_SPEC, VMEM_SPEC] if reduce else VMEM_SPEC,
        out_shape=[sd((N_DEV,) + x.shape, x.dtype), sd(x.shape, x.dtype)] if reduce else sd((N_DEV,) + x.shape, x.dtype),
        scratch_shapes=_GATHER_SEMS, compiler_params=pltpu.CompilerParams(vmem_limit_bytes=VMEM_LIMIT_V7X),
    )(x)


N_CHIPS = 4


def pair_exchange(g, *, name):
    _, R, C = g.shape

    def body(g_ref, recv_ref, send_sems, recv_sems):
        x, y, c, _ = _place()
        sibling = (x, y, 1 - c)
        remote = [pltpu.make_async_remote_copy(
            src_ref=g_ref.at[2 * q + (1 - c)], dst_ref=recv_ref.at[q], send_sem=send_sems.at[q], recv_sem=recv_sems.at[q],
            device_id=sibling, device_id_type=MESH) for q in range(N_CHIPS)]
        for cp in remote:
            cp.start()
        for cp in remote:
            cp.wait_recv()
        for cp in remote:
            cp.wait_send()

    return pl.pallas_call(
        body, name=name, in_specs=[ANY], out_specs=ANY, out_shape=jax.ShapeDtypeStruct((N_CHIPS, R, C), g.dtype),
        scratch_shapes=[pltpu.SemaphoreType.DMA((N_CHIPS,))] * 2,
    )(g)


def chip_exchange(s, *, name):
    _, R, C = s.shape

    def body(s_ref, o0, o1, o2, send_sems, recv_sems):
        x, y, c, chips = _place()
        outs = (o0, o1, o2)
        copies = [pltpu.make_async_remote_copy(
            src_ref=s_ref.at[2 * cx + cy], dst_ref=outs[j], send_sem=send_sems.at[j], recv_sem=recv_sems.at[j],
            device_id=(cx, cy, c), device_id_type=MESH) for j, (cx, cy) in enumerate(chips)]
        for cp in copies:
            cp.start()
        for cp in copies:
            cp.wait_recv()
        for cp in copies:
            cp.wait_send()

    sd = jax.ShapeDtypeStruct((R, C), s.dtype)
    return pl.pallas_call(
        body, name=name, in_specs=[ANY], out_specs=[ANY] * 3, out_shape=[sd] * 3,
        scratch_shapes=[pltpu.SemaphoreType.DMA((3,)), pltpu.SemaphoreType.DMA((3,))],
    )(s)


WEIGHTS = ("norm_mix_g", "w_in", "b_forget", "lru_conv_w", "lru_conv_b", "lru_w_a", "lru_b_a", "lru_w_x", "lru_b_x", "lru_lambda",
           "w_out", "norm_cross_g", "norm_mem_g", "w_cq", "w_ck", "w_cv", "w_co", "norm_ffn_g", "w_up", "ffn_conv_w", "ffn_conv_b",
           "w_down", "rel_bias", "final_norm_g")
LARGE = ("w_in", "w_out", "w_cq", "w_ck", "w_cv", "w_co", "w_up", "w_down")
COLUMN_SPLIT_SMALL = ("lru_conv_w", "ffn_conv_w")
PACK = (("qkv", 128, 2304), ("aux", 128, 640), ("out", 128, 1024), ("cq", 128, 256), ("ckv", 128, 512), ("coT", 128, 256),
        ("upT", 704, 1024), ("down", 352, 1024))
PACK_W = 1024


def _pack_rows(parts):
    return jnp.concatenate([p.reshape(-1, PACK_W) for p in parts], axis=0)


def _pad_rows(flat, mult=8 * LANES):
    n = flat.shape[0]
    return jnp.pad(flat, (0, (-n) % mult)).reshape(-1, LANES)


def kernel(x, mem, norm_mix_g, w_in, b_forget, lru_conv_w, lru_conv_b, lru_w_a, lru_b_a, lru_w_x, lru_b_x, lru_lambda, w_out, norm_cross_g, norm_mem_g, w_cq, w_ck, w_cv, w_co, norm_ffn_g, w_up, ffn_conv_w, ffn_conv_b, w_down, rel_bias, final_norm_g, loss_target, m_norm_mix_g, m_w_in, m_b_forget, m_lru_conv_w, m_lru_conv_b, m_lru_w_a, m_lru_b_a, m_lru_w_x, m_lru_b_x, m_lru_lambda, m_w_out, m_norm_cross_g, m_norm_mem_g, m_w_cq, m_w_ck, m_w_cv, m_w_co, m_norm_ffn_g, m_w_up, m_ffn_conv_w, m_ffn_conv_b, m_w_down, m_rel_bias, m_final_norm_g, v_norm_mix_g, v_w_in, v_b_forget, v_lru_conv_w, v_lru_conv_b, v_lru_w_a, v_lru_b_a, v_lru_w_x, v_lru_b_x, v_lru_lambda, v_w_out, v_norm_cross_g, v_norm_mem_g, v_w_cq, v_w_ck, v_w_cv, v_w_co, v_norm_ffn_g, v_w_up, v_ffn_conv_w, v_ffn_conv_b, v_w_down, v_rel_bias, v_final_norm_g):
    w = dict(norm_mix_g=norm_mix_g, w_in=w_in, b_forget=b_forget, lru_conv_w=lru_conv_w, lru_conv_b=lru_conv_b, lru_w_a=lru_w_a,
             lru_b_a=lru_b_a, lru_w_x=lru_w_x, lru_b_x=lru_b_x, lru_lambda=lru_lambda, w_out=w_out, norm_cross_g=norm_cross_g,
             norm_mem_g=norm_mem_g, w_cq=w_cq, w_ck=w_ck, w_cv=w_cv, w_co=w_co, norm_ffn_g=norm_ffn_g, w_up=w_up,
             ffn_conv_w=ffn_conv_w, ffn_conv_b=ffn_conv_b, w_down=w_down, rel_bias=rel_bias, final_norm_g=final_norm_g)
    m = dict(norm_mix_g=m_norm_mix_g, w_in=m_w_in, b_forget=m_b_forget, lru_conv_w=m_lru_conv_w, lru_conv_b=m_lru_conv_b,
             lru_w_a=m_lru_w_a, lru_b_a=m_lru_b_a, lru_w_x=m_lru_w_x, lru_b_x=m_lru_b_x, lru_lambda=m_lru_lambda, w_out=m_w_out,
             norm_cross_g=m_norm_cross_g, norm_mem_g=m_norm_mem_g, w_cq=m_w_cq, w_ck=m_w_ck, w_cv=m_w_cv, w_co=m_w_co,
             norm_ffn_g=m_norm_ffn_g, w_up=m_w_up, ffn_conv_w=m_ffn_conv_w, ffn_conv_b=m_ffn_conv_b, w_down=m_w_down,
             rel_bias=m_rel_bias, final_norm_g=m_final_norm_g)
    v = dict(norm_mix_g=v_norm_mix_g, w_in=v_w_in, b_forget=v_b_forget, lru_conv_w=v_lru_conv_w, lru_conv_b=v_lru_conv_b,
             lru_w_a=v_lru_w_a, lru_b_a=v_lru_b_a, lru_w_x=v_lru_w_x, lru_b_x=v_lru_b_x, lru_lambda=v_lru_lambda, w_out=v_w_out,
             norm_cross_g=v_norm_cross_g, norm_mem_g=v_norm_mem_g, w_cq=v_w_cq, w_ck=v_w_ck, w_cv=v_w_cv, w_co=v_w_co,
             norm_ffn_g=v_norm_ffn_g, w_up=v_w_up, ffn_conv_w=v_ffn_conv_w, ffn_conv_b=v_ffn_conv_b, w_down=v_w_down,
             rel_bias=v_rel_bias, final_norm_g=v_final_norm_g)
    me = 4 * lax.axis_index("x") + 2 * lax.axis_index("y") + lax.axis_index("c")

    conv_shard = jnp.concatenate([w[n].reshape(-1) for n in COLUMN_SPLIT_SMALL])
    conv_all = allgather_small(_pad_rows(conv_shard), name="gather_conv").reshape(N_DEV, -1)
    full = dict(w)
    off = 0
    for n in COLUMN_SPLIT_SMALL:
        d, k, c = w[n].shape
        blocks = conv_all[:, off:off + d * k * c].reshape(N_DEV, d, k, c)
        full[n] = blocks.transpose(1, 2, 0, 3).reshape(d, k, N_DEV * c)
        off += d * k * c

    shard_parts = []
    for l in range(DEPTH):
        canon = canonical_weights(*[w[n][l] for n in LARGE])
        shard_parts += [canon[k].astype(BF16) for k, _, _ in PACK]
    packed = allgather_hbm(_pack_rows(shard_parts), me, name="gather_weights")
    Ws, row = [], 0
    for l in range(DEPTH):
        W = {}
        for k, r, c in PACK:
            n_rows = r * c // PACK_W
            W[k] = packed[:, row:row + n_rows].reshape(N_DEV * r, c)
            row += n_rows
        upT = W.pop("upT")
        W["up_u"], W["up_g"] = upT[:D_FF], upT[D_FF:]
        Ws.append(W)
    Ps = [small_params(full, l) for l in range(DEPTH)]

    loss, grad_x, gWs, gPs, d_rel, d_final = local_step(x, mem, loss_target, Ws, Ps, rel_bias, final_norm_g.reshape(1, -1))

    grad_parts = []
    for l in range(DEPTH):
        g = dict(gWs[l])
        g["upT"] = jnp.concatenate([g.pop("up_u"), g.pop("up_g")], axis=0)
        grad_parts += [g[k].reshape(N_DEV, r * c // PACK_W, PACK_W) for k, r, c in PACK]
    g_all = jnp.concatenate(grad_parts, axis=1)
    rows = g_all.shape[1]
    got = pair_exchange(g_all, name="grads_pair_exchange")
    own = lax.dynamic_index_in_dim(g_all.reshape(N_CHIPS, 2, rows, PACK_W), lax.axis_index("c"), axis=1, keepdims=False)
    pair = sum_cast([own.reshape(-1, PACK_W), got.reshape(-1, PACK_W)], GRAD_WIRE, name="grads_pair_sum").reshape(N_CHIPS, rows, PACK_W)
    from_x, from_y, from_xy = chip_exchange(pair, name="grads_chip_exchange")
    mine = lax.dynamic_index_in_dim(pair, 2 * lax.axis_index("x") + lax.axis_index("y"), axis=0, keepdims=False)
    g_shard = sum_cast([mine, from_x, from_y, from_xy], F32, name="grads_chip_sum")
    grads, row = {}, 0
    per_layer = []
    for l in range(DEPTH):
        g = {}
        for k, r, c in PACK:
            n_rows = r * c // PACK_W
            g[k] = g_shard[row:row + n_rows].reshape(r, c)
            row += n_rows
        per_layer.append(native_grads(g))
    for i, n in enumerate(LARGE):
        grads[n] = jnp.stack([per_layer[l][i] for l in range(DEPTH)])

    small_names = [n for n in WEIGHTS if n not in LARGE and n not in ("rel_bias", "final_norm_g")]
    pieces = [gPs[l][n].reshape(-1) for n in small_names for l in range(DEPTH)] + [d_rel.reshape(-1), d_final.reshape(-1), loss[0, :1]]
    sizes = [p.shape[0] for p in pieces]
    _, total = allgather_small(_pad_rows(jnp.concatenate(pieces)), name="allreduce_small", reduce=True)
    total = total.reshape(-1)
    off, it = 0, iter(sizes)
    for n in small_names:
        per = []
        for l in range(DEPTH):
            sz = next(it)
            per.append(total[off:off + sz])
            off += sz
        full_shape = (DEPTH,) + full[n].shape[1:]
        gfull = jnp.stack(per).reshape(full_shape)
        if n in COLUMN_SPLIT_SMALL:
            c = w[n].shape[-1]
            gfull = lax.dynamic_slice_in_dim(gfull, me * c, c, axis=gfull.ndim - 1)
        grads[n] = gfull
    grads["rel_bias"] = total[off:off + rel_bias.size].reshape(rel_bias.shape)
    off += rel_bias.size
    grads["final_norm_g"] = total[off:off + D_MODEL]
    off += D_MODEL
    loss_out = total[off]

    delta, new_m, new_v = {}, {}, {}
    for n in LARGE:
        shape = w[n].shape
        two_d = lambda a: a.reshape(-1, shape[-1])
        d_, m_, v_ = adamw(two_d(w[n]), two_d(grads[n]), two_d(m[n]), two_d(v[n]), name=f"adamw_{n}")
        delta[n], new_m[n], new_v[n] = d_.reshape(shape), m_.reshape(shape), v_.reshape(shape)
    small_all = [n for n in WEIGHTS if n not in LARGE]
    flat = lambda src: _pad_rows(jnp.concatenate([src[n].reshape(-1) for n in small_all]))
    d_, m_, v_ = adamw(flat(w), flat(grads), flat(m), flat(v), name="adamw_small")
    off = 0
    for n in small_all:
        sz, shape = w[n].size, w[n].shape
        delta[n], new_m[n], new_v[n] = (a.reshape(-1)[off:off + sz].reshape(shape) for a in (d_, m_, v_))
        off += sz

    return (loss_out, grad_x, *[grads[n] for n in WEIGHTS], *[delta[n] for n in WEIGHTS], *[new_m[n] for n in WEIGHTS],
            *[new_v[n] for n in WEIGHTS])
```

```python
import functools
import math

import numpy as np
import jax
import jax.numpy as jnp
from jax import lax
from jax.experimental import pallas as pl
from jax.experimental.pallas import tpu as pltpu

F32 = jnp.float32
BF16 = jnp.bfloat16
MESH = pl.DeviceIdType.MESH

N_DEV = 8
D_MODEL = 1024
SEQ = 2048
DEPTH = 2
HEAD_DIM = 64
N_HEADS = 4
GROUP_W = N_HEADS * HEAD_DIM
D_FF = 2816
N_MEM = 256
NUM_BUCKETS = 32
MAX_DISTANCE = 2048
BLOCK = 128
DILATIONS = (1, 4, 16)
EPS = 1e-6
LRU_C = 8.0
Q_SCALE = HEAD_DIM ** -0.5
AUX_W = 640
LRU_HALF_W = 128
LRU_HALVES = GROUP_W // LRU_HALF_W
ADAM_LR, ADAM_B1, ADAM_B2, ADAM_EPS, ADAM_WD, ADAM_STEP = 0.001, 0.9, 0.999, 1e-08, 0.01, 10

VMEM_LIMIT_V7X = 48 * 1024 * 1024


def _params(*sem):
    return pltpu.CompilerParams(dimension_semantics=sem if sem else None, vmem_limit_bytes=VMEM_LIMIT_V7X)


def _pick(n, cands):
    for c in cands:
        if n % c == 0:
            return c
    return n


def _largest_tile(n, cap, align):
    best = None
    for t in range(align, min(n, cap) + 1, align):
        if n % t == 0:
            best = t
    return n if best is None else best


def matmul(a, b, *, name, trans_a=False, trans_b=False, out_dtype=F32, residual=None):
    (K, M) = a.shape if trans_a else a.shape[::-1]
    (N, Kb) = b.shape if trans_b else b.shape[::-1]
    assert K == Kb, (a.shape, b.shape)
    tm = _largest_tile(M, 512, 128)
    tn = _largest_tile(N, 1408, 128)
    tk = _largest_tile(K, 2816, 128)
    nk = K // tk
    a_spec = pl.BlockSpec((tk, tm), lambda i, j, k: (k, i)) if trans_a else pl.BlockSpec((tm, tk), lambda i, j, k: (i, k))
    b_spec = pl.BlockSpec((tn, tk), lambda i, j, k: (j, k)) if trans_b else pl.BlockSpec((tk, tn), lambda i, j, k: (k, j))
    o_spec = pl.BlockSpec((tm, tn), lambda i, j, k: (i, j))
    dims = (((0 if trans_a else 1,), (1 if trans_b else 0,)), ((), ()))
    has_res = residual is not None

    def body(*refs):
        a_ref, b_ref = refs[0], refs[1]
        r_ref = refs[2] if has_res else None
        part = lax.dot_general(a_ref[...].astype(BF16), b_ref[...].astype(BF16), dims, preferred_element_type=F32)
        if nk == 1:
            if has_res:
                part = part + r_ref[...].astype(F32)
            refs[-1][...] = part.astype(out_dtype)
            return
        o_ref, acc_ref = refs[-2], refs[-1]
        k = pl.program_id(2)

        @pl.when(k == 0)
        def _():
            acc_ref[...] = part

        @pl.when(k > 0)
        def _():
            acc_ref[...] += part

        @pl.when(k == nk - 1)
        def _():
            r = acc_ref[...]
            if has_res:
                r = r + r_ref[...].astype(F32)
            o_ref[...] = r.astype(out_dtype)

    ops = (a, b) + ((residual,) if has_res else ())
    return pl.pallas_call(
        body, name=name, grid=(M // tm, N // tn, nk),
        in_specs=[a_spec, b_spec] + ([o_spec] if has_res else []),
        out_specs=o_spec, out_shape=jax.ShapeDtypeStruct((M, N), out_dtype),
        scratch_shapes=[pltpu.VMEM((tm, tn), F32)] if nk > 1 else [],
        compiler_params=_params("parallel", "parallel", "arbitrary"),
    )(*ops)


def rmsnorm_fwd(x, g, *, name):
    R, D = x.shape
    tr = _pick(R, (512, 256))

    def body(x_ref, g_ref, o_ref):
        xv = x_ref[...]
        r = lax.rsqrt(jnp.mean(xv * xv, axis=-1, keepdims=True) + EPS)
        o_ref[...] = (xv * r * g_ref[...]).astype(BF16)

    return pl.pallas_call(
        body, name=name, grid=(R // tr,),
        in_specs=[pl.BlockSpec((tr, D), lambda i: (i, 0)), pl.BlockSpec((1, D), lambda i: (0, 0))],
        out_specs=pl.BlockSpec((tr, D), lambda i: (i, 0)), out_shape=jax.ShapeDtypeStruct((R, D), BF16),
        compiler_params=_params("parallel"),
    )(x, g)


def rmsnorm_bwd(x, g, dh, dres, *, name):
    R, D = x.shape
    tr = _pick(R, (512, 256))
    has_res = dres is not None

    def body(*refs):
        x_ref, g_ref, dh_ref = refs[:3]
        dx_ref, dg_ref = refs[-2], refs[-1]
        xv = x_ref[...]
        r = lax.rsqrt(jnp.mean(xv * xv, axis=-1, keepdims=True) + EPS)
        n = xv * r
        dhv = dh_ref[...]
        dn = dhv * g_ref[...]
        dx = r * (dn - n * jnp.mean(dn * n, axis=-1, keepdims=True))
        if has_res:
            dx = dx + refs[3][...]
        dx_ref[...] = dx
        part = jnp.sum(dhv * n, axis=0, keepdims=True)

        @pl.when(pl.program_id(0) == 0)
        def _():
            dg_ref[...] = part

        @pl.when(pl.program_id(0) > 0)
        def _():
            dg_ref[...] += part

    row = pl.BlockSpec((tr, D), lambda i: (i, 0))
    vec = pl.BlockSpec((1, D), lambda i: (0, 0))
    ops = (x, g, dh) + ((dres,) if has_res else ())
    return pl.pallas_call(
        body, name=name, grid=(R // tr,),
        in_specs=[row, vec, row] + ([row] if has_res else []),
        out_specs=[row, vec],
        out_shape=[jax.ShapeDtypeStruct((R, D), F32), jax.ShapeDtypeStruct((1, D), F32)],
        compiler_params=_params("arbitrary"),
    )(*ops)


_SQRT_HALF = 0.7071067811865476
_INV_SQRT_2PI = 0.3989422804014327


def _erf(x):
    ax = jnp.abs(x)
    t = 1.0 / (1.0 + 0.3275911 * ax)
    poly = t * (0.254829592 + t * (-0.284496736 + t * (1.421413741 + t * (-1.453152027 + t * 1.061405429))))
    y = 1.0 - poly * jnp.exp(-ax * ax)
    return jnp.where(x < 0, -y, y)


def _gelu_cdf(x):
    return 0.5 * (1.0 + _erf(x * _SQRT_HALF))


def _gelu_and_grad(x):
    cdf = _gelu_cdf(x)
    return x * cdf, cdf + x * _INV_SQRT_2PI * jnp.exp(-0.5 * x * x)


def _shift_down(main, halo, first, shifts):
    halo = jnp.where(first, 0.0, halo)
    ext = jnp.concatenate([halo, main], axis=0)
    return [pltpu.roll(ext, s, 0)[8:] for s in shifts]


def _conv3(main, halo, first, w, b):
    m1, m2 = _shift_down(main, halo, first, (1, 2))
    return ((b + w[0:1] * m2) + w[1:2] * m1) + w[2:3] * main, m1, m2


def glu_fwd(hu, hg, wu, wg, bu, bg, *, name):
    T, F = hu.shape
    tm, tf = 512, _largest_tile(F, 704, 128)
    hb = tm // 8
    blocks_per_example = SEQ // tm

    def body(hu_ref, hg_ref, hau_ref, hag_ref, wu_ref, wg_ref, bu_ref, bg_ref, o_ref):
        first = pl.program_id(0) % blocks_per_example == 0
        up, _, _ = _conv3(hu_ref[...], hau_ref[...], first, wu_ref[...], bu_ref[...])
        gate, _, _ = _conv3(hg_ref[...], hag_ref[...], first, wg_ref[...], bg_ref[...])
        o_ref[...] = (gate * _gelu_cdf(gate) * up).astype(BF16)

    main = pl.BlockSpec((tm, tf), lambda i, j: (i, j))
    halo = pl.BlockSpec((8, tf), lambda i, j: (jnp.maximum(i * hb - 1, 0), j))
    w3 = pl.BlockSpec((3, tf), lambda i, j: (0, j))
    b1 = pl.BlockSpec((1, tf), lambda i, j: (0, j))
    return pl.pallas_call(
        body, name=name, grid=(T // tm, F // tf),
        in_specs=[main, main, halo, halo, w3, w3, b1, b1],
        out_specs=main, out_shape=jax.ShapeDtypeStruct((T, F), BF16),
        compiler_params=_params("parallel", "parallel"),
    )(hu, hg, hu, hg, wu, wg, bu, bg)


def glu_bwd(hu, hg, dact, wu, wg, bu, bg, *, name):
    T, F = hu.shape
    tm, tf = 512, _largest_tile(F, 704, 128)
    hb = tm // 8
    blocks_per_example = SEQ // tm

    def body(hu_ref, hg_ref, hau_ref, hag_ref, da_ref, wu_ref, wg_ref, bu_ref, bg_ref,
             du_ref, dg_ref, dwu_ref, dwg_ref, dbu_ref, dbg_ref):
        i = pl.program_id(1)
        first = i % blocks_per_example == 0
        xu, xg = hu_ref[...], hg_ref[...]
        up, u1, u2 = _conv3(xu, hau_ref[...], first, wu_ref[...], bu_ref[...])
        gate, g1, g2 = _conv3(xg, hag_ref[...], first, wg_ref[...], bg_ref[...])
        act, dact_dgate = _gelu_and_grad(gate)
        da = da_ref[...]
        dup = da * act
        dgate = da * up * dact_dgate
        du_ref[...] = dup
        dg_ref[...] = dgate

        def sums(d, x0, x1, x2):
            s = lambda v: jnp.sum(v, axis=0, keepdims=True)
            return jnp.concatenate([s(d * x2), s(d * x1), s(d * x0)], axis=0), s(d)

        pwu, pbu = sums(dup, xu, u1, u2)
        pwg, pbg = sums(dgate, xg, g1, g2)

        @pl.when(i == 0)
        def _():
            dwu_ref[...] = pwu
            dwg_ref[...] = pwg
            dbu_ref[...] = pbu
            dbg_ref[...] = pbg

        @pl.when(i > 0)
        def _():
            dwu_ref[...] += pwu
            dwg_ref[...] += pwg
            dbu_ref[...] += pbu
            dbg_ref[...] += pbg

    main = pl.BlockSpec((tm, tf), lambda j, i: (i, j))
    halo = pl.BlockSpec((8, tf), lambda j, i: (jnp.maximum(i * hb - 1, 0), j))
    w3 = pl.BlockSpec((3, tf), lambda j, i: (0, j))
    b1 = pl.BlockSpec((1, tf), lambda j, i: (0, j))
    sd = jax.ShapeDtypeStruct
    return pl.pallas_call(
        body, name=name, grid=(F // tf, T // tm),
        in_specs=[main, main, halo, halo, main, w3, w3, b1, b1],
        out_specs=[main, main, w3, w3, b1, b1],
        out_shape=[sd((T, F), F32), sd((T, F), F32), sd((3, F), F32), sd((3, F), F32), sd((1, F), F32), sd((1, F), F32)],
        compiler_params=_params("parallel", "arbitrary"),
    )(hu, hg, hu, hg, dact, wu, wg, bu, bg)


def conv3_transpose(d, w, *, name):
    T, F = d.shape
    tm, tf = 512, _largest_tile(F, 704, 128)
    hb = tm // 8
    blocks_per_example = SEQ // tm
    n_halo_blocks = T // 8

    def body(d_ref, ha_ref, w_ref, o_ref):
        last = pl.program_id(0) % blocks_per_example == blocks_per_example - 1
        main = d_ref[...]
        halo = jnp.where(last, 0.0, ha_ref[...])
        ext = jnp.concatenate([main, halo], axis=0)
        n = tm + 8
        p1 = pltpu.roll(ext, n - 1, 0)[:tm]
        p2 = pltpu.roll(ext, n - 2, 0)[:tm]
        w = w_ref[...]
        o_ref[...] = (w[2:3] * main + w[1:2] * p1 + w[0:1] * p2).astype(BF16)

    main = pl.BlockSpec((tm, tf), lambda i, j: (i, j))
    halo = pl.BlockSpec((8, tf), lambda i, j: (jnp.minimum((i + 1) * hb, n_halo_blocks - 1), j))
    return pl.pallas_call(
        body, name=name, grid=(T // tm, F // tf),
        in_specs=[main, halo, pl.BlockSpec((3, tf), lambda i, j: (0, j))],
        out_specs=main, out_shape=jax.ShapeDtypeStruct((T, F), BF16),
        compiler_params=_params("parallel", "parallel"),
    )(d, d, w)


def loss_head(x, g, target, *, name):
    T, D = x.shape
    tr = 256

    def body(x_ref, g_ref, t_ref, loss_ref, dx_ref, dg_ref):
        xv = x_ref[...]
        gv = g_ref[...]
        r = lax.rsqrt(jnp.mean(xv * xv, axis=-1, keepdims=True) + EPS)
        n = xv * r
        err = n * gv - t_ref[...]
        part_loss = jnp.zeros((1, 128), F32) + 0.5 * jnp.sum(jnp.mean(err * err, axis=-1, keepdims=True))
        dy = err * (1.0 / D)
        dn = dy * gv
        dx_ref[...] = r * (dn - n * jnp.mean(dn * n, axis=-1, keepdims=True))
        part_g = jnp.sum(dy * n, axis=0, keepdims=True)

        @pl.when(pl.program_id(0) == 0)
        def _():
            loss_ref[...] = part_loss
            dg_ref[...] = part_g

        @pl.when(pl.program_id(0) > 0)
        def _():
            loss_ref[...] += part_loss
            dg_ref[...] += part_g

    row = pl.BlockSpec((tr, D), lambda i: (i, 0))
    vec = pl.BlockSpec((1, D), lambda i: (0, 0))
    sd = jax.ShapeDtypeStruct
    return pl.pallas_call(
        body, name=name, grid=(T // tr,),
        in_specs=[row, vec, row],
        out_specs=[pl.BlockSpec((1, 128), lambda i: (0, 0)), row, vec],
        out_shape=[sd((1, 128), F32), sd((T, D), F32), sd((1, D), F32)],
        compiler_params=_params("arbitrary"),
    )(x, g, target)


def adamw(w, g, m, v, *, name):
    R, C = w.shape
    tr = _pick(R, (256, 128, 64, 32, 16, 8))

    def body(w_ref, g_ref, m_ref, v_ref, d_ref, nm_ref, nv_ref):
        gv = g_ref[...]
        mn = ADAM_B1 * m_ref[...] + (1.0 - ADAM_B1) * gv
        vn = ADAM_B2 * v_ref[...] + (1.0 - ADAM_B2) * (gv * gv)
        m_hat = mn / (1.0 - ADAM_B1 ** ADAM_STEP)
        v_hat = vn / (1.0 - ADAM_B2 ** ADAM_STEP)
        d_ref[...] = -ADAM_LR * (m_hat / (jnp.sqrt(v_hat) + ADAM_EPS) + ADAM_WD * w_ref[...])
        nm_ref[...] = mn
        nv_ref[...] = vn

    blk = pl.BlockSpec((tr, C), lambda i: (i, 0))
    sd = jax.ShapeDtypeStruct((R, C), F32)
    return pl.pallas_call(
        body, name=name, grid=(R // tr,), in_specs=[blk] * 4, out_specs=[blk] * 3, out_shape=[sd] * 3,
        compiler_params=_params("parallel"),
    )(w, g, m, v)


def _softplus(x):
    return jnp.maximum(x, 0.0) + jnp.log(1.0 + jnp.exp(-jnp.abs(x)))


def _lru_gates(x, cw, cb, wa, ba, wx, bx, lam):
    S = x.shape[0]
    row = lax.broadcasted_iota(jnp.int32, (S, 1), 0)

    def back(s):
        return jnp.where(row >= s, pltpu.roll(x, s, 0), 0.0)

    xc = (((cb + cw[0:1] * back(3)) + cw[1:2] * back(2)) + cw[2:3] * back(1)) + cw[3:4] * x
    xb = xc.astype(BF16)
    r = jax.nn.sigmoid(jnp.dot(xb, wa, preferred_element_type=F32) + ba)
    ig = jax.nn.sigmoid(jnp.dot(xb, wx, preferred_element_type=F32) + bx)
    sp = _softplus(-lam)
    la = -LRU_C * r * sp
    a = jnp.exp(la)
    y = 2.0 * la
    one_minus_a2 = jnp.where(y > -0.05, -y * (1.0 + y * (0.5 + y * (1.0 / 6.0 + y * (1.0 / 24.0)))), 1.0 - jnp.exp(y))
    mm = jnp.sqrt(one_minus_a2)
    return xc, xb, r, ig, sp, a, mm


def lru_fwd(aux, cw, cb, wa, ba, wx, bx, lam, *, name):
    T = aux.shape[0]
    S, C = SEQ, LRU_HALF_W

    def body(x_ref, g_ref, cw_ref, cb_ref, wa_ref, ba_ref, wx_ref, bx_ref, lam_ref, o_ref, h_ref, a_s, u_s):
        xc, _, r, ig, sp, a, mm = _lru_gates(x_ref[...], cw_ref[...], cb_ref[...], wa_ref[...], ba_ref[...],
                                             wx_ref[...], bx_ref[...], lam_ref[...])
        a_s[...] = a
        u_s[...] = mm * (ig * xc)

        def group(i, h):
            base = pl.multiple_of(i * 8, 8)
            a8 = a_s[pl.ds(base, 8), :]
            u8 = u_s[pl.ds(base, 8), :]
            for rr in range(8):
                h = a8[rr:rr + 1] * h + u8[rr:rr + 1]
                h_ref[pl.ds(base + rr, 1), :] = h
            return h

        lax.fori_loop(0, S // 8, group, jnp.zeros((1, C), F32))
        gate = g_ref[...]
        o_ref[...] = (h_ref[...] * (gate * _gelu_cdf(gate))).astype(BF16)

    blk = lambda col: pl.BlockSpec((S, C), lambda c, b: (b, col + c))
    par = lambda rows: pl.BlockSpec((rows, C), lambda c, b: (0, c))
    sq = pl.BlockSpec((None, C, C), lambda c, b: (c, 0, 0))
    sd = jax.ShapeDtypeStruct
    W = LRU_HALVES * C
    return pl.pallas_call(
        body, name=name, grid=(LRU_HALVES, T // S),
        in_specs=[blk(0), blk(LRU_HALVES), par(4), par(1), sq, par(1), sq, par(1), par(1)],
        out_specs=[blk(0), blk(0)], out_shape=[sd((T, W), BF16), sd((T, W), F32)],
        scratch_shapes=[pltpu.VMEM((S, C), F32), pltpu.VMEM((S, C), F32)],
        compiler_params=_params("parallel", "parallel"),
    )(aux, aux, cw, cb, wa, ba, wx, bx, lam)


def lru_bwd(aux, h, dmixed, cw, cb, wa, ba, wx, bx, lam, *, name):
    T = aux.shape[0]
    S, C = SEQ, LRU_HALF_W

    def body(x_ref, g_ref, h_ref, do_ref, cw_ref, cb_ref, wa_ref, ba_ref, wx_ref, bx_ref, lam_ref,
             dx_ref, dgate_ref, dcw_ref, dcb_ref, dwa_ref, dba_ref, dwx_ref, dbx_ref, dlam_ref, a_s, d_s):
        x = x_ref[...]
        cw = cw_ref[...]
        lam = lam_ref[...]
        xc, xb, r, ig, sp, a, mm = _lru_gates(x, cw, cb_ref[...], wa_ref[...], ba_ref[...], wx_ref[...], bx_ref[...], lam)
        gate = g_ref[...]
        gl, dgl = _gelu_and_grad(gate)
        dout = do_ref[...]
        hv = h_ref[...]
        dgate_ref[...] = dout * hv * dgl
        a_s[...] = a
        d_s[...] = dout * gl

        def group(i, c):
            base = pl.multiple_of((S // 8 - 1 - i) * 8, 8)
            a8 = a_s[pl.ds(base, 8), :]
            d8 = d_s[pl.ds(base, 8), :]
            for rr in range(7, -1, -1):
                d = d8[rr:rr + 1] + c
                d_s[pl.ds(base + rr, 1), :] = d
                c = a8[rr:rr + 1] * d
            return c

        lax.fori_loop(0, S // 8, group, jnp.zeros((1, C), F32))
        row = lax.broadcasted_iota(jnp.int32, (S, 1), 0)
        dht = d_s[...]
        h_prev = jnp.where(row >= 1, pltpu.roll(hv, 1, 0), 0.0)
        da = dht * h_prev
        gx = ig * xc
        dmm = dht * gx
        dig = dht * mm * xc
        dxc = dht * mm * ig
        dla = da * a - dmm * (a * a) / mm
        dr = dla * (-LRU_C * sp)
        dsp = jnp.sum(dla * (-LRU_C * r), axis=0, keepdims=True)
        dlam = dsp * (-jax.nn.sigmoid(-lam))
        dpa = dr * r * (1.0 - r)
        dpx = dig * ig * (1.0 - ig)
        dpa_b, dpx_b = dpa.astype(BF16), dpx.astype(BF16)
        nt = (((1,), (1,)), ((), ()))
        tn = (((0,), (0,)), ((), ()))
        dxc = dxc + lax.dot_general(dpa_b, wa_ref[...], nt, preferred_element_type=F32) \
                  + lax.dot_general(dpx_b, wx_ref[...], nt, preferred_element_type=F32)
        dwa = lax.dot_general(xb, dpa_b, tn, preferred_element_type=F32)
        dwx = lax.dot_general(xb, dpx_b, tn, preferred_element_type=F32)

        def fwd(v, s):
            return jnp.where(row < S - s, pltpu.roll(v, S - s, 0), 0.0)

        def back(v, s):
            return jnp.where(row >= s, pltpu.roll(v, s, 0), 0.0)

        dx_ref[...] = cw[3:4] * dxc + cw[2:3] * fwd(dxc, 1) + cw[1:2] * fwd(dxc, 2) + cw[0:1] * fwd(dxc, 3)
        s0 = lambda v: jnp.sum(v, axis=0, keepdims=True)
        dcw = jnp.concatenate([s0(dxc * back(x, 3)), s0(dxc * back(x, 2)), s0(dxc * back(x, 1)), s0(dxc * x)], axis=0)
        parts = ((dcw_ref, dcw), (dcb_ref, s0(dxc)), (dwa_ref, dwa), (dba_ref, s0(dpa)), (dwx_ref, dwx),
                 (dbx_ref, s0(dpx)), (dlam_ref, dlam))

        @pl.when(pl.program_id(1) == 0)
        def _():
            for ref, val in parts:
                ref[...] = val

        @pl.when(pl.program_id(1) > 0)
        def _():
            for ref, val in parts:
                ref[...] += val

    blk = lambda col: pl.BlockSpec((S, C), lambda c, b: (b, col + c))
    par = lambda rows: pl.BlockSpec((rows, C), lambda c, b: (0, c))
    sq = pl.BlockSpec((None, C, C), lambda c, b: (c, 0, 0))
    sd = jax.ShapeDtypeStruct
    W = LRU_HALVES * C
    vec = sd((1, W), F32)
    return pl.pallas_call(
        body, name=name, grid=(LRU_HALVES, T // S),
        in_specs=[blk(0), blk(LRU_HALVES), blk(0), blk(3 * LRU_HALVES), par(4), par(1), sq, par(1), sq, par(1), par(1)],
        out_specs=[blk(0), blk(0), par(4), par(1), sq, par(1), sq, par(1), par(1)],
        out_shape=[sd((T, W), F32), sd((T, W), F32), sd((4, W), F32), vec, sd((LRU_HALVES, C, C), F32), vec,
                   sd((LRU_HALVES, C, C), F32), vec, vec],
        scratch_shapes=[pltpu.VMEM((S, C), F32), pltpu.VMEM((S, C), F32)],
        compiler_params=_params("parallel", "arbitrary"),
    )(aux, aux, h, dmixed, cw, cb, wa, ba, wx, bx, lam)


_NT = (((1,), (1,)), ((), ()))
_TN = (((0,), (0,)), ((), ()))


def _dot(a, b, dims=None):
    if dims is None:
        return jnp.dot(a, b, preferred_element_type=F32)
    return lax.dot_general(a, b, dims, preferred_element_type=F32)


def _hs(h):
    return slice(h * HEAD_DIM, (h + 1) * HEAD_DIM)


def cross_fwd(q, kv, *, name):
    T = q.shape[0]
    tq = 512

    def body(q_ref, kv_ref, o_ref):
        for h in range(N_HEADS):
            qh = q_ref[:, _hs(h)] * Q_SCALE
            k = kv_ref[:, _hs(h)]
            v = kv_ref[:, GROUP_W + h * HEAD_DIM:GROUP_W + (h + 1) * HEAD_DIM]
            s = _dot(qh, k, _NT)
            p = jnp.exp(s - jnp.max(s, axis=-1, keepdims=True))
            p = p / jnp.sum(p, axis=-1, keepdims=True)
            o_ref[:, _hs(h)] = _dot(p.astype(BF16), v).astype(BF16)

    per = SEQ // tq
    return pl.pallas_call(
        body, name=name, grid=(T // tq,),
        in_specs=[pl.BlockSpec((tq, GROUP_W), lambda i: (i, 0)), pl.BlockSpec((N_MEM, 2 * GROUP_W), lambda i: (i // per, 0))],
        out_specs=pl.BlockSpec((tq, GROUP_W), lambda i: (i, 0)), out_shape=jax.ShapeDtypeStruct((T, GROUP_W), BF16),
        compiler_params=_params("parallel"),
    )(q, kv)


def cross_bwd(q, kv, do, *, name):
    T = q.shape[0]
    tq = 512
    per = SEQ // tq

    def body(q_ref, kv_ref, do_ref, dq_ref, dkv_ref):
        first = pl.program_id(0) % per == 0
        for h in range(N_HEADS):
            vs = slice(GROUP_W + h * HEAD_DIM, GROUP_W + (h + 1) * HEAD_DIM)
            qh = q_ref[:, _hs(h)] * Q_SCALE
            k = kv_ref[:, _hs(h)]
            v = kv_ref[:, vs]
            doh = do_ref[:, _hs(h)].astype(BF16)
            s = _dot(qh, k, _NT)
            p = jnp.exp(s - jnp.max(s, axis=-1, keepdims=True))
            p = p / jnp.sum(p, axis=-1, keepdims=True)
            dp = _dot(doh, v, _NT)
            ds = (p * (dp - jnp.sum(p * dp, axis=-1, keepdims=True))).astype(BF16)
            dq_ref[:, _hs(h)] = (_dot(ds, k) * Q_SCALE).astype(BF16)
            dk = _dot(ds, qh, _TN)
            dv = _dot(p.astype(BF16), doh, _TN)

            @pl.when(first)
            def _():
                dkv_ref[:, _hs(h)] = dk
                dkv_ref[:, vs] = dv

            @pl.when(jnp.logical_not(first))
            def _():
                dkv_ref[:, _hs(h)] += dk
                dkv_ref[:, vs] += dv

    qb = pl.BlockSpec((tq, GROUP_W), lambda i: (i, 0))
    kvb = pl.BlockSpec((N_MEM, 2 * GROUP_W), lambda i: (i // per, 0))
    sd = jax.ShapeDtypeStruct
    return pl.pallas_call(
        body, name=name, grid=(T // tq,),
        in_specs=[qb, kvb, qb], out_specs=[qb, kvb],
        out_shape=[sd((T, GROUP_W), BF16), sd(kv.shape, F32)],
        compiler_params=_params("arbitrary"),
    )(q, kv, do)


NB = SEQ // BLOCK
NEG = -1e30


def _split_dot(x, tri):
    hi = x.astype(BF16)
    lo = (x - hi.astype(F32)).astype(BF16)
    return _dot(hi, tri) + _dot(lo, tri)


def _blk(i):
    return pl.ds(pl.multiple_of(i * BLOCK, BLOCK), BLOCK)


def _iotas():
    row = lax.broadcasted_iota(jnp.int32, (BLOCK, BLOCK), 0)
    col = lax.broadcasted_iota(jnp.int32, (BLOCK, BLOCK), 1)
    return row, col


def _sb_scores(q, k, mask, later, csum, want_sigmoid=False):
    z = _dot(q, k, _NT)
    lk = -_softplus(z)
    if mask is not None:
        lk = jnp.where(mask, lk, 0.0)
    lka = _split_dot(lk, later) + csum
    att = jnp.exp(z + lk + lka)
    sg = jnp.exp(z + lk) if want_sigmoid else None
    if mask is not None:
        att = jnp.where(mask, att, 0.0)
        sg = jnp.where(mask, sg, 0.0) if want_sigmoid else None
    return att, sg, lk


def _rowsum(v):
    return jnp.sum(v, axis=1, keepdims=True)


HEADS = tuple(range(N_HEADS))


def _qkv_specs(first_col):
    return [pl.BlockSpec((SEQ, GROUP_W), lambda b, c=first_col + j: (b, c)) for j in range(3)]


LANES = 128
CUM_BLK = 256


def col_to_row(c):
    b = c.shape[0] // SEQ
    return c.reshape(b, SEQ, LANES)[:, :, :8].transpose(0, 2, 1).reshape(b * 8, SEQ)


def row_to_col(r):
    b = r.shape[0] // 8
    c = r.reshape(b, 8, SEQ).transpose(0, 2, 1)
    return jnp.pad(c, ((0, 0), (0, 0), (0, LANES - 8))).reshape(b * SEQ, LANES)


def fox_prep(aux, bf, *, name):
    T = aux.shape[0]

    def body(f_ref, b_ref, o_ref):
        row = lax.broadcasted_iota(jnp.int32, (CUM_BLK, CUM_BLK), 0)
        col = lax.broadcasted_iota(jnp.int32, (CUM_BLK, CUM_BLK), 1)
        upto = (col <= row).astype(BF16)
        carry = jnp.zeros((1, LANES), F32)
        for n in range(SEQ // CUM_BLK):
            rows = slice(n * CUM_BLK, (n + 1) * CUM_BLK)
            logf = -_softplus(-(f_ref[rows, :] + b_ref[...]))
            hi = logf.astype(BF16)
            lo = (logf - hi.astype(F32)).astype(BF16)
            cum = _dot(upto, hi) + _dot(upto, lo) + carry
            o_ref[rows, :] = cum
            carry = cum[CUM_BLK - 1:CUM_BLK]

    return pl.pallas_call(
        body, name=name, grid=(T // SEQ,),
        in_specs=[pl.BlockSpec((SEQ, LANES), lambda b: (b, 4)), pl.BlockSpec((1, LANES), lambda b: (0, 0))],
        out_specs=pl.BlockSpec((SEQ, LANES), lambda b: (b, 0)), out_shape=jax.ShapeDtypeStruct((T, LANES), F32),
        compiler_params=_params("parallel"),
    )(aux, bf)


def fox_prep_bwd(aux, bf, dcum, *, name):
    T = aux.shape[0]

    def body(f_ref, b_ref, d_ref, df_ref, db_ref):
        row = lax.broadcasted_iota(jnp.int32, (CUM_BLK, CUM_BLK), 0)
        col = lax.broadcasted_iota(jnp.int32, (CUM_BLK, CUM_BLK), 1)
        onward = (col >= row).astype(BF16)
        carry = jnp.zeros((1, LANES), F32)
        tot = jnp.zeros((1, LANES), F32)
        for n in range(SEQ // CUM_BLK - 1, -1, -1):
            rows = slice(n * CUM_BLK, (n + 1) * CUM_BLK)
            d = d_ref[rows, :]
            hi = d.astype(BF16)
            lo = (d - hi.astype(F32)).astype(BF16)
            dlogf = _dot(onward, hi) + _dot(onward, lo) + carry
            carry = dlogf[0:1]
            df = dlogf * jax.nn.sigmoid(-(f_ref[rows, :] + b_ref[...]))
            df_ref[rows, :] = df
            tot = tot + jnp.sum(df, axis=0, keepdims=True)

        @pl.when(pl.program_id(0) == 0)
        def _():
            db_ref[...] = tot

        @pl.when(pl.program_id(0) > 0)
        def _():
            db_ref[...] += tot

    blk = pl.BlockSpec((SEQ, LANES), lambda b: (b, 0))
    vec = pl.BlockSpec((1, LANES), lambda b: (0, 0))
    sd = jax.ShapeDtypeStruct
    return pl.pallas_call(
        body, name=name, grid=(T // SEQ,),
        in_specs=[pl.BlockSpec((SEQ, LANES), lambda b: (b, 4)), vec, blk],
        out_specs=[blk, vec], out_shape=[sd((T, LANES), F32), sd((1, LANES), F32)],
        compiler_params=_params("arbitrary"),
    )(aux, bf, dcum)


def _fox_logits(q, k, cq, ck, mask):
    z = _dot(q, k, _NT) + cq - ck
    return z if mask is None else jnp.where(mask, z, NEG)


def fox_fwd(qkv, cumc, cumr, *, name):
    T = qkv.shape[0]

    def body(q_ref, k_ref, v_ref, cc_ref, cr_ref, o_ref, lse_ref, z_s):
        row, col = _iotas()
        causal = col <= row
        lse_ref[...] = jnp.zeros_like(lse_ref)

        def qblock(i, _):
            qs = [q_ref[_blk(i), _hs(h)] * Q_SCALE for h in HEADS]
            cqs = [cc_ref[_blk(i), h:h + 1] for h in HEADS]

            def logits(j, mask, ms):
                out = []
                for h in HEADS:
                    z = _fox_logits(qs[h], k_ref[_blk(j), _hs(h)], cqs[h], cr_ref[h:h + 1, _blk(j)], mask)
                    z_s[h, j] = z
                    out.append(jnp.maximum(ms[h], jnp.max(z, axis=1, keepdims=True)))
                return tuple(out)

            ms = logits(i, causal, (jnp.full((BLOCK, 1), NEG, F32),) * N_HEADS)
            ms = lax.fori_loop(0, i, lambda j, c: logits(j, None, c), ms)

            def values(j, carry):
                out = []
                for h in HEADS:
                    acc, l = carry[h]
                    p = jnp.exp(z_s[h, j] - ms[h])
                    out.append((acc + _dot(p.astype(BF16), v_ref[_blk(j), _hs(h)]), l + _rowsum(p)))
                return tuple(out)

            zero = (jnp.zeros((BLOCK, HEAD_DIM), F32), jnp.zeros((BLOCK, 1), F32))
            res = lax.fori_loop(0, i + 1, values, (zero,) * N_HEADS)
            for h in HEADS:
                acc, l = res[h]
                o_ref[_blk(i), _hs(h)] = (acc / l).astype(BF16)
                lse_ref[_blk(i), h:h + 1] = ms[h] + jnp.log(l)
            return 0

        lax.fori_loop(0, NB, qblock, 0)

    out = pl.BlockSpec((SEQ, GROUP_W), lambda b: (b, 0))
    colb = pl.BlockSpec((SEQ, LANES), lambda b: (b, 0))
    sd = jax.ShapeDtypeStruct
    return pl.pallas_call(
        body, name=name, grid=(T // SEQ,),
        in_specs=_qkv_specs(3) + [colb, pl.BlockSpec((8, SEQ), lambda b: (b, 0))],
        out_specs=[out, colb], out_shape=[sd((T, GROUP_W), BF16), sd((T, LANES), F32)],
        scratch_shapes=[pltpu.VMEM((N_HEADS, NB, BLOCK, BLOCK), F32)],
        compiler_params=_params("parallel"),
    )(qkv, qkv, qkv, cumc, cumr)


def fox_bwd(qkv, cumc, cumr, lse, dmixed, *, name):
    T = qkv.shape[0]

    def body(q_ref, k_ref, v_ref, cc_ref, cr_ref, lse_ref, do_ref, dq_ref, dk_ref, dv_ref, dcc_ref, dcr_ref, p_s, dp_s):
        row, col = _iotas()
        causal = col <= row
        dk_ref[...] = jnp.zeros_like(dk_ref)
        dv_ref[...] = jnp.zeros_like(dv_ref)
        dcc_ref[...] = jnp.zeros_like(dcc_ref)
        dcr_ref[...] = jnp.zeros_like(dcr_ref)

        def qblock(i, _):
            qs = [q_ref[_blk(i), _hs(h)] * Q_SCALE for h in HEADS]
            dos = [do_ref[_blk(i), _hs(h)].astype(BF16) for h in HEADS]
            cqs = [cc_ref[_blk(i), h:h + 1] for h in HEADS]
            lses = [lse_ref[_blk(i), h:h + 1] for h in HEADS]

            def probs(j, mask, deltas):
                out = []
                for h in HEADS:
                    z = _fox_logits(qs[h], k_ref[_blk(j), _hs(h)], cqs[h], cr_ref[h:h + 1, _blk(j)], mask)
                    p = jnp.exp(z - lses[h])
                    dp = _dot(dos[h], v_ref[_blk(j), _hs(h)], _NT)
                    p_s[h, j] = p
                    dp_s[h, j] = dp
                    out.append(deltas[h] + _rowsum(p * dp))
                return tuple(out)

            deltas = probs(i, causal, (jnp.zeros((BLOCK, 1), F32),) * N_HEADS)
            deltas = lax.fori_loop(0, i, lambda j, c: probs(j, None, c), deltas)

            def kblock(j, carry):
                out = []
                for h in HEADS:
                    dq, dcq = carry[h]
                    p = p_s[h, j]
                    ds = p * (dp_s[h, j] - deltas[h])
                    dsb = ds.astype(BF16)
                    dk_ref[_blk(j), _hs(h)] += _dot(dsb, qs[h], _TN)
                    dv_ref[_blk(j), _hs(h)] += _dot(p.astype(BF16), dos[h], _TN)
                    dcr_ref[h:h + 1, _blk(j)] -= jnp.sum(ds, axis=0, keepdims=True)
                    out.append((dq + _dot(dsb, k_ref[_blk(j), _hs(h)]), dcq + _rowsum(ds)))
                return tuple(out)

            zero = (jnp.zeros((BLOCK, HEAD_DIM), F32), jnp.zeros((BLOCK, 1), F32))
            res = lax.fori_loop(0, i + 1, kblock, (zero,) * N_HEADS)
            for h in HEADS:
                dq_ref[_blk(i), _hs(h)] = res[h][0] * Q_SCALE
                dcc_ref[_blk(i), h:h + 1] = res[h][1]
            return 0

        lax.fori_loop(0, NB, qblock, 0)

    out = pl.BlockSpec((SEQ, GROUP_W), lambda b: (b, 0))
    colb = pl.BlockSpec((SEQ, LANES), lambda b: (b, 0))
    rowb = pl.BlockSpec((8, SEQ), lambda b: (b, 0))
    sd = jax.ShapeDtypeStruct
    big = sd((T, GROUP_W), F32)
    return pl.pallas_call(
        body, name=name, grid=(T // SEQ,),
        in_specs=_qkv_specs(3) + [colb, rowb, colb, pl.BlockSpec((SEQ, GROUP_W), lambda b: (b, 1))],
        out_specs=[out, out, out, colb, rowb],
        out_shape=[big, big, big, sd((T, LANES), F32), sd((T // SEQ * 8, SEQ), F32)],
        scratch_shapes=[pltpu.VMEM((N_HEADS, NB, BLOCK, BLOCK), F32), pltpu.VMEM((N_HEADS, NB, BLOCK, BLOCK), F32)],
        compiler_params=_params("parallel"),
    )(qkv, qkv, qkv, cumc, cumr, lse, dmixed)


CHUNK = 256
WIDE = N_HEADS * CHUNK
NCH = SEQ // CHUNK


def _seg(h):
    return slice(h * CHUNK, (h + 1) * CHUNK)


def _chunk_rows(c):
    return pl.ds(pl.multiple_of(c * CHUNK, CHUNK), CHUNK)


def _wide_consts():
    r = lax.broadcasted_iota(jnp.int32, (WIDE, GROUP_W), 0)
    f = lax.broadcasted_iota(jnp.int32, (WIDE, GROUP_W), 1)
    bd = (r // CHUNK) == (f // HEAD_DIM)
    row = lax.broadcasted_iota(jnp.int32, (BLOCK, WIDE), 0)
    key = lax.broadcasted_iota(jnp.int32, (BLOCK, WIDE), 1) % CHUNK
    return bd, row, key


def _block_diag(x, bd):
    return jnp.where(bd, jnp.concatenate([x] * N_HEADS, axis=0), jnp.zeros((), x.dtype))


def _fold_heads(w, bd):
    w = jnp.where(bd, w, 0.0)
    return (w[0:CHUNK] + w[CHUNK:2 * CHUNK]) + (w[2 * CHUNK:3 * CHUNK] + w[3 * CHUNK:])


def _widen(cols):
    return jnp.concatenate([jnp.broadcast_to(c, (BLOCK, CHUNK)) for c in cols], axis=1)


def _head_rowsums(w):
    return [jnp.sum(w[:, _seg(h)], axis=1, keepdims=True) for h in HEADS]


def _tri_wide(x, tri):
    hi = x.astype(BF16)
    lo = (x - hi.astype(F32)).astype(BF16)
    y = _dot(jnp.concatenate([hi[:, _seg(h)] for h in HEADS] + [lo[:, _seg(h)] for h in HEADS], axis=0), tri)
    return jnp.concatenate([y[h * BLOCK:(h + 1) * BLOCK] + y[(N_HEADS + h) * BLOCK:(N_HEADS + h + 1) * BLOCK] for h in HEADS], axis=1)


def _feature_widen(cols):
    return jnp.concatenate([jnp.broadcast_to(c, (BLOCK, HEAD_DIM)) for c in cols], axis=1)


def _sbw_tile(q, kbd, mask, later, csum):
    z = _dot(q, kbd, _NT)
    lk = -_softplus(z)
    if mask is not None:
        lk = jnp.where(mask, lk, 0.0)
    e = z + lk
    att = jnp.exp(e + _tri_wide(lk, later) + csum)
    if mask is not None:
        att = jnp.where(mask, att, 0.0)
    return att, e, lk


def sbw_fwd(qkv, *, name):
    T = qkv.shape[0]

    def body(q_ref, k_ref, v_ref, o_ref):
        bd, row, key = _wide_consts()
        r2 = lax.broadcasted_iota(jnp.int32, (CHUNK, CHUNK), 0)
        c2 = lax.broadcasted_iota(jnp.int32, (CHUNK, CHUNK), 1)
        later = (r2 > c2).astype(BF16)

        def qblock(i, _):
            q = q_ref[_blk(i), :] * Q_SCALE
            cd = i // 2
            strict = key < row + BLOCK * (i % 2)

            def tile(c, mask, carry):
                acc, csum = carry
                att, _, lk = _sbw_tile(q, _block_diag(k_ref[_chunk_rows(c), :], bd), mask, later, csum)
                acc = acc + _dot(att.astype(BF16), _block_diag(v_ref[_chunk_rows(c), :], bd))
                return acc, csum + _widen(_head_rowsums(lk))

            carry = tile(cd, strict, (jnp.zeros((BLOCK, GROUP_W), F32), jnp.zeros((BLOCK, WIDE), F32)))
            acc, _ = lax.fori_loop(0, cd, lambda n, cr: tile(cd - 1 - n, None, cr), carry)
            o_ref[_blk(i), :] = acc.astype(BF16)
            return 0

        lax.fori_loop(0, NB, qblock, 0)

    return pl.pallas_call(
        body, name=name, grid=(T // SEQ,), in_specs=_qkv_specs(0),
        out_specs=pl.BlockSpec((SEQ, GROUP_W), lambda b: (b, 0)), out_shape=jax.ShapeDtypeStruct((T, GROUP_W), BF16),
        compiler_params=_params("parallel"),
    )(qkv, qkv, qkv)


def sbw_bwd(qkv, dmixed, *, name):
    T = qkv.shape[0]

    def body(q_ref, k_ref, v_ref, do_ref, dq_ref, dk_ref, dv_ref, att_s, sg_s):
        bd, row, key = _wide_consts()
        r2 = lax.broadcasted_iota(jnp.int32, (CHUNK, CHUNK), 0)
        c2 = lax.broadcasted_iota(jnp.int32, (CHUNK, CHUNK), 1)
        later = (r2 > c2).astype(BF16)
        earlier = (r2 < c2).astype(BF16)
        dk_ref[...] = jnp.zeros_like(dk_ref)
        dv_ref[...] = jnp.zeros_like(dv_ref)

        def qblock(i, _):
            q = q_ref[_blk(i), :] * Q_SCALE
            do = do_ref[_blk(i), :].astype(BF16)
            cd = i // 2
            strict = key < row + BLOCK * (i % 2)

            def recompute(c, mask, csum):
                att, e, lk = _sbw_tile(q, _block_diag(k_ref[_chunk_rows(c), :], bd), mask, later, csum)
                sg = jnp.exp(e)
                att_s[c] = att
                sg_s[c] = sg if mask is None else jnp.where(mask, sg, 0.0)
                return csum + _widen(_head_rowsums(lk))

            csum = recompute(cd, strict, jnp.zeros((BLOCK, WIDE), F32))
            lax.fori_loop(0, cd, lambda n, cs: recompute(cd - 1 - n, None, cs), csum)

            def tile(c, carry):
                dq, pre = carry
                kbd = _block_diag(k_ref[_chunk_rows(c), :], bd)
                vbd = _block_diag(v_ref[_chunk_rows(c), :], bd)
                att = att_s[c]
                ds = _dot(do, vbd, _NT) * att
                dlk = ds + _tri_wide(ds, earlier) + pre
                dz = (ds - dlk * sg_s[c]).astype(BF16)
                dk_ref[_chunk_rows(c), :] += _fold_heads(_dot(dz, q, _TN), bd)
                dv_ref[_chunk_rows(c), :] += _fold_heads(_dot(att.astype(BF16), do, _TN), bd)
                return dq + _dot(dz, kbd), pre + _widen(_head_rowsums(ds))

            dq, _ = lax.fori_loop(0, cd + 1, tile, (jnp.zeros((BLOCK, GROUP_W), F32), jnp.zeros((BLOCK, WIDE), F32)))
            dq_ref[_blk(i), :] = dq * Q_SCALE
            return 0

        lax.fori_loop(0, NB, qblock, 0)

    out = pl.BlockSpec((SEQ, GROUP_W), lambda b: (b, 0))
    sd = jax.ShapeDtypeStruct((T, GROUP_W), F32)
    return pl.pallas_call(
        body, name=name, grid=(T // SEQ,), in_specs=_qkv_specs(0) + [out],
        out_specs=[out] * 3, out_shape=[sd] * 3,
        scratch_shapes=[pltpu.VMEM((NCH, BLOCK, WIDE), F32), pltpu.VMEM((NCH, BLOCK, WIDE), F32)],
        compiler_params=_params("parallel"),
    )(qkv, qkv, qkv, dmixed)


def _foxw_logits(q, kbd, cq, cr_ref, c, mask):
    ck = jnp.concatenate([cr_ref[h:h + 1, _chunk_rows(c)] for h in HEADS], axis=1)
    z = _dot(q, kbd, _NT) + cq - ck
    return z if mask is None else jnp.where(mask, z, NEG)


def foxw_fwd(qkv, cumc, cumr, *, name):
    T = qkv.shape[0]

    def body(q_ref, k_ref, v_ref, cc_ref, cr_ref, o_ref, o32_ref, lse_ref, z_s):
        bd, row, key = _wide_consts()
        lse_ref[...] = jnp.zeros_like(lse_ref)

        def qblock(i, _):
            q = q_ref[_blk(i), :] * Q_SCALE
            cq = _widen([cc_ref[_blk(i), h:h + 1] for h in HEADS])
            cd = i // 2
            causal = key <= row + BLOCK * (i % 2)

            def logits(c, mask, ms):
                z = _foxw_logits(q, _block_diag(k_ref[_chunk_rows(c), :], bd), cq, cr_ref, c, mask)
                z_s[c] = z
                return tuple(jnp.maximum(ms[h], jnp.max(z[:, _seg(h)], axis=1, keepdims=True)) for h in HEADS)

            ms = logits(cd, causal, (jnp.full((BLOCK, 1), NEG, F32),) * N_HEADS)
            ms = lax.fori_loop(0, cd, lambda c, m: logits(c, None, m), ms)
            m_wide = _widen(ms)

            def values(c, carry):
                acc, l = carry
                p = jnp.exp(z_s[c] - m_wide)
                return acc + _dot(p.astype(BF16), _block_diag(v_ref[_chunk_rows(c), :], bd)), l + _widen(_head_rowsums(p))

            acc, l = lax.fori_loop(0, cd + 1, values, (jnp.zeros((BLOCK, GROUP_W), F32), jnp.zeros((BLOCK, WIDE), F32)))
            ls = [l[:, h * CHUNK:h * CHUNK + 1] for h in HEADS]
            o = acc / _feature_widen(ls)
            o_ref[_blk(i), :] = o.astype(BF16)
            o32_ref[_blk(i), :] = o
            for h in HEADS:
                lse_ref[_blk(i), h:h + 1] = ms[h] + jnp.log(ls[h])
            return 0

        lax.fori_loop(0, NB, qblock, 0)

    out = pl.BlockSpec((SEQ, GROUP_W), lambda b: (b, 0))
    colb = pl.BlockSpec((SEQ, LANES), lambda b: (b, 0))
    sd = jax.ShapeDtypeStruct
    return pl.pallas_call(
        body, name=name, grid=(T // SEQ,),
        in_specs=_qkv_specs(3) + [colb, pl.BlockSpec((8, SEQ), lambda b: (b, 0))],
        out_specs=[out, out, colb], out_shape=[sd((T, GROUP_W), BF16), sd((T, GROUP_W), F32), sd((T, LANES), F32)],
        scratch_shapes=[pltpu.VMEM((NCH, BLOCK, WIDE), F32)],
        compiler_params=_params("parallel"),
    )(qkv, qkv, qkv, cumc, cumr)


def foxw_bwd(qkv, cumc, cumr, lse, o32, dmixed, *, name):
    T = qkv.shape[0]

    def body(q_ref, k_ref, v_ref, cc_ref, cr_ref, lse_ref, o_ref, do_ref, dq_ref, dk_ref, dv_ref, dcc_ref, dcr_ref):
        bd, row, key = _wide_consts()
        dk_ref[...] = jnp.zeros_like(dk_ref)
        dv_ref[...] = jnp.zeros_like(dv_ref)
        dcc_ref[...] = jnp.zeros_like(dcc_ref)
        dcr_ref[...] = jnp.zeros_like(dcr_ref)

        def qblock(i, _):
            q = q_ref[_blk(i), :] * Q_SCALE
            do32 = do_ref[_blk(i), :]
            do = do32.astype(BF16)
            prod = do32 * o_ref[_blk(i), :]
            delta = _widen([jnp.sum(prod[:, _hs(h)], axis=1, keepdims=True) for h in HEADS])
            cq = _widen([cc_ref[_blk(i), h:h + 1] for h in HEADS])
            lse_w = _widen([lse_ref[_blk(i), h:h + 1] for h in HEADS])
            cd = i // 2
            causal = key <= row + BLOCK * (i % 2)

            def tile(c, mask, carry):
                dq, dcq = carry
                kbd = _block_diag(k_ref[_chunk_rows(c), :], bd)
                vbd = _block_diag(v_ref[_chunk_rows(c), :], bd)
                p = jnp.exp(_foxw_logits(q, kbd, cq, cr_ref, c, mask) - lse_w)
                ds = p * (_dot(do, vbd, _NT) - delta)
                dsb = ds.astype(BF16)
                dk_ref[_chunk_rows(c), :] += _fold_heads(_dot(dsb, q, _TN), bd)
                dv_ref[_chunk_rows(c), :] += _fold_heads(_dot(p.astype(BF16), do, _TN), bd)
                for h in HEADS:
                    dcr_ref[h:h + 1, _chunk_rows(c)] -= jnp.sum(ds[:, _seg(h)], axis=0, keepdims=True)
                return dq + _dot(dsb, kbd), dcq + _widen(_head_rowsums(ds))

            carry = tile(cd, causal, (jnp.zeros((BLOCK, GROUP_W), F32), jnp.zeros((BLOCK, WIDE), F32)))
            dq, dcq = lax.fori_loop(0, cd, lambda c, cr: tile(c, None, cr), carry)
            dq_ref[_blk(i), :] = dq * Q_SCALE
            for h in HEADS:
                dcc_ref[_blk(i), h:h + 1] = dcq[:, h * CHUNK:h * CHUNK + 1]
            return 0

        lax.fori_loop(0, NB, qblock, 0)

    out = pl.BlockSpec((SEQ, GROUP_W), lambda b: (b, 0))
    colb = pl.BlockSpec((SEQ, LANES), lambda b: (b, 0))
    rowb = pl.BlockSpec((8, SEQ), lambda b: (b, 0))
    sd = jax.ShapeDtypeStruct
    big = sd((T, GROUP_W), F32)
    return pl.pallas_call(
        body, name=name, grid=(T // SEQ,),
        in_specs=_qkv_specs(3) + [colb, rowb, colb, out, pl.BlockSpec((SEQ, GROUP_W), lambda b: (b, 1))],
        out_specs=[out, out, out, colb, rowb],
        out_shape=[big, big, big, sd((T, LANES), F32), sd((T // SEQ * 8, SEQ), F32)],
        compiler_params=_params("parallel"),
    )(qkv, qkv, qkv, cumc, cumr, lse, o32, dmixed)


BAND = 2 * BLOCK


def _t5_bucket_np(dist):
    n = np.maximum(dist, 0)
    max_exact = NUM_BUCKETS // 2
    nf = np.maximum(n, 1).astype(np.float32)
    large = max_exact + (np.log(nf / np.float32(max_exact)) / np.float32(math.log(MAX_DISTANCE / max_exact))
                         * np.float32(NUM_BUCKETS - max_exact)).astype(np.int32)
    large = np.minimum(large, NUM_BUCKETS - 1)
    return np.where(n < max_exact, n, large).astype(np.int32)


def _band_buckets():
    qi = np.arange(BLOCK)[:, None]
    ki = np.arange(BAND)[None, :]
    delta = np.clip(qi - ki + BLOCK, 0, BLOCK)
    return np.stack([_t5_bucket_np(delta * d) for d in DILATIONS])


def to_classes(a, d):
    if d == 1:
        return a
    T, C = a.shape
    return a.reshape(T // SEQ, SEQ // d, d, C).transpose(0, 2, 1, 3).reshape(T, C)


def from_classes(a, d):
    if d == 1:
        return a
    T, C = a.shape
    return a.reshape(T // SEQ, d, SEQ // d, C).transpose(0, 2, 1, 3).reshape(T, C)


def relbias_expand(rel, *, name):
    buckets = jnp.asarray(_band_buckets())
    n_pat = len(DILATIONS)

    def body(rel_ref, bk_ref, o_ref):
        for p in range(n_pat):
            bk = bk_ref[p]
            for h in range(N_HEADS):
                acc = jnp.zeros((BLOCK, BAND), F32)
                for b in range(NUM_BUCKETS):
                    acc = jnp.where(bk == b, rel_ref[b, h], acc)
                o_ref[p * N_HEADS + h] = acc

    return pl.pallas_call(
        body, name=name,
        in_specs=[pl.BlockSpec(memory_space=pltpu.SMEM), pl.BlockSpec(memory_space=pltpu.VMEM)],
        out_specs=pl.BlockSpec(memory_space=pltpu.VMEM),
        out_shape=jax.ShapeDtypeStruct((n_pat * N_HEADS, BLOCK, BAND), F32),
        compiler_params=_params(),
    )(rel, buckets)


def relbias_reduce(ds_all, *, name):
    buckets = jnp.asarray(_band_buckets())
    n_pat = len(DILATIONS)

    def body(ds_ref, bk_ref, o_ref):
        for b in range(NUM_BUCKETS):
            for h in range(N_HEADS):
                tot = jnp.float32(0.0)
                for p in range(n_pat):
                    tot = tot + jnp.sum(jnp.where(bk_ref[p] == b, ds_ref[p * N_HEADS + h], 0.0))
                o_ref[b, h] = tot

    return pl.pallas_call(
        body, name=name,
        in_specs=[pl.BlockSpec(memory_space=pltpu.VMEM), pl.BlockSpec(memory_space=pltpu.VMEM)],
        out_specs=pl.BlockSpec(memory_space=pltpu.SMEM),
        out_shape=jax.ShapeDtypeStruct((NUM_BUCKETS, N_HEADS), F32),
        compiler_params=_params(),
    )(ds_all, buckets)


def _band_valid_wide(first, row, key):
    inside = jnp.logical_and(key >= row, key <= row + BLOCK)
    return jnp.logical_and(inside, jnp.logical_or(jnp.logical_not(first), key >= BLOCK))


def _band_specs(pattern):
    cur = lambda c: pl.BlockSpec((BLOCK, GROUP_W), lambda g: (g, c))
    prev = lambda c: pl.BlockSpec((BLOCK, GROUP_W), lambda g: (jnp.maximum(g - 1, 0), c))
    bias = pl.BlockSpec((N_HEADS, BLOCK, BAND), lambda g: (pattern, 0, 0))
    return [cur(0), prev(1), cur(1), prev(2), cur(2), bias]


def band_fwd(qd, bias, pattern, *, name):
    T = qd.shape[0]
    seq_blocks = SEQ // DILATIONS[pattern] // BLOCK

    def body(q_ref, kp_ref, kc_ref, vp_ref, vc_ref, b_ref, o_ref, lse_ref):
        bd, row, key = _wide_consts()
        valid = _band_valid_wide(pl.program_id(0) % seq_blocks == 0, row, key)
        q = q_ref[...] * Q_SCALE
        kbd = _block_diag(jnp.concatenate([kp_ref[...], kc_ref[...]], axis=0), bd)
        vbd = _block_diag(jnp.concatenate([vp_ref[...], vc_ref[...]], axis=0), bd)
        bias = jnp.concatenate([b_ref[h] for h in HEADS], axis=1)
        sc = jnp.where(valid, _dot(q, kbd, _NT) + bias, NEG)
        ms = [jnp.max(sc[:, _seg(h)], axis=1, keepdims=True) for h in HEADS]
        p = jnp.exp(sc - _widen(ms))
        ls = _head_rowsums(p)
        o_ref[...] = _dot(p.astype(BF16), vbd) / _feature_widen(ls)
        lse_ref[...] = jnp.zeros_like(lse_ref)
        for h in HEADS:
            lse_ref[:, h:h + 1] = ms[h] + jnp.log(ls[h])

    sd = jax.ShapeDtypeStruct
    return pl.pallas_call(
        body, name=name, grid=(T // BLOCK,), in_specs=_band_specs(pattern),
        out_specs=[pl.BlockSpec((BLOCK, GROUP_W), lambda g: (g, 0)), pl.BlockSpec((BLOCK, LANES), lambda g: (g, 0))],
        out_shape=[sd((T, GROUP_W), F32), sd((T, LANES), F32)],
        compiler_params=_params("parallel"),
    )(qd, qd, qd, qd, qd, bias)


def band_bwd(qd, bias, lse, do, dlse, pattern, *, name):
    T = qd.shape[0]
    seq_blocks = SEQ // DILATIONS[pattern] // BLOCK

    def body(q_ref, kp_ref, kc_ref, vp_ref, vc_ref, b_ref, lse_ref, do_ref, dlse_ref,
             dq_ref, dkc_ref, dkp_ref, dvc_ref, dvp_ref, ds_ref):
        g = pl.program_id(0)
        bd, row, key = _wide_consts()
        valid = _band_valid_wide(g % seq_blocks == 0, row, key)
        q = q_ref[...] * Q_SCALE
        do = do_ref[...].astype(BF16)
        kbd = _block_diag(jnp.concatenate([kp_ref[...], kc_ref[...]], axis=0), bd)
        vbd = _block_diag(jnp.concatenate([vp_ref[...], vc_ref[...]], axis=0), bd)
        bias = jnp.concatenate([b_ref[h] for h in HEADS], axis=1)
        lse_w = _widen([lse_ref[:, h:h + 1] for h in HEADS])
        dlse_w = _widen([dlse_ref[:, h:h + 1] for h in HEADS])
        p = jnp.where(valid, jnp.exp(_dot(q, kbd, _NT) + bias - lse_w), 0.0)
        dp = _dot(do, vbd, _NT)
        ds = p * (dp - _widen(_head_rowsums(p * dp)) + dlse_w)
        dsb, pb = ds.astype(BF16), p.astype(BF16)
        dq_ref[...] = _dot(dsb, kbd) * Q_SCALE
        dk = _fold_heads(_dot(dsb, q, _TN), bd)
        dv = _fold_heads(_dot(pb, do, _TN), bd)
        dkp_ref[...] = dk[:BLOCK]
        dkc_ref[...] = dk[BLOCK:]
        dvp_ref[...] = dv[:BLOCK]
        dvc_ref[...] = dv[BLOCK:]

        @pl.when(g == 0)
        def _():
            for h in HEADS:
                ds_ref[h] = ds[:, _seg(h)]

        @pl.when(g > 0)
        def _():
            for h in HEADS:
                ds_ref[h] += ds[:, _seg(h)]

    big = pl.BlockSpec((BLOCK, GROUP_W), lambda g: (g, 0))
    colb = pl.BlockSpec((BLOCK, LANES), lambda g: (g, 0))
    sd = jax.ShapeDtypeStruct
    return pl.pallas_call(
        body, name=name, grid=(T // BLOCK,), in_specs=_band_specs(pattern) + [colb, big, colb],
        out_specs=[big] * 5 + [pl.BlockSpec((N_HEADS, BLOCK, BAND), lambda g: (0, 0, 0))],
        out_shape=[sd((T, GROUP_W), F32)] * 5 + [sd((N_HEADS, BLOCK, BAND), F32)],
        compiler_params=_params("arbitrary"),
    )(qd, qd, qd, qd, qd, bias, lse, do, dlse)


def shift_add(cur, prev, *, name):
    T = cur.shape[0]
    nb = T // BLOCK

    def body(c_ref, p_ref, o_ref):
        keep = (pl.program_id(0) < nb - 1).astype(F32)
        o_ref[...] = c_ref[...] + keep * p_ref[...]

    return pl.pallas_call(
        body, name=name, grid=(nb,),
        in_specs=[pl.BlockSpec((BLOCK, GROUP_W), lambda g: (g, 0)), pl.BlockSpec((BLOCK, GROUP_W), lambda g: (jnp.minimum(g + 1, nb - 1), 0))],
        out_specs=pl.BlockSpec((BLOCK, GROUP_W), lambda g: (g, 0)), out_shape=jax.ShapeDtypeStruct((T, GROUP_W), F32),
        compiler_params=_params("parallel"),
    )(cur, prev)


def _pattern_weights(lse_refs, h):
    ls = [r[:, h:h + 1] for r in lse_refs]
    mx = functools.reduce(jnp.maximum, ls)
    es = [jnp.exp(l - mx) for l in ls]
    tot = functools.reduce(lambda a, b: a + b, es)
    return [e / tot for e in es]


def dil_combine_fwd(outs, *, name):
    T = outs[0][0].shape[0]
    n = len(outs)
    tm = 512

    def body(*refs):
        o_refs, l_refs, out_ref = refs[:n], refs[n:2 * n], refs[2 * n]
        for h in range(N_HEADS):
            w = _pattern_weights(l_refs, h)
            acc = w[0] * o_refs[0][:, _hs(h)]
            for p in range(1, n):
                acc = acc + w[p] * o_refs[p][:, _hs(h)]
            out_ref[:, _hs(h)] = acc.astype(BF16)

    big = pl.BlockSpec((tm, GROUP_W), lambda i: (i, 0))
    colb = pl.BlockSpec((tm, LANES), lambda i: (i, 0))
    return pl.pallas_call(
        body, name=name, grid=(T // tm,), in_specs=[big] * n + [colb] * n,
        out_specs=big, out_shape=jax.ShapeDtypeStruct((T, GROUP_W), BF16),
        compiler_params=_params("parallel"),
    )(*[o for o, _ in outs], *[l for _, l in outs])


def dil_combine_bwd(outs, dmixed, *, name):
    T = outs[0][0].shape[0]
    n = len(outs)
    tm = 512

    def body(*refs):
        o_refs, l_refs, do_ref = refs[:n], refs[n:2 * n], refs[2 * n]
        do_refs, dl_refs = refs[2 * n + 1:3 * n + 1], refs[3 * n + 1:]
        for r in dl_refs:
            r[...] = jnp.zeros_like(r)
        for h in range(N_HEADS):
            w = _pattern_weights(l_refs, h)
            do = do_ref[:, _hs(h)]
            dw = [jnp.sum(do * o_refs[p][:, _hs(h)], axis=1, keepdims=True) for p in range(n)]
            mean = functools.reduce(lambda a, b: a + b, [w[p] * dw[p] for p in range(n)])
            for p in range(n):
                do_refs[p][:, _hs(h)] = w[p] * do
                dl_refs[p][:, h:h + 1] = w[p] * (dw[p] - mean)

    big = pl.BlockSpec((tm, GROUP_W), lambda i: (i, 0))
    colb = pl.BlockSpec((tm, LANES), lambda i: (i, 0))
    sd = jax.ShapeDtypeStruct
    res = pl.pallas_call(
        body, name=name, grid=(T // tm,),
        in_specs=[big] * n + [colb] * n + [pl.BlockSpec((tm, GROUP_W), lambda i: (i, 2))],
        out_specs=[big] * n + [colb] * n, out_shape=[sd((T, GROUP_W), F32)] * n + [sd((T, LANES), F32)] * n,
        compiler_params=_params("parallel"),
    )(*[o for o, _ in outs], *[l for _, l in outs], dmixed)
    return list(zip(res[:n], res[n:]))


def dilated_fwd(qkv, bias, tag):
    qd = qkv[:, 6 * GROUP_W:]
    outs = []
    for p, d in enumerate(DILATIONS):
        o, l = band_fwd(to_classes(qd, d), bias, p, name=f"{tag}_band_fwd{p}")
        outs.append((from_classes(o, d), from_classes(l, d)))
    return outs


def dilated_bwd(qkv, bias, outs, dmixed, tag):
    qd = qkv[:, 6 * GROUP_W:]
    grads = dil_combine_bwd(outs, dmixed, name=f"{tag}_combine_bwd")
    parts, ds_all = [], []
    for p, d in enumerate(DILATIONS):
        (_, lse), (do, dlse) = outs[p], grads[p]
        dq, dkc, dkp, dvc, dvp, ds = band_bwd(to_classes(qd, d), bias, to_classes(lse, d), to_classes(do, d),
                                              to_classes(dlse, d), p, name=f"{tag}_band_bwd{p}")
        dk = shift_add(dkc, dkp, name=f"{tag}_dk{p}")
        dv = shift_add(dvc, dvp, name=f"{tag}_dv{p}")
        parts.append(from_classes(jnp.concatenate([dq, dk, dv], axis=1), d))
        ds_all.append(ds)
    return sum_cast(parts, BF16, name=f"{tag}_dqkv_sum"), jnp.concatenate(ds_all, axis=0)


def sum_cast(arrs, dtype, *, name):
    R, C = arrs[0].shape
    tr = _largest_tile(R, 512, 16)
    n = len(arrs)

    def body(*refs):
        acc = refs[0][...].astype(F32)
        for r in refs[1:n]:
            acc = acc + r[...].astype(F32)
        refs[n][...] = acc.astype(dtype)

    blk = pl.BlockSpec((tr, C), lambda i: (i, 0))
    return pl.pallas_call(
        body, name=name, grid=(R // tr,), in_specs=[blk] * n, out_specs=blk, out_shape=jax.ShapeDtypeStruct((R, C), dtype),
        compiler_params=_params("parallel"),
    )(*arrs)


GRAD_WIRE = BF16


def _block_diag_halves(w):
    z = jnp.zeros((HEAD_DIM, HEAD_DIM), w.dtype)
    half = lambda a, b: jnp.concatenate([jnp.concatenate([a, z], axis=1), jnp.concatenate([z, b], axis=1)], axis=0)
    return jnp.stack([half(w[0], w[1]), half(w[2], w[3])]).astype(BF16)


def _diag_blocks(d):
    h = HEAD_DIM
    return jnp.stack([d[0, :h, :h], d[0, h:, h:], d[1, :h, :h], d[1, h:, h:]])


def layer_fwd(x, mem2d, W, P, bias, tag):
    s = {}
    s["x"] = x
    h1 = rmsnorm_fwd(x, P["norm_mix_g"], name=f"{tag}_norm_mix")
    qkv = matmul(h1, W["qkv"], out_dtype=BF16, name=f"{tag}_qkv")
    aux = matmul(h1, W["aux"], name=f"{tag}_aux")
    o_sb = sbw_fwd(qkv, name=f"{tag}_sb_fwd")
    cumc = fox_prep(aux, P["bf"], name=f"{tag}_fox_prep")
    cumr = col_to_row(cumc)
    o_fox, o_fox32, lse_fox = foxw_fwd(qkv, cumc, cumr, name=f"{tag}_fox_fwd")
    dil = dilated_fwd(qkv, bias, tag)
    o_dil = dil_combine_fwd(dil, name=f"{tag}_dil_combine")
    o_lru, h_lru = lru_fwd(aux, P["lru_conv_w"], P["lru_conv_b"], P["wa"], P["lru_b_a"], P["wx"], P["lru_b_x"],
                           P["lru_lambda"], name=f"{tag}_lru_fwd")
    mixed = jnp.concatenate([o_sb, o_fox, o_dil, o_lru], axis=1)
    x1 = matmul(mixed, W["out"], residual=x, name=f"{tag}_out")
    hq = rmsnorm_fwd(x1, P["norm_cross_g"], name=f"{tag}_norm_cross")
    qc = matmul(hq, W["cq"], out_dtype=BF16, name=f"{tag}_cq")
    memn = rmsnorm_fwd(mem2d, P["norm_mem_g"], name=f"{tag}_norm_mem")
    kv = matmul(memn, W["ckv"], out_dtype=BF16, name=f"{tag}_ckv")
    oc = cross_fwd(qc, kv, name=f"{tag}_cross_fwd")
    x2 = matmul(oc, W["coT"], trans_b=True, residual=x1, name=f"{tag}_co")
    h2 = rmsnorm_fwd(x2, P["norm_ffn_g"], name=f"{tag}_norm_ffn")
    hu = matmul(h2, W["up_u"], trans_b=True, name=f"{tag}_up_u")
    hg = matmul(h2, W["up_g"], trans_b=True, name=f"{tag}_up_g")
    act = glu_fwd(hu, hg, P["wu"], P["wg"], P["bu"], P["bg"], name=f"{tag}_glu_fwd")
    x3 = matmul(act, W["down"], residual=x2, name=f"{tag}_down")
    s.update(h1=h1, qkv=qkv, aux=aux, cumc=cumc, cumr=cumr, lse_fox=lse_fox, o_fox32=o_fox32, dil=dil, h_lru=h_lru, mixed=mixed,
             x1=x1, hq=hq, qc=qc, memn=memn, kv=kv, oc=oc, x2=x2, h2=h2, hu=hu, hg=hg, act=act)
    return x3, s


def layer_bwd(dx3, mem2d, W, P, bias, s, tag):
    mm = functools.partial(matmul, out_dtype=GRAD_WIRE, trans_a=True)
    gW, gP = {}, {}
    dact = matmul(dx3, W["down"], trans_b=True, name=f"{tag}_d_act")
    gW["down"] = mm(s["act"], dx3, name=f"{tag}_g_down")
    dcu, dcg, dwu, dwg, dbu, dbg = glu_bwd(s["hu"], s["hg"], dact, P["wu"], P["wg"], P["bu"], P["bg"], name=f"{tag}_glu_bwd")
    gP["ffn_conv_w"] = jnp.concatenate([dwu, dwg], axis=1)
    gP["ffn_conv_b"] = jnp.concatenate([dbu, dbg], axis=1)
    dhu = conv3_transpose(dcu, P["wu"], name=f"{tag}_convT_u")
    dhg = conv3_transpose(dcg, P["wg"], name=f"{tag}_convT_g")
    dh2 = matmul(dhu, W["up_u"], name=f"{tag}_d_h2u")
    dh2 = matmul(dhg, W["up_g"], residual=dh2, name=f"{tag}_d_h2g")
    gW["up_u"] = mm(dhu, s["h2"], name=f"{tag}_g_up_u")
    gW["up_g"] = mm(dhg, s["h2"], name=f"{tag}_g_up_g")
    dx2, gP["norm_ffn_g"] = rmsnorm_bwd(s["x2"], P["norm_ffn_g"], dh2, dx3, name=f"{tag}_norm_ffn_bwd")
    doc = matmul(dx2, W["coT"], name=f"{tag}_d_oc")
    gW["coT"] = mm(dx2, s["oc"], name=f"{tag}_g_co")
    dqc, dkv = cross_bwd(s["qc"], s["kv"], doc, name=f"{tag}_cross_bwd")
    dhq = matmul(dqc, W["cq"], trans_b=True, name=f"{tag}_d_hq")
    gW["cq"] = mm(s["hq"], dqc, name=f"{tag}_g_cq")
    dmemn = matmul(dkv, W["ckv"], trans_b=True, name=f"{tag}_d_memn")
    gW["ckv"] = mm(s["memn"], dkv, name=f"{tag}_g_ckv")
    _, gP["norm_mem_g"] = rmsnorm_bwd(mem2d, P["norm_mem_g"], dmemn, None, name=f"{tag}_norm_mem_bwd")
    dx1, gP["norm_cross_g"] = rmsnorm_bwd(s["x1"], P["norm_cross_g"], dhq, dx2, name=f"{tag}_norm_cross_bwd")
    dmixed = matmul(dx1, W["out"], trans_b=True, name=f"{tag}_d_mixed")
    gW["out"] = mm(s["mixed"], dx1, name=f"{tag}_g_out")
    qkv, aux = s["qkv"], s["aux"]
    d_sb = sbw_bwd(qkv, dmixed, name=f"{tag}_sb_bwd")
    dfq, dfk, dfv, dcc, dcr = foxw_bwd(qkv, s["cumc"], s["cumr"], s["lse_fox"], s["o_fox32"], dmixed, name=f"{tag}_fox_bwd")
    dcum = sum_cast([dcc, row_to_col(dcr)], F32, name=f"{tag}_dcum")
    df, dbf = fox_prep_bwd(aux, P["bf"], dcum, name=f"{tag}_fox_prep_bwd")
    gP["b_forget"] = dbf[0, :N_HEADS]
    d_dil, ds_band = dilated_bwd(qkv, bias, s["dil"], dmixed, tag)
    dlx, dlg, dcw, dcb, dwa, dba, dwx, dbx, dlam = lru_bwd(
        aux, s["h_lru"], dmixed, P["lru_conv_w"], P["lru_conv_b"], P["wa"], P["lru_b_a"], P["wx"], P["lru_b_x"],
        P["lru_lambda"], name=f"{tag}_lru_bwd")
    gP.update(lru_conv_w=dcw, lru_conv_b=dcb, lru_w_a=_diag_blocks(dwa), lru_b_a=dba, lru_w_x=_diag_blocks(dwx),
              lru_b_x=dbx, lru_lambda=dlam)
    dqkv = jnp.concatenate([d.astype(BF16) for d in d_sb] + [dfq.astype(BF16), dfk.astype(BF16), dfv.astype(BF16), d_dil], axis=1)
    daux = jnp.concatenate([dlx, dlg, df], axis=1)
    dh1 = matmul(dqkv, W["qkv"], trans_b=True, name=f"{tag}_d_h1a")
    dh1 = matmul(daux, W["aux"], trans_b=True, residual=dh1, name=f"{tag}_d_h1b")
    gW["qkv"] = mm(s["h1"], dqkv, name=f"{tag}_g_qkv")
    gW["aux"] = mm(s["h1"], daux, name=f"{tag}_g_aux")
    dx, gP["norm_mix_g"] = rmsnorm_bwd(s["x"], P["norm_mix_g"], dh1, dx1, name=f"{tag}_norm_mix_bwd")
    return dx, gW, gP, ds_band


def local_step(x, mem, target, Ws, Ps, rel_bias, final_norm_g):
    B = x.shape[0]
    x2d = x.reshape(B * SEQ, D_MODEL)
    mem2d = mem.reshape(B * N_MEM, D_MODEL)
    bias = relbias_expand(rel_bias, name="relbias_expand")
    saved = []
    h = x2d
    for l in range(DEPTH):
        h, s = layer_fwd(h, mem2d, Ws[l], Ps[l], bias, f"l{l}")
        saved.append(s)
    loss, dh, d_final = loss_head(h, final_norm_g, target.reshape(B * SEQ, D_MODEL), name="loss_head")
    gWs, gPs, ds_bands = [None] * DEPTH, [None] * DEPTH, []
    for l in range(DEPTH - 1, -1, -1):
        dh, gWs[l], gPs[l], ds = layer_bwd(dh, mem2d, Ws[l], Ps[l], bias, saved[l], f"l{l}")
        ds_bands.append(ds)
    d_rel = relbias_reduce(sum_cast([d.reshape(-1, BAND) for d in ds_bands], F32, name="ds_band_sum").reshape(-1, BLOCK, BAND),
                           name="relbias_reduce")
    return loss, dh.reshape(B, SEQ, D_MODEL), gWs, gPs, d_rel, d_final


def small_params(p, l):
    row = lambda name: p[name][l].reshape(1, -1)
    ffn_w, ffn_b = p["ffn_conv_w"][l], row("ffn_conv_b")
    return dict(
        norm_mix_g=row("norm_mix_g"), norm_cross_g=row("norm_cross_g"), norm_mem_g=row("norm_mem_g"), norm_ffn_g=row("norm_ffn_g"),
        bf=jnp.pad(row("b_forget"), ((0, 0), (0, LANES - N_HEADS))),
        lru_conv_w=p["lru_conv_w"][l], lru_conv_b=row("lru_conv_b"), wa=_block_diag_halves(p["lru_w_a"][l]), lru_b_a=row("lru_b_a"),
        wx=_block_diag_halves(p["lru_w_x"][l]), lru_b_x=row("lru_b_x"), lru_lambda=row("lru_lambda"),
        wu=ffn_w[:, :D_FF], wg=ffn_w[:, D_FF:], bu=ffn_b[:, :D_FF], bg=ffn_b[:, D_FF:])


def canonical_weights(w_in, w_out, w_cq, w_ck, w_cv, w_co, w_up, w_down):
    sb_fox, fox_f, rest = w_in[:, :6 * GROUP_W], w_in[:, 6 * GROUP_W:6 * GROUP_W + N_HEADS], w_in[:, 6 * GROUP_W + N_HEADS:]
    dil, lru = rest[:, :3 * GROUP_W], rest[:, 3 * GROUP_W:]
    pad = jnp.zeros((w_in.shape[0], AUX_W - 2 * GROUP_W - N_HEADS), w_in.dtype)
    return dict(qkv=jnp.concatenate([sb_fox, dil], axis=1), aux=jnp.concatenate([lru, fox_f, pad], axis=1), out=w_out,
                cq=w_cq, ckv=jnp.concatenate([w_ck, w_cv], axis=1), coT=w_co.T, upT=w_up.T, down=w_down)


def native_grads(g):
    qkv, aux = g["qkv"], g["aux"]
    w_in = jnp.concatenate([qkv[:, :6 * GROUP_W], aux[:, 2 * GROUP_W:2 * GROUP_W + N_HEADS], qkv[:, 6 * GROUP_W:],
                            aux[:, :2 * GROUP_W]], axis=1)
    return (w_in, g["out"], g["cq"], g["ckv"][:, :GROUP_W], g["ckv"][:, GROUP_W:], g["coT"].T, g["upT"].T, g["down"])


ANY = pl.BlockSpec(memory_space=pl.ANY)
VMEM_SPEC = pl.BlockSpec(memory_space=pltpu.VMEM)


def _place():
    x, y, c = lax.axis_index("x"), lax.axis_index("y"), lax.axis_index("c")
    other_chips = [(1 - x, y), (x, 1 - y), (1 - x, 1 - y)]
    return x, y, c, other_chips


def _gather_body(x_ref, out_ref, send_sems, recv_sems, local_sem):
    x, y, c, chips = _place()
    me, sibling = (x, y, c), (x, y, 1 - c)

    def slot(px, py, pc):
        return out_ref.at[4 * px + 2 * py + pc]

    def copy(k, block, to, src=None):
        return pltpu.make_async_remote_copy(
            src_ref=slot(*block) if src is None else src, dst_ref=slot(*block),
            send_sem=send_sems.at[k], recv_sem=recv_sems.at[k], device_id=to, device_id_type=MESH)

    if local_sem is not None:
        mine = pltpu.make_async_copy(x_ref, slot(*me), local_sem)
        mine.start()
    first = [copy(0, me, sibling, src=x_ref)]
    first += [copy(1 + j, me, (*chip, c), src=x_ref) for j, chip in enumerate(chips)]
    for cp in first:
        cp.start()
    passed = [copy(4 + j, (*chip, c), sibling) for j, chip in enumerate(chips)]
    for j, chip in enumerate(chips):
        copy(1 + j, (*chip, c), me).wait_recv()
        passed[j].start()
    copy(0, sibling, me).wait_recv()
    for j, chip in enumerate(chips):
        copy(4 + j, (*chip, 1 - c), me).wait_recv()
    for cp in first + passed:
        cp.wait_send()
    if local_sem is not None:
        mine.wait()


_GATHER_SEMS = [pltpu.SemaphoreType.DMA((7,)), pltpu.SemaphoreType.DMA((7,)), pltpu.SemaphoreType.DMA]


def allgather_hbm(shard, me, *, name):
    def body(x_ref, out_ref, send_sems, recv_sems):
        _gather_body(x_ref, out_ref, send_sems, recv_sems, None)

    others = pl.pallas_call(
        body, name=name, in_specs=[ANY], out_specs=ANY,
        out_shape=jax.ShapeDtypeStruct((N_DEV,) + shard.shape, shard.dtype), scratch_shapes=_GATHER_SEMS[:2],
    )(shard)
    return lax.dynamic_update_slice(others, shard[None], (me, 0, 0))


def allgather_small(x, *, name, reduce=False):
    def body(x_ref, out_ref, *rest):
        _gather_body(x_ref, out_ref, *rest[-3:])
        if reduce:
            acc = out_ref[0]
            for d in range(1, N_DEV):
                acc = acc + out_ref[d]
            rest[0][...] = acc

    sd = jax.ShapeDtypeStruct
    return pl.pallas_call(
        body, name=name, in_specs=[VMEM_SPEC], out_specs=[VMEM_SPEC, VMEM_SPEC] if reduce else VMEM_SPEC,
        out_shape=[sd((N_DEV,) + x.shape, x.dtype), sd(x.shape, x.dtype)] if reduce else sd((N_DEV,) + x.shape, x.dtype),
        scratch_shapes=_GATHER_SEMS, compiler_params=pltpu.CompilerParams(vmem_limit_bytes=VMEM_LIMIT_V7X),
    )(x)


N_CHIPS = 4


def pair_exchange(g, *, name):
    _, R, C = g.shape

    def body(g_ref, recv_ref, send_sems, recv_sems):
        x, y, c, _ = _place()
        sibling = (x, y, 1 - c)
        remote = [pltpu.make_async_remote_copy(
            src_ref=g_ref.at[2 * q + (1 - c)], dst_ref=recv_ref.at[q], send_sem=send_sems.at[q], recv_sem=recv_sems.at[q],
            device_id=sibling, device_id_type=MESH) for q in range(N_CHIPS)]
        for cp in remote:
            cp.start()
        for cp in remote:
            cp.wait_recv()
        for cp in remote:
            cp.wait_send()

    return pl.pallas_call(
        body, name=name, in_specs=[ANY], out_specs=ANY, out_shape=jax.ShapeDtypeStruct((N_CHIPS, R, C), g.dtype),
        scratch_shapes=[pltpu.SemaphoreType.DMA((N_CHIPS,))] * 2,
    )(g)


def chip_exchange(s, *, name):
    _, R, C = s.shape

    def body(s_ref, o0, o1, o2, send_sems, recv_sems):
        x, y, c, chips = _place()
        outs = (o0, o1, o2)
        copies = [pltpu.make_async_remote_copy(
            src_ref=s_ref.at[2 * cx + cy], dst_ref=outs[j], send_sem=send_sems.at[j], recv_sem=recv_sems.at[j],
            device_id=(cx, cy, c), device_id_type=MESH) for j, (cx, cy) in enumerate(chips)]
        for cp in copies:
            cp.start()
        for cp in copies:
            cp.wait_recv()
        for cp in copies:
            cp.wait_send()

    sd = jax.ShapeDtypeStruct((R, C), s.dtype)
    return pl.pallas_call(
        body, name=name, in_specs=[ANY], out_specs=[ANY] * 3, out_shape=[sd] * 3,
        scratch_shapes=[pltpu.SemaphoreType.DMA((3,)), pltpu.SemaphoreType.DMA((3,))],
    )(s)


WEIGHTS = ("norm_mix_g", "w_in", "b_forget", "lru_conv_w", "lru_conv_b", "lru_w_a", "lru_b_a", "lru_w_x", "lru_b_x", "lru_lambda",
           "w_out", "norm_cross_g", "norm_mem_g", "w_cq", "w_ck", "w_cv", "w_co", "norm_ffn_g", "w_up", "ffn_conv_w", "ffn_conv_b",
           "w_down", "rel_bias", "final_norm_g")
LARGE = ("w_in", "w_out", "w_cq", "w_ck", "w_cv", "w_co", "w_up", "w_down")
COLUMN_SPLIT_SMALL = ("lru_conv_w", "ffn_conv_w")
PACK = (("qkv", 128, 2304), ("aux", 128, 640), ("out", 128, 1024), ("cq", 128, 256), ("ckv", 128, 512), ("coT", 128, 256),
        ("upT", 704, 1024), ("down", 352, 1024))
PACK_W = 1024


def _pack_rows(parts):
    return jnp.concatenate([p.reshape(-1, PACK_W) for p in parts], axis=0)


def _pad_rows(flat, mult=8 * LANES):
    n = flat.shape[0]
    return jnp.pad(flat, (0, (-n) % mult)).reshape(-1, LANES)


def kernel(x, mem, norm_mix_g, w_in, b_forget, lru_conv_w, lru_conv_b, lru_w_a, lru_b_a, lru_w_x, lru_b_x, lru_lambda, w_out, norm_cross_g, norm_mem_g, w_cq, w_ck, w_cv, w_co, norm_ffn_g, w_up, ffn_conv_w, ffn_conv_b, w_down, rel_bias, final_norm_g, loss_target, m_norm_mix_g, m_w_in, m_b_forget, m_lru_conv_w, m_lru_conv_b, m_lru_w_a, m_lru_b_a, m_lru_w_x, m_lru_b_x, m_lru_lambda, m_w_out, m_norm_cross_g, m_norm_mem_g, m_w_cq, m_w_ck, m_w_cv, m_w_co, m_norm_ffn_g, m_w_up, m_ffn_conv_w, m_ffn_conv_b, m_w_down, m_rel_bias, m_final_norm_g, v_norm_mix_g, v_w_in, v_b_forget, v_lru_conv_w, v_lru_conv_b, v_lru_w_a, v_lru_b_a, v_lru_w_x, v_lru_b_x, v_lru_lambda, v_w_out, v_norm_cross_g, v_norm_mem_g, v_w_cq, v_w_ck, v_w_cv, v_w_co, v_norm_ffn_g, v_w_up, v_ffn_conv_w, v_ffn_conv_b, v_w_down, v_rel_bias, v_final_norm_g):
    w = dict(norm_mix_g=norm_mix_g, w_in=w_in, b_forget=b_forget, lru_conv_w=lru_conv_w, lru_conv_b=lru_conv_b, lru_w_a=lru_w_a,
             lru_b_a=lru_b_a, lru_w_x=lru_w_x, lru_b_x=lru_b_x, lru_lambda=lru_lambda, w_out=w_out, norm_cross_g=norm_cross_g,
             norm_mem_g=norm_mem_g, w_cq=w_cq, w_ck=w_ck, w_cv=w_cv, w_co=w_co, norm_ffn_g=norm_ffn_g, w_up=w_up,
             ffn_conv_w=ffn_conv_w, ffn_conv_b=ffn_conv_b, w_down=w_down, rel_bias=rel_bias, final_norm_g=final_norm_g)
    m = dict(norm_mix_g=m_norm_mix_g, w_in=m_w_in, b_forget=m_b_forget, lru_conv_w=m_lru_conv_w, lru_conv_b=m_lru_conv_b,
             lru_w_a=m_lru_w_a, lru_b_a=m_lru_b_a, lru_w_x=m_lru_w_x, lru_b_x=m_lru_b_x, lru_lambda=m_lru_lambda, w_out=m_w_out,
             norm_cross_g=m_norm_cross_g, norm_mem_g=m_norm_mem_g, w_cq=m_w_cq, w_ck=m_w_ck, w_cv=m_w_cv, w_co=m_w_co,
             norm_ffn_g=m_norm_ffn_g, w_up=m_w_up, ffn_conv_w=m_ffn_conv_w, ffn_conv_b=m_ffn_conv_b, w_down=m_w_down,
             rel_bias=m_rel_bias, final_norm_g=m_final_norm_g)
    v = dict(norm_mix_g=v_norm_mix_g, w_in=v_w_in, b_forget=v_b_forget, lru_conv_w=v_lru_conv_w, lru_conv_b=v_lru_conv_b,
             lru_w_a=v_lru_w_a, lru_b_a=v_lru_b_a, lru_w_x=v_lru_w_x, lru_b_x=v_lru_b_x, lru_lambda=v_lru_lambda, w_out=v_w_out,
             norm_cross_g=v_norm_cross_g, norm_mem_g=v_norm_mem_g, w_cq=v_w_cq, w_ck=v_w_ck, w_cv=v_w_cv, w_co=v_w_co,
             norm_ffn_g=v_norm_ffn_g, w_up=v_w_up, ffn_conv_w=v_ffn_conv_w, ffn_conv_b=v_ffn_conv_b, w_down=v_w_down,
             rel_bias=v_rel_bias, final_norm_g=v_final_norm_g)
    me = 4 * lax.axis_index("x") + 2 * lax.axis_index("y") + lax.axis_index("c")

    conv_shard = jnp.concatenate([w[n].reshape(-1) for n in COLUMN_SPLIT_SMALL])
    conv_all = allgather_small(_pad_rows(conv_shard), name="gather_conv").reshape(N_DEV, -1)
    full = dict(w)
    off = 0
    for n in COLUMN_SPLIT_SMALL:
        d, k, c = w[n].shape
        blocks = conv_all[:, off:off + d * k * c].reshape(N_DEV, d, k, c)
        full[n] = blocks.transpose(1, 2, 0, 3).reshape(d, k, N_DEV * c)
        off += d * k * c

    shard_parts = []
    for l in range(DEPTH):
        canon = canonical_weights(*[w[n][l] for n in LARGE])
        shard_parts += [canon[k].astype(BF16) for k, _, _ in PACK]
    packed = allgather_hbm(_pack_rows(shard_parts), me, name="gather_weights")
    Ws, row = [], 0
    for l in range(DEPTH):
        W = {}
        for k, r, c in PACK:
            n_rows = r * c // PACK_W
            W[k] = packed[:, row:row + n_rows].reshape(N_DEV * r, c)
            row += n_rows
        upT = W.pop("upT")
        W["up_u"], W["up_g"] = upT[:D_FF], upT[D_FF:]
        Ws.append(W)
    Ps = [small_params(full, l) for l in range(DEPTH)]

    loss, grad_x, gWs, gPs, d_rel, d_final = local_step(x, mem, loss_target, Ws, Ps, rel_bias, final_norm_g.reshape(1, -1))

    grad_parts = []
    for l in range(DEPTH):
        g = dict(gWs[l])
        g["upT"] = jnp.concatenate([g.pop("up_u"), g.pop("up_g")], axis=0)
        grad_parts += [g[k].reshape(N_DEV, r * c // PACK_W, PACK_W) for k, r, c in PACK]
    g_all = jnp.concatenate(grad_parts, axis=1)
    rows = g_all.shape[1]
    got = pair_exchange(g_all, name="grads_pair_exchange")
    own = lax.dynamic_index_in_dim(g_all.reshape(N_CHIPS, 2, rows, PACK_W), lax.axis_index("c"), axis=1, keepdims=False)
    pair = sum_cast([own.reshape(-1, PACK_W), got.reshape(-1, PACK_W)], GRAD_WIRE, name="grads_pair_sum").reshape(N_CHIPS, rows, PACK_W)
    from_x, from_y, from_xy = chip_exchange(pair, name="grads_chip_exchange")
    mine = lax.dynamic_index_in_dim(pair, 2 * lax.axis_index("x") + lax.axis_index("y"), axis=0, keepdims=False)
    g_shard = sum_cast([mine, from_x, from_y, from_xy], F32, name="grads_chip_sum")
    grads, row = {}, 0
    per_layer = []
    for l in range(DEPTH):
        g = {}
        for k, r, c in PACK:
            n_rows = r * c // PACK_W
            g[k] = g_shard[row:row + n_rows].reshape(r, c)
            row += n_rows
        per_layer.append(native_grads(g))
    for i, n in enumerate(LARGE):
        grads[n] = jnp.stack([per_layer[l][i] for l in range(DEPTH)])

    small_names = [n for n in WEIGHTS if n not in LARGE and n not in ("rel_bias", "final_norm_g")]
    pieces = [gPs[l][n].reshape(-1) for n in small_names for l in range(DEPTH)] + [d_rel.reshape(-1), d_final.reshape(-1), loss[0, :1]]
    sizes = [p.shape[0] for p in pieces]
    _, total = allgather_small(_pad_rows(jnp.concatenate(pieces)), name="allreduce_small", reduce=True)
    total = total.reshape(-1)
    off, it = 0, iter(sizes)
    for n in small_names:
        per = []
        for l in range(DEPTH):
            sz = next(it)
            per.append(total[off:off + sz])
            off += sz
        full_shape = (DEPTH,) + full[n].shape[1:]
        gfull = jnp.stack(per).reshape(full_shape)
        if n in COLUMN_SPLIT_SMALL:
            c = w[n].shape[-1]
            gfull = lax.dynamic_slice_in_dim(gfull, me * c, c, axis=gfull.ndim - 1)
        grads[n] = gfull
    grads["rel_bias"] = total[off:off + rel_bias.size].reshape(rel_bias.shape)
    off += rel_bias.size
    grads["final_norm_g"] = total[off:off + D_MODEL]
    off += D_MODEL
    loss_out = total[off]

    delta, new_m, new_v = {}, {}, {}
    for n in LARGE:
        shape = w[n].shape
        two_d = lambda a: a.reshape(-1, shape[-1])
        d_, m_, v_ = adamw(two_d(w[n]), two_d(grads[n]), two_d(m[n]), two_d(v[n]), name=f"adamw_{n}")
        delta[n], new_m[n], new_v[n] = d_.reshape(shape), m_.reshape(shape), v_.reshape(shape)
    small_all = [n for n in WEIGHTS if n not in LARGE]
    flat = lambda src: _pad_rows(jnp.concatenate([src[n].reshape(-1) for n in small_all]))
    d_, m_, v_ = adamw(flat(w), flat(grads), flat(m), flat(v), name="adamw_small")
    off = 0
    for n in small_all:
        sz, shape = w[n].size, w[n].shape
        delta[n], new_m[n], new_v[n] = (a.reshape(-1)[off:off + sz].reshape(shape) for a in (d_, m_, v_))
        off += sz

    return (loss_out, grad_x, *[grads[n] for n in WEIGHTS], *[delta[n] for n in WEIGHTS], *[new_m[n] for n in WEIGHTS],
            *[new_v[n] for n in WEIGHTS])
```

```python
import functools
import math

import numpy as np
import jax
import jax.numpy as jnp
from jax import lax
from jax.experimental import pallas as pl
from jax.experimental.pallas import tpu as pltpu

F32 = jnp.float32
BF16 = jnp.bfloat16
MESH = pl.DeviceIdType.MESH

N_DEV = 8
D_MODEL = 1024
SEQ = 2048
DEPTH = 2
HEAD_DIM = 64
N_HEADS = 4
GROUP_W = N_HEADS * HEAD_DIM
D_FF = 2816
N_MEM = 256
NUM_BUCKETS = 32
MAX_DISTANCE = 2048
BLOCK = 128
DILATIONS = (1, 4, 16)
EPS = 1e-6
LRU_C = 8.0
Q_SCALE = HEAD_DIM ** -0.5
AUX_W = 640
LRU_HALF_W = 128
LRU_HALVES = GROUP_W // LRU_HALF_W
ADAM_LR, ADAM_B1, ADAM_B2, ADAM_EPS, ADAM_WD, ADAM_STEP = 0.001, 0.9, 0.999, 1e-08, 0.01, 10

VMEM_LIMIT_V7X = 48 * 1024 * 1024


def _params(*sem):
    return pltpu.CompilerParams(dimension_semantics=sem if sem else None, vmem_limit_bytes=VMEM_LIMIT_V7X)


def _pick(n, cands):
    for c in cands:
        if n % c == 0:
            return c
    return n


def _largest_tile(n, cap, align):
    best = None
    for t in range(align, min(n, cap) + 1, align):
        if n % t == 0:
            best = t
    return n if best is None else best


def matmul(a, b, *, name, trans_a=False, trans_b=False, out_dtype=F32, residual=None):
    (K, M) = a.shape if trans_a else a.shape[::-1]
    (N, Kb) = b.shape if trans_b else b.shape[::-1]
    assert K == Kb, (a.shape, b.shape)
    tm = _largest_tile(M, 512, 128)
    tn = _largest_tile(N, 1408, 128)
    tk = _largest_tile(K, 2816, 128)
    nk = K // tk
    a_spec = pl.BlockSpec((tk, tm), lambda i, j, k: (k, i)) if trans_a else pl.BlockSpec((tm, tk), lambda i, j, k: (i, k))
    b_spec = pl.BlockSpec((tn, tk), lambda i, j, k: (j, k)) if trans_b else pl.BlockSpec((tk, tn), lambda i, j, k: (k, j))
    o_spec = pl.BlockSpec((tm, tn), lambda i, j, k: (i, j))
    dims = (((0 if trans_a else 1,), (1 if trans_b else 0,)), ((), ()))
    has_res = residual is not None

    def body(*refs):
        a_ref, b_ref = refs[0], refs[1]
        r_ref = refs[2] if has_res else None
        part = lax.dot_general(a_ref[...].astype(BF16), b_ref[...].astype(BF16), dims, preferred_element_type=F32)
        if nk == 1:
            if has_res:
                part = part + r_ref[...].astype(F32)
            refs[-1][...] = part.astype(out_dtype)
            return
        o_ref, acc_ref = refs[-2], refs[-1]
        k = pl.program_id(2)

        @pl.when(k == 0)
        def _():
            acc_ref[...] = part

        @pl.when(k > 0)
        def _():
            acc_ref[...] += part

        @pl.when(k == nk - 1)
        def _():
            r = acc_ref[...]
            if has_res:
                r = r + r_ref[...].astype(F32)
            o_ref[...] = r.astype(out_dtype)

    ops = (a, b) + ((residual,) if has_res else ())
    return pl.pallas_call(
        body, name=name, grid=(M // tm, N // tn, nk),
        in_specs=[a_spec, b_spec] + ([o_spec] if has_res else []),
        out_specs=o_spec, out_shape=jax.ShapeDtypeStruct((M, N), out_dtype),
        scratch_shapes=[pltpu.VMEM((tm, tn), F32)] if nk > 1 else [],
        compiler_params=_params("parallel", "parallel", "arbitrary"),
    )(*ops)


def rmsnorm_fwd(x, g, *, name):
    R, D = x.shape
    tr = _pick(R, (512, 256))

    def body(x_ref, g_ref, o_ref):
        xv = x_ref[...]
        r = lax.rsqrt(jnp.mean(xv * xv, axis=-1, keepdims=True) + EPS)
        o_ref[...] = (xv * r * g_ref[...]).astype(BF16)

    return pl.pallas_call(
        body, name=name, grid=(R // tr,),
        in_specs=[pl.BlockSpec((tr, D), lambda i: (i, 0)), pl.BlockSpec((1, D), lambda i: (0, 0))],
        out_specs=pl.BlockSpec((tr, D), lambda i: (i, 0)), out_shape=jax.ShapeDtypeStruct((R, D), BF16),
        compiler_params=_params("parallel"),
    )(x, g)


def rmsnorm_bwd(x, g, dh, dres, *, name):
    R, D = x.shape
    tr = _pick(R, (512, 256))
    has_res = dres is not None

    def body(*refs):
        x_ref, g_ref, dh_ref = refs[:3]
        dx_ref, dg_ref = refs[-2], refs[-1]
        xv = x_ref[...]
        r = lax.rsqrt(jnp.mean(xv * xv, axis=-1, keepdims=True) + EPS)
        n = xv * r
        dhv = dh_ref[...]
        dn = dhv * g_ref[...]
        dx = r * (dn - n * jnp.mean(dn * n, axis=-1, keepdims=True))
        if has_res:
            dx = dx + refs[3][...]
        dx_ref[...] = dx
        part = jnp.sum(dhv * n, axis=0, keepdims=True)

        @pl.when(pl.program_id(0) == 0)
        def _():
            dg_ref[...] = part

        @pl.when(pl.program_id(0) > 0)
        def _():
            dg_ref[...] += part

    row = pl.BlockSpec((tr, D), lambda i: (i, 0))
    vec = pl.BlockSpec((1, D), lambda i: (0, 0))
    ops = (x, g, dh) + ((dres,) if has_res else ())
    return pl.pallas_call(
        body, name=name, grid=(R // tr,),
        in_specs=[row, vec, row] + ([row] if has_res else []),
        out_specs=[row, vec],
        out_shape=[jax.ShapeDtypeStruct((R, D), F32), jax.ShapeDtypeStruct((1, D), F32)],
        compiler_params=_params("arbitrary"),
    )(*ops)


_SQRT_HALF = 0.7071067811865476
_INV_SQRT_2PI = 0.3989422804014327


def _erf(x):
    ax = jnp.abs(x)
    t = 1.0 / (1.0 + 0.3275911 * ax)
    poly = t * (0.254829592 + t * (-0.284496736 + t * (1.421413741 + t * (-1.453152027 + t * 1.061405429))))
    y = 1.0 - poly * jnp.exp(-ax * ax)
    return jnp.where(x < 0, -y, y)


def _gelu_cdf(x):
    return 0.5 * (1.0 + _erf(x * _SQRT_HALF))


def _gelu_and_grad(x):
    cdf = _gelu_cdf(x)
    return x * cdf, cdf + x * _INV_SQRT_2PI * jnp.exp(-0.5 * x * x)


def _shift_down(main, halo, first, shifts):
    halo = jnp.where(first, 0.0, halo)
    ext = jnp.concatenate([halo, main], axis=0)
    return [pltpu.roll(ext, s, 0)[8:] for s in shifts]


def _conv3(main, halo, first, w, b):
    m1, m2 = _shift_down(main, halo, first, (1, 2))
    return ((b + w[0:1] * m2) + w[1:2] * m1) + w[2:3] * main, m1, m2


def glu_fwd(hu, hg, wu, wg, bu, bg, *, name):
    T, F = hu.shape
    tm, tf = 512, _largest_tile(F, 704, 128)
    hb = tm // 8
    blocks_per_example = SEQ // tm

    def body(hu_ref, hg_ref, hau_ref, hag_ref, wu_ref, wg_ref, bu_ref, bg_ref, o_ref):
        first = pl.program_id(0) % blocks_per_example == 0
        up, _, _ = _conv3(hu_ref[...], hau_ref[...], first, wu_ref[...], bu_ref[...])
        gate, _, _ = _conv3(hg_ref[...], hag_ref[...], first, wg_ref[...], bg_ref[...])
        o_ref[...] = (gate * _gelu_cdf(gate) * up).astype(BF16)

    main = pl.BlockSpec((tm, tf), lambda i, j: (i, j))
    halo = pl.BlockSpec((8, tf), lambda i, j: (jnp.maximum(i * hb - 1, 0), j))
    w3 = pl.BlockSpec((3, tf), lambda i, j: (0, j))
    b1 = pl.BlockSpec((1, tf), lambda i, j: (0, j))
    return pl.pallas_call(
        body, name=name, grid=(T // tm, F // tf),
        in_specs=[main, main, halo, halo, w3, w3, b1, b1],
        out_specs=main, out_shape=jax.ShapeDtypeStruct((T, F), BF16),
        compiler_params=_params("parallel", "parallel"),
    )(hu, hg, hu, hg, wu, wg, bu, bg)


def glu_bwd(hu, hg, dact, wu, wg, bu, bg, *, name):
    T, F = hu.shape
    tm, tf = 512, _largest_tile(F, 704, 128)
    hb = tm // 8
    blocks_per_example = SEQ // tm

    def body(hu_ref, hg_ref, hau_ref, hag_ref, da_ref, wu_ref, wg_ref, bu_ref, bg_ref,
             du_ref, dg_ref, dwu_ref, dwg_ref, dbu_ref, dbg_ref):
        i = pl.program_id(1)
        first = i % blocks_per_example == 0
        xu, xg = hu_ref[...], hg_ref[...]
        up, u1, u2 = _conv3(xu, hau_ref[...], first, wu_ref[...], bu_ref[...])
        gate, g1, g2 = _conv3(xg, hag_ref[...], first, wg_ref[...], bg_ref[...])
        act, dact_dgate = _gelu_and_grad(gate)
        da = da_ref[...]
        dup = da * act
        dgate = da * up * dact_dgate
        du_ref[...] = dup
        dg_ref[...] = dgate

        def sums(d, x0, x1, x2):
            s = lambda v: jnp.sum(v, axis=0, keepdims=True)
            return jnp.concatenate([s(d * x2), s(d * x1), s(d * x0)], axis=0), s(d)

        pwu, pbu = sums(dup, xu, u1, u2)
        pwg, pbg = sums(dgate, xg, g1, g2)

        @pl.when(i == 0)
        def _():
            dwu_ref[...] = pwu
            dwg_ref[...] = pwg
            dbu_ref[...] = pbu
            dbg_ref[...] = pbg

        @pl.when(i > 0)
        def _():
            dwu_ref[...] += pwu
            dwg_ref[...] += pwg
            dbu_ref[...] += pbu
            dbg_ref[...] += pbg

    main = pl.BlockSpec((tm, tf), lambda j, i: (i, j))
    halo = pl.BlockSpec((8, tf), lambda j, i: (jnp.maximum(i * hb - 1, 0), j))
    w3 = pl.BlockSpec((3, tf), lambda j, i: (0, j))
    b1 = pl.BlockSpec((1, tf), lambda j, i: (0, j))
    sd = jax.ShapeDtypeStruct
    return pl.pallas_call(
        body, name=name, grid=(F // tf, T // tm),
        in_specs=[main, main, halo, halo, main, w3, w3, b1, b1],
        out_specs=[main, main, w3, w3, b1, b1],
        out_shape=[sd((T, F), F32), sd((T, F), F32), sd((3, F), F32), sd((3, F), F32), sd((1, F), F32), sd((1, F), F32)],
        compiler_params=_params("parallel", "arbitrary"),
    )(hu, hg, hu, hg, dact, wu, wg, bu, bg)


def conv3_transpose(d, w, *, name):
    T, F = d.shape
    tm, tf = 512, _largest_tile(F, 704, 128)
    hb = tm // 8
    blocks_per_example = SEQ // tm
    n_halo_blocks = T // 8

    def body(d_ref, ha_ref, w_ref, o_ref):
        last = pl.program_id(0) % blocks_per_example == blocks_per_example - 1
        main = d_ref[...]
        halo = jnp.where(last, 0.0, ha_ref[...])
        ext = jnp.concatenate([main, halo], axis=0)
        n = tm + 8
        p1 = pltpu.roll(ext, n - 1, 0)[:tm]
        p2 = pltpu.roll(ext, n - 2, 0)[:tm]
        w = w_ref[...]
        o_ref[...] = (w[2:3] * main + w[1:2] * p1 + w[0:1] * p2).astype(BF16)

    main = pl.BlockSpec((tm, tf), lambda i, j: (i, j))
    halo = pl.BlockSpec((8, tf), lambda i, j: (jnp.minimum((i + 1) * hb, n_halo_blocks - 1), j))
    return pl.pallas_call(
        body, name=name, grid=(T // tm, F // tf),
        in_specs=[main, halo, pl.BlockSpec((3, tf), lambda i, j: (0, j))],
        out_specs=main, out_shape=jax.ShapeDtypeStruct((T, F), BF16),
        compiler_params=_params("parallel", "parallel"),
    )(d, d, w)


def loss_head(x, g, target, *, name):
    T, D = x.shape
    tr = 256

    def body(x_ref, g_ref, t_ref, loss_ref, dx_ref, dg_ref):
        xv = x_ref[...]
        gv = g_ref[...]
        r = lax.rsqrt(jnp.mean(xv * xv, axis=-1, keepdims=True) + EPS)
        n = xv * r
        err = n * gv - t_ref[...]
        part_loss = jnp.zeros((1, 128), F32) + 0.5 * jnp.sum(jnp.mean(err * err, axis=-1, keepdims=True))
        dy = err * (1.0 / D)
        dn = dy * gv
        dx_ref[...] = r * (dn - n * jnp.mean(dn * n, axis=-1, keepdims=True))
        part_g = jnp.sum(dy * n, axis=0, keepdims=True)

        @pl.when(pl.program_id(0) == 0)
        def _():
            loss_ref[...] = part_loss
            dg_ref[...] = part_g

        @pl.when(pl.program_id(0) > 0)
        def _():
            loss_ref[...] += part_loss
            dg_ref[...] += part_g

    row = pl.BlockSpec((tr, D), lambda i: (i, 0))
    vec = pl.BlockSpec((1, D), lambda i: (0, 0))
    sd = jax.ShapeDtypeStruct
    return pl.pallas_call(
        body, name=name, grid=(T // tr,),
        in_specs=[row, vec, row],
        out_specs=[pl.BlockSpec((1, 128), lambda i: (0, 0)), row, vec],
        out_shape=[sd((1, 128), F32), sd((T, D), F32), sd((1, D), F32)],
        compiler_params=_params("arbitrary"),
    )(x, g, target)


def adamw(w, g, m, v, *, name):
    R, C = w.shape
    tr = _pick(R, (256, 128, 64, 32, 16, 8))

    def body(w_ref, g_ref, m_ref, v_ref, d_ref, nm_ref, nv_ref):
        gv = g_ref[...]
        mn = ADAM_B1 * m_ref[...] + (1.0 - ADAM_B1) * gv
        vn = ADAM_B2 * v_ref[...] + (1.0 - ADAM_B2) * (gv * gv)
        m_hat = mn / (1.0 - ADAM_B1 ** ADAM_STEP)
        v_hat = vn / (1.0 - ADAM_B2 ** ADAM_STEP)
        d_ref[...] = -ADAM_LR * (m_hat / (jnp.sqrt(v_hat) + ADAM_EPS) + ADAM_WD * w_ref[...])
        nm_ref[...] = mn
        nv_ref[...] = vn

    blk = pl.BlockSpec((tr, C), lambda i: (i, 0))
    sd = jax.ShapeDtypeStruct((R, C), F32)
    return pl.pallas_call(
        body, name=name, grid=(R // tr,), in_specs=[blk] * 4, out_specs=[blk] * 3, out_shape=[sd] * 3,
        compiler_params=_params("parallel"),
    )(w, g, m, v)


def adamw_many(ws, gs, ms, vs, *, name):
    n = len(ws)

    def body(*refs):
        ins, outs = refs[:4 * n], refs[4 * n:]
        for i in range(n):
            w_ref, g_ref, m_ref, v_ref = ins[i], ins[n + i], ins[2 * n + i], ins[3 * n + i]
            gv = g_ref[...]
            mn = ADAM_B1 * m_ref[...] + (1.0 - ADAM_B1) * gv
            vn = ADAM_B2 * v_ref[...] + (1.0 - ADAM_B2) * (gv * gv)
            m_hat = mn / (1.0 - ADAM_B1 ** ADAM_STEP)
            v_hat = vn / (1.0 - ADAM_B2 ** ADAM_STEP)
            outs[i][...] = -ADAM_LR * (m_hat / (jnp.sqrt(v_hat) + ADAM_EPS) + ADAM_WD * w_ref[...])
            outs[n + i][...] = mn
            outs[2 * n + i][...] = vn

    vm = pl.BlockSpec(memory_space=pltpu.VMEM)
    shapes = [jax.ShapeDtypeStruct(w.shape, F32) for w in ws]
    res = pl.pallas_call(
        body, name=name, in_specs=[vm] * (4 * n), out_specs=[vm] * (3 * n), out_shape=shapes * 3, compiler_params=_params(),
    )(*ws, *gs, *ms, *vs)
    return res[:n], res[n:2 * n], res[2 * n:]


def _softplus(x):
    return jnp.maximum(x, 0.0) + jnp.log(1.0 + jnp.exp(-jnp.abs(x)))


def _lru_gates(x, cw, cb, wa, ba, wx, bx, lam):
    S = x.shape[0]
    row = lax.broadcasted_iota(jnp.int32, (S, 1), 0)

    def back(s):
        return jnp.where(row >= s, pltpu.roll(x, s, 0), 0.0)

    xc = (((cb + cw[0:1] * back(3)) + cw[1:2] * back(2)) + cw[2:3] * back(1)) + cw[3:4] * x
    xb = xc.astype(BF16)
    r = jax.nn.sigmoid(jnp.dot(xb, wa, preferred_element_type=F32) + ba)
    ig = jax.nn.sigmoid(jnp.dot(xb, wx, preferred_element_type=F32) + bx)
    sp = _softplus(-lam)
    la = -LRU_C * r * sp
    a = jnp.exp(la)
    y = 2.0 * la
    one_minus_a2 = jnp.where(y > -0.05, -y * (1.0 + y * (0.5 + y * (1.0 / 6.0 + y * (1.0 / 24.0)))), 1.0 - jnp.exp(y))
    mm = jnp.sqrt(one_minus_a2)
    return xc, xb, r, ig, sp, a, mm


def lru_fwd(aux, cw, cb, wa, ba, wx, bx, lam, *, name):
    T = aux.shape[0]
    S, C = SEQ, LRU_HALF_W

    def body(x_ref, g_ref, cw_ref, cb_ref, wa_ref, ba_ref, wx_ref, bx_ref, lam_ref, o_ref, h_ref, a_s, u_s):
        xc, _, r, ig, sp, a, mm = _lru_gates(x_ref[...], cw_ref[...], cb_ref[...], wa_ref[...], ba_ref[...],
                                             wx_ref[...], bx_ref[...], lam_ref[...])
        a_s[...] = a
        u_s[...] = mm * (ig * xc)

        def group(i, h):
            base = pl.multiple_of(i * 8, 8)
            a8 = a_s[pl.ds(base, 8), :]
            u8 = u_s[pl.ds(base, 8), :]
            for rr in range(8):
                h = a8[rr:rr + 1] * h + u8[rr:rr + 1]
                h_ref[pl.ds(base + rr, 1), :] = h
            return h

        lax.fori_loop(0, S // 8, group, jnp.zeros((1, C), F32))
        gate = g_ref[...]
        o_ref[...] = (h_ref[...] * (gate * _gelu_cdf(gate))).astype(BF16)

    blk = lambda col: pl.BlockSpec((S, C), lambda c, b: (b, col + c))
    par = lambda rows: pl.BlockSpec((rows, C), lambda c, b: (0, c))
    sq = pl.BlockSpec((None, C, C), lambda c, b: (c, 0, 0))
    sd = jax.ShapeDtypeStruct
    W = LRU_HALVES * C
    return pl.pallas_call(
        body, name=name, grid=(LRU_HALVES, T // S),
        in_specs=[blk(0), blk(LRU_HALVES), par(4), par(1), sq, par(1), sq, par(1), par(1)],
        out_specs=[blk(0), blk(0)], out_shape=[sd((T, W), BF16), sd((T, W), F32)],
        scratch_shapes=[pltpu.VMEM((S, C), F32), pltpu.VMEM((S, C), F32)],
        compiler_params=_params("parallel", "parallel"),
    )(aux, aux, cw, cb, wa, ba, wx, bx, lam)


def lru_bwd(aux, h, dmixed, cw, cb, wa, ba, wx, bx, lam, *, name):
    T = aux.shape[0]
    S, C = SEQ, LRU_HALF_W

    def body(x_ref, g_ref, h_ref, do_ref, cw_ref, cb_ref, wa_ref, ba_ref, wx_ref, bx_ref, lam_ref,
             dx_ref, dgate_ref, dcw_ref, dcb_ref, dwa_ref, dba_ref, dwx_ref, dbx_ref, dlam_ref, a_s, d_s):
        x = x_ref[...]
        cw = cw_ref[...]
        lam = lam_ref[...]
        xc, xb, r, ig, sp, a, mm = _lru_gates(x, cw, cb_ref[...], wa_ref[...], ba_ref[...], wx_ref[...], bx_ref[...], lam)
        gate = g_ref[...]
        gl, dgl = _gelu_and_grad(gate)
        dout = do_ref[...]
        hv = h_ref[...]
        dgate_ref[...] = dout * hv * dgl
        a_s[...] = a
        d_s[...] = dout * gl

        def group(i, c):
            base = pl.multiple_of((S // 8 - 1 - i) * 8, 8)
            a8 = a_s[pl.ds(base, 8), :]
            d8 = d_s[pl.ds(base, 8), :]
            for rr in range(7, -1, -1):
                d = d8[rr:rr + 1] + c
                d_s[pl.ds(base + rr, 1), :] = d
                c = a8[rr:rr + 1] * d
            return c

        lax.fori_loop(0, S // 8, group, jnp.zeros((1, C), F32))
        row = lax.broadcasted_iota(jnp.int32, (S, 1), 0)
        dht = d_s[...]
        h_prev = jnp.where(row >= 1, pltpu.roll(hv, 1, 0), 0.0)
        da = dht * h_prev
        gx = ig * xc
        dmm = dht * gx
        dig = dht * mm * xc
        dxc = dht * mm * ig
        dla = da * a - dmm * (a * a) / mm
        dr = dla * (-LRU_C * sp)
        dsp = jnp.sum(dla * (-LRU_C * r), axis=0, keepdims=True)
        dlam = dsp * (-jax.nn.sigmoid(-lam))
        dpa = dr * r * (1.0 - r)
        dpx = dig * ig * (1.0 - ig)
        dpa_b, dpx_b = dpa.astype(BF16), dpx.astype(BF16)
        nt = (((1,), (1,)), ((), ()))
        tn = (((0,), (0,)), ((), ()))
        dxc = dxc + lax.dot_general(dpa_b, wa_ref[...], nt, preferred_element_type=F32) \
                  + lax.dot_general(dpx_b, wx_ref[...], nt, preferred_element_type=F32)
        dwa = lax.dot_general(xb, dpa_b, tn, preferred_element_type=F32)
        dwx = lax.dot_general(xb, dpx_b, tn, preferred_element_type=F32)

        def fwd(v, s):
            return jnp.where(row < S - s, pltpu.roll(v, S - s, 0), 0.0)

        def back(v, s):
            return jnp.where(row >= s, pltpu.roll(v, s, 0), 0.0)

        dx_ref[...] = cw[3:4] * dxc + cw[2:3] * fwd(dxc, 1) + cw[1:2] * fwd(dxc, 2) + cw[0:1] * fwd(dxc, 3)
        s0 = lambda v: jnp.sum(v, axis=0, keepdims=True)
        dcw = jnp.concatenate([s0(dxc * back(x, 3)), s0(dxc * back(x, 2)), s0(dxc * back(x, 1)), s0(dxc * x)], axis=0)
        parts = ((dcw_ref, dcw), (dcb_ref, s0(dxc)), (dwa_ref, dwa), (dba_ref, s0(dpa)), (dwx_ref, dwx),
                 (dbx_ref, s0(dpx)), (dlam_ref, dlam))

        @pl.when(pl.program_id(1) == 0)
        def _():
            for ref, val in parts:
                ref[...] = val

        @pl.when(pl.program_id(1) > 0)
        def _():
            for ref, val in parts:
                ref[...] += val

    blk = lambda col: pl.BlockSpec((S, C), lambda c, b: (b, col + c))
    par = lambda rows: pl.BlockSpec((rows, C), lambda c, b: (0, c))
    sq = pl.BlockSpec((None, C, C), lambda c, b: (c, 0, 0))
    sd = jax.ShapeDtypeStruct
    W = LRU_HALVES * C
    vec = sd((1, W), F32)
    return pl.pallas_call(
        body, name=name, grid=(LRU_HALVES, T // S),
        in_specs=[blk(0), blk(LRU_HALVES), blk(0), blk(3 * LRU_HALVES), par(4), par(1), sq, par(1), sq, par(1), par(1)],
        out_specs=[blk(0), blk(0), par(4), par(1), sq, par(1), sq, par(1), par(1)],
        out_shape=[sd((T, W), F32), sd((T, W), F32), sd((4, W), F32), vec, sd((LRU_HALVES, C, C), F32), vec,
                   sd((LRU_HALVES, C, C), F32), vec, vec],
        scratch_shapes=[pltpu.VMEM((S, C), F32), pltpu.VMEM((S, C), F32)],
        compiler_params=_params("parallel", "arbitrary"),
    )(aux, aux, h, dmixed, cw, cb, wa, ba, wx, bx, lam)


_NT = (((1,), (1,)), ((), ()))
_TN = (((0,), (0,)), ((), ()))


def _dot(a, b, dims=None):
    if dims is None:
        return jnp.dot(a, b, preferred_element_type=F32)
    return lax.dot_general(a, b, dims, preferred_element_type=F32)


def _hs(h):
    return slice(h * HEAD_DIM, (h + 1) * HEAD_DIM)


def cross_fwd(q, kv, *, name):
    T = q.shape[0]
    tq = 512

    def body(q_ref, kv_ref, o_ref):
        for h in range(N_HEADS):
            qh = q_ref[:, _hs(h)] * Q_SCALE
            k = kv_ref[:, _hs(h)]
            v = kv_ref[:, GROUP_W + h * HEAD_DIM:GROUP_W + (h + 1) * HEAD_DIM]
            s = _dot(qh, k, _NT)
            p = jnp.exp(s - jnp.max(s, axis=-1, keepdims=True))
            p = p / jnp.sum(p, axis=-1, keepdims=True)
            o_ref[:, _hs(h)] = _dot(p.astype(BF16), v).astype(BF16)

    per = SEQ // tq
    return pl.pallas_call(
        body, name=name, grid=(T // tq,),
        in_specs=[pl.BlockSpec((tq, GROUP_W), lambda i: (i, 0)), pl.BlockSpec((N_MEM, 2 * GROUP_W), lambda i: (i // per, 0))],
        out_specs=pl.BlockSpec((tq, GROUP_W), lambda i: (i, 0)), out_shape=jax.ShapeDtypeStruct((T, GROUP_W), BF16),
        compiler_params=_params("parallel"),
    )(q, kv)


def cross_bwd(q, kv, do, *, name):
    T = q.shape[0]
    tq = 512
    per = SEQ // tq

    def body(q_ref, kv_ref, do_ref, dq_ref, dkv_ref):
        first = pl.program_id(0) % per == 0
        for h in range(N_HEADS):
            vs = slice(GROUP_W + h * HEAD_DIM, GROUP_W + (h + 1) * HEAD_DIM)
            qh = q_ref[:, _hs(h)] * Q_SCALE
            k = kv_ref[:, _hs(h)]
            v = kv_ref[:, vs]
            doh = do_ref[:, _hs(h)].astype(BF16)
            s = _dot(qh, k, _NT)
            p = jnp.exp(s - jnp.max(s, axis=-1, keepdims=True))
            p = p / jnp.sum(p, axis=-1, keepdims=True)
            dp = _dot(doh, v, _NT)
            ds = (p * (dp - jnp.sum(p * dp, axis=-1, keepdims=True))).astype(BF16)
            dq_ref[:, _hs(h)] = (_dot(ds, k) * Q_SCALE).astype(BF16)
            dk = _dot(ds, qh, _TN)
            dv = _dot(p.astype(BF16), doh, _TN)

            @pl.when(first)
            def _():
                dkv_ref[:, _hs(h)] = dk
                dkv_ref[:, vs] = dv

            @pl.when(jnp.logical_not(first))
            def _():
                dkv_ref[:, _hs(h)] += dk
                dkv_ref[:, vs] += dv

    qb = pl.BlockSpec((tq, GROUP_W), lambda i: (i, 0))
    kvb = pl.BlockSpec((N_MEM, 2 * GROUP_W), lambda i: (i // per, 0))
    sd = jax.ShapeDtypeStruct
    return pl.pallas_call(
        body, name=name, grid=(T // tq,),
        in_specs=[qb, kvb, qb], out_specs=[qb, kvb],
        out_shape=[sd((T, GROUP_W), BF16), sd(kv.shape, F32)],
        compiler_params=_params("arbitrary"),
    )(q, kv, do)


NB = SEQ // BLOCK
NEG = -1e30


def _split_dot(x, tri):
    hi = x.astype(BF16)
    lo = (x - hi.astype(F32)).astype(BF16)
    return _dot(hi, tri) + _dot(lo, tri)


def _blk(i):
    return pl.ds(pl.multiple_of(i * BLOCK, BLOCK), BLOCK)


def _iotas():
    row = lax.broadcasted_iota(jnp.int32, (BLOCK, BLOCK), 0)
    col = lax.broadcasted_iota(jnp.int32, (BLOCK, BLOCK), 1)
    return row, col


def _sb_scores(q, k, mask, later, csum, want_sigmoid=False):
    z = _dot(q, k, _NT)
    lk = -_softplus(z)
    if mask is not None:
        lk = jnp.where(mask, lk, 0.0)
    lka = _split_dot(lk, later) + csum
    att = jnp.exp(z + lk + lka)
    sg = jnp.exp(z + lk) if want_sigmoid else None
    if mask is not None:
        att = jnp.where(mask, att, 0.0)
        sg = jnp.where(mask, sg, 0.0) if want_sigmoid else None
    return att, sg, lk


def _rowsum(v):
    return jnp.sum(v, axis=1, keepdims=True)


HEADS = tuple(range(N_HEADS))


def _qkv_specs(first_col):
    return [pl.BlockSpec((SEQ, GROUP_W), lambda b, c=first_col + j: (b, c)) for j in range(3)]


LANES = 128
CUM_BLK = 256


def col_to_row(c):
    b = c.shape[0] // SEQ
    return c.reshape(b, SEQ, LANES)[:, :, :8].transpose(0, 2, 1).reshape(b * 8, SEQ)


def row_to_col(r):
    b = r.shape[0] // 8
    c = r.reshape(b, 8, SEQ).transpose(0, 2, 1)
    return jnp.pad(c, ((0, 0), (0, 0), (0, LANES - 8))).reshape(b * SEQ, LANES)


def fox_prep(aux, bf, *, name):
    T = aux.shape[0]

    def body(f_ref, b_ref, o_ref):
        row = lax.broadcasted_iota(jnp.int32, (CUM_BLK, CUM_BLK), 0)
        col = lax.broadcasted_iota(jnp.int32, (CUM_BLK, CUM_BLK), 1)
        upto = (col <= row).astype(BF16)
        carry = jnp.zeros((1, LANES), F32)
        for n in range(SEQ // CUM_BLK):
            rows = slice(n * CUM_BLK, (n + 1) * CUM_BLK)
            logf = -_softplus(-(f_ref[rows, :] + b_ref[...]))
            hi = logf.astype(BF16)
            lo = (logf - hi.astype(F32)).astype(BF16)
            cum = _dot(upto, hi) + _dot(upto, lo) + carry
            o_ref[rows, :] = cum
            carry = cum[CUM_BLK - 1:CUM_BLK]

    return pl.pallas_call(
        body, name=name, grid=(T // SEQ,),
        in_specs=[pl.BlockSpec((SEQ, LANES), lambda b: (b, 4)), pl.BlockSpec((1, LANES), lambda b: (0, 0))],
        out_specs=pl.BlockSpec((SEQ, LANES), lambda b: (b, 0)), out_shape=jax.ShapeDtypeStruct((T, LANES), F32),
        compiler_params=_params("parallel"),
    )(aux, bf)


def fox_prep_bwd(aux, bf, dcum, *, name):
    T = aux.shape[0]

    def body(f_ref, b_ref, d_ref, df_ref, db_ref):
        row = lax.broadcasted_iota(jnp.int32, (CUM_BLK, CUM_BLK), 0)
        col = lax.broadcasted_iota(jnp.int32, (CUM_BLK, CUM_BLK), 1)
        onward = (col >= row).astype(BF16)
        carry = jnp.zeros((1, LANES), F32)
        tot = jnp.zeros((1, LANES), F32)
        for n in range(SEQ // CUM_BLK - 1, -1, -1):
            rows = slice(n * CUM_BLK, (n + 1) * CUM_BLK)
            d = d_ref[rows, :]
            hi = d.astype(BF16)
            lo = (d - hi.astype(F32)).astype(BF16)
            dlogf = _dot(onward, hi) + _dot(onward, lo) + carry
            carry = dlogf[0:1]
            df = dlogf * jax.nn.sigmoid(-(f_ref[rows, :] + b_ref[...]))
            df_ref[rows, :] = df
            tot = tot + jnp.sum(df, axis=0, keepdims=True)

        @pl.when(pl.program_id(0) == 0)
        def _():
            db_ref[...] = tot

        @pl.when(pl.program_id(0) > 0)
        def _():
            db_ref[...] += tot

    blk = pl.BlockSpec((SEQ, LANES), lambda b: (b, 0))
    vec = pl.BlockSpec((1, LANES), lambda b: (0, 0))
    sd = jax.ShapeDtypeStruct
    return pl.pallas_call(
        body, name=name, grid=(T // SEQ,),
        in_specs=[pl.BlockSpec((SEQ, LANES), lambda b: (b, 4)), vec, blk],
        out_specs=[blk, vec], out_shape=[sd((T, LANES), F32), sd((1, LANES), F32)],
        compiler_params=_params("arbitrary"),
    )(aux, bf, dcum)


def _fox_logits(q, k, cq, ck, mask):
    z = _dot(q, k, _NT) + cq - ck
    return z if mask is None else jnp.where(mask, z, NEG)


def fox_fwd(qkv, cumc, cumr, *, name):
    T = qkv.shape[0]

    def body(q_ref, k_ref, v_ref, cc_ref, cr_ref, o_ref, lse_ref, z_s):
        row, col = _iotas()
        causal = col <= row
        lse_ref[...] = jnp.zeros_like(lse_ref)

        def qblock(i, _):
            qs = [q_ref[_blk(i), _hs(h)] * Q_SCALE for h in HEADS]
            cqs = [cc_ref[_blk(i), h:h + 1] for h in HEADS]

            def logits(j, mask, ms):
                out = []
                for h in HEADS:
                    z = _fox_logits(qs[h], k_ref[_blk(j), _hs(h)], cqs[h], cr_ref[h:h + 1, _blk(j)], mask)
                    z_s[h, j] = z
                    out.append(jnp.maximum(ms[h], jnp.max(z, axis=1, keepdims=True)))
                return tuple(out)

            ms = logits(i, causal, (jnp.full((BLOCK, 1), NEG, F32),) * N_HEADS)
            ms = lax.fori_loop(0, i, lambda j, c: logits(j, None, c), ms)

            def values(j, carry):
                out = []
                for h in HEADS:
                    acc, l = carry[h]
                    p = jnp.exp(z_s[h, j] - ms[h])
                    out.append((acc + _dot(p.astype(BF16), v_ref[_blk(j), _hs(h)]), l + _rowsum(p)))
                return tuple(out)

            zero = (jnp.zeros((BLOCK, HEAD_DIM), F32), jnp.zeros((BLOCK, 1), F32))
            res = lax.fori_loop(0, i + 1, values, (zero,) * N_HEADS)
            for h in HEADS:
                acc, l = res[h]
                o_ref[_blk(i), _hs(h)] = (acc / l).astype(BF16)
                lse_ref[_blk(i), h:h + 1] = ms[h] + jnp.log(l)
            return 0

        lax.fori_loop(0, NB, qblock, 0)

    out = pl.BlockSpec((SEQ, GROUP_W), lambda b: (b, 0))
    colb = pl.BlockSpec((SEQ, LANES), lambda b: (b, 0))
    sd = jax.ShapeDtypeStruct
    return pl.pallas_call(
        body, name=name, grid=(T // SEQ,),
        in_specs=_qkv_specs(3) + [colb, pl.BlockSpec((8, SEQ), lambda b: (b, 0))],
        out_specs=[out, colb], out_shape=[sd((T, GROUP_W), BF16), sd((T, LANES), F32)],
        scratch_shapes=[pltpu.VMEM((N_HEADS, NB, BLOCK, BLOCK), F32)],
        compiler_params=_params("parallel"),
    )(qkv, qkv, qkv, cumc, cumr)


def fox_bwd(qkv, cumc, cumr, lse, dmixed, *, name):
    T = qkv.shape[0]

    def body(q_ref, k_ref, v_ref, cc_ref, cr_ref, lse_ref, do_ref, dq_ref, dk_ref, dv_ref, dcc_ref, dcr_ref, p_s, dp_s):
        row, col = _iotas()
        causal = col <= row
        dk_ref[...] = jnp.zeros_like(dk_ref)
        dv_ref[...] = jnp.zeros_like(dv_ref)
        dcc_ref[...] = jnp.zeros_like(dcc_ref)
        dcr_ref[...] = jnp.zeros_like(dcr_ref)

        def qblock(i, _):
            qs = [q_ref[_blk(i), _hs(h)] * Q_SCALE for h in HEADS]
            dos = [do_ref[_blk(i), _hs(h)].astype(BF16) for h in HEADS]
            cqs = [cc_ref[_blk(i), h:h + 1] for h in HEADS]
            lses = [lse_ref[_blk(i), h:h + 1] for h in HEADS]

            def probs(j, mask, deltas):
                out = []
                for h in HEADS:
                    z = _fox_logits(qs[h], k_ref[_blk(j), _hs(h)], cqs[h], cr_ref[h:h + 1, _blk(j)], mask)
                    p = jnp.exp(z - lses[h])
                    dp = _dot(dos[h], v_ref[_blk(j), _hs(h)], _NT)
                    p_s[h, j] = p
                    dp_s[h, j] = dp
                    out.append(deltas[h] + _rowsum(p * dp))
                return tuple(out)

            deltas = probs(i, causal, (jnp.zeros((BLOCK, 1), F32),) * N_HEADS)
            deltas = lax.fori_loop(0, i, lambda j, c: probs(j, None, c), deltas)

            def kblock(j, carry):
                out = []
                for h in HEADS:
                    dq, dcq = carry[h]
                    p = p_s[h, j]
                    ds = p * (dp_s[h, j] - deltas[h])
                    dsb = ds.astype(BF16)
                    dk_ref[_blk(j), _hs(h)] += _dot(dsb, qs[h], _TN)
                    dv_ref[_blk(j), _hs(h)] += _dot(p.astype(BF16), dos[h], _TN)
                    dcr_ref[h:h + 1, _blk(j)] -= jnp.sum(ds, axis=0, keepdims=True)
                    out.append((dq + _dot(dsb, k_ref[_blk(j), _hs(h)]), dcq + _rowsum(ds)))
                return tuple(out)

            zero = (jnp.zeros((BLOCK, HEAD_DIM), F32), jnp.zeros((BLOCK, 1), F32))
            res = lax.fori_loop(0, i + 1, kblock, (zero,) * N_HEADS)
            for h in HEADS:
                dq_ref[_blk(i), _hs(h)] = res[h][0] * Q_SCALE
                dcc_ref[_blk(i), h:h + 1] = res[h][1]
            return 0

        lax.fori_loop(0, NB, qblock, 0)

    out = pl.BlockSpec((SEQ, GROUP_W), lambda b: (b, 0))
    colb = pl.BlockSpec((SEQ, LANES), lambda b: (b, 0))
    rowb = pl.BlockSpec((8, SEQ), lambda b: (b, 0))
    sd = jax.ShapeDtypeStruct
    big = sd((T, GROUP_W), F32)
    return pl.pallas_call(
        body, name=name, grid=(T // SEQ,),
        in_specs=_qkv_specs(3) + [colb, rowb, colb, pl.BlockSpec((SEQ, GROUP_W), lambda b: (b, 1))],
        out_specs=[out, out, out, colb, rowb],
        out_shape=[big, big, big, sd((T, LANES), F32), sd((T // SEQ * 8, SEQ), F32)],
        scratch_shapes=[pltpu.VMEM((N_HEADS, NB, BLOCK, BLOCK), F32), pltpu.VMEM((N_HEADS, NB, BLOCK, BLOCK), F32)],
        compiler_params=_params("parallel"),
    )(qkv, qkv, qkv, cumc, cumr, lse, dmixed)


CHUNK = 256
WIDE = N_HEADS * CHUNK
NCH = SEQ // CHUNK


def _seg(h):
    return slice(h * CHUNK, (h + 1) * CHUNK)


def _chunk_rows(c):
    return pl.ds(pl.multiple_of(c * CHUNK, CHUNK), CHUNK)


def _wide_consts():
    r = lax.broadcasted_iota(jnp.int32, (WIDE, GROUP_W), 0)
    f = lax.broadcasted_iota(jnp.int32, (WIDE, GROUP_W), 1)
    bd = (r // CHUNK) == (f // HEAD_DIM)
    row = lax.broadcasted_iota(jnp.int32, (BLOCK, WIDE), 0)
    key = lax.broadcasted_iota(jnp.int32, (BLOCK, WIDE), 1) % CHUNK
    return bd, row, key


def _block_diag(x, bd):
    return jnp.where(bd, jnp.concatenate([x] * N_HEADS, axis=0), jnp.zeros((), x.dtype))


def _fold_heads(w, bd):
    w = jnp.where(bd, w, 0.0)
    return (w[0:CHUNK] + w[CHUNK:2 * CHUNK]) + (w[2 * CHUNK:3 * CHUNK] + w[3 * CHUNK:])


def _widen(cols):
    return jnp.concatenate([jnp.broadcast_to(c, (BLOCK, CHUNK)) for c in cols], axis=1)


def _head_rowsums(w):
    return [jnp.sum(w[:, _seg(h)], axis=1, keepdims=True) for h in HEADS]


def _tri_wide(x, tri):
    hi = x.astype(BF16)
    lo = (x - hi.astype(F32)).astype(BF16)
    y = _dot(jnp.concatenate([hi[:, _seg(h)] for h in HEADS] + [lo[:, _seg(h)] for h in HEADS], axis=0), tri)
    return jnp.concatenate([y[h * BLOCK:(h + 1) * BLOCK] + y[(N_HEADS + h) * BLOCK:(N_HEADS + h + 1) * BLOCK] for h in HEADS], axis=1)


def _feature_widen(cols):
    return jnp.concatenate([jnp.broadcast_to(c, (BLOCK, HEAD_DIM)) for c in cols], axis=1)


def _sbw_tile(q, kbd, mask, later, csum):
    z = _dot(q, kbd, _NT)
    lk = -_softplus(z)
    if mask is not None:
        lk = jnp.where(mask, lk, 0.0)
    e = z + lk
    att = jnp.exp(e + _tri_wide(lk, later) + csum)
    if mask is not None:
        att = jnp.where(mask, att, 0.0)
    return att, e, lk


def sbw_fwd(qkv, *, name):
    T = qkv.shape[0]

    def body(q_ref, k_ref, v_ref, o_ref):
        bd, row, key = _wide_consts()
        r2 = lax.broadcasted_iota(jnp.int32, (CHUNK, CHUNK), 0)
        c2 = lax.broadcasted_iota(jnp.int32, (CHUNK, CHUNK), 1)
        later = (r2 > c2).astype(BF16)

        def qblock(i, _):
            q = q_ref[_blk(i), :] * Q_SCALE
            cd = i // 2
            strict = key < row + BLOCK * (i % 2)

            def tile(c, mask, carry):
                acc, csum = carry
                att, _, lk = _sbw_tile(q, _block_diag(k_ref[_chunk_rows(c), :], bd), mask, later, csum)
                acc = acc + _dot(att.astype(BF16), _block_diag(v_ref[_chunk_rows(c), :], bd))
                return acc, csum + _widen(_head_rowsums(lk))

            carry = tile(cd, strict, (jnp.zeros((BLOCK, GROUP_W), F32), jnp.zeros((BLOCK, WIDE), F32)))
            acc, _ = lax.fori_loop(0, cd, lambda n, cr: tile(cd - 1 - n, None, cr), carry)
            o_ref[_blk(i), :] = acc.astype(BF16)
            return 0

        lax.fori_loop(0, NB, qblock, 0)

    return pl.pallas_call(
        body, name=name, grid=(T // SEQ,), in_specs=_qkv_specs(0),
        out_specs=pl.BlockSpec((SEQ, GROUP_W), lambda b: (b, 0)), out_shape=jax.ShapeDtypeStruct((T, GROUP_W), BF16),
        compiler_params=_params("parallel"),
    )(qkv, qkv, qkv)


def sbw_bwd(qkv, dmixed, *, name):
    T = qkv.shape[0]

    def body(q_ref, k_ref, v_ref, do_ref, dq_ref, dk_ref, dv_ref, att_s, sg_s):
        bd, row, key = _wide_consts()
        r2 = lax.broadcasted_iota(jnp.int32, (CHUNK, CHUNK), 0)
        c2 = lax.broadcasted_iota(jnp.int32, (CHUNK, CHUNK), 1)
        later = (r2 > c2).astype(BF16)
        earlier = (r2 < c2).astype(BF16)
        dk_ref[...] = jnp.zeros_like(dk_ref)
        dv_ref[...] = jnp.zeros_like(dv_ref)

        def qblock(i, _):
            q = q_ref[_blk(i), :] * Q_SCALE
            do = do_ref[_blk(i), :].astype(BF16)
            cd = i // 2
            strict = key < row + BLOCK * (i % 2)

            def recompute(c, mask, csum):
                att, e, lk = _sbw_tile(q, _block_diag(k_ref[_chunk_rows(c), :], bd), mask, later, csum)
                sg = jnp.exp(e)
                att_s[c] = att
                sg_s[c] = sg if mask is None else jnp.where(mask, sg, 0.0)
                return csum + _widen(_head_rowsums(lk))

            csum = recompute(cd, strict, jnp.zeros((BLOCK, WIDE), F32))
            lax.fori_loop(0, cd, lambda n, cs: recompute(cd - 1 - n, None, cs), csum)

            def tile(c, carry):
                dq, pre = carry
                kbd = _block_diag(k_ref[_chunk_rows(c), :], bd)
                vbd = _block_diag(v_ref[_chunk_rows(c), :], bd)
                att = att_s[c]
                ds = _dot(do, vbd, _NT) * att
                dlk = ds + _tri_wide(ds, earlier) + pre
                dz = (ds - dlk * sg_s[c]).astype(BF16)
                dk_ref[_chunk_rows(c), :] += _fold_heads(_dot(dz, q, _TN), bd)
                dv_ref[_chunk_rows(c), :] += _fold_heads(_dot(att.astype(BF16), do, _TN), bd)
                return dq + _dot(dz, kbd), pre + _widen(_head_rowsums(ds))

            dq, _ = lax.fori_loop(0, cd + 1, tile, (jnp.zeros((BLOCK, GROUP_W), F32), jnp.zeros((BLOCK, WIDE), F32)))
            dq_ref[_blk(i), :] = dq * Q_SCALE
            return 0

        lax.fori_loop(0, NB, qblock, 0)

    out = pl.BlockSpec((SEQ, GROUP_W), lambda b: (b, 0))
    sd = jax.ShapeDtypeStruct((T, GROUP_W), F32)
    return pl.pallas_call(
        body, name=name, grid=(T // SEQ,), in_specs=_qkv_specs(0) + [out],
        out_specs=[out] * 3, out_shape=[sd] * 3,
        scratch_shapes=[pltpu.VMEM((NCH, BLOCK, WIDE), F32), pltpu.VMEM((NCH, BLOCK, WIDE), F32)],
        compiler_params=_params("parallel"),
    )(qkv, qkv, qkv, dmixed)


def _foxw_logits(q, kbd, cq, cr_ref, c, mask):
    ck = jnp.concatenate([cr_ref[h:h + 1, _chunk_rows(c)] for h in HEADS], axis=1)
    z = _dot(q, kbd, _NT) + cq - ck
    return z if mask is None else jnp.where(mask, z, NEG)


def foxw_fwd(qkv, cumc, cumr, *, name):
    T = qkv.shape[0]

    def body(q_ref, k_ref, v_ref, cc_ref, cr_ref, o_ref, o32_ref, lse_ref, z_s):
        bd, row, key = _wide_consts()
        lse_ref[...] = jnp.zeros_like(lse_ref)

        def qblock(i, _):
            q = q_ref[_blk(i), :] * Q_SCALE
            cq = _widen([cc_ref[_blk(i), h:h + 1] for h in HEADS])
            cd = i // 2
            causal = key <= row + BLOCK * (i % 2)

            def logits(c, mask, ms):
                z = _foxw_logits(q, _block_diag(k_ref[_chunk_rows(c), :], bd), cq, cr_ref, c, mask)
                z_s[c] = z
                return tuple(jnp.maximum(ms[h], jnp.max(z[:, _seg(h)], axis=1, keepdims=True)) for h in HEADS)

            ms = logits(cd, causal, (jnp.full((BLOCK, 1), NEG, F32),) * N_HEADS)
            ms = lax.fori_loop(0, cd, lambda c, m: logits(c, None, m), ms)
            m_wide = _widen(ms)

            def values(c, carry):
                acc, l = carry
                p = jnp.exp(z_s[c] - m_wide)
                return acc + _dot(p.astype(BF16), _block_diag(v_ref[_chunk_rows(c), :], bd)), l + _widen(_head_rowsums(p))

            acc, l = lax.fori_loop(0, cd + 1, values, (jnp.zeros((BLOCK, GROUP_W), F32), jnp.zeros((BLOCK, WIDE), F32)))
            ls = [l[:, h * CHUNK:h * CHUNK + 1] for h in HEADS]
            o = acc / _feature_widen(ls)
            o_ref[_blk(i), :] = o.astype(BF16)
            o32_ref[_blk(i), :] = o
            for h in HEADS:
                lse_ref[_blk(i), h:h + 1] = ms[h] + jnp.log(ls[h])
            return 0

        lax.fori_loop(0, NB, qblock, 0)

    out = pl.BlockSpec((SEQ, GROUP_W), lambda b: (b, 0))
    colb = pl.BlockSpec((SEQ, LANES), lambda b: (b, 0))
    sd = jax.ShapeDtypeStruct
    return pl.pallas_call(
        body, name=name, grid=(T // SEQ,),
        in_specs=_qkv_specs(3) + [colb, pl.BlockSpec((8, SEQ), lambda b: (b, 0))],
        out_specs=[out, out, colb], out_shape=[sd((T, GROUP_W), BF16), sd((T, GROUP_W), F32), sd((T, LANES), F32)],
        scratch_shapes=[pltpu.VMEM((NCH, BLOCK, WIDE), F32)],
        compiler_params=_params("parallel"),
    )(qkv, qkv, qkv, cumc, cumr)


def foxw_bwd(qkv, cumc, cumr, lse, o32, dmixed, *, name):
    T = qkv.shape[0]

    def body(q_ref, k_ref, v_ref, cc_ref, cr_ref, lse_ref, o_ref, do_ref, dq_ref, dk_ref, dv_ref, dcc_ref, dcr_ref):
        bd, row, key = _wide_consts()
        dk_ref[...] = jnp.zeros_like(dk_ref)
        dv_ref[...] = jnp.zeros_like(dv_ref)
        dcc_ref[...] = jnp.zeros_like(dcc_ref)
        dcr_ref[...] = jnp.zeros_like(dcr_ref)

        def qblock(i, _):
            q = q_ref[_blk(i), :] * Q_SCALE
            do32 = do_ref[_blk(i), :]
            do = do32.astype(BF16)
            prod = do32 * o_ref[_blk(i), :]
            delta = _widen([jnp.sum(prod[:, _hs(h)], axis=1, keepdims=True) for h in HEADS])
            cq = _widen([cc_ref[_blk(i), h:h + 1] for h in HEADS])
            lse_w = _widen([lse_ref[_blk(i), h:h + 1] for h in HEADS])
            cd = i // 2
            causal = key <= row + BLOCK * (i % 2)

            def tile(c, mask, carry):
                dq, dcq = carry
                kbd = _block_diag(k_ref[_chunk_rows(c), :], bd)
                vbd = _block_diag(v_ref[_chunk_rows(c), :], bd)
                p = jnp.exp(_foxw_logits(q, kbd, cq, cr_ref, c, mask) - lse_w)
                ds = p * (_dot(do, vbd, _NT) - delta)
                dsb = ds.astype(BF16)
                dk_ref[_chunk_rows(c), :] += _fold_heads(_dot(dsb, q, _TN), bd)
                dv_ref[_chunk_rows(c), :] += _fold_heads(_dot(p.astype(BF16), do, _TN), bd)
                for h in HEADS:
                    dcr_ref[h:h + 1, _chunk_rows(c)] -= jnp.sum(ds[:, _seg(h)], axis=0, keepdims=True)
                return dq + _dot(dsb, kbd), dcq + _widen(_head_rowsums(ds))

            carry = tile(cd, causal, (jnp.zeros((BLOCK, GROUP_W), F32), jnp.zeros((BLOCK, WIDE), F32)))
            dq, dcq = lax.fori_loop(0, cd, lambda c, cr: tile(c, None, cr), carry)
            dq_ref[_blk(i), :] = dq * Q_SCALE
            for h in HEADS:
                dcc_ref[_blk(i), h:h + 1] = dcq[:, h * CHUNK:h * CHUNK + 1]
            return 0

        lax.fori_loop(0, NB, qblock, 0)

    out = pl.BlockSpec((SEQ, GROUP_W), lambda b: (b, 0))
    colb = pl.BlockSpec((SEQ, LANES), lambda b: (b, 0))
    rowb = pl.BlockSpec((8, SEQ), lambda b: (b, 0))
    sd = jax.ShapeDtypeStruct
    big = sd((T, GROUP_W), F32)
    return pl.pallas_call(
        body, name=name, grid=(T // SEQ,),
        in_specs=_qkv_specs(3) + [colb, rowb, colb, out, pl.BlockSpec((SEQ, GROUP_W), lambda b: (b, 1))],
        out_specs=[out, out, out, colb, rowb],
        out_shape=[big, big, big, sd((T, LANES), F32), sd((T // SEQ * 8, SEQ), F32)],
        compiler_params=_params("parallel"),
    )(qkv, qkv, qkv, cumc, cumr, lse, o32, dmixed)


BAND = 2 * BLOCK


def _t5_bucket_np(dist):
    n = np.maximum(dist, 0)
    max_exact = NUM_BUCKETS // 2
    nf = np.maximum(n, 1).astype(np.float32)
    large = max_exact + (np.log(nf / np.float32(max_exact)) / np.float32(math.log(MAX_DISTANCE / max_exact))
                         * np.float32(NUM_BUCKETS - max_exact)).astype(np.int32)
    large = np.minimum(large, NUM_BUCKETS - 1)
    return np.where(n < max_exact, n, large).astype(np.int32)


def _band_buckets():
    qi = np.arange(BLOCK)[:, None]
    ki = np.arange(BAND)[None, :]
    delta = np.clip(qi - ki + BLOCK, 0, BLOCK)
    return np.stack([_t5_bucket_np(delta * d) for d in DILATIONS])


def to_classes(a, d):
    if d == 1:
        return a
    T, C = a.shape
    return a.reshape(T // SEQ, SEQ // d, d, C).transpose(0, 2, 1, 3).reshape(T, C)


def from_classes(a, d):
    if d == 1:
        return a
    T, C = a.shape
    return a.reshape(T // SEQ, d, SEQ // d, C).transpose(0, 2, 1, 3).reshape(T, C)


def relbias_expand(rel, *, name):
    buckets = jnp.asarray(_band_buckets())
    n_pat = len(DILATIONS)

    def body(rel_ref, bk_ref, o_ref):
        for p in range(n_pat):
            bk = bk_ref[p]
            for h in range(N_HEADS):
                acc = jnp.zeros((BLOCK, BAND), F32)
                for b in range(NUM_BUCKETS):
                    acc = jnp.where(bk == b, rel_ref[b, h], acc)
                o_ref[p * N_HEADS + h] = acc

    return pl.pallas_call(
        body, name=name,
        in_specs=[pl.BlockSpec(memory_space=pltpu.SMEM), pl.BlockSpec(memory_space=pltpu.VMEM)],
        out_specs=pl.BlockSpec(memory_space=pltpu.VMEM),
        out_shape=jax.ShapeDtypeStruct((n_pat * N_HEADS, BLOCK, BAND), F32),
        compiler_params=_params(),
    )(rel, buckets)


def relbias_reduce(ds_all, *, name):
    buckets = jnp.asarray(_band_buckets())
    n_pat = len(DILATIONS)

    def body(ds_ref, bk_ref, o_ref):
        for b in range(NUM_BUCKETS):
            for h in range(N_HEADS):
                tot = jnp.float32(0.0)
                for p in range(n_pat):
                    tot = tot + jnp.sum(jnp.where(bk_ref[p] == b, ds_ref[p * N_HEADS + h], 0.0))
                o_ref[b, h] = tot

    return pl.pallas_call(
        body, name=name,
        in_specs=[pl.BlockSpec(memory_space=pltpu.VMEM), pl.BlockSpec(memory_space=pltpu.VMEM)],
        out_specs=pl.BlockSpec(memory_space=pltpu.SMEM),
        out_shape=jax.ShapeDtypeStruct((NUM_BUCKETS, N_HEADS), F32),
        compiler_params=_params(),
    )(ds_all, buckets)


def _band_valid_wide(first, row, key):
    inside = jnp.logical_and(key >= row, key <= row + BLOCK)
    return jnp.logical_and(inside, jnp.logical_or(jnp.logical_not(first), key >= BLOCK))


QKV_BLOCKS = 9


def class_view(a, d):
    T, C = a.shape
    return a.reshape(T // d, d * C)


def _class_block(width, col=0, per_class=1):
    return pl.BlockSpec((BLOCK, width), lambda r, m: (m, r * per_class + col))


def _band_specs(pattern):
    cur = lambda c: _class_block(GROUP_W, c, QKV_BLOCKS)
    prev = lambda c: pl.BlockSpec((BLOCK, GROUP_W), lambda r, m: (jnp.maximum(m - 1, 0), r * QKV_BLOCKS + c))
    bias = pl.BlockSpec((N_HEADS, BLOCK, BAND), lambda r, m: (pattern, 0, 0))
    return [cur(6), prev(7), cur(7), prev(8), cur(8), bias]


def band_fwd(qkv, bias, pattern, *, name):
    T = qkv.shape[0]
    d = DILATIONS[pattern]
    seq_blocks = SEQ // d // BLOCK
    qv = class_view(qkv, d)

    def body(q_ref, kp_ref, kc_ref, vp_ref, vc_ref, b_ref, o_ref, lse_ref):
        bd, row, key = _wide_consts()
        valid = _band_valid_wide(pl.program_id(1) % seq_blocks == 0, row, key)
        q = q_ref[...] * Q_SCALE
        kbd = _block_diag(jnp.concatenate([kp_ref[...], kc_ref[...]], axis=0), bd)
        vbd = _block_diag(jnp.concatenate([vp_ref[...], vc_ref[...]], axis=0), bd)
        bias = jnp.concatenate([b_ref[h] for h in HEADS], axis=1)
        sc = jnp.where(valid, _dot(q, kbd, _NT) + bias, NEG)
        ms = [jnp.max(sc[:, _seg(h)], axis=1, keepdims=True) for h in HEADS]
        p = jnp.exp(sc - _widen(ms))
        ls = _head_rowsums(p)
        o_ref[...] = _dot(p.astype(BF16), vbd) / _feature_widen(ls)
        lse_ref[...] = jnp.zeros_like(lse_ref)
        for h in HEADS:
            lse_ref[:, h:h + 1] = ms[h] + jnp.log(ls[h])

    sd = jax.ShapeDtypeStruct
    o, lse = pl.pallas_call(
        body, name=name, grid=(d, T // d // BLOCK), in_specs=_band_specs(pattern),
        out_specs=[_class_block(GROUP_W), _class_block(LANES)],
        out_shape=[sd((T // d, d * GROUP_W), F32), sd((T // d, d * LANES), F32)],
        compiler_params=_params("parallel", "parallel"),
    )(qv, qv, qv, qv, qv, bias)
    return o.reshape(T, GROUP_W), lse.reshape(T, LANES)


def band_bwd(qkv, bias, lse, do, dlse, pattern, *, name):
    T = qkv.shape[0]
    d = DILATIONS[pattern]
    seq_blocks = SEQ // d // BLOCK
    qv = class_view(qkv, d)

    def body(q_ref, kp_ref, kc_ref, vp_ref, vc_ref, b_ref, lse_ref, do_ref, dlse_ref,
             dq_ref, dkc_ref, dkp_ref, dvc_ref, dvp_ref, ds_ref):
        first_step = jnp.logical_and(pl.program_id(0) == 0, pl.program_id(1) == 0)
        bd, row, key = _wide_consts()
        valid = _band_valid_wide(pl.program_id(1) % seq_blocks == 0, row, key)
        q = q_ref[...] * Q_SCALE
        do = do_ref[...].astype(BF16)
        kbd = _block_diag(jnp.concatenate([kp_ref[...], kc_ref[...]], axis=0), bd)
        vbd = _block_diag(jnp.concatenate([vp_ref[...], vc_ref[...]], axis=0), bd)
        bias = jnp.concatenate([b_ref[h] for h in HEADS], axis=1)
        lse_w = _widen([lse_ref[:, h:h + 1] for h in HEADS])
        dlse_w = _widen([dlse_ref[:, h:h + 1] for h in HEADS])
        p = jnp.where(valid, jnp.exp(_dot(q, kbd, _NT) + bias - lse_w), 0.0)
        dp = _dot(do, vbd, _NT)
        ds = p * (dp - _widen(_head_rowsums(p * dp)) + dlse_w)
        dsb, pb = ds.astype(BF16), p.astype(BF16)
        dq_ref[...] = _dot(dsb, kbd) * Q_SCALE
        dk = _fold_heads(_dot(dsb, q, _TN), bd)
        dv = _fold_heads(_dot(pb, do, _TN), bd)
        dkp_ref[...] = dk[:BLOCK]
        dkc_ref[...] = dk[BLOCK:]
        dvp_ref[...] = dv[:BLOCK]
        dvc_ref[...] = dv[BLOCK:]

        @pl.when(first_step)
        def _():
            for h in HEADS:
                ds_ref[h] = ds[:, _seg(h)]

        @pl.when(jnp.logical_not(first_step))
        def _():
            for h in HEADS:
                ds_ref[h] += ds[:, _seg(h)]

    big, colb = _class_block(GROUP_W), _class_block(LANES)
    sd = jax.ShapeDtypeStruct
    return pl.pallas_call(
        body, name=name, grid=(d, T // d // BLOCK), in_specs=_band_specs(pattern) + [colb, big, colb],
        out_specs=[big] * 5 + [pl.BlockSpec((N_HEADS, BLOCK, BAND), lambda r, m: (0, 0, 0))],
        out_shape=[sd((T // d, d * GROUP_W), F32)] * 5 + [sd((N_HEADS, BLOCK, BAND), F32)],
        compiler_params=_params("arbitrary", "arbitrary"),
    )(qv, qv, qv, qv, qv, bias, class_view(lse, d), class_view(do, d), class_view(dlse, d))


def shift_add(cur, prev, d, *, name):
    nb = cur.shape[0] // BLOCK

    def body(c_ref, p_ref, o_ref):
        keep = (pl.program_id(1) < nb - 1).astype(F32)
        o_ref[...] = c_ref[...] + keep * p_ref[...]

    blk = _class_block(GROUP_W)
    nxt = pl.BlockSpec((BLOCK, GROUP_W), lambda r, m: (jnp.minimum(m + 1, nb - 1), r))
    out = pl.pallas_call(
        body, name=name, grid=(d, nb), in_specs=[blk, nxt], out_specs=blk,
        out_shape=jax.ShapeDtypeStruct(cur.shape, F32), compiler_params=_params("parallel", "parallel"),
    )(cur, prev)
    return out.reshape(-1, GROUP_W)


def _pattern_weights(lse_refs, h):
    ls = [r[:, h:h + 1] for r in lse_refs]
    mx = functools.reduce(jnp.maximum, ls)
    es = [jnp.exp(l - mx) for l in ls]
    tot = functools.reduce(lambda a, b: a + b, es)
    return [e / tot for e in es]


def dil_combine_fwd(outs, *, name):
    T = outs[0][0].shape[0]
    n = len(outs)
    tm = 512

    def body(*refs):
        o_refs, l_refs, out_ref = refs[:n], refs[n:2 * n], refs[2 * n]
        for h in range(N_HEADS):
            w = _pattern_weights(l_refs, h)
            acc = w[0] * o_refs[0][:, _hs(h)]
            for p in range(1, n):
                acc = acc + w[p] * o_refs[p][:, _hs(h)]
            out_ref[:, _hs(h)] = acc.astype(BF16)

    big = pl.BlockSpec((tm, GROUP_W), lambda i: (i, 0))
    colb = pl.BlockSpec((tm, LANES), lambda i: (i, 0))
    return pl.pallas_call(
        body, name=name, grid=(T // tm,), in_specs=[big] * n + [colb] * n,
        out_specs=big, out_shape=jax.ShapeDtypeStruct((T, GROUP_W), BF16),
        compiler_params=_params("parallel"),
    )(*[o for o, _ in outs], *[l for _, l in outs])


def dil_combine_bwd(outs, dmixed, *, name):
    T = outs[0][0].shape[0]
    n = len(outs)
    tm = 512

    def body(*refs):
        o_refs, l_refs, do_ref = refs[:n], refs[n:2 * n], refs[2 * n]
        do_refs, dl_refs = refs[2 * n + 1:3 * n + 1], refs[3 * n + 1:]
        for r in dl_refs:
            r[...] = jnp.zeros_like(r)
        for h in range(N_HEADS):
            w = _pattern_weights(l_refs, h)
            do = do_ref[:, _hs(h)]
            dw = [jnp.sum(do * o_refs[p][:, _hs(h)], axis=1, keepdims=True) for p in range(n)]
            mean = functools.reduce(lambda a, b: a + b, [w[p] * dw[p] for p in range(n)])
            for p in range(n):
                do_refs[p][:, _hs(h)] = w[p] * do
                dl_refs[p][:, h:h + 1] = w[p] * (dw[p] - mean)

    big = pl.BlockSpec((tm, GROUP_W), lambda i: (i, 0))
    colb = pl.BlockSpec((tm, LANES), lambda i: (i, 0))
    sd = jax.ShapeDtypeStruct
    res = pl.pallas_call(
        body, name=name, grid=(T // tm,),
        in_specs=[big] * n + [colb] * n + [pl.BlockSpec((tm, GROUP_W), lambda i: (i, 2))],
        out_specs=[big] * n + [colb] * n, out_shape=[sd((T, GROUP_W), F32)] * n + [sd((T, LANES), F32)] * n,
        compiler_params=_params("parallel"),
    )(*[o for o, _ in outs], *[l for _, l in outs], dmixed)
    return list(zip(res[:n], res[n:]))


def dilated_fwd(qkv, bias, tag):
    return [band_fwd(qkv, bias, p, name=f"{tag}_band_fwd{p}") for p in range(len(DILATIONS))]


def dilated_bwd(qkv, bias, outs, dmixed, tag):
    grads = dil_combine_bwd(outs, dmixed, name=f"{tag}_combine_bwd")
    parts, ds_all = [], []
    for p, d in enumerate(DILATIONS):
        (_, lse), (do, dlse) = outs[p], grads[p]
        dq, dkc, dkp, dvc, dvp, ds = band_bwd(qkv, bias, lse, do, dlse, p, name=f"{tag}_band_bwd{p}")
        if SEQ // d == BLOCK:
            dk, dv = dkc.reshape(-1, GROUP_W), dvc.reshape(-1, GROUP_W)
        else:
            dk = shift_add(dkc, dkp, d, name=f"{tag}_dk{p}")
            dv = shift_add(dvc, dvp, d, name=f"{tag}_dv{p}")
        parts.append([dq.reshape(-1, GROUP_W), dk, dv])
        ds_all.append(ds)
    return parts, jnp.concatenate(ds_all, axis=0)


def assemble_dqkv(d_sb, d_fox, d_dil, *, name):
    T = d_sb[0].shape[0]
    tr = 512
    n_pat = len(d_dil)
    flat = list(d_sb) + list(d_fox) + [a for part in d_dil for a in part]

    def body(*refs):
        o_ref = refs[-1]
        for j in range(6):
            o_ref[:, j * GROUP_W:(j + 1) * GROUP_W] = refs[j][...].astype(BF16)
        for j in range(3):
            acc = refs[6 + j][...]
            for p in range(1, n_pat):
                acc = acc + refs[6 + 3 * p + j][...]
            o_ref[:, (6 + j) * GROUP_W:(7 + j) * GROUP_W] = acc.astype(BF16)

    blk = pl.BlockSpec((tr, GROUP_W), lambda i: (i, 0))
    return pl.pallas_call(
        body, name=name, grid=(T // tr,), in_specs=[blk] * len(flat),
        out_specs=pl.BlockSpec((tr, QKV_BLOCKS * GROUP_W), lambda i: (i, 0)),
        out_shape=jax.ShapeDtypeStruct((T, QKV_BLOCKS * GROUP_W), BF16), compiler_params=_params("parallel"),
    )(*flat)


def sum_cast(arrs, dtype, *, name):
    R, C = arrs[0].shape
    tr = _largest_tile(R, 512, 16)
    n = len(arrs)

    def body(*refs):
        acc = refs[0][...].astype(F32)
        for r in refs[1:n]:
            acc = acc + r[...].astype(F32)
        refs[n][...] = acc.astype(dtype)

    blk = pl.BlockSpec((tr, C), lambda i: (i, 0))
    return pl.pallas_call(
        body, name=name, grid=(R // tr,), in_specs=[blk] * n, out_specs=blk, out_shape=jax.ShapeDtypeStruct((R, C), dtype),
        compiler_params=_params("parallel"),
    )(*arrs)


GRAD_WIRE = BF16


def _block_diag_halves(w):
    z = jnp.zeros((HEAD_DIM, HEAD_DIM), w.dtype)
    half = lambda a, b: jnp.concatenate([jnp.concatenate([a, z], axis=1), jnp.concatenate([z, b], axis=1)], axis=0)
    return jnp.stack([half(w[0], w[1]), half(w[2], w[3])]).astype(BF16)


def _diag_blocks(d):
    h = HEAD_DIM
    return jnp.stack([d[0, :h, :h], d[0, h:, h:], d[1, :h, :h], d[1, h:, h:]])


def layer_fwd(x, mem2d, W, P, bias, tag):
    s = {}
    s["x"] = x
    h1 = rmsnorm_fwd(x, P["norm_mix_g"], name=f"{tag}_norm_mix")
    qkv = matmul(h1, W["qkv"], out_dtype=BF16, name=f"{tag}_qkv")
    aux = matmul(h1, W["aux"], name=f"{tag}_aux")
    o_sb = sbw_fwd(qkv, name=f"{tag}_sb_fwd")
    cumc = fox_prep(aux, P["bf"], name=f"{tag}_fox_prep")
    cumr = col_to_row(cumc)
    o_fox, o_fox32, lse_fox = foxw_fwd(qkv, cumc, cumr, name=f"{tag}_fox_fwd")
    dil = dilated_fwd(qkv, bias, tag)
    o_dil = dil_combine_fwd(dil, name=f"{tag}_dil_combine")
    o_lru, h_lru = lru_fwd(aux, P["lru_conv_w"], P["lru_conv_b"], P["wa"], P["lru_b_a"], P["wx"], P["lru_b_x"],
                           P["lru_lambda"], name=f"{tag}_lru_fwd")
    mixed = jnp.concatenate([o_sb, o_fox, o_dil, o_lru], axis=1)
    x1 = matmul(mixed, W["out"], residual=x, name=f"{tag}_out")
    hq = rmsnorm_fwd(x1, P["norm_cross_g"], name=f"{tag}_norm_cross")
    qc = matmul(hq, W["cq"], out_dtype=BF16, name=f"{tag}_cq")
    memn = rmsnorm_fwd(mem2d, P["norm_mem_g"], name=f"{tag}_norm_mem")
    kv = matmul(memn, W["ckv"], out_dtype=BF16, name=f"{tag}_ckv")
    oc = cross_fwd(qc, kv, name=f"{tag}_cross_fwd")
    x2 = matmul(oc, W["coT"], trans_b=True, residual=x1, name=f"{tag}_co")
    h2 = rmsnorm_fwd(x2, P["norm_ffn_g"], name=f"{tag}_norm_ffn")
    hu = matmul(h2, W["up_u"], trans_b=True, name=f"{tag}_up_u")
    hg = matmul(h2, W["up_g"], trans_b=True, name=f"{tag}_up_g")
    act = glu_fwd(hu, hg, P["wu"], P["wg"], P["bu"], P["bg"], name=f"{tag}_glu_fwd")
    x3 = matmul(act, W["down"], residual=x2, name=f"{tag}_down")
    s.update(h1=h1, qkv=qkv, aux=aux, cumc=cumc, cumr=cumr, lse_fox=lse_fox, o_fox32=o_fox32, dil=dil, h_lru=h_lru, mixed=mixed,
             x1=x1, hq=hq, qc=qc, memn=memn, kv=kv, oc=oc, x2=x2, h2=h2, hu=hu, hg=hg, act=act)
    return x3, s


def layer_bwd(dx3, mem2d, W, P, bias, s, tag):
    mm = functools.partial(matmul, out_dtype=GRAD_WIRE, trans_a=True)
    gW, gP = {}, {}
    dact = matmul(dx3, W["down"], trans_b=True, name=f"{tag}_d_act")
    gW["down"] = mm(s["act"], dx3, name=f"{tag}_g_down")
    dcu, dcg, dwu, dwg, dbu, dbg = glu_bwd(s["hu"], s["hg"], dact, P["wu"], P["wg"], P["bu"], P["bg"], name=f"{tag}_glu_bwd")
    gP["ffn_conv_w"] = jnp.concatenate([dwu, dwg], axis=1)
    gP["ffn_conv_b"] = jnp.concatenate([dbu, dbg], axis=1)
    dhu = conv3_transpose(dcu, P["wu"], name=f"{tag}_convT_u")
    dhg = conv3_transpose(dcg, P["wg"], name=f"{tag}_convT_g")
    dh2 = matmul(dhu, W["up_u"], name=f"{tag}_d_h2u")
    dh2 = matmul(dhg, W["up_g"], residual=dh2, name=f"{tag}_d_h2g")
    gW["up_u"] = mm(dhu, s["h2"], name=f"{tag}_g_up_u")
    gW["up_g"] = mm(dhg, s["h2"], name=f"{tag}_g_up_g")
    dx2, gP["norm_ffn_g"] = rmsnorm_bwd(s["x2"], P["norm_ffn_g"], dh2, dx3, name=f"{tag}_norm_ffn_bwd")
    doc = matmul(dx2, W["coT"], name=f"{tag}_d_oc")
    gW["coT"] = mm(dx2, s["oc"], name=f"{tag}_g_co")
    dqc, dkv = cross_bwd(s["qc"], s["kv"], doc, name=f"{tag}_cross_bwd")
    dhq = matmul(dqc, W["cq"], trans_b=True, name=f"{tag}_d_hq")
    gW["cq"] = mm(s["hq"], dqc, name=f"{tag}_g_cq")
    dmemn = matmul(dkv, W["ckv"], trans_b=True, name=f"{tag}_d_memn")
    gW["ckv"] = mm(s["memn"], dkv, name=f"{tag}_g_ckv")
    _, gP["norm_mem_g"] = rmsnorm_bwd(mem2d, P["norm_mem_g"], dmemn, None, name=f"{tag}_norm_mem_bwd")
    dx1, gP["norm_cross_g"] = rmsnorm_bwd(s["x1"], P["norm_cross_g"], dhq, dx2, name=f"{tag}_norm_cross_bwd")
    dmixed = matmul(dx1, W["out"], trans_b=True, name=f"{tag}_d_mixed")
    gW["out"] = mm(s["mixed"], dx1, name=f"{tag}_g_out")
    qkv, aux = s["qkv"], s["aux"]
    d_sb = sbw_bwd(qkv, dmixed, name=f"{tag}_sb_bwd")
    dfq, dfk, dfv, dcc, dcr = foxw_bwd(qkv, s["cumc"], s["cumr"], s["lse_fox"], s["o_fox32"], dmixed, name=f"{tag}_fox_bwd")
    dcum = sum_cast([dcc, row_to_col(dcr)], F32, name=f"{tag}_dcum")
    df, dbf = fox_prep_bwd(aux, P["bf"], dcum, name=f"{tag}_fox_prep_bwd")
    gP["b_forget"] = dbf[0, :N_HEADS]
    d_dil, ds_band = dilated_bwd(qkv, bias, s["dil"], dmixed, tag)
    dlx, dlg, dcw, dcb, dwa, dba, dwx, dbx, dlam = lru_bwd(
        aux, s["h_lru"], dmixed, P["lru_conv_w"], P["lru_conv_b"], P["wa"], P["lru_b_a"], P["wx"], P["lru_b_x"],
        P["lru_lambda"], name=f"{tag}_lru_bwd")
    gP.update(lru_conv_w=dcw, lru_conv_b=dcb, lru_w_a=_diag_blocks(dwa), lru_b_a=dba, lru_w_x=_diag_blocks(dwx),
              lru_b_x=dbx, lru_lambda=dlam)
    dqkv = assemble_dqkv(d_sb, [dfq, dfk, dfv], d_dil, name=f"{tag}_dqkv")
    daux = jnp.concatenate([dlx, dlg, df], axis=1)
    dh1 = matmul(dqkv, W["qkv"], trans_b=True, name=f"{tag}_d_h1a")
    dh1 = matmul(daux, W["aux"], trans_b=True, residual=dh1, name=f"{tag}_d_h1b")
    gW["qkv"] = mm(s["h1"], dqkv, name=f"{tag}_g_qkv")
    gW["aux"] = mm(s["h1"], daux, name=f"{tag}_g_aux")
    dx, gP["norm_mix_g"] = rmsnorm_bwd(s["x"], P["norm_mix_g"], dh1, dx1, name=f"{tag}_norm_mix_bwd")
    return dx, gW, gP, ds_band


def local_step(x, mem, target, Ws, Ps, rel_bias, final_norm_g):
    B = x.shape[0]
    x2d = x.reshape(B * SEQ, D_MODEL)
    mem2d = mem.reshape(B * N_MEM, D_MODEL)
    bias = relbias_expand(rel_bias, name="relbias_expand")
    saved = []
    h = x2d
    for l in range(DEPTH):
        h, s = layer_fwd(h, mem2d, Ws[l], Ps[l], bias, f"l{l}")
        saved.append(s)
    loss, dh, d_final = loss_head(h, final_norm_g, target.reshape(B * SEQ, D_MODEL), name="loss_head")
    gWs, gPs, ds_bands = [None] * DEPTH, [None] * DEPTH, []
    for l in range(DEPTH - 1, -1, -1):
        dh, gWs[l], gPs[l], ds = layer_bwd(dh, mem2d, Ws[l], Ps[l], bias, saved[l], f"l{l}")
        ds_bands.append(ds)
    d_rel = relbias_reduce(sum_cast([d.reshape(-1, BAND) for d in ds_bands], F32, name="ds_band_sum").reshape(-1, BLOCK, BAND),
                           name="relbias_reduce")
    return loss, dh.reshape(B, SEQ, D_MODEL), gWs, gPs, d_rel, d_final


def small_params(p, l):
    row = lambda name: p[name][l].reshape(1, -1)
    ffn_w, ffn_b = p["ffn_conv_w"][l], row("ffn_conv_b")
    return dict(
        norm_mix_g=row("norm_mix_g"), norm_cross_g=row("norm_cross_g"), norm_mem_g=row("norm_mem_g"), norm_ffn_g=row("norm_ffn_g"),
        bf=jnp.pad(row("b_forget"), ((0, 0), (0, LANES - N_HEADS))),
        lru_conv_w=p["lru_conv_w"][l], lru_conv_b=row("lru_conv_b"), wa=_block_diag_halves(p["lru_w_a"][l]), lru_b_a=row("lru_b_a"),
        wx=_block_diag_halves(p["lru_w_x"][l]), lru_b_x=row("lru_b_x"), lru_lambda=row("lru_lambda"),
        wu=ffn_w[:, :D_FF], wg=ffn_w[:, D_FF:], bu=ffn_b[:, :D_FF], bg=ffn_b[:, D_FF:])


def canonical_weights(w_in, w_out, w_cq, w_ck, w_cv, w_co, w_up, w_down):
    sb_fox, fox_f, rest = w_in[:, :6 * GROUP_W], w_in[:, 6 * GROUP_W:6 * GROUP_W + N_HEADS], w_in[:, 6 * GROUP_W + N_HEADS:]
    dil, lru = rest[:, :3 * GROUP_W], rest[:, 3 * GROUP_W:]
    pad = jnp.zeros((w_in.shape[0], AUX_W - 2 * GROUP_W - N_HEADS), w_in.dtype)
    return dict(qkv=jnp.concatenate([sb_fox, dil], axis=1), aux=jnp.concatenate([lru, fox_f, pad], axis=1), out=w_out,
                cq=w_cq, ckv=jnp.concatenate([w_ck, w_cv], axis=1), coT=w_co.T, upT=w_up.T, down=w_down)


def native_grads(g):
    qkv, aux = g["qkv"], g["aux"]
    a, b = 6 * GROUP_W, 6 * GROUP_W + N_HEADS
    w_in = jnp.zeros((qkv.shape[0], b + 5 * GROUP_W), qkv.dtype)
    w_in = w_in.at[:, :a].set(qkv[:, :a]).at[:, a:b].set(aux[:, 2 * GROUP_W:2 * GROUP_W + N_HEADS])
    w_in = w_in.at[:, b:b + 3 * GROUP_W].set(qkv[:, a:]).at[:, b + 3 * GROUP_W:].set(aux[:, :2 * GROUP_W])
    return (w_in, g["out"], g["cq"], g["ckv"][:, :GROUP_W], g["ckv"][:, GROUP_W:], g["coT"].T, g["upT"].T, g["down"])


ANY = pl.BlockSpec(memory_space=pl.ANY)
VMEM_SPEC = pl.BlockSpec(memory_space=pltpu.VMEM)


def _place():
    x, y, c = lax.axis_index("x"), lax.axis_index("y"), lax.axis_index("c")
    other_chips = [(1 - x, y), (x, 1 - y), (1 - x, 1 - y)]
    return x, y, c, other_chips


def _gather_body(x_ref, out_ref, send_sems, recv_sems, local_sem):
    x, y, c, chips = _place()
    me, sibling = (x, y, c), (x, y, 1 - c)

    def slot(px, py, pc):
        return out_ref.at[4 * px + 2 * py + pc]

    def copy(k, block, to, src=None):
        return pltpu.make_async_remote_copy(
            src_ref=slot(*block) if src is None else src, dst_ref=slot(*block),
            send_sem=send_sems.at[k], recv_sem=recv_sems.at[k], device_id=to, device_id_type=MESH)

    if local_sem is not None:
        mine = pltpu.make_async_copy(x_ref, slot(*me), local_sem)
        mine.start()
    first = [copy(0, me, sibling, src=x_ref)]
    first += [copy(1 + j, me, (*chip, c), src=x_ref) for j, chip in enumerate(chips)]
    for cp in first:
        cp.start()
    passed = [copy(4 + j, (*chip, c), sibling) for j, chip in enumerate(chips)]
    for j, chip in enumerate(chips):
        copy(1 + j, (*chip, c), me).wait_recv()
        passed[j].start()
    copy(0, sibling, me).wait_recv()
    for j, chip in enumerate(chips):
        copy(4 + j, (*chip, 1 - c), me).wait_recv()
    for cp in first + passed:
        cp.wait_send()
    if local_sem is not None:
        mine.wait()


_GATHER_SEMS = [pltpu.SemaphoreType.DMA((7,)), pltpu.SemaphoreType.DMA((7,)), pltpu.SemaphoreType.DMA]


def allgather_hbm(shard, me, *, name):
    def body(x_ref, out_ref, send_sems, recv_sems):
        _gather_body(x_ref, out_ref, send_sems, recv_sems, None)

    others = pl.pallas_call(
        body, name=name, in_specs=[ANY], out_specs=ANY,
        out_shape=jax.ShapeDtypeStruct((N_DEV,) + shard.shape, shard.dtype), scratch_shapes=_GATHER_SEMS[:2],
    )(shard)
    return lax.dynamic_update_slice(others, shard[None], (me, 0, 0))


def allgather_small(x, *, name, reduce=False):
    def body(x_ref, out_ref, *rest):
        _gather_body(x_ref, out_ref, *rest[-3:])
        if reduce:
            acc = out_ref[0]
            for d in range(1, N_DEV):
                acc = acc + out_ref[d]
            rest[0][...] = acc

    sd = jax.ShapeDtypeStruct
    return pl.pallas_call(
        body, name=name, in_specs=[VMEM_SPEC], out_specs=[VMEM_SPEC, VMEM_SPEC] if reduce else VMEM_SPEC,
        out_shape=[sd((N_DEV,) + x.shape, x.dtype), sd(x.shape, x.dtype)] if reduce else sd((N_DEV,) + x.shape, x.dtype),
        scratch_shapes=_GATHER_SEMS, compiler_params=pltpu.CompilerParams(vmem_limit_bytes=VMEM_LIMIT_V7X),
    )(x)


N_CHIPS = 4


def pair_exchange(g, *, name):
    _, R, C = g.shape

    def body(g_ref, recv_ref, send_sems, recv_sems):
        x, y, c, _ = _place()
        sibling = (x, y, 1 - c)
        remote = [pltpu.make_async_remote_copy(
            src_ref=g_ref.at[2 * q + (1 - c)], dst_ref=recv_ref.at[q], send_sem=send_sems.at[q], recv_sem=recv_sems.at[q],
            device_id=sibling, device_id_type=MESH) for q in range(N_CHIPS)]
        for cp in remote:
            cp.start()
        for cp in remote:
            cp.wait_recv()
        for cp in remote:
            cp.wait_send()

    return pl.pallas_call(
        body, name=name, in_specs=[ANY], out_specs=ANY, out_shape=jax.ShapeDtypeStruct((N_CHIPS, R, C), g.dtype),
        scratch_shapes=[pltpu.SemaphoreType.DMA((N_CHIPS,))] * 2,
    )(g)


def chip_exchange(s, *, name):
    _, R, C = s.shape

    def body(s_ref, o0, o1, o2, send_sems, recv_sems):
        x, y, c, chips = _place()
        outs = (o0, o1, o2)
        copies = [pltpu.make_async_remote_copy(
            src_ref=s_ref.at[2 * cx + cy], dst_ref=outs[j], send_sem=send_sems.at[j], recv_sem=recv_sems.at[j],
            device_id=(cx, cy, c), device_id_type=MESH) for j, (cx, cy) in enumerate(chips)]
        for cp in copies:
            cp.start()
        for cp in copies:
            cp.wait_recv()
        for cp in copies:
            cp.wait_send()

    sd = jax.ShapeDtypeStruct((R, C), s.dtype)
    return pl.pallas_call(
        body, name=name, in_specs=[ANY], out_specs=[ANY] * 3, out_shape=[sd] * 3,
        scratch_shapes=[pltpu.SemaphoreType.DMA((3,)), pltpu.SemaphoreType.DMA((3,))],
    )(s)


WEIGHTS = ("norm_mix_g", "w_in", "b_forget", "lru_conv_w", "lru_conv_b", "lru_w_a", "lru_b_a", "lru_w_x", "lru_b_x", "lru_lambda",
           "w_out", "norm_cross_g", "norm_mem_g", "w_cq", "w_ck", "w_cv", "w_co", "norm_ffn_g", "w_up", "ffn_conv_w", "ffn_conv_b",
           "w_down", "rel_bias", "final_norm_g")
LARGE = ("w_in", "w_out", "w_cq", "w_ck", "w_cv", "w_co", "w_up", "w_down")
COLUMN_SPLIT_SMALL = ("lru_conv_w", "ffn_conv_w")
PACK = (("qkv", 128, 2304), ("aux", 128, 640), ("out", 128, 1024), ("cq", 128, 256), ("ckv", 128, 512), ("coT", 128, 256),
        ("upT", 704, 1024), ("down", 352, 1024))
PACK_W = 1024


def _pack_rows(parts):
    return jnp.concatenate([p.reshape(-1, PACK_W) for p in parts], axis=0)


def _pad_rows(flat, mult=8 * LANES):
    n = flat.shape[0]
    return jnp.pad(flat, (0, (-n) % mult)).reshape(-1, LANES)


def kernel(x, mem, norm_mix_g, w_in, b_forget, lru_conv_w, lru_conv_b, lru_w_a, lru_b_a, lru_w_x, lru_b_x, lru_lambda, w_out, norm_cross_g, norm_mem_g, w_cq, w_ck, w_cv, w_co, norm_ffn_g, w_up, ffn_conv_w, ffn_conv_b, w_down, rel_bias, final_norm_g, loss_target, m_norm_mix_g, m_w_in, m_b_forget, m_lru_conv_w, m_lru_conv_b, m_lru_w_a, m_lru_b_a, m_lru_w_x, m_lru_b_x, m_lru_lambda, m_w_out, m_norm_cross_g, m_norm_mem_g, m_w_cq, m_w_ck, m_w_cv, m_w_co, m_norm_ffn_g, m_w_up, m_ffn_conv_w, m_ffn_conv_b, m_w_down, m_rel_bias, m_final_norm_g, v_norm_mix_g, v_w_in, v_b_forget, v_lru_conv_w, v_lru_conv_b, v_lru_w_a, v_lru_b_a, v_lru_w_x, v_lru_b_x, v_lru_lambda, v_w_out, v_norm_cross_g, v_norm_mem_g, v_w_cq, v_w_ck, v_w_cv, v_w_co, v_norm_ffn_g, v_w_up, v_ffn_conv_w, v_ffn_conv_b, v_w_down, v_rel_bias, v_final_norm_g):
    w = dict(norm_mix_g=norm_mix_g, w_in=w_in, b_forget=b_forget, lru_conv_w=lru_conv_w, lru_conv_b=lru_conv_b, lru_w_a=lru_w_a,
             lru_b_a=lru_b_a, lru_w_x=lru_w_x, lru_b_x=lru_b_x, lru_lambda=lru_lambda, w_out=w_out, norm_cross_g=norm_cross_g,
             norm_mem_g=norm_mem_g, w_cq=w_cq, w_ck=w_ck, w_cv=w_cv, w_co=w_co, norm_ffn_g=norm_ffn_g, w_up=w_up,
             ffn_conv_w=ffn_conv_w, ffn_conv_b=ffn_conv_b, w_down=w_down, rel_bias=rel_bias, final_norm_g=final_norm_g)
    m = dict(norm_mix_g=m_norm_mix_g, w_in=m_w_in, b_forget=m_b_forget, lru_conv_w=m_lru_conv_w, lru_conv_b=m_lru_conv_b,
             lru_w_a=m_lru_w_a, lru_b_a=m_lru_b_a, lru_w_x=m_lru_w_x, lru_b_x=m_lru_b_x, lru_lambda=m_lru_lambda, w_out=m_w_out,
             norm_cross_g=m_norm_cross_g, norm_mem_g=m_norm_mem_g, w_cq=m_w_cq, w_ck=m_w_ck, w_cv=m_w_cv, w_co=m_w_co,
             norm_ffn_g=m_norm_ffn_g, w_up=m_w_up, ffn_conv_w=m_ffn_conv_w, ffn_conv_b=m_ffn_conv_b, w_down=m_w_down,
             rel_bias=m_rel_bias, final_norm_g=m_final_norm_g)
    v = dict(norm_mix_g=v_norm_mix_g, w_in=v_w_in, b_forget=v_b_forget, lru_conv_w=v_lru_conv_w, lru_conv_b=v_lru_conv_b,
             lru_w_a=v_lru_w_a, lru_b_a=v_lru_b_a, lru_w_x=v_lru_w_x, lru_b_x=v_lru_b_x, lru_lambda=v_lru_lambda, w_out=v_w_out,
             norm_cross_g=v_norm_cross_g, norm_mem_g=v_norm_mem_g, w_cq=v_w_cq, w_ck=v_w_ck, w_cv=v_w_cv, w_co=v_w_co,
             norm_ffn_g=v_norm_ffn_g, w_up=v_w_up, ffn_conv_w=v_ffn_conv_w, ffn_conv_b=v_ffn_conv_b, w_down=v_w_down,
             rel_bias=v_rel_bias, final_norm_g=v_final_norm_g)
    me = 4 * lax.axis_index("x") + 2 * lax.axis_index("y") + lax.axis_index("c")

    conv_shard = jnp.concatenate([w[n].reshape(-1) for n in COLUMN_SPLIT_SMALL])
    conv_all = allgather_small(_pad_rows(conv_shard), name="gather_conv").reshape(N_DEV, -1)
    full = dict(w)
    off = 0
    for n in COLUMN_SPLIT_SMALL:
        d, k, c = w[n].shape
        blocks = conv_all[:, off:off + d * k * c].reshape(N_DEV, d, k, c)
        full[n] = blocks.transpose(1, 2, 0, 3).reshape(d, k, N_DEV * c)
        off += d * k * c

    shard_parts = []
    for l in range(DEPTH):
        canon = canonical_weights(*[w[n][l] for n in LARGE])
        shard_parts += [canon[k].astype(BF16) for k, _, _ in PACK]
    packed = allgather_hbm(_pack_rows(shard_parts), me, name="gather_weights")
    Ws, row = [], 0
    for l in range(DEPTH):
        W = {}
        for k, r, c in PACK:
            n_rows = r * c // PACK_W
            W[k] = packed[:, row:row + n_rows].reshape(N_DEV * r, c)
            row += n_rows
        upT = W.pop("upT")
        W["up_u"], W["up_g"] = upT[:D_FF], upT[D_FF:]
        Ws.append(W)
    Ps = [small_params(full, l) for l in range(DEPTH)]

    loss, grad_x, gWs, gPs, d_rel, d_final = local_step(x, mem, loss_target, Ws, Ps, rel_bias, final_norm_g.reshape(1, -1))

    grad_parts = []
    for l in range(DEPTH):
        g = dict(gWs[l])
        g["upT"] = jnp.concatenate([g.pop("up_u"), g.pop("up_g")], axis=0)
        grad_parts += [g[k].reshape(N_DEV, r * c // PACK_W, PACK_W) for k, r, c in PACK]
    g_all = jnp.concatenate(grad_parts, axis=1)
    rows = g_all.shape[1]
    got = pair_exchange(g_all, name="grads_pair_exchange")
    own = lax.dynamic_index_in_dim(g_all.reshape(N_CHIPS, 2, rows, PACK_W), lax.axis_index("c"), axis=1, keepdims=False)
    pair = sum_cast([own.reshape(-1, PACK_W), got.reshape(-1, PACK_W)], GRAD_WIRE, name="grads_pair_sum").reshape(N_CHIPS, rows, PACK_W)
    from_x, from_y, from_xy = chip_exchange(pair, name="grads_chip_exchange")
    mine = lax.dynamic_index_in_dim(pair, 2 * lax.axis_index("x") + lax.axis_index("y"), axis=0, keepdims=False)
    g_shard = sum_cast([mine, from_x, from_y, from_xy], F32, name="grads_chip_sum")
    grads, row = {}, 0
    per_layer = []
    for l in range(DEPTH):
        g = {}
        for k, r, c in PACK:
            n_rows = r * c // PACK_W
            g[k] = g_shard[row:row + n_rows].reshape(r, c)
            row += n_rows
        per_layer.append(native_grads(g))
    for i, n in enumerate(LARGE):
        grads[n] = jnp.stack([per_layer[l][i] for l in range(DEPTH)])

    small_names = [n for n in WEIGHTS if n not in LARGE and n not in ("rel_bias", "final_norm_g")]
    pieces = [gPs[l][n].reshape(-1) for n in small_names for l in range(DEPTH)] + [d_rel.reshape(-1), d_final.reshape(-1), loss[0, :1]]
    sizes = [p.shape[0] for p in pieces]
    _, total = allgather_small(_pad_rows(jnp.concatenate(pieces)), name="allreduce_small", reduce=True)
    total = total.reshape(-1)
    off, it = 0, iter(sizes)
    for n in small_names:
        per = []
        for l in range(DEPTH):
            sz = next(it)
            per.append(total[off:off + sz])
            off += sz
        full_shape = (DEPTH,) + full[n].shape[1:]
        gfull = jnp.stack(per).reshape(full_shape)
        if n in COLUMN_SPLIT_SMALL:
            c = w[n].shape[-1]
            gfull = lax.dynamic_slice_in_dim(gfull, me * c, c, axis=gfull.ndim - 1)
        grads[n] = gfull
    grads["rel_bias"] = total[off:off + rel_bias.size].reshape(rel_bias.shape)
    off += rel_bias.size
    grads["final_norm_g"] = total[off:off + D_MODEL]
    off += D_MODEL
    loss_out = total[off]

    delta, new_m, new_v = {}, {}, {}
    for n in LARGE:
        shape = w[n].shape
        two_d = lambda a: a.reshape(-1, shape[-1])
        d_, m_, v_ = adamw(two_d(w[n]), two_d(grads[n]), two_d(m[n]), two_d(v[n]), name=f"adamw_{n}")
        delta[n], new_m[n], new_v[n] = d_.reshape(shape), m_.reshape(shape), v_.reshape(shape)
    small_all = [n for n in WEIGHTS if n not in LARGE]
    two_d = lambda a: a.reshape(-1, a.shape[-1])
    d_, m_, v_ = adamw_many(*[[two_d(src[n]) for n in small_all] for src in (w, grads, m, v)], name="adamw_small")
    for i, n in enumerate(small_all):
        delta[n], new_m[n], new_v[n] = (a[i].reshape(w[n].shape) for a in (d_, m_, v_))

    return (loss_out, grad_x, *[grads[n] for n in WEIGHTS], *[delta[n] for n in WEIGHTS], *[new_m[n] for n in WEIGHTS],
            *[new_v[n] for n in WEIGHTS])
```

```python
import functools
import math

import numpy as np
import jax
import jax.numpy as jnp
from jax import lax
from jax.experimental import pallas as pl
from jax.experimental.pallas import tpu as pltpu

F32 = jnp.float32
BF16 = jnp.bfloat16
MESH = pl.DeviceIdType.MESH

N_DEV = 8
D_MODEL = 1024
SEQ = 2048
DEPTH = 2
HEAD_DIM = 64
N_HEADS = 4
GROUP_W = N_HEADS * HEAD_DIM
D_FF = 2816
N_MEM = 256
NUM_BUCKETS = 32
MAX_DISTANCE = 2048
BLOCK = 128
DILATIONS = (1, 4, 16)
EPS = 1e-6
LRU_C = 8.0
Q_SCALE = HEAD_DIM ** -0.5
AUX_W = 640
LRU_HALF_W = 128
LRU_HALVES = GROUP_W // LRU_HALF_W
ADAM_LR, ADAM_B1, ADAM_B2, ADAM_EPS, ADAM_WD, ADAM_STEP = 0.001, 0.9, 0.999, 1e-08, 0.01, 10

VMEM_LIMIT_V7X = 48 * 1024 * 1024


def _params(*sem):
    return pltpu.CompilerParams(dimension_semantics=sem if sem else None, vmem_limit_bytes=VMEM_LIMIT_V7X)


def _pick(n, cands):
    for c in cands:
        if n % c == 0:
            return c
    return n


def _largest_tile(n, cap, align):
    best = None
    for t in range(align, min(n, cap) + 1, align):
        if n % t == 0:
            best = t
    return n if best is None else best


def matmul(a, b, *, name, trans_a=False, trans_b=False, out_dtype=F32, residual=None):
    (K, M) = a.shape if trans_a else a.shape[::-1]
    (N, Kb) = b.shape if trans_b else b.shape[::-1]
    assert K == Kb, (a.shape, b.shape)
    tm = _largest_tile(M, 1024 if trans_a else 512, 128)
    tn = _largest_tile(N, 1408, 128)
    tk = _largest_tile(K, 2816, 128)
    nk = K // tk
    a_spec = pl.BlockSpec((tk, tm), lambda i, j, k: (k, i)) if trans_a else pl.BlockSpec((tm, tk), lambda i, j, k: (i, k))
    b_spec = pl.BlockSpec((tn, tk), lambda i, j, k: (j, k)) if trans_b else pl.BlockSpec((tk, tn), lambda i, j, k: (k, j))
    o_spec = pl.BlockSpec((tm, tn), lambda i, j, k: (i, j))
    dims = (((0 if trans_a else 1,), (1 if trans_b else 0,)), ((), ()))
    has_res = residual is not None

    def body(*refs):
        a_ref, b_ref = refs[0], refs[1]
        r_ref = refs[2] if has_res else None
        part = lax.dot_general(a_ref[...].astype(BF16), b_ref[...].astype(BF16), dims, preferred_element_type=F32)
        if nk == 1:
            if has_res:
                part = part + r_ref[...].astype(F32)
            refs[-1][...] = part.astype(out_dtype)
            return
        o_ref, acc_ref = refs[-2], refs[-1]
        k = pl.program_id(2)

        @pl.when(k == 0)
        def _():
            acc_ref[...] = part

        @pl.when(k > 0)
        def _():
            acc_ref[...] += part

        @pl.when(k == nk - 1)
        def _():
            r = acc_ref[...]
            if has_res:
                r = r + r_ref[...].astype(F32)
            o_ref[...] = r.astype(out_dtype)

    ops = (a, b) + ((residual,) if has_res else ())
    return pl.pallas_call(
        body, name=name, grid=(M // tm, N // tn, nk),
        in_specs=[a_spec, b_spec] + ([o_spec] if has_res else []),
        out_specs=o_spec, out_shape=jax.ShapeDtypeStruct((M, N), out_dtype),
        scratch_shapes=[pltpu.VMEM((tm, tn), F32)] if nk > 1 else [],
        compiler_params=_params("parallel", "parallel", "arbitrary"),
    )(*ops)


def rmsnorm_fwd(x, g, *, name):
    R, D = x.shape
    tr = _pick(R, (512, 256))

    def body(x_ref, g_ref, o_ref):
        xv = x_ref[...]
        r = lax.rsqrt(jnp.mean(xv * xv, axis=-1, keepdims=True) + EPS)
        o_ref[...] = (xv * r * g_ref[...]).astype(BF16)

    return pl.pallas_call(
        body, name=name, grid=(R // tr,),
        in_specs=[pl.BlockSpec((tr, D), lambda i: (i, 0)), pl.BlockSpec((1, D), lambda i: (0, 0))],
        out_specs=pl.BlockSpec((tr, D), lambda i: (i, 0)), out_shape=jax.ShapeDtypeStruct((R, D), BF16),
        compiler_params=_params("parallel"),
    )(x, g)


def rmsnorm_bwd(x, g, dh, dres, *, name):
    R, D = x.shape
    tr = _pick(R, (512, 256))
    has_res = dres is not None

    def body(*refs):
        x_ref, g_ref, dh_ref = refs[:3]
        dx_ref, dg_ref = refs[-2], refs[-1]
        xv = x_ref[...]
        r = lax.rsqrt(jnp.mean(xv * xv, axis=-1, keepdims=True) + EPS)
        n = xv * r
        dhv = dh_ref[...]
        dn = dhv * g_ref[...]
        dx = r * (dn - n * jnp.mean(dn * n, axis=-1, keepdims=True))
        if has_res:
            dx = dx + refs[3][...]
        dx_ref[...] = dx
        part = jnp.sum(dhv * n, axis=0, keepdims=True)

        @pl.when(pl.program_id(0) == 0)
        def _():
            dg_ref[...] = part

        @pl.when(pl.program_id(0) > 0)
        def _():
            dg_ref[...] += part

    row = pl.BlockSpec((tr, D), lambda i: (i, 0))
    vec = pl.BlockSpec((1, D), lambda i: (0, 0))
    ops = (x, g, dh) + ((dres,) if has_res else ())
    return pl.pallas_call(
        body, name=name, grid=(R // tr,),
        in_specs=[row, vec, row] + ([row] if has_res else []),
        out_specs=[row, vec],
        out_shape=[jax.ShapeDtypeStruct((R, D), F32), jax.ShapeDtypeStruct((1, D), F32)],
        compiler_params=_params("arbitrary"),
    )(*ops)


_SQRT_HALF = 0.7071067811865476
_INV_SQRT_2PI = 0.3989422804014327


def _erf(x):
    ax = jnp.abs(x)
    t = 1.0 / (1.0 + 0.3275911 * ax)
    poly = t * (0.254829592 + t * (-0.284496736 + t * (1.421413741 + t * (-1.453152027 + t * 1.061405429))))
    y = 1.0 - poly * jnp.exp(-ax * ax)
    return jnp.where(x < 0, -y, y)


def _gelu_cdf(x):
    return 0.5 * (1.0 + _erf(x * _SQRT_HALF))


def _gelu_and_grad(x):
    cdf = _gelu_cdf(x)
    return x * cdf, cdf + x * _INV_SQRT_2PI * jnp.exp(-0.5 * x * x)


def _shift_down(main, halo, first, shifts):
    halo = jnp.where(first, 0.0, halo)
    ext = jnp.concatenate([halo, main], axis=0)
    return [pltpu.roll(ext, s, 0)[8:] for s in shifts]


def _conv3(main, halo, first, w, b):
    m1, m2 = _shift_down(main, halo, first, (1, 2))
    return ((b + w[0:1] * m2) + w[1:2] * m1) + w[2:3] * main, m1, m2


def glu_fwd(hu, hg, wu, wg, bu, bg, *, name):
    T, F = hu.shape
    tm, tf = 512, _largest_tile(F, 704, 128)
    hb = tm // 8
    blocks_per_example = SEQ // tm

    def body(hu_ref, hg_ref, hau_ref, hag_ref, wu_ref, wg_ref, bu_ref, bg_ref, o_ref):
        first = pl.program_id(0) % blocks_per_example == 0
        up, _, _ = _conv3(hu_ref[...], hau_ref[...], first, wu_ref[...], bu_ref[...])
        gate, _, _ = _conv3(hg_ref[...], hag_ref[...], first, wg_ref[...], bg_ref[...])
        o_ref[...] = (gate * _gelu_cdf(gate) * up).astype(BF16)

    main = pl.BlockSpec((tm, tf), lambda i, j: (i, j))
    halo = pl.BlockSpec((8, tf), lambda i, j: (jnp.maximum(i * hb - 1, 0), j))
    w3 = pl.BlockSpec((3, tf), lambda i, j: (0, j))
    b1 = pl.BlockSpec((1, tf), lambda i, j: (0, j))
    return pl.pallas_call(
        body, name=name, grid=(T // tm, F // tf),
        in_specs=[main, main, halo, halo, w3, w3, b1, b1],
        out_specs=main, out_shape=jax.ShapeDtypeStruct((T, F), BF16),
        compiler_params=_params("parallel", "parallel"),
    )(hu, hg, hu, hg, wu, wg, bu, bg)


def glu_bwd(hu, hg, dact, wu, wg, bu, bg, *, name):
    T, F = hu.shape
    tm, tf = 512, _largest_tile(F, 704, 128)
    hb = tm // 8
    blocks_per_example = SEQ // tm

    def body(hu_ref, hg_ref, hau_ref, hag_ref, da_ref, wu_ref, wg_ref, bu_ref, bg_ref,
             du_ref, dg_ref, dwu_ref, dwg_ref, dbu_ref, dbg_ref):
        i = pl.program_id(1)
        first = i % blocks_per_example == 0
        xu, xg = hu_ref[...], hg_ref[...]
        up, u1, u2 = _conv3(xu, hau_ref[...], first, wu_ref[...], bu_ref[...])
        gate, g1, g2 = _conv3(xg, hag_ref[...], first, wg_ref[...], bg_ref[...])
        act, dact_dgate = _gelu_and_grad(gate)
        da = da_ref[...]
        dup = da * act
        dgate = da * up * dact_dgate
        du_ref[...] = dup
        dg_ref[...] = dgate

        def sums(d, x0, x1, x2):
            s = lambda v: jnp.sum(v, axis=0, keepdims=True)
            return jnp.concatenate([s(d * x2), s(d * x1), s(d * x0)], axis=0), s(d)

        pwu, pbu = sums(dup, xu, u1, u2)
        pwg, pbg = sums(dgate, xg, g1, g2)

        @pl.when(i == 0)
        def _():
            dwu_ref[...] = pwu
            dwg_ref[...] = pwg
            dbu_ref[...] = pbu
            dbg_ref[...] = pbg

        @pl.when(i > 0)
        def _():
            dwu_ref[...] += pwu
            dwg_ref[...] += pwg
            dbu_ref[...] += pbu
            dbg_ref[...] += pbg

    main = pl.BlockSpec((tm, tf), lambda j, i: (i, j))
    halo = pl.BlockSpec((8, tf), lambda j, i: (jnp.maximum(i * hb - 1, 0), j))
    w3 = pl.BlockSpec((3, tf), lambda j, i: (0, j))
    b1 = pl.BlockSpec((1, tf), lambda j, i: (0, j))
    sd = jax.ShapeDtypeStruct
    return pl.pallas_call(
        body, name=name, grid=(F // tf, T // tm),
        in_specs=[main, main, halo, halo, main, w3, w3, b1, b1],
        out_specs=[main, main, w3, w3, b1, b1],
        out_shape=[sd((T, F), F32), sd((T, F), F32), sd((3, F), F32), sd((3, F), F32), sd((1, F), F32), sd((1, F), F32)],
        compiler_params=_params("parallel", "arbitrary"),
    )(hu, hg, hu, hg, dact, wu, wg, bu, bg)


def conv3_transpose(d, w, *, name):
    T, F = d.shape
    tm, tf = 512, _largest_tile(F, 704, 128)
    hb = tm // 8
    blocks_per_example = SEQ // tm
    n_halo_blocks = T // 8

    def body(d_ref, ha_ref, w_ref, o_ref):
        last = pl.program_id(0) % blocks_per_example == blocks_per_example - 1
        main = d_ref[...]
        halo = jnp.where(last, 0.0, ha_ref[...])
        ext = jnp.concatenate([main, halo], axis=0)
        n = tm + 8
        p1 = pltpu.roll(ext, n - 1, 0)[:tm]
        p2 = pltpu.roll(ext, n - 2, 0)[:tm]
        w = w_ref[...]
        o_ref[...] = (w[2:3] * main + w[1:2] * p1 + w[0:1] * p2).astype(BF16)

    main = pl.BlockSpec((tm, tf), lambda i, j: (i, j))
    halo = pl.BlockSpec((8, tf), lambda i, j: (jnp.minimum((i + 1) * hb, n_halo_blocks - 1), j))
    return pl.pallas_call(
        body, name=name, grid=(T // tm, F // tf),
        in_specs=[main, halo, pl.BlockSpec((3, tf), lambda i, j: (0, j))],
        out_specs=main, out_shape=jax.ShapeDtypeStruct((T, F), BF16),
        compiler_params=_params("parallel", "parallel"),
    )(d, d, w)


def loss_head(x, g, target, *, name):
    T, D = x.shape
    tr = 256

    def body(x_ref, g_ref, t_ref, loss_ref, dx_ref, dg_ref):
        xv = x_ref[...]
        gv = g_ref[...]
        r = lax.rsqrt(jnp.mean(xv * xv, axis=-1, keepdims=True) + EPS)
        n = xv * r
        err = n * gv - t_ref[...]
        part_loss = jnp.zeros((1, 128), F32) + 0.5 * jnp.sum(jnp.mean(err * err, axis=-1, keepdims=True))
        dy = err * (1.0 / D)
        dn = dy * gv
        dx_ref[...] = r * (dn - n * jnp.mean(dn * n, axis=-1, keepdims=True))
        part_g = jnp.sum(dy * n, axis=0, keepdims=True)

        @pl.when(pl.program_id(0) == 0)
        def _():
            loss_ref[...] = part_loss
            dg_ref[...] = part_g

        @pl.when(pl.program_id(0) > 0)
        def _():
            loss_ref[...] += part_loss
            dg_ref[...] += part_g

    row = pl.BlockSpec((tr, D), lambda i: (i, 0))
    vec = pl.BlockSpec((1, D), lambda i: (0, 0))
    sd = jax.ShapeDtypeStruct
    return pl.pallas_call(
        body, name=name, grid=(T // tr,),
        in_specs=[row, vec, row],
        out_specs=[pl.BlockSpec((1, 128), lambda i: (0, 0)), row, vec],
        out_shape=[sd((1, 128), F32), sd((T, D), F32), sd((1, D), F32)],
        compiler_params=_params("arbitrary"),
    )(x, g, target)


def adamw(w, g, m, v, *, name):
    R, C = w.shape
    tr = _pick(R, (256, 128, 64, 32, 16, 8))

    def body(w_ref, g_ref, m_ref, v_ref, d_ref, nm_ref, nv_ref):
        gv = g_ref[...]
        mn = ADAM_B1 * m_ref[...] + (1.0 - ADAM_B1) * gv
        vn = ADAM_B2 * v_ref[...] + (1.0 - ADAM_B2) * (gv * gv)
        m_hat = mn / (1.0 - ADAM_B1 ** ADAM_STEP)
        v_hat = vn / (1.0 - ADAM_B2 ** ADAM_STEP)
        d_ref[...] = -ADAM_LR * (m_hat / (jnp.sqrt(v_hat) + ADAM_EPS) + ADAM_WD * w_ref[...])
        nm_ref[...] = mn
        nv_ref[...] = vn

    blk = pl.BlockSpec((tr, C), lambda i: (i, 0))
    sd = jax.ShapeDtypeStruct((R, C), F32)
    return pl.pallas_call(
        body, name=name, grid=(R // tr,), in_specs=[blk] * 4, out_specs=[blk] * 3, out_shape=[sd] * 3,
        compiler_params=_params("parallel"),
    )(w, g, m, v)


def adamw_many(ws, gs, ms, vs, *, name):
    n = len(ws)

    def body(*refs):
        ins, outs = refs[:4 * n], refs[4 * n:]
        for i in range(n):
            w_ref, g_ref, m_ref, v_ref = ins[i], ins[n + i], ins[2 * n + i], ins[3 * n + i]
            gv = g_ref[...]
            mn = ADAM_B1 * m_ref[...] + (1.0 - ADAM_B1) * gv
            vn = ADAM_B2 * v_ref[...] + (1.0 - ADAM_B2) * (gv * gv)
            m_hat = mn / (1.0 - ADAM_B1 ** ADAM_STEP)
            v_hat = vn / (1.0 - ADAM_B2 ** ADAM_STEP)
            outs[i][...] = -ADAM_LR * (m_hat / (jnp.sqrt(v_hat) + ADAM_EPS) + ADAM_WD * w_ref[...])
            outs[n + i][...] = mn
            outs[2 * n + i][...] = vn

    vm = pl.BlockSpec(memory_space=pltpu.VMEM)
    shapes = [jax.ShapeDtypeStruct(w.shape, F32) for w in ws]
    res = pl.pallas_call(
        body, name=name, in_specs=[vm] * (4 * n), out_specs=[vm] * (3 * n), out_shape=shapes * 3, compiler_params=_params(),
    )(*ws, *gs, *ms, *vs)
    return res[:n], res[n:2 * n], res[2 * n:]


def _softplus(x):
    return jnp.maximum(x, 0.0) + jnp.log(1.0 + jnp.exp(-jnp.abs(x)))


def _lru_gates(x, cw, cb, wa, ba, wx, bx, lam):
    S = x.shape[0]
    row = lax.broadcasted_iota(jnp.int32, (S, 1), 0)

    def back(s):
        return jnp.where(row >= s, pltpu.roll(x, s, 0), 0.0)

    xc = (((cb + cw[0:1] * back(3)) + cw[1:2] * back(2)) + cw[2:3] * back(1)) + cw[3:4] * x
    xb = xc.astype(BF16)
    r = jax.nn.sigmoid(jnp.dot(xb, wa, preferred_element_type=F32) + ba)
    ig = jax.nn.sigmoid(jnp.dot(xb, wx, preferred_element_type=F32) + bx)
    sp = _softplus(-lam)
    la = -LRU_C * r * sp
    a = jnp.exp(la)
    y = 2.0 * la
    one_minus_a2 = jnp.where(y > -0.05, -y * (1.0 + y * (0.5 + y * (1.0 / 6.0 + y * (1.0 / 24.0)))), 1.0 - jnp.exp(y))
    mm = jnp.sqrt(one_minus_a2)
    return xc, xb, r, ig, sp, a, mm


def lru_fwd(aux, cw, cb, wa, ba, wx, bx, lam, *, name):
    T = aux.shape[0]
    S, C = SEQ, LRU_HALF_W

    def body(x_ref, g_ref, cw_ref, cb_ref, wa_ref, ba_ref, wx_ref, bx_ref, lam_ref, o_ref, h_ref, a_s, u_s):
        xc, _, r, ig, sp, a, mm = _lru_gates(x_ref[...], cw_ref[...], cb_ref[...], wa_ref[...], ba_ref[...],
                                             wx_ref[...], bx_ref[...], lam_ref[...])
        a_s[...] = a
        u_s[...] = mm * (ig * xc)

        def group(i, h):
            base = pl.multiple_of(i * 8, 8)
            a8 = a_s[pl.ds(base, 8), :]
            u8 = u_s[pl.ds(base, 8), :]
            for rr in range(8):
                h = a8[rr:rr + 1] * h + u8[rr:rr + 1]
                h_ref[pl.ds(base + rr, 1), :] = h
            return h

        lax.fori_loop(0, S // 8, group, jnp.zeros((1, C), F32))
        gate = g_ref[...]
        o_ref[...] = (h_ref[...] * (gate * _gelu_cdf(gate))).astype(BF16)

    blk = lambda col: pl.BlockSpec((S, C), lambda c, b: (b, col + c))
    par = lambda rows: pl.BlockSpec((rows, C), lambda c, b: (0, c))
    sq = pl.BlockSpec((None, C, C), lambda c, b: (c, 0, 0))
    sd = jax.ShapeDtypeStruct
    W = LRU_HALVES * C
    return pl.pallas_call(
        body, name=name, grid=(LRU_HALVES, T // S),
        in_specs=[blk(0), blk(LRU_HALVES), par(4), par(1), sq, par(1), sq, par(1), par(1)],
        out_specs=[blk(0), blk(0)], out_shape=[sd((T, W), BF16), sd((T, W), F32)],
        scratch_shapes=[pltpu.VMEM((S, C), F32), pltpu.VMEM((S, C), F32)],
        compiler_params=_params("parallel", "parallel"),
    )(aux, aux, cw, cb, wa, ba, wx, bx, lam)


def lru_bwd(aux, h, dmixed, cw, cb, wa, ba, wx, bx, lam, *, name):
    T = aux.shape[0]
    S, C = SEQ, LRU_HALF_W

    def body(x_ref, g_ref, h_ref, do_ref, cw_ref, cb_ref, wa_ref, ba_ref, wx_ref, bx_ref, lam_ref,
             dx_ref, dgate_ref, dcw_ref, dcb_ref, dwa_ref, dba_ref, dwx_ref, dbx_ref, dlam_ref, a_s, d_s):
        x = x_ref[...]
        cw = cw_ref[...]
        lam = lam_ref[...]
        xc, xb, r, ig, sp, a, mm = _lru_gates(x, cw, cb_ref[...], wa_ref[...], ba_ref[...], wx_ref[...], bx_ref[...], lam)
        gate = g_ref[...]
        gl, dgl = _gelu_and_grad(gate)
        dout = do_ref[...]
        hv = h_ref[...]
        dgate_ref[...] = dout * hv * dgl
        a_s[...] = a
        d_s[...] = dout * gl

        def group(i, c):
            base = pl.multiple_of((S // 8 - 1 - i) * 8, 8)
            a8 = a_s[pl.ds(base, 8), :]
            d8 = d_s[pl.ds(base, 8), :]
            for rr in range(7, -1, -1):
                d = d8[rr:rr + 1] + c
                d_s[pl.ds(base + rr, 1), :] = d
                c = a8[rr:rr + 1] * d
            return c

        lax.fori_loop(0, S // 8, group, jnp.zeros((1, C), F32))
        row = lax.broadcasted_iota(jnp.int32, (S, 1), 0)
        dht = d_s[...]
        h_prev = jnp.where(row >= 1, pltpu.roll(hv, 1, 0), 0.0)
        da = dht * h_prev
        gx = ig * xc
        dmm = dht * gx
        dig = dht * mm * xc
        dxc = dht * mm * ig
        dla = da * a - dmm * (a * a) / mm
        dr = dla * (-LRU_C * sp)
        dsp = jnp.sum(dla * (-LRU_C * r), axis=0, keepdims=True)
        dlam = dsp * (-jax.nn.sigmoid(-lam))
        dpa = dr * r * (1.0 - r)
        dpx = dig * ig * (1.0 - ig)
        dpa_b, dpx_b = dpa.astype(BF16), dpx.astype(BF16)
        nt = (((1,), (1,)), ((), ()))
        tn = (((0,), (0,)), ((), ()))
        dxc = dxc + lax.dot_general(dpa_b, wa_ref[...], nt, preferred_element_type=F32) \
                  + lax.dot_general(dpx_b, wx_ref[...], nt, preferred_element_type=F32)
        dwa = lax.dot_general(xb, dpa_b, tn, preferred_element_type=F32)
        dwx = lax.dot_general(xb, dpx_b, tn, preferred_element_type=F32)

        def fwd(v, s):
            return jnp.where(row < S - s, pltpu.roll(v, S - s, 0), 0.0)

        def back(v, s):
            return jnp.where(row >= s, pltpu.roll(v, s, 0), 0.0)

        dx_ref[...] = cw[3:4] * dxc + cw[2:3] * fwd(dxc, 1) + cw[1:2] * fwd(dxc, 2) + cw[0:1] * fwd(dxc, 3)
        s0 = lambda v: jnp.sum(v, axis=0, keepdims=True)
        dcw = jnp.concatenate([s0(dxc * back(x, 3)), s0(dxc * back(x, 2)), s0(dxc * back(x, 1)), s0(dxc * x)], axis=0)
        parts = ((dcw_ref, dcw), (dcb_ref, s0(dxc)), (dwa_ref, dwa), (dba_ref, s0(dpa)), (dwx_ref, dwx),
                 (dbx_ref, s0(dpx)), (dlam_ref, dlam))

        @pl.when(pl.program_id(1) == 0)
        def _():
            for ref, val in parts:
                ref[...] = val

        @pl.when(pl.program_id(1) > 0)
        def _():
            for ref, val in parts:
                ref[...] += val

    blk = lambda col: pl.BlockSpec((S, C), lambda c, b: (b, col + c))
    par = lambda rows: pl.BlockSpec((rows, C), lambda c, b: (0, c))
    sq = pl.BlockSpec((None, C, C), lambda c, b: (c, 0, 0))
    sd = jax.ShapeDtypeStruct
    W = LRU_HALVES * C
    vec = sd((1, W), F32)
    return pl.pallas_call(
        body, name=name, grid=(LRU_HALVES, T // S),
        in_specs=[blk(0), blk(LRU_HALVES), blk(0), blk(3 * LRU_HALVES), par(4), par(1), sq, par(1), sq, par(1), par(1)],
        out_specs=[blk(0), blk(0), par(4), par(1), sq, par(1), sq, par(1), par(1)],
        out_shape=[sd((T, W), F32), sd((T, W), F32), sd((4, W), F32), vec, sd((LRU_HALVES, C, C), F32), vec,
                   sd((LRU_HALVES, C, C), F32), vec, vec],
        scratch_shapes=[pltpu.VMEM((S, C), F32), pltpu.VMEM((S, C), F32)],
        compiler_params=_params("parallel", "arbitrary"),
    )(aux, aux, h, dmixed, cw, cb, wa, ba, wx, bx, lam)


_NT = (((1,), (1,)), ((), ()))
_TN = (((0,), (0,)), ((), ()))


def _dot(a, b, dims=None):
    if dims is None:
        return jnp.dot(a, b, preferred_element_type=F32)
    return lax.dot_general(a, b, dims, preferred_element_type=F32)


def _hs(h):
    return slice(h * HEAD_DIM, (h + 1) * HEAD_DIM)


def cross_fwd(q, kv, *, name):
    T = q.shape[0]
    tq = 512

    def body(q_ref, kv_ref, o_ref):
        for h in range(N_HEADS):
            qh = q_ref[:, _hs(h)] * Q_SCALE
            k = kv_ref[:, _hs(h)]
            v = kv_ref[:, GROUP_W + h * HEAD_DIM:GROUP_W + (h + 1) * HEAD_DIM]
            s = _dot(qh, k, _NT)
            p = jnp.exp(s - jnp.max(s, axis=-1, keepdims=True))
            p = p / jnp.sum(p, axis=-1, keepdims=True)
            o_ref[:, _hs(h)] = _dot(p.astype(BF16), v).astype(BF16)

    per = SEQ // tq
    return pl.pallas_call(
        body, name=name, grid=(T // tq,),
        in_specs=[pl.BlockSpec((tq, GROUP_W), lambda i: (i, 0)), pl.BlockSpec((N_MEM, 2 * GROUP_W), lambda i: (i // per, 0))],
        out_specs=pl.BlockSpec((tq, GROUP_W), lambda i: (i, 0)), out_shape=jax.ShapeDtypeStruct((T, GROUP_W), BF16),
        compiler_params=_params("parallel"),
    )(q, kv)


def cross_bwd(q, kv, do, *, name):
    T = q.shape[0]
    tq = 512
    per = SEQ // tq

    def body(q_ref, kv_ref, do_ref, dq_ref, dkv_ref):
        first = pl.program_id(0) % per == 0
        for h in range(N_HEADS):
            vs = slice(GROUP_W + h * HEAD_DIM, GROUP_W + (h + 1) * HEAD_DIM)
            qh = q_ref[:, _hs(h)] * Q_SCALE
            k = kv_ref[:, _hs(h)]
            v = kv_ref[:, vs]
            doh = do_ref[:, _hs(h)].astype(BF16)
            s = _dot(qh, k, _NT)
            p = jnp.exp(s - jnp.max(s, axis=-1, keepdims=True))
            p = p / jnp.sum(p, axis=-1, keepdims=True)
            dp = _dot(doh, v, _NT)
            ds = (p * (dp - jnp.sum(p * dp, axis=-1, keepdims=True))).astype(BF16)
            dq_ref[:, _hs(h)] = (_dot(ds, k) * Q_SCALE).astype(BF16)
            dk = _dot(ds, qh, _TN)
            dv = _dot(p.astype(BF16), doh, _TN)

            @pl.when(first)
            def _():
                dkv_ref[:, _hs(h)] = dk
                dkv_ref[:, vs] = dv

            @pl.when(jnp.logical_not(first))
            def _():
                dkv_ref[:, _hs(h)] += dk
                dkv_ref[:, vs] += dv

    qb = pl.BlockSpec((tq, GROUP_W), lambda i: (i, 0))
    kvb = pl.BlockSpec((N_MEM, 2 * GROUP_W), lambda i: (i // per, 0))
    sd = jax.ShapeDtypeStruct
    return pl.pallas_call(
        body, name=name, grid=(T // tq,),
        in_specs=[qb, kvb, qb], out_specs=[qb, kvb],
        out_shape=[sd((T, GROUP_W), BF16), sd(kv.shape, F32)],
        compiler_params=_params("arbitrary"),
    )(q, kv, do)


NB = SEQ // BLOCK
NEG = -1e30


def _split_dot(x, tri):
    hi = x.astype(BF16)
    lo = (x - hi.astype(F32)).astype(BF16)
    return _dot(hi, tri) + _dot(lo, tri)


def _blk(i):
    return pl.ds(pl.multiple_of(i * BLOCK, BLOCK), BLOCK)


def _iotas():
    row = lax.broadcasted_iota(jnp.int32, (BLOCK, BLOCK), 0)
    col = lax.broadcasted_iota(jnp.int32, (BLOCK, BLOCK), 1)
    return row, col


def _sb_scores(q, k, mask, later, csum, want_sigmoid=False):
    z = _dot(q, k, _NT)
    lk = -_softplus(z)
    if mask is not None:
        lk = jnp.where(mask, lk, 0.0)
    lka = _split_dot(lk, later) + csum
    att = jnp.exp(z + lk + lka)
    sg = jnp.exp(z + lk) if want_sigmoid else None
    if mask is not None:
        att = jnp.where(mask, att, 0.0)
        sg = jnp.where(mask, sg, 0.0) if want_sigmoid else None
    return att, sg, lk


def _rowsum(v):
    return jnp.sum(v, axis=1, keepdims=True)


HEADS = tuple(range(N_HEADS))


def _qkv_specs(first_col):
    return [pl.BlockSpec((SEQ, GROUP_W), lambda b, c=first_col + j: (b, c)) for j in range(3)]


LANES = 128
CUM_BLK = 256


def col_to_row(c):
    b = c.shape[0] // SEQ
    return c.reshape(b, SEQ, LANES)[:, :, :8].transpose(0, 2, 1).reshape(b * 8, SEQ)


def row_to_col(r):
    b = r.shape[0] // 8
    c = r.reshape(b, 8, SEQ).transpose(0, 2, 1)
    return jnp.pad(c, ((0, 0), (0, 0), (0, LANES - 8))).reshape(b * SEQ, LANES)


def fox_prep(aux, bf, *, name):
    T = aux.shape[0]

    def body(f_ref, b_ref, o_ref):
        row = lax.broadcasted_iota(jnp.int32, (CUM_BLK, CUM_BLK), 0)
        col = lax.broadcasted_iota(jnp.int32, (CUM_BLK, CUM_BLK), 1)
        upto = (col <= row).astype(BF16)
        carry = jnp.zeros((1, LANES), F32)
        for n in range(SEQ // CUM_BLK):
            rows = slice(n * CUM_BLK, (n + 1) * CUM_BLK)
            logf = -_softplus(-(f_ref[rows, :] + b_ref[...]))
            hi = logf.astype(BF16)
            lo = (logf - hi.astype(F32)).astype(BF16)
            cum = _dot(upto, hi) + _dot(upto, lo) + carry
            o_ref[rows, :] = cum
            carry = cum[CUM_BLK - 1:CUM_BLK]

    return pl.pallas_call(
        body, name=name, grid=(T // SEQ,),
        in_specs=[pl.BlockSpec((SEQ, LANES), lambda b: (b, 4)), pl.BlockSpec((1, LANES), lambda b: (0, 0))],
        out_specs=pl.BlockSpec((SEQ, LANES), lambda b: (b, 0)), out_shape=jax.ShapeDtypeStruct((T, LANES), F32),
        compiler_params=_params("parallel"),
    )(aux, bf)


def fox_prep_bwd(aux, bf, dcum, *, name):
    T = aux.shape[0]

    def body(f_ref, b_ref, d_ref, df_ref, db_ref):
        row = lax.broadcasted_iota(jnp.int32, (CUM_BLK, CUM_BLK), 0)
        col = lax.broadcasted_iota(jnp.int32, (CUM_BLK, CUM_BLK), 1)
        onward = (col >= row).astype(BF16)
        carry = jnp.zeros((1, LANES), F32)
        tot = jnp.zeros((1, LANES), F32)
        for n in range(SEQ // CUM_BLK - 1, -1, -1):
            rows = slice(n * CUM_BLK, (n + 1) * CUM_BLK)
            d = d_ref[rows, :]
            hi = d.astype(BF16)
            lo = (d - hi.astype(F32)).astype(BF16)
            dlogf = _dot(onward, hi) + _dot(onward, lo) + carry
            carry = dlogf[0:1]
            df = dlogf * jax.nn.sigmoid(-(f_ref[rows, :] + b_ref[...]))
            df_ref[rows, :] = df
            tot = tot + jnp.sum(df, axis=0, keepdims=True)

        @pl.when(pl.program_id(0) == 0)
        def _():
            db_ref[...] = tot

        @pl.when(pl.program_id(0) > 0)
        def _():
            db_ref[...] += tot

    blk = pl.BlockSpec((SEQ, LANES), lambda b: (b, 0))
    vec = pl.BlockSpec((1, LANES), lambda b: (0, 0))
    sd = jax.ShapeDtypeStruct
    return pl.pallas_call(
        body, name=name, grid=(T // SEQ,),
        in_specs=[pl.BlockSpec((SEQ, LANES), lambda b: (b, 4)), vec, blk],
        out_specs=[blk, vec], out_shape=[sd((T, LANES), F32), sd((1, LANES), F32)],
        compiler_params=_params("arbitrary"),
    )(aux, bf, dcum)


def _fox_logits(q, k, cq, ck, mask):
    z = _dot(q, k, _NT) + cq - ck
    return z if mask is None else jnp.where(mask, z, NEG)


def fox_fwd(qkv, cumc, cumr, *, name):
    T = qkv.shape[0]

    def body(q_ref, k_ref, v_ref, cc_ref, cr_ref, o_ref, lse_ref, z_s):
        row, col = _iotas()
        causal = col <= row
        lse_ref[...] = jnp.zeros_like(lse_ref)

        def qblock(i, _):
            qs = [q_ref[_blk(i), _hs(h)] * Q_SCALE for h in HEADS]
            cqs = [cc_ref[_blk(i), h:h + 1] for h in HEADS]

            def logits(j, mask, ms):
                out = []
                for h in HEADS:
                    z = _fox_logits(qs[h], k_ref[_blk(j), _hs(h)], cqs[h], cr_ref[h:h + 1, _blk(j)], mask)
                    z_s[h, j] = z
                    out.append(jnp.maximum(ms[h], jnp.max(z, axis=1, keepdims=True)))
                return tuple(out)

            ms = logits(i, causal, (jnp.full((BLOCK, 1), NEG, F32),) * N_HEADS)
            ms = lax.fori_loop(0, i, lambda j, c: logits(j, None, c), ms)

            def values(j, carry):
                out = []
                for h in HEADS:
                    acc, l = carry[h]
                    p = jnp.exp(z_s[h, j] - ms[h])
                    out.append((acc + _dot(p.astype(BF16), v_ref[_blk(j), _hs(h)]), l + _rowsum(p)))
                return tuple(out)

            zero = (jnp.zeros((BLOCK, HEAD_DIM), F32), jnp.zeros((BLOCK, 1), F32))
            res = lax.fori_loop(0, i + 1, values, (zero,) * N_HEADS)
            for h in HEADS:
                acc, l = res[h]
                o_ref[_blk(i), _hs(h)] = (acc / l).astype(BF16)
                lse_ref[_blk(i), h:h + 1] = ms[h] + jnp.log(l)
            return 0

        lax.fori_loop(0, NB, qblock, 0)

    out = pl.BlockSpec((SEQ, GROUP_W), lambda b: (b, 0))
    colb = pl.BlockSpec((SEQ, LANES), lambda b: (b, 0))
    sd = jax.ShapeDtypeStruct
    return pl.pallas_call(
        body, name=name, grid=(T // SEQ,),
        in_specs=_qkv_specs(3) + [colb, pl.BlockSpec((8, SEQ), lambda b: (b, 0))],
        out_specs=[out, colb], out_shape=[sd((T, GROUP_W), BF16), sd((T, LANES), F32)],
        scratch_shapes=[pltpu.VMEM((N_HEADS, NB, BLOCK, BLOCK), F32)],
        compiler_params=_params("parallel"),
    )(qkv, qkv, qkv, cumc, cumr)


def fox_bwd(qkv, cumc, cumr, lse, dmixed, *, name):
    T = qkv.shape[0]

    def body(q_ref, k_ref, v_ref, cc_ref, cr_ref, lse_ref, do_ref, dq_ref, dk_ref, dv_ref, dcc_ref, dcr_ref, p_s, dp_s):
        row, col = _iotas()
        causal = col <= row
        dk_ref[...] = jnp.zeros_like(dk_ref)
        dv_ref[...] = jnp.zeros_like(dv_ref)
        dcc_ref[...] = jnp.zeros_like(dcc_ref)
        dcr_ref[...] = jnp.zeros_like(dcr_ref)

        def qblock(i, _):
            qs = [q_ref[_blk(i), _hs(h)] * Q_SCALE for h in HEADS]
            dos = [do_ref[_blk(i), _hs(h)].astype(BF16) for h in HEADS]
            cqs = [cc_ref[_blk(i), h:h + 1] for h in HEADS]
            lses = [lse_ref[_blk(i), h:h + 1] for h in HEADS]

            def probs(j, mask, deltas):
                out = []
                for h in HEADS:
                    z = _fox_logits(qs[h], k_ref[_blk(j), _hs(h)], cqs[h], cr_ref[h:h + 1, _blk(j)], mask)
                    p = jnp.exp(z - lses[h])
                    dp = _dot(dos[h], v_ref[_blk(j), _hs(h)], _NT)
                    p_s[h, j] = p
                    dp_s[h, j] = dp
                    out.append(deltas[h] + _rowsum(p * dp))
                return tuple(out)

            deltas = probs(i, causal, (jnp.zeros((BLOCK, 1), F32),) * N_HEADS)
            deltas = lax.fori_loop(0, i, lambda j, c: probs(j, None, c), deltas)

            def kblock(j, carry):
                out = []
                for h in HEADS:
                    dq, dcq = carry[h]
                    p = p_s[h, j]
                    ds = p * (dp_s[h, j] - deltas[h])
                    dsb = ds.astype(BF16)
                    dk_ref[_blk(j), _hs(h)] += _dot(dsb, qs[h], _TN)
                    dv_ref[_blk(j), _hs(h)] += _dot(p.astype(BF16), dos[h], _TN)
                    dcr_ref[h:h + 1, _blk(j)] -= jnp.sum(ds, axis=0, keepdims=True)
                    out.append((dq + _dot(dsb, k_ref[_blk(j), _hs(h)]), dcq + _rowsum(ds)))
                return tuple(out)

            zero = (jnp.zeros((BLOCK, HEAD_DIM), F32), jnp.zeros((BLOCK, 1), F32))
            res = lax.fori_loop(0, i + 1, kblock, (zero,) * N_HEADS)
            for h in HEADS:
                dq_ref[_blk(i), _hs(h)] = res[h][0] * Q_SCALE
                dcc_ref[_blk(i), h:h + 1] = res[h][1]
            return 0

        lax.fori_loop(0, NB, qblock, 0)

    out = pl.BlockSpec((SEQ, GROUP_W), lambda b: (b, 0))
    colb = pl.BlockSpec((SEQ, LANES), lambda b: (b, 0))
    rowb = pl.BlockSpec((8, SEQ), lambda b: (b, 0))
    sd = jax.ShapeDtypeStruct
    big = sd((T, GROUP_W), F32)
    return pl.pallas_call(
        body, name=name, grid=(T // SEQ,),
        in_specs=_qkv_specs(3) + [colb, rowb, colb, pl.BlockSpec((SEQ, GROUP_W), lambda b: (b, 1))],
        out_specs=[out, out, out, colb, rowb],
        out_shape=[big, big, big, sd((T, LANES), F32), sd((T // SEQ * 8, SEQ), F32)],
        scratch_shapes=[pltpu.VMEM((N_HEADS, NB, BLOCK, BLOCK), F32), pltpu.VMEM((N_HEADS, NB, BLOCK, BLOCK), F32)],
        compiler_params=_params("parallel"),
    )(qkv, qkv, qkv, cumc, cumr, lse, dmixed)


CHUNK = 256
WIDE = N_HEADS * CHUNK
NCH = SEQ // CHUNK


def _seg(h):
    return slice(h * CHUNK, (h + 1) * CHUNK)


def _chunk_rows(c):
    return pl.ds(pl.multiple_of(c * CHUNK, CHUNK), CHUNK)


def _wide_consts():
    r = lax.broadcasted_iota(jnp.int32, (WIDE, GROUP_W), 0)
    f = lax.broadcasted_iota(jnp.int32, (WIDE, GROUP_W), 1)
    bd = (r // CHUNK) == (f // HEAD_DIM)
    row = lax.broadcasted_iota(jnp.int32, (BLOCK, WIDE), 0)
    key = lax.broadcasted_iota(jnp.int32, (BLOCK, WIDE), 1) % CHUNK
    return bd, row, key


def _block_diag(x, bd):
    return jnp.where(bd, jnp.concatenate([x] * N_HEADS, axis=0), jnp.zeros((), x.dtype))


def _fold_heads(w, bd):
    w = jnp.where(bd, w, 0.0)
    return (w[0:CHUNK] + w[CHUNK:2 * CHUNK]) + (w[2 * CHUNK:3 * CHUNK] + w[3 * CHUNK:])


def _widen(cols):
    return jnp.concatenate([jnp.broadcast_to(c, (BLOCK, CHUNK)) for c in cols], axis=1)


def _head_rowsums(w):
    return [jnp.sum(w[:, _seg(h)], axis=1, keepdims=True) for h in HEADS]


def _tri_wide(x, tri):
    hi = x.astype(BF16)
    lo = (x - hi.astype(F32)).astype(BF16)
    y = _dot(jnp.concatenate([hi[:, _seg(h)] for h in HEADS] + [lo[:, _seg(h)] for h in HEADS], axis=0), tri)
    return jnp.concatenate([y[h * BLOCK:(h + 1) * BLOCK] + y[(N_HEADS + h) * BLOCK:(N_HEADS + h + 1) * BLOCK] for h in HEADS], axis=1)


def _feature_widen(cols):
    return jnp.concatenate([jnp.broadcast_to(c, (BLOCK, HEAD_DIM)) for c in cols], axis=1)


def _sbw_tile(q, kbd, mask, later, csum):
    z = _dot(q, kbd, _NT)
    lk = -_softplus(z)
    if mask is not None:
        lk = jnp.where(mask, lk, 0.0)
    e = z + lk
    att = jnp.exp(e + _tri_wide(lk, later) + csum)
    if mask is not None:
        att = jnp.where(mask, att, 0.0)
    return att, e, lk


def sbw_fwd(qkv, *, name):
    T = qkv.shape[0]

    def body(q_ref, k_ref, v_ref, o_ref):
        bd, row, key = _wide_consts()
        r2 = lax.broadcasted_iota(jnp.int32, (CHUNK, CHUNK), 0)
        c2 = lax.broadcasted_iota(jnp.int32, (CHUNK, CHUNK), 1)
        later = (r2 > c2).astype(BF16)

        def qblock(i, _):
            q = q_ref[_blk(i), :] * Q_SCALE
            cd = i // 2
            strict = key < row + BLOCK * (i % 2)

            def tile(c, mask, carry):
                acc, csum = carry
                att, _, lk = _sbw_tile(q, _block_diag(k_ref[_chunk_rows(c), :], bd), mask, later, csum)
                acc = acc + _dot(att.astype(BF16), _block_diag(v_ref[_chunk_rows(c), :], bd))
                return acc, csum + _widen(_head_rowsums(lk))

            carry = tile(cd, strict, (jnp.zeros((BLOCK, GROUP_W), F32), jnp.zeros((BLOCK, WIDE), F32)))
            acc, _ = lax.fori_loop(0, cd, lambda n, cr: tile(cd - 1 - n, None, cr), carry)
            o_ref[_blk(i), :] = acc.astype(BF16)
            return 0

        lax.fori_loop(0, NB, qblock, 0)

    return pl.pallas_call(
        body, name=name, grid=(T // SEQ,), in_specs=_qkv_specs(0),
        out_specs=pl.BlockSpec((SEQ, GROUP_W), lambda b: (b, 0)), out_shape=jax.ShapeDtypeStruct((T, GROUP_W), BF16),
        compiler_params=_params("parallel"),
    )(qkv, qkv, qkv)


def sbw_bwd(qkv, dmixed, *, name):
    T = qkv.shape[0]

    def body(q_ref, k_ref, v_ref, do_ref, dq_ref, dk_ref, dv_ref, att_s, sg_s):
        bd, row, key = _wide_consts()
        r2 = lax.broadcasted_iota(jnp.int32, (CHUNK, CHUNK), 0)
        c2 = lax.broadcasted_iota(jnp.int32, (CHUNK, CHUNK), 1)
        later = (r2 > c2).astype(BF16)
        earlier = (r2 < c2).astype(BF16)
        dk_ref[...] = jnp.zeros_like(dk_ref)
        dv_ref[...] = jnp.zeros_like(dv_ref)

        def qblock(i, _):
            q = q_ref[_blk(i), :] * Q_SCALE
            do = do_ref[_blk(i), :].astype(BF16)
            cd = i // 2
            strict = key < row + BLOCK * (i % 2)

            def recompute(c, mask, csum):
                att, e, lk = _sbw_tile(q, _block_diag(k_ref[_chunk_rows(c), :], bd), mask, later, csum)
                sg = jnp.exp(e)
                att_s[c] = att
                sg_s[c] = sg if mask is None else jnp.where(mask, sg, 0.0)
                return csum + _widen(_head_rowsums(lk))

            csum = recompute(cd, strict, jnp.zeros((BLOCK, WIDE), F32))
            lax.fori_loop(0, cd, lambda n, cs: recompute(cd - 1 - n, None, cs), csum)

            def tile(c, carry):
                dq, pre = carry
                kbd = _block_diag(k_ref[_chunk_rows(c), :], bd)
                vbd = _block_diag(v_ref[_chunk_rows(c), :], bd)
                att = att_s[c]
                ds = _dot(do, vbd, _NT) * att
                dlk = ds + _tri_wide(ds, earlier) + pre
                dz = (ds - dlk * sg_s[c]).astype(BF16)
                dk_ref[_chunk_rows(c), :] += _fold_heads(_dot(dz, q, _TN), bd)
                dv_ref[_chunk_rows(c), :] += _fold_heads(_dot(att.astype(BF16), do, _TN), bd)
                return dq + _dot(dz, kbd), pre + _widen(_head_rowsums(ds))

            dq, _ = lax.fori_loop(0, cd + 1, tile, (jnp.zeros((BLOCK, GROUP_W), F32), jnp.zeros((BLOCK, WIDE), F32)))
            dq_ref[_blk(i), :] = dq * Q_SCALE
            return 0

        lax.fori_loop(0, NB, qblock, 0)

    out = pl.BlockSpec((SEQ, GROUP_W), lambda b: (b, 0))
    sd = jax.ShapeDtypeStruct((T, GROUP_W), F32)
    return pl.pallas_call(
        body, name=name, grid=(T // SEQ,), in_specs=_qkv_specs(0) + [out],
        out_specs=[out] * 3, out_shape=[sd] * 3,
        scratch_shapes=[pltpu.VMEM((NCH, BLOCK, WIDE), F32), pltpu.VMEM((NCH, BLOCK, WIDE), F32)],
        compiler_params=_params("parallel"),
    )(qkv, qkv, qkv, dmixed)


def _foxw_logits(q, kbd, cq, cr_ref, c, mask):
    ck = jnp.concatenate([cr_ref[h:h + 1, _chunk_rows(c)] for h in HEADS], axis=1)
    z = _dot(q, kbd, _NT) + cq - ck
    return z if mask is None else jnp.where(mask, z, NEG)


def foxw_fwd(qkv, cumc, cumr, *, name):
    T = qkv.shape[0]

    def body(q_ref, k_ref, v_ref, cc_ref, cr_ref, o_ref, o32_ref, lse_ref, z_s):
        bd, row, key = _wide_consts()
        lse_ref[...] = jnp.zeros_like(lse_ref)

        def qblock(i, _):
            q = q_ref[_blk(i), :] * Q_SCALE
            cq = _widen([cc_ref[_blk(i), h:h + 1] for h in HEADS])
            cd = i // 2
            causal = key <= row + BLOCK * (i % 2)

            def logits(c, mask, ms):
                z = _foxw_logits(q, _block_diag(k_ref[_chunk_rows(c), :], bd), cq, cr_ref, c, mask)
                z_s[c] = z
                return tuple(jnp.maximum(ms[h], jnp.max(z[:, _seg(h)], axis=1, keepdims=True)) for h in HEADS)

            ms = logits(cd, causal, (jnp.full((BLOCK, 1), NEG, F32),) * N_HEADS)
            ms = lax.fori_loop(0, cd, lambda c, m: logits(c, None, m), ms)
            m_wide = _widen(ms)

            def values(c, carry):
                acc, l = carry
                p = jnp.exp(z_s[c] - m_wide)
                return acc + _dot(p.astype(BF16), _block_diag(v_ref[_chunk_rows(c), :], bd)), l + _widen(_head_rowsums(p))

            acc, l = lax.fori_loop(0, cd + 1, values, (jnp.zeros((BLOCK, GROUP_W), F32), jnp.zeros((BLOCK, WIDE), F32)))
            ls = [l[:, h * CHUNK:h * CHUNK + 1] for h in HEADS]
            o = acc / _feature_widen(ls)
            o_ref[_blk(i), :] = o.astype(BF16)
            o32_ref[_blk(i), :] = o
            for h in HEADS:
                lse_ref[_blk(i), h:h + 1] = ms[h] + jnp.log(ls[h])
            return 0

        lax.fori_loop(0, NB, qblock, 0)

    out = pl.BlockSpec((SEQ, GROUP_W), lambda b: (b, 0))
    colb = pl.BlockSpec((SEQ, LANES), lambda b: (b, 0))
    sd = jax.ShapeDtypeStruct
    return pl.pallas_call(
        body, name=name, grid=(T // SEQ,),
        in_specs=_qkv_specs(3) + [colb, pl.BlockSpec((8, SEQ), lambda b: (b, 0))],
        out_specs=[out, out, colb], out_shape=[sd((T, GROUP_W), BF16), sd((T, GROUP_W), F32), sd((T, LANES), F32)],
        scratch_shapes=[pltpu.VMEM((NCH, BLOCK, WIDE), F32)],
        compiler_params=_params("parallel"),
    )(qkv, qkv, qkv, cumc, cumr)


def foxw_bwd(qkv, cumc, cumr, lse, o32, dmixed, *, name):
    T = qkv.shape[0]

    def body(q_ref, k_ref, v_ref, cc_ref, cr_ref, lse_ref, o_ref, do_ref, dq_ref, dk_ref, dv_ref, dcc_ref, dcr_ref):
        bd, row, key = _wide_consts()
        dk_ref[...] = jnp.zeros_like(dk_ref)
        dv_ref[...] = jnp.zeros_like(dv_ref)
        dcc_ref[...] = jnp.zeros_like(dcc_ref)
        dcr_ref[...] = jnp.zeros_like(dcr_ref)

        def qblock(i, _):
            q = q_ref[_blk(i), :] * Q_SCALE
            do32 = do_ref[_blk(i), :]
            do = do32.astype(BF16)
            prod = do32 * o_ref[_blk(i), :]
            delta = _widen([jnp.sum(prod[:, _hs(h)], axis=1, keepdims=True) for h in HEADS])
            cq = _widen([cc_ref[_blk(i), h:h + 1] for h in HEADS])
            lse_w = _widen([lse_ref[_blk(i), h:h + 1] for h in HEADS])
            cd = i // 2
            causal = key <= row + BLOCK * (i % 2)

            def tile(c, mask, carry):
                dq, dcq = carry
                kbd = _block_diag(k_ref[_chunk_rows(c), :], bd)
                vbd = _block_diag(v_ref[_chunk_rows(c), :], bd)
                p = jnp.exp(_foxw_logits(q, kbd, cq, cr_ref, c, mask) - lse_w)
                ds = p * (_dot(do, vbd, _NT) - delta)
                dsb = ds.astype(BF16)
                dk_ref[_chunk_rows(c), :] += _fold_heads(_dot(dsb, q, _TN), bd)
                dv_ref[_chunk_rows(c), :] += _fold_heads(_dot(p.astype(BF16), do, _TN), bd)
                for h in HEADS:
                    dcr_ref[h:h + 1, _chunk_rows(c)] -= jnp.sum(ds[:, _seg(h)], axis=0, keepdims=True)
                return dq + _dot(dsb, kbd), dcq + _widen(_head_rowsums(ds))

            carry = tile(cd, causal, (jnp.zeros((BLOCK, GROUP_W), F32), jnp.zeros((BLOCK, WIDE), F32)))
            dq, dcq = lax.fori_loop(0, cd, lambda c, cr: tile(c, None, cr), carry)
            dq_ref[_blk(i), :] = dq * Q_SCALE
            for h in HEADS:
                dcc_ref[_blk(i), h:h + 1] = dcq[:, h * CHUNK:h * CHUNK + 1]
            return 0

        lax.fori_loop(0, NB, qblock, 0)

    out = pl.BlockSpec((SEQ, GROUP_W), lambda b: (b, 0))
    colb = pl.BlockSpec((SEQ, LANES), lambda b: (b, 0))
    rowb = pl.BlockSpec((8, SEQ), lambda b: (b, 0))
    sd = jax.ShapeDtypeStruct
    big = sd((T, GROUP_W), F32)
    return pl.pallas_call(
        body, name=name, grid=(T // SEQ,),
        in_specs=_qkv_specs(3) + [colb, rowb, colb, out, pl.BlockSpec((SEQ, GROUP_W), lambda b: (b, 1))],
        out_specs=[out, out, out, colb, rowb],
        out_shape=[big, big, big, sd((T, LANES), F32), sd((T // SEQ * 8, SEQ), F32)],
        compiler_params=_params("parallel"),
    )(qkv, qkv, qkv, cumc, cumr, lse, o32, dmixed)


BAND = 2 * BLOCK


def _t5_bucket_np(dist):
    n = np.maximum(dist, 0)
    max_exact = NUM_BUCKETS // 2
    nf = np.maximum(n, 1).astype(np.float32)
    large = max_exact + (np.log(nf / np.float32(max_exact)) / np.float32(math.log(MAX_DISTANCE / max_exact))
                         * np.float32(NUM_BUCKETS - max_exact)).astype(np.int32)
    large = np.minimum(large, NUM_BUCKETS - 1)
    return np.where(n < max_exact, n, large).astype(np.int32)


def _band_buckets():
    qi = np.arange(BLOCK)[:, None]
    ki = np.arange(BAND)[None, :]
    delta = np.clip(qi - ki + BLOCK, 0, BLOCK)
    return np.stack([_t5_bucket_np(delta * d) for d in DILATIONS])


def to_classes(a, d):
    if d == 1:
        return a
    T, C = a.shape
    return a.reshape(T // SEQ, SEQ // d, d, C).transpose(0, 2, 1, 3).reshape(T, C)


def from_classes(a, d):
    if d == 1:
        return a
    T, C = a.shape
    return a.reshape(T // SEQ, d, SEQ // d, C).transpose(0, 2, 1, 3).reshape(T, C)


def relbias_expand(rel, *, name):
    buckets = jnp.asarray(_band_buckets())
    n_pat = len(DILATIONS)

    def body(rel_ref, bk_ref, o_ref):
        for p in range(n_pat):
            bk = bk_ref[p]
            for h in range(N_HEADS):
                acc = jnp.zeros((BLOCK, BAND), F32)
                for b in range(NUM_BUCKETS):
                    acc = jnp.where(bk == b, rel_ref[b, h], acc)
                o_ref[p * N_HEADS + h] = acc

    return pl.pallas_call(
        body, name=name,
        in_specs=[pl.BlockSpec(memory_space=pltpu.SMEM), pl.BlockSpec(memory_space=pltpu.VMEM)],
        out_specs=pl.BlockSpec(memory_space=pltpu.VMEM),
        out_shape=jax.ShapeDtypeStruct((n_pat * N_HEADS, BLOCK, BAND), F32),
        compiler_params=_params(),
    )(rel, buckets)


def relbias_reduce(ds_all, *, name):
    buckets = jnp.asarray(_band_buckets())
    n_pat = len(DILATIONS)

    def body(ds_ref, bk_ref, o_ref):
        for b in range(NUM_BUCKETS):
            for h in range(N_HEADS):
                tot = jnp.float32(0.0)
                for p in range(n_pat):
                    tot = tot + jnp.sum(jnp.where(bk_ref[p] == b, ds_ref[p * N_HEADS + h], 0.0))
                o_ref[b, h] = tot

    return pl.pallas_call(
        body, name=name,
        in_specs=[pl.BlockSpec(memory_space=pltpu.VMEM), pl.BlockSpec(memory_space=pltpu.VMEM)],
        out_specs=pl.BlockSpec(memory_space=pltpu.SMEM),
        out_shape=jax.ShapeDtypeStruct((NUM_BUCKETS, N_HEADS), F32),
        compiler_params=_params(),
    )(ds_all, buckets)


def _band_valid_wide(first, row, key):
    inside = jnp.logical_and(key >= row, key <= row + BLOCK)
    return jnp.logical_and(inside, jnp.logical_or(jnp.logical_not(first), key >= BLOCK))


QKV_BLOCKS = 9


def _band_in_specs(d, pattern, has_prev):
    rows = BLOCK * d
    cur = lambda c: pl.BlockSpec((rows, GROUP_W), lambda tb, r: (tb, c))
    prev = lambda c: pl.BlockSpec((rows, GROUP_W), lambda tb, r: (jnp.maximum(tb - 1, 0), c))
    bias = pl.BlockSpec((N_HEADS, BLOCK, BAND), lambda tb, r: (pattern, 0, 0))
    return [cur(6), cur(7), cur(8)] + ([prev(7), prev(8)] if has_prev else []) + [bias]


def _class_rows(d):
    return pl.ds(pl.program_id(1), BLOCK, stride=d) if d > 1 else pl.ds(0, BLOCK)


def _halves_scratch(rows, n):
    return [pltpu.VMEM((2, rows, LANES), F32)] * n


def _stage(refs, scratch):
    @pl.when(pl.program_id(1) == 0)
    def _():
        for src, dst in zip(refs, scratch):
            dst[0] = src[:, :LANES].astype(F32)
            dst[1] = src[:, LANES:].astype(F32)


def _take_class(s, d):
    rows = _class_rows(d)
    return jnp.concatenate([s.at[0][rows, :], s.at[1][rows, :]], axis=1)


def _put_class(s, d, x):
    rows = _class_rows(d)
    s.at[0][rows, :] = x[:, :LANES]
    s.at[1][rows, :] = x[:, LANES:]


def _flush(scratch, refs, d):
    @pl.when(pl.program_id(1) == d - 1)
    def _():
        for s, o in zip(scratch, refs):
            o[...] = jnp.concatenate([s[0], s[1]], axis=1)


def _band_operands(scratch, d, has_prev):
    take = lambda s: _take_class(s, d).astype(BF16)
    q = (_take_class(scratch[0], d) * Q_SCALE).astype(BF16)
    if has_prev:
        k = jnp.concatenate([take(scratch[3]), take(scratch[1])], axis=0)
        v = jnp.concatenate([take(scratch[4]), take(scratch[2])], axis=0)
    else:
        k = jnp.concatenate([jnp.zeros((BLOCK, GROUP_W), BF16), take(scratch[1])], axis=0)
        v = jnp.concatenate([jnp.zeros((BLOCK, GROUP_W), BF16), take(scratch[2])], axis=0)
    return q, k, v


def _lane_columns(cols):
    lane = lax.broadcasted_iota(jnp.int32, (BLOCK, LANES), 1)
    out = jnp.zeros((BLOCK, LANES), F32)
    for h, c in enumerate(cols):
        out = jnp.where(lane == h, c, out)
    return out


def band_fwd(qkv, bias, pattern, *, name):
    T = qkv.shape[0]
    d = DILATIONS[pattern]
    rows_per_block = BLOCK * d
    seq_blocks = SEQ // rows_per_block
    has_prev = seq_blocks > 1
    n_in = 5 if has_prev else 3

    def body(*refs):
        ins, b_ref, o_ref, lse_ref = refs[:n_in], refs[n_in], refs[n_in + 1], refs[n_in + 2]
        staged, o_s = refs[n_in + 3:2 * n_in + 3], refs[2 * n_in + 3]
        bd, row, key = _wide_consts()
        valid = _band_valid_wide(pl.program_id(0) % seq_blocks == 0, row, key)
        _stage(ins, staged)
        q, k, v = _band_operands(staged, d, has_prev)
        kbd, vbd = _block_diag(k, bd), _block_diag(v, bd)
        bias_w = jnp.concatenate([b_ref[h] for h in HEADS], axis=1)
        sc = jnp.where(valid, _dot(q, kbd, _NT) + bias_w, NEG)
        ms = [jnp.max(sc[:, _seg(h)], axis=1, keepdims=True) for h in HEADS]
        p = jnp.exp(sc - _widen(ms))
        ls = _head_rowsums(p)
        _put_class(o_s, d, _dot(p.astype(BF16), vbd) / _feature_widen(ls))
        lse_ref[_class_rows(d), :] = _lane_columns([ms[h] + jnp.log(ls[h]) for h in HEADS])
        _flush([o_s], [o_ref], d)

    sd = jax.ShapeDtypeStruct
    return pl.pallas_call(
        body, name=name, grid=(T // rows_per_block, d), in_specs=_band_in_specs(d, pattern, has_prev),
        out_specs=[pl.BlockSpec((rows_per_block, GROUP_W), lambda tb, r: (tb, 0)),
                   pl.BlockSpec((rows_per_block, LANES), lambda tb, r: (tb, 0))],
        out_shape=[sd((T, GROUP_W), F32), sd((T, LANES), F32)],
        scratch_shapes=_halves_scratch(rows_per_block, n_in + 1),
        compiler_params=_params("parallel", "arbitrary"),
    )(*([qkv] * n_in), bias)


def band_bwd(qkv, bias, lse, do, dlse, pattern, *, name):
    T = qkv.shape[0]
    d = DILATIONS[pattern]
    rows_per_block = BLOCK * d
    seq_blocks = SEQ // rows_per_block
    has_prev = seq_blocks > 1
    n_in = 5 if has_prev else 3
    n_out = 5 if has_prev else 3

    def body(*refs):
        ins, b_ref, lse_ref, do_ref, dlse_ref = refs[:n_in], refs[n_in], refs[n_in + 1], refs[n_in + 2], refs[n_in + 3]
        outs = refs[n_in + 4:n_in + 4 + n_out]
        ds_ref = refs[n_in + 4 + n_out]
        scratch = refs[n_in + 5 + n_out:]
        staged, do_s, out_s = scratch[:n_in], scratch[n_in], scratch[n_in + 1:]
        first_step = jnp.logical_and(pl.program_id(0) == 0, pl.program_id(1) == 0)
        bd, row, key = _wide_consts()
        valid = _band_valid_wide(pl.program_id(0) % seq_blocks == 0, row, key)
        _stage(list(ins) + [do_ref], list(staged) + [do_s])
        q, k, v = _band_operands(staged, d, has_prev)
        kbd, vbd = _block_diag(k, bd), _block_diag(v, bd)
        rows = _class_rows(d)
        do = _take_class(do_s, d).astype(BF16)
        lse_t, dlse_t = lse_ref[rows, :], dlse_ref[rows, :]
        bias_w = jnp.concatenate([b_ref[h] for h in HEADS], axis=1)
        lse_w = _widen([lse_t[:, h:h + 1] for h in HEADS])
        dlse_w = _widen([dlse_t[:, h:h + 1] for h in HEADS])
        p = jnp.where(valid, jnp.exp(_dot(q, kbd, _NT) + bias_w - lse_w), 0.0)
        dp = _dot(do, vbd, _NT)
        ds = p * (dp - _widen(_head_rowsums(p * dp)) + dlse_w)
        dsb, pb = ds.astype(BF16), p.astype(BF16)
        _put_class(out_s[0], d, _dot(dsb, kbd) * Q_SCALE)
        dk = _fold_heads(_dot(dsb, q, _TN), bd)
        dv = _fold_heads(_dot(pb, do, _TN), bd)
        _put_class(out_s[1], d, dk[BLOCK:])
        _put_class(out_s[2], d, dv[BLOCK:])
        if has_prev:
            _put_class(out_s[3], d, dk[:BLOCK])
            _put_class(out_s[4], d, dv[:BLOCK])
        _flush(out_s, outs, d)

        @pl.when(first_step)
        def _():
            for h in HEADS:
                ds_ref[h] = ds[:, _seg(h)]

        @pl.when(jnp.logical_not(first_step))
        def _():
            for h in HEADS:
                ds_ref[h] += ds[:, _seg(h)]

    big = pl.BlockSpec((rows_per_block, GROUP_W), lambda tb, r: (tb, 0))
    colb = pl.BlockSpec((rows_per_block, LANES), lambda tb, r: (tb, 0))
    sd = jax.ShapeDtypeStruct
    return pl.pallas_call(
        body, name=name, grid=(T // rows_per_block, d), in_specs=_band_in_specs(d, pattern, has_prev) + [colb, big, colb],
        out_specs=[big] * n_out + [pl.BlockSpec((N_HEADS, BLOCK, BAND), lambda tb, r: (0, 0, 0))],
        out_shape=[sd((T, GROUP_W), F32)] * n_out + [sd((N_HEADS, BLOCK, BAND), F32)],
        scratch_shapes=_halves_scratch(rows_per_block, n_in + 1 + n_out),
        compiler_params=_params("arbitrary", "arbitrary"),
    )(*([qkv] * n_in), bias, lse, do, dlse)


def shift_add(cur, prev, d, *, name):
    rows = BLOCK * d
    nb = cur.shape[0] // rows

    def body(c_ref, p_ref, o_ref):
        keep = (pl.program_id(0) < nb - 1).astype(F32)
        o_ref[...] = c_ref[...] + keep * p_ref[...]

    blk = pl.BlockSpec((rows, GROUP_W), lambda tb: (tb, 0))
    nxt = pl.BlockSpec((rows, GROUP_W), lambda tb: (jnp.minimum(tb + 1, nb - 1), 0))
    return pl.pallas_call(
        body, name=name, grid=(nb,), in_specs=[blk, nxt], out_specs=blk,
        out_shape=jax.ShapeDtypeStruct(cur.shape, F32), compiler_params=_params("parallel"),
    )(cur, prev)


def _pattern_weights(lse_refs, h):
    ls = [r[:, h:h + 1] for r in lse_refs]
    mx = functools.reduce(jnp.maximum, ls)
    es = [jnp.exp(l - mx) for l in ls]
    tot = functools.reduce(lambda a, b: a + b, es)
    return [e / tot for e in es]


def dil_combine_fwd(outs, *, name):
    T = outs[0][0].shape[0]
    n = len(outs)
    tm = 512

    def body(*refs):
        o_refs, l_refs, out_ref = refs[:n], refs[n:2 * n], refs[2 * n]
        for h in range(N_HEADS):
            w = _pattern_weights(l_refs, h)
            acc = w[0] * o_refs[0][:, _hs(h)]
            for p in range(1, n):
                acc = acc + w[p] * o_refs[p][:, _hs(h)]
            out_ref[:, _hs(h)] = acc.astype(BF16)

    big = pl.BlockSpec((tm, GROUP_W), lambda i: (i, 0))
    colb = pl.BlockSpec((tm, LANES), lambda i: (i, 0))
    return pl.pallas_call(
        body, name=name, grid=(T // tm,), in_specs=[big] * n + [colb] * n,
        out_specs=big, out_shape=jax.ShapeDtypeStruct((T, GROUP_W), BF16),
        compiler_params=_params("parallel"),
    )(*[o for o, _ in outs], *[l for _, l in outs])


def dil_combine_bwd(outs, dmixed, *, name):
    T = outs[0][0].shape[0]
    n = len(outs)
    tm = 512

    def body(*refs):
        o_refs, l_refs, do_ref = refs[:n], refs[n:2 * n], refs[2 * n]
        do_refs, dl_refs = refs[2 * n + 1:3 * n + 1], refs[3 * n + 1:]
        for r in dl_refs:
            r[...] = jnp.zeros_like(r)
        for h in range(N_HEADS):
            w = _pattern_weights(l_refs, h)
            do = do_ref[:, _hs(h)]
            dw = [jnp.sum(do * o_refs[p][:, _hs(h)], axis=1, keepdims=True) for p in range(n)]
            mean = functools.reduce(lambda a, b: a + b, [w[p] * dw[p] for p in range(n)])
            for p in range(n):
                do_refs[p][:, _hs(h)] = w[p] * do
                dl_refs[p][:, h:h + 1] = w[p] * (dw[p] - mean)

    big = pl.BlockSpec((tm, GROUP_W), lambda i: (i, 0))
    colb = pl.BlockSpec((tm, LANES), lambda i: (i, 0))
    sd = jax.ShapeDtypeStruct
    res = pl.pallas_call(
        body, name=name, grid=(T // tm,),
        in_specs=[big] * n + [colb] * n + [pl.BlockSpec((tm, GROUP_W), lambda i: (i, 2))],
        out_specs=[big] * n + [colb] * n, out_shape=[sd((T, GROUP_W), F32)] * n + [sd((T, LANES), F32)] * n,
        compiler_params=_params("parallel"),
    )(*[o for o, _ in outs], *[l for _, l in outs], dmixed)
    return list(zip(res[:n], res[n:]))


def dilated_fwd(qkv, bias, tag):
    return [band_fwd(qkv, bias, p, name=f"{tag}_band_fwd{p}") for p in range(len(DILATIONS))]


def dilated_bwd(qkv, bias, outs, dmixed, tag):
    grads = dil_combine_bwd(outs, dmixed, name=f"{tag}_combine_bwd")
    parts, ds_all = [], []
    for p, d in enumerate(DILATIONS):
        (_, lse), (do, dlse) = outs[p], grads[p]
        res = band_bwd(qkv, bias, lse, do, dlse, p, name=f"{tag}_band_bwd{p}")
        dq, dk, dv, ds = res[0], res[1], res[2], res[-1]
        if len(res) > 4:
            dk = shift_add(dk, res[3], d, name=f"{tag}_dk{p}")
            dv = shift_add(dv, res[4], d, name=f"{tag}_dv{p}")
        parts.append([dq, dk, dv])
        ds_all.append(ds)
    return parts, jnp.concatenate(ds_all, axis=0)


def assemble_dqkv(d_sb, d_fox, d_dil, *, name):
    T = d_sb[0].shape[0]
    tr = 512
    n_pat = len(d_dil)
    flat = list(d_sb) + list(d_fox) + [a for part in d_dil for a in part]

    def body(*refs):
        o_ref = refs[-1]
        for j in range(6):
            o_ref[:, j * GROUP_W:(j + 1) * GROUP_W] = refs[j][...].astype(BF16)
        for j in range(3):
            acc = refs[6 + j][...]
            for p in range(1, n_pat):
                acc = acc + refs[6 + 3 * p + j][...]
            o_ref[:, (6 + j) * GROUP_W:(7 + j) * GROUP_W] = acc.astype(BF16)

    blk = pl.BlockSpec((tr, GROUP_W), lambda i: (i, 0))
    return pl.pallas_call(
        body, name=name, grid=(T // tr,), in_specs=[blk] * len(flat),
        out_specs=pl.BlockSpec((tr, QKV_BLOCKS * GROUP_W), lambda i: (i, 0)),
        out_shape=jax.ShapeDtypeStruct((T, QKV_BLOCKS * GROUP_W), BF16), compiler_params=_params("parallel"),
    )(*flat)


def sum_cast(arrs, dtype, *, name):
    R, C = arrs[0].shape
    tr = _largest_tile(R, 512, 16)
    n = len(arrs)

    def body(*refs):
        acc = refs[0][...].astype(F32)
        for r in refs[1:n]:
            acc = acc + r[...].astype(F32)
        refs[n][...] = acc.astype(dtype)

    blk = pl.BlockSpec((tr, C), lambda i: (i, 0))
    return pl.pallas_call(
        body, name=name, grid=(R // tr,), in_specs=[blk] * n, out_specs=blk, out_shape=jax.ShapeDtypeStruct((R, C), dtype),
        compiler_params=_params("parallel"),
    )(*arrs)


GRAD_WIRE = BF16


def _block_diag_halves(w):
    z = jnp.zeros((HEAD_DIM, HEAD_DIM), w.dtype)
    half = lambda a, b: jnp.concatenate([jnp.concatenate([a, z], axis=1), jnp.concatenate([z, b], axis=1)], axis=0)
    return jnp.stack([half(w[0], w[1]), half(w[2], w[3])]).astype(BF16)


def _diag_blocks(d):
    h = HEAD_DIM
    return jnp.stack([d[0, :h, :h], d[0, h:, h:], d[1, :h, :h], d[1, h:, h:]])


def layer_fwd(x, mem2d, W, P, bias, tag):
    s = {}
    s["x"] = x
    h1 = rmsnorm_fwd(x, P["norm_mix_g"], name=f"{tag}_norm_mix")
    qkv = matmul(h1, W["qkv"], out_dtype=BF16, name=f"{tag}_qkv")
    aux = matmul(h1, W["aux"], name=f"{tag}_aux")
    o_sb = sbw_fwd(qkv, name=f"{tag}_sb_fwd")
    cumc = fox_prep(aux, P["bf"], name=f"{tag}_fox_prep")
    cumr = col_to_row(cumc)
    o_fox, o_fox32, lse_fox = foxw_fwd(qkv, cumc, cumr, name=f"{tag}_fox_fwd")
    dil = dilated_fwd(qkv, bias, tag)
    o_dil = dil_combine_fwd(dil, name=f"{tag}_dil_combine")
    o_lru, h_lru = lru_fwd(aux, P["lru_conv_w"], P["lru_conv_b"], P["wa"], P["lru_b_a"], P["wx"], P["lru_b_x"],
                           P["lru_lambda"], name=f"{tag}_lru_fwd")
    mixed = jnp.concatenate([o_sb, o_fox, o_dil, o_lru], axis=1)
    x1 = matmul(mixed, W["out"], residual=x, name=f"{tag}_out")
    hq = rmsnorm_fwd(x1, P["norm_cross_g"], name=f"{tag}_norm_cross")
    qc = matmul(hq, W["cq"], out_dtype=BF16, name=f"{tag}_cq")
    memn = rmsnorm_fwd(mem2d, P["norm_mem_g"], name=f"{tag}_norm_mem")
    kv = matmul(memn, W["ckv"], out_dtype=BF16, name=f"{tag}_ckv")
    oc = cross_fwd(qc, kv, name=f"{tag}_cross_fwd")
    x2 = matmul(oc, W["coT"], trans_b=True, residual=x1, name=f"{tag}_co")
    h2 = rmsnorm_fwd(x2, P["norm_ffn_g"], name=f"{tag}_norm_ffn")
    hu = matmul(h2, W["up_u"], trans_b=True, name=f"{tag}_up_u")
    hg = matmul(h2, W["up_g"], trans_b=True, name=f"{tag}_up_g")
    act = glu_fwd(hu, hg, P["wu"], P["wg"], P["bu"], P["bg"], name=f"{tag}_glu_fwd")
    x3 = matmul(act, W["down"], residual=x2, name=f"{tag}_down")
    s.update(h1=h1, qkv=qkv, aux=aux, cumc=cumc, cumr=cumr, lse_fox=lse_fox, o_fox32=o_fox32, dil=dil, h_lru=h_lru, mixed=mixed,
             x1=x1, hq=hq, qc=qc, memn=memn, kv=kv, oc=oc, x2=x2, h2=h2, hu=hu, hg=hg, act=act)
    return x3, s


def layer_bwd(dx3, mem2d, W, P, bias, s, tag):
    mm = functools.partial(matmul, out_dtype=GRAD_WIRE, trans_a=True)
    gW, gP = {}, {}
    dact = matmul(dx3, W["down"], trans_b=True, name=f"{tag}_d_act")
    gW["down"] = mm(s["act"], dx3, name=f"{tag}_g_down")
    dcu, dcg, dwu, dwg, dbu, dbg = glu_bwd(s["hu"], s["hg"], dact, P["wu"], P["wg"], P["bu"], P["bg"], name=f"{tag}_glu_bwd")
    gP["ffn_conv_w"] = jnp.concatenate([dwu, dwg], axis=1)
    gP["ffn_conv_b"] = jnp.concatenate([dbu, dbg], axis=1)
    dhu = conv3_transpose(dcu, P["wu"], name=f"{tag}_convT_u")
    dhg = conv3_transpose(dcg, P["wg"], name=f"{tag}_convT_g")
    dh2 = matmul(dhu, W["up_u"], name=f"{tag}_d_h2u")
    dh2 = matmul(dhg, W["up_g"], residual=dh2, name=f"{tag}_d_h2g")
    gW["up_u"] = mm(dhu, s["h2"], name=f"{tag}_g_up_u")
    gW["up_g"] = mm(dhg, s["h2"], name=f"{tag}_g_up_g")
    dx2, gP["norm_ffn_g"] = rmsnorm_bwd(s["x2"], P["norm_ffn_g"], dh2, dx3, name=f"{tag}_norm_ffn_bwd")
    doc = matmul(dx2, W["coT"], name=f"{tag}_d_oc")
    gW["coT"] = mm(dx2, s["oc"], name=f"{tag}_g_co")
    dqc, dkv = cross_bwd(s["qc"], s["kv"], doc, name=f"{tag}_cross_bwd")
    dhq = matmul(dqc, W["cq"], trans_b=True, name=f"{tag}_d_hq")
    gW["cq"] = mm(s["hq"], dqc, name=f"{tag}_g_cq")
    dmemn = matmul(dkv, W["ckv"], trans_b=True, name=f"{tag}_d_memn")
    gW["ckv"] = mm(s["memn"], dkv, name=f"{tag}_g_ckv")
    _, gP["norm_mem_g"] = rmsnorm_bwd(mem2d, P["norm_mem_g"], dmemn, None, name=f"{tag}_norm_mem_bwd")
    dx1, gP["norm_cross_g"] = rmsnorm_bwd(s["x1"], P["norm_cross_g"], dhq, dx2, name=f"{tag}_norm_cross_bwd")
    dmixed = matmul(dx1, W["out"], trans_b=True, name=f"{tag}_d_mixed")
    gW["out"] = mm(s["mixed"], dx1, name=f"{tag}_g_out")
    qkv, aux = s["qkv"], s["aux"]
    d_sb = sbw_bwd(qkv, dmixed, name=f"{tag}_sb_bwd")
    dfq, dfk, dfv, dcc, dcr = foxw_bwd(qkv, s["cumc"], s["cumr"], s["lse_fox"], s["o_fox32"], dmixed, name=f"{tag}_fox_bwd")
    dcum = sum_cast([dcc, row_to_col(dcr)], F32, name=f"{tag}_dcum")
    df, dbf = fox_prep_bwd(aux, P["bf"], dcum, name=f"{tag}_fox_prep_bwd")
    gP["b_forget"] = dbf[0, :N_HEADS]
    d_dil, ds_band = dilated_bwd(qkv, bias, s["dil"], dmixed, tag)
    dlx, dlg, dcw, dcb, dwa, dba, dwx, dbx, dlam = lru_bwd(
        aux, s["h_lru"], dmixed, P["lru_conv_w"], P["lru_conv_b"], P["wa"], P["lru_b_a"], P["wx"], P["lru_b_x"],
        P["lru_lambda"], name=f"{tag}_lru_bwd")
    gP.update(lru_conv_w=dcw, lru_conv_b=dcb, lru_w_a=_diag_blocks(dwa), lru_b_a=dba, lru_w_x=_diag_blocks(dwx),
              lru_b_x=dbx, lru_lambda=dlam)
    dqkv = assemble_dqkv(d_sb, [dfq, dfk, dfv], d_dil, name=f"{tag}_dqkv")
    daux = jnp.concatenate([dlx, dlg, df], axis=1)
    dh1 = matmul(dqkv, W["qkv"], trans_b=True, name=f"{tag}_d_h1a")
    dh1 = matmul(daux, W["aux"], trans_b=True, residual=dh1, name=f"{tag}_d_h1b")
    gW["qkv"] = mm(s["h1"], dqkv, name=f"{tag}_g_qkv")
    gW["aux"] = mm(s["h1"], daux, name=f"{tag}_g_aux")
    dx, gP["norm_mix_g"] = rmsnorm_bwd(s["x"], P["norm_mix_g"], dh1, dx1, name=f"{tag}_norm_mix_bwd")
    return dx, gW, gP, ds_band


def local_step(x, mem, target, Ws, Ps, rel_bias, final_norm_g):
    B = x.shape[0]
    x2d = x.reshape(B * SEQ, D_MODEL)
    mem2d = mem.reshape(B * N_MEM, D_MODEL)
    bias = relbias_expand(rel_bias, name="relbias_expand")
    saved = []
    h = x2d
    for l in range(DEPTH):
        h, s = layer_fwd(h, mem2d, Ws[l], Ps[l], bias, f"l{l}")
        saved.append(s)
    loss, dh, d_final = loss_head(h, final_norm_g, target.reshape(B * SEQ, D_MODEL), name="loss_head")
    gWs, gPs, ds_bands = [None] * DEPTH, [None] * DEPTH, []
    for l in range(DEPTH - 1, -1, -1):
        dh, gWs[l], gPs[l], ds = layer_bwd(dh, mem2d, Ws[l], Ps[l], bias, saved[l], f"l{l}")
        ds_bands.append(ds)
    d_rel = relbias_reduce(sum_cast([d.reshape(-1, BAND) for d in ds_bands], F32, name="ds_band_sum").reshape(-1, BLOCK, BAND),
                           name="relbias_reduce")
    return loss, dh.reshape(B, SEQ, D_MODEL), gWs, gPs, d_rel, d_final


def small_params(p, l):
    row = lambda name: p[name][l].reshape(1, -1)
    ffn_w, ffn_b = p["ffn_conv_w"][l], row("ffn_conv_b")
    return dict(
        norm_mix_g=row("norm_mix_g"), norm_cross_g=row("norm_cross_g"), norm_mem_g=row("norm_mem_g"), norm_ffn_g=row("norm_ffn_g"),
        bf=jnp.pad(row("b_forget"), ((0, 0), (0, LANES - N_HEADS))),
        lru_conv_w=p["lru_conv_w"][l], lru_conv_b=row("lru_conv_b"), wa=_block_diag_halves(p["lru_w_a"][l]), lru_b_a=row("lru_b_a"),
        wx=_block_diag_halves(p["lru_w_x"][l]), lru_b_x=row("lru_b_x"), lru_lambda=row("lru_lambda"),
        wu=ffn_w[:, :D_FF], wg=ffn_w[:, D_FF:], bu=ffn_b[:, :D_FF], bg=ffn_b[:, D_FF:])


def canonical_weights(w_in, w_out, w_cq, w_ck, w_cv, w_co, w_up, w_down):
    sb_fox, fox_f, rest = w_in[:, :6 * GROUP_W], w_in[:, 6 * GROUP_W:6 * GROUP_W + N_HEADS], w_in[:, 6 * GROUP_W + N_HEADS:]
    dil, lru = rest[:, :3 * GROUP_W], rest[:, 3 * GROUP_W:]
    pad = jnp.zeros((w_in.shape[0], AUX_W - 2 * GROUP_W - N_HEADS), w_in.dtype)
    return dict(qkv=jnp.concatenate([sb_fox, dil], axis=1), aux=jnp.concatenate([lru, fox_f, pad], axis=1), out=w_out,
                cq=w_cq, ckv=jnp.concatenate([w_ck, w_cv], axis=1), coT=w_co.T, upT=w_up.T, down=w_down)


def native_grads(g):
    qkv, aux = g["qkv"], g["aux"]
    a, b = 6 * GROUP_W, 6 * GROUP_W + N_HEADS
    w_in = jnp.zeros((qkv.shape[0], b + 5 * GROUP_W), qkv.dtype)
    w_in = w_in.at[:, :a].set(qkv[:, :a]).at[:, a:b].set(aux[:, 2 * GROUP_W:2 * GROUP_W + N_HEADS])
    w_in = w_in.at[:, b:b + 3 * GROUP_W].set(qkv[:, a:]).at[:, b + 3 * GROUP_W:].set(aux[:, :2 * GROUP_W])
    return (w_in, g["out"], g["cq"], g["ckv"][:, :GROUP_W], g["ckv"][:, GROUP_W:], g["coT"].T, g["upT"].T, g["down"])


ANY = pl.BlockSpec(memory_space=pl.ANY)
VMEM_SPEC = pl.BlockSpec(memory_space=pltpu.VMEM)


def _place():
    x, y, c = lax.axis_index("x"), lax.axis_index("y"), lax.axis_index("c")
    other_chips = [(1 - x, y), (x, 1 - y), (1 - x, 1 - y)]
    return x, y, c, other_chips


def _gather_body(x_ref, out_ref, send_sems, recv_sems, local_sem):
    x, y, c, chips = _place()
    me, sibling = (x, y, c), (x, y, 1 - c)

    def slot(px, py, pc):
        return out_ref.at[4 * px + 2 * py + pc]

    def copy(k, block, to, src=None):
        return pltpu.make_async_remote_copy(
            src_ref=slot(*block) if src is None else src, dst_ref=slot(*block),
            send_sem=send_sems.at[k], recv_sem=recv_sems.at[k], device_id=to, device_id_type=MESH)

    if local_sem is not None:
        mine = pltpu.make_async_copy(x_ref, slot(*me), local_sem)
        mine.start()
    first = [copy(0, me, sibling, src=x_ref)]
    first += [copy(1 + j, me, (*chip, c), src=x_ref) for j, chip in enumerate(chips)]
    for cp in first:
        cp.start()
    passed = [copy(4 + j, (*chip, c), sibling) for j, chip in enumerate(chips)]
    for j, chip in enumerate(chips):
        copy(1 + j, (*chip, c), me).wait_recv()
        passed[j].start()
    copy(0, sibling, me).wait_recv()
    for j, chip in enumerate(chips):
        copy(4 + j, (*chip, 1 - c), me).wait_recv()
    for cp in first + passed:
        cp.wait_send()
    if local_sem is not None:
        mine.wait()


_GATHER_SEMS = [pltpu.SemaphoreType.DMA((7,)), pltpu.SemaphoreType.DMA((7,)), pltpu.SemaphoreType.DMA]


def allgather_hbm(shard, me, *, name):
    def body(x_ref, out_ref, send_sems, recv_sems):
        _gather_body(x_ref, out_ref, send_sems, recv_sems, None)

    others = pl.pallas_call(
        body, name=name, in_specs=[ANY], out_specs=ANY,
        out_shape=jax.ShapeDtypeStruct((N_DEV,) + shard.shape, shard.dtype), scratch_shapes=_GATHER_SEMS[:2],
    )(shard)
    return lax.dynamic_update_slice(others, shard[None], (me, 0, 0))


def allgather_small(x, *, name, reduce=False):
    def body(x_ref, out_ref, *rest):
        _gather_body(x_ref, out_ref, *rest[-3:])
        if reduce:
            acc = out_ref[0]
            for d in range(1, N_DEV):
                acc = acc + out_ref[d]
            rest[0][...] = acc

    sd = jax.ShapeDtypeStruct
    return pl.pallas_call(
        body, name=name, in_specs=[VMEM_SPEC], out_specs=[VMEM_SPEC, VMEM_SPEC] if reduce else VMEM_SPEC,
        out_shape=[sd((N_DEV,) + x.shape, x.dtype), sd(x.shape, x.dtype)] if reduce else sd((N_DEV,) + x.shape, x.dtype),
        scratch_shapes=_GATHER_SEMS, compiler_params=pltpu.CompilerParams(vmem_limit_bytes=VMEM_LIMIT_V7X),
    )(x)


N_CHIPS = 4


def pair_exchange(g, *, name):
    _, R, C = g.shape

    def body(g_ref, recv_ref, send_sems, recv_sems):
        x, y, c, _ = _place()
        sibling = (x, y, 1 - c)
        remote = [pltpu.make_async_remote_copy(
            src_ref=g_ref.at[2 * q + (1 - c)], dst_ref=recv_ref.at[q], send_sem=send_sems.at[q], recv_sem=recv_sems.at[q],
            device_id=sibling, device_id_type=MESH) for q in range(N_CHIPS)]
        for cp in remote:
            cp.start()
        for cp in remote:
            cp.wait_recv()
        for cp in remote:
            cp.wait_send()

    return pl.pallas_call(
        body, name=name, in_specs=[ANY], out_specs=ANY, out_shape=jax.ShapeDtypeStruct((N_CHIPS, R, C), g.dtype),
        scratch_shapes=[pltpu.SemaphoreType.DMA((N_CHIPS,))] * 2,
    )(g)


def chip_exchange(s, *, name):
    _, R, C = s.shape

    def body(s_ref, o0, o1, o2, send_sems, recv_sems):
        x, y, c, chips = _place()
        outs = (o0, o1, o2)
        copies = [pltpu.make_async_remote_copy(
            src_ref=s_ref.at[2 * cx + cy], dst_ref=outs[j], send_sem=send_sems.at[j], recv_sem=recv_sems.at[j],
            device_id=(cx, cy, c), device_id_type=MESH) for j, (cx, cy) in enumerate(chips)]
        for cp in copies:
            cp.start()
        for cp in copies:
            cp.wait_recv()
        for cp in copies:
            cp.wait_send()

    sd = jax.ShapeDtypeStruct((R, C), s.dtype)
    return pl.pallas_call(
        body, name=name, in_specs=[ANY], out_specs=[ANY] * 3, out_shape=[sd] * 3,
        scratch_shapes=[pltpu.SemaphoreType.DMA((3,)), pltpu.SemaphoreType.DMA((3,))],
    )(s)


WEIGHTS = ("norm_mix_g", "w_in", "b_forget", "lru_conv_w", "lru_conv_b", "lru_w_a", "lru_b_a", "lru_w_x", "lru_b_x", "lru_lambda",
           "w_out", "norm_cross_g", "norm_mem_g", "w_cq", "w_ck", "w_cv", "w_co", "norm_ffn_g", "w_up", "ffn_conv_w", "ffn_conv_b",
           "w_down", "rel_bias", "final_norm_g")
LARGE = ("w_in", "w_out", "w_cq", "w_ck", "w_cv", "w_co", "w_up", "w_down")
COLUMN_SPLIT_SMALL = ("lru_conv_w", "ffn_conv_w")
PACK = (("qkv", 128, 2304), ("aux", 128, 640), ("out", 128, 1024), ("cq", 128, 256), ("ckv", 128, 512), ("coT", 128, 256),
        ("upT", 704, 1024), ("down", 352, 1024))
PACK_W = 1024


def _pack_rows(parts):
    return jnp.concatenate([p.reshape(-1, PACK_W) for p in parts], axis=0)


def _pad_rows(flat, mult=8 * LANES):
    n = flat.shape[0]
    return jnp.pad(flat, (0, (-n) % mult)).reshape(-1, LANES)


def kernel(x, mem, norm_mix_g, w_in, b_forget, lru_conv_w, lru_conv_b, lru_w_a, lru_b_a, lru_w_x, lru_b_x, lru_lambda, w_out, norm_cross_g, norm_mem_g, w_cq, w_ck, w_cv, w_co, norm_ffn_g, w_up, ffn_conv_w, ffn_conv_b, w_down, rel_bias, final_norm_g, loss_target, m_norm_mix_g, m_w_in, m_b_forget, m_lru_conv_w, m_lru_conv_b, m_lru_w_a, m_lru_b_a, m_lru_w_x, m_lru_b_x, m_lru_lambda, m_w_out, m_norm_cross_g, m_norm_mem_g, m_w_cq, m_w_ck, m_w_cv, m_w_co, m_norm_ffn_g, m_w_up, m_ffn_conv_w, m_ffn_conv_b, m_w_down, m_rel_bias, m_final_norm_g, v_norm_mix_g, v_w_in, v_b_forget, v_lru_conv_w, v_lru_conv_b, v_lru_w_a, v_lru_b_a, v_lru_w_x, v_lru_b_x, v_lru_lambda, v_w_out, v_norm_cross_g, v_norm_mem_g, v_w_cq, v_w_ck, v_w_cv, v_w_co, v_norm_ffn_g, v_w_up, v_ffn_conv_w, v_ffn_conv_b, v_w_down, v_rel_bias, v_final_norm_g):
    w = dict(norm_mix_g=norm_mix_g, w_in=w_in, b_forget=b_forget, lru_conv_w=lru_conv_w, lru_conv_b=lru_conv_b, lru_w_a=lru_w_a,
             lru_b_a=lru_b_a, lru_w_x=lru_w_x, lru_b_x=lru_b_x, lru_lambda=lru_lambda, w_out=w_out, norm_cross_g=norm_cross_g,
             norm_mem_g=norm_mem_g, w_cq=w_cq, w_ck=w_ck, w_cv=w_cv, w_co=w_co, norm_ffn_g=norm_ffn_g, w_up=w_up,
             ffn_conv_w=ffn_conv_w, ffn_conv_b=ffn_conv_b, w_down=w_down, rel_bias=rel_bias, final_norm_g=final_norm_g)
    m = dict(norm_mix_g=m_norm_mix_g, w_in=m_w_in, b_forget=m_b_forget, lru_conv_w=m_lru_conv_w, lru_conv_b=m_lru_conv_b,
             lru_w_a=m_lru_w_a, lru_b_a=m_lru_b_a, lru_w_x=m_lru_w_x, lru_b_x=m_lru_b_x, lru_lambda=m_lru_lambda, w_out=m_w_out,
             norm_cross_g=m_norm_cross_g, norm_mem_g=m_norm_mem_g, w_cq=m_w_cq, w_ck=m_w_ck, w_cv=m_w_cv, w_co=m_w_co,
             norm_ffn_g=m_norm_ffn_g, w_up=m_w_up, ffn_conv_w=m_ffn_conv_w, ffn_conv_b=m_ffn_conv_b, w_down=m_w_down,
             rel_bias=m_rel_bias, final_norm_g=m_final_norm_g)
    v = dict(norm_mix_g=v_norm_mix_g, w_in=v_w_in, b_forget=v_b_forget, lru_conv_w=v_lru_conv_w, lru_conv_b=v_lru_conv_b,
             lru_w_a=v_lru_w_a, lru_b_a=v_lru_b_a, lru_w_x=v_lru_w_x, lru_b_x=v_lru_b_x, lru_lambda=v_lru_lambda, w_out=v_w_out,
             norm_cross_g=v_norm_cross_g, norm_mem_g=v_norm_mem_g, w_cq=v_w_cq, w_ck=v_w_ck, w_cv=v_w_cv, w_co=v_w_co,
             norm_ffn_g=v_norm_ffn_g, w_up=v_w_up, ffn_conv_w=v_ffn_conv_w, ffn_conv_b=v_ffn_conv_b, w_down=v_w_down,
             rel_bias=v_rel_bias, final_norm_g=v_final_norm_g)
    me = 4 * lax.axis_index("x") + 2 * lax.axis_index("y") + lax.axis_index("c")

    conv_shard = jnp.concatenate([w[n].reshape(-1) for n in COLUMN_SPLIT_SMALL])
    conv_all = allgather_small(_pad_rows(conv_shard), name="gather_conv").reshape(N_DEV, -1)
    full = dict(w)
    off = 0
    for n in COLUMN_SPLIT_SMALL:
        d, k, c = w[n].shape
        blocks = conv_all[:, off:off + d * k * c].reshape(N_DEV, d, k, c)
        full[n] = blocks.transpose(1, 2, 0, 3).reshape(d, k, N_DEV * c)
        off += d * k * c

    shard_parts = []
    for l in range(DEPTH):
        canon = canonical_weights(*[w[n][l] for n in LARGE])
        shard_parts += [canon[k].astype(BF16) for k, _, _ in PACK]
    packed = allgather_hbm(_pack_rows(shard_parts), me, name="gather_weights")
    Ws, row = [], 0
    for l in range(DEPTH):
        W = {}
        for k, r, c in PACK:
            n_rows = r * c // PACK_W
            W[k] = packed[:, row:row + n_rows].reshape(N_DEV * r, c)
            row += n_rows
        upT = W.pop("upT")
        W["up_u"], W["up_g"] = upT[:D_FF], upT[D_FF:]
        Ws.append(W)
    Ps = [small_params(full, l) for l in range(DEPTH)]

    loss, grad_x, gWs, gPs, d_rel, d_final = local_step(x, mem, loss_target, Ws, Ps, rel_bias, final_norm_g.reshape(1, -1))

    grad_parts = []
    for l in range(DEPTH):
        g = dict(gWs[l])
        g["upT"] = jnp.concatenate([g.pop("up_u"), g.pop("up_g")], axis=0)
        grad_parts += [g[k].reshape(N_DEV, r * c // PACK_W, PACK_W) for k, r, c in PACK]
    g_all = jnp.concatenate(grad_parts, axis=1)
    rows = g_all.shape[1]
    got = pair_exchange(g_all, name="grads_pair_exchange")
    own = lax.dynamic_index_in_dim(g_all.reshape(N_CHIPS, 2, rows, PACK_W), lax.axis_index("c"), axis=1, keepdims=False)
    pair = sum_cast([own.reshape(-1, PACK_W), got.reshape(-1, PACK_W)], GRAD_WIRE, name="grads_pair_sum").reshape(N_CHIPS, rows, PACK_W)
    from_x, from_y, from_xy = chip_exchange(pair, name="grads_chip_exchange")
    mine = lax.dynamic_index_in_dim(pair, 2 * lax.axis_index("x") + lax.axis_index("y"), axis=0, keepdims=False)
    g_shard = sum_cast([mine, from_x, from_y, from_xy], F32, name="grads_chip_sum")
    grads, row = {}, 0
    per_layer = []
    for l in range(DEPTH):
        g = {}
        for k, r, c in PACK:
            n_rows = r * c // PACK_W
            g[k] = g_shard[row:row + n_rows].reshape(r, c)
            row += n_rows
        per_layer.append(native_grads(g))
    for i, n in enumerate(LARGE):
        grads[n] = jnp.stack([per_layer[l][i] for l in range(DEPTH)])

    small_names = [n for n in WEIGHTS if n not in LARGE and n not in ("rel_bias", "final_norm_g")]
    pieces = [gPs[l][n].reshape(-1) for n in small_names for l in range(DEPTH)] + [d_rel.reshape(-1), d_final.reshape(-1), loss[0, :1]]
    sizes = [p.shape[0] for p in pieces]
    _, total = allgather_small(_pad_rows(jnp.concatenate(pieces)), name="allreduce_small", reduce=True)
    total = total.reshape(-1)
    off, it = 0, iter(sizes)
    for n in small_names:
        per = []
        for l in range(DEPTH):
            sz = next(it)
            per.append(total[off:off + sz])
            off += sz
        full_shape = (DEPTH,) + full[n].shape[1:]
        gfull = jnp.stack(per).reshape(full_shape)
        if n in COLUMN_SPLIT_SMALL:
            c = w[n].shape[-1]
            gfull = lax.dynamic_slice_in_dim(gfull, me * c, c, axis=gfull.ndim - 1)
        grads[n] = gfull
    grads["rel_bias"] = total[off:off + rel_bias.size].reshape(rel_bias.shape)
    off += rel_bias.size
    grads["final_norm_g"] = total[off:off + D_MODEL]
    off += D_MODEL
    loss_out = total[off]

    delta, new_m, new_v = {}, {}, {}
    for n in LARGE:
        shape = w[n].shape
        two_d = lambda a: a.reshape(-1, shape[-1])
        d_, m_, v_ = adamw(two_d(w[n]), two_d(grads[n]), two_d(m[n]), two_d(v[n]), name=f"adamw_{n}")
        delta[n], new_m[n], new_v[n] = d_.reshape(shape), m_.reshape(shape), v_.reshape(shape)
    small_all = [n for n in WEIGHTS if n not in LARGE]
    two_d = lambda a: a.reshape(-1, a.shape[-1])
    d_, m_, v_ = adamw_many(*[[two_d(src[n]) for n in small_all] for src in (w, grads, m, v)], name="adamw_small")
    for i, n in enumerate(small_all):
        delta[n], new_m[n], new_v[n] = (a[i].reshape(w[n].shape) for a in (d_, m_, v_))

    return (loss_out, grad_x, *[grads[n] for n in WEIGHTS], *[delta[n] for n in WEIGHTS], *[new_m[n] for n in WEIGHTS],
            *[new_v[n] for n in WEIGHTS])
```

```python
import functools
import math

import numpy as np
import jax
import jax.numpy as jnp
from jax import lax
from jax.experimental import pallas as pl
from jax.experimental.pallas import tpu as pltpu

F32 = jnp.float32
BF16 = jnp.bfloat16
MESH = pl.DeviceIdType.MESH

N_DEV = 8
D_MODEL = 1024
SEQ = 2048
DEPTH = 2
HEAD_DIM = 64
N_HEADS = 4
GROUP_W = N_HEADS * HEAD_DIM
D_FF = 2816
N_MEM = 256
NUM_BUCKETS = 32
MAX_DISTANCE = 2048
BLOCK = 128
DILATIONS = (1, 4, 16)
EPS = 1e-6
LRU_C = 8.0
Q_SCALE = HEAD_DIM ** -0.5
AUX_W = 640
LRU_HALF_W = 128
LRU_HALVES = GROUP_W // LRU_HALF_W
ADAM_LR, ADAM_B1, ADAM_B2, ADAM_EPS, ADAM_WD, ADAM_STEP = 0.001, 0.9, 0.999, 1e-08, 0.01, 10

VMEM_LIMIT_V7X = 48 * 1024 * 1024


def _params(*sem):
    return pltpu.CompilerParams(dimension_semantics=sem if sem else None, vmem_limit_bytes=VMEM_LIMIT_V7X)


def _pick(n, cands):
    for c in cands:
        if n % c == 0:
            return c
    return n


def _largest_tile(n, cap, align):
    best = None
    for t in range(align, min(n, cap) + 1, align):
        if n % t == 0:
            best = t
    return n if best is None else best


def matmul(a, b, *, name, trans_a=False, trans_b=False, out_dtype=F32, residual=None):
    (K, M) = a.shape if trans_a else a.shape[::-1]
    (N, Kb) = b.shape if trans_b else b.shape[::-1]
    assert K == Kb, (a.shape, b.shape)
    tm = _largest_tile(M, 1024 if trans_a else 512, 128)
    tn = _largest_tile(N, 1408, 128)
    tk = _largest_tile(K, 2816, 128)
    nk = K // tk
    a_spec = pl.BlockSpec((tk, tm), lambda i, j, k: (k, i)) if trans_a else pl.BlockSpec((tm, tk), lambda i, j, k: (i, k))
    b_spec = pl.BlockSpec((tn, tk), lambda i, j, k: (j, k)) if trans_b else pl.BlockSpec((tk, tn), lambda i, j, k: (k, j))
    o_spec = pl.BlockSpec((tm, tn), lambda i, j, k: (i, j))
    dims = (((0 if trans_a else 1,), (1 if trans_b else 0,)), ((), ()))
    has_res = residual is not None

    def body(*refs):
        a_ref, b_ref = refs[0], refs[1]
        r_ref = refs[2] if has_res else None
        part = lax.dot_general(a_ref[...].astype(BF16), b_ref[...].astype(BF16), dims, preferred_element_type=F32)
        if nk == 1:
            if has_res:
                part = part + r_ref[...].astype(F32)
            refs[-1][...] = part.astype(out_dtype)
            return
        o_ref, acc_ref = refs[-2], refs[-1]
        k = pl.program_id(2)

        @pl.when(k == 0)
        def _():
            acc_ref[...] = part

        @pl.when(k > 0)
        def _():
            acc_ref[...] += part

        @pl.when(k == nk - 1)
        def _():
            r = acc_ref[...]
            if has_res:
                r = r + r_ref[...].astype(F32)
            o_ref[...] = r.astype(out_dtype)

    ops = (a, b) + ((residual,) if has_res else ())
    return pl.pallas_call(
        body, name=name, grid=(M // tm, N // tn, nk),
        in_specs=[a_spec, b_spec] + ([o_spec] if has_res else []),
        out_specs=o_spec, out_shape=jax.ShapeDtypeStruct((M, N), out_dtype),
        scratch_shapes=[pltpu.VMEM((tm, tn), F32)] if nk > 1 else [],
        compiler_params=_params("parallel", "parallel", "arbitrary"),
    )(*ops)


def rmsnorm_fwd(x, g, *, name):
    R, D = x.shape
    tr = _pick(R, (512, 256))

    def body(x_ref, g_ref, o_ref):
        xv = x_ref[...]
        r = lax.rsqrt(jnp.mean(xv * xv, axis=-1, keepdims=True) + EPS)
        o_ref[...] = (xv * r * g_ref[...]).astype(BF16)

    return pl.pallas_call(
        body, name=name, grid=(R // tr,),
        in_specs=[pl.BlockSpec((tr, D), lambda i: (i, 0)), pl.BlockSpec((1, D), lambda i: (0, 0))],
        out_specs=pl.BlockSpec((tr, D), lambda i: (i, 0)), out_shape=jax.ShapeDtypeStruct((R, D), BF16),
        compiler_params=_params("parallel"),
    )(x, g)


def rmsnorm_bwd(x, g, dh, dres, *, name):
    R, D = x.shape
    tr = _pick(R, (512, 256))
    has_res = dres is not None

    def body(*refs):
        x_ref, g_ref, dh_ref = refs[:3]
        dx_ref, dg_ref = refs[-2], refs[-1]
        xv = x_ref[...]
        r = lax.rsqrt(jnp.mean(xv * xv, axis=-1, keepdims=True) + EPS)
        n = xv * r
        dhv = dh_ref[...]
        dn = dhv * g_ref[...]
        dx = r * (dn - n * jnp.mean(dn * n, axis=-1, keepdims=True))
        if has_res:
            dx = dx + refs[3][...]
        dx_ref[...] = dx
        part = jnp.sum(dhv * n, axis=0, keepdims=True)

        @pl.when(pl.program_id(0) == 0)
        def _():
            dg_ref[...] = part

        @pl.when(pl.program_id(0) > 0)
        def _():
            dg_ref[...] += part

    row = pl.BlockSpec((tr, D), lambda i: (i, 0))
    vec = pl.BlockSpec((1, D), lambda i: (0, 0))
    ops = (x, g, dh) + ((dres,) if has_res else ())
    return pl.pallas_call(
        body, name=name, grid=(R // tr,),
        in_specs=[row, vec, row] + ([row] if has_res else []),
        out_specs=[row, vec],
        out_shape=[jax.ShapeDtypeStruct((R, D), F32), jax.ShapeDtypeStruct((1, D), F32)],
        compiler_params=_params("arbitrary"),
    )(*ops)


_SQRT_HALF = 0.7071067811865476
_INV_SQRT_2PI = 0.3989422804014327


def _erf(x):
    ax = jnp.abs(x)
    t = 1.0 / (1.0 + 0.3275911 * ax)
    poly = t * (0.254829592 + t * (-0.284496736 + t * (1.421413741 + t * (-1.453152027 + t * 1.061405429))))
    y = 1.0 - poly * jnp.exp(-ax * ax)
    return jnp.where(x < 0, -y, y)


def _gelu_cdf(x):
    return 0.5 * (1.0 + _erf(x * _SQRT_HALF))


def _gelu_and_grad(x):
    cdf = _gelu_cdf(x)
    return x * cdf, cdf + x * _INV_SQRT_2PI * jnp.exp(-0.5 * x * x)


def _shift_down(main, halo, first, shifts):
    halo = jnp.where(first, 0.0, halo)
    ext = jnp.concatenate([halo, main], axis=0)
    return [pltpu.roll(ext, s, 0)[8:] for s in shifts]


def _conv3(main, halo, first, w, b):
    m1, m2 = _shift_down(main, halo, first, (1, 2))
    return ((b + w[0:1] * m2) + w[1:2] * m1) + w[2:3] * main, m1, m2


def glu_fwd(hu, hg, wu, wg, bu, bg, *, name):
    T, F = hu.shape
    tm, tf = 512, _largest_tile(F, 704, 128)
    hb = tm // 8
    blocks_per_example = SEQ // tm

    def body(hu_ref, hg_ref, hau_ref, hag_ref, wu_ref, wg_ref, bu_ref, bg_ref, o_ref):
        first = pl.program_id(0) % blocks_per_example == 0
        up, _, _ = _conv3(hu_ref[...], hau_ref[...], first, wu_ref[...], bu_ref[...])
        gate, _, _ = _conv3(hg_ref[...], hag_ref[...], first, wg_ref[...], bg_ref[...])
        o_ref[...] = (gate * _gelu_cdf(gate) * up).astype(BF16)

    main = pl.BlockSpec((tm, tf), lambda i, j: (i, j))
    halo = pl.BlockSpec((8, tf), lambda i, j: (jnp.maximum(i * hb - 1, 0), j))
    w3 = pl.BlockSpec((3, tf), lambda i, j: (0, j))
    b1 = pl.BlockSpec((1, tf), lambda i, j: (0, j))
    return pl.pallas_call(
        body, name=name, grid=(T // tm, F // tf),
        in_specs=[main, main, halo, halo, w3, w3, b1, b1],
        out_specs=main, out_shape=jax.ShapeDtypeStruct((T, F), BF16),
        compiler_params=_params("parallel", "parallel"),
    )(hu, hg, hu, hg, wu, wg, bu, bg)


def glu_bwd(hu, hg, dact, wu, wg, bu, bg, *, name):
    T, F = hu.shape
    tm, tf = 512, _largest_tile(F, 704, 128)
    hb = tm // 8
    blocks_per_example = SEQ // tm
    n_halo_blocks = T // 8
    n_ext = tm + 8

    def body(hu_ref, hg_ref, hau_ref, hag_ref, hnu_ref, hng_ref, da_ref, dan_ref, wu_ref, wg_ref, bu_ref, bg_ref,
             du_ref, dg_ref, dwu_ref, dwg_ref, dbu_ref, dbg_ref):
        i = pl.program_id(1)
        first = i % blocks_per_example == 0
        last = i % blocks_per_example == blocks_per_example - 1
        wu, wg = wu_ref[...], wg_ref[...]

        def conv_ext(main_ref, prev_ref, next_ref, w, b):
            ext = jnp.concatenate([jnp.where(first, 0.0, prev_ref[...]), main_ref[...], next_ref[...]], axis=0)
            x0, x1, x2 = ext[8:], pltpu.roll(ext, 1, 0)[8:], pltpu.roll(ext, 2, 0)[8:]
            return ((b + w[0:1] * x2) + w[1:2] * x1) + w[2:3] * x0, x0, x1, x2

        up, xu, u1, u2 = conv_ext(hu_ref, hau_ref, hnu_ref, wu, bu_ref[...])
        gate, xg, g1, g2 = conv_ext(hg_ref, hag_ref, hng_ref, wg, bg_ref[...])
        act, dact_dgate = _gelu_and_grad(gate)
        da = jnp.concatenate([da_ref[...], jnp.where(last, 0.0, dan_ref[...])], axis=0)
        dup = da * act
        dgate = da * up * dact_dgate

        def conv_t(d, w):
            return (w[2:3] * d[:tm] + w[1:2] * pltpu.roll(d, n_ext - 1, 0)[:tm] + w[0:1] * pltpu.roll(d, n_ext - 2, 0)[:tm]).astype(BF16)

        du_ref[...] = conv_t(dup, wu)
        dg_ref[...] = conv_t(dgate, wg)

        def sums(d, x0, x1, x2):
            s = lambda v: jnp.sum(v[:tm], axis=0, keepdims=True)
            return jnp.concatenate([s(d * x2), s(d * x1), s(d * x0)], axis=0), s(d)

        pwu, pbu = sums(dup, xu, u1, u2)
        pwg, pbg = sums(dgate, xg, g1, g2)

        @pl.when(i == 0)
        def _():
            dwu_ref[...] = pwu
            dwg_ref[...] = pwg
            dbu_ref[...] = pbu
            dbg_ref[...] = pbg

        @pl.when(i > 0)
        def _():
            dwu_ref[...] += pwu
            dwg_ref[...] += pwg
            dbu_ref[...] += pbu
            dbg_ref[...] += pbg

    main = pl.BlockSpec((tm, tf), lambda j, i: (i, j))
    before = pl.BlockSpec((8, tf), lambda j, i: (jnp.maximum(i * hb - 1, 0), j))
    after = pl.BlockSpec((8, tf), lambda j, i: (jnp.minimum((i + 1) * hb, n_halo_blocks - 1), j))
    w3 = pl.BlockSpec((3, tf), lambda j, i: (0, j))
    b1 = pl.BlockSpec((1, tf), lambda j, i: (0, j))
    sd = jax.ShapeDtypeStruct
    return pl.pallas_call(
        body, name=name, grid=(F // tf, T // tm),
        in_specs=[main, main, before, before, after, after, main, after, w3, w3, b1, b1],
        out_specs=[main, main, w3, w3, b1, b1],
        out_shape=[sd((T, F), BF16), sd((T, F), BF16), sd((3, F), F32), sd((3, F), F32), sd((1, F), F32), sd((1, F), F32)],
        compiler_params=_params("parallel", "arbitrary"),
    )(hu, hg, hu, hg, hu, hg, dact, dact, wu, wg, bu, bg)


def loss_head(x, g, target, *, name):
    T, D = x.shape
    tr = 256

    def body(x_ref, g_ref, t_ref, loss_ref, dx_ref, dg_ref):
        xv = x_ref[...]
        gv = g_ref[...]
        r = lax.rsqrt(jnp.mean(xv * xv, axis=-1, keepdims=True) + EPS)
        n = xv * r
        err = n * gv - t_ref[...]
        part_loss = jnp.zeros((1, 128), F32) + 0.5 * jnp.sum(jnp.mean(err * err, axis=-1, keepdims=True))
        dy = err * (1.0 / D)
        dn = dy * gv
        dx_ref[...] = r * (dn - n * jnp.mean(dn * n, axis=-1, keepdims=True))
        part_g = jnp.sum(dy * n, axis=0, keepdims=True)

        @pl.when(pl.program_id(0) == 0)
        def _():
            loss_ref[...] = part_loss
            dg_ref[...] = part_g

        @pl.when(pl.program_id(0) > 0)
        def _():
            loss_ref[...] += part_loss
            dg_ref[...] += part_g

    row = pl.BlockSpec((tr, D), lambda i: (i, 0))
    vec = pl.BlockSpec((1, D), lambda i: (0, 0))
    sd = jax.ShapeDtypeStruct
    return pl.pallas_call(
        body, name=name, grid=(T // tr,),
        in_specs=[row, vec, row],
        out_specs=[pl.BlockSpec((1, 128), lambda i: (0, 0)), row, vec],
        out_shape=[sd((1, 128), F32), sd((T, D), F32), sd((1, D), F32)],
        compiler_params=_params("arbitrary"),
    )(x, g, target)


def adamw(w, g, m, v, *, name):
    R, C = w.shape
    tr = _pick(R, (256, 128, 64, 32, 16, 8))

    def body(w_ref, g_ref, m_ref, v_ref, d_ref, nm_ref, nv_ref):
        gv = g_ref[...]
        mn = ADAM_B1 * m_ref[...] + (1.0 - ADAM_B1) * gv
        vn = ADAM_B2 * v_ref[...] + (1.0 - ADAM_B2) * (gv * gv)
        m_hat = mn / (1.0 - ADAM_B1 ** ADAM_STEP)
        v_hat = vn / (1.0 - ADAM_B2 ** ADAM_STEP)
        d_ref[...] = -ADAM_LR * (m_hat / (jnp.sqrt(v_hat) + ADAM_EPS) + ADAM_WD * w_ref[...])
        nm_ref[...] = mn
        nv_ref[...] = vn

    blk = pl.BlockSpec((tr, C), lambda i: (i, 0))
    sd = jax.ShapeDtypeStruct((R, C), F32)
    return pl.pallas_call(
        body, name=name, grid=(R // tr,), in_specs=[blk] * 4, out_specs=[blk] * 3, out_shape=[sd] * 3,
        compiler_params=_params("parallel"),
    )(w, g, m, v)


def adamw_many(ws, gs, ms, vs, *, name):
    n = len(ws)

    def body(*refs):
        ins, outs = refs[:4 * n], refs[4 * n:]
        for i in range(n):
            w_ref, g_ref, m_ref, v_ref = ins[i], ins[n + i], ins[2 * n + i], ins[3 * n + i]
            gv = g_ref[...]
            mn = ADAM_B1 * m_ref[...] + (1.0 - ADAM_B1) * gv
            vn = ADAM_B2 * v_ref[...] + (1.0 - ADAM_B2) * (gv * gv)
            m_hat = mn / (1.0 - ADAM_B1 ** ADAM_STEP)
            v_hat = vn / (1.0 - ADAM_B2 ** ADAM_STEP)
            outs[i][...] = -ADAM_LR * (m_hat / (jnp.sqrt(v_hat) + ADAM_EPS) + ADAM_WD * w_ref[...])
            outs[n + i][...] = mn
            outs[2 * n + i][...] = vn

    vm = pl.BlockSpec(memory_space=pltpu.VMEM)
    shapes = [jax.ShapeDtypeStruct(w.shape, F32) for w in ws]
    res = pl.pallas_call(
        body, name=name, in_specs=[vm] * (4 * n), out_specs=[vm] * (3 * n), out_shape=shapes * 3, compiler_params=_params(),
    )(*ws, *gs, *ms, *vs)
    return res[:n], res[n:2 * n], res[2 * n:]


def _softplus(x):
    return jnp.maximum(x, 0.0) + jnp.log(1.0 + jnp.exp(-jnp.abs(x)))


def _lru_gates(x, cw, cb, wa, ba, wx, bx, lam):
    S = x.shape[0]
    row = lax.broadcasted_iota(jnp.int32, (S, 1), 0)

    def back(s):
        return jnp.where(row >= s, pltpu.roll(x, s, 0), 0.0)

    xc = (((cb + cw[0:1] * back(3)) + cw[1:2] * back(2)) + cw[2:3] * back(1)) + cw[3:4] * x
    xb = xc.astype(BF16)
    r = jax.nn.sigmoid(jnp.dot(xb, wa, preferred_element_type=F32) + ba)
    ig = jax.nn.sigmoid(jnp.dot(xb, wx, preferred_element_type=F32) + bx)
    sp = _softplus(-lam)
    la = -LRU_C * r * sp
    a = jnp.exp(la)
    y = 2.0 * la
    one_minus_a2 = jnp.where(y > -0.05, -y * (1.0 + y * (0.5 + y * (1.0 / 6.0 + y * (1.0 / 24.0)))), 1.0 - jnp.exp(y))
    mm = jnp.sqrt(one_minus_a2)
    return xc, xb, r, ig, sp, a, mm


def lru_fwd(aux, cw, cb, wa, ba, wx, bx, lam, *, name):
    T = aux.shape[0]
    S, C = SEQ, LRU_HALF_W

    def body(x_ref, g_ref, cw_ref, cb_ref, wa_ref, ba_ref, wx_ref, bx_ref, lam_ref, o_ref, h_ref, a_s, u_s):
        xc, _, r, ig, sp, a, mm = _lru_gates(x_ref[...], cw_ref[...], cb_ref[...], wa_ref[...], ba_ref[...],
                                             wx_ref[...], bx_ref[...], lam_ref[...])
        a_s[...] = a
        u_s[...] = mm * (ig * xc)

        def group(i, h):
            base = pl.multiple_of(i * 8, 8)
            a8 = a_s[pl.ds(base, 8), :]
            u8 = u_s[pl.ds(base, 8), :]
            for rr in range(8):
                h = a8[rr:rr + 1] * h + u8[rr:rr + 1]
                h_ref[pl.ds(base + rr, 1), :] = h
            return h

        lax.fori_loop(0, S // 8, group, jnp.zeros((1, C), F32))
        gate = g_ref[...]
        o_ref[...] = (h_ref[...] * (gate * _gelu_cdf(gate))).astype(BF16)

    blk = lambda col: pl.BlockSpec((S, C), lambda c, b: (b, col + c))
    par = lambda rows: pl.BlockSpec((rows, C), lambda c, b: (0, c))
    sq = pl.BlockSpec((None, C, C), lambda c, b: (c, 0, 0))
    sd = jax.ShapeDtypeStruct
    W = LRU_HALVES * C
    return pl.pallas_call(
        body, name=name, grid=(LRU_HALVES, T // S),
        in_specs=[blk(0), blk(LRU_HALVES), par(4), par(1), sq, par(1), sq, par(1), par(1)],
        out_specs=[blk(0), blk(0)], out_shape=[sd((T, W), BF16), sd((T, W), F32)],
        scratch_shapes=[pltpu.VMEM((S, C), F32), pltpu.VMEM((S, C), F32)],
        compiler_params=_params("parallel", "parallel"),
    )(aux, aux, cw, cb, wa, ba, wx, bx, lam)


def lru_bwd(aux, h, dmixed, cw, cb, wa, ba, wx, bx, lam, *, name):
    T = aux.shape[0]
    S, C = SEQ, LRU_HALF_W

    def body(x_ref, g_ref, h_ref, do_ref, cw_ref, cb_ref, wa_ref, ba_ref, wx_ref, bx_ref, lam_ref,
             dx_ref, dgate_ref, dcw_ref, dcb_ref, dwa_ref, dba_ref, dwx_ref, dbx_ref, dlam_ref, a_s, d_s):
        x = x_ref[...]
        cw = cw_ref[...]
        lam = lam_ref[...]
        xc, xb, r, ig, sp, a, mm = _lru_gates(x, cw, cb_ref[...], wa_ref[...], ba_ref[...], wx_ref[...], bx_ref[...], lam)
        gate = g_ref[...]
        gl, dgl = _gelu_and_grad(gate)
        dout = do_ref[...]
        hv = h_ref[...]
        dgate_ref[...] = dout * hv * dgl
        a_s[...] = a
        d_s[...] = dout * gl

        def group(i, c):
            base = pl.multiple_of((S // 8 - 1 - i) * 8, 8)
            a8 = a_s[pl.ds(base, 8), :]
            d8 = d_s[pl.ds(base, 8), :]
            for rr in range(7, -1, -1):
                d = d8[rr:rr + 1] + c
                d_s[pl.ds(base + rr, 1), :] = d
                c = a8[rr:rr + 1] * d
            return c

        lax.fori_loop(0, S // 8, group, jnp.zeros((1, C), F32))
        row = lax.broadcasted_iota(jnp.int32, (S, 1), 0)
        dht = d_s[...]
        h_prev = jnp.where(row >= 1, pltpu.roll(hv, 1, 0), 0.0)
        da = dht * h_prev
        gx = ig * xc
        dmm = dht * gx
        dig = dht * mm * xc
        dxc = dht * mm * ig
        dla = da * a - dmm * (a * a) / mm
        dr = dla * (-LRU_C * sp)
        dsp = jnp.sum(dla * (-LRU_C * r), axis=0, keepdims=True)
        dlam = dsp * (-jax.nn.sigmoid(-lam))
        dpa = dr * r * (1.0 - r)
        dpx = dig * ig * (1.0 - ig)
        dpa_b, dpx_b = dpa.astype(BF16), dpx.astype(BF16)
        nt = (((1,), (1,)), ((), ()))
        tn = (((0,), (0,)), ((), ()))
        dxc = dxc + lax.dot_general(dpa_b, wa_ref[...], nt, preferred_element_type=F32) \
                  + lax.dot_general(dpx_b, wx_ref[...], nt, preferred_element_type=F32)
        dwa = lax.dot_general(xb, dpa_b, tn, preferred_element_type=F32)
        dwx = lax.dot_general(xb, dpx_b, tn, preferred_element_type=F32)

        def fwd(v, s):
            return jnp.where(row < S - s, pltpu.roll(v, S - s, 0), 0.0)

        def back(v, s):
            return jnp.where(row >= s, pltpu.roll(v, s, 0), 0.0)

        dx_ref[...] = cw[3:4] * dxc + cw[2:3] * fwd(dxc, 1) + cw[1:2] * fwd(dxc, 2) + cw[0:1] * fwd(dxc, 3)
        s0 = lambda v: jnp.sum(v, axis=0, keepdims=True)
        dcw = jnp.concatenate([s0(dxc * back(x, 3)), s0(dxc * back(x, 2)), s0(dxc * back(x, 1)), s0(dxc * x)], axis=0)
        parts = ((dcw_ref, dcw), (dcb_ref, s0(dxc)), (dwa_ref, dwa), (dba_ref, s0(dpa)), (dwx_ref, dwx),
                 (dbx_ref, s0(dpx)), (dlam_ref, dlam))

        @pl.when(pl.program_id(1) == 0)
        def _():
            for ref, val in parts:
                ref[...] = val

        @pl.when(pl.program_id(1) > 0)
        def _():
            for ref, val in parts:
                ref[...] += val

    blk = lambda col: pl.BlockSpec((S, C), lambda c, b: (b, col + c))
    par = lambda rows: pl.BlockSpec((rows, C), lambda c, b: (0, c))
    sq = pl.BlockSpec((None, C, C), lambda c, b: (c, 0, 0))
    sd = jax.ShapeDtypeStruct
    W = LRU_HALVES * C
    vec = sd((1, W), F32)
    return pl.pallas_call(
        body, name=name, grid=(LRU_HALVES, T // S),
        in_specs=[blk(0), blk(LRU_HALVES), blk(0), blk(3 * LRU_HALVES), par(4), par(1), sq, par(1), sq, par(1), par(1)],
        out_specs=[blk(0), blk(0), par(4), par(1), sq, par(1), sq, par(1), par(1)],
        out_shape=[sd((T, W), F32), sd((T, W), F32), sd((4, W), F32), vec, sd((LRU_HALVES, C, C), F32), vec,
                   sd((LRU_HALVES, C, C), F32), vec, vec],
        scratch_shapes=[pltpu.VMEM((S, C), F32), pltpu.VMEM((S, C), F32)],
        compiler_params=_params("parallel", "arbitrary"),
    )(aux, aux, h, dmixed, cw, cb, wa, ba, wx, bx, lam)


_NT = (((1,), (1,)), ((), ()))
_TN = (((0,), (0,)), ((), ()))


def _dot(a, b, dims=None):
    if dims is None:
        return jnp.dot(a, b, preferred_element_type=F32)
    return lax.dot_general(a, b, dims, preferred_element_type=F32)


def _hs(h):
    return slice(h * HEAD_DIM, (h + 1) * HEAD_DIM)


def cross_fwd(q, kv, *, name):
    T = q.shape[0]
    tq = 512

    def body(q_ref, kv_ref, o_ref):
        for h in range(N_HEADS):
            qh = q_ref[:, _hs(h)] * Q_SCALE
            k = kv_ref[:, _hs(h)]
            v = kv_ref[:, GROUP_W + h * HEAD_DIM:GROUP_W + (h + 1) * HEAD_DIM]
            s = _dot(qh, k, _NT)
            p = jnp.exp(s - jnp.max(s, axis=-1, keepdims=True))
            p = p / jnp.sum(p, axis=-1, keepdims=True)
            o_ref[:, _hs(h)] = _dot(p.astype(BF16), v).astype(BF16)

    per = SEQ // tq
    return pl.pallas_call(
        body, name=name, grid=(T // tq,),
        in_specs=[pl.BlockSpec((tq, GROUP_W), lambda i: (i, 0)), pl.BlockSpec((N_MEM, 2 * GROUP_W), lambda i: (i // per, 0))],
        out_specs=pl.BlockSpec((tq, GROUP_W), lambda i: (i, 0)), out_shape=jax.ShapeDtypeStruct((T, GROUP_W), BF16),
        compiler_params=_params("parallel"),
    )(q, kv)


def cross_bwd(q, kv, do, *, name):
    T = q.shape[0]
    tq = 512
    per = SEQ // tq

    def body(q_ref, kv_ref, do_ref, dq_ref, dkv_ref):
        first = pl.program_id(0) % per == 0
        for h in range(N_HEADS):
            vs = slice(GROUP_W + h * HEAD_DIM, GROUP_W + (h + 1) * HEAD_DIM)
            qh = q_ref[:, _hs(h)] * Q_SCALE
            k = kv_ref[:, _hs(h)]
            v = kv_ref[:, vs]
            doh = do_ref[:, _hs(h)].astype(BF16)
            s = _dot(qh, k, _NT)
            p = jnp.exp(s - jnp.max(s, axis=-1, keepdims=True))
            p = p / jnp.sum(p, axis=-1, keepdims=True)
            dp = _dot(doh, v, _NT)
            ds = (p * (dp - jnp.sum(p * dp, axis=-1, keepdims=True))).astype(BF16)
            dq_ref[:, _hs(h)] = (_dot(ds, k) * Q_SCALE).astype(BF16)
            dk = _dot(ds, qh, _TN)
            dv = _dot(p.astype(BF16), doh, _TN)

            @pl.when(first)
            def _():
                dkv_ref[:, _hs(h)] = dk
                dkv_ref[:, vs] = dv

            @pl.when(jnp.logical_not(first))
            def _():
                dkv_ref[:, _hs(h)] += dk
                dkv_ref[:, vs] += dv

    qb = pl.BlockSpec((tq, GROUP_W), lambda i: (i, 0))
    kvb = pl.BlockSpec((N_MEM, 2 * GROUP_W), lambda i: (i // per, 0))
    sd = jax.ShapeDtypeStruct
    return pl.pallas_call(
        body, name=name, grid=(T // tq,),
        in_specs=[qb, kvb, qb], out_specs=[qb, kvb],
        out_shape=[sd((T, GROUP_W), BF16), sd(kv.shape, F32)],
        compiler_params=_params("arbitrary"),
    )(q, kv, do)


NB = SEQ // BLOCK
NEG = -1e30


def _split_dot(x, tri):
    hi = x.astype(BF16)
    lo = (x - hi.astype(F32)).astype(BF16)
    return _dot(hi, tri) + _dot(lo, tri)


def _blk(i):
    return pl.ds(pl.multiple_of(i * BLOCK, BLOCK), BLOCK)


def _iotas():
    row = lax.broadcasted_iota(jnp.int32, (BLOCK, BLOCK), 0)
    col = lax.broadcasted_iota(jnp.int32, (BLOCK, BLOCK), 1)
    return row, col


def _sb_scores(q, k, mask, later, csum, want_sigmoid=False):
    z = _dot(q, k, _NT)
    lk = -_softplus(z)
    if mask is not None:
        lk = jnp.where(mask, lk, 0.0)
    lka = _split_dot(lk, later) + csum
    att = jnp.exp(z + lk + lka)
    sg = jnp.exp(z + lk) if want_sigmoid else None
    if mask is not None:
        att = jnp.where(mask, att, 0.0)
        sg = jnp.where(mask, sg, 0.0) if want_sigmoid else None
    return att, sg, lk


def _rowsum(v):
    return jnp.sum(v, axis=1, keepdims=True)


HEADS = tuple(range(N_HEADS))


def _qkv_specs(first_col):
    return [pl.BlockSpec((SEQ, GROUP_W), lambda b, c=first_col + j: (b, c)) for j in range(3)]


LANES = 128
CUM_BLK = 256


def col_to_row(c):
    b = c.shape[0] // SEQ
    return c.reshape(b, SEQ, LANES)[:, :, :8].transpose(0, 2, 1).reshape(b * 8, SEQ)


def row_to_col(r):
    b = r.shape[0] // 8
    c = r.reshape(b, 8, SEQ).transpose(0, 2, 1)
    return jnp.pad(c, ((0, 0), (0, 0), (0, LANES - 8))).reshape(b * SEQ, LANES)


def fox_prep(aux, bf, *, name):
    T = aux.shape[0]

    def body(f_ref, b_ref, o_ref):
        row = lax.broadcasted_iota(jnp.int32, (CUM_BLK, CUM_BLK), 0)
        col = lax.broadcasted_iota(jnp.int32, (CUM_BLK, CUM_BLK), 1)
        upto = (col <= row).astype(BF16)
        carry = jnp.zeros((1, LANES), F32)
        for n in range(SEQ // CUM_BLK):
            rows = slice(n * CUM_BLK, (n + 1) * CUM_BLK)
            logf = -_softplus(-(f_ref[rows, :] + b_ref[...]))
            hi = logf.astype(BF16)
            lo = (logf - hi.astype(F32)).astype(BF16)
            cum = _dot(upto, hi) + _dot(upto, lo) + carry
            o_ref[rows, :] = cum
            carry = cum[CUM_BLK - 1:CUM_BLK]

    return pl.pallas_call(
        body, name=name, grid=(T // SEQ,),
        in_specs=[pl.BlockSpec((SEQ, LANES), lambda b: (b, 4)), pl.BlockSpec((1, LANES), lambda b: (0, 0))],
        out_specs=pl.BlockSpec((SEQ, LANES), lambda b: (b, 0)), out_shape=jax.ShapeDtypeStruct((T, LANES), F32),
        compiler_params=_params("parallel"),
    )(aux, bf)


def fox_prep_bwd(aux, bf, dcum, *, name):
    T = aux.shape[0]

    def body(f_ref, b_ref, d_ref, df_ref, db_ref):
        row = lax.broadcasted_iota(jnp.int32, (CUM_BLK, CUM_BLK), 0)
        col = lax.broadcasted_iota(jnp.int32, (CUM_BLK, CUM_BLK), 1)
        onward = (col >= row).astype(BF16)
        carry = jnp.zeros((1, LANES), F32)
        tot = jnp.zeros((1, LANES), F32)
        for n in range(SEQ // CUM_BLK - 1, -1, -1):
            rows = slice(n * CUM_BLK, (n + 1) * CUM_BLK)
            d = d_ref[rows, :]
            hi = d.astype(BF16)
            lo = (d - hi.astype(F32)).astype(BF16)
            dlogf = _dot(onward, hi) + _dot(onward, lo) + carry
            carry = dlogf[0:1]
            df = dlogf * jax.nn.sigmoid(-(f_ref[rows, :] + b_ref[...]))
            df_ref[rows, :] = df
            tot = tot + jnp.sum(df, axis=0, keepdims=True)

        @pl.when(pl.program_id(0) == 0)
        def _():
            db_ref[...] = tot

        @pl.when(pl.program_id(0) > 0)
        def _():
            db_ref[...] += tot

    blk = pl.BlockSpec((SEQ, LANES), lambda b: (b, 0))
    vec = pl.BlockSpec((1, LANES), lambda b: (0, 0))
    sd = jax.ShapeDtypeStruct
    return pl.pallas_call(
        body, name=name, grid=(T // SEQ,),
        in_specs=[pl.BlockSpec((SEQ, LANES), lambda b: (b, 4)), vec, blk],
        out_specs=[blk, vec], out_shape=[sd((T, LANES), F32), sd((1, LANES), F32)],
        compiler_params=_params("arbitrary"),
    )(aux, bf, dcum)


def _fox_logits(q, k, cq, ck, mask):
    z = _dot(q, k, _NT) + cq - ck
    return z if mask is None else jnp.where(mask, z, NEG)


def fox_fwd(qkv, cumc, cumr, *, name):
    T = qkv.shape[0]

    def body(q_ref, k_ref, v_ref, cc_ref, cr_ref, o_ref, lse_ref, z_s):
        row, col = _iotas()
        causal = col <= row
        lse_ref[...] = jnp.zeros_like(lse_ref)

        def qblock(i, _):
            qs = [q_ref[_blk(i), _hs(h)] * Q_SCALE for h in HEADS]
            cqs = [cc_ref[_blk(i), h:h + 1] for h in HEADS]

            def logits(j, mask, ms):
                out = []
                for h in HEADS:
                    z = _fox_logits(qs[h], k_ref[_blk(j), _hs(h)], cqs[h], cr_ref[h:h + 1, _blk(j)], mask)
                    z_s[h, j] = z
                    out.append(jnp.maximum(ms[h], jnp.max(z, axis=1, keepdims=True)))
                return tuple(out)

            ms = logits(i, causal, (jnp.full((BLOCK, 1), NEG, F32),) * N_HEADS)
            ms = lax.fori_loop(0, i, lambda j, c: logits(j, None, c), ms)

            def values(j, carry):
                out = []
                for h in HEADS:
                    acc, l = carry[h]
                    p = jnp.exp(z_s[h, j] - ms[h])
                    out.append((acc + _dot(p.astype(BF16), v_ref[_blk(j), _hs(h)]), l + _rowsum(p)))
                return tuple(out)

            zero = (jnp.zeros((BLOCK, HEAD_DIM), F32), jnp.zeros((BLOCK, 1), F32))
            res = lax.fori_loop(0, i + 1, values, (zero,) * N_HEADS)
            for h in HEADS:
                acc, l = res[h]
                o_ref[_blk(i), _hs(h)] = (acc / l).astype(BF16)
                lse_ref[_blk(i), h:h + 1] = ms[h] + jnp.log(l)
            return 0

        lax.fori_loop(0, NB, qblock, 0)

    out = pl.BlockSpec((SEQ, GROUP_W), lambda b: (b, 0))
    colb = pl.BlockSpec((SEQ, LANES), lambda b: (b, 0))
    sd = jax.ShapeDtypeStruct
    return pl.pallas_call(
        body, name=name, grid=(T // SEQ,),
        in_specs=_qkv_specs(3) + [colb, pl.BlockSpec((8, SEQ), lambda b: (b, 0))],
        out_specs=[out, colb], out_shape=[sd((T, GROUP_W), BF16), sd((T, LANES), F32)],
        scratch_shapes=[pltpu.VMEM((N_HEADS, NB, BLOCK, BLOCK), F32)],
        compiler_params=_params("parallel"),
    )(qkv, qkv, qkv, cumc, cumr)


def fox_bwd(qkv, cumc, cumr, lse, dmixed, *, name):
    T = qkv.shape[0]

    def body(q_ref, k_ref, v_ref, cc_ref, cr_ref, lse_ref, do_ref, dq_ref, dk_ref, dv_ref, dcc_ref, dcr_ref, p_s, dp_s):
        row, col = _iotas()
        causal = col <= row
        dk_ref[...] = jnp.zeros_like(dk_ref)
        dv_ref[...] = jnp.zeros_like(dv_ref)
        dcc_ref[...] = jnp.zeros_like(dcc_ref)
        dcr_ref[...] = jnp.zeros_like(dcr_ref)

        def qblock(i, _):
            qs = [q_ref[_blk(i), _hs(h)] * Q_SCALE for h in HEADS]
            dos = [do_ref[_blk(i), _hs(h)].astype(BF16) for h in HEADS]
            cqs = [cc_ref[_blk(i), h:h + 1] for h in HEADS]
            lses = [lse_ref[_blk(i), h:h + 1] for h in HEADS]

            def probs(j, mask, deltas):
                out = []
                for h in HEADS:
                    z = _fox_logits(qs[h], k_ref[_blk(j), _hs(h)], cqs[h], cr_ref[h:h + 1, _blk(j)], mask)
                    p = jnp.exp(z - lses[h])
                    dp = _dot(dos[h], v_ref[_blk(j), _hs(h)], _NT)
                    p_s[h, j] = p
                    dp_s[h, j] = dp
                    out.append(deltas[h] + _rowsum(p * dp))
                return tuple(out)

            deltas = probs(i, causal, (jnp.zeros((BLOCK, 1), F32),) * N_HEADS)
            deltas = lax.fori_loop(0, i, lambda j, c: probs(j, None, c), deltas)

            def kblock(j, carry):
                out = []
                for h in HEADS:
                    dq, dcq = carry[h]
                    p = p_s[h, j]
                    ds = p * (dp_s[h, j] - deltas[h])
                    dsb = ds.astype(BF16)
                    dk_ref[_blk(j), _hs(h)] += _dot(dsb, qs[h], _TN)
                    dv_ref[_blk(j), _hs(h)] += _dot(p.astype(BF16), dos[h], _TN)
                    dcr_ref[h:h + 1, _blk(j)] -= jnp.sum(ds, axis=0, keepdims=True)
                    out.append((dq + _dot(dsb, k_ref[_blk(j), _hs(h)]), dcq + _rowsum(ds)))
                return tuple(out)

            zero = (jnp.zeros((BLOCK, HEAD_DIM), F32), jnp.zeros((BLOCK, 1), F32))
            res = lax.fori_loop(0, i + 1, kblock, (zero,) * N_HEADS)
            for h in HEADS:
                dq_ref[_blk(i), _hs(h)] = res[h][0] * Q_SCALE
                dcc_ref[_blk(i), h:h + 1] = res[h][1]
            return 0

        lax.fori_loop(0, NB, qblock, 0)

    out = pl.BlockSpec((SEQ, GROUP_W), lambda b: (b, 0))
    colb = pl.BlockSpec((SEQ, LANES), lambda b: (b, 0))
    rowb = pl.BlockSpec((8, SEQ), lambda b: (b, 0))
    sd = jax.ShapeDtypeStruct
    big = sd((T, GROUP_W), F32)
    return pl.pallas_call(
        body, name=name, grid=(T // SEQ,),
        in_specs=_qkv_specs(3) + [colb, rowb, colb, pl.BlockSpec((SEQ, GROUP_W), lambda b: (b, 1))],
        out_specs=[out, out, out, colb, rowb],
        out_shape=[big, big, big, sd((T, LANES), F32), sd((T // SEQ * 8, SEQ), F32)],
        scratch_shapes=[pltpu.VMEM((N_HEADS, NB, BLOCK, BLOCK), F32), pltpu.VMEM((N_HEADS, NB, BLOCK, BLOCK), F32)],
        compiler_params=_params("parallel"),
    )(qkv, qkv, qkv, cumc, cumr, lse, dmixed)


CHUNK = 256
WIDE = N_HEADS * CHUNK
NCH = SEQ // CHUNK


def _seg(h):
    return slice(h * CHUNK, (h + 1) * CHUNK)


def _chunk_rows(c):
    return pl.ds(pl.multiple_of(c * CHUNK, CHUNK), CHUNK)


def _wide_consts():
    r = lax.broadcasted_iota(jnp.int32, (WIDE, GROUP_W), 0)
    f = lax.broadcasted_iota(jnp.int32, (WIDE, GROUP_W), 1)
    bd = (r // CHUNK) == (f // HEAD_DIM)
    row = lax.broadcasted_iota(jnp.int32, (BLOCK, WIDE), 0)
    key = lax.broadcasted_iota(jnp.int32, (BLOCK, WIDE), 1) % CHUNK
    return bd, row, key


def _block_diag(x, bd):
    return jnp.where(bd, jnp.concatenate([x] * N_HEADS, axis=0), jnp.zeros((), x.dtype))


def _fold_heads(w, bd):
    w = jnp.where(bd, w, 0.0)
    return (w[0:CHUNK] + w[CHUNK:2 * CHUNK]) + (w[2 * CHUNK:3 * CHUNK] + w[3 * CHUNK:])


def _widen(cols):
    return jnp.concatenate([jnp.broadcast_to(c, (BLOCK, CHUNK)) for c in cols], axis=1)


def _head_rowsums(w):
    return [jnp.sum(w[:, _seg(h)], axis=1, keepdims=True) for h in HEADS]


def _tri_wide(x, tri):
    hi = x.astype(BF16)
    lo = (x - hi.astype(F32)).astype(BF16)
    y = _dot(jnp.concatenate([hi[:, _seg(h)] for h in HEADS] + [lo[:, _seg(h)] for h in HEADS], axis=0), tri)
    return jnp.concatenate([y[h * BLOCK:(h + 1) * BLOCK] + y[(N_HEADS + h) * BLOCK:(N_HEADS + h + 1) * BLOCK] for h in HEADS], axis=1)


def _feature_widen(cols):
    return jnp.concatenate([jnp.broadcast_to(c, (BLOCK, HEAD_DIM)) for c in cols], axis=1)


def _sbw_tile(q, kbd, mask, later, csum):
    z = _dot(q, kbd, _NT)
    lk = -_softplus(z)
    if mask is not None:
        lk = jnp.where(mask, lk, 0.0)
    e = z + lk
    att = jnp.exp(e + _tri_wide(lk, later) + csum)
    if mask is not None:
        att = jnp.where(mask, att, 0.0)
    return att, e, lk


def sbw_fwd(qkv, *, name):
    T = qkv.shape[0]

    def body(q_ref, k_ref, v_ref, o_ref):
        bd, row, key = _wide_consts()
        r2 = lax.broadcasted_iota(jnp.int32, (CHUNK, CHUNK), 0)
        c2 = lax.broadcasted_iota(jnp.int32, (CHUNK, CHUNK), 1)
        later = (r2 > c2).astype(BF16)

        def qblock(i, _):
            q = q_ref[_blk(i), :] * Q_SCALE
            cd = i // 2
            strict = key < row + BLOCK * (i % 2)

            def tile(c, mask, carry):
                acc, csum = carry
                att, _, lk = _sbw_tile(q, _block_diag(k_ref[_chunk_rows(c), :], bd), mask, later, csum)
                acc = acc + _dot(att.astype(BF16), _block_diag(v_ref[_chunk_rows(c), :], bd))
                return acc, csum + _widen(_head_rowsums(lk))

            carry = tile(cd, strict, (jnp.zeros((BLOCK, GROUP_W), F32), jnp.zeros((BLOCK, WIDE), F32)))
            acc, _ = lax.fori_loop(0, cd, lambda n, cr: tile(cd - 1 - n, None, cr), carry)
            o_ref[_blk(i), :] = acc.astype(BF16)
            return 0

        lax.fori_loop(0, NB, qblock, 0)

    return pl.pallas_call(
        body, name=name, grid=(T // SEQ,), in_specs=_qkv_specs(0),
        out_specs=pl.BlockSpec((SEQ, GROUP_W), lambda b: (b, 0)), out_shape=jax.ShapeDtypeStruct((T, GROUP_W), BF16),
        compiler_params=_params("parallel"),
    )(qkv, qkv, qkv)


def sbw_bwd(qkv, dmixed, *, name):
    T = qkv.shape[0]

    def body(q_ref, k_ref, v_ref, do_ref, dq_ref, dk_ref, dv_ref, att_s, sg_s):
        bd, row, key = _wide_consts()
        r2 = lax.broadcasted_iota(jnp.int32, (CHUNK, CHUNK), 0)
        c2 = lax.broadcasted_iota(jnp.int32, (CHUNK, CHUNK), 1)
        later = (r2 > c2).astype(BF16)
        earlier = (r2 < c2).astype(BF16)
        dk_ref[...] = jnp.zeros_like(dk_ref)
        dv_ref[...] = jnp.zeros_like(dv_ref)

        def qblock(i, _):
            q = q_ref[_blk(i), :] * Q_SCALE
            do = do_ref[_blk(i), :].astype(BF16)
            cd = i // 2
            strict = key < row + BLOCK * (i % 2)

            def recompute(c, mask, csum):
                att, e, lk = _sbw_tile(q, _block_diag(k_ref[_chunk_rows(c), :], bd), mask, later, csum)
                sg = jnp.exp(e)
                att_s[c] = att
                sg_s[c] = sg if mask is None else jnp.where(mask, sg, 0.0)
                return csum + _widen(_head_rowsums(lk))

            csum = recompute(cd, strict, jnp.zeros((BLOCK, WIDE), F32))
            lax.fori_loop(0, cd, lambda n, cs: recompute(cd - 1 - n, None, cs), csum)

            def tile(c, carry):
                dq, pre = carry
                kbd = _block_diag(k_ref[_chunk_rows(c), :], bd)
                vbd = _block_diag(v_ref[_chunk_rows(c), :], bd)
                att = att_s[c]
                ds = _dot(do, vbd, _NT) * att
                dlk = ds + _tri_wide(ds, earlier) + pre
                dz = (ds - dlk * sg_s[c]).astype(BF16)
                dk_ref[_chunk_rows(c), :] += _fold_heads(_dot(dz, q, _TN), bd)
                dv_ref[_chunk_rows(c), :] += _fold_heads(_dot(att.astype(BF16), do, _TN), bd)
                return dq + _dot(dz, kbd), pre + _widen(_head_rowsums(ds))

            dq, _ = lax.fori_loop(0, cd + 1, tile, (jnp.zeros((BLOCK, GROUP_W), F32), jnp.zeros((BLOCK, WIDE), F32)))
            dq_ref[_blk(i), :] = dq * Q_SCALE
            return 0

        lax.fori_loop(0, NB, qblock, 0)

    out = pl.BlockSpec((SEQ, GROUP_W), lambda b: (b, 0))
    sd = jax.ShapeDtypeStruct((T, GROUP_W), F32)
    return pl.pallas_call(
        body, name=name, grid=(T // SEQ,), in_specs=_qkv_specs(0) + [out],
        out_specs=[out] * 3, out_shape=[sd] * 3,
        scratch_shapes=[pltpu.VMEM((NCH, BLOCK, WIDE), F32), pltpu.VMEM((NCH, BLOCK, WIDE), F32)],
        compiler_params=_params("parallel"),
    )(qkv, qkv, qkv, dmixed)


def _foxw_logits(q, kbd, cq, cr_ref, c, mask):
    ck = jnp.concatenate([cr_ref[h:h + 1, _chunk_rows(c)] for h in HEADS], axis=1)
    z = _dot(q, kbd, _NT) + cq - ck
    return z if mask is None else jnp.where(mask, z, NEG)


def foxw_fwd(qkv, cumc, cumr, *, name):
    T = qkv.shape[0]

    def body(q_ref, k_ref, v_ref, cc_ref, cr_ref, o_ref, o32_ref, lse_ref, z_s):
        bd, row, key = _wide_consts()
        lse_ref[...] = jnp.zeros_like(lse_ref)

        def qblock(i, _):
            q = q_ref[_blk(i), :] * Q_SCALE
            cq = _widen([cc_ref[_blk(i), h:h + 1] for h in HEADS])
            cd = i // 2
            causal = key <= row + BLOCK * (i % 2)

            def logits(c, mask, ms):
                z = _foxw_logits(q, _block_diag(k_ref[_chunk_rows(c), :], bd), cq, cr_ref, c, mask)
                z_s[c] = z
                return tuple(jnp.maximum(ms[h], jnp.max(z[:, _seg(h)], axis=1, keepdims=True)) for h in HEADS)

            ms = logits(cd, causal, (jnp.full((BLOCK, 1), NEG, F32),) * N_HEADS)
            ms = lax.fori_loop(0, cd, lambda c, m: logits(c, None, m), ms)
            m_wide = _widen(ms)

            def values(c, carry):
                acc, l = carry
                p = jnp.exp(z_s[c] - m_wide)
                return acc + _dot(p.astype(BF16), _block_diag(v_ref[_chunk_rows(c), :], bd)), l + _widen(_head_rowsums(p))

            acc, l = lax.fori_loop(0, cd + 1, values, (jnp.zeros((BLOCK, GROUP_W), F32), jnp.zeros((BLOCK, WIDE), F32)))
            ls = [l[:, h * CHUNK:h * CHUNK + 1] for h in HEADS]
            o = acc / _feature_widen(ls)
            o_ref[_blk(i), :] = o.astype(BF16)
            o32_ref[_blk(i), :] = o
            for h in HEADS:
                lse_ref[_blk(i), h:h + 1] = ms[h] + jnp.log(ls[h])
            return 0

        lax.fori_loop(0, NB, qblock, 0)

    out = pl.BlockSpec((SEQ, GROUP_W), lambda b: (b, 0))
    colb = pl.BlockSpec((SEQ, LANES), lambda b: (b, 0))
    sd = jax.ShapeDtypeStruct
    return pl.pallas_call(
        body, name=name, grid=(T // SEQ,),
        in_specs=_qkv_specs(3) + [colb, pl.BlockSpec((8, SEQ), lambda b: (b, 0))],
        out_specs=[out, out, colb], out_shape=[sd((T, GROUP_W), BF16), sd((T, GROUP_W), F32), sd((T, LANES), F32)],
        scratch_shapes=[pltpu.VMEM((NCH, BLOCK, WIDE), F32)],
        compiler_params=_params("parallel"),
    )(qkv, qkv, qkv, cumc, cumr)


def foxw_bwd(qkv, cumc, cumr, lse, o32, dmixed, *, name):
    T = qkv.shape[0]

    def body(q_ref, k_ref, v_ref, cc_ref, cr_ref, lse_ref, o_ref, do_ref, dq_ref, dk_ref, dv_ref, dcc_ref, dcr_ref):
        bd, row, key = _wide_consts()
        dk_ref[...] = jnp.zeros_like(dk_ref)
        dv_ref[...] = jnp.zeros_like(dv_ref)
        dcc_ref[...] = jnp.zeros_like(dcc_ref)
        dcr_ref[...] = jnp.zeros_like(dcr_ref)

        def qblock(i, _):
            q = q_ref[_blk(i), :] * Q_SCALE
            do32 = do_ref[_blk(i), :]
            do = do32.astype(BF16)
            prod = do32 * o_ref[_blk(i), :]
            delta = _widen([jnp.sum(prod[:, _hs(h)], axis=1, keepdims=True) for h in HEADS])
            cq = _widen([cc_ref[_blk(i), h:h + 1] for h in HEADS])
            lse_w = _widen([lse_ref[_blk(i), h:h + 1] for h in HEADS])
            cd = i // 2
            causal = key <= row + BLOCK * (i % 2)

            def tile(c, mask, carry):
                dq, dcq = carry
                kbd = _block_diag(k_ref[_chunk_rows(c), :], bd)
                vbd = _block_diag(v_ref[_chunk_rows(c), :], bd)
                p = jnp.exp(_foxw_logits(q, kbd, cq, cr_ref, c, mask) - lse_w)
                ds = p * (_dot(do, vbd, _NT) - delta)
                dsb = ds.astype(BF16)
                dk_ref[_chunk_rows(c), :] += _fold_heads(_dot(dsb, q, _TN), bd)
                dv_ref[_chunk_rows(c), :] += _fold_heads(_dot(p.astype(BF16), do, _TN), bd)
                for h in HEADS:
                    dcr_ref[h:h + 1, _chunk_rows(c)] -= jnp.sum(ds[:, _seg(h)], axis=0, keepdims=True)
                return dq + _dot(dsb, kbd), dcq + _widen(_head_rowsums(ds))

            carry = tile(cd, causal, (jnp.zeros((BLOCK, GROUP_W), F32), jnp.zeros((BLOCK, WIDE), F32)))
            dq, dcq = lax.fori_loop(0, cd, lambda c, cr: tile(c, None, cr), carry)
            dq_ref[_blk(i), :] = dq * Q_SCALE
            for h in HEADS:
                dcc_ref[_blk(i), h:h + 1] = dcq[:, h * CHUNK:h * CHUNK + 1]
            return 0

        lax.fori_loop(0, NB, qblock, 0)

    out = pl.BlockSpec((SEQ, GROUP_W), lambda b: (b, 0))
    colb = pl.BlockSpec((SEQ, LANES), lambda b: (b, 0))
    rowb = pl.BlockSpec((8, SEQ), lambda b: (b, 0))
    sd = jax.ShapeDtypeStruct
    big = sd((T, GROUP_W), F32)
    return pl.pallas_call(
        body, name=name, grid=(T // SEQ,),
        in_specs=_qkv_specs(3) + [colb, rowb, colb, out, pl.BlockSpec((SEQ, GROUP_W), lambda b: (b, 1))],
        out_specs=[out, out, out, colb, rowb],
        out_shape=[big, big, big, sd((T, LANES), F32), sd((T // SEQ * 8, SEQ), F32)],
        compiler_params=_params("parallel"),
    )(qkv, qkv, qkv, cumc, cumr, lse, o32, dmixed)


BAND = 2 * BLOCK


def _t5_bucket_np(dist):
    n = np.maximum(dist, 0)
    max_exact = NUM_BUCKETS // 2
    nf = np.maximum(n, 1).astype(np.float32)
    large = max_exact + (np.log(nf / np.float32(max_exact)) / np.float32(math.log(MAX_DISTANCE / max_exact))
                         * np.float32(NUM_BUCKETS - max_exact)).astype(np.int32)
    large = np.minimum(large, NUM_BUCKETS - 1)
    return np.where(n < max_exact, n, large).astype(np.int32)


def _band_buckets():
    qi = np.arange(BLOCK)[:, None]
    ki = np.arange(BAND)[None, :]
    delta = np.clip(qi - ki + BLOCK, 0, BLOCK)
    return np.stack([_t5_bucket_np(delta * d) for d in DILATIONS])


def to_classes(a, d):
    if d == 1:
        return a
    T, C = a.shape
    return a.reshape(T // SEQ, SEQ // d, d, C).transpose(0, 2, 1, 3).reshape(T, C)


def from_classes(a, d):
    if d == 1:
        return a
    T, C = a.shape
    return a.reshape(T // SEQ, d, SEQ // d, C).transpose(0, 2, 1, 3).reshape(T, C)


def relbias_expand(rel, *, name):
    buckets = jnp.asarray(_band_buckets())
    n_pat = len(DILATIONS)

    def body(rel_ref, bk_ref, o_ref):
        for p in range(n_pat):
            bk = bk_ref[p]
            for h in range(N_HEADS):
                acc = jnp.zeros((BLOCK, BAND), F32)
                for b in range(NUM_BUCKETS):
                    acc = jnp.where(bk == b, rel_ref[b, h], acc)
                o_ref[p * N_HEADS + h] = acc

    return pl.pallas_call(
        body, name=name,
        in_specs=[pl.BlockSpec(memory_space=pltpu.SMEM), pl.BlockSpec(memory_space=pltpu.VMEM)],
        out_specs=pl.BlockSpec(memory_space=pltpu.VMEM),
        out_shape=jax.ShapeDtypeStruct((n_pat * N_HEADS, BLOCK, BAND), F32),
        compiler_params=_params(),
    )(rel, buckets)


def relbias_reduce(ds_all, *, name):
    buckets = jnp.asarray(_band_buckets())
    n_pat = len(DILATIONS)

    def body(ds_ref, bk_ref, o_ref):
        for b in range(NUM_BUCKETS):
            for h in range(N_HEADS):
                tot = jnp.float32(0.0)
                for p in range(n_pat):
                    tot = tot + jnp.sum(jnp.where(bk_ref[p] == b, ds_ref[p * N_HEADS + h], 0.0))
                o_ref[b, h] = tot

    return pl.pallas_call(
        body, name=name,
        in_specs=[pl.BlockSpec(memory_space=pltpu.VMEM), pl.BlockSpec(memory_space=pltpu.VMEM)],
        out_specs=pl.BlockSpec(memory_space=pltpu.SMEM),
        out_shape=jax.ShapeDtypeStruct((NUM_BUCKETS, N_HEADS), F32),
        compiler_params=_params(),
    )(ds_all, buckets)


def _band_valid_wide(first, row, key):
    inside = jnp.logical_and(key >= row, key <= row + BLOCK)
    return jnp.logical_and(inside, jnp.logical_or(jnp.logical_not(first), key >= BLOCK))


QKV_BLOCKS = 9


def _band_in_specs(d, pattern, has_prev):
    rows = BLOCK * d
    cur = lambda c: pl.BlockSpec((rows, GROUP_W), lambda tb, r: (tb, c))
    prev = lambda c: pl.BlockSpec((rows, GROUP_W), lambda tb, r: (jnp.maximum(tb - 1, 0), c))
    bias = pl.BlockSpec((N_HEADS, BLOCK, BAND), lambda tb, r: (pattern, 0, 0))
    return [cur(6), cur(7), cur(8)] + ([prev(7), prev(8)] if has_prev else []) + [bias]


def _class_rows(d):
    return pl.ds(pl.program_id(1), BLOCK, stride=d) if d > 1 else pl.ds(0, BLOCK)


def _halves_scratch(rows, n):
    return [pltpu.VMEM((2, rows, LANES), F32)] * n


def _stage(refs, scratch):
    @pl.when(pl.program_id(1) == 0)
    def _():
        for src, dst in zip(refs, scratch):
            dst[0] = src[:, :LANES].astype(F32)
            dst[1] = src[:, LANES:].astype(F32)


def _take_class(s, d):
    rows = _class_rows(d)
    return jnp.concatenate([s.at[0][rows, :], s.at[1][rows, :]], axis=1)


def _put_class(s, d, x):
    rows = _class_rows(d)
    s.at[0][rows, :] = x[:, :LANES]
    s.at[1][rows, :] = x[:, LANES:]


def _flush(scratch, refs, d):
    @pl.when(pl.program_id(1) == d - 1)
    def _():
        for s, o in zip(scratch, refs):
            o[...] = jnp.concatenate([s[0], s[1]], axis=1)


def _band_operands(scratch, d, has_prev):
    take = lambda s: _take_class(s, d).astype(BF16)
    q = (_take_class(scratch[0], d) * Q_SCALE).astype(BF16)
    if has_prev:
        k = jnp.concatenate([take(scratch[3]), take(scratch[1])], axis=0)
        v = jnp.concatenate([take(scratch[4]), take(scratch[2])], axis=0)
    else:
        k = jnp.concatenate([jnp.zeros((BLOCK, GROUP_W), BF16), take(scratch[1])], axis=0)
        v = jnp.concatenate([jnp.zeros((BLOCK, GROUP_W), BF16), take(scratch[2])], axis=0)
    return q, k, v


def _lane_columns(cols):
    lane = lax.broadcasted_iota(jnp.int32, (BLOCK, LANES), 1)
    out = jnp.zeros((BLOCK, LANES), F32)
    for h, c in enumerate(cols):
        out = jnp.where(lane == h, c, out)
    return out


def band_fwd(qkv, bias, pattern, *, name):
    T = qkv.shape[0]
    d = DILATIONS[pattern]
    rows_per_block = BLOCK * d
    seq_blocks = SEQ // rows_per_block
    has_prev = seq_blocks > 1
    n_in = 5 if has_prev else 3

    def body(*refs):
        ins, b_ref, o_ref, lse_ref = refs[:n_in], refs[n_in], refs[n_in + 1], refs[n_in + 2]
        staged, o_s = refs[n_in + 3:2 * n_in + 3], refs[2 * n_in + 3]
        bd, row, key = _wide_consts()
        valid = _band_valid_wide(pl.program_id(0) % seq_blocks == 0, row, key)
        _stage(ins, staged)
        q, k, v = _band_operands(staged, d, has_prev)
        kbd, vbd = _block_diag(k, bd), _block_diag(v, bd)
        bias_w = jnp.concatenate([b_ref[h] for h in HEADS], axis=1)
        sc = jnp.where(valid, _dot(q, kbd, _NT) + bias_w, NEG)
        ms = [jnp.max(sc[:, _seg(h)], axis=1, keepdims=True) for h in HEADS]
        p = jnp.exp(sc - _widen(ms))
        ls = _head_rowsums(p)
        _put_class(o_s, d, _dot(p.astype(BF16), vbd) / _feature_widen(ls))
        lse_ref[_class_rows(d), :] = _lane_columns([ms[h] + jnp.log(ls[h]) for h in HEADS])
        _flush([o_s], [o_ref], d)

    sd = jax.ShapeDtypeStruct
    return pl.pallas_call(
        body, name=name, grid=(T // rows_per_block, d), in_specs=_band_in_specs(d, pattern, has_prev),
        out_specs=[pl.BlockSpec((rows_per_block, GROUP_W), lambda tb, r: (tb, 0)),
                   pl.BlockSpec((rows_per_block, LANES), lambda tb, r: (tb, 0))],
        out_shape=[sd((T, GROUP_W), F32), sd((T, LANES), F32)],
        scratch_shapes=_halves_scratch(rows_per_block, n_in + 1),
        compiler_params=_params("parallel", "arbitrary"),
    )(*([qkv] * n_in), bias)


def band_bwd(qkv, bias, lse, do, dlse, pattern, *, name):
    T = qkv.shape[0]
    d = DILATIONS[pattern]
    rows_per_block = BLOCK * d
    seq_blocks = SEQ // rows_per_block
    has_prev = seq_blocks > 1
    n_in = 5 if has_prev else 3
    n_out = 5 if has_prev else 3

    def body(*refs):
        ins, b_ref, lse_ref, do_ref, dlse_ref = refs[:n_in], refs[n_in], refs[n_in + 1], refs[n_in + 2], refs[n_in + 3]
        outs = refs[n_in + 4:n_in + 4 + n_out]
        ds_ref = refs[n_in + 4 + n_out]
        scratch = refs[n_in + 5 + n_out:]
        staged, do_s, out_s = scratch[:n_in], scratch[n_in], scratch[n_in + 1:]
        first_step = jnp.logical_and(pl.program_id(0) == 0, pl.program_id(1) == 0)
        bd, row, key = _wide_consts()
        valid = _band_valid_wide(pl.program_id(0) % seq_blocks == 0, row, key)
        _stage(list(ins) + [do_ref], list(staged) + [do_s])
        q, k, v = _band_operands(staged, d, has_prev)
        kbd, vbd = _block_diag(k, bd), _block_diag(v, bd)
        rows = _class_rows(d)
        do = _take_class(do_s, d).astype(BF16)
        lse_t, dlse_t = lse_ref[rows, :], dlse_ref[rows, :]
        bias_w = jnp.concatenate([b_ref[h] for h in HEADS], axis=1)
        lse_w = _widen([lse_t[:, h:h + 1] for h in HEADS])
        dlse_w = _widen([dlse_t[:, h:h + 1] for h in HEADS])
        p = jnp.where(valid, jnp.exp(_dot(q, kbd, _NT) + bias_w - lse_w), 0.0)
        dp = _dot(do, vbd, _NT)
        ds = p * (dp - _widen(_head_rowsums(p * dp)) + dlse_w)
        dsb, pb = ds.astype(BF16), p.astype(BF16)
        _put_class(out_s[0], d, _dot(dsb, kbd) * Q_SCALE)
        dk = _fold_heads(_dot(dsb, q, _TN), bd)
        dv = _fold_heads(_dot(pb, do, _TN), bd)
        _put_class(out_s[1], d, dk[BLOCK:])
        _put_class(out_s[2], d, dv[BLOCK:])
        if has_prev:
            _put_class(out_s[3], d, dk[:BLOCK])
            _put_class(out_s[4], d, dv[:BLOCK])
        _flush(out_s, outs, d)

        @pl.when(first_step)
        def _():
            for h in HEADS:
                ds_ref[h] = ds[:, _seg(h)]

        @pl.when(jnp.logical_not(first_step))
        def _():
            for h in HEADS:
                ds_ref[h] += ds[:, _seg(h)]

    big = pl.BlockSpec((rows_per_block, GROUP_W), lambda tb, r: (tb, 0))
    colb = pl.BlockSpec((rows_per_block, LANES), lambda tb, r: (tb, 0))
    sd = jax.ShapeDtypeStruct
    return pl.pallas_call(
        body, name=name, grid=(T // rows_per_block, d), in_specs=_band_in_specs(d, pattern, has_prev) + [colb, big, colb],
        out_specs=[big] * n_out + [pl.BlockSpec((N_HEADS, BLOCK, BAND), lambda tb, r: (0, 0, 0))],
        out_shape=[sd((T, GROUP_W), F32)] * n_out + [sd((N_HEADS, BLOCK, BAND), F32)],
        scratch_shapes=_halves_scratch(rows_per_block, n_in + 1 + n_out),
        compiler_params=_params("arbitrary", "arbitrary"),
    )(*([qkv] * n_in), bias, lse, do, dlse)


def shift_add(cur, prev, d, *, name):
    rows = BLOCK * d
    nb = cur.shape[0] // rows

    def body(c_ref, p_ref, o_ref):
        keep = (pl.program_id(0) < nb - 1).astype(F32)
        o_ref[...] = c_ref[...] + keep * p_ref[...]

    blk = pl.BlockSpec((rows, GROUP_W), lambda tb: (tb, 0))
    nxt = pl.BlockSpec((rows, GROUP_W), lambda tb: (jnp.minimum(tb + 1, nb - 1), 0))
    return pl.pallas_call(
        body, name=name, grid=(nb,), in_specs=[blk, nxt], out_specs=blk,
        out_shape=jax.ShapeDtypeStruct(cur.shape, F32), compiler_params=_params("parallel"),
    )(cur, prev)


def _pattern_weights(lse_refs, h):
    ls = [r[:, h:h + 1] for r in lse_refs]
    mx = functools.reduce(jnp.maximum, ls)
    es = [jnp.exp(l - mx) for l in ls]
    tot = functools.reduce(lambda a, b: a + b, es)
    return [e / tot for e in es]


def dil_combine_fwd(outs, *, name):
    T = outs[0][0].shape[0]
    n = len(outs)
    tm = 512

    def body(*refs):
        o_refs, l_refs, out_ref = refs[:n], refs[n:2 * n], refs[2 * n]
        for h in range(N_HEADS):
            w = _pattern_weights(l_refs, h)
            acc = w[0] * o_refs[0][:, _hs(h)]
            for p in range(1, n):
                acc = acc + w[p] * o_refs[p][:, _hs(h)]
            out_ref[:, _hs(h)] = acc.astype(BF16)

    big = pl.BlockSpec((tm, GROUP_W), lambda i: (i, 0))
    colb = pl.BlockSpec((tm, LANES), lambda i: (i, 0))
    return pl.pallas_call(
        body, name=name, grid=(T // tm,), in_specs=[big] * n + [colb] * n,
        out_specs=big, out_shape=jax.ShapeDtypeStruct((T, GROUP_W), BF16),
        compiler_params=_params("parallel"),
    )(*[o for o, _ in outs], *[l for _, l in outs])


def dil_combine_bwd(outs, dmixed, *, name):
    T = outs[0][0].shape[0]
    n = len(outs)
    tm = 512

    def body(*refs):
        o_refs, l_refs, do_ref = refs[:n], refs[n:2 * n], refs[2 * n]
        do_refs, dl_refs = refs[2 * n + 1:3 * n + 1], refs[3 * n + 1:]
        for r in dl_refs:
            r[...] = jnp.zeros_like(r)
        for h in range(N_HEADS):
            w = _pattern_weights(l_refs, h)
            do = do_ref[:, _hs(h)]
            dw = [jnp.sum(do * o_refs[p][:, _hs(h)], axis=1, keepdims=True) for p in range(n)]
            mean = functools.reduce(lambda a, b: a + b, [w[p] * dw[p] for p in range(n)])
            for p in range(n):
                do_refs[p][:, _hs(h)] = w[p] * do
                dl_refs[p][:, h:h + 1] = w[p] * (dw[p] - mean)

    big = pl.BlockSpec((tm, GROUP_W), lambda i: (i, 0))
    colb = pl.BlockSpec((tm, LANES), lambda i: (i, 0))
    sd = jax.ShapeDtypeStruct
    res = pl.pallas_call(
        body, name=name, grid=(T // tm,),
        in_specs=[big] * n + [colb] * n + [pl.BlockSpec((tm, GROUP_W), lambda i: (i, 2))],
        out_specs=[big] * n + [colb] * n, out_shape=[sd((T, GROUP_W), F32)] * n + [sd((T, LANES), F32)] * n,
        compiler_params=_params("parallel"),
    )(*[o for o, _ in outs], *[l for _, l in outs], dmixed)
    return list(zip(res[:n], res[n:]))


def dilated_fwd(qkv, bias, tag):
    return [band_fwd(qkv, bias, p, name=f"{tag}_band_fwd{p}") for p in range(len(DILATIONS))]


def dilated_bwd(qkv, bias, outs, dmixed, tag):
    grads = dil_combine_bwd(outs, dmixed, name=f"{tag}_combine_bwd")
    parts, ds_all = [], []
    for p, d in enumerate(DILATIONS):
        (_, lse), (do, dlse) = outs[p], grads[p]
        res = band_bwd(qkv, bias, lse, do, dlse, p, name=f"{tag}_band_bwd{p}")
        dq, dk, dv, ds = res[0], res[1], res[2], res[-1]
        if len(res) > 4:
            dk = shift_add(dk, res[3], d, name=f"{tag}_dk{p}")
            dv = shift_add(dv, res[4], d, name=f"{tag}_dv{p}")
        parts.append([dq, dk, dv])
        ds_all.append(ds)
    return parts, jnp.concatenate(ds_all, axis=0)


def assemble_dqkv(d_sb, d_fox, d_dil, *, name):
    T = d_sb[0].shape[0]
    tr = 512
    n_pat = len(d_dil)
    flat = list(d_sb) + list(d_fox) + [a for part in d_dil for a in part]

    def body(*refs):
        o_ref = refs[-1]
        for j in range(6):
            o_ref[:, j * GROUP_W:(j + 1) * GROUP_W] = refs[j][...].astype(BF16)
        for j in range(3):
            acc = refs[6 + j][...]
            for p in range(1, n_pat):
                acc = acc + refs[6 + 3 * p + j][...]
            o_ref[:, (6 + j) * GROUP_W:(7 + j) * GROUP_W] = acc.astype(BF16)

    blk = pl.BlockSpec((tr, GROUP_W), lambda i: (i, 0))
    return pl.pallas_call(
        body, name=name, grid=(T // tr,), in_specs=[blk] * len(flat),
        out_specs=pl.BlockSpec((tr, QKV_BLOCKS * GROUP_W), lambda i: (i, 0)),
        out_shape=jax.ShapeDtypeStruct((T, QKV_BLOCKS * GROUP_W), BF16), compiler_params=_params("parallel"),
    )(*flat)


def sum_cast(arrs, dtype, *, name):
    R, C = arrs[0].shape
    tr = _largest_tile(R, 512, 16)
    n = len(arrs)

    def body(*refs):
        acc = refs[0][...].astype(F32)
        for r in refs[1:n]:
            acc = acc + r[...].astype(F32)
        refs[n][...] = acc.astype(dtype)

    blk = pl.BlockSpec((tr, C), lambda i: (i, 0))
    return pl.pallas_call(
        body, name=name, grid=(R // tr,), in_specs=[blk] * n, out_specs=blk, out_shape=jax.ShapeDtypeStruct((R, C), dtype),
        compiler_params=_params("parallel"),
    )(*arrs)


GRAD_WIRE = BF16


def _block_diag_halves(w):
    z = jnp.zeros((HEAD_DIM, HEAD_DIM), w.dtype)
    half = lambda a, b: jnp.concatenate([jnp.concatenate([a, z], axis=1), jnp.concatenate([z, b], axis=1)], axis=0)
    return jnp.stack([half(w[0], w[1]), half(w[2], w[3])]).astype(BF16)


def _diag_blocks(d):
    h = HEAD_DIM
    return jnp.stack([d[0, :h, :h], d[0, h:, h:], d[1, :h, :h], d[1, h:, h:]])


def layer_fwd(x, mem2d, W, P, bias, tag):
    s = {}
    s["x"] = x
    h1 = rmsnorm_fwd(x, P["norm_mix_g"], name=f"{tag}_norm_mix")
    qkv = matmul(h1, W["qkv"], out_dtype=BF16, name=f"{tag}_qkv")
    aux = matmul(h1, W["aux"], name=f"{tag}_aux")
    o_sb = sbw_fwd(qkv, name=f"{tag}_sb_fwd")
    cumc = fox_prep(aux, P["bf"], name=f"{tag}_fox_prep")
    cumr = col_to_row(cumc)
    o_fox, o_fox32, lse_fox = foxw_fwd(qkv, cumc, cumr, name=f"{tag}_fox_fwd")
    dil = dilated_fwd(qkv, bias, tag)
    o_dil = dil_combine_fwd(dil, name=f"{tag}_dil_combine")
    o_lru, h_lru = lru_fwd(aux, P["lru_conv_w"], P["lru_conv_b"], P["wa"], P["lru_b_a"], P["wx"], P["lru_b_x"],
                           P["lru_lambda"], name=f"{tag}_lru_fwd")
    mixed = jnp.concatenate([o_sb, o_fox, o_dil, o_lru], axis=1)
    x1 = matmul(mixed, W["out"], residual=x, name=f"{tag}_out")
    hq = rmsnorm_fwd(x1, P["norm_cross_g"], name=f"{tag}_norm_cross")
    qc = matmul(hq, W["cq"], out_dtype=BF16, name=f"{tag}_cq")
    memn = rmsnorm_fwd(mem2d, P["norm_mem_g"], name=f"{tag}_norm_mem")
    kv = matmul(memn, W["ckv"], out_dtype=BF16, name=f"{tag}_ckv")
    oc = cross_fwd(qc, kv, name=f"{tag}_cross_fwd")
    x2 = matmul(oc, W["coT"], trans_b=True, residual=x1, name=f"{tag}_co")
    h2 = rmsnorm_fwd(x2, P["norm_ffn_g"], name=f"{tag}_norm_ffn")
    hu = matmul(h2, W["up_u"], trans_b=True, name=f"{tag}_up_u")
    hg = matmul(h2, W["up_g"], trans_b=True, name=f"{tag}_up_g")
    act = glu_fwd(hu, hg, P["wu"], P["wg"], P["bu"], P["bg"], name=f"{tag}_glu_fwd")
    x3 = matmul(act, W["down"], residual=x2, name=f"{tag}_down")
    s.update(h1=h1, qkv=qkv, aux=aux, cumc=cumc, cumr=cumr, lse_fox=lse_fox, o_fox32=o_fox32, dil=dil, h_lru=h_lru, mixed=mixed,
             x1=x1, hq=hq, qc=qc, memn=memn, kv=kv, oc=oc, x2=x2, h2=h2, hu=hu, hg=hg, act=act)
    return x3, s


def layer_bwd(dx3, mem2d, W, P, bias, s, tag):
    mm = functools.partial(matmul, out_dtype=GRAD_WIRE, trans_a=True)
    gW, gP = {}, {}
    dact = matmul(dx3, W["down"], trans_b=True, name=f"{tag}_d_act")
    gW["down"] = mm(s["act"], dx3, name=f"{tag}_g_down")
    dhu, dhg, dwu, dwg, dbu, dbg = glu_bwd(s["hu"], s["hg"], dact, P["wu"], P["wg"], P["bu"], P["bg"], name=f"{tag}_glu_bwd")
    gP["ffn_conv_w"] = jnp.concatenate([dwu, dwg], axis=1)
    gP["ffn_conv_b"] = jnp.concatenate([dbu, dbg], axis=1)
    dh2 = matmul(dhu, W["up_u"], name=f"{tag}_d_h2u")
    dh2 = matmul(dhg, W["up_g"], residual=dh2, name=f"{tag}_d_h2g")
    gW["up_u"] = mm(dhu, s["h2"], name=f"{tag}_g_up_u")
    gW["up_g"] = mm(dhg, s["h2"], name=f"{tag}_g_up_g")
    dx2, gP["norm_ffn_g"] = rmsnorm_bwd(s["x2"], P["norm_ffn_g"], dh2, dx3, name=f"{tag}_norm_ffn_bwd")
    doc = matmul(dx2, W["coT"], name=f"{tag}_d_oc")
    gW["coT"] = mm(dx2, s["oc"], name=f"{tag}_g_co")
    dqc, dkv = cross_bwd(s["qc"], s["kv"], doc, name=f"{tag}_cross_bwd")
    dhq = matmul(dqc, W["cq"], trans_b=True, name=f"{tag}_d_hq")
    gW["cq"] = mm(s["hq"], dqc, name=f"{tag}_g_cq")
    dmemn = matmul(dkv, W["ckv"], trans_b=True, name=f"{tag}_d_memn")
    gW["ckv"] = mm(s["memn"], dkv, name=f"{tag}_g_ckv")
    _, gP["norm_mem_g"] = rmsnorm_bwd(mem2d, P["norm_mem_g"], dmemn, None, name=f"{tag}_norm_mem_bwd")
    dx1, gP["norm_cross_g"] = rmsnorm_bwd(s["x1"], P["norm_cross_g"], dhq, dx2, name=f"{tag}_norm_cross_bwd")
    dmixed = matmul(dx1, W["out"], trans_b=True, name=f"{tag}_d_mixed")
    gW["out"] = mm(s["mixed"], dx1, name=f"{tag}_g_out")
    qkv, aux = s["qkv"], s["aux"]
    d_sb = sbw_bwd(qkv, dmixed, name=f"{tag}_sb_bwd")
    dfq, dfk, dfv, dcc, dcr = foxw_bwd(qkv, s["cumc"], s["cumr"], s["lse_fox"], s["o_fox32"], dmixed, name=f"{tag}_fox_bwd")
    dcum = sum_cast([dcc, row_to_col(dcr)], F32, name=f"{tag}_dcum")
    df, dbf = fox_prep_bwd(aux, P["bf"], dcum, name=f"{tag}_fox_prep_bwd")
    gP["b_forget"] = dbf[0, :N_HEADS]
    d_dil, ds_band = dilated_bwd(qkv, bias, s["dil"], dmixed, tag)
    dlx, dlg, dcw, dcb, dwa, dba, dwx, dbx, dlam = lru_bwd(
        aux, s["h_lru"], dmixed, P["lru_conv_w"], P["lru_conv_b"], P["wa"], P["lru_b_a"], P["wx"], P["lru_b_x"],
        P["lru_lambda"], name=f"{tag}_lru_bwd")
    gP.update(lru_conv_w=dcw, lru_conv_b=dcb, lru_w_a=_diag_blocks(dwa), lru_b_a=dba, lru_w_x=_diag_blocks(dwx),
              lru_b_x=dbx, lru_lambda=dlam)
    dqkv = assemble_dqkv(d_sb, [dfq, dfk, dfv], d_dil, name=f"{tag}_dqkv")
    daux = jnp.concatenate([dlx, dlg, df], axis=1)
    dh1 = matmul(dqkv, W["qkv"], trans_b=True, name=f"{tag}_d_h1a")
    dh1 = matmul(daux, W["aux"], trans_b=True, residual=dh1, name=f"{tag}_d_h1b")
    gW["qkv"] = mm(s["h1"], dqkv, name=f"{tag}_g_qkv")
    gW["aux"] = mm(s["h1"], daux, name=f"{tag}_g_aux")
    dx, gP["norm_mix_g"] = rmsnorm_bwd(s["x"], P["norm_mix_g"], dh1, dx1, name=f"{tag}_norm_mix_bwd")
    return dx, gW, gP, ds_band


def local_step(x, mem, target, weights_of, Ps, rel_bias, final_norm_g, grads_done=None):
    B = x.shape[0]
    x2d = x.reshape(B * SEQ, D_MODEL)
    mem2d = mem.reshape(B * N_MEM, D_MODEL)
    bias = relbias_expand(rel_bias, name="relbias_expand")
    saved, Ws = [], []
    h = x2d
    for l in range(DEPTH):
        Ws.append(weights_of(l, h))
        h, s = layer_fwd(h, mem2d, Ws[l], Ps[l], bias, f"l{l}")
        saved.append(s)
    loss, dh, d_final = loss_head(h, final_norm_g, target.reshape(B * SEQ, D_MODEL), name="loss_head")
    gWs, gPs, ds_bands = [None] * DEPTH, [None] * DEPTH, []
    for l in range(DEPTH - 1, -1, -1):
        dh, gWs[l], gPs[l], ds = layer_bwd(dh, mem2d, Ws[l], Ps[l], bias, saved[l], f"l{l}")
        if grads_done is not None:
            dh = grads_done(l, gWs[l], dh)
        ds_bands.append(ds)
    d_rel = relbias_reduce(sum_cast([d.reshape(-1, BAND) for d in ds_bands], F32, name="ds_band_sum").reshape(-1, BLOCK, BAND),
                           name="relbias_reduce")
    return loss, dh.reshape(B, SEQ, D_MODEL), gWs, gPs, d_rel, d_final


def small_params(p, l):
    row = lambda name: p[name][l].reshape(1, -1)
    ffn_w, ffn_b = p["ffn_conv_w"][l], row("ffn_conv_b")
    return dict(
        norm_mix_g=row("norm_mix_g"), norm_cross_g=row("norm_cross_g"), norm_mem_g=row("norm_mem_g"), norm_ffn_g=row("norm_ffn_g"),
        bf=jnp.pad(row("b_forget"), ((0, 0), (0, LANES - N_HEADS))),
        lru_conv_w=p["lru_conv_w"][l], lru_conv_b=row("lru_conv_b"), wa=_block_diag_halves(p["lru_w_a"][l]), lru_b_a=row("lru_b_a"),
        wx=_block_diag_halves(p["lru_w_x"][l]), lru_b_x=row("lru_b_x"), lru_lambda=row("lru_lambda"),
        wu=ffn_w[:, :D_FF], wg=ffn_w[:, D_FF:], bu=ffn_b[:, :D_FF], bg=ffn_b[:, D_FF:])


def canonical_weights(w_in, w_out, w_cq, w_ck, w_cv, w_co, w_up, w_down):
    sb_fox, fox_f, rest = w_in[:, :6 * GROUP_W], w_in[:, 6 * GROUP_W:6 * GROUP_W + N_HEADS], w_in[:, 6 * GROUP_W + N_HEADS:]
    dil, lru = rest[:, :3 * GROUP_W], rest[:, 3 * GROUP_W:]
    pad = jnp.zeros((w_in.shape[0], AUX_W - 2 * GROUP_W - N_HEADS), w_in.dtype)
    return dict(qkv=jnp.concatenate([sb_fox, dil], axis=1), aux=jnp.concatenate([lru, fox_f, pad], axis=1), out=w_out,
                cq=w_cq, ckv=jnp.concatenate([w_ck, w_cv], axis=1), coT=w_co.T, upT=w_up.T, down=w_down)


def native_grads(g):
    qkv, aux = g["qkv"], g["aux"]
    a, b = 6 * GROUP_W, 6 * GROUP_W + N_HEADS
    w_in = jnp.zeros((qkv.shape[0], b + 5 * GROUP_W), qkv.dtype)
    w_in = w_in.at[:, :a].set(qkv[:, :a]).at[:, a:b].set(aux[:, 2 * GROUP_W:2 * GROUP_W + N_HEADS])
    w_in = w_in.at[:, b:b + 3 * GROUP_W].set(qkv[:, a:]).at[:, b + 3 * GROUP_W:].set(aux[:, :2 * GROUP_W])
    return (w_in, g["out"], g["cq"], g["ckv"][:, :GROUP_W], g["ckv"][:, GROUP_W:], g["coT"].T, g["upT"].T, g["down"])


ANY = pl.BlockSpec(memory_space=pl.ANY)
VMEM_SPEC = pl.BlockSpec(memory_space=pltpu.VMEM)


def _place():
    x, y, c = lax.axis_index("x"), lax.axis_index("y"), lax.axis_index("c")
    other_chips = [(1 - x, y), (x, 1 - y), (1 - x, 1 - y)]
    return x, y, c, other_chips


def _gather_body(x_ref, out_ref, send_sems, recv_sems, local_sem):
    x, y, c, chips = _place()
    me, sibling = (x, y, c), (x, y, 1 - c)

    def slot(px, py, pc):
        return out_ref.at[4 * px + 2 * py + pc]

    def copy(k, block, to, src=None):
        return pltpu.make_async_remote_copy(
            src_ref=slot(*block) if src is None else src, dst_ref=slot(*block),
            send_sem=send_sems.at[k], recv_sem=recv_sems.at[k], device_id=to, device_id_type=MESH)

    if local_sem is not None:
        mine = pltpu.make_async_copy(x_ref, slot(*me), local_sem)
        mine.start()
    first = [copy(0, me, sibling, src=x_ref)]
    first += [copy(1 + j, me, (*chip, c), src=x_ref) for j, chip in enumerate(chips)]
    for cp in first:
        cp.start()
    passed = [copy(4 + j, (*chip, c), sibling) for j, chip in enumerate(chips)]
    for j, chip in enumerate(chips):
        copy(1 + j, (*chip, c), me).wait_recv()
        passed[j].start()
    copy(0, sibling, me).wait_recv()
    for j, chip in enumerate(chips):
        copy(4 + j, (*chip, 1 - c), me).wait_recv()
    for cp in first + passed:
        cp.wait_send()
    if local_sem is not None:
        mine.wait()


_GATHER_SEMS = [pltpu.SemaphoreType.DMA((7,)), pltpu.SemaphoreType.DMA((7,)), pltpu.SemaphoreType.DMA]


def allgather_hbm(shard, me, *, name):
    def body(x_ref, out_ref, send_sems, recv_sems):
        _gather_body(x_ref, out_ref, send_sems, recv_sems, None)

    others = pl.pallas_call(
        body, name=name, in_specs=[ANY], out_specs=ANY,
        out_shape=jax.ShapeDtypeStruct((N_DEV,) + shard.shape, shard.dtype), scratch_shapes=_GATHER_SEMS[:2],
    )(shard)
    return lax.dynamic_update_slice(others, shard[None], (me, 0, 0))


def allgather_small(x, *, name, reduce=False):
    def body(x_ref, out_ref, *rest):
        _gather_body(x_ref, out_ref, *rest[-3:])
        if reduce:
            acc = out_ref[0]
            for d in range(1, N_DEV):
                acc = acc + out_ref[d]
            rest[0][...] = acc

    sd = jax.ShapeDtypeStruct
    return pl.pallas_call(
        body, name=name, in_specs=[VMEM_SPEC], out_specs=[VMEM_SPEC, VMEM_SPEC] if reduce else VMEM_SPEC,
        out_shape=[sd((N_DEV,) + x.shape, x.dtype), sd(x.shape, x.dtype)] if reduce else sd((N_DEV,) + x.shape, x.dtype),
        scratch_shapes=_GATHER_SEMS, compiler_params=pltpu.CompilerParams(vmem_limit_bytes=VMEM_LIMIT_V7X),
    )(x)


N_CHIPS = 4


def pair_exchange(g, *, name):
    _, R, C = g.shape

    def body(g_ref, recv_ref, send_sems, recv_sems):
        x, y, c, _ = _place()
        sibling = (x, y, 1 - c)
        remote = [pltpu.make_async_remote_copy(
            src_ref=g_ref.at[2 * q + (1 - c)], dst_ref=recv_ref.at[q], send_sem=send_sems.at[q], recv_sem=recv_sems.at[q],
            device_id=sibling, device_id_type=MESH) for q in range(N_CHIPS)]
        for cp in remote:
            cp.start()
        for cp in remote:
            cp.wait_recv()
        for cp in remote:
            cp.wait_send()

    return pl.pallas_call(
        body, name=name, in_specs=[ANY], out_specs=ANY, out_shape=jax.ShapeDtypeStruct((N_CHIPS, R, C), g.dtype),
        scratch_shapes=[pltpu.SemaphoreType.DMA((N_CHIPS,))] * 2,
    )(g)


def chip_exchange(s, *, name):
    _, R, C = s.shape

    def body(s_ref, o0, o1, o2, send_sems, recv_sems):
        x, y, c, chips = _place()
        outs = (o0, o1, o2)
        copies = [pltpu.make_async_remote_copy(
            src_ref=s_ref.at[2 * cx + cy], dst_ref=outs[j], send_sem=send_sems.at[j], recv_sem=recv_sems.at[j],
            device_id=(cx, cy, c), device_id_type=MESH) for j, (cx, cy) in enumerate(chips)]
        for cp in copies:
            cp.start()
        for cp in copies:
            cp.wait_recv()
        for cp in copies:
            cp.wait_send()

    sd = jax.ShapeDtypeStruct((R, C), s.dtype)
    return pl.pallas_call(
        body, name=name, in_specs=[ANY], out_specs=[ANY] * 3, out_shape=[sd] * 3,
        scratch_shapes=[pltpu.SemaphoreType.DMA((3,)), pltpu.SemaphoreType.DMA((3,))],
    )(s)


HBM_SPEC = pl.BlockSpec(memory_space=pltpu.HBM)
SEM_SPEC = pl.BlockSpec(memory_space=pltpu.SEMAPHORE)
N_PEERS = N_DEV - 1


def _peers():
    x, y, c = lax.axis_index("x"), lax.axis_index("y"), lax.axis_index("c")
    flip = lambda v, bit: 1 - v if bit else v
    out = []
    for k in range(1, N_DEV):
        px, py, pc = flip(x, (k >> 2) & 1), flip(y, (k >> 1) & 1), flip(c, k & 1)
        out.append(((px, py, pc), 4 * px + 2 * py + pc))
    return out, 4 * x + 2 * y + c


def _peer_copies(src_ref, land_ref, send_sems, recv_sems, scatter, landing):
    peers, me = _peers()
    return [pltpu.make_async_remote_copy(
        src_ref=src_ref.at[idx] if scatter else src_ref, dst_ref=land_ref.at[me if landing == "mine" else idx],
        send_sem=send_sems.at[k], recv_sem=recv_sems.at[k], device_id=peer, device_id_type=MESH)
        for k, (peer, idx) in enumerate(peers)]


def exchange_start(src, scatter, *, name):
    shape = (N_DEV,) + src.shape[-2:]

    def body(src_ref, land_ref, send_sems, recv_sems, src_thru, land_thru, token):
        for cp in _peer_copies(src_ref, land_ref, send_sems, recv_sems, scatter, "mine"):
            cp.start()
        token[...] = jnp.zeros_like(token)

    sems = pltpu.SemaphoreType.DMA((N_PEERS,))
    return pl.pallas_call(
        body, name=name,
        out_shape=(sems, sems, pltpu.HBM(src.shape, src.dtype), pltpu.HBM(shape, src.dtype), jax.ShapeDtypeStruct((8, LANES), F32)),
        in_specs=(HBM_SPEC, HBM_SPEC), out_specs=(SEM_SPEC, SEM_SPEC, HBM_SPEC, HBM_SPEC, VMEM_SPEC),
        input_output_aliases={0: 2, 1: 3},
        compiler_params=pltpu.CompilerParams(has_side_effects=pltpu.SideEffectType.DATAFLOW_SIDE_EFFECTING),
    )(pltpu.with_memory_space_constraint(src, pltpu.HBM), pltpu.with_memory_space_constraint(lax.empty(shape, src.dtype), pltpu.HBM))


def exchange_wait(started, after, scatter, *, name):
    send_sems, recv_sems, src_thru, land_thru, _ = started

    def body(src_ref, land_ref, send_sems, recv_sems, after_ref, src_dead, got_ref):
        for cp in _peer_copies(src_ref, land_ref, send_sems, recv_sems, scatter, "theirs"):
            cp.wait_send()
            cp.wait_recv()

    return pl.pallas_call(
        body, name=name, out_shape=(pltpu.HBM(src_thru.shape, src_thru.dtype), pltpu.HBM(land_thru.shape, land_thru.dtype)),
        in_specs=(HBM_SPEC, HBM_SPEC, SEM_SPEC, SEM_SPEC, ANY), out_specs=(HBM_SPEC, HBM_SPEC), input_output_aliases={0: 0, 1: 1},
        compiler_params=pltpu.CompilerParams(has_side_effects=pltpu.SideEffectType.DATAFLOW_SIDE_EFFECTING),
    )(src_thru, land_thru, send_sems, recv_sems, after)[1]


def sum_blocks(blocks, *, name):
    n, R, C = blocks.shape
    tr = _largest_tile(R, 512, 16)

    def body(b_ref, o_ref):
        d = pl.program_id(1)
        v = b_ref[...].astype(F32)

        @pl.when(d == 0)
        def _():
            o_ref[...] = v

        @pl.when(d > 0)
        def _():
            o_ref[...] += v

    return pl.pallas_call(
        body, name=name, grid=(R // tr, n),
        in_specs=[pl.BlockSpec((None, tr, C), lambda i, d: (d, i, 0))], out_specs=pl.BlockSpec((tr, C), lambda i, d: (i, 0)),
        out_shape=jax.ShapeDtypeStruct((R, C), F32), compiler_params=_params("parallel", "arbitrary"),
    )(blocks)


WEIGHTS = ("norm_mix_g", "w_in", "b_forget", "lru_conv_w", "lru_conv_b", "lru_w_a", "lru_b_a", "lru_w_x", "lru_b_x", "lru_lambda",
           "w_out", "norm_cross_g", "norm_mem_g", "w_cq", "w_ck", "w_cv", "w_co", "norm_ffn_g", "w_up", "ffn_conv_w", "ffn_conv_b",
           "w_down", "rel_bias", "final_norm_g")
LARGE = ("w_in", "w_out", "w_cq", "w_ck", "w_cv", "w_co", "w_up", "w_down")
COLUMN_SPLIT_SMALL = ("lru_conv_w", "ffn_conv_w")
PACK = (("qkv", 128, 2304), ("aux", 128, 640), ("out", 128, 1024), ("cq", 128, 256), ("ckv", 128, 512), ("coT", 128, 256),
        ("upT", 704, 1024), ("down", 352, 1024))
PACK_W = 1024


def _pack_rows(parts):
    return jnp.concatenate([p.reshape(-1, PACK_W) for p in parts], axis=0)


def _pad_rows(flat, mult=8 * LANES):
    n = flat.shape[0]
    return jnp.pad(flat, (0, (-n) % mult)).reshape(-1, LANES)


def kernel(x, mem, norm_mix_g, w_in, b_forget, lru_conv_w, lru_conv_b, lru_w_a, lru_b_a, lru_w_x, lru_b_x, lru_lambda, w_out, norm_cross_g, norm_mem_g, w_cq, w_ck, w_cv, w_co, norm_ffn_g, w_up, ffn_conv_w, ffn_conv_b, w_down, rel_bias, final_norm_g, loss_target, m_norm_mix_g, m_w_in, m_b_forget, m_lru_conv_w, m_lru_conv_b, m_lru_w_a, m_lru_b_a, m_lru_w_x, m_lru_b_x, m_lru_lambda, m_w_out, m_norm_cross_g, m_norm_mem_g, m_w_cq, m_w_ck, m_w_cv, m_w_co, m_norm_ffn_g, m_w_up, m_ffn_conv_w, m_ffn_conv_b, m_w_down, m_rel_bias, m_final_norm_g, v_norm_mix_g, v_w_in, v_b_forget, v_lru_conv_w, v_lru_conv_b, v_lru_w_a, v_lru_b_a, v_lru_w_x, v_lru_b_x, v_lru_lambda, v_w_out, v_norm_cross_g, v_norm_mem_g, v_w_cq, v_w_ck, v_w_cv, v_w_co, v_norm_ffn_g, v_w_up, v_ffn_conv_w, v_ffn_conv_b, v_w_down, v_rel_bias, v_final_norm_g):
    w = dict(norm_mix_g=norm_mix_g, w_in=w_in, b_forget=b_forget, lru_conv_w=lru_conv_w, lru_conv_b=lru_conv_b, lru_w_a=lru_w_a,
             lru_b_a=lru_b_a, lru_w_x=lru_w_x, lru_b_x=lru_b_x, lru_lambda=lru_lambda, w_out=w_out, norm_cross_g=norm_cross_g,
             norm_mem_g=norm_mem_g, w_cq=w_cq, w_ck=w_ck, w_cv=w_cv, w_co=w_co, norm_ffn_g=norm_ffn_g, w_up=w_up,
             ffn_conv_w=ffn_conv_w, ffn_conv_b=ffn_conv_b, w_down=w_down, rel_bias=rel_bias, final_norm_g=final_norm_g)
    m = dict(norm_mix_g=m_norm_mix_g, w_in=m_w_in, b_forget=m_b_forget, lru_conv_w=m_lru_conv_w, lru_conv_b=m_lru_conv_b,
             lru_w_a=m_lru_w_a, lru_b_a=m_lru_b_a, lru_w_x=m_lru_w_x, lru_b_x=m_lru_b_x, lru_lambda=m_lru_lambda, w_out=m_w_out,
             norm_cross_g=m_norm_cross_g, norm_mem_g=m_norm_mem_g, w_cq=m_w_cq, w_ck=m_w_ck, w_cv=m_w_cv, w_co=m_w_co,
             norm_ffn_g=m_norm_ffn_g, w_up=m_w_up, ffn_conv_w=m_ffn_conv_w, ffn_conv_b=m_ffn_conv_b, w_down=m_w_down,
             rel_bias=m_rel_bias, final_norm_g=m_final_norm_g)
    v = dict(norm_mix_g=v_norm_mix_g, w_in=v_w_in, b_forget=v_b_forget, lru_conv_w=v_lru_conv_w, lru_conv_b=v_lru_conv_b,
             lru_w_a=v_lru_w_a, lru_b_a=v_lru_b_a, lru_w_x=v_lru_w_x, lru_b_x=v_lru_b_x, lru_lambda=v_lru_lambda, w_out=v_w_out,
             norm_cross_g=v_norm_cross_g, norm_mem_g=v_norm_mem_g, w_cq=v_w_cq, w_ck=v_w_ck, w_cv=v_w_cv, w_co=v_w_co,
             norm_ffn_g=v_norm_ffn_g, w_up=v_w_up, ffn_conv_w=v_ffn_conv_w, ffn_conv_b=v_ffn_conv_b, w_down=v_w_down,
             rel_bias=v_rel_bias, final_norm_g=v_final_norm_g)
    me = 4 * lax.axis_index("x") + 2 * lax.axis_index("y") + lax.axis_index("c")

    conv_shard = jnp.concatenate([w[n].reshape(-1) for n in COLUMN_SPLIT_SMALL])
    conv_all = allgather_small(_pad_rows(conv_shard), name="gather_conv").reshape(N_DEV, -1)
    full = dict(w)
    off = 0
    for n in COLUMN_SPLIT_SMALL:
        d, k, c = w[n].shape
        blocks = conv_all[:, off:off + d * k * c].reshape(N_DEV, d, k, c)
        full[n] = blocks.transpose(1, 2, 0, 3).reshape(d, k, N_DEV * c)
        off += d * k * c

    def packed_shard(l):
        canon = canonical_weights(*[w[n][l] for n in LARGE])
        return _pack_rows([canon[k].astype(BF16) for k, _, _ in PACK])

    def unpack_weights(packed):
        W, row = {}, 0
        for k, r, c in PACK:
            n_rows = r * c // PACK_W
            W[k] = packed[:, row:row + n_rows].reshape(N_DEV * r, c)
            row += n_rows
        upT = W.pop("upT")
        W["up_u"], W["up_g"] = upT[:D_FF], upT[D_FF:]
        return W

    def own_block_in(landed, block):
        return lax.dynamic_update_slice(landed, block[None], (me, 0, 0))

    last = DEPTH - 1
    packed0 = allgather_hbm(packed_shard(0), me, name="gather_weights")
    _, shard_last = lax.optimization_barrier((packed0, packed_shard(last)))
    gather_last = exchange_start(shard_last, False, name="gather_last_start")
    x, _ = lax.optimization_barrier((x, gather_last[4]))

    def weights_of(l, h):
        if l == 0:
            return unpack_weights(packed0)
        assert l == last
        return unpack_weights(own_block_in(exchange_wait(gather_last, h, False, name="gather_last_wait"), shard_last))

    def packed_grads(gW):
        g = dict(gW)
        g["upT"] = jnp.concatenate([g.pop("up_u"), g.pop("up_g")], axis=0)
        return jnp.concatenate([g[k].reshape(N_DEV, r * c // PACK_W, PACK_W) for k, r, c in PACK], axis=1)

    in_flight = {}

    def grads_done(l, gW, dh):
        if l != last:
            return dh
        g_all = packed_grads(gW)
        in_flight["src"] = g_all
        in_flight["copies"] = exchange_start(g_all, True, name="grads_last_start")
        return lax.optimization_barrier((dh, in_flight["copies"][4]))[0]

    Ps = [small_params(full, l) for l in range(DEPTH)]
    loss, grad_x, gWs, gPs, d_rel, d_final = local_step(x, mem, loss_target, weights_of, Ps, rel_bias,
                                                        final_norm_g.reshape(1, -1), grads_done)

    landed = exchange_wait(in_flight["copies"], grad_x, True, name="grads_last_wait")
    mine_last = lax.dynamic_index_in_dim(in_flight["src"], me, axis=0, keepdims=False)
    shard_sums = {last: sum_blocks(own_block_in(landed, mine_last), name="grads_last_sum")}

    g_all = packed_grads(gWs[0])
    rows = g_all.shape[1]
    got = pair_exchange(g_all, name="grads_pair_exchange")
    own = lax.dynamic_index_in_dim(g_all.reshape(N_CHIPS, 2, rows, PACK_W), lax.axis_index("c"), axis=1, keepdims=False)
    pair = sum_cast([own.reshape(-1, PACK_W), got.reshape(-1, PACK_W)], GRAD_WIRE, name="grads_pair_sum").reshape(N_CHIPS, rows, PACK_W)
    from_x, from_y, from_xy = chip_exchange(pair, name="grads_chip_exchange")
    mine = lax.dynamic_index_in_dim(pair, 2 * lax.axis_index("x") + lax.axis_index("y"), axis=0, keepdims=False)
    shard_sums[0] = sum_cast([mine, from_x, from_y, from_xy], F32, name="grads_chip_sum")

    grads = {}
    per_layer = []
    for l in range(DEPTH):
        g, row = {}, 0
        for k, r, c in PACK:
            n_rows = r * c // PACK_W
            g[k] = shard_sums[l][row:row + n_rows].reshape(r, c)
            row += n_rows
        per_layer.append(native_grads(g))
    for i, n in enumerate(LARGE):
        grads[n] = jnp.stack([per_layer[l][i] for l in range(DEPTH)])

    small_names = [n for n in WEIGHTS if n not in LARGE and n not in ("rel_bias", "final_norm_g")]
    pieces = [gPs[l][n].reshape(-1) for n in small_names for l in range(DEPTH)] + [d_rel.reshape(-1), d_final.reshape(-1), loss[0, :1]]
    sizes = [p.shape[0] for p in pieces]
    _, total = allgather_small(_pad_rows(jnp.concatenate(pieces)), name="allreduce_small", reduce=True)
    total = total.reshape(-1)
    off, it = 0, iter(sizes)
    for n in small_names:
        per = []
        for l in range(DEPTH):
            sz = next(it)
            per.append(total[off:off + sz])
            off += sz
        full_shape = (DEPTH,) + full[n].shape[1:]
        gfull = jnp.stack(per).reshape(full_shape)
        if n in COLUMN_SPLIT_SMALL:
            c = w[n].shape[-1]
            gfull = lax.dynamic_slice_in_dim(gfull, me * c, c, axis=gfull.ndim - 1)
        grads[n] = gfull
    grads["rel_bias"] = total[off:off + rel_bias.size].reshape(rel_bias.shape)
    off += rel_bias.size
    grads["final_norm_g"] = total[off:off + D_MODEL]
    off += D_MODEL
    loss_out = total[off]

    delta, new_m, new_v = {}, {}, {}
    for n in LARGE:
        shape = w[n].shape
        two_d = lambda a: a.reshape(-1, shape[-1])
        d_, m_, v_ = adamw(two_d(w[n]), two_d(grads[n]), two_d(m[n]), two_d(v[n]), name=f"adamw_{n}")
        delta[n], new_m[n], new_v[n] = d_.reshape(shape), m_.reshape(shape), v_.reshape(shape)
    small_all = [n for n in WEIGHTS if n not in LARGE]
    two_d = lambda a: a.reshape(-1, a.shape[-1])
    d_, m_, v_ = adamw_many(*[[two_d(src[n]) for n in small_all] for src in (w, grads, m, v)], name="adamw_small")
    for i, n in enumerate(small_all):
        delta[n], new_m[n], new_v[n] = (a[i].reshape(w[n].shape) for a in (d_, m_, v_))

    return (loss_out, grad_x, *[grads[n] for n in WEIGHTS], *[delta[n] for n in WEIGHTS], *[new_m[n] for n in WEIGHTS],
            *[new_v[n] for n in WEIGHTS])
```

```python
import functools
import math

import numpy as np
import jax
import jax.numpy as jnp
from jax import lax
from jax.experimental import pallas as pl
from jax.experimental.pallas import tpu as pltpu

F32 = jnp.float32
BF16 = jnp.bfloat16
MESH = pl.DeviceIdType.MESH

N_DEV = 8
D_MODEL = 1024
SEQ = 2048
DEPTH = 2
HEAD_DIM = 64
N_HEADS = 4
GROUP_W = N_HEADS * HEAD_DIM
D_FF = 2816
N_MEM = 256
NUM_BUCKETS = 32
MAX_DISTANCE = 2048
BLOCK = 128
DILATIONS = (1, 4, 16)
EPS = 1e-6
LRU_C = 8.0
Q_SCALE = HEAD_DIM ** -0.5
AUX_W = 640
LRU_HALF_W = 128
LRU_HALVES = GROUP_W // LRU_HALF_W
ADAM_LR, ADAM_B1, ADAM_B2, ADAM_EPS, ADAM_WD, ADAM_STEP = 0.001, 0.9, 0.999, 1e-08, 0.01, 10

VMEM_LIMIT_V7X = 48 * 1024 * 1024


def _params(*sem):
    return pltpu.CompilerParams(dimension_semantics=sem if sem else None, vmem_limit_bytes=VMEM_LIMIT_V7X)


def _pick(n, cands):
    for c in cands:
        if n % c == 0:
            return c
    return n


def _largest_tile(n, cap, align):
    best = None
    for t in range(align, min(n, cap) + 1, align):
        if n % t == 0:
            best = t
    return n if best is None else best


def matmul(a, b, *, name, trans_a=False, trans_b=False, out_dtype=F32, residual=None):
    (K, M) = a.shape if trans_a else a.shape[::-1]
    (N, Kb) = b.shape if trans_b else b.shape[::-1]
    assert K == Kb, (a.shape, b.shape)
    tm = _largest_tile(M, 1024 if trans_a else 512, 128)
    tn = _largest_tile(N, 1408, 128)
    tk = _largest_tile(K, 2816, 128)
    nk = K // tk
    a_spec = pl.BlockSpec((tk, tm), lambda i, j, k: (k, i)) if trans_a else pl.BlockSpec((tm, tk), lambda i, j, k: (i, k))
    b_spec = pl.BlockSpec((tn, tk), lambda i, j, k: (j, k)) if trans_b else pl.BlockSpec((tk, tn), lambda i, j, k: (k, j))
    o_spec = pl.BlockSpec((tm, tn), lambda i, j, k: (i, j))
    dims = (((0 if trans_a else 1,), (1 if trans_b else 0,)), ((), ()))
    has_res = residual is not None

    def body(*refs):
        a_ref, b_ref = refs[0], refs[1]
        r_ref = refs[2] if has_res else None
        part = lax.dot_general(a_ref[...].astype(BF16), b_ref[...].astype(BF16), dims, preferred_element_type=F32)
        if nk == 1:
            if has_res:
                part = part + r_ref[...].astype(F32)
            refs[-1][...] = part.astype(out_dtype)
            return
        o_ref, acc_ref = refs[-2], refs[-1]
        k = pl.program_id(2)

        @pl.when(k == 0)
        def _():
            acc_ref[...] = part

        @pl.when(k > 0)
        def _():
            acc_ref[...] += part

        @pl.when(k == nk - 1)
        def _():
            r = acc_ref[...]
            if has_res:
                r = r + r_ref[...].astype(F32)
            o_ref[...] = r.astype(out_dtype)

    ops = (a, b) + ((residual,) if has_res else ())
    return pl.pallas_call(
        body, name=name, grid=(M // tm, N // tn, nk),
        in_specs=[a_spec, b_spec] + ([o_spec] if has_res else []),
        out_specs=o_spec, out_shape=jax.ShapeDtypeStruct((M, N), out_dtype),
        scratch_shapes=[pltpu.VMEM((tm, tn), F32)] if nk > 1 else [],
        compiler_params=_params("parallel", "parallel", "arbitrary"),
    )(*ops)


def rmsnorm_fwd(x, g, *, name):
    R, D = x.shape
    tr = _pick(R, (512, 256))

    def body(x_ref, g_ref, o_ref):
        xv = x_ref[...]
        r = lax.rsqrt(jnp.mean(xv * xv, axis=-1, keepdims=True) + EPS)
        o_ref[...] = (xv * r * g_ref[...]).astype(BF16)

    return pl.pallas_call(
        body, name=name, grid=(R // tr,),
        in_specs=[pl.BlockSpec((tr, D), lambda i: (i, 0)), pl.BlockSpec((1, D), lambda i: (0, 0))],
        out_specs=pl.BlockSpec((tr, D), lambda i: (i, 0)), out_shape=jax.ShapeDtypeStruct((R, D), BF16),
        compiler_params=_params("parallel"),
    )(x, g)


def rmsnorm_bwd(x, g, dh, dres, *, name):
    R, D = x.shape
    tr = _pick(R, (512, 256))
    has_res = dres is not None

    def body(*refs):
        x_ref, g_ref, dh_ref = refs[:3]
        dx_ref, dg_ref = refs[-2], refs[-1]
        xv = x_ref[...]
        r = lax.rsqrt(jnp.mean(xv * xv, axis=-1, keepdims=True) + EPS)
        n = xv * r
        dhv = dh_ref[...]
        dn = dhv * g_ref[...]
        dx = r * (dn - n * jnp.mean(dn * n, axis=-1, keepdims=True))
        if has_res:
            dx = dx + refs[3][...]
        dx_ref[...] = dx
        part = jnp.sum(dhv * n, axis=0, keepdims=True)

        @pl.when(pl.program_id(0) == 0)
        def _():
            dg_ref[...] = part

        @pl.when(pl.program_id(0) > 0)
        def _():
            dg_ref[...] += part

    row = pl.BlockSpec((tr, D), lambda i: (i, 0))
    vec = pl.BlockSpec((1, D), lambda i: (0, 0))
    ops = (x, g, dh) + ((dres,) if has_res else ())
    return pl.pallas_call(
        body, name=name, grid=(R // tr,),
        in_specs=[row, vec, row] + ([row] if has_res else []),
        out_specs=[row, vec],
        out_shape=[jax.ShapeDtypeStruct((R, D), F32), jax.ShapeDtypeStruct((1, D), F32)],
        compiler_params=_params("arbitrary"),
    )(*ops)


_SQRT_HALF = 0.7071067811865476
_INV_SQRT_2PI = 0.3989422804014327


def _erf(x):
    ax = jnp.abs(x)
    t = 1.0 / (1.0 + 0.3275911 * ax)
    poly = t * (0.254829592 + t * (-0.284496736 + t * (1.421413741 + t * (-1.453152027 + t * 1.061405429))))
    y = 1.0 - poly * jnp.exp(-ax * ax)
    return jnp.where(x < 0, -y, y)


def _gelu_cdf(x):
    return 0.5 * (1.0 + _erf(x * _SQRT_HALF))


def _gelu_and_grad(x):
    cdf = _gelu_cdf(x)
    return x * cdf, cdf + x * _INV_SQRT_2PI * jnp.exp(-0.5 * x * x)


def _shift_down(main, halo, first, shifts):
    halo = jnp.where(first, 0.0, halo)
    ext = jnp.concatenate([halo, main], axis=0)
    return [pltpu.roll(ext, s, 0)[8:] for s in shifts]


def _conv3(main, halo, first, w, b):
    m1, m2 = _shift_down(main, halo, first, (1, 2))
    return ((b + w[0:1] * m2) + w[1:2] * m1) + w[2:3] * main, m1, m2


def glu_fwd(hu, hg, wu, wg, bu, bg, *, name):
    T, F = hu.shape
    tm, tf = 512, _largest_tile(F, 704, 128)
    hb = tm // 8
    blocks_per_example = SEQ // tm

    def body(hu_ref, hg_ref, hau_ref, hag_ref, wu_ref, wg_ref, bu_ref, bg_ref, o_ref):
        first = pl.program_id(0) % blocks_per_example == 0
        up, _, _ = _conv3(hu_ref[...], hau_ref[...], first, wu_ref[...], bu_ref[...])
        gate, _, _ = _conv3(hg_ref[...], hag_ref[...], first, wg_ref[...], bg_ref[...])
        o_ref[...] = (gate * _gelu_cdf(gate) * up).astype(BF16)

    main = pl.BlockSpec((tm, tf), lambda i, j: (i, j))
    halo = pl.BlockSpec((8, tf), lambda i, j: (jnp.maximum(i * hb - 1, 0), j))
    w3 = pl.BlockSpec((3, tf), lambda i, j: (0, j))
    b1 = pl.BlockSpec((1, tf), lambda i, j: (0, j))
    return pl.pallas_call(
        body, name=name, grid=(T // tm, F // tf),
        in_specs=[main, main, halo, halo, w3, w3, b1, b1],
        out_specs=main, out_shape=jax.ShapeDtypeStruct((T, F), BF16),
        compiler_params=_params("parallel", "parallel"),
    )(hu, hg, hu, hg, wu, wg, bu, bg)


def glu_bwd(hu, hg, dact, wu, wg, bu, bg, *, name):
    T, F = hu.shape
    tm, tf = 512, _largest_tile(F, 704, 128)
    hb = tm // 8
    blocks_per_example = SEQ // tm
    n_halo_blocks = T // 8
    n_ext = tm + 8

    def body(hu_ref, hg_ref, hau_ref, hag_ref, hnu_ref, hng_ref, da_ref, dan_ref, wu_ref, wg_ref, bu_ref, bg_ref,
             du_ref, dg_ref, dwu_ref, dwg_ref, dbu_ref, dbg_ref):
        i = pl.program_id(1)
        first = i % blocks_per_example == 0
        last = i % blocks_per_example == blocks_per_example - 1
        wu, wg = wu_ref[...], wg_ref[...]

        def conv_ext(main_ref, prev_ref, next_ref, w, b):
            ext = jnp.concatenate([jnp.where(first, 0.0, prev_ref[...]), main_ref[...], next_ref[...]], axis=0)
            x0, x1, x2 = ext[8:], pltpu.roll(ext, 1, 0)[8:], pltpu.roll(ext, 2, 0)[8:]
            return ((b + w[0:1] * x2) + w[1:2] * x1) + w[2:3] * x0, x0, x1, x2

        up, xu, u1, u2 = conv_ext(hu_ref, hau_ref, hnu_ref, wu, bu_ref[...])
        gate, xg, g1, g2 = conv_ext(hg_ref, hag_ref, hng_ref, wg, bg_ref[...])
        act, dact_dgate = _gelu_and_grad(gate)
        da = jnp.concatenate([da_ref[...], jnp.where(last, 0.0, dan_ref[...])], axis=0)
        dup = da * act
        dgate = da * up * dact_dgate

        def conv_t(d, w):
            return (w[2:3] * d[:tm] + w[1:2] * pltpu.roll(d, n_ext - 1, 0)[:tm] + w[0:1] * pltpu.roll(d, n_ext - 2, 0)[:tm]).astype(BF16)

        du_ref[...] = conv_t(dup, wu)
        dg_ref[...] = conv_t(dgate, wg)

        def sums(d, x0, x1, x2):
            s = lambda v: jnp.sum(v[:tm], axis=0, keepdims=True)
            return jnp.concatenate([s(d * x2), s(d * x1), s(d * x0)], axis=0), s(d)

        pwu, pbu = sums(dup, xu, u1, u2)
        pwg, pbg = sums(dgate, xg, g1, g2)

        @pl.when(i == 0)
        def _():
            dwu_ref[...] = pwu
            dwg_ref[...] = pwg
            dbu_ref[...] = pbu
            dbg_ref[...] = pbg

        @pl.when(i > 0)
        def _():
            dwu_ref[...] += pwu
            dwg_ref[...] += pwg
            dbu_ref[...] += pbu
            dbg_ref[...] += pbg

    main = pl.BlockSpec((tm, tf), lambda j, i: (i, j))
    before = pl.BlockSpec((8, tf), lambda j, i: (jnp.maximum(i * hb - 1, 0), j))
    after = pl.BlockSpec((8, tf), lambda j, i: (jnp.minimum((i + 1) * hb, n_halo_blocks - 1), j))
    w3 = pl.BlockSpec((3, tf), lambda j, i: (0, j))
    b1 = pl.BlockSpec((1, tf), lambda j, i: (0, j))
    sd = jax.ShapeDtypeStruct
    return pl.pallas_call(
        body, name=name, grid=(F // tf, T // tm),
        in_specs=[main, main, before, before, after, after, main, after, w3, w3, b1, b1],
        out_specs=[main, main, w3, w3, b1, b1],
        out_shape=[sd((T, F), BF16), sd((T, F), BF16), sd((3, F), F32), sd((3, F), F32), sd((1, F), F32), sd((1, F), F32)],
        compiler_params=_params("parallel", "arbitrary"),
    )(hu, hg, hu, hg, hu, hg, dact, dact, wu, wg, bu, bg)


def loss_head(x, g, target, *, name):
    T, D = x.shape
    tr = 256

    def body(x_ref, g_ref, t_ref, loss_ref, dx_ref, dg_ref):
        xv = x_ref[...]
        gv = g_ref[...]
        r = lax.rsqrt(jnp.mean(xv * xv, axis=-1, keepdims=True) + EPS)
        n = xv * r
        err = n * gv - t_ref[...]
        part_loss = jnp.zeros((1, 128), F32) + 0.5 * jnp.sum(jnp.mean(err * err, axis=-1, keepdims=True))
        dy = err * (1.0 / D)
        dn = dy * gv
        dx_ref[...] = r * (dn - n * jnp.mean(dn * n, axis=-1, keepdims=True))
        part_g = jnp.sum(dy * n, axis=0, keepdims=True)

        @pl.when(pl.program_id(0) == 0)
        def _():
            loss_ref[...] = part_loss
            dg_ref[...] = part_g

        @pl.when(pl.program_id(0) > 0)
        def _():
            loss_ref[...] += part_loss
            dg_ref[...] += part_g

    row = pl.BlockSpec((tr, D), lambda i: (i, 0))
    vec = pl.BlockSpec((1, D), lambda i: (0, 0))
    sd = jax.ShapeDtypeStruct
    return pl.pallas_call(
        body, name=name, grid=(T // tr,),
        in_specs=[row, vec, row],
        out_specs=[pl.BlockSpec((1, 128), lambda i: (0, 0)), row, vec],
        out_shape=[sd((1, 128), F32), sd((T, D), F32), sd((1, D), F32)],
        compiler_params=_params("arbitrary"),
    )(x, g, target)


def adamw(w, g, m, v, *, name):
    R, C = w.shape
    tr = _pick(R, (256, 128, 64, 32, 16, 8))

    def body(w_ref, g_ref, m_ref, v_ref, d_ref, nm_ref, nv_ref):
        gv = g_ref[...]
        mn = ADAM_B1 * m_ref[...] + (1.0 - ADAM_B1) * gv
        vn = ADAM_B2 * v_ref[...] + (1.0 - ADAM_B2) * (gv * gv)
        m_hat = mn / (1.0 - ADAM_B1 ** ADAM_STEP)
        v_hat = vn / (1.0 - ADAM_B2 ** ADAM_STEP)
        d_ref[...] = -ADAM_LR * (m_hat / (jnp.sqrt(v_hat) + ADAM_EPS) + ADAM_WD * w_ref[...])
        nm_ref[...] = mn
        nv_ref[...] = vn

    blk = pl.BlockSpec((tr, C), lambda i: (i, 0))
    sd = jax.ShapeDtypeStruct((R, C), F32)
    return pl.pallas_call(
        body, name=name, grid=(R // tr,), in_specs=[blk] * 4, out_specs=[blk] * 3, out_shape=[sd] * 3,
        compiler_params=_params("parallel"),
    )(w, g, m, v)


def adamw_many(ws, gs, ms, vs, *, name):
    n = len(ws)

    def body(*refs):
        ins, outs = refs[:4 * n], refs[4 * n:]
        for i in range(n):
            w_ref, g_ref, m_ref, v_ref = ins[i], ins[n + i], ins[2 * n + i], ins[3 * n + i]
            gv = g_ref[...]
            mn = ADAM_B1 * m_ref[...] + (1.0 - ADAM_B1) * gv
            vn = ADAM_B2 * v_ref[...] + (1.0 - ADAM_B2) * (gv * gv)
            m_hat = mn / (1.0 - ADAM_B1 ** ADAM_STEP)
            v_hat = vn / (1.0 - ADAM_B2 ** ADAM_STEP)
            outs[i][...] = -ADAM_LR * (m_hat / (jnp.sqrt(v_hat) + ADAM_EPS) + ADAM_WD * w_ref[...])
            outs[n + i][...] = mn
            outs[2 * n + i][...] = vn

    vm = pl.BlockSpec(memory_space=pltpu.VMEM)
    shapes = [jax.ShapeDtypeStruct(w.shape, F32) for w in ws]
    res = pl.pallas_call(
        body, name=name, in_specs=[vm] * (4 * n), out_specs=[vm] * (3 * n), out_shape=shapes * 3, compiler_params=_params(),
    )(*ws, *gs, *ms, *vs)
    return res[:n], res[n:2 * n], res[2 * n:]


def _softplus(x):
    return jnp.maximum(x, 0.0) + jnp.log(1.0 + jnp.exp(-jnp.abs(x)))


def _lru_gates(x, cw, cb, wa, ba, wx, bx, lam):
    S = x.shape[0]
    row = lax.broadcasted_iota(jnp.int32, (S, 1), 0)

    def back(s):
        return jnp.where(row >= s, pltpu.roll(x, s, 0), 0.0)

    xc = (((cb + cw[0:1] * back(3)) + cw[1:2] * back(2)) + cw[2:3] * back(1)) + cw[3:4] * x
    xb = xc.astype(BF16)
    r = jax.nn.sigmoid(jnp.dot(xb, wa, preferred_element_type=F32) + ba)
    ig = jax.nn.sigmoid(jnp.dot(xb, wx, preferred_element_type=F32) + bx)
    sp = _softplus(-lam)
    la = -LRU_C * r * sp
    a = jnp.exp(la)
    y = 2.0 * la
    one_minus_a2 = jnp.where(y > -0.05, -y * (1.0 + y * (0.5 + y * (1.0 / 6.0 + y * (1.0 / 24.0)))), 1.0 - jnp.exp(y))
    mm = jnp.sqrt(one_minus_a2)
    return xc, xb, r, ig, sp, a, mm


def lru_fwd(aux, cw, cb, wa, ba, wx, bx, lam, *, name):
    T = aux.shape[0]
    S, C = SEQ, LRU_HALF_W

    def body(x_ref, g_ref, cw_ref, cb_ref, wa_ref, ba_ref, wx_ref, bx_ref, lam_ref, o_ref, h_ref, a_s, u_s):
        xc, _, r, ig, sp, a, mm = _lru_gates(x_ref[...], cw_ref[...], cb_ref[...], wa_ref[...], ba_ref[...],
                                             wx_ref[...], bx_ref[...], lam_ref[...])
        a_s[...] = a
        u_s[...] = mm * (ig * xc)

        def group(i, h):
            base = pl.multiple_of(i * 8, 8)
            a8 = a_s[pl.ds(base, 8), :]
            u8 = u_s[pl.ds(base, 8), :]
            for rr in range(8):
                h = a8[rr:rr + 1] * h + u8[rr:rr + 1]
                h_ref[pl.ds(base + rr, 1), :] = h
            return h

        lax.fori_loop(0, S // 8, group, jnp.zeros((1, C), F32))
        gate = g_ref[...]
        o_ref[...] = (h_ref[...] * (gate * _gelu_cdf(gate))).astype(BF16)

    blk = lambda col: pl.BlockSpec((S, C), lambda c, b: (b, col + c))
    par = lambda rows: pl.BlockSpec((rows, C), lambda c, b: (0, c))
    sq = pl.BlockSpec((None, C, C), lambda c, b: (c, 0, 0))
    sd = jax.ShapeDtypeStruct
    W = LRU_HALVES * C
    return pl.pallas_call(
        body, name=name, grid=(LRU_HALVES, T // S),
        in_specs=[blk(0), blk(LRU_HALVES), par(4), par(1), sq, par(1), sq, par(1), par(1)],
        out_specs=[blk(0), blk(0)], out_shape=[sd((T, W), BF16), sd((T, W), F32)],
        scratch_shapes=[pltpu.VMEM((S, C), F32), pltpu.VMEM((S, C), F32)],
        compiler_params=_params("parallel", "parallel"),
    )(aux, aux, cw, cb, wa, ba, wx, bx, lam)


def lru_bwd(aux, h, dmixed, cw, cb, wa, ba, wx, bx, lam, *, name):
    T = aux.shape[0]
    S, C = SEQ, LRU_HALF_W

    def body(x_ref, g_ref, h_ref, do_ref, cw_ref, cb_ref, wa_ref, ba_ref, wx_ref, bx_ref, lam_ref,
             dx_ref, dgate_ref, dcw_ref, dcb_ref, dwa_ref, dba_ref, dwx_ref, dbx_ref, dlam_ref, a_s, d_s):
        x = x_ref[...]
        cw = cw_ref[...]
        lam = lam_ref[...]
        xc, xb, r, ig, sp, a, mm = _lru_gates(x, cw, cb_ref[...], wa_ref[...], ba_ref[...], wx_ref[...], bx_ref[...], lam)
        gate = g_ref[...]
        gl, dgl = _gelu_and_grad(gate)
        dout = do_ref[...]
        hv = h_ref[...]
        dgate_ref[...] = dout * hv * dgl
        a_s[...] = a
        d_s[...] = dout * gl

        def group(i, c):
            base = pl.multiple_of((S // 8 - 1 - i) * 8, 8)
            a8 = a_s[pl.ds(base, 8), :]
            d8 = d_s[pl.ds(base, 8), :]
            for rr in range(7, -1, -1):
                d = d8[rr:rr + 1] + c
                d_s[pl.ds(base + rr, 1), :] = d
                c = a8[rr:rr + 1] * d
            return c

        lax.fori_loop(0, S // 8, group, jnp.zeros((1, C), F32))
        row = lax.broadcasted_iota(jnp.int32, (S, 1), 0)
        dht = d_s[...]
        h_prev = jnp.where(row >= 1, pltpu.roll(hv, 1, 0), 0.0)
        da = dht * h_prev
        gx = ig * xc
        dmm = dht * gx
        dig = dht * mm * xc
        dxc = dht * mm * ig
        dla = da * a - dmm * (a * a) / mm
        dr = dla * (-LRU_C * sp)
        dsp = jnp.sum(dla * (-LRU_C * r), axis=0, keepdims=True)
        dlam = dsp * (-jax.nn.sigmoid(-lam))
        dpa = dr * r * (1.0 - r)
        dpx = dig * ig * (1.0 - ig)
        dpa_b, dpx_b = dpa.astype(BF16), dpx.astype(BF16)
        nt = (((1,), (1,)), ((), ()))
        tn = (((0,), (0,)), ((), ()))
        dxc = dxc + lax.dot_general(dpa_b, wa_ref[...], nt, preferred_element_type=F32) \
                  + lax.dot_general(dpx_b, wx_ref[...], nt, preferred_element_type=F32)
        dwa = lax.dot_general(xb, dpa_b, tn, preferred_element_type=F32)
        dwx = lax.dot_general(xb, dpx_b, tn, preferred_element_type=F32)

        def fwd(v, s):
            return jnp.where(row < S - s, pltpu.roll(v, S - s, 0), 0.0)

        def back(v, s):
            return jnp.where(row >= s, pltpu.roll(v, s, 0), 0.0)

        dx_ref[...] = cw[3:4] * dxc + cw[2:3] * fwd(dxc, 1) + cw[1:2] * fwd(dxc, 2) + cw[0:1] * fwd(dxc, 3)
        s0 = lambda v: jnp.sum(v, axis=0, keepdims=True)
        dcw = jnp.concatenate([s0(dxc * back(x, 3)), s0(dxc * back(x, 2)), s0(dxc * back(x, 1)), s0(dxc * x)], axis=0)
        parts = ((dcw_ref, dcw), (dcb_ref, s0(dxc)), (dwa_ref, dwa), (dba_ref, s0(dpa)), (dwx_ref, dwx),
                 (dbx_ref, s0(dpx)), (dlam_ref, dlam))

        @pl.when(pl.program_id(1) == 0)
        def _():
            for ref, val in parts:
                ref[...] = val

        @pl.when(pl.program_id(1) > 0)
        def _():
            for ref, val in parts:
                ref[...] += val

    blk = lambda col: pl.BlockSpec((S, C), lambda c, b: (b, col + c))
    par = lambda rows: pl.BlockSpec((rows, C), lambda c, b: (0, c))
    sq = pl.BlockSpec((None, C, C), lambda c, b: (c, 0, 0))
    sd = jax.ShapeDtypeStruct
    W = LRU_HALVES * C
    vec = sd((1, W), F32)
    return pl.pallas_call(
        body, name=name, grid=(LRU_HALVES, T // S),
        in_specs=[blk(0), blk(LRU_HALVES), blk(0), blk(3 * LRU_HALVES), par(4), par(1), sq, par(1), sq, par(1), par(1)],
        out_specs=[blk(0), blk(0), par(4), par(1), sq, par(1), sq, par(1), par(1)],
        out_shape=[sd((T, W), F32), sd((T, W), F32), sd((4, W), F32), vec, sd((LRU_HALVES, C, C), F32), vec,
                   sd((LRU_HALVES, C, C), F32), vec, vec],
        scratch_shapes=[pltpu.VMEM((S, C), F32), pltpu.VMEM((S, C), F32)],
        compiler_params=_params("parallel", "arbitrary"),
    )(aux, aux, h, dmixed, cw, cb, wa, ba, wx, bx, lam)


_NT = (((1,), (1,)), ((), ()))
_TN = (((0,), (0,)), ((), ()))


def _dot(a, b, dims=None):
    if dims is None:
        return jnp.dot(a, b, preferred_element_type=F32)
    return lax.dot_general(a, b, dims, preferred_element_type=F32)


def _hs(h):
    return slice(h * HEAD_DIM, (h + 1) * HEAD_DIM)


def cross_fwd(q, kv, *, name):
    T = q.shape[0]
    tq = 512

    def body(q_ref, kv_ref, o_ref):
        for h in range(N_HEADS):
            qh = q_ref[:, _hs(h)] * Q_SCALE
            k = kv_ref[:, _hs(h)]
            v = kv_ref[:, GROUP_W + h * HEAD_DIM:GROUP_W + (h + 1) * HEAD_DIM]
            s = _dot(qh, k, _NT)
            p = jnp.exp(s - jnp.max(s, axis=-1, keepdims=True))
            p = p / jnp.sum(p, axis=-1, keepdims=True)
            o_ref[:, _hs(h)] = _dot(p.astype(BF16), v).astype(BF16)

    per = SEQ // tq
    return pl.pallas_call(
        body, name=name, grid=(T // tq,),
        in_specs=[pl.BlockSpec((tq, GROUP_W), lambda i: (i, 0)), pl.BlockSpec((N_MEM, 2 * GROUP_W), lambda i: (i // per, 0))],
        out_specs=pl.BlockSpec((tq, GROUP_W), lambda i: (i, 0)), out_shape=jax.ShapeDtypeStruct((T, GROUP_W), BF16),
        compiler_params=_params("parallel"),
    )(q, kv)


def cross_bwd(q, kv, do, *, name):
    T = q.shape[0]
    tq = 512
    per = SEQ // tq

    def body(q_ref, kv_ref, do_ref, dq_ref, dkv_ref):
        first = pl.program_id(0) % per == 0
        for h in range(N_HEADS):
            vs = slice(GROUP_W + h * HEAD_DIM, GROUP_W + (h + 1) * HEAD_DIM)
            qh = q_ref[:, _hs(h)] * Q_SCALE
            k = kv_ref[:, _hs(h)]
            v = kv_ref[:, vs]
            doh = do_ref[:, _hs(h)].astype(BF16)
            s = _dot(qh, k, _NT)
            p = jnp.exp(s - jnp.max(s, axis=-1, keepdims=True))
            p = p / jnp.sum(p, axis=-1, keepdims=True)
            dp = _dot(doh, v, _NT)
            ds = (p * (dp - jnp.sum(p * dp, axis=-1, keepdims=True))).astype(BF16)
            dq_ref[:, _hs(h)] = (_dot(ds, k) * Q_SCALE).astype(BF16)
            dk = _dot(ds, qh, _TN)
            dv = _dot(p.astype(BF16), doh, _TN)

            @pl.when(first)
            def _():
                dkv_ref[:, _hs(h)] = dk
                dkv_ref[:, vs] = dv

            @pl.when(jnp.logical_not(first))
            def _():
                dkv_ref[:, _hs(h)] += dk
                dkv_ref[:, vs] += dv

    qb = pl.BlockSpec((tq, GROUP_W), lambda i: (i, 0))
    kvb = pl.BlockSpec((N_MEM, 2 * GROUP_W), lambda i: (i // per, 0))
    sd = jax.ShapeDtypeStruct
    return pl.pallas_call(
        body, name=name, grid=(T // tq,),
        in_specs=[qb, kvb, qb], out_specs=[qb, kvb],
        out_shape=[sd((T, GROUP_W), BF16), sd(kv.shape, F32)],
        compiler_params=_params("arbitrary"),
    )(q, kv, do)


NB = SEQ // BLOCK
NEG = -1e30


def _split_dot(x, tri):
    hi = x.astype(BF16)
    lo = (x - hi.astype(F32)).astype(BF16)
    return _dot(hi, tri) + _dot(lo, tri)


def _blk(i):
    return pl.ds(pl.multiple_of(i * BLOCK, BLOCK), BLOCK)


def _iotas():
    row = lax.broadcasted_iota(jnp.int32, (BLOCK, BLOCK), 0)
    col = lax.broadcasted_iota(jnp.int32, (BLOCK, BLOCK), 1)
    return row, col


def _sb_scores(q, k, mask, later, csum, want_sigmoid=False):
    z = _dot(q, k, _NT)
    lk = -_softplus(z)
    if mask is not None:
        lk = jnp.where(mask, lk, 0.0)
    lka = _split_dot(lk, later) + csum
    att = jnp.exp(z + lk + lka)
    sg = jnp.exp(z + lk) if want_sigmoid else None
    if mask is not None:
        att = jnp.where(mask, att, 0.0)
        sg = jnp.where(mask, sg, 0.0) if want_sigmoid else None
    return att, sg, lk


def _rowsum(v):
    return jnp.sum(v, axis=1, keepdims=True)


HEADS = tuple(range(N_HEADS))


def _qkv_specs(first_col):
    return [pl.BlockSpec((SEQ, GROUP_W), lambda b, c=first_col + j: (b, c)) for j in range(3)]


LANES = 128
CUM_BLK = 256


def col_to_row(c):
    b = c.shape[0] // SEQ
    return c.reshape(b, SEQ, LANES)[:, :, :8].transpose(0, 2, 1).reshape(b * 8, SEQ)


def row_to_col(r):
    b = r.shape[0] // 8
    c = r.reshape(b, 8, SEQ).transpose(0, 2, 1)
    return jnp.pad(c, ((0, 0), (0, 0), (0, LANES - 8))).reshape(b * SEQ, LANES)


def fox_prep(aux, bf, *, name):
    T = aux.shape[0]

    def body(f_ref, b_ref, o_ref):
        row = lax.broadcasted_iota(jnp.int32, (CUM_BLK, CUM_BLK), 0)
        col = lax.broadcasted_iota(jnp.int32, (CUM_BLK, CUM_BLK), 1)
        upto = (col <= row).astype(BF16)
        carry = jnp.zeros((1, LANES), F32)
        for n in range(SEQ // CUM_BLK):
            rows = slice(n * CUM_BLK, (n + 1) * CUM_BLK)
            logf = -_softplus(-(f_ref[rows, :] + b_ref[...]))
            hi = logf.astype(BF16)
            lo = (logf - hi.astype(F32)).astype(BF16)
            cum = _dot(upto, hi) + _dot(upto, lo) + carry
            o_ref[rows, :] = cum
            carry = cum[CUM_BLK - 1:CUM_BLK]

    return pl.pallas_call(
        body, name=name, grid=(T // SEQ,),
        in_specs=[pl.BlockSpec((SEQ, LANES), lambda b: (b, 4)), pl.BlockSpec((1, LANES), lambda b: (0, 0))],
        out_specs=pl.BlockSpec((SEQ, LANES), lambda b: (b, 0)), out_shape=jax.ShapeDtypeStruct((T, LANES), F32),
        compiler_params=_params("parallel"),
    )(aux, bf)


def fox_prep_bwd(aux, bf, dcum, *, name):
    T = aux.shape[0]

    def body(f_ref, b_ref, d_ref, df_ref, db_ref):
        row = lax.broadcasted_iota(jnp.int32, (CUM_BLK, CUM_BLK), 0)
        col = lax.broadcasted_iota(jnp.int32, (CUM_BLK, CUM_BLK), 1)
        onward = (col >= row).astype(BF16)
        carry = jnp.zeros((1, LANES), F32)
        tot = jnp.zeros((1, LANES), F32)
        for n in range(SEQ // CUM_BLK - 1, -1, -1):
            rows = slice(n * CUM_BLK, (n + 1) * CUM_BLK)
            d = d_ref[rows, :]
            hi = d.astype(BF16)
            lo = (d - hi.astype(F32)).astype(BF16)
            dlogf = _dot(onward, hi) + _dot(onward, lo) + carry
            carry = dlogf[0:1]
            df = dlogf * jax.nn.sigmoid(-(f_ref[rows, :] + b_ref[...]))
            df_ref[rows, :] = df
            tot = tot + jnp.sum(df, axis=0, keepdims=True)

        @pl.when(pl.program_id(0) == 0)
        def _():
            db_ref[...] = tot

        @pl.when(pl.program_id(0) > 0)
        def _():
            db_ref[...] += tot

    blk = pl.BlockSpec((SEQ, LANES), lambda b: (b, 0))
    vec = pl.BlockSpec((1, LANES), lambda b: (0, 0))
    sd = jax.ShapeDtypeStruct
    return pl.pallas_call(
        body, name=name, grid=(T // SEQ,),
        in_specs=[pl.BlockSpec((SEQ, LANES), lambda b: (b, 4)), vec, blk],
        out_specs=[blk, vec], out_shape=[sd((T, LANES), F32), sd((1, LANES), F32)],
        compiler_params=_params("arbitrary"),
    )(aux, bf, dcum)


def _fox_logits(q, k, cq, ck, mask):
    z = _dot(q, k, _NT) + cq - ck
    return z if mask is None else jnp.where(mask, z, NEG)


def fox_fwd(qkv, cumc, cumr, *, name):
    T = qkv.shape[0]

    def body(q_ref, k_ref, v_ref, cc_ref, cr_ref, o_ref, lse_ref, z_s):
        row, col = _iotas()
        causal = col <= row
        lse_ref[...] = jnp.zeros_like(lse_ref)

        def qblock(i, _):
            qs = [q_ref[_blk(i), _hs(h)] * Q_SCALE for h in HEADS]
            cqs = [cc_ref[_blk(i), h:h + 1] for h in HEADS]

            def logits(j, mask, ms):
                out = []
                for h in HEADS:
                    z = _fox_logits(qs[h], k_ref[_blk(j), _hs(h)], cqs[h], cr_ref[h:h + 1, _blk(j)], mask)
                    z_s[h, j] = z
                    out.append(jnp.maximum(ms[h], jnp.max(z, axis=1, keepdims=True)))
                return tuple(out)

            ms = logits(i, causal, (jnp.full((BLOCK, 1), NEG, F32),) * N_HEADS)
            ms = lax.fori_loop(0, i, lambda j, c: logits(j, None, c), ms)

            def values(j, carry):
                out = []
                for h in HEADS:
                    acc, l = carry[h]
                    p = jnp.exp(z_s[h, j] - ms[h])
                    out.append((acc + _dot(p.astype(BF16), v_ref[_blk(j), _hs(h)]), l + _rowsum(p)))
                return tuple(out)

            zero = (jnp.zeros((BLOCK, HEAD_DIM), F32), jnp.zeros((BLOCK, 1), F32))
            res = lax.fori_loop(0, i + 1, values, (zero,) * N_HEADS)
            for h in HEADS:
                acc, l = res[h]
                o_ref[_blk(i), _hs(h)] = (acc / l).astype(BF16)
                lse_ref[_blk(i), h:h + 1] = ms[h] + jnp.log(l)
            return 0

        lax.fori_loop(0, NB, qblock, 0)

    out = pl.BlockSpec((SEQ, GROUP_W), lambda b: (b, 0))
    colb = pl.BlockSpec((SEQ, LANES), lambda b: (b, 0))
    sd = jax.ShapeDtypeStruct
    return pl.pallas_call(
        body, name=name, grid=(T // SEQ,),
        in_specs=_qkv_specs(3) + [colb, pl.BlockSpec((8, SEQ), lambda b: (b, 0))],
        out_specs=[out, colb], out_shape=[sd((T, GROUP_W), BF16), sd((T, LANES), F32)],
        scratch_shapes=[pltpu.VMEM((N_HEADS, NB, BLOCK, BLOCK), F32)],
        compiler_params=_params("parallel"),
    )(qkv, qkv, qkv, cumc, cumr)


def fox_bwd(qkv, cumc, cumr, lse, dmixed, *, name):
    T = qkv.shape[0]

    def body(q_ref, k_ref, v_ref, cc_ref, cr_ref, lse_ref, do_ref, dq_ref, dk_ref, dv_ref, dcc_ref, dcr_ref, p_s, dp_s):
        row, col = _iotas()
        causal = col <= row
        dk_ref[...] = jnp.zeros_like(dk_ref)
        dv_ref[...] = jnp.zeros_like(dv_ref)
        dcc_ref[...] = jnp.zeros_like(dcc_ref)
        dcr_ref[...] = jnp.zeros_like(dcr_ref)

        def qblock(i, _):
            qs = [q_ref[_blk(i), _hs(h)] * Q_SCALE for h in HEADS]
            dos = [do_ref[_blk(i), _hs(h)].astype(BF16) for h in HEADS]
            cqs = [cc_ref[_blk(i), h:h + 1] for h in HEADS]
            lses = [lse_ref[_blk(i), h:h + 1] for h in HEADS]

            def probs(j, mask, deltas):
                out = []
                for h in HEADS:
                    z = _fox_logits(qs[h], k_ref[_blk(j), _hs(h)], cqs[h], cr_ref[h:h + 1, _blk(j)], mask)
                    p = jnp.exp(z - lses[h])
                    dp = _dot(dos[h], v_ref[_blk(j), _hs(h)], _NT)
                    p_s[h, j] = p
                    dp_s[h, j] = dp
                    out.append(deltas[h] + _rowsum(p * dp))
                return tuple(out)

            deltas = probs(i, causal, (jnp.zeros((BLOCK, 1), F32),) * N_HEADS)
            deltas = lax.fori_loop(0, i, lambda j, c: probs(j, None, c), deltas)

            def kblock(j, carry):
                out = []
                for h in HEADS:
                    dq, dcq = carry[h]
                    p = p_s[h, j]
                    ds = p * (dp_s[h, j] - deltas[h])
                    dsb = ds.astype(BF16)
                    dk_ref[_blk(j), _hs(h)] += _dot(dsb, qs[h], _TN)
                    dv_ref[_blk(j), _hs(h)] += _dot(p.astype(BF16), dos[h], _TN)
                    dcr_ref[h:h + 1, _blk(j)] -= jnp.sum(ds, axis=0, keepdims=True)
                    out.append((dq + _dot(dsb, k_ref[_blk(j), _hs(h)]), dcq + _rowsum(ds)))
                return tuple(out)

            zero = (jnp.zeros((BLOCK, HEAD_DIM), F32), jnp.zeros((BLOCK, 1), F32))
            res = lax.fori_loop(0, i + 1, kblock, (zero,) * N_HEADS)
            for h in HEADS:
                dq_ref[_blk(i), _hs(h)] = res[h][0] * Q_SCALE
                dcc_ref[_blk(i), h:h + 1] = res[h][1]
            return 0

        lax.fori_loop(0, NB, qblock, 0)

    out = pl.BlockSpec((SEQ, GROUP_W), lambda b: (b, 0))
    colb = pl.BlockSpec((SEQ, LANES), lambda b: (b, 0))
    rowb = pl.BlockSpec((8, SEQ), lambda b: (b, 0))
    sd = jax.ShapeDtypeStruct
    big = sd((T, GROUP_W), F32)
    return pl.pallas_call(
        body, name=name, grid=(T // SEQ,),
        in_specs=_qkv_specs(3) + [colb, rowb, colb, pl.BlockSpec((SEQ, GROUP_W), lambda b: (b, 1))],
        out_specs=[out, out, out, colb, rowb],
        out_shape=[big, big, big, sd((T, LANES), F32), sd((T // SEQ * 8, SEQ), F32)],
        scratch_shapes=[pltpu.VMEM((N_HEADS, NB, BLOCK, BLOCK), F32), pltpu.VMEM((N_HEADS, NB, BLOCK, BLOCK), F32)],
        compiler_params=_params("parallel"),
    )(qkv, qkv, qkv, cumc, cumr, lse, dmixed)


CHUNK = 256
WIDE = N_HEADS * CHUNK
NCH = SEQ // CHUNK


def _seg(h):
    return slice(h * CHUNK, (h + 1) * CHUNK)


def _chunk_rows(c):
    return pl.ds(pl.multiple_of(c * CHUNK, CHUNK), CHUNK)


def _wide_consts():
    r = lax.broadcasted_iota(jnp.int32, (WIDE, GROUP_W), 0)
    f = lax.broadcasted_iota(jnp.int32, (WIDE, GROUP_W), 1)
    bd = (r // CHUNK) == (f // HEAD_DIM)
    row = lax.broadcasted_iota(jnp.int32, (BLOCK, WIDE), 0)
    key = lax.broadcasted_iota(jnp.int32, (BLOCK, WIDE), 1) % CHUNK
    return bd, row, key


def _block_diag(x, bd):
    return jnp.where(bd, jnp.concatenate([x] * N_HEADS, axis=0), jnp.zeros((), x.dtype))


def _fold_heads(w, bd):
    w = jnp.where(bd, w, 0.0)
    return (w[0:CHUNK] + w[CHUNK:2 * CHUNK]) + (w[2 * CHUNK:3 * CHUNK] + w[3 * CHUNK:])


def _widen(cols):
    return jnp.concatenate([jnp.broadcast_to(c, (BLOCK, CHUNK)) for c in cols], axis=1)


def _head_rowsums(w):
    return [jnp.sum(w[:, _seg(h)], axis=1, keepdims=True) for h in HEADS]


def _tri_wide(x, tri):
    hi = x.astype(BF16)
    lo = (x - hi.astype(F32)).astype(BF16)
    y = _dot(jnp.concatenate([hi[:, _seg(h)] for h in HEADS] + [lo[:, _seg(h)] for h in HEADS], axis=0), tri)
    return jnp.concatenate([y[h * BLOCK:(h + 1) * BLOCK] + y[(N_HEADS + h) * BLOCK:(N_HEADS + h + 1) * BLOCK] for h in HEADS], axis=1)


def _feature_widen(cols):
    return jnp.concatenate([jnp.broadcast_to(c, (BLOCK, HEAD_DIM)) for c in cols], axis=1)


def _sbw_tile(q, kbd, mask, later, csum):
    z = _dot(q, kbd, _NT)
    lk = -_softplus(z)
    if mask is not None:
        lk = jnp.where(mask, lk, 0.0)
    e = z + lk
    att = jnp.exp(e + _tri_wide(lk, later) + csum)
    if mask is not None:
        att = jnp.where(mask, att, 0.0)
    return att, e, lk


def sbw_fwd(qkv, *, name):
    T = qkv.shape[0]

    def body(q_ref, k_ref, v_ref, o_ref):
        bd, row, key = _wide_consts()
        r2 = lax.broadcasted_iota(jnp.int32, (CHUNK, CHUNK), 0)
        c2 = lax.broadcasted_iota(jnp.int32, (CHUNK, CHUNK), 1)
        later = (r2 > c2).astype(BF16)

        def qblock(i, _):
            q = q_ref[_blk(i), :] * Q_SCALE
            cd = i // 2
            strict = key < row + BLOCK * (i % 2)

            def tile(c, mask, carry):
                acc, csum = carry
                att, _, lk = _sbw_tile(q, _block_diag(k_ref[_chunk_rows(c), :], bd), mask, later, csum)
                acc = acc + _dot(att.astype(BF16), _block_diag(v_ref[_chunk_rows(c), :], bd))
                return acc, csum + _widen(_head_rowsums(lk))

            carry = tile(cd, strict, (jnp.zeros((BLOCK, GROUP_W), F32), jnp.zeros((BLOCK, WIDE), F32)))
            acc, _ = lax.fori_loop(0, cd, lambda n, cr: tile(cd - 1 - n, None, cr), carry)
            o_ref[_blk(i), :] = acc.astype(BF16)
            return 0

        lax.fori_loop(0, NB, qblock, 0)

    return pl.pallas_call(
        body, name=name, grid=(T // SEQ,), in_specs=_qkv_specs(0),
        out_specs=pl.BlockSpec((SEQ, GROUP_W), lambda b: (b, 0)), out_shape=jax.ShapeDtypeStruct((T, GROUP_W), BF16),
        compiler_params=_params("parallel"),
    )(qkv, qkv, qkv)


def sbw_bwd(qkv, dmixed, *, name):
    T = qkv.shape[0]

    def body(q_ref, k_ref, v_ref, do_ref, dq_ref, dk_ref, dv_ref, att_s, sg_s):
        bd, row, key = _wide_consts()
        r2 = lax.broadcasted_iota(jnp.int32, (CHUNK, CHUNK), 0)
        c2 = lax.broadcasted_iota(jnp.int32, (CHUNK, CHUNK), 1)
        later = (r2 > c2).astype(BF16)
        earlier = (r2 < c2).astype(BF16)
        dk_ref[...] = jnp.zeros_like(dk_ref)
        dv_ref[...] = jnp.zeros_like(dv_ref)

        def qblock(i, _):
            q = q_ref[_blk(i), :] * Q_SCALE
            do = do_ref[_blk(i), :].astype(BF16)
            cd = i // 2
            strict = key < row + BLOCK * (i % 2)

            def recompute(c, mask, csum):
                att, e, lk = _sbw_tile(q, _block_diag(k_ref[_chunk_rows(c), :], bd), mask, later, csum)
                sg = jnp.exp(e)
                att_s[c] = att
                sg_s[c] = sg if mask is None else jnp.where(mask, sg, 0.0)
                return csum + _widen(_head_rowsums(lk))

            csum = recompute(cd, strict, jnp.zeros((BLOCK, WIDE), F32))
            lax.fori_loop(0, cd, lambda n, cs: recompute(cd - 1 - n, None, cs), csum)

            def tile(c, carry):
                dq, pre = carry
                kbd = _block_diag(k_ref[_chunk_rows(c), :], bd)
                vbd = _block_diag(v_ref[_chunk_rows(c), :], bd)
                att = att_s[c]
                ds = _dot(do, vbd, _NT) * att
                dlk = ds + _tri_wide(ds, earlier) + pre
                dz = (ds - dlk * sg_s[c]).astype(BF16)
                dk_ref[_chunk_rows(c), :] += _fold_heads(_dot(dz, q, _TN), bd)
                dv_ref[_chunk_rows(c), :] += _fold_heads(_dot(att.astype(BF16), do, _TN), bd)
                return dq + _dot(dz, kbd), pre + _widen(_head_rowsums(ds))

            dq, _ = lax.fori_loop(0, cd + 1, tile, (jnp.zeros((BLOCK, GROUP_W), F32), jnp.zeros((BLOCK, WIDE), F32)))
            dq_ref[_blk(i), :] = dq * Q_SCALE
            return 0

        lax.fori_loop(0, NB, qblock, 0)

    out = pl.BlockSpec((SEQ, GROUP_W), lambda b: (b, 0))
    sd = jax.ShapeDtypeStruct((T, GROUP_W), F32)
    return pl.pallas_call(
        body, name=name, grid=(T // SEQ,), in_specs=_qkv_specs(0) + [out],
        out_specs=[out] * 3, out_shape=[sd] * 3,
        scratch_shapes=[pltpu.VMEM((NCH, BLOCK, WIDE), F32), pltpu.VMEM((NCH, BLOCK, WIDE), F32)],
        compiler_params=_params("parallel"),
    )(qkv, qkv, qkv, dmixed)


def _foxw_logits(q, kbd, cq, cr_ref, c, mask):
    ck = jnp.concatenate([cr_ref[h:h + 1, _chunk_rows(c)] for h in HEADS], axis=1)
    z = _dot(q, kbd, _NT) + cq - ck
    return z if mask is None else jnp.where(mask, z, NEG)


def foxw_fwd(qkv, cumc, cumr, *, name):
    T = qkv.shape[0]

    def body(q_ref, k_ref, v_ref, cc_ref, cr_ref, o_ref, o32_ref, lse_ref, z_s):
        bd, row, key = _wide_consts()
        lse_ref[...] = jnp.zeros_like(lse_ref)

        def qblock(i, _):
            q = q_ref[_blk(i), :] * Q_SCALE
            cq = _widen([cc_ref[_blk(i), h:h + 1] for h in HEADS])
            cd = i // 2
            causal = key <= row + BLOCK * (i % 2)

            def logits(c, mask, ms):
                z = _foxw_logits(q, _block_diag(k_ref[_chunk_rows(c), :], bd), cq, cr_ref, c, mask)
                z_s[c] = z
                return tuple(jnp.maximum(ms[h], jnp.max(z[:, _seg(h)], axis=1, keepdims=True)) for h in HEADS)

            ms = logits(cd, causal, (jnp.full((BLOCK, 1), NEG, F32),) * N_HEADS)
            ms = lax.fori_loop(0, cd, lambda c, m: logits(c, None, m), ms)
            m_wide = _widen(ms)

            def values(c, carry):
                acc, l = carry
                p = jnp.exp(z_s[c] - m_wide)
                return acc + _dot(p.astype(BF16), _block_diag(v_ref[_chunk_rows(c), :], bd)), l + _widen(_head_rowsums(p))

            acc, l = lax.fori_loop(0, cd + 1, values, (jnp.zeros((BLOCK, GROUP_W), F32), jnp.zeros((BLOCK, WIDE), F32)))
            ls = [l[:, h * CHUNK:h * CHUNK + 1] for h in HEADS]
            o = acc / _feature_widen(ls)
            o_ref[_blk(i), :] = o.astype(BF16)
            o32_ref[_blk(i), :] = o
            for h in HEADS:
                lse_ref[_blk(i), h:h + 1] = ms[h] + jnp.log(ls[h])
            return 0

        lax.fori_loop(0, NB, qblock, 0)

    out = pl.BlockSpec((SEQ, GROUP_W), lambda b: (b, 0))
    colb = pl.BlockSpec((SEQ, LANES), lambda b: (b, 0))
    sd = jax.ShapeDtypeStruct
    return pl.pallas_call(
        body, name=name, grid=(T // SEQ,),
        in_specs=_qkv_specs(3) + [colb, pl.BlockSpec((8, SEQ), lambda b: (b, 0))],
        out_specs=[out, out, colb], out_shape=[sd((T, GROUP_W), BF16), sd((T, GROUP_W), F32), sd((T, LANES), F32)],
        scratch_shapes=[pltpu.VMEM((NCH, BLOCK, WIDE), F32)],
        compiler_params=_params("parallel"),
    )(qkv, qkv, qkv, cumc, cumr)


def foxw_bwd(qkv, cumc, cumr, lse, o32, dmixed, *, name):
    T = qkv.shape[0]

    def body(q_ref, k_ref, v_ref, cc_ref, cr_ref, lse_ref, o_ref, do_ref, dq_ref, dk_ref, dv_ref, dcc_ref, dcr_ref):
        bd, row, key = _wide_consts()
        dk_ref[...] = jnp.zeros_like(dk_ref)
        dv_ref[...] = jnp.zeros_like(dv_ref)
        dcc_ref[...] = jnp.zeros_like(dcc_ref)
        dcr_ref[...] = jnp.zeros_like(dcr_ref)

        def qblock(i, _):
            q = q_ref[_blk(i), :] * Q_SCALE
            do32 = do_ref[_blk(i), :]
            do = do32.astype(BF16)
            prod = do32 * o_ref[_blk(i), :]
            delta = _widen([jnp.sum(prod[:, _hs(h)], axis=1, keepdims=True) for h in HEADS])
            cq = _widen([cc_ref[_blk(i), h:h + 1] for h in HEADS])
            lse_w = _widen([lse_ref[_blk(i), h:h + 1] for h in HEADS])
            cd = i // 2
            causal = key <= row + BLOCK * (i % 2)

            def tile(c, mask, carry):
                dq, dcq = carry
                kbd = _block_diag(k_ref[_chunk_rows(c), :], bd)
                vbd = _block_diag(v_ref[_chunk_rows(c), :], bd)
                p = jnp.exp(_foxw_logits(q, kbd, cq, cr_ref, c, mask) - lse_w)
                ds = p * (_dot(do, vbd, _NT) - delta)
                dsb = ds.astype(BF16)
                dk_ref[_chunk_rows(c), :] += _fold_heads(_dot(dsb, q, _TN), bd)
                dv_ref[_chunk_rows(c), :] += _fold_heads(_dot(p.astype(BF16), do, _TN), bd)
                for h in HEADS:
                    dcr_ref[h:h + 1, _chunk_rows(c)] -= jnp.sum(ds[:, _seg(h)], axis=0, keepdims=True)
                return dq + _dot(dsb, kbd), dcq + _widen(_head_rowsums(ds))

            carry = tile(cd, causal, (jnp.zeros((BLOCK, GROUP_W), F32), jnp.zeros((BLOCK, WIDE), F32)))
            dq, dcq = lax.fori_loop(0, cd, lambda c, cr: tile(c, None, cr), carry)
            dq_ref[_blk(i), :] = dq * Q_SCALE
            for h in HEADS:
                dcc_ref[_blk(i), h:h + 1] = dcq[:, h * CHUNK:h * CHUNK + 1]
            return 0

        lax.fori_loop(0, NB, qblock, 0)

    out = pl.BlockSpec((SEQ, GROUP_W), lambda b: (b, 0))
    colb = pl.BlockSpec((SEQ, LANES), lambda b: (b, 0))
    rowb = pl.BlockSpec((8, SEQ), lambda b: (b, 0))
    sd = jax.ShapeDtypeStruct
    big = sd((T, GROUP_W), F32)
    return pl.pallas_call(
        body, name=name, grid=(T // SEQ,),
        in_specs=_qkv_specs(3) + [colb, rowb, colb, out, pl.BlockSpec((SEQ, GROUP_W), lambda b: (b, 1))],
        out_specs=[out, out, out, colb, rowb],
        out_shape=[big, big, big, sd((T, LANES), F32), sd((T // SEQ * 8, SEQ), F32)],
        compiler_params=_params("parallel"),
    )(qkv, qkv, qkv, cumc, cumr, lse, o32, dmixed)


BAND = 2 * BLOCK


def _t5_bucket_np(dist):
    n = np.maximum(dist, 0)
    max_exact = NUM_BUCKETS // 2
    nf = np.maximum(n, 1).astype(np.float32)
    large = max_exact + (np.log(nf / np.float32(max_exact)) / np.float32(math.log(MAX_DISTANCE / max_exact))
                         * np.float32(NUM_BUCKETS - max_exact)).astype(np.int32)
    large = np.minimum(large, NUM_BUCKETS - 1)
    return np.where(n < max_exact, n, large).astype(np.int32)


def _band_buckets():
    qi = np.arange(BLOCK)[:, None]
    ki = np.arange(BAND)[None, :]
    delta = np.clip(qi - ki + BLOCK, 0, BLOCK)
    return np.stack([_t5_bucket_np(delta * d) for d in DILATIONS])


def to_classes(a, d):
    if d == 1:
        return a
    T, C = a.shape
    return a.reshape(T // SEQ, SEQ // d, d, C).transpose(0, 2, 1, 3).reshape(T, C)


def from_classes(a, d):
    if d == 1:
        return a
    T, C = a.shape
    return a.reshape(T // SEQ, d, SEQ // d, C).transpose(0, 2, 1, 3).reshape(T, C)


def relbias_expand(rel, *, name):
    buckets = jnp.asarray(_band_buckets())
    n_pat = len(DILATIONS)

    def body(rel_ref, bk_ref, o_ref):
        for p in range(n_pat):
            bk = bk_ref[p]
            for h in range(N_HEADS):
                acc = jnp.zeros((BLOCK, BAND), F32)
                for b in range(NUM_BUCKETS):
                    acc = jnp.where(bk == b, rel_ref[b, h], acc)
                o_ref[p * N_HEADS + h] = acc

    return pl.pallas_call(
        body, name=name,
        in_specs=[pl.BlockSpec(memory_space=pltpu.SMEM), pl.BlockSpec(memory_space=pltpu.VMEM)],
        out_specs=pl.BlockSpec(memory_space=pltpu.VMEM),
        out_shape=jax.ShapeDtypeStruct((n_pat * N_HEADS, BLOCK, BAND), F32),
        compiler_params=_params(),
    )(rel, buckets)


def relbias_reduce(ds_all, *, name):
    buckets = jnp.asarray(_band_buckets())
    n_pat = len(DILATIONS)

    def body(ds_ref, bk_ref, o_ref):
        for b in range(NUM_BUCKETS):
            for h in range(N_HEADS):
                tot = jnp.float32(0.0)
                for p in range(n_pat):
                    tot = tot + jnp.sum(jnp.where(bk_ref[p] == b, ds_ref[p * N_HEADS + h], 0.0))
                o_ref[b, h] = tot

    return pl.pallas_call(
        body, name=name,
        in_specs=[pl.BlockSpec(memory_space=pltpu.VMEM), pl.BlockSpec(memory_space=pltpu.VMEM)],
        out_specs=pl.BlockSpec(memory_space=pltpu.SMEM),
        out_shape=jax.ShapeDtypeStruct((NUM_BUCKETS, N_HEADS), F32),
        compiler_params=_params(),
    )(ds_all, buckets)


def _band_valid_wide(first, row, key):
    inside = jnp.logical_and(key >= row, key <= row + BLOCK)
    return jnp.logical_and(inside, jnp.logical_or(jnp.logical_not(first), key >= BLOCK))


QKV_BLOCKS = 9


def _band_in_specs(d, pattern, has_prev):
    rows = BLOCK * d
    cur = lambda c: pl.BlockSpec((rows, GROUP_W), lambda tb, r: (tb, c))
    prev = lambda c: pl.BlockSpec((rows, GROUP_W), lambda tb, r: (jnp.maximum(tb - 1, 0), c))
    bias = pl.BlockSpec((N_HEADS, BLOCK, BAND), lambda tb, r: (pattern, 0, 0))
    return [cur(6), cur(7), cur(8)] + ([prev(7), prev(8)] if has_prev else []) + [bias]


def _class_rows(d):
    return pl.ds(pl.program_id(1), BLOCK, stride=d) if d > 1 else pl.ds(0, BLOCK)


def _halves_scratch(rows, n):
    return [pltpu.VMEM((2, rows, LANES), F32)] * n


def _stage(refs, scratch):
    @pl.when(pl.program_id(1) == 0)
    def _():
        for src, dst in zip(refs, scratch):
            dst[0] = src[:, :LANES].astype(F32)
            dst[1] = src[:, LANES:].astype(F32)


def _take_class(s, d):
    rows = _class_rows(d)
    return jnp.concatenate([s.at[0][rows, :], s.at[1][rows, :]], axis=1)


def _put_class(s, d, x):
    rows = _class_rows(d)
    s.at[0][rows, :] = x[:, :LANES]
    s.at[1][rows, :] = x[:, LANES:]


def _flush(scratch, refs, d):
    @pl.when(pl.program_id(1) == d - 1)
    def _():
        for s, o in zip(scratch, refs):
            o[...] = jnp.concatenate([s[0], s[1]], axis=1)


def _band_operands(scratch, d, has_prev):
    take = lambda s: _take_class(s, d).astype(BF16)
    q = (_take_class(scratch[0], d) * Q_SCALE).astype(BF16)
    if has_prev:
        k = jnp.concatenate([take(scratch[3]), take(scratch[1])], axis=0)
        v = jnp.concatenate([take(scratch[4]), take(scratch[2])], axis=0)
    else:
        k = jnp.concatenate([jnp.zeros((BLOCK, GROUP_W), BF16), take(scratch[1])], axis=0)
        v = jnp.concatenate([jnp.zeros((BLOCK, GROUP_W), BF16), take(scratch[2])], axis=0)
    return q, k, v


def _lane_columns(cols):
    lane = lax.broadcasted_iota(jnp.int32, (BLOCK, LANES), 1)
    out = jnp.zeros((BLOCK, LANES), F32)
    for h, c in enumerate(cols):
        out = jnp.where(lane == h, c, out)
    return out


def band_fwd(qkv, bias, pattern, *, name):
    T = qkv.shape[0]
    d = DILATIONS[pattern]
    rows_per_block = BLOCK * d
    seq_blocks = SEQ // rows_per_block
    has_prev = seq_blocks > 1
    n_in = 5 if has_prev else 3

    def body(*refs):
        ins, b_ref, o_ref, lse_ref = refs[:n_in], refs[n_in], refs[n_in + 1], refs[n_in + 2]
        staged, o_s = refs[n_in + 3:2 * n_in + 3], refs[2 * n_in + 3]
        bd, row, key = _wide_consts()
        valid = _band_valid_wide(pl.program_id(0) % seq_blocks == 0, row, key)
        _stage(ins, staged)
        q, k, v = _band_operands(staged, d, has_prev)
        kbd, vbd = _block_diag(k, bd), _block_diag(v, bd)
        bias_w = jnp.concatenate([b_ref[h] for h in HEADS], axis=1)
        sc = jnp.where(valid, _dot(q, kbd, _NT) + bias_w, NEG)
        ms = [jnp.max(sc[:, _seg(h)], axis=1, keepdims=True) for h in HEADS]
        p = jnp.exp(sc - _widen(ms))
        ls = _head_rowsums(p)
        _put_class(o_s, d, _dot(p.astype(BF16), vbd) / _feature_widen(ls))
        lse_ref[_class_rows(d), :] = _lane_columns([ms[h] + jnp.log(ls[h]) for h in HEADS])
        _flush([o_s], [o_ref], d)

    sd = jax.ShapeDtypeStruct
    return pl.pallas_call(
        body, name=name, grid=(T // rows_per_block, d), in_specs=_band_in_specs(d, pattern, has_prev),
        out_specs=[pl.BlockSpec((rows_per_block, GROUP_W), lambda tb, r: (tb, 0)),
                   pl.BlockSpec((rows_per_block, LANES), lambda tb, r: (tb, 0))],
        out_shape=[sd((T, GROUP_W), F32), sd((T, LANES), F32)],
        scratch_shapes=_halves_scratch(rows_per_block, n_in + 1),
        compiler_params=_params("parallel", "arbitrary"),
    )(*([qkv] * n_in), bias)


def band_bwd(qkv, bias, lse, do, dlse, pattern, *, name):
    T = qkv.shape[0]
    d = DILATIONS[pattern]
    rows_per_block = BLOCK * d
    seq_blocks = SEQ // rows_per_block
    has_prev = seq_blocks > 1
    n_in = 5 if has_prev else 3
    n_out = 5 if has_prev else 3

    def body(*refs):
        ins, b_ref, lse_ref, do_ref, dlse_ref = refs[:n_in], refs[n_in], refs[n_in + 1], refs[n_in + 2], refs[n_in + 3]
        outs = refs[n_in + 4:n_in + 4 + n_out]
        ds_ref = refs[n_in + 4 + n_out]
        scratch = refs[n_in + 5 + n_out:]
        staged, do_s, out_s = scratch[:n_in], scratch[n_in], scratch[n_in + 1:]
        first_step = jnp.logical_and(pl.program_id(0) == 0, pl.program_id(1) == 0)
        bd, row, key = _wide_consts()
        valid = _band_valid_wide(pl.program_id(0) % seq_blocks == 0, row, key)
        _stage(list(ins) + [do_ref], list(staged) + [do_s])
        q, k, v = _band_operands(staged, d, has_prev)
        kbd, vbd = _block_diag(k, bd), _block_diag(v, bd)
        rows = _class_rows(d)
        do = _take_class(do_s, d).astype(BF16)
        lse_t, dlse_t = lse_ref[rows, :], dlse_ref[rows, :]
        bias_w = jnp.concatenate([b_ref[h] for h in HEADS], axis=1)
        lse_w = _widen([lse_t[:, h:h + 1] for h in HEADS])
        dlse_w = _widen([dlse_t[:, h:h + 1] for h in HEADS])
        p = jnp.where(valid, jnp.exp(_dot(q, kbd, _NT) + bias_w - lse_w), 0.0)
        dp = _dot(do, vbd, _NT)
        ds = p * (dp - _widen(_head_rowsums(p * dp)) + dlse_w)
        dsb, pb = ds.astype(BF16), p.astype(BF16)
        _put_class(out_s[0], d, _dot(dsb, kbd) * Q_SCALE)
        dk = _fold_heads(_dot(dsb, q, _TN), bd)
        dv = _fold_heads(_dot(pb, do, _TN), bd)
        _put_class(out_s[1], d, dk[BLOCK:])
        _put_class(out_s[2], d, dv[BLOCK:])
        if has_prev:
            _put_class(out_s[3], d, dk[:BLOCK])
            _put_class(out_s[4], d, dv[:BLOCK])
        _flush(out_s, outs, d)

        @pl.when(first_step)
        def _():
            for h in HEADS:
                ds_ref[h] = ds[:, _seg(h)]

        @pl.when(jnp.logical_not(first_step))
        def _():
            for h in HEADS:
                ds_ref[h] += ds[:, _seg(h)]

    big = pl.BlockSpec((rows_per_block, GROUP_W), lambda tb, r: (tb, 0))
    colb = pl.BlockSpec((rows_per_block, LANES), lambda tb, r: (tb, 0))
    sd = jax.ShapeDtypeStruct
    return pl.pallas_call(
        body, name=name, grid=(T // rows_per_block, d), in_specs=_band_in_specs(d, pattern, has_prev) + [colb, big, colb],
        out_specs=[big] * n_out + [pl.BlockSpec((N_HEADS, BLOCK, BAND), lambda tb, r: (0, 0, 0))],
        out_shape=[sd((T, GROUP_W), F32)] * n_out + [sd((N_HEADS, BLOCK, BAND), F32)],
        scratch_shapes=_halves_scratch(rows_per_block, n_in + 1 + n_out),
        compiler_params=_params("arbitrary", "arbitrary"),
    )(*([qkv] * n_in), bias, lse, do, dlse)


def shift_add(cur, prev, d, *, name):
    rows = BLOCK * d
    nb = cur.shape[0] // rows

    def body(c_ref, p_ref, o_ref):
        keep = (pl.program_id(0) < nb - 1).astype(F32)
        o_ref[...] = c_ref[...] + keep * p_ref[...]

    blk = pl.BlockSpec((rows, GROUP_W), lambda tb: (tb, 0))
    nxt = pl.BlockSpec((rows, GROUP_W), lambda tb: (jnp.minimum(tb + 1, nb - 1), 0))
    return pl.pallas_call(
        body, name=name, grid=(nb,), in_specs=[blk, nxt], out_specs=blk,
        out_shape=jax.ShapeDtypeStruct(cur.shape, F32), compiler_params=_params("parallel"),
    )(cur, prev)


def _pattern_weights(lse_refs, h):
    ls = [r[:, h:h + 1] for r in lse_refs]
    mx = functools.reduce(jnp.maximum, ls)
    es = [jnp.exp(l - mx) for l in ls]
    tot = functools.reduce(lambda a, b: a + b, es)
    return [e / tot for e in es]


def dil_combine_fwd(outs, *, name):
    T = outs[0][0].shape[0]
    n = len(outs)
    tm = 512

    def body(*refs):
        o_refs, l_refs, out_ref = refs[:n], refs[n:2 * n], refs[2 * n]
        for h in range(N_HEADS):
            w = _pattern_weights(l_refs, h)
            acc = w[0] * o_refs[0][:, _hs(h)]
            for p in range(1, n):
                acc = acc + w[p] * o_refs[p][:, _hs(h)]
            out_ref[:, _hs(h)] = acc.astype(BF16)

    big = pl.BlockSpec((tm, GROUP_W), lambda i: (i, 0))
    colb = pl.BlockSpec((tm, LANES), lambda i: (i, 0))
    return pl.pallas_call(
        body, name=name, grid=(T // tm,), in_specs=[big] * n + [colb] * n,
        out_specs=big, out_shape=jax.ShapeDtypeStruct((T, GROUP_W), BF16),
        compiler_params=_params("parallel"),
    )(*[o for o, _ in outs], *[l for _, l in outs])


def dil_combine_bwd(outs, dmixed, *, name):
    T = outs[0][0].shape[0]
    n = len(outs)
    tm = 512

    def body(*refs):
        o_refs, l_refs, do_ref = refs[:n], refs[n:2 * n], refs[2 * n]
        do_refs, dl_refs = refs[2 * n + 1:3 * n + 1], refs[3 * n + 1:]
        for r in dl_refs:
            r[...] = jnp.zeros_like(r)
        for h in range(N_HEADS):
            w = _pattern_weights(l_refs, h)
            do = do_ref[:, _hs(h)]
            dw = [jnp.sum(do * o_refs[p][:, _hs(h)], axis=1, keepdims=True) for p in range(n)]
            mean = functools.reduce(lambda a, b: a + b, [w[p] * dw[p] for p in range(n)])
            for p in range(n):
                do_refs[p][:, _hs(h)] = w[p] * do
                dl_refs[p][:, h:h + 1] = w[p] * (dw[p] - mean)

    big = pl.BlockSpec((tm, GROUP_W), lambda i: (i, 0))
    colb = pl.BlockSpec((tm, LANES), lambda i: (i, 0))
    sd = jax.ShapeDtypeStruct
    res = pl.pallas_call(
        body, name=name, grid=(T // tm,),
        in_specs=[big] * n + [colb] * n + [pl.BlockSpec((tm, GROUP_W), lambda i: (i, 2))],
        out_specs=[big] * n + [colb] * n, out_shape=[sd((T, GROUP_W), F32)] * n + [sd((T, LANES), F32)] * n,
        compiler_params=_params("parallel"),
    )(*[o for o, _ in outs], *[l for _, l in outs], dmixed)
    return list(zip(res[:n], res[n:]))


def dilated_fwd(qkv, bias, tag):
    return [band_fwd(qkv, bias, p, name=f"{tag}_band_fwd{p}") for p in range(len(DILATIONS))]


def dilated_bwd(qkv, bias, outs, dmixed, tag):
    grads = dil_combine_bwd(outs, dmixed, name=f"{tag}_combine_bwd")
    parts, ds_all = [], []
    for p, d in enumerate(DILATIONS):
        (_, lse), (do, dlse) = outs[p], grads[p]
        res = band_bwd(qkv, bias, lse, do, dlse, p, name=f"{tag}_band_bwd{p}")
        dq, dk, dv, ds = res[0], res[1], res[2], res[-1]
        if len(res) > 4:
            dk = shift_add(dk, res[3], d, name=f"{tag}_dk{p}")
            dv = shift_add(dv, res[4], d, name=f"{tag}_dv{p}")
        parts.append([dq, dk, dv])
        ds_all.append(ds)
    return parts, jnp.concatenate(ds_all, axis=0)


def assemble_dqkv(d_sb, d_fox, d_dil, *, name):
    T = d_sb[0].shape[0]
    tr = 512
    n_pat = len(d_dil)
    flat = list(d_sb) + list(d_fox) + [a for part in d_dil for a in part]

    def body(*refs):
        o_ref = refs[-1]
        for j in range(6):
            o_ref[:, j * GROUP_W:(j + 1) * GROUP_W] = refs[j][...].astype(BF16)
        for j in range(3):
            acc = refs[6 + j][...]
            for p in range(1, n_pat):
                acc = acc + refs[6 + 3 * p + j][...]
            o_ref[:, (6 + j) * GROUP_W:(7 + j) * GROUP_W] = acc.astype(BF16)

    blk = pl.BlockSpec((tr, GROUP_W), lambda i: (i, 0))
    return pl.pallas_call(
        body, name=name, grid=(T // tr,), in_specs=[blk] * len(flat),
        out_specs=pl.BlockSpec((tr, QKV_BLOCKS * GROUP_W), lambda i: (i, 0)),
        out_shape=jax.ShapeDtypeStruct((T, QKV_BLOCKS * GROUP_W), BF16), compiler_params=_params("parallel"),
    )(*flat)


def sum_cast(arrs, dtype, *, name):
    R, C = arrs[0].shape
    tr = _largest_tile(R, 512, 16)
    n = len(arrs)

    def body(*refs):
        acc = refs[0][...].astype(F32)
        for r in refs[1:n]:
            acc = acc + r[...].astype(F32)
        refs[n][...] = acc.astype(dtype)

    blk = pl.BlockSpec((tr, C), lambda i: (i, 0))
    return pl.pallas_call(
        body, name=name, grid=(R // tr,), in_specs=[blk] * n, out_specs=blk, out_shape=jax.ShapeDtypeStruct((R, C), dtype),
        compiler_params=_params("parallel"),
    )(*arrs)


GRAD_WIRE = BF16


def _block_diag_halves(w):
    z = jnp.zeros((HEAD_DIM, HEAD_DIM), w.dtype)
    half = lambda a, b: jnp.concatenate([jnp.concatenate([a, z], axis=1), jnp.concatenate([z, b], axis=1)], axis=0)
    return jnp.stack([half(w[0], w[1]), half(w[2], w[3])]).astype(BF16)


def _diag_blocks(d):
    h = HEAD_DIM
    return jnp.stack([d[0, :h, :h], d[0, h:, h:], d[1, :h, :h], d[1, h:, h:]])


def layer_fwd(x, mem2d, W, P, bias, tag):
    s = {}
    s["x"] = x
    h1 = rmsnorm_fwd(x, P["norm_mix_g"], name=f"{tag}_norm_mix")
    qkv = matmul(h1, W["qkv"], out_dtype=BF16, name=f"{tag}_qkv")
    aux = matmul(h1, W["aux"], name=f"{tag}_aux")
    o_sb = sbw_fwd(qkv, name=f"{tag}_sb_fwd")
    cumc = fox_prep(aux, P["bf"], name=f"{tag}_fox_prep")
    cumr = col_to_row(cumc)
    o_fox, o_fox32, lse_fox = foxw_fwd(qkv, cumc, cumr, name=f"{tag}_fox_fwd")
    dil = dilated_fwd(qkv, bias, tag)
    o_dil = dil_combine_fwd(dil, name=f"{tag}_dil_combine")
    o_lru, h_lru = lru_fwd(aux, P["lru_conv_w"], P["lru_conv_b"], P["wa"], P["lru_b_a"], P["wx"], P["lru_b_x"],
                           P["lru_lambda"], name=f"{tag}_lru_fwd")
    mixed = jnp.concatenate([o_sb, o_fox, o_dil, o_lru], axis=1)
    x1 = matmul(mixed, W["out"], residual=x, name=f"{tag}_out")
    hq = rmsnorm_fwd(x1, P["norm_cross_g"], name=f"{tag}_norm_cross")
    qc = matmul(hq, W["cq"], out_dtype=BF16, name=f"{tag}_cq")
    memn = rmsnorm_fwd(mem2d, P["norm_mem_g"], name=f"{tag}_norm_mem")
    kv = matmul(memn, W["ckv"], out_dtype=BF16, name=f"{tag}_ckv")
    oc = cross_fwd(qc, kv, name=f"{tag}_cross_fwd")
    x2 = matmul(oc, W["coT"], trans_b=True, residual=x1, name=f"{tag}_co")
    h2 = rmsnorm_fwd(x2, P["norm_ffn_g"], name=f"{tag}_norm_ffn")
    hu = matmul(h2, W["up_u"], trans_b=True, name=f"{tag}_up_u")
    hg = matmul(h2, W["up_g"], trans_b=True, name=f"{tag}_up_g")
    act = glu_fwd(hu, hg, P["wu"], P["wg"], P["bu"], P["bg"], name=f"{tag}_glu_fwd")
    x3 = matmul(act, W["down"], residual=x2, name=f"{tag}_down")
    s.update(h1=h1, qkv=qkv, aux=aux, cumc=cumc, cumr=cumr, lse_fox=lse_fox, o_fox32=o_fox32, dil=dil, h_lru=h_lru, mixed=mixed,
             x1=x1, hq=hq, qc=qc, memn=memn, kv=kv, oc=oc, x2=x2, h2=h2, hu=hu, hg=hg, act=act)
    return x3, s


def layer_bwd(dx3, mem2d, W, P, bias, s, tag):
    mm = functools.partial(matmul, out_dtype=GRAD_WIRE, trans_a=True)
    gW, gP = {}, {}
    dact = matmul(dx3, W["down"], trans_b=True, name=f"{tag}_d_act")
    gW["down"] = mm(s["act"], dx3, name=f"{tag}_g_down")
    dhu, dhg, dwu, dwg, dbu, dbg = glu_bwd(s["hu"], s["hg"], dact, P["wu"], P["wg"], P["bu"], P["bg"], name=f"{tag}_glu_bwd")
    gP["ffn_conv_w"] = jnp.concatenate([dwu, dwg], axis=1)
    gP["ffn_conv_b"] = jnp.concatenate([dbu, dbg], axis=1)
    dh2 = matmul(dhu, W["up_u"], name=f"{tag}_d_h2u")
    dh2 = matmul(dhg, W["up_g"], residual=dh2, name=f"{tag}_d_h2g")
    gW["up_u"] = mm(dhu, s["h2"], name=f"{tag}_g_up_u")
    gW["up_g"] = mm(dhg, s["h2"], name=f"{tag}_g_up_g")
    dx2, gP["norm_ffn_g"] = rmsnorm_bwd(s["x2"], P["norm_ffn_g"], dh2, dx3, name=f"{tag}_norm_ffn_bwd")
    doc = matmul(dx2, W["coT"], name=f"{tag}_d_oc")
    gW["coT"] = mm(dx2, s["oc"], name=f"{tag}_g_co")
    dqc, dkv = cross_bwd(s["qc"], s["kv"], doc, name=f"{tag}_cross_bwd")
    dhq = matmul(dqc, W["cq"], trans_b=True, name=f"{tag}_d_hq")
    gW["cq"] = mm(s["hq"], dqc, name=f"{tag}_g_cq")
    dmemn = matmul(dkv, W["ckv"], trans_b=True, name=f"{tag}_d_memn")
    gW["ckv"] = mm(s["memn"], dkv, name=f"{tag}_g_ckv")
    _, gP["norm_mem_g"] = rmsnorm_bwd(mem2d, P["norm_mem_g"], dmemn, None, name=f"{tag}_norm_mem_bwd")
    dx1, gP["norm_cross_g"] = rmsnorm_bwd(s["x1"], P["norm_cross_g"], dhq, dx2, name=f"{tag}_norm_cross_bwd")
    dmixed = matmul(dx1, W["out"], trans_b=True, name=f"{tag}_d_mixed")
    gW["out"] = mm(s["mixed"], dx1, name=f"{tag}_g_out")
    qkv, aux = s["qkv"], s["aux"]
    d_sb = sbw_bwd(qkv, dmixed, name=f"{tag}_sb_bwd")
    dfq, dfk, dfv, dcc, dcr = foxw_bwd(qkv, s["cumc"], s["cumr"], s["lse_fox"], s["o_fox32"], dmixed, name=f"{tag}_fox_bwd")
    dcum = sum_cast([dcc, row_to_col(dcr)], F32, name=f"{tag}_dcum")
    df, dbf = fox_prep_bwd(aux, P["bf"], dcum, name=f"{tag}_fox_prep_bwd")
    gP["b_forget"] = dbf[0, :N_HEADS]
    d_dil, ds_band = dilated_bwd(qkv, bias, s["dil"], dmixed, tag)
    dlx, dlg, dcw, dcb, dwa, dba, dwx, dbx, dlam = lru_bwd(
        aux, s["h_lru"], dmixed, P["lru_conv_w"], P["lru_conv_b"], P["wa"], P["lru_b_a"], P["wx"], P["lru_b_x"],
        P["lru_lambda"], name=f"{tag}_lru_bwd")
    gP.update(lru_conv_w=dcw, lru_conv_b=dcb, lru_w_a=_diag_blocks(dwa), lru_b_a=dba, lru_w_x=_diag_blocks(dwx),
              lru_b_x=dbx, lru_lambda=dlam)
    dqkv = assemble_dqkv(d_sb, [dfq, dfk, dfv], d_dil, name=f"{tag}_dqkv")
    daux = jnp.concatenate([dlx, dlg, df], axis=1)
    dh1 = matmul(dqkv, W["qkv"], trans_b=True, name=f"{tag}_d_h1a")
    dh1 = matmul(daux, W["aux"], trans_b=True, residual=dh1, name=f"{tag}_d_h1b")
    gW["qkv"] = mm(s["h1"], dqkv, name=f"{tag}_g_qkv")
    gW["aux"] = mm(s["h1"], daux, name=f"{tag}_g_aux")
    dx, gP["norm_mix_g"] = rmsnorm_bwd(s["x"], P["norm_mix_g"], dh1, dx1, name=f"{tag}_norm_mix_bwd")
    return dx, gW, gP, ds_band


def local_step(x, mem, target, weights_of, Ps, rel_bias, final_norm_g, grads_done=None):
    B = x.shape[0]
    x2d = x.reshape(B * SEQ, D_MODEL)
    mem2d = mem.reshape(B * N_MEM, D_MODEL)
    bias = relbias_expand(rel_bias, name="relbias_expand")
    saved, Ws = [], []
    h = x2d
    for l in range(DEPTH):
        Ws.append(weights_of(l, h))
        h, s = layer_fwd(h, mem2d, Ws[l], Ps[l], bias, f"l{l}")
        saved.append(s)
    loss, dh, d_final = loss_head(h, final_norm_g, target.reshape(B * SEQ, D_MODEL), name="loss_head")
    gWs, gPs, ds_bands = [None] * DEPTH, [None] * DEPTH, []
    for l in range(DEPTH - 1, -1, -1):
        dh, gWs[l], gPs[l], ds = layer_bwd(dh, mem2d, Ws[l], Ps[l], bias, saved[l], f"l{l}")
        if grads_done is not None:
            dh = grads_done(l, gWs[l], dh)
        ds_bands.append(ds)
    d_rel = relbias_reduce(sum_cast([d.reshape(-1, BAND) for d in ds_bands], F32, name="ds_band_sum").reshape(-1, BLOCK, BAND),
                           name="relbias_reduce")
    return loss, dh.reshape(B, SEQ, D_MODEL), gWs, gPs, d_rel, d_final


def small_params(p, l):
    row = lambda name: p[name][l].reshape(1, -1)
    ffn_w, ffn_b = p["ffn_conv_w"][l], row("ffn_conv_b")
    return dict(
        norm_mix_g=row("norm_mix_g"), norm_cross_g=row("norm_cross_g"), norm_mem_g=row("norm_mem_g"), norm_ffn_g=row("norm_ffn_g"),
        bf=jnp.pad(row("b_forget"), ((0, 0), (0, LANES - N_HEADS))),
        lru_conv_w=p["lru_conv_w"][l], lru_conv_b=row("lru_conv_b"), wa=_block_diag_halves(p["lru_w_a"][l]), lru_b_a=row("lru_b_a"),
        wx=_block_diag_halves(p["lru_w_x"][l]), lru_b_x=row("lru_b_x"), lru_lambda=row("lru_lambda"),
        wu=ffn_w[:, :D_FF], wg=ffn_w[:, D_FF:], bu=ffn_b[:, :D_FF], bg=ffn_b[:, D_FF:])


def canonical_weights(w_in, w_out, w_cq, w_ck, w_cv, w_co, w_up, w_down):
    sb_fox, fox_f, rest = w_in[:, :6 * GROUP_W], w_in[:, 6 * GROUP_W:6 * GROUP_W + N_HEADS], w_in[:, 6 * GROUP_W + N_HEADS:]
    dil, lru = rest[:, :3 * GROUP_W], rest[:, 3 * GROUP_W:]
    pad = jnp.zeros((w_in.shape[0], AUX_W - 2 * GROUP_W - N_HEADS), w_in.dtype)
    return dict(qkv=jnp.concatenate([sb_fox, dil], axis=1), aux=jnp.concatenate([lru, fox_f, pad], axis=1), out=w_out,
                cq=w_cq, ckv=jnp.concatenate([w_ck, w_cv], axis=1), coT=w_co.T, upT=w_up.T, down=w_down)


def native_grads(g):
    qkv, aux = g["qkv"], g["aux"]
    a, b = 6 * GROUP_W, 6 * GROUP_W + N_HEADS
    w_in = jnp.zeros((qkv.shape[0], b + 5 * GROUP_W), qkv.dtype)
    w_in = w_in.at[:, :a].set(qkv[:, :a]).at[:, a:b].set(aux[:, 2 * GROUP_W:2 * GROUP_W + N_HEADS])
    w_in = w_in.at[:, b:b + 3 * GROUP_W].set(qkv[:, a:]).at[:, b + 3 * GROUP_W:].set(aux[:, :2 * GROUP_W])
    return (w_in, g["out"], g["cq"], g["ckv"][:, :GROUP_W], g["ckv"][:, GROUP_W:], g["coT"].T, g["upT"].T, g["down"])


ANY = pl.BlockSpec(memory_space=pl.ANY)
VMEM_SPEC = pl.BlockSpec(memory_space=pltpu.VMEM)


def _place():
    x, y, c = lax.axis_index("x"), lax.axis_index("y"), lax.axis_index("c")
    other_chips = [(1 - x, y), (x, 1 - y), (1 - x, 1 - y)]
    return x, y, c, other_chips


def _gather_body(x_ref, out_ref, send_sems, recv_sems, local_sem):
    x, y, c, chips = _place()
    me, sibling = (x, y, c), (x, y, 1 - c)

    def slot(px, py, pc):
        return out_ref.at[4 * px + 2 * py + pc]

    def copy(k, block, to, src=None):
        return pltpu.make_async_remote_copy(
            src_ref=slot(*block) if src is None else src, dst_ref=slot(*block),
            send_sem=send_sems.at[k], recv_sem=recv_sems.at[k], device_id=to, device_id_type=MESH)

    if local_sem is not None:
        mine = pltpu.make_async_copy(x_ref, slot(*me), local_sem)
        mine.start()
    first = [copy(0, me, sibling, src=x_ref)]
    first += [copy(1 + j, me, (*chip, c), src=x_ref) for j, chip in enumerate(chips)]
    for cp in first:
        cp.start()
    passed = [copy(4 + j, (*chip, c), sibling) for j, chip in enumerate(chips)]
    for j, chip in enumerate(chips):
        copy(1 + j, (*chip, c), me).wait_recv()
        passed[j].start()
    copy(0, sibling, me).wait_recv()
    for j, chip in enumerate(chips):
        copy(4 + j, (*chip, 1 - c), me).wait_recv()
    for cp in first + passed:
        cp.wait_send()
    if local_sem is not None:
        mine.wait()


_GATHER_SEMS = [pltpu.SemaphoreType.DMA((7,)), pltpu.SemaphoreType.DMA((7,)), pltpu.SemaphoreType.DMA]


def allgather_hbm(shard, me, *, name):
    def body(x_ref, out_ref, done_ref, send_sems, recv_sems):
        _gather_body(x_ref, out_ref, send_sems, recv_sems, None)
        done_ref[...] = jnp.zeros_like(done_ref)

    others, done = pl.pallas_call(
        body, name=name, in_specs=[ANY], out_specs=[ANY, VMEM_SPEC],
        out_shape=[jax.ShapeDtypeStruct((N_DEV,) + shard.shape, shard.dtype), jax.ShapeDtypeStruct((8, LANES), F32)],
        scratch_shapes=_GATHER_SEMS[:2],
    )(shard)
    return lax.dynamic_update_slice(others, shard[None], (me, 0, 0)), done


def allgather_small(x, *, name, reduce=False):
    def body(x_ref, out_ref, *rest):
        _gather_body(x_ref, out_ref, *rest[-3:])
        if reduce:
            acc = out_ref[0]
            for d in range(1, N_DEV):
                acc = acc + out_ref[d]
            rest[0][...] = acc

    sd = jax.ShapeDtypeStruct
    return pl.pallas_call(
        body, name=name, in_specs=[VMEM_SPEC], out_specs=[VMEM_SPEC, VMEM_SPEC] if reduce else VMEM_SPEC,
        out_shape=[sd((N_DEV,) + x.shape, x.dtype), sd(x.shape, x.dtype)] if reduce else sd((N_DEV,) + x.shape, x.dtype),
        scratch_shapes=_GATHER_SEMS, compiler_params=pltpu.CompilerParams(vmem_limit_bytes=VMEM_LIMIT_V7X),
    )(x)


N_CHIPS = 4


def pair_exchange(g, *, name):
    _, R, C = g.shape

    def body(g_ref, recv_ref, send_sems, recv_sems):
        x, y, c, _ = _place()
        sibling = (x, y, 1 - c)
        remote = [pltpu.make_async_remote_copy(
            src_ref=g_ref.at[2 * q + (1 - c)], dst_ref=recv_ref.at[q], send_sem=send_sems.at[q], recv_sem=recv_sems.at[q],
            device_id=sibling, device_id_type=MESH) for q in range(N_CHIPS)]
        for cp in remote:
            cp.start()
        for cp in remote:
            cp.wait_recv()
        for cp in remote:
            cp.wait_send()

    return pl.pallas_call(
        body, name=name, in_specs=[ANY], out_specs=ANY, out_shape=jax.ShapeDtypeStruct((N_CHIPS, R, C), g.dtype),
        scratch_shapes=[pltpu.SemaphoreType.DMA((N_CHIPS,))] * 2,
    )(g)


def chip_exchange(s, *, name):
    _, R, C = s.shape

    def body(s_ref, o0, o1, o2, send_sems, recv_sems):
        x, y, c, chips = _place()
        outs = (o0, o1, o2)
        copies = [pltpu.make_async_remote_copy(
            src_ref=s_ref.at[2 * cx + cy], dst_ref=outs[j], send_sem=send_sems.at[j], recv_sem=recv_sems.at[j],
            device_id=(cx, cy, c), device_id_type=MESH) for j, (cx, cy) in enumerate(chips)]
        for cp in copies:
            cp.start()
        for cp in copies:
            cp.wait_recv()
        for cp in copies:
            cp.wait_send()

    sd = jax.ShapeDtypeStruct((R, C), s.dtype)
    return pl.pallas_call(
        body, name=name, in_specs=[ANY], out_specs=[ANY] * 3, out_shape=[sd] * 3,
        scratch_shapes=[pltpu.SemaphoreType.DMA((3,)), pltpu.SemaphoreType.DMA((3,))],
    )(s)


HBM_SPEC = pl.BlockSpec(memory_space=pltpu.HBM)
SEM_SPEC = pl.BlockSpec(memory_space=pltpu.SEMAPHORE)
N_PEERS = N_DEV - 1


def _peers():
    x, y, c = lax.axis_index("x"), lax.axis_index("y"), lax.axis_index("c")
    flip = lambda v, bit: 1 - v if bit else v
    out = []
    for k in range(1, N_DEV):
        px, py, pc = flip(x, (k >> 2) & 1), flip(y, (k >> 1) & 1), flip(c, k & 1)
        out.append(((px, py, pc), 4 * px + 2 * py + pc))
    return out, 4 * x + 2 * y + c


def _peer_copies(src_ref, land_ref, send_sems, recv_sems, scatter, landing):
    peers, me = _peers()
    return [pltpu.make_async_remote_copy(
        src_ref=src_ref.at[idx] if scatter else src_ref, dst_ref=land_ref.at[me if landing == "mine" else idx],
        send_sem=send_sems.at[k], recv_sem=recv_sems.at[k], device_id=peer, device_id_type=MESH)
        for k, (peer, idx) in enumerate(peers)]


def exchange_start(src, scatter, *, name):
    shape = (N_DEV,) + src.shape[-2:]

    def body(src_ref, land_ref, send_sems, recv_sems, src_thru, land_thru, token):
        for cp in _peer_copies(src_ref, land_ref, send_sems, recv_sems, scatter, "mine"):
            cp.start()
        token[...] = jnp.zeros_like(token)

    sems = pltpu.SemaphoreType.DMA((N_PEERS,))
    return pl.pallas_call(
        body, name=name,
        out_shape=(sems, sems, pltpu.HBM(src.shape, src.dtype), pltpu.HBM(shape, src.dtype), jax.ShapeDtypeStruct((8, LANES), F32)),
        in_specs=(HBM_SPEC, HBM_SPEC), out_specs=(SEM_SPEC, SEM_SPEC, HBM_SPEC, HBM_SPEC, VMEM_SPEC),
        input_output_aliases={0: 2, 1: 3},
        compiler_params=pltpu.CompilerParams(has_side_effects=pltpu.SideEffectType.DATAFLOW_SIDE_EFFECTING),
    )(pltpu.with_memory_space_constraint(src, pltpu.HBM), pltpu.with_memory_space_constraint(lax.empty(shape, src.dtype), pltpu.HBM))


def exchange_wait(started, after, scatter, *, name):
    send_sems, recv_sems, src_thru, land_thru, _ = started

    def body(src_ref, land_ref, send_sems, recv_sems, after_ref, src_dead, got_ref):
        for cp in _peer_copies(src_ref, land_ref, send_sems, recv_sems, scatter, "theirs"):
            cp.wait_send()
            cp.wait_recv()

    return pl.pallas_call(
        body, name=name, out_shape=(pltpu.HBM(src_thru.shape, src_thru.dtype), pltpu.HBM(land_thru.shape, land_thru.dtype)),
        in_specs=(HBM_SPEC, HBM_SPEC, SEM_SPEC, SEM_SPEC, ANY), out_specs=(HBM_SPEC, HBM_SPEC), input_output_aliases={0: 0, 1: 1},
        compiler_params=pltpu.CompilerParams(has_side_effects=pltpu.SideEffectType.DATAFLOW_SIDE_EFFECTING),
    )(src_thru, land_thru, send_sems, recv_sems, after)[1]


def sum_blocks(blocks, *, name):
    n, R, C = blocks.shape
    tr = _largest_tile(R, 512, 16)

    def body(b_ref, o_ref):
        d = pl.program_id(1)
        v = b_ref[...].astype(F32)

        @pl.when(d == 0)
        def _():
            o_ref[...] = v

        @pl.when(d > 0)
        def _():
            o_ref[...] += v

    return pl.pallas_call(
        body, name=name, grid=(R // tr, n),
        in_specs=[pl.BlockSpec((None, tr, C), lambda i, d: (d, i, 0))], out_specs=pl.BlockSpec((tr, C), lambda i, d: (i, 0)),
        out_shape=jax.ShapeDtypeStruct((R, C), F32), compiler_params=_params("parallel", "arbitrary"),
    )(blocks)


WEIGHTS = ("norm_mix_g", "w_in", "b_forget", "lru_conv_w", "lru_conv_b", "lru_w_a", "lru_b_a", "lru_w_x", "lru_b_x", "lru_lambda",
           "w_out", "norm_cross_g", "norm_mem_g", "w_cq", "w_ck", "w_cv", "w_co", "norm_ffn_g", "w_up", "ffn_conv_w", "ffn_conv_b",
           "w_down", "rel_bias", "final_norm_g")
LARGE = ("w_in", "w_out", "w_cq", "w_ck", "w_cv", "w_co", "w_up", "w_down")
COLUMN_SPLIT_SMALL = ("lru_conv_w", "ffn_conv_w")
PACK = (("qkv", 128, 2304), ("aux", 128, 640), ("out", 128, 1024), ("cq", 128, 256), ("ckv", 128, 512), ("coT", 128, 256),
        ("upT", 704, 1024), ("down", 352, 1024))
PACK_W = 1024


def _pack_rows(parts):
    return jnp.concatenate([p.reshape(-1, PACK_W) for p in parts], axis=0)


def _pad_rows(flat, mult=8 * LANES):
    n = flat.shape[0]
    return jnp.pad(flat, (0, (-n) % mult)).reshape(-1, LANES)


def kernel(x, mem, norm_mix_g, w_in, b_forget, lru_conv_w, lru_conv_b, lru_w_a, lru_b_a, lru_w_x, lru_b_x, lru_lambda, w_out, norm_cross_g, norm_mem_g, w_cq, w_ck, w_cv, w_co, norm_ffn_g, w_up, ffn_conv_w, ffn_conv_b, w_down, rel_bias, final_norm_g, loss_target, m_norm_mix_g, m_w_in, m_b_forget, m_lru_conv_w, m_lru_conv_b, m_lru_w_a, m_lru_b_a, m_lru_w_x, m_lru_b_x, m_lru_lambda, m_w_out, m_norm_cross_g, m_norm_mem_g, m_w_cq, m_w_ck, m_w_cv, m_w_co, m_norm_ffn_g, m_w_up, m_ffn_conv_w, m_ffn_conv_b, m_w_down, m_rel_bias, m_final_norm_g, v_norm_mix_g, v_w_in, v_b_forget, v_lru_conv_w, v_lru_conv_b, v_lru_w_a, v_lru_b_a, v_lru_w_x, v_lru_b_x, v_lru_lambda, v_w_out, v_norm_cross_g, v_norm_mem_g, v_w_cq, v_w_ck, v_w_cv, v_w_co, v_norm_ffn_g, v_w_up, v_ffn_conv_w, v_ffn_conv_b, v_w_down, v_rel_bias, v_final_norm_g):
    w = dict(norm_mix_g=norm_mix_g, w_in=w_in, b_forget=b_forget, lru_conv_w=lru_conv_w, lru_conv_b=lru_conv_b, lru_w_a=lru_w_a,
             lru_b_a=lru_b_a, lru_w_x=lru_w_x, lru_b_x=lru_b_x, lru_lambda=lru_lambda, w_out=w_out, norm_cross_g=norm_cross_g,
             norm_mem_g=norm_mem_g, w_cq=w_cq, w_ck=w_ck, w_cv=w_cv, w_co=w_co, norm_ffn_g=norm_ffn_g, w_up=w_up,
             ffn_conv_w=ffn_conv_w, ffn_conv_b=ffn_conv_b, w_down=w_down, rel_bias=rel_bias, final_norm_g=final_norm_g)
    m = dict(norm_mix_g=m_norm_mix_g, w_in=m_w_in, b_forget=m_b_forget, lru_conv_w=m_lru_conv_w, lru_conv_b=m_lru_conv_b,
             lru_w_a=m_lru_w_a, lru_b_a=m_lru_b_a, lru_w_x=m_lru_w_x, lru_b_x=m_lru_b_x, lru_lambda=m_lru_lambda, w_out=m_w_out,
             norm_cross_g=m_norm_cross_g, norm_mem_g=m_norm_mem_g, w_cq=m_w_cq, w_ck=m_w_ck, w_cv=m_w_cv, w_co=m_w_co,
             norm_ffn_g=m_norm_ffn_g, w_up=m_w_up, ffn_conv_w=m_ffn_conv_w, ffn_conv_b=m_ffn_conv_b, w_down=m_w_down,
             rel_bias=m_rel_bias, final_norm_g=m_final_norm_g)
    v = dict(norm_mix_g=v_norm_mix_g, w_in=v_w_in, b_forget=v_b_forget, lru_conv_w=v_lru_conv_w, lru_conv_b=v_lru_conv_b,
             lru_w_a=v_lru_w_a, lru_b_a=v_lru_b_a, lru_w_x=v_lru_w_x, lru_b_x=v_lru_b_x, lru_lambda=v_lru_lambda, w_out=v_w_out,
             norm_cross_g=v_norm_cross_g, norm_mem_g=v_norm_mem_g, w_cq=v_w_cq, w_ck=v_w_ck, w_cv=v_w_cv, w_co=v_w_co,
             norm_ffn_g=v_norm_ffn_g, w_up=v_w_up, ffn_conv_w=v_ffn_conv_w, ffn_conv_b=v_ffn_conv_b, w_down=v_w_down,
             rel_bias=v_rel_bias, final_norm_g=v_final_norm_g)
    me = 4 * lax.axis_index("x") + 2 * lax.axis_index("y") + lax.axis_index("c")

    conv_shard = jnp.concatenate([w[n].reshape(-1) for n in COLUMN_SPLIT_SMALL])
    conv_all = allgather_small(_pad_rows(conv_shard), name="gather_conv").reshape(N_DEV, -1)
    full = dict(w)
    off = 0
    for n in COLUMN_SPLIT_SMALL:
        d, k, c = w[n].shape
        blocks = conv_all[:, off:off + d * k * c].reshape(N_DEV, d, k, c)
        full[n] = blocks.transpose(1, 2, 0, 3).reshape(d, k, N_DEV * c)
        off += d * k * c

    def packed_shard(l):
        canon = canonical_weights(*[w[n][l] for n in LARGE])
        return _pack_rows([canon[k].astype(BF16) for k, _, _ in PACK])

    def unpack_weights(packed):
        W, row = {}, 0
        for k, r, c in PACK:
            n_rows = r * c // PACK_W
            W[k] = packed[:, row:row + n_rows].reshape(N_DEV * r, c)
            row += n_rows
        upT = W.pop("upT")
        W["up_u"], W["up_g"] = upT[:D_FF], upT[D_FF:]
        return W

    def own_block_in(landed, block):
        return lax.dynamic_update_slice(landed, block[None], (me, 0, 0))

    last = DEPTH - 1
    packed0, gathered = allgather_hbm(packed_shard(0), me, name="gather_weights")
    shard_last = packed_shard(last) + gathered[0, 0].astype(BF16)
    gather_last = exchange_start(shard_last, False, name="gather_last_start")
    started = gather_last[4][0, 0]

    def weights_of(l, h):
        if l == 0:
            return unpack_weights(packed0)
        assert l == last
        return unpack_weights(own_block_in(exchange_wait(gather_last, h, False, name="gather_last_wait"), shard_last))

    def packed_grads(gW):
        g = dict(gW)
        g["upT"] = jnp.concatenate([g.pop("up_u"), g.pop("up_g")], axis=0)
        return jnp.concatenate([g[k].reshape(N_DEV, r * c // PACK_W, PACK_W) for k, r, c in PACK], axis=1)

    in_flight = {}

    def grads_done(l, gW, dh):
        if l != last:
            return dh
        g_all = packed_grads(gW)
        in_flight["src"] = g_all
        in_flight["copies"] = exchange_start(g_all, True, name="grads_last_start")
        return dh + in_flight["copies"][4][0, 0]

    Ps = [small_params(full, l) for l in range(DEPTH)]
    Ps[0]["norm_mix_g"] = Ps[0]["norm_mix_g"] + started
    loss, grad_x, gWs, gPs, d_rel, d_final = local_step(x, mem, loss_target, weights_of, Ps, rel_bias,
                                                        final_norm_g.reshape(1, -1), grads_done)

    landed = exchange_wait(in_flight["copies"], grad_x, True, name="grads_last_wait")
    mine_last = lax.dynamic_index_in_dim(in_flight["src"], me, axis=0, keepdims=False)
    shard_sums = {last: sum_blocks(own_block_in(landed, mine_last), name="grads_last_sum")}

    g_all = packed_grads(gWs[0])
    rows = g_all.shape[1]
    got = pair_exchange(g_all, name="grads_pair_exchange")
    own = lax.dynamic_index_in_dim(g_all.reshape(N_CHIPS, 2, rows, PACK_W), lax.axis_index("c"), axis=1, keepdims=False)
    pair = sum_cast([own.reshape(-1, PACK_W), got.reshape(-1, PACK_W)], GRAD_WIRE, name="grads_pair_sum").reshape(N_CHIPS, rows, PACK_W)
    from_x, from_y, from_xy = chip_exchange(pair, name="grads_chip_exchange")
    mine = lax.dynamic_index_in_dim(pair, 2 * lax.axis_index("x") + lax.axis_index("y"), axis=0, keepdims=False)
    shard_sums[0] = sum_cast([mine, from_x, from_y, from_xy], F32, name="grads_chip_sum")

    grads = {}
    per_layer = []
    for l in range(DEPTH):
        g, row = {}, 0
        for k, r, c in PACK:
            n_rows = r * c // PACK_W
            g[k] = shard_sums[l][row:row + n_rows].reshape(r, c)
            row += n_rows
        per_layer.append(native_grads(g))
    for i, n in enumerate(LARGE):
        grads[n] = jnp.stack([per_layer[l][i] for l in range(DEPTH)])

    small_names = [n for n in WEIGHTS if n not in LARGE and n not in ("rel_bias", "final_norm_g")]
    pieces = [gPs[l][n].reshape(-1) for n in small_names for l in range(DEPTH)] + [d_rel.reshape(-1), d_final.reshape(-1), loss[0, :1]]
    sizes = [p.shape[0] for p in pieces]
    _, total = allgather_small(_pad_rows(jnp.concatenate(pieces)), name="allreduce_small", reduce=True)
    total = total.reshape(-1)
    off, it = 0, iter(sizes)
    for n in small_names:
        per = []
        for l in range(DEPTH):
            sz = next(it)
            per.append(total[off:off + sz])
            off += sz
        full_shape = (DEPTH,) + full[n].shape[1:]
        gfull = jnp.stack(per).reshape(full_shape)
        if n in COLUMN_SPLIT_SMALL:
            c = w[n].shape[-1]
            gfull = lax.dynamic_slice_in_dim(gfull, me * c, c, axis=gfull.ndim - 1)
        grads[n] = gfull
    grads["rel_bias"] = total[off:off + rel_bias.size].reshape(rel_bias.shape)
    off += rel_bias.size
    grads["final_norm_g"] = total[off:off + D_MODEL]
    off += D_MODEL
    loss_out = total[off]

    delta, new_m, new_v = {}, {}, {}
    for n in LARGE:
        shape = w[n].shape
        two_d = lambda a: a.reshape(-1, shape[-1])
        d_, m_, v_ = adamw(two_d(w[n]), two_d(grads[n]), two_d(m[n]), two_d(v[n]), name=f"adamw_{n}")
        delta[n], new_m[n], new_v[n] = d_.reshape(shape), m_.reshape(shape), v_.reshape(shape)
    small_all = [n for n in WEIGHTS if n not in LARGE]
    two_d = lambda a: a.reshape(-1, a.shape[-1])
    d_, m_, v_ = adamw_many(*[[two_d(src[n]) for n in small_all] for src in (w, grads, m, v)], name="adamw_small")
    for i, n in enumerate(small_all):
        delta[n], new_m[n], new_v[n] = (a[i].reshape(w[n].shape) for a in (d_, m_, v_))

    return (loss_out, grad_x, *[grads[n] for n in WEIGHTS], *[delta[n] for n in WEIGHTS], *[new_m[n] for n in WEIGHTS],
            *[new_v[n] for n in WEIGHTS])
```

```python
import functools
import math

import numpy as np
import jax
import jax.numpy as jnp
from jax import lax
from jax.experimental import pallas as pl
from jax.experimental.pallas import tpu as pltpu

F32 = jnp.float32
BF16 = jnp.bfloat16
MESH = pl.DeviceIdType.MESH

N_DEV = 8
D_MODEL = 1024
SEQ = 2048
DEPTH = 2
HEAD_DIM = 64
N_HEADS = 4
GROUP_W = N_HEADS * HEAD_DIM
D_FF = 2816
N_MEM = 256
NUM_BUCKETS = 32
MAX_DISTANCE = 2048
BLOCK = 128
DILATIONS = (1, 4, 16)
EPS = 1e-6
LRU_C = 8.0
Q_SCALE = HEAD_DIM ** -0.5
AUX_W = 640
LRU_HALF_W = 128
LRU_HALVES = GROUP_W // LRU_HALF_W
ADAM_LR, ADAM_B1, ADAM_B2, ADAM_EPS, ADAM_WD, ADAM_STEP = 0.001, 0.9, 0.999, 1e-08, 0.01, 10

VMEM_LIMIT_V7X = 48 * 1024 * 1024


def _params(*sem):
    return pltpu.CompilerParams(dimension_semantics=sem if sem else None, vmem_limit_bytes=VMEM_LIMIT_V7X)


def _pick(n, cands):
    for c in cands:
        if n % c == 0:
            return c
    return n


def _largest_tile(n, cap, align):
    best = None
    for t in range(align, min(n, cap) + 1, align):
        if n % t == 0:
            best = t
    return n if best is None else best


def matmul(a, b, *, name, trans_a=False, trans_b=False, out_dtype=F32, residual=None):
    (K, M) = a.shape if trans_a else a.shape[::-1]
    (N, Kb) = b.shape if trans_b else b.shape[::-1]
    assert K == Kb, (a.shape, b.shape)
    tm = _largest_tile(M, 1024 if trans_a else 512, 128)
    tn = _largest_tile(N, 1408, 128)
    tk = _largest_tile(K, 2816, 128)
    nk = K // tk
    a_spec = pl.BlockSpec((tk, tm), lambda i, j, k: (k, i)) if trans_a else pl.BlockSpec((tm, tk), lambda i, j, k: (i, k))
    b_spec = pl.BlockSpec((tn, tk), lambda i, j, k: (j, k)) if trans_b else pl.BlockSpec((tk, tn), lambda i, j, k: (k, j))
    o_spec = pl.BlockSpec((tm, tn), lambda i, j, k: (i, j))
    dims = (((0 if trans_a else 1,), (1 if trans_b else 0,)), ((), ()))
    has_res = residual is not None

    def body(*refs):
        a_ref, b_ref = refs[0], refs[1]
        r_ref = refs[2] if has_res else None
        part = lax.dot_general(a_ref[...].astype(BF16), b_ref[...].astype(BF16), dims, preferred_element_type=F32)
        if nk == 1:
            if has_res:
                part = part + r_ref[...].astype(F32)
            refs[-1][...] = part.astype(out_dtype)
            return
        o_ref, acc_ref = refs[-2], refs[-1]
        k = pl.program_id(2)

        @pl.when(k == 0)
        def _():
            acc_ref[...] = part

        @pl.when(k > 0)
        def _():
            acc_ref[...] += part

        @pl.when(k == nk - 1)
        def _():
            r = acc_ref[...]
            if has_res:
                r = r + r_ref[...].astype(F32)
            o_ref[...] = r.astype(out_dtype)

    ops = (a, b) + ((residual,) if has_res else ())
    return pl.pallas_call(
        body, name=name, grid=(M // tm, N // tn, nk),
        in_specs=[a_spec, b_spec] + ([o_spec] if has_res else []),
        out_specs=o_spec, out_shape=jax.ShapeDtypeStruct((M, N), out_dtype),
        scratch_shapes=[pltpu.VMEM((tm, tn), F32)] if nk > 1 else [],
        compiler_params=_params("parallel", "parallel", "arbitrary"),
    )(*ops)


def rmsnorm_fwd(x, g, *, name):
    R, D = x.shape
    tr = _pick(R, (512, 256))

    def body(x_ref, g_ref, o_ref):
        xv = x_ref[...]
        r = lax.rsqrt(jnp.mean(xv * xv, axis=-1, keepdims=True) + EPS)
        o_ref[...] = (xv * r * g_ref[...]).astype(BF16)

    return pl.pallas_call(
        body, name=name, grid=(R // tr,),
        in_specs=[pl.BlockSpec((tr, D), lambda i: (i, 0)), pl.BlockSpec((1, D), lambda i: (0, 0))],
        out_specs=pl.BlockSpec((tr, D), lambda i: (i, 0)), out_shape=jax.ShapeDtypeStruct((R, D), BF16),
        compiler_params=_params("parallel"),
    )(x, g)


def rmsnorm_bwd(x, g, dh, dres, *, name):
    R, D = x.shape
    tr = _pick(R, (512, 256))
    has_res = dres is not None

    def body(*refs):
        x_ref, g_ref, dh_ref = refs[:3]
        dx_ref, dg_ref = refs[-2], refs[-1]
        xv = x_ref[...]
        r = lax.rsqrt(jnp.mean(xv * xv, axis=-1, keepdims=True) + EPS)
        n = xv * r
        dhv = dh_ref[...]
        dn = dhv * g_ref[...]
        dx = r * (dn - n * jnp.mean(dn * n, axis=-1, keepdims=True))
        if has_res:
            dx = dx + refs[3][...]
        dx_ref[...] = dx
        part = jnp.sum(dhv * n, axis=0, keepdims=True)

        @pl.when(pl.program_id(0) == 0)
        def _():
            dg_ref[...] = part

        @pl.when(pl.program_id(0) > 0)
        def _():
            dg_ref[...] += part

    row = pl.BlockSpec((tr, D), lambda i: (i, 0))
    vec = pl.BlockSpec((1, D), lambda i: (0, 0))
    ops = (x, g, dh) + ((dres,) if has_res else ())
    return pl.pallas_call(
        body, name=name, grid=(R // tr,),
        in_specs=[row, vec, row] + ([row] if has_res else []),
        out_specs=[row, vec],
        out_shape=[jax.ShapeDtypeStruct((R, D), F32), jax.ShapeDtypeStruct((1, D), F32)],
        compiler_params=_params("arbitrary"),
    )(*ops)


_SQRT_HALF = 0.7071067811865476
_INV_SQRT_2PI = 0.3989422804014327


def _erf(x):
    ax = jnp.abs(x)
    t = 1.0 / (1.0 + 0.3275911 * ax)
    poly = t * (0.254829592 + t * (-0.284496736 + t * (1.421413741 + t * (-1.453152027 + t * 1.061405429))))
    y = 1.0 - poly * jnp.exp(-ax * ax)
    return jnp.where(x < 0, -y, y)


def _gelu_cdf(x):
    return 0.5 * (1.0 + _erf(x * _SQRT_HALF))


def _gelu_and_grad(x):
    cdf = _gelu_cdf(x)
    return x * cdf, cdf + x * _INV_SQRT_2PI * jnp.exp(-0.5 * x * x)


def _shift_down(main, halo, first, shifts):
    halo = jnp.where(first, 0.0, halo)
    ext = jnp.concatenate([halo, main], axis=0)
    return [pltpu.roll(ext, s, 0)[8:] for s in shifts]


def _conv3(main, halo, first, w, b):
    m1, m2 = _shift_down(main, halo, first, (1, 2))
    return ((b + w[0:1] * m2) + w[1:2] * m1) + w[2:3] * main, m1, m2


def glu_fwd(hu, hg, wu, wg, bu, bg, *, name):
    T, F = hu.shape
    tm, tf = 512, _largest_tile(F, 704, 128)
    hb = tm // 8
    blocks_per_example = SEQ // tm

    def body(hu_ref, hg_ref, hau_ref, hag_ref, wu_ref, wg_ref, bu_ref, bg_ref, o_ref):
        first = pl.program_id(0) % blocks_per_example == 0
        up, _, _ = _conv3(hu_ref[...], hau_ref[...], first, wu_ref[...], bu_ref[...])
        gate, _, _ = _conv3(hg_ref[...], hag_ref[...], first, wg_ref[...], bg_ref[...])
        o_ref[...] = (gate * _gelu_cdf(gate) * up).astype(BF16)

    main = pl.BlockSpec((tm, tf), lambda i, j: (i, j))
    halo = pl.BlockSpec((8, tf), lambda i, j: (jnp.maximum(i * hb - 1, 0), j))
    w3 = pl.BlockSpec((3, tf), lambda i, j: (0, j))
    b1 = pl.BlockSpec((1, tf), lambda i, j: (0, j))
    return pl.pallas_call(
        body, name=name, grid=(T // tm, F // tf),
        in_specs=[main, main, halo, halo, w3, w3, b1, b1],
        out_specs=main, out_shape=jax.ShapeDtypeStruct((T, F), BF16),
        compiler_params=_params("parallel", "parallel"),
    )(hu, hg, hu, hg, wu, wg, bu, bg)


def glu_bwd(hu, hg, dact, wu, wg, bu, bg, *, name):
    T, F = hu.shape
    tm, tf = 512, _largest_tile(F, 704, 128)
    hb = tm // 8
    blocks_per_example = SEQ // tm
    n_halo_blocks = T // 8
    n_ext = tm + 8

    def body(hu_ref, hg_ref, hau_ref, hag_ref, hnu_ref, hng_ref, da_ref, dan_ref, wu_ref, wg_ref, bu_ref, bg_ref,
             du_ref, dg_ref, dwu_ref, dwg_ref, dbu_ref, dbg_ref):
        i = pl.program_id(1)
        first = i % blocks_per_example == 0
        last = i % blocks_per_example == blocks_per_example - 1
        wu, wg = wu_ref[...], wg_ref[...]

        def conv_ext(main_ref, prev_ref, next_ref, w, b):
            ext = jnp.concatenate([jnp.where(first, 0.0, prev_ref[...]), main_ref[...], next_ref[...]], axis=0)
            x0, x1, x2 = ext[8:], pltpu.roll(ext, 1, 0)[8:], pltpu.roll(ext, 2, 0)[8:]
            return ((b + w[0:1] * x2) + w[1:2] * x1) + w[2:3] * x0, x0, x1, x2

        up, xu, u1, u2 = conv_ext(hu_ref, hau_ref, hnu_ref, wu, bu_ref[...])
        gate, xg, g1, g2 = conv_ext(hg_ref, hag_ref, hng_ref, wg, bg_ref[...])
        act, dact_dgate = _gelu_and_grad(gate)
        da = jnp.concatenate([da_ref[...], jnp.where(last, 0.0, dan_ref[...])], axis=0)
        dup = da * act
        dgate = da * up * dact_dgate

        def conv_t(d, w):
            return (w[2:3] * d[:tm] + w[1:2] * pltpu.roll(d, n_ext - 1, 0)[:tm] + w[0:1] * pltpu.roll(d, n_ext - 2, 0)[:tm]).astype(BF16)

        du_ref[...] = conv_t(dup, wu)
        dg_ref[...] = conv_t(dgate, wg)

        def sums(d, x0, x1, x2):
            s = lambda v: jnp.sum(v[:tm], axis=0, keepdims=True)
            return jnp.concatenate([s(d * x2), s(d * x1), s(d * x0)], axis=0), s(d)

        pwu, pbu = sums(dup, xu, u1, u2)
        pwg, pbg = sums(dgate, xg, g1, g2)

        @pl.when(i == 0)
        def _():
            dwu_ref[...] = pwu
            dwg_ref[...] = pwg
            dbu_ref[...] = pbu
            dbg_ref[...] = pbg

        @pl.when(i > 0)
        def _():
            dwu_ref[...] += pwu
            dwg_ref[...] += pwg
            dbu_ref[...] += pbu
            dbg_ref[...] += pbg

    main = pl.BlockSpec((tm, tf), lambda j, i: (i, j))
    before = pl.BlockSpec((8, tf), lambda j, i: (jnp.maximum(i * hb - 1, 0), j))
    after = pl.BlockSpec((8, tf), lambda j, i: (jnp.minimum((i + 1) * hb, n_halo_blocks - 1), j))
    w3 = pl.BlockSpec((3, tf), lambda j, i: (0, j))
    b1 = pl.BlockSpec((1, tf), lambda j, i: (0, j))
    sd = jax.ShapeDtypeStruct
    return pl.pallas_call(
        body, name=name, grid=(F // tf, T // tm),
        in_specs=[main, main, before, before, after, after, main, after, w3, w3, b1, b1],
        out_specs=[main, main, w3, w3, b1, b1],
        out_shape=[sd((T, F), BF16), sd((T, F), BF16), sd((3, F), F32), sd((3, F), F32), sd((1, F), F32), sd((1, F), F32)],
        compiler_params=_params("parallel", "arbitrary"),
    )(hu, hg, hu, hg, hu, hg, dact, dact, wu, wg, bu, bg)


def loss_head(x, g, target, *, name):
    T, D = x.shape
    tr = 256

    def body(x_ref, g_ref, t_ref, loss_ref, dx_ref, dg_ref):
        xv = x_ref[...]
        gv = g_ref[...]
        r = lax.rsqrt(jnp.mean(xv * xv, axis=-1, keepdims=True) + EPS)
        n = xv * r
        err = n * gv - t_ref[...]
        part_loss = jnp.zeros((1, 128), F32) + 0.5 * jnp.sum(jnp.mean(err * err, axis=-1, keepdims=True))
        dy = err * (1.0 / D)
        dn = dy * gv
        dx_ref[...] = r * (dn - n * jnp.mean(dn * n, axis=-1, keepdims=True))
        part_g = jnp.sum(dy * n, axis=0, keepdims=True)

        @pl.when(pl.program_id(0) == 0)
        def _():
            loss_ref[...] = part_loss
            dg_ref[...] = part_g

        @pl.when(pl.program_id(0) > 0)
        def _():
            loss_ref[...] += part_loss
            dg_ref[...] += part_g

    row = pl.BlockSpec((tr, D), lambda i: (i, 0))
    vec = pl.BlockSpec((1, D), lambda i: (0, 0))
    sd = jax.ShapeDtypeStruct
    return pl.pallas_call(
        body, name=name, grid=(T // tr,),
        in_specs=[row, vec, row],
        out_specs=[pl.BlockSpec((1, 128), lambda i: (0, 0)), row, vec],
        out_shape=[sd((1, 128), F32), sd((T, D), F32), sd((1, D), F32)],
        compiler_params=_params("arbitrary"),
    )(x, g, target)


def adamw(w, g, m, v, *, name):
    R, C = w.shape
    tr = _pick(R, (256, 128, 64, 32, 16, 8))

    def body(w_ref, g_ref, m_ref, v_ref, d_ref, nm_ref, nv_ref):
        gv = g_ref[...]
        mn = ADAM_B1 * m_ref[...] + (1.0 - ADAM_B1) * gv
        vn = ADAM_B2 * v_ref[...] + (1.0 - ADAM_B2) * (gv * gv)
        m_hat = mn / (1.0 - ADAM_B1 ** ADAM_STEP)
        v_hat = vn / (1.0 - ADAM_B2 ** ADAM_STEP)
        d_ref[...] = -ADAM_LR * (m_hat / (jnp.sqrt(v_hat) + ADAM_EPS) + ADAM_WD * w_ref[...])
        nm_ref[...] = mn
        nv_ref[...] = vn

    blk = pl.BlockSpec((tr, C), lambda i: (i, 0))
    sd = jax.ShapeDtypeStruct((R, C), F32)
    return pl.pallas_call(
        body, name=name, grid=(R // tr,), in_specs=[blk] * 4, out_specs=[blk] * 3, out_shape=[sd] * 3,
        compiler_params=_params("parallel"),
    )(w, g, m, v)


def adamw_many(ws, gs, ms, vs, *, name):
    n = len(ws)

    def body(*refs):
        ins, outs = refs[:4 * n], refs[4 * n:]
        for i in range(n):
            w_ref, g_ref, m_ref, v_ref = ins[i], ins[n + i], ins[2 * n + i], ins[3 * n + i]
            gv = g_ref[...]
            mn = ADAM_B1 * m_ref[...] + (1.0 - ADAM_B1) * gv
            vn = ADAM_B2 * v_ref[...] + (1.0 - ADAM_B2) * (gv * gv)
            m_hat = mn / (1.0 - ADAM_B1 ** ADAM_STEP)
            v_hat = vn / (1.0 - ADAM_B2 ** ADAM_STEP)
            outs[i][...] = -ADAM_LR * (m_hat / (jnp.sqrt(v_hat) + ADAM_EPS) + ADAM_WD * w_ref[...])
            outs[n + i][...] = mn
            outs[2 * n + i][...] = vn

    vm = pl.BlockSpec(memory_space=pltpu.VMEM)
    shapes = [jax.ShapeDtypeStruct(w.shape, F32) for w in ws]
    res = pl.pallas_call(
        body, name=name, in_specs=[vm] * (4 * n), out_specs=[vm] * (3 * n), out_shape=shapes * 3, compiler_params=_params(),
    )(*ws, *gs, *ms, *vs)
    return res[:n], res[n:2 * n], res[2 * n:]


def _softplus(x):
    return jnp.maximum(x, 0.0) + jnp.log(1.0 + jnp.exp(-jnp.abs(x)))


def _lru_gates(x, cw, cb, wa, ba, wx, bx, lam):
    S = x.shape[0]
    row = lax.broadcasted_iota(jnp.int32, (S, 1), 0)

    def back(s):
        return jnp.where(row >= s, pltpu.roll(x, s, 0), 0.0)

    xc = (((cb + cw[0:1] * back(3)) + cw[1:2] * back(2)) + cw[2:3] * back(1)) + cw[3:4] * x
    xb = xc.astype(BF16)
    r = jax.nn.sigmoid(jnp.dot(xb, wa, preferred_element_type=F32) + ba)
    ig = jax.nn.sigmoid(jnp.dot(xb, wx, preferred_element_type=F32) + bx)
    sp = _softplus(-lam)
    la = -LRU_C * r * sp
    a = jnp.exp(la)
    y = 2.0 * la
    one_minus_a2 = jnp.where(y > -0.05, -y * (1.0 + y * (0.5 + y * (1.0 / 6.0 + y * (1.0 / 24.0)))), 1.0 - jnp.exp(y))
    mm = jnp.sqrt(one_minus_a2)
    return xc, xb, r, ig, sp, a, mm


def lru_fwd(aux, cw, cb, wa, ba, wx, bx, lam, *, name):
    T = aux.shape[0]
    S, C = SEQ, LRU_HALF_W

    def body(x_ref, g_ref, cw_ref, cb_ref, wa_ref, ba_ref, wx_ref, bx_ref, lam_ref, o_ref, h_ref, a_s, u_s):
        xc, _, r, ig, sp, a, mm = _lru_gates(x_ref[...], cw_ref[...], cb_ref[...], wa_ref[...], ba_ref[...],
                                             wx_ref[...], bx_ref[...], lam_ref[...])
        a_s[...] = a
        u_s[...] = mm * (ig * xc)

        def group(i, h):
            base = pl.multiple_of(i * 8, 8)
            a8 = a_s[pl.ds(base, 8), :]
            u8 = u_s[pl.ds(base, 8), :]
            for rr in range(8):
                h = a8[rr:rr + 1] * h + u8[rr:rr + 1]
                h_ref[pl.ds(base + rr, 1), :] = h
            return h

        lax.fori_loop(0, S // 8, group, jnp.zeros((1, C), F32))
        gate = g_ref[...]
        o_ref[...] = (h_ref[...] * (gate * _gelu_cdf(gate))).astype(BF16)

    blk = lambda col: pl.BlockSpec((S, C), lambda c, b: (b, col + c))
    par = lambda rows: pl.BlockSpec((rows, C), lambda c, b: (0, c))
    sq = pl.BlockSpec((None, C, C), lambda c, b: (c, 0, 0))
    sd = jax.ShapeDtypeStruct
    W = LRU_HALVES * C
    return pl.pallas_call(
        body, name=name, grid=(LRU_HALVES, T // S),
        in_specs=[blk(0), blk(LRU_HALVES), par(4), par(1), sq, par(1), sq, par(1), par(1)],
        out_specs=[blk(0), blk(0)], out_shape=[sd((T, W), BF16), sd((T, W), F32)],
        scratch_shapes=[pltpu.VMEM((S, C), F32), pltpu.VMEM((S, C), F32)],
        compiler_params=_params("parallel", "parallel"),
    )(aux, aux, cw, cb, wa, ba, wx, bx, lam)


def lru_bwd(aux, h, dmixed, cw, cb, wa, ba, wx, bx, lam, *, name):
    T = aux.shape[0]
    S, C = SEQ, LRU_HALF_W

    def body(x_ref, g_ref, h_ref, do_ref, cw_ref, cb_ref, wa_ref, ba_ref, wx_ref, bx_ref, lam_ref,
             dx_ref, dgate_ref, dcw_ref, dcb_ref, dwa_ref, dba_ref, dwx_ref, dbx_ref, dlam_ref, a_s, d_s):
        x = x_ref[...]
        cw = cw_ref[...]
        lam = lam_ref[...]
        xc, xb, r, ig, sp, a, mm = _lru_gates(x, cw, cb_ref[...], wa_ref[...], ba_ref[...], wx_ref[...], bx_ref[...], lam)
        gate = g_ref[...]
        gl, dgl = _gelu_and_grad(gate)
        dout = do_ref[...]
        hv = h_ref[...]
        dgate_ref[...] = dout * hv * dgl
        a_s[...] = a
        d_s[...] = dout * gl

        def group(i, c):
            base = pl.multiple_of((S // 8 - 1 - i) * 8, 8)
            a8 = a_s[pl.ds(base, 8), :]
            d8 = d_s[pl.ds(base, 8), :]
            for rr in range(7, -1, -1):
                d = d8[rr:rr + 1] + c
                d_s[pl.ds(base + rr, 1), :] = d
                c = a8[rr:rr + 1] * d
            return c

        lax.fori_loop(0, S // 8, group, jnp.zeros((1, C), F32))
        row = lax.broadcasted_iota(jnp.int32, (S, 1), 0)
        dht = d_s[...]
        h_prev = jnp.where(row >= 1, pltpu.roll(hv, 1, 0), 0.0)
        da = dht * h_prev
        gx = ig * xc
        dmm = dht * gx
        dig = dht * mm * xc
        dxc = dht * mm * ig
        dla = da * a - dmm * (a * a) / mm
        dr = dla * (-LRU_C * sp)
        dsp = jnp.sum(dla * (-LRU_C * r), axis=0, keepdims=True)
        dlam = dsp * (-jax.nn.sigmoid(-lam))
        dpa = dr * r * (1.0 - r)
        dpx = dig * ig * (1.0 - ig)
        dpa_b, dpx_b = dpa.astype(BF16), dpx.astype(BF16)
        nt = (((1,), (1,)), ((), ()))
        tn = (((0,), (0,)), ((), ()))
        dxc = dxc + lax.dot_general(dpa_b, wa_ref[...], nt, preferred_element_type=F32) \
                  + lax.dot_general(dpx_b, wx_ref[...], nt, preferred_element_type=F32)
        dwa = lax.dot_general(xb, dpa_b, tn, preferred_element_type=F32)
        dwx = lax.dot_general(xb, dpx_b, tn, preferred_element_type=F32)

        def fwd(v, s):
            return jnp.where(row < S - s, pltpu.roll(v, S - s, 0), 0.0)

        def back(v, s):
            return jnp.where(row >= s, pltpu.roll(v, s, 0), 0.0)

        dx_ref[...] = cw[3:4] * dxc + cw[2:3] * fwd(dxc, 1) + cw[1:2] * fwd(dxc, 2) + cw[0:1] * fwd(dxc, 3)
        s0 = lambda v: jnp.sum(v, axis=0, keepdims=True)
        dcw = jnp.concatenate([s0(dxc * back(x, 3)), s0(dxc * back(x, 2)), s0(dxc * back(x, 1)), s0(dxc * x)], axis=0)
        parts = ((dcw_ref, dcw), (dcb_ref, s0(dxc)), (dwa_ref, dwa), (dba_ref, s0(dpa)), (dwx_ref, dwx),
                 (dbx_ref, s0(dpx)), (dlam_ref, dlam))

        @pl.when(pl.program_id(1) == 0)
        def _():
            for ref, val in parts:
                ref[...] = val

        @pl.when(pl.program_id(1) > 0)
        def _():
            for ref, val in parts:
                ref[...] += val

    blk = lambda col: pl.BlockSpec((S, C), lambda c, b: (b, col + c))
    par = lambda rows: pl.BlockSpec((rows, C), lambda c, b: (0, c))
    sq = pl.BlockSpec((None, C, C), lambda c, b: (c, 0, 0))
    sd = jax.ShapeDtypeStruct
    W = LRU_HALVES * C
    vec = sd((1, W), F32)
    return pl.pallas_call(
        body, name=name, grid=(LRU_HALVES, T // S),
        in_specs=[blk(0), blk(LRU_HALVES), blk(0), blk(3 * LRU_HALVES), par(4), par(1), sq, par(1), sq, par(1), par(1)],
        out_specs=[blk(0), blk(0), par(4), par(1), sq, par(1), sq, par(1), par(1)],
        out_shape=[sd((T, W), F32), sd((T, W), F32), sd((4, W), F32), vec, sd((LRU_HALVES, C, C), F32), vec,
                   sd((LRU_HALVES, C, C), F32), vec, vec],
        scratch_shapes=[pltpu.VMEM((S, C), F32), pltpu.VMEM((S, C), F32)],
        compiler_params=_params("parallel", "arbitrary"),
    )(aux, aux, h, dmixed, cw, cb, wa, ba, wx, bx, lam)


_NT = (((1,), (1,)), ((), ()))
_TN = (((0,), (0,)), ((), ()))


def _dot(a, b, dims=None):
    if dims is None:
        return jnp.dot(a, b, preferred_element_type=F32)
    return lax.dot_general(a, b, dims, preferred_element_type=F32)


def _hs(h):
    return slice(h * HEAD_DIM, (h + 1) * HEAD_DIM)


def cross_fwd(q, kv, *, name):
    T = q.shape[0]
    tq = 512

    def body(q_ref, kv_ref, o_ref):
        for h in range(N_HEADS):
            qh = q_ref[:, _hs(h)] * Q_SCALE
            k = kv_ref[:, _hs(h)]
            v = kv_ref[:, GROUP_W + h * HEAD_DIM:GROUP_W + (h + 1) * HEAD_DIM]
            s = _dot(qh, k, _NT)
            p = jnp.exp(s - jnp.max(s, axis=-1, keepdims=True))
            p = p / jnp.sum(p, axis=-1, keepdims=True)
            o_ref[:, _hs(h)] = _dot(p.astype(BF16), v).astype(BF16)

    per = SEQ // tq
    return pl.pallas_call(
        body, name=name, grid=(T // tq,),
        in_specs=[pl.BlockSpec((tq, GROUP_W), lambda i: (i, 0)), pl.BlockSpec((N_MEM, 2 * GROUP_W), lambda i: (i // per, 0))],
        out_specs=pl.BlockSpec((tq, GROUP_W), lambda i: (i, 0)), out_shape=jax.ShapeDtypeStruct((T, GROUP_W), BF16),
        compiler_params=_params("parallel"),
    )(q, kv)


def cross_bwd(q, kv, do, *, name):
    T = q.shape[0]
    tq = 512
    per = SEQ // tq

    def body(q_ref, kv_ref, do_ref, dq_ref, dkv_ref):
        first = pl.program_id(0) % per == 0
        for h in range(N_HEADS):
            vs = slice(GROUP_W + h * HEAD_DIM, GROUP_W + (h + 1) * HEAD_DIM)
            qh = q_ref[:, _hs(h)] * Q_SCALE
            k = kv_ref[:, _hs(h)]
            v = kv_ref[:, vs]
            doh = do_ref[:, _hs(h)].astype(BF16)
            s = _dot(qh, k, _NT)
            p = jnp.exp(s - jnp.max(s, axis=-1, keepdims=True))
            p = p / jnp.sum(p, axis=-1, keepdims=True)
            dp = _dot(doh, v, _NT)
            ds = (p * (dp - jnp.sum(p * dp, axis=-1, keepdims=True))).astype(BF16)
            dq_ref[:, _hs(h)] = (_dot(ds, k) * Q_SCALE).astype(BF16)
            dk = _dot(ds, qh, _TN)
            dv = _dot(p.astype(BF16), doh, _TN)

            @pl.when(first)
            def _():
                dkv_ref[:, _hs(h)] = dk
                dkv_ref[:, vs] = dv

            @pl.when(jnp.logical_not(first))
            def _():
                dkv_ref[:, _hs(h)] += dk
                dkv_ref[:, vs] += dv

    qb = pl.BlockSpec((tq, GROUP_W), lambda i: (i, 0))
    kvb = pl.BlockSpec((N_MEM, 2 * GROUP_W), lambda i: (i // per, 0))
    sd = jax.ShapeDtypeStruct
    return pl.pallas_call(
        body, name=name, grid=(T // tq,),
        in_specs=[qb, kvb, qb], out_specs=[qb, kvb],
        out_shape=[sd((T, GROUP_W), BF16), sd(kv.shape, F32)],
        compiler_params=_params("arbitrary"),
    )(q, kv, do)


NB = SEQ // BLOCK
NEG = -1e30


def _split_dot(x, tri):
    hi = x.astype(BF16)
    lo = (x - hi.astype(F32)).astype(BF16)
    return _dot(hi, tri) + _dot(lo, tri)


def _blk(i):
    return pl.ds(pl.multiple_of(i * BLOCK, BLOCK), BLOCK)


def _iotas():
    row = lax.broadcasted_iota(jnp.int32, (BLOCK, BLOCK), 0)
    col = lax.broadcasted_iota(jnp.int32, (BLOCK, BLOCK), 1)
    return row, col


def _sb_scores(q, k, mask, later, csum, want_sigmoid=False):
    z = _dot(q, k, _NT)
    lk = -_softplus(z)
    if mask is not None:
        lk = jnp.where(mask, lk, 0.0)
    lka = _split_dot(lk, later) + csum
    att = jnp.exp(z + lk + lka)
    sg = jnp.exp(z + lk) if want_sigmoid else None
    if mask is not None:
        att = jnp.where(mask, att, 0.0)
        sg = jnp.where(mask, sg, 0.0) if want_sigmoid else None
    return att, sg, lk


def _rowsum(v):
    return jnp.sum(v, axis=1, keepdims=True)


HEADS = tuple(range(N_HEADS))


def _qkv_specs(first_col):
    return [pl.BlockSpec((SEQ, GROUP_W), lambda b, c=first_col + j: (b, c)) for j in range(3)]


LANES = 128
CUM_BLK = 256


def col_to_row(c):
    b = c.shape[0] // SEQ
    return c.reshape(b, SEQ, LANES)[:, :, :8].transpose(0, 2, 1).reshape(b * 8, SEQ)


def row_to_col(r):
    b = r.shape[0] // 8
    c = r.reshape(b, 8, SEQ).transpose(0, 2, 1)
    return jnp.pad(c, ((0, 0), (0, 0), (0, LANES - 8))).reshape(b * SEQ, LANES)


def fox_prep(aux, bf, *, name):
    T = aux.shape[0]

    def body(f_ref, b_ref, o_ref):
        row = lax.broadcasted_iota(jnp.int32, (CUM_BLK, CUM_BLK), 0)
        col = lax.broadcasted_iota(jnp.int32, (CUM_BLK, CUM_BLK), 1)
        upto = (col <= row).astype(BF16)
        carry = jnp.zeros((1, LANES), F32)
        for n in range(SEQ // CUM_BLK):
            rows = slice(n * CUM_BLK, (n + 1) * CUM_BLK)
            logf = -_softplus(-(f_ref[rows, :] + b_ref[...]))
            hi = logf.astype(BF16)
            lo = (logf - hi.astype(F32)).astype(BF16)
            cum = _dot(upto, hi) + _dot(upto, lo) + carry
            o_ref[rows, :] = cum
            carry = cum[CUM_BLK - 1:CUM_BLK]

    return pl.pallas_call(
        body, name=name, grid=(T // SEQ,),
        in_specs=[pl.BlockSpec((SEQ, LANES), lambda b: (b, 4)), pl.BlockSpec((1, LANES), lambda b: (0, 0))],
        out_specs=pl.BlockSpec((SEQ, LANES), lambda b: (b, 0)), out_shape=jax.ShapeDtypeStruct((T, LANES), F32),
        compiler_params=_params("parallel"),
    )(aux, bf)


def fox_prep_bwd(aux, bf, dcum, *, name):
    T = aux.shape[0]

    def body(f_ref, b_ref, d_ref, df_ref, db_ref):
        row = lax.broadcasted_iota(jnp.int32, (CUM_BLK, CUM_BLK), 0)
        col = lax.broadcasted_iota(jnp.int32, (CUM_BLK, CUM_BLK), 1)
        onward = (col >= row).astype(BF16)
        carry = jnp.zeros((1, LANES), F32)
        tot = jnp.zeros((1, LANES), F32)
        for n in range(SEQ // CUM_BLK - 1, -1, -1):
            rows = slice(n * CUM_BLK, (n + 1) * CUM_BLK)
            d = d_ref[rows, :]
            hi = d.astype(BF16)
            lo = (d - hi.astype(F32)).astype(BF16)
            dlogf = _dot(onward, hi) + _dot(onward, lo) + carry
            carry = dlogf[0:1]
            df = dlogf * jax.nn.sigmoid(-(f_ref[rows, :] + b_ref[...]))
            df_ref[rows, :] = df
            tot = tot + jnp.sum(df, axis=0, keepdims=True)

        @pl.when(pl.program_id(0) == 0)
        def _():
            db_ref[...] = tot

        @pl.when(pl.program_id(0) > 0)
        def _():
            db_ref[...] += tot

    blk = pl.BlockSpec((SEQ, LANES), lambda b: (b, 0))
    vec = pl.BlockSpec((1, LANES), lambda b: (0, 0))
    sd = jax.ShapeDtypeStruct
    return pl.pallas_call(
        body, name=name, grid=(T // SEQ,),
        in_specs=[pl.BlockSpec((SEQ, LANES), lambda b: (b, 4)), vec, blk],
        out_specs=[blk, vec], out_shape=[sd((T, LANES), F32), sd((1, LANES), F32)],
        compiler_params=_params("arbitrary"),
    )(aux, bf, dcum)


def _fox_logits(q, k, cq, ck, mask):
    z = _dot(q, k, _NT) + cq - ck
    return z if mask is None else jnp.where(mask, z, NEG)


def fox_fwd(qkv, cumc, cumr, *, name):
    T = qkv.shape[0]

    def body(q_ref, k_ref, v_ref, cc_ref, cr_ref, o_ref, lse_ref, z_s):
        row, col = _iotas()
        causal = col <= row
        lse_ref[...] = jnp.zeros_like(lse_ref)

        def qblock(i, _):
            qs = [q_ref[_blk(i), _hs(h)] * Q_SCALE for h in HEADS]
            cqs = [cc_ref[_blk(i), h:h + 1] for h in HEADS]

            def logits(j, mask, ms):
                out = []
                for h in HEADS:
                    z = _fox_logits(qs[h], k_ref[_blk(j), _hs(h)], cqs[h], cr_ref[h:h + 1, _blk(j)], mask)
                    z_s[h, j] = z
                    out.append(jnp.maximum(ms[h], jnp.max(z, axis=1, keepdims=True)))
                return tuple(out)

            ms = logits(i, causal, (jnp.full((BLOCK, 1), NEG, F32),) * N_HEADS)
            ms = lax.fori_loop(0, i, lambda j, c: logits(j, None, c), ms)

            def values(j, carry):
                out = []
                for h in HEADS:
                    acc, l = carry[h]
                    p = jnp.exp(z_s[h, j] - ms[h])
                    out.append((acc + _dot(p.astype(BF16), v_ref[_blk(j), _hs(h)]), l + _rowsum(p)))
                return tuple(out)

            zero = (jnp.zeros((BLOCK, HEAD_DIM), F32), jnp.zeros((BLOCK, 1), F32))
            res = lax.fori_loop(0, i + 1, values, (zero,) * N_HEADS)
            for h in HEADS:
                acc, l = res[h]
                o_ref[_blk(i), _hs(h)] = (acc / l).astype(BF16)
                lse_ref[_blk(i), h:h + 1] = ms[h] + jnp.log(l)
            return 0

        lax.fori_loop(0, NB, qblock, 0)

    out = pl.BlockSpec((SEQ, GROUP_W), lambda b: (b, 0))
    colb = pl.BlockSpec((SEQ, LANES), lambda b: (b, 0))
    sd = jax.ShapeDtypeStruct
    return pl.pallas_call(
        body, name=name, grid=(T // SEQ,),
        in_specs=_qkv_specs(3) + [colb, pl.BlockSpec((8, SEQ), lambda b: (b, 0))],
        out_specs=[out, colb], out_shape=[sd((T, GROUP_W), BF16), sd((T, LANES), F32)],
        scratch_shapes=[pltpu.VMEM((N_HEADS, NB, BLOCK, BLOCK), F32)],
        compiler_params=_params("parallel"),
    )(qkv, qkv, qkv, cumc, cumr)


def fox_bwd(qkv, cumc, cumr, lse, dmixed, *, name):
    T = qkv.shape[0]

    def body(q_ref, k_ref, v_ref, cc_ref, cr_ref, lse_ref, do_ref, dq_ref, dk_ref, dv_ref, dcc_ref, dcr_ref, p_s, dp_s):
        row, col = _iotas()
        causal = col <= row
        dk_ref[...] = jnp.zeros_like(dk_ref)
        dv_ref[...] = jnp.zeros_like(dv_ref)
        dcc_ref[...] = jnp.zeros_like(dcc_ref)
        dcr_ref[...] = jnp.zeros_like(dcr_ref)

        def qblock(i, _):
            qs = [q_ref[_blk(i), _hs(h)] * Q_SCALE for h in HEADS]
            dos = [do_ref[_blk(i), _hs(h)].astype(BF16) for h in HEADS]
            cqs = [cc_ref[_blk(i), h:h + 1] for h in HEADS]
            lses = [lse_ref[_blk(i), h:h + 1] for h in HEADS]

            def probs(j, mask, deltas):
                out = []
                for h in HEADS:
                    z = _fox_logits(qs[h], k_ref[_blk(j), _hs(h)], cqs[h], cr_ref[h:h + 1, _blk(j)], mask)
                    p = jnp.exp(z - lses[h])
                    dp = _dot(dos[h], v_ref[_blk(j), _hs(h)], _NT)
                    p_s[h, j] = p
                    dp_s[h, j] = dp
                    out.append(deltas[h] + _rowsum(p * dp))
                return tuple(out)

            deltas = probs(i, causal, (jnp.zeros((BLOCK, 1), F32),) * N_HEADS)
            deltas = lax.fori_loop(0, i, lambda j, c: probs(j, None, c), deltas)

            def kblock(j, carry):
                out = []
                for h in HEADS:
                    dq, dcq = carry[h]
                    p = p_s[h, j]
                    ds = p * (dp_s[h, j] - deltas[h])
                    dsb = ds.astype(BF16)
                    dk_ref[_blk(j), _hs(h)] += _dot(dsb, qs[h], _TN)
                    dv_ref[_blk(j), _hs(h)] += _dot(p.astype(BF16), dos[h], _TN)
                    dcr_ref[h:h + 1, _blk(j)] -= jnp.sum(ds, axis=0, keepdims=True)
                    out.append((dq + _dot(dsb, k_ref[_blk(j), _hs(h)]), dcq + _rowsum(ds)))
                return tuple(out)

            zero = (jnp.zeros((BLOCK, HEAD_DIM), F32), jnp.zeros((BLOCK, 1), F32))
            res = lax.fori_loop(0, i + 1, kblock, (zero,) * N_HEADS)
            for h in HEADS:
                dq_ref[_blk(i), _hs(h)] = res[h][0] * Q_SCALE
                dcc_ref[_blk(i), h:h + 1] = res[h][1]
            return 0

        lax.fori_loop(0, NB, qblock, 0)

    out = pl.BlockSpec((SEQ, GROUP_W), lambda b: (b, 0))
    colb = pl.BlockSpec((SEQ, LANES), lambda b: (b, 0))
    rowb = pl.BlockSpec((8, SEQ), lambda b: (b, 0))
    sd = jax.ShapeDtypeStruct
    big = sd((T, GROUP_W), F32)
    return pl.pallas_call(
        body, name=name, grid=(T // SEQ,),
        in_specs=_qkv_specs(3) + [colb, rowb, colb, pl.BlockSpec((SEQ, GROUP_W), lambda b: (b, 1))],
        out_specs=[out, out, out, colb, rowb],
        out_shape=[big, big, big, sd((T, LANES), F32), sd((T // SEQ * 8, SEQ), F32)],
        scratch_shapes=[pltpu.VMEM((N_HEADS, NB, BLOCK, BLOCK), F32), pltpu.VMEM((N_HEADS, NB, BLOCK, BLOCK), F32)],
        compiler_params=_params("parallel"),
    )(qkv, qkv, qkv, cumc, cumr, lse, dmixed)


CHUNK = 256
WIDE = N_HEADS * CHUNK
NCH = SEQ // CHUNK


def _seg(h):
    return slice(h * CHUNK, (h + 1) * CHUNK)


def _chunk_rows(c):
    return pl.ds(pl.multiple_of(c * CHUNK, CHUNK), CHUNK)


def _wide_consts():
    r = lax.broadcasted_iota(jnp.int32, (WIDE, GROUP_W), 0)
    f = lax.broadcasted_iota(jnp.int32, (WIDE, GROUP_W), 1)
    bd = (r // CHUNK) == (f // HEAD_DIM)
    row = lax.broadcasted_iota(jnp.int32, (BLOCK, WIDE), 0)
    key = lax.broadcasted_iota(jnp.int32, (BLOCK, WIDE), 1) % CHUNK
    return bd, row, key


def _block_diag(x, bd):
    return jnp.where(bd, jnp.concatenate([x] * N_HEADS, axis=0), jnp.zeros((), x.dtype))


def _fold_heads(w, bd):
    w = jnp.where(bd, w, 0.0)
    return (w[0:CHUNK] + w[CHUNK:2 * CHUNK]) + (w[2 * CHUNK:3 * CHUNK] + w[3 * CHUNK:])


def _widen(cols):
    return jnp.concatenate([jnp.broadcast_to(c, (BLOCK, CHUNK)) for c in cols], axis=1)


def _head_rowsums(w):
    return [jnp.sum(w[:, _seg(h)], axis=1, keepdims=True) for h in HEADS]


def _tri_wide(x, tri):
    hi = x.astype(BF16)
    lo = (x - hi.astype(F32)).astype(BF16)
    y = _dot(jnp.concatenate([hi[:, _seg(h)] for h in HEADS] + [lo[:, _seg(h)] for h in HEADS], axis=0), tri)
    return jnp.concatenate([y[h * BLOCK:(h + 1) * BLOCK] + y[(N_HEADS + h) * BLOCK:(N_HEADS + h + 1) * BLOCK] for h in HEADS], axis=1)


def _feature_widen(cols):
    return jnp.concatenate([jnp.broadcast_to(c, (BLOCK, HEAD_DIM)) for c in cols], axis=1)


def _sbw_tile(q, kbd, mask, later, csum):
    z = _dot(q, kbd, _NT)
    lk = -_softplus(z)
    if mask is not None:
        lk = jnp.where(mask, lk, 0.0)
    e = z + lk
    att = jnp.exp(e + _tri_wide(lk, later) + csum)
    if mask is not None:
        att = jnp.where(mask, att, 0.0)
    return att, e, lk


def sbw_fwd(qkv, *, name):
    T = qkv.shape[0]

    def body(q_ref, k_ref, v_ref, o_ref):
        bd, row, key = _wide_consts()
        r2 = lax.broadcasted_iota(jnp.int32, (CHUNK, CHUNK), 0)
        c2 = lax.broadcasted_iota(jnp.int32, (CHUNK, CHUNK), 1)
        later = (r2 > c2).astype(BF16)

        def qblock(i, _):
            q = q_ref[_blk(i), :] * Q_SCALE
            cd = i // 2
            strict = key < row + BLOCK * (i % 2)

            def tile(c, mask, carry):
                acc, csum = carry
                att, _, lk = _sbw_tile(q, _block_diag(k_ref[_chunk_rows(c), :], bd), mask, later, csum)
                acc = acc + _dot(att.astype(BF16), _block_diag(v_ref[_chunk_rows(c), :], bd))
                return acc, csum + _widen(_head_rowsums(lk))

            carry = tile(cd, strict, (jnp.zeros((BLOCK, GROUP_W), F32), jnp.zeros((BLOCK, WIDE), F32)))
            acc, _ = lax.fori_loop(0, cd, lambda n, cr: tile(cd - 1 - n, None, cr), carry)
            o_ref[_blk(i), :] = acc.astype(BF16)
            return 0

        lax.fori_loop(0, NB, qblock, 0)

    return pl.pallas_call(
        body, name=name, grid=(T // SEQ,), in_specs=_qkv_specs(0),
        out_specs=pl.BlockSpec((SEQ, GROUP_W), lambda b: (b, 0)), out_shape=jax.ShapeDtypeStruct((T, GROUP_W), BF16),
        compiler_params=_params("parallel"),
    )(qkv, qkv, qkv)


def sbw_bwd(qkv, dmixed, *, name):
    T = qkv.shape[0]

    def body(q_ref, k_ref, v_ref, do_ref, dq_ref, dk_ref, dv_ref, att_s, sg_s):
        bd, row, key = _wide_consts()
        r2 = lax.broadcasted_iota(jnp.int32, (CHUNK, CHUNK), 0)
        c2 = lax.broadcasted_iota(jnp.int32, (CHUNK, CHUNK), 1)
        later = (r2 > c2).astype(BF16)
        earlier = (r2 < c2).astype(BF16)
        dk_ref[...] = jnp.zeros_like(dk_ref)
        dv_ref[...] = jnp.zeros_like(dv_ref)

        def qblock(i, _):
            q = q_ref[_blk(i), :] * Q_SCALE
            do = do_ref[_blk(i), :].astype(BF16)
            cd = i // 2
            strict = key < row + BLOCK * (i % 2)

            def recompute(c, mask, csum):
                att, e, lk = _sbw_tile(q, _block_diag(k_ref[_chunk_rows(c), :], bd), mask, later, csum)
                sg = jnp.exp(e)
                att_s[c] = att
                sg_s[c] = sg if mask is None else jnp.where(mask, sg, 0.0)
                return csum + _widen(_head_rowsums(lk))

            csum = recompute(cd, strict, jnp.zeros((BLOCK, WIDE), F32))
            lax.fori_loop(0, cd, lambda n, cs: recompute(cd - 1 - n, None, cs), csum)

            def tile(c, carry):
                dq, pre = carry
                kbd = _block_diag(k_ref[_chunk_rows(c), :], bd)
                vbd = _block_diag(v_ref[_chunk_rows(c), :], bd)
                att = att_s[c]
                ds = _dot(do, vbd, _NT) * att
                dlk = ds + _tri_wide(ds, earlier) + pre
                dz = (ds - dlk * sg_s[c]).astype(BF16)
                dk_ref[_chunk_rows(c), :] += _fold_heads(_dot(dz, q, _TN), bd)
                dv_ref[_chunk_rows(c), :] += _fold_heads(_dot(att.astype(BF16), do, _TN), bd)
                return dq + _dot(dz, kbd), pre + _widen(_head_rowsums(ds))

            dq, _ = lax.fori_loop(0, cd + 1, tile, (jnp.zeros((BLOCK, GROUP_W), F32), jnp.zeros((BLOCK, WIDE), F32)))
            dq_ref[_blk(i), :] = dq * Q_SCALE
            return 0

        lax.fori_loop(0, NB, qblock, 0)

    out = pl.BlockSpec((SEQ, GROUP_W), lambda b: (b, 0))
    sd = jax.ShapeDtypeStruct((T, GROUP_W), F32)
    return pl.pallas_call(
        body, name=name, grid=(T // SEQ,), in_specs=_qkv_specs(0) + [out],
        out_specs=[out] * 3, out_shape=[sd] * 3,
        scratch_shapes=[pltpu.VMEM((NCH, BLOCK, WIDE), F32), pltpu.VMEM((NCH, BLOCK, WIDE), F32)],
        compiler_params=_params("parallel"),
    )(qkv, qkv, qkv, dmixed)


def _foxw_logits(q, kbd, cq, cr_ref, c, mask):
    ck = jnp.concatenate([cr_ref[h:h + 1, _chunk_rows(c)] for h in HEADS], axis=1)
    z = _dot(q, kbd, _NT) + cq - ck
    return z if mask is None else jnp.where(mask, z, NEG)


def foxw_fwd(qkv, cumc, cumr, *, name):
    T = qkv.shape[0]

    def body(q_ref, k_ref, v_ref, cc_ref, cr_ref, o_ref, o32_ref, lse_ref, z_s):
        bd, row, key = _wide_consts()
        lse_ref[...] = jnp.zeros_like(lse_ref)

        def qblock(i, _):
            q = q_ref[_blk(i), :] * Q_SCALE
            cq = _widen([cc_ref[_blk(i), h:h + 1] for h in HEADS])
            cd = i // 2
            causal = key <= row + BLOCK * (i % 2)

            def logits(c, mask, ms):
                z = _foxw_logits(q, _block_diag(k_ref[_chunk_rows(c), :], bd), cq, cr_ref, c, mask)
                z_s[c] = z
                return tuple(jnp.maximum(ms[h], jnp.max(z[:, _seg(h)], axis=1, keepdims=True)) for h in HEADS)

            ms = logits(cd, causal, (jnp.full((BLOCK, 1), NEG, F32),) * N_HEADS)
            ms = lax.fori_loop(0, cd, lambda c, m: logits(c, None, m), ms)
            m_wide = _widen(ms)

            def values(c, carry):
                acc, l = carry
                p = jnp.exp(z_s[c] - m_wide)
                return acc + _dot(p.astype(BF16), _block_diag(v_ref[_chunk_rows(c), :], bd)), l + _widen(_head_rowsums(p))

            acc, l = lax.fori_loop(0, cd + 1, values, (jnp.zeros((BLOCK, GROUP_W), F32), jnp.zeros((BLOCK, WIDE), F32)))
            ls = [l[:, h * CHUNK:h * CHUNK + 1] for h in HEADS]
            o = acc / _feature_widen(ls)
            o_ref[_blk(i), :] = o.astype(BF16)
            o32_ref[_blk(i), :] = o
            for h in HEADS:
                lse_ref[_blk(i), h:h + 1] = ms[h] + jnp.log(ls[h])
            return 0

        lax.fori_loop(0, NB, qblock, 0)

    out = pl.BlockSpec((SEQ, GROUP_W), lambda b: (b, 0))
    colb = pl.BlockSpec((SEQ, LANES), lambda b: (b, 0))
    sd = jax.ShapeDtypeStruct
    return pl.pallas_call(
        body, name=name, grid=(T // SEQ,),
        in_specs=_qkv_specs(3) + [colb, pl.BlockSpec((8, SEQ), lambda b: (b, 0))],
        out_specs=[out, out, colb], out_shape=[sd((T, GROUP_W), BF16), sd((T, GROUP_W), F32), sd((T, LANES), F32)],
        scratch_shapes=[pltpu.VMEM((NCH, BLOCK, WIDE), F32)],
        compiler_params=_params("parallel"),
    )(qkv, qkv, qkv, cumc, cumr)


def foxw_bwd(qkv, cumc, cumr, lse, o32, dmixed, *, name):
    T = qkv.shape[0]

    def body(q_ref, k_ref, v_ref, cc_ref, cr_ref, lse_ref, o_ref, do_ref, dq_ref, dk_ref, dv_ref, dcc_ref, dcr_ref):
        bd, row, key = _wide_consts()
        dk_ref[...] = jnp.zeros_like(dk_ref)
        dv_ref[...] = jnp.zeros_like(dv_ref)
        dcc_ref[...] = jnp.zeros_like(dcc_ref)
        dcr_ref[...] = jnp.zeros_like(dcr_ref)

        def qblock(i, _):
            q = q_ref[_blk(i), :] * Q_SCALE
            do32 = do_ref[_blk(i), :]
            do = do32.astype(BF16)
            prod = do32 * o_ref[_blk(i), :]
            delta = _widen([jnp.sum(prod[:, _hs(h)], axis=1, keepdims=True) for h in HEADS])
            cq = _widen([cc_ref[_blk(i), h:h + 1] for h in HEADS])
            lse_w = _widen([lse_ref[_blk(i), h:h + 1] for h in HEADS])
            cd = i // 2
            causal = key <= row + BLOCK * (i % 2)

            def tile(c, mask, carry):
                dq, dcq = carry
                kbd = _block_diag(k_ref[_chunk_rows(c), :], bd)
                vbd = _block_diag(v_ref[_chunk_rows(c), :], bd)
                p = jnp.exp(_foxw_logits(q, kbd, cq, cr_ref, c, mask) - lse_w)
                ds = p * (_dot(do, vbd, _NT) - delta)
                dsb = ds.astype(BF16)
                dk_ref[_chunk_rows(c), :] += _fold_heads(_dot(dsb, q, _TN), bd)
                dv_ref[_chunk_rows(c), :] += _fold_heads(_dot(p.astype(BF16), do, _TN), bd)
                for h in HEADS:
                    dcr_ref[h:h + 1, _chunk_rows(c)] -= jnp.sum(ds[:, _seg(h)], axis=0, keepdims=True)
                return dq + _dot(dsb, kbd), dcq + _widen(_head_rowsums(ds))

            carry = tile(cd, causal, (jnp.zeros((BLOCK, GROUP_W), F32), jnp.zeros((BLOCK, WIDE), F32)))
            dq, dcq = lax.fori_loop(0, cd, lambda c, cr: tile(c, None, cr), carry)
            dq_ref[_blk(i), :] = dq * Q_SCALE
            for h in HEADS:
                dcc_ref[_blk(i), h:h + 1] = dcq[:, h * CHUNK:h * CHUNK + 1]
            return 0

        lax.fori_loop(0, NB, qblock, 0)

    out = pl.BlockSpec((SEQ, GROUP_W), lambda b: (b, 0))
    colb = pl.BlockSpec((SEQ, LANES), lambda b: (b, 0))
    rowb = pl.BlockSpec((8, SEQ), lambda b: (b, 0))
    sd = jax.ShapeDtypeStruct
    big = sd((T, GROUP_W), F32)
    return pl.pallas_call(
        body, name=name, grid=(T // SEQ,),
        in_specs=_qkv_specs(3) + [colb, rowb, colb, out, pl.BlockSpec((SEQ, GROUP_W), lambda b: (b, 1))],
        out_specs=[out, out, out, colb, rowb],
        out_shape=[big, big, big, sd((T, LANES), F32), sd((T // SEQ * 8, SEQ), F32)],
        compiler_params=_params("parallel"),
    )(qkv, qkv, qkv, cumc, cumr, lse, o32, dmixed)


BAND = 2 * BLOCK


def _t5_bucket_np(dist):
    n = np.maximum(dist, 0)
    max_exact = NUM_BUCKETS // 2
    nf = np.maximum(n, 1).astype(np.float32)
    large = max_exact + (np.log(nf / np.float32(max_exact)) / np.float32(math.log(MAX_DISTANCE / max_exact))
                         * np.float32(NUM_BUCKETS - max_exact)).astype(np.int32)
    large = np.minimum(large, NUM_BUCKETS - 1)
    return np.where(n < max_exact, n, large).astype(np.int32)


def _band_buckets():
    qi = np.arange(BLOCK)[:, None]
    ki = np.arange(BAND)[None, :]
    delta = np.clip(qi - ki + BLOCK, 0, BLOCK)
    return np.stack([_t5_bucket_np(delta * d) for d in DILATIONS])


def to_classes(a, d):
    if d == 1:
        return a
    T, C = a.shape
    return a.reshape(T // SEQ, SEQ // d, d, C).transpose(0, 2, 1, 3).reshape(T, C)


def from_classes(a, d):
    if d == 1:
        return a
    T, C = a.shape
    return a.reshape(T // SEQ, d, SEQ // d, C).transpose(0, 2, 1, 3).reshape(T, C)


def relbias_expand(rel, *, name):
    buckets = jnp.asarray(_band_buckets())
    n_pat = len(DILATIONS)

    def body(rel_ref, bk_ref, o_ref):
        for p in range(n_pat):
            bk = bk_ref[p]
            for h in range(N_HEADS):
                acc = jnp.zeros((BLOCK, BAND), F32)
                for b in range(NUM_BUCKETS):
                    acc = jnp.where(bk == b, rel_ref[b, h], acc)
                o_ref[p * N_HEADS + h] = acc

    return pl.pallas_call(
        body, name=name,
        in_specs=[pl.BlockSpec(memory_space=pltpu.SMEM), pl.BlockSpec(memory_space=pltpu.VMEM)],
        out_specs=pl.BlockSpec(memory_space=pltpu.VMEM),
        out_shape=jax.ShapeDtypeStruct((n_pat * N_HEADS, BLOCK, BAND), F32),
        compiler_params=_params(),
    )(rel, buckets)


def relbias_reduce(ds_all, *, name):
    buckets = jnp.asarray(_band_buckets())
    n_pat = len(DILATIONS)

    def body(ds_ref, bk_ref, o_ref):
        for b in range(NUM_BUCKETS):
            for h in range(N_HEADS):
                tot = jnp.float32(0.0)
                for p in range(n_pat):
                    tot = tot + jnp.sum(jnp.where(bk_ref[p] == b, ds_ref[p * N_HEADS + h], 0.0))
                o_ref[b, h] = tot

    return pl.pallas_call(
        body, name=name,
        in_specs=[pl.BlockSpec(memory_space=pltpu.VMEM), pl.BlockSpec(memory_space=pltpu.VMEM)],
        out_specs=pl.BlockSpec(memory_space=pltpu.SMEM),
        out_shape=jax.ShapeDtypeStruct((NUM_BUCKETS, N_HEADS), F32),
        compiler_params=_params(),
    )(ds_all, buckets)


def _band_valid_wide(first, row, key):
    inside = jnp.logical_and(key >= row, key <= row + BLOCK)
    return jnp.logical_and(inside, jnp.logical_or(jnp.logical_not(first), key >= BLOCK))


QKV_BLOCKS = 9


def _band_in_specs(d, pattern, has_prev):
    rows = BLOCK * d
    cur = lambda c: pl.BlockSpec((rows, GROUP_W), lambda tb, r: (tb, c))
    prev = lambda c: pl.BlockSpec((rows, GROUP_W), lambda tb, r: (jnp.maximum(tb - 1, 0), c))
    bias = pl.BlockSpec((N_HEADS, BLOCK, BAND), lambda tb, r: (pattern, 0, 0))
    return [cur(6), cur(7), cur(8)] + ([prev(7), prev(8)] if has_prev else []) + [bias]


def _class_rows(d):
    return pl.ds(pl.program_id(1), BLOCK, stride=d) if d > 1 else pl.ds(0, BLOCK)


def _halves_scratch(rows, n):
    return [pltpu.VMEM((2, rows, LANES), F32)] * n


def _stage(refs, scratch):
    @pl.when(pl.program_id(1) == 0)
    def _():
        for src, dst in zip(refs, scratch):
            dst[0] = src[:, :LANES].astype(F32)
            dst[1] = src[:, LANES:].astype(F32)


def _take_class(s, d):
    rows = _class_rows(d)
    return jnp.concatenate([s.at[0][rows, :], s.at[1][rows, :]], axis=1)


def _put_class(s, d, x):
    rows = _class_rows(d)
    s.at[0][rows, :] = x[:, :LANES]
    s.at[1][rows, :] = x[:, LANES:]


def _flush(scratch, refs, d):
    @pl.when(pl.program_id(1) == d - 1)
    def _():
        for s, o in zip(scratch, refs):
            o[...] = jnp.concatenate([s[0], s[1]], axis=1)


def _band_operands(scratch, d, has_prev):
    take = lambda s: _take_class(s, d).astype(BF16)
    q = (_take_class(scratch[0], d) * Q_SCALE).astype(BF16)
    if has_prev:
        k = jnp.concatenate([take(scratch[3]), take(scratch[1])], axis=0)
        v = jnp.concatenate([take(scratch[4]), take(scratch[2])], axis=0)
    else:
        k = jnp.concatenate([jnp.zeros((BLOCK, GROUP_W), BF16), take(scratch[1])], axis=0)
        v = jnp.concatenate([jnp.zeros((BLOCK, GROUP_W), BF16), take(scratch[2])], axis=0)
    return q, k, v


def _lane_columns(cols):
    lane = lax.broadcasted_iota(jnp.int32, (BLOCK, LANES), 1)
    out = jnp.zeros((BLOCK, LANES), F32)
    for h, c in enumerate(cols):
        out = jnp.where(lane == h, c, out)
    return out


def band_fwd(qkv, bias, pattern, *, name):
    T = qkv.shape[0]
    d = DILATIONS[pattern]
    rows_per_block = BLOCK * d
    seq_blocks = SEQ // rows_per_block
    has_prev = seq_blocks > 1
    n_in = 5 if has_prev else 3

    def body(*refs):
        ins, b_ref, o_ref, lse_ref = refs[:n_in], refs[n_in], refs[n_in + 1], refs[n_in + 2]
        staged, o_s = refs[n_in + 3:2 * n_in + 3], refs[2 * n_in + 3]
        bd, row, key = _wide_consts()
        valid = _band_valid_wide(pl.program_id(0) % seq_blocks == 0, row, key)
        _stage(ins, staged)
        q, k, v = _band_operands(staged, d, has_prev)
        kbd, vbd = _block_diag(k, bd), _block_diag(v, bd)
        bias_w = jnp.concatenate([b_ref[h] for h in HEADS], axis=1)
        sc = jnp.where(valid, _dot(q, kbd, _NT) + bias_w, NEG)
        ms = [jnp.max(sc[:, _seg(h)], axis=1, keepdims=True) for h in HEADS]
        p = jnp.exp(sc - _widen(ms))
        ls = _head_rowsums(p)
        _put_class(o_s, d, _dot(p.astype(BF16), vbd) / _feature_widen(ls))
        lse_ref[_class_rows(d), :] = _lane_columns([ms[h] + jnp.log(ls[h]) for h in HEADS])
        _flush([o_s], [o_ref], d)

    sd = jax.ShapeDtypeStruct
    return pl.pallas_call(
        body, name=name, grid=(T // rows_per_block, d), in_specs=_band_in_specs(d, pattern, has_prev),
        out_specs=[pl.BlockSpec((rows_per_block, GROUP_W), lambda tb, r: (tb, 0)),
                   pl.BlockSpec((rows_per_block, LANES), lambda tb, r: (tb, 0))],
        out_shape=[sd((T, GROUP_W), F32), sd((T, LANES), F32)],
        scratch_shapes=_halves_scratch(rows_per_block, n_in + 1),
        compiler_params=_params("parallel", "arbitrary"),
    )(*([qkv] * n_in), bias)


def band_bwd(qkv, bias, lse, do, dlse, pattern, *, name):
    T = qkv.shape[0]
    d = DILATIONS[pattern]
    rows_per_block = BLOCK * d
    seq_blocks = SEQ // rows_per_block
    has_prev = seq_blocks > 1
    n_in = 5 if has_prev else 3
    n_out = 5 if has_prev else 3

    def body(*refs):
        ins, b_ref, lse_ref, do_ref, dlse_ref = refs[:n_in], refs[n_in], refs[n_in + 1], refs[n_in + 2], refs[n_in + 3]
        outs = refs[n_in + 4:n_in + 4 + n_out]
        ds_ref = refs[n_in + 4 + n_out]
        scratch = refs[n_in + 5 + n_out:]
        staged, do_s, out_s = scratch[:n_in], scratch[n_in], scratch[n_in + 1:]
        first_step = jnp.logical_and(pl.program_id(0) == 0, pl.program_id(1) == 0)
        bd, row, key = _wide_consts()
        valid = _band_valid_wide(pl.program_id(0) % seq_blocks == 0, row, key)
        _stage(list(ins) + [do_ref], list(staged) + [do_s])
        q, k, v = _band_operands(staged, d, has_prev)
        kbd, vbd = _block_diag(k, bd), _block_diag(v, bd)
        rows = _class_rows(d)
        do = _take_class(do_s, d).astype(BF16)
        lse_t, dlse_t = lse_ref[rows, :], dlse_ref[rows, :]
        bias_w = jnp.concatenate([b_ref[h] for h in HEADS], axis=1)
        lse_w = _widen([lse_t[:, h:h + 1] for h in HEADS])
        dlse_w = _widen([dlse_t[:, h:h + 1] for h in HEADS])
        p = jnp.where(valid, jnp.exp(_dot(q, kbd, _NT) + bias_w - lse_w), 0.0)
        dp = _dot(do, vbd, _NT)
        ds = p * (dp - _widen(_head_rowsums(p * dp)) + dlse_w)
        dsb, pb = ds.astype(BF16), p.astype(BF16)
        _put_class(out_s[0], d, _dot(dsb, kbd) * Q_SCALE)
        dk = _fold_heads(_dot(dsb, q, _TN), bd)
        dv = _fold_heads(_dot(pb, do, _TN), bd)
        _put_class(out_s[1], d, dk[BLOCK:])
        _put_class(out_s[2], d, dv[BLOCK:])
        if has_prev:
            _put_class(out_s[3], d, dk[:BLOCK])
            _put_class(out_s[4], d, dv[:BLOCK])
        _flush(out_s, outs, d)

        @pl.when(first_step)
        def _():
            for h in HEADS:
                ds_ref[h] = ds[:, _seg(h)]

        @pl.when(jnp.logical_not(first_step))
        def _():
            for h in HEADS:
                ds_ref[h] += ds[:, _seg(h)]

    big = pl.BlockSpec((rows_per_block, GROUP_W), lambda tb, r: (tb, 0))
    colb = pl.BlockSpec((rows_per_block, LANES), lambda tb, r: (tb, 0))
    sd = jax.ShapeDtypeStruct
    return pl.pallas_call(
        body, name=name, grid=(T // rows_per_block, d), in_specs=_band_in_specs(d, pattern, has_prev) + [colb, big, colb],
        out_specs=[big] * n_out + [pl.BlockSpec((N_HEADS, BLOCK, BAND), lambda tb, r: (0, 0, 0))],
        out_shape=[sd((T, GROUP_W), F32)] * n_out + [sd((N_HEADS, BLOCK, BAND), F32)],
        scratch_shapes=_halves_scratch(rows_per_block, n_in + 1 + n_out),
        compiler_params=_params("arbitrary", "arbitrary"),
    )(*([qkv] * n_in), bias, lse, do, dlse)


def shift_add(cur, prev, d, *, name):
    rows = BLOCK * d
    nb = cur.shape[0] // rows

    def body(c_ref, p_ref, o_ref):
        keep = (pl.program_id(0) < nb - 1).astype(F32)
        o_ref[...] = c_ref[...] + keep * p_ref[...]

    blk = pl.BlockSpec((rows, GROUP_W), lambda tb: (tb, 0))
    nxt = pl.BlockSpec((rows, GROUP_W), lambda tb: (jnp.minimum(tb + 1, nb - 1), 0))
    return pl.pallas_call(
        body, name=name, grid=(nb,), in_specs=[blk, nxt], out_specs=blk,
        out_shape=jax.ShapeDtypeStruct(cur.shape, F32), compiler_params=_params("parallel"),
    )(cur, prev)


def _pattern_weights(lse_refs, h):
    ls = [r[:, h:h + 1] for r in lse_refs]
    mx = functools.reduce(jnp.maximum, ls)
    es = [jnp.exp(l - mx) for l in ls]
    tot = functools.reduce(lambda a, b: a + b, es)
    return [e / tot for e in es]


def dil_combine_fwd(outs, *, name):
    T = outs[0][0].shape[0]
    n = len(outs)
    tm = 512

    def body(*refs):
        o_refs, l_refs, out_ref = refs[:n], refs[n:2 * n], refs[2 * n]
        for h in range(N_HEADS):
            w = _pattern_weights(l_refs, h)
            acc = w[0] * o_refs[0][:, _hs(h)]
            for p in range(1, n):
                acc = acc + w[p] * o_refs[p][:, _hs(h)]
            out_ref[:, _hs(h)] = acc.astype(BF16)

    big = pl.BlockSpec((tm, GROUP_W), lambda i: (i, 0))
    colb = pl.BlockSpec((tm, LANES), lambda i: (i, 0))
    return pl.pallas_call(
        body, name=name, grid=(T // tm,), in_specs=[big] * n + [colb] * n,
        out_specs=big, out_shape=jax.ShapeDtypeStruct((T, GROUP_W), BF16),
        compiler_params=_params("parallel"),
    )(*[o for o, _ in outs], *[l for _, l in outs])


def dil_combine_bwd(outs, dmixed, *, name):
    T = outs[0][0].shape[0]
    n = len(outs)
    tm = 512

    def body(*refs):
        o_refs, l_refs, do_ref = refs[:n], refs[n:2 * n], refs[2 * n]
        do_refs, dl_refs = refs[2 * n + 1:3 * n + 1], refs[3 * n + 1:]
        for r in dl_refs:
            r[...] = jnp.zeros_like(r)
        for h in range(N_HEADS):
            w = _pattern_weights(l_refs, h)
            do = do_ref[:, _hs(h)]
            dw = [jnp.sum(do * o_refs[p][:, _hs(h)], axis=1, keepdims=True) for p in range(n)]
            mean = functools.reduce(lambda a, b: a + b, [w[p] * dw[p] for p in range(n)])
            for p in range(n):
                do_refs[p][:, _hs(h)] = w[p] * do
                dl_refs[p][:, h:h + 1] = w[p] * (dw[p] - mean)

    big = pl.BlockSpec((tm, GROUP_W), lambda i: (i, 0))
    colb = pl.BlockSpec((tm, LANES), lambda i: (i, 0))
    sd = jax.ShapeDtypeStruct
    res = pl.pallas_call(
        body, name=name, grid=(T // tm,),
        in_specs=[big] * n + [colb] * n + [pl.BlockSpec((tm, GROUP_W), lambda i: (i, 2))],
        out_specs=[big] * n + [colb] * n, out_shape=[sd((T, GROUP_W), F32)] * n + [sd((T, LANES), F32)] * n,
        compiler_params=_params("parallel"),
    )(*[o for o, _ in outs], *[l for _, l in outs], dmixed)
    return list(zip(res[:n], res[n:]))


def dilated_fwd(qkv, bias, tag):
    return [band_fwd(qkv, bias, p, name=f"{tag}_band_fwd{p}") for p in range(len(DILATIONS))]


def dilated_bwd(qkv, bias, outs, dmixed, tag):
    grads = dil_combine_bwd(outs, dmixed, name=f"{tag}_combine_bwd")
    parts, ds_all = [], []
    for p, d in enumerate(DILATIONS):
        (_, lse), (do, dlse) = outs[p], grads[p]
        res = band_bwd(qkv, bias, lse, do, dlse, p, name=f"{tag}_band_bwd{p}")
        dq, dk, dv, ds = res[0], res[1], res[2], res[-1]
        if len(res) > 4:
            dk = shift_add(dk, res[3], d, name=f"{tag}_dk{p}")
            dv = shift_add(dv, res[4], d, name=f"{tag}_dv{p}")
        parts.append([dq, dk, dv])
        ds_all.append(ds)
    return parts, jnp.concatenate(ds_all, axis=0)


def assemble_dqkv(d_sb, d_fox, d_dil, *, name):
    T = d_sb[0].shape[0]
    tr = 512
    n_pat = len(d_dil)
    flat = list(d_sb) + list(d_fox) + [a for part in d_dil for a in part]

    def body(*refs):
        o_ref = refs[-1]
        for j in range(6):
            o_ref[:, j * GROUP_W:(j + 1) * GROUP_W] = refs[j][...].astype(BF16)
        for j in range(3):
            acc = refs[6 + j][...]
            for p in range(1, n_pat):
                acc = acc + refs[6 + 3 * p + j][...]
            o_ref[:, (6 + j) * GROUP_W:(7 + j) * GROUP_W] = acc.astype(BF16)

    blk = pl.BlockSpec((tr, GROUP_W), lambda i: (i, 0))
    return pl.pallas_call(
        body, name=name, grid=(T // tr,), in_specs=[blk] * len(flat),
        out_specs=pl.BlockSpec((tr, QKV_BLOCKS * GROUP_W), lambda i: (i, 0)),
        out_shape=jax.ShapeDtypeStruct((T, QKV_BLOCKS * GROUP_W), BF16), compiler_params=_params("parallel"),
    )(*flat)


def sum_cast(arrs, dtype, *, name):
    R, C = arrs[0].shape
    tr = _largest_tile(R, 512, 16)
    n = len(arrs)

    def body(*refs):
        acc = refs[0][...].astype(F32)
        for r in refs[1:n]:
            acc = acc + r[...].astype(F32)
        refs[n][...] = acc.astype(dtype)

    blk = pl.BlockSpec((tr, C), lambda i: (i, 0))
    return pl.pallas_call(
        body, name=name, grid=(R // tr,), in_specs=[blk] * n, out_specs=blk, out_shape=jax.ShapeDtypeStruct((R, C), dtype),
        compiler_params=_params("parallel"),
    )(*arrs)


GRAD_WIRE = BF16


def _block_diag_halves(w):
    z = jnp.zeros((HEAD_DIM, HEAD_DIM), w.dtype)
    half = lambda a, b: jnp.concatenate([jnp.concatenate([a, z], axis=1), jnp.concatenate([z, b], axis=1)], axis=0)
    return jnp.stack([half(w[0], w[1]), half(w[2], w[3])]).astype(BF16)


def _diag_blocks(d):
    h = HEAD_DIM
    return jnp.stack([d[0, :h, :h], d[0, h:, h:], d[1, :h, :h], d[1, h:, h:]])


def layer_fwd(x, mem2d, W, P, bias, tag):
    s = {}
    s["x"] = x
    h1 = rmsnorm_fwd(x, P["norm_mix_g"], name=f"{tag}_norm_mix")
    qkv = matmul(h1, W["qkv"], out_dtype=BF16, name=f"{tag}_qkv")
    aux = matmul(h1, W["aux"], name=f"{tag}_aux")
    o_sb = sbw_fwd(qkv, name=f"{tag}_sb_fwd")
    cumc = fox_prep(aux, P["bf"], name=f"{tag}_fox_prep")
    cumr = col_to_row(cumc)
    o_fox, o_fox32, lse_fox = foxw_fwd(qkv, cumc, cumr, name=f"{tag}_fox_fwd")
    dil = dilated_fwd(qkv, bias, tag)
    o_dil = dil_combine_fwd(dil, name=f"{tag}_dil_combine")
    o_lru, h_lru = lru_fwd(aux, P["lru_conv_w"], P["lru_conv_b"], P["wa"], P["lru_b_a"], P["wx"], P["lru_b_x"],
                           P["lru_lambda"], name=f"{tag}_lru_fwd")
    mixed = jnp.concatenate([o_sb, o_fox, o_dil, o_lru], axis=1)
    x1 = matmul(mixed, W["out"], residual=x, name=f"{tag}_out")
    hq = rmsnorm_fwd(x1, P["norm_cross_g"], name=f"{tag}_norm_cross")
    qc = matmul(hq, W["cq"], out_dtype=BF16, name=f"{tag}_cq")
    memn = rmsnorm_fwd(mem2d, P["norm_mem_g"], name=f"{tag}_norm_mem")
    kv = matmul(memn, W["ckv"], out_dtype=BF16, name=f"{tag}_ckv")
    oc = cross_fwd(qc, kv, name=f"{tag}_cross_fwd")
    x2 = matmul(oc, W["coT"], trans_b=True, residual=x1, name=f"{tag}_co")
    h2 = rmsnorm_fwd(x2, P["norm_ffn_g"], name=f"{tag}_norm_ffn")
    if "ffn" in W:
        W.update(W.pop("ffn")(x2))
    hu = matmul(h2, W["up_u"], trans_b=True, name=f"{tag}_up_u")
    hg = matmul(h2, W["up_g"], trans_b=True, name=f"{tag}_up_g")
    act = glu_fwd(hu, hg, P["wu"], P["wg"], P["bu"], P["bg"], name=f"{tag}_glu_fwd")
    x3 = matmul(act, W["down"], residual=x2, name=f"{tag}_down")
    s.update(h1=h1, qkv=qkv, aux=aux, cumc=cumc, cumr=cumr, lse_fox=lse_fox, o_fox32=o_fox32, dil=dil, h_lru=h_lru, mixed=mixed,
             x1=x1, hq=hq, qc=qc, memn=memn, kv=kv, oc=oc, x2=x2, h2=h2, hu=hu, hg=hg, act=act)
    return x3, s


def layer_bwd(dx3, mem2d, W, P, bias, s, tag, ffn_grads_done=None):
    mm = functools.partial(matmul, out_dtype=GRAD_WIRE, trans_a=True)
    gW, gP = {}, {}
    dact = matmul(dx3, W["down"], trans_b=True, name=f"{tag}_d_act")
    gW["down"] = mm(s["act"], dx3, name=f"{tag}_g_down")
    dhu, dhg, dwu, dwg, dbu, dbg = glu_bwd(s["hu"], s["hg"], dact, P["wu"], P["wg"], P["bu"], P["bg"], name=f"{tag}_glu_bwd")
    gP["ffn_conv_w"] = jnp.concatenate([dwu, dwg], axis=1)
    gP["ffn_conv_b"] = jnp.concatenate([dbu, dbg], axis=1)
    dh2 = matmul(dhu, W["up_u"], name=f"{tag}_d_h2u")
    dh2 = matmul(dhg, W["up_g"], residual=dh2, name=f"{tag}_d_h2g")
    gW["up_u"] = mm(dhu, s["h2"], name=f"{tag}_g_up_u")
    gW["up_g"] = mm(dhg, s["h2"], name=f"{tag}_g_up_g")
    dx2, gP["norm_ffn_g"] = rmsnorm_bwd(s["x2"], P["norm_ffn_g"], dh2, dx3, name=f"{tag}_norm_ffn_bwd")
    if ffn_grads_done is not None:
        dx2 = ffn_grads_done(gW, dx2)
    doc = matmul(dx2, W["coT"], name=f"{tag}_d_oc")
    gW["coT"] = mm(dx2, s["oc"], name=f"{tag}_g_co")
    dqc, dkv = cross_bwd(s["qc"], s["kv"], doc, name=f"{tag}_cross_bwd")
    dhq = matmul(dqc, W["cq"], trans_b=True, name=f"{tag}_d_hq")
    gW["cq"] = mm(s["hq"], dqc, name=f"{tag}_g_cq")
    dmemn = matmul(dkv, W["ckv"], trans_b=True, name=f"{tag}_d_memn")
    gW["ckv"] = mm(s["memn"], dkv, name=f"{tag}_g_ckv")
    _, gP["norm_mem_g"] = rmsnorm_bwd(mem2d, P["norm_mem_g"], dmemn, None, name=f"{tag}_norm_mem_bwd")
    dx1, gP["norm_cross_g"] = rmsnorm_bwd(s["x1"], P["norm_cross_g"], dhq, dx2, name=f"{tag}_norm_cross_bwd")
    dmixed = matmul(dx1, W["out"], trans_b=True, name=f"{tag}_d_mixed")
    gW["out"] = mm(s["mixed"], dx1, name=f"{tag}_g_out")
    qkv, aux = s["qkv"], s["aux"]
    d_sb = sbw_bwd(qkv, dmixed, name=f"{tag}_sb_bwd")
    dfq, dfk, dfv, dcc, dcr = foxw_bwd(qkv, s["cumc"], s["cumr"], s["lse_fox"], s["o_fox32"], dmixed, name=f"{tag}_fox_bwd")
    dcum = sum_cast([dcc, row_to_col(dcr)], F32, name=f"{tag}_dcum")
    df, dbf = fox_prep_bwd(aux, P["bf"], dcum, name=f"{tag}_fox_prep_bwd")
    gP["b_forget"] = dbf[0, :N_HEADS]
    d_dil, ds_band = dilated_bwd(qkv, bias, s["dil"], dmixed, tag)
    dlx, dlg, dcw, dcb, dwa, dba, dwx, dbx, dlam = lru_bwd(
        aux, s["h_lru"], dmixed, P["lru_conv_w"], P["lru_conv_b"], P["wa"], P["lru_b_a"], P["wx"], P["lru_b_x"],
        P["lru_lambda"], name=f"{tag}_lru_bwd")
    gP.update(lru_conv_w=dcw, lru_conv_b=dcb, lru_w_a=_diag_blocks(dwa), lru_b_a=dba, lru_w_x=_diag_blocks(dwx),
              lru_b_x=dbx, lru_lambda=dlam)
    dqkv = assemble_dqkv(d_sb, [dfq, dfk, dfv], d_dil, name=f"{tag}_dqkv")
    daux = jnp.concatenate([dlx, dlg, df], axis=1)
    dh1 = matmul(dqkv, W["qkv"], trans_b=True, name=f"{tag}_d_h1a")
    dh1 = matmul(daux, W["aux"], trans_b=True, residual=dh1, name=f"{tag}_d_h1b")
    gW["qkv"] = mm(s["h1"], dqkv, name=f"{tag}_g_qkv")
    gW["aux"] = mm(s["h1"], daux, name=f"{tag}_g_aux")
    dx, gP["norm_mix_g"] = rmsnorm_bwd(s["x"], P["norm_mix_g"], dh1, dx1, name=f"{tag}_norm_mix_bwd")
    return dx, gW, gP, ds_band


def local_step(x, mem, target, weights_of, Ps, rel_bias, final_norm_g, grads_done=None, ffn_grads_done=None):
    B = x.shape[0]
    x2d = x.reshape(B * SEQ, D_MODEL)
    mem2d = mem.reshape(B * N_MEM, D_MODEL)
    bias = relbias_expand(rel_bias, name="relbias_expand")
    saved, Ws = [], []
    h = x2d
    for l in range(DEPTH):
        Ws.append(weights_of(l, h))
        h, s = layer_fwd(h, mem2d, Ws[l], Ps[l], bias, f"l{l}")
        saved.append(s)
    loss, dh, d_final = loss_head(h, final_norm_g, target.reshape(B * SEQ, D_MODEL), name="loss_head")
    gWs, gPs, ds_bands = [None] * DEPTH, [None] * DEPTH, []
    for l in range(DEPTH - 1, -1, -1):
        hook = None if ffn_grads_done is None else functools.partial(ffn_grads_done, l)
        dh, gWs[l], gPs[l], ds = layer_bwd(dh, mem2d, Ws[l], Ps[l], bias, saved[l], f"l{l}", hook)
        if grads_done is not None:
            dh = grads_done(l, gWs[l], dh)
        ds_bands.append(ds)
    d_rel = relbias_reduce(sum_cast([d.reshape(-1, BAND) for d in ds_bands], F32, name="ds_band_sum").reshape(-1, BLOCK, BAND),
                           name="relbias_reduce")
    return loss, dh.reshape(B, SEQ, D_MODEL), gWs, gPs, d_rel, d_final


def small_params(p, l):
    row = lambda name: p[name][l].reshape(1, -1)
    ffn_w, ffn_b = p["ffn_conv_w"][l], row("ffn_conv_b")
    return dict(
        norm_mix_g=row("norm_mix_g"), norm_cross_g=row("norm_cross_g"), norm_mem_g=row("norm_mem_g"), norm_ffn_g=row("norm_ffn_g"),
        bf=jnp.pad(row("b_forget"), ((0, 0), (0, LANES - N_HEADS))),
        lru_conv_w=p["lru_conv_w"][l], lru_conv_b=row("lru_conv_b"), wa=_block_diag_halves(p["lru_w_a"][l]), lru_b_a=row("lru_b_a"),
        wx=_block_diag_halves(p["lru_w_x"][l]), lru_b_x=row("lru_b_x"), lru_lambda=row("lru_lambda"),
        wu=ffn_w[:, :D_FF], wg=ffn_w[:, D_FF:], bu=ffn_b[:, :D_FF], bg=ffn_b[:, D_FF:])


def canonical_weights(w_in, w_out, w_cq, w_ck, w_cv, w_co, w_up, w_down):
    sb_fox, fox_f, rest = w_in[:, :6 * GROUP_W], w_in[:, 6 * GROUP_W:6 * GROUP_W + N_HEADS], w_in[:, 6 * GROUP_W + N_HEADS:]
    dil, lru = rest[:, :3 * GROUP_W], rest[:, 3 * GROUP_W:]
    pad = jnp.zeros((w_in.shape[0], AUX_W - 2 * GROUP_W - N_HEADS), w_in.dtype)
    return dict(qkv=jnp.concatenate([sb_fox, dil], axis=1), aux=jnp.concatenate([lru, fox_f, pad], axis=1), out=w_out,
                cq=w_cq, ckv=jnp.concatenate([w_ck, w_cv], axis=1), coT=w_co.T, upT=w_up.T, down=w_down)


def native_grads(g):
    qkv, aux = g["qkv"], g["aux"]
    a, b = 6 * GROUP_W, 6 * GROUP_W + N_HEADS
    w_in = jnp.zeros((qkv.shape[0], b + 5 * GROUP_W), qkv.dtype)
    w_in = w_in.at[:, :a].set(qkv[:, :a]).at[:, a:b].set(aux[:, 2 * GROUP_W:2 * GROUP_W + N_HEADS])
    w_in = w_in.at[:, b:b + 3 * GROUP_W].set(qkv[:, a:]).at[:, b + 3 * GROUP_W:].set(aux[:, :2 * GROUP_W])
    return (w_in, g["out"], g["cq"], g["ckv"][:, :GROUP_W], g["ckv"][:, GROUP_W:], g["coT"].T, g["upT"].T, g["down"])


ANY = pl.BlockSpec(memory_space=pl.ANY)
VMEM_SPEC = pl.BlockSpec(memory_space=pltpu.VMEM)


def _place():
    x, y, c = lax.axis_index("x"), lax.axis_index("y"), lax.axis_index("c")
    other_chips = [(1 - x, y), (x, 1 - y), (1 - x, 1 - y)]
    return x, y, c, other_chips


def _gather_body(x_ref, out_ref, send_sems, recv_sems, local_sem):
    x, y, c, chips = _place()
    me, sibling = (x, y, c), (x, y, 1 - c)

    def slot(px, py, pc):
        return out_ref.at[4 * px + 2 * py + pc]

    def copy(k, block, to, src=None):
        return pltpu.make_async_remote_copy(
            src_ref=slot(*block) if src is None else src, dst_ref=slot(*block),
            send_sem=send_sems.at[k], recv_sem=recv_sems.at[k], device_id=to, device_id_type=MESH)

    if local_sem is not None:
        mine = pltpu.make_async_copy(x_ref, slot(*me), local_sem)
        mine.start()
    first = [copy(0, me, sibling, src=x_ref)]
    first += [copy(1 + j, me, (*chip, c), src=x_ref) for j, chip in enumerate(chips)]
    for cp in first:
        cp.start()
    passed = [copy(4 + j, (*chip, c), sibling) for j, chip in enumerate(chips)]
    for j, chip in enumerate(chips):
        copy(1 + j, (*chip, c), me).wait_recv()
        passed[j].start()
    copy(0, sibling, me).wait_recv()
    for j, chip in enumerate(chips):
        copy(4 + j, (*chip, 1 - c), me).wait_recv()
    for cp in first + passed:
        cp.wait_send()
    if local_sem is not None:
        mine.wait()


_GATHER_SEMS = [pltpu.SemaphoreType.DMA((7,)), pltpu.SemaphoreType.DMA((7,)), pltpu.SemaphoreType.DMA]


def allgather_hbm(shard, me, *, name):
    def body(x_ref, out_ref, done_ref, send_sems, recv_sems):
        _gather_body(x_ref, out_ref, send_sems, recv_sems, None)
        done_ref[...] = jnp.zeros_like(done_ref)

    others, done = pl.pallas_call(
        body, name=name, in_specs=[ANY], out_specs=[ANY, VMEM_SPEC],
        out_shape=[jax.ShapeDtypeStruct((N_DEV,) + shard.shape, shard.dtype), jax.ShapeDtypeStruct((8, LANES), F32)],
        scratch_shapes=_GATHER_SEMS[:2],
    )(shard)
    return lax.dynamic_update_slice(others, shard[None], (me, 0, 0)), done


def allgather_small(x, *, name, reduce=False):
    def body(x_ref, out_ref, *rest):
        _gather_body(x_ref, out_ref, *rest[-3:])
        if reduce:
            acc = out_ref[0]
            for d in range(1, N_DEV):
                acc = acc + out_ref[d]
            rest[0][...] = acc

    sd = jax.ShapeDtypeStruct
    return pl.pallas_call(
        body, name=name, in_specs=[VMEM_SPEC], out_specs=[VMEM_SPEC, VMEM_SPEC] if reduce else VMEM_SPEC,
        out_shape=[sd((N_DEV,) + x.shape, x.dtype), sd(x.shape, x.dtype)] if reduce else sd((N_DEV,) + x.shape, x.dtype),
        scratch_shapes=_GATHER_SEMS, compiler_params=pltpu.CompilerParams(vmem_limit_bytes=VMEM_LIMIT_V7X),
    )(x)


N_CHIPS = 4


def pair_exchange(g, *, name):
    _, R, C = g.shape

    def body(g_ref, recv_ref, send_sems, recv_sems):
        x, y, c, _ = _place()
        sibling = (x, y, 1 - c)
        remote = [pltpu.make_async_remote_copy(
            src_ref=g_ref.at[2 * q + (1 - c)], dst_ref=recv_ref.at[q], send_sem=send_sems.at[q], recv_sem=recv_sems.at[q],
            device_id=sibling, device_id_type=MESH) for q in range(N_CHIPS)]
        for cp in remote:
            cp.start()
        for cp in remote:
            cp.wait_recv()
        for cp in remote:
            cp.wait_send()

    return pl.pallas_call(
        body, name=name, in_specs=[ANY], out_specs=ANY, out_shape=jax.ShapeDtypeStruct((N_CHIPS, R, C), g.dtype),
        scratch_shapes=[pltpu.SemaphoreType.DMA((N_CHIPS,))] * 2,
    )(g)


def chip_exchange(s, *, name):
    _, R, C = s.shape

    def body(s_ref, o0, o1, o2, send_sems, recv_sems):
        x, y, c, chips = _place()
        outs = (o0, o1, o2)
        copies = [pltpu.make_async_remote_copy(
            src_ref=s_ref.at[2 * cx + cy], dst_ref=outs[j], send_sem=send_sems.at[j], recv_sem=recv_sems.at[j],
            device_id=(cx, cy, c), device_id_type=MESH) for j, (cx, cy) in enumerate(chips)]
        for cp in copies:
            cp.start()
        for cp in copies:
            cp.wait_recv()
        for cp in copies:
            cp.wait_send()

    sd = jax.ShapeDtypeStruct((R, C), s.dtype)
    return pl.pallas_call(
        body, name=name, in_specs=[ANY], out_specs=[ANY] * 3, out_shape=[sd] * 3,
        scratch_shapes=[pltpu.SemaphoreType.DMA((3,)), pltpu.SemaphoreType.DMA((3,))],
    )(s)


HBM_SPEC = pl.BlockSpec(memory_space=pltpu.HBM)
SEM_SPEC = pl.BlockSpec(memory_space=pltpu.SEMAPHORE)
N_PEERS = N_DEV - 1


def _peers():
    x, y, c = lax.axis_index("x"), lax.axis_index("y"), lax.axis_index("c")
    flip = lambda v, bit: 1 - v if bit else v
    out = []
    for k in range(1, N_DEV):
        px, py, pc = flip(x, (k >> 2) & 1), flip(y, (k >> 1) & 1), flip(c, k & 1)
        out.append(((px, py, pc), 4 * px + 2 * py + pc))
    return out, 4 * x + 2 * y + c


def _peer_copies(src_ref, land_ref, send_sems, recv_sems, scatter, landing):
    peers, me = _peers()
    return [pltpu.make_async_remote_copy(
        src_ref=src_ref.at[idx] if scatter else src_ref, dst_ref=land_ref.at[me if landing == "mine" else idx],
        send_sem=send_sems.at[k], recv_sem=recv_sems.at[k], device_id=peer, device_id_type=MESH)
        for k, (peer, idx) in enumerate(peers)]


def exchange_start(src, scatter, *, name):
    shape = (N_DEV,) + src.shape[-2:]

    def body(src_ref, land_ref, send_sems, recv_sems, src_thru, land_thru, token):
        for cp in _peer_copies(src_ref, land_ref, send_sems, recv_sems, scatter, "mine"):
            cp.start()
        token[...] = jnp.zeros_like(token)

    sems = pltpu.SemaphoreType.DMA((N_PEERS,))
    return pl.pallas_call(
        body, name=name,
        out_shape=(sems, sems, pltpu.HBM(src.shape, src.dtype), pltpu.HBM(shape, src.dtype), jax.ShapeDtypeStruct((8, LANES), F32)),
        in_specs=(HBM_SPEC, HBM_SPEC), out_specs=(SEM_SPEC, SEM_SPEC, HBM_SPEC, HBM_SPEC, VMEM_SPEC),
        input_output_aliases={0: 2, 1: 3},
        compiler_params=pltpu.CompilerParams(has_side_effects=pltpu.SideEffectType.DATAFLOW_SIDE_EFFECTING),
    )(pltpu.with_memory_space_constraint(src, pltpu.HBM), pltpu.with_memory_space_constraint(lax.empty(shape, src.dtype), pltpu.HBM))


def exchange_wait(started, after, scatter, *, name):
    send_sems, recv_sems, src_thru, land_thru, _ = started

    def body(src_ref, land_ref, send_sems, recv_sems, after_ref, src_dead, got_ref):
        for cp in _peer_copies(src_ref, land_ref, send_sems, recv_sems, scatter, "theirs"):
            cp.wait_send()
            cp.wait_recv()

    return pl.pallas_call(
        body, name=name, out_shape=(pltpu.HBM(src_thru.shape, src_thru.dtype), pltpu.HBM(land_thru.shape, land_thru.dtype)),
        in_specs=(HBM_SPEC, HBM_SPEC, SEM_SPEC, SEM_SPEC, ANY), out_specs=(HBM_SPEC, HBM_SPEC), input_output_aliases={0: 0, 1: 1},
        compiler_params=pltpu.CompilerParams(has_side_effects=pltpu.SideEffectType.DATAFLOW_SIDE_EFFECTING),
    )(src_thru, land_thru, send_sems, recv_sems, after)[1]


def sum_blocks(blocks, *, name):
    n, R, C = blocks.shape
    tr = _largest_tile(R, 512, 16)

    def body(b_ref, o_ref):
        d = pl.program_id(1)
        v = b_ref[...].astype(F32)

        @pl.when(d == 0)
        def _():
            o_ref[...] = v

        @pl.when(d > 0)
        def _():
            o_ref[...] += v

    return pl.pallas_call(
        body, name=name, grid=(R // tr, n),
        in_specs=[pl.BlockSpec((None, tr, C), lambda i, d: (d, i, 0))], out_specs=pl.BlockSpec((tr, C), lambda i, d: (i, 0)),
        out_shape=jax.ShapeDtypeStruct((R, C), F32), compiler_params=_params("parallel", "arbitrary"),
    )(blocks)


WEIGHTS = ("norm_mix_g", "w_in", "b_forget", "lru_conv_w", "lru_conv_b", "lru_w_a", "lru_b_a", "lru_w_x", "lru_b_x", "lru_lambda",
           "w_out", "norm_cross_g", "norm_mem_g", "w_cq", "w_ck", "w_cv", "w_co", "norm_ffn_g", "w_up", "ffn_conv_w", "ffn_conv_b",
           "w_down", "rel_bias", "final_norm_g")
LARGE = ("w_in", "w_out", "w_cq", "w_ck", "w_cv", "w_co", "w_up", "w_down")
COLUMN_SPLIT_SMALL = ("lru_conv_w", "ffn_conv_w")
PACK = (("qkv", 128, 2304), ("aux", 128, 640), ("out", 128, 1024), ("cq", 128, 256), ("ckv", 128, 512), ("coT", 128, 256),
        ("upT", 704, 1024), ("down", 352, 1024))
PACK_W = 1024


def _pack_rows(parts):
    return jnp.concatenate([p.reshape(-1, PACK_W) for p in parts], axis=0)


def _pad_rows(flat, mult=8 * LANES):
    n = flat.shape[0]
    return jnp.pad(flat, (0, (-n) % mult)).reshape(-1, LANES)


def kernel(x, mem, norm_mix_g, w_in, b_forget, lru_conv_w, lru_conv_b, lru_w_a, lru_b_a, lru_w_x, lru_b_x, lru_lambda, w_out, norm_cross_g, norm_mem_g, w_cq, w_ck, w_cv, w_co, norm_ffn_g, w_up, ffn_conv_w, ffn_conv_b, w_down, rel_bias, final_norm_g, loss_target, m_norm_mix_g, m_w_in, m_b_forget, m_lru_conv_w, m_lru_conv_b, m_lru_w_a, m_lru_b_a, m_lru_w_x, m_lru_b_x, m_lru_lambda, m_w_out, m_norm_cross_g, m_norm_mem_g, m_w_cq, m_w_ck, m_w_cv, m_w_co, m_norm_ffn_g, m_w_up, m_ffn_conv_w, m_ffn_conv_b, m_w_down, m_rel_bias, m_final_norm_g, v_norm_mix_g, v_w_in, v_b_forget, v_lru_conv_w, v_lru_conv_b, v_lru_w_a, v_lru_b_a, v_lru_w_x, v_lru_b_x, v_lru_lambda, v_w_out, v_norm_cross_g, v_norm_mem_g, v_w_cq, v_w_ck, v_w_cv, v_w_co, v_norm_ffn_g, v_w_up, v_ffn_conv_w, v_ffn_conv_b, v_w_down, v_rel_bias, v_final_norm_g):
    w = dict(norm_mix_g=norm_mix_g, w_in=w_in, b_forget=b_forget, lru_conv_w=lru_conv_w, lru_conv_b=lru_conv_b, lru_w_a=lru_w_a,
             lru_b_a=lru_b_a, lru_w_x=lru_w_x, lru_b_x=lru_b_x, lru_lambda=lru_lambda, w_out=w_out, norm_cross_g=norm_cross_g,
             norm_mem_g=norm_mem_g, w_cq=w_cq, w_ck=w_ck, w_cv=w_cv, w_co=w_co, norm_ffn_g=norm_ffn_g, w_up=w_up,
             ffn_conv_w=ffn_conv_w, ffn_conv_b=ffn_conv_b, w_down=w_down, rel_bias=rel_bias, final_norm_g=final_norm_g)
    m = dict(norm_mix_g=m_norm_mix_g, w_in=m_w_in, b_forget=m_b_forget, lru_conv_w=m_lru_conv_w, lru_conv_b=m_lru_conv_b,
             lru_w_a=m_lru_w_a, lru_b_a=m_lru_b_a, lru_w_x=m_lru_w_x, lru_b_x=m_lru_b_x, lru_lambda=m_lru_lambda, w_out=m_w_out,
             norm_cross_g=m_norm_cross_g, norm_mem_g=m_norm_mem_g, w_cq=m_w_cq, w_ck=m_w_ck, w_cv=m_w_cv, w_co=m_w_co,
             norm_ffn_g=m_norm_ffn_g, w_up=m_w_up, ffn_conv_w=m_ffn_conv_w, ffn_conv_b=m_ffn_conv_b, w_down=m_w_down,
             rel_bias=m_rel_bias, final_norm_g=m_final_norm_g)
    v = dict(norm_mix_g=v_norm_mix_g, w_in=v_w_in, b_forget=v_b_forget, lru_conv_w=v_lru_conv_w, lru_conv_b=v_lru_conv_b,
             lru_w_a=v_lru_w_a, lru_b_a=v_lru_b_a, lru_w_x=v_lru_w_x, lru_b_x=v_lru_b_x, lru_lambda=v_lru_lambda, w_out=v_w_out,
             norm_cross_g=v_norm_cross_g, norm_mem_g=v_norm_mem_g, w_cq=v_w_cq, w_ck=v_w_ck, w_cv=v_w_cv, w_co=v_w_co,
             norm_ffn_g=v_norm_ffn_g, w_up=v_w_up, ffn_conv_w=v_ffn_conv_w, ffn_conv_b=v_ffn_conv_b, w_down=v_w_down,
             rel_bias=v_rel_bias, final_norm_g=v_final_norm_g)
    me = 4 * lax.axis_index("x") + 2 * lax.axis_index("y") + lax.axis_index("c")

    conv_shard = jnp.concatenate([w[n].reshape(-1) for n in COLUMN_SPLIT_SMALL])
    conv_all = allgather_small(_pad_rows(conv_shard), name="gather_conv").reshape(N_DEV, -1)
    full = dict(w)
    off = 0
    for n in COLUMN_SPLIT_SMALL:
        d, k, c = w[n].shape
        blocks = conv_all[:, off:off + d * k * c].reshape(N_DEV, d, k, c)
        full[n] = blocks.transpose(1, 2, 0, 3).reshape(d, k, N_DEV * c)
        off += d * k * c

    MIX, FFN = PACK[:6], PACK[6:]

    def packed_shard(l, group):
        canon = canonical_weights(*[w[n][l] for n in LARGE])
        return _pack_rows([canon[k].astype(BF16) for k, _, _ in group])

    def unpack_weights(packed, group):
        W, row = {}, 0
        for k, r, c in group:
            n_rows = r * c // PACK_W
            W[k] = packed[:, row:row + n_rows].reshape(N_DEV * r, c)
            row += n_rows
        if "upT" in W:
            upT = W.pop("upT")
            W["up_u"], W["up_g"] = upT[:D_FF], upT[D_FF:]
        return W

    def packed_grads(gW, group):
        g = dict(gW)
        if "up_u" in g:
            g["upT"] = jnp.concatenate([g.pop("up_u"), g.pop("up_g")], axis=0)
        return jnp.concatenate([g[k].reshape(N_DEV, r * c // PACK_W, PACK_W) for k, r, c in group], axis=1)

    def unpack_grads(shard_sum, group):
        g, row = {}, 0
        for k, r, c in group:
            n_rows = r * c // PACK_W
            g[k] = shard_sum[row:row + n_rows].reshape(r, c)
            row += n_rows
        return g

    def own_block_in(landed, block):
        return lax.dynamic_update_slice(landed, block[None], (me, 0, 0))

    def gathered_weights(copies, shard, after, group, name):
        return unpack_weights(own_block_in(exchange_wait(copies, after, False, name=name), shard), group)

    def scattered_sum(src, copies, after, tag):
        landed = exchange_wait(copies, after, True, name=f"{tag}_wait")
        mine = lax.dynamic_index_in_dim(src, me, axis=0, keepdims=False)
        return sum_blocks(own_block_in(landed, mine), name=f"{tag}_sum")

    last = DEPTH - 1
    mix0, gathered = allgather_hbm(packed_shard(0, MIX), me, name="gather_weights")
    ffn0_shard = packed_shard(0, FFN) + gathered[0, 0].astype(BF16)
    gather_ffn0 = exchange_start(ffn0_shard, False, name="gather_ffn0_start")
    last_shard = packed_shard(last, PACK) + gather_ffn0[4][0, 0].astype(BF16)
    gather_last = exchange_start(last_shard, False, name="gather_last_start")
    started = gather_last[4][0, 0]

    def weights_of(l, h):
        if l == 0:
            W = unpack_weights(mix0, MIX)
            W["ffn"] = lambda after: gathered_weights(gather_ffn0, ffn0_shard, after, FFN, "gather_ffn0_wait")
            return W
        assert l == last
        return gathered_weights(gather_last, last_shard, h, PACK, "gather_last_wait")

    in_flight = {}

    def scatter(key, g_all, dx, name):
        in_flight[key] = (g_all, exchange_start(g_all, True, name=name))
        return dx + in_flight[key][1][4][0, 0]

    def grads_done(l, gW, dh):
        return scatter("last", packed_grads(gW, PACK), dh, "grads_last_start") if l == last else dh

    def ffn_grads_done(l, gW, dx):
        if l != 0:
            return dx
        return scatter("ffn0", packed_grads({k: gW[k] for k in ("up_u", "up_g", "down")}, FFN), dx, "grads_ffn0_start")

    Ps = [small_params(full, l) for l in range(DEPTH)]
    Ps[0]["norm_mix_g"] = Ps[0]["norm_mix_g"] + started
    loss, grad_x, gWs, gPs, d_rel, d_final = local_step(x, mem, loss_target, weights_of, Ps, rel_bias,
                                                        final_norm_g.reshape(1, -1), grads_done, ffn_grads_done)

    shard_grads = {last: unpack_grads(scattered_sum(*in_flight["last"], grad_x, "grads_last"), PACK)}
    shard_grads[0] = unpack_grads(scattered_sum(*in_flight["ffn0"], grad_x, "grads_ffn0"), FFN)

    g_all = packed_grads({k: gWs[0][k] for k, _, _ in MIX}, MIX)
    rows = g_all.shape[1]
    got = pair_exchange(g_all, name="grads_pair_exchange")
    own = lax.dynamic_index_in_dim(g_all.reshape(N_CHIPS, 2, rows, PACK_W), lax.axis_index("c"), axis=1, keepdims=False)
    pair = sum_cast([own.reshape(-1, PACK_W), got.reshape(-1, PACK_W)], GRAD_WIRE, name="grads_pair_sum").reshape(N_CHIPS, rows, PACK_W)
    from_x, from_y, from_xy = chip_exchange(pair, name="grads_chip_exchange")
    mine = lax.dynamic_index_in_dim(pair, 2 * lax.axis_index("x") + lax.axis_index("y"), axis=0, keepdims=False)
    shard_grads[0].update(unpack_grads(sum_cast([mine, from_x, from_y, from_xy], F32, name="grads_chip_sum"), MIX))

    grads = {}
    per_layer = [native_grads(shard_grads[l]) for l in range(DEPTH)]
    for i, n in enumerate(LARGE):
        grads[n] = jnp.stack([per_layer[l][i] for l in range(DEPTH)])

    small_names = [n for n in WEIGHTS if n not in LARGE and n not in ("rel_bias", "final_norm_g")]
    pieces = [gPs[l][n].reshape(-1) for n in small_names for l in range(DEPTH)] + [d_rel.reshape(-1), d_final.reshape(-1), loss[0, :1]]
    sizes = [p.shape[0] for p in pieces]
    _, total = allgather_small(_pad_rows(jnp.concatenate(pieces)), name="allreduce_small", reduce=True)
    total = total.reshape(-1)
    off, it = 0, iter(sizes)
    for n in small_names:
        per = []
        for l in range(DEPTH):
            sz = next(it)
            per.append(total[off:off + sz])
            off += sz
        full_shape = (DEPTH,) + full[n].shape[1:]
        gfull = jnp.stack(per).reshape(full_shape)
        if n in COLUMN_SPLIT_SMALL:
            c = w[n].shape[-1]
            gfull = lax.dynamic_slice_in_dim(gfull, me * c, c, axis=gfull.ndim - 1)
        grads[n] = gfull
    grads["rel_bias"] = total[off:off + rel_bias.size].reshape(rel_bias.shape)
    off += rel_bias.size
    grads["final_norm_g"] = total[off:off + D_MODEL]
    off += D_MODEL
    loss_out = total[off]

    delta, new_m, new_v = {}, {}, {}
    for n in LARGE:
        shape = w[n].shape
        two_d = lambda a: a.reshape(-1, shape[-1])
        d_, m_, v_ = adamw(two_d(w[n]), two_d(grads[n]), two_d(m[n]), two_d(v[n]), name=f"adamw_{n}")
        delta[n], new_m[n], new_v[n] = d_.reshape(shape), m_.reshape(shape), v_.reshape(shape)
    small_all = [n for n in WEIGHTS if n not in LARGE]
    two_d = lambda a: a.reshape(-1, a.shape[-1])
    d_, m_, v_ = adamw_many(*[[two_d(src[n]) for n in small_all] for src in (w, grads, m, v)], name="adamw_small")
    for i, n in enumerate(small_all):
        delta[n], new_m[n], new_v[n] = (a[i].reshape(w[n].shape) for a in (d_, m_, v_))

    return (loss_out, grad_x, *[grads[n] for n in WEIGHTS], *[delta[n] for n in WEIGHTS], *[new_m[n] for n in WEIGHTS],
            *[new_v[n] for n in WEIGHTS])
```

```python
import functools
import math

import numpy as np
import jax
import jax.numpy as jnp
from jax import lax
from jax.experimental import pallas as pl
from jax.experimental.pallas import tpu as pltpu

F32 = jnp.float32
BF16 = jnp.bfloat16
MESH = pl.DeviceIdType.MESH

N_DEV = 8
D_MODEL = 1024
SEQ = 2048
DEPTH = 2
HEAD_DIM = 64
N_HEADS = 4
GROUP_W = N_HEADS * HEAD_DIM
D_FF = 2816
N_MEM = 256
NUM_BUCKETS = 32
MAX_DISTANCE = 2048
BLOCK = 128
DILATIONS = (1, 4, 16)
EPS = 1e-6
LRU_C = 8.0
Q_SCALE = HEAD_DIM ** -0.5
AUX_W = 640
LRU_HALF_W = 128
LRU_HALVES = GROUP_W // LRU_HALF_W
ADAM_LR, ADAM_B1, ADAM_B2, ADAM_EPS, ADAM_WD, ADAM_STEP = 0.001, 0.9, 0.999, 1e-08, 0.01, 10

VMEM_LIMIT_V7X = 48 * 1024 * 1024


def _params(*sem):
    return pltpu.CompilerParams(dimension_semantics=sem if sem else None, vmem_limit_bytes=VMEM_LIMIT_V7X)


def _pick(n, cands):
    for c in cands:
        if n % c == 0:
            return c
    return n


def _largest_tile(n, cap, align):
    best = None
    for t in range(align, min(n, cap) + 1, align):
        if n % t == 0:
            best = t
    return n if best is None else best


def matmul(a, b, *, name, trans_a=False, trans_b=False, out_dtype=F32, residual=None):
    (K, M) = a.shape if trans_a else a.shape[::-1]
    (N, Kb) = b.shape if trans_b else b.shape[::-1]
    assert K == Kb, (a.shape, b.shape)
    tm = _largest_tile(M, 1024 if trans_a else 512, 128)
    tn = _largest_tile(N, 1408, 128)
    tk = _largest_tile(K, 2816, 128)
    nk = K // tk
    a_spec = pl.BlockSpec((tk, tm), lambda i, j, k: (k, i)) if trans_a else pl.BlockSpec((tm, tk), lambda i, j, k: (i, k))
    b_spec = pl.BlockSpec((tn, tk), lambda i, j, k: (j, k)) if trans_b else pl.BlockSpec((tk, tn), lambda i, j, k: (k, j))
    o_spec = pl.BlockSpec((tm, tn), lambda i, j, k: (i, j))
    dims = (((0 if trans_a else 1,), (1 if trans_b else 0,)), ((), ()))
    has_res = residual is not None

    def body(*refs):
        a_ref, b_ref = refs[0], refs[1]
        r_ref = refs[2] if has_res else None
        part = lax.dot_general(a_ref[...].astype(BF16), b_ref[...].astype(BF16), dims, preferred_element_type=F32)
        if nk == 1:
            if has_res:
                part = part + r_ref[...].astype(F32)
            refs[-1][...] = part.astype(out_dtype)
            return
        o_ref, acc_ref = refs[-2], refs[-1]
        k = pl.program_id(2)

        @pl.when(k == 0)
        def _():
            acc_ref[...] = part

        @pl.when(k > 0)
        def _():
            acc_ref[...] += part

        @pl.when(k == nk - 1)
        def _():
            r = acc_ref[...]
            if has_res:
                r = r + r_ref[...].astype(F32)
            o_ref[...] = r.astype(out_dtype)

    ops = (a, b) + ((residual,) if has_res else ())
    return pl.pallas_call(
        body, name=name, grid=(M // tm, N // tn, nk),
        in_specs=[a_spec, b_spec] + ([o_spec] if has_res else []),
        out_specs=o_spec, out_shape=jax.ShapeDtypeStruct((M, N), out_dtype),
        scratch_shapes=[pltpu.VMEM((tm, tn), F32)] if nk > 1 else [],
        compiler_params=_params("parallel", "parallel", "arbitrary"),
    )(*ops)


def rmsnorm_fwd(x, g, *, name):
    R, D = x.shape
    tr = _pick(R, (512, 256))

    def body(x_ref, g_ref, o_ref):
        xv = x_ref[...]
        r = lax.rsqrt(jnp.mean(xv * xv, axis=-1, keepdims=True) + EPS)
        o_ref[...] = (xv * r * g_ref[...]).astype(BF16)

    return pl.pallas_call(
        body, name=name, grid=(R // tr,),
        in_specs=[pl.BlockSpec((tr, D), lambda i: (i, 0)), pl.BlockSpec((1, D), lambda i: (0, 0))],
        out_specs=pl.BlockSpec((tr, D), lambda i: (i, 0)), out_shape=jax.ShapeDtypeStruct((R, D), BF16),
        compiler_params=_params("parallel"),
    )(x, g)


def rmsnorm_bwd(x, g, dh, dres, *, name):
    R, D = x.shape
    tr = _pick(R, (512, 256))
    has_res = dres is not None

    def body(*refs):
        x_ref, g_ref, dh_ref = refs[:3]
        dx_ref, dg_ref = refs[-2], refs[-1]
        xv = x_ref[...]
        r = lax.rsqrt(jnp.mean(xv * xv, axis=-1, keepdims=True) + EPS)
        n = xv * r
        dhv = dh_ref[...]
        dn = dhv * g_ref[...]
        dx = r * (dn - n * jnp.mean(dn * n, axis=-1, keepdims=True))
        if has_res:
            dx = dx + refs[3][...]
        dx_ref[...] = dx
        part = jnp.sum(dhv * n, axis=0, keepdims=True)

        @pl.when(pl.program_id(0) == 0)
        def _():
            dg_ref[...] = part

        @pl.when(pl.program_id(0) > 0)
        def _():
            dg_ref[...] += part

    row = pl.BlockSpec((tr, D), lambda i: (i, 0))
    vec = pl.BlockSpec((1, D), lambda i: (0, 0))
    ops = (x, g, dh) + ((dres,) if has_res else ())
    return pl.pallas_call(
        body, name=name, grid=(R // tr,),
        in_specs=[row, vec, row] + ([row] if has_res else []),
        out_specs=[row, vec],
        out_shape=[jax.ShapeDtypeStruct((R, D), F32), jax.ShapeDtypeStruct((1, D), F32)],
        compiler_params=_params("arbitrary"),
    )(*ops)


_SQRT_HALF = 0.7071067811865476
_INV_SQRT_2PI = 0.3989422804014327


def _erf(x):
    ax = jnp.abs(x)
    t = 1.0 / (1.0 + 0.3275911 * ax)
    poly = t * (0.254829592 + t * (-0.284496736 + t * (1.421413741 + t * (-1.453152027 + t * 1.061405429))))
    y = 1.0 - poly * jnp.exp(-ax * ax)
    return jnp.where(x < 0, -y, y)


def _gelu_cdf(x):
    return 0.5 * (1.0 + _erf(x * _SQRT_HALF))


def _gelu_and_grad(x):
    cdf = _gelu_cdf(x)
    return x * cdf, cdf + x * _INV_SQRT_2PI * jnp.exp(-0.5 * x * x)


def _shift_down(main, halo, first, shifts):
    halo = jnp.where(first, 0.0, halo)
    ext = jnp.concatenate([halo, main], axis=0)
    return [pltpu.roll(ext, s, 0)[8:] for s in shifts]


def _conv3(main, halo, first, w, b):
    m1, m2 = _shift_down(main, halo, first, (1, 2))
    return ((b + w[0:1] * m2) + w[1:2] * m1) + w[2:3] * main, m1, m2


def glu_fwd(hu, hg, wu, wg, bu, bg, *, name):
    T, F = hu.shape
    tm, tf = 512, _largest_tile(F, 704, 128)
    hb = tm // 8
    blocks_per_example = SEQ // tm

    def body(hu_ref, hg_ref, hau_ref, hag_ref, wu_ref, wg_ref, bu_ref, bg_ref, o_ref):
        first = pl.program_id(0) % blocks_per_example == 0
        up, _, _ = _conv3(hu_ref[...], hau_ref[...], first, wu_ref[...], bu_ref[...])
        gate, _, _ = _conv3(hg_ref[...], hag_ref[...], first, wg_ref[...], bg_ref[...])
        o_ref[...] = (gate * _gelu_cdf(gate) * up).astype(BF16)

    main = pl.BlockSpec((tm, tf), lambda i, j: (i, j))
    halo = pl.BlockSpec((8, tf), lambda i, j: (jnp.maximum(i * hb - 1, 0), j))
    w3 = pl.BlockSpec((3, tf), lambda i, j: (0, j))
    b1 = pl.BlockSpec((1, tf), lambda i, j: (0, j))
    return pl.pallas_call(
        body, name=name, grid=(T // tm, F // tf),
        in_specs=[main, main, halo, halo, w3, w3, b1, b1],
        out_specs=main, out_shape=jax.ShapeDtypeStruct((T, F), BF16),
        compiler_params=_params("parallel", "parallel"),
    )(hu, hg, hu, hg, wu, wg, bu, bg)


def glu_bwd(hu, hg, dact, wu, wg, bu, bg, *, name):
    T, F = hu.shape
    tm, tf = 512, _largest_tile(F, 704, 128)
    hb = tm // 8
    blocks_per_example = SEQ // tm
    n_halo_blocks = T // 8
    n_ext = tm + 8

    def body(hu_ref, hg_ref, hau_ref, hag_ref, hnu_ref, hng_ref, da_ref, dan_ref, wu_ref, wg_ref, bu_ref, bg_ref,
             du_ref, dg_ref, dwu_ref, dwg_ref, dbu_ref, dbg_ref):
        i = pl.program_id(1)
        first = i % blocks_per_example == 0
        last = i % blocks_per_example == blocks_per_example - 1
        wu, wg = wu_ref[...], wg_ref[...]

        def conv_ext(main_ref, prev_ref, next_ref, w, b):
            ext = jnp.concatenate([jnp.where(first, 0.0, prev_ref[...]), main_ref[...], next_ref[...]], axis=0)
            x0, x1, x2 = ext[8:], pltpu.roll(ext, 1, 0)[8:], pltpu.roll(ext, 2, 0)[8:]
            return ((b + w[0:1] * x2) + w[1:2] * x1) + w[2:3] * x0, x0, x1, x2

        up, xu, u1, u2 = conv_ext(hu_ref, hau_ref, hnu_ref, wu, bu_ref[...])
        gate, xg, g1, g2 = conv_ext(hg_ref, hag_ref, hng_ref, wg, bg_ref[...])
        act, dact_dgate = _gelu_and_grad(gate)
        da = jnp.concatenate([da_ref[...], jnp.where(last, 0.0, dan_ref[...])], axis=0)
        dup = da * act
        dgate = da * up * dact_dgate

        def conv_t(d, w):
            return (w[2:3] * d[:tm] + w[1:2] * pltpu.roll(d, n_ext - 1, 0)[:tm] + w[0:1] * pltpu.roll(d, n_ext - 2, 0)[:tm]).astype(BF16)

        du_ref[...] = conv_t(dup, wu)
        dg_ref[...] = conv_t(dgate, wg)

        def sums(d, x0, x1, x2):
            s = lambda v: jnp.sum(v[:tm], axis=0, keepdims=True)
            return jnp.concatenate([s(d * x2), s(d * x1), s(d * x0)], axis=0), s(d)

        pwu, pbu = sums(dup, xu, u1, u2)
        pwg, pbg = sums(dgate, xg, g1, g2)

        @pl.when(i == 0)
        def _():
            dwu_ref[...] = pwu
            dwg_ref[...] = pwg
            dbu_ref[...] = pbu
            dbg_ref[...] = pbg

        @pl.when(i > 0)
        def _():
            dwu_ref[...] += pwu
            dwg_ref[...] += pwg
            dbu_ref[...] += pbu
            dbg_ref[...] += pbg

    main = pl.BlockSpec((tm, tf), lambda j, i: (i, j))
    before = pl.BlockSpec((8, tf), lambda j, i: (jnp.maximum(i * hb - 1, 0), j))
    after = pl.BlockSpec((8, tf), lambda j, i: (jnp.minimum((i + 1) * hb, n_halo_blocks - 1), j))
    w3 = pl.BlockSpec((3, tf), lambda j, i: (0, j))
    b1 = pl.BlockSpec((1, tf), lambda j, i: (0, j))
    sd = jax.ShapeDtypeStruct
    return pl.pallas_call(
        body, name=name, grid=(F // tf, T // tm),
        in_specs=[main, main, before, before, after, after, main, after, w3, w3, b1, b1],
        out_specs=[main, main, w3, w3, b1, b1],
        out_shape=[sd((T, F), BF16), sd((T, F), BF16), sd((3, F), F32), sd((3, F), F32), sd((1, F), F32), sd((1, F), F32)],
        compiler_params=_params("parallel", "arbitrary"),
    )(hu, hg, hu, hg, hu, hg, dact, dact, wu, wg, bu, bg)


def loss_head(x, g, target, *, name):
    T, D = x.shape
    tr = 256

    def body(x_ref, g_ref, t_ref, loss_ref, dx_ref, dg_ref):
        xv = x_ref[...]
        gv = g_ref[...]
        r = lax.rsqrt(jnp.mean(xv * xv, axis=-1, keepdims=True) + EPS)
        n = xv * r
        err = n * gv - t_ref[...]
        part_loss = jnp.zeros((1, 128), F32) + 0.5 * jnp.sum(jnp.mean(err * err, axis=-1, keepdims=True))
        dy = err * (1.0 / D)
        dn = dy * gv
        dx_ref[...] = r * (dn - n * jnp.mean(dn * n, axis=-1, keepdims=True))
        part_g = jnp.sum(dy * n, axis=0, keepdims=True)

        @pl.when(pl.program_id(0) == 0)
        def _():
            loss_ref[...] = part_loss
            dg_ref[...] = part_g

        @pl.when(pl.program_id(0) > 0)
        def _():
            loss_ref[...] += part_loss
            dg_ref[...] += part_g

    row = pl.BlockSpec((tr, D), lambda i: (i, 0))
    vec = pl.BlockSpec((1, D), lambda i: (0, 0))
    sd = jax.ShapeDtypeStruct
    return pl.pallas_call(
        body, name=name, grid=(T // tr,),
        in_specs=[row, vec, row],
        out_specs=[pl.BlockSpec((1, 128), lambda i: (0, 0)), row, vec],
        out_shape=[sd((1, 128), F32), sd((T, D), F32), sd((1, D), F32)],
        compiler_params=_params("arbitrary"),
    )(x, g, target)


def adamw(w, g, m, v, *, name):
    R, C = w.shape
    tr = _pick(R, (256, 128, 64, 32, 16, 8))

    def body(w_ref, g_ref, m_ref, v_ref, d_ref, nm_ref, nv_ref):
        gv = g_ref[...]
        mn = ADAM_B1 * m_ref[...] + (1.0 - ADAM_B1) * gv
        vn = ADAM_B2 * v_ref[...] + (1.0 - ADAM_B2) * (gv * gv)
        m_hat = mn / (1.0 - ADAM_B1 ** ADAM_STEP)
        v_hat = vn / (1.0 - ADAM_B2 ** ADAM_STEP)
        d_ref[...] = -ADAM_LR * (m_hat / (jnp.sqrt(v_hat) + ADAM_EPS) + ADAM_WD * w_ref[...])
        nm_ref[...] = mn
        nv_ref[...] = vn

    blk = pl.BlockSpec((tr, C), lambda i: (i, 0))
    sd = jax.ShapeDtypeStruct((R, C), F32)
    return pl.pallas_call(
        body, name=name, grid=(R // tr,), in_specs=[blk] * 4, out_specs=[blk] * 3, out_shape=[sd] * 3,
        compiler_params=_params("parallel"),
    )(w, g, m, v)


def adamw_many(ws, gs, ms, vs, *, name):
    n = len(ws)

    def body(*refs):
        ins, outs = refs[:4 * n], refs[4 * n:]
        for i in range(n):
            w_ref, g_ref, m_ref, v_ref = ins[i], ins[n + i], ins[2 * n + i], ins[3 * n + i]
            gv = g_ref[...]
            mn = ADAM_B1 * m_ref[...] + (1.0 - ADAM_B1) * gv
            vn = ADAM_B2 * v_ref[...] + (1.0 - ADAM_B2) * (gv * gv)
            m_hat = mn / (1.0 - ADAM_B1 ** ADAM_STEP)
            v_hat = vn / (1.0 - ADAM_B2 ** ADAM_STEP)
            outs[i][...] = -ADAM_LR * (m_hat / (jnp.sqrt(v_hat) + ADAM_EPS) + ADAM_WD * w_ref[...])
            outs[n + i][...] = mn
            outs[2 * n + i][...] = vn

    vm = pl.BlockSpec(memory_space=pltpu.VMEM)
    shapes = [jax.ShapeDtypeStruct(w.shape, F32) for w in ws]
    res = pl.pallas_call(
        body, name=name, in_specs=[vm] * (4 * n), out_specs=[vm] * (3 * n), out_shape=shapes * 3, compiler_params=_params(),
    )(*ws, *gs, *ms, *vs)
    return res[:n], res[n:2 * n], res[2 * n:]


def _softplus(x):
    return jnp.maximum(x, 0.0) + jnp.log(1.0 + jnp.exp(-jnp.abs(x)))


def _lru_gates(x, cw, cb, wa, ba, wx, bx, lam):
    S = x.shape[0]
    row = lax.broadcasted_iota(jnp.int32, (S, 1), 0)

    def back(s):
        return jnp.where(row >= s, pltpu.roll(x, s, 0), 0.0)

    xc = (((cb + cw[0:1] * back(3)) + cw[1:2] * back(2)) + cw[2:3] * back(1)) + cw[3:4] * x
    xb = xc.astype(BF16)
    r = jax.nn.sigmoid(jnp.dot(xb, wa, preferred_element_type=F32) + ba)
    ig = jax.nn.sigmoid(jnp.dot(xb, wx, preferred_element_type=F32) + bx)
    sp = _softplus(-lam)
    la = -LRU_C * r * sp
    a = jnp.exp(la)
    y = 2.0 * la
    one_minus_a2 = jnp.where(y > -0.05, -y * (1.0 + y * (0.5 + y * (1.0 / 6.0 + y * (1.0 / 24.0)))), 1.0 - jnp.exp(y))
    mm = jnp.sqrt(one_minus_a2)
    return xc, xb, r, ig, sp, a, mm


def lru_fwd(aux, cw, cb, wa, ba, wx, bx, lam, *, name):
    T = aux.shape[0]
    S, C = SEQ, LRU_HALF_W

    def body(x_ref, g_ref, cw_ref, cb_ref, wa_ref, ba_ref, wx_ref, bx_ref, lam_ref, o_ref, h_ref, a_s, u_s):
        xc, _, r, ig, sp, a, mm = _lru_gates(x_ref[...], cw_ref[...], cb_ref[...], wa_ref[...], ba_ref[...],
                                             wx_ref[...], bx_ref[...], lam_ref[...])
        a_s[...] = a
        u_s[...] = mm * (ig * xc)

        def group(i, h):
            base = pl.multiple_of(i * 8, 8)
            a8 = a_s[pl.ds(base, 8), :]
            u8 = u_s[pl.ds(base, 8), :]
            for rr in range(8):
                h = a8[rr:rr + 1] * h + u8[rr:rr + 1]
                h_ref[pl.ds(base + rr, 1), :] = h
            return h

        lax.fori_loop(0, S // 8, group, jnp.zeros((1, C), F32))
        gate = g_ref[...]
        o_ref[...] = (h_ref[...] * (gate * _gelu_cdf(gate))).astype(BF16)

    blk = lambda col: pl.BlockSpec((S, C), lambda c, b: (b, col + c))
    par = lambda rows: pl.BlockSpec((rows, C), lambda c, b: (0, c))
    sq = pl.BlockSpec((None, C, C), lambda c, b: (c, 0, 0))
    sd = jax.ShapeDtypeStruct
    W = LRU_HALVES * C
    return pl.pallas_call(
        body, name=name, grid=(LRU_HALVES, T // S),
        in_specs=[blk(0), blk(LRU_HALVES), par(4), par(1), sq, par(1), sq, par(1), par(1)],
        out_specs=[blk(0), blk(0)], out_shape=[sd((T, W), BF16), sd((T, W), F32)],
        scratch_shapes=[pltpu.VMEM((S, C), F32), pltpu.VMEM((S, C), F32)],
        compiler_params=_params("parallel", "parallel"),
    )(aux, aux, cw, cb, wa, ba, wx, bx, lam)


def lru_bwd(aux, h, dmixed, cw, cb, wa, ba, wx, bx, lam, *, name):
    T = aux.shape[0]
    S, C = SEQ, LRU_HALF_W

    def body(x_ref, g_ref, h_ref, do_ref, cw_ref, cb_ref, wa_ref, ba_ref, wx_ref, bx_ref, lam_ref,
             dx_ref, dgate_ref, dcw_ref, dcb_ref, dwa_ref, dba_ref, dwx_ref, dbx_ref, dlam_ref, a_s, d_s):
        x = x_ref[...]
        cw = cw_ref[...]
        lam = lam_ref[...]
        xc, xb, r, ig, sp, a, mm = _lru_gates(x, cw, cb_ref[...], wa_ref[...], ba_ref[...], wx_ref[...], bx_ref[...], lam)
        gate = g_ref[...]
        gl, dgl = _gelu_and_grad(gate)
        dout = do_ref[...]
        hv = h_ref[...]
        dgate_ref[...] = dout * hv * dgl
        a_s[...] = a
        d_s[...] = dout * gl

        def group(i, c):
            base = pl.multiple_of((S // 8 - 1 - i) * 8, 8)
            a8 = a_s[pl.ds(base, 8), :]
            d8 = d_s[pl.ds(base, 8), :]
            for rr in range(7, -1, -1):
                d = d8[rr:rr + 1] + c
                d_s[pl.ds(base + rr, 1), :] = d
                c = a8[rr:rr + 1] * d
            return c

        lax.fori_loop(0, S // 8, group, jnp.zeros((1, C), F32))
        row = lax.broadcasted_iota(jnp.int32, (S, 1), 0)
        dht = d_s[...]
        h_prev = jnp.where(row >= 1, pltpu.roll(hv, 1, 0), 0.0)
        da = dht * h_prev
        gx = ig * xc
        dmm = dht * gx
        dig = dht * mm * xc
        dxc = dht * mm * ig
        dla = da * a - dmm * (a * a) / mm
        dr = dla * (-LRU_C * sp)
        dsp = jnp.sum(dla * (-LRU_C * r), axis=0, keepdims=True)
        dlam = dsp * (-jax.nn.sigmoid(-lam))
        dpa = dr * r * (1.0 - r)
        dpx = dig * ig * (1.0 - ig)
        dpa_b, dpx_b = dpa.astype(BF16), dpx.astype(BF16)
        nt = (((1,), (1,)), ((), ()))
        tn = (((0,), (0,)), ((), ()))
        dxc = dxc + lax.dot_general(dpa_b, wa_ref[...], nt, preferred_element_type=F32) \
                  + lax.dot_general(dpx_b, wx_ref[...], nt, preferred_element_type=F32)
        dwa = lax.dot_general(xb, dpa_b, tn, preferred_element_type=F32)
        dwx = lax.dot_general(xb, dpx_b, tn, preferred_element_type=F32)

        def fwd(v, s):
            return jnp.where(row < S - s, pltpu.roll(v, S - s, 0), 0.0)

        def back(v, s):
            return jnp.where(row >= s, pltpu.roll(v, s, 0), 0.0)

        dx_ref[...] = cw[3:4] * dxc + cw[2:3] * fwd(dxc, 1) + cw[1:2] * fwd(dxc, 2) + cw[0:1] * fwd(dxc, 3)
        s0 = lambda v: jnp.sum(v, axis=0, keepdims=True)
        dcw = jnp.concatenate([s0(dxc * back(x, 3)), s0(dxc * back(x, 2)), s0(dxc * back(x, 1)), s0(dxc * x)], axis=0)
        parts = ((dcw_ref, dcw), (dcb_ref, s0(dxc)), (dwa_ref, dwa), (dba_ref, s0(dpa)), (dwx_ref, dwx),
                 (dbx_ref, s0(dpx)), (dlam_ref, dlam))

        @pl.when(pl.program_id(1) == 0)
        def _():
            for ref, val in parts:
                ref[...] = val

        @pl.when(pl.program_id(1) > 0)
        def _():
            for ref, val in parts:
                ref[...] += val

    blk = lambda col: pl.BlockSpec((S, C), lambda c, b: (b, col + c))
    par = lambda rows: pl.BlockSpec((rows, C), lambda c, b: (0, c))
    sq = pl.BlockSpec((None, C, C), lambda c, b: (c, 0, 0))
    sd = jax.ShapeDtypeStruct
    W = LRU_HALVES * C
    vec = sd((1, W), F32)
    return pl.pallas_call(
        body, name=name, grid=(LRU_HALVES, T // S),
        in_specs=[blk(0), blk(LRU_HALVES), blk(0), blk(3 * LRU_HALVES), par(4), par(1), sq, par(1), sq, par(1), par(1)],
        out_specs=[blk(0), blk(0), par(4), par(1), sq, par(1), sq, par(1), par(1)],
        out_shape=[sd((T, W), F32), sd((T, W), F32), sd((4, W), F32), vec, sd((LRU_HALVES, C, C), F32), vec,
                   sd((LRU_HALVES, C, C), F32), vec, vec],
        scratch_shapes=[pltpu.VMEM((S, C), F32), pltpu.VMEM((S, C), F32)],
        compiler_params=_params("parallel", "arbitrary"),
    )(aux, aux, h, dmixed, cw, cb, wa, ba, wx, bx, lam)


_NT = (((1,), (1,)), ((), ()))
_TN = (((0,), (0,)), ((), ()))


def _dot(a, b, dims=None):
    if dims is None:
        return jnp.dot(a, b, preferred_element_type=F32)
    return lax.dot_general(a, b, dims, preferred_element_type=F32)


def _hs(h):
    return slice(h * HEAD_DIM, (h + 1) * HEAD_DIM)


def cross_fwd(q, kv, *, name):
    T = q.shape[0]
    tq = 512

    def body(q_ref, kv_ref, o_ref):
        for h in range(N_HEADS):
            qh = q_ref[:, _hs(h)] * Q_SCALE
            k = kv_ref[:, _hs(h)]
            v = kv_ref[:, GROUP_W + h * HEAD_DIM:GROUP_W + (h + 1) * HEAD_DIM]
            s = _dot(qh, k, _NT)
            p = jnp.exp(s - jnp.max(s, axis=-1, keepdims=True))
            p = p / jnp.sum(p, axis=-1, keepdims=True)
            o_ref[:, _hs(h)] = _dot(p.astype(BF16), v).astype(BF16)

    per = SEQ // tq
    return pl.pallas_call(
        body, name=name, grid=(T // tq,),
        in_specs=[pl.BlockSpec((tq, GROUP_W), lambda i: (i, 0)), pl.BlockSpec((N_MEM, 2 * GROUP_W), lambda i: (i // per, 0))],
        out_specs=pl.BlockSpec((tq, GROUP_W), lambda i: (i, 0)), out_shape=jax.ShapeDtypeStruct((T, GROUP_W), BF16),
        compiler_params=_params("parallel"),
    )(q, kv)


def cross_bwd(q, kv, do, *, name):
    T = q.shape[0]
    tq = 512
    per = SEQ // tq

    def body(q_ref, kv_ref, do_ref, dq_ref, dkv_ref):
        first = pl.program_id(0) % per == 0
        for h in range(N_HEADS):
            vs = slice(GROUP_W + h * HEAD_DIM, GROUP_W + (h + 1) * HEAD_DIM)
            qh = q_ref[:, _hs(h)] * Q_SCALE
            k = kv_ref[:, _hs(h)]
            v = kv_ref[:, vs]
            doh = do_ref[:, _hs(h)].astype(BF16)
            s = _dot(qh, k, _NT)
            p = jnp.exp(s - jnp.max(s, axis=-1, keepdims=True))
            p = p / jnp.sum(p, axis=-1, keepdims=True)
            dp = _dot(doh, v, _NT)
            ds = (p * (dp - jnp.sum(p * dp, axis=-1, keepdims=True))).astype(BF16)
            dq_ref[:, _hs(h)] = (_dot(ds, k) * Q_SCALE).astype(BF16)
            dk = _dot(ds, qh, _TN)
            dv = _dot(p.astype(BF16), doh, _TN)

            @pl.when(first)
            def _():
                dkv_ref[:, _hs(h)] = dk
                dkv_ref[:, vs] = dv

            @pl.when(jnp.logical_not(first))
            def _():
                dkv_ref[:, _hs(h)] += dk
                dkv_ref[:, vs] += dv

    qb = pl.BlockSpec((tq, GROUP_W), lambda i: (i, 0))
    kvb = pl.BlockSpec((N_MEM, 2 * GROUP_W), lambda i: (i // per, 0))
    sd = jax.ShapeDtypeStruct
    return pl.pallas_call(
        body, name=name, grid=(T // tq,),
        in_specs=[qb, kvb, qb], out_specs=[qb, kvb],
        out_shape=[sd((T, GROUP_W), BF16), sd(kv.shape, F32)],
        compiler_params=_params("arbitrary"),
    )(q, kv, do)


NB = SEQ // BLOCK
NEG = -1e30


def _split_dot(x, tri):
    hi = x.astype(BF16)
    lo = (x - hi.astype(F32)).astype(BF16)
    return _dot(hi, tri) + _dot(lo, tri)


def _blk(i):
    return pl.ds(pl.multiple_of(i * BLOCK, BLOCK), BLOCK)


def _iotas():
    row = lax.broadcasted_iota(jnp.int32, (BLOCK, BLOCK), 0)
    col = lax.broadcasted_iota(jnp.int32, (BLOCK, BLOCK), 1)
    return row, col


def _sb_scores(q, k, mask, later, csum, want_sigmoid=False):
    z = _dot(q, k, _NT)
    lk = -_softplus(z)
    if mask is not None:
        lk = jnp.where(mask, lk, 0.0)
    lka = _split_dot(lk, later) + csum
    att = jnp.exp(z + lk + lka)
    sg = jnp.exp(z + lk) if want_sigmoid else None
    if mask is not None:
        att = jnp.where(mask, att, 0.0)
        sg = jnp.where(mask, sg, 0.0) if want_sigmoid else None
    return att, sg, lk


def _rowsum(v):
    return jnp.sum(v, axis=1, keepdims=True)


HEADS = tuple(range(N_HEADS))


def _qkv_specs(first_col):
    return [pl.BlockSpec((SEQ, GROUP_W), lambda b, c=first_col + j: (b, c)) for j in range(3)]


LANES = 128
CUM_BLK = 256


def col_to_row(c):
    b = c.shape[0] // SEQ
    return c.reshape(b, SEQ, LANES)[:, :, :8].transpose(0, 2, 1).reshape(b * 8, SEQ)


def row_to_col(r):
    b = r.shape[0] // 8
    c = r.reshape(b, 8, SEQ).transpose(0, 2, 1)
    return jnp.pad(c, ((0, 0), (0, 0), (0, LANES - 8))).reshape(b * SEQ, LANES)


def fox_prep(aux, bf, *, name):
    T = aux.shape[0]

    def body(f_ref, b_ref, o_ref):
        row = lax.broadcasted_iota(jnp.int32, (CUM_BLK, CUM_BLK), 0)
        col = lax.broadcasted_iota(jnp.int32, (CUM_BLK, CUM_BLK), 1)
        upto = (col <= row).astype(BF16)
        carry = jnp.zeros((1, LANES), F32)
        for n in range(SEQ // CUM_BLK):
            rows = slice(n * CUM_BLK, (n + 1) * CUM_BLK)
            logf = -_softplus(-(f_ref[rows, :] + b_ref[...]))
            hi = logf.astype(BF16)
            lo = (logf - hi.astype(F32)).astype(BF16)
            cum = _dot(upto, hi) + _dot(upto, lo) + carry
            o_ref[rows, :] = cum
            carry = cum[CUM_BLK - 1:CUM_BLK]

    return pl.pallas_call(
        body, name=name, grid=(T // SEQ,),
        in_specs=[pl.BlockSpec((SEQ, LANES), lambda b: (b, 4)), pl.BlockSpec((1, LANES), lambda b: (0, 0))],
        out_specs=pl.BlockSpec((SEQ, LANES), lambda b: (b, 0)), out_shape=jax.ShapeDtypeStruct((T, LANES), F32),
        compiler_params=_params("parallel"),
    )(aux, bf)


def fox_prep_bwd(aux, bf, dcum, *, name):
    T = aux.shape[0]

    def body(f_ref, b_ref, d_ref, df_ref, db_ref):
        row = lax.broadcasted_iota(jnp.int32, (CUM_BLK, CUM_BLK), 0)
        col = lax.broadcasted_iota(jnp.int32, (CUM_BLK, CUM_BLK), 1)
        onward = (col >= row).astype(BF16)
        carry = jnp.zeros((1, LANES), F32)
        tot = jnp.zeros((1, LANES), F32)
        for n in range(SEQ // CUM_BLK - 1, -1, -1):
            rows = slice(n * CUM_BLK, (n + 1) * CUM_BLK)
            d = d_ref[rows, :]
            hi = d.astype(BF16)
            lo = (d - hi.astype(F32)).astype(BF16)
            dlogf = _dot(onward, hi) + _dot(onward, lo) + carry
            carry = dlogf[0:1]
            df = dlogf * jax.nn.sigmoid(-(f_ref[rows, :] + b_ref[...]))
            df_ref[rows, :] = df
            tot = tot + jnp.sum(df, axis=0, keepdims=True)

        @pl.when(pl.program_id(0) == 0)
        def _():
            db_ref[...] = tot

        @pl.when(pl.program_id(0) > 0)
        def _():
            db_ref[...] += tot

    blk = pl.BlockSpec((SEQ, LANES), lambda b: (b, 0))
    vec = pl.BlockSpec((1, LANES), lambda b: (0, 0))
    sd = jax.ShapeDtypeStruct
    return pl.pallas_call(
        body, name=name, grid=(T // SEQ,),
        in_specs=[pl.BlockSpec((SEQ, LANES), lambda b: (b, 4)), vec, blk],
        out_specs=[blk, vec], out_shape=[sd((T, LANES), F32), sd((1, LANES), F32)],
        compiler_params=_params("arbitrary"),
    )(aux, bf, dcum)


def _fox_logits(q, k, cq, ck, mask):
    z = _dot(q, k, _NT) + cq - ck
    return z if mask is None else jnp.where(mask, z, NEG)


def fox_fwd(qkv, cumc, cumr, *, name):
    T = qkv.shape[0]

    def body(q_ref, k_ref, v_ref, cc_ref, cr_ref, o_ref, lse_ref, z_s):
        row, col = _iotas()
        causal = col <= row
        lse_ref[...] = jnp.zeros_like(lse_ref)

        def qblock(i, _):
            qs = [q_ref[_blk(i), _hs(h)] * Q_SCALE for h in HEADS]
            cqs = [cc_ref[_blk(i), h:h + 1] for h in HEADS]

            def logits(j, mask, ms):
                out = []
                for h in HEADS:
                    z = _fox_logits(qs[h], k_ref[_blk(j), _hs(h)], cqs[h], cr_ref[h:h + 1, _blk(j)], mask)
                    z_s[h, j] = z
                    out.append(jnp.maximum(ms[h], jnp.max(z, axis=1, keepdims=True)))
                return tuple(out)

            ms = logits(i, causal, (jnp.full((BLOCK, 1), NEG, F32),) * N_HEADS)
            ms = lax.fori_loop(0, i, lambda j, c: logits(j, None, c), ms)

            def values(j, carry):
                out = []
                for h in HEADS:
                    acc, l = carry[h]
                    p = jnp.exp(z_s[h, j] - ms[h])
                    out.append((acc + _dot(p.astype(BF16), v_ref[_blk(j), _hs(h)]), l + _rowsum(p)))
                return tuple(out)

            zero = (jnp.zeros((BLOCK, HEAD_DIM), F32), jnp.zeros((BLOCK, 1), F32))
            res = lax.fori_loop(0, i + 1, values, (zero,) * N_HEADS)
            for h in HEADS:
                acc, l = res[h]
                o_ref[_blk(i), _hs(h)] = (acc / l).astype(BF16)
                lse_ref[_blk(i), h:h + 1] = ms[h] + jnp.log(l)
            return 0

        lax.fori_loop(0, NB, qblock, 0)

    out = pl.BlockSpec((SEQ, GROUP_W), lambda b: (b, 0))
    colb = pl.BlockSpec((SEQ, LANES), lambda b: (b, 0))
    sd = jax.ShapeDtypeStruct
    return pl.pallas_call(
        body, name=name, grid=(T // SEQ,),
        in_specs=_qkv_specs(3) + [colb, pl.BlockSpec((8, SEQ), lambda b: (b, 0))],
        out_specs=[out, colb], out_shape=[sd((T, GROUP_W), BF16), sd((T, LANES), F32)],
        scratch_shapes=[pltpu.VMEM((N_HEADS, NB, BLOCK, BLOCK), F32)],
        compiler_params=_params("parallel"),
    )(qkv, qkv, qkv, cumc, cumr)


def fox_bwd(qkv, cumc, cumr, lse, dmixed, *, name):
    T = qkv.shape[0]

    def body(q_ref, k_ref, v_ref, cc_ref, cr_ref, lse_ref, do_ref, dq_ref, dk_ref, dv_ref, dcc_ref, dcr_ref, p_s, dp_s):
        row, col = _iotas()
        causal = col <= row
        dk_ref[...] = jnp.zeros_like(dk_ref)
        dv_ref[...] = jnp.zeros_like(dv_ref)
        dcc_ref[...] = jnp.zeros_like(dcc_ref)
        dcr_ref[...] = jnp.zeros_like(dcr_ref)

        def qblock(i, _):
            qs = [q_ref[_blk(i), _hs(h)] * Q_SCALE for h in HEADS]
            dos = [do_ref[_blk(i), _hs(h)].astype(BF16) for h in HEADS]
            cqs = [cc_ref[_blk(i), h:h + 1] for h in HEADS]
            lses = [lse_ref[_blk(i), h:h + 1] for h in HEADS]

            def probs(j, mask, deltas):
                out = []
                for h in HEADS:
                    z = _fox_logits(qs[h], k_ref[_blk(j), _hs(h)], cqs[h], cr_ref[h:h + 1, _blk(j)], mask)
                    p = jnp.exp(z - lses[h])
                    dp = _dot(dos[h], v_ref[_blk(j), _hs(h)], _NT)
                    p_s[h, j] = p
                    dp_s[h, j] = dp
                    out.append(deltas[h] + _rowsum(p * dp))
                return tuple(out)

            deltas = probs(i, causal, (jnp.zeros((BLOCK, 1), F32),) * N_HEADS)
            deltas = lax.fori_loop(0, i, lambda j, c: probs(j, None, c), deltas)

            def kblock(j, carry):
                out = []
                for h in HEADS:
                    dq, dcq = carry[h]
                    p = p_s[h, j]
                    ds = p * (dp_s[h, j] - deltas[h])
                    dsb = ds.astype(BF16)
                    dk_ref[_blk(j), _hs(h)] += _dot(dsb, qs[h], _TN)
                    dv_ref[_blk(j), _hs(h)] += _dot(p.astype(BF16), dos[h], _TN)
                    dcr_ref[h:h + 1, _blk(j)] -= jnp.sum(ds, axis=0, keepdims=True)
                    out.append((dq + _dot(dsb, k_ref[_blk(j), _hs(h)]), dcq + _rowsum(ds)))
                return tuple(out)

            zero = (jnp.zeros((BLOCK, HEAD_DIM), F32), jnp.zeros((BLOCK, 1), F32))
            res = lax.fori_loop(0, i + 1, kblock, (zero,) * N_HEADS)
            for h in HEADS:
                dq_ref[_blk(i), _hs(h)] = res[h][0] * Q_SCALE
                dcc_ref[_blk(i), h:h + 1] = res[h][1]
            return 0

        lax.fori_loop(0, NB, qblock, 0)

    out = pl.BlockSpec((SEQ, GROUP_W), lambda b: (b, 0))
    colb = pl.BlockSpec((SEQ, LANES), lambda b: (b, 0))
    rowb = pl.BlockSpec((8, SEQ), lambda b: (b, 0))
    sd = jax.ShapeDtypeStruct
    big = sd((T, GROUP_W), F32)
    return pl.pallas_call(
        body, name=name, grid=(T // SEQ,),
        in_specs=_qkv_specs(3) + [colb, rowb, colb, pl.BlockSpec((SEQ, GROUP_W), lambda b: (b, 1))],
        out_specs=[out, out, out, colb, rowb],
        out_shape=[big, big, big, sd((T, LANES), F32), sd((T // SEQ * 8, SEQ), F32)],
        scratch_shapes=[pltpu.VMEM((N_HEADS, NB, BLOCK, BLOCK), F32), pltpu.VMEM((N_HEADS, NB, BLOCK, BLOCK), F32)],
        compiler_params=_params("parallel"),
    )(qkv, qkv, qkv, cumc, cumr, lse, dmixed)


CHUNK = 256
WIDE = N_HEADS * CHUNK
NCH = SEQ // CHUNK


def _seg(h):
    return slice(h * CHUNK, (h + 1) * CHUNK)


def _chunk_rows(c):
    return pl.ds(pl.multiple_of(c * CHUNK, CHUNK), CHUNK)


def _wide_consts():
    r = lax.broadcasted_iota(jnp.int32, (WIDE, GROUP_W), 0)
    f = lax.broadcasted_iota(jnp.int32, (WIDE, GROUP_W), 1)
    bd = (r // CHUNK) == (f // HEAD_DIM)
    row = lax.broadcasted_iota(jnp.int32, (BLOCK, WIDE), 0)
    key = lax.broadcasted_iota(jnp.int32, (BLOCK, WIDE), 1) % CHUNK
    return bd, row, key


def _block_diag(x, bd):
    return jnp.where(bd, jnp.concatenate([x] * N_HEADS, axis=0), jnp.zeros((), x.dtype))


def _fold_heads(w, bd):
    w = jnp.where(bd, w, 0.0)
    return (w[0:CHUNK] + w[CHUNK:2 * CHUNK]) + (w[2 * CHUNK:3 * CHUNK] + w[3 * CHUNK:])


def _widen(cols):
    return jnp.concatenate([jnp.broadcast_to(c, (BLOCK, CHUNK)) for c in cols], axis=1)


def _head_rowsums(w):
    return [jnp.sum(w[:, _seg(h)], axis=1, keepdims=True) for h in HEADS]


def _tri_wide(x, tri):
    hi = x.astype(BF16)
    lo = (x - hi.astype(F32)).astype(BF16)
    y = _dot(jnp.concatenate([hi[:, _seg(h)] for h in HEADS] + [lo[:, _seg(h)] for h in HEADS], axis=0), tri)
    return jnp.concatenate([y[h * BLOCK:(h + 1) * BLOCK] + y[(N_HEADS + h) * BLOCK:(N_HEADS + h + 1) * BLOCK] for h in HEADS], axis=1)


def _feature_widen(cols):
    return jnp.concatenate([jnp.broadcast_to(c, (BLOCK, HEAD_DIM)) for c in cols], axis=1)


def _sbw_tile(q, kbd, mask, later, csum):
    z = _dot(q, kbd, _NT)
    lk = -_softplus(z)
    if mask is not None:
        lk = jnp.where(mask, lk, 0.0)
    e = z + lk
    att = jnp.exp(e + _tri_wide(lk, later) + csum)
    if mask is not None:
        att = jnp.where(mask, att, 0.0)
    return att, e, lk


def sbw_fwd(qkv, *, name):
    T = qkv.shape[0]

    def body(q_ref, k_ref, v_ref, o_ref):
        bd, row, key = _wide_consts()
        r2 = lax.broadcasted_iota(jnp.int32, (CHUNK, CHUNK), 0)
        c2 = lax.broadcasted_iota(jnp.int32, (CHUNK, CHUNK), 1)
        later = (r2 > c2).astype(BF16)

        def qblock(i, _):
            q = q_ref[_blk(i), :] * Q_SCALE
            cd = i // 2
            strict = key < row + BLOCK * (i % 2)

            def tile(c, mask, carry):
                acc, csum = carry
                att, _, lk = _sbw_tile(q, _block_diag(k_ref[_chunk_rows(c), :], bd), mask, later, csum)
                acc = acc + _dot(att.astype(BF16), _block_diag(v_ref[_chunk_rows(c), :], bd))
                return acc, csum + _widen(_head_rowsums(lk))

            carry = tile(cd, strict, (jnp.zeros((BLOCK, GROUP_W), F32), jnp.zeros((BLOCK, WIDE), F32)))
            acc, _ = lax.fori_loop(0, cd, lambda n, cr: tile(cd - 1 - n, None, cr), carry)
            o_ref[_blk(i), :] = acc.astype(BF16)
            return 0

        lax.fori_loop(0, NB, qblock, 0)

    return pl.pallas_call(
        body, name=name, grid=(T // SEQ,), in_specs=_qkv_specs(0),
        out_specs=pl.BlockSpec((SEQ, GROUP_W), lambda b: (b, 0)), out_shape=jax.ShapeDtypeStruct((T, GROUP_W), BF16),
        compiler_params=_params("parallel"),
    )(qkv, qkv, qkv)


def sbw_bwd(qkv, dmixed, *, name):
    T = qkv.shape[0]

    def body(q_ref, k_ref, v_ref, do_ref, dq_ref, dk_ref, dv_ref, att_s, sg_s):
        bd, row, key = _wide_consts()
        r2 = lax.broadcasted_iota(jnp.int32, (CHUNK, CHUNK), 0)
        c2 = lax.broadcasted_iota(jnp.int32, (CHUNK, CHUNK), 1)
        later = (r2 > c2).astype(BF16)
        earlier = (r2 < c2).astype(BF16)
        dk_ref[...] = jnp.zeros_like(dk_ref)
        dv_ref[...] = jnp.zeros_like(dv_ref)

        def qblock(i, _):
            q = q_ref[_blk(i), :] * Q_SCALE
            do = do_ref[_blk(i), :].astype(BF16)
            cd = i // 2
            strict = key < row + BLOCK * (i % 2)

            def recompute(c, mask, csum):
                att, e, lk = _sbw_tile(q, _block_diag(k_ref[_chunk_rows(c), :], bd), mask, later, csum)
                sg = jnp.exp(e)
                att_s[c] = att
                sg_s[c] = sg if mask is None else jnp.where(mask, sg, 0.0)
                return csum + _widen(_head_rowsums(lk))

            csum = recompute(cd, strict, jnp.zeros((BLOCK, WIDE), F32))
            lax.fori_loop(0, cd, lambda n, cs: recompute(cd - 1 - n, None, cs), csum)

            def tile(c, carry):
                dq, pre = carry
                kbd = _block_diag(k_ref[_chunk_rows(c), :], bd)
                vbd = _block_diag(v_ref[_chunk_rows(c), :], bd)
                att = att_s[c]
                ds = _dot(do, vbd, _NT) * att
                dlk = ds + _tri_wide(ds, earlier) + pre
                dz = (ds - dlk * sg_s[c]).astype(BF16)
                dk_ref[_chunk_rows(c), :] += _fold_heads(_dot(dz, q, _TN), bd)
                dv_ref[_chunk_rows(c), :] += _fold_heads(_dot(att.astype(BF16), do, _TN), bd)
                return dq + _dot(dz, kbd), pre + _widen(_head_rowsums(ds))

            dq, _ = lax.fori_loop(0, cd + 1, tile, (jnp.zeros((BLOCK, GROUP_W), F32), jnp.zeros((BLOCK, WIDE), F32)))
            dq_ref[_blk(i), :] = dq * Q_SCALE
            return 0

        lax.fori_loop(0, NB, qblock, 0)

    out = pl.BlockSpec((SEQ, GROUP_W), lambda b: (b, 0))
    sd = jax.ShapeDtypeStruct((T, GROUP_W), F32)
    return pl.pallas_call(
        body, name=name, grid=(T // SEQ,), in_specs=_qkv_specs(0) + [out],
        out_specs=[out] * 3, out_shape=[sd] * 3,
        scratch_shapes=[pltpu.VMEM((NCH, BLOCK, WIDE), F32), pltpu.VMEM((NCH, BLOCK, WIDE), F32)],
        compiler_params=_params("parallel"),
    )(qkv, qkv, qkv, dmixed)


def _foxw_logits(q, kbd, cq, cr_ref, c, mask):
    ck = jnp.concatenate([cr_ref[h:h + 1, _chunk_rows(c)] for h in HEADS], axis=1)
    z = _dot(q, kbd, _NT) + cq - ck
    return z if mask is None else jnp.where(mask, z, NEG)


def foxw_fwd(qkv, cumc, cumr, *, name):
    T = qkv.shape[0]

    def body(q_ref, k_ref, v_ref, cc_ref, cr_ref, o_ref, o32_ref, lse_ref, z_s):
        bd, row, key = _wide_consts()
        lse_ref[...] = jnp.zeros_like(lse_ref)

        def qblock(i, _):
            q = q_ref[_blk(i), :] * Q_SCALE
            cq = _widen([cc_ref[_blk(i), h:h + 1] for h in HEADS])
            cd = i // 2
            causal = key <= row + BLOCK * (i % 2)

            def logits(c, mask, ms):
                z = _foxw_logits(q, _block_diag(k_ref[_chunk_rows(c), :], bd), cq, cr_ref, c, mask)
                z_s[c] = z
                return tuple(jnp.maximum(ms[h], jnp.max(z[:, _seg(h)], axis=1, keepdims=True)) for h in HEADS)

            ms = logits(cd, causal, (jnp.full((BLOCK, 1), NEG, F32),) * N_HEADS)
            ms = lax.fori_loop(0, cd, lambda c, m: logits(c, None, m), ms)
            m_wide = _widen(ms)

            def values(c, carry):
                acc, l = carry
                p = jnp.exp(z_s[c] - m_wide)
                return acc + _dot(p.astype(BF16), _block_diag(v_ref[_chunk_rows(c), :], bd)), l + _widen(_head_rowsums(p))

            acc, l = lax.fori_loop(0, cd + 1, values, (jnp.zeros((BLOCK, GROUP_W), F32), jnp.zeros((BLOCK, WIDE), F32)))
            ls = [l[:, h * CHUNK:h * CHUNK + 1] for h in HEADS]
            o = acc / _feature_widen(ls)
            o_ref[_blk(i), :] = o.astype(BF16)
            o32_ref[_blk(i), :] = o
            for h in HEADS:
                lse_ref[_blk(i), h:h + 1] = ms[h] + jnp.log(ls[h])
            return 0

        lax.fori_loop(0, NB, qblock, 0)

    out = pl.BlockSpec((SEQ, GROUP_W), lambda b: (b, 0))
    colb = pl.BlockSpec((SEQ, LANES), lambda b: (b, 0))
    sd = jax.ShapeDtypeStruct
    return pl.pallas_call(
        body, name=name, grid=(T // SEQ,),
        in_specs=_qkv_specs(3) + [colb, pl.BlockSpec((8, SEQ), lambda b: (b, 0))],
        out_specs=[out, out, colb], out_shape=[sd((T, GROUP_W), BF16), sd((T, GROUP_W), F32), sd((T, LANES), F32)],
        scratch_shapes=[pltpu.VMEM((NCH, BLOCK, WIDE), F32)],
        compiler_params=_params("parallel"),
    )(qkv, qkv, qkv, cumc, cumr)


def foxw_bwd(qkv, cumc, cumr, lse, o32, dmixed, *, name):
    T = qkv.shape[0]

    def body(q_ref, k_ref, v_ref, cc_ref, cr_ref, lse_ref, o_ref, do_ref, dq_ref, dk_ref, dv_ref, dcc_ref, dcr_ref):
        bd, row, key = _wide_consts()
        dk_ref[...] = jnp.zeros_like(dk_ref)
        dv_ref[...] = jnp.zeros_like(dv_ref)
        dcc_ref[...] = jnp.zeros_like(dcc_ref)
        dcr_ref[...] = jnp.zeros_like(dcr_ref)

        def qblock(i, _):
            q = q_ref[_blk(i), :] * Q_SCALE
            do32 = do_ref[_blk(i), :]
            do = do32.astype(BF16)
            prod = do32 * o_ref[_blk(i), :]
            delta = _widen([jnp.sum(prod[:, _hs(h)], axis=1, keepdims=True) for h in HEADS])
            cq = _widen([cc_ref[_blk(i), h:h + 1] for h in HEADS])
            lse_w = _widen([lse_ref[_blk(i), h:h + 1] for h in HEADS])
            cd = i // 2
            causal = key <= row + BLOCK * (i % 2)

            def tile(c, mask, carry):
                dq, dcq = carry
                kbd = _block_diag(k_ref[_chunk_rows(c), :], bd)
                vbd = _block_diag(v_ref[_chunk_rows(c), :], bd)
                p = jnp.exp(_foxw_logits(q, kbd, cq, cr_ref, c, mask) - lse_w)
                ds = p * (_dot(do, vbd, _NT) - delta)
                dsb = ds.astype(BF16)
                dk_ref[_chunk_rows(c), :] += _fold_heads(_dot(dsb, q, _TN), bd)
                dv_ref[_chunk_rows(c), :] += _fold_heads(_dot(p.astype(BF16), do, _TN), bd)
                for h in HEADS:
                    dcr_ref[h:h + 1, _chunk_rows(c)] -= jnp.sum(ds[:, _seg(h)], axis=0, keepdims=True)
                return dq + _dot(dsb, kbd), dcq + _widen(_head_rowsums(ds))

            carry = tile(cd, causal, (jnp.zeros((BLOCK, GROUP_W), F32), jnp.zeros((BLOCK, WIDE), F32)))
            dq, dcq = lax.fori_loop(0, cd, lambda c, cr: tile(c, None, cr), carry)
            dq_ref[_blk(i), :] = dq * Q_SCALE
            for h in HEADS:
                dcc_ref[_blk(i), h:h + 1] = dcq[:, h * CHUNK:h * CHUNK + 1]
            return 0

        lax.fori_loop(0, NB, qblock, 0)

    out = pl.BlockSpec((SEQ, GROUP_W), lambda b: (b, 0))
    colb = pl.BlockSpec((SEQ, LANES), lambda b: (b, 0))
    rowb = pl.BlockSpec((8, SEQ), lambda b: (b, 0))
    sd = jax.ShapeDtypeStruct
    big = sd((T, GROUP_W), F32)
    return pl.pallas_call(
        body, name=name, grid=(T // SEQ,),
        in_specs=_qkv_specs(3) + [colb, rowb, colb, out, pl.BlockSpec((SEQ, GROUP_W), lambda b: (b, 1))],
        out_specs=[out, out, out, colb, rowb],
        out_shape=[big, big, big, sd((T, LANES), F32), sd((T // SEQ * 8, SEQ), F32)],
        compiler_params=_params("parallel"),
    )(qkv, qkv, qkv, cumc, cumr, lse, o32, dmixed)


BAND = 2 * BLOCK


def _t5_bucket_np(dist):
    n = np.maximum(dist, 0)
    max_exact = NUM_BUCKETS // 2
    nf = np.maximum(n, 1).astype(np.float32)
    large = max_exact + (np.log(nf / np.float32(max_exact)) / np.float32(math.log(MAX_DISTANCE / max_exact))
                         * np.float32(NUM_BUCKETS - max_exact)).astype(np.int32)
    large = np.minimum(large, NUM_BUCKETS - 1)
    return np.where(n < max_exact, n, large).astype(np.int32)


def _band_buckets():
    qi = np.arange(BLOCK)[:, None]
    ki = np.arange(BAND)[None, :]
    delta = np.clip(qi - ki + BLOCK, 0, BLOCK)
    return np.stack([_t5_bucket_np(delta * d) for d in DILATIONS])


def to_classes(a, d):
    if d == 1:
        return a
    T, C = a.shape
    return a.reshape(T // SEQ, SEQ // d, d, C).transpose(0, 2, 1, 3).reshape(T, C)


def from_classes(a, d):
    if d == 1:
        return a
    T, C = a.shape
    return a.reshape(T // SEQ, d, SEQ // d, C).transpose(0, 2, 1, 3).reshape(T, C)


def relbias_expand(rel, *, name):
    buckets = jnp.asarray(_band_buckets())
    n_pat = len(DILATIONS)

    def body(rel_ref, bk_ref, o_ref):
        for p in range(n_pat):
            bk = bk_ref[p]
            for h in range(N_HEADS):
                acc = jnp.zeros((BLOCK, BAND), F32)
                for b in range(NUM_BUCKETS):
                    acc = jnp.where(bk == b, rel_ref[b, h], acc)
                o_ref[p * N_HEADS + h] = acc

    return pl.pallas_call(
        body, name=name,
        in_specs=[pl.BlockSpec(memory_space=pltpu.SMEM), pl.BlockSpec(memory_space=pltpu.VMEM)],
        out_specs=pl.BlockSpec(memory_space=pltpu.VMEM),
        out_shape=jax.ShapeDtypeStruct((n_pat * N_HEADS, BLOCK, BAND), F32),
        compiler_params=_params(),
    )(rel, buckets)


def relbias_reduce(ds_all, *, name):
    buckets = jnp.asarray(_band_buckets())
    n_pat = len(DILATIONS)

    def body(ds_ref, bk_ref, o_ref):
        for b in range(NUM_BUCKETS):
            for h in range(N_HEADS):
                tot = jnp.float32(0.0)
                for p in range(n_pat):
                    tot = tot + jnp.sum(jnp.where(bk_ref[p] == b, ds_ref[p * N_HEADS + h], 0.0))
                o_ref[b, h] = tot

    return pl.pallas_call(
        body, name=name,
        in_specs=[pl.BlockSpec(memory_space=pltpu.VMEM), pl.BlockSpec(memory_space=pltpu.VMEM)],
        out_specs=pl.BlockSpec(memory_space=pltpu.SMEM),
        out_shape=jax.ShapeDtypeStruct((NUM_BUCKETS, N_HEADS), F32),
        compiler_params=_params(),
    )(ds_all, buckets)


def _band_valid_wide(first, row, key):
    inside = jnp.logical_and(key >= row, key <= row + BLOCK)
    return jnp.logical_and(inside, jnp.logical_or(jnp.logical_not(first), key >= BLOCK))


QKV_BLOCKS = 9


def _band_in_specs(d, pattern, has_prev):
    rows = BLOCK * d
    cur = lambda c: pl.BlockSpec((rows, GROUP_W), lambda tb, r: (tb, c))
    prev = lambda c: pl.BlockSpec((rows, GROUP_W), lambda tb, r: (jnp.maximum(tb - 1, 0), c))
    bias = pl.BlockSpec((N_HEADS, BLOCK, BAND), lambda tb, r: (pattern, 0, 0))
    return [cur(6), cur(7), cur(8)] + ([prev(7), prev(8)] if has_prev else []) + [bias]


def _class_rows(d):
    return pl.ds(pl.program_id(1), BLOCK, stride=d) if d > 1 else pl.ds(0, BLOCK)


def _halves_scratch(rows, n):
    return [pltpu.VMEM((2, rows, LANES), F32)] * n


def _stage(refs, scratch):
    @pl.when(pl.program_id(1) == 0)
    def _():
        for src, dst in zip(refs, scratch):
            dst[0] = src[:, :LANES].astype(F32)
            dst[1] = src[:, LANES:].astype(F32)


def _take_class(s, d):
    rows = _class_rows(d)
    return jnp.concatenate([s.at[0][rows, :], s.at[1][rows, :]], axis=1)


def _put_class(s, d, x):
    rows = _class_rows(d)
    s.at[0][rows, :] = x[:, :LANES]
    s.at[1][rows, :] = x[:, LANES:]


def _flush(scratch, refs, d):
    @pl.when(pl.program_id(1) == d - 1)
    def _():
        for s, o in zip(scratch, refs):
            o[...] = jnp.concatenate([s[0], s[1]], axis=1)


def _band_operands(scratch, d, has_prev):
    take = lambda s: _take_class(s, d).astype(BF16)
    q = (_take_class(scratch[0], d) * Q_SCALE).astype(BF16)
    if has_prev:
        k = jnp.concatenate([take(scratch[3]), take(scratch[1])], axis=0)
        v = jnp.concatenate([take(scratch[4]), take(scratch[2])], axis=0)
    else:
        k = jnp.concatenate([jnp.zeros((BLOCK, GROUP_W), BF16), take(scratch[1])], axis=0)
        v = jnp.concatenate([jnp.zeros((BLOCK, GROUP_W), BF16), take(scratch[2])], axis=0)
    return q, k, v


def _lane_columns(cols):
    lane = lax.broadcasted_iota(jnp.int32, (BLOCK, LANES), 1)
    out = jnp.zeros((BLOCK, LANES), F32)
    for h, c in enumerate(cols):
        out = jnp.where(lane == h, c, out)
    return out


def band_fwd(qkv, bias, pattern, *, name):
    T = qkv.shape[0]
    d = DILATIONS[pattern]
    rows_per_block = BLOCK * d
    seq_blocks = SEQ // rows_per_block
    has_prev = seq_blocks > 1
    n_in = 5 if has_prev else 3

    def body(*refs):
        ins, b_ref, o_ref, lse_ref = refs[:n_in], refs[n_in], refs[n_in + 1], refs[n_in + 2]
        staged, o_s = refs[n_in + 3:2 * n_in + 3], refs[2 * n_in + 3]
        bd, row, key = _wide_consts()
        valid = _band_valid_wide(pl.program_id(0) % seq_blocks == 0, row, key)
        _stage(ins, staged)
        q, k, v = _band_operands(staged, d, has_prev)
        kbd, vbd = _block_diag(k, bd), _block_diag(v, bd)
        bias_w = jnp.concatenate([b_ref[h] for h in HEADS], axis=1)
        sc = jnp.where(valid, _dot(q, kbd, _NT) + bias_w, NEG)
        ms = [jnp.max(sc[:, _seg(h)], axis=1, keepdims=True) for h in HEADS]
        p = jnp.exp(sc - _widen(ms))
        ls = _head_rowsums(p)
        _put_class(o_s, d, _dot(p.astype(BF16), vbd) / _feature_widen(ls))
        lse_ref[_class_rows(d), :] = _lane_columns([ms[h] + jnp.log(ls[h]) for h in HEADS])
        _flush([o_s], [o_ref], d)

    sd = jax.ShapeDtypeStruct
    return pl.pallas_call(
        body, name=name, grid=(T // rows_per_block, d), in_specs=_band_in_specs(d, pattern, has_prev),
        out_specs=[pl.BlockSpec((rows_per_block, GROUP_W), lambda tb, r: (tb, 0)),
                   pl.BlockSpec((rows_per_block, LANES), lambda tb, r: (tb, 0))],
        out_shape=[sd((T, GROUP_W), F32), sd((T, LANES), F32)],
        scratch_shapes=_halves_scratch(rows_per_block, n_in + 1),
        compiler_params=_params("parallel", "arbitrary"),
    )(*([qkv] * n_in), bias)


def band_bwd(qkv, bias, lse, do, dlse, pattern, *, name):
    T = qkv.shape[0]
    d = DILATIONS[pattern]
    rows_per_block = BLOCK * d
    seq_blocks = SEQ // rows_per_block
    has_prev = seq_blocks > 1
    n_in = 5 if has_prev else 3
    n_out = 5 if has_prev else 3

    def body(*refs):
        ins, b_ref, lse_ref, do_ref, dlse_ref = refs[:n_in], refs[n_in], refs[n_in + 1], refs[n_in + 2], refs[n_in + 3]
        outs = refs[n_in + 4:n_in + 4 + n_out]
        ds_ref = refs[n_in + 4 + n_out]
        scratch = refs[n_in + 5 + n_out:]
        staged, do_s, out_s = scratch[:n_in], scratch[n_in], scratch[n_in + 1:]
        first_step = jnp.logical_and(pl.program_id(0) == 0, pl.program_id(1) == 0)
        bd, row, key = _wide_consts()
        valid = _band_valid_wide(pl.program_id(0) % seq_blocks == 0, row, key)
        _stage(list(ins) + [do_ref], list(staged) + [do_s])
        q, k, v = _band_operands(staged, d, has_prev)
        kbd, vbd = _block_diag(k, bd), _block_diag(v, bd)
        rows = _class_rows(d)
        do = _take_class(do_s, d).astype(BF16)
        lse_t, dlse_t = lse_ref[rows, :], dlse_ref[rows, :]
        bias_w = jnp.concatenate([b_ref[h] for h in HEADS], axis=1)
        lse_w = _widen([lse_t[:, h:h + 1] for h in HEADS])
        dlse_w = _widen([dlse_t[:, h:h + 1] for h in HEADS])
        p = jnp.where(valid, jnp.exp(_dot(q, kbd, _NT) + bias_w - lse_w), 0.0)
        dp = _dot(do, vbd, _NT)
        ds = p * (dp - _widen(_head_rowsums(p * dp)) + dlse_w)
        dsb, pb = ds.astype(BF16), p.astype(BF16)
        _put_class(out_s[0], d, _dot(dsb, kbd) * Q_SCALE)
        dk = _fold_heads(_dot(dsb, q, _TN), bd)
        dv = _fold_heads(_dot(pb, do, _TN), bd)
        _put_class(out_s[1], d, dk[BLOCK:])
        _put_class(out_s[2], d, dv[BLOCK:])
        if has_prev:
            _put_class(out_s[3], d, dk[:BLOCK])
            _put_class(out_s[4], d, dv[:BLOCK])
        _flush(out_s, outs, d)

        @pl.when(first_step)
        def _():
            for h in HEADS:
                ds_ref[h] = ds[:, _seg(h)]

        @pl.when(jnp.logical_not(first_step))
        def _():
            for h in HEADS:
                ds_ref[h] += ds[:, _seg(h)]

    big = pl.BlockSpec((rows_per_block, GROUP_W), lambda tb, r: (tb, 0))
    colb = pl.BlockSpec((rows_per_block, LANES), lambda tb, r: (tb, 0))
    sd = jax.ShapeDtypeStruct
    return pl.pallas_call(
        body, name=name, grid=(T // rows_per_block, d), in_specs=_band_in_specs(d, pattern, has_prev) + [colb, big, colb],
        out_specs=[big] * n_out + [pl.BlockSpec((N_HEADS, BLOCK, BAND), lambda tb, r: (0, 0, 0))],
        out_shape=[sd((T, GROUP_W), F32)] * n_out + [sd((N_HEADS, BLOCK, BAND), F32)],
        scratch_shapes=_halves_scratch(rows_per_block, n_in + 1 + n_out),
        compiler_params=_params("arbitrary", "arbitrary"),
    )(*([qkv] * n_in), bias, lse, do, dlse)


def shift_add(cur, prev, d, *, name):
    rows = BLOCK * d
    nb = cur.shape[0] // rows

    def body(c_ref, p_ref, o_ref):
        keep = (pl.program_id(0) < nb - 1).astype(F32)
        o_ref[...] = c_ref[...] + keep * p_ref[...]

    blk = pl.BlockSpec((rows, GROUP_W), lambda tb: (tb, 0))
    nxt = pl.BlockSpec((rows, GROUP_W), lambda tb: (jnp.minimum(tb + 1, nb - 1), 0))
    return pl.pallas_call(
        body, name=name, grid=(nb,), in_specs=[blk, nxt], out_specs=blk,
        out_shape=jax.ShapeDtypeStruct(cur.shape, F32), compiler_params=_params("parallel"),
    )(cur, prev)


def _pattern_weights(lse_refs, h):
    ls = [r[:, h:h + 1] for r in lse_refs]
    mx = functools.reduce(jnp.maximum, ls)
    es = [jnp.exp(l - mx) for l in ls]
    tot = functools.reduce(lambda a, b: a + b, es)
    return [e / tot for e in es]


def dil_combine_fwd(outs, *, name):
    T = outs[0][0].shape[0]
    n = len(outs)
    tm = 512

    def body(*refs):
        o_refs, l_refs, out_ref = refs[:n], refs[n:2 * n], refs[2 * n]
        for h in range(N_HEADS):
            w = _pattern_weights(l_refs, h)
            acc = w[0] * o_refs[0][:, _hs(h)]
            for p in range(1, n):
                acc = acc + w[p] * o_refs[p][:, _hs(h)]
            out_ref[:, _hs(h)] = acc.astype(BF16)

    big = pl.BlockSpec((tm, GROUP_W), lambda i: (i, 0))
    colb = pl.BlockSpec((tm, LANES), lambda i: (i, 0))
    return pl.pallas_call(
        body, name=name, grid=(T // tm,), in_specs=[big] * n + [colb] * n,
        out_specs=big, out_shape=jax.ShapeDtypeStruct((T, GROUP_W), BF16),
        compiler_params=_params("parallel"),
    )(*[o for o, _ in outs], *[l for _, l in outs])


def dil_combine_bwd(outs, dmixed, *, name):
    T = outs[0][0].shape[0]
    n = len(outs)
    tm = 512

    def body(*refs):
        o_refs, l_refs, do_ref = refs[:n], refs[n:2 * n], refs[2 * n]
        do_refs, dl_refs = refs[2 * n + 1:3 * n + 1], refs[3 * n + 1:]
        for r in dl_refs:
            r[...] = jnp.zeros_like(r)
        for h in range(N_HEADS):
            w = _pattern_weights(l_refs, h)
            do = do_ref[:, _hs(h)]
            dw = [jnp.sum(do * o_refs[p][:, _hs(h)], axis=1, keepdims=True) for p in range(n)]
            mean = functools.reduce(lambda a, b: a + b, [w[p] * dw[p] for p in range(n)])
            for p in range(n):
                do_refs[p][:, _hs(h)] = w[p] * do
                dl_refs[p][:, h:h + 1] = w[p] * (dw[p] - mean)

    big = pl.BlockSpec((tm, GROUP_W), lambda i: (i, 0))
    colb = pl.BlockSpec((tm, LANES), lambda i: (i, 0))
    sd = jax.ShapeDtypeStruct
    res = pl.pallas_call(
        body, name=name, grid=(T // tm,),
        in_specs=[big] * n + [colb] * n + [pl.BlockSpec((tm, GROUP_W), lambda i: (i, 2))],
        out_specs=[big] * n + [colb] * n, out_shape=[sd((T, GROUP_W), F32)] * n + [sd((T, LANES), F32)] * n,
        compiler_params=_params("parallel"),
    )(*[o for o, _ in outs], *[l for _, l in outs], dmixed)
    return list(zip(res[:n], res[n:]))


def dilated_fwd(qkv, bias, tag):
    return [band_fwd(qkv, bias, p, name=f"{tag}_band_fwd{p}") for p in range(len(DILATIONS))]


def dilated_bwd(qkv, bias, outs, dmixed, tag):
    grads = dil_combine_bwd(outs, dmixed, name=f"{tag}_combine_bwd")
    parts, ds_all = [], []
    for p, d in enumerate(DILATIONS):
        (_, lse), (do, dlse) = outs[p], grads[p]
        res = band_bwd(qkv, bias, lse, do, dlse, p, name=f"{tag}_band_bwd{p}")
        dq, dk, dv, ds = res[0], res[1], res[2], res[-1]
        if len(res) > 4:
            dk = shift_add(dk, res[3], d, name=f"{tag}_dk{p}")
            dv = shift_add(dv, res[4], d, name=f"{tag}_dv{p}")
        parts.append([dq, dk, dv])
        ds_all.append(ds)
    return parts, jnp.concatenate(ds_all, axis=0)


def assemble_dqkv(d_sb, d_fox, d_dil, *, name):
    T = d_sb[0].shape[0]
    tr = 512
    n_pat = len(d_dil)
    flat = list(d_sb) + list(d_fox) + [a for part in d_dil for a in part]

    def body(*refs):
        o_ref = refs[-1]
        for j in range(6):
            o_ref[:, j * GROUP_W:(j + 1) * GROUP_W] = refs[j][...].astype(BF16)
        for j in range(3):
            acc = refs[6 + j][...]
            for p in range(1, n_pat):
                acc = acc + refs[6 + 3 * p + j][...]
            o_ref[:, (6 + j) * GROUP_W:(7 + j) * GROUP_W] = acc.astype(BF16)

    blk = pl.BlockSpec((tr, GROUP_W), lambda i: (i, 0))
    return pl.pallas_call(
        body, name=name, grid=(T // tr,), in_specs=[blk] * len(flat),
        out_specs=pl.BlockSpec((tr, QKV_BLOCKS * GROUP_W), lambda i: (i, 0)),
        out_shape=jax.ShapeDtypeStruct((T, QKV_BLOCKS * GROUP_W), BF16), compiler_params=_params("parallel"),
    )(*flat)


def sum_cast(arrs, dtype, *, name):
    R, C = arrs[0].shape
    tr = _largest_tile(R, 512, 16)
    n = len(arrs)

    def body(*refs):
        acc = refs[0][...].astype(F32)
        for r in refs[1:n]:
            acc = acc + r[...].astype(F32)
        refs[n][...] = acc.astype(dtype)

    blk = pl.BlockSpec((tr, C), lambda i: (i, 0))
    return pl.pallas_call(
        body, name=name, grid=(R // tr,), in_specs=[blk] * n, out_specs=blk, out_shape=jax.ShapeDtypeStruct((R, C), dtype),
        compiler_params=_params("parallel"),
    )(*arrs)


GRAD_WIRE = BF16


def _block_diag_halves(w):
    z = jnp.zeros((HEAD_DIM, HEAD_DIM), w.dtype)
    half = lambda a, b: jnp.concatenate([jnp.concatenate([a, z], axis=1), jnp.concatenate([z, b], axis=1)], axis=0)
    return jnp.stack([half(w[0], w[1]), half(w[2], w[3])]).astype(BF16)


def _diag_blocks(d):
    h = HEAD_DIM
    return jnp.stack([d[0, :h, :h], d[0, h:, h:], d[1, :h, :h], d[1, h:, h:]])


def layer_fwd(x, mem2d, W, P, bias, tag):
    s = {}
    s["x"] = x
    h1 = rmsnorm_fwd(x, P["norm_mix_g"], name=f"{tag}_norm_mix")
    qkv = matmul(h1, W["qkv"], out_dtype=BF16, name=f"{tag}_qkv")
    aux = matmul(h1, W["aux"], name=f"{tag}_aux")
    o_sb = sbw_fwd(qkv, name=f"{tag}_sb_fwd")
    cumc = fox_prep(aux, P["bf"], name=f"{tag}_fox_prep")
    cumr = col_to_row(cumc)
    o_fox, o_fox32, lse_fox = foxw_fwd(qkv, cumc, cumr, name=f"{tag}_fox_fwd")
    dil = dilated_fwd(qkv, bias, tag)
    o_dil = dil_combine_fwd(dil, name=f"{tag}_dil_combine")
    o_lru, h_lru = lru_fwd(aux, P["lru_conv_w"], P["lru_conv_b"], P["wa"], P["lru_b_a"], P["wx"], P["lru_b_x"],
                           P["lru_lambda"], name=f"{tag}_lru_fwd")
    mixed = jnp.concatenate([o_sb, o_fox, o_dil, o_lru], axis=1)
    x1 = matmul(mixed, W["out"], residual=x, name=f"{tag}_out")
    hq = rmsnorm_fwd(x1, P["norm_cross_g"], name=f"{tag}_norm_cross")
    qc = matmul(hq, W["cq"], out_dtype=BF16, name=f"{tag}_cq")
    memn = rmsnorm_fwd(mem2d, P["norm_mem_g"], name=f"{tag}_norm_mem")
    kv = matmul(memn, W["ckv"], out_dtype=BF16, name=f"{tag}_ckv")
    oc = cross_fwd(qc, kv, name=f"{tag}_cross_fwd")
    x2 = matmul(oc, W["coT"], trans_b=True, residual=x1, name=f"{tag}_co")
    h2 = rmsnorm_fwd(x2, P["norm_ffn_g"], name=f"{tag}_norm_ffn")
    if "ffn" in W:
        W.update(W.pop("ffn")(x2))
    hu = matmul(h2, W["up_u"], trans_b=True, name=f"{tag}_up_u")
    hg = matmul(h2, W["up_g"], trans_b=True, name=f"{tag}_up_g")
    act = glu_fwd(hu, hg, P["wu"], P["wg"], P["bu"], P["bg"], name=f"{tag}_glu_fwd")
    x3 = matmul(act, W["down"], residual=x2, name=f"{tag}_down")
    s.update(h1=h1, qkv=qkv, aux=aux, cumc=cumc, cumr=cumr, lse_fox=lse_fox, o_fox32=o_fox32, dil=dil, h_lru=h_lru, mixed=mixed,
             x1=x1, hq=hq, qc=qc, memn=memn, kv=kv, oc=oc, x2=x2, h2=h2, hu=hu, hg=hg, act=act)
    return x3, s


def layer_bwd(dx3, mem2d, W, P, bias, s, tag, ffn_grads_done=None):
    mm = functools.partial(matmul, out_dtype=GRAD_WIRE, trans_a=True)
    gW, gP = {}, {}
    dact = matmul(dx3, W["down"], trans_b=True, name=f"{tag}_d_act")
    gW["down"] = mm(s["act"], dx3, name=f"{tag}_g_down")
    dhu, dhg, dwu, dwg, dbu, dbg = glu_bwd(s["hu"], s["hg"], dact, P["wu"], P["wg"], P["bu"], P["bg"], name=f"{tag}_glu_bwd")
    gP["ffn_conv_w"] = jnp.concatenate([dwu, dwg], axis=1)
    gP["ffn_conv_b"] = jnp.concatenate([dbu, dbg], axis=1)
    dh2 = matmul(dhu, W["up_u"], name=f"{tag}_d_h2u")
    dh2 = matmul(dhg, W["up_g"], residual=dh2, name=f"{tag}_d_h2g")
    gW["up_u"] = mm(dhu, s["h2"], name=f"{tag}_g_up_u")
    gW["up_g"] = mm(dhg, s["h2"], name=f"{tag}_g_up_g")
    dx2, gP["norm_ffn_g"] = rmsnorm_bwd(s["x2"], P["norm_ffn_g"], dh2, dx3, name=f"{tag}_norm_ffn_bwd")
    if ffn_grads_done is not None:
        dx2 = ffn_grads_done(gW, dx2)
    doc = matmul(dx2, W["coT"], name=f"{tag}_d_oc")
    gW["coT"] = mm(dx2, s["oc"], name=f"{tag}_g_co")
    dqc, dkv = cross_bwd(s["qc"], s["kv"], doc, name=f"{tag}_cross_bwd")
    dhq = matmul(dqc, W["cq"], trans_b=True, name=f"{tag}_d_hq")
    gW["cq"] = mm(s["hq"], dqc, name=f"{tag}_g_cq")
    dmemn = matmul(dkv, W["ckv"], trans_b=True, name=f"{tag}_d_memn")
    gW["ckv"] = mm(s["memn"], dkv, name=f"{tag}_g_ckv")
    _, gP["norm_mem_g"] = rmsnorm_bwd(mem2d, P["norm_mem_g"], dmemn, None, name=f"{tag}_norm_mem_bwd")
    dx1, gP["norm_cross_g"] = rmsnorm_bwd(s["x1"], P["norm_cross_g"], dhq, dx2, name=f"{tag}_norm_cross_bwd")
    dmixed = matmul(dx1, W["out"], trans_b=True, name=f"{tag}_d_mixed")
    gW["out"] = mm(s["mixed"], dx1, name=f"{tag}_g_out")
    qkv, aux = s["qkv"], s["aux"]
    d_sb = sbw_bwd(qkv, dmixed, name=f"{tag}_sb_bwd")
    dfq, dfk, dfv, dcc, dcr = foxw_bwd(qkv, s["cumc"], s["cumr"], s["lse_fox"], s["o_fox32"], dmixed, name=f"{tag}_fox_bwd")
    dcum = sum_cast([dcc, row_to_col(dcr)], F32, name=f"{tag}_dcum")
    df, dbf = fox_prep_bwd(aux, P["bf"], dcum, name=f"{tag}_fox_prep_bwd")
    gP["b_forget"] = dbf[0, :N_HEADS]
    d_dil, ds_band = dilated_bwd(qkv, bias, s["dil"], dmixed, tag)
    dlx, dlg, dcw, dcb, dwa, dba, dwx, dbx, dlam = lru_bwd(
        aux, s["h_lru"], dmixed, P["lru_conv_w"], P["lru_conv_b"], P["wa"], P["lru_b_a"], P["wx"], P["lru_b_x"],
        P["lru_lambda"], name=f"{tag}_lru_bwd")
    gP.update(lru_conv_w=dcw, lru_conv_b=dcb, lru_w_a=_diag_blocks(dwa), lru_b_a=dba, lru_w_x=_diag_blocks(dwx),
              lru_b_x=dbx, lru_lambda=dlam)
    dqkv = assemble_dqkv(d_sb, [dfq, dfk, dfv], d_dil, name=f"{tag}_dqkv")
    daux = jnp.concatenate([dlx, dlg, df], axis=1)
    dh1 = matmul(dqkv, W["qkv"], trans_b=True, name=f"{tag}_d_h1a")
    dh1 = matmul(daux, W["aux"], trans_b=True, residual=dh1, name=f"{tag}_d_h1b")
    gW["qkv"] = mm(s["h1"], dqkv, name=f"{tag}_g_qkv")
    gW["aux"] = mm(s["h1"], daux, name=f"{tag}_g_aux")
    dx, gP["norm_mix_g"] = rmsnorm_bwd(s["x"], P["norm_mix_g"], dh1, dx1, name=f"{tag}_norm_mix_bwd")
    return dx, gW, gP, ds_band


def local_step(x, mem, target, weights_of, Ps, rel_bias, final_norm_g, grads_done=None, ffn_grads_done=None):
    B = x.shape[0]
    x2d = x.reshape(B * SEQ, D_MODEL)
    mem2d = mem.reshape(B * N_MEM, D_MODEL)
    bias = relbias_expand(rel_bias, name="relbias_expand")
    saved, Ws = [], []
    h = x2d
    for l in range(DEPTH):
        Ws.append(weights_of(l, h))
        h, s = layer_fwd(h, mem2d, Ws[l], Ps[l], bias, f"l{l}")
        saved.append(s)
    loss, dh, d_final = loss_head(h, final_norm_g, target.reshape(B * SEQ, D_MODEL), name="loss_head")
    gWs, gPs, ds_bands = [None] * DEPTH, [None] * DEPTH, []
    for l in range(DEPTH - 1, -1, -1):
        hook = None if ffn_grads_done is None else functools.partial(ffn_grads_done, l)
        dh, gWs[l], gPs[l], ds = layer_bwd(dh, mem2d, Ws[l], Ps[l], bias, saved[l], f"l{l}", hook)
        if grads_done is not None:
            dh = grads_done(l, gWs[l], dh)
        ds_bands.append(ds)
    d_rel = relbias_reduce(sum_cast([d.reshape(-1, BAND) for d in ds_bands], F32, name="ds_band_sum").reshape(-1, BLOCK, BAND),
                           name="relbias_reduce")
    return loss, dh.reshape(B, SEQ, D_MODEL), gWs, gPs, d_rel, d_final


def small_params(p, l):
    row = lambda name: p[name][l].reshape(1, -1)
    ffn_w, ffn_b = p["ffn_conv_w"][l], row("ffn_conv_b")
    return dict(
        norm_mix_g=row("norm_mix_g"), norm_cross_g=row("norm_cross_g"), norm_mem_g=row("norm_mem_g"), norm_ffn_g=row("norm_ffn_g"),
        bf=jnp.pad(row("b_forget"), ((0, 0), (0, LANES - N_HEADS))),
        lru_conv_w=p["lru_conv_w"][l], lru_conv_b=row("lru_conv_b"), wa=_block_diag_halves(p["lru_w_a"][l]), lru_b_a=row("lru_b_a"),
        wx=_block_diag_halves(p["lru_w_x"][l]), lru_b_x=row("lru_b_x"), lru_lambda=row("lru_lambda"),
        wu=ffn_w[:, :D_FF], wg=ffn_w[:, D_FF:], bu=ffn_b[:, :D_FF], bg=ffn_b[:, D_FF:])


def canonical_weights(w_in, w_out, w_cq, w_ck, w_cv, w_co, w_up, w_down):
    sb_fox, fox_f, rest = w_in[:, :6 * GROUP_W], w_in[:, 6 * GROUP_W:6 * GROUP_W + N_HEADS], w_in[:, 6 * GROUP_W + N_HEADS:]
    dil, lru = rest[:, :3 * GROUP_W], rest[:, 3 * GROUP_W:]
    pad = jnp.zeros((w_in.shape[0], AUX_W - 2 * GROUP_W - N_HEADS), w_in.dtype)
    return dict(qkv=jnp.concatenate([sb_fox, dil], axis=1), aux=jnp.concatenate([lru, fox_f, pad], axis=1), out=w_out,
                cq=w_cq, ckv=jnp.concatenate([w_ck, w_cv], axis=1), coT=w_co.T, upT=w_up.T, down=w_down)


def native_grads(g):
    qkv, aux = g["qkv"], g["aux"]
    a, b = 6 * GROUP_W, 6 * GROUP_W + N_HEADS
    w_in = jnp.zeros((qkv.shape[0], b + 5 * GROUP_W), qkv.dtype)
    w_in = w_in.at[:, :a].set(qkv[:, :a]).at[:, a:b].set(aux[:, 2 * GROUP_W:2 * GROUP_W + N_HEADS])
    w_in = w_in.at[:, b:b + 3 * GROUP_W].set(qkv[:, a:]).at[:, b + 3 * GROUP_W:].set(aux[:, :2 * GROUP_W])
    return (w_in, g["out"], g["cq"], g["ckv"][:, :GROUP_W], g["ckv"][:, GROUP_W:], g["coT"].T, g["upT"].T, g["down"])


ANY = pl.BlockSpec(memory_space=pl.ANY)
VMEM_SPEC = pl.BlockSpec(memory_space=pltpu.VMEM)


def _place():
    x, y, c = lax.axis_index("x"), lax.axis_index("y"), lax.axis_index("c")
    other_chips = [(1 - x, y), (x, 1 - y), (1 - x, 1 - y)]
    return x, y, c, other_chips


def _gather_body(x_ref, out_ref, send_sems, recv_sems, local_sem):
    x, y, c, chips = _place()
    me, sibling = (x, y, c), (x, y, 1 - c)

    def slot(px, py, pc):
        return out_ref.at[4 * px + 2 * py + pc]

    def copy(k, block, to, src=None):
        return pltpu.make_async_remote_copy(
            src_ref=slot(*block) if src is None else src, dst_ref=slot(*block),
            send_sem=send_sems.at[k], recv_sem=recv_sems.at[k], device_id=to, device_id_type=MESH)

    if local_sem is not None:
        mine = pltpu.make_async_copy(x_ref, slot(*me), local_sem)
        mine.start()
    first = [copy(0, me, sibling, src=x_ref)]
    first += [copy(1 + j, me, (*chip, c), src=x_ref) for j, chip in enumerate(chips)]
    for cp in first:
        cp.start()
    passed = [copy(4 + j, (*chip, c), sibling) for j, chip in enumerate(chips)]
    for j, chip in enumerate(chips):
        copy(1 + j, (*chip, c), me).wait_recv()
        passed[j].start()
    copy(0, sibling, me).wait_recv()
    for j, chip in enumerate(chips):
        copy(4 + j, (*chip, 1 - c), me).wait_recv()
    for cp in first + passed:
        cp.wait_send()
    if local_sem is not None:
        mine.wait()


_GATHER_SEMS = [pltpu.SemaphoreType.DMA((7,)), pltpu.SemaphoreType.DMA((7,)), pltpu.SemaphoreType.DMA]


def allgather_hbm(shard, me, *, name):
    def body(x_ref, out_ref, done_ref, send_sems, recv_sems):
        _gather_body(x_ref, out_ref, send_sems, recv_sems, None)
        done_ref[...] = jnp.zeros_like(done_ref)

    others, done = pl.pallas_call(
        body, name=name, in_specs=[ANY], out_specs=[ANY, VMEM_SPEC],
        out_shape=[jax.ShapeDtypeStruct((N_DEV,) + shard.shape, shard.dtype), jax.ShapeDtypeStruct((8, LANES), F32)],
        scratch_shapes=_GATHER_SEMS[:2],
    )(shard)
    return lax.dynamic_update_slice(others, shard[None], (me, 0, 0)), done


def allgather_small(x, *, name, reduce=False):
    def body(x_ref, out_ref, second_ref, *sems):
        _gather_body(x_ref, out_ref, *sems)
        if reduce:
            acc = out_ref[0]
            for d in range(1, N_DEV):
                acc = acc + out_ref[d]
            second_ref[...] = acc
        else:
            second_ref[...] = jnp.zeros_like(second_ref)

    sd = jax.ShapeDtypeStruct
    return pl.pallas_call(
        body, name=name, in_specs=[VMEM_SPEC], out_specs=[VMEM_SPEC, VMEM_SPEC],
        out_shape=[sd((N_DEV,) + x.shape, x.dtype), sd(x.shape if reduce else (8, LANES), x.dtype)],
        scratch_shapes=_GATHER_SEMS, compiler_params=pltpu.CompilerParams(vmem_limit_bytes=VMEM_LIMIT_V7X),
    )(x)


N_CHIPS = 4


def pair_exchange(g, *, name):
    _, R, C = g.shape

    def body(g_ref, recv_ref, send_sems, recv_sems):
        x, y, c, _ = _place()
        sibling = (x, y, 1 - c)
        remote = [pltpu.make_async_remote_copy(
            src_ref=g_ref.at[2 * q + (1 - c)], dst_ref=recv_ref.at[q], send_sem=send_sems.at[q], recv_sem=recv_sems.at[q],
            device_id=sibling, device_id_type=MESH) for q in range(N_CHIPS)]
        for cp in remote:
            cp.start()
        for cp in remote:
            cp.wait_recv()
        for cp in remote:
            cp.wait_send()

    return pl.pallas_call(
        body, name=name, in_specs=[ANY], out_specs=ANY, out_shape=jax.ShapeDtypeStruct((N_CHIPS, R, C), g.dtype),
        scratch_shapes=[pltpu.SemaphoreType.DMA((N_CHIPS,))] * 2,
    )(g)


def chip_exchange(s, *, name):
    _, R, C = s.shape

    def body(s_ref, o0, o1, o2, send_sems, recv_sems):
        x, y, c, chips = _place()
        outs = (o0, o1, o2)
        copies = [pltpu.make_async_remote_copy(
            src_ref=s_ref.at[2 * cx + cy], dst_ref=outs[j], send_sem=send_sems.at[j], recv_sem=recv_sems.at[j],
            device_id=(cx, cy, c), device_id_type=MESH) for j, (cx, cy) in enumerate(chips)]
        for cp in copies:
            cp.start()
        for cp in copies:
            cp.wait_recv()
        for cp in copies:
            cp.wait_send()

    sd = jax.ShapeDtypeStruct((R, C), s.dtype)
    return pl.pallas_call(
        body, name=name, in_specs=[ANY], out_specs=[ANY] * 3, out_shape=[sd] * 3,
        scratch_shapes=[pltpu.SemaphoreType.DMA((3,)), pltpu.SemaphoreType.DMA((3,))],
    )(s)


HBM_SPEC = pl.BlockSpec(memory_space=pltpu.HBM)
SEM_SPEC = pl.BlockSpec(memory_space=pltpu.SEMAPHORE)
N_PEERS = N_DEV - 1


def _peers():
    x, y, c = lax.axis_index("x"), lax.axis_index("y"), lax.axis_index("c")
    flip = lambda v, bit: 1 - v if bit else v
    out = []
    for k in range(1, N_DEV):
        px, py, pc = flip(x, (k >> 2) & 1), flip(y, (k >> 1) & 1), flip(c, k & 1)
        out.append(((px, py, pc), 4 * px + 2 * py + pc))
    return out, 4 * x + 2 * y + c


def _peer_copies(src_ref, land_ref, send_sems, recv_sems, scatter, landing):
    peers, me = _peers()
    return [pltpu.make_async_remote_copy(
        src_ref=src_ref.at[idx] if scatter else src_ref, dst_ref=land_ref.at[me if landing == "mine" else idx],
        send_sem=send_sems.at[k], recv_sem=recv_sems.at[k], device_id=peer, device_id_type=MESH)
        for k, (peer, idx) in enumerate(peers)]


def exchange_start(src, scatter, *, name):
    shape = (N_DEV,) + src.shape[-2:]

    def body(src_ref, land_ref, send_sems, recv_sems, src_thru, land_thru, token):
        for cp in _peer_copies(src_ref, land_ref, send_sems, recv_sems, scatter, "mine"):
            cp.start()
        token[...] = jnp.zeros_like(token)

    sems = pltpu.SemaphoreType.DMA((N_PEERS,))
    return pl.pallas_call(
        body, name=name,
        out_shape=(sems, sems, pltpu.HBM(src.shape, src.dtype), pltpu.HBM(shape, src.dtype), jax.ShapeDtypeStruct((8, LANES), F32)),
        in_specs=(HBM_SPEC, HBM_SPEC), out_specs=(SEM_SPEC, SEM_SPEC, HBM_SPEC, HBM_SPEC, VMEM_SPEC),
        input_output_aliases={0: 2, 1: 3},
        compiler_params=pltpu.CompilerParams(has_side_effects=pltpu.SideEffectType.DATAFLOW_SIDE_EFFECTING),
    )(pltpu.with_memory_space_constraint(src, pltpu.HBM), pltpu.with_memory_space_constraint(lax.empty(shape, src.dtype), pltpu.HBM))


def exchange_wait(started, after, scatter, *, name):
    send_sems, recv_sems, src_thru, land_thru, _ = started

    def body(src_ref, land_ref, send_sems, recv_sems, after_ref, src_dead, got_ref):
        for cp in _peer_copies(src_ref, land_ref, send_sems, recv_sems, scatter, "theirs"):
            cp.wait_send()
            cp.wait_recv()

    return pl.pallas_call(
        body, name=name, out_shape=(pltpu.HBM(src_thru.shape, src_thru.dtype), pltpu.HBM(land_thru.shape, land_thru.dtype)),
        in_specs=(HBM_SPEC, HBM_SPEC, SEM_SPEC, SEM_SPEC, ANY), out_specs=(HBM_SPEC, HBM_SPEC), input_output_aliases={0: 0, 1: 1},
        compiler_params=pltpu.CompilerParams(has_side_effects=pltpu.SideEffectType.DATAFLOW_SIDE_EFFECTING),
    )(src_thru, land_thru, send_sems, recv_sems, after)[1]


def sum_blocks(blocks, *, name):
    n, R, C = blocks.shape
    tr = _largest_tile(R, 512, 16)

    def body(b_ref, o_ref):
        d = pl.program_id(1)
        v = b_ref[...].astype(F32)

        @pl.when(d == 0)
        def _():
            o_ref[...] = v

        @pl.when(d > 0)
        def _():
            o_ref[...] += v

    return pl.pallas_call(
        body, name=name, grid=(R // tr, n),
        in_specs=[pl.BlockSpec((None, tr, C), lambda i, d: (d, i, 0))], out_specs=pl.BlockSpec((tr, C), lambda i, d: (i, 0)),
        out_shape=jax.ShapeDtypeStruct((R, C), F32), compiler_params=_params("parallel", "arbitrary"),
    )(blocks)


WEIGHTS = ("norm_mix_g", "w_in", "b_forget", "lru_conv_w", "lru_conv_b", "lru_w_a", "lru_b_a", "lru_w_x", "lru_b_x", "lru_lambda",
           "w_out", "norm_cross_g", "norm_mem_g", "w_cq", "w_ck", "w_cv", "w_co", "norm_ffn_g", "w_up", "ffn_conv_w", "ffn_conv_b",
           "w_down", "rel_bias", "final_norm_g")
LARGE = ("w_in", "w_out", "w_cq", "w_ck", "w_cv", "w_co", "w_up", "w_down")
COLUMN_SPLIT_SMALL = ("lru_conv_w", "ffn_conv_w")
PACK = (("qkv", 128, 2304), ("aux", 128, 640), ("out", 128, 1024), ("cq", 128, 256), ("ckv", 128, 512), ("coT", 128, 256),
        ("upT", 704, 1024), ("down", 352, 1024))
PACK_W = 1024


def _pack_rows(parts):
    return jnp.concatenate([p.reshape(-1, PACK_W) for p in parts], axis=0)


def _pad_rows(flat, mult=8 * LANES):
    n = flat.shape[0]
    return jnp.pad(flat, (0, (-n) % mult)).reshape(-1, LANES)


def kernel(x, mem, norm_mix_g, w_in, b_forget, lru_conv_w, lru_conv_b, lru_w_a, lru_b_a, lru_w_x, lru_b_x, lru_lambda, w_out, norm_cross_g, norm_mem_g, w_cq, w_ck, w_cv, w_co, norm_ffn_g, w_up, ffn_conv_w, ffn_conv_b, w_down, rel_bias, final_norm_g, loss_target, m_norm_mix_g, m_w_in, m_b_forget, m_lru_conv_w, m_lru_conv_b, m_lru_w_a, m_lru_b_a, m_lru_w_x, m_lru_b_x, m_lru_lambda, m_w_out, m_norm_cross_g, m_norm_mem_g, m_w_cq, m_w_ck, m_w_cv, m_w_co, m_norm_ffn_g, m_w_up, m_ffn_conv_w, m_ffn_conv_b, m_w_down, m_rel_bias, m_final_norm_g, v_norm_mix_g, v_w_in, v_b_forget, v_lru_conv_w, v_lru_conv_b, v_lru_w_a, v_lru_b_a, v_lru_w_x, v_lru_b_x, v_lru_lambda, v_w_out, v_norm_cross_g, v_norm_mem_g, v_w_cq, v_w_ck, v_w_cv, v_w_co, v_norm_ffn_g, v_w_up, v_ffn_conv_w, v_ffn_conv_b, v_w_down, v_rel_bias, v_final_norm_g):
    w = dict(norm_mix_g=norm_mix_g, w_in=w_in, b_forget=b_forget, lru_conv_w=lru_conv_w, lru_conv_b=lru_conv_b, lru_w_a=lru_w_a,
             lru_b_a=lru_b_a, lru_w_x=lru_w_x, lru_b_x=lru_b_x, lru_lambda=lru_lambda, w_out=w_out, norm_cross_g=norm_cross_g,
             norm_mem_g=norm_mem_g, w_cq=w_cq, w_ck=w_ck, w_cv=w_cv, w_co=w_co, norm_ffn_g=norm_ffn_g, w_up=w_up,
             ffn_conv_w=ffn_conv_w, ffn_conv_b=ffn_conv_b, w_down=w_down, rel_bias=rel_bias, final_norm_g=final_norm_g)
    m = dict(norm_mix_g=m_norm_mix_g, w_in=m_w_in, b_forget=m_b_forget, lru_conv_w=m_lru_conv_w, lru_conv_b=m_lru_conv_b,
             lru_w_a=m_lru_w_a, lru_b_a=m_lru_b_a, lru_w_x=m_lru_w_x, lru_b_x=m_lru_b_x, lru_lambda=m_lru_lambda, w_out=m_w_out,
             norm_cross_g=m_norm_cross_g, norm_mem_g=m_norm_mem_g, w_cq=m_w_cq, w_ck=m_w_ck, w_cv=m_w_cv, w_co=m_w_co,
             norm_ffn_g=m_norm_ffn_g, w_up=m_w_up, ffn_conv_w=m_ffn_conv_w, ffn_conv_b=m_ffn_conv_b, w_down=m_w_down,
             rel_bias=m_rel_bias, final_norm_g=m_final_norm_g)
    v = dict(norm_mix_g=v_norm_mix_g, w_in=v_w_in, b_forget=v_b_forget, lru_conv_w=v_lru_conv_w, lru_conv_b=v_lru_conv_b,
             lru_w_a=v_lru_w_a, lru_b_a=v_lru_b_a, lru_w_x=v_lru_w_x, lru_b_x=v_lru_b_x, lru_lambda=v_lru_lambda, w_out=v_w_out,
             norm_cross_g=v_norm_cross_g, norm_mem_g=v_norm_mem_g, w_cq=v_w_cq, w_ck=v_w_ck, w_cv=v_w_cv, w_co=v_w_co,
             norm_ffn_g=v_norm_ffn_g, w_up=v_w_up, ffn_conv_w=v_ffn_conv_w, ffn_conv_b=v_ffn_conv_b, w_down=v_w_down,
             rel_bias=v_rel_bias, final_norm_g=v_final_norm_g)
    me = 4 * lax.axis_index("x") + 2 * lax.axis_index("y") + lax.axis_index("c")

    conv_shard = jnp.concatenate([w[n].reshape(-1) for n in COLUMN_SPLIT_SMALL])
    conv_all, conv_gathered = allgather_small(_pad_rows(conv_shard), name="gather_conv")
    conv_all = conv_all.reshape(N_DEV, -1)
    full = dict(w)
    off = 0
    for n in COLUMN_SPLIT_SMALL:
        d, k, c = w[n].shape
        blocks = conv_all[:, off:off + d * k * c].reshape(N_DEV, d, k, c)
        full[n] = blocks.transpose(1, 2, 0, 3).reshape(d, k, N_DEV * c)
        off += d * k * c

    MIX, FFN = PACK[:6], PACK[6:]

    def packed_shard(l, group):
        canon = canonical_weights(*[w[n][l] for n in LARGE])
        return _pack_rows([canon[k].astype(BF16) for k, _, _ in group])

    def unpack_weights(packed, group):
        W, row = {}, 0
        for k, r, c in group:
            n_rows = r * c // PACK_W
            W[k] = packed[:, row:row + n_rows].reshape(N_DEV * r, c)
            row += n_rows
        if "upT" in W:
            upT = W.pop("upT")
            W["up_u"], W["up_g"] = upT[:D_FF], upT[D_FF:]
        return W

    def packed_grads(gW, group):
        g = dict(gW)
        if "up_u" in g:
            g["upT"] = jnp.concatenate([g.pop("up_u"), g.pop("up_g")], axis=0)
        return jnp.concatenate([g[k].reshape(N_DEV, r * c // PACK_W, PACK_W) for k, r, c in group], axis=1)

    def unpack_grads(shard_sum, group):
        g, row = {}, 0
        for k, r, c in group:
            n_rows = r * c // PACK_W
            g[k] = shard_sum[row:row + n_rows].reshape(r, c)
            row += n_rows
        return g

    def own_block_in(landed, block):
        return lax.dynamic_update_slice(landed, block[None], (me, 0, 0))

    def gathered_weights(copies, shard, after, group, name):
        return unpack_weights(own_block_in(exchange_wait(copies, after, False, name=name), shard), group)

    def scattered_sum(src, copies, after, tag):
        landed = exchange_wait(copies, after, True, name=f"{tag}_wait")
        mine = lax.dynamic_index_in_dim(src, me, axis=0, keepdims=False)
        return sum_blocks(own_block_in(landed, mine), name=f"{tag}_sum")

    last = DEPTH - 1
    mix0, gathered = allgather_hbm(packed_shard(0, MIX) + conv_gathered[0, 0].astype(BF16), me, name="gather_weights")
    ffn0_shard = packed_shard(0, FFN) + gathered[0, 0].astype(BF16)
    gather_ffn0 = exchange_start(ffn0_shard, False, name="gather_ffn0_start")
    last_shard = packed_shard(last, PACK) + gather_ffn0[4][0, 0].astype(BF16)
    gather_last = exchange_start(last_shard, False, name="gather_last_start")
    started = gather_last[4][0, 0]

    def weights_of(l, h):
        if l == 0:
            W = unpack_weights(mix0, MIX)
            W["ffn"] = lambda after: gathered_weights(gather_ffn0, ffn0_shard, after, FFN, "gather_ffn0_wait")
            return W
        assert l == last
        return gathered_weights(gather_last, last_shard, h, PACK, "gather_last_wait")

    in_flight = {}

    def scatter(key, g_all, dx, name):
        in_flight[key] = (g_all, exchange_start(g_all, True, name=name))
        return dx + in_flight[key][1][4][0, 0]

    def grads_done(l, gW, dh):
        return scatter("last", packed_grads(gW, PACK), dh, "grads_last_start") if l == last else dh

    def ffn_grads_done(l, gW, dx):
        if l != 0:
            return dx
        return scatter("ffn0", packed_grads({k: gW[k] for k in ("up_u", "up_g", "down")}, FFN), dx, "grads_ffn0_start")

    Ps = [small_params(full, l) for l in range(DEPTH)]
    Ps[0]["norm_mix_g"] = Ps[0]["norm_mix_g"] + started
    loss, grad_x, gWs, gPs, d_rel, d_final = local_step(x, mem, loss_target, weights_of, Ps, rel_bias,
                                                        final_norm_g.reshape(1, -1), grads_done, ffn_grads_done)

    shard_grads = {last: unpack_grads(scattered_sum(*in_flight["last"], grad_x, "grads_last"), PACK)}
    shard_grads[0] = unpack_grads(scattered_sum(*in_flight["ffn0"], grad_x, "grads_ffn0"), FFN)

    g_all = packed_grads({k: gWs[0][k] for k, _, _ in MIX}, MIX)
    rows = g_all.shape[1]
    got = pair_exchange(g_all, name="grads_pair_exchange")
    own = lax.dynamic_index_in_dim(g_all.reshape(N_CHIPS, 2, rows, PACK_W), lax.axis_index("c"), axis=1, keepdims=False)
    pair = sum_cast([own.reshape(-1, PACK_W), got.reshape(-1, PACK_W)], GRAD_WIRE, name="grads_pair_sum").reshape(N_CHIPS, rows, PACK_W)
    from_x, from_y, from_xy = chip_exchange(pair, name="grads_chip_exchange")
    mine = lax.dynamic_index_in_dim(pair, 2 * lax.axis_index("x") + lax.axis_index("y"), axis=0, keepdims=False)
    shard_grads[0].update(unpack_grads(sum_cast([mine, from_x, from_y, from_xy], F32, name="grads_chip_sum"), MIX))

    grads = {}
    per_layer = [native_grads(shard_grads[l]) for l in range(DEPTH)]
    for i, n in enumerate(LARGE):
        grads[n] = jnp.stack([per_layer[l][i] for l in range(DEPTH)])

    small_names = [n for n in WEIGHTS if n not in LARGE and n not in ("rel_bias", "final_norm_g")]
    pieces = [gPs[l][n].reshape(-1) for n in small_names for l in range(DEPTH)] + [d_rel.reshape(-1), d_final.reshape(-1), loss[0, :1]]
    sizes = [p.shape[0] for p in pieces]
    _, total = allgather_small(_pad_rows(jnp.concatenate(pieces)), name="allreduce_small", reduce=True)
    total = total.reshape(-1)
    off, it = 0, iter(sizes)
    for n in small_names:
        per = []
        for l in range(DEPTH):
            sz = next(it)
            per.append(total[off:off + sz])
            off += sz
        full_shape = (DEPTH,) + full[n].shape[1:]
        gfull = jnp.stack(per).reshape(full_shape)
        if n in COLUMN_SPLIT_SMALL:
            c = w[n].shape[-1]
            gfull = lax.dynamic_slice_in_dim(gfull, me * c, c, axis=gfull.ndim - 1)
        grads[n] = gfull
    grads["rel_bias"] = total[off:off + rel_bias.size].reshape(rel_bias.shape)
    off += rel_bias.size
    grads["final_norm_g"] = total[off:off + D_MODEL]
    off += D_MODEL
    loss_out = total[off]

    delta, new_m, new_v = {}, {}, {}
    for n in LARGE:
        shape = w[n].shape
        two_d = lambda a: a.reshape(-1, shape[-1])
        d_, m_, v_ = adamw(two_d(w[n]), two_d(grads[n]), two_d(m[n]), two_d(v[n]), name=f"adamw_{n}")
        delta[n], new_m[n], new_v[n] = d_.reshape(shape), m_.reshape(shape), v_.reshape(shape)
    small_all = [n for n in WEIGHTS if n not in LARGE]
    two_d = lambda a: a.reshape(-1, a.shape[-1])
    d_, m_, v_ = adamw_many(*[[two_d(src[n]) for n in small_all] for src in (w, grads, m, v)], name="adamw_small")
    for i, n in enumerate(small_all):
        delta[n], new_m[n], new_v[n] = (a[i].reshape(w[n].shape) for a in (d_, m_, v_))

    return (loss_out, grad_x, *[grads[n] for n in WEIGHTS], *[delta[n] for n in WEIGHTS], *[new_m[n] for n in WEIGHTS],
            *[new_v[n] for n in WEIGHTS])
```

```python
import functools
import math

import numpy as np
import jax
import jax.numpy as jnp
from jax import lax
from jax.experimental import pallas as pl
from jax.experimental.pallas import tpu as pltpu

F32 = jnp.float32
BF16 = jnp.bfloat16
MESH = pl.DeviceIdType.MESH

N_DEV = 8
D_MODEL = 1024
SEQ = 2048
DEPTH = 2
HEAD_DIM = 64
N_HEADS = 4
GROUP_W = N_HEADS * HEAD_DIM
D_FF = 2816
N_MEM = 256
NUM_BUCKETS = 32
MAX_DISTANCE = 2048
BLOCK = 128
DILATIONS = (1, 4, 16)
EPS = 1e-6
LRU_C = 8.0
Q_SCALE = HEAD_DIM ** -0.5
AUX_W = 640
LRU_HALF_W = 128
LRU_HALVES = GROUP_W // LRU_HALF_W
ADAM_LR, ADAM_B1, ADAM_B2, ADAM_EPS, ADAM_WD, ADAM_STEP = 0.001, 0.9, 0.999, 1e-08, 0.01, 10

VMEM_LIMIT_V7X = 48 * 1024 * 1024


def _params(*sem):
    return pltpu.CompilerParams(dimension_semantics=sem if sem else None, vmem_limit_bytes=VMEM_LIMIT_V7X)


def _pick(n, cands):
    for c in cands:
        if n % c == 0:
            return c
    return n


def _largest_tile(n, cap, align):
    best = None
    for t in range(align, min(n, cap) + 1, align):
        if n % t == 0:
            best = t
    return n if best is None else best


def matmul(a, b, *, name, trans_a=False, trans_b=False, out_dtype=F32, residual=None):
    (K, M) = a.shape if trans_a else a.shape[::-1]
    (N, Kb) = b.shape if trans_b else b.shape[::-1]
    assert K == Kb, (a.shape, b.shape)
    tm = _largest_tile(M, 1024 if trans_a else 512, 128)
    tn = _largest_tile(N, 1408, 128)
    tk = _largest_tile(K, 2816, 128)
    nk = K // tk
    a_spec = pl.BlockSpec((tk, tm), lambda i, j, k: (k, i)) if trans_a else pl.BlockSpec((tm, tk), lambda i, j, k: (i, k))
    b_spec = pl.BlockSpec((tn, tk), lambda i, j, k: (j, k)) if trans_b else pl.BlockSpec((tk, tn), lambda i, j, k: (k, j))
    o_spec = pl.BlockSpec((tm, tn), lambda i, j, k: (i, j))
    dims = (((0 if trans_a else 1,), (1 if trans_b else 0,)), ((), ()))
    has_res = residual is not None

    def body(*refs):
        a_ref, b_ref = refs[0], refs[1]
        r_ref = refs[2] if has_res else None
        part = lax.dot_general(a_ref[...].astype(BF16), b_ref[...].astype(BF16), dims, preferred_element_type=F32)
        if nk == 1:
            if has_res:
                part = part + r_ref[...].astype(F32)
            refs[-1][...] = part.astype(out_dtype)
            return
        o_ref, acc_ref = refs[-2], refs[-1]
        k = pl.program_id(2)

        @pl.when(k == 0)
        def _():
            acc_ref[...] = part

        @pl.when(k > 0)
        def _():
            acc_ref[...] += part

        @pl.when(k == nk - 1)
        def _():
            r = acc_ref[...]
            if has_res:
                r = r + r_ref[...].astype(F32)
            o_ref[...] = r.astype(out_dtype)

    ops = (a, b) + ((residual,) if has_res else ())
    return pl.pallas_call(
        body, name=name, grid=(M // tm, N // tn, nk),
        in_specs=[a_spec, b_spec] + ([o_spec] if has_res else []),
        out_specs=o_spec, out_shape=jax.ShapeDtypeStruct((M, N), out_dtype),
        scratch_shapes=[pltpu.VMEM((tm, tn), F32)] if nk > 1 else [],
        compiler_params=_params("parallel", "parallel", "arbitrary"),
    )(*ops)


def rmsnorm_fwd(x, g, *, name):
    R, D = x.shape
    tr = _pick(R, (512, 256))

    def body(x_ref, g_ref, o_ref):
        xv = x_ref[...]
        r = lax.rsqrt(jnp.mean(xv * xv, axis=-1, keepdims=True) + EPS)
        o_ref[...] = (xv * r * g_ref[...]).astype(BF16)

    return pl.pallas_call(
        body, name=name, grid=(R // tr,),
        in_specs=[pl.BlockSpec((tr, D), lambda i: (i, 0)), pl.BlockSpec((1, D), lambda i: (0, 0))],
        out_specs=pl.BlockSpec((tr, D), lambda i: (i, 0)), out_shape=jax.ShapeDtypeStruct((R, D), BF16),
        compiler_params=_params("parallel"),
    )(x, g)


def rmsnorm_bwd(x, g, dh, dres, *, name):
    R, D = x.shape
    tr = _pick(R, (512, 256))
    has_res = dres is not None

    def body(*refs):
        x_ref, g_ref, dh_ref = refs[:3]
        dx_ref, dg_ref = refs[-2], refs[-1]
        xv = x_ref[...]
        r = lax.rsqrt(jnp.mean(xv * xv, axis=-1, keepdims=True) + EPS)
        n = xv * r
        dhv = dh_ref[...]
        dn = dhv * g_ref[...]
        dx = r * (dn - n * jnp.mean(dn * n, axis=-1, keepdims=True))
        if has_res:
            dx = dx + refs[3][...]
        dx_ref[...] = dx
        part = jnp.sum(dhv * n, axis=0, keepdims=True)

        @pl.when(pl.program_id(0) == 0)
        def _():
            dg_ref[...] = part

        @pl.when(pl.program_id(0) > 0)
        def _():
            dg_ref[...] += part

    row = pl.BlockSpec((tr, D), lambda i: (i, 0))
    vec = pl.BlockSpec((1, D), lambda i: (0, 0))
    ops = (x, g, dh) + ((dres,) if has_res else ())
    return pl.pallas_call(
        body, name=name, grid=(R // tr,),
        in_specs=[row, vec, row] + ([row] if has_res else []),
        out_specs=[row, vec],
        out_shape=[jax.ShapeDtypeStruct((R, D), F32), jax.ShapeDtypeStruct((1, D), F32)],
        compiler_params=_params("arbitrary"),
    )(*ops)


_SQRT_HALF = 0.7071067811865476
_INV_SQRT_2PI = 0.3989422804014327


def _erf(x):
    ax = jnp.abs(x)
    t = 1.0 / (1.0 + 0.3275911 * ax)
    poly = t * (0.254829592 + t * (-0.284496736 + t * (1.421413741 + t * (-1.453152027 + t * 1.061405429))))
    y = 1.0 - poly * jnp.exp(-ax * ax)
    return jnp.where(x < 0, -y, y)


def _gelu_cdf(x):
    return 0.5 * (1.0 + _erf(x * _SQRT_HALF))


def _gelu_and_grad(x):
    cdf = _gelu_cdf(x)
    return x * cdf, cdf + x * _INV_SQRT_2PI * jnp.exp(-0.5 * x * x)


def _shift_down(main, halo, first, shifts):
    halo = jnp.where(first, 0.0, halo)
    ext = jnp.concatenate([halo, main], axis=0)
    return [pltpu.roll(ext, s, 0)[8:] for s in shifts]


def _conv3(main, halo, first, w, b):
    m1, m2 = _shift_down(main, halo, first, (1, 2))
    return ((b + w[0:1] * m2) + w[1:2] * m1) + w[2:3] * main, m1, m2


def glu_fwd(hu, hg, wu, wg, bu, bg, *, name):
    T, F = hu.shape
    tm, tf = 512, _largest_tile(F, 704, 128)
    hb = tm // 8
    blocks_per_example = SEQ // tm

    def body(hu_ref, hg_ref, hau_ref, hag_ref, wu_ref, wg_ref, bu_ref, bg_ref, o_ref):
        first = pl.program_id(0) % blocks_per_example == 0
        up, _, _ = _conv3(hu_ref[...], hau_ref[...], first, wu_ref[...], bu_ref[...])
        gate, _, _ = _conv3(hg_ref[...], hag_ref[...], first, wg_ref[...], bg_ref[...])
        o_ref[...] = (gate * _gelu_cdf(gate) * up).astype(BF16)

    main = pl.BlockSpec((tm, tf), lambda i, j: (i, j))
    halo = pl.BlockSpec((8, tf), lambda i, j: (jnp.maximum(i * hb - 1, 0), j))
    w3 = pl.BlockSpec((3, tf), lambda i, j: (0, j))
    b1 = pl.BlockSpec((1, tf), lambda i, j: (0, j))
    return pl.pallas_call(
        body, name=name, grid=(T // tm, F // tf),
        in_specs=[main, main, halo, halo, w3, w3, b1, b1],
        out_specs=main, out_shape=jax.ShapeDtypeStruct((T, F), BF16),
        compiler_params=_params("parallel", "parallel"),
    )(hu, hg, hu, hg, wu, wg, bu, bg)


def glu_bwd(hu, hg, dact, wu, wg, bu, bg, *, name):
    T, F = hu.shape
    tm, tf = 512, _largest_tile(F, 704, 128)
    hb = tm // 8
    blocks_per_example = SEQ // tm
    n_halo_blocks = T // 8
    n_ext = tm + 8

    def body(hu_ref, hg_ref, hau_ref, hag_ref, hnu_ref, hng_ref, da_ref, dan_ref, wu_ref, wg_ref, bu_ref, bg_ref,
             du_ref, dg_ref, dwu_ref, dwg_ref, dbu_ref, dbg_ref):
        i = pl.program_id(1)
        first = i % blocks_per_example == 0
        last = i % blocks_per_example == blocks_per_example - 1
        wu, wg = wu_ref[...], wg_ref[...]

        def conv_ext(main_ref, prev_ref, next_ref, w, b):
            ext = jnp.concatenate([jnp.where(first, 0.0, prev_ref[...]), main_ref[...], next_ref[...]], axis=0)
            x0, x1, x2 = ext[8:], pltpu.roll(ext, 1, 0)[8:], pltpu.roll(ext, 2, 0)[8:]
            return ((b + w[0:1] * x2) + w[1:2] * x1) + w[2:3] * x0, x0, x1, x2

        up, xu, u1, u2 = conv_ext(hu_ref, hau_ref, hnu_ref, wu, bu_ref[...])
        gate, xg, g1, g2 = conv_ext(hg_ref, hag_ref, hng_ref, wg, bg_ref[...])
        act, dact_dgate = _gelu_and_grad(gate)
        da = jnp.concatenate([da_ref[...], jnp.where(last, 0.0, dan_ref[...])], axis=0)
        dup = da * act
        dgate = da * up * dact_dgate

        def conv_t(d, w):
            return (w[2:3] * d[:tm] + w[1:2] * pltpu.roll(d, n_ext - 1, 0)[:tm] + w[0:1] * pltpu.roll(d, n_ext - 2, 0)[:tm]).astype(BF16)

        du_ref[...] = conv_t(dup, wu)
        dg_ref[...] = conv_t(dgate, wg)

        def sums(d, x0, x1, x2):
            s = lambda v: jnp.sum(v[:tm], axis=0, keepdims=True)
            return jnp.concatenate([s(d * x2), s(d * x1), s(d * x0)], axis=0), s(d)

        pwu, pbu = sums(dup, xu, u1, u2)
        pwg, pbg = sums(dgate, xg, g1, g2)

        @pl.when(i == 0)
        def _():
            dwu_ref[...] = pwu
            dwg_ref[...] = pwg
            dbu_ref[...] = pbu
            dbg_ref[...] = pbg

        @pl.when(i > 0)
        def _():
            dwu_ref[...] += pwu
            dwg_ref[...] += pwg
            dbu_ref[...] += pbu
            dbg_ref[...] += pbg

    main = pl.BlockSpec((tm, tf), lambda j, i: (i, j))
    before = pl.BlockSpec((8, tf), lambda j, i: (jnp.maximum(i * hb - 1, 0), j))
    after = pl.BlockSpec((8, tf), lambda j, i: (jnp.minimum((i + 1) * hb, n_halo_blocks - 1), j))
    w3 = pl.BlockSpec((3, tf), lambda j, i: (0, j))
    b1 = pl.BlockSpec((1, tf), lambda j, i: (0, j))
    sd = jax.ShapeDtypeStruct
    return pl.pallas_call(
        body, name=name, grid=(F // tf, T // tm),
        in_specs=[main, main, before, before, after, after, main, after, w3, w3, b1, b1],
        out_specs=[main, main, w3, w3, b1, b1],
        out_shape=[sd((T, F), BF16), sd((T, F), BF16), sd((3, F), F32), sd((3, F), F32), sd((1, F), F32), sd((1, F), F32)],
        compiler_params=_params("parallel", "arbitrary"),
    )(hu, hg, hu, hg, hu, hg, dact, dact, wu, wg, bu, bg)


def loss_head(x, g, target, *, name):
    T, D = x.shape
    tr = 256

    def body(x_ref, g_ref, t_ref, loss_ref, dx_ref, dg_ref):
        xv = x_ref[...]
        gv = g_ref[...]
        r = lax.rsqrt(jnp.mean(xv * xv, axis=-1, keepdims=True) + EPS)
        n = xv * r
        err = n * gv - t_ref[...]
        part_loss = jnp.zeros((1, 128), F32) + 0.5 * jnp.sum(jnp.mean(err * err, axis=-1, keepdims=True))
        dy = err * (1.0 / D)
        dn = dy * gv
        dx_ref[...] = r * (dn - n * jnp.mean(dn * n, axis=-1, keepdims=True))
        part_g = jnp.sum(dy * n, axis=0, keepdims=True)

        @pl.when(pl.program_id(0) == 0)
        def _():
            loss_ref[...] = part_loss
            dg_ref[...] = part_g

        @pl.when(pl.program_id(0) > 0)
        def _():
            loss_ref[...] += part_loss
            dg_ref[...] += part_g

    row = pl.BlockSpec((tr, D), lambda i: (i, 0))
    vec = pl.BlockSpec((1, D), lambda i: (0, 0))
    sd = jax.ShapeDtypeStruct
    return pl.pallas_call(
        body, name=name, grid=(T // tr,),
        in_specs=[row, vec, row],
        out_specs=[pl.BlockSpec((1, 128), lambda i: (0, 0)), row, vec],
        out_shape=[sd((1, 128), F32), sd((T, D), F32), sd((1, D), F32)],
        compiler_params=_params("arbitrary"),
    )(x, g, target)


def adamw(w, g, m, v, *, name):
    R, C = w.shape
    tr = _pick(R, (256, 128, 64, 32, 16, 8))

    def body(w_ref, g_ref, m_ref, v_ref, d_ref, nm_ref, nv_ref):
        gv = g_ref[...]
        mn = ADAM_B1 * m_ref[...] + (1.0 - ADAM_B1) * gv
        vn = ADAM_B2 * v_ref[...] + (1.0 - ADAM_B2) * (gv * gv)
        m_hat = mn / (1.0 - ADAM_B1 ** ADAM_STEP)
        v_hat = vn / (1.0 - ADAM_B2 ** ADAM_STEP)
        d_ref[...] = -ADAM_LR * (m_hat / (jnp.sqrt(v_hat) + ADAM_EPS) + ADAM_WD * w_ref[...])
        nm_ref[...] = mn
        nv_ref[...] = vn

    blk = pl.BlockSpec((tr, C), lambda i: (i, 0))
    sd = jax.ShapeDtypeStruct((R, C), F32)
    return pl.pallas_call(
        body, name=name, grid=(R // tr,), in_specs=[blk] * 4, out_specs=[blk] * 3, out_shape=[sd] * 3,
        compiler_params=_params("parallel"),
    )(w, g, m, v)


def adamw_many(ws, gs, ms, vs, *, name):
    n = len(ws)

    def body(*refs):
        ins, outs = refs[:4 * n], refs[4 * n:]
        for i in range(n):
            w_ref, g_ref, m_ref, v_ref = ins[i], ins[n + i], ins[2 * n + i], ins[3 * n + i]
            gv = g_ref[...]
            mn = ADAM_B1 * m_ref[...] + (1.0 - ADAM_B1) * gv
            vn = ADAM_B2 * v_ref[...] + (1.0 - ADAM_B2) * (gv * gv)
            m_hat = mn / (1.0 - ADAM_B1 ** ADAM_STEP)
            v_hat = vn / (1.0 - ADAM_B2 ** ADAM_STEP)
            outs[i][...] = -ADAM_LR * (m_hat / (jnp.sqrt(v_hat) + ADAM_EPS) + ADAM_WD * w_ref[...])
            outs[n + i][...] = mn
            outs[2 * n + i][...] = vn

    vm = pl.BlockSpec(memory_space=pltpu.VMEM)
    shapes = [jax.ShapeDtypeStruct(w.shape, F32) for w in ws]
    res = pl.pallas_call(
        body, name=name, in_specs=[vm] * (4 * n), out_specs=[vm] * (3 * n), out_shape=shapes * 3, compiler_params=_params(),
    )(*ws, *gs, *ms, *vs)
    return res[:n], res[n:2 * n], res[2 * n:]


def _softplus(x):
    return jnp.maximum(x, 0.0) + jnp.log(1.0 + jnp.exp(-jnp.abs(x)))


def _lru_gates(x, cw, cb, wa, ba, wx, bx, lam):
    S = x.shape[0]
    row = lax.broadcasted_iota(jnp.int32, (S, 1), 0)

    def back(s):
        return jnp.where(row >= s, pltpu.roll(x, s, 0), 0.0)

    xc = (((cb + cw[0:1] * back(3)) + cw[1:2] * back(2)) + cw[2:3] * back(1)) + cw[3:4] * x
    xb = xc.astype(BF16)
    r = jax.nn.sigmoid(jnp.dot(xb, wa, preferred_element_type=F32) + ba)
    ig = jax.nn.sigmoid(jnp.dot(xb, wx, preferred_element_type=F32) + bx)
    sp = _softplus(-lam)
    la = -LRU_C * r * sp
    a = jnp.exp(la)
    y = 2.0 * la
    one_minus_a2 = jnp.where(y > -0.05, -y * (1.0 + y * (0.5 + y * (1.0 / 6.0 + y * (1.0 / 24.0)))), 1.0 - jnp.exp(y))
    mm = jnp.sqrt(one_minus_a2)
    return xc, xb, r, ig, sp, a, mm


def lru_fwd(aux, cw, cb, wa, ba, wx, bx, lam, *, name):
    T = aux.shape[0]
    S, C = SEQ, LRU_HALF_W

    def body(x_ref, g_ref, cw_ref, cb_ref, wa_ref, ba_ref, wx_ref, bx_ref, lam_ref, o_ref, h_ref, a_s, u_s):
        xc, _, r, ig, sp, a, mm = _lru_gates(x_ref[...], cw_ref[...], cb_ref[...], wa_ref[...], ba_ref[...],
                                             wx_ref[...], bx_ref[...], lam_ref[...])
        a_s[...] = a
        u_s[...] = mm * (ig * xc)

        def group(i, h):
            base = pl.multiple_of(i * 8, 8)
            a8 = a_s[pl.ds(base, 8), :]
            u8 = u_s[pl.ds(base, 8), :]
            for rr in range(8):
                h = a8[rr:rr + 1] * h + u8[rr:rr + 1]
                h_ref[pl.ds(base + rr, 1), :] = h
            return h

        lax.fori_loop(0, S // 8, group, jnp.zeros((1, C), F32))
        gate = g_ref[...]
        o_ref[...] = (h_ref[...] * (gate * _gelu_cdf(gate))).astype(BF16)

    blk = lambda col: pl.BlockSpec((S, C), lambda c, b: (b, col + c))
    par = lambda rows: pl.BlockSpec((rows, C), lambda c, b: (0, c))
    sq = pl.BlockSpec((None, C, C), lambda c, b: (c, 0, 0))
    sd = jax.ShapeDtypeStruct
    W = LRU_HALVES * C
    return pl.pallas_call(
        body, name=name, grid=(LRU_HALVES, T // S),
        in_specs=[blk(0), blk(LRU_HALVES), par(4), par(1), sq, par(1), sq, par(1), par(1)],
        out_specs=[blk(0), blk(0)], out_shape=[sd((T, W), BF16), sd((T, W), F32)],
        scratch_shapes=[pltpu.VMEM((S, C), F32), pltpu.VMEM((S, C), F32)],
        compiler_params=_params("parallel", "parallel"),
    )(aux, aux, cw, cb, wa, ba, wx, bx, lam)


def lru_bwd(aux, h, dmixed, cw, cb, wa, ba, wx, bx, lam, *, name):
    T = aux.shape[0]
    S, C = SEQ, LRU_HALF_W

    def body(x_ref, g_ref, h_ref, do_ref, cw_ref, cb_ref, wa_ref, ba_ref, wx_ref, bx_ref, lam_ref,
             dx_ref, dgate_ref, dcw_ref, dcb_ref, dwa_ref, dba_ref, dwx_ref, dbx_ref, dlam_ref, a_s, d_s):
        x = x_ref[...]
        cw = cw_ref[...]
        lam = lam_ref[...]
        xc, xb, r, ig, sp, a, mm = _lru_gates(x, cw, cb_ref[...], wa_ref[...], ba_ref[...], wx_ref[...], bx_ref[...], lam)
        gate = g_ref[...]
        gl, dgl = _gelu_and_grad(gate)
        dout = do_ref[...]
        hv = h_ref[...]
        dgate_ref[...] = dout * hv * dgl
        a_s[...] = a
        d_s[...] = dout * gl

        def group(i, c):
            base = pl.multiple_of((S // 8 - 1 - i) * 8, 8)
            a8 = a_s[pl.ds(base, 8), :]
            d8 = d_s[pl.ds(base, 8), :]
            for rr in range(7, -1, -1):
                d = d8[rr:rr + 1] + c
                d_s[pl.ds(base + rr, 1), :] = d
                c = a8[rr:rr + 1] * d
            return c

        lax.fori_loop(0, S // 8, group, jnp.zeros((1, C), F32))
        row = lax.broadcasted_iota(jnp.int32, (S, 1), 0)
        dht = d_s[...]
        h_prev = jnp.where(row >= 1, pltpu.roll(hv, 1, 0), 0.0)
        da = dht * h_prev
        gx = ig * xc
        dmm = dht * gx
        dig = dht * mm * xc
        dxc = dht * mm * ig
        dla = da * a - dmm * (a * a) / mm
        dr = dla * (-LRU_C * sp)
        dsp = jnp.sum(dla * (-LRU_C * r), axis=0, keepdims=True)
        dlam = dsp * (-jax.nn.sigmoid(-lam))
        dpa = dr * r * (1.0 - r)
        dpx = dig * ig * (1.0 - ig)
        dpa_b, dpx_b = dpa.astype(BF16), dpx.astype(BF16)
        nt = (((1,), (1,)), ((), ()))
        tn = (((0,), (0,)), ((), ()))
        dxc = dxc + lax.dot_general(dpa_b, wa_ref[...], nt, preferred_element_type=F32) \
                  + lax.dot_general(dpx_b, wx_ref[...], nt, preferred_element_type=F32)
        dwa = lax.dot_general(xb, dpa_b, tn, preferred_element_type=F32)
        dwx = lax.dot_general(xb, dpx_b, tn, preferred_element_type=F32)

        def fwd(v, s):
            return jnp.where(row < S - s, pltpu.roll(v, S - s, 0), 0.0)

        def back(v, s):
            return jnp.where(row >= s, pltpu.roll(v, s, 0), 0.0)

        dx_ref[...] = cw[3:4] * dxc + cw[2:3] * fwd(dxc, 1) + cw[1:2] * fwd(dxc, 2) + cw[0:1] * fwd(dxc, 3)
        s0 = lambda v: jnp.sum(v, axis=0, keepdims=True)
        dcw = jnp.concatenate([s0(dxc * back(x, 3)), s0(dxc * back(x, 2)), s0(dxc * back(x, 1)), s0(dxc * x)], axis=0)
        parts = ((dcw_ref, dcw), (dcb_ref, s0(dxc)), (dwa_ref, dwa), (dba_ref, s0(dpa)), (dwx_ref, dwx),
                 (dbx_ref, s0(dpx)), (dlam_ref, dlam))

        @pl.when(pl.program_id(1) == 0)
        def _():
            for ref, val in parts:
                ref[...] = val

        @pl.when(pl.program_id(1) > 0)
        def _():
            for ref, val in parts:
                ref[...] += val

    blk = lambda col: pl.BlockSpec((S, C), lambda c, b: (b, col + c))
    par = lambda rows: pl.BlockSpec((rows, C), lambda c, b: (0, c))
    sq = pl.BlockSpec((None, C, C), lambda c, b: (c, 0, 0))
    sd = jax.ShapeDtypeStruct
    W = LRU_HALVES * C
    vec = sd((1, W), F32)
    return pl.pallas_call(
        body, name=name, grid=(LRU_HALVES, T // S),
        in_specs=[blk(0), blk(LRU_HALVES), blk(0), blk(3 * LRU_HALVES), par(4), par(1), sq, par(1), sq, par(1), par(1)],
        out_specs=[blk(0), blk(0), par(4), par(1), sq, par(1), sq, par(1), par(1)],
        out_shape=[sd((T, W), F32), sd((T, W), F32), sd((4, W), F32), vec, sd((LRU_HALVES, C, C), F32), vec,
                   sd((LRU_HALVES, C, C), F32), vec, vec],
        scratch_shapes=[pltpu.VMEM((S, C), F32), pltpu.VMEM((S, C), F32)],
        compiler_params=_params("parallel", "arbitrary"),
    )(aux, aux, h, dmixed, cw, cb, wa, ba, wx, bx, lam)


_NT = (((1,), (1,)), ((), ()))
_TN = (((0,), (0,)), ((), ()))


def _dot(a, b, dims=None):
    if dims is None:
        return jnp.dot(a, b, preferred_element_type=F32)
    return lax.dot_general(a, b, dims, preferred_element_type=F32)


def _hs(h):
    return slice(h * HEAD_DIM, (h + 1) * HEAD_DIM)


def cross_fwd(q, kv, *, name):
    T = q.shape[0]
    tq = 512

    def body(q_ref, kv_ref, o_ref):
        for h in range(N_HEADS):
            qh = q_ref[:, _hs(h)] * Q_SCALE
            k = kv_ref[:, _hs(h)]
            v = kv_ref[:, GROUP_W + h * HEAD_DIM:GROUP_W + (h + 1) * HEAD_DIM]
            s = _dot(qh, k, _NT)
            p = jnp.exp(s - jnp.max(s, axis=-1, keepdims=True))
            p = p / jnp.sum(p, axis=-1, keepdims=True)
            o_ref[:, _hs(h)] = _dot(p.astype(BF16), v).astype(BF16)

    per = SEQ // tq
    return pl.pallas_call(
        body, name=name, grid=(T // tq,),
        in_specs=[pl.BlockSpec((tq, GROUP_W), lambda i: (i, 0)), pl.BlockSpec((N_MEM, 2 * GROUP_W), lambda i: (i // per, 0))],
        out_specs=pl.BlockSpec((tq, GROUP_W), lambda i: (i, 0)), out_shape=jax.ShapeDtypeStruct((T, GROUP_W), BF16),
        compiler_params=_params("parallel"),
    )(q, kv)


def cross_bwd(q, kv, do, *, name):
    T = q.shape[0]
    tq = 512
    per = SEQ // tq

    def body(q_ref, kv_ref, do_ref, dq_ref, dkv_ref):
        first = pl.program_id(0) % per == 0
        for h in range(N_HEADS):
            vs = slice(GROUP_W + h * HEAD_DIM, GROUP_W + (h + 1) * HEAD_DIM)
            qh = q_ref[:, _hs(h)] * Q_SCALE
            k = kv_ref[:, _hs(h)]
            v = kv_ref[:, vs]
            doh = do_ref[:, _hs(h)].astype(BF16)
            s = _dot(qh, k, _NT)
            p = jnp.exp(s - jnp.max(s, axis=-1, keepdims=True))
            p = p / jnp.sum(p, axis=-1, keepdims=True)
            dp = _dot(doh, v, _NT)
            ds = (p * (dp - jnp.sum(p * dp, axis=-1, keepdims=True))).astype(BF16)
            dq_ref[:, _hs(h)] = (_dot(ds, k) * Q_SCALE).astype(BF16)
            dk = _dot(ds, qh, _TN)
            dv = _dot(p.astype(BF16), doh, _TN)

            @pl.when(first)
            def _():
                dkv_ref[:, _hs(h)] = dk
                dkv_ref[:, vs] = dv

            @pl.when(jnp.logical_not(first))
            def _():
                dkv_ref[:, _hs(h)] += dk
                dkv_ref[:, vs] += dv

    qb = pl.BlockSpec((tq, GROUP_W), lambda i: (i, 0))
    kvb = pl.BlockSpec((N_MEM, 2 * GROUP_W), lambda i: (i // per, 0))
    sd = jax.ShapeDtypeStruct
    return pl.pallas_call(
        body, name=name, grid=(T // tq,),
        in_specs=[qb, kvb, qb], out_specs=[qb, kvb],
        out_shape=[sd((T, GROUP_W), BF16), sd(kv.shape, F32)],
        compiler_params=_params("arbitrary"),
    )(q, kv, do)


NB = SEQ // BLOCK
NEG = -1e30


def _split_dot(x, tri):
    hi = x.astype(BF16)
    lo = (x - hi.astype(F32)).astype(BF16)
    return _dot(hi, tri) + _dot(lo, tri)


def _blk(i):
    return pl.ds(pl.multiple_of(i * BLOCK, BLOCK), BLOCK)


def _iotas():
    row = lax.broadcasted_iota(jnp.int32, (BLOCK, BLOCK), 0)
    col = lax.broadcasted_iota(jnp.int32, (BLOCK, BLOCK), 1)
    return row, col


def _sb_scores(q, k, mask, later, csum, want_sigmoid=False):
    z = _dot(q, k, _NT)
    lk = -_softplus(z)
    if mask is not None:
        lk = jnp.where(mask, lk, 0.0)
    lka = _split_dot(lk, later) + csum
    att = jnp.exp(z + lk + lka)
    sg = jnp.exp(z + lk) if want_sigmoid else None
    if mask is not None:
        att = jnp.where(mask, att, 0.0)
        sg = jnp.where(mask, sg, 0.0) if want_sigmoid else None
    return att, sg, lk


def _rowsum(v):
    return jnp.sum(v, axis=1, keepdims=True)


HEADS = tuple(range(N_HEADS))


def _qkv_specs(first_col):
    return [pl.BlockSpec((SEQ, GROUP_W), lambda b, c=first_col + j: (b, c)) for j in range(3)]


LANES = 128
CUM_BLK = 256


def col_to_row(c):
    b = c.shape[0] // SEQ
    return c.reshape(b, SEQ, LANES)[:, :, :8].transpose(0, 2, 1).reshape(b * 8, SEQ)


def row_to_col(r):
    b = r.shape[0] // 8
    c = r.reshape(b, 8, SEQ).transpose(0, 2, 1)
    return jnp.pad(c, ((0, 0), (0, 0), (0, LANES - 8))).reshape(b * SEQ, LANES)


def fox_prep(aux, bf, *, name):
    T = aux.shape[0]

    def body(f_ref, b_ref, o_ref):
        row = lax.broadcasted_iota(jnp.int32, (CUM_BLK, CUM_BLK), 0)
        col = lax.broadcasted_iota(jnp.int32, (CUM_BLK, CUM_BLK), 1)
        upto = (col <= row).astype(BF16)
        carry = jnp.zeros((1, LANES), F32)
        for n in range(SEQ // CUM_BLK):
            rows = slice(n * CUM_BLK, (n + 1) * CUM_BLK)
            logf = -_softplus(-(f_ref[rows, :] + b_ref[...]))
            hi = logf.astype(BF16)
            lo = (logf - hi.astype(F32)).astype(BF16)
            cum = _dot(upto, hi) + _dot(upto, lo) + carry
            o_ref[rows, :] = cum
            carry = cum[CUM_BLK - 1:CUM_BLK]

    return pl.pallas_call(
        body, name=name, grid=(T // SEQ,),
        in_specs=[pl.BlockSpec((SEQ, LANES), lambda b: (b, 4)), pl.BlockSpec((1, LANES), lambda b: (0, 0))],
        out_specs=pl.BlockSpec((SEQ, LANES), lambda b: (b, 0)), out_shape=jax.ShapeDtypeStruct((T, LANES), F32),
        compiler_params=_params("parallel"),
    )(aux, bf)


def fox_prep_bwd(aux, bf, dcum, *, name):
    T = aux.shape[0]

    def body(f_ref, b_ref, d_ref, df_ref, db_ref):
        row = lax.broadcasted_iota(jnp.int32, (CUM_BLK, CUM_BLK), 0)
        col = lax.broadcasted_iota(jnp.int32, (CUM_BLK, CUM_BLK), 1)
        onward = (col >= row).astype(BF16)
        carry = jnp.zeros((1, LANES), F32)
        tot = jnp.zeros((1, LANES), F32)
        for n in range(SEQ // CUM_BLK - 1, -1, -1):
            rows = slice(n * CUM_BLK, (n + 1) * CUM_BLK)
            d = d_ref[rows, :]
            hi = d.astype(BF16)
            lo = (d - hi.astype(F32)).astype(BF16)
            dlogf = _dot(onward, hi) + _dot(onward, lo) + carry
            carry = dlogf[0:1]
            df = dlogf * jax.nn.sigmoid(-(f_ref[rows, :] + b_ref[...]))
            df_ref[rows, :] = df
            tot = tot + jnp.sum(df, axis=0, keepdims=True)

        @pl.when(pl.program_id(0) == 0)
        def _():
            db_ref[...] = tot

        @pl.when(pl.program_id(0) > 0)
        def _():
            db_ref[...] += tot

    blk = pl.BlockSpec((SEQ, LANES), lambda b: (b, 0))
    vec = pl.BlockSpec((1, LANES), lambda b: (0, 0))
    sd = jax.ShapeDtypeStruct
    return pl.pallas_call(
        body, name=name, grid=(T // SEQ,),
        in_specs=[pl.BlockSpec((SEQ, LANES), lambda b: (b, 4)), vec, blk],
        out_specs=[blk, vec], out_shape=[sd((T, LANES), F32), sd((1, LANES), F32)],
        compiler_params=_params("arbitrary"),
    )(aux, bf, dcum)


def _fox_logits(q, k, cq, ck, mask):
    z = _dot(q, k, _NT) + cq - ck
    return z if mask is None else jnp.where(mask, z, NEG)


def fox_fwd(qkv, cumc, cumr, *, name):
    T = qkv.shape[0]

    def body(q_ref, k_ref, v_ref, cc_ref, cr_ref, o_ref, lse_ref, z_s):
        row, col = _iotas()
        causal = col <= row
        lse_ref[...] = jnp.zeros_like(lse_ref)

        def qblock(i, _):
            qs = [q_ref[_blk(i), _hs(h)] * Q_SCALE for h in HEADS]
            cqs = [cc_ref[_blk(i), h:h + 1] for h in HEADS]

            def logits(j, mask, ms):
                out = []
                for h in HEADS:
                    z = _fox_logits(qs[h], k_ref[_blk(j), _hs(h)], cqs[h], cr_ref[h:h + 1, _blk(j)], mask)
                    z_s[h, j] = z
                    out.append(jnp.maximum(ms[h], jnp.max(z, axis=1, keepdims=True)))
                return tuple(out)

            ms = logits(i, causal, (jnp.full((BLOCK, 1), NEG, F32),) * N_HEADS)
            ms = lax.fori_loop(0, i, lambda j, c: logits(j, None, c), ms)

            def values(j, carry):
                out = []
                for h in HEADS:
                    acc, l = carry[h]
                    p = jnp.exp(z_s[h, j] - ms[h])
                    out.append((acc + _dot(p.astype(BF16), v_ref[_blk(j), _hs(h)]), l + _rowsum(p)))
                return tuple(out)

            zero = (jnp.zeros((BLOCK, HEAD_DIM), F32), jnp.zeros((BLOCK, 1), F32))
            res = lax.fori_loop(0, i + 1, values, (zero,) * N_HEADS)
            for h in HEADS:
                acc, l = res[h]
                o_ref[_blk(i), _hs(h)] = (acc / l).astype(BF16)
                lse_ref[_blk(i), h:h + 1] = ms[h] + jnp.log(l)
            return 0

        lax.fori_loop(0, NB, qblock, 0)

    out = pl.BlockSpec((SEQ, GROUP_W), lambda b: (b, 0))
    colb = pl.BlockSpec((SEQ, LANES), lambda b: (b, 0))
    sd = jax.ShapeDtypeStruct
    return pl.pallas_call(
        body, name=name, grid=(T // SEQ,),
        in_specs=_qkv_specs(3) + [colb, pl.BlockSpec((8, SEQ), lambda b: (b, 0))],
        out_specs=[out, colb], out_shape=[sd((T, GROUP_W), BF16), sd((T, LANES), F32)],
        scratch_shapes=[pltpu.VMEM((N_HEADS, NB, BLOCK, BLOCK), F32)],
        compiler_params=_params("parallel"),
    )(qkv, qkv, qkv, cumc, cumr)


def fox_bwd(qkv, cumc, cumr, lse, dmixed, *, name):
    T = qkv.shape[0]

    def body(q_ref, k_ref, v_ref, cc_ref, cr_ref, lse_ref, do_ref, dq_ref, dk_ref, dv_ref, dcc_ref, dcr_ref, p_s, dp_s):
        row, col = _iotas()
        causal = col <= row
        dk_ref[...] = jnp.zeros_like(dk_ref)
        dv_ref[...] = jnp.zeros_like(dv_ref)
        dcc_ref[...] = jnp.zeros_like(dcc_ref)
        dcr_ref[...] = jnp.zeros_like(dcr_ref)

        def qblock(i, _):
            qs = [q_ref[_blk(i), _hs(h)] * Q_SCALE for h in HEADS]
            dos = [do_ref[_blk(i), _hs(h)].astype(BF16) for h in HEADS]
            cqs = [cc_ref[_blk(i), h:h + 1] for h in HEADS]
            lses = [lse_ref[_blk(i), h:h + 1] for h in HEADS]

            def probs(j, mask, deltas):
                out = []
                for h in HEADS:
                    z = _fox_logits(qs[h], k_ref[_blk(j), _hs(h)], cqs[h], cr_ref[h:h + 1, _blk(j)], mask)
                    p = jnp.exp(z - lses[h])
                    dp = _dot(dos[h], v_ref[_blk(j), _hs(h)], _NT)
                    p_s[h, j] = p
                    dp_s[h, j] = dp
                    out.append(deltas[h] + _rowsum(p * dp))
                return tuple(out)

            deltas = probs(i, causal, (jnp.zeros((BLOCK, 1), F32),) * N_HEADS)
            deltas = lax.fori_loop(0, i, lambda j, c: probs(j, None, c), deltas)

            def kblock(j, carry):
                out = []
                for h in HEADS:
                    dq, dcq = carry[h]
                    p = p_s[h, j]
                    ds = p * (dp_s[h, j] - deltas[h])
                    dsb = ds.astype(BF16)
                    dk_ref[_blk(j), _hs(h)] += _dot(dsb, qs[h], _TN)
                    dv_ref[_blk(j), _hs(h)] += _dot(p.astype(BF16), dos[h], _TN)
                    dcr_ref[h:h + 1, _blk(j)] -= jnp.sum(ds, axis=0, keepdims=True)
                    out.append((dq + _dot(dsb, k_ref[_blk(j), _hs(h)]), dcq + _rowsum(ds)))
                return tuple(out)

            zero = (jnp.zeros((BLOCK, HEAD_DIM), F32), jnp.zeros((BLOCK, 1), F32))
            res = lax.fori_loop(0, i + 1, kblock, (zero,) * N_HEADS)
            for h in HEADS:
                dq_ref[_blk(i), _hs(h)] = res[h][0] * Q_SCALE
                dcc_ref[_blk(i), h:h + 1] = res[h][1]
            return 0

        lax.fori_loop(0, NB, qblock, 0)

    out = pl.BlockSpec((SEQ, GROUP_W), lambda b: (b, 0))
    colb = pl.BlockSpec((SEQ, LANES), lambda b: (b, 0))
    rowb = pl.BlockSpec((8, SEQ), lambda b: (b, 0))
    sd = jax.ShapeDtypeStruct
    big = sd((T, GROUP_W), F32)
    return pl.pallas_call(
        body, name=name, grid=(T // SEQ,),
        in_specs=_qkv_specs(3) + [colb, rowb, colb, pl.BlockSpec((SEQ, GROUP_W), lambda b: (b, 1))],
        out_specs=[out, out, out, colb, rowb],
        out_shape=[big, big, big, sd((T, LANES), F32), sd((T // SEQ * 8, SEQ), F32)],
        scratch_shapes=[pltpu.VMEM((N_HEADS, NB, BLOCK, BLOCK), F32), pltpu.VMEM((N_HEADS, NB, BLOCK, BLOCK), F32)],
        compiler_params=_params("parallel"),
    )(qkv, qkv, qkv, cumc, cumr, lse, dmixed)


CHUNK = 256
WIDE = N_HEADS * CHUNK
NCH = SEQ // CHUNK


def _seg(h):
    return slice(h * CHUNK, (h + 1) * CHUNK)


def _chunk_rows(c):
    return pl.ds(pl.multiple_of(c * CHUNK, CHUNK), CHUNK)


def _wide_consts():
    r = lax.broadcasted_iota(jnp.int32, (WIDE, GROUP_W), 0)
    f = lax.broadcasted_iota(jnp.int32, (WIDE, GROUP_W), 1)
    bd = (r // CHUNK) == (f // HEAD_DIM)
    row = lax.broadcasted_iota(jnp.int32, (BLOCK, WIDE), 0)
    key = lax.broadcasted_iota(jnp.int32, (BLOCK, WIDE), 1) % CHUNK
    return bd, row, key


def _block_diag(x, bd):
    return jnp.where(bd, jnp.concatenate([x] * N_HEADS, axis=0), jnp.zeros((), x.dtype))


def _fold_heads(w, bd):
    w = jnp.where(bd, w, 0.0)
    return (w[0:CHUNK] + w[CHUNK:2 * CHUNK]) + (w[2 * CHUNK:3 * CHUNK] + w[3 * CHUNK:])


def _widen(cols):
    return jnp.concatenate([jnp.broadcast_to(c, (BLOCK, CHUNK)) for c in cols], axis=1)


def _head_rowsums(w):
    return [jnp.sum(w[:, _seg(h)], axis=1, keepdims=True) for h in HEADS]


def _tri_wide(x, tri):
    hi = x.astype(BF16)
    lo = (x - hi.astype(F32)).astype(BF16)
    y = _dot(jnp.concatenate([hi[:, _seg(h)] for h in HEADS] + [lo[:, _seg(h)] for h in HEADS], axis=0), tri)
    return jnp.concatenate([y[h * BLOCK:(h + 1) * BLOCK] + y[(N_HEADS + h) * BLOCK:(N_HEADS + h + 1) * BLOCK] for h in HEADS], axis=1)


def _feature_widen(cols):
    return jnp.concatenate([jnp.broadcast_to(c, (BLOCK, HEAD_DIM)) for c in cols], axis=1)


def _loop_by_two(n, index, body, carry):
    odd = n % 2
    carry = lax.fori_loop(0, odd, lambda _, cr: body(index(0), cr), carry)
    return lax.fori_loop(0, n // 2, lambda t, cr: body(index(odd + 2 * t + 1), body(index(odd + 2 * t), cr)), carry)


def _sbw_scores(q, kbd, later):
    z = _dot(q, kbd, _NT)
    lk = -_softplus(z)
    return z + lk, lk, _tri_wide(lk, later)


def _sbw_tile(q, kbd, mask, later, csum):
    z = _dot(q, kbd, _NT)
    lk = -_softplus(z)
    if mask is not None:
        lk = jnp.where(mask, lk, 0.0)
    e = z + lk
    att = jnp.exp(e + _tri_wide(lk, later) + csum)
    if mask is not None:
        att = jnp.where(mask, att, 0.0)
    return att, e, lk


def sbw_fwd(qkv, *, name):
    T = qkv.shape[0]

    def body(q_ref, k_ref, v_ref, o_ref):
        bd, row, key = _wide_consts()
        r2 = lax.broadcasted_iota(jnp.int32, (CHUNK, CHUNK), 0)
        c2 = lax.broadcasted_iota(jnp.int32, (CHUNK, CHUNK), 1)
        later = (r2 > c2).astype(BF16)

        def qblock(i, _):
            q = q_ref[_blk(i), :] * Q_SCALE
            cd = i // 2
            strict = key < row + BLOCK * (i % 2)

            def tile(c, mask, carry):
                acc, csum = carry
                att, _, lk = _sbw_tile(q, _block_diag(k_ref[_chunk_rows(c), :], bd), mask, later, csum)
                acc = acc + _dot(att.astype(BF16), _block_diag(v_ref[_chunk_rows(c), :], bd))
                return acc, csum + _widen(_head_rowsums(lk))

            def two_tiles(c1, carry):
                acc, csum = carry
                e1, lk1, t1 = _sbw_scores(q, _block_diag(k_ref[_chunk_rows(c1), :], bd), later)
                e2, lk2, t2 = _sbw_scores(q, _block_diag(k_ref[_chunk_rows(c1 - 1), :], bd), later)
                att1 = jnp.exp(e1 + t1 + csum)
                csum = csum + _widen(_head_rowsums(lk1))
                att2 = jnp.exp(e2 + t2 + csum)
                csum = csum + _widen(_head_rowsums(lk2))
                acc = acc + _dot(att1.astype(BF16), _block_diag(v_ref[_chunk_rows(c1), :], bd))
                acc = acc + _dot(att2.astype(BF16), _block_diag(v_ref[_chunk_rows(c1 - 1), :], bd))
                return acc, csum

            carry = tile(cd, strict, (jnp.zeros((BLOCK, GROUP_W), F32), jnp.zeros((BLOCK, WIDE), F32)))
            odd = cd % 2
            carry = lax.fori_loop(0, odd, lambda n, cr: tile(cd - 1, None, cr), carry)
            acc, _ = lax.fori_loop(0, cd // 2, lambda n, cr: two_tiles(cd - 1 - odd - 2 * n, cr), carry)
            o_ref[_blk(i), :] = acc.astype(BF16)
            return 0

        lax.fori_loop(0, NB, qblock, 0)

    return pl.pallas_call(
        body, name=name, grid=(T // SEQ,), in_specs=_qkv_specs(0),
        out_specs=pl.BlockSpec((SEQ, GROUP_W), lambda b: (b, 0)), out_shape=jax.ShapeDtypeStruct((T, GROUP_W), BF16),
        compiler_params=_params("parallel"),
    )(qkv, qkv, qkv)


def sbw_bwd(qkv, dmixed, *, name):
    T = qkv.shape[0]

    def body(q_ref, k_ref, v_ref, do_ref, dq_ref, dk_ref, dv_ref, att_s, sg_s):
        bd, row, key = _wide_consts()
        r2 = lax.broadcasted_iota(jnp.int32, (CHUNK, CHUNK), 0)
        c2 = lax.broadcasted_iota(jnp.int32, (CHUNK, CHUNK), 1)
        later = (r2 > c2).astype(BF16)
        earlier = (r2 < c2).astype(BF16)
        dk_ref[...] = jnp.zeros_like(dk_ref)
        dv_ref[...] = jnp.zeros_like(dv_ref)

        def qblock(i, _):
            q = q_ref[_blk(i), :] * Q_SCALE
            do = do_ref[_blk(i), :].astype(BF16)
            cd = i // 2
            strict = key < row + BLOCK * (i % 2)

            def recompute(c, mask, csum):
                att, e, lk = _sbw_tile(q, _block_diag(k_ref[_chunk_rows(c), :], bd), mask, later, csum)
                sg = jnp.exp(e)
                att_s[c] = att
                sg_s[c] = sg if mask is None else jnp.where(mask, sg, 0.0)
                return csum + _widen(_head_rowsums(lk))

            def recompute_two(c1, csum):
                e1, lk1, t1 = _sbw_scores(q, _block_diag(k_ref[_chunk_rows(c1), :], bd), later)
                e2, lk2, t2 = _sbw_scores(q, _block_diag(k_ref[_chunk_rows(c1 - 1), :], bd), later)
                sg_s[c1] = jnp.exp(e1)
                sg_s[c1 - 1] = jnp.exp(e2)
                att_s[c1] = jnp.exp(e1 + t1 + csum)
                csum = csum + _widen(_head_rowsums(lk1))
                att_s[c1 - 1] = jnp.exp(e2 + t2 + csum)
                return csum + _widen(_head_rowsums(lk2))

            csum = recompute(cd, strict, jnp.zeros((BLOCK, WIDE), F32))
            odd = cd % 2
            csum = lax.fori_loop(0, odd, lambda n, cs: recompute(cd - 1, None, cs), csum)
            lax.fori_loop(0, cd // 2, lambda n, cs: recompute_two(cd - 1 - odd - 2 * n, cs), csum)

            def tile(c, carry):
                dq, pre = carry
                kbd = _block_diag(k_ref[_chunk_rows(c), :], bd)
                vbd = _block_diag(v_ref[_chunk_rows(c), :], bd)
                att = att_s[c]
                ds = _dot(do, vbd, _NT) * att
                dlk = ds + _tri_wide(ds, earlier) + pre
                dz = (ds - dlk * sg_s[c]).astype(BF16)
                dk_ref[_chunk_rows(c), :] += _fold_heads(_dot(dz, q, _TN), bd)
                dv_ref[_chunk_rows(c), :] += _fold_heads(_dot(att.astype(BF16), do, _TN), bd)
                return dq + _dot(dz, kbd), pre + _widen(_head_rowsums(ds))

            def two_tiles(c1, carry):
                dq, pre = carry
                c2 = c1 + 1
                kbd1, kbd2 = _block_diag(k_ref[_chunk_rows(c1), :], bd), _block_diag(k_ref[_chunk_rows(c2), :], bd)
                att1, att2 = att_s[c1], att_s[c2]
                ds1 = _dot(do, _block_diag(v_ref[_chunk_rows(c1), :], bd), _NT) * att1
                ds2 = _dot(do, _block_diag(v_ref[_chunk_rows(c2), :], bd), _NT) * att2
                tri1, tri2 = _tri_wide(ds1, earlier), _tri_wide(ds2, earlier)
                dv_ref[_chunk_rows(c1), :] += _fold_heads(_dot(att1.astype(BF16), do, _TN), bd)
                dv_ref[_chunk_rows(c2), :] += _fold_heads(_dot(att2.astype(BF16), do, _TN), bd)
                dz1 = (ds1 - (ds1 + tri1 + pre) * sg_s[c1]).astype(BF16)
                pre = pre + _widen(_head_rowsums(ds1))
                dz2 = (ds2 - (ds2 + tri2 + pre) * sg_s[c2]).astype(BF16)
                pre = pre + _widen(_head_rowsums(ds2))
                dk_ref[_chunk_rows(c1), :] += _fold_heads(_dot(dz1, q, _TN), bd)
                dk_ref[_chunk_rows(c2), :] += _fold_heads(_dot(dz2, q, _TN), bd)
                return dq + _dot(dz1, kbd1) + _dot(dz2, kbd2), pre

            n_tiles = cd + 1
            odd = n_tiles % 2
            carry = (jnp.zeros((BLOCK, GROUP_W), F32), jnp.zeros((BLOCK, WIDE), F32))
            carry = lax.fori_loop(0, odd, lambda n, cr: tile(0, cr), carry)
            dq, _ = lax.fori_loop(0, n_tiles // 2, lambda n, cr: two_tiles(odd + 2 * n, cr), carry)
            dq_ref[_blk(i), :] = dq * Q_SCALE
            return 0

        lax.fori_loop(0, NB, qblock, 0)

    out = pl.BlockSpec((SEQ, GROUP_W), lambda b: (b, 0))
    sd = jax.ShapeDtypeStruct((T, GROUP_W), F32)
    return pl.pallas_call(
        body, name=name, grid=(T // SEQ,), in_specs=_qkv_specs(0) + [out],
        out_specs=[out] * 3, out_shape=[sd] * 3,
        scratch_shapes=[pltpu.VMEM((NCH, BLOCK, WIDE), F32), pltpu.VMEM((NCH, BLOCK, WIDE), F32)],
        compiler_params=_params("parallel"),
    )(qkv, qkv, qkv, dmixed)


def _foxw_logits(q, kbd, cq, cr_ref, c, mask):
    ck = jnp.concatenate([cr_ref[h:h + 1, _chunk_rows(c)] for h in HEADS], axis=1)
    z = _dot(q, kbd, _NT) + cq - ck
    return z if mask is None else jnp.where(mask, z, NEG)


def foxw_fwd(qkv, cumc, cumr, *, name):
    T = qkv.shape[0]

    def body(q_ref, k_ref, v_ref, cc_ref, cr_ref, o_ref, o32_ref, lse_ref, z_s):
        bd, row, key = _wide_consts()
        lse_ref[...] = jnp.zeros_like(lse_ref)

        def qblock(i, _):
            q = q_ref[_blk(i), :] * Q_SCALE
            cq = _widen([cc_ref[_blk(i), h:h + 1] for h in HEADS])
            cd = i // 2
            causal = key <= row + BLOCK * (i % 2)

            def logits(c, mask, ms):
                z = _foxw_logits(q, _block_diag(k_ref[_chunk_rows(c), :], bd), cq, cr_ref, c, mask)
                z_s[c] = z
                return tuple(jnp.maximum(ms[h], jnp.max(z[:, _seg(h)], axis=1, keepdims=True)) for h in HEADS)

            ms = logits(cd, causal, (jnp.full((BLOCK, 1), NEG, F32),) * N_HEADS)
            ms = _loop_by_two(cd, lambda n: n, lambda c, m: logits(c, None, m), ms)
            m_wide = _widen(ms)

            def values(c, carry):
                acc, l = carry
                p = jnp.exp(z_s[c] - m_wide)
                return acc + _dot(p.astype(BF16), _block_diag(v_ref[_chunk_rows(c), :], bd)), l + _widen(_head_rowsums(p))

            acc, l = _loop_by_two(cd + 1, lambda n: n, values, (jnp.zeros((BLOCK, GROUP_W), F32), jnp.zeros((BLOCK, WIDE), F32)))
            ls = [l[:, h * CHUNK:h * CHUNK + 1] for h in HEADS]
            o = acc / _feature_widen(ls)
            o_ref[_blk(i), :] = o.astype(BF16)
            o32_ref[_blk(i), :] = o
            for h in HEADS:
                lse_ref[_blk(i), h:h + 1] = ms[h] + jnp.log(ls[h])
            return 0

        lax.fori_loop(0, NB, qblock, 0)

    out = pl.BlockSpec((SEQ, GROUP_W), lambda b: (b, 0))
    colb = pl.BlockSpec((SEQ, LANES), lambda b: (b, 0))
    sd = jax.ShapeDtypeStruct
    return pl.pallas_call(
        body, name=name, grid=(T // SEQ,),
        in_specs=_qkv_specs(3) + [colb, pl.BlockSpec((8, SEQ), lambda b: (b, 0))],
        out_specs=[out, out, colb], out_shape=[sd((T, GROUP_W), BF16), sd((T, GROUP_W), F32), sd((T, LANES), F32)],
        scratch_shapes=[pltpu.VMEM((NCH, BLOCK, WIDE), F32)],
        compiler_params=_params("parallel"),
    )(qkv, qkv, qkv, cumc, cumr)


def foxw_bwd(qkv, cumc, cumr, lse, o32, dmixed, *, name):
    T = qkv.shape[0]

    def body(q_ref, k_ref, v_ref, cc_ref, cr_ref, lse_ref, o_ref, do_ref, dq_ref, dk_ref, dv_ref, dcc_ref, dcr_ref):
        bd, row, key = _wide_consts()
        dk_ref[...] = jnp.zeros_like(dk_ref)
        dv_ref[...] = jnp.zeros_like(dv_ref)
        dcc_ref[...] = jnp.zeros_like(dcc_ref)
        dcr_ref[...] = jnp.zeros_like(dcr_ref)

        def qblock(i, _):
            q = q_ref[_blk(i), :] * Q_SCALE
            do32 = do_ref[_blk(i), :]
            do = do32.astype(BF16)
            prod = do32 * o_ref[_blk(i), :]
            delta = _widen([jnp.sum(prod[:, _hs(h)], axis=1, keepdims=True) for h in HEADS])
            cq = _widen([cc_ref[_blk(i), h:h + 1] for h in HEADS])
            lse_w = _widen([lse_ref[_blk(i), h:h + 1] for h in HEADS])
            cd = i // 2
            causal = key <= row + BLOCK * (i % 2)

            def tile(c, mask, carry):
                dq, dcq = carry
                kbd = _block_diag(k_ref[_chunk_rows(c), :], bd)
                vbd = _block_diag(v_ref[_chunk_rows(c), :], bd)
                p = jnp.exp(_foxw_logits(q, kbd, cq, cr_ref, c, mask) - lse_w)
                ds = p * (_dot(do, vbd, _NT) - delta)
                dsb = ds.astype(BF16)
                dk_ref[_chunk_rows(c), :] += _fold_heads(_dot(dsb, q, _TN), bd)
                dv_ref[_chunk_rows(c), :] += _fold_heads(_dot(p.astype(BF16), do, _TN), bd)
                for h in HEADS:
                    dcr_ref[h:h + 1, _chunk_rows(c)] -= jnp.sum(ds[:, _seg(h)], axis=0, keepdims=True)
                return dq + _dot(dsb, kbd), dcq + _widen(_head_rowsums(ds))

            carry = tile(cd, causal, (jnp.zeros((BLOCK, GROUP_W), F32), jnp.zeros((BLOCK, WIDE), F32)))
            dq, dcq = _loop_by_two(cd, lambda n: n, lambda c, cr: tile(c, None, cr), carry)
            dq_ref[_blk(i), :] = dq * Q_SCALE
            for h in HEADS:
                dcc_ref[_blk(i), h:h + 1] = dcq[:, h * CHUNK:h * CHUNK + 1]
            return 0

        lax.fori_loop(0, NB, qblock, 0)

    out = pl.BlockSpec((SEQ, GROUP_W), lambda b: (b, 0))
    colb = pl.BlockSpec((SEQ, LANES), lambda b: (b, 0))
    rowb = pl.BlockSpec((8, SEQ), lambda b: (b, 0))
    sd = jax.ShapeDtypeStruct
    big = sd((T, GROUP_W), F32)
    return pl.pallas_call(
        body, name=name, grid=(T // SEQ,),
        in_specs=_qkv_specs(3) + [colb, rowb, colb, out, pl.BlockSpec((SEQ, GROUP_W), lambda b: (b, 1))],
        out_specs=[out, out, out, colb, rowb],
        out_shape=[big, big, big, sd((T, LANES), F32), sd((T // SEQ * 8, SEQ), F32)],
        compiler_params=_params("parallel"),
    )(qkv, qkv, qkv, cumc, cumr, lse, o32, dmixed)


BAND = 2 * BLOCK


def _t5_bucket_np(dist):
    n = np.maximum(dist, 0)
    max_exact = NUM_BUCKETS // 2
    nf = np.maximum(n, 1).astype(np.float32)
    large = max_exact + (np.log(nf / np.float32(max_exact)) / np.float32(math.log(MAX_DISTANCE / max_exact))
                         * np.float32(NUM_BUCKETS - max_exact)).astype(np.int32)
    large = np.minimum(large, NUM_BUCKETS - 1)
    return np.where(n < max_exact, n, large).astype(np.int32)


def _band_buckets():
    qi = np.arange(BLOCK)[:, None]
    ki = np.arange(BAND)[None, :]
    delta = np.clip(qi - ki + BLOCK, 0, BLOCK)
    return np.stack([_t5_bucket_np(delta * d) for d in DILATIONS])


def to_classes(a, d):
    if d == 1:
        return a
    T, C = a.shape
    return a.reshape(T // SEQ, SEQ // d, d, C).transpose(0, 2, 1, 3).reshape(T, C)


def from_classes(a, d):
    if d == 1:
        return a
    T, C = a.shape
    return a.reshape(T // SEQ, d, SEQ // d, C).transpose(0, 2, 1, 3).reshape(T, C)


def relbias_expand(rel, *, name):
    buckets = jnp.asarray(_band_buckets())
    n_pat = len(DILATIONS)

    def body(rel_ref, bk_ref, o_ref):
        for p in range(n_pat):
            bk = bk_ref[p]
            for h in range(N_HEADS):
                acc = jnp.zeros((BLOCK, BAND), F32)
                for b in range(NUM_BUCKETS):
                    acc = jnp.where(bk == b, rel_ref[b, h], acc)
                o_ref[p * N_HEADS + h] = acc

    return pl.pallas_call(
        body, name=name,
        in_specs=[pl.BlockSpec(memory_space=pltpu.SMEM), pl.BlockSpec(memory_space=pltpu.VMEM)],
        out_specs=pl.BlockSpec(memory_space=pltpu.VMEM),
        out_shape=jax.ShapeDtypeStruct((n_pat * N_HEADS, BLOCK, BAND), F32),
        compiler_params=_params(),
    )(rel, buckets)


def relbias_reduce(ds_all, *, name):
    buckets = jnp.asarray(_band_buckets())
    n_pat = len(DILATIONS)

    def body(ds_ref, bk_ref, o_ref):
        for b in range(NUM_BUCKETS):
            for h in range(N_HEADS):
                tot = jnp.float32(0.0)
                for p in range(n_pat):
                    tot = tot + jnp.sum(jnp.where(bk_ref[p] == b, ds_ref[p * N_HEADS + h], 0.0))
                o_ref[b, h] = tot

    return pl.pallas_call(
        body, name=name,
        in_specs=[pl.BlockSpec(memory_space=pltpu.VMEM), pl.BlockSpec(memory_space=pltpu.VMEM)],
        out_specs=pl.BlockSpec(memory_space=pltpu.SMEM),
        out_shape=jax.ShapeDtypeStruct((NUM_BUCKETS, N_HEADS), F32),
        compiler_params=_params(),
    )(ds_all, buckets)


def _band_valid_wide(first, row, key):
    inside = jnp.logical_and(key >= row, key <= row + BLOCK)
    return jnp.logical_and(inside, jnp.logical_or(jnp.logical_not(first), key >= BLOCK))


QKV_BLOCKS = 9


def _band_in_specs(d, pattern, has_prev):
    rows = BLOCK * d
    cur = lambda c: pl.BlockSpec((rows, GROUP_W), lambda tb, r: (tb, c))
    prev = lambda c: pl.BlockSpec((rows, GROUP_W), lambda tb, r: (jnp.maximum(tb - 1, 0), c))
    bias = pl.BlockSpec((N_HEADS, BLOCK, BAND), lambda tb, r: (pattern, 0, 0))
    return [cur(6), cur(7), cur(8)] + ([prev(7), prev(8)] if has_prev else []) + [bias]


def _class_rows(d):
    return pl.ds(pl.program_id(1), BLOCK, stride=d) if d > 1 else pl.ds(0, BLOCK)


def _halves_scratch(rows, n):
    return [pltpu.VMEM((2, rows, LANES), F32)] * n


def _stage(refs, scratch):
    @pl.when(pl.program_id(1) == 0)
    def _():
        for src, dst in zip(refs, scratch):
            dst[0] = src[:, :LANES].astype(F32)
            dst[1] = src[:, LANES:].astype(F32)


def _take_class(s, d):
    rows = _class_rows(d)
    return jnp.concatenate([s.at[0][rows, :], s.at[1][rows, :]], axis=1)


def _put_class(s, d, x):
    rows = _class_rows(d)
    s.at[0][rows, :] = x[:, :LANES]
    s.at[1][rows, :] = x[:, LANES:]


def _flush(scratch, refs, d):
    @pl.when(pl.program_id(1) == d - 1)
    def _():
        for s, o in zip(scratch, refs):
            o[...] = jnp.concatenate([s[0], s[1]], axis=1)


def _band_operands(scratch, d, has_prev):
    take = lambda s: _take_class(s, d).astype(BF16)
    q = (_take_class(scratch[0], d) * Q_SCALE).astype(BF16)
    if has_prev:
        k = jnp.concatenate([take(scratch[3]), take(scratch[1])], axis=0)
        v = jnp.concatenate([take(scratch[4]), take(scratch[2])], axis=0)
    else:
        k = jnp.concatenate([jnp.zeros((BLOCK, GROUP_W), BF16), take(scratch[1])], axis=0)
        v = jnp.concatenate([jnp.zeros((BLOCK, GROUP_W), BF16), take(scratch[2])], axis=0)
    return q, k, v


def _lane_columns(cols):
    lane = lax.broadcasted_iota(jnp.int32, (BLOCK, LANES), 1)
    out = jnp.zeros((BLOCK, LANES), F32)
    for h, c in enumerate(cols):
        out = jnp.where(lane == h, c, out)
    return out


def band_fwd(qkv, bias, pattern, *, name):
    T = qkv.shape[0]
    d = DILATIONS[pattern]
    rows_per_block = BLOCK * d
    seq_blocks = SEQ // rows_per_block
    has_prev = seq_blocks > 1
    n_in = 5 if has_prev else 3

    def body(*refs):
        ins, b_ref, o_ref, lse_ref = refs[:n_in], refs[n_in], refs[n_in + 1], refs[n_in + 2]
        staged, o_s = refs[n_in + 3:2 * n_in + 3], refs[2 * n_in + 3]
        bd, row, key = _wide_consts()
        valid = _band_valid_wide(pl.program_id(0) % seq_blocks == 0, row, key)
        _stage(ins, staged)
        q, k, v = _band_operands(staged, d, has_prev)
        kbd, vbd = _block_diag(k, bd), _block_diag(v, bd)
        bias_w = jnp.concatenate([b_ref[h] for h in HEADS], axis=1)
        sc = jnp.where(valid, _dot(q, kbd, _NT) + bias_w, NEG)
        ms = [jnp.max(sc[:, _seg(h)], axis=1, keepdims=True) for h in HEADS]
        p = jnp.exp(sc - _widen(ms))
        ls = _head_rowsums(p)
        _put_class(o_s, d, _dot(p.astype(BF16), vbd) / _feature_widen(ls))
        lse_ref[_class_rows(d), :] = _lane_columns([ms[h] + jnp.log(ls[h]) for h in HEADS])
        _flush([o_s], [o_ref], d)

    sd = jax.ShapeDtypeStruct
    return pl.pallas_call(
        body, name=name, grid=(T // rows_per_block, d), in_specs=_band_in_specs(d, pattern, has_prev),
        out_specs=[pl.BlockSpec((rows_per_block, GROUP_W), lambda tb, r: (tb, 0)),
                   pl.BlockSpec((rows_per_block, LANES), lambda tb, r: (tb, 0))],
        out_shape=[sd((T, GROUP_W), F32), sd((T, LANES), F32)],
        scratch_shapes=_halves_scratch(rows_per_block, n_in + 1),
        compiler_params=_params("parallel", "arbitrary"),
    )(*([qkv] * n_in), bias)


def band_bwd(qkv, bias, lse, do, dlse, pattern, *, name):
    T = qkv.shape[0]
    d = DILATIONS[pattern]
    rows_per_block = BLOCK * d
    seq_blocks = SEQ // rows_per_block
    has_prev = seq_blocks > 1
    n_in = 5 if has_prev else 3
    n_out = 5 if has_prev else 3

    def body(*refs):
        ins, b_ref, lse_ref, do_ref, dlse_ref = refs[:n_in], refs[n_in], refs[n_in + 1], refs[n_in + 2], refs[n_in + 3]
        outs = refs[n_in + 4:n_in + 4 + n_out]
        ds_ref = refs[n_in + 4 + n_out]
        scratch = refs[n_in + 5 + n_out:]
        staged, do_s, out_s = scratch[:n_in], scratch[n_in], scratch[n_in + 1:]
        first_step = jnp.logical_and(pl.program_id(0) == 0, pl.program_id(1) == 0)
        bd, row, key = _wide_consts()
        valid = _band_valid_wide(pl.program_id(0) % seq_blocks == 0, row, key)
        _stage(list(ins) + [do_ref], list(staged) + [do_s])
        q, k, v = _band_operands(staged, d, has_prev)
        kbd, vbd = _block_diag(k, bd), _block_diag(v, bd)
        rows = _class_rows(d)
        do = _take_class(do_s, d).astype(BF16)
        lse_t, dlse_t = lse_ref[rows, :], dlse_ref[rows, :]
        bias_w = jnp.concatenate([b_ref[h] for h in HEADS], axis=1)
        lse_w = _widen([lse_t[:, h:h + 1] for h in HEADS])
        dlse_w = _widen([dlse_t[:, h:h + 1] for h in HEADS])
        p = jnp.where(valid, jnp.exp(_dot(q, kbd, _NT) + bias_w - lse_w), 0.0)
        dp = _dot(do, vbd, _NT)
        ds = p * (dp - _widen(_head_rowsums(p * dp)) + dlse_w)
        dsb, pb = ds.astype(BF16), p.astype(BF16)
        _put_class(out_s[0], d, _dot(dsb, kbd) * Q_SCALE)
        dk = _fold_heads(_dot(dsb, q, _TN), bd)
        dv = _fold_heads(_dot(pb, do, _TN), bd)
        _put_class(out_s[1], d, dk[BLOCK:])
        _put_class(out_s[2], d, dv[BLOCK:])
        if has_prev:
            _put_class(out_s[3], d, dk[:BLOCK])
            _put_class(out_s[4], d, dv[:BLOCK])
        _flush(out_s, outs, d)

        @pl.when(first_step)
        def _():
            for h in HEADS:
                ds_ref[h] = ds[:, _seg(h)]

        @pl.when(jnp.logical_not(first_step))
        def _():
            for h in HEADS:
                ds_ref[h] += ds[:, _seg(h)]

    big = pl.BlockSpec((rows_per_block, GROUP_W), lambda tb, r: (tb, 0))
    colb = pl.BlockSpec((rows_per_block, LANES), lambda tb, r: (tb, 0))
    sd = jax.ShapeDtypeStruct
    return pl.pallas_call(
        body, name=name, grid=(T // rows_per_block, d), in_specs=_band_in_specs(d, pattern, has_prev) + [colb, big, colb],
        out_specs=[big] * n_out + [pl.BlockSpec((N_HEADS, BLOCK, BAND), lambda tb, r: (0, 0, 0))],
        out_shape=[sd((T, GROUP_W), F32)] * n_out + [sd((N_HEADS, BLOCK, BAND), F32)],
        scratch_shapes=_halves_scratch(rows_per_block, n_in + 1 + n_out),
        compiler_params=_params("arbitrary", "arbitrary"),
    )(*([qkv] * n_in), bias, lse, do, dlse)


def shift_add(cur, prev, d, *, name):
    rows = BLOCK * d
    nb = cur.shape[0] // rows

    def body(c_ref, p_ref, o_ref):
        keep = (pl.program_id(0) < nb - 1).astype(F32)
        o_ref[...] = c_ref[...] + keep * p_ref[...]

    blk = pl.BlockSpec((rows, GROUP_W), lambda tb: (tb, 0))
    nxt = pl.BlockSpec((rows, GROUP_W), lambda tb: (jnp.minimum(tb + 1, nb - 1), 0))
    return pl.pallas_call(
        body, name=name, grid=(nb,), in_specs=[blk, nxt], out_specs=blk,
        out_shape=jax.ShapeDtypeStruct(cur.shape, F32), compiler_params=_params("parallel"),
    )(cur, prev)


def _pattern_weights(lse_refs, h):
    ls = [r[:, h:h + 1] for r in lse_refs]
    mx = functools.reduce(jnp.maximum, ls)
    es = [jnp.exp(l - mx) for l in ls]
    tot = functools.reduce(lambda a, b: a + b, es)
    return [e / tot for e in es]


def dil_combine_fwd(outs, *, name):
    T = outs[0][0].shape[0]
    n = len(outs)
    tm = 512

    def body(*refs):
        o_refs, l_refs, out_ref = refs[:n], refs[n:2 * n], refs[2 * n]
        for h in range(N_HEADS):
            w = _pattern_weights(l_refs, h)
            acc = w[0] * o_refs[0][:, _hs(h)]
            for p in range(1, n):
                acc = acc + w[p] * o_refs[p][:, _hs(h)]
            out_ref[:, _hs(h)] = acc.astype(BF16)

    big = pl.BlockSpec((tm, GROUP_W), lambda i: (i, 0))
    colb = pl.BlockSpec((tm, LANES), lambda i: (i, 0))
    return pl.pallas_call(
        body, name=name, grid=(T // tm,), in_specs=[big] * n + [colb] * n,
        out_specs=big, out_shape=jax.ShapeDtypeStruct((T, GROUP_W), BF16),
        compiler_params=_params("parallel"),
    )(*[o for o, _ in outs], *[l for _, l in outs])


def dil_combine_bwd(outs, dmixed, *, name):
    T = outs[0][0].shape[0]
    n = len(outs)
    tm = 512

    def body(*refs):
        o_refs, l_refs, do_ref = refs[:n], refs[n:2 * n], refs[2 * n]
        do_refs, dl_refs = refs[2 * n + 1:3 * n + 1], refs[3 * n + 1:]
        for r in dl_refs:
            r[...] = jnp.zeros_like(r)
        for h in range(N_HEADS):
            w = _pattern_weights(l_refs, h)
            do = do_ref[:, _hs(h)]
            dw = [jnp.sum(do * o_refs[p][:, _hs(h)], axis=1, keepdims=True) for p in range(n)]
            mean = functools.reduce(lambda a, b: a + b, [w[p] * dw[p] for p in range(n)])
            for p in range(n):
                do_refs[p][:, _hs(h)] = w[p] * do
                dl_refs[p][:, h:h + 1] = w[p] * (dw[p] - mean)

    big = pl.BlockSpec((tm, GROUP_W), lambda i: (i, 0))
    colb = pl.BlockSpec((tm, LANES), lambda i: (i, 0))
    sd = jax.ShapeDtypeStruct
    res = pl.pallas_call(
        body, name=name, grid=(T // tm,),
        in_specs=[big] * n + [colb] * n + [pl.BlockSpec((tm, GROUP_W), lambda i: (i, 2))],
        out_specs=[big] * n + [colb] * n, out_shape=[sd((T, GROUP_W), F32)] * n + [sd((T, LANES), F32)] * n,
        compiler_params=_params("parallel"),
    )(*[o for o, _ in outs], *[l for _, l in outs], dmixed)
    return list(zip(res[:n], res[n:]))


def dilated_fwd(qkv, bias, tag):
    return [band_fwd(qkv, bias, p, name=f"{tag}_band_fwd{p}") for p in range(len(DILATIONS))]


def dilated_bwd(qkv, bias, outs, dmixed, tag):
    grads = dil_combine_bwd(outs, dmixed, name=f"{tag}_combine_bwd")
    parts, ds_all = [], []
    for p, d in enumerate(DILATIONS):
        (_, lse), (do, dlse) = outs[p], grads[p]
        res = band_bwd(qkv, bias, lse, do, dlse, p, name=f"{tag}_band_bwd{p}")
        dq, dk, dv, ds = res[0], res[1], res[2], res[-1]
        if len(res) > 4:
            dk = shift_add(dk, res[3], d, name=f"{tag}_dk{p}")
            dv = shift_add(dv, res[4], d, name=f"{tag}_dv{p}")
        parts.append([dq, dk, dv])
        ds_all.append(ds)
    return parts, jnp.concatenate(ds_all, axis=0)


def assemble_dqkv(d_sb, d_fox, d_dil, *, name):
    T = d_sb[0].shape[0]
    tr = 512
    n_pat = len(d_dil)
    flat = list(d_sb) + list(d_fox) + [a for part in d_dil for a in part]

    def body(*refs):
        o_ref = refs[-1]
        for j in range(6):
            o_ref[:, j * GROUP_W:(j + 1) * GROUP_W] = refs[j][...].astype(BF16)
        for j in range(3):
            acc = refs[6 + j][...]
            for p in range(1, n_pat):
                acc = acc + refs[6 + 3 * p + j][...]
            o_ref[:, (6 + j) * GROUP_W:(7 + j) * GROUP_W] = acc.astype(BF16)

    blk = pl.BlockSpec((tr, GROUP_W), lambda i: (i, 0))
    return pl.pallas_call(
        body, name=name, grid=(T // tr,), in_specs=[blk] * len(flat),
        out_specs=pl.BlockSpec((tr, QKV_BLOCKS * GROUP_W), lambda i: (i, 0)),
        out_shape=jax.ShapeDtypeStruct((T, QKV_BLOCKS * GROUP_W), BF16), compiler_params=_params("parallel"),
    )(*flat)


def sum_cast(arrs, dtype, *, name):
    R, C = arrs[0].shape
    tr = _largest_tile(R, 512, 16)
    n = len(arrs)

    def body(*refs):
        acc = refs[0][...].astype(F32)
        for r in refs[1:n]:
            acc = acc + r[...].astype(F32)
        refs[n][...] = acc.astype(dtype)

    blk = pl.BlockSpec((tr, C), lambda i: (i, 0))
    return pl.pallas_call(
        body, name=name, grid=(R // tr,), in_specs=[blk] * n, out_specs=blk, out_shape=jax.ShapeDtypeStruct((R, C), dtype),
        compiler_params=_params("parallel"),
    )(*arrs)


GRAD_WIRE = BF16


def _block_diag_halves(w):
    z = jnp.zeros((HEAD_DIM, HEAD_DIM), w.dtype)
    half = lambda a, b: jnp.concatenate([jnp.concatenate([a, z], axis=1), jnp.concatenate([z, b], axis=1)], axis=0)
    return jnp.stack([half(w[0], w[1]), half(w[2], w[3])]).astype(BF16)


def _diag_blocks(d):
    h = HEAD_DIM
    return jnp.stack([d[0, :h, :h], d[0, h:, h:], d[1, :h, :h], d[1, h:, h:]])


def layer_fwd(x, mem2d, W, P, bias, tag):
    s = {}
    s["x"] = x
    h1 = rmsnorm_fwd(x, P["norm_mix_g"], name=f"{tag}_norm_mix")
    qkv = matmul(h1, W["qkv"], out_dtype=BF16, name=f"{tag}_qkv")
    aux = matmul(h1, W["aux"], name=f"{tag}_aux")
    o_sb = sbw_fwd(qkv, name=f"{tag}_sb_fwd")
    cumc = fox_prep(aux, P["bf"], name=f"{tag}_fox_prep")
    cumr = col_to_row(cumc)
    o_fox, o_fox32, lse_fox = foxw_fwd(qkv, cumc, cumr, name=f"{tag}_fox_fwd")
    dil = dilated_fwd(qkv, bias, tag)
    o_dil = dil_combine_fwd(dil, name=f"{tag}_dil_combine")
    o_lru, h_lru = lru_fwd(aux, P["lru_conv_w"], P["lru_conv_b"], P["wa"], P["lru_b_a"], P["wx"], P["lru_b_x"],
                           P["lru_lambda"], name=f"{tag}_lru_fwd")
    mixed = jnp.concatenate([o_sb, o_fox, o_dil, o_lru], axis=1)
    x1 = matmul(mixed, W["out"], residual=x, name=f"{tag}_out")
    hq = rmsnorm_fwd(x1, P["norm_cross_g"], name=f"{tag}_norm_cross")
    qc = matmul(hq, W["cq"], out_dtype=BF16, name=f"{tag}_cq")
    memn = rmsnorm_fwd(mem2d, P["norm_mem_g"], name=f"{tag}_norm_mem")
    kv = matmul(memn, W["ckv"], out_dtype=BF16, name=f"{tag}_ckv")
    oc = cross_fwd(qc, kv, name=f"{tag}_cross_fwd")
    x2 = matmul(oc, W["coT"], trans_b=True, residual=x1, name=f"{tag}_co")
    h2 = rmsnorm_fwd(x2, P["norm_ffn_g"], name=f"{tag}_norm_ffn")
    if "ffn" in W:
        W.update(W.pop("ffn")(x2))
    hu = matmul(h2, W["up_u"], trans_b=True, name=f"{tag}_up_u")
    hg = matmul(h2, W["up_g"], trans_b=True, name=f"{tag}_up_g")
    act = glu_fwd(hu, hg, P["wu"], P["wg"], P["bu"], P["bg"], name=f"{tag}_glu_fwd")
    x3 = matmul(act, W["down"], residual=x2, name=f"{tag}_down")
    s.update(h1=h1, qkv=qkv, aux=aux, cumc=cumc, cumr=cumr, lse_fox=lse_fox, o_fox32=o_fox32, dil=dil, h_lru=h_lru, mixed=mixed,
             x1=x1, hq=hq, qc=qc, memn=memn, kv=kv, oc=oc, x2=x2, h2=h2, hu=hu, hg=hg, act=act)
    return x3, s


def layer_bwd(dx3, mem2d, W, P, bias, s, tag, ffn_grads_done=None):
    mm = functools.partial(matmul, out_dtype=GRAD_WIRE, trans_a=True)
    gW, gP = {}, {}
    dact = matmul(dx3, W["down"], trans_b=True, name=f"{tag}_d_act")
    gW["down"] = mm(s["act"], dx3, name=f"{tag}_g_down")
    dhu, dhg, dwu, dwg, dbu, dbg = glu_bwd(s["hu"], s["hg"], dact, P["wu"], P["wg"], P["bu"], P["bg"], name=f"{tag}_glu_bwd")
    gP["ffn_conv_w"] = jnp.concatenate([dwu, dwg], axis=1)
    gP["ffn_conv_b"] = jnp.concatenate([dbu, dbg], axis=1)
    dh2 = matmul(dhu, W["up_u"], name=f"{tag}_d_h2u")
    dh2 = matmul(dhg, W["up_g"], residual=dh2, name=f"{tag}_d_h2g")
    gW["up_u"] = mm(dhu, s["h2"], name=f"{tag}_g_up_u")
    gW["up_g"] = mm(dhg, s["h2"], name=f"{tag}_g_up_g")
    dx2, gP["norm_ffn_g"] = rmsnorm_bwd(s["x2"], P["norm_ffn_g"], dh2, dx3, name=f"{tag}_norm_ffn_bwd")
    if ffn_grads_done is not None:
        dx2 = ffn_grads_done(gW, dx2)
    doc = matmul(dx2, W["coT"], name=f"{tag}_d_oc")
    gW["coT"] = mm(dx2, s["oc"], name=f"{tag}_g_co")
    dqc, dkv = cross_bwd(s["qc"], s["kv"], doc, name=f"{tag}_cross_bwd")
    dhq = matmul(dqc, W["cq"], trans_b=True, name=f"{tag}_d_hq")
    gW["cq"] = mm(s["hq"], dqc, name=f"{tag}_g_cq")
    dmemn = matmul(dkv, W["ckv"], trans_b=True, name=f"{tag}_d_memn")
    gW["ckv"] = mm(s["memn"], dkv, name=f"{tag}_g_ckv")
    _, gP["norm_mem_g"] = rmsnorm_bwd(mem2d, P["norm_mem_g"], dmemn, None, name=f"{tag}_norm_mem_bwd")
    dx1, gP["norm_cross_g"] = rmsnorm_bwd(s["x1"], P["norm_cross_g"], dhq, dx2, name=f"{tag}_norm_cross_bwd")
    dmixed = matmul(dx1, W["out"], trans_b=True, name=f"{tag}_d_mixed")
    gW["out"] = mm(s["mixed"], dx1, name=f"{tag}_g_out")
    qkv, aux = s["qkv"], s["aux"]
    d_sb = sbw_bwd(qkv, dmixed, name=f"{tag}_sb_bwd")
    dfq, dfk, dfv, dcc, dcr = foxw_bwd(qkv, s["cumc"], s["cumr"], s["lse_fox"], s["o_fox32"], dmixed, name=f"{tag}_fox_bwd")
    dcum = sum_cast([dcc, row_to_col(dcr)], F32, name=f"{tag}_dcum")
    df, dbf = fox_prep_bwd(aux, P["bf"], dcum, name=f"{tag}_fox_prep_bwd")
    gP["b_forget"] = dbf[0, :N_HEADS]
    d_dil, ds_band = dilated_bwd(qkv, bias, s["dil"], dmixed, tag)
    dlx, dlg, dcw, dcb, dwa, dba, dwx, dbx, dlam = lru_bwd(
        aux, s["h_lru"], dmixed, P["lru_conv_w"], P["lru_conv_b"], P["wa"], P["lru_b_a"], P["wx"], P["lru_b_x"],
        P["lru_lambda"], name=f"{tag}_lru_bwd")
    gP.update(lru_conv_w=dcw, lru_conv_b=dcb, lru_w_a=_diag_blocks(dwa), lru_b_a=dba, lru_w_x=_diag_blocks(dwx),
              lru_b_x=dbx, lru_lambda=dlam)
    dqkv = assemble_dqkv(d_sb, [dfq, dfk, dfv], d_dil, name=f"{tag}_dqkv")
    daux = jnp.concatenate([dlx, dlg, df], axis=1)
    dh1 = matmul(dqkv, W["qkv"], trans_b=True, name=f"{tag}_d_h1a")
    dh1 = matmul(daux, W["aux"], trans_b=True, residual=dh1, name=f"{tag}_d_h1b")
    gW["qkv"] = mm(s["h1"], dqkv, name=f"{tag}_g_qkv")
    gW["aux"] = mm(s["h1"], daux, name=f"{tag}_g_aux")
    dx, gP["norm_mix_g"] = rmsnorm_bwd(s["x"], P["norm_mix_g"], dh1, dx1, name=f"{tag}_norm_mix_bwd")
    return dx, gW, gP, ds_band


def local_step(x, mem, target, weights_of, Ps, rel_bias, final_norm_g, grads_done=None, ffn_grads_done=None):
    B = x.shape[0]
    x2d = x.reshape(B * SEQ, D_MODEL)
    mem2d = mem.reshape(B * N_MEM, D_MODEL)
    bias = relbias_expand(rel_bias, name="relbias_expand")
    saved, Ws = [], []
    h = x2d
    for l in range(DEPTH):
        Ws.append(weights_of(l, h))
        h, s = layer_fwd(h, mem2d, Ws[l], Ps[l], bias, f"l{l}")
        saved.append(s)
    loss, dh, d_final = loss_head(h, final_norm_g, target.reshape(B * SEQ, D_MODEL), name="loss_head")
    gWs, gPs, ds_bands = [None] * DEPTH, [None] * DEPTH, []
    for l in range(DEPTH - 1, -1, -1):
        hook = None if ffn_grads_done is None else functools.partial(ffn_grads_done, l)
        dh, gWs[l], gPs[l], ds = layer_bwd(dh, mem2d, Ws[l], Ps[l], bias, saved[l], f"l{l}", hook)
        if grads_done is not None:
            dh = grads_done(l, gWs[l], dh)
        ds_bands.append(ds)
    d_rel = relbias_reduce(sum_cast([d.reshape(-1, BAND) for d in ds_bands], F32, name="ds_band_sum").reshape(-1, BLOCK, BAND),
                           name="relbias_reduce")
    return loss, dh.reshape(B, SEQ, D_MODEL), gWs, gPs, d_rel, d_final


def small_params(p, l):
    row = lambda name: p[name][l].reshape(1, -1)
    ffn_w, ffn_b = p["ffn_conv_w"][l], row("ffn_conv_b")
    return dict(
        norm_mix_g=row("norm_mix_g"), norm_cross_g=row("norm_cross_g"), norm_mem_g=row("norm_mem_g"), norm_ffn_g=row("norm_ffn_g"),
        bf=jnp.pad(row("b_forget"), ((0, 0), (0, LANES - N_HEADS))),
        lru_conv_w=p["lru_conv_w"][l], lru_conv_b=row("lru_conv_b"), wa=_block_diag_halves(p["lru_w_a"][l]), lru_b_a=row("lru_b_a"),
        wx=_block_diag_halves(p["lru_w_x"][l]), lru_b_x=row("lru_b_x"), lru_lambda=row("lru_lambda"),
        wu=ffn_w[:, :D_FF], wg=ffn_w[:, D_FF:], bu=ffn_b[:, :D_FF], bg=ffn_b[:, D_FF:])


def canonical_weights(w_in, w_out, w_cq, w_ck, w_cv, w_co, w_up, w_down):
    sb_fox, fox_f, rest = w_in[:, :6 * GROUP_W], w_in[:, 6 * GROUP_W:6 * GROUP_W + N_HEADS], w_in[:, 6 * GROUP_W + N_HEADS:]
    dil, lru = rest[:, :3 * GROUP_W], rest[:, 3 * GROUP_W:]
    pad = jnp.zeros((w_in.shape[0], AUX_W - 2 * GROUP_W - N_HEADS), w_in.dtype)
    return dict(qkv=jnp.concatenate([sb_fox, dil], axis=1), aux=jnp.concatenate([lru, fox_f, pad], axis=1), out=w_out,
                cq=w_cq, ckv=jnp.concatenate([w_ck, w_cv], axis=1), coT=w_co.T, upT=w_up.T, down=w_down)


def native_grads(g):
    qkv, aux = g["qkv"], g["aux"]
    a, b = 6 * GROUP_W, 6 * GROUP_W + N_HEADS
    w_in = jnp.zeros((qkv.shape[0], b + 5 * GROUP_W), qkv.dtype)
    w_in = w_in.at[:, :a].set(qkv[:, :a]).at[:, a:b].set(aux[:, 2 * GROUP_W:2 * GROUP_W + N_HEADS])
    w_in = w_in.at[:, b:b + 3 * GROUP_W].set(qkv[:, a:]).at[:, b + 3 * GROUP_W:].set(aux[:, :2 * GROUP_W])
    return (w_in, g["out"], g["cq"], g["ckv"][:, :GROUP_W], g["ckv"][:, GROUP_W:], g["coT"].T, g["upT"].T, g["down"])


ANY = pl.BlockSpec(memory_space=pl.ANY)
VMEM_SPEC = pl.BlockSpec(memory_space=pltpu.VMEM)


def _place():
    x, y, c = lax.axis_index("x"), lax.axis_index("y"), lax.axis_index("c")
    other_chips = [(1 - x, y), (x, 1 - y), (1 - x, 1 - y)]
    return x, y, c, other_chips


def _gather_body(x_ref, out_ref, send_sems, recv_sems, local_sem):
    x, y, c, chips = _place()
    me, sibling = (x, y, c), (x, y, 1 - c)

    def slot(px, py, pc):
        return out_ref.at[4 * px + 2 * py + pc]

    def copy(k, block, to, src=None):
        return pltpu.make_async_remote_copy(
            src_ref=slot(*block) if src is None else src, dst_ref=slot(*block),
            send_sem=send_sems.at[k], recv_sem=recv_sems.at[k], device_id=to, device_id_type=MESH)

    if local_sem is not None:
        mine = pltpu.make_async_copy(x_ref, slot(*me), local_sem)
        mine.start()
    first = [copy(0, me, sibling, src=x_ref)]
    first += [copy(1 + j, me, (*chip, c), src=x_ref) for j, chip in enumerate(chips)]
    for cp in first:
        cp.start()
    passed = [copy(4 + j, (*chip, c), sibling) for j, chip in enumerate(chips)]
    for j, chip in enumerate(chips):
        copy(1 + j, (*chip, c), me).wait_recv()
        passed[j].start()
    copy(0, sibling, me).wait_recv()
    for j, chip in enumerate(chips):
        copy(4 + j, (*chip, 1 - c), me).wait_recv()
    for cp in first + passed:
        cp.wait_send()
    if local_sem is not None:
        mine.wait()


_GATHER_SEMS = [pltpu.SemaphoreType.DMA((7,)), pltpu.SemaphoreType.DMA((7,)), pltpu.SemaphoreType.DMA]


def allgather_hbm(shard, me, *, name):
    def body(x_ref, out_ref, done_ref, send_sems, recv_sems):
        _gather_body(x_ref, out_ref, send_sems, recv_sems, None)
        done_ref[...] = jnp.zeros_like(done_ref)

    others, done = pl.pallas_call(
        body, name=name, in_specs=[ANY], out_specs=[ANY, VMEM_SPEC],
        out_shape=[jax.ShapeDtypeStruct((N_DEV,) + shard.shape, shard.dtype), jax.ShapeDtypeStruct((8, LANES), F32)],
        scratch_shapes=_GATHER_SEMS[:2],
    )(shard)
    return lax.dynamic_update_slice(others, shard[None], (me, 0, 0)), done


def allgather_small(x, *, name, reduce=False):
    def body(x_ref, out_ref, second_ref, *sems):
        _gather_body(x_ref, out_ref, *sems)
        if reduce:
            acc = out_ref[0]
            for d in range(1, N_DEV):
                acc = acc + out_ref[d]
            second_ref[...] = acc
        else:
            second_ref[...] = jnp.zeros_like(second_ref)

    sd = jax.ShapeDtypeStruct
    return pl.pallas_call(
        body, name=name, in_specs=[VMEM_SPEC], out_specs=[VMEM_SPEC, VMEM_SPEC],
        out_shape=[sd((N_DEV,) + x.shape, x.dtype), sd(x.shape if reduce else (8, LANES), x.dtype)],
        scratch_shapes=_GATHER_SEMS, compiler_params=pltpu.CompilerParams(vmem_limit_bytes=VMEM_LIMIT_V7X),
    )(x)


N_CHIPS = 4


def pair_exchange(g, *, name):
    _, R, C = g.shape

    def body(g_ref, recv_ref, send_sems, recv_sems):
        x, y, c, _ = _place()
        sibling = (x, y, 1 - c)
        remote = [pltpu.make_async_remote_copy(
            src_ref=g_ref.at[2 * q + (1 - c)], dst_ref=recv_ref.at[q], send_sem=send_sems.at[q], recv_sem=recv_sems.at[q],
            device_id=sibling, device_id_type=MESH) for q in range(N_CHIPS)]
        for cp in remote:
            cp.start()
        for cp in remote:
            cp.wait_recv()
        for cp in remote:
            cp.wait_send()

    return pl.pallas_call(
        body, name=name, in_specs=[ANY], out_specs=ANY, out_shape=jax.ShapeDtypeStruct((N_CHIPS, R, C), g.dtype),
        scratch_shapes=[pltpu.SemaphoreType.DMA((N_CHIPS,))] * 2,
    )(g)


def chip_exchange(s, *, name):
    _, R, C = s.shape

    def body(s_ref, o0, o1, o2, send_sems, recv_sems):
        x, y, c, chips = _place()
        outs = (o0, o1, o2)
        copies = [pltpu.make_async_remote_copy(
            src_ref=s_ref.at[2 * cx + cy], dst_ref=outs[j], send_sem=send_sems.at[j], recv_sem=recv_sems.at[j],
            device_id=(cx, cy, c), device_id_type=MESH) for j, (cx, cy) in enumerate(chips)]
        for cp in copies:
            cp.start()
        for cp in copies:
            cp.wait_recv()
        for cp in copies:
            cp.wait_send()

    sd = jax.ShapeDtypeStruct((R, C), s.dtype)
    return pl.pallas_call(
        body, name=name, in_specs=[ANY], out_specs=[ANY] * 3, out_shape=[sd] * 3,
        scratch_shapes=[pltpu.SemaphoreType.DMA((3,)), pltpu.SemaphoreType.DMA((3,))],
    )(s)


HBM_SPEC = pl.BlockSpec(memory_space=pltpu.HBM)
SEM_SPEC = pl.BlockSpec(memory_space=pltpu.SEMAPHORE)
N_PEERS = N_DEV - 1


def _peers():
    x, y, c = lax.axis_index("x"), lax.axis_index("y"), lax.axis_index("c")
    flip = lambda v, bit: 1 - v if bit else v
    out = []
    for k in range(1, N_DEV):
        px, py, pc = flip(x, (k >> 2) & 1), flip(y, (k >> 1) & 1), flip(c, k & 1)
        out.append(((px, py, pc), 4 * px + 2 * py + pc))
    return out, 4 * x + 2 * y + c


def _peer_copies(src_ref, land_ref, send_sems, recv_sems, scatter, landing):
    peers, me = _peers()
    return [pltpu.make_async_remote_copy(
        src_ref=src_ref.at[idx] if scatter else src_ref, dst_ref=land_ref.at[me if landing == "mine" else idx],
        send_sem=send_sems.at[k], recv_sem=recv_sems.at[k], device_id=peer, device_id_type=MESH)
        for k, (peer, idx) in enumerate(peers)]


def exchange_start(src, scatter, *, name):
    shape = (N_DEV,) + src.shape[-2:]

    def body(src_ref, land_ref, send_sems, recv_sems, src_thru, land_thru, token):
        for cp in _peer_copies(src_ref, land_ref, send_sems, recv_sems, scatter, "mine"):
            cp.start()
        token[...] = jnp.zeros_like(token)

    sems = pltpu.SemaphoreType.DMA((N_PEERS,))
    return pl.pallas_call(
        body, name=name,
        out_shape=(sems, sems, pltpu.HBM(src.shape, src.dtype), pltpu.HBM(shape, src.dtype), jax.ShapeDtypeStruct((8, LANES), F32)),
        in_specs=(HBM_SPEC, HBM_SPEC), out_specs=(SEM_SPEC, SEM_SPEC, HBM_SPEC, HBM_SPEC, VMEM_SPEC),
        input_output_aliases={0: 2, 1: 3},
        compiler_params=pltpu.CompilerParams(has_side_effects=pltpu.SideEffectType.DATAFLOW_SIDE_EFFECTING),
    )(pltpu.with_memory_space_constraint(src, pltpu.HBM), pltpu.with_memory_space_constraint(lax.empty(shape, src.dtype), pltpu.HBM))


def exchange_wait(started, after, scatter, *, name):
    send_sems, recv_sems, src_thru, land_thru, _ = started

    def body(src_ref, land_ref, send_sems, recv_sems, after_ref, src_dead, got_ref):
        for cp in _peer_copies(src_ref, land_ref, send_sems, recv_sems, scatter, "theirs"):
            cp.wait_send()
            cp.wait_recv()

    return pl.pallas_call(
        body, name=name, out_shape=(pltpu.HBM(src_thru.shape, src_thru.dtype), pltpu.HBM(land_thru.shape, land_thru.dtype)),
        in_specs=(HBM_SPEC, HBM_SPEC, SEM_SPEC, SEM_SPEC, ANY), out_specs=(HBM_SPEC, HBM_SPEC), input_output_aliases={0: 0, 1: 1},
        compiler_params=pltpu.CompilerParams(has_side_effects=pltpu.SideEffectType.DATAFLOW_SIDE_EFFECTING),
    )(src_thru, land_thru, send_sems, recv_sems, after)[1]


def sum_blocks(blocks, *, name):
    n, R, C = blocks.shape
    tr = _largest_tile(R, 512, 16)

    def body(b_ref, o_ref):
        d = pl.program_id(1)
        v = b_ref[...].astype(F32)

        @pl.when(d == 0)
        def _():
            o_ref[...] = v

        @pl.when(d > 0)
        def _():
            o_ref[...] += v

    return pl.pallas_call(
        body, name=name, grid=(R // tr, n),
        in_specs=[pl.BlockSpec((None, tr, C), lambda i, d: (d, i, 0))], out_specs=pl.BlockSpec((tr, C), lambda i, d: (i, 0)),
        out_shape=jax.ShapeDtypeStruct((R, C), F32), compiler_params=_params("parallel", "arbitrary"),
    )(blocks)


WEIGHTS = ("norm_mix_g", "w_in", "b_forget", "lru_conv_w", "lru_conv_b", "lru_w_a", "lru_b_a", "lru_w_x", "lru_b_x", "lru_lambda",
           "w_out", "norm_cross_g", "norm_mem_g", "w_cq", "w_ck", "w_cv", "w_co", "norm_ffn_g", "w_up", "ffn_conv_w", "ffn_conv_b",
           "w_down", "rel_bias", "final_norm_g")
LARGE = ("w_in", "w_out", "w_cq", "w_ck", "w_cv", "w_co", "w_up", "w_down")
COLUMN_SPLIT_SMALL = ("lru_conv_w", "ffn_conv_w")
PACK = (("qkv", 128, 2304), ("aux", 128, 640), ("out", 128, 1024), ("cq", 128, 256), ("ckv", 128, 512), ("coT", 128, 256),
        ("upT", 704, 1024), ("down", 352, 1024))
PACK_W = 1024


def _pack_rows(parts):
    return jnp.concatenate([p.reshape(-1, PACK_W) for p in parts], axis=0)


def _pad_rows(flat, mult=8 * LANES):
    n = flat.shape[0]
    return jnp.pad(flat, (0, (-n) % mult)).reshape(-1, LANES)


def kernel(x, mem, norm_mix_g, w_in, b_forget, lru_conv_w, lru_conv_b, lru_w_a, lru_b_a, lru_w_x, lru_b_x, lru_lambda, w_out, norm_cross_g, norm_mem_g, w_cq, w_ck, w_cv, w_co, norm_ffn_g, w_up, ffn_conv_w, ffn_conv_b, w_down, rel_bias, final_norm_g, loss_target, m_norm_mix_g, m_w_in, m_b_forget, m_lru_conv_w, m_lru_conv_b, m_lru_w_a, m_lru_b_a, m_lru_w_x, m_lru_b_x, m_lru_lambda, m_w_out, m_norm_cross_g, m_norm_mem_g, m_w_cq, m_w_ck, m_w_cv, m_w_co, m_norm_ffn_g, m_w_up, m_ffn_conv_w, m_ffn_conv_b, m_w_down, m_rel_bias, m_final_norm_g, v_norm_mix_g, v_w_in, v_b_forget, v_lru_conv_w, v_lru_conv_b, v_lru_w_a, v_lru_b_a, v_lru_w_x, v_lru_b_x, v_lru_lambda, v_w_out, v_norm_cross_g, v_norm_mem_g, v_w_cq, v_w_ck, v_w_cv, v_w_co, v_norm_ffn_g, v_w_up, v_ffn_conv_w, v_ffn_conv_b, v_w_down, v_rel_bias, v_final_norm_g):
    w = dict(norm_mix_g=norm_mix_g, w_in=w_in, b_forget=b_forget, lru_conv_w=lru_conv_w, lru_conv_b=lru_conv_b, lru_w_a=lru_w_a,
             lru_b_a=lru_b_a, lru_w_x=lru_w_x, lru_b_x=lru_b_x, lru_lambda=lru_lambda, w_out=w_out, norm_cross_g=norm_cross_g,
             norm_mem_g=norm_mem_g, w_cq=w_cq, w_ck=w_ck, w_cv=w_cv, w_co=w_co, norm_ffn_g=norm_ffn_g, w_up=w_up,
             ffn_conv_w=ffn_conv_w, ffn_conv_b=ffn_conv_b, w_down=w_down, rel_bias=rel_bias, final_norm_g=final_norm_g)
    m = dict(norm_mix_g=m_norm_mix_g, w_in=m_w_in, b_forget=m_b_forget, lru_conv_w=m_lru_conv_w, lru_conv_b=m_lru_conv_b,
             lru_w_a=m_lru_w_a, lru_b_a=m_lru_b_a, lru_w_x=m_lru_w_x, lru_b_x=m_lru_b_x, lru_lambda=m_lru_lambda, w_out=m_w_out,
             norm_cross_g=m_norm_cross_g, norm_mem_g=m_norm_mem_g, w_cq=m_w_cq, w_ck=m_w_ck, w_cv=m_w_cv, w_co=m_w_co,
             norm_ffn_g=m_norm_ffn_g, w_up=m_w_up, ffn_conv_w=m_ffn_conv_w, ffn_conv_b=m_ffn_conv_b, w_down=m_w_down,
             rel_bias=m_rel_bias, final_norm_g=m_final_norm_g)
    v = dict(norm_mix_g=v_norm_mix_g, w_in=v_w_in, b_forget=v_b_forget, lru_conv_w=v_lru_conv_w, lru_conv_b=v_lru_conv_b,
             lru_w_a=v_lru_w_a, lru_b_a=v_lru_b_a, lru_w_x=v_lru_w_x, lru_b_x=v_lru_b_x, lru_lambda=v_lru_lambda, w_out=v_w_out,
             norm_cross_g=v_norm_cross_g, norm_mem_g=v_norm_mem_g, w_cq=v_w_cq, w_ck=v_w_ck, w_cv=v_w_cv, w_co=v_w_co,
             norm_ffn_g=v_norm_ffn_g, w_up=v_w_up, ffn_conv_w=v_ffn_conv_w, ffn_conv_b=v_ffn_conv_b, w_down=v_w_down,
             rel_bias=v_rel_bias, final_norm_g=v_final_norm_g)
    me = 4 * lax.axis_index("x") + 2 * lax.axis_index("y") + lax.axis_index("c")

    conv_shard = jnp.concatenate([w[n].reshape(-1) for n in COLUMN_SPLIT_SMALL])
    conv_all, conv_gathered = allgather_small(_pad_rows(conv_shard), name="gather_conv")
    conv_all = conv_all.reshape(N_DEV, -1)
    full = dict(w)
    off = 0
    for n in COLUMN_SPLIT_SMALL:
        d, k, c = w[n].shape
        blocks = conv_all[:, off:off + d * k * c].reshape(N_DEV, d, k, c)
        full[n] = blocks.transpose(1, 2, 0, 3).reshape(d, k, N_DEV * c)
        off += d * k * c

    MIX, FFN = PACK[:6], PACK[6:]

    def packed_shard(l, group):
        canon = canonical_weights(*[w[n][l] for n in LARGE])
        return _pack_rows([canon[k].astype(BF16) for k, _, _ in group])

    def unpack_weights(packed, group):
        W, row = {}, 0
        for k, r, c in group:
            n_rows = r * c // PACK_W
            W[k] = packed[:, row:row + n_rows].reshape(N_DEV * r, c)
            row += n_rows
        if "upT" in W:
            upT = W.pop("upT")
            W["up_u"], W["up_g"] = upT[:D_FF], upT[D_FF:]
        return W

    def packed_grads(gW, group):
        g = dict(gW)
        if "up_u" in g:
            g["upT"] = jnp.concatenate([g.pop("up_u"), g.pop("up_g")], axis=0)
        return jnp.concatenate([g[k].reshape(N_DEV, r * c // PACK_W, PACK_W) for k, r, c in group], axis=1)

    def unpack_grads(shard_sum, group):
        g, row = {}, 0
        for k, r, c in group:
            n_rows = r * c // PACK_W
            g[k] = shard_sum[row:row + n_rows].reshape(r, c)
            row += n_rows
        return g

    def own_block_in(landed, block):
        return lax.dynamic_update_slice(landed, block[None], (me, 0, 0))

    def gathered_weights(copies, shard, after, group, name):
        return unpack_weights(own_block_in(exchange_wait(copies, after, False, name=name), shard), group)

    def scattered_sum(src, copies, after, tag):
        landed = exchange_wait(copies, after, True, name=f"{tag}_wait")
        mine = lax.dynamic_index_in_dim(src, me, axis=0, keepdims=False)
        return sum_blocks(own_block_in(landed, mine), name=f"{tag}_sum")

    last = DEPTH - 1
    mix0, gathered = allgather_hbm(packed_shard(0, MIX) + conv_gathered[0, 0].astype(BF16), me, name="gather_weights")
    ffn0_shard = packed_shard(0, FFN) + gathered[0, 0].astype(BF16)
    gather_ffn0 = exchange_start(ffn0_shard, False, name="gather_ffn0_start")
    last_shard = packed_shard(last, PACK) + gather_ffn0[4][0, 0].astype(BF16)
    gather_last = exchange_start(last_shard, False, name="gather_last_start")
    started = gather_last[4][0, 0]

    def weights_of(l, h):
        if l == 0:
            W = unpack_weights(mix0, MIX)
            W["ffn"] = lambda after: gathered_weights(gather_ffn0, ffn0_shard, after, FFN, "gather_ffn0_wait")
            return W
        assert l == last
        return gathered_weights(gather_last, last_shard, h, PACK, "gather_last_wait")

    in_flight = {}

    def scatter(key, g_all, dx, name):
        in_flight[key] = (g_all, exchange_start(g_all, True, name=name))
        return dx + in_flight[key][1][4][0, 0]

    def grads_done(l, gW, dh):
        return scatter("last", packed_grads(gW, PACK), dh, "grads_last_start") if l == last else dh

    def ffn_grads_done(l, gW, dx):
        if l != 0:
            return dx
        return scatter("ffn0", packed_grads({k: gW[k] for k in ("up_u", "up_g", "down")}, FFN), dx, "grads_ffn0_start")

    Ps = [small_params(full, l) for l in range(DEPTH)]
    Ps[0]["norm_mix_g"] = Ps[0]["norm_mix_g"] + started
    loss, grad_x, gWs, gPs, d_rel, d_final = local_step(x, mem, loss_target, weights_of, Ps, rel_bias,
                                                        final_norm_g.reshape(1, -1), grads_done, ffn_grads_done)

    shard_grads = {last: unpack_grads(scattered_sum(*in_flight["last"], grad_x, "grads_last"), PACK)}
    shard_grads[0] = unpack_grads(scattered_sum(*in_flight["ffn0"], grad_x, "grads_ffn0"), FFN)

    g_all = packed_grads({k: gWs[0][k] for k, _, _ in MIX}, MIX)
    rows = g_all.shape[1]
    got = pair_exchange(g_all, name="grads_pair_exchange")
    own = lax.dynamic_index_in_dim(g_all.reshape(N_CHIPS, 2, rows, PACK_W), lax.axis_index("c"), axis=1, keepdims=False)
    pair = sum_cast([own.reshape(-1, PACK_W), got.reshape(-1, PACK_W)], GRAD_WIRE, name="grads_pair_sum").reshape(N_CHIPS, rows, PACK_W)
    from_x, from_y, from_xy = chip_exchange(pair, name="grads_chip_exchange")
    mine = lax.dynamic_index_in_dim(pair, 2 * lax.axis_index("x") + lax.axis_index("y"), axis=0, keepdims=False)
    shard_grads[0].update(unpack_grads(sum_cast([mine, from_x, from_y, from_xy], F32, name="grads_chip_sum"), MIX))

    grads = {}
    per_layer = [native_grads(shard_grads[l]) for l in range(DEPTH)]
    for i, n in enumerate(LARGE):
        grads[n] = jnp.stack([per_layer[l][i] for l in range(DEPTH)])

    small_names = [n for n in WEIGHTS if n not in LARGE and n not in ("rel_bias", "final_norm_g")]
    pieces = [gPs[l][n].reshape(-1) for n in small_names for l in range(DEPTH)] + [d_rel.reshape(-1), d_final.reshape(-1), loss[0, :1]]
    sizes = [p.shape[0] for p in pieces]
    _, total = allgather_small(_pad_rows(jnp.concatenate(pieces)), name="allreduce_small", reduce=True)
    total = total.reshape(-1)
    off, it = 0, iter(sizes)
    for n in small_names:
        per = []
        for l in range(DEPTH):
            sz = next(it)
            per.append(total[off:off + sz])
            off += sz
        full_shape = (DEPTH,) + full[n].shape[1:]
        gfull = jnp.stack(per).reshape(full_shape)
        if n in COLUMN_SPLIT_SMALL:
            c = w[n].shape[-1]
            gfull = lax.dynamic_slice_in_dim(gfull, me * c, c, axis=gfull.ndim - 1)
        grads[n] = gfull
    grads["rel_bias"] = total[off:off + rel_bias.size].reshape(rel_bias.shape)
    off += rel_bias.size
    grads["final_norm_g"] = total[off:off + D_MODEL]
    off += D_MODEL
    loss_out = total[off]

    delta, new_m, new_v = {}, {}, {}
    for n in LARGE:
        shape = w[n].shape
        two_d = lambda a: a.reshape(-1, shape[-1])
        d_, m_, v_ = adamw(two_d(w[n]), two_d(grads[n]), two_d(m[n]), two_d(v[n]), name=f"adamw_{n}")
        delta[n], new_m[n], new_v[n] = d_.reshape(shape), m_.reshape(shape), v_.reshape(shape)
    small_all = [n for n in WEIGHTS if n not in LARGE]
    two_d = lambda a: a.reshape(-1, a.shape[-1])
    d_, m_, v_ = adamw_many(*[[two_d(src[n]) for n in small_all] for src in (w, grads, m, v)], name="adamw_small")
    for i, n in enumerate(small_all):
        delta[n], new_m[n], new_v[n] = (a[i].reshape(w[n].shape) for a in (d_, m_, v_))

    return (loss_out, grad_x, *[grads[n] for n in WEIGHTS], *[delta[n] for n in WEIGHTS], *[new_m[n] for n in WEIGHTS],
            *[new_v[n] for n in WEIGHTS])
```

```python
import functools
import math

import numpy as np
import jax
import jax.numpy as jnp
from jax import lax
from jax.experimental import pallas as pl
from jax.experimental.pallas import tpu as pltpu

F32 = jnp.float32
BF16 = jnp.bfloat16
MESH = pl.DeviceIdType.MESH

N_DEV = 8
D_MODEL = 1024
SEQ = 2048
DEPTH = 2
HEAD_DIM = 64
N_HEADS = 4
GROUP_W = N_HEADS * HEAD_DIM
D_FF = 2816
N_MEM = 256
NUM_BUCKETS = 32
MAX_DISTANCE = 2048
BLOCK = 128
DILATIONS = (1, 4, 16)
EPS = 1e-6
LRU_C = 8.0
Q_SCALE = HEAD_DIM ** -0.5
AUX_W = 640
LRU_HALF_W = 128
LRU_HALVES = GROUP_W // LRU_HALF_W
ADAM_LR, ADAM_B1, ADAM_B2, ADAM_EPS, ADAM_WD, ADAM_STEP = 0.001, 0.9, 0.999, 1e-08, 0.01, 10

VMEM_LIMIT_V7X = 48 * 1024 * 1024


def _params(*sem):
    return pltpu.CompilerParams(dimension_semantics=sem if sem else None, vmem_limit_bytes=VMEM_LIMIT_V7X)


def _pick(n, cands):
    for c in cands:
        if n % c == 0:
            return c
    return n


def _largest_tile(n, cap, align):
    best = None
    for t in range(align, min(n, cap) + 1, align):
        if n % t == 0:
            best = t
    return n if best is None else best


def matmul(a, b, *, name, trans_a=False, trans_b=False, out_dtype=F32, residual=None):
    (K, M) = a.shape if trans_a else a.shape[::-1]
    (N, Kb) = b.shape if trans_b else b.shape[::-1]
    assert K == Kb, (a.shape, b.shape)
    tm = _largest_tile(M, 1408 if trans_a else 512, 128)
    tn = _largest_tile(N, 1408, 128)
    tk = _largest_tile(K, 1024 if trans_a else 2816, 128)
    nk = K // tk
    a_spec = pl.BlockSpec((tk, tm), lambda i, j, k: (k, i)) if trans_a else pl.BlockSpec((tm, tk), lambda i, j, k: (i, k))
    b_spec = pl.BlockSpec((tn, tk), lambda i, j, k: (j, k)) if trans_b else pl.BlockSpec((tk, tn), lambda i, j, k: (k, j))
    o_spec = pl.BlockSpec((tm, tn), lambda i, j, k: (i, j))
    dims = (((0 if trans_a else 1,), (1 if trans_b else 0,)), ((), ()))
    has_res = residual is not None

    def body(*refs):
        a_ref, b_ref = refs[0], refs[1]
        r_ref = refs[2] if has_res else None
        part = lax.dot_general(a_ref[...].astype(BF16), b_ref[...].astype(BF16), dims, preferred_element_type=F32)
        if nk == 1:
            if has_res:
                part = part + r_ref[...].astype(F32)
            refs[-1][...] = part.astype(out_dtype)
            return
        o_ref, acc_ref = refs[-2], refs[-1]
        k = pl.program_id(2)

        @pl.when(k == 0)
        def _():
            acc_ref[...] = part

        @pl.when(k > 0)
        def _():
            acc_ref[...] += part

        @pl.when(k == nk - 1)
        def _():
            r = acc_ref[...]
            if has_res:
                r = r + r_ref[...].astype(F32)
            o_ref[...] = r.astype(out_dtype)

    ops = (a, b) + ((residual,) if has_res else ())
    return pl.pallas_call(
        body, name=name, grid=(M // tm, N // tn, nk),
        in_specs=[a_spec, b_spec] + ([o_spec] if has_res else []),
        out_specs=o_spec, out_shape=jax.ShapeDtypeStruct((M, N), out_dtype),
        scratch_shapes=[pltpu.VMEM((tm, tn), F32)] if nk > 1 else [],
        compiler_params=_params("parallel", "parallel", "arbitrary"),
    )(*ops)


def rmsnorm_fwd(x, g, *, name):
    R, D = x.shape
    tr = _pick(R, (512, 256))

    def body(x_ref, g_ref, o_ref):
        xv = x_ref[...]
        r = lax.rsqrt(jnp.mean(xv * xv, axis=-1, keepdims=True) + EPS)
        o_ref[...] = (xv * r * g_ref[...]).astype(BF16)

    return pl.pallas_call(
        body, name=name, grid=(R // tr,),
        in_specs=[pl.BlockSpec((tr, D), lambda i: (i, 0)), pl.BlockSpec((1, D), lambda i: (0, 0))],
        out_specs=pl.BlockSpec((tr, D), lambda i: (i, 0)), out_shape=jax.ShapeDtypeStruct((R, D), BF16),
        compiler_params=_params("parallel"),
    )(x, g)


def rmsnorm_bwd(x, g, dh, dres, *, name):
    R, D = x.shape
    tr = _pick(R, (512, 256))
    has_res = dres is not None

    def body(*refs):
        x_ref, g_ref, dh_ref = refs[:3]
        dx_ref, dg_ref = refs[-2], refs[-1]
        xv = x_ref[...]
        r = lax.rsqrt(jnp.mean(xv * xv, axis=-1, keepdims=True) + EPS)
        n = xv * r
        dhv = dh_ref[...]
        dn = dhv * g_ref[...]
        dx = r * (dn - n * jnp.mean(dn * n, axis=-1, keepdims=True))
        if has_res:
            dx = dx + refs[3][...]
        dx_ref[...] = dx
        part = jnp.sum(dhv * n, axis=0, keepdims=True)

        @pl.when(pl.program_id(0) == 0)
        def _():
            dg_ref[...] = part

        @pl.when(pl.program_id(0) > 0)
        def _():
            dg_ref[...] += part

    row = pl.BlockSpec((tr, D), lambda i: (i, 0))
    vec = pl.BlockSpec((1, D), lambda i: (0, 0))
    ops = (x, g, dh) + ((dres,) if has_res else ())
    return pl.pallas_call(
        body, name=name, grid=(R // tr,),
        in_specs=[row, vec, row] + ([row] if has_res else []),
        out_specs=[row, vec],
        out_shape=[jax.ShapeDtypeStruct((R, D), F32), jax.ShapeDtypeStruct((1, D), F32)],
        compiler_params=_params("arbitrary"),
    )(*ops)


_SQRT_HALF = 0.7071067811865476
_INV_SQRT_2PI = 0.3989422804014327


def _erf(x):
    ax = jnp.abs(x)
    t = 1.0 / (1.0 + 0.3275911 * ax)
    poly = t * (0.254829592 + t * (-0.284496736 + t * (1.421413741 + t * (-1.453152027 + t * 1.061405429))))
    y = 1.0 - poly * jnp.exp(-ax * ax)
    return jnp.where(x < 0, -y, y)


def _gelu_cdf(x):
    return 0.5 * (1.0 + _erf(x * _SQRT_HALF))


def _gelu_and_grad(x):
    cdf = _gelu_cdf(x)
    return x * cdf, cdf + x * _INV_SQRT_2PI * jnp.exp(-0.5 * x * x)


def _shift_down(main, halo, first, shifts):
    halo = jnp.where(first, 0.0, halo)
    ext = jnp.concatenate([halo, main], axis=0)
    return [pltpu.roll(ext, s, 0)[8:] for s in shifts]


def _conv3(main, halo, first, w, b):
    m1, m2 = _shift_down(main, halo, first, (1, 2))
    return ((b + w[0:1] * m2) + w[1:2] * m1) + w[2:3] * main, m1, m2


def glu_fwd(hu, hg, wu, wg, bu, bg, *, name):
    T, F = hu.shape
    tm, tf = 512, _largest_tile(F, 704, 128)
    hb = tm // 8
    blocks_per_example = SEQ // tm

    def body(hu_ref, hg_ref, hau_ref, hag_ref, wu_ref, wg_ref, bu_ref, bg_ref, o_ref):
        first = pl.program_id(0) % blocks_per_example == 0
        up, _, _ = _conv3(hu_ref[...], hau_ref[...], first, wu_ref[...], bu_ref[...])
        gate, _, _ = _conv3(hg_ref[...], hag_ref[...], first, wg_ref[...], bg_ref[...])
        o_ref[...] = (gate * _gelu_cdf(gate) * up).astype(BF16)

    main = pl.BlockSpec((tm, tf), lambda i, j: (i, j))
    halo = pl.BlockSpec((8, tf), lambda i, j: (jnp.maximum(i * hb - 1, 0), j))
    w3 = pl.BlockSpec((3, tf), lambda i, j: (0, j))
    b1 = pl.BlockSpec((1, tf), lambda i, j: (0, j))
    return pl.pallas_call(
        body, name=name, grid=(T // tm, F // tf),
        in_specs=[main, main, halo, halo, w3, w3, b1, b1],
        out_specs=main, out_shape=jax.ShapeDtypeStruct((T, F), BF16),
        compiler_params=_params("parallel", "parallel"),
    )(hu, hg, hu, hg, wu, wg, bu, bg)


def glu_bwd(hu, hg, dact, wu, wg, bu, bg, *, name):
    T, F = hu.shape
    tm, tf = 512, _largest_tile(F, 704, 128)
    hb = tm // 8
    blocks_per_example = SEQ // tm
    n_halo_blocks = T // 8
    n_ext = tm + 8

    def body(hu_ref, hg_ref, hau_ref, hag_ref, hnu_ref, hng_ref, da_ref, dan_ref, wu_ref, wg_ref, bu_ref, bg_ref,
             du_ref, dg_ref, dwu_ref, dwg_ref, dbu_ref, dbg_ref):
        i = pl.program_id(1)
        first = i % blocks_per_example == 0
        last = i % blocks_per_example == blocks_per_example - 1
        wu, wg = wu_ref[...], wg_ref[...]

        def conv_ext(main_ref, prev_ref, next_ref, w, b):
            ext = jnp.concatenate([jnp.where(first, 0.0, prev_ref[...]), main_ref[...], next_ref[...]], axis=0)
            x0, x1, x2 = ext[8:], pltpu.roll(ext, 1, 0)[8:], pltpu.roll(ext, 2, 0)[8:]
            return ((b + w[0:1] * x2) + w[1:2] * x1) + w[2:3] * x0, x0, x1, x2

        up, xu, u1, u2 = conv_ext(hu_ref, hau_ref, hnu_ref, wu, bu_ref[...])
        gate, xg, g1, g2 = conv_ext(hg_ref, hag_ref, hng_ref, wg, bg_ref[...])
        act, dact_dgate = _gelu_and_grad(gate)
        da = jnp.concatenate([da_ref[...], jnp.where(last, 0.0, dan_ref[...])], axis=0)
        dup = da * act
        dgate = da * up * dact_dgate

        def conv_t(d, w):
            return (w[2:3] * d[:tm] + w[1:2] * pltpu.roll(d, n_ext - 1, 0)[:tm] + w[0:1] * pltpu.roll(d, n_ext - 2, 0)[:tm]).astype(BF16)

        du_ref[...] = conv_t(dup, wu)
        dg_ref[...] = conv_t(dgate, wg)

        def sums(d, x0, x1, x2):
            s = lambda v: jnp.sum(v[:tm], axis=0, keepdims=True)
            return jnp.concatenate([s(d * x2), s(d * x1), s(d * x0)], axis=0), s(d)

        pwu, pbu = sums(dup, xu, u1, u2)
        pwg, pbg = sums(dgate, xg, g1, g2)

        @pl.when(i == 0)
        def _():
            dwu_ref[...] = pwu
            dwg_ref[...] = pwg
            dbu_ref[...] = pbu
            dbg_ref[...] = pbg

        @pl.when(i > 0)
        def _():
            dwu_ref[...] += pwu
            dwg_ref[...] += pwg
            dbu_ref[...] += pbu
            dbg_ref[...] += pbg

    main = pl.BlockSpec((tm, tf), lambda j, i: (i, j))
    before = pl.BlockSpec((8, tf), lambda j, i: (jnp.maximum(i * hb - 1, 0), j))
    after = pl.BlockSpec((8, tf), lambda j, i: (jnp.minimum((i + 1) * hb, n_halo_blocks - 1), j))
    w3 = pl.BlockSpec((3, tf), lambda j, i: (0, j))
    b1 = pl.BlockSpec((1, tf), lambda j, i: (0, j))
    sd = jax.ShapeDtypeStruct
    return pl.pallas_call(
        body, name=name, grid=(F // tf, T // tm),
        in_specs=[main, main, before, before, after, after, main, after, w3, w3, b1, b1],
        out_specs=[main, main, w3, w3, b1, b1],
        out_shape=[sd((T, F), BF16), sd((T, F), BF16), sd((3, F), F32), sd((3, F), F32), sd((1, F), F32), sd((1, F), F32)],
        compiler_params=_params("parallel", "arbitrary"),
    )(hu, hg, hu, hg, hu, hg, dact, dact, wu, wg, bu, bg)


def loss_head(x, g, target, *, name):
    T, D = x.shape
    tr = 256

    def body(x_ref, g_ref, t_ref, loss_ref, dx_ref, dg_ref):
        xv = x_ref[...]
        gv = g_ref[...]
        r = lax.rsqrt(jnp.mean(xv * xv, axis=-1, keepdims=True) + EPS)
        n = xv * r
        err = n * gv - t_ref[...]
        part_loss = jnp.zeros((1, 128), F32) + 0.5 * jnp.sum(jnp.mean(err * err, axis=-1, keepdims=True))
        dy = err * (1.0 / D)
        dn = dy * gv
        dx_ref[...] = r * (dn - n * jnp.mean(dn * n, axis=-1, keepdims=True))
        part_g = jnp.sum(dy * n, axis=0, keepdims=True)

        @pl.when(pl.program_id(0) == 0)
        def _():
            loss_ref[...] = part_loss
            dg_ref[...] = part_g

        @pl.when(pl.program_id(0) > 0)
        def _():
            loss_ref[...] += part_loss
            dg_ref[...] += part_g

    row = pl.BlockSpec((tr, D), lambda i: (i, 0))
    vec = pl.BlockSpec((1, D), lambda i: (0, 0))
    sd = jax.ShapeDtypeStruct
    return pl.pallas_call(
        body, name=name, grid=(T // tr,),
        in_specs=[row, vec, row],
        out_specs=[pl.BlockSpec((1, 128), lambda i: (0, 0)), row, vec],
        out_shape=[sd((1, 128), F32), sd((T, D), F32), sd((1, D), F32)],
        compiler_params=_params("arbitrary"),
    )(x, g, target)


def adamw(w, g, m, v, *, name):
    R, C = w.shape
    tr = _pick(R, (256, 128, 64, 32, 16, 8))

    def body(w_ref, g_ref, m_ref, v_ref, d_ref, nm_ref, nv_ref):
        gv = g_ref[...]
        mn = ADAM_B1 * m_ref[...] + (1.0 - ADAM_B1) * gv
        vn = ADAM_B2 * v_ref[...] + (1.0 - ADAM_B2) * (gv * gv)
        m_hat = mn / (1.0 - ADAM_B1 ** ADAM_STEP)
        v_hat = vn / (1.0 - ADAM_B2 ** ADAM_STEP)
        d_ref[...] = -ADAM_LR * (m_hat / (jnp.sqrt(v_hat) + ADAM_EPS) + ADAM_WD * w_ref[...])
        nm_ref[...] = mn
        nv_ref[...] = vn

    blk = pl.BlockSpec((tr, C), lambda i: (i, 0))
    sd = jax.ShapeDtypeStruct((R, C), F32)
    return pl.pallas_call(
        body, name=name, grid=(R // tr,), in_specs=[blk] * 4, out_specs=[blk] * 3, out_shape=[sd] * 3,
        compiler_params=_params("parallel"),
    )(w, g, m, v)


def adamw_many(ws, gs, ms, vs, *, name):
    n = len(ws)

    def body(*refs):
        ins, outs = refs[:4 * n], refs[4 * n:]
        for i in range(n):
            w_ref, g_ref, m_ref, v_ref = ins[i], ins[n + i], ins[2 * n + i], ins[3 * n + i]
            gv = g_ref[...]
            mn = ADAM_B1 * m_ref[...] + (1.0 - ADAM_B1) * gv
            vn = ADAM_B2 * v_ref[...] + (1.0 - ADAM_B2) * (gv * gv)
            m_hat = mn / (1.0 - ADAM_B1 ** ADAM_STEP)
            v_hat = vn / (1.0 - ADAM_B2 ** ADAM_STEP)
            outs[i][...] = -ADAM_LR * (m_hat / (jnp.sqrt(v_hat) + ADAM_EPS) + ADAM_WD * w_ref[...])
            outs[n + i][...] = mn
            outs[2 * n + i][...] = vn

    vm = pl.BlockSpec(memory_space=pltpu.VMEM)
    shapes = [jax.ShapeDtypeStruct(w.shape, F32) for w in ws]
    res = pl.pallas_call(
        body, name=name, in_specs=[vm] * (4 * n), out_specs=[vm] * (3 * n), out_shape=shapes * 3, compiler_params=_params(),
    )(*ws, *gs, *ms, *vs)
    return res[:n], res[n:2 * n], res[2 * n:]


def _softplus(x):
    return jnp.maximum(x, 0.0) + jnp.log(1.0 + jnp.exp(-jnp.abs(x)))


def _lru_gates(x, cw, cb, wa, ba, wx, bx, lam):
    S = x.shape[0]
    row = lax.broadcasted_iota(jnp.int32, (S, 1), 0)

    def back(s):
        return jnp.where(row >= s, pltpu.roll(x, s, 0), 0.0)

    xc = (((cb + cw[0:1] * back(3)) + cw[1:2] * back(2)) + cw[2:3] * back(1)) + cw[3:4] * x
    xb = xc.astype(BF16)
    r = jax.nn.sigmoid(jnp.dot(xb, wa, preferred_element_type=F32) + ba)
    ig = jax.nn.sigmoid(jnp.dot(xb, wx, preferred_element_type=F32) + bx)
    sp = _softplus(-lam)
    la = -LRU_C * r * sp
    a = jnp.exp(la)
    y = 2.0 * la
    one_minus_a2 = jnp.where(y > -0.05, -y * (1.0 + y * (0.5 + y * (1.0 / 6.0 + y * (1.0 / 24.0)))), 1.0 - jnp.exp(y))
    mm = jnp.sqrt(one_minus_a2)
    return xc, xb, r, ig, sp, a, mm


def lru_fwd(aux, cw, cb, wa, ba, wx, bx, lam, *, name):
    T = aux.shape[0]
    S, C = SEQ, LRU_HALF_W

    def body(x_ref, g_ref, cw_ref, cb_ref, wa_ref, ba_ref, wx_ref, bx_ref, lam_ref, o_ref, h_ref, a_s, u_s):
        xc, _, r, ig, sp, a, mm = _lru_gates(x_ref[...], cw_ref[...], cb_ref[...], wa_ref[...], ba_ref[...],
                                             wx_ref[...], bx_ref[...], lam_ref[...])
        a_s[...] = a
        u_s[...] = mm * (ig * xc)

        def group(i, h):
            base = pl.multiple_of(i * 8, 8)
            a8 = a_s[pl.ds(base, 8), :]
            u8 = u_s[pl.ds(base, 8), :]
            for rr in range(8):
                h = a8[rr:rr + 1] * h + u8[rr:rr + 1]
                h_ref[pl.ds(base + rr, 1), :] = h
            return h

        lax.fori_loop(0, S // 8, group, jnp.zeros((1, C), F32))
        gate = g_ref[...]
        o_ref[...] = (h_ref[...] * (gate * _gelu_cdf(gate))).astype(BF16)

    blk = lambda col: pl.BlockSpec((S, C), lambda c, b: (b, col + c))
    par = lambda rows: pl.BlockSpec((rows, C), lambda c, b: (0, c))
    sq = pl.BlockSpec((None, C, C), lambda c, b: (c, 0, 0))
    sd = jax.ShapeDtypeStruct
    W = LRU_HALVES * C
    return pl.pallas_call(
        body, name=name, grid=(LRU_HALVES, T // S),
        in_specs=[blk(0), blk(LRU_HALVES), par(4), par(1), sq, par(1), sq, par(1), par(1)],
        out_specs=[blk(0), blk(0)], out_shape=[sd((T, W), BF16), sd((T, W), F32)],
        scratch_shapes=[pltpu.VMEM((S, C), F32), pltpu.VMEM((S, C), F32)],
        compiler_params=_params("parallel", "parallel"),
    )(aux, aux, cw, cb, wa, ba, wx, bx, lam)


def lru_bwd(aux, h, dmixed, cw, cb, wa, ba, wx, bx, lam, *, name):
    T = aux.shape[0]
    S, C = SEQ, LRU_HALF_W

    def body(x_ref, g_ref, h_ref, do_ref, cw_ref, cb_ref, wa_ref, ba_ref, wx_ref, bx_ref, lam_ref,
             dx_ref, dgate_ref, dcw_ref, dcb_ref, dwa_ref, dba_ref, dwx_ref, dbx_ref, dlam_ref, a_s, d_s):
        x = x_ref[...]
        cw = cw_ref[...]
        lam = lam_ref[...]
        xc, xb, r, ig, sp, a, mm = _lru_gates(x, cw, cb_ref[...], wa_ref[...], ba_ref[...], wx_ref[...], bx_ref[...], lam)
        gate = g_ref[...]
        gl, dgl = _gelu_and_grad(gate)
        dout = do_ref[...]
        hv = h_ref[...]
        dgate_ref[...] = dout * hv * dgl
        a_s[...] = a
        d_s[...] = dout * gl

        def group(i, c):
            base = pl.multiple_of((S // 8 - 1 - i) * 8, 8)
            a8 = a_s[pl.ds(base, 8), :]
            d8 = d_s[pl.ds(base, 8), :]
            for rr in range(7, -1, -1):
                d = d8[rr:rr + 1] + c
                d_s[pl.ds(base + rr, 1), :] = d
                c = a8[rr:rr + 1] * d
            return c

        lax.fori_loop(0, S // 8, group, jnp.zeros((1, C), F32))
        row = lax.broadcasted_iota(jnp.int32, (S, 1), 0)
        dht = d_s[...]
        h_prev = jnp.where(row >= 1, pltpu.roll(hv, 1, 0), 0.0)
        da = dht * h_prev
        gx = ig * xc
        dmm = dht * gx
        dig = dht * mm * xc
        dxc = dht * mm * ig
        dla = da * a - dmm * (a * a) / mm
        dr = dla * (-LRU_C * sp)
        dsp = jnp.sum(dla * (-LRU_C * r), axis=0, keepdims=True)
        dlam = dsp * (-jax.nn.sigmoid(-lam))
        dpa = dr * r * (1.0 - r)
        dpx = dig * ig * (1.0 - ig)
        dpa_b, dpx_b = dpa.astype(BF16), dpx.astype(BF16)
        nt = (((1,), (1,)), ((), ()))
        tn = (((0,), (0,)), ((), ()))
        dxc = dxc + lax.dot_general(dpa_b, wa_ref[...], nt, preferred_element_type=F32) \
                  + lax.dot_general(dpx_b, wx_ref[...], nt, preferred_element_type=F32)
        dwa = lax.dot_general(xb, dpa_b, tn, preferred_element_type=F32)
        dwx = lax.dot_general(xb, dpx_b, tn, preferred_element_type=F32)

        def fwd(v, s):
            return jnp.where(row < S - s, pltpu.roll(v, S - s, 0), 0.0)

        def back(v, s):
            return jnp.where(row >= s, pltpu.roll(v, s, 0), 0.0)

        dx_ref[...] = cw[3:4] * dxc + cw[2:3] * fwd(dxc, 1) + cw[1:2] * fwd(dxc, 2) + cw[0:1] * fwd(dxc, 3)
        s0 = lambda v: jnp.sum(v, axis=0, keepdims=True)
        dcw = jnp.concatenate([s0(dxc * back(x, 3)), s0(dxc * back(x, 2)), s0(dxc * back(x, 1)), s0(dxc * x)], axis=0)
        parts = ((dcw_ref, dcw), (dcb_ref, s0(dxc)), (dwa_ref, dwa), (dba_ref, s0(dpa)), (dwx_ref, dwx),
                 (dbx_ref, s0(dpx)), (dlam_ref, dlam))

        @pl.when(pl.program_id(1) == 0)
        def _():
            for ref, val in parts:
                ref[...] = val

        @pl.when(pl.program_id(1) > 0)
        def _():
            for ref, val in parts:
                ref[...] += val

    blk = lambda col: pl.BlockSpec((S, C), lambda c, b: (b, col + c))
    par = lambda rows: pl.BlockSpec((rows, C), lambda c, b: (0, c))
    sq = pl.BlockSpec((None, C, C), lambda c, b: (c, 0, 0))
    sd = jax.ShapeDtypeStruct
    W = LRU_HALVES * C
    vec = sd((1, W), F32)
    return pl.pallas_call(
        body, name=name, grid=(LRU_HALVES, T // S),
        in_specs=[blk(0), blk(LRU_HALVES), blk(0), blk(3 * LRU_HALVES), par(4), par(1), sq, par(1), sq, par(1), par(1)],
        out_specs=[blk(0), blk(0), par(4), par(1), sq, par(1), sq, par(1), par(1)],
        out_shape=[sd((T, W), F32), sd((T, W), F32), sd((4, W), F32), vec, sd((LRU_HALVES, C, C), F32), vec,
                   sd((LRU_HALVES, C, C), F32), vec, vec],
        scratch_shapes=[pltpu.VMEM((S, C), F32), pltpu.VMEM((S, C), F32)],
        compiler_params=_params("parallel", "arbitrary"),
    )(aux, aux, h, dmixed, cw, cb, wa, ba, wx, bx, lam)


_NT = (((1,), (1,)), ((), ()))
_TN = (((0,), (0,)), ((), ()))


def _dot(a, b, dims=None):
    if dims is None:
        return jnp.dot(a, b, preferred_element_type=F32)
    return lax.dot_general(a, b, dims, preferred_element_type=F32)


def _hs(h):
    return slice(h * HEAD_DIM, (h + 1) * HEAD_DIM)


def cross_fwd(q, kv, *, name):
    T = q.shape[0]
    tq = 512

    def body(q_ref, kv_ref, o_ref):
        for h in range(N_HEADS):
            qh = q_ref[:, _hs(h)] * Q_SCALE
            k = kv_ref[:, _hs(h)]
            v = kv_ref[:, GROUP_W + h * HEAD_DIM:GROUP_W + (h + 1) * HEAD_DIM]
            s = _dot(qh, k, _NT)
            p = jnp.exp(s - jnp.max(s, axis=-1, keepdims=True))
            p = p / jnp.sum(p, axis=-1, keepdims=True)
            o_ref[:, _hs(h)] = _dot(p.astype(BF16), v).astype(BF16)

    per = SEQ // tq
    return pl.pallas_call(
        body, name=name, grid=(T // tq,),
        in_specs=[pl.BlockSpec((tq, GROUP_W), lambda i: (i, 0)), pl.BlockSpec((N_MEM, 2 * GROUP_W), lambda i: (i // per, 0))],
        out_specs=pl.BlockSpec((tq, GROUP_W), lambda i: (i, 0)), out_shape=jax.ShapeDtypeStruct((T, GROUP_W), BF16),
        compiler_params=_params("parallel"),
    )(q, kv)


def cross_bwd(q, kv, do, *, name):
    T = q.shape[0]
    tq = 512
    per = SEQ // tq

    def body(q_ref, kv_ref, do_ref, dq_ref, dkv_ref):
        first = pl.program_id(0) % per == 0
        for h in range(N_HEADS):
            vs = slice(GROUP_W + h * HEAD_DIM, GROUP_W + (h + 1) * HEAD_DIM)
            qh = q_ref[:, _hs(h)] * Q_SCALE
            k = kv_ref[:, _hs(h)]
            v = kv_ref[:, vs]
            doh = do_ref[:, _hs(h)].astype(BF16)
            s = _dot(qh, k, _NT)
            p = jnp.exp(s - jnp.max(s, axis=-1, keepdims=True))
            p = p / jnp.sum(p, axis=-1, keepdims=True)
            dp = _dot(doh, v, _NT)
            ds = (p * (dp - jnp.sum(p * dp, axis=-1, keepdims=True))).astype(BF16)
            dq_ref[:, _hs(h)] = (_dot(ds, k) * Q_SCALE).astype(BF16)
            dk = _dot(ds, qh, _TN)
            dv = _dot(p.astype(BF16), doh, _TN)

            @pl.when(first)
            def _():
                dkv_ref[:, _hs(h)] = dk
                dkv_ref[:, vs] = dv

            @pl.when(jnp.logical_not(first))
            def _():
                dkv_ref[:, _hs(h)] += dk
                dkv_ref[:, vs] += dv

    qb = pl.BlockSpec((tq, GROUP_W), lambda i: (i, 0))
    kvb = pl.BlockSpec((N_MEM, 2 * GROUP_W), lambda i: (i // per, 0))
    sd = jax.ShapeDtypeStruct
    return pl.pallas_call(
        body, name=name, grid=(T // tq,),
        in_specs=[qb, kvb, qb], out_specs=[qb, kvb],
        out_shape=[sd((T, GROUP_W), BF16), sd(kv.shape, F32)],
        compiler_params=_params("arbitrary"),
    )(q, kv, do)


NB = SEQ // BLOCK
NEG = -1e30
HEADS = tuple(range(N_HEADS))


def _blk(i):
    return pl.ds(pl.multiple_of(i * BLOCK, BLOCK), BLOCK)


def _qkv_specs(first_col):
    return [pl.BlockSpec((SEQ, GROUP_W), lambda b, c=first_col + j: (b, c)) for j in range(3)]


LANES = 128
CUM_BLK = 256


def col_to_row(c):
    b = c.shape[0] // SEQ
    return c.reshape(b, SEQ, LANES)[:, :, :8].transpose(0, 2, 1).reshape(b * 8, SEQ)


def row_to_col(r):
    b = r.shape[0] // 8
    c = r.reshape(b, 8, SEQ).transpose(0, 2, 1)
    return jnp.pad(c, ((0, 0), (0, 0), (0, LANES - 8))).reshape(b * SEQ, LANES)


def fox_prep(aux, bf, *, name):
    T = aux.shape[0]

    def body(f_ref, b_ref, o_ref):
        row = lax.broadcasted_iota(jnp.int32, (CUM_BLK, CUM_BLK), 0)
        col = lax.broadcasted_iota(jnp.int32, (CUM_BLK, CUM_BLK), 1)
        upto = (col <= row).astype(BF16)
        carry = jnp.zeros((1, LANES), F32)
        for n in range(SEQ // CUM_BLK):
            rows = slice(n * CUM_BLK, (n + 1) * CUM_BLK)
            logf = -_softplus(-(f_ref[rows, :] + b_ref[...]))
            hi = logf.astype(BF16)
            lo = (logf - hi.astype(F32)).astype(BF16)
            cum = _dot(upto, hi) + _dot(upto, lo) + carry
            o_ref[rows, :] = cum
            carry = cum[CUM_BLK - 1:CUM_BLK]

    return pl.pallas_call(
        body, name=name, grid=(T // SEQ,),
        in_specs=[pl.BlockSpec((SEQ, LANES), lambda b: (b, 4)), pl.BlockSpec((1, LANES), lambda b: (0, 0))],
        out_specs=pl.BlockSpec((SEQ, LANES), lambda b: (b, 0)), out_shape=jax.ShapeDtypeStruct((T, LANES), F32),
        compiler_params=_params("parallel"),
    )(aux, bf)


def fox_prep_bwd(aux, bf, dcum, *, name):
    T = aux.shape[0]

    def body(f_ref, b_ref, d_ref, df_ref, db_ref):
        row = lax.broadcasted_iota(jnp.int32, (CUM_BLK, CUM_BLK), 0)
        col = lax.broadcasted_iota(jnp.int32, (CUM_BLK, CUM_BLK), 1)
        onward = (col >= row).astype(BF16)
        carry = jnp.zeros((1, LANES), F32)
        tot = jnp.zeros((1, LANES), F32)
        for n in range(SEQ // CUM_BLK - 1, -1, -1):
            rows = slice(n * CUM_BLK, (n + 1) * CUM_BLK)
            d = d_ref[rows, :]
            hi = d.astype(BF16)
            lo = (d - hi.astype(F32)).astype(BF16)
            dlogf = _dot(onward, hi) + _dot(onward, lo) + carry
            carry = dlogf[0:1]
            df = dlogf * jax.nn.sigmoid(-(f_ref[rows, :] + b_ref[...]))
            df_ref[rows, :] = df
            tot = tot + jnp.sum(df, axis=0, keepdims=True)

        @pl.when(pl.program_id(0) == 0)
        def _():
            db_ref[...] = tot

        @pl.when(pl.program_id(0) > 0)
        def _():
            db_ref[...] += tot

    blk = pl.BlockSpec((SEQ, LANES), lambda b: (b, 0))
    vec = pl.BlockSpec((1, LANES), lambda b: (0, 0))
    sd = jax.ShapeDtypeStruct
    return pl.pallas_call(
        body, name=name, grid=(T // SEQ,),
        in_specs=[pl.BlockSpec((SEQ, LANES), lambda b: (b, 4)), vec, blk],
        out_specs=[blk, vec], out_shape=[sd((T, LANES), F32), sd((1, LANES), F32)],
        compiler_params=_params("arbitrary"),
    )(aux, bf, dcum)


CHUNK = 256
WIDE = N_HEADS * CHUNK
NCH = SEQ // CHUNK


def _seg(h):
    return slice(h * CHUNK, (h + 1) * CHUNK)


def _chunk_rows(c):
    return pl.ds(pl.multiple_of(c * CHUNK, CHUNK), CHUNK)


def _wide_consts():
    r = lax.broadcasted_iota(jnp.int32, (WIDE, GROUP_W), 0)
    f = lax.broadcasted_iota(jnp.int32, (WIDE, GROUP_W), 1)
    bd = (r // CHUNK) == (f // HEAD_DIM)
    row = lax.broadcasted_iota(jnp.int32, (BLOCK, WIDE), 0)
    key = lax.broadcasted_iota(jnp.int32, (BLOCK, WIDE), 1) % CHUNK
    return bd, row, key


def _block_diag(x, bd):
    return jnp.where(bd, jnp.concatenate([x] * N_HEADS, axis=0), jnp.zeros((), x.dtype))


def _fold_heads(w, bd):
    w = jnp.where(bd, w, 0.0)
    return (w[0:CHUNK] + w[CHUNK:2 * CHUNK]) + (w[2 * CHUNK:3 * CHUNK] + w[3 * CHUNK:])


def _widen(cols):
    return jnp.concatenate([jnp.broadcast_to(c, (BLOCK, CHUNK)) for c in cols], axis=1)


def _head_rowsums(w):
    return [jnp.sum(w[:, _seg(h)], axis=1, keepdims=True) for h in HEADS]


def _tri_wide(x, tri):
    hi = x.astype(BF16)
    lo = (x - hi.astype(F32)).astype(BF16)
    y = _dot(jnp.concatenate([hi[:, _seg(h)] for h in HEADS] + [lo[:, _seg(h)] for h in HEADS], axis=0), tri)
    return jnp.concatenate([y[h * BLOCK:(h + 1) * BLOCK] + y[(N_HEADS + h) * BLOCK:(N_HEADS + h + 1) * BLOCK] for h in HEADS], axis=1)


def _feature_widen(cols):
    return jnp.concatenate([jnp.broadcast_to(c, (BLOCK, HEAD_DIM)) for c in cols], axis=1)


def _loop_by_two(n, index, body, carry):
    odd = n % 2
    carry = lax.fori_loop(0, odd, lambda _, cr: body(index(0), cr), carry)
    return lax.fori_loop(0, n // 2, lambda t, cr: body(index(odd + 2 * t + 1), body(index(odd + 2 * t), cr)), carry)


def _sbw_scores(q, kbd, later):
    z = _dot(q, kbd, _NT)
    lk = -_softplus(z)
    return z + lk, lk, _tri_wide(lk, later)


def _sbw_tile(q, kbd, mask, later, csum):
    z = _dot(q, kbd, _NT)
    lk = -_softplus(z)
    if mask is not None:
        lk = jnp.where(mask, lk, 0.0)
    e = z + lk
    att = jnp.exp(e + _tri_wide(lk, later) + csum)
    if mask is not None:
        att = jnp.where(mask, att, 0.0)
    return att, e, lk


def sbw_fwd(qkv, *, name):
    T = qkv.shape[0]

    def body(q_ref, k_ref, v_ref, o_ref):
        bd, row, key = _wide_consts()
        r2 = lax.broadcasted_iota(jnp.int32, (CHUNK, CHUNK), 0)
        c2 = lax.broadcasted_iota(jnp.int32, (CHUNK, CHUNK), 1)
        later = (r2 > c2).astype(BF16)

        def qblock(i, _):
            q = q_ref[_blk(i), :] * Q_SCALE
            cd = i // 2
            strict = key < row + BLOCK * (i % 2)

            def tile(c, mask, carry):
                acc, csum = carry
                att, _, lk = _sbw_tile(q, _block_diag(k_ref[_chunk_rows(c), :], bd), mask, later, csum)
                acc = acc + _dot(att.astype(BF16), _block_diag(v_ref[_chunk_rows(c), :], bd))
                return acc, csum + _widen(_head_rowsums(lk))

            def two_tiles(c1, carry):
                acc, csum = carry
                e1, lk1, t1 = _sbw_scores(q, _block_diag(k_ref[_chunk_rows(c1), :], bd), later)
                e2, lk2, t2 = _sbw_scores(q, _block_diag(k_ref[_chunk_rows(c1 - 1), :], bd), later)
                att1 = jnp.exp(e1 + t1 + csum)
                csum = csum + _widen(_head_rowsums(lk1))
                att2 = jnp.exp(e2 + t2 + csum)
                csum = csum + _widen(_head_rowsums(lk2))
                acc = acc + _dot(att1.astype(BF16), _block_diag(v_ref[_chunk_rows(c1), :], bd))
                acc = acc + _dot(att2.astype(BF16), _block_diag(v_ref[_chunk_rows(c1 - 1), :], bd))
                return acc, csum

            carry = tile(cd, strict, (jnp.zeros((BLOCK, GROUP_W), F32), jnp.zeros((BLOCK, WIDE), F32)))
            odd = cd % 2
            carry = lax.fori_loop(0, odd, lambda n, cr: tile(cd - 1, None, cr), carry)
            acc, _ = lax.fori_loop(0, cd // 2, lambda n, cr: two_tiles(cd - 1 - odd - 2 * n, cr), carry)
            o_ref[_blk(i), :] = acc.astype(BF16)
            return 0

        lax.fori_loop(0, NB, qblock, 0)

    return pl.pallas_call(
        body, name=name, grid=(T // SEQ,), in_specs=_qkv_specs(0),
        out_specs=pl.BlockSpec((SEQ, GROUP_W), lambda b: (b, 0)), out_shape=jax.ShapeDtypeStruct((T, GROUP_W), BF16),
        compiler_params=_params("parallel"),
    )(qkv, qkv, qkv)


def sbw_bwd(qkv, dmixed, *, name):
    T = qkv.shape[0]

    def body(q_ref, k_ref, v_ref, do_ref, dq_ref, dk_ref, dv_ref, att_s, sg_s):
        bd, row, key = _wide_consts()
        r2 = lax.broadcasted_iota(jnp.int32, (CHUNK, CHUNK), 0)
        c2 = lax.broadcasted_iota(jnp.int32, (CHUNK, CHUNK), 1)
        later = (r2 > c2).astype(BF16)
        earlier = (r2 < c2).astype(BF16)
        dk_ref[...] = jnp.zeros_like(dk_ref)
        dv_ref[...] = jnp.zeros_like(dv_ref)

        def qblock(i, _):
            q = q_ref[_blk(i), :] * Q_SCALE
            do = do_ref[_blk(i), :].astype(BF16)
            cd = i // 2
            strict = key < row + BLOCK * (i % 2)

            def recompute(c, mask, csum):
                att, e, lk = _sbw_tile(q, _block_diag(k_ref[_chunk_rows(c), :], bd), mask, later, csum)
                sg = jnp.exp(e)
                att_s[c] = att
                sg_s[c] = sg if mask is None else jnp.where(mask, sg, 0.0)
                return csum + _widen(_head_rowsums(lk))

            def recompute_two(c1, csum):
                e1, lk1, t1 = _sbw_scores(q, _block_diag(k_ref[_chunk_rows(c1), :], bd), later)
                e2, lk2, t2 = _sbw_scores(q, _block_diag(k_ref[_chunk_rows(c1 - 1), :], bd), later)
                sg_s[c1] = jnp.exp(e1)
                sg_s[c1 - 1] = jnp.exp(e2)
                att_s[c1] = jnp.exp(e1 + t1 + csum)
                csum = csum + _widen(_head_rowsums(lk1))
                att_s[c1 - 1] = jnp.exp(e2 + t2 + csum)
                return csum + _widen(_head_rowsums(lk2))

            csum = recompute(cd, strict, jnp.zeros((BLOCK, WIDE), F32))
            odd = cd % 2
            csum = lax.fori_loop(0, odd, lambda n, cs: recompute(cd - 1, None, cs), csum)
            lax.fori_loop(0, cd // 2, lambda n, cs: recompute_two(cd - 1 - odd - 2 * n, cs), csum)

            def tile(c, carry):
                dq, pre = carry
                kbd = _block_diag(k_ref[_chunk_rows(c), :], bd)
                vbd = _block_diag(v_ref[_chunk_rows(c), :], bd)
                att = att_s[c]
                ds = _dot(do, vbd, _NT) * att
                dlk = ds + _tri_wide(ds, earlier) + pre
                dz = (ds - dlk * sg_s[c]).astype(BF16)
                dk_ref[_chunk_rows(c), :] += _fold_heads(_dot(dz, q, _TN), bd)
                dv_ref[_chunk_rows(c), :] += _fold_heads(_dot(att.astype(BF16), do, _TN), bd)
                return dq + _dot(dz, kbd), pre + _widen(_head_rowsums(ds))

            def two_tiles(c1, carry):
                dq, pre = carry
                c2 = c1 + 1
                kbd1, kbd2 = _block_diag(k_ref[_chunk_rows(c1), :], bd), _block_diag(k_ref[_chunk_rows(c2), :], bd)
                att1, att2 = att_s[c1], att_s[c2]
                ds1 = _dot(do, _block_diag(v_ref[_chunk_rows(c1), :], bd), _NT) * att1
                ds2 = _dot(do, _block_diag(v_ref[_chunk_rows(c2), :], bd), _NT) * att2
                tri1, tri2 = _tri_wide(ds1, earlier), _tri_wide(ds2, earlier)
                dv_ref[_chunk_rows(c1), :] += _fold_heads(_dot(att1.astype(BF16), do, _TN), bd)
                dv_ref[_chunk_rows(c2), :] += _fold_heads(_dot(att2.astype(BF16), do, _TN), bd)
                dz1 = (ds1 - (ds1 + tri1 + pre) * sg_s[c1]).astype(BF16)
                pre = pre + _widen(_head_rowsums(ds1))
                dz2 = (ds2 - (ds2 + tri2 + pre) * sg_s[c2]).astype(BF16)
                pre = pre + _widen(_head_rowsums(ds2))
                dk_ref[_chunk_rows(c1), :] += _fold_heads(_dot(dz1, q, _TN), bd)
                dk_ref[_chunk_rows(c2), :] += _fold_heads(_dot(dz2, q, _TN), bd)
                return dq + _dot(dz1, kbd1) + _dot(dz2, kbd2), pre

            n_tiles = cd + 1
            odd = n_tiles % 2
            carry = (jnp.zeros((BLOCK, GROUP_W), F32), jnp.zeros((BLOCK, WIDE), F32))
            carry = lax.fori_loop(0, odd, lambda n, cr: tile(0, cr), carry)
            dq, _ = lax.fori_loop(0, n_tiles // 2, lambda n, cr: two_tiles(odd + 2 * n, cr), carry)
            dq_ref[_blk(i), :] = dq * Q_SCALE
            return 0

        lax.fori_loop(0, NB, qblock, 0)

    out = pl.BlockSpec((SEQ, GROUP_W), lambda b: (b, 0))
    sd = jax.ShapeDtypeStruct((T, GROUP_W), F32)
    return pl.pallas_call(
        body, name=name, grid=(T // SEQ,), in_specs=_qkv_specs(0) + [out],
        out_specs=[out] * 3, out_shape=[sd] * 3,
        scratch_shapes=[pltpu.VMEM((NCH, BLOCK, WIDE), F32), pltpu.VMEM((NCH, BLOCK, WIDE), F32)],
        compiler_params=_params("parallel"),
    )(qkv, qkv, qkv, dmixed)


def _foxw_logits(q, kbd, cq, cr_ref, c, mask):
    ck = jnp.concatenate([cr_ref[h:h + 1, _chunk_rows(c)] for h in HEADS], axis=1)
    z = _dot(q, kbd, _NT) + cq - ck
    return z if mask is None else jnp.where(mask, z, NEG)


def foxw_fwd(qkv, cumc, cumr, *, name):
    T = qkv.shape[0]

    def body(q_ref, k_ref, v_ref, cc_ref, cr_ref, o_ref, o32_ref, lse_ref, z_s):
        bd, row, key = _wide_consts()
        lse_ref[...] = jnp.zeros_like(lse_ref)

        def qblock(i, _):
            q = q_ref[_blk(i), :] * Q_SCALE
            cq = _widen([cc_ref[_blk(i), h:h + 1] for h in HEADS])
            cd = i // 2
            causal = key <= row + BLOCK * (i % 2)

            def logits(c, mask, ms):
                z = _foxw_logits(q, _block_diag(k_ref[_chunk_rows(c), :], bd), cq, cr_ref, c, mask)
                z_s[c] = z
                return tuple(jnp.maximum(ms[h], jnp.max(z[:, _seg(h)], axis=1, keepdims=True)) for h in HEADS)

            ms = logits(cd, causal, (jnp.full((BLOCK, 1), NEG, F32),) * N_HEADS)
            ms = _loop_by_two(cd, lambda n: n, lambda c, m: logits(c, None, m), ms)
            m_wide = _widen(ms)

            def values(c, carry):
                acc, l = carry
                p = jnp.exp(z_s[c] - m_wide)
                return acc + _dot(p.astype(BF16), _block_diag(v_ref[_chunk_rows(c), :], bd)), l + _widen(_head_rowsums(p))

            acc, l = _loop_by_two(cd + 1, lambda n: n, values, (jnp.zeros((BLOCK, GROUP_W), F32), jnp.zeros((BLOCK, WIDE), F32)))
            ls = [l[:, h * CHUNK:h * CHUNK + 1] for h in HEADS]
            o = acc / _feature_widen(ls)
            o_ref[_blk(i), :] = o.astype(BF16)
            o32_ref[_blk(i), :] = o
            for h in HEADS:
                lse_ref[_blk(i), h:h + 1] = ms[h] + jnp.log(ls[h])
            return 0

        lax.fori_loop(0, NB, qblock, 0)

    out = pl.BlockSpec((SEQ, GROUP_W), lambda b: (b, 0))
    colb = pl.BlockSpec((SEQ, LANES), lambda b: (b, 0))
    sd = jax.ShapeDtypeStruct
    return pl.pallas_call(
        body, name=name, grid=(T // SEQ,),
        in_specs=_qkv_specs(3) + [colb, pl.BlockSpec((8, SEQ), lambda b: (b, 0))],
        out_specs=[out, out, colb], out_shape=[sd((T, GROUP_W), BF16), sd((T, GROUP_W), F32), sd((T, LANES), F32)],
        scratch_shapes=[pltpu.VMEM((NCH, BLOCK, WIDE), F32)],
        compiler_params=_params("parallel"),
    )(qkv, qkv, qkv, cumc, cumr)


def foxw_bwd(qkv, cumc, cumr, lse, o32, dmixed, *, name):
    T = qkv.shape[0]

    def body(q_ref, k_ref, v_ref, cc_ref, cr_ref, lse_ref, o_ref, do_ref, dq_ref, dk_ref, dv_ref, dcc_ref, dcr_ref):
        bd, row, key = _wide_consts()
        dk_ref[...] = jnp.zeros_like(dk_ref)
        dv_ref[...] = jnp.zeros_like(dv_ref)
        dcc_ref[...] = jnp.zeros_like(dcc_ref)
        dcr_ref[...] = jnp.zeros_like(dcr_ref)

        def qblock(i, _):
            q = q_ref[_blk(i), :] * Q_SCALE
            do32 = do_ref[_blk(i), :]
            do = do32.astype(BF16)
            prod = do32 * o_ref[_blk(i), :]
            delta = _widen([jnp.sum(prod[:, _hs(h)], axis=1, keepdims=True) for h in HEADS])
            cq = _widen([cc_ref[_blk(i), h:h + 1] for h in HEADS])
            lse_w = _widen([lse_ref[_blk(i), h:h + 1] for h in HEADS])
            cd = i // 2
            causal = key <= row + BLOCK * (i % 2)

            def tile(c, mask, carry):
                dq, dcq = carry
                kbd = _block_diag(k_ref[_chunk_rows(c), :], bd)
                vbd = _block_diag(v_ref[_chunk_rows(c), :], bd)
                p = jnp.exp(_foxw_logits(q, kbd, cq, cr_ref, c, mask) - lse_w)
                ds = p * (_dot(do, vbd, _NT) - delta)
                dsb = ds.astype(BF16)
                dk_ref[_chunk_rows(c), :] += _fold_heads(_dot(dsb, q, _TN), bd)
                dv_ref[_chunk_rows(c), :] += _fold_heads(_dot(p.astype(BF16), do, _TN), bd)
                for h in HEADS:
                    dcr_ref[h:h + 1, _chunk_rows(c)] -= jnp.sum(ds[:, _seg(h)], axis=0, keepdims=True)
                return dq + _dot(dsb, kbd), dcq + _widen(_head_rowsums(ds))

            def two_tiles(c1, carry):
                dq, dcq = carry
                cs = (c1, c1 + 1)
                kbds = [_block_diag(k_ref[_chunk_rows(c), :], bd) for c in cs]
                vbds = [_block_diag(v_ref[_chunk_rows(c), :], bd) for c in cs]
                ps = [jnp.exp(_foxw_logits(q, kbds[j], cq, cr_ref, cs[j], None) - lse_w) for j in range(2)]
                dss = [ps[j] * (_dot(do, vbds[j], _NT) - delta) for j in range(2)]
                dsbs = [d.astype(BF16) for d in dss]
                for j, c in enumerate(cs):
                    dk_ref[_chunk_rows(c), :] += _fold_heads(_dot(dsbs[j], q, _TN), bd)
                    dv_ref[_chunk_rows(c), :] += _fold_heads(_dot(ps[j].astype(BF16), do, _TN), bd)
                    for h in HEADS:
                        dcr_ref[h:h + 1, _chunk_rows(c)] -= jnp.sum(dss[j][:, _seg(h)], axis=0, keepdims=True)
                dq = dq + _dot(dsbs[0], kbds[0]) + _dot(dsbs[1], kbds[1])
                return dq, dcq + _widen(_head_rowsums(dss[0])) + _widen(_head_rowsums(dss[1]))

            carry = tile(cd, causal, (jnp.zeros((BLOCK, GROUP_W), F32), jnp.zeros((BLOCK, WIDE), F32)))
            odd = cd % 2
            carry = lax.fori_loop(0, odd, lambda n, cr: tile(0, None, cr), carry)
            dq, dcq = lax.fori_loop(0, cd // 2, lambda n, cr: two_tiles(odd + 2 * n, cr), carry)
            dq_ref[_blk(i), :] = dq * Q_SCALE
            for h in HEADS:
                dcc_ref[_blk(i), h:h + 1] = dcq[:, h * CHUNK:h * CHUNK + 1]
            return 0

        lax.fori_loop(0, NB, qblock, 0)

    out = pl.BlockSpec((SEQ, GROUP_W), lambda b: (b, 0))
    colb = pl.BlockSpec((SEQ, LANES), lambda b: (b, 0))
    rowb = pl.BlockSpec((8, SEQ), lambda b: (b, 0))
    sd = jax.ShapeDtypeStruct
    big = sd((T, GROUP_W), F32)
    return pl.pallas_call(
        body, name=name, grid=(T // SEQ,),
        in_specs=_qkv_specs(3) + [colb, rowb, colb, out, pl.BlockSpec((SEQ, GROUP_W), lambda b: (b, 1))],
        out_specs=[out, out, out, colb, rowb],
        out_shape=[big, big, big, sd((T, LANES), F32), sd((T // SEQ * 8, SEQ), F32)],
        compiler_params=_params("parallel"),
    )(qkv, qkv, qkv, cumc, cumr, lse, o32, dmixed)


BAND = 2 * BLOCK


def _t5_bucket_np(dist):
    n = np.maximum(dist, 0)
    max_exact = NUM_BUCKETS // 2
    nf = np.maximum(n, 1).astype(np.float32)
    large = max_exact + (np.log(nf / np.float32(max_exact)) / np.float32(math.log(MAX_DISTANCE / max_exact))
                         * np.float32(NUM_BUCKETS - max_exact)).astype(np.int32)
    large = np.minimum(large, NUM_BUCKETS - 1)
    return np.where(n < max_exact, n, large).astype(np.int32)


def _band_buckets():
    qi = np.arange(BLOCK)[:, None]
    ki = np.arange(BAND)[None, :]
    delta = np.clip(qi - ki + BLOCK, 0, BLOCK)
    return np.stack([_t5_bucket_np(delta * d) for d in DILATIONS])


def relbias_expand(rel, *, name):
    buckets = jnp.asarray(_band_buckets())
    n_pat = len(DILATIONS)

    def body(rel_ref, bk_ref, o_ref):
        for p in range(n_pat):
            bk = bk_ref[p]
            for h in range(N_HEADS):
                acc = jnp.zeros((BLOCK, BAND), F32)
                for b in range(NUM_BUCKETS):
                    acc = jnp.where(bk == b, rel_ref[b, h], acc)
                o_ref[p * N_HEADS + h] = acc

    return pl.pallas_call(
        body, name=name,
        in_specs=[pl.BlockSpec(memory_space=pltpu.SMEM), pl.BlockSpec(memory_space=pltpu.VMEM)],
        out_specs=pl.BlockSpec(memory_space=pltpu.VMEM),
        out_shape=jax.ShapeDtypeStruct((n_pat * N_HEADS, BLOCK, BAND), F32),
        compiler_params=_params(),
    )(rel, buckets)


def relbias_reduce(ds_all, *, name):
    buckets = jnp.asarray(_band_buckets())
    n_pat = len(DILATIONS)

    def body(ds_ref, bk_ref, o_ref):
        for b in range(NUM_BUCKETS):
            for h in range(N_HEADS):
                tot = jnp.float32(0.0)
                for p in range(n_pat):
                    tot = tot + jnp.sum(jnp.where(bk_ref[p] == b, ds_ref[p * N_HEADS + h], 0.0))
                o_ref[b, h] = tot

    return pl.pallas_call(
        body, name=name,
        in_specs=[pl.BlockSpec(memory_space=pltpu.VMEM), pl.BlockSpec(memory_space=pltpu.VMEM)],
        out_specs=pl.BlockSpec(memory_space=pltpu.SMEM),
        out_shape=jax.ShapeDtypeStruct((NUM_BUCKETS, N_HEADS), F32),
        compiler_params=_params(),
    )(ds_all, buckets)


def _band_valid_wide(first, row, key):
    inside = jnp.logical_and(key >= row, key <= row + BLOCK)
    return jnp.logical_and(inside, jnp.logical_or(jnp.logical_not(first), key >= BLOCK))


QKV_BLOCKS = 9


def _band_in_specs(d, pattern, has_prev):
    rows = BLOCK * d
    cur = lambda c: pl.BlockSpec((rows, GROUP_W), lambda tb, r: (tb, c))
    prev = lambda c: pl.BlockSpec((rows, GROUP_W), lambda tb, r: (jnp.maximum(tb - 1, 0), c))
    bias = pl.BlockSpec((N_HEADS, BLOCK, BAND), lambda tb, r: (pattern, 0, 0))
    return [cur(6), cur(7), cur(8)] + ([prev(7), prev(8)] if has_prev else []) + [bias]


def _classes_per_step(d):
    return 2 if d > 1 else 1


def _step_classes(d):
    n = _classes_per_step(d)
    return [pl.program_id(1) * n + j for j in range(n)]


def _class_rows(d, cls):
    return pl.ds(cls, BLOCK, stride=d) if d > 1 else pl.ds(0, BLOCK)


def _halves_scratch(rows, n):
    return [pltpu.VMEM((2, rows, LANES), F32)] * n


def _stage(refs, scratch):
    @pl.when(pl.program_id(1) == 0)
    def _():
        for src, dst in zip(refs, scratch):
            dst[0] = src[:, :LANES].astype(F32)
            dst[1] = src[:, LANES:].astype(F32)


def _take_class(s, d, cls):
    rows = _class_rows(d, cls)
    return jnp.concatenate([s.at[0][rows, :], s.at[1][rows, :]], axis=1)


def _put_class(s, d, cls, x):
    rows = _class_rows(d, cls)
    s.at[0][rows, :] = x[:, :LANES]
    s.at[1][rows, :] = x[:, LANES:]


def _flush(scratch, refs, d):
    @pl.when(pl.program_id(1) == d // _classes_per_step(d) - 1)
    def _():
        for s, o in zip(scratch, refs):
            o[...] = jnp.concatenate([s[0], s[1]], axis=1)


def _band_operands(scratch, d, cls, has_prev):
    take = lambda s: _take_class(s, d, cls).astype(BF16)
    q = (_take_class(scratch[0], d, cls) * Q_SCALE).astype(BF16)
    if has_prev:
        k = jnp.concatenate([take(scratch[3]), take(scratch[1])], axis=0)
        v = jnp.concatenate([take(scratch[4]), take(scratch[2])], axis=0)
    else:
        k = jnp.concatenate([jnp.zeros((BLOCK, GROUP_W), BF16), take(scratch[1])], axis=0)
        v = jnp.concatenate([jnp.zeros((BLOCK, GROUP_W), BF16), take(scratch[2])], axis=0)
    return q, k, v


def _lane_columns(cols):
    lane = lax.broadcasted_iota(jnp.int32, (BLOCK, LANES), 1)
    out = jnp.zeros((BLOCK, LANES), F32)
    for h, c in enumerate(cols):
        out = jnp.where(lane == h, c, out)
    return out


def band_fwd(qkv, bias, pattern, *, name):
    T = qkv.shape[0]
    d = DILATIONS[pattern]
    rows_per_block = BLOCK * d
    seq_blocks = SEQ // rows_per_block
    has_prev = seq_blocks > 1
    n_in = 5 if has_prev else 3

    def body(*refs):
        ins, b_ref, o_ref, lse_ref = refs[:n_in], refs[n_in], refs[n_in + 1], refs[n_in + 2]
        staged, o_s = refs[n_in + 3:2 * n_in + 3], refs[2 * n_in + 3]
        bd, row, key = _wide_consts()
        valid = _band_valid_wide(pl.program_id(0) % seq_blocks == 0, row, key)
        _stage(ins, staged)
        bias_w = jnp.concatenate([b_ref[h] for h in HEADS], axis=1)
        for cls in _step_classes(d):
            q, k, v = _band_operands(staged, d, cls, has_prev)
            kbd, vbd = _block_diag(k, bd), _block_diag(v, bd)
            sc = jnp.where(valid, _dot(q, kbd, _NT) + bias_w, NEG)
            ms = [jnp.max(sc[:, _seg(h)], axis=1, keepdims=True) for h in HEADS]
            p = jnp.exp(sc - _widen(ms))
            ls = _head_rowsums(p)
            _put_class(o_s, d, cls, _dot(p.astype(BF16), vbd) / _feature_widen(ls))
            lse_ref[_class_rows(d, cls), :] = _lane_columns([ms[h] + jnp.log(ls[h]) for h in HEADS])
        _flush([o_s], [o_ref], d)

    sd = jax.ShapeDtypeStruct
    return pl.pallas_call(
        body, name=name, grid=(T // rows_per_block, d // _classes_per_step(d)), in_specs=_band_in_specs(d, pattern, has_prev),
        out_specs=[pl.BlockSpec((rows_per_block, GROUP_W), lambda tb, r: (tb, 0)),
                   pl.BlockSpec((rows_per_block, LANES), lambda tb, r: (tb, 0))],
        out_shape=[sd((T, GROUP_W), F32), sd((T, LANES), F32)],
        scratch_shapes=_halves_scratch(rows_per_block, n_in + 1),
        compiler_params=_params("parallel", "arbitrary"),
    )(*([qkv] * n_in), bias)


def band_bwd(qkv, bias, lse, do, dlse, pattern, *, name):
    T = qkv.shape[0]
    d = DILATIONS[pattern]
    rows_per_block = BLOCK * d
    seq_blocks = SEQ // rows_per_block
    has_prev = seq_blocks > 1
    n_in = 5 if has_prev else 3
    n_out = 5 if has_prev else 3

    def body(*refs):
        ins, b_ref, lse_ref, do_ref, dlse_ref = refs[:n_in], refs[n_in], refs[n_in + 1], refs[n_in + 2], refs[n_in + 3]
        outs = refs[n_in + 4:n_in + 4 + n_out]
        ds_ref = refs[n_in + 4 + n_out]
        scratch = refs[n_in + 5 + n_out:]
        staged, do_s, out_s = scratch[:n_in], scratch[n_in], scratch[n_in + 1:]
        first_step = jnp.logical_and(pl.program_id(0) == 0, pl.program_id(1) == 0)
        bd, row, key = _wide_consts()
        valid = _band_valid_wide(pl.program_id(0) % seq_blocks == 0, row, key)
        _stage(list(ins) + [do_ref], list(staged) + [do_s])
        bias_w = jnp.concatenate([b_ref[h] for h in HEADS], axis=1)
        ds = None
        for cls in _step_classes(d):
            q, k, v = _band_operands(staged, d, cls, has_prev)
            kbd, vbd = _block_diag(k, bd), _block_diag(v, bd)
            rows = _class_rows(d, cls)
            do = _take_class(do_s, d, cls).astype(BF16)
            lse_t, dlse_t = lse_ref[rows, :], dlse_ref[rows, :]
            lse_w = _widen([lse_t[:, h:h + 1] for h in HEADS])
            dlse_w = _widen([dlse_t[:, h:h + 1] for h in HEADS])
            p = jnp.where(valid, jnp.exp(_dot(q, kbd, _NT) + bias_w - lse_w), 0.0)
            dp = _dot(do, vbd, _NT)
            ds_c = p * (dp - _widen(_head_rowsums(p * dp)) + dlse_w)
            dsb, pb = ds_c.astype(BF16), p.astype(BF16)
            _put_class(out_s[0], d, cls, _dot(dsb, kbd) * Q_SCALE)
            dk = _fold_heads(_dot(dsb, q, _TN), bd)
            dv = _fold_heads(_dot(pb, do, _TN), bd)
            _put_class(out_s[1], d, cls, dk[BLOCK:])
            _put_class(out_s[2], d, cls, dv[BLOCK:])
            if has_prev:
                _put_class(out_s[3], d, cls, dk[:BLOCK])
                _put_class(out_s[4], d, cls, dv[:BLOCK])
            ds = ds_c if ds is None else ds + ds_c
        _flush(out_s, outs, d)

        @pl.when(first_step)
        def _():
            for h in HEADS:
                ds_ref[h] = ds[:, _seg(h)]

        @pl.when(jnp.logical_not(first_step))
        def _():
            for h in HEADS:
                ds_ref[h] += ds[:, _seg(h)]

    big = pl.BlockSpec((rows_per_block, GROUP_W), lambda tb, r: (tb, 0))
    colb = pl.BlockSpec((rows_per_block, LANES), lambda tb, r: (tb, 0))
    sd = jax.ShapeDtypeStruct
    return pl.pallas_call(
        body, name=name, grid=(T // rows_per_block, d // _classes_per_step(d)),
        in_specs=_band_in_specs(d, pattern, has_prev) + [colb, big, colb],
        out_specs=[big] * n_out + [pl.BlockSpec((N_HEADS, BLOCK, BAND), lambda tb, r: (0, 0, 0))],
        out_shape=[sd((T, GROUP_W), F32)] * n_out + [sd((N_HEADS, BLOCK, BAND), F32)],
        scratch_shapes=_halves_scratch(rows_per_block, n_in + 1 + n_out),
        compiler_params=_params("arbitrary", "arbitrary"),
    )(*([qkv] * n_in), bias, lse, do, dlse)


def shift_add(cur, prev, d, *, name):
    rows = BLOCK * d
    nb = cur.shape[0] // rows

    def body(c_ref, p_ref, o_ref):
        keep = (pl.program_id(0) < nb - 1).astype(F32)
        o_ref[...] = c_ref[...] + keep * p_ref[...]

    blk = pl.BlockSpec((rows, GROUP_W), lambda tb: (tb, 0))
    nxt = pl.BlockSpec((rows, GROUP_W), lambda tb: (jnp.minimum(tb + 1, nb - 1), 0))
    return pl.pallas_call(
        body, name=name, grid=(nb,), in_specs=[blk, nxt], out_specs=blk,
        out_shape=jax.ShapeDtypeStruct(cur.shape, F32), compiler_params=_params("parallel"),
    )(cur, prev)


def _pattern_weights(lse_refs, h):
    ls = [r[:, h:h + 1] for r in lse_refs]
    mx = functools.reduce(jnp.maximum, ls)
    es = [jnp.exp(l - mx) for l in ls]
    tot = functools.reduce(lambda a, b: a + b, es)
    return [e / tot for e in es]


def dil_combine_fwd(outs, *, name):
    T = outs[0][0].shape[0]
    n = len(outs)
    tm = 512

    def body(*refs):
        o_refs, l_refs, out_ref = refs[:n], refs[n:2 * n], refs[2 * n]
        for h in range(N_HEADS):
            w = _pattern_weights(l_refs, h)
            acc = w[0] * o_refs[0][:, _hs(h)]
            for p in range(1, n):
                acc = acc + w[p] * o_refs[p][:, _hs(h)]
            out_ref[:, _hs(h)] = acc.astype(BF16)

    big = pl.BlockSpec((tm, GROUP_W), lambda i: (i, 0))
    colb = pl.BlockSpec((tm, LANES), lambda i: (i, 0))
    return pl.pallas_call(
        body, name=name, grid=(T // tm,), in_specs=[big] * n + [colb] * n,
        out_specs=big, out_shape=jax.ShapeDtypeStruct((T, GROUP_W), BF16),
        compiler_params=_params("parallel"),
    )(*[o for o, _ in outs], *[l for _, l in outs])


def dil_combine_bwd(outs, dmixed, *, name):
    T = outs[0][0].shape[0]
    n = len(outs)
    tm = 512

    def body(*refs):
        o_refs, l_refs, do_ref = refs[:n], refs[n:2 * n], refs[2 * n]
        do_refs, dl_refs = refs[2 * n + 1:3 * n + 1], refs[3 * n + 1:]
        for r in dl_refs:
            r[...] = jnp.zeros_like(r)
        for h in range(N_HEADS):
            w = _pattern_weights(l_refs, h)
            do = do_ref[:, _hs(h)]
            dw = [jnp.sum(do * o_refs[p][:, _hs(h)], axis=1, keepdims=True) for p in range(n)]
            mean = functools.reduce(lambda a, b: a + b, [w[p] * dw[p] for p in range(n)])
            for p in range(n):
                do_refs[p][:, _hs(h)] = w[p] * do
                dl_refs[p][:, h:h + 1] = w[p] * (dw[p] - mean)

    big = pl.BlockSpec((tm, GROUP_W), lambda i: (i, 0))
    colb = pl.BlockSpec((tm, LANES), lambda i: (i, 0))
    sd = jax.ShapeDtypeStruct
    res = pl.pallas_call(
        body, name=name, grid=(T // tm,),
        in_specs=[big] * n + [colb] * n + [pl.BlockSpec((tm, GROUP_W), lambda i: (i, 2))],
        out_specs=[big] * n + [colb] * n, out_shape=[sd((T, GROUP_W), F32)] * n + [sd((T, LANES), F32)] * n,
        compiler_params=_params("parallel"),
    )(*[o for o, _ in outs], *[l for _, l in outs], dmixed)
    return list(zip(res[:n], res[n:]))


def dilated_fwd(qkv, bias, tag):
    return [band_fwd(qkv, bias, p, name=f"{tag}_band_fwd{p}") for p in range(len(DILATIONS))]


def dilated_bwd(qkv, bias, outs, dmixed, tag):
    grads = dil_combine_bwd(outs, dmixed, name=f"{tag}_combine_bwd")
    parts, ds_all = [], []
    for p, d in enumerate(DILATIONS):
        (_, lse), (do, dlse) = outs[p], grads[p]
        res = band_bwd(qkv, bias, lse, do, dlse, p, name=f"{tag}_band_bwd{p}")
        dq, dk, dv, ds = res[0], res[1], res[2], res[-1]
        if len(res) > 4:
            dk = shift_add(dk, res[3], d, name=f"{tag}_dk{p}")
            dv = shift_add(dv, res[4], d, name=f"{tag}_dv{p}")
        parts.append([dq, dk, dv])
        ds_all.append(ds)
    return parts, jnp.concatenate(ds_all, axis=0)


def assemble_dqkv(d_sb, d_fox, d_dil, *, name):
    T = d_sb[0].shape[0]
    tr = 512
    n_pat = len(d_dil)
    flat = list(d_sb) + list(d_fox) + [a for part in d_dil for a in part]

    def body(*refs):
        o_ref = refs[-1]
        for j in range(6):
            o_ref[:, j * GROUP_W:(j + 1) * GROUP_W] = refs[j][...].astype(BF16)
        for j in range(3):
            acc = refs[6 + j][...]
            for p in range(1, n_pat):
                acc = acc + refs[6 + 3 * p + j][...]
            o_ref[:, (6 + j) * GROUP_W:(7 + j) * GROUP_W] = acc.astype(BF16)

    blk = pl.BlockSpec((tr, GROUP_W), lambda i: (i, 0))
    return pl.pallas_call(
        body, name=name, grid=(T // tr,), in_specs=[blk] * len(flat),
        out_specs=pl.BlockSpec((tr, QKV_BLOCKS * GROUP_W), lambda i: (i, 0)),
        out_shape=jax.ShapeDtypeStruct((T, QKV_BLOCKS * GROUP_W), BF16), compiler_params=_params("parallel"),
    )(*flat)


def sum_cast(arrs, dtype, *, name):
    R, C = arrs[0].shape
    tr = _largest_tile(R, 512, 16)
    n = len(arrs)

    def body(*refs):
        acc = refs[0][...].astype(F32)
        for r in refs[1:n]:
            acc = acc + r[...].astype(F32)
        refs[n][...] = acc.astype(dtype)

    blk = pl.BlockSpec((tr, C), lambda i: (i, 0))
    return pl.pallas_call(
        body, name=name, grid=(R // tr,), in_specs=[blk] * n, out_specs=blk, out_shape=jax.ShapeDtypeStruct((R, C), dtype),
        compiler_params=_params("parallel"),
    )(*arrs)


GRAD_WIRE = BF16


def _block_diag_halves(w):
    z = jnp.zeros((HEAD_DIM, HEAD_DIM), w.dtype)
    half = lambda a, b: jnp.concatenate([jnp.concatenate([a, z], axis=1), jnp.concatenate([z, b], axis=1)], axis=0)
    return jnp.stack([half(w[0], w[1]), half(w[2], w[3])]).astype(BF16)


def _diag_blocks(d):
    h = HEAD_DIM
    return jnp.stack([d[0, :h, :h], d[0, h:, h:], d[1, :h, :h], d[1, h:, h:]])


def layer_fwd(x, mem2d, W, P, bias, tag):
    s = {}
    s["x"] = x
    h1 = rmsnorm_fwd(x, P["norm_mix_g"], name=f"{tag}_norm_mix")
    qkv = matmul(h1, W["qkv"], out_dtype=BF16, name=f"{tag}_qkv")
    aux = matmul(h1, W["aux"], name=f"{tag}_aux")
    o_sb = sbw_fwd(qkv, name=f"{tag}_sb_fwd")
    cumc = fox_prep(aux, P["bf"], name=f"{tag}_fox_prep")
    cumr = col_to_row(cumc)
    o_fox, o_fox32, lse_fox = foxw_fwd(qkv, cumc, cumr, name=f"{tag}_fox_fwd")
    dil = dilated_fwd(qkv, bias, tag)
    o_dil = dil_combine_fwd(dil, name=f"{tag}_dil_combine")
    o_lru, h_lru = lru_fwd(aux, P["lru_conv_w"], P["lru_conv_b"], P["wa"], P["lru_b_a"], P["wx"], P["lru_b_x"],
                           P["lru_lambda"], name=f"{tag}_lru_fwd")
    mixed = jnp.concatenate([o_sb, o_fox, o_dil, o_lru], axis=1)
    x1 = matmul(mixed, W["out"], residual=x, name=f"{tag}_out")
    hq = rmsnorm_fwd(x1, P["norm_cross_g"], name=f"{tag}_norm_cross")
    qc = matmul(hq, W["cq"], out_dtype=BF16, name=f"{tag}_cq")
    memn = rmsnorm_fwd(mem2d, P["norm_mem_g"], name=f"{tag}_norm_mem")
    kv = matmul(memn, W["ckv"], out_dtype=BF16, name=f"{tag}_ckv")
    oc = cross_fwd(qc, kv, name=f"{tag}_cross_fwd")
    x2 = matmul(oc, W["coT"], trans_b=True, residual=x1, name=f"{tag}_co")
    h2 = rmsnorm_fwd(x2, P["norm_ffn_g"], name=f"{tag}_norm_ffn")
    if "ffn" in W:
        W.update(W.pop("ffn")(x2))
    hu = matmul(h2, W["up_u"], trans_b=True, name=f"{tag}_up_u")
    hg = matmul(h2, W["up_g"], trans_b=True, name=f"{tag}_up_g")
    act = glu_fwd(hu, hg, P["wu"], P["wg"], P["bu"], P["bg"], name=f"{tag}_glu_fwd")
    x3 = matmul(act, W["down"], residual=x2, name=f"{tag}_down")
    s.update(h1=h1, qkv=qkv, aux=aux, cumc=cumc, cumr=cumr, lse_fox=lse_fox, o_fox32=o_fox32, dil=dil, h_lru=h_lru, mixed=mixed,
             x1=x1, hq=hq, qc=qc, memn=memn, kv=kv, oc=oc, x2=x2, h2=h2, hu=hu, hg=hg, act=act)
    return x3, s


def layer_bwd(dx3, mem2d, W, P, bias, s, tag, ffn_grads_done=None):
    mm = functools.partial(matmul, out_dtype=GRAD_WIRE, trans_a=True)
    gW, gP = {}, {}
    dact = matmul(dx3, W["down"], trans_b=True, name=f"{tag}_d_act")
    gW["down"] = mm(s["act"], dx3, name=f"{tag}_g_down")
    dhu, dhg, dwu, dwg, dbu, dbg = glu_bwd(s["hu"], s["hg"], dact, P["wu"], P["wg"], P["bu"], P["bg"], name=f"{tag}_glu_bwd")
    gP["ffn_conv_w"] = jnp.concatenate([dwu, dwg], axis=1)
    gP["ffn_conv_b"] = jnp.concatenate([dbu, dbg], axis=1)
    dh2 = matmul(dhu, W["up_u"], name=f"{tag}_d_h2u")
    dh2 = matmul(dhg, W["up_g"], residual=dh2, name=f"{tag}_d_h2g")
    gW["up_u"] = mm(dhu, s["h2"], name=f"{tag}_g_up_u")
    gW["up_g"] = mm(dhg, s["h2"], name=f"{tag}_g_up_g")
    dx2, gP["norm_ffn_g"] = rmsnorm_bwd(s["x2"], P["norm_ffn_g"], dh2, dx3, name=f"{tag}_norm_ffn_bwd")
    if ffn_grads_done is not None:
        dx2 = ffn_grads_done(gW, dx2)
    doc = matmul(dx2, W["coT"], name=f"{tag}_d_oc")
    gW["coT"] = mm(dx2, s["oc"], name=f"{tag}_g_co")
    dqc, dkv = cross_bwd(s["qc"], s["kv"], doc, name=f"{tag}_cross_bwd")
    dhq = matmul(dqc, W["cq"], trans_b=True, name=f"{tag}_d_hq")
    gW["cq"] = mm(s["hq"], dqc, name=f"{tag}_g_cq")
    dmemn = matmul(dkv, W["ckv"], trans_b=True, name=f"{tag}_d_memn")
    gW["ckv"] = mm(s["memn"], dkv, name=f"{tag}_g_ckv")
    _, gP["norm_mem_g"] = rmsnorm_bwd(mem2d, P["norm_mem_g"], dmemn, None, name=f"{tag}_norm_mem_bwd")
    dx1, gP["norm_cross_g"] = rmsnorm_bwd(s["x1"], P["norm_cross_g"], dhq, dx2, name=f"{tag}_norm_cross_bwd")
    dmixed = matmul(dx1, W["out"], trans_b=True, name=f"{tag}_d_mixed")
    gW["out"] = mm(s["mixed"], dx1, name=f"{tag}_g_out")
    qkv, aux = s["qkv"], s["aux"]
    d_sb = sbw_bwd(qkv, dmixed, name=f"{tag}_sb_bwd")
    dfq, dfk, dfv, dcc, dcr = foxw_bwd(qkv, s["cumc"], s["cumr"], s["lse_fox"], s["o_fox32"], dmixed, name=f"{tag}_fox_bwd")
    dcum = sum_cast([dcc, row_to_col(dcr)], F32, name=f"{tag}_dcum")
    df, dbf = fox_prep_bwd(aux, P["bf"], dcum, name=f"{tag}_fox_prep_bwd")
    gP["b_forget"] = dbf[0, :N_HEADS]
    d_dil, ds_band = dilated_bwd(qkv, bias, s["dil"], dmixed, tag)
    dlx, dlg, dcw, dcb, dwa, dba, dwx, dbx, dlam = lru_bwd(
        aux, s["h_lru"], dmixed, P["lru_conv_w"], P["lru_conv_b"], P["wa"], P["lru_b_a"], P["wx"], P["lru_b_x"],
        P["lru_lambda"], name=f"{tag}_lru_bwd")
    gP.update(lru_conv_w=dcw, lru_conv_b=dcb, lru_w_a=_diag_blocks(dwa), lru_b_a=dba, lru_w_x=_diag_blocks(dwx),
              lru_b_x=dbx, lru_lambda=dlam)
    dqkv = assemble_dqkv(d_sb, [dfq, dfk, dfv], d_dil, name=f"{tag}_dqkv")
    daux = jnp.concatenate([dlx, dlg, df], axis=1)
    dh1 = matmul(dqkv, W["qkv"], trans_b=True, name=f"{tag}_d_h1a")
    dh1 = matmul(daux, W["aux"], trans_b=True, residual=dh1, name=f"{tag}_d_h1b")
    gW["qkv"] = mm(s["h1"], dqkv, name=f"{tag}_g_qkv")
    gW["aux"] = mm(s["h1"], daux, name=f"{tag}_g_aux")
    dx, gP["norm_mix_g"] = rmsnorm_bwd(s["x"], P["norm_mix_g"], dh1, dx1, name=f"{tag}_norm_mix_bwd")
    return dx, gW, gP, ds_band


def local_step(x, mem, target, weights_of, Ps, rel_bias, final_norm_g, grads_done=None, ffn_grads_done=None):
    B = x.shape[0]
    x2d = x.reshape(B * SEQ, D_MODEL)
    mem2d = mem.reshape(B * N_MEM, D_MODEL)
    bias = relbias_expand(rel_bias, name="relbias_expand")
    saved, Ws = [], []
    h = x2d
    for l in range(DEPTH):
        Ws.append(weights_of(l, h))
        h, s = layer_fwd(h, mem2d, Ws[l], Ps[l], bias, f"l{l}")
        saved.append(s)
    loss, dh, d_final = loss_head(h, final_norm_g, target.reshape(B * SEQ, D_MODEL), name="loss_head")
    gWs, gPs, ds_bands = [None] * DEPTH, [None] * DEPTH, []
    for l in range(DEPTH - 1, -1, -1):
        hook = None if ffn_grads_done is None else functools.partial(ffn_grads_done, l)
        dh, gWs[l], gPs[l], ds = layer_bwd(dh, mem2d, Ws[l], Ps[l], bias, saved[l], f"l{l}", hook)
        if grads_done is not None:
            dh = grads_done(l, gWs[l], dh)
        ds_bands.append(ds)
    d_rel = relbias_reduce(sum_cast([d.reshape(-1, BAND) for d in ds_bands], F32, name="ds_band_sum").reshape(-1, BLOCK, BAND),
                           name="relbias_reduce")
    return loss, dh.reshape(B, SEQ, D_MODEL), gWs, gPs, d_rel, d_final


def small_params(p, l):
    row = lambda name: p[name][l].reshape(1, -1)
    ffn_w, ffn_b = p["ffn_conv_w"][l], row("ffn_conv_b")
    return dict(
        norm_mix_g=row("norm_mix_g"), norm_cross_g=row("norm_cross_g"), norm_mem_g=row("norm_mem_g"), norm_ffn_g=row("norm_ffn_g"),
        bf=jnp.pad(row("b_forget"), ((0, 0), (0, LANES - N_HEADS))),
        lru_conv_w=p["lru_conv_w"][l], lru_conv_b=row("lru_conv_b"), wa=_block_diag_halves(p["lru_w_a"][l]), lru_b_a=row("lru_b_a"),
        wx=_block_diag_halves(p["lru_w_x"][l]), lru_b_x=row("lru_b_x"), lru_lambda=row("lru_lambda"),
        wu=ffn_w[:, :D_FF], wg=ffn_w[:, D_FF:], bu=ffn_b[:, :D_FF], bg=ffn_b[:, D_FF:])


def canonical_weights(w_in, w_out, w_cq, w_ck, w_cv, w_co, w_up, w_down):
    sb_fox, fox_f, rest = w_in[:, :6 * GROUP_W], w_in[:, 6 * GROUP_W:6 * GROUP_W + N_HEADS], w_in[:, 6 * GROUP_W + N_HEADS:]
    dil, lru = rest[:, :3 * GROUP_W], rest[:, 3 * GROUP_W:]
    pad = jnp.zeros((w_in.shape[0], AUX_W - 2 * GROUP_W - N_HEADS), w_in.dtype)
    return dict(qkv=jnp.concatenate([sb_fox, dil], axis=1), aux=jnp.concatenate([lru, fox_f, pad], axis=1), out=w_out,
                cq=w_cq, ckv=jnp.concatenate([w_ck, w_cv], axis=1), coT=w_co.T, upT=w_up.T, down=w_down)


def native_grads(g):
    qkv, aux = g["qkv"], g["aux"]
    a, b = 6 * GROUP_W, 6 * GROUP_W + N_HEADS
    w_in = jnp.zeros((qkv.shape[0], b + 5 * GROUP_W), qkv.dtype)
    w_in = w_in.at[:, :a].set(qkv[:, :a]).at[:, a:b].set(aux[:, 2 * GROUP_W:2 * GROUP_W + N_HEADS])
    w_in = w_in.at[:, b:b + 3 * GROUP_W].set(qkv[:, a:]).at[:, b + 3 * GROUP_W:].set(aux[:, :2 * GROUP_W])
    return (w_in, g["out"], g["cq"], g["ckv"][:, :GROUP_W], g["ckv"][:, GROUP_W:], g["coT"].T) + native_ffn_grads(g)


def native_ffn_grads(g):
    return (g["upT"].T, g["down"])


ANY = pl.BlockSpec(memory_space=pl.ANY)
VMEM_SPEC = pl.BlockSpec(memory_space=pltpu.VMEM)


def _place():
    x, y, c = lax.axis_index("x"), lax.axis_index("y"), lax.axis_index("c")
    other_chips = [(1 - x, y), (x, 1 - y), (1 - x, 1 - y)]
    return x, y, c, other_chips


def _gather_body(x_ref, out_ref, send_sems, recv_sems, local_sem):
    x, y, c, chips = _place()
    me, sibling = (x, y, c), (x, y, 1 - c)

    def slot(px, py, pc):
        return out_ref.at[4 * px + 2 * py + pc]

    def copy(k, block, to, src=None):
        return pltpu.make_async_remote_copy(
            src_ref=slot(*block) if src is None else src, dst_ref=slot(*block),
            send_sem=send_sems.at[k], recv_sem=recv_sems.at[k], device_id=to, device_id_type=MESH)

    if local_sem is not None:
        mine = pltpu.make_async_copy(x_ref, slot(*me), local_sem)
        mine.start()
    first = [copy(0, me, sibling, src=x_ref)]
    first += [copy(1 + j, me, (*chip, c), src=x_ref) for j, chip in enumerate(chips)]
    for cp in first:
        cp.start()
    passed = [copy(4 + j, (*chip, c), sibling) for j, chip in enumerate(chips)]
    for j, chip in enumerate(chips):
        copy(1 + j, (*chip, c), me).wait_recv()
        passed[j].start()
    copy(0, sibling, me).wait_recv()
    for j, chip in enumerate(chips):
        copy(4 + j, (*chip, 1 - c), me).wait_recv()
    for cp in first + passed:
        cp.wait_send()
    if local_sem is not None:
        mine.wait()


_GATHER_SEMS = [pltpu.SemaphoreType.DMA((7,)), pltpu.SemaphoreType.DMA((7,)), pltpu.SemaphoreType.DMA]


def allgather_hbm(shard, me, *, name):
    def body(x_ref, out_ref, done_ref, send_sems, recv_sems):
        _gather_body(x_ref, out_ref, send_sems, recv_sems, None)
        done_ref[...] = jnp.zeros_like(done_ref)

    others, done = pl.pallas_call(
        body, name=name, in_specs=[ANY], out_specs=[ANY, VMEM_SPEC],
        out_shape=[jax.ShapeDtypeStruct((N_DEV,) + shard.shape, shard.dtype), jax.ShapeDtypeStruct((8, LANES), F32)],
        scratch_shapes=_GATHER_SEMS[:2],
    )(shard)
    return lax.dynamic_update_slice(others, shard[None], (me, 0, 0)), done


def allgather_small(x, *, name, reduce=False):
    def body(x_ref, out_ref, second_ref, *sems):
        _gather_body(x_ref, out_ref, *sems)
        if reduce:
            acc = out_ref[0]
            for d in range(1, N_DEV):
                acc = acc + out_ref[d]
            second_ref[...] = acc
        else:
            second_ref[...] = jnp.zeros_like(second_ref)

    sd = jax.ShapeDtypeStruct
    return pl.pallas_call(
        body, name=name, in_specs=[VMEM_SPEC], out_specs=[VMEM_SPEC, VMEM_SPEC],
        out_shape=[sd((N_DEV,) + x.shape, x.dtype), sd(x.shape if reduce else (8, LANES), x.dtype)],
        scratch_shapes=_GATHER_SEMS, compiler_params=pltpu.CompilerParams(vmem_limit_bytes=VMEM_LIMIT_V7X),
    )(x)


N_CHIPS = 4


def pair_exchange(g, *, name):
    _, R, C = g.shape

    def body(g_ref, recv_ref, send_sems, recv_sems):
        x, y, c, _ = _place()
        sibling = (x, y, 1 - c)
        remote = [pltpu.make_async_remote_copy(
            src_ref=g_ref.at[2 * q + (1 - c)], dst_ref=recv_ref.at[q], send_sem=send_sems.at[q], recv_sem=recv_sems.at[q],
            device_id=sibling, device_id_type=MESH) for q in range(N_CHIPS)]
        for cp in remote:
            cp.start()
        for cp in remote:
            cp.wait_recv()
        for cp in remote:
            cp.wait_send()

    return pl.pallas_call(
        body, name=name, in_specs=[ANY], out_specs=ANY, out_shape=jax.ShapeDtypeStruct((N_CHIPS, R, C), g.dtype),
        scratch_shapes=[pltpu.SemaphoreType.DMA((N_CHIPS,))] * 2,
    )(g)


def chip_exchange(s, *, name):
    _, R, C = s.shape

    def body(s_ref, o0, o1, o2, send_sems, recv_sems):
        x, y, c, chips = _place()
        outs = (o0, o1, o2)
        copies = [pltpu.make_async_remote_copy(
            src_ref=s_ref.at[2 * cx + cy], dst_ref=outs[j], send_sem=send_sems.at[j], recv_sem=recv_sems.at[j],
            device_id=(cx, cy, c), device_id_type=MESH) for j, (cx, cy) in enumerate(chips)]
        for cp in copies:
            cp.start()
        for cp in copies:
            cp.wait_recv()
        for cp in copies:
            cp.wait_send()

    sd = jax.ShapeDtypeStruct((R, C), s.dtype)
    return pl.pallas_call(
        body, name=name, in_specs=[ANY], out_specs=[ANY] * 3, out_shape=[sd] * 3,
        scratch_shapes=[pltpu.SemaphoreType.DMA((3,)), pltpu.SemaphoreType.DMA((3,))],
    )(s)


HBM_SPEC = pl.BlockSpec(memory_space=pltpu.HBM)
SEM_SPEC = pl.BlockSpec(memory_space=pltpu.SEMAPHORE)
N_PEERS = N_DEV - 1


def _peers():
    x, y, c = lax.axis_index("x"), lax.axis_index("y"), lax.axis_index("c")
    flip = lambda v, bit: 1 - v if bit else v
    out = []
    for k in range(1, N_DEV):
        px, py, pc = flip(x, (k >> 2) & 1), flip(y, (k >> 1) & 1), flip(c, k & 1)
        out.append(((px, py, pc), 4 * px + 2 * py + pc))
    return out, 4 * x + 2 * y + c


def _peer_copies(src_ref, land_ref, send_sems, recv_sems, scatter, landing):
    peers, me = _peers()
    return [pltpu.make_async_remote_copy(
        src_ref=src_ref.at[idx] if scatter else src_ref, dst_ref=land_ref.at[me if landing == "mine" else idx],
        send_sem=send_sems.at[k], recv_sem=recv_sems.at[k], device_id=peer, device_id_type=MESH)
        for k, (peer, idx) in enumerate(peers)]


def exchange_start(src, scatter, *, name):
    shape = (N_DEV,) + src.shape[-2:]

    def body(src_ref, land_ref, send_sems, recv_sems, src_thru, land_thru, token):
        for cp in _peer_copies(src_ref, land_ref, send_sems, recv_sems, scatter, "mine"):
            cp.start()
        token[...] = jnp.zeros_like(token)

    sems = pltpu.SemaphoreType.DMA((N_PEERS,))
    return pl.pallas_call(
        body, name=name,
        out_shape=(sems, sems, pltpu.HBM(src.shape, src.dtype), pltpu.HBM(shape, src.dtype), jax.ShapeDtypeStruct((8, LANES), F32)),
        in_specs=(HBM_SPEC, HBM_SPEC), out_specs=(SEM_SPEC, SEM_SPEC, HBM_SPEC, HBM_SPEC, VMEM_SPEC),
        input_output_aliases={0: 2, 1: 3},
        compiler_params=pltpu.CompilerParams(has_side_effects=pltpu.SideEffectType.DATAFLOW_SIDE_EFFECTING),
    )(pltpu.with_memory_space_constraint(src, pltpu.HBM), pltpu.with_memory_space_constraint(lax.empty(shape, src.dtype), pltpu.HBM))


def exchange_wait(started, after, scatter, *, name):
    send_sems, recv_sems, src_thru, land_thru, _ = started

    def body(src_ref, land_ref, send_sems, recv_sems, after_ref, src_dead, got_ref):
        for cp in _peer_copies(src_ref, land_ref, send_sems, recv_sems, scatter, "theirs"):
            cp.wait_send()
            cp.wait_recv()

    return pl.pallas_call(
        body, name=name, out_shape=(pltpu.HBM(src_thru.shape, src_thru.dtype), pltpu.HBM(land_thru.shape, land_thru.dtype)),
        in_specs=(HBM_SPEC, HBM_SPEC, SEM_SPEC, SEM_SPEC, ANY), out_specs=(HBM_SPEC, HBM_SPEC), input_output_aliases={0: 0, 1: 1},
        compiler_params=pltpu.CompilerParams(has_side_effects=pltpu.SideEffectType.DATAFLOW_SIDE_EFFECTING),
    )(src_thru, land_thru, send_sems, recv_sems, after)[1]


def sum_blocks(blocks, *, name):
    n, R, C = blocks.shape
    tr = _largest_tile(R, 512, 16)

    def body(b_ref, o_ref):
        d = pl.program_id(1)
        v = b_ref[...].astype(F32)

        @pl.when(d == 0)
        def _():
            o_ref[...] = v

        @pl.when(d > 0)
        def _():
            o_ref[...] += v

    return pl.pallas_call(
        body, name=name, grid=(R // tr, n),
        in_specs=[pl.BlockSpec((None, tr, C), lambda i, d: (d, i, 0))], out_specs=pl.BlockSpec((tr, C), lambda i, d: (i, 0)),
        out_shape=jax.ShapeDtypeStruct((R, C), F32), compiler_params=_params("parallel", "arbitrary"),
    )(blocks)


WEIGHTS = ("norm_mix_g", "w_in", "b_forget", "lru_conv_w", "lru_conv_b", "lru_w_a", "lru_b_a", "lru_w_x", "lru_b_x", "lru_lambda",
           "w_out", "norm_cross_g", "norm_mem_g", "w_cq", "w_ck", "w_cv", "w_co", "norm_ffn_g", "w_up", "ffn_conv_w", "ffn_conv_b",
           "w_down", "rel_bias", "final_norm_g")
LARGE = ("w_in", "w_out", "w_cq", "w_ck", "w_cv", "w_co", "w_up", "w_down")
COLUMN_SPLIT_SMALL = ("lru_conv_w", "ffn_conv_w")
PACK = (("qkv", 128, 2304), ("aux", 128, 640), ("out", 128, 1024), ("cq", 128, 256), ("ckv", 128, 512), ("coT", 128, 256),
        ("upT", 704, 1024), ("down", 352, 1024))
PACK_W = 1024


def _pack_rows(parts):
    return jnp.concatenate([p.reshape(-1, PACK_W) for p in parts], axis=0)


def _pad_rows(flat, mult=8 * LANES):
    n = flat.shape[0]
    return jnp.pad(flat, (0, (-n) % mult)).reshape(-1, LANES)


def kernel(x, mem, norm_mix_g, w_in, b_forget, lru_conv_w, lru_conv_b, lru_w_a, lru_b_a, lru_w_x, lru_b_x, lru_lambda, w_out, norm_cross_g, norm_mem_g, w_cq, w_ck, w_cv, w_co, norm_ffn_g, w_up, ffn_conv_w, ffn_conv_b, w_down, rel_bias, final_norm_g, loss_target, m_norm_mix_g, m_w_in, m_b_forget, m_lru_conv_w, m_lru_conv_b, m_lru_w_a, m_lru_b_a, m_lru_w_x, m_lru_b_x, m_lru_lambda, m_w_out, m_norm_cross_g, m_norm_mem_g, m_w_cq, m_w_ck, m_w_cv, m_w_co, m_norm_ffn_g, m_w_up, m_ffn_conv_w, m_ffn_conv_b, m_w_down, m_rel_bias, m_final_norm_g, v_norm_mix_g, v_w_in, v_b_forget, v_lru_conv_w, v_lru_conv_b, v_lru_w_a, v_lru_b_a, v_lru_w_x, v_lru_b_x, v_lru_lambda, v_w_out, v_norm_cross_g, v_norm_mem_g, v_w_cq, v_w_ck, v_w_cv, v_w_co, v_norm_ffn_g, v_w_up, v_ffn_conv_w, v_ffn_conv_b, v_w_down, v_rel_bias, v_final_norm_g):
    w = dict(norm_mix_g=norm_mix_g, w_in=w_in, b_forget=b_forget, lru_conv_w=lru_conv_w, lru_conv_b=lru_conv_b, lru_w_a=lru_w_a,
             lru_b_a=lru_b_a, lru_w_x=lru_w_x, lru_b_x=lru_b_x, lru_lambda=lru_lambda, w_out=w_out, norm_cross_g=norm_cross_g,
             norm_mem_g=norm_mem_g, w_cq=w_cq, w_ck=w_ck, w_cv=w_cv, w_co=w_co, norm_ffn_g=norm_ffn_g, w_up=w_up,
             ffn_conv_w=ffn_conv_w, ffn_conv_b=ffn_conv_b, w_down=w_down, rel_bias=rel_bias, final_norm_g=final_norm_g)
    m = dict(norm_mix_g=m_norm_mix_g, w_in=m_w_in, b_forget=m_b_forget, lru_conv_w=m_lru_conv_w, lru_conv_b=m_lru_conv_b,
             lru_w_a=m_lru_w_a, lru_b_a=m_lru_b_a, lru_w_x=m_lru_w_x, lru_b_x=m_lru_b_x, lru_lambda=m_lru_lambda, w_out=m_w_out,
             norm_cross_g=m_norm_cross_g, norm_mem_g=m_norm_mem_g, w_cq=m_w_cq, w_ck=m_w_ck, w_cv=m_w_cv, w_co=m_w_co,
             norm_ffn_g=m_norm_ffn_g, w_up=m_w_up, ffn_conv_w=m_ffn_conv_w, ffn_conv_b=m_ffn_conv_b, w_down=m_w_down,
             rel_bias=m_rel_bias, final_norm_g=m_final_norm_g)
    v = dict(norm_mix_g=v_norm_mix_g, w_in=v_w_in, b_forget=v_b_forget, lru_conv_w=v_lru_conv_w, lru_conv_b=v_lru_conv_b,
             lru_w_a=v_lru_w_a, lru_b_a=v_lru_b_a, lru_w_x=v_lru_w_x, lru_b_x=v_lru_b_x, lru_lambda=v_lru_lambda, w_out=v_w_out,
             norm_cross_g=v_norm_cross_g, norm_mem_g=v_norm_mem_g, w_cq=v_w_cq, w_ck=v_w_ck, w_cv=v_w_cv, w_co=v_w_co,
             norm_ffn_g=v_norm_ffn_g, w_up=v_w_up, ffn_conv_w=v_ffn_conv_w, ffn_conv_b=v_ffn_conv_b, w_down=v_w_down,
             rel_bias=v_rel_bias, final_norm_g=v_final_norm_g)
    me = 4 * lax.axis_index("x") + 2 * lax.axis_index("y") + lax.axis_index("c")

    conv_shard = jnp.concatenate([w[n].reshape(-1) for n in COLUMN_SPLIT_SMALL])
    conv_all, conv_gathered = allgather_small(_pad_rows(conv_shard), name="gather_conv")
    conv_all = conv_all.reshape(N_DEV, -1)
    full = dict(w)
    off = 0
    for n in COLUMN_SPLIT_SMALL:
        d, k, c = w[n].shape
        blocks = conv_all[:, off:off + d * k * c].reshape(N_DEV, d, k, c)
        full[n] = blocks.transpose(1, 2, 0, 3).reshape(d, k, N_DEV * c)
        off += d * k * c

    MIX, FFN = PACK[:6], PACK[6:]

    def packed_shard(l, group):
        canon = canonical_weights(*[w[n][l] for n in LARGE])
        return _pack_rows([canon[k].astype(BF16) for k, _, _ in group])

    def unpack_weights(packed, group):
        W, row = {}, 0
        for k, r, c in group:
            n_rows = r * c // PACK_W
            W[k] = packed[:, row:row + n_rows].reshape(N_DEV * r, c)
            row += n_rows
        if "upT" in W:
            upT = W.pop("upT")
            W["up_u"], W["up_g"] = upT[:D_FF], upT[D_FF:]
        return W

    def packed_grads(gW, group):
        g = dict(gW)
        if "up_u" in g:
            g["upT"] = jnp.concatenate([g.pop("up_u"), g.pop("up_g")], axis=0)
        return jnp.concatenate([g[k].reshape(N_DEV, r * c // PACK_W, PACK_W) for k, r, c in group], axis=1)

    def unpack_grads(shard_sum, group):
        g, row = {}, 0
        for k, r, c in group:
            n_rows = r * c // PACK_W
            g[k] = shard_sum[row:row + n_rows].reshape(r, c)
            row += n_rows
        return g

    def own_block_in(landed, block):
        return lax.dynamic_update_slice(landed, block[None], (me, 0, 0))

    def gathered_weights(copies, shard, after, group, name):
        return unpack_weights(own_block_in(exchange_wait(copies, after, False, name=name), shard), group)

    def scattered_sum(src, copies, after, tag):
        landed = exchange_wait(copies, after, True, name=f"{tag}_wait")
        mine = lax.dynamic_index_in_dim(src, me, axis=0, keepdims=False)
        return sum_blocks(own_block_in(landed, mine), name=f"{tag}_sum")

    last = DEPTH - 1
    mix0, gathered = allgather_hbm(packed_shard(0, MIX) + conv_gathered[0, 0].astype(BF16), me, name="gather_weights")
    ffn0_shard = packed_shard(0, FFN) + gathered[0, 0].astype(BF16)
    gather_ffn0 = exchange_start(ffn0_shard, False, name="gather_ffn0_start")
    last_shard = packed_shard(last, PACK) + gather_ffn0[4][0, 0].astype(BF16)
    gather_last = exchange_start(last_shard, False, name="gather_last_start")
    started = gather_last[4][0, 0]

    def weights_of(l, h):
        if l == 0:
            W = unpack_weights(mix0, MIX)
            W["ffn"] = lambda after: gathered_weights(gather_ffn0, ffn0_shard, after, FFN, "gather_ffn0_wait")
            return W
        assert l == last
        return gathered_weights(gather_last, last_shard, h, PACK, "gather_last_wait")

    in_flight = {}

    def scatter(key, g_all, dx, name):
        in_flight[key] = (g_all, exchange_start(g_all, True, name=name))
        return dx + in_flight[key][1][4][0, 0]

    def grads_done(l, gW, dh):
        return scatter("last", packed_grads(gW, PACK), dh, "grads_last_start") if l == last else dh

    def ffn_grads_done(l, gW, dx):
        if l != 0:
            return dx
        return scatter("ffn0", packed_grads({k: gW[k] for k in ("up_u", "up_g", "down")}, FFN), dx, "grads_ffn0_start")

    Ps = [small_params(full, l) for l in range(DEPTH)]
    Ps[0]["norm_mix_g"] = Ps[0]["norm_mix_g"] + started
    loss, grad_x, gWs, gPs, d_rel, d_final = local_step(x, mem, loss_target, weights_of, Ps, rel_bias,
                                                        final_norm_g.reshape(1, -1), grads_done, ffn_grads_done)

    shard_grads = {last: unpack_grads(scattered_sum(*in_flight["last"], grad_x, "grads_last"), PACK)}
    shard_grads[0] = unpack_grads(scattered_sum(*in_flight["ffn0"], grad_x, "grads_ffn0"), FFN)

    g_all = packed_grads({k: gWs[0][k] for k, _, _ in MIX}, MIX)
    rows = g_all.shape[1]
    got = pair_exchange(g_all, name="grads_pair_exchange")
    own = lax.dynamic_index_in_dim(g_all.reshape(N_CHIPS, 2, rows, PACK_W), lax.axis_index("c"), axis=1, keepdims=False)
    pair = sum_cast([own.reshape(-1, PACK_W), got.reshape(-1, PACK_W)], GRAD_WIRE, name="grads_pair_sum").reshape(N_CHIPS, rows, PACK_W)
    from_x, from_y, from_xy = chip_exchange(pair, name="grads_chip_exchange")
    mine = lax.dynamic_index_in_dim(pair, 2 * lax.axis_index("x") + lax.axis_index("y"), axis=0, keepdims=False)
    shard_grads[0].update(unpack_grads(sum_cast([mine, from_x, from_y, from_xy], F32, name="grads_chip_sum"), MIX))

    grads = {}
    per_layer = [native_grads(shard_grads[l]) for l in range(DEPTH)]
    for i, n in enumerate(LARGE):
        grads[n] = jnp.stack([per_layer[l][i] for l in range(DEPTH)])

    small_names = [n for n in WEIGHTS if n not in LARGE and n not in ("rel_bias", "final_norm_g")]
    pieces = [gPs[l][n].reshape(-1) for n in small_names for l in range(DEPTH)] + [d_rel.reshape(-1), d_final.reshape(-1), loss[0, :1]]
    sizes = [p.shape[0] for p in pieces]
    _, total = allgather_small(_pad_rows(jnp.concatenate(pieces)), name="allreduce_small", reduce=True)
    total = total.reshape(-1)
    off, it = 0, iter(sizes)
    for n in small_names:
        per = []
        for l in range(DEPTH):
            sz = next(it)
            per.append(total[off:off + sz])
            off += sz
        full_shape = (DEPTH,) + full[n].shape[1:]
        gfull = jnp.stack(per).reshape(full_shape)
        if n in COLUMN_SPLIT_SMALL:
            c = w[n].shape[-1]
            gfull = lax.dynamic_slice_in_dim(gfull, me * c, c, axis=gfull.ndim - 1)
        grads[n] = gfull
    grads["rel_bias"] = total[off:off + rel_bias.size].reshape(rel_bias.shape)
    off += rel_bias.size
    grads["final_norm_g"] = total[off:off + D_MODEL]
    off += D_MODEL
    loss_out = total[off]

    delta, new_m, new_v = {}, {}, {}
    for n in LARGE:
        shape = w[n].shape
        two_d = lambda a: a.reshape(-1, shape[-1])
        d_, m_, v_ = adamw(two_d(w[n]), two_d(grads[n]), two_d(m[n]), two_d(v[n]), name=f"adamw_{n}")
        delta[n], new_m[n], new_v[n] = d_.reshape(shape), m_.reshape(shape), v_.reshape(shape)
    small_all = [n for n in WEIGHTS if n not in LARGE]
    two_d = lambda a: a.reshape(-1, a.shape[-1])
    d_, m_, v_ = adamw_many(*[[two_d(src[n]) for n in small_all] for src in (w, grads, m, v)], name="adamw_small")
    for i, n in enumerate(small_all):
        delta[n], new_m[n], new_v[n] = (a[i].reshape(w[n].shape) for a in (d_, m_, v_))

    return (loss_out, grad_x, *[grads[n] for n in WEIGHTS], *[delta[n] for n in WEIGHTS], *[new_m[n] for n in WEIGHTS],
            *[new_v[n] for n in WEIGHTS])
```

```python
import functools
import math

import numpy as np
import jax
import jax.numpy as jnp
from jax import lax
from jax.experimental import pallas as pl
from jax.experimental.pallas import tpu as pltpu

F32 = jnp.float32
BF16 = jnp.bfloat16
MESH = pl.DeviceIdType.MESH

N_DEV = 8
D_MODEL = 1024
SEQ = 2048
DEPTH = 2
HEAD_DIM = 64
N_HEADS = 4
GROUP_W = N_HEADS * HEAD_DIM
D_FF = 2816
N_MEM = 256
NUM_BUCKETS = 32
MAX_DISTANCE = 2048
BLOCK = 128
DILATIONS = (1, 4, 16)
EPS = 1e-6
LRU_C = 8.0
Q_SCALE = HEAD_DIM ** -0.5
AUX_W = 640
LRU_HALF_W = 128
LRU_HALVES = GROUP_W // LRU_HALF_W
ADAM_LR, ADAM_B1, ADAM_B2, ADAM_EPS, ADAM_WD, ADAM_STEP = 0.001, 0.9, 0.999, 1e-08, 0.01, 10

VMEM_LIMIT_V7X = 48 * 1024 * 1024


def _params(*sem):
    return pltpu.CompilerParams(dimension_semantics=sem if sem else None, vmem_limit_bytes=VMEM_LIMIT_V7X)


def _pick(n, cands):
    for c in cands:
        if n % c == 0:
            return c
    return n


def _largest_tile(n, cap, align):
    best = None
    for t in range(align, min(n, cap) + 1, align):
        if n % t == 0:
            best = t
    return n if best is None else best


def matmul(a, b, *, name, trans_a=False, trans_b=False, out_dtype=F32, residual=None):
    (K, M) = a.shape if trans_a else a.shape[::-1]
    (N, Kb) = b.shape if trans_b else b.shape[::-1]
    assert K == Kb, (a.shape, b.shape)
    tm = _largest_tile(M, 1408 if trans_a else (1024 if K <= 1024 else 512), 128)
    tn = _largest_tile(N, 1408, 128)
    tk = _largest_tile(K, 1024 if trans_a else 2816, 128)
    nk = K // tk
    a_spec = pl.BlockSpec((tk, tm), lambda i, j, k: (k, i)) if trans_a else pl.BlockSpec((tm, tk), lambda i, j, k: (i, k))
    b_spec = pl.BlockSpec((tn, tk), lambda i, j, k: (j, k)) if trans_b else pl.BlockSpec((tk, tn), lambda i, j, k: (k, j))
    o_spec = pl.BlockSpec((tm, tn), lambda i, j, k: (i, j))
    dims = (((0 if trans_a else 1,), (1 if trans_b else 0,)), ((), ()))
    has_res = residual is not None

    def body(*refs):
        a_ref, b_ref = refs[0], refs[1]
        r_ref = refs[2] if has_res else None
        part = lax.dot_general(a_ref[...].astype(BF16), b_ref[...].astype(BF16), dims, preferred_element_type=F32)
        if nk == 1:
            if has_res:
                part = part + r_ref[...].astype(F32)
            refs[-1][...] = part.astype(out_dtype)
            return
        o_ref, acc_ref = refs[-2], refs[-1]
        k = pl.program_id(2)

        @pl.when(k == 0)
        def _():
            acc_ref[...] = part

        @pl.when(k > 0)
        def _():
            acc_ref[...] += part

        @pl.when(k == nk - 1)
        def _():
            r = acc_ref[...]
            if has_res:
                r = r + r_ref[...].astype(F32)
            o_ref[...] = r.astype(out_dtype)

    ops = (a, b) + ((residual,) if has_res else ())
    return pl.pallas_call(
        body, name=name, grid=(M // tm, N // tn, nk),
        in_specs=[a_spec, b_spec] + ([o_spec] if has_res else []),
        out_specs=o_spec, out_shape=jax.ShapeDtypeStruct((M, N), out_dtype),
        scratch_shapes=[pltpu.VMEM((tm, tn), F32)] if nk > 1 else [],
        compiler_params=_params("parallel", "parallel", "arbitrary"),
    )(*ops)


def rmsnorm_fwd(x, g, *, name):
    R, D = x.shape
    tr = _pick(R, (512, 256))

    def body(x_ref, g_ref, o_ref):
        xv = x_ref[...]
        r = lax.rsqrt(jnp.mean(xv * xv, axis=-1, keepdims=True) + EPS)
        o_ref[...] = (xv * r * g_ref[...]).astype(BF16)

    return pl.pallas_call(
        body, name=name, grid=(R // tr,),
        in_specs=[pl.BlockSpec((tr, D), lambda i: (i, 0)), pl.BlockSpec((1, D), lambda i: (0, 0))],
        out_specs=pl.BlockSpec((tr, D), lambda i: (i, 0)), out_shape=jax.ShapeDtypeStruct((R, D), BF16),
        compiler_params=_params("parallel"),
    )(x, g)


def rmsnorm_bwd(x, g, dh, dres, *, name):
    R, D = x.shape
    tr = _pick(R, (512, 256))
    has_res = dres is not None

    def body(*refs):
        x_ref, g_ref, dh_ref = refs[:3]
        dx_ref, dg_ref = refs[-2], refs[-1]
        xv = x_ref[...]
        r = lax.rsqrt(jnp.mean(xv * xv, axis=-1, keepdims=True) + EPS)
        n = xv * r
        dhv = dh_ref[...]
        dn = dhv * g_ref[...]
        dx = r * (dn - n * jnp.mean(dn * n, axis=-1, keepdims=True))
        if has_res:
            dx = dx + refs[3][...]
        dx_ref[...] = dx
        part = jnp.sum(dhv * n, axis=0, keepdims=True)

        @pl.when(pl.program_id(0) == 0)
        def _():
            dg_ref[...] = part

        @pl.when(pl.program_id(0) > 0)
        def _():
            dg_ref[...] += part

    row = pl.BlockSpec((tr, D), lambda i: (i, 0))
    vec = pl.BlockSpec((1, D), lambda i: (0, 0))
    ops = (x, g, dh) + ((dres,) if has_res else ())
    return pl.pallas_call(
        body, name=name, grid=(R // tr,),
        in_specs=[row, vec, row] + ([row] if has_res else []),
        out_specs=[row, vec],
        out_shape=[jax.ShapeDtypeStruct((R, D), F32), jax.ShapeDtypeStruct((1, D), F32)],
        compiler_params=_params("arbitrary"),
    )(*ops)


_SQRT_HALF = 0.7071067811865476
_INV_SQRT_2PI = 0.3989422804014327


def _normal_cdf_pdf(x):
    ax = jnp.abs(x) * _SQRT_HALF
    t = 1.0 / (1.0 + 0.3275911 * ax)
    poly = t * (0.254829592 + t * (-0.284496736 + t * (1.421413741 + t * (-1.453152027 + t * 1.061405429))))
    e = jnp.exp(-0.5 * x * x)
    half_tail = 0.5 * poly * e
    return jnp.where(x < 0, half_tail, 1.0 - half_tail), e


def _gelu_cdf(x):
    return _normal_cdf_pdf(x)[0]


def _gelu_and_grad(x):
    cdf, e = _normal_cdf_pdf(x)
    return x * cdf, cdf + x * _INV_SQRT_2PI * e


def _shift_down(main, halo, first, shifts):
    halo = jnp.where(first, 0.0, halo)
    ext = jnp.concatenate([halo, main], axis=0)
    return [pltpu.roll(ext, s, 0)[8:] for s in shifts]


def _conv3(main, halo, first, w, b):
    m1, m2 = _shift_down(main, halo, first, (1, 2))
    return ((b + w[0:1] * m2) + w[1:2] * m1) + w[2:3] * main, m1, m2


def glu_fwd(hu, hg, wu, wg, bu, bg, *, name):
    T, F = hu.shape
    tm, tf = 512, _largest_tile(F, 704, 128)
    hb = tm // 8
    blocks_per_example = SEQ // tm

    def body(hu_ref, hg_ref, hau_ref, hag_ref, wu_ref, wg_ref, bu_ref, bg_ref, o_ref):
        first = pl.program_id(0) % blocks_per_example == 0
        up, _, _ = _conv3(hu_ref[...], hau_ref[...], first, wu_ref[...], bu_ref[...])
        gate, _, _ = _conv3(hg_ref[...], hag_ref[...], first, wg_ref[...], bg_ref[...])
        o_ref[...] = (gate * _gelu_cdf(gate) * up).astype(BF16)

    main = pl.BlockSpec((tm, tf), lambda i, j: (i, j))
    halo = pl.BlockSpec((8, tf), lambda i, j: (jnp.maximum(i * hb - 1, 0), j))
    w3 = pl.BlockSpec((3, tf), lambda i, j: (0, j))
    b1 = pl.BlockSpec((1, tf), lambda i, j: (0, j))
    return pl.pallas_call(
        body, name=name, grid=(T // tm, F // tf),
        in_specs=[main, main, halo, halo, w3, w3, b1, b1],
        out_specs=main, out_shape=jax.ShapeDtypeStruct((T, F), BF16),
        compiler_params=_params("parallel", "parallel"),
    )(hu, hg, hu, hg, wu, wg, bu, bg)


def glu_bwd(hu, hg, dact, wu, wg, bu, bg, *, name):
    T, F = hu.shape
    tm, tf = 512, _largest_tile(F, 704, 128)
    hb = tm // 8
    blocks_per_example = SEQ // tm
    n_halo_blocks = T // 8
    n_ext = tm + 8

    def body(hu_ref, hg_ref, hau_ref, hag_ref, hnu_ref, hng_ref, da_ref, dan_ref, wu_ref, wg_ref, bu_ref, bg_ref,
             du_ref, dg_ref, dwu_ref, dwg_ref, dbu_ref, dbg_ref):
        i = pl.program_id(1)
        first = i % blocks_per_example == 0
        last = i % blocks_per_example == blocks_per_example - 1
        wu, wg = wu_ref[...], wg_ref[...]

        def conv_ext(main_ref, prev_ref, next_ref, w, b):
            ext = jnp.concatenate([jnp.where(first, 0.0, prev_ref[...]), main_ref[...], next_ref[...]], axis=0)
            x0, x1, x2 = ext[8:], pltpu.roll(ext, 1, 0)[8:], pltpu.roll(ext, 2, 0)[8:]
            return ((b + w[0:1] * x2) + w[1:2] * x1) + w[2:3] * x0, x0, x1, x2

        up, xu, u1, u2 = conv_ext(hu_ref, hau_ref, hnu_ref, wu, bu_ref[...])
        gate, xg, g1, g2 = conv_ext(hg_ref, hag_ref, hng_ref, wg, bg_ref[...])
        act, dact_dgate = _gelu_and_grad(gate)
        da = jnp.concatenate([da_ref[...], jnp.where(last, 0.0, dan_ref[...])], axis=0)
        dup = da * act
        dgate = da * up * dact_dgate

        def conv_t(d, w):
            return (w[2:3] * d[:tm] + w[1:2] * pltpu.roll(d, n_ext - 1, 0)[:tm] + w[0:1] * pltpu.roll(d, n_ext - 2, 0)[:tm]).astype(BF16)

        du_ref[...] = conv_t(dup, wu)
        dg_ref[...] = conv_t(dgate, wg)

        def sums(d, x0, x1, x2):
            s = lambda v: jnp.sum(v[:tm], axis=0, keepdims=True)
            return jnp.concatenate([s(d * x2), s(d * x1), s(d * x0)], axis=0), s(d)

        pwu, pbu = sums(dup, xu, u1, u2)
        pwg, pbg = sums(dgate, xg, g1, g2)

        @pl.when(i == 0)
        def _():
            dwu_ref[...] = pwu
            dwg_ref[...] = pwg
            dbu_ref[...] = pbu
            dbg_ref[...] = pbg

        @pl.when(i > 0)
        def _():
            dwu_ref[...] += pwu
            dwg_ref[...] += pwg
            dbu_ref[...] += pbu
            dbg_ref[...] += pbg

    main = pl.BlockSpec((tm, tf), lambda j, i: (i, j))
    before = pl.BlockSpec((8, tf), lambda j, i: (jnp.maximum(i * hb - 1, 0), j))
    after = pl.BlockSpec((8, tf), lambda j, i: (jnp.minimum((i + 1) * hb, n_halo_blocks - 1), j))
    w3 = pl.BlockSpec((3, tf), lambda j, i: (0, j))
    b1 = pl.BlockSpec((1, tf), lambda j, i: (0, j))
    sd = jax.ShapeDtypeStruct
    return pl.pallas_call(
        body, name=name, grid=(F // tf, T // tm),
        in_specs=[main, main, before, before, after, after, main, after, w3, w3, b1, b1],
        out_specs=[main, main, w3, w3, b1, b1],
        out_shape=[sd((T, F), BF16), sd((T, F), BF16), sd((3, F), F32), sd((3, F), F32), sd((1, F), F32), sd((1, F), F32)],
        compiler_params=_params("parallel", "arbitrary"),
    )(hu, hg, hu, hg, hu, hg, dact, dact, wu, wg, bu, bg)


def loss_head(x, g, target, *, name):
    T, D = x.shape
    tr = 256

    def body(x_ref, g_ref, t_ref, loss_ref, dx_ref, dg_ref):
        xv = x_ref[...]
        gv = g_ref[...]
        r = lax.rsqrt(jnp.mean(xv * xv, axis=-1, keepdims=True) + EPS)
        n = xv * r
        err = n * gv - t_ref[...]
        part_loss = jnp.zeros((1, 128), F32) + 0.5 * jnp.sum(jnp.mean(err * err, axis=-1, keepdims=True))
        dy = err * (1.0 / D)
        dn = dy * gv
        dx_ref[...] = r * (dn - n * jnp.mean(dn * n, axis=-1, keepdims=True))
        part_g = jnp.sum(dy * n, axis=0, keepdims=True)

        @pl.when(pl.program_id(0) == 0)
        def _():
            loss_ref[...] = part_loss
            dg_ref[...] = part_g

        @pl.when(pl.program_id(0) > 0)
        def _():
            loss_ref[...] += part_loss
            dg_ref[...] += part_g

    row = pl.BlockSpec((tr, D), lambda i: (i, 0))
    vec = pl.BlockSpec((1, D), lambda i: (0, 0))
    sd = jax.ShapeDtypeStruct
    return pl.pallas_call(
        body, name=name, grid=(T // tr,),
        in_specs=[row, vec, row],
        out_specs=[pl.BlockSpec((1, 128), lambda i: (0, 0)), row, vec],
        out_shape=[sd((1, 128), F32), sd((T, D), F32), sd((1, D), F32)],
        compiler_params=_params("arbitrary"),
    )(x, g, target)


def adamw(w, g, m, v, *, name):
    R, C = w.shape
    tr = _pick(R, (256, 128, 64, 32, 16, 8))

    def body(w_ref, g_ref, m_ref, v_ref, d_ref, nm_ref, nv_ref):
        gv = g_ref[...]
        mn = ADAM_B1 * m_ref[...] + (1.0 - ADAM_B1) * gv
        vn = ADAM_B2 * v_ref[...] + (1.0 - ADAM_B2) * (gv * gv)
        m_hat = mn / (1.0 - ADAM_B1 ** ADAM_STEP)
        v_hat = vn / (1.0 - ADAM_B2 ** ADAM_STEP)
        d_ref[...] = -ADAM_LR * (m_hat / (jnp.sqrt(v_hat) + ADAM_EPS) + ADAM_WD * w_ref[...])
        nm_ref[...] = mn
        nv_ref[...] = vn

    blk = pl.BlockSpec((tr, C), lambda i: (i, 0))
    sd = jax.ShapeDtypeStruct((R, C), F32)
    return pl.pallas_call(
        body, name=name, grid=(R // tr,), in_specs=[blk] * 4, out_specs=[blk] * 3, out_shape=[sd] * 3,
        compiler_params=_params("parallel"),
    )(w, g, m, v)


def adamw_many(ws, gs, ms, vs, *, name):
    n = len(ws)

    def body(*refs):
        ins, outs = refs[:4 * n], refs[4 * n:]
        for i in range(n):
            w_ref, g_ref, m_ref, v_ref = ins[i], ins[n + i], ins[2 * n + i], ins[3 * n + i]
            gv = g_ref[...]
            mn = ADAM_B1 * m_ref[...] + (1.0 - ADAM_B1) * gv
            vn = ADAM_B2 * v_ref[...] + (1.0 - ADAM_B2) * (gv * gv)
            m_hat = mn / (1.0 - ADAM_B1 ** ADAM_STEP)
            v_hat = vn / (1.0 - ADAM_B2 ** ADAM_STEP)
            outs[i][...] = -ADAM_LR * (m_hat / (jnp.sqrt(v_hat) + ADAM_EPS) + ADAM_WD * w_ref[...])
            outs[n + i][...] = mn
            outs[2 * n + i][...] = vn

    vm = pl.BlockSpec(memory_space=pltpu.VMEM)
    shapes = [jax.ShapeDtypeStruct(w.shape, F32) for w in ws]
    res = pl.pallas_call(
        body, name=name, in_specs=[vm] * (4 * n), out_specs=[vm] * (3 * n), out_shape=shapes * 3, compiler_params=_params(),
    )(*ws, *gs, *ms, *vs)
    return res[:n], res[n:2 * n], res[2 * n:]


def _softplus(x):
    return jnp.maximum(x, 0.0) + jnp.log(1.0 + jnp.exp(-jnp.abs(x)))


def _lru_gates(x, cw, cb, wa, ba, wx, bx, lam):
    S = x.shape[0]
    row = lax.broadcasted_iota(jnp.int32, (S, 1), 0)

    def back(s):
        return jnp.where(row >= s, pltpu.roll(x, s, 0), 0.0)

    xc = (((cb + cw[0:1] * back(3)) + cw[1:2] * back(2)) + cw[2:3] * back(1)) + cw[3:4] * x
    xb = xc.astype(BF16)
    r = jax.nn.sigmoid(jnp.dot(xb, wa, preferred_element_type=F32) + ba)
    ig = jax.nn.sigmoid(jnp.dot(xb, wx, preferred_element_type=F32) + bx)
    sp = _softplus(-lam)
    la = -LRU_C * r * sp
    a = jnp.exp(la)
    y = 2.0 * la
    one_minus_a2 = jnp.where(y > -0.05, -y * (1.0 + y * (0.5 + y * (1.0 / 6.0 + y * (1.0 / 24.0)))), 1.0 - jnp.exp(y))
    mm = jnp.sqrt(one_minus_a2)
    return xc, xb, r, ig, sp, a, mm


def lru_fwd(aux, cw, cb, wa, ba, wx, bx, lam, *, name):
    T = aux.shape[0]
    S, C = SEQ, LRU_HALF_W

    def body(x_ref, g_ref, cw_ref, cb_ref, wa_ref, ba_ref, wx_ref, bx_ref, lam_ref, o_ref, h_ref, a_s, u_s):
        xc, _, r, ig, sp, a, mm = _lru_gates(x_ref[...], cw_ref[...], cb_ref[...], wa_ref[...], ba_ref[...],
                                             wx_ref[...], bx_ref[...], lam_ref[...])
        a_s[...] = a
        u_s[...] = mm * (ig * xc)

        def group(i, h):
            base = pl.multiple_of(i * 8, 8)
            a8 = a_s[pl.ds(base, 8), :]
            u8 = u_s[pl.ds(base, 8), :]
            for rr in range(8):
                h = a8[rr:rr + 1] * h + u8[rr:rr + 1]
                h_ref[pl.ds(base + rr, 1), :] = h
            return h

        lax.fori_loop(0, S // 8, group, jnp.zeros((1, C), F32))
        gate = g_ref[...]
        o_ref[...] = (h_ref[...] * (gate * _gelu_cdf(gate))).astype(BF16)

    blk = lambda col: pl.BlockSpec((S, C), lambda c, b: (b, col + c))
    par = lambda rows: pl.BlockSpec((rows, C), lambda c, b: (0, c))
    sq = pl.BlockSpec((None, C, C), lambda c, b: (c, 0, 0))
    sd = jax.ShapeDtypeStruct
    W = LRU_HALVES * C
    return pl.pallas_call(
        body, name=name, grid=(LRU_HALVES, T // S),
        in_specs=[blk(0), blk(LRU_HALVES), par(4), par(1), sq, par(1), sq, par(1), par(1)],
        out_specs=[blk(0), blk(0)], out_shape=[sd((T, W), BF16), sd((T, W), F32)],
        scratch_shapes=[pltpu.VMEM((S, C), F32), pltpu.VMEM((S, C), F32)],
        compiler_params=_params("parallel", "parallel"),
    )(aux, aux, cw, cb, wa, ba, wx, bx, lam)


def lru_bwd(aux, h, dmixed, cw, cb, wa, ba, wx, bx, lam, *, name):
    T = aux.shape[0]
    S, C = SEQ, LRU_HALF_W

    def body(x_ref, g_ref, h_ref, do_ref, cw_ref, cb_ref, wa_ref, ba_ref, wx_ref, bx_ref, lam_ref,
             dx_ref, dgate_ref, dcw_ref, dcb_ref, dwa_ref, dba_ref, dwx_ref, dbx_ref, dlam_ref, a_s, d_s):
        x = x_ref[...]
        cw = cw_ref[...]
        lam = lam_ref[...]
        xc, xb, r, ig, sp, a, mm = _lru_gates(x, cw, cb_ref[...], wa_ref[...], ba_ref[...], wx_ref[...], bx_ref[...], lam)
        gate = g_ref[...]
        gl, dgl = _gelu_and_grad(gate)
        dout = do_ref[...]
        hv = h_ref[...]
        dgate_ref[...] = dout * hv * dgl
        a_s[...] = a
        d_s[...] = dout * gl

        def group(i, c):
            base = pl.multiple_of((S // 8 - 1 - i) * 8, 8)
            a8 = a_s[pl.ds(base, 8), :]
            d8 = d_s[pl.ds(base, 8), :]
            for rr in range(7, -1, -1):
                d = d8[rr:rr + 1] + c
                d_s[pl.ds(base + rr, 1), :] = d
                c = a8[rr:rr + 1] * d
            return c

        lax.fori_loop(0, S // 8, group, jnp.zeros((1, C), F32))
        row = lax.broadcasted_iota(jnp.int32, (S, 1), 0)
        dht = d_s[...]
        h_prev = jnp.where(row >= 1, pltpu.roll(hv, 1, 0), 0.0)
        da = dht * h_prev
        gx = ig * xc
        dmm = dht * gx
        dig = dht * mm * xc
        dxc = dht * mm * ig
        dla = da * a - dmm * (a * a) / mm
        dr = dla * (-LRU_C * sp)
        dsp = jnp.sum(dla * (-LRU_C * r), axis=0, keepdims=True)
        dlam = dsp * (-jax.nn.sigmoid(-lam))
        dpa = dr * r * (1.0 - r)
        dpx = dig * ig * (1.0 - ig)
        dpa_b, dpx_b = dpa.astype(BF16), dpx.astype(BF16)
        nt = (((1,), (1,)), ((), ()))
        tn = (((0,), (0,)), ((), ()))
        dxc = dxc + lax.dot_general(dpa_b, wa_ref[...], nt, preferred_element_type=F32) \
                  + lax.dot_general(dpx_b, wx_ref[...], nt, preferred_element_type=F32)
        dwa = lax.dot_general(xb, dpa_b, tn, preferred_element_type=F32)
        dwx = lax.dot_general(xb, dpx_b, tn, preferred_element_type=F32)

        def fwd(v, s):
            return jnp.where(row < S - s, pltpu.roll(v, S - s, 0), 0.0)

        def back(v, s):
            return jnp.where(row >= s, pltpu.roll(v, s, 0), 0.0)

        dx_ref[...] = cw[3:4] * dxc + cw[2:3] * fwd(dxc, 1) + cw[1:2] * fwd(dxc, 2) + cw[0:1] * fwd(dxc, 3)
        s0 = lambda v: jnp.sum(v, axis=0, keepdims=True)
        dcw = jnp.concatenate([s0(dxc * back(x, 3)), s0(dxc * back(x, 2)), s0(dxc * back(x, 1)), s0(dxc * x)], axis=0)
        parts = ((dcw_ref, dcw), (dcb_ref, s0(dxc)), (dwa_ref, dwa), (dba_ref, s0(dpa)), (dwx_ref, dwx),
                 (dbx_ref, s0(dpx)), (dlam_ref, dlam))

        @pl.when(pl.program_id(1) == 0)
        def _():
            for ref, val in parts:
                ref[...] = val

        @pl.when(pl.program_id(1) > 0)
        def _():
            for ref, val in parts:
                ref[...] += val

    blk = lambda col: pl.BlockSpec((S, C), lambda c, b: (b, col + c))
    par = lambda rows: pl.BlockSpec((rows, C), lambda c, b: (0, c))
    sq = pl.BlockSpec((None, C, C), lambda c, b: (c, 0, 0))
    sd = jax.ShapeDtypeStruct
    W = LRU_HALVES * C
    vec = sd((1, W), F32)
    return pl.pallas_call(
        body, name=name, grid=(LRU_HALVES, T // S),
        in_specs=[blk(0), blk(LRU_HALVES), blk(0), blk(3 * LRU_HALVES), par(4), par(1), sq, par(1), sq, par(1), par(1)],
        out_specs=[blk(0), blk(0), par(4), par(1), sq, par(1), sq, par(1), par(1)],
        out_shape=[sd((T, W), F32), sd((T, W), F32), sd((4, W), F32), vec, sd((LRU_HALVES, C, C), F32), vec,
                   sd((LRU_HALVES, C, C), F32), vec, vec],
        scratch_shapes=[pltpu.VMEM((S, C), F32), pltpu.VMEM((S, C), F32)],
        compiler_params=_params("parallel", "arbitrary"),
    )(aux, aux, h, dmixed, cw, cb, wa, ba, wx, bx, lam)


_NT = (((1,), (1,)), ((), ()))
_TN = (((0,), (0,)), ((), ()))


def _dot(a, b, dims=None):
    if dims is None:
        return jnp.dot(a, b, preferred_element_type=F32)
    return lax.dot_general(a, b, dims, preferred_element_type=F32)


def _hs(h):
    return slice(h * HEAD_DIM, (h + 1) * HEAD_DIM)


def cross_fwd(q, kv, *, name):
    T = q.shape[0]
    tq = 512

    def body(q_ref, kv_ref, o_ref):
        for h in range(N_HEADS):
            qh = q_ref[:, _hs(h)] * Q_SCALE
            k = kv_ref[:, _hs(h)]
            v = kv_ref[:, GROUP_W + h * HEAD_DIM:GROUP_W + (h + 1) * HEAD_DIM]
            s = _dot(qh, k, _NT)
            p = jnp.exp(s - jnp.max(s, axis=-1, keepdims=True))
            p = p / jnp.sum(p, axis=-1, keepdims=True)
            o_ref[:, _hs(h)] = _dot(p.astype(BF16), v).astype(BF16)

    per = SEQ // tq
    return pl.pallas_call(
        body, name=name, grid=(T // tq,),
        in_specs=[pl.BlockSpec((tq, GROUP_W), lambda i: (i, 0)), pl.BlockSpec((N_MEM, 2 * GROUP_W), lambda i: (i // per, 0))],
        out_specs=pl.BlockSpec((tq, GROUP_W), lambda i: (i, 0)), out_shape=jax.ShapeDtypeStruct((T, GROUP_W), BF16),
        compiler_params=_params("parallel"),
    )(q, kv)


def cross_bwd(q, kv, do, *, name):
    T = q.shape[0]
    tq = 512
    per = SEQ // tq

    def body(q_ref, kv_ref, do_ref, dq_ref, dkv_ref):
        first = pl.program_id(0) % per == 0
        for h in range(N_HEADS):
            vs = slice(GROUP_W + h * HEAD_DIM, GROUP_W + (h + 1) * HEAD_DIM)
            qh = q_ref[:, _hs(h)] * Q_SCALE
            k = kv_ref[:, _hs(h)]
            v = kv_ref[:, vs]
            doh = do_ref[:, _hs(h)].astype(BF16)
            s = _dot(qh, k, _NT)
            p = jnp.exp(s - jnp.max(s, axis=-1, keepdims=True))
            p = p / jnp.sum(p, axis=-1, keepdims=True)
            dp = _dot(doh, v, _NT)
            ds = (p * (dp - jnp.sum(p * dp, axis=-1, keepdims=True))).astype(BF16)
            dq_ref[:, _hs(h)] = (_dot(ds, k) * Q_SCALE).astype(BF16)
            dk = _dot(ds, qh, _TN)
            dv = _dot(p.astype(BF16), doh, _TN)

            @pl.when(first)
            def _():
                dkv_ref[:, _hs(h)] = dk
                dkv_ref[:, vs] = dv

            @pl.when(jnp.logical_not(first))
            def _():
                dkv_ref[:, _hs(h)] += dk
                dkv_ref[:, vs] += dv

    qb = pl.BlockSpec((tq, GROUP_W), lambda i: (i, 0))
    kvb = pl.BlockSpec((N_MEM, 2 * GROUP_W), lambda i: (i // per, 0))
    sd = jax.ShapeDtypeStruct
    return pl.pallas_call(
        body, name=name, grid=(T // tq,),
        in_specs=[qb, kvb, qb], out_specs=[qb, kvb],
        out_shape=[sd((T, GROUP_W), BF16), sd(kv.shape, F32)],
        compiler_params=_params("arbitrary"),
    )(q, kv, do)


NB = SEQ // BLOCK
NEG = -1e30
HEADS = tuple(range(N_HEADS))


def _blk(i):
    return pl.ds(pl.multiple_of(i * BLOCK, BLOCK), BLOCK)


def _qkv_specs(first_col):
    return [pl.BlockSpec((SEQ, GROUP_W), lambda b, c=first_col + j: (b, c)) for j in range(3)]


LANES = 128
CUM_BLK = 256


def col_to_row(c):
    b = c.shape[0] // SEQ
    return c.reshape(b, SEQ, LANES)[:, :, :8].transpose(0, 2, 1).reshape(b * 8, SEQ)


def row_to_col(r):
    b = r.shape[0] // 8
    c = r.reshape(b, 8, SEQ).transpose(0, 2, 1)
    return jnp.pad(c, ((0, 0), (0, 0), (0, LANES - 8))).reshape(b * SEQ, LANES)


def fox_prep(aux, bf, *, name):
    T = aux.shape[0]

    def body(f_ref, b_ref, o_ref):
        row = lax.broadcasted_iota(jnp.int32, (CUM_BLK, CUM_BLK), 0)
        col = lax.broadcasted_iota(jnp.int32, (CUM_BLK, CUM_BLK), 1)
        upto = (col <= row).astype(BF16)
        carry = jnp.zeros((1, LANES), F32)
        for n in range(SEQ // CUM_BLK):
            rows = slice(n * CUM_BLK, (n + 1) * CUM_BLK)
            logf = -_softplus(-(f_ref[rows, :] + b_ref[...]))
            hi = logf.astype(BF16)
            lo = (logf - hi.astype(F32)).astype(BF16)
            cum = _dot(upto, hi) + _dot(upto, lo) + carry
            o_ref[rows, :] = cum
            carry = cum[CUM_BLK - 1:CUM_BLK]

    return pl.pallas_call(
        body, name=name, grid=(T // SEQ,),
        in_specs=[pl.BlockSpec((SEQ, LANES), lambda b: (b, 4)), pl.BlockSpec((1, LANES), lambda b: (0, 0))],
        out_specs=pl.BlockSpec((SEQ, LANES), lambda b: (b, 0)), out_shape=jax.ShapeDtypeStruct((T, LANES), F32),
        compiler_params=_params("parallel"),
    )(aux, bf)


def fox_prep_bwd(aux, bf, dcum, *, name):
    T = aux.shape[0]

    def body(f_ref, b_ref, d_ref, df_ref, db_ref):
        row = lax.broadcasted_iota(jnp.int32, (CUM_BLK, CUM_BLK), 0)
        col = lax.broadcasted_iota(jnp.int32, (CUM_BLK, CUM_BLK), 1)
        onward = (col >= row).astype(BF16)
        carry = jnp.zeros((1, LANES), F32)
        tot = jnp.zeros((1, LANES), F32)
        for n in range(SEQ // CUM_BLK - 1, -1, -1):
            rows = slice(n * CUM_BLK, (n + 1) * CUM_BLK)
            d = d_ref[rows, :]
            hi = d.astype(BF16)
            lo = (d - hi.astype(F32)).astype(BF16)
            dlogf = _dot(onward, hi) + _dot(onward, lo) + carry
            carry = dlogf[0:1]
            df = dlogf * jax.nn.sigmoid(-(f_ref[rows, :] + b_ref[...]))
            df_ref[rows, :] = df
            tot = tot + jnp.sum(df, axis=0, keepdims=True)

        @pl.when(pl.program_id(0) == 0)
        def _():
            db_ref[...] = tot

        @pl.when(pl.program_id(0) > 0)
        def _():
            db_ref[...] += tot

    blk = pl.BlockSpec((SEQ, LANES), lambda b: (b, 0))
    vec = pl.BlockSpec((1, LANES), lambda b: (0, 0))
    sd = jax.ShapeDtypeStruct
    return pl.pallas_call(
        body, name=name, grid=(T // SEQ,),
        in_specs=[pl.BlockSpec((SEQ, LANES), lambda b: (b, 4)), vec, blk],
        out_specs=[blk, vec], out_shape=[sd((T, LANES), F32), sd((1, LANES), F32)],
        compiler_params=_params("arbitrary"),
    )(aux, bf, dcum)


CHUNK = 256
WIDE = N_HEADS * CHUNK
NCH = SEQ // CHUNK


def _seg(h):
    return slice(h * CHUNK, (h + 1) * CHUNK)


def _chunk_rows(c):
    return pl.ds(pl.multiple_of(c * CHUNK, CHUNK), CHUNK)


def _wide_consts():
    r = lax.broadcasted_iota(jnp.int32, (WIDE, GROUP_W), 0)
    f = lax.broadcasted_iota(jnp.int32, (WIDE, GROUP_W), 1)
    bd = (r // CHUNK) == (f // HEAD_DIM)
    row = lax.broadcasted_iota(jnp.int32, (BLOCK, WIDE), 0)
    key = lax.broadcasted_iota(jnp.int32, (BLOCK, WIDE), 1) % CHUNK
    return bd, row, key


def _block_diag(x, bd):
    return jnp.where(bd, jnp.concatenate([x] * N_HEADS, axis=0), jnp.zeros((), x.dtype))


def _fold_heads(w, bd):
    w = jnp.where(bd, w, 0.0)
    return (w[0:CHUNK] + w[CHUNK:2 * CHUNK]) + (w[2 * CHUNK:3 * CHUNK] + w[3 * CHUNK:])


def _widen(cols):
    return jnp.concatenate([jnp.broadcast_to(c, (BLOCK, CHUNK)) for c in cols], axis=1)


def _head_rowsums(w):
    return [jnp.sum(w[:, _seg(h)], axis=1, keepdims=True) for h in HEADS]


def _tri_wide(x, tri):
    hi = x.astype(BF16)
    lo = (x - hi.astype(F32)).astype(BF16)
    y = _dot(jnp.concatenate([hi[:, _seg(h)] for h in HEADS] + [lo[:, _seg(h)] for h in HEADS], axis=0), tri)
    return jnp.concatenate([y[h * BLOCK:(h + 1) * BLOCK] + y[(N_HEADS + h) * BLOCK:(N_HEADS + h + 1) * BLOCK] for h in HEADS], axis=1)


def _feature_widen(cols):
    return jnp.concatenate([jnp.broadcast_to(c, (BLOCK, HEAD_DIM)) for c in cols], axis=1)


def _loop_by_two(n, index, body, carry):
    odd = n % 2
    carry = lax.fori_loop(0, odd, lambda _, cr: body(index(0), cr), carry)
    return lax.fori_loop(0, n // 2, lambda t, cr: body(index(odd + 2 * t + 1), body(index(odd + 2 * t), cr)), carry)


def _sbw_scores(q, kbd, later):
    z = _dot(q, kbd, _NT)
    lk = -_softplus(z)
    return z + lk, lk, _tri_wide(lk, later)


def _sbw_tile(q, kbd, mask, later, csum):
    z = _dot(q, kbd, _NT)
    lk = -_softplus(z)
    if mask is not None:
        lk = jnp.where(mask, lk, 0.0)
    e = z + lk
    att = jnp.exp(e + _tri_wide(lk, later) + csum)
    if mask is not None:
        att = jnp.where(mask, att, 0.0)
    return att, e, lk


def sbw_fwd(qkv, *, name):
    T = qkv.shape[0]

    def body(q_ref, k_ref, v_ref, o_ref):
        bd, row, key = _wide_consts()
        r2 = lax.broadcasted_iota(jnp.int32, (CHUNK, CHUNK), 0)
        c2 = lax.broadcasted_iota(jnp.int32, (CHUNK, CHUNK), 1)
        later = (r2 > c2).astype(BF16)

        def qblock(i, _):
            q = q_ref[_blk(i), :] * Q_SCALE
            cd = i // 2
            strict = key < row + BLOCK * (i % 2)

            def tile(c, mask, carry):
                acc, csum = carry
                att, _, lk = _sbw_tile(q, _block_diag(k_ref[_chunk_rows(c), :], bd), mask, later, csum)
                acc = acc + _dot(att.astype(BF16), _block_diag(v_ref[_chunk_rows(c), :], bd))
                return acc, csum + _widen(_head_rowsums(lk))

            def two_tiles(c1, carry):
                acc, csum = carry
                e1, lk1, t1 = _sbw_scores(q, _block_diag(k_ref[_chunk_rows(c1), :], bd), later)
                e2, lk2, t2 = _sbw_scores(q, _block_diag(k_ref[_chunk_rows(c1 - 1), :], bd), later)
                att1 = jnp.exp(e1 + t1 + csum)
                csum = csum + _widen(_head_rowsums(lk1))
                att2 = jnp.exp(e2 + t2 + csum)
                csum = csum + _widen(_head_rowsums(lk2))
                acc = acc + _dot(att1.astype(BF16), _block_diag(v_ref[_chunk_rows(c1), :], bd))
                acc = acc + _dot(att2.astype(BF16), _block_diag(v_ref[_chunk_rows(c1 - 1), :], bd))
                return acc, csum

            carry = tile(cd, strict, (jnp.zeros((BLOCK, GROUP_W), F32), jnp.zeros((BLOCK, WIDE), F32)))
            odd = cd % 2
            carry = lax.fori_loop(0, odd, lambda n, cr: tile(cd - 1, None, cr), carry)
            acc, _ = lax.fori_loop(0, cd // 2, lambda n, cr: two_tiles(cd - 1 - odd - 2 * n, cr), carry)
            o_ref[_blk(i), :] = acc.astype(BF16)
            return 0

        lax.fori_loop(0, NB, qblock, 0)

    return pl.pallas_call(
        body, name=name, grid=(T // SEQ,), in_specs=_qkv_specs(0),
        out_specs=pl.BlockSpec((SEQ, GROUP_W), lambda b: (b, 0)), out_shape=jax.ShapeDtypeStruct((T, GROUP_W), BF16),
        compiler_params=_params("parallel"),
    )(qkv, qkv, qkv)


def sbw_bwd(qkv, dmixed, *, name):
    T = qkv.shape[0]

    def body(q_ref, k_ref, v_ref, do_ref, dq_ref, dk_ref, dv_ref, att_s, sg_s):
        bd, row, key = _wide_consts()
        r2 = lax.broadcasted_iota(jnp.int32, (CHUNK, CHUNK), 0)
        c2 = lax.broadcasted_iota(jnp.int32, (CHUNK, CHUNK), 1)
        later = (r2 > c2).astype(BF16)
        earlier = (r2 < c2).astype(BF16)
        dk_ref[...] = jnp.zeros_like(dk_ref)
        dv_ref[...] = jnp.zeros_like(dv_ref)

        def qblock(i, _):
            q = q_ref[_blk(i), :] * Q_SCALE
            do = do_ref[_blk(i), :].astype(BF16)
            cd = i // 2
            strict = key < row + BLOCK * (i % 2)

            def recompute(c, mask, csum):
                att, e, lk = _sbw_tile(q, _block_diag(k_ref[_chunk_rows(c), :], bd), mask, later, csum)
                sg = jnp.exp(e)
                att_s[c] = att
                sg_s[c] = sg if mask is None else jnp.where(mask, sg, 0.0)
                return csum + _widen(_head_rowsums(lk))

            def recompute_two(c1, csum):
                e1, lk1, t1 = _sbw_scores(q, _block_diag(k_ref[_chunk_rows(c1), :], bd), later)
                e2, lk2, t2 = _sbw_scores(q, _block_diag(k_ref[_chunk_rows(c1 - 1), :], bd), later)
                sg_s[c1] = jnp.exp(e1)
                sg_s[c1 - 1] = jnp.exp(e2)
                att_s[c1] = jnp.exp(e1 + t1 + csum)
                csum = csum + _widen(_head_rowsums(lk1))
                att_s[c1 - 1] = jnp.exp(e2 + t2 + csum)
                return csum + _widen(_head_rowsums(lk2))

            csum = recompute(cd, strict, jnp.zeros((BLOCK, WIDE), F32))
            odd = cd % 2
            csum = lax.fori_loop(0, odd, lambda n, cs: recompute(cd - 1, None, cs), csum)
            lax.fori_loop(0, cd // 2, lambda n, cs: recompute_two(cd - 1 - odd - 2 * n, cs), csum)

            def tile(c, carry):
                dq, pre = carry
                kbd = _block_diag(k_ref[_chunk_rows(c), :], bd)
                vbd = _block_diag(v_ref[_chunk_rows(c), :], bd)
                att = att_s[c]
                ds = _dot(do, vbd, _NT) * att
                dlk = ds + _tri_wide(ds, earlier) + pre
                dz = (ds - dlk * sg_s[c]).astype(BF16)
                dk_ref[_chunk_rows(c), :] += _fold_heads(_dot(dz, q, _TN), bd)
                dv_ref[_chunk_rows(c), :] += _fold_heads(_dot(att.astype(BF16), do, _TN), bd)
                return dq + _dot(dz, kbd), pre + _widen(_head_rowsums(ds))

            def two_tiles(c1, carry):
                dq, pre = carry
                c2 = c1 + 1
                kbd1, kbd2 = _block_diag(k_ref[_chunk_rows(c1), :], bd), _block_diag(k_ref[_chunk_rows(c2), :], bd)
                att1, att2 = att_s[c1], att_s[c2]
                ds1 = _dot(do, _block_diag(v_ref[_chunk_rows(c1), :], bd), _NT) * att1
                ds2 = _dot(do, _block_diag(v_ref[_chunk_rows(c2), :], bd), _NT) * att2
                tri1, tri2 = _tri_wide(ds1, earlier), _tri_wide(ds2, earlier)
                dv_ref[_chunk_rows(c1), :] += _fold_heads(_dot(att1.astype(BF16), do, _TN), bd)
                dv_ref[_chunk_rows(c2), :] += _fold_heads(_dot(att2.astype(BF16), do, _TN), bd)
                dz1 = (ds1 - (ds1 + tri1 + pre) * sg_s[c1]).astype(BF16)
                pre = pre + _widen(_head_rowsums(ds1))
                dz2 = (ds2 - (ds2 + tri2 + pre) * sg_s[c2]).astype(BF16)
                pre = pre + _widen(_head_rowsums(ds2))
                dk_ref[_chunk_rows(c1), :] += _fold_heads(_dot(dz1, q, _TN), bd)
                dk_ref[_chunk_rows(c2), :] += _fold_heads(_dot(dz2, q, _TN), bd)
                return dq + _dot(dz1, kbd1) + _dot(dz2, kbd2), pre

            n_tiles = cd + 1
            odd = n_tiles % 2
            carry = (jnp.zeros((BLOCK, GROUP_W), F32), jnp.zeros((BLOCK, WIDE), F32))
            carry = lax.fori_loop(0, odd, lambda n, cr: tile(0, cr), carry)
            dq, _ = lax.fori_loop(0, n_tiles // 2, lambda n, cr: two_tiles(odd + 2 * n, cr), carry)
            dq_ref[_blk(i), :] = dq * Q_SCALE
            return 0

        lax.fori_loop(0, NB, qblock, 0)

    out = pl.BlockSpec((SEQ, GROUP_W), lambda b: (b, 0))
    sd = jax.ShapeDtypeStruct((T, GROUP_W), F32)
    return pl.pallas_call(
        body, name=name, grid=(T // SEQ,), in_specs=_qkv_specs(0) + [out],
        out_specs=[out] * 3, out_shape=[sd] * 3,
        scratch_shapes=[pltpu.VMEM((NCH, BLOCK, WIDE), F32), pltpu.VMEM((NCH, BLOCK, WIDE), F32)],
        compiler_params=_params("parallel"),
    )(qkv, qkv, qkv, dmixed)


def _foxw_logits(q, kbd, cq, cr_ref, c, mask):
    ck = jnp.concatenate([cr_ref[h:h + 1, _chunk_rows(c)] for h in HEADS], axis=1)
    z = _dot(q, kbd, _NT) + cq - ck
    return z if mask is None else jnp.where(mask, z, NEG)


def foxw_fwd(qkv, cumc, cumr, *, name):
    T = qkv.shape[0]

    def body(q_ref, k_ref, v_ref, cc_ref, cr_ref, o_ref, o32_ref, lse_ref, z_s):
        bd, row, key = _wide_consts()
        lse_ref[...] = jnp.zeros_like(lse_ref)

        def qblock(i, _):
            q = q_ref[_blk(i), :] * Q_SCALE
            cq = _widen([cc_ref[_blk(i), h:h + 1] for h in HEADS])
            cd = i // 2
            causal = key <= row + BLOCK * (i % 2)

            def logits(c, mask, ms):
                z = _foxw_logits(q, _block_diag(k_ref[_chunk_rows(c), :], bd), cq, cr_ref, c, mask)
                z_s[c] = z
                return tuple(jnp.maximum(ms[h], jnp.max(z[:, _seg(h)], axis=1, keepdims=True)) for h in HEADS)

            ms = logits(cd, causal, (jnp.full((BLOCK, 1), NEG, F32),) * N_HEADS)
            ms = _loop_by_two(cd, lambda n: n, lambda c, m: logits(c, None, m), ms)
            m_wide = _widen(ms)

            def values(c, carry):
                acc, l = carry
                p = jnp.exp(z_s[c] - m_wide)
                return acc + _dot(p.astype(BF16), _block_diag(v_ref[_chunk_rows(c), :], bd)), l + _widen(_head_rowsums(p))

            acc, l = _loop_by_two(cd + 1, lambda n: n, values, (jnp.zeros((BLOCK, GROUP_W), F32), jnp.zeros((BLOCK, WIDE), F32)))
            ls = [l[:, h * CHUNK:h * CHUNK + 1] for h in HEADS]
            o = acc / _feature_widen(ls)
            o_ref[_blk(i), :] = o.astype(BF16)
            o32_ref[_blk(i), :] = o
            for h in HEADS:
                lse_ref[_blk(i), h:h + 1] = ms[h] + jnp.log(ls[h])
            return 0

        lax.fori_loop(0, NB, qblock, 0)

    out = pl.BlockSpec((SEQ, GROUP_W), lambda b: (b, 0))
    colb = pl.BlockSpec((SEQ, LANES), lambda b: (b, 0))
    sd = jax.ShapeDtypeStruct
    return pl.pallas_call(
        body, name=name, grid=(T // SEQ,),
        in_specs=_qkv_specs(3) + [colb, pl.BlockSpec((8, SEQ), lambda b: (b, 0))],
        out_specs=[out, out, colb], out_shape=[sd((T, GROUP_W), BF16), sd((T, GROUP_W), F32), sd((T, LANES), F32)],
        scratch_shapes=[pltpu.VMEM((NCH, BLOCK, WIDE), F32)],
        compiler_params=_params("parallel"),
    )(qkv, qkv, qkv, cumc, cumr)


def foxw_bwd(qkv, cumc, cumr, lse, o32, dmixed, *, name):
    T = qkv.shape[0]

    def body(q_ref, k_ref, v_ref, cc_ref, cr_ref, lse_ref, o_ref, do_ref, dq_ref, dk_ref, dv_ref, dcc_ref, dcr_ref):
        bd, row, key = _wide_consts()
        dk_ref[...] = jnp.zeros_like(dk_ref)
        dv_ref[...] = jnp.zeros_like(dv_ref)
        dcc_ref[...] = jnp.zeros_like(dcc_ref)
        dcr_ref[...] = jnp.zeros_like(dcr_ref)

        def qblock(i, _):
            q = q_ref[_blk(i), :] * Q_SCALE
            do32 = do_ref[_blk(i), :]
            do = do32.astype(BF16)
            prod = do32 * o_ref[_blk(i), :]
            delta = _widen([jnp.sum(prod[:, _hs(h)], axis=1, keepdims=True) for h in HEADS])
            cq = _widen([cc_ref[_blk(i), h:h + 1] for h in HEADS])
            lse_w = _widen([lse_ref[_blk(i), h:h + 1] for h in HEADS])
            cd = i // 2
            causal = key <= row + BLOCK * (i % 2)

            def tile(c, mask, carry):
                dq, dcq = carry
                kbd = _block_diag(k_ref[_chunk_rows(c), :], bd)
                vbd = _block_diag(v_ref[_chunk_rows(c), :], bd)
                p = jnp.exp(_foxw_logits(q, kbd, cq, cr_ref, c, mask) - lse_w)
                ds = p * (_dot(do, vbd, _NT) - delta)
                dsb = ds.astype(BF16)
                dk_ref[_chunk_rows(c), :] += _fold_heads(_dot(dsb, q, _TN), bd)
                dv_ref[_chunk_rows(c), :] += _fold_heads(_dot(p.astype(BF16), do, _TN), bd)
                for h in HEADS:
                    dcr_ref[h:h + 1, _chunk_rows(c)] -= jnp.sum(ds[:, _seg(h)], axis=0, keepdims=True)
                return dq + _dot(dsb, kbd), dcq + _widen(_head_rowsums(ds))

            def two_tiles(c1, carry):
                dq, dcq = carry
                cs = (c1, c1 + 1)
                kbds = [_block_diag(k_ref[_chunk_rows(c), :], bd) for c in cs]
                vbds = [_block_diag(v_ref[_chunk_rows(c), :], bd) for c in cs]
                ps = [jnp.exp(_foxw_logits(q, kbds[j], cq, cr_ref, cs[j], None) - lse_w) for j in range(2)]
                dss = [ps[j] * (_dot(do, vbds[j], _NT) - delta) for j in range(2)]
                dsbs = [d.astype(BF16) for d in dss]
                for j, c in enumerate(cs):
                    dk_ref[_chunk_rows(c), :] += _fold_heads(_dot(dsbs[j], q, _TN), bd)
                    dv_ref[_chunk_rows(c), :] += _fold_heads(_dot(ps[j].astype(BF16), do, _TN), bd)
                    for h in HEADS:
                        dcr_ref[h:h + 1, _chunk_rows(c)] -= jnp.sum(dss[j][:, _seg(h)], axis=0, keepdims=True)
                dq = dq + _dot(dsbs[0], kbds[0]) + _dot(dsbs[1], kbds[1])
                return dq, dcq + _widen(_head_rowsums(dss[0])) + _widen(_head_rowsums(dss[1]))

            carry = tile(cd, causal, (jnp.zeros((BLOCK, GROUP_W), F32), jnp.zeros((BLOCK, WIDE), F32)))
            odd = cd % 2
            carry = lax.fori_loop(0, odd, lambda n, cr: tile(0, None, cr), carry)
            dq, dcq = lax.fori_loop(0, cd // 2, lambda n, cr: two_tiles(odd + 2 * n, cr), carry)
            dq_ref[_blk(i), :] = dq * Q_SCALE
            for h in HEADS:
                dcc_ref[_blk(i), h:h + 1] = dcq[:, h * CHUNK:h * CHUNK + 1]
            return 0

        lax.fori_loop(0, NB, qblock, 0)

    out = pl.BlockSpec((SEQ, GROUP_W), lambda b: (b, 0))
    colb = pl.BlockSpec((SEQ, LANES), lambda b: (b, 0))
    rowb = pl.BlockSpec((8, SEQ), lambda b: (b, 0))
    sd = jax.ShapeDtypeStruct
    big = sd((T, GROUP_W), F32)
    return pl.pallas_call(
        body, name=name, grid=(T // SEQ,),
        in_specs=_qkv_specs(3) + [colb, rowb, colb, out, pl.BlockSpec((SEQ, GROUP_W), lambda b: (b, 1))],
        out_specs=[out, out, out, colb, rowb],
        out_shape=[big, big, big, sd((T, LANES), F32), sd((T // SEQ * 8, SEQ), F32)],
        compiler_params=_params("parallel"),
    )(qkv, qkv, qkv, cumc, cumr, lse, o32, dmixed)


BAND = 2 * BLOCK


def _t5_bucket_np(dist):
    n = np.maximum(dist, 0)
    max_exact = NUM_BUCKETS // 2
    nf = np.maximum(n, 1).astype(np.float32)
    large = max_exact + (np.log(nf / np.float32(max_exact)) / np.float32(math.log(MAX_DISTANCE / max_exact))
                         * np.float32(NUM_BUCKETS - max_exact)).astype(np.int32)
    large = np.minimum(large, NUM_BUCKETS - 1)
    return np.where(n < max_exact, n, large).astype(np.int32)


def _band_buckets():
    qi = np.arange(BLOCK)[:, None]
    ki = np.arange(BAND)[None, :]
    delta = np.clip(qi - ki + BLOCK, 0, BLOCK)
    return np.stack([_t5_bucket_np(delta * d) for d in DILATIONS])


def relbias_expand(rel, *, name):
    buckets = jnp.asarray(_band_buckets())
    n_pat = len(DILATIONS)

    def body(rel_ref, bk_ref, o_ref):
        for p in range(n_pat):
            bk = bk_ref[p]
            for h in range(N_HEADS):
                acc = jnp.zeros((BLOCK, BAND), F32)
                for b in range(NUM_BUCKETS):
                    acc = jnp.where(bk == b, rel_ref[b, h], acc)
                o_ref[p * N_HEADS + h] = acc

    return pl.pallas_call(
        body, name=name,
        in_specs=[pl.BlockSpec(memory_space=pltpu.SMEM), pl.BlockSpec(memory_space=pltpu.VMEM)],
        out_specs=pl.BlockSpec(memory_space=pltpu.VMEM),
        out_shape=jax.ShapeDtypeStruct((n_pat * N_HEADS, BLOCK, BAND), F32),
        compiler_params=_params(),
    )(rel, buckets)


def relbias_reduce(ds_all, *, name):
    buckets = jnp.asarray(_band_buckets())
    n_pat = len(DILATIONS)

    def body(ds_ref, bk_ref, o_ref):
        for b in range(NUM_BUCKETS):
            for h in range(N_HEADS):
                tot = jnp.float32(0.0)
                for p in range(n_pat):
                    tot = tot + jnp.sum(jnp.where(bk_ref[p] == b, ds_ref[p * N_HEADS + h], 0.0))
                o_ref[b, h] = tot

    return pl.pallas_call(
        body, name=name,
        in_specs=[pl.BlockSpec(memory_space=pltpu.VMEM), pl.BlockSpec(memory_space=pltpu.VMEM)],
        out_specs=pl.BlockSpec(memory_space=pltpu.SMEM),
        out_shape=jax.ShapeDtypeStruct((NUM_BUCKETS, N_HEADS), F32),
        compiler_params=_params(),
    )(ds_all, buckets)


def _band_valid_wide(first, row, key):
    inside = jnp.logical_and(key >= row, key <= row + BLOCK)
    return jnp.logical_and(inside, jnp.logical_or(jnp.logical_not(first), key >= BLOCK))


QKV_BLOCKS = 9


def _band_in_specs(d, pattern, has_prev):
    rows = BLOCK * d
    cur = lambda c: pl.BlockSpec((rows, GROUP_W), lambda tb, r: (tb, c))
    prev = lambda c: pl.BlockSpec((rows, GROUP_W), lambda tb, r: (jnp.maximum(tb - 1, 0), c))
    bias = pl.BlockSpec((N_HEADS, BLOCK, BAND), lambda tb, r: (pattern, 0, 0))
    return [cur(6), cur(7), cur(8)] + ([prev(7), prev(8)] if has_prev else []) + [bias]


def _classes_per_step(d):
    return 2 if d > 1 else 1


def _step_classes(d):
    n = _classes_per_step(d)
    return [pl.program_id(1) * n + j for j in range(n)]


def _class_rows(d, cls):
    return pl.ds(cls, BLOCK, stride=d) if d > 1 else pl.ds(0, BLOCK)


def _halves_scratch(rows, n):
    return [pltpu.VMEM((2, rows, LANES), F32)] * n


def _stage(refs, scratch):
    @pl.when(pl.program_id(1) == 0)
    def _():
        for src, dst in zip(refs, scratch):
            dst[0] = src[:, :LANES].astype(F32)
            dst[1] = src[:, LANES:].astype(F32)


def _take_class(s, d, cls):
    rows = _class_rows(d, cls)
    return jnp.concatenate([s.at[0][rows, :], s.at[1][rows, :]], axis=1)


def _put_class(s, d, cls, x):
    rows = _class_rows(d, cls)
    s.at[0][rows, :] = x[:, :LANES]
    s.at[1][rows, :] = x[:, LANES:]


def _flush(scratch, refs, d):
    @pl.when(pl.program_id(1) == d // _classes_per_step(d) - 1)
    def _():
        for s, o in zip(scratch, refs):
            o[...] = jnp.concatenate([s[0], s[1]], axis=1)


def _band_operands(scratch, d, cls, has_prev):
    take = lambda s: _take_class(s, d, cls).astype(BF16)
    q = (_take_class(scratch[0], d, cls) * Q_SCALE).astype(BF16)
    if has_prev:
        k = jnp.concatenate([take(scratch[3]), take(scratch[1])], axis=0)
        v = jnp.concatenate([take(scratch[4]), take(scratch[2])], axis=0)
    else:
        k = jnp.concatenate([jnp.zeros((BLOCK, GROUP_W), BF16), take(scratch[1])], axis=0)
        v = jnp.concatenate([jnp.zeros((BLOCK, GROUP_W), BF16), take(scratch[2])], axis=0)
    return q, k, v


def _lane_columns(cols):
    lane = lax.broadcasted_iota(jnp.int32, (BLOCK, LANES), 1)
    out = jnp.zeros((BLOCK, LANES), F32)
    for h, c in enumerate(cols):
        out = jnp.where(lane == h, c, out)
    return out


def band_fwd(qkv, bias, pattern, *, name):
    T = qkv.shape[0]
    d = DILATIONS[pattern]
    rows_per_block = BLOCK * d
    seq_blocks = SEQ // rows_per_block
    has_prev = seq_blocks > 1
    n_in = 5 if has_prev else 3

    def body(*refs):
        ins, b_ref, o_ref, lse_ref = refs[:n_in], refs[n_in], refs[n_in + 1], refs[n_in + 2]
        staged, o_s = refs[n_in + 3:2 * n_in + 3], refs[2 * n_in + 3]
        bd, row, key = _wide_consts()
        valid = _band_valid_wide(pl.program_id(0) % seq_blocks == 0, row, key)
        _stage(ins, staged)
        bias_w = jnp.concatenate([b_ref[h] for h in HEADS], axis=1)
        for cls in _step_classes(d):
            q, k, v = _band_operands(staged, d, cls, has_prev)
            kbd, vbd = _block_diag(k, bd), _block_diag(v, bd)
            sc = jnp.where(valid, _dot(q, kbd, _NT) + bias_w, NEG)
            ms = [jnp.max(sc[:, _seg(h)], axis=1, keepdims=True) for h in HEADS]
            p = jnp.exp(sc - _widen(ms))
            ls = _head_rowsums(p)
            _put_class(o_s, d, cls, _dot(p.astype(BF16), vbd) / _feature_widen(ls))
            lse_ref[_class_rows(d, cls), :] = _lane_columns([ms[h] + jnp.log(ls[h]) for h in HEADS])
        _flush([o_s], [o_ref], d)

    sd = jax.ShapeDtypeStruct
    return pl.pallas_call(
        body, name=name, grid=(T // rows_per_block, d // _classes_per_step(d)), in_specs=_band_in_specs(d, pattern, has_prev),
        out_specs=[pl.BlockSpec((rows_per_block, GROUP_W), lambda tb, r: (tb, 0)),
                   pl.BlockSpec((rows_per_block, LANES), lambda tb, r: (tb, 0))],
        out_shape=[sd((T, GROUP_W), F32), sd((T, LANES), F32)],
        scratch_shapes=_halves_scratch(rows_per_block, n_in + 1),
        compiler_params=_params("parallel", "arbitrary"),
    )(*([qkv] * n_in), bias)


def band_bwd(qkv, bias, lse, do, dlse, pattern, *, name):
    T = qkv.shape[0]
    d = DILATIONS[pattern]
    rows_per_block = BLOCK * d
    seq_blocks = SEQ // rows_per_block
    has_prev = seq_blocks > 1
    n_in = 5 if has_prev else 3
    n_out = 5 if has_prev else 3

    def body(*refs):
        ins, b_ref, lse_ref, do_ref, dlse_ref = refs[:n_in], refs[n_in], refs[n_in + 1], refs[n_in + 2], refs[n_in + 3]
        outs = refs[n_in + 4:n_in + 4 + n_out]
        ds_ref = refs[n_in + 4 + n_out]
        scratch = refs[n_in + 5 + n_out:]
        staged, do_s, out_s = scratch[:n_in], scratch[n_in], scratch[n_in + 1:]
        first_step = jnp.logical_and(pl.program_id(0) == 0, pl.program_id(1) == 0)
        bd, row, key = _wide_consts()
        valid = _band_valid_wide(pl.program_id(0) % seq_blocks == 0, row, key)
        _stage(list(ins) + [do_ref], list(staged) + [do_s])
        bias_w = jnp.concatenate([b_ref[h] for h in HEADS], axis=1)
        ds = None
        for cls in _step_classes(d):
            q, k, v = _band_operands(staged, d, cls, has_prev)
            kbd, vbd = _block_diag(k, bd), _block_diag(v, bd)
            rows = _class_rows(d, cls)
            do = _take_class(do_s, d, cls).astype(BF16)
            lse_t, dlse_t = lse_ref[rows, :], dlse_ref[rows, :]
            lse_w = _widen([lse_t[:, h:h + 1] for h in HEADS])
            dlse_w = _widen([dlse_t[:, h:h + 1] for h in HEADS])
            p = jnp.where(valid, jnp.exp(_dot(q, kbd, _NT) + bias_w - lse_w), 0.0)
            dp = _dot(do, vbd, _NT)
            ds_c = p * (dp - _widen(_head_rowsums(p * dp)) + dlse_w)
            dsb, pb = ds_c.astype(BF16), p.astype(BF16)
            _put_class(out_s[0], d, cls, _dot(dsb, kbd) * Q_SCALE)
            dk = _fold_heads(_dot(dsb, q, _TN), bd)
            dv = _fold_heads(_dot(pb, do, _TN), bd)
            _put_class(out_s[1], d, cls, dk[BLOCK:])
            _put_class(out_s[2], d, cls, dv[BLOCK:])
            if has_prev:
                _put_class(out_s[3], d, cls, dk[:BLOCK])
                _put_class(out_s[4], d, cls, dv[:BLOCK])
            ds = ds_c if ds is None else ds + ds_c
        _flush(out_s, outs, d)

        @pl.when(first_step)
        def _():
            for h in HEADS:
                ds_ref[h] = ds[:, _seg(h)]

        @pl.when(jnp.logical_not(first_step))
        def _():
            for h in HEADS:
                ds_ref[h] += ds[:, _seg(h)]

    big = pl.BlockSpec((rows_per_block, GROUP_W), lambda tb, r: (tb, 0))
    colb = pl.BlockSpec((rows_per_block, LANES), lambda tb, r: (tb, 0))
    sd = jax.ShapeDtypeStruct
    return pl.pallas_call(
        body, name=name, grid=(T // rows_per_block, d // _classes_per_step(d)),
        in_specs=_band_in_specs(d, pattern, has_prev) + [colb, big, colb],
        out_specs=[big] * n_out + [pl.BlockSpec((N_HEADS, BLOCK, BAND), lambda tb, r: (0, 0, 0))],
        out_shape=[sd((T, GROUP_W), F32)] * n_out + [sd((N_HEADS, BLOCK, BAND), F32)],
        scratch_shapes=_halves_scratch(rows_per_block, n_in + 1 + n_out),
        compiler_params=_params("arbitrary", "arbitrary"),
    )(*([qkv] * n_in), bias, lse, do, dlse)


def shift_add(cur, prev, d, *, name):
    rows = BLOCK * d
    nb = cur.shape[0] // rows

    def body(c_ref, p_ref, o_ref):
        keep = (pl.program_id(0) < nb - 1).astype(F32)
        o_ref[...] = c_ref[...] + keep * p_ref[...]

    blk = pl.BlockSpec((rows, GROUP_W), lambda tb: (tb, 0))
    nxt = pl.BlockSpec((rows, GROUP_W), lambda tb: (jnp.minimum(tb + 1, nb - 1), 0))
    return pl.pallas_call(
        body, name=name, grid=(nb,), in_specs=[blk, nxt], out_specs=blk,
        out_shape=jax.ShapeDtypeStruct(cur.shape, F32), compiler_params=_params("parallel"),
    )(cur, prev)


def _pattern_weights(lse_refs, h):
    ls = [r[:, h:h + 1] for r in lse_refs]
    mx = functools.reduce(jnp.maximum, ls)
    es = [jnp.exp(l - mx) for l in ls]
    tot = functools.reduce(lambda a, b: a + b, es)
    return [e / tot for e in es]


def dil_combine_fwd(outs, *, name):
    T = outs[0][0].shape[0]
    n = len(outs)
    tm = 512

    def body(*refs):
        o_refs, l_refs, out_ref = refs[:n], refs[n:2 * n], refs[2 * n]
        for h in range(N_HEADS):
            w = _pattern_weights(l_refs, h)
            acc = w[0] * o_refs[0][:, _hs(h)]
            for p in range(1, n):
                acc = acc + w[p] * o_refs[p][:, _hs(h)]
            out_ref[:, _hs(h)] = acc.astype(BF16)

    big = pl.BlockSpec((tm, GROUP_W), lambda i: (i, 0))
    colb = pl.BlockSpec((tm, LANES), lambda i: (i, 0))
    return pl.pallas_call(
        body, name=name, grid=(T // tm,), in_specs=[big] * n + [colb] * n,
        out_specs=big, out_shape=jax.ShapeDtypeStruct((T, GROUP_W), BF16),
        compiler_params=_params("parallel"),
    )(*[o for o, _ in outs], *[l for _, l in outs])


def dil_combine_bwd(outs, dmixed, *, name):
    T = outs[0][0].shape[0]
    n = len(outs)
    tm = 512

    def body(*refs):
        o_refs, l_refs, do_ref = refs[:n], refs[n:2 * n], refs[2 * n]
        do_refs, dl_refs = refs[2 * n + 1:3 * n + 1], refs[3 * n + 1:]
        for r in dl_refs:
            r[...] = jnp.zeros_like(r)
        for h in range(N_HEADS):
            w = _pattern_weights(l_refs, h)
            do = do_ref[:, _hs(h)]
            dw = [jnp.sum(do * o_refs[p][:, _hs(h)], axis=1, keepdims=True) for p in range(n)]
            mean = functools.reduce(lambda a, b: a + b, [w[p] * dw[p] for p in range(n)])
            for p in range(n):
                do_refs[p][:, _hs(h)] = w[p] * do
                dl_refs[p][:, h:h + 1] = w[p] * (dw[p] - mean)

    big = pl.BlockSpec((tm, GROUP_W), lambda i: (i, 0))
    colb = pl.BlockSpec((tm, LANES), lambda i: (i, 0))
    sd = jax.ShapeDtypeStruct
    res = pl.pallas_call(
        body, name=name, grid=(T // tm,),
        in_specs=[big] * n + [colb] * n + [pl.BlockSpec((tm, GROUP_W), lambda i: (i, 2))],
        out_specs=[big] * n + [colb] * n, out_shape=[sd((T, GROUP_W), F32)] * n + [sd((T, LANES), F32)] * n,
        compiler_params=_params("parallel"),
    )(*[o for o, _ in outs], *[l for _, l in outs], dmixed)
    return list(zip(res[:n], res[n:]))


def dilated_fwd(qkv, bias, tag):
    return [band_fwd(qkv, bias, p, name=f"{tag}_band_fwd{p}") for p in range(len(DILATIONS))]


def dilated_bwd(qkv, bias, outs, dmixed, tag):
    grads = dil_combine_bwd(outs, dmixed, name=f"{tag}_combine_bwd")
    parts, ds_all = [], []
    for p, d in enumerate(DILATIONS):
        (_, lse), (do, dlse) = outs[p], grads[p]
        res = band_bwd(qkv, bias, lse, do, dlse, p, name=f"{tag}_band_bwd{p}")
        dq, dk, dv, ds = res[0], res[1], res[2], res[-1]
        if len(res) > 4:
            dk = shift_add(dk, res[3], d, name=f"{tag}_dk{p}")
            dv = shift_add(dv, res[4], d, name=f"{tag}_dv{p}")
        parts.append([dq, dk, dv])
        ds_all.append(ds)
    return parts, jnp.concatenate(ds_all, axis=0)


def assemble_dqkv(d_sb, d_fox, d_dil, *, name):
    T = d_sb[0].shape[0]
    tr = 512
    n_pat = len(d_dil)
    flat = list(d_sb) + list(d_fox) + [a for part in d_dil for a in part]

    def body(*refs):
        o_ref = refs[-1]
        for j in range(6):
            o_ref[:, j * GROUP_W:(j + 1) * GROUP_W] = refs[j][...].astype(BF16)
        for j in range(3):
            acc = refs[6 + j][...]
            for p in range(1, n_pat):
                acc = acc + refs[6 + 3 * p + j][...]
            o_ref[:, (6 + j) * GROUP_W:(7 + j) * GROUP_W] = acc.astype(BF16)

    blk = pl.BlockSpec((tr, GROUP_W), lambda i: (i, 0))
    return pl.pallas_call(
        body, name=name, grid=(T // tr,), in_specs=[blk] * len(flat),
        out_specs=pl.BlockSpec((tr, QKV_BLOCKS * GROUP_W), lambda i: (i, 0)),
        out_shape=jax.ShapeDtypeStruct((T, QKV_BLOCKS * GROUP_W), BF16), compiler_params=_params("parallel"),
    )(*flat)


def sum_cast(arrs, dtype, *, name):
    R, C = arrs[0].shape
    tr = _largest_tile(R, 512, 16)
    n = len(arrs)

    def body(*refs):
        acc = refs[0][...].astype(F32)
        for r in refs[1:n]:
            acc = acc + r[...].astype(F32)
        refs[n][...] = acc.astype(dtype)

    blk = pl.BlockSpec((tr, C), lambda i: (i, 0))
    return pl.pallas_call(
        body, name=name, grid=(R // tr,), in_specs=[blk] * n, out_specs=blk, out_shape=jax.ShapeDtypeStruct((R, C), dtype),
        compiler_params=_params("parallel"),
    )(*arrs)


GRAD_WIRE = BF16


def _block_diag_halves(w):
    z = jnp.zeros((HEAD_DIM, HEAD_DIM), w.dtype)
    half = lambda a, b: jnp.concatenate([jnp.concatenate([a, z], axis=1), jnp.concatenate([z, b], axis=1)], axis=0)
    return jnp.stack([half(w[0], w[1]), half(w[2], w[3])]).astype(BF16)


def _diag_blocks(d):
    h = HEAD_DIM
    return jnp.stack([d[0, :h, :h], d[0, h:, h:], d[1, :h, :h], d[1, h:, h:]])


def layer_fwd(x, mem2d, W, P, bias, tag):
    s = {}
    s["x"] = x
    h1 = rmsnorm_fwd(x, P["norm_mix_g"], name=f"{tag}_norm_mix")
    qkv = matmul(h1, W["qkv"], out_dtype=BF16, name=f"{tag}_qkv")
    aux = matmul(h1, W["aux"], name=f"{tag}_aux")
    o_sb = sbw_fwd(qkv, name=f"{tag}_sb_fwd")
    cumc = fox_prep(aux, P["bf"], name=f"{tag}_fox_prep")
    cumr = col_to_row(cumc)
    o_fox, o_fox32, lse_fox = foxw_fwd(qkv, cumc, cumr, name=f"{tag}_fox_fwd")
    dil = dilated_fwd(qkv, bias, tag)
    o_dil = dil_combine_fwd(dil, name=f"{tag}_dil_combine")
    o_lru, h_lru = lru_fwd(aux, P["lru_conv_w"], P["lru_conv_b"], P["wa"], P["lru_b_a"], P["wx"], P["lru_b_x"],
                           P["lru_lambda"], name=f"{tag}_lru_fwd")
    mixed = jnp.concatenate([o_sb, o_fox, o_dil, o_lru], axis=1)
    x1 = matmul(mixed, W["out"], residual=x, name=f"{tag}_out")
    hq = rmsnorm_fwd(x1, P["norm_cross_g"], name=f"{tag}_norm_cross")
    qc = matmul(hq, W["cq"], out_dtype=BF16, name=f"{tag}_cq")
    memn = rmsnorm_fwd(mem2d, P["norm_mem_g"], name=f"{tag}_norm_mem")
    kv = matmul(memn, W["ckv"], out_dtype=BF16, name=f"{tag}_ckv")
    oc = cross_fwd(qc, kv, name=f"{tag}_cross_fwd")
    x2 = matmul(oc, W["coT"], trans_b=True, residual=x1, name=f"{tag}_co")
    h2 = rmsnorm_fwd(x2, P["norm_ffn_g"], name=f"{tag}_norm_ffn")
    if "ffn" in W:
        W.update(W.pop("ffn")(x2))
    hu = matmul(h2, W["up_u"], trans_b=True, name=f"{tag}_up_u")
    hg = matmul(h2, W["up_g"], trans_b=True, name=f"{tag}_up_g")
    act = glu_fwd(hu, hg, P["wu"], P["wg"], P["bu"], P["bg"], name=f"{tag}_glu_fwd")
    x3 = matmul(act, W["down"], residual=x2, name=f"{tag}_down")
    s.update(h1=h1, qkv=qkv, aux=aux, cumc=cumc, cumr=cumr, lse_fox=lse_fox, o_fox32=o_fox32, dil=dil, h_lru=h_lru, mixed=mixed,
             x1=x1, hq=hq, qc=qc, memn=memn, kv=kv, oc=oc, x2=x2, h2=h2, hu=hu, hg=hg, act=act)
    return x3, s


def layer_bwd(dx3, mem2d, W, P, bias, s, tag, ffn_grads_done=None):
    mm = functools.partial(matmul, out_dtype=GRAD_WIRE, trans_a=True)
    gW, gP = {}, {}
    dact = matmul(dx3, W["down"], trans_b=True, name=f"{tag}_d_act")
    gW["down"] = mm(s["act"], dx3, name=f"{tag}_g_down")
    dhu, dhg, dwu, dwg, dbu, dbg = glu_bwd(s["hu"], s["hg"], dact, P["wu"], P["wg"], P["bu"], P["bg"], name=f"{tag}_glu_bwd")
    gP["ffn_conv_w"] = jnp.concatenate([dwu, dwg], axis=1)
    gP["ffn_conv_b"] = jnp.concatenate([dbu, dbg], axis=1)
    dh2 = matmul(dhu, W["up_u"], name=f"{tag}_d_h2u")
    dh2 = matmul(dhg, W["up_g"], residual=dh2, name=f"{tag}_d_h2g")
    gW["up_u"] = mm(dhu, s["h2"], name=f"{tag}_g_up_u")
    gW["up_g"] = mm(dhg, s["h2"], name=f"{tag}_g_up_g")
    dx2, gP["norm_ffn_g"] = rmsnorm_bwd(s["x2"], P["norm_ffn_g"], dh2, dx3, name=f"{tag}_norm_ffn_bwd")
    if ffn_grads_done is not None:
        dx2 = ffn_grads_done(gW, dx2)
    doc = matmul(dx2, W["coT"], name=f"{tag}_d_oc")
    gW["coT"] = mm(dx2, s["oc"], name=f"{tag}_g_co")
    dqc, dkv = cross_bwd(s["qc"], s["kv"], doc, name=f"{tag}_cross_bwd")
    dhq = matmul(dqc, W["cq"], trans_b=True, name=f"{tag}_d_hq")
    gW["cq"] = mm(s["hq"], dqc, name=f"{tag}_g_cq")
    dmemn = matmul(dkv, W["ckv"], trans_b=True, name=f"{tag}_d_memn")
    gW["ckv"] = mm(s["memn"], dkv, name=f"{tag}_g_ckv")
    _, gP["norm_mem_g"] = rmsnorm_bwd(mem2d, P["norm_mem_g"], dmemn, None, name=f"{tag}_norm_mem_bwd")
    dx1, gP["norm_cross_g"] = rmsnorm_bwd(s["x1"], P["norm_cross_g"], dhq, dx2, name=f"{tag}_norm_cross_bwd")
    dmixed = matmul(dx1, W["out"], trans_b=True, name=f"{tag}_d_mixed")
    gW["out"] = mm(s["mixed"], dx1, name=f"{tag}_g_out")
    qkv, aux = s["qkv"], s["aux"]
    d_sb = sbw_bwd(qkv, dmixed, name=f"{tag}_sb_bwd")
    dfq, dfk, dfv, dcc, dcr = foxw_bwd(qkv, s["cumc"], s["cumr"], s["lse_fox"], s["o_fox32"], dmixed, name=f"{tag}_fox_bwd")
    dcum = sum_cast([dcc, row_to_col(dcr)], F32, name=f"{tag}_dcum")
    df, dbf = fox_prep_bwd(aux, P["bf"], dcum, name=f"{tag}_fox_prep_bwd")
    gP["b_forget"] = dbf[0, :N_HEADS]
    d_dil, ds_band = dilated_bwd(qkv, bias, s["dil"], dmixed, tag)
    dlx, dlg, dcw, dcb, dwa, dba, dwx, dbx, dlam = lru_bwd(
        aux, s["h_lru"], dmixed, P["lru_conv_w"], P["lru_conv_b"], P["wa"], P["lru_b_a"], P["wx"], P["lru_b_x"],
        P["lru_lambda"], name=f"{tag}_lru_bwd")
    gP.update(lru_conv_w=dcw, lru_conv_b=dcb, lru_w_a=_diag_blocks(dwa), lru_b_a=dba, lru_w_x=_diag_blocks(dwx),
              lru_b_x=dbx, lru_lambda=dlam)
    dqkv = assemble_dqkv(d_sb, [dfq, dfk, dfv], d_dil, name=f"{tag}_dqkv")
    daux = jnp.concatenate([dlx, dlg, df], axis=1)
    dh1 = matmul(dqkv, W["qkv"], trans_b=True, name=f"{tag}_d_h1a")
    dh1 = matmul(daux, W["aux"], trans_b=True, residual=dh1, name=f"{tag}_d_h1b")
    gW["qkv"] = mm(s["h1"], dqkv, name=f"{tag}_g_qkv")
    gW["aux"] = mm(s["h1"], daux, name=f"{tag}_g_aux")
    dx, gP["norm_mix_g"] = rmsnorm_bwd(s["x"], P["norm_mix_g"], dh1, dx1, name=f"{tag}_norm_mix_bwd")
    return dx, gW, gP, ds_band


def local_step(x, mem, target, weights_of, Ps, rel_bias, final_norm_g, grads_done=None, ffn_grads_done=None):
    B = x.shape[0]
    x2d = x.reshape(B * SEQ, D_MODEL)
    mem2d = mem.reshape(B * N_MEM, D_MODEL)
    bias = relbias_expand(rel_bias, name="relbias_expand")
    saved, Ws = [], []
    h = x2d
    for l in range(DEPTH):
        Ws.append(weights_of(l, h))
        h, s = layer_fwd(h, mem2d, Ws[l], Ps[l], bias, f"l{l}")
        saved.append(s)
    loss, dh, d_final = loss_head(h, final_norm_g, target.reshape(B * SEQ, D_MODEL), name="loss_head")
    gWs, gPs, ds_bands = [None] * DEPTH, [None] * DEPTH, []
    for l in range(DEPTH - 1, -1, -1):
        hook = None if ffn_grads_done is None else functools.partial(ffn_grads_done, l)
        dh, gWs[l], gPs[l], ds = layer_bwd(dh, mem2d, Ws[l], Ps[l], bias, saved[l], f"l{l}", hook)
        if grads_done is not None:
            dh = grads_done(l, gWs[l], dh)
        ds_bands.append(ds)
    d_rel = relbias_reduce(sum_cast([d.reshape(-1, BAND) for d in ds_bands], F32, name="ds_band_sum").reshape(-1, BLOCK, BAND),
                           name="relbias_reduce")
    return loss, dh.reshape(B, SEQ, D_MODEL), gWs, gPs, d_rel, d_final


def small_params(p, l):
    row = lambda name: p[name][l].reshape(1, -1)
    ffn_w, ffn_b = p["ffn_conv_w"][l], row("ffn_conv_b")
    return dict(
        norm_mix_g=row("norm_mix_g"), norm_cross_g=row("norm_cross_g"), norm_mem_g=row("norm_mem_g"), norm_ffn_g=row("norm_ffn_g"),
        bf=jnp.pad(row("b_forget"), ((0, 0), (0, LANES - N_HEADS))),
        lru_conv_w=p["lru_conv_w"][l], lru_conv_b=row("lru_conv_b"), wa=_block_diag_halves(p["lru_w_a"][l]), lru_b_a=row("lru_b_a"),
        wx=_block_diag_halves(p["lru_w_x"][l]), lru_b_x=row("lru_b_x"), lru_lambda=row("lru_lambda"),
        wu=ffn_w[:, :D_FF], wg=ffn_w[:, D_FF:], bu=ffn_b[:, :D_FF], bg=ffn_b[:, D_FF:])


def canonical_weights(w_in, w_out, w_cq, w_ck, w_cv, w_co, w_up, w_down):
    sb_fox, fox_f, rest = w_in[:, :6 * GROUP_W], w_in[:, 6 * GROUP_W:6 * GROUP_W + N_HEADS], w_in[:, 6 * GROUP_W + N_HEADS:]
    dil, lru = rest[:, :3 * GROUP_W], rest[:, 3 * GROUP_W:]
    pad = jnp.zeros((w_in.shape[0], AUX_W - 2 * GROUP_W - N_HEADS), w_in.dtype)
    return dict(qkv=jnp.concatenate([sb_fox, dil], axis=1), aux=jnp.concatenate([lru, fox_f, pad], axis=1), out=w_out,
                cq=w_cq, ckv=jnp.concatenate([w_ck, w_cv], axis=1), coT=w_co.T, upT=w_up.T, down=w_down)


def native_grads(g):
    qkv, aux = g["qkv"], g["aux"]
    a, b = 6 * GROUP_W, 6 * GROUP_W + N_HEADS
    w_in = jnp.zeros((qkv.shape[0], b + 5 * GROUP_W), qkv.dtype)
    w_in = w_in.at[:, :a].set(qkv[:, :a]).at[:, a:b].set(aux[:, 2 * GROUP_W:2 * GROUP_W + N_HEADS])
    w_in = w_in.at[:, b:b + 3 * GROUP_W].set(qkv[:, a:]).at[:, b + 3 * GROUP_W:].set(aux[:, :2 * GROUP_W])
    return (w_in, g["out"], g["cq"], g["ckv"][:, :GROUP_W], g["ckv"][:, GROUP_W:], g["coT"].T) + native_ffn_grads(g)


def native_ffn_grads(g):
    return (g["upT"].T, g["down"])


ANY = pl.BlockSpec(memory_space=pl.ANY)
VMEM_SPEC = pl.BlockSpec(memory_space=pltpu.VMEM)


def _place():
    x, y, c = lax.axis_index("x"), lax.axis_index("y"), lax.axis_index("c")
    other_chips = [(1 - x, y), (x, 1 - y), (1 - x, 1 - y)]
    return x, y, c, other_chips


def _gather_body(x_ref, out_ref, send_sems, recv_sems, local_sem):
    x, y, c, chips = _place()
    me, sibling = (x, y, c), (x, y, 1 - c)

    def slot(px, py, pc):
        return out_ref.at[4 * px + 2 * py + pc]

    def copy(k, block, to, src=None):
        return pltpu.make_async_remote_copy(
            src_ref=slot(*block) if src is None else src, dst_ref=slot(*block),
            send_sem=send_sems.at[k], recv_sem=recv_sems.at[k], device_id=to, device_id_type=MESH)

    if local_sem is not None:
        mine = pltpu.make_async_copy(x_ref, slot(*me), local_sem)
        mine.start()
    first = [copy(0, me, sibling, src=x_ref)]
    first += [copy(1 + j, me, (*chip, c), src=x_ref) for j, chip in enumerate(chips)]
    for cp in first:
        cp.start()
    passed = [copy(4 + j, (*chip, c), sibling) for j, chip in enumerate(chips)]
    for j, chip in enumerate(chips):
        copy(1 + j, (*chip, c), me).wait_recv()
        passed[j].start()
    copy(0, sibling, me).wait_recv()
    for j, chip in enumerate(chips):
        copy(4 + j, (*chip, 1 - c), me).wait_recv()
    for cp in first + passed:
        cp.wait_send()
    if local_sem is not None:
        mine.wait()


_GATHER_SEMS = [pltpu.SemaphoreType.DMA((7,)), pltpu.SemaphoreType.DMA((7,)), pltpu.SemaphoreType.DMA]


def allgather_hbm(shard, me, *, name):
    def body(x_ref, out_ref, done_ref, send_sems, recv_sems):
        _gather_body(x_ref, out_ref, send_sems, recv_sems, None)
        done_ref[...] = jnp.zeros_like(done_ref)

    others, done = pl.pallas_call(
        body, name=name, in_specs=[ANY], out_specs=[ANY, VMEM_SPEC],
        out_shape=[jax.ShapeDtypeStruct((N_DEV,) + shard.shape, shard.dtype), jax.ShapeDtypeStruct((8, LANES), F32)],
        scratch_shapes=_GATHER_SEMS[:2],
    )(shard)
    return lax.dynamic_update_slice(others, shard[None], (me, 0, 0)), done


def allgather_small(x, *, name, reduce=False):
    def body(x_ref, out_ref, second_ref, *sems):
        _gather_body(x_ref, out_ref, *sems)
        if reduce:
            acc = out_ref[0]
            for d in range(1, N_DEV):
                acc = acc + out_ref[d]
            second_ref[...] = acc
        else:
            second_ref[...] = jnp.zeros_like(second_ref)

    sd = jax.ShapeDtypeStruct
    return pl.pallas_call(
        body, name=name, in_specs=[VMEM_SPEC], out_specs=[VMEM_SPEC, VMEM_SPEC],
        out_shape=[sd((N_DEV,) + x.shape, x.dtype), sd(x.shape if reduce else (8, LANES), x.dtype)],
        scratch_shapes=_GATHER_SEMS, compiler_params=pltpu.CompilerParams(vmem_limit_bytes=VMEM_LIMIT_V7X),
    )(x)


N_CHIPS = 4


def pair_exchange(g, *, name):
    _, R, C = g.shape

    def body(g_ref, recv_ref, send_sems, recv_sems):
        x, y, c, _ = _place()
        sibling = (x, y, 1 - c)
        remote = [pltpu.make_async_remote_copy(
            src_ref=g_ref.at[2 * q + (1 - c)], dst_ref=recv_ref.at[q], send_sem=send_sems.at[q], recv_sem=recv_sems.at[q],
            device_id=sibling, device_id_type=MESH) for q in range(N_CHIPS)]
        for cp in remote:
            cp.start()
        for cp in remote:
            cp.wait_recv()
        for cp in remote:
            cp.wait_send()

    return pl.pallas_call(
        body, name=name, in_specs=[ANY], out_specs=ANY, out_shape=jax.ShapeDtypeStruct((N_CHIPS, R, C), g.dtype),
        scratch_shapes=[pltpu.SemaphoreType.DMA((N_CHIPS,))] * 2,
    )(g)


def chip_exchange(s, *, name):
    _, R, C = s.shape

    def body(s_ref, o0, o1, o2, send_sems, recv_sems):
        x, y, c, chips = _place()
        outs = (o0, o1, o2)
        copies = [pltpu.make_async_remote_copy(
            src_ref=s_ref.at[2 * cx + cy], dst_ref=outs[j], send_sem=send_sems.at[j], recv_sem=recv_sems.at[j],
            device_id=(cx, cy, c), device_id_type=MESH) for j, (cx, cy) in enumerate(chips)]
        for cp in copies:
            cp.start()
        for cp in copies:
            cp.wait_recv()
        for cp in copies:
            cp.wait_send()

    sd = jax.ShapeDtypeStruct((R, C), s.dtype)
    return pl.pallas_call(
        body, name=name, in_specs=[ANY], out_specs=[ANY] * 3, out_shape=[sd] * 3,
        scratch_shapes=[pltpu.SemaphoreType.DMA((3,)), pltpu.SemaphoreType.DMA((3,))],
    )(s)


HBM_SPEC = pl.BlockSpec(memory_space=pltpu.HBM)
SEM_SPEC = pl.BlockSpec(memory_space=pltpu.SEMAPHORE)
N_PEERS = N_DEV - 1


def _peers():
    x, y, c = lax.axis_index("x"), lax.axis_index("y"), lax.axis_index("c")
    flip = lambda v, bit: 1 - v if bit else v
    out = []
    for k in range(1, N_DEV):
        px, py, pc = flip(x, (k >> 2) & 1), flip(y, (k >> 1) & 1), flip(c, k & 1)
        out.append(((px, py, pc), 4 * px + 2 * py + pc))
    return out, 4 * x + 2 * y + c


def _peer_copies(src_ref, land_ref, send_sems, recv_sems, scatter, landing):
    peers, me = _peers()
    return [pltpu.make_async_remote_copy(
        src_ref=src_ref.at[idx] if scatter else src_ref, dst_ref=land_ref.at[me if landing == "mine" else idx],
        send_sem=send_sems.at[k], recv_sem=recv_sems.at[k], device_id=peer, device_id_type=MESH)
        for k, (peer, idx) in enumerate(peers)]


def exchange_start(src, scatter, *, name):
    shape = (N_DEV,) + src.shape[-2:]

    def body(src_ref, land_ref, send_sems, recv_sems, src_thru, land_thru, token):
        for cp in _peer_copies(src_ref, land_ref, send_sems, recv_sems, scatter, "mine"):
            cp.start()
        token[...] = jnp.zeros_like(token)

    sems = pltpu.SemaphoreType.DMA((N_PEERS,))
    return pl.pallas_call(
        body, name=name,
        out_shape=(sems, sems, pltpu.HBM(src.shape, src.dtype), pltpu.HBM(shape, src.dtype), jax.ShapeDtypeStruct((8, LANES), F32)),
        in_specs=(HBM_SPEC, HBM_SPEC), out_specs=(SEM_SPEC, SEM_SPEC, HBM_SPEC, HBM_SPEC, VMEM_SPEC),
        input_output_aliases={0: 2, 1: 3},
        compiler_params=pltpu.CompilerParams(has_side_effects=pltpu.SideEffectType.DATAFLOW_SIDE_EFFECTING),
    )(pltpu.with_memory_space_constraint(src, pltpu.HBM), pltpu.with_memory_space_constraint(lax.empty(shape, src.dtype), pltpu.HBM))


def exchange_wait(started, after, scatter, *, name):
    send_sems, recv_sems, src_thru, land_thru, _ = started

    def body(src_ref, land_ref, send_sems, recv_sems, after_ref, src_dead, got_ref):
        for cp in _peer_copies(src_ref, land_ref, send_sems, recv_sems, scatter, "theirs"):
            cp.wait_send()
            cp.wait_recv()

    return pl.pallas_call(
        body, name=name, out_shape=(pltpu.HBM(src_thru.shape, src_thru.dtype), pltpu.HBM(land_thru.shape, land_thru.dtype)),
        in_specs=(HBM_SPEC, HBM_SPEC, SEM_SPEC, SEM_SPEC, ANY), out_specs=(HBM_SPEC, HBM_SPEC), input_output_aliases={0: 0, 1: 1},
        compiler_params=pltpu.CompilerParams(has_side_effects=pltpu.SideEffectType.DATAFLOW_SIDE_EFFECTING),
    )(src_thru, land_thru, send_sems, recv_sems, after)[1]


def sum_blocks(blocks, *, name):
    n, R, C = blocks.shape
    tr = _largest_tile(R, 512, 16)

    def body(b_ref, o_ref):
        d = pl.program_id(1)
        v = b_ref[...].astype(F32)

        @pl.when(d == 0)
        def _():
            o_ref[...] = v

        @pl.when(d > 0)
        def _():
            o_ref[...] += v

    return pl.pallas_call(
        body, name=name, grid=(R // tr, n),
        in_specs=[pl.BlockSpec((None, tr, C), lambda i, d: (d, i, 0))], out_specs=pl.BlockSpec((tr, C), lambda i, d: (i, 0)),
        out_shape=jax.ShapeDtypeStruct((R, C), F32), compiler_params=_params("parallel", "arbitrary"),
    )(blocks)


WEIGHTS = ("norm_mix_g", "w_in", "b_forget", "lru_conv_w", "lru_conv_b", "lru_w_a", "lru_b_a", "lru_w_x", "lru_b_x", "lru_lambda",
           "w_out", "norm_cross_g", "norm_mem_g", "w_cq", "w_ck", "w_cv", "w_co", "norm_ffn_g", "w_up", "ffn_conv_w", "ffn_conv_b",
           "w_down", "rel_bias", "final_norm_g")
LARGE = ("w_in", "w_out", "w_cq", "w_ck", "w_cv", "w_co", "w_up", "w_down")
COLUMN_SPLIT_SMALL = ("lru_conv_w", "ffn_conv_w")
PACK = (("qkv", 128, 2304), ("aux", 128, 640), ("out", 128, 1024), ("cq", 128, 256), ("ckv", 128, 512), ("coT", 128, 256),
        ("upT", 704, 1024), ("down", 352, 1024))
PACK_W = 1024


def _pack_rows(parts):
    return jnp.concatenate([p.reshape(-1, PACK_W) for p in parts], axis=0)


def _pad_rows(flat, mult=8 * LANES):
    n = flat.shape[0]
    return jnp.pad(flat, (0, (-n) % mult)).reshape(-1, LANES)


def kernel(x, mem, norm_mix_g, w_in, b_forget, lru_conv_w, lru_conv_b, lru_w_a, lru_b_a, lru_w_x, lru_b_x, lru_lambda, w_out, norm_cross_g, norm_mem_g, w_cq, w_ck, w_cv, w_co, norm_ffn_g, w_up, ffn_conv_w, ffn_conv_b, w_down, rel_bias, final_norm_g, loss_target, m_norm_mix_g, m_w_in, m_b_forget, m_lru_conv_w, m_lru_conv_b, m_lru_w_a, m_lru_b_a, m_lru_w_x, m_lru_b_x, m_lru_lambda, m_w_out, m_norm_cross_g, m_norm_mem_g, m_w_cq, m_w_ck, m_w_cv, m_w_co, m_norm_ffn_g, m_w_up, m_ffn_conv_w, m_ffn_conv_b, m_w_down, m_rel_bias, m_final_norm_g, v_norm_mix_g, v_w_in, v_b_forget, v_lru_conv_w, v_lru_conv_b, v_lru_w_a, v_lru_b_a, v_lru_w_x, v_lru_b_x, v_lru_lambda, v_w_out, v_norm_cross_g, v_norm_mem_g, v_w_cq, v_w_ck, v_w_cv, v_w_co, v_norm_ffn_g, v_w_up, v_ffn_conv_w, v_ffn_conv_b, v_w_down, v_rel_bias, v_final_norm_g):
    w = dict(norm_mix_g=norm_mix_g, w_in=w_in, b_forget=b_forget, lru_conv_w=lru_conv_w, lru_conv_b=lru_conv_b, lru_w_a=lru_w_a,
             lru_b_a=lru_b_a, lru_w_x=lru_w_x, lru_b_x=lru_b_x, lru_lambda=lru_lambda, w_out=w_out, norm_cross_g=norm_cross_g,
             norm_mem_g=norm_mem_g, w_cq=w_cq, w_ck=w_ck, w_cv=w_cv, w_co=w_co, norm_ffn_g=norm_ffn_g, w_up=w_up,
             ffn_conv_w=ffn_conv_w, ffn_conv_b=ffn_conv_b, w_down=w_down, rel_bias=rel_bias, final_norm_g=final_norm_g)
    m = dict(norm_mix_g=m_norm_mix_g, w_in=m_w_in, b_forget=m_b_forget, lru_conv_w=m_lru_conv_w, lru_conv_b=m_lru_conv_b,
             lru_w_a=m_lru_w_a, lru_b_a=m_lru_b_a, lru_w_x=m_lru_w_x, lru_b_x=m_lru_b_x, lru_lambda=m_lru_lambda, w_out=m_w_out,
             norm_cross_g=m_norm_cross_g, norm_mem_g=m_norm_mem_g, w_cq=m_w_cq, w_ck=m_w_ck, w_cv=m_w_cv, w_co=m_w_co,
             norm_ffn_g=m_norm_ffn_g, w_up=m_w_up, ffn_conv_w=m_ffn_conv_w, ffn_conv_b=m_ffn_conv_b, w_down=m_w_down,
             rel_bias=m_rel_bias, final_norm_g=m_final_norm_g)
    v = dict(norm_mix_g=v_norm_mix_g, w_in=v_w_in, b_forget=v_b_forget, lru_conv_w=v_lru_conv_w, lru_conv_b=v_lru_conv_b,
             lru_w_a=v_lru_w_a, lru_b_a=v_lru_b_a, lru_w_x=v_lru_w_x, lru_b_x=v_lru_b_x, lru_lambda=v_lru_lambda, w_out=v_w_out,
             norm_cross_g=v_norm_cross_g, norm_mem_g=v_norm_mem_g, w_cq=v_w_cq, w_ck=v_w_ck, w_cv=v_w_cv, w_co=v_w_co,
             norm_ffn_g=v_norm_ffn_g, w_up=v_w_up, ffn_conv_w=v_ffn_conv_w, ffn_conv_b=v_ffn_conv_b, w_down=v_w_down,
             rel_bias=v_rel_bias, final_norm_g=v_final_norm_g)
    me = 4 * lax.axis_index("x") + 2 * lax.axis_index("y") + lax.axis_index("c")

    conv_shard = jnp.concatenate([w[n].reshape(-1) for n in COLUMN_SPLIT_SMALL])
    conv_all, conv_gathered = allgather_small(_pad_rows(conv_shard), name="gather_conv")
    conv_all = conv_all.reshape(N_DEV, -1)
    full = dict(w)
    off = 0
    for n in COLUMN_SPLIT_SMALL:
        d, k, c = w[n].shape
        blocks = conv_all[:, off:off + d * k * c].reshape(N_DEV, d, k, c)
        full[n] = blocks.transpose(1, 2, 0, 3).reshape(d, k, N_DEV * c)
        off += d * k * c

    MIX, FFN = PACK[:6], PACK[6:]

    def packed_shard(l, group):
        canon = canonical_weights(*[w[n][l] for n in LARGE])
        return _pack_rows([canon[k].astype(BF16) for k, _, _ in group])

    def unpack_weights(packed, group):
        W, row = {}, 0
        for k, r, c in group:
            n_rows = r * c // PACK_W
            W[k] = packed[:, row:row + n_rows].reshape(N_DEV * r, c)
            row += n_rows
        if "upT" in W:
            upT = W.pop("upT")
            W["up_u"], W["up_g"] = upT[:D_FF], upT[D_FF:]
        return W

    def packed_grads(gW, group):
        g = dict(gW)
        if "up_u" in g:
            g["upT"] = jnp.concatenate([g.pop("up_u"), g.pop("up_g")], axis=0)
        return jnp.concatenate([g[k].reshape(N_DEV, r * c // PACK_W, PACK_W) for k, r, c in group], axis=1)

    def unpack_grads(shard_sum, group):
        g, row = {}, 0
        for k, r, c in group:
            n_rows = r * c // PACK_W
            g[k] = shard_sum[row:row + n_rows].reshape(r, c)
            row += n_rows
        return g

    def own_block_in(landed, block):
        return lax.dynamic_update_slice(landed, block[None], (me, 0, 0))

    def gathered_weights(copies, shard, after, group, name):
        return unpack_weights(own_block_in(exchange_wait(copies, after, False, name=name), shard), group)

    def scattered_sum(src, copies, after, tag):
        landed = exchange_wait(copies, after, True, name=f"{tag}_wait")
        mine = lax.dynamic_index_in_dim(src, me, axis=0, keepdims=False)
        return sum_blocks(own_block_in(landed, mine), name=f"{tag}_sum")

    last = DEPTH - 1
    mix0, gathered = allgather_hbm(packed_shard(0, MIX) + conv_gathered[0, 0].astype(BF16), me, name="gather_weights")
    ffn0_shard = packed_shard(0, FFN) + gathered[0, 0].astype(BF16)
    gather_ffn0 = exchange_start(ffn0_shard, False, name="gather_ffn0_start")
    last_shard = packed_shard(last, PACK) + gather_ffn0[4][0, 0].astype(BF16)
    gather_last = exchange_start(last_shard, False, name="gather_last_start")
    started = gather_last[4][0, 0]

    def weights_of(l, h):
        if l == 0:
            W = unpack_weights(mix0, MIX)
            W["ffn"] = lambda after: gathered_weights(gather_ffn0, ffn0_shard, after, FFN, "gather_ffn0_wait")
            return W
        assert l == last
        return gathered_weights(gather_last, last_shard, h, PACK, "gather_last_wait")

    in_flight = {}

    def scatter(key, g_all, dx, name):
        in_flight[key] = (g_all, exchange_start(g_all, True, name=name))
        return dx + in_flight[key][1][4][0, 0]

    def grads_done(l, gW, dh):
        return scatter("last", packed_grads(gW, PACK), dh, "grads_last_start") if l == last else dh

    def ffn_grads_done(l, gW, dx):
        if l != 0:
            return dx
        return scatter("ffn0", packed_grads({k: gW[k] for k in ("up_u", "up_g", "down")}, FFN), dx, "grads_ffn0_start")

    Ps = [small_params(full, l) for l in range(DEPTH)]
    Ps[0]["norm_mix_g"] = Ps[0]["norm_mix_g"] + started
    loss, grad_x, gWs, gPs, d_rel, d_final = local_step(x, mem, loss_target, weights_of, Ps, rel_bias,
                                                        final_norm_g.reshape(1, -1), grads_done, ffn_grads_done)

    shard_grads = {last: unpack_grads(scattered_sum(*in_flight["last"], grad_x, "grads_last"), PACK)}
    shard_grads[0] = unpack_grads(scattered_sum(*in_flight["ffn0"], grad_x, "grads_ffn0"), FFN)

    g_all = packed_grads({k: gWs[0][k] for k, _, _ in MIX}, MIX)
    rows = g_all.shape[1]
    got = pair_exchange(g_all, name="grads_pair_exchange")
    own = lax.dynamic_index_in_dim(g_all.reshape(N_CHIPS, 2, rows, PACK_W), lax.axis_index("c"), axis=1, keepdims=False)
    pair = sum_cast([own.reshape(-1, PACK_W), got.reshape(-1, PACK_W)], GRAD_WIRE, name="grads_pair_sum").reshape(N_CHIPS, rows, PACK_W)
    from_x, from_y, from_xy = chip_exchange(pair, name="grads_chip_exchange")
    mine = lax.dynamic_index_in_dim(pair, 2 * lax.axis_index("x") + lax.axis_index("y"), axis=0, keepdims=False)
    shard_grads[0].update(unpack_grads(sum_cast([mine, from_x, from_y, from_xy], F32, name="grads_chip_sum"), MIX))

    grads = {}
    per_layer = [native_grads(shard_grads[l]) for l in range(DEPTH)]
    for i, n in enumerate(LARGE):
        grads[n] = jnp.stack([per_layer[l][i] for l in range(DEPTH)])

    small_names = [n for n in WEIGHTS if n not in LARGE and n not in ("rel_bias", "final_norm_g")]
    pieces = [gPs[l][n].reshape(-1) for n in small_names for l in range(DEPTH)] + [d_rel.reshape(-1), d_final.reshape(-1), loss[0, :1]]
    sizes = [p.shape[0] for p in pieces]
    _, total = allgather_small(_pad_rows(jnp.concatenate(pieces)), name="allreduce_small", reduce=True)
    total = total.reshape(-1)
    off, it = 0, iter(sizes)
    for n in small_names:
        per = []
        for l in range(DEPTH):
            sz = next(it)
            per.append(total[off:off + sz])
            off += sz
        full_shape = (DEPTH,) + full[n].shape[1:]
        gfull = jnp.stack(per).reshape(full_shape)
        if n in COLUMN_SPLIT_SMALL:
            c = w[n].shape[-1]
            gfull = lax.dynamic_slice_in_dim(gfull, me * c, c, axis=gfull.ndim - 1)
        grads[n] = gfull
    grads["rel_bias"] = total[off:off + rel_bias.size].reshape(rel_bias.shape)
    off += rel_bias.size
    grads["final_norm_g"] = total[off:off + D_MODEL]
    off += D_MODEL
    loss_out = total[off]

    delta, new_m, new_v = {}, {}, {}
    for n in LARGE:
        shape = w[n].shape
        two_d = lambda a: a.reshape(-1, shape[-1])
        d_, m_, v_ = adamw(two_d(w[n]), two_d(grads[n]), two_d(m[n]), two_d(v[n]), name=f"adamw_{n}")
        delta[n], new_m[n], new_v[n] = d_.reshape(shape), m_.reshape(shape), v_.reshape(shape)
    small_all = [n for n in WEIGHTS if n not in LARGE]
    two_d = lambda a: a.reshape(-1, a.shape[-1])
    d_, m_, v_ = adamw_many(*[[two_d(src[n]) for n in small_all] for src in (w, grads, m, v)], name="adamw_small")
    for i, n in enumerate(small_all):
        delta[n], new_m[n], new_v[n] = (a[i].reshape(w[n].shape) for a in (d_, m_, v_))

    return (loss_out, grad_x, *[grads[n] for n in WEIGHTS], *[delta[n] for n in WEIGHTS], *[new_m[n] for n in WEIGHTS],
            *[new_v[n] for n in WEIGHTS])
```

```python
import functools
import math

import numpy as np
import jax
import jax.numpy as jnp
from jax import lax
from jax.experimental import pallas as pl
from jax.experimental.pallas import tpu as pltpu

F32 = jnp.float32
BF16 = jnp.bfloat16
MESH = pl.DeviceIdType.MESH

N_DEV = 8
D_MODEL = 1024
SEQ = 2048
DEPTH = 2
HEAD_DIM = 64
N_HEADS = 4
GROUP_W = N_HEADS * HEAD_DIM
D_FF = 2816
N_MEM = 256
NUM_BUCKETS = 32
MAX_DISTANCE = 2048
BLOCK = 128
DILATIONS = (1, 4, 16)
EPS = 1e-6
LRU_C = 8.0
Q_SCALE = HEAD_DIM ** -0.5
AUX_W = 640
LRU_HALF_W = 128
LRU_HALVES = GROUP_W // LRU_HALF_W
ADAM_LR, ADAM_B1, ADAM_B2, ADAM_EPS, ADAM_WD, ADAM_STEP = 0.001, 0.9, 0.999, 1e-08, 0.01, 10

VMEM_LIMIT_V7X = 48 * 1024 * 1024


def _params(*sem):
    return pltpu.CompilerParams(dimension_semantics=sem if sem else None, vmem_limit_bytes=VMEM_LIMIT_V7X)


def _pick(n, cands):
    for c in cands:
        if n % c == 0:
            return c
    return n


def _largest_tile(n, cap, align):
    best = None
    for t in range(align, min(n, cap) + 1, align):
        if n % t == 0:
            best = t
    return n if best is None else best


def matmul(a, b, *, name, trans_a=False, trans_b=False, out_dtype=F32, residual=None):
    (K, M) = a.shape if trans_a else a.shape[::-1]
    (N, Kb) = b.shape if trans_b else b.shape[::-1]
    assert K == Kb, (a.shape, b.shape)
    tm = _largest_tile(M, 1408 if trans_a else (1024 if K <= 1024 else 512), 128)
    tn = _largest_tile(N, 1408, 128)
    tk = _largest_tile(K, 1024 if trans_a else 2816, 128)
    nk = K // tk
    a_spec = pl.BlockSpec((tk, tm), lambda i, j, k: (k, i)) if trans_a else pl.BlockSpec((tm, tk), lambda i, j, k: (i, k))
    b_spec = pl.BlockSpec((tn, tk), lambda i, j, k: (j, k)) if trans_b else pl.BlockSpec((tk, tn), lambda i, j, k: (k, j))
    o_spec = pl.BlockSpec((tm, tn), lambda i, j, k: (i, j))
    dims = (((0 if trans_a else 1,), (1 if trans_b else 0,)), ((), ()))
    has_res = residual is not None

    def body(*refs):
        a_ref, b_ref = refs[0], refs[1]
        r_ref = refs[2] if has_res else None
        part = lax.dot_general(a_ref[...].astype(BF16), b_ref[...].astype(BF16), dims, preferred_element_type=F32)
        if nk == 1:
            if has_res:
                part = part + r_ref[...].astype(F32)
            refs[-1][...] = part.astype(out_dtype)
            return
        o_ref, acc_ref = refs[-2], refs[-1]
        k = pl.program_id(2)

        @pl.when(k == 0)
        def _():
            acc_ref[...] = part

        @pl.when(k > 0)
        def _():
            acc_ref[...] += part

        @pl.when(k == nk - 1)
        def _():
            r = acc_ref[...]
            if has_res:
                r = r + r_ref[...].astype(F32)
            o_ref[...] = r.astype(out_dtype)

    ops = (a, b) + ((residual,) if has_res else ())
    return pl.pallas_call(
        body, name=name, grid=(M // tm, N // tn, nk),
        in_specs=[a_spec, b_spec] + ([o_spec] if has_res else []),
        out_specs=o_spec, out_shape=jax.ShapeDtypeStruct((M, N), out_dtype),
        scratch_shapes=[pltpu.VMEM((tm, tn), F32)] if nk > 1 else [],
        compiler_params=_params("parallel", "parallel", "arbitrary"),
    )(*ops)


def rmsnorm_fwd(x, g, *, name):
    R, D = x.shape
    tr = _pick(R, (512, 256))

    def body(x_ref, g_ref, o_ref):
        xv = x_ref[...]
        r = lax.rsqrt(jnp.mean(xv * xv, axis=-1, keepdims=True) + EPS)
        o_ref[...] = (xv * r * g_ref[...]).astype(BF16)

    return pl.pallas_call(
        body, name=name, grid=(R // tr,),
        in_specs=[pl.BlockSpec((tr, D), lambda i: (i, 0)), pl.BlockSpec((1, D), lambda i: (0, 0))],
        out_specs=pl.BlockSpec((tr, D), lambda i: (i, 0)), out_shape=jax.ShapeDtypeStruct((R, D), BF16),
        compiler_params=_params("parallel"),
    )(x, g)


def rmsnorm_bwd(x, g, dh, dres, *, name):
    R, D = x.shape
    tr = _pick(R, (512, 256))
    has_res = dres is not None

    def body(*refs):
        x_ref, g_ref, dh_ref = refs[:3]
        dx_ref, dg_ref = refs[-2], refs[-1]
        xv = x_ref[...]
        r = lax.rsqrt(jnp.mean(xv * xv, axis=-1, keepdims=True) + EPS)
        n = xv * r
        dhv = dh_ref[...]
        dn = dhv * g_ref[...]
        dx = r * (dn - n * jnp.mean(dn * n, axis=-1, keepdims=True))
        if has_res:
            dx = dx + refs[3][...]
        dx_ref[...] = dx
        part = jnp.sum(dhv * n, axis=0, keepdims=True)

        @pl.when(pl.program_id(0) == 0)
        def _():
            dg_ref[...] = part

        @pl.when(pl.program_id(0) > 0)
        def _():
            dg_ref[...] += part

    row = pl.BlockSpec((tr, D), lambda i: (i, 0))
    vec = pl.BlockSpec((1, D), lambda i: (0, 0))
    ops = (x, g, dh) + ((dres,) if has_res else ())
    return pl.pallas_call(
        body, name=name, grid=(R // tr,),
        in_specs=[row, vec, row] + ([row] if has_res else []),
        out_specs=[row, vec],
        out_shape=[jax.ShapeDtypeStruct((R, D), F32), jax.ShapeDtypeStruct((1, D), F32)],
        compiler_params=_params("arbitrary"),
    )(*ops)


_SQRT_HALF = 0.7071067811865476
_INV_SQRT_2PI = 0.3989422804014327


def _normal_cdf_pdf(x):
    ax = jnp.abs(x) * _SQRT_HALF
    t = 1.0 / (1.0 + 0.3275911 * ax)
    poly = t * (0.254829592 + t * (-0.284496736 + t * (1.421413741 + t * (-1.453152027 + t * 1.061405429))))
    e = jnp.exp(-0.5 * x * x)
    half_tail = 0.5 * poly * e
    return jnp.where(x < 0, half_tail, 1.0 - half_tail), e


def _gelu_cdf(x):
    return _normal_cdf_pdf(x)[0]


def _gelu_and_grad(x):
    cdf, e = _normal_cdf_pdf(x)
    return x * cdf, cdf + x * _INV_SQRT_2PI * e


def _shift_down(main, halo, first, shifts):
    halo = jnp.where(first, 0.0, halo)
    ext = jnp.concatenate([halo, main], axis=0)
    return [pltpu.roll(ext, s, 0)[8:] for s in shifts]


def _conv3(main, halo, first, w, b):
    m1, m2 = _shift_down(main, halo, first, (1, 2))
    return ((b + w[0:1] * m2) + w[1:2] * m1) + w[2:3] * main, m1, m2


def glu_fwd(hu, hg, wu, wg, bu, bg, *, name):
    T, F = hu.shape
    tm, tf = 512, _largest_tile(F, 704, 128)
    hb = tm // 8
    blocks_per_example = SEQ // tm

    def body(hu_ref, hg_ref, hau_ref, hag_ref, wu_ref, wg_ref, bu_ref, bg_ref, o_ref):
        first = pl.program_id(0) % blocks_per_example == 0
        up, _, _ = _conv3(hu_ref[...], hau_ref[...], first, wu_ref[...], bu_ref[...])
        gate, _, _ = _conv3(hg_ref[...], hag_ref[...], first, wg_ref[...], bg_ref[...])
        o_ref[...] = (gate * _gelu_cdf(gate) * up).astype(BF16)

    main = pl.BlockSpec((tm, tf), lambda i, j: (i, j))
    halo = pl.BlockSpec((8, tf), lambda i, j: (jnp.maximum(i * hb - 1, 0), j))
    w3 = pl.BlockSpec((3, tf), lambda i, j: (0, j))
    b1 = pl.BlockSpec((1, tf), lambda i, j: (0, j))
    return pl.pallas_call(
        body, name=name, grid=(T // tm, F // tf),
        in_specs=[main, main, halo, halo, w3, w3, b1, b1],
        out_specs=main, out_shape=jax.ShapeDtypeStruct((T, F), BF16),
        compiler_params=_params("parallel", "parallel"),
    )(hu, hg, hu, hg, wu, wg, bu, bg)


def glu_bwd(hu, hg, dact, wu, wg, bu, bg, *, name):
    T, F = hu.shape
    tm, tf = 512, _largest_tile(F, 704, 128)
    hb = tm // 8
    blocks_per_example = SEQ // tm
    n_halo_blocks = T // 8
    n_ext = tm + 8

    def body(hu_ref, hg_ref, hau_ref, hag_ref, hnu_ref, hng_ref, da_ref, dan_ref, wu_ref, wg_ref, bu_ref, bg_ref,
             du_ref, dg_ref, dwu_ref, dwg_ref, dbu_ref, dbg_ref):
        i = pl.program_id(1)
        first = i % blocks_per_example == 0
        last = i % blocks_per_example == blocks_per_example - 1
        wu, wg = wu_ref[...], wg_ref[...]

        def conv_ext(main_ref, prev_ref, next_ref, w, b):
            ext = jnp.concatenate([jnp.where(first, 0.0, prev_ref[...]), main_ref[...], next_ref[...]], axis=0)
            x0, x1, x2 = ext[8:], pltpu.roll(ext, 1, 0)[8:], pltpu.roll(ext, 2, 0)[8:]
            return ((b + w[0:1] * x2) + w[1:2] * x1) + w[2:3] * x0, x0, x1, x2

        up, xu, u1, u2 = conv_ext(hu_ref, hau_ref, hnu_ref, wu, bu_ref[...])
        gate, xg, g1, g2 = conv_ext(hg_ref, hag_ref, hng_ref, wg, bg_ref[...])
        act, dact_dgate = _gelu_and_grad(gate)
        da = jnp.concatenate([da_ref[...], jnp.where(last, 0.0, dan_ref[...])], axis=0)
        dup = da * act
        dgate = da * up * dact_dgate

        def conv_t(d, w):
            return (w[2:3] * d[:tm] + w[1:2] * pltpu.roll(d, n_ext - 1, 0)[:tm] + w[0:1] * pltpu.roll(d, n_ext - 2, 0)[:tm]).astype(BF16)

        du_ref[...] = conv_t(dup, wu)
        dg_ref[...] = conv_t(dgate, wg)

        def sums(d, x0, x1, x2):
            s = lambda v: jnp.sum(v[:tm], axis=0, keepdims=True)
            return jnp.concatenate([s(d * x2), s(d * x1), s(d * x0)], axis=0), s(d)

        pwu, pbu = sums(dup, xu, u1, u2)
        pwg, pbg = sums(dgate, xg, g1, g2)

        @pl.when(i == 0)
        def _():
            dwu_ref[...] = pwu
            dwg_ref[...] = pwg
            dbu_ref[...] = pbu
            dbg_ref[...] = pbg

        @pl.when(i > 0)
        def _():
            dwu_ref[...] += pwu
            dwg_ref[...] += pwg
            dbu_ref[...] += pbu
            dbg_ref[...] += pbg

    main = pl.BlockSpec((tm, tf), lambda j, i: (i, j))
    before = pl.BlockSpec((8, tf), lambda j, i: (jnp.maximum(i * hb - 1, 0), j))
    after = pl.BlockSpec((8, tf), lambda j, i: (jnp.minimum((i + 1) * hb, n_halo_blocks - 1), j))
    w3 = pl.BlockSpec((3, tf), lambda j, i: (0, j))
    b1 = pl.BlockSpec((1, tf), lambda j, i: (0, j))
    sd = jax.ShapeDtypeStruct
    return pl.pallas_call(
        body, name=name, grid=(F // tf, T // tm),
        in_specs=[main, main, before, before, after, after, main, after, w3, w3, b1, b1],
        out_specs=[main, main, w3, w3, b1, b1],
        out_shape=[sd((T, F), BF16), sd((T, F), BF16), sd((3, F), F32), sd((3, F), F32), sd((1, F), F32), sd((1, F), F32)],
        compiler_params=_params("parallel", "arbitrary"),
    )(hu, hg, hu, hg, hu, hg, dact, dact, wu, wg, bu, bg)


def loss_head(x, g, target, *, name):
    T, D = x.shape
    tr = 256

    def body(x_ref, g_ref, t_ref, loss_ref, dx_ref, dg_ref):
        xv = x_ref[...]
        gv = g_ref[...]
        r = lax.rsqrt(jnp.mean(xv * xv, axis=-1, keepdims=True) + EPS)
        n = xv * r
        err = n * gv - t_ref[...]
        part_loss = jnp.zeros((1, 128), F32) + 0.5 * jnp.sum(jnp.mean(err * err, axis=-1, keepdims=True))
        dy = err * (1.0 / D)
        dn = dy * gv
        dx_ref[...] = r * (dn - n * jnp.mean(dn * n, axis=-1, keepdims=True))
        part_g = jnp.sum(dy * n, axis=0, keepdims=True)

        @pl.when(pl.program_id(0) == 0)
        def _():
            loss_ref[...] = part_loss
            dg_ref[...] = part_g

        @pl.when(pl.program_id(0) > 0)
        def _():
            loss_ref[...] += part_loss
            dg_ref[...] += part_g

    row = pl.BlockSpec((tr, D), lambda i: (i, 0))
    vec = pl.BlockSpec((1, D), lambda i: (0, 0))
    sd = jax.ShapeDtypeStruct
    return pl.pallas_call(
        body, name=name, grid=(T // tr,),
        in_specs=[row, vec, row],
        out_specs=[pl.BlockSpec((1, 128), lambda i: (0, 0)), row, vec],
        out_shape=[sd((1, 128), F32), sd((T, D), F32), sd((1, D), F32)],
        compiler_params=_params("arbitrary"),
    )(x, g, target)


def adamw(w, g, m, v, *, name):
    R, C = w.shape
    tr = _pick(R, (256, 128, 64, 32, 16, 8))

    def body(w_ref, g_ref, m_ref, v_ref, d_ref, nm_ref, nv_ref):
        gv = g_ref[...]
        mn = ADAM_B1 * m_ref[...] + (1.0 - ADAM_B1) * gv
        vn = ADAM_B2 * v_ref[...] + (1.0 - ADAM_B2) * (gv * gv)
        m_hat = mn / (1.0 - ADAM_B1 ** ADAM_STEP)
        v_hat = vn / (1.0 - ADAM_B2 ** ADAM_STEP)
        d_ref[...] = -ADAM_LR * (m_hat / (jnp.sqrt(v_hat) + ADAM_EPS) + ADAM_WD * w_ref[...])
        nm_ref[...] = mn
        nv_ref[...] = vn

    blk = pl.BlockSpec((tr, C), lambda i: (i, 0))
    sd = jax.ShapeDtypeStruct((R, C), F32)
    return pl.pallas_call(
        body, name=name, grid=(R // tr,), in_specs=[blk] * 4, out_specs=[blk] * 3, out_shape=[sd] * 3,
        compiler_params=_params("parallel"),
    )(w, g, m, v)


def adamw_many(ws, gs, ms, vs, *, name):
    n = len(ws)

    def body(*refs):
        ins, outs = refs[:4 * n], refs[4 * n:]
        for i in range(n):
            w_ref, g_ref, m_ref, v_ref = ins[i], ins[n + i], ins[2 * n + i], ins[3 * n + i]
            gv = g_ref[...]
            mn = ADAM_B1 * m_ref[...] + (1.0 - ADAM_B1) * gv
            vn = ADAM_B2 * v_ref[...] + (1.0 - ADAM_B2) * (gv * gv)
            m_hat = mn / (1.0 - ADAM_B1 ** ADAM_STEP)
            v_hat = vn / (1.0 - ADAM_B2 ** ADAM_STEP)
            outs[i][...] = -ADAM_LR * (m_hat / (jnp.sqrt(v_hat) + ADAM_EPS) + ADAM_WD * w_ref[...])
            outs[n + i][...] = mn
            outs[2 * n + i][...] = vn

    vm = pl.BlockSpec(memory_space=pltpu.VMEM)
    shapes = [jax.ShapeDtypeStruct(w.shape, F32) for w in ws]
    res = pl.pallas_call(
        body, name=name, in_specs=[vm] * (4 * n), out_specs=[vm] * (3 * n), out_shape=shapes * 3, compiler_params=_params(),
    )(*ws, *gs, *ms, *vs)
    return res[:n], res[n:2 * n], res[2 * n:]


def _softplus(x):
    return jnp.maximum(x, 0.0) + jnp.log(1.0 + jnp.exp(-jnp.abs(x)))


def _lru_gates(x, cw, cb, wa, ba, wx, bx, lam):
    S = x.shape[0]
    row = lax.broadcasted_iota(jnp.int32, (S, 1), 0)

    def back(s):
        return jnp.where(row >= s, pltpu.roll(x, s, 0), 0.0)

    xc = (((cb + cw[0:1] * back(3)) + cw[1:2] * back(2)) + cw[2:3] * back(1)) + cw[3:4] * x
    xb = xc.astype(BF16)
    r = jax.nn.sigmoid(jnp.dot(xb, wa, preferred_element_type=F32) + ba)
    ig = jax.nn.sigmoid(jnp.dot(xb, wx, preferred_element_type=F32) + bx)
    sp = _softplus(-lam)
    la = -LRU_C * r * sp
    a = jnp.exp(la)
    y = 2.0 * la
    one_minus_a2 = jnp.where(y > -0.05, -y * (1.0 + y * (0.5 + y * (1.0 / 6.0 + y * (1.0 / 24.0)))), 1.0 - jnp.exp(y))
    mm = jnp.sqrt(one_minus_a2)
    return xc, xb, r, ig, sp, a, mm


def lru_fwd(aux, cw, cb, wa, ba, wx, bx, lam, *, name):
    T = aux.shape[0]
    S, C = SEQ, LRU_HALF_W

    def body(x_ref, g_ref, cw_ref, cb_ref, wa_ref, ba_ref, wx_ref, bx_ref, lam_ref, o_ref, h_ref, a_s, u_s):
        xc, _, r, ig, sp, a, mm = _lru_gates(x_ref[...], cw_ref[...], cb_ref[...], wa_ref[...], ba_ref[...],
                                             wx_ref[...], bx_ref[...], lam_ref[...])
        a_s[...] = a
        u_s[...] = mm * (ig * xc)

        def group(i, h):
            base = pl.multiple_of(i * 8, 8)
            a8 = a_s[pl.ds(base, 8), :]
            u8 = u_s[pl.ds(base, 8), :]
            for rr in range(8):
                h = a8[rr:rr + 1] * h + u8[rr:rr + 1]
                h_ref[pl.ds(base + rr, 1), :] = h
            return h

        lax.fori_loop(0, S // 8, group, jnp.zeros((1, C), F32))
        gate = g_ref[...]
        o_ref[...] = (h_ref[...] * (gate * _gelu_cdf(gate))).astype(BF16)

    blk = lambda col: pl.BlockSpec((S, C), lambda c, b: (b, col + c))
    par = lambda rows: pl.BlockSpec((rows, C), lambda c, b: (0, c))
    sq = pl.BlockSpec((None, C, C), lambda c, b: (c, 0, 0))
    sd = jax.ShapeDtypeStruct
    W = LRU_HALVES * C
    return pl.pallas_call(
        body, name=name, grid=(LRU_HALVES, T // S),
        in_specs=[blk(0), blk(LRU_HALVES), par(4), par(1), sq, par(1), sq, par(1), par(1)],
        out_specs=[blk(0), blk(0)], out_shape=[sd((T, W), BF16), sd((T, W), F32)],
        scratch_shapes=[pltpu.VMEM((S, C), F32), pltpu.VMEM((S, C), F32)],
        compiler_params=_params("parallel", "parallel"),
    )(aux, aux, cw, cb, wa, ba, wx, bx, lam)


def lru_bwd(aux, h, dmixed, cw, cb, wa, ba, wx, bx, lam, *, name):
    T = aux.shape[0]
    S, C = SEQ, LRU_HALF_W

    def body(x_ref, g_ref, h_ref, do_ref, cw_ref, cb_ref, wa_ref, ba_ref, wx_ref, bx_ref, lam_ref,
             dx_ref, dgate_ref, dcw_ref, dcb_ref, dwa_ref, dba_ref, dwx_ref, dbx_ref, dlam_ref, a_s, d_s):
        x = x_ref[...]
        cw = cw_ref[...]
        lam = lam_ref[...]
        xc, xb, r, ig, sp, a, mm = _lru_gates(x, cw, cb_ref[...], wa_ref[...], ba_ref[...], wx_ref[...], bx_ref[...], lam)
        gate = g_ref[...]
        gl, dgl = _gelu_and_grad(gate)
        dout = do_ref[...]
        hv = h_ref[...]
        dgate_ref[...] = dout * hv * dgl
        a_s[...] = a
        d_s[...] = dout * gl

        def group(i, c):
            base = pl.multiple_of((S // 8 - 1 - i) * 8, 8)
            a8 = a_s[pl.ds(base, 8), :]
            d8 = d_s[pl.ds(base, 8), :]
            for rr in range(7, -1, -1):
                d = d8[rr:rr + 1] + c
                d_s[pl.ds(base + rr, 1), :] = d
                c = a8[rr:rr + 1] * d
            return c

        lax.fori_loop(0, S // 8, group, jnp.zeros((1, C), F32))
        row = lax.broadcasted_iota(jnp.int32, (S, 1), 0)
        dht = d_s[...]
        h_prev = jnp.where(row >= 1, pltpu.roll(hv, 1, 0), 0.0)
        da = dht * h_prev
        gx = ig * xc
        dmm = dht * gx
        dig = dht * mm * xc
        dxc = dht * mm * ig
        dla = da * a - dmm * (a * a) / mm
        dr = dla * (-LRU_C * sp)
        dsp = jnp.sum(dla * (-LRU_C * r), axis=0, keepdims=True)
        dlam = dsp * (-jax.nn.sigmoid(-lam))
        dpa = dr * r * (1.0 - r)
        dpx = dig * ig * (1.0 - ig)
        dpa_b, dpx_b = dpa.astype(BF16), dpx.astype(BF16)
        nt = (((1,), (1,)), ((), ()))
        tn = (((0,), (0,)), ((), ()))
        dxc = dxc + lax.dot_general(dpa_b, wa_ref[...], nt, preferred_element_type=F32) \
                  + lax.dot_general(dpx_b, wx_ref[...], nt, preferred_element_type=F32)
        dwa = lax.dot_general(xb, dpa_b, tn, preferred_element_type=F32)
        dwx = lax.dot_general(xb, dpx_b, tn, preferred_element_type=F32)

        def fwd(v, s):
            return jnp.where(row < S - s, pltpu.roll(v, S - s, 0), 0.0)

        def back(v, s):
            return jnp.where(row >= s, pltpu.roll(v, s, 0), 0.0)

        dx_ref[...] = cw[3:4] * dxc + cw[2:3] * fwd(dxc, 1) + cw[1:2] * fwd(dxc, 2) + cw[0:1] * fwd(dxc, 3)
        s0 = lambda v: jnp.sum(v, axis=0, keepdims=True)
        dcw = jnp.concatenate([s0(dxc * back(x, 3)), s0(dxc * back(x, 2)), s0(dxc * back(x, 1)), s0(dxc * x)], axis=0)
        parts = ((dcw_ref, dcw), (dcb_ref, s0(dxc)), (dwa_ref, dwa), (dba_ref, s0(dpa)), (dwx_ref, dwx),
                 (dbx_ref, s0(dpx)), (dlam_ref, dlam))

        @pl.when(pl.program_id(1) == 0)
        def _():
            for ref, val in parts:
                ref[...] = val

        @pl.when(pl.program_id(1) > 0)
        def _():
            for ref, val in parts:
                ref[...] += val

    blk = lambda col: pl.BlockSpec((S, C), lambda c, b: (b, col + c))
    par = lambda rows: pl.BlockSpec((rows, C), lambda c, b: (0, c))
    sq = pl.BlockSpec((None, C, C), lambda c, b: (c, 0, 0))
    sd = jax.ShapeDtypeStruct
    W = LRU_HALVES * C
    vec = sd((1, W), F32)
    return pl.pallas_call(
        body, name=name, grid=(LRU_HALVES, T // S),
        in_specs=[blk(0), blk(LRU_HALVES), blk(0), blk(3 * LRU_HALVES), par(4), par(1), sq, par(1), sq, par(1), par(1)],
        out_specs=[blk(0), blk(0), par(4), par(1), sq, par(1), sq, par(1), par(1)],
        out_shape=[sd((T, W), F32), sd((T, W), F32), sd((4, W), F32), vec, sd((LRU_HALVES, C, C), F32), vec,
                   sd((LRU_HALVES, C, C), F32), vec, vec],
        scratch_shapes=[pltpu.VMEM((S, C), F32), pltpu.VMEM((S, C), F32)],
        compiler_params=_params("parallel", "arbitrary"),
    )(aux, aux, h, dmixed, cw, cb, wa, ba, wx, bx, lam)


_NT = (((1,), (1,)), ((), ()))
_TN = (((0,), (0,)), ((), ()))


def _dot(a, b, dims=None):
    if dims is None:
        return jnp.dot(a, b, preferred_element_type=F32)
    return lax.dot_general(a, b, dims, preferred_element_type=F32)


def _hs(h):
    return slice(h * HEAD_DIM, (h + 1) * HEAD_DIM)


def cross_fwd(q, kv, *, name):
    T = q.shape[0]
    tq = 512

    def body(q_ref, kv_ref, o_ref):
        for h in range(N_HEADS):
            qh = q_ref[:, _hs(h)] * Q_SCALE
            k = kv_ref[:, _hs(h)]
            v = kv_ref[:, GROUP_W + h * HEAD_DIM:GROUP_W + (h + 1) * HEAD_DIM]
            s = _dot(qh, k, _NT)
            p = jnp.exp(s - jnp.max(s, axis=-1, keepdims=True))
            p = p / jnp.sum(p, axis=-1, keepdims=True)
            o_ref[:, _hs(h)] = _dot(p.astype(BF16), v).astype(BF16)

    per = SEQ // tq
    return pl.pallas_call(
        body, name=name, grid=(T // tq,),
        in_specs=[pl.BlockSpec((tq, GROUP_W), lambda i: (i, 0)), pl.BlockSpec((N_MEM, 2 * GROUP_W), lambda i: (i // per, 0))],
        out_specs=pl.BlockSpec((tq, GROUP_W), lambda i: (i, 0)), out_shape=jax.ShapeDtypeStruct((T, GROUP_W), BF16),
        compiler_params=_params("parallel"),
    )(q, kv)


def cross_bwd(q, kv, do, *, name):
    T = q.shape[0]
    tq = 512
    per = SEQ // tq

    def body(q_ref, kv_ref, do_ref, dq_ref, dkv_ref):
        first = pl.program_id(0) % per == 0
        for h in range(N_HEADS):
            vs = slice(GROUP_W + h * HEAD_DIM, GROUP_W + (h + 1) * HEAD_DIM)
            qh = q_ref[:, _hs(h)] * Q_SCALE
            k = kv_ref[:, _hs(h)]
            v = kv_ref[:, vs]
            doh = do_ref[:, _hs(h)].astype(BF16)
            s = _dot(qh, k, _NT)
            p = jnp.exp(s - jnp.max(s, axis=-1, keepdims=True))
            p = p / jnp.sum(p, axis=-1, keepdims=True)
            dp = _dot(doh, v, _NT)
            ds = (p * (dp - jnp.sum(p * dp, axis=-1, keepdims=True))).astype(BF16)
            dq_ref[:, _hs(h)] = (_dot(ds, k) * Q_SCALE).astype(BF16)
            dk = _dot(ds, qh, _TN)
            dv = _dot(p.astype(BF16), doh, _TN)

            @pl.when(first)
            def _():
                dkv_ref[:, _hs(h)] = dk
                dkv_ref[:, vs] = dv

            @pl.when(jnp.logical_not(first))
            def _():
                dkv_ref[:, _hs(h)] += dk
                dkv_ref[:, vs] += dv

    qb = pl.BlockSpec((tq, GROUP_W), lambda i: (i, 0))
    kvb = pl.BlockSpec((N_MEM, 2 * GROUP_W), lambda i: (i // per, 0))
    sd = jax.ShapeDtypeStruct
    return pl.pallas_call(
        body, name=name, grid=(T // tq,),
        in_specs=[qb, kvb, qb], out_specs=[qb, kvb],
        out_shape=[sd((T, GROUP_W), BF16), sd(kv.shape, F32)],
        compiler_params=_params("arbitrary"),
    )(q, kv, do)


NB = SEQ // BLOCK
NEG = -1e30
HEADS = tuple(range(N_HEADS))


def _blk(i):
    return pl.ds(pl.multiple_of(i * BLOCK, BLOCK), BLOCK)


def _qkv_specs(first_col):
    return [pl.BlockSpec((SEQ, GROUP_W), lambda b, c=first_col + j: (b, c)) for j in range(3)]


LANES = 128
CUM_BLK = 256


def col_to_row(c):
    b = c.shape[0] // SEQ
    return c.reshape(b, SEQ, LANES)[:, :, :8].transpose(0, 2, 1).reshape(b * 8, SEQ)


def row_to_col(r):
    b = r.shape[0] // 8
    c = r.reshape(b, 8, SEQ).transpose(0, 2, 1)
    return jnp.pad(c, ((0, 0), (0, 0), (0, LANES - 8))).reshape(b * SEQ, LANES)


def fox_prep(aux, bf, *, name):
    T = aux.shape[0]

    def body(f_ref, b_ref, o_ref):
        row = lax.broadcasted_iota(jnp.int32, (CUM_BLK, CUM_BLK), 0)
        col = lax.broadcasted_iota(jnp.int32, (CUM_BLK, CUM_BLK), 1)
        upto = (col <= row).astype(BF16)
        carry = jnp.zeros((1, LANES), F32)
        for n in range(SEQ // CUM_BLK):
            rows = slice(n * CUM_BLK, (n + 1) * CUM_BLK)
            logf = -_softplus(-(f_ref[rows, :] + b_ref[...]))
            hi = logf.astype(BF16)
            lo = (logf - hi.astype(F32)).astype(BF16)
            cum = _dot(upto, hi) + _dot(upto, lo) + carry
            o_ref[rows, :] = cum
            carry = cum[CUM_BLK - 1:CUM_BLK]

    return pl.pallas_call(
        body, name=name, grid=(T // SEQ,),
        in_specs=[pl.BlockSpec((SEQ, LANES), lambda b: (b, 4)), pl.BlockSpec((1, LANES), lambda b: (0, 0))],
        out_specs=pl.BlockSpec((SEQ, LANES), lambda b: (b, 0)), out_shape=jax.ShapeDtypeStruct((T, LANES), F32),
        compiler_params=_params("parallel"),
    )(aux, bf)


def fox_prep_bwd(aux, bf, dcum, *, name):
    T = aux.shape[0]

    def body(f_ref, b_ref, d_ref, df_ref, db_ref):
        row = lax.broadcasted_iota(jnp.int32, (CUM_BLK, CUM_BLK), 0)
        col = lax.broadcasted_iota(jnp.int32, (CUM_BLK, CUM_BLK), 1)
        onward = (col >= row).astype(BF16)
        carry = jnp.zeros((1, LANES), F32)
        tot = jnp.zeros((1, LANES), F32)
        for n in range(SEQ // CUM_BLK - 1, -1, -1):
            rows = slice(n * CUM_BLK, (n + 1) * CUM_BLK)
            d = d_ref[rows, :]
            hi = d.astype(BF16)
            lo = (d - hi.astype(F32)).astype(BF16)
            dlogf = _dot(onward, hi) + _dot(onward, lo) + carry
            carry = dlogf[0:1]
            df = dlogf * jax.nn.sigmoid(-(f_ref[rows, :] + b_ref[...]))
            df_ref[rows, :] = df
            tot = tot + jnp.sum(df, axis=0, keepdims=True)

        @pl.when(pl.program_id(0) == 0)
        def _():
            db_ref[...] = tot

        @pl.when(pl.program_id(0) > 0)
        def _():
            db_ref[...] += tot

    blk = pl.BlockSpec((SEQ, LANES), lambda b: (b, 0))
    vec = pl.BlockSpec((1, LANES), lambda b: (0, 0))
    sd = jax.ShapeDtypeStruct
    return pl.pallas_call(
        body, name=name, grid=(T // SEQ,),
        in_specs=[pl.BlockSpec((SEQ, LANES), lambda b: (b, 4)), vec, blk],
        out_specs=[blk, vec], out_shape=[sd((T, LANES), F32), sd((1, LANES), F32)],
        compiler_params=_params("arbitrary"),
    )(aux, bf, dcum)


CHUNK = 256
WIDE = N_HEADS * CHUNK
NCH = SEQ // CHUNK


def _seg(h):
    return slice(h * CHUNK, (h + 1) * CHUNK)


def _chunk_rows(c):
    return pl.ds(pl.multiple_of(c * CHUNK, CHUNK), CHUNK)


def _wide_consts():
    r = lax.broadcasted_iota(jnp.int32, (WIDE, GROUP_W), 0)
    f = lax.broadcasted_iota(jnp.int32, (WIDE, GROUP_W), 1)
    bd = (r // CHUNK) == (f // HEAD_DIM)
    row = lax.broadcasted_iota(jnp.int32, (BLOCK, WIDE), 0)
    key = lax.broadcasted_iota(jnp.int32, (BLOCK, WIDE), 1) % CHUNK
    return bd, row, key


def _block_diag(x, bd):
    return jnp.where(bd, jnp.concatenate([x] * N_HEADS, axis=0), jnp.zeros((), x.dtype))


def _fold_heads(w, bd):
    w = jnp.where(bd, w, 0.0)
    return (w[0:CHUNK] + w[CHUNK:2 * CHUNK]) + (w[2 * CHUNK:3 * CHUNK] + w[3 * CHUNK:])


def _widen(cols):
    return jnp.concatenate([jnp.broadcast_to(c, (BLOCK, CHUNK)) for c in cols], axis=1)


def _head_rowsums(w):
    return [jnp.sum(w[:, _seg(h)], axis=1, keepdims=True) for h in HEADS]


def _tri_wide(x, tri):
    hi = x.astype(BF16)
    lo = (x - hi.astype(F32)).astype(BF16)
    y = _dot(jnp.concatenate([hi[:, _seg(h)] for h in HEADS] + [lo[:, _seg(h)] for h in HEADS], axis=0), tri)
    return jnp.concatenate([y[h * BLOCK:(h + 1) * BLOCK] + y[(N_HEADS + h) * BLOCK:(N_HEADS + h + 1) * BLOCK] for h in HEADS], axis=1)


def _feature_widen(cols):
    return jnp.concatenate([jnp.broadcast_to(c, (BLOCK, HEAD_DIM)) for c in cols], axis=1)


def _loop_by_two(n, index, body, carry):
    odd = n % 2
    carry = lax.fori_loop(0, odd, lambda _, cr: body(index(0), cr), carry)
    return lax.fori_loop(0, n // 2, lambda t, cr: body(index(odd + 2 * t + 1), body(index(odd + 2 * t), cr)), carry)


def _sbw_scores(q, kbd, later):
    z = _dot(q, kbd, _NT)
    lk = -_softplus(z)
    return z + lk, lk, _tri_wide(lk, later)


def _sbw_tile(q, kbd, mask, later, csum):
    z = _dot(q, kbd, _NT)
    lk = -_softplus(z)
    if mask is not None:
        lk = jnp.where(mask, lk, 0.0)
    e = z + lk
    att = jnp.exp(e + _tri_wide(lk, later) + csum)
    if mask is not None:
        att = jnp.where(mask, att, 0.0)
    return att, e, lk


def sbw_fwd(qkv, *, name):
    T = qkv.shape[0]

    def body(q_ref, k_ref, v_ref, o_ref):
        bd, row, key = _wide_consts()
        r2 = lax.broadcasted_iota(jnp.int32, (CHUNK, CHUNK), 0)
        c2 = lax.broadcasted_iota(jnp.int32, (CHUNK, CHUNK), 1)
        later = (r2 > c2).astype(BF16)

        def qblock(i, _):
            q = q_ref[_blk(i), :] * Q_SCALE
            cd = i // 2
            strict = key < row + BLOCK * (i % 2)

            def tile(c, mask, carry):
                acc, csum = carry
                att, _, lk = _sbw_tile(q, _block_diag(k_ref[_chunk_rows(c), :], bd), mask, later, csum)
                acc = acc + _dot(att.astype(BF16), _block_diag(v_ref[_chunk_rows(c), :], bd))
                return acc, csum + _widen(_head_rowsums(lk))

            def two_tiles(c1, carry):
                acc, csum = carry
                e1, lk1, t1 = _sbw_scores(q, _block_diag(k_ref[_chunk_rows(c1), :], bd), later)
                e2, lk2, t2 = _sbw_scores(q, _block_diag(k_ref[_chunk_rows(c1 - 1), :], bd), later)
                att1 = jnp.exp(e1 + t1 + csum)
                csum = csum + _widen(_head_rowsums(lk1))
                att2 = jnp.exp(e2 + t2 + csum)
                csum = csum + _widen(_head_rowsums(lk2))
                acc = acc + _dot(att1.astype(BF16), _block_diag(v_ref[_chunk_rows(c1), :], bd))
                acc = acc + _dot(att2.astype(BF16), _block_diag(v_ref[_chunk_rows(c1 - 1), :], bd))
                return acc, csum

            carry = tile(cd, strict, (jnp.zeros((BLOCK, GROUP_W), F32), jnp.zeros((BLOCK, WIDE), F32)))
            odd = cd % 2
            carry = lax.fori_loop(0, odd, lambda n, cr: tile(cd - 1, None, cr), carry)
            acc, _ = lax.fori_loop(0, cd // 2, lambda n, cr: two_tiles(cd - 1 - odd - 2 * n, cr), carry)
            o_ref[_blk(i), :] = acc.astype(BF16)
            return 0

        lax.fori_loop(0, NB, qblock, 0)

    return pl.pallas_call(
        body, name=name, grid=(T // SEQ,), in_specs=_qkv_specs(0),
        out_specs=pl.BlockSpec((SEQ, GROUP_W), lambda b: (b, 0)), out_shape=jax.ShapeDtypeStruct((T, GROUP_W), BF16),
        compiler_params=_params("parallel"),
    )(qkv, qkv, qkv)


def sbw_bwd(qkv, dmixed, *, name):
    T = qkv.shape[0]

    def body(q_ref, k_ref, v_ref, do_ref, dq_ref, dk_ref, dv_ref, att_s, sg_s):
        bd, row, key = _wide_consts()
        r2 = lax.broadcasted_iota(jnp.int32, (CHUNK, CHUNK), 0)
        c2 = lax.broadcasted_iota(jnp.int32, (CHUNK, CHUNK), 1)
        later = (r2 > c2).astype(BF16)
        earlier = (r2 < c2).astype(BF16)
        dk_ref[...] = jnp.zeros_like(dk_ref)
        dv_ref[...] = jnp.zeros_like(dv_ref)

        def qblock(i, _):
            q = q_ref[_blk(i), :] * Q_SCALE
            do = do_ref[_blk(i), :].astype(BF16)
            cd = i // 2
            strict = key < row + BLOCK * (i % 2)

            def recompute(c, mask, csum):
                att, e, lk = _sbw_tile(q, _block_diag(k_ref[_chunk_rows(c), :], bd), mask, later, csum)
                sg = jnp.exp(e)
                att_s[c] = att
                sg_s[c] = sg if mask is None else jnp.where(mask, sg, 0.0)
                return csum + _widen(_head_rowsums(lk))

            def recompute_two(c1, csum):
                e1, lk1, t1 = _sbw_scores(q, _block_diag(k_ref[_chunk_rows(c1), :], bd), later)
                e2, lk2, t2 = _sbw_scores(q, _block_diag(k_ref[_chunk_rows(c1 - 1), :], bd), later)
                sg_s[c1] = jnp.exp(e1)
                sg_s[c1 - 1] = jnp.exp(e2)
                att_s[c1] = jnp.exp(e1 + t1 + csum)
                csum = csum + _widen(_head_rowsums(lk1))
                att_s[c1 - 1] = jnp.exp(e2 + t2 + csum)
                return csum + _widen(_head_rowsums(lk2))

            csum = recompute(cd, strict, jnp.zeros((BLOCK, WIDE), F32))
            odd = cd % 2
            csum = lax.fori_loop(0, odd, lambda n, cs: recompute(cd - 1, None, cs), csum)
            lax.fori_loop(0, cd // 2, lambda n, cs: recompute_two(cd - 1 - odd - 2 * n, cs), csum)

            def tile(c, carry):
                dq, pre = carry
                kbd = _block_diag(k_ref[_chunk_rows(c), :], bd)
                vbd = _block_diag(v_ref[_chunk_rows(c), :], bd)
                att = att_s[c]
                ds = _dot(do, vbd, _NT) * att
                dlk = ds + _tri_wide(ds, earlier) + pre
                dz = (ds - dlk * sg_s[c]).astype(BF16)
                dk_ref[_chunk_rows(c), :] += _fold_heads(_dot(dz, q, _TN), bd)
                dv_ref[_chunk_rows(c), :] += _fold_heads(_dot(att.astype(BF16), do, _TN), bd)
                return dq + _dot(dz, kbd), pre + _widen(_head_rowsums(ds))

            def two_tiles(c1, carry):
                dq, pre = carry
                c2 = c1 + 1
                kbd1, kbd2 = _block_diag(k_ref[_chunk_rows(c1), :], bd), _block_diag(k_ref[_chunk_rows(c2), :], bd)
                att1, att2 = att_s[c1], att_s[c2]
                ds1 = _dot(do, _block_diag(v_ref[_chunk_rows(c1), :], bd), _NT) * att1
                ds2 = _dot(do, _block_diag(v_ref[_chunk_rows(c2), :], bd), _NT) * att2
                tri1, tri2 = _tri_wide(ds1, earlier), _tri_wide(ds2, earlier)
                dv_ref[_chunk_rows(c1), :] += _fold_heads(_dot(att1.astype(BF16), do, _TN), bd)
                dv_ref[_chunk_rows(c2), :] += _fold_heads(_dot(att2.astype(BF16), do, _TN), bd)
                dz1 = (ds1 - (ds1 + tri1 + pre) * sg_s[c1]).astype(BF16)
                pre = pre + _widen(_head_rowsums(ds1))
                dz2 = (ds2 - (ds2 + tri2 + pre) * sg_s[c2]).astype(BF16)
                pre = pre + _widen(_head_rowsums(ds2))
                dk_ref[_chunk_rows(c1), :] += _fold_heads(_dot(dz1, q, _TN), bd)
                dk_ref[_chunk_rows(c2), :] += _fold_heads(_dot(dz2, q, _TN), bd)
                return dq + _dot(dz1, kbd1) + _dot(dz2, kbd2), pre

            n_tiles = cd + 1
            odd = n_tiles % 2
            carry = (jnp.zeros((BLOCK, GROUP_W), F32), jnp.zeros((BLOCK, WIDE), F32))
            carry = lax.fori_loop(0, odd, lambda n, cr: tile(0, cr), carry)
            dq, _ = lax.fori_loop(0, n_tiles // 2, lambda n, cr: two_tiles(odd + 2 * n, cr), carry)
            dq_ref[_blk(i), :] = dq * Q_SCALE
            return 0

        lax.fori_loop(0, NB, qblock, 0)

    out = pl.BlockSpec((SEQ, GROUP_W), lambda b: (b, 0))
    sd = jax.ShapeDtypeStruct((T, GROUP_W), F32)
    return pl.pallas_call(
        body, name=name, grid=(T // SEQ,), in_specs=_qkv_specs(0) + [out],
        out_specs=[out] * 3, out_shape=[sd] * 3,
        scratch_shapes=[pltpu.VMEM((NCH, BLOCK, WIDE), F32), pltpu.VMEM((NCH, BLOCK, WIDE), F32)],
        compiler_params=_params("parallel"),
    )(qkv, qkv, qkv, dmixed)


def _foxw_logits(q, kbd, cq, cr_ref, c, mask):
    ck = jnp.concatenate([cr_ref[h:h + 1, _chunk_rows(c)] for h in HEADS], axis=1)
    z = _dot(q, kbd, _NT) + cq - ck
    return z if mask is None else jnp.where(mask, z, NEG)


def foxw_fwd(qkv, cumc, cumr, *, name):
    T = qkv.shape[0]

    def body(q_ref, k_ref, v_ref, cc_ref, cr_ref, o_ref, o32_ref, lse_ref, z_s):
        bd, row, key = _wide_consts()
        lse_ref[...] = jnp.zeros_like(lse_ref)

        def qblock(i, _):
            q = q_ref[_blk(i), :] * Q_SCALE
            cq = _widen([cc_ref[_blk(i), h:h + 1] for h in HEADS])
            cd = i // 2
            causal = key <= row + BLOCK * (i % 2)

            def logits(c, mask, ms):
                z = _foxw_logits(q, _block_diag(k_ref[_chunk_rows(c), :], bd), cq, cr_ref, c, mask)
                z_s[c] = z
                return tuple(jnp.maximum(ms[h], jnp.max(z[:, _seg(h)], axis=1, keepdims=True)) for h in HEADS)

            ms = logits(cd, causal, (jnp.full((BLOCK, 1), NEG, F32),) * N_HEADS)
            ms = _loop_by_two(cd, lambda n: n, lambda c, m: logits(c, None, m), ms)
            m_wide = _widen(ms)

            def values(c, carry):
                acc, l = carry
                p = jnp.exp(z_s[c] - m_wide)
                return acc + _dot(p.astype(BF16), _block_diag(v_ref[_chunk_rows(c), :], bd)), l + _widen(_head_rowsums(p))

            acc, l = _loop_by_two(cd + 1, lambda n: n, values, (jnp.zeros((BLOCK, GROUP_W), F32), jnp.zeros((BLOCK, WIDE), F32)))
            ls = [l[:, h * CHUNK:h * CHUNK + 1] for h in HEADS]
            o = acc / _feature_widen(ls)
            o_ref[_blk(i), :] = o.astype(BF16)
            o32_ref[_blk(i), :] = o
            for h in HEADS:
                lse_ref[_blk(i), h:h + 1] = ms[h] + jnp.log(ls[h])
            return 0

        lax.fori_loop(0, NB, qblock, 0)

    out = pl.BlockSpec((SEQ, GROUP_W), lambda b: (b, 0))
    colb = pl.BlockSpec((SEQ, LANES), lambda b: (b, 0))
    sd = jax.ShapeDtypeStruct
    return pl.pallas_call(
        body, name=name, grid=(T // SEQ,),
        in_specs=_qkv_specs(3) + [colb, pl.BlockSpec((8, SEQ), lambda b: (b, 0))],
        out_specs=[out, out, colb], out_shape=[sd((T, GROUP_W), BF16), sd((T, GROUP_W), F32), sd((T, LANES), F32)],
        scratch_shapes=[pltpu.VMEM((NCH, BLOCK, WIDE), F32)],
        compiler_params=_params("parallel"),
    )(qkv, qkv, qkv, cumc, cumr)


def foxw_bwd(qkv, cumc, cumr, lse, o32, dmixed, *, name):
    T = qkv.shape[0]

    def body(q_ref, k_ref, v_ref, cc_ref, cr_ref, lse_ref, o_ref, do_ref, dq_ref, dk_ref, dv_ref, dcc_ref, dcr_ref):
        bd, row, key = _wide_consts()
        dk_ref[...] = jnp.zeros_like(dk_ref)
        dv_ref[...] = jnp.zeros_like(dv_ref)
        dcc_ref[...] = jnp.zeros_like(dcc_ref)
        dcr_ref[...] = jnp.zeros_like(dcr_ref)

        def qblock(i, _):
            q = q_ref[_blk(i), :] * Q_SCALE
            do32 = do_ref[_blk(i), :]
            do = do32.astype(BF16)
            prod = do32 * o_ref[_blk(i), :]
            delta = _widen([jnp.sum(prod[:, _hs(h)], axis=1, keepdims=True) for h in HEADS])
            cq = _widen([cc_ref[_blk(i), h:h + 1] for h in HEADS])
            lse_w = _widen([lse_ref[_blk(i), h:h + 1] for h in HEADS])
            cd = i // 2
            causal = key <= row + BLOCK * (i % 2)

            def tile(c, mask, carry):
                dq, dcq = carry
                kbd = _block_diag(k_ref[_chunk_rows(c), :], bd)
                vbd = _block_diag(v_ref[_chunk_rows(c), :], bd)
                p = jnp.exp(_foxw_logits(q, kbd, cq, cr_ref, c, mask) - lse_w)
                ds = p * (_dot(do, vbd, _NT) - delta)
                dsb = ds.astype(BF16)
                dk_ref[_chunk_rows(c), :] += _fold_heads(_dot(dsb, q, _TN), bd)
                dv_ref[_chunk_rows(c), :] += _fold_heads(_dot(p.astype(BF16), do, _TN), bd)
                for h in HEADS:
                    dcr_ref[h:h + 1, _chunk_rows(c)] -= jnp.sum(ds[:, _seg(h)], axis=0, keepdims=True)
                return dq + _dot(dsb, kbd), dcq + _widen(_head_rowsums(ds))

            def two_tiles(c1, carry):
                dq, dcq = carry
                cs = (c1, c1 + 1)
                kbds = [_block_diag(k_ref[_chunk_rows(c), :], bd) for c in cs]
                vbds = [_block_diag(v_ref[_chunk_rows(c), :], bd) for c in cs]
                ps = [jnp.exp(_foxw_logits(q, kbds[j], cq, cr_ref, cs[j], None) - lse_w) for j in range(2)]
                dss = [ps[j] * (_dot(do, vbds[j], _NT) - delta) for j in range(2)]
                dsbs = [d.astype(BF16) for d in dss]
                for j, c in enumerate(cs):
                    dk_ref[_chunk_rows(c), :] += _fold_heads(_dot(dsbs[j], q, _TN), bd)
                    dv_ref[_chunk_rows(c), :] += _fold_heads(_dot(ps[j].astype(BF16), do, _TN), bd)
                    for h in HEADS:
                        dcr_ref[h:h + 1, _chunk_rows(c)] -= jnp.sum(dss[j][:, _seg(h)], axis=0, keepdims=True)
                dq = dq + _dot(dsbs[0], kbds[0]) + _dot(dsbs[1], kbds[1])
                return dq, dcq + _widen(_head_rowsums(dss[0])) + _widen(_head_rowsums(dss[1]))

            carry = tile(cd, causal, (jnp.zeros((BLOCK, GROUP_W), F32), jnp.zeros((BLOCK, WIDE), F32)))
            odd = cd % 2
            carry = lax.fori_loop(0, odd, lambda n, cr: tile(0, None, cr), carry)
            dq, dcq = lax.fori_loop(0, cd // 2, lambda n, cr: two_tiles(odd + 2 * n, cr), carry)
            dq_ref[_blk(i), :] = dq * Q_SCALE
            for h in HEADS:
                dcc_ref[_blk(i), h:h + 1] = dcq[:, h * CHUNK:h * CHUNK + 1]
            return 0

        lax.fori_loop(0, NB, qblock, 0)

    out = pl.BlockSpec((SEQ, GROUP_W), lambda b: (b, 0))
    colb = pl.BlockSpec((SEQ, LANES), lambda b: (b, 0))
    rowb = pl.BlockSpec((8, SEQ), lambda b: (b, 0))
    sd = jax.ShapeDtypeStruct
    big = sd((T, GROUP_W), F32)
    return pl.pallas_call(
        body, name=name, grid=(T // SEQ,),
        in_specs=_qkv_specs(3) + [colb, rowb, colb, out, pl.BlockSpec((SEQ, GROUP_W), lambda b: (b, 1))],
        out_specs=[out, out, out, colb, rowb],
        out_shape=[big, big, big, sd((T, LANES), F32), sd((T // SEQ * 8, SEQ), F32)],
        compiler_params=_params("parallel"),
    )(qkv, qkv, qkv, cumc, cumr, lse, o32, dmixed)


BAND = 2 * BLOCK


def _t5_bucket_np(dist):
    n = np.maximum(dist, 0)
    max_exact = NUM_BUCKETS // 2
    nf = np.maximum(n, 1).astype(np.float32)
    large = max_exact + (np.log(nf / np.float32(max_exact)) / np.float32(math.log(MAX_DISTANCE / max_exact))
                         * np.float32(NUM_BUCKETS - max_exact)).astype(np.int32)
    large = np.minimum(large, NUM_BUCKETS - 1)
    return np.where(n < max_exact, n, large).astype(np.int32)


def _band_buckets():
    qi = np.arange(BLOCK)[:, None]
    ki = np.arange(BAND)[None, :]
    delta = np.clip(qi - ki + BLOCK, 0, BLOCK)
    return np.stack([_t5_bucket_np(delta * d) for d in DILATIONS])


def relbias_expand(rel, *, name):
    buckets = jnp.asarray(_band_buckets())
    n_pat = len(DILATIONS)

    def body(rel_ref, bk_ref, o_ref):
        for p in range(n_pat):
            bk = bk_ref[p]
            for h in range(N_HEADS):
                acc = jnp.zeros((BLOCK, BAND), F32)
                for b in range(NUM_BUCKETS):
                    acc = jnp.where(bk == b, rel_ref[b, h], acc)
                o_ref[p * N_HEADS + h] = acc

    return pl.pallas_call(
        body, name=name,
        in_specs=[pl.BlockSpec(memory_space=pltpu.SMEM), pl.BlockSpec(memory_space=pltpu.VMEM)],
        out_specs=pl.BlockSpec(memory_space=pltpu.VMEM),
        out_shape=jax.ShapeDtypeStruct((n_pat * N_HEADS, BLOCK, BAND), F32),
        compiler_params=_params(),
    )(rel, buckets)


def relbias_reduce(ds_all, *, name):
    buckets = jnp.asarray(_band_buckets())
    n_pat = len(DILATIONS)

    def body(ds_ref, bk_ref, o_ref):
        for b in range(NUM_BUCKETS):
            for h in range(N_HEADS):
                tot = jnp.float32(0.0)
                for p in range(n_pat):
                    tot = tot + jnp.sum(jnp.where(bk_ref[p] == b, ds_ref[p * N_HEADS + h], 0.0))
                o_ref[b, h] = tot

    return pl.pallas_call(
        body, name=name,
        in_specs=[pl.BlockSpec(memory_space=pltpu.VMEM), pl.BlockSpec(memory_space=pltpu.VMEM)],
        out_specs=pl.BlockSpec(memory_space=pltpu.SMEM),
        out_shape=jax.ShapeDtypeStruct((NUM_BUCKETS, N_HEADS), F32),
        compiler_params=_params(),
    )(ds_all, buckets)


def _band_valid_wide(first, row, key):
    inside = jnp.logical_and(key >= row, key <= row + BLOCK)
    return jnp.logical_and(inside, jnp.logical_or(jnp.logical_not(first), key >= BLOCK))


QKV_BLOCKS = 9


def _band_in_specs(d, pattern, has_prev):
    rows = BLOCK * d
    cur = lambda c: pl.BlockSpec((rows, GROUP_W), lambda tb, r: (tb, c))
    prev = lambda c: pl.BlockSpec((rows, GROUP_W), lambda tb, r: (jnp.maximum(tb - 1, 0), c))
    bias = pl.BlockSpec((N_HEADS, BLOCK, BAND), lambda tb, r: (pattern, 0, 0))
    return [cur(6), cur(7), cur(8)] + ([prev(7), prev(8)] if has_prev else []) + [bias]


def _classes_per_step(d):
    return 2 if d > 1 else 1


def _step_classes(d):
    n = _classes_per_step(d)
    return [pl.program_id(1) * n + j for j in range(n)]


def _class_rows(d, cls):
    return pl.ds(cls, BLOCK, stride=d) if d > 1 else pl.ds(0, BLOCK)


def _halves_scratch(rows, n):
    return [pltpu.VMEM((2, rows, LANES), F32)] * n


def _stage(refs, scratch):
    @pl.when(pl.program_id(1) == 0)
    def _():
        for src, dst in zip(refs, scratch):
            dst[0] = src[:, :LANES].astype(F32)
            dst[1] = src[:, LANES:].astype(F32)


def _take_class(s, d, cls):
    rows = _class_rows(d, cls)
    return jnp.concatenate([s.at[0][rows, :], s.at[1][rows, :]], axis=1)


def _put_class(s, d, cls, x):
    rows = _class_rows(d, cls)
    s.at[0][rows, :] = x[:, :LANES]
    s.at[1][rows, :] = x[:, LANES:]


def _flush(scratch, refs, d):
    @pl.when(pl.program_id(1) == d // _classes_per_step(d) - 1)
    def _():
        for s, o in zip(scratch, refs):
            o[...] = jnp.concatenate([s[0], s[1]], axis=1)


def _band_operands(scratch, d, cls, has_prev):
    take = lambda s: _take_class(s, d, cls).astype(BF16)
    q = (_take_class(scratch[0], d, cls) * Q_SCALE).astype(BF16)
    if has_prev:
        k = jnp.concatenate([take(scratch[3]), take(scratch[1])], axis=0)
        v = jnp.concatenate([take(scratch[4]), take(scratch[2])], axis=0)
    else:
        k = jnp.concatenate([jnp.zeros((BLOCK, GROUP_W), BF16), take(scratch[1])], axis=0)
        v = jnp.concatenate([jnp.zeros((BLOCK, GROUP_W), BF16), take(scratch[2])], axis=0)
    return q, k, v


def _lane_columns(cols):
    lane = lax.broadcasted_iota(jnp.int32, (BLOCK, LANES), 1)
    out = jnp.zeros((BLOCK, LANES), F32)
    for h, c in enumerate(cols):
        out = jnp.where(lane == h, c, out)
    return out


def band_fwd(qkv, bias, pattern, *, name):
    T = qkv.shape[0]
    d = DILATIONS[pattern]
    rows_per_block = BLOCK * d
    seq_blocks = SEQ // rows_per_block
    has_prev = seq_blocks > 1
    n_in = 5 if has_prev else 3

    def body(*refs):
        ins, b_ref, o_ref, lse_ref = refs[:n_in], refs[n_in], refs[n_in + 1], refs[n_in + 2]
        staged, o_s = refs[n_in + 3:2 * n_in + 3], refs[2 * n_in + 3]
        bd, row, key = _wide_consts()
        valid = _band_valid_wide(pl.program_id(0) % seq_blocks == 0, row, key)
        _stage(ins, staged)
        bias_w = jnp.concatenate([b_ref[h] for h in HEADS], axis=1)
        for cls in _step_classes(d):
            q, k, v = _band_operands(staged, d, cls, has_prev)
            kbd, vbd = _block_diag(k, bd), _block_diag(v, bd)
            sc = jnp.where(valid, _dot(q, kbd, _NT) + bias_w, NEG)
            ms = [jnp.max(sc[:, _seg(h)], axis=1, keepdims=True) for h in HEADS]
            p = jnp.exp(sc - _widen(ms))
            ls = _head_rowsums(p)
            _put_class(o_s, d, cls, _dot(p.astype(BF16), vbd) / _feature_widen(ls))
            lse_ref[_class_rows(d, cls), :] = _lane_columns([ms[h] + jnp.log(ls[h]) for h in HEADS])
        _flush([o_s], [o_ref], d)

    sd = jax.ShapeDtypeStruct
    return pl.pallas_call(
        body, name=name, grid=(T // rows_per_block, d // _classes_per_step(d)), in_specs=_band_in_specs(d, pattern, has_prev),
        out_specs=[pl.BlockSpec((rows_per_block, GROUP_W), lambda tb, r: (tb, 0)),
                   pl.BlockSpec((rows_per_block, LANES), lambda tb, r: (tb, 0))],
        out_shape=[sd((T, GROUP_W), F32), sd((T, LANES), F32)],
        scratch_shapes=_halves_scratch(rows_per_block, n_in + 1),
        compiler_params=_params("parallel", "arbitrary"),
    )(*([qkv] * n_in), bias)


def band_bwd(qkv, bias, lse, do, dlse, pattern, *, name):
    T = qkv.shape[0]
    d = DILATIONS[pattern]
    rows_per_block = BLOCK * d
    seq_blocks = SEQ // rows_per_block
    has_prev = seq_blocks > 1
    n_in = 5 if has_prev else 3
    n_out = 5 if has_prev else 3

    def body(*refs):
        ins, b_ref, lse_ref, do_ref, dlse_ref = refs[:n_in], refs[n_in], refs[n_in + 1], refs[n_in + 2], refs[n_in + 3]
        outs = refs[n_in + 4:n_in + 4 + n_out]
        ds_ref = refs[n_in + 4 + n_out]
        scratch = refs[n_in + 5 + n_out:]
        staged, do_s, out_s = scratch[:n_in], scratch[n_in], scratch[n_in + 1:]
        first_step = jnp.logical_and(pl.program_id(0) == 0, pl.program_id(1) == 0)
        bd, row, key = _wide_consts()
        valid = _band_valid_wide(pl.program_id(0) % seq_blocks == 0, row, key)
        _stage(list(ins) + [do_ref], list(staged) + [do_s])
        bias_w = jnp.concatenate([b_ref[h] for h in HEADS], axis=1)
        ds = None
        for cls in _step_classes(d):
            q, k, v = _band_operands(staged, d, cls, has_prev)
            kbd, vbd = _block_diag(k, bd), _block_diag(v, bd)
            rows = _class_rows(d, cls)
            do = _take_class(do_s, d, cls).astype(BF16)
            lse_t, dlse_t = lse_ref[rows, :], dlse_ref[rows, :]
            lse_w = _widen([lse_t[:, h:h + 1] for h in HEADS])
            dlse_w = _widen([dlse_t[:, h:h + 1] for h in HEADS])
            p = jnp.where(valid, jnp.exp(_dot(q, kbd, _NT) + bias_w - lse_w), 0.0)
            dp = _dot(do, vbd, _NT)
            ds_c = p * (dp - _widen(_head_rowsums(p * dp)) + dlse_w)
            dsb, pb = ds_c.astype(BF16), p.astype(BF16)
            _put_class(out_s[0], d, cls, _dot(dsb, kbd) * Q_SCALE)
            dk = _fold_heads(_dot(dsb, q, _TN), bd)
            dv = _fold_heads(_dot(pb, do, _TN), bd)
            _put_class(out_s[1], d, cls, dk[BLOCK:])
            _put_class(out_s[2], d, cls, dv[BLOCK:])
            if has_prev:
                _put_class(out_s[3], d, cls, dk[:BLOCK])
                _put_class(out_s[4], d, cls, dv[:BLOCK])
            ds = ds_c if ds is None else ds + ds_c
        _flush(out_s, outs, d)

        @pl.when(first_step)
        def _():
            for h in HEADS:
                ds_ref[h] = ds[:, _seg(h)]

        @pl.when(jnp.logical_not(first_step))
        def _():
            for h in HEADS:
                ds_ref[h] += ds[:, _seg(h)]

    big = pl.BlockSpec((rows_per_block, GROUP_W), lambda tb, r: (tb, 0))
    colb = pl.BlockSpec((rows_per_block, LANES), lambda tb, r: (tb, 0))
    sd = jax.ShapeDtypeStruct
    return pl.pallas_call(
        body, name=name, grid=(T // rows_per_block, d // _classes_per_step(d)),
        in_specs=_band_in_specs(d, pattern, has_prev) + [colb, big, colb],
        out_specs=[big] * n_out + [pl.BlockSpec((N_HEADS, BLOCK, BAND), lambda tb, r: (0, 0, 0))],
        out_shape=[sd((T, GROUP_W), F32)] * n_out + [sd((N_HEADS, BLOCK, BAND), F32)],
        scratch_shapes=_halves_scratch(rows_per_block, n_in + 1 + n_out),
        compiler_params=_params("arbitrary", "arbitrary"),
    )(*([qkv] * n_in), bias, lse, do, dlse)


def shift_add(cur, prev, d, *, name):
    rows = BLOCK * d
    nb = cur.shape[0] // rows

    def body(c_ref, p_ref, o_ref):
        keep = (pl.program_id(0) < nb - 1).astype(F32)
        o_ref[...] = c_ref[...] + keep * p_ref[...]

    blk = pl.BlockSpec((rows, GROUP_W), lambda tb: (tb, 0))
    nxt = pl.BlockSpec((rows, GROUP_W), lambda tb: (jnp.minimum(tb + 1, nb - 1), 0))
    return pl.pallas_call(
        body, name=name, grid=(nb,), in_specs=[blk, nxt], out_specs=blk,
        out_shape=jax.ShapeDtypeStruct(cur.shape, F32), compiler_params=_params("parallel"),
    )(cur, prev)


def _pattern_weights(lse_refs, h):
    ls = [r[:, h:h + 1] for r in lse_refs]
    mx = functools.reduce(jnp.maximum, ls)
    es = [jnp.exp(l - mx) for l in ls]
    tot = functools.reduce(lambda a, b: a + b, es)
    return [e / tot for e in es]


def dil_combine_fwd(outs, *, name):
    T = outs[0][0].shape[0]
    n = len(outs)
    tm = 512

    def body(*refs):
        o_refs, l_refs, out_ref = refs[:n], refs[n:2 * n], refs[2 * n]
        for h in range(N_HEADS):
            w = _pattern_weights(l_refs, h)
            acc = w[0] * o_refs[0][:, _hs(h)]
            for p in range(1, n):
                acc = acc + w[p] * o_refs[p][:, _hs(h)]
            out_ref[:, _hs(h)] = acc.astype(BF16)

    big = pl.BlockSpec((tm, GROUP_W), lambda i: (i, 0))
    colb = pl.BlockSpec((tm, LANES), lambda i: (i, 0))
    return pl.pallas_call(
        body, name=name, grid=(T // tm,), in_specs=[big] * n + [colb] * n,
        out_specs=big, out_shape=jax.ShapeDtypeStruct((T, GROUP_W), BF16),
        compiler_params=_params("parallel"),
    )(*[o for o, _ in outs], *[l for _, l in outs])


def dil_combine_bwd(outs, dmixed, *, name):
    T = outs[0][0].shape[0]
    n = len(outs)
    tm = 512

    def body(*refs):
        o_refs, l_refs, do_ref = refs[:n], refs[n:2 * n], refs[2 * n]
        do_refs, dl_refs = refs[2 * n + 1:3 * n + 1], refs[3 * n + 1:]
        for r in dl_refs:
            r[...] = jnp.zeros_like(r)
        for h in range(N_HEADS):
            w = _pattern_weights(l_refs, h)
            do = do_ref[:, _hs(h)]
            dw = [jnp.sum(do * o_refs[p][:, _hs(h)], axis=1, keepdims=True) for p in range(n)]
            mean = functools.reduce(lambda a, b: a + b, [w[p] * dw[p] for p in range(n)])
            for p in range(n):
                do_refs[p][:, _hs(h)] = w[p] * do
                dl_refs[p][:, h:h + 1] = w[p] * (dw[p] - mean)

    big = pl.BlockSpec((tm, GROUP_W), lambda i: (i, 0))
    colb = pl.BlockSpec((tm, LANES), lambda i: (i, 0))
    sd = jax.ShapeDtypeStruct
    res = pl.pallas_call(
        body, name=name, grid=(T // tm,),
        in_specs=[big] * n + [colb] * n + [pl.BlockSpec((tm, GROUP_W), lambda i: (i, 2))],
        out_specs=[big] * n + [colb] * n, out_shape=[sd((T, GROUP_W), F32)] * n + [sd((T, LANES), F32)] * n,
        compiler_params=_params("parallel"),
    )(*[o for o, _ in outs], *[l for _, l in outs], dmixed)
    return list(zip(res[:n], res[n:]))


def dilated_fwd(qkv, bias, tag):
    return [band_fwd(qkv, bias, p, name=f"{tag}_band_fwd{p}") for p in range(len(DILATIONS))]


def dilated_bwd(qkv, bias, outs, dmixed, tag):
    grads = dil_combine_bwd(outs, dmixed, name=f"{tag}_combine_bwd")
    parts, ds_all = [], []
    for p, d in enumerate(DILATIONS):
        (_, lse), (do, dlse) = outs[p], grads[p]
        res = band_bwd(qkv, bias, lse, do, dlse, p, name=f"{tag}_band_bwd{p}")
        dq, dk, dv, ds = res[0], res[1], res[2], res[-1]
        if len(res) > 4:
            dk = shift_add(dk, res[3], d, name=f"{tag}_dk{p}")
            dv = shift_add(dv, res[4], d, name=f"{tag}_dv{p}")
        parts.append([dq, dk, dv])
        ds_all.append(ds)
    return parts, jnp.concatenate(ds_all, axis=0)


def assemble_dqkv(d_sb, d_fox, d_dil, *, name):
    T = d_sb[0].shape[0]
    tr = 512
    n_pat = len(d_dil)
    flat = list(d_sb) + list(d_fox) + [a for part in d_dil for a in part]

    def body(*refs):
        o_ref = refs[-1]
        for j in range(6):
            o_ref[:, j * GROUP_W:(j + 1) * GROUP_W] = refs[j][...].astype(BF16)
        for j in range(3):
            acc = refs[6 + j][...]
            for p in range(1, n_pat):
                acc = acc + refs[6 + 3 * p + j][...]
            o_ref[:, (6 + j) * GROUP_W:(7 + j) * GROUP_W] = acc.astype(BF16)

    blk = pl.BlockSpec((tr, GROUP_W), lambda i: (i, 0))
    return pl.pallas_call(
        body, name=name, grid=(T // tr,), in_specs=[blk] * len(flat),
        out_specs=pl.BlockSpec((tr, QKV_BLOCKS * GROUP_W), lambda i: (i, 0)),
        out_shape=jax.ShapeDtypeStruct((T, QKV_BLOCKS * GROUP_W), BF16), compiler_params=_params("parallel"),
    )(*flat)


def sum_cast(arrs, dtype, *, name):
    R, C = arrs[0].shape
    tr = _largest_tile(R, 512, 16)
    n = len(arrs)

    def body(*refs):
        acc = refs[0][...].astype(F32)
        for r in refs[1:n]:
            acc = acc + r[...].astype(F32)
        refs[n][...] = acc.astype(dtype)

    blk = pl.BlockSpec((tr, C), lambda i: (i, 0))
    return pl.pallas_call(
        body, name=name, grid=(R // tr,), in_specs=[blk] * n, out_specs=blk, out_shape=jax.ShapeDtypeStruct((R, C), dtype),
        compiler_params=_params("parallel"),
    )(*arrs)


GRAD_WIRE = BF16


def _block_diag_halves(w):
    z = jnp.zeros((HEAD_DIM, HEAD_DIM), w.dtype)
    half = lambda a, b: jnp.concatenate([jnp.concatenate([a, z], axis=1), jnp.concatenate([z, b], axis=1)], axis=0)
    return jnp.stack([half(w[0], w[1]), half(w[2], w[3])]).astype(BF16)


def _diag_blocks(d):
    h = HEAD_DIM
    return jnp.stack([d[0, :h, :h], d[0, h:, h:], d[1, :h, :h], d[1, h:, h:]])


def layer_fwd(x, mem2d, W, P, bias, tag):
    s = {}
    s["x"] = x
    h1 = rmsnorm_fwd(x, P["norm_mix_g"], name=f"{tag}_norm_mix")
    qkv = matmul(h1, W["qkv"], out_dtype=BF16, name=f"{tag}_qkv")
    aux = matmul(h1, W["aux"], name=f"{tag}_aux")
    o_sb = sbw_fwd(qkv, name=f"{tag}_sb_fwd")
    cumc = fox_prep(aux, P["bf"], name=f"{tag}_fox_prep")
    cumr = col_to_row(cumc)
    o_fox, o_fox32, lse_fox = foxw_fwd(qkv, cumc, cumr, name=f"{tag}_fox_fwd")
    dil = dilated_fwd(qkv, bias, tag)
    o_dil = dil_combine_fwd(dil, name=f"{tag}_dil_combine")
    o_lru, h_lru = lru_fwd(aux, P["lru_conv_w"], P["lru_conv_b"], P["wa"], P["lru_b_a"], P["wx"], P["lru_b_x"],
                           P["lru_lambda"], name=f"{tag}_lru_fwd")
    mixed = jnp.concatenate([o_sb, o_fox, o_dil, o_lru], axis=1)
    if "rest" in W:
        W.update(W.pop("rest")(mixed))
    x1 = matmul(mixed, W["out"], residual=x, name=f"{tag}_out")
    hq = rmsnorm_fwd(x1, P["norm_cross_g"], name=f"{tag}_norm_cross")
    qc = matmul(hq, W["cq"], out_dtype=BF16, name=f"{tag}_cq")
    memn = rmsnorm_fwd(mem2d, P["norm_mem_g"], name=f"{tag}_norm_mem")
    kv = matmul(memn, W["ckv"], out_dtype=BF16, name=f"{tag}_ckv")
    oc = cross_fwd(qc, kv, name=f"{tag}_cross_fwd")
    x2 = matmul(oc, W["coT"], trans_b=True, residual=x1, name=f"{tag}_co")
    h2 = rmsnorm_fwd(x2, P["norm_ffn_g"], name=f"{tag}_norm_ffn")
    hu = matmul(h2, W["up_u"], trans_b=True, name=f"{tag}_up_u")
    hg = matmul(h2, W["up_g"], trans_b=True, name=f"{tag}_up_g")
    act = glu_fwd(hu, hg, P["wu"], P["wg"], P["bu"], P["bg"], name=f"{tag}_glu_fwd")
    x3 = matmul(act, W["down"], residual=x2, name=f"{tag}_down")
    s.update(h1=h1, qkv=qkv, aux=aux, cumc=cumc, cumr=cumr, lse_fox=lse_fox, o_fox32=o_fox32, dil=dil, h_lru=h_lru, mixed=mixed,
             x1=x1, hq=hq, qc=qc, memn=memn, kv=kv, oc=oc, x2=x2, h2=h2, hu=hu, hg=hg, act=act)
    return x3, s


def layer_bwd(dx3, mem2d, W, P, bias, s, tag, hooks=None):
    mm = functools.partial(matmul, out_dtype=GRAD_WIRE, trans_a=True)
    gW, gP = {}, {}
    hooks = hooks or {}
    dact = matmul(dx3, W["down"], trans_b=True, name=f"{tag}_d_act")
    gW["down"] = mm(s["act"], dx3, name=f"{tag}_g_down")
    dhu, dhg, dwu, dwg, dbu, dbg = glu_bwd(s["hu"], s["hg"], dact, P["wu"], P["wg"], P["bu"], P["bg"], name=f"{tag}_glu_bwd")
    gP["ffn_conv_w"] = jnp.concatenate([dwu, dwg], axis=1)
    gP["ffn_conv_b"] = jnp.concatenate([dbu, dbg], axis=1)
    dh2 = matmul(dhu, W["up_u"], name=f"{tag}_d_h2u")
    dh2 = matmul(dhg, W["up_g"], residual=dh2, name=f"{tag}_d_h2g")
    gW["up_u"] = mm(dhu, s["h2"], name=f"{tag}_g_up_u")
    gW["up_g"] = mm(dhg, s["h2"], name=f"{tag}_g_up_g")
    dx2, gP["norm_ffn_g"] = rmsnorm_bwd(s["x2"], P["norm_ffn_g"], dh2, dx3, name=f"{tag}_norm_ffn_bwd")
    if "ffn" in hooks:
        hooks["ffn"](gW, W, s)
    doc = matmul(dx2, W["coT"], name=f"{tag}_d_oc")
    gW["coT"] = mm(dx2, s["oc"], name=f"{tag}_g_co")
    dqc, dkv = cross_bwd(s["qc"], s["kv"], doc, name=f"{tag}_cross_bwd")
    dhq = matmul(dqc, W["cq"], trans_b=True, name=f"{tag}_d_hq")
    gW["cq"] = mm(s["hq"], dqc, name=f"{tag}_g_cq")
    dmemn = matmul(dkv, W["ckv"], trans_b=True, name=f"{tag}_d_memn")
    gW["ckv"] = mm(s["memn"], dkv, name=f"{tag}_g_ckv")
    _, gP["norm_mem_g"] = rmsnorm_bwd(mem2d, P["norm_mem_g"], dmemn, None, name=f"{tag}_norm_mem_bwd")
    dx1, gP["norm_cross_g"] = rmsnorm_bwd(s["x1"], P["norm_cross_g"], dhq, dx2, name=f"{tag}_norm_cross_bwd")
    dmixed = matmul(dx1, W["out"], trans_b=True, name=f"{tag}_d_mixed")
    gW["out"] = mm(s["mixed"], dx1, name=f"{tag}_g_out")
    if "mid" in hooks:
        hooks["mid"](gW, W, s)
    qkv, aux = s["qkv"], s["aux"]
    d_sb = sbw_bwd(qkv, dmixed, name=f"{tag}_sb_bwd")
    dfq, dfk, dfv, dcc, dcr = foxw_bwd(qkv, s["cumc"], s["cumr"], s["lse_fox"], s["o_fox32"], dmixed, name=f"{tag}_fox_bwd")
    dcum = sum_cast([dcc, row_to_col(dcr)], F32, name=f"{tag}_dcum")
    df, dbf = fox_prep_bwd(aux, P["bf"], dcum, name=f"{tag}_fox_prep_bwd")
    gP["b_forget"] = dbf[0, :N_HEADS]
    d_dil, ds_band = dilated_bwd(qkv, bias, s["dil"], dmixed, tag)
    dlx, dlg, dcw, dcb, dwa, dba, dwx, dbx, dlam = lru_bwd(
        aux, s["h_lru"], dmixed, P["lru_conv_w"], P["lru_conv_b"], P["wa"], P["lru_b_a"], P["wx"], P["lru_b_x"],
        P["lru_lambda"], name=f"{tag}_lru_bwd")
    gP.update(lru_conv_w=dcw, lru_conv_b=dcb, lru_w_a=_diag_blocks(dwa), lru_b_a=dba, lru_w_x=_diag_blocks(dwx),
              lru_b_x=dbx, lru_lambda=dlam)
    dqkv = assemble_dqkv(d_sb, [dfq, dfk, dfv], d_dil, name=f"{tag}_dqkv")
    daux = jnp.concatenate([dlx, dlg, df], axis=1)
    dh1 = matmul(dqkv, W["qkv"], trans_b=True, name=f"{tag}_d_h1a")
    dh1 = matmul(daux, W["aux"], trans_b=True, residual=dh1, name=f"{tag}_d_h1b")
    gW["qkv"] = mm(s["h1"], dqkv, name=f"{tag}_g_qkv")
    gW["aux"] = mm(s["h1"], daux, name=f"{tag}_g_aux")
    dx, gP["norm_mix_g"] = rmsnorm_bwd(s["x"], P["norm_mix_g"], dh1, dx1, name=f"{tag}_norm_mix_bwd")
    return dx, gW, gP, ds_band


def local_step(x, mem, target, weights_of, Ps, rel_bias, final_norm_g, grads_done=None, bwd_hooks=None):
    B = x.shape[0]
    x2d = x.reshape(B * SEQ, D_MODEL)
    mem2d = mem.reshape(B * N_MEM, D_MODEL)
    bias = relbias_expand(rel_bias, name="relbias_expand")
    saved, Ws = [], []
    h = x2d
    for l in range(DEPTH):
        Ws.append(weights_of(l, h))
        h, s = layer_fwd(h, mem2d, Ws[l], Ps[l], bias, f"l{l}")
        saved.append(s)
    loss, dh, d_final = loss_head(h, final_norm_g, target.reshape(B * SEQ, D_MODEL), name="loss_head")
    gWs, gPs, ds_bands = [None] * DEPTH, [None] * DEPTH, []
    for l in range(DEPTH - 1, -1, -1):
        hooks = None if bwd_hooks is None else bwd_hooks(l)
        dh, gWs[l], gPs[l], ds = layer_bwd(dh, mem2d, Ws[l], Ps[l], bias, saved[l], f"l{l}", hooks)
        if grads_done is not None:
            grads_done(l, gWs[l])
        ds_bands.append(ds)
    d_rel = relbias_reduce(sum_cast([d.reshape(-1, BAND) for d in ds_bands], F32, name="ds_band_sum").reshape(-1, BLOCK, BAND),
                           name="relbias_reduce")
    return loss, dh.reshape(B, SEQ, D_MODEL), gWs, gPs, d_rel, d_final


def small_params(p, l):
    row = lambda name: p[name][l].reshape(1, -1)
    ffn_w, ffn_b = p["ffn_conv_w"][l], row("ffn_conv_b")
    return dict(
        norm_mix_g=row("norm_mix_g"), norm_cross_g=row("norm_cross_g"), norm_mem_g=row("norm_mem_g"), norm_ffn_g=row("norm_ffn_g"),
        bf=jnp.pad(row("b_forget"), ((0, 0), (0, LANES - N_HEADS))),
        lru_conv_w=p["lru_conv_w"][l], lru_conv_b=row("lru_conv_b"), wa=_block_diag_halves(p["lru_w_a"][l]), lru_b_a=row("lru_b_a"),
        wx=_block_diag_halves(p["lru_w_x"][l]), lru_b_x=row("lru_b_x"), lru_lambda=row("lru_lambda"),
        wu=ffn_w[:, :D_FF], wg=ffn_w[:, D_FF:], bu=ffn_b[:, :D_FF], bg=ffn_b[:, D_FF:])


def canonical_weights(w_in, w_out, w_cq, w_ck, w_cv, w_co, w_up, w_down):
    sb_fox, fox_f, rest = w_in[:, :6 * GROUP_W], w_in[:, 6 * GROUP_W:6 * GROUP_W + N_HEADS], w_in[:, 6 * GROUP_W + N_HEADS:]
    dil, lru = rest[:, :3 * GROUP_W], rest[:, 3 * GROUP_W:]
    pad = jnp.zeros((w_in.shape[0], AUX_W - 2 * GROUP_W - N_HEADS), w_in.dtype)
    return dict(qkv=jnp.concatenate([sb_fox, dil], axis=1), aux=jnp.concatenate([lru, fox_f, pad], axis=1), out=w_out,
                cq=w_cq, ckv=jnp.concatenate([w_ck, w_cv], axis=1), coT=w_co.T, upT=w_up.T, down=w_down)


def native_grads(g):
    qkv, aux = g["qkv"], g["aux"]
    a, b = 6 * GROUP_W, 6 * GROUP_W + N_HEADS
    w_in = jnp.zeros((qkv.shape[0], b + 5 * GROUP_W), qkv.dtype)
    w_in = w_in.at[:, :a].set(qkv[:, :a]).at[:, a:b].set(aux[:, 2 * GROUP_W:2 * GROUP_W + N_HEADS])
    w_in = w_in.at[:, b:b + 3 * GROUP_W].set(qkv[:, a:]).at[:, b + 3 * GROUP_W:].set(aux[:, :2 * GROUP_W])
    return (w_in, g["out"], g["cq"], g["ckv"][:, :GROUP_W], g["ckv"][:, GROUP_W:], g["coT"].T) + native_ffn_grads(g)


def native_ffn_grads(g):
    return (g["upT"].T, g["down"])


ANY = pl.BlockSpec(memory_space=pl.ANY)
VMEM_SPEC = pl.BlockSpec(memory_space=pltpu.VMEM)


def _place():
    x, y, c = lax.axis_index("x"), lax.axis_index("y"), lax.axis_index("c")
    other_chips = [(1 - x, y), (x, 1 - y), (1 - x, 1 - y)]
    return x, y, c, other_chips


def _gather_body(x_ref, out_ref, send_sems, recv_sems, local_sem):
    x, y, c, chips = _place()
    me, sibling = (x, y, c), (x, y, 1 - c)

    def slot(px, py, pc):
        return out_ref.at[4 * px + 2 * py + pc]

    def copy(k, block, to, src=None):
        return pltpu.make_async_remote_copy(
            src_ref=slot(*block) if src is None else src, dst_ref=slot(*block),
            send_sem=send_sems.at[k], recv_sem=recv_sems.at[k], device_id=to, device_id_type=MESH)

    if local_sem is not None:
        mine = pltpu.make_async_copy(x_ref, slot(*me), local_sem)
        mine.start()
    first = [copy(0, me, sibling, src=x_ref)]
    first += [copy(1 + j, me, (*chip, c), src=x_ref) for j, chip in enumerate(chips)]
    for cp in first:
        cp.start()
    passed = [copy(4 + j, (*chip, c), sibling) for j, chip in enumerate(chips)]
    for j, chip in enumerate(chips):
        copy(1 + j, (*chip, c), me).wait_recv()
        passed[j].start()
    copy(0, sibling, me).wait_recv()
    for j, chip in enumerate(chips):
        copy(4 + j, (*chip, 1 - c), me).wait_recv()
    for cp in first + passed:
        cp.wait_send()
    if local_sem is not None:
        mine.wait()


_GATHER_SEMS = [pltpu.SemaphoreType.DMA((7,)), pltpu.SemaphoreType.DMA((7,)), pltpu.SemaphoreType.DMA]


def allgather_hbm(shard, me, *, name):
    def body(x_ref, out_ref, done_ref, send_sems, recv_sems):
        _gather_body(x_ref, out_ref, send_sems, recv_sems, None)
        done_ref[...] = jnp.zeros_like(done_ref)

    others, done = pl.pallas_call(
        body, name=name, in_specs=[ANY], out_specs=[ANY, VMEM_SPEC],
        out_shape=[jax.ShapeDtypeStruct((N_DEV,) + shard.shape, shard.dtype), jax.ShapeDtypeStruct((8, LANES), F32)],
        scratch_shapes=_GATHER_SEMS[:2],
    )(shard)
    return lax.dynamic_update_slice(others, shard[None], (me, 0, 0)), done


def allgather_small(x, *, name, reduce=False):
    def body(x_ref, out_ref, second_ref, *sems):
        _gather_body(x_ref, out_ref, *sems)
        if reduce:
            acc = out_ref[0]
            for d in range(1, N_DEV):
                acc = acc + out_ref[d]
            second_ref[...] = acc
        else:
            second_ref[...] = jnp.zeros_like(second_ref)

    sd = jax.ShapeDtypeStruct
    return pl.pallas_call(
        body, name=name, in_specs=[VMEM_SPEC], out_specs=[VMEM_SPEC, VMEM_SPEC],
        out_shape=[sd((N_DEV,) + x.shape, x.dtype), sd(x.shape if reduce else (8, LANES), x.dtype)],
        scratch_shapes=_GATHER_SEMS, compiler_params=pltpu.CompilerParams(vmem_limit_bytes=VMEM_LIMIT_V7X),
    )(x)


N_CHIPS = 4


def pair_exchange(g, *, name):
    _, R, C = g.shape

    def body(g_ref, recv_ref, send_sems, recv_sems):
        x, y, c, _ = _place()
        sibling = (x, y, 1 - c)
        remote = [pltpu.make_async_remote_copy(
            src_ref=g_ref.at[2 * q + (1 - c)], dst_ref=recv_ref.at[q], send_sem=send_sems.at[q], recv_sem=recv_sems.at[q],
            device_id=sibling, device_id_type=MESH) for q in range(N_CHIPS)]
        for cp in remote:
            cp.start()
        for cp in remote:
            cp.wait_recv()
        for cp in remote:
            cp.wait_send()

    return pl.pallas_call(
        body, name=name, in_specs=[ANY], out_specs=ANY, out_shape=jax.ShapeDtypeStruct((N_CHIPS, R, C), g.dtype),
        scratch_shapes=[pltpu.SemaphoreType.DMA((N_CHIPS,))] * 2,
    )(g)


def chip_exchange(s, *, name):
    _, R, C = s.shape

    def body(s_ref, o0, o1, o2, send_sems, recv_sems):
        x, y, c, chips = _place()
        outs = (o0, o1, o2)
        copies = [pltpu.make_async_remote_copy(
            src_ref=s_ref.at[2 * cx + cy], dst_ref=outs[j], send_sem=send_sems.at[j], recv_sem=recv_sems.at[j],
            device_id=(cx, cy, c), device_id_type=MESH) for j, (cx, cy) in enumerate(chips)]
        for cp in copies:
            cp.start()
        for cp in copies:
            cp.wait_recv()
        for cp in copies:
            cp.wait_send()

    sd = jax.ShapeDtypeStruct((R, C), s.dtype)
    return pl.pallas_call(
        body, name=name, in_specs=[ANY], out_specs=[ANY] * 3, out_shape=[sd] * 3,
        scratch_shapes=[pltpu.SemaphoreType.DMA((3,)), pltpu.SemaphoreType.DMA((3,))],
    )(s)


HBM_SPEC = pl.BlockSpec(memory_space=pltpu.HBM)
SEM_SPEC = pl.BlockSpec(memory_space=pltpu.SEMAPHORE)
N_PEERS = N_DEV - 1


def _peers():
    x, y, c = lax.axis_index("x"), lax.axis_index("y"), lax.axis_index("c")
    flip = lambda v, bit: 1 - v if bit else v
    out = []
    for k in range(1, N_DEV):
        px, py, pc = flip(x, (k >> 2) & 1), flip(y, (k >> 1) & 1), flip(c, k & 1)
        out.append(((px, py, pc), 4 * px + 2 * py + pc))
    return out, 4 * x + 2 * y + c


def _peer_copies(src_ref, land_ref, send_sems, recv_sems, scatter, landing):
    peers, me = _peers()
    return [pltpu.make_async_remote_copy(
        src_ref=src_ref.at[idx] if scatter else src_ref, dst_ref=land_ref.at[me if landing == "mine" else idx],
        send_sem=send_sems.at[k], recv_sem=recv_sems.at[k], device_id=peer, device_id_type=MESH)
        for k, (peer, idx) in enumerate(peers)]


def exchange_start(src, scatter, *, name):
    shape = (N_DEV,) + src.shape[-2:]

    def body(src_ref, land_ref, send_sems, recv_sems, src_thru, land_thru, token):
        for cp in _peer_copies(src_ref, land_ref, send_sems, recv_sems, scatter, "mine"):
            cp.start()
        token[...] = jnp.zeros_like(token)

    sems = pltpu.SemaphoreType.DMA((N_PEERS,))
    return pl.pallas_call(
        body, name=name,
        out_shape=(sems, sems, pltpu.HBM(src.shape, src.dtype), pltpu.HBM(shape, src.dtype), jax.ShapeDtypeStruct((8, LANES), F32)),
        in_specs=(HBM_SPEC, HBM_SPEC), out_specs=(SEM_SPEC, SEM_SPEC, HBM_SPEC, HBM_SPEC, VMEM_SPEC),
        input_output_aliases={0: 2, 1: 3},
        compiler_params=pltpu.CompilerParams(has_side_effects=pltpu.SideEffectType.DATAFLOW_SIDE_EFFECTING),
    )(pltpu.with_memory_space_constraint(src, pltpu.HBM), pltpu.with_memory_space_constraint(lax.empty(shape, src.dtype), pltpu.HBM))


def exchange_wait(started, after, scatter, *, name):
    send_sems, recv_sems, src_thru, land_thru, _ = started

    def body(src_ref, land_ref, send_sems, recv_sems, after_ref, src_dead, got_ref):
        for cp in _peer_copies(src_ref, land_ref, send_sems, recv_sems, scatter, "theirs"):
            cp.wait_send()
            cp.wait_recv()

    return pl.pallas_call(
        body, name=name, out_shape=(pltpu.HBM(src_thru.shape, src_thru.dtype), pltpu.HBM(land_thru.shape, land_thru.dtype)),
        in_specs=(HBM_SPEC, HBM_SPEC, SEM_SPEC, SEM_SPEC, ANY), out_specs=(HBM_SPEC, HBM_SPEC), input_output_aliases={0: 0, 1: 1},
        compiler_params=pltpu.CompilerParams(has_side_effects=pltpu.SideEffectType.DATAFLOW_SIDE_EFFECTING),
    )(src_thru, land_thru, send_sems, recv_sems, after)[1]


def sum_blocks(blocks, *, name):
    n, R, C = blocks.shape
    tr = _largest_tile(R, 512, 16)

    def body(b_ref, o_ref):
        d = pl.program_id(1)
        v = b_ref[...].astype(F32)

        @pl.when(d == 0)
        def _():
            o_ref[...] = v

        @pl.when(d > 0)
        def _():
            o_ref[...] += v

    return pl.pallas_call(
        body, name=name, grid=(R // tr, n),
        in_specs=[pl.BlockSpec((None, tr, C), lambda i, d: (d, i, 0))], out_specs=pl.BlockSpec((tr, C), lambda i, d: (i, 0)),
        out_shape=jax.ShapeDtypeStruct((R, C), F32), compiler_params=_params("parallel", "arbitrary"),
    )(blocks)


WEIGHTS = ("norm_mix_g", "w_in", "b_forget", "lru_conv_w", "lru_conv_b", "lru_w_a", "lru_b_a", "lru_w_x", "lru_b_x", "lru_lambda",
           "w_out", "norm_cross_g", "norm_mem_g", "w_cq", "w_ck", "w_cv", "w_co", "norm_ffn_g", "w_up", "ffn_conv_w", "ffn_conv_b",
           "w_down", "rel_bias", "final_norm_g")
LARGE = ("w_in", "w_out", "w_cq", "w_ck", "w_cv", "w_co", "w_up", "w_down")
COLUMN_SPLIT_SMALL = ("lru_conv_w", "ffn_conv_w")
PACK = (("qkv", 128, 2304), ("aux", 128, 640), ("out", 128, 1024), ("cq", 128, 256), ("ckv", 128, 512), ("coT", 128, 256),
        ("upT", 704, 1024), ("down", 352, 1024))
PACK_W = 1024


def _pack_rows(parts):
    return jnp.concatenate([p.reshape(-1, PACK_W) for p in parts], axis=0)


def _pad_rows(flat, mult=8 * LANES):
    n = flat.shape[0]
    return jnp.pad(flat, (0, (-n) % mult)).reshape(-1, LANES)


def kernel(x, mem, norm_mix_g, w_in, b_forget, lru_conv_w, lru_conv_b, lru_w_a, lru_b_a, lru_w_x, lru_b_x, lru_lambda, w_out, norm_cross_g, norm_mem_g, w_cq, w_ck, w_cv, w_co, norm_ffn_g, w_up, ffn_conv_w, ffn_conv_b, w_down, rel_bias, final_norm_g, loss_target, m_norm_mix_g, m_w_in, m_b_forget, m_lru_conv_w, m_lru_conv_b, m_lru_w_a, m_lru_b_a, m_lru_w_x, m_lru_b_x, m_lru_lambda, m_w_out, m_norm_cross_g, m_norm_mem_g, m_w_cq, m_w_ck, m_w_cv, m_w_co, m_norm_ffn_g, m_w_up, m_ffn_conv_w, m_ffn_conv_b, m_w_down, m_rel_bias, m_final_norm_g, v_norm_mix_g, v_w_in, v_b_forget, v_lru_conv_w, v_lru_conv_b, v_lru_w_a, v_lru_b_a, v_lru_w_x, v_lru_b_x, v_lru_lambda, v_w_out, v_norm_cross_g, v_norm_mem_g, v_w_cq, v_w_ck, v_w_cv, v_w_co, v_norm_ffn_g, v_w_up, v_ffn_conv_w, v_ffn_conv_b, v_w_down, v_rel_bias, v_final_norm_g):
    w = dict(norm_mix_g=norm_mix_g, w_in=w_in, b_forget=b_forget, lru_conv_w=lru_conv_w, lru_conv_b=lru_conv_b, lru_w_a=lru_w_a,
             lru_b_a=lru_b_a, lru_w_x=lru_w_x, lru_b_x=lru_b_x, lru_lambda=lru_lambda, w_out=w_out, norm_cross_g=norm_cross_g,
             norm_mem_g=norm_mem_g, w_cq=w_cq, w_ck=w_ck, w_cv=w_cv, w_co=w_co, norm_ffn_g=norm_ffn_g, w_up=w_up,
             ffn_conv_w=ffn_conv_w, ffn_conv_b=ffn_conv_b, w_down=w_down, rel_bias=rel_bias, final_norm_g=final_norm_g)
    m = dict(norm_mix_g=m_norm_mix_g, w_in=m_w_in, b_forget=m_b_forget, lru_conv_w=m_lru_conv_w, lru_conv_b=m_lru_conv_b,
             lru_w_a=m_lru_w_a, lru_b_a=m_lru_b_a, lru_w_x=m_lru_w_x, lru_b_x=m_lru_b_x, lru_lambda=m_lru_lambda, w_out=m_w_out,
             norm_cross_g=m_norm_cross_g, norm_mem_g=m_norm_mem_g, w_cq=m_w_cq, w_ck=m_w_ck, w_cv=m_w_cv, w_co=m_w_co,
             norm_ffn_g=m_norm_ffn_g, w_up=m_w_up, ffn_conv_w=m_ffn_conv_w, ffn_conv_b=m_ffn_conv_b, w_down=m_w_down,
             rel_bias=m_rel_bias, final_norm_g=m_final_norm_g)
    v = dict(norm_mix_g=v_norm_mix_g, w_in=v_w_in, b_forget=v_b_forget, lru_conv_w=v_lru_conv_w, lru_conv_b=v_lru_conv_b,
             lru_w_a=v_lru_w_a, lru_b_a=v_lru_b_a, lru_w_x=v_lru_w_x, lru_b_x=v_lru_b_x, lru_lambda=v_lru_lambda, w_out=v_w_out,
             norm_cross_g=v_norm_cross_g, norm_mem_g=v_norm_mem_g, w_cq=v_w_cq, w_ck=v_w_ck, w_cv=v_w_cv, w_co=v_w_co,
             norm_ffn_g=v_norm_ffn_g, w_up=v_w_up, ffn_conv_w=v_ffn_conv_w, ffn_conv_b=v_ffn_conv_b, w_down=v_w_down,
             rel_bias=v_rel_bias, final_norm_g=v_final_norm_g)
    me = 4 * lax.axis_index("x") + 2 * lax.axis_index("y") + lax.axis_index("c")

    conv_shard = jnp.concatenate([w[n].reshape(-1) for n in COLUMN_SPLIT_SMALL])
    conv_all, conv_gathered = allgather_small(_pad_rows(conv_shard), name="gather_conv")
    conv_all = conv_all.reshape(N_DEV, -1)
    full = dict(w)
    off = 0
    for n in COLUMN_SPLIT_SMALL:
        d, k, c = w[n].shape
        blocks = conv_all[:, off:off + d * k * c].reshape(N_DEV, d, k, c)
        full[n] = blocks.transpose(1, 2, 0, 3).reshape(d, k, N_DEV * c)
        off += d * k * c

    IN, MID, FFN = PACK[:2], PACK[2:6], PACK[6:]
    REST = MID + FFN

    def packed_shard(l, group):
        canon = canonical_weights(*[w[n][l] for n in LARGE])
        return _pack_rows([canon[k].astype(BF16) for k, _, _ in group])

    def unpack_weights(packed, group):
        W, row = {}, 0
        for k, r, c in group:
            n_rows = r * c // PACK_W
            W[k] = packed[:, row:row + n_rows].reshape(N_DEV * r, c)
            row += n_rows
        if "upT" in W:
            upT = W.pop("upT")
            W["up_u"], W["up_g"] = upT[:D_FF], upT[D_FF:]
        return W

    def packed_grads(gW, group):
        g = dict(gW)
        if "up_u" in g:
            g["upT"] = jnp.concatenate([g.pop("up_u"), g.pop("up_g")], axis=0)
        return jnp.concatenate([g[k].reshape(N_DEV, r * c // PACK_W, PACK_W) for k, r, c in group], axis=1)

    def unpack_grads(shard_sum, group):
        g, row = {}, 0
        for k, r, c in group:
            n_rows = r * c // PACK_W
            g[k] = shard_sum[row:row + n_rows].reshape(r, c)
            row += n_rows
        return g

    def own_block_in(landed, block):
        return lax.dynamic_update_slice(landed, block[None], (me, 0, 0))

    def gathered_weights(copies, shard, after, group, name):
        return unpack_weights(own_block_in(exchange_wait(copies, after, False, name=name), shard), group)

    def scattered_sum(src, copies, after, tag):
        landed = exchange_wait(copies, after, True, name=f"{tag}_wait")
        mine = lax.dynamic_index_in_dim(src, me, axis=0, keepdims=False)
        return sum_blocks(own_block_in(landed, mine), name=f"{tag}_sum")

    last = DEPTH - 1
    in0, gathered = allgather_hbm(packed_shard(0, IN) + conv_gathered[0, 0].astype(BF16), me, name="gather_weights")
    rest0_shard = packed_shard(0, REST) + gathered[0, 0].astype(BF16)
    gather_rest0 = exchange_start(rest0_shard, False, name="gather_rest0_start")
    last_shard = packed_shard(last, PACK) + gather_rest0[4][0, 0].astype(BF16)
    gather_last = exchange_start(last_shard, False, name="gather_last_start")
    started = gather_last[4][0, 0]
    layer_weights = {}

    def weights_of(l, h):
        if l == 0:
            W = unpack_weights(in0, IN)
            W["rest"] = lambda after: gathered_weights(gather_rest0, rest0_shard, after, REST, "gather_rest0_wait")
        else:
            assert l == last
            W = gathered_weights(gather_last, last_shard, h, PACK, "gather_last_wait")
        layer_weights[l] = W
        return W

    in_flight = {}

    def scatter(key, g_all, name):
        in_flight[key] = (g_all, exchange_start(g_all, True, name=name))
        return in_flight[key][1][4][0, 0].astype(BF16)

    def grads_done(l, gW):
        if l == last:
            W0 = layer_weights[0]
            W0["down"] = W0["down"] + scatter("last", packed_grads(gW, PACK), "grads_last_start")

    def ffn0_grads_done(gW, W, s):
        W["coT"] = W["coT"] + scatter("ffn0", packed_grads({k: gW[k] for k in ("up_u", "up_g", "down")}, FFN), "grads_ffn0_start")

    def mid0_grads_done(gW, W, s):
        s["cumc"] = s["cumc"] + scatter("mid0", packed_grads({k: gW[k] for k, _, _ in MID}, MID), "grads_mid0_start").astype(F32)

    Ps = [small_params(full, l) for l in range(DEPTH)]
    Ps[0]["norm_mix_g"] = Ps[0]["norm_mix_g"] + started
    loss, grad_x, gWs, gPs, d_rel, d_final = local_step(
        x, mem, loss_target, weights_of, Ps, rel_bias, final_norm_g.reshape(1, -1), grads_done,
        lambda l: {"ffn": ffn0_grads_done, "mid": mid0_grads_done} if l == 0 else None)

    shard_grads = {last: unpack_grads(scattered_sum(*in_flight["last"], grad_x, "grads_last"), PACK)}
    shard_grads[0] = unpack_grads(scattered_sum(*in_flight["ffn0"], grad_x, "grads_ffn0"), FFN)
    shard_grads[0].update(unpack_grads(scattered_sum(*in_flight["mid0"], grad_x, "grads_mid0"), MID))

    g_all = packed_grads({k: gWs[0][k] for k, _, _ in IN}, IN)
    rows = g_all.shape[1]
    got = pair_exchange(g_all, name="grads_pair_exchange")
    own = lax.dynamic_index_in_dim(g_all.reshape(N_CHIPS, 2, rows, PACK_W), lax.axis_index("c"), axis=1, keepdims=False)
    pair = sum_cast([own.reshape(-1, PACK_W), got.reshape(-1, PACK_W)], GRAD_WIRE, name="grads_pair_sum").reshape(N_CHIPS, rows, PACK_W)
    from_x, from_y, from_xy = chip_exchange(pair, name="grads_chip_exchange")
    mine = lax.dynamic_index_in_dim(pair, 2 * lax.axis_index("x") + lax.axis_index("y"), axis=0, keepdims=False)
    shard_grads[0].update(unpack_grads(sum_cast([mine, from_x, from_y, from_xy], F32, name="grads_chip_sum"), IN))

    grads = {}
    per_layer = [native_grads(shard_grads[l]) for l in range(DEPTH)]
    for i, n in enumerate(LARGE):
        grads[n] = jnp.stack([per_layer[l][i] for l in range(DEPTH)])

    small_names = [n for n in WEIGHTS if n not in LARGE and n not in ("rel_bias", "final_norm_g")]
    pieces = [gPs[l][n].reshape(-1) for n in small_names for l in range(DEPTH)] + [d_rel.reshape(-1), d_final.reshape(-1), loss[0, :1]]
    sizes = [p.shape[0] for p in pieces]
    _, total = allgather_small(_pad_rows(jnp.concatenate(pieces)), name="allreduce_small", reduce=True)
    total = total.reshape(-1)
    off, it = 0, iter(sizes)
    for n in small_names:
        per = []
        for l in range(DEPTH):
            sz = next(it)
            per.append(total[off:off + sz])
            off += sz
        full_shape = (DEPTH,) + full[n].shape[1:]
        gfull = jnp.stack(per).reshape(full_shape)
        if n in COLUMN_SPLIT_SMALL:
            c = w[n].shape[-1]
            gfull = lax.dynamic_slice_in_dim(gfull, me * c, c, axis=gfull.ndim - 1)
        grads[n] = gfull
    grads["rel_bias"] = total[off:off + rel_bias.size].reshape(rel_bias.shape)
    off += rel_bias.size
    grads["final_norm_g"] = total[off:off + D_MODEL]
    off += D_MODEL
    loss_out = total[off]

    delta, new_m, new_v = {}, {}, {}
    for n in LARGE:
        shape = w[n].shape
        two_d = lambda a: a.reshape(-1, shape[-1])
        d_, m_, v_ = adamw(two_d(w[n]), two_d(grads[n]), two_d(m[n]), two_d(v[n]), name=f"adamw_{n}")
        delta[n], new_m[n], new_v[n] = d_.reshape(shape), m_.reshape(shape), v_.reshape(shape)
    small_all = [n for n in WEIGHTS if n not in LARGE]
    two_d = lambda a: a.reshape(-1, a.shape[-1])
    d_, m_, v_ = adamw_many(*[[two_d(src[n]) for n in small_all] for src in (w, grads, m, v)], name="adamw_small")
    for i, n in enumerate(small_all):
        delta[n], new_m[n], new_v[n] = (a[i].reshape(w[n].shape) for a in (d_, m_, v_))

    return (loss_out, grad_x, *[grads[n] for n in WEIGHTS], *[delta[n] for n in WEIGHTS], *[new_m[n] for n in WEIGHTS],
            *[new_v[n] for n in WEIGHTS])
```

```python
import functools
import math

import numpy as np
import jax
import jax.numpy as jnp
from jax import lax
from jax.experimental import pallas as pl
from jax.experimental.pallas import tpu as pltpu

F32 = jnp.float32
BF16 = jnp.bfloat16
MESH = pl.DeviceIdType.MESH

N_DEV = 8
D_MODEL = 1024
SEQ = 2048
DEPTH = 2
HEAD_DIM = 64
N_HEADS = 4
GROUP_W = N_HEADS * HEAD_DIM
D_FF = 2816
N_MEM = 256
NUM_BUCKETS = 32
MAX_DISTANCE = 2048
BLOCK = 128
DILATIONS = (1, 4, 16)
EPS = 1e-6
LRU_C = 8.0
Q_SCALE = HEAD_DIM ** -0.5
AUX_W = 640
LRU_HALF_W = 128
LRU_HALVES = GROUP_W // LRU_HALF_W
ADAM_LR, ADAM_B1, ADAM_B2, ADAM_EPS, ADAM_WD, ADAM_STEP = 0.001, 0.9, 0.999, 1e-08, 0.01, 10

VMEM_LIMIT_V7X = 48 * 1024 * 1024


def _params(*sem):
    return pltpu.CompilerParams(dimension_semantics=sem if sem else None, vmem_limit_bytes=VMEM_LIMIT_V7X)


def _pick(n, cands):
    for c in cands:
        if n % c == 0:
            return c
    return n


def _largest_tile(n, cap, align):
    best = None
    for t in range(align, min(n, cap) + 1, align):
        if n % t == 0:
            best = t
    return n if best is None else best


def matmul(a, b, *, name, trans_a=False, trans_b=False, out_dtype=F32, residual=None):
    (K, M) = a.shape if trans_a else a.shape[::-1]
    (N, Kb) = b.shape if trans_b else b.shape[::-1]
    assert K == Kb, (a.shape, b.shape)
    tm = _largest_tile(M, 1408 if trans_a else (1024 if K <= 1024 else 512), 128)
    tn = _largest_tile(N, 1408, 128)
    tk = _largest_tile(K, 1024 if trans_a else 2816, 128)
    nk = K // tk
    a_spec = pl.BlockSpec((tk, tm), lambda i, j, k: (k, i)) if trans_a else pl.BlockSpec((tm, tk), lambda i, j, k: (i, k))
    b_spec = pl.BlockSpec((tn, tk), lambda i, j, k: (j, k)) if trans_b else pl.BlockSpec((tk, tn), lambda i, j, k: (k, j))
    o_spec = pl.BlockSpec((tm, tn), lambda i, j, k: (i, j))
    dims = (((0 if trans_a else 1,), (1 if trans_b else 0,)), ((), ()))
    has_res = residual is not None

    def body(*refs):
        a_ref, b_ref = refs[0], refs[1]
        r_ref = refs[2] if has_res else None
        part = lax.dot_general(a_ref[...].astype(BF16), b_ref[...].astype(BF16), dims, preferred_element_type=F32)
        if nk == 1:
            if has_res:
                part = part + r_ref[...].astype(F32)
            refs[-1][...] = part.astype(out_dtype)
            return
        o_ref, acc_ref = refs[-2], refs[-1]
        k = pl.program_id(2)

        @pl.when(k == 0)
        def _():
            acc_ref[...] = part

        @pl.when(k > 0)
        def _():
            acc_ref[...] += part

        @pl.when(k == nk - 1)
        def _():
            r = acc_ref[...]
            if has_res:
                r = r + r_ref[...].astype(F32)
            o_ref[...] = r.astype(out_dtype)

    ops = (a, b) + ((residual,) if has_res else ())
    return pl.pallas_call(
        body, name=name, grid=(M // tm, N // tn, nk),
        in_specs=[a_spec, b_spec] + ([o_spec] if has_res else []),
        out_specs=o_spec, out_shape=jax.ShapeDtypeStruct((M, N), out_dtype),
        scratch_shapes=[pltpu.VMEM((tm, tn), F32)] if nk > 1 else [],
        compiler_params=_params("parallel", "parallel", "arbitrary"),
    )(*ops)


def rmsnorm_fwd(x, g, *, name):
    R, D = x.shape
    tr = _pick(R, (512, 256))

    def body(x_ref, g_ref, o_ref):
        xv = x_ref[...]
        r = lax.rsqrt(jnp.mean(xv * xv, axis=-1, keepdims=True) + EPS)
        o_ref[...] = (xv * r * g_ref[...]).astype(BF16)

    return pl.pallas_call(
        body, name=name, grid=(R // tr,),
        in_specs=[pl.BlockSpec((tr, D), lambda i: (i, 0)), pl.BlockSpec((1, D), lambda i: (0, 0))],
        out_specs=pl.BlockSpec((tr, D), lambda i: (i, 0)), out_shape=jax.ShapeDtypeStruct((R, D), BF16),
        compiler_params=_params("parallel"),
    )(x, g)


def rmsnorm_bwd(x, g, dh, dres, *, name):
    R, D = x.shape
    tr = _pick(R, (512, 256))
    has_res = dres is not None

    def body(*refs):
        x_ref, g_ref, dh_ref = refs[:3]
        dx_ref, dg_ref = refs[-2], refs[-1]
        xv = x_ref[...]
        r = lax.rsqrt(jnp.mean(xv * xv, axis=-1, keepdims=True) + EPS)
        n = xv * r
        dhv = dh_ref[...]
        dn = dhv * g_ref[...]
        dx = r * (dn - n * jnp.mean(dn * n, axis=-1, keepdims=True))
        if has_res:
            dx = dx + refs[3][...]
        dx_ref[...] = dx
        part = jnp.sum(dhv * n, axis=0, keepdims=True)

        @pl.when(pl.program_id(0) == 0)
        def _():
            dg_ref[...] = part

        @pl.when(pl.program_id(0) > 0)
        def _():
            dg_ref[...] += part

    row = pl.BlockSpec((tr, D), lambda i: (i, 0))
    vec = pl.BlockSpec((1, D), lambda i: (0, 0))
    ops = (x, g, dh) + ((dres,) if has_res else ())
    return pl.pallas_call(
        body, name=name, grid=(R // tr,),
        in_specs=[row, vec, row] + ([row] if has_res else []),
        out_specs=[row, vec],
        out_shape=[jax.ShapeDtypeStruct((R, D), F32), jax.ShapeDtypeStruct((1, D), F32)],
        compiler_params=_params("arbitrary"),
    )(*ops)


_SQRT_HALF = 0.7071067811865476
_INV_SQRT_2PI = 0.3989422804014327


def _normal_cdf_pdf(x):
    ax = jnp.abs(x) * _SQRT_HALF
    t = 1.0 / (1.0 + 0.3275911 * ax)
    poly = t * (0.254829592 + t * (-0.284496736 + t * (1.421413741 + t * (-1.453152027 + t * 1.061405429))))
    e = jnp.exp(-0.5 * x * x)
    half_tail = 0.5 * poly * e
    return jnp.where(x < 0, half_tail, 1.0 - half_tail), e


def _gelu_cdf(x):
    return _normal_cdf_pdf(x)[0]


def _gelu_and_grad(x):
    cdf, e = _normal_cdf_pdf(x)
    return x * cdf, cdf + x * _INV_SQRT_2PI * e


def _shift_down(main, halo, first, shifts):
    halo = jnp.where(first, 0.0, halo)
    ext = jnp.concatenate([halo, main], axis=0)
    return [pltpu.roll(ext, s, 0)[8:] for s in shifts]


def _conv3(main, halo, first, w, b):
    m1, m2 = _shift_down(main, halo, first, (1, 2))
    return ((b + w[0:1] * m2) + w[1:2] * m1) + w[2:3] * main, m1, m2


def glu_fwd(hu, hg, wu, wg, bu, bg, *, name):
    T, F = hu.shape
    tm, tf = 512, _largest_tile(F, 704, 128)
    hb = tm // 8
    blocks_per_example = SEQ // tm

    def body(hu_ref, hg_ref, hau_ref, hag_ref, wu_ref, wg_ref, bu_ref, bg_ref, o_ref):
        first = pl.program_id(0) % blocks_per_example == 0
        up, _, _ = _conv3(hu_ref[...], hau_ref[...], first, wu_ref[...], bu_ref[...])
        gate, _, _ = _conv3(hg_ref[...], hag_ref[...], first, wg_ref[...], bg_ref[...])
        o_ref[...] = (gate * _gelu_cdf(gate) * up).astype(BF16)

    main = pl.BlockSpec((tm, tf), lambda i, j: (i, j))
    halo = pl.BlockSpec((8, tf), lambda i, j: (jnp.maximum(i * hb - 1, 0), j))
    w3 = pl.BlockSpec((3, tf), lambda i, j: (0, j))
    b1 = pl.BlockSpec((1, tf), lambda i, j: (0, j))
    return pl.pallas_call(
        body, name=name, grid=(T // tm, F // tf),
        in_specs=[main, main, halo, halo, w3, w3, b1, b1],
        out_specs=main, out_shape=jax.ShapeDtypeStruct((T, F), BF16),
        compiler_params=_params("parallel", "parallel"),
    )(hu, hg, hu, hg, wu, wg, bu, bg)


def glu_bwd(hu, hg, dact, wu, wg, bu, bg, *, name):
    T, F = hu.shape
    tm, tf = 512, _largest_tile(F, 704, 128)
    hb = tm // 8
    blocks_per_example = SEQ // tm
    n_halo_blocks = T // 8
    n_ext = tm + 8

    def body(hu_ref, hg_ref, hau_ref, hag_ref, hnu_ref, hng_ref, da_ref, dan_ref, wu_ref, wg_ref, bu_ref, bg_ref,
             du_ref, dg_ref, dwu_ref, dwg_ref, dbu_ref, dbg_ref):
        i = pl.program_id(1)
        first = i % blocks_per_example == 0
        last = i % blocks_per_example == blocks_per_example - 1
        wu, wg = wu_ref[...], wg_ref[...]

        def conv_ext(main_ref, prev_ref, next_ref, w, b):
            ext = jnp.concatenate([jnp.where(first, 0.0, prev_ref[...]), main_ref[...], next_ref[...]], axis=0)
            x0, x1, x2 = ext[8:], pltpu.roll(ext, 1, 0)[8:], pltpu.roll(ext, 2, 0)[8:]
            return ((b + w[0:1] * x2) + w[1:2] * x1) + w[2:3] * x0, x0, x1, x2

        up, xu, u1, u2 = conv_ext(hu_ref, hau_ref, hnu_ref, wu, bu_ref[...])
        gate, xg, g1, g2 = conv_ext(hg_ref, hag_ref, hng_ref, wg, bg_ref[...])
        act, dact_dgate = _gelu_and_grad(gate)
        da = jnp.concatenate([da_ref[...], jnp.where(last, 0.0, dan_ref[...])], axis=0)
        dup = da * act
        dgate = da * up * dact_dgate

        def conv_t(d, w):
            return (w[2:3] * d[:tm] + w[1:2] * pltpu.roll(d, n_ext - 1, 0)[:tm] + w[0:1] * pltpu.roll(d, n_ext - 2, 0)[:tm]).astype(BF16)

        du_ref[...] = conv_t(dup, wu)
        dg_ref[...] = conv_t(dgate, wg)

        def sums(d, x0, x1, x2):
            s = lambda v: jnp.sum(v[:tm], axis=0, keepdims=True)
            return jnp.concatenate([s(d * x2), s(d * x1), s(d * x0)], axis=0), s(d)

        pwu, pbu = sums(dup, xu, u1, u2)
        pwg, pbg = sums(dgate, xg, g1, g2)

        @pl.when(i == 0)
        def _():
            dwu_ref[...] = pwu
            dwg_ref[...] = pwg
            dbu_ref[...] = pbu
            dbg_ref[...] = pbg

        @pl.when(i > 0)
        def _():
            dwu_ref[...] += pwu
            dwg_ref[...] += pwg
            dbu_ref[...] += pbu
            dbg_ref[...] += pbg

    main = pl.BlockSpec((tm, tf), lambda j, i: (i, j))
    before = pl.BlockSpec((8, tf), lambda j, i: (jnp.maximum(i * hb - 1, 0), j))
    after = pl.BlockSpec((8, tf), lambda j, i: (jnp.minimum((i + 1) * hb, n_halo_blocks - 1), j))
    w3 = pl.BlockSpec((3, tf), lambda j, i: (0, j))
    b1 = pl.BlockSpec((1, tf), lambda j, i: (0, j))
    sd = jax.ShapeDtypeStruct
    return pl.pallas_call(
        body, name=name, grid=(F // tf, T // tm),
        in_specs=[main, main, before, before, after, after, main, after, w3, w3, b1, b1],
        out_specs=[main, main, w3, w3, b1, b1],
        out_shape=[sd((T, F), BF16), sd((T, F), BF16), sd((3, F), F32), sd((3, F), F32), sd((1, F), F32), sd((1, F), F32)],
        compiler_params=_params("parallel", "arbitrary"),
    )(hu, hg, hu, hg, hu, hg, dact, dact, wu, wg, bu, bg)


def loss_head(x, g, target, *, name):
    T, D = x.shape
    tr = 256

    def body(x_ref, g_ref, t_ref, loss_ref, dx_ref, dg_ref):
        xv = x_ref[...]
        gv = g_ref[...]
        r = lax.rsqrt(jnp.mean(xv * xv, axis=-1, keepdims=True) + EPS)
        n = xv * r
        err = n * gv - t_ref[...]
        part_loss = jnp.zeros((1, 128), F32) + 0.5 * jnp.sum(jnp.mean(err * err, axis=-1, keepdims=True))
        dy = err * (1.0 / D)
        dn = dy * gv
        dx_ref[...] = r * (dn - n * jnp.mean(dn * n, axis=-1, keepdims=True))
        part_g = jnp.sum(dy * n, axis=0, keepdims=True)

        @pl.when(pl.program_id(0) == 0)
        def _():
            loss_ref[...] = part_loss
            dg_ref[...] = part_g

        @pl.when(pl.program_id(0) > 0)
        def _():
            loss_ref[...] += part_loss
            dg_ref[...] += part_g

    row = pl.BlockSpec((tr, D), lambda i: (i, 0))
    vec = pl.BlockSpec((1, D), lambda i: (0, 0))
    sd = jax.ShapeDtypeStruct
    return pl.pallas_call(
        body, name=name, grid=(T // tr,),
        in_specs=[row, vec, row],
        out_specs=[pl.BlockSpec((1, 128), lambda i: (0, 0)), row, vec],
        out_shape=[sd((1, 128), F32), sd((T, D), F32), sd((1, D), F32)],
        compiler_params=_params("arbitrary"),
    )(x, g, target)


def adamw(w, g, m, v, *, name):
    R, C = w.shape
    tr = _pick(R, (256, 128, 64, 32, 16, 8))

    def body(w_ref, g_ref, m_ref, v_ref, d_ref, nm_ref, nv_ref):
        gv = g_ref[...]
        mn = ADAM_B1 * m_ref[...] + (1.0 - ADAM_B1) * gv
        vn = ADAM_B2 * v_ref[...] + (1.0 - ADAM_B2) * (gv * gv)
        m_hat = mn / (1.0 - ADAM_B1 ** ADAM_STEP)
        v_hat = vn / (1.0 - ADAM_B2 ** ADAM_STEP)
        d_ref[...] = -ADAM_LR * (m_hat / (jnp.sqrt(v_hat) + ADAM_EPS) + ADAM_WD * w_ref[...])
        nm_ref[...] = mn
        nv_ref[...] = vn

    blk = pl.BlockSpec((tr, C), lambda i: (i, 0))
    sd = jax.ShapeDtypeStruct((R, C), F32)
    return pl.pallas_call(
        body, name=name, grid=(R // tr,), in_specs=[blk] * 4, out_specs=[blk] * 3, out_shape=[sd] * 3,
        compiler_params=_params("parallel"),
    )(w, g, m, v)


def adamw_many(ws, gs, ms, vs, *, name):
    n = len(ws)

    def body(*refs):
        ins, outs = refs[:4 * n], refs[4 * n:]
        for i in range(n):
            w_ref, g_ref, m_ref, v_ref = ins[i], ins[n + i], ins[2 * n + i], ins[3 * n + i]
            gv = g_ref[...]
            mn = ADAM_B1 * m_ref[...] + (1.0 - ADAM_B1) * gv
            vn = ADAM_B2 * v_ref[...] + (1.0 - ADAM_B2) * (gv * gv)
            m_hat = mn / (1.0 - ADAM_B1 ** ADAM_STEP)
            v_hat = vn / (1.0 - ADAM_B2 ** ADAM_STEP)
            outs[i][...] = -ADAM_LR * (m_hat / (jnp.sqrt(v_hat) + ADAM_EPS) + ADAM_WD * w_ref[...])
            outs[n + i][...] = mn
            outs[2 * n + i][...] = vn

    vm = pl.BlockSpec(memory_space=pltpu.VMEM)
    shapes = [jax.ShapeDtypeStruct(w.shape, F32) for w in ws]
    res = pl.pallas_call(
        body, name=name, in_specs=[vm] * (4 * n), out_specs=[vm] * (3 * n), out_shape=shapes * 3, compiler_params=_params(),
    )(*ws, *gs, *ms, *vs)
    return res[:n], res[n:2 * n], res[2 * n:]


def _softplus(x):
    return jnp.maximum(x, 0.0) + jnp.log(1.0 + jnp.exp(-jnp.abs(x)))


def _lru_gates(x, cw, cb, wa, ba, wx, bx, lam):
    S = x.shape[0]
    row = lax.broadcasted_iota(jnp.int32, (S, 1), 0)

    def back(s):
        return jnp.where(row >= s, pltpu.roll(x, s, 0), 0.0)

    xc = (((cb + cw[0:1] * back(3)) + cw[1:2] * back(2)) + cw[2:3] * back(1)) + cw[3:4] * x
    xb = xc.astype(BF16)
    r = jax.nn.sigmoid(jnp.dot(xb, wa, preferred_element_type=F32) + ba)
    ig = jax.nn.sigmoid(jnp.dot(xb, wx, preferred_element_type=F32) + bx)
    sp = _softplus(-lam)
    la = -LRU_C * r * sp
    a = jnp.exp(la)
    y = 2.0 * la
    one_minus_a2 = jnp.where(y > -0.05, -y * (1.0 + y * (0.5 + y * (1.0 / 6.0 + y * (1.0 / 24.0)))), 1.0 - jnp.exp(y))
    mm = jnp.sqrt(one_minus_a2)
    return xc, xb, r, ig, sp, a, mm


def lru_fwd(aux, cw, cb, wa, ba, wx, bx, lam, *, name):
    T = aux.shape[0]
    S, C = SEQ, LRU_HALF_W

    def body(x_ref, g_ref, cw_ref, cb_ref, wa_ref, ba_ref, wx_ref, bx_ref, lam_ref, o_ref, h_ref, a_s, u_s):
        xc, _, r, ig, sp, a, mm = _lru_gates(x_ref[...], cw_ref[...], cb_ref[...], wa_ref[...], ba_ref[...],
                                             wx_ref[...], bx_ref[...], lam_ref[...])
        a_s[...] = a
        u_s[...] = mm * (ig * xc)

        def group(i, h):
            base = pl.multiple_of(i * 8, 8)
            a8 = a_s[pl.ds(base, 8), :]
            u8 = u_s[pl.ds(base, 8), :]
            for rr in range(8):
                h = a8[rr:rr + 1] * h + u8[rr:rr + 1]
                h_ref[pl.ds(base + rr, 1), :] = h
            return h

        lax.fori_loop(0, S // 8, group, jnp.zeros((1, C), F32))
        gate = g_ref[...]
        o_ref[...] = (h_ref[...] * (gate * _gelu_cdf(gate))).astype(BF16)

    blk = lambda col: pl.BlockSpec((S, C), lambda c, b: (b, col + c))
    par = lambda rows: pl.BlockSpec((rows, C), lambda c, b: (0, c))
    sq = pl.BlockSpec((None, C, C), lambda c, b: (c, 0, 0))
    sd = jax.ShapeDtypeStruct
    W = LRU_HALVES * C
    return pl.pallas_call(
        body, name=name, grid=(LRU_HALVES, T // S),
        in_specs=[blk(0), blk(LRU_HALVES), par(4), par(1), sq, par(1), sq, par(1), par(1)],
        out_specs=[blk(0), blk(0)], out_shape=[sd((T, W), BF16), sd((T, W), F32)],
        scratch_shapes=[pltpu.VMEM((S, C), F32), pltpu.VMEM((S, C), F32)],
        compiler_params=_params("parallel", "parallel"),
    )(aux, aux, cw, cb, wa, ba, wx, bx, lam)


def lru_bwd(aux, h, dmixed, cw, cb, wa, ba, wx, bx, lam, *, name):
    T = aux.shape[0]
    S, C = SEQ, LRU_HALF_W

    def body(x_ref, g_ref, h_ref, do_ref, cw_ref, cb_ref, wa_ref, ba_ref, wx_ref, bx_ref, lam_ref,
             dx_ref, dgate_ref, dcw_ref, dcb_ref, dwa_ref, dba_ref, dwx_ref, dbx_ref, dlam_ref, a_s, d_s):
        x = x_ref[...]
        cw = cw_ref[...]
        lam = lam_ref[...]
        xc, xb, r, ig, sp, a, mm = _lru_gates(x, cw, cb_ref[...], wa_ref[...], ba_ref[...], wx_ref[...], bx_ref[...], lam)
        gate = g_ref[...]
        gl, dgl = _gelu_and_grad(gate)
        dout = do_ref[...]
        hv = h_ref[...]
        dgate_ref[...] = dout * hv * dgl
        a_s[...] = a
        d_s[...] = dout * gl

        def group(i, c):
            base = pl.multiple_of((S // 8 - 1 - i) * 8, 8)
            a8 = a_s[pl.ds(base, 8), :]
            d8 = d_s[pl.ds(base, 8), :]
            for rr in range(7, -1, -1):
                d = d8[rr:rr + 1] + c
                d_s[pl.ds(base + rr, 1), :] = d
                c = a8[rr:rr + 1] * d
            return c

        lax.fori_loop(0, S // 8, group, jnp.zeros((1, C), F32))
        row = lax.broadcasted_iota(jnp.int32, (S, 1), 0)
        dht = d_s[...]
        h_prev = jnp.where(row >= 1, pltpu.roll(hv, 1, 0), 0.0)
        da = dht * h_prev
        gx = ig * xc
        dmm = dht * gx
        dig = dht * mm * xc
        dxc = dht * mm * ig
        dla = da * a - dmm * (a * a) / mm
        dr = dla * (-LRU_C * sp)
        dsp = jnp.sum(dla * (-LRU_C * r), axis=0, keepdims=True)
        dlam = dsp * (-jax.nn.sigmoid(-lam))
        dpa = dr * r * (1.0 - r)
        dpx = dig * ig * (1.0 - ig)
        dpa_b, dpx_b = dpa.astype(BF16), dpx.astype(BF16)
        nt = (((1,), (1,)), ((), ()))
        tn = (((0,), (0,)), ((), ()))
        dxc = dxc + lax.dot_general(dpa_b, wa_ref[...], nt, preferred_element_type=F32) \
                  + lax.dot_general(dpx_b, wx_ref[...], nt, preferred_element_type=F32)
        dwa = lax.dot_general(xb, dpa_b, tn, preferred_element_type=F32)
        dwx = lax.dot_general(xb, dpx_b, tn, preferred_element_type=F32)

        def fwd(v, s):
            return jnp.where(row < S - s, pltpu.roll(v, S - s, 0), 0.0)

        def back(v, s):
            return jnp.where(row >= s, pltpu.roll(v, s, 0), 0.0)

        dx_ref[...] = cw[3:4] * dxc + cw[2:3] * fwd(dxc, 1) + cw[1:2] * fwd(dxc, 2) + cw[0:1] * fwd(dxc, 3)
        s0 = lambda v: jnp.sum(v, axis=0, keepdims=True)
        dcw = jnp.concatenate([s0(dxc * back(x, 3)), s0(dxc * back(x, 2)), s0(dxc * back(x, 1)), s0(dxc * x)], axis=0)
        parts = ((dcw_ref, dcw), (dcb_ref, s0(dxc)), (dwa_ref, dwa), (dba_ref, s0(dpa)), (dwx_ref, dwx),
                 (dbx_ref, s0(dpx)), (dlam_ref, dlam))

        @pl.when(pl.program_id(1) == 0)
        def _():
            for ref, val in parts:
                ref[...] = val

        @pl.when(pl.program_id(1) > 0)
        def _():
            for ref, val in parts:
                ref[...] += val

    blk = lambda col: pl.BlockSpec((S, C), lambda c, b: (b, col + c))
    par = lambda rows: pl.BlockSpec((rows, C), lambda c, b: (0, c))
    sq = pl.BlockSpec((None, C, C), lambda c, b: (c, 0, 0))
    sd = jax.ShapeDtypeStruct
    W = LRU_HALVES * C
    vec = sd((1, W), F32)
    return pl.pallas_call(
        body, name=name, grid=(LRU_HALVES, T // S),
        in_specs=[blk(0), blk(LRU_HALVES), blk(0), blk(3 * LRU_HALVES), par(4), par(1), sq, par(1), sq, par(1), par(1)],
        out_specs=[blk(0), blk(0), par(4), par(1), sq, par(1), sq, par(1), par(1)],
        out_shape=[sd((T, W), F32), sd((T, W), F32), sd((4, W), F32), vec, sd((LRU_HALVES, C, C), F32), vec,
                   sd((LRU_HALVES, C, C), F32), vec, vec],
        scratch_shapes=[pltpu.VMEM((S, C), F32), pltpu.VMEM((S, C), F32)],
        compiler_params=_params("parallel", "arbitrary"),
    )(aux, aux, h, dmixed, cw, cb, wa, ba, wx, bx, lam)


_NT = (((1,), (1,)), ((), ()))
_TN = (((0,), (0,)), ((), ()))


def _dot(a, b, dims=None):
    if dims is None:
        return jnp.dot(a, b, preferred_element_type=F32)
    return lax.dot_general(a, b, dims, preferred_element_type=F32)


def _hs(h):
    return slice(h * HEAD_DIM, (h + 1) * HEAD_DIM)


def cross_fwd(q, kv, *, name):
    T = q.shape[0]
    tq = 512

    def body(q_ref, kv_ref, o_ref):
        for h in range(N_HEADS):
            qh = q_ref[:, _hs(h)] * Q_SCALE
            k = kv_ref[:, _hs(h)]
            v = kv_ref[:, GROUP_W + h * HEAD_DIM:GROUP_W + (h + 1) * HEAD_DIM]
            s = _dot(qh, k, _NT)
            p = jnp.exp(s - jnp.max(s, axis=-1, keepdims=True))
            p = p / jnp.sum(p, axis=-1, keepdims=True)
            o_ref[:, _hs(h)] = _dot(p.astype(BF16), v).astype(BF16)

    per = SEQ // tq
    return pl.pallas_call(
        body, name=name, grid=(T // tq,),
        in_specs=[pl.BlockSpec((tq, GROUP_W), lambda i: (i, 0)), pl.BlockSpec((N_MEM, 2 * GROUP_W), lambda i: (i // per, 0))],
        out_specs=pl.BlockSpec((tq, GROUP_W), lambda i: (i, 0)), out_shape=jax.ShapeDtypeStruct((T, GROUP_W), BF16),
        compiler_params=_params("parallel"),
    )(q, kv)


def cross_bwd(q, kv, do, *, name):
    T = q.shape[0]
    tq = 512
    per = SEQ // tq

    def body(q_ref, kv_ref, do_ref, dq_ref, dkv_ref):
        first = pl.program_id(0) % per == 0
        for h in range(N_HEADS):
            vs = slice(GROUP_W + h * HEAD_DIM, GROUP_W + (h + 1) * HEAD_DIM)
            qh = q_ref[:, _hs(h)] * Q_SCALE
            k = kv_ref[:, _hs(h)]
            v = kv_ref[:, vs]
            doh = do_ref[:, _hs(h)].astype(BF16)
            s = _dot(qh, k, _NT)
            p = jnp.exp(s - jnp.max(s, axis=-1, keepdims=True))
            p = p / jnp.sum(p, axis=-1, keepdims=True)
            dp = _dot(doh, v, _NT)
            ds = (p * (dp - jnp.sum(p * dp, axis=-1, keepdims=True))).astype(BF16)
            dq_ref[:, _hs(h)] = (_dot(ds, k) * Q_SCALE).astype(BF16)
            dk = _dot(ds, qh, _TN)
            dv = _dot(p.astype(BF16), doh, _TN)

            @pl.when(first)
            def _():
                dkv_ref[:, _hs(h)] = dk
                dkv_ref[:, vs] = dv

            @pl.when(jnp.logical_not(first))
            def _():
                dkv_ref[:, _hs(h)] += dk
                dkv_ref[:, vs] += dv

    qb = pl.BlockSpec((tq, GROUP_W), lambda i: (i, 0))
    kvb = pl.BlockSpec((N_MEM, 2 * GROUP_W), lambda i: (i // per, 0))
    sd = jax.ShapeDtypeStruct
    return pl.pallas_call(
        body, name=name, grid=(T // tq,),
        in_specs=[qb, kvb, qb], out_specs=[qb, kvb],
        out_shape=[sd((T, GROUP_W), BF16), sd(kv.shape, F32)],
        compiler_params=_params("arbitrary"),
    )(q, kv, do)


NB = SEQ // BLOCK
NEG = -1e30
HEADS = tuple(range(N_HEADS))


def _blk(i):
    return pl.ds(pl.multiple_of(i * BLOCK, BLOCK), BLOCK)


def _qkv_specs(first_col):
    return [pl.BlockSpec((SEQ, GROUP_W), lambda b, c=first_col + j: (b, c)) for j in range(3)]


LANES = 128
CUM_BLK = 256


def col_to_row(c):
    b = c.shape[0] // SEQ
    return c.reshape(b, SEQ, LANES)[:, :, :8].transpose(0, 2, 1).reshape(b * 8, SEQ)


def row_to_col(r):
    b = r.shape[0] // 8
    c = r.reshape(b, 8, SEQ).transpose(0, 2, 1)
    return jnp.pad(c, ((0, 0), (0, 0), (0, LANES - 8))).reshape(b * SEQ, LANES)


def fox_prep(aux, bf, *, name):
    T = aux.shape[0]

    def body(f_ref, b_ref, o_ref):
        row = lax.broadcasted_iota(jnp.int32, (CUM_BLK, CUM_BLK), 0)
        col = lax.broadcasted_iota(jnp.int32, (CUM_BLK, CUM_BLK), 1)
        upto = (col <= row).astype(BF16)
        carry = jnp.zeros((1, LANES), F32)
        for n in range(SEQ // CUM_BLK):
            rows = slice(n * CUM_BLK, (n + 1) * CUM_BLK)
            logf = -_softplus(-(f_ref[rows, :] + b_ref[...]))
            hi = logf.astype(BF16)
            lo = (logf - hi.astype(F32)).astype(BF16)
            cum = _dot(upto, hi) + _dot(upto, lo) + carry
            o_ref[rows, :] = cum
            carry = cum[CUM_BLK - 1:CUM_BLK]

    return pl.pallas_call(
        body, name=name, grid=(T // SEQ,),
        in_specs=[pl.BlockSpec((SEQ, LANES), lambda b: (b, 4)), pl.BlockSpec((1, LANES), lambda b: (0, 0))],
        out_specs=pl.BlockSpec((SEQ, LANES), lambda b: (b, 0)), out_shape=jax.ShapeDtypeStruct((T, LANES), F32),
        compiler_params=_params("parallel"),
    )(aux, bf)


def fox_prep_bwd(aux, bf, dcum, *, name):
    T = aux.shape[0]

    def body(f_ref, b_ref, d_ref, df_ref, db_ref):
        row = lax.broadcasted_iota(jnp.int32, (CUM_BLK, CUM_BLK), 0)
        col = lax.broadcasted_iota(jnp.int32, (CUM_BLK, CUM_BLK), 1)
        onward = (col >= row).astype(BF16)
        carry = jnp.zeros((1, LANES), F32)
        tot = jnp.zeros((1, LANES), F32)
        for n in range(SEQ // CUM_BLK - 1, -1, -1):
            rows = slice(n * CUM_BLK, (n + 1) * CUM_BLK)
            d = d_ref[rows, :]
            hi = d.astype(BF16)
            lo = (d - hi.astype(F32)).astype(BF16)
            dlogf = _dot(onward, hi) + _dot(onward, lo) + carry
            carry = dlogf[0:1]
            df = dlogf * jax.nn.sigmoid(-(f_ref[rows, :] + b_ref[...]))
            df_ref[rows, :] = df
            tot = tot + jnp.sum(df, axis=0, keepdims=True)

        @pl.when(pl.program_id(0) == 0)
        def _():
            db_ref[...] = tot

        @pl.when(pl.program_id(0) > 0)
        def _():
            db_ref[...] += tot

    blk = pl.BlockSpec((SEQ, LANES), lambda b: (b, 0))
    vec = pl.BlockSpec((1, LANES), lambda b: (0, 0))
    sd = jax.ShapeDtypeStruct
    return pl.pallas_call(
        body, name=name, grid=(T // SEQ,),
        in_specs=[pl.BlockSpec((SEQ, LANES), lambda b: (b, 4)), vec, blk],
        out_specs=[blk, vec], out_shape=[sd((T, LANES), F32), sd((1, LANES), F32)],
        compiler_params=_params("arbitrary"),
    )(aux, bf, dcum)


CHUNK = 256
WIDE = N_HEADS * CHUNK
NCH = SEQ // CHUNK


def _seg(h):
    return slice(h * CHUNK, (h + 1) * CHUNK)


def _chunk_rows(c):
    return pl.ds(pl.multiple_of(c * CHUNK, CHUNK), CHUNK)


def _wide_consts():
    r = lax.broadcasted_iota(jnp.int32, (WIDE, GROUP_W), 0)
    f = lax.broadcasted_iota(jnp.int32, (WIDE, GROUP_W), 1)
    bd = (r // CHUNK) == (f // HEAD_DIM)
    row = lax.broadcasted_iota(jnp.int32, (BLOCK, WIDE), 0)
    key = lax.broadcasted_iota(jnp.int32, (BLOCK, WIDE), 1) % CHUNK
    return bd, row, key


def _block_diag(x, bd):
    return jnp.where(bd, jnp.concatenate([x] * N_HEADS, axis=0), jnp.zeros((), x.dtype))


def _fold_heads(w, bd):
    w = jnp.where(bd, w, 0.0)
    return (w[0:CHUNK] + w[CHUNK:2 * CHUNK]) + (w[2 * CHUNK:3 * CHUNK] + w[3 * CHUNK:])


def _widen(cols):
    return jnp.concatenate([jnp.broadcast_to(c, (BLOCK, CHUNK)) for c in cols], axis=1)


def _head_rowsums(w):
    return [jnp.sum(w[:, _seg(h)], axis=1, keepdims=True) for h in HEADS]


def _tri_wide(x, tri):
    hi = x.astype(BF16)
    lo = (x - hi.astype(F32)).astype(BF16)
    y = _dot(jnp.concatenate([hi[:, _seg(h)] for h in HEADS] + [lo[:, _seg(h)] for h in HEADS], axis=0), tri)
    return jnp.concatenate([y[h * BLOCK:(h + 1) * BLOCK] + y[(N_HEADS + h) * BLOCK:(N_HEADS + h + 1) * BLOCK] for h in HEADS], axis=1)


def _feature_widen(cols):
    return jnp.concatenate([jnp.broadcast_to(c, (BLOCK, HEAD_DIM)) for c in cols], axis=1)


def _loop_by_two(n, index, body, carry):
    odd = n % 2
    carry = lax.fori_loop(0, odd, lambda _, cr: body(index(0), cr), carry)
    return lax.fori_loop(0, n // 2, lambda t, cr: body(index(odd + 2 * t + 1), body(index(odd + 2 * t), cr)), carry)


def _sbw_scores(q, kbd, later):
    z = _dot(q, kbd, _NT)
    lk = -_softplus(z)
    return z + lk, lk, _tri_wide(lk, later)


def _sbw_tile(q, kbd, mask, later, csum):
    z = _dot(q, kbd, _NT)
    lk = -_softplus(z)
    if mask is not None:
        lk = jnp.where(mask, lk, 0.0)
    e = z + lk
    att = jnp.exp(e + _tri_wide(lk, later) + csum)
    if mask is not None:
        att = jnp.where(mask, att, 0.0)
    return att, e, lk


def sbw_fwd(qkv, *, name):
    T = qkv.shape[0]

    def body(q_ref, k_ref, v_ref, o_ref):
        bd, row, key = _wide_consts()
        r2 = lax.broadcasted_iota(jnp.int32, (CHUNK, CHUNK), 0)
        c2 = lax.broadcasted_iota(jnp.int32, (CHUNK, CHUNK), 1)
        later = (r2 > c2).astype(BF16)

        def qblock(i, _):
            q = q_ref[_blk(i), :] * Q_SCALE
            cd = i // 2
            strict = key < row + BLOCK * (i % 2)

            def tile(c, mask, carry):
                acc, csum = carry
                att, _, lk = _sbw_tile(q, _block_diag(k_ref[_chunk_rows(c), :], bd), mask, later, csum)
                acc = acc + _dot(att.astype(BF16), _block_diag(v_ref[_chunk_rows(c), :], bd))
                return acc, csum + _widen(_head_rowsums(lk))

            def two_tiles(c1, carry):
                acc, csum = carry
                e1, lk1, t1 = _sbw_scores(q, _block_diag(k_ref[_chunk_rows(c1), :], bd), later)
                e2, lk2, t2 = _sbw_scores(q, _block_diag(k_ref[_chunk_rows(c1 - 1), :], bd), later)
                att1 = jnp.exp(e1 + t1 + csum)
                csum = csum + _widen(_head_rowsums(lk1))
                att2 = jnp.exp(e2 + t2 + csum)
                csum = csum + _widen(_head_rowsums(lk2))
                acc = acc + _dot(att1.astype(BF16), _block_diag(v_ref[_chunk_rows(c1), :], bd))
                acc = acc + _dot(att2.astype(BF16), _block_diag(v_ref[_chunk_rows(c1 - 1), :], bd))
                return acc, csum

            carry = tile(cd, strict, (jnp.zeros((BLOCK, GROUP_W), F32), jnp.zeros((BLOCK, WIDE), F32)))
            odd = cd % 2
            carry = lax.fori_loop(0, odd, lambda n, cr: tile(cd - 1, None, cr), carry)
            acc, _ = lax.fori_loop(0, cd // 2, lambda n, cr: two_tiles(cd - 1 - odd - 2 * n, cr), carry)
            o_ref[_blk(i), :] = acc.astype(BF16)
            return 0

        lax.fori_loop(0, NB, qblock, 0)

    return pl.pallas_call(
        body, name=name, grid=(T // SEQ,), in_specs=_qkv_specs(0),
        out_specs=pl.BlockSpec((SEQ, GROUP_W), lambda b: (b, 0)), out_shape=jax.ShapeDtypeStruct((T, GROUP_W), BF16),
        compiler_params=_params("parallel"),
    )(qkv, qkv, qkv)


def sbw_bwd(qkv, dmixed, *, name):
    T = qkv.shape[0]

    def body(q_ref, k_ref, v_ref, do_ref, dq_ref, dk_ref, dv_ref, att_s, sg_s):
        bd, row, key = _wide_consts()
        r2 = lax.broadcasted_iota(jnp.int32, (CHUNK, CHUNK), 0)
        c2 = lax.broadcasted_iota(jnp.int32, (CHUNK, CHUNK), 1)
        later = (r2 > c2).astype(BF16)
        earlier = (r2 < c2).astype(BF16)
        dk_ref[...] = jnp.zeros_like(dk_ref)
        dv_ref[...] = jnp.zeros_like(dv_ref)

        def qblock(i, _):
            q = q_ref[_blk(i), :] * Q_SCALE
            do = do_ref[_blk(i), :].astype(BF16)
            cd = i // 2
            strict = key < row + BLOCK * (i % 2)

            def recompute(c, mask, csum):
                att, e, lk = _sbw_tile(q, _block_diag(k_ref[_chunk_rows(c), :], bd), mask, later, csum)
                sg = jnp.exp(e)
                att_s[c] = att
                sg_s[c] = sg if mask is None else jnp.where(mask, sg, 0.0)
                return csum + _widen(_head_rowsums(lk))

            def recompute_two(c1, csum):
                e1, lk1, t1 = _sbw_scores(q, _block_diag(k_ref[_chunk_rows(c1), :], bd), later)
                e2, lk2, t2 = _sbw_scores(q, _block_diag(k_ref[_chunk_rows(c1 - 1), :], bd), later)
                sg_s[c1] = jnp.exp(e1)
                sg_s[c1 - 1] = jnp.exp(e2)
                att_s[c1] = jnp.exp(e1 + t1 + csum)
                csum = csum + _widen(_head_rowsums(lk1))
                att_s[c1 - 1] = jnp.exp(e2 + t2 + csum)
                return csum + _widen(_head_rowsums(lk2))

            csum = recompute(cd, strict, jnp.zeros((BLOCK, WIDE), F32))
            odd = cd % 2
            csum = lax.fori_loop(0, odd, lambda n, cs: recompute(cd - 1, None, cs), csum)
            lax.fori_loop(0, cd // 2, lambda n, cs: recompute_two(cd - 1 - odd - 2 * n, cs), csum)

            def tile(c, carry):
                dq, pre = carry
                kbd = _block_diag(k_ref[_chunk_rows(c), :], bd)
                vbd = _block_diag(v_ref[_chunk_rows(c), :], bd)
                att = att_s[c]
                ds = _dot(do, vbd, _NT) * att
                dlk = ds + _tri_wide(ds, earlier) + pre
                dz = (ds - dlk * sg_s[c]).astype(BF16)
                dk_ref[_chunk_rows(c), :] += _fold_heads(_dot(dz, q, _TN), bd)
                dv_ref[_chunk_rows(c), :] += _fold_heads(_dot(att.astype(BF16), do, _TN), bd)
                return dq + _dot(dz, kbd), pre + _widen(_head_rowsums(ds))

            def two_tiles(c1, carry):
                dq, pre = carry
                c2 = c1 + 1
                kbd1, kbd2 = _block_diag(k_ref[_chunk_rows(c1), :], bd), _block_diag(k_ref[_chunk_rows(c2), :], bd)
                att1, att2 = att_s[c1], att_s[c2]
                ds1 = _dot(do, _block_diag(v_ref[_chunk_rows(c1), :], bd), _NT) * att1
                ds2 = _dot(do, _block_diag(v_ref[_chunk_rows(c2), :], bd), _NT) * att2
                tri1, tri2 = _tri_wide(ds1, earlier), _tri_wide(ds2, earlier)
                dv_ref[_chunk_rows(c1), :] += _fold_heads(_dot(att1.astype(BF16), do, _TN), bd)
                dv_ref[_chunk_rows(c2), :] += _fold_heads(_dot(att2.astype(BF16), do, _TN), bd)
                dz1 = (ds1 - (ds1 + tri1 + pre) * sg_s[c1]).astype(BF16)
                pre = pre + _widen(_head_rowsums(ds1))
                dz2 = (ds2 - (ds2 + tri2 + pre) * sg_s[c2]).astype(BF16)
                pre = pre + _widen(_head_rowsums(ds2))
                dk_ref[_chunk_rows(c1), :] += _fold_heads(_dot(dz1, q, _TN), bd)
                dk_ref[_chunk_rows(c2), :] += _fold_heads(_dot(dz2, q, _TN), bd)
                return dq + _dot(dz1, kbd1) + _dot(dz2, kbd2), pre

            n_tiles = cd + 1
            odd = n_tiles % 2
            carry = (jnp.zeros((BLOCK, GROUP_W), F32), jnp.zeros((BLOCK, WIDE), F32))
            carry = lax.fori_loop(0, odd, lambda n, cr: tile(0, cr), carry)
            dq, _ = lax.fori_loop(0, n_tiles // 2, lambda n, cr: two_tiles(odd + 2 * n, cr), carry)
            dq_ref[_blk(i), :] = dq * Q_SCALE
            return 0

        lax.fori_loop(0, NB, qblock, 0)

    out = pl.BlockSpec((SEQ, GROUP_W), lambda b: (b, 0))
    sd = jax.ShapeDtypeStruct((T, GROUP_W), F32)
    return pl.pallas_call(
        body, name=name, grid=(T // SEQ,), in_specs=_qkv_specs(0) + [out],
        out_specs=[out] * 3, out_shape=[sd] * 3,
        scratch_shapes=[pltpu.VMEM((NCH, BLOCK, WIDE), F32), pltpu.VMEM((NCH, BLOCK, WIDE), F32)],
        compiler_params=_params("parallel"),
    )(qkv, qkv, qkv, dmixed)


def _foxw_logits(q, kbd, cq, cr_ref, c, mask):
    ck = jnp.concatenate([cr_ref[h:h + 1, _chunk_rows(c)] for h in HEADS], axis=1)
    z = _dot(q, kbd, _NT) + cq - ck
    return z if mask is None else jnp.where(mask, z, NEG)


def foxw_fwd(qkv, cumc, cumr, *, name):
    T = qkv.shape[0]

    def body(q_ref, k_ref, v_ref, cc_ref, cr_ref, o_ref, o32_ref, lse_ref, z_s):
        bd, row, key = _wide_consts()
        lse_ref[...] = jnp.zeros_like(lse_ref)

        def qblock(i, _):
            q = q_ref[_blk(i), :] * Q_SCALE
            cq = _widen([cc_ref[_blk(i), h:h + 1] for h in HEADS])
            cd = i // 2
            causal = key <= row + BLOCK * (i % 2)

            def logits(c, mask, ms):
                z = _foxw_logits(q, _block_diag(k_ref[_chunk_rows(c), :], bd), cq, cr_ref, c, mask)
                z_s[c] = z
                return tuple(jnp.maximum(ms[h], jnp.max(z[:, _seg(h)], axis=1, keepdims=True)) for h in HEADS)

            ms = logits(cd, causal, (jnp.full((BLOCK, 1), NEG, F32),) * N_HEADS)
            ms = _loop_by_two(cd, lambda n: n, lambda c, m: logits(c, None, m), ms)
            m_wide = _widen(ms)

            def values(c, carry):
                acc, l = carry
                p = jnp.exp(z_s[c] - m_wide)
                return acc + _dot(p.astype(BF16), _block_diag(v_ref[_chunk_rows(c), :], bd)), l + _widen(_head_rowsums(p))

            acc, l = _loop_by_two(cd + 1, lambda n: n, values, (jnp.zeros((BLOCK, GROUP_W), F32), jnp.zeros((BLOCK, WIDE), F32)))
            ls = [l[:, h * CHUNK:h * CHUNK + 1] for h in HEADS]
            o = acc / _feature_widen(ls)
            o_ref[_blk(i), :] = o.astype(BF16)
            o32_ref[_blk(i), :] = o
            for h in HEADS:
                lse_ref[_blk(i), h:h + 1] = ms[h] + jnp.log(ls[h])
            return 0

        lax.fori_loop(0, NB, qblock, 0)

    out = pl.BlockSpec((SEQ, GROUP_W), lambda b: (b, 0))
    colb = pl.BlockSpec((SEQ, LANES), lambda b: (b, 0))
    sd = jax.ShapeDtypeStruct
    return pl.pallas_call(
        body, name=name, grid=(T // SEQ,),
        in_specs=_qkv_specs(3) + [colb, pl.BlockSpec((8, SEQ), lambda b: (b, 0))],
        out_specs=[out, out, colb], out_shape=[sd((T, GROUP_W), BF16), sd((T, GROUP_W), F32), sd((T, LANES), F32)],
        scratch_shapes=[pltpu.VMEM((NCH, BLOCK, WIDE), F32)],
        compiler_params=_params("parallel"),
    )(qkv, qkv, qkv, cumc, cumr)


def foxw_bwd(qkv, cumc, cumr, lse, o32, dmixed, *, name):
    T = qkv.shape[0]

    def body(q_ref, k_ref, v_ref, cc_ref, cr_ref, lse_ref, o_ref, do_ref, dq_ref, dk_ref, dv_ref, dcc_ref, dcr_ref):
        bd, row, key = _wide_consts()
        dk_ref[...] = jnp.zeros_like(dk_ref)
        dv_ref[...] = jnp.zeros_like(dv_ref)
        dcc_ref[...] = jnp.zeros_like(dcc_ref)
        dcr_ref[...] = jnp.zeros_like(dcr_ref)

        def qblock(i, _):
            q = q_ref[_blk(i), :] * Q_SCALE
            do = do_ref[_blk(i), :].astype(BF16)
            prod = do.astype(F32) * o_ref[_blk(i), :]
            delta = _widen([jnp.sum(prod[:, _hs(h)], axis=1, keepdims=True) for h in HEADS])
            cq = _widen([cc_ref[_blk(i), h:h + 1] for h in HEADS])
            lse_w = _widen([lse_ref[_blk(i), h:h + 1] for h in HEADS])
            cd = i // 2
            causal = key <= row + BLOCK * (i % 2)

            def tile(c, mask, carry):
                dq, dcq = carry
                kbd = _block_diag(k_ref[_chunk_rows(c), :], bd)
                vbd = _block_diag(v_ref[_chunk_rows(c), :], bd)
                p = jnp.exp(_foxw_logits(q, kbd, cq, cr_ref, c, mask) - lse_w)
                ds = p * (_dot(do, vbd, _NT) - delta)
                dsb = ds.astype(BF16)
                dk_ref[_chunk_rows(c), :] += _fold_heads(_dot(dsb, q, _TN), bd)
                dv_ref[_chunk_rows(c), :] += _fold_heads(_dot(p.astype(BF16), do, _TN), bd)
                for h in HEADS:
                    dcr_ref[h:h + 1, _chunk_rows(c)] -= jnp.sum(ds[:, _seg(h)], axis=0, keepdims=True)
                return dq + _dot(dsb, kbd), dcq + _widen(_head_rowsums(ds))

            def two_tiles(c1, carry):
                dq, dcq = carry
                cs = (c1, c1 + 1)
                kbds = [_block_diag(k_ref[_chunk_rows(c), :], bd) for c in cs]
                vbds = [_block_diag(v_ref[_chunk_rows(c), :], bd) for c in cs]
                ps = [jnp.exp(_foxw_logits(q, kbds[j], cq, cr_ref, cs[j], None) - lse_w) for j in range(2)]
                dss = [ps[j] * (_dot(do, vbds[j], _NT) - delta) for j in range(2)]
                dsbs = [d.astype(BF16) for d in dss]
                for j, c in enumerate(cs):
                    dk_ref[_chunk_rows(c), :] += _fold_heads(_dot(dsbs[j], q, _TN), bd)
                    dv_ref[_chunk_rows(c), :] += _fold_heads(_dot(ps[j].astype(BF16), do, _TN), bd)
                    for h in HEADS:
                        dcr_ref[h:h + 1, _chunk_rows(c)] -= jnp.sum(dss[j][:, _seg(h)], axis=0, keepdims=True)
                dq = dq + _dot(dsbs[0], kbds[0]) + _dot(dsbs[1], kbds[1])
                return dq, dcq + _widen(_head_rowsums(dss[0])) + _widen(_head_rowsums(dss[1]))

            carry = tile(cd, causal, (jnp.zeros((BLOCK, GROUP_W), F32), jnp.zeros((BLOCK, WIDE), F32)))
            odd = cd % 2
            carry = lax.fori_loop(0, odd, lambda n, cr: tile(0, None, cr), carry)
            dq, dcq = lax.fori_loop(0, cd // 2, lambda n, cr: two_tiles(odd + 2 * n, cr), carry)
            dq_ref[_blk(i), :] = dq * Q_SCALE
            for h in HEADS:
                dcc_ref[_blk(i), h:h + 1] = dcq[:, h * CHUNK:h * CHUNK + 1]
            return 0

        lax.fori_loop(0, NB, qblock, 0)

    out = pl.BlockSpec((SEQ, GROUP_W), lambda b: (b, 0))
    colb = pl.BlockSpec((SEQ, LANES), lambda b: (b, 0))
    rowb = pl.BlockSpec((8, SEQ), lambda b: (b, 0))
    sd = jax.ShapeDtypeStruct
    big = sd((T, GROUP_W), F32)
    return pl.pallas_call(
        body, name=name, grid=(T // SEQ,),
        in_specs=_qkv_specs(3) + [colb, rowb, colb, out, pl.BlockSpec((SEQ, GROUP_W), lambda b: (b, 1))],
        out_specs=[out, out, out, colb, rowb],
        out_shape=[big, big, big, sd((T, LANES), F32), sd((T // SEQ * 8, SEQ), F32)],
        compiler_params=_params("parallel"),
    )(qkv, qkv, qkv, cumc, cumr, lse, o32, dmixed)


BAND = 2 * BLOCK


def _t5_bucket_np(dist):
    n = np.maximum(dist, 0)
    max_exact = NUM_BUCKETS // 2
    nf = np.maximum(n, 1).astype(np.float32)
    large = max_exact + (np.log(nf / np.float32(max_exact)) / np.float32(math.log(MAX_DISTANCE / max_exact))
                         * np.float32(NUM_BUCKETS - max_exact)).astype(np.int32)
    large = np.minimum(large, NUM_BUCKETS - 1)
    return np.where(n < max_exact, n, large).astype(np.int32)


def _band_buckets():
    qi = np.arange(BLOCK)[:, None]
    ki = np.arange(BAND)[None, :]
    delta = np.clip(qi - ki + BLOCK, 0, BLOCK)
    return np.stack([_t5_bucket_np(delta * d) for d in DILATIONS])


def relbias_expand(rel, *, name):
    buckets = jnp.asarray(_band_buckets())
    n_pat = len(DILATIONS)

    def body(rel_ref, bk_ref, o_ref):
        for p in range(n_pat):
            bk = bk_ref[p]
            for h in range(N_HEADS):
                acc = jnp.zeros((BLOCK, BAND), F32)
                for b in range(NUM_BUCKETS):
                    acc = jnp.where(bk == b, rel_ref[b, h], acc)
                o_ref[p * N_HEADS + h] = acc

    return pl.pallas_call(
        body, name=name,
        in_specs=[pl.BlockSpec(memory_space=pltpu.SMEM), pl.BlockSpec(memory_space=pltpu.VMEM)],
        out_specs=pl.BlockSpec(memory_space=pltpu.VMEM),
        out_shape=jax.ShapeDtypeStruct((n_pat * N_HEADS, BLOCK, BAND), F32),
        compiler_params=_params(),
    )(rel, buckets)


def relbias_reduce(ds_all, *, name):
    buckets = jnp.asarray(_band_buckets())
    n_pat = len(DILATIONS)

    def body(ds_ref, bk_ref, o_ref):
        for b in range(NUM_BUCKETS):
            for h in range(N_HEADS):
                tot = jnp.float32(0.0)
                for p in range(n_pat):
                    tot = tot + jnp.sum(jnp.where(bk_ref[p] == b, ds_ref[p * N_HEADS + h], 0.0))
                o_ref[b, h] = tot

    return pl.pallas_call(
        body, name=name,
        in_specs=[pl.BlockSpec(memory_space=pltpu.VMEM), pl.BlockSpec(memory_space=pltpu.VMEM)],
        out_specs=pl.BlockSpec(memory_space=pltpu.SMEM),
        out_shape=jax.ShapeDtypeStruct((NUM_BUCKETS, N_HEADS), F32),
        compiler_params=_params(),
    )(ds_all, buckets)


def _band_valid_wide(first, row, key):
    inside = jnp.logical_and(key >= row, key <= row + BLOCK)
    return jnp.logical_and(inside, jnp.logical_or(jnp.logical_not(first), key >= BLOCK))


QKV_BLOCKS = 9


def _band_in_specs(d, pattern, has_prev):
    rows = BLOCK * d
    cur = lambda c: pl.BlockSpec((rows, GROUP_W), lambda tb, r: (tb, c))
    prev = lambda c: pl.BlockSpec((rows, GROUP_W), lambda tb, r: (jnp.maximum(tb - 1, 0), c))
    bias = pl.BlockSpec((N_HEADS, BLOCK, BAND), lambda tb, r: (pattern, 0, 0))
    return [cur(6), cur(7), cur(8)] + ([prev(7), prev(8)] if has_prev else []) + [bias]


def _classes_per_step(d):
    return 2 if d > 1 else 1


def _step_classes(d):
    n = _classes_per_step(d)
    return [pl.program_id(1) * n + j for j in range(n)]


def _class_rows(d, cls):
    return pl.ds(cls, BLOCK, stride=d) if d > 1 else pl.ds(0, BLOCK)


def _halves_scratch(rows, n):
    return [pltpu.VMEM((2, rows, LANES), F32)] * n


def _stage(refs, scratch):
    @pl.when(pl.program_id(1) == 0)
    def _():
        for src, dst in zip(refs, scratch):
            dst[0] = src[:, :LANES].astype(F32)
            dst[1] = src[:, LANES:].astype(F32)


def _take_class(s, d, cls):
    rows = _class_rows(d, cls)
    return jnp.concatenate([s.at[0][rows, :], s.at[1][rows, :]], axis=1)


def _put_class(s, d, cls, x):
    rows = _class_rows(d, cls)
    s.at[0][rows, :] = x[:, :LANES]
    s.at[1][rows, :] = x[:, LANES:]


def _flush(scratch, refs, d):
    @pl.when(pl.program_id(1) == d // _classes_per_step(d) - 1)
    def _():
        for s, o in zip(scratch, refs):
            o[...] = jnp.concatenate([s[0], s[1]], axis=1)


def _band_operands(scratch, d, cls, has_prev):
    take = lambda s: _take_class(s, d, cls).astype(BF16)
    q = (_take_class(scratch[0], d, cls) * Q_SCALE).astype(BF16)
    if has_prev:
        k = jnp.concatenate([take(scratch[3]), take(scratch[1])], axis=0)
        v = jnp.concatenate([take(scratch[4]), take(scratch[2])], axis=0)
    else:
        k = jnp.concatenate([jnp.zeros((BLOCK, GROUP_W), BF16), take(scratch[1])], axis=0)
        v = jnp.concatenate([jnp.zeros((BLOCK, GROUP_W), BF16), take(scratch[2])], axis=0)
    return q, k, v


def _lane_columns(cols):
    lane = lax.broadcasted_iota(jnp.int32, (BLOCK, LANES), 1)
    out = jnp.zeros((BLOCK, LANES), F32)
    for h, c in enumerate(cols):
        out = jnp.where(lane == h, c, out)
    return out


def band_fwd(qkv, bias, pattern, *, name):
    T = qkv.shape[0]
    d = DILATIONS[pattern]
    rows_per_block = BLOCK * d
    seq_blocks = SEQ // rows_per_block
    has_prev = seq_blocks > 1
    n_in = 5 if has_prev else 3

    def body(*refs):
        ins, b_ref, o_ref, lse_ref = refs[:n_in], refs[n_in], refs[n_in + 1], refs[n_in + 2]
        staged, o_s = refs[n_in + 3:2 * n_in + 3], refs[2 * n_in + 3]
        bd, row, key = _wide_consts()
        valid = _band_valid_wide(pl.program_id(0) % seq_blocks == 0, row, key)
        _stage(ins, staged)
        bias_w = jnp.concatenate([b_ref[h] for h in HEADS], axis=1)
        for cls in _step_classes(d):
            q, k, v = _band_operands(staged, d, cls, has_prev)
            kbd, vbd = _block_diag(k, bd), _block_diag(v, bd)
            sc = jnp.where(valid, _dot(q, kbd, _NT) + bias_w, NEG)
            ms = [jnp.max(sc[:, _seg(h)], axis=1, keepdims=True) for h in HEADS]
            p = jnp.exp(sc - _widen(ms))
            ls = _head_rowsums(p)
            _put_class(o_s, d, cls, _dot(p.astype(BF16), vbd) / _feature_widen(ls))
            lse_ref[_class_rows(d, cls), :] = _lane_columns([ms[h] + jnp.log(ls[h]) for h in HEADS])
        _flush([o_s], [o_ref], d)

    sd = jax.ShapeDtypeStruct
    return pl.pallas_call(
        body, name=name, grid=(T // rows_per_block, d // _classes_per_step(d)), in_specs=_band_in_specs(d, pattern, has_prev),
        out_specs=[pl.BlockSpec((rows_per_block, GROUP_W), lambda tb, r: (tb, 0)),
                   pl.BlockSpec((rows_per_block, LANES), lambda tb, r: (tb, 0))],
        out_shape=[sd((T, GROUP_W), F32), sd((T, LANES), F32)],
        scratch_shapes=_halves_scratch(rows_per_block, n_in + 1),
        compiler_params=_params("parallel", "arbitrary"),
    )(*([qkv] * n_in), bias)


def band_bwd(qkv, bias, lse, do, dlse, pattern, *, name):
    T = qkv.shape[0]
    d = DILATIONS[pattern]
    rows_per_block = BLOCK * d
    seq_blocks = SEQ // rows_per_block
    has_prev = seq_blocks > 1
    n_in = 5 if has_prev else 3
    n_out = 5 if has_prev else 3

    def body(*refs):
        ins, b_ref, lse_ref, do_ref, dlse_ref = refs[:n_in], refs[n_in], refs[n_in + 1], refs[n_in + 2], refs[n_in + 3]
        outs = refs[n_in + 4:n_in + 4 + n_out]
        ds_ref = refs[n_in + 4 + n_out]
        scratch = refs[n_in + 5 + n_out:]
        staged, do_s, out_s = scratch[:n_in], scratch[n_in], scratch[n_in + 1:]
        first_step = jnp.logical_and(pl.program_id(0) == 0, pl.program_id(1) == 0)
        bd, row, key = _wide_consts()
        valid = _band_valid_wide(pl.program_id(0) % seq_blocks == 0, row, key)
        _stage(list(ins) + [do_ref], list(staged) + [do_s])
        bias_w = jnp.concatenate([b_ref[h] for h in HEADS], axis=1)
        ds = None
        for cls in _step_classes(d):
            q, k, v = _band_operands(staged, d, cls, has_prev)
            kbd, vbd = _block_diag(k, bd), _block_diag(v, bd)
            rows = _class_rows(d, cls)
            do = _take_class(do_s, d, cls).astype(BF16)
            lse_t, dlse_t = lse_ref[rows, :], dlse_ref[rows, :]
            lse_w = _widen([lse_t[:, h:h + 1] for h in HEADS])
            dlse_w = _widen([dlse_t[:, h:h + 1] for h in HEADS])
            p = jnp.where(valid, jnp.exp(_dot(q, kbd, _NT) + bias_w - lse_w), 0.0)
            dp = _dot(do, vbd, _NT)
            ds_c = p * (dp - _widen(_head_rowsums(p * dp)) + dlse_w)
            dsb, pb = ds_c.astype(BF16), p.astype(BF16)
            _put_class(out_s[0], d, cls, _dot(dsb, kbd) * Q_SCALE)
            dk = _fold_heads(_dot(dsb, q, _TN), bd)
            dv = _fold_heads(_dot(pb, do, _TN), bd)
            _put_class(out_s[1], d, cls, dk[BLOCK:])
            _put_class(out_s[2], d, cls, dv[BLOCK:])
            if has_prev:
                _put_class(out_s[3], d, cls, dk[:BLOCK])
                _put_class(out_s[4], d, cls, dv[:BLOCK])
            ds = ds_c if ds is None else ds + ds_c
        _flush(out_s, outs, d)

        @pl.when(first_step)
        def _():
            for h in HEADS:
                ds_ref[h] = ds[:, _seg(h)]

        @pl.when(jnp.logical_not(first_step))
        def _():
            for h in HEADS:
                ds_ref[h] += ds[:, _seg(h)]

    big = pl.BlockSpec((rows_per_block, GROUP_W), lambda tb, r: (tb, 0))
    colb = pl.BlockSpec((rows_per_block, LANES), lambda tb, r: (tb, 0))
    sd = jax.ShapeDtypeStruct
    return pl.pallas_call(
        body, name=name, grid=(T // rows_per_block, d // _classes_per_step(d)),
        in_specs=_band_in_specs(d, pattern, has_prev) + [colb, big, colb],
        out_specs=[big] * n_out + [pl.BlockSpec((N_HEADS, BLOCK, BAND), lambda tb, r: (0, 0, 0))],
        out_shape=[sd((T, GROUP_W), F32)] * n_out + [sd((N_HEADS, BLOCK, BAND), F32)],
        scratch_shapes=_halves_scratch(rows_per_block, n_in + 1 + n_out),
        compiler_params=_params("arbitrary", "arbitrary"),
    )(*([qkv] * n_in), bias, lse, do, dlse)


def shift_add(cur, prev, d, *, name):
    rows = BLOCK * d
    nb = cur.shape[0] // rows

    def body(c_ref, p_ref, o_ref):
        keep = (pl.program_id(0) < nb - 1).astype(F32)
        o_ref[...] = c_ref[...] + keep * p_ref[...]

    blk = pl.BlockSpec((rows, GROUP_W), lambda tb: (tb, 0))
    nxt = pl.BlockSpec((rows, GROUP_W), lambda tb: (jnp.minimum(tb + 1, nb - 1), 0))
    return pl.pallas_call(
        body, name=name, grid=(nb,), in_specs=[blk, nxt], out_specs=blk,
        out_shape=jax.ShapeDtypeStruct(cur.shape, F32), compiler_params=_params("parallel"),
    )(cur, prev)


def _pattern_weights(lse_refs, h):
    ls = [r[:, h:h + 1] for r in lse_refs]
    mx = functools.reduce(jnp.maximum, ls)
    es = [jnp.exp(l - mx) for l in ls]
    tot = functools.reduce(lambda a, b: a + b, es)
    return [e / tot for e in es]


def dil_combine_fwd(outs, *, name):
    T = outs[0][0].shape[0]
    n = len(outs)
    tm = 512

    def body(*refs):
        o_refs, l_refs, out_ref = refs[:n], refs[n:2 * n], refs[2 * n]
        for h in range(N_HEADS):
            w = _pattern_weights(l_refs, h)
            acc = w[0] * o_refs[0][:, _hs(h)]
            for p in range(1, n):
                acc = acc + w[p] * o_refs[p][:, _hs(h)]
            out_ref[:, _hs(h)] = acc.astype(BF16)

    big = pl.BlockSpec((tm, GROUP_W), lambda i: (i, 0))
    colb = pl.BlockSpec((tm, LANES), lambda i: (i, 0))
    return pl.pallas_call(
        body, name=name, grid=(T // tm,), in_specs=[big] * n + [colb] * n,
        out_specs=big, out_shape=jax.ShapeDtypeStruct((T, GROUP_W), BF16),
        compiler_params=_params("parallel"),
    )(*[o for o, _ in outs], *[l for _, l in outs])


def dil_combine_bwd(outs, dmixed, *, name):
    T = outs[0][0].shape[0]
    n = len(outs)
    tm = 512

    def body(*refs):
        o_refs, l_refs, do_ref = refs[:n], refs[n:2 * n], refs[2 * n]
        do_refs, dl_refs = refs[2 * n + 1:3 * n + 1], refs[3 * n + 1:]
        for r in dl_refs:
            r[...] = jnp.zeros_like(r)
        for h in range(N_HEADS):
            w = _pattern_weights(l_refs, h)
            do = do_ref[:, _hs(h)]
            dw = [jnp.sum(do * o_refs[p][:, _hs(h)], axis=1, keepdims=True) for p in range(n)]
            mean = functools.reduce(lambda a, b: a + b, [w[p] * dw[p] for p in range(n)])
            for p in range(n):
                do_refs[p][:, _hs(h)] = w[p] * do
                dl_refs[p][:, h:h + 1] = w[p] * (dw[p] - mean)

    big = pl.BlockSpec((tm, GROUP_W), lambda i: (i, 0))
    colb = pl.BlockSpec((tm, LANES), lambda i: (i, 0))
    sd = jax.ShapeDtypeStruct
    res = pl.pallas_call(
        body, name=name, grid=(T // tm,),
        in_specs=[big] * n + [colb] * n + [pl.BlockSpec((tm, GROUP_W), lambda i: (i, 2))],
        out_specs=[big] * n + [colb] * n, out_shape=[sd((T, GROUP_W), F32)] * n + [sd((T, LANES), F32)] * n,
        compiler_params=_params("parallel"),
    )(*[o for o, _ in outs], *[l for _, l in outs], dmixed)
    return list(zip(res[:n], res[n:]))


def dilated_fwd(qkv, bias, tag):
    return [band_fwd(qkv, bias, p, name=f"{tag}_band_fwd{p}") for p in range(len(DILATIONS))]


def dilated_bwd(qkv, bias, outs, dmixed, tag):
    grads = dil_combine_bwd(outs, dmixed, name=f"{tag}_combine_bwd")
    parts, ds_all = [], []
    for p, d in enumerate(DILATIONS):
        (_, lse), (do, dlse) = outs[p], grads[p]
        res = band_bwd(qkv, bias, lse, do, dlse, p, name=f"{tag}_band_bwd{p}")
        dq, dk, dv, ds = res[0], res[1], res[2], res[-1]
        if len(res) > 4:
            dk = shift_add(dk, res[3], d, name=f"{tag}_dk{p}")
            dv = shift_add(dv, res[4], d, name=f"{tag}_dv{p}")
        parts.append([dq, dk, dv])
        ds_all.append(ds)
    return parts, jnp.concatenate(ds_all, axis=0)


def assemble_dqkv(d_sb, d_fox, d_dil, *, name):
    T = d_sb[0].shape[0]
    tr = 512
    n_pat = len(d_dil)
    flat = list(d_sb) + list(d_fox) + [a for part in d_dil for a in part]

    def body(*refs):
        o_ref = refs[-1]
        for j in range(6):
            o_ref[:, j * GROUP_W:(j + 1) * GROUP_W] = refs[j][...].astype(BF16)
        for j in range(3):
            acc = refs[6 + j][...]
            for p in range(1, n_pat):
                acc = acc + refs[6 + 3 * p + j][...]
            o_ref[:, (6 + j) * GROUP_W:(7 + j) * GROUP_W] = acc.astype(BF16)

    blk = pl.BlockSpec((tr, GROUP_W), lambda i: (i, 0))
    return pl.pallas_call(
        body, name=name, grid=(T // tr,), in_specs=[blk] * len(flat),
        out_specs=pl.BlockSpec((tr, QKV_BLOCKS * GROUP_W), lambda i: (i, 0)),
        out_shape=jax.ShapeDtypeStruct((T, QKV_BLOCKS * GROUP_W), BF16), compiler_params=_params("parallel"),
    )(*flat)


def sum_cast(arrs, dtype, *, name):
    R, C = arrs[0].shape
    tr = _largest_tile(R, 512, 16)
    n = len(arrs)

    def body(*refs):
        acc = refs[0][...].astype(F32)
        for r in refs[1:n]:
            acc = acc + r[...].astype(F32)
        refs[n][...] = acc.astype(dtype)

    blk = pl.BlockSpec((tr, C), lambda i: (i, 0))
    return pl.pallas_call(
        body, name=name, grid=(R // tr,), in_specs=[blk] * n, out_specs=blk, out_shape=jax.ShapeDtypeStruct((R, C), dtype),
        compiler_params=_params("parallel"),
    )(*arrs)


GRAD_WIRE = BF16


def _block_diag_halves(w):
    z = jnp.zeros((HEAD_DIM, HEAD_DIM), w.dtype)
    half = lambda a, b: jnp.concatenate([jnp.concatenate([a, z], axis=1), jnp.concatenate([z, b], axis=1)], axis=0)
    return jnp.stack([half(w[0], w[1]), half(w[2], w[3])]).astype(BF16)


def _diag_blocks(d):
    h = HEAD_DIM
    return jnp.stack([d[0, :h, :h], d[0, h:, h:], d[1, :h, :h], d[1, h:, h:]])


def layer_fwd(x, mem2d, W, P, bias, tag):
    s = {}
    s["x"] = x
    h1 = rmsnorm_fwd(x, P["norm_mix_g"], name=f"{tag}_norm_mix")
    qkv = matmul(h1, W["qkv"], out_dtype=BF16, name=f"{tag}_qkv")
    aux = matmul(h1, W["aux"], name=f"{tag}_aux")
    o_sb = sbw_fwd(qkv, name=f"{tag}_sb_fwd")
    cumc = fox_prep(aux, P["bf"], name=f"{tag}_fox_prep")
    cumr = col_to_row(cumc)
    o_fox, o_fox32, lse_fox = foxw_fwd(qkv, cumc, cumr, name=f"{tag}_fox_fwd")
    dil = dilated_fwd(qkv, bias, tag)
    o_dil = dil_combine_fwd(dil, name=f"{tag}_dil_combine")
    o_lru, h_lru = lru_fwd(aux, P["lru_conv_w"], P["lru_conv_b"], P["wa"], P["lru_b_a"], P["wx"], P["lru_b_x"],
                           P["lru_lambda"], name=f"{tag}_lru_fwd")
    mixed = jnp.concatenate([o_sb, o_fox, o_dil, o_lru], axis=1)
    if "rest" in W:
        W.update(W.pop("rest")(mixed))
    x1 = matmul(mixed, W["out"], residual=x, name=f"{tag}_out")
    hq = rmsnorm_fwd(x1, P["norm_cross_g"], name=f"{tag}_norm_cross")
    qc = matmul(hq, W["cq"], out_dtype=BF16, name=f"{tag}_cq")
    memn = rmsnorm_fwd(mem2d, P["norm_mem_g"], name=f"{tag}_norm_mem")
    kv = matmul(memn, W["ckv"], out_dtype=BF16, name=f"{tag}_ckv")
    oc = cross_fwd(qc, kv, name=f"{tag}_cross_fwd")
    x2 = matmul(oc, W["coT"], trans_b=True, residual=x1, name=f"{tag}_co")
    h2 = rmsnorm_fwd(x2, P["norm_ffn_g"], name=f"{tag}_norm_ffn")
    hu = matmul(h2, W["up_u"], trans_b=True, name=f"{tag}_up_u")
    hg = matmul(h2, W["up_g"], trans_b=True, name=f"{tag}_up_g")
    act = glu_fwd(hu, hg, P["wu"], P["wg"], P["bu"], P["bg"], name=f"{tag}_glu_fwd")
    x3 = matmul(act, W["down"], residual=x2, name=f"{tag}_down")
    s.update(h1=h1, qkv=qkv, aux=aux, cumc=cumc, cumr=cumr, lse_fox=lse_fox, o_fox32=o_fox32, dil=dil, h_lru=h_lru, mixed=mixed,
             x1=x1, hq=hq, qc=qc, memn=memn, kv=kv, oc=oc, x2=x2, h2=h2, hu=hu, hg=hg, act=act)
    return x3, s


def layer_bwd(dx3, mem2d, W, P, bias, s, tag, hooks=None):
    mm = functools.partial(matmul, out_dtype=GRAD_WIRE, trans_a=True)
    gW, gP = {}, {}
    hooks = hooks or {}
    dact = matmul(dx3, W["down"], trans_b=True, name=f"{tag}_d_act")
    gW["down"] = mm(s["act"], dx3, name=f"{tag}_g_down")
    dhu, dhg, dwu, dwg, dbu, dbg = glu_bwd(s["hu"], s["hg"], dact, P["wu"], P["wg"], P["bu"], P["bg"], name=f"{tag}_glu_bwd")
    gP["ffn_conv_w"] = jnp.concatenate([dwu, dwg], axis=1)
    gP["ffn_conv_b"] = jnp.concatenate([dbu, dbg], axis=1)
    dh2 = matmul(dhu, W["up_u"], name=f"{tag}_d_h2u")
    dh2 = matmul(dhg, W["up_g"], residual=dh2, name=f"{tag}_d_h2g")
    gW["up_u"] = mm(dhu, s["h2"], name=f"{tag}_g_up_u")
    gW["up_g"] = mm(dhg, s["h2"], name=f"{tag}_g_up_g")
    dx2, gP["norm_ffn_g"] = rmsnorm_bwd(s["x2"], P["norm_ffn_g"], dh2, dx3, name=f"{tag}_norm_ffn_bwd")
    if "ffn" in hooks:
        hooks["ffn"](gW, W, s)
    doc = matmul(dx2, W["coT"], name=f"{tag}_d_oc")
    gW["coT"] = mm(dx2, s["oc"], name=f"{tag}_g_co")
    dqc, dkv = cross_bwd(s["qc"], s["kv"], doc, name=f"{tag}_cross_bwd")
    dhq = matmul(dqc, W["cq"], trans_b=True, name=f"{tag}_d_hq")
    gW["cq"] = mm(s["hq"], dqc, name=f"{tag}_g_cq")
    dmemn = matmul(dkv, W["ckv"], trans_b=True, name=f"{tag}_d_memn")
    gW["ckv"] = mm(s["memn"], dkv, name=f"{tag}_g_ckv")
    _, gP["norm_mem_g"] = rmsnorm_bwd(mem2d, P["norm_mem_g"], dmemn, None, name=f"{tag}_norm_mem_bwd")
    dx1, gP["norm_cross_g"] = rmsnorm_bwd(s["x1"], P["norm_cross_g"], dhq, dx2, name=f"{tag}_norm_cross_bwd")
    dmixed = matmul(dx1, W["out"], trans_b=True, name=f"{tag}_d_mixed")
    gW["out"] = mm(s["mixed"], dx1, name=f"{tag}_g_out")
    if "mid" in hooks:
        hooks["mid"](gW, W, s)
    qkv, aux = s["qkv"], s["aux"]
    d_sb = sbw_bwd(qkv, dmixed, name=f"{tag}_sb_bwd")
    dfq, dfk, dfv, dcc, dcr = foxw_bwd(qkv, s["cumc"], s["cumr"], s["lse_fox"], s["o_fox32"], dmixed, name=f"{tag}_fox_bwd")
    dcum = sum_cast([dcc, row_to_col(dcr)], F32, name=f"{tag}_dcum")
    df, dbf = fox_prep_bwd(aux, P["bf"], dcum, name=f"{tag}_fox_prep_bwd")
    gP["b_forget"] = dbf[0, :N_HEADS]
    d_dil, ds_band = dilated_bwd(qkv, bias, s["dil"], dmixed, tag)
    dlx, dlg, dcw, dcb, dwa, dba, dwx, dbx, dlam = lru_bwd(
        aux, s["h_lru"], dmixed, P["lru_conv_w"], P["lru_conv_b"], P["wa"], P["lru_b_a"], P["wx"], P["lru_b_x"],
        P["lru_lambda"], name=f"{tag}_lru_bwd")
    gP.update(lru_conv_w=dcw, lru_conv_b=dcb, lru_w_a=_diag_blocks(dwa), lru_b_a=dba, lru_w_x=_diag_blocks(dwx),
              lru_b_x=dbx, lru_lambda=dlam)
    dqkv = assemble_dqkv(d_sb, [dfq, dfk, dfv], d_dil, name=f"{tag}_dqkv")
    daux = jnp.concatenate([dlx, dlg, df], axis=1)
    dh1 = matmul(dqkv, W["qkv"], trans_b=True, name=f"{tag}_d_h1a")
    dh1 = matmul(daux, W["aux"], trans_b=True, residual=dh1, name=f"{tag}_d_h1b")
    gW["qkv"] = mm(s["h1"], dqkv, name=f"{tag}_g_qkv")
    gW["aux"] = mm(s["h1"], daux, name=f"{tag}_g_aux")
    dx, gP["norm_mix_g"] = rmsnorm_bwd(s["x"], P["norm_mix_g"], dh1, dx1, name=f"{tag}_norm_mix_bwd")
    return dx, gW, gP, ds_band


def local_step(x, mem, target, weights_of, Ps, rel_bias, final_norm_g, grads_done=None, bwd_hooks=None):
    B = x.shape[0]
    x2d = x.reshape(B * SEQ, D_MODEL)
    mem2d = mem.reshape(B * N_MEM, D_MODEL)
    bias = relbias_expand(rel_bias, name="relbias_expand")
    saved, Ws = [], []
    h = x2d
    for l in range(DEPTH):
        Ws.append(weights_of(l, h))
        h, s = layer_fwd(h, mem2d, Ws[l], Ps[l], bias, f"l{l}")
        saved.append(s)
    loss, dh, d_final = loss_head(h, final_norm_g, target.reshape(B * SEQ, D_MODEL), name="loss_head")
    gWs, gPs, ds_bands = [None] * DEPTH, [None] * DEPTH, []
    for l in range(DEPTH - 1, -1, -1):
        hooks = None if bwd_hooks is None else bwd_hooks(l)
        dh, gWs[l], gPs[l], ds = layer_bwd(dh, mem2d, Ws[l], Ps[l], bias, saved[l], f"l{l}", hooks)
        if grads_done is not None:
            grads_done(l, gWs[l])
        ds_bands.append(ds)
    d_rel = relbias_reduce(sum_cast([d.reshape(-1, BAND) for d in ds_bands], F32, name="ds_band_sum").reshape(-1, BLOCK, BAND),
                           name="relbias_reduce")
    return loss, dh.reshape(B, SEQ, D_MODEL), gWs, gPs, d_rel, d_final


def small_params(p, l):
    row = lambda name: p[name][l].reshape(1, -1)
    ffn_w, ffn_b = p["ffn_conv_w"][l], row("ffn_conv_b")
    return dict(
        norm_mix_g=row("norm_mix_g"), norm_cross_g=row("norm_cross_g"), norm_mem_g=row("norm_mem_g"), norm_ffn_g=row("norm_ffn_g"),
        bf=jnp.pad(row("b_forget"), ((0, 0), (0, LANES - N_HEADS))),
        lru_conv_w=p["lru_conv_w"][l], lru_conv_b=row("lru_conv_b"), wa=_block_diag_halves(p["lru_w_a"][l]), lru_b_a=row("lru_b_a"),
        wx=_block_diag_halves(p["lru_w_x"][l]), lru_b_x=row("lru_b_x"), lru_lambda=row("lru_lambda"),
        wu=ffn_w[:, :D_FF], wg=ffn_w[:, D_FF:], bu=ffn_b[:, :D_FF], bg=ffn_b[:, D_FF:])


def canonical_weights(w_in, w_out, w_cq, w_ck, w_cv, w_co, w_up, w_down):
    sb_fox, fox_f, rest = w_in[:, :6 * GROUP_W], w_in[:, 6 * GROUP_W:6 * GROUP_W + N_HEADS], w_in[:, 6 * GROUP_W + N_HEADS:]
    dil, lru = rest[:, :3 * GROUP_W], rest[:, 3 * GROUP_W:]
    pad = jnp.zeros((w_in.shape[0], AUX_W - 2 * GROUP_W - N_HEADS), w_in.dtype)
    return dict(qkv=jnp.concatenate([sb_fox, dil], axis=1), aux=jnp.concatenate([lru, fox_f, pad], axis=1), out=w_out,
                cq=w_cq, ckv=jnp.concatenate([w_ck, w_cv], axis=1), coT=w_co.T, upT=w_up.T, down=w_down)


def native_grads(g):
    qkv, aux = g["qkv"], g["aux"]
    a, b = 6 * GROUP_W, 6 * GROUP_W + N_HEADS
    w_in = jnp.zeros((qkv.shape[0], b + 5 * GROUP_W), qkv.dtype)
    w_in = w_in.at[:, :a].set(qkv[:, :a]).at[:, a:b].set(aux[:, 2 * GROUP_W:2 * GROUP_W + N_HEADS])
    w_in = w_in.at[:, b:b + 3 * GROUP_W].set(qkv[:, a:]).at[:, b + 3 * GROUP_W:].set(aux[:, :2 * GROUP_W])
    return (w_in, g["out"], g["cq"], g["ckv"][:, :GROUP_W], g["ckv"][:, GROUP_W:], g["coT"].T) + native_ffn_grads(g)


def native_ffn_grads(g):
    return (g["upT"].T, g["down"])


ANY = pl.BlockSpec(memory_space=pl.ANY)
VMEM_SPEC = pl.BlockSpec(memory_space=pltpu.VMEM)


def _place():
    x, y, c = lax.axis_index("x"), lax.axis_index("y"), lax.axis_index("c")
    other_chips = [(1 - x, y), (x, 1 - y), (1 - x, 1 - y)]
    return x, y, c, other_chips


def _gather_body(x_ref, out_ref, send_sems, recv_sems, local_sem):
    x, y, c, chips = _place()
    me, sibling = (x, y, c), (x, y, 1 - c)

    def slot(px, py, pc):
        return out_ref.at[4 * px + 2 * py + pc]

    def copy(k, block, to, src=None):
        return pltpu.make_async_remote_copy(
            src_ref=slot(*block) if src is None else src, dst_ref=slot(*block),
            send_sem=send_sems.at[k], recv_sem=recv_sems.at[k], device_id=to, device_id_type=MESH)

    if local_sem is not None:
        mine = pltpu.make_async_copy(x_ref, slot(*me), local_sem)
        mine.start()
    first = [copy(0, me, sibling, src=x_ref)]
    first += [copy(1 + j, me, (*chip, c), src=x_ref) for j, chip in enumerate(chips)]
    for cp in first:
        cp.start()
    passed = [copy(4 + j, (*chip, c), sibling) for j, chip in enumerate(chips)]
    for j, chip in enumerate(chips):
        copy(1 + j, (*chip, c), me).wait_recv()
        passed[j].start()
    copy(0, sibling, me).wait_recv()
    for j, chip in enumerate(chips):
        copy(4 + j, (*chip, 1 - c), me).wait_recv()
    for cp in first + passed:
        cp.wait_send()
    if local_sem is not None:
        mine.wait()


_GATHER_SEMS = [pltpu.SemaphoreType.DMA((7,)), pltpu.SemaphoreType.DMA((7,)), pltpu.SemaphoreType.DMA]


def allgather_hbm(shard, me, *, name):
    def body(x_ref, out_ref, done_ref, send_sems, recv_sems):
        _gather_body(x_ref, out_ref, send_sems, recv_sems, None)
        done_ref[...] = jnp.zeros_like(done_ref)

    others, done = pl.pallas_call(
        body, name=name, in_specs=[ANY], out_specs=[ANY, VMEM_SPEC],
        out_shape=[jax.ShapeDtypeStruct((N_DEV,) + shard.shape, shard.dtype), jax.ShapeDtypeStruct((8, LANES), F32)],
        scratch_shapes=_GATHER_SEMS[:2],
    )(shard)
    return lax.dynamic_update_slice(others, shard[None], (me, 0, 0)), done


def allgather_small(x, *, name, reduce=False):
    def body(x_ref, out_ref, second_ref, *sems):
        _gather_body(x_ref, out_ref, *sems)
        if reduce:
            acc = out_ref[0]
            for d in range(1, N_DEV):
                acc = acc + out_ref[d]
            second_ref[...] = acc
        else:
            second_ref[...] = jnp.zeros_like(second_ref)

    sd = jax.ShapeDtypeStruct
    return pl.pallas_call(
        body, name=name, in_specs=[VMEM_SPEC], out_specs=[VMEM_SPEC, VMEM_SPEC],
        out_shape=[sd((N_DEV,) + x.shape, x.dtype), sd(x.shape if reduce else (8, LANES), x.dtype)],
        scratch_shapes=_GATHER_SEMS, compiler_params=pltpu.CompilerParams(vmem_limit_bytes=VMEM_LIMIT_V7X),
    )(x)


N_CHIPS = 4


def pair_exchange(g, *, name):
    _, R, C = g.shape

    def body(g_ref, recv_ref, send_sems, recv_sems):
        x, y, c, _ = _place()
        sibling = (x, y, 1 - c)
        remote = [pltpu.make_async_remote_copy(
            src_ref=g_ref.at[2 * q + (1 - c)], dst_ref=recv_ref.at[q], send_sem=send_sems.at[q], recv_sem=recv_sems.at[q],
            device_id=sibling, device_id_type=MESH) for q in range(N_CHIPS)]
        for cp in remote:
            cp.start()
        for cp in remote:
            cp.wait_recv()
        for cp in remote:
            cp.wait_send()

    return pl.pallas_call(
        body, name=name, in_specs=[ANY], out_specs=ANY, out_shape=jax.ShapeDtypeStruct((N_CHIPS, R, C), g.dtype),
        scratch_shapes=[pltpu.SemaphoreType.DMA((N_CHIPS,))] * 2,
    )(g)


def chip_exchange(s, *, name):
    _, R, C = s.shape

    def body(s_ref, o0, o1, o2, send_sems, recv_sems):
        x, y, c, chips = _place()
        outs = (o0, o1, o2)
        copies = [pltpu.make_async_remote_copy(
            src_ref=s_ref.at[2 * cx + cy], dst_ref=outs[j], send_sem=send_sems.at[j], recv_sem=recv_sems.at[j],
            device_id=(cx, cy, c), device_id_type=MESH) for j, (cx, cy) in enumerate(chips)]
        for cp in copies:
            cp.start()
        for cp in copies:
            cp.wait_recv()
        for cp in copies:
            cp.wait_send()

    sd = jax.ShapeDtypeStruct((R, C), s.dtype)
    return pl.pallas_call(
        body, name=name, in_specs=[ANY], out_specs=[ANY] * 3, out_shape=[sd] * 3,
        scratch_shapes=[pltpu.SemaphoreType.DMA((3,)), pltpu.SemaphoreType.DMA((3,))],
    )(s)


HBM_SPEC = pl.BlockSpec(memory_space=pltpu.HBM)
SEM_SPEC = pl.BlockSpec(memory_space=pltpu.SEMAPHORE)
N_PEERS = N_DEV - 1


def _peers():
    x, y, c = lax.axis_index("x"), lax.axis_index("y"), lax.axis_index("c")
    flip = lambda v, bit: 1 - v if bit else v
    out = []
    for k in range(1, N_DEV):
        px, py, pc = flip(x, (k >> 2) & 1), flip(y, (k >> 1) & 1), flip(c, k & 1)
        out.append(((px, py, pc), 4 * px + 2 * py + pc))
    return out, 4 * x + 2 * y + c


def _peer_copies(src_ref, land_ref, send_sems, recv_sems, scatter, landing):
    peers, me = _peers()
    return [pltpu.make_async_remote_copy(
        src_ref=src_ref.at[idx] if scatter else src_ref, dst_ref=land_ref.at[me if landing == "mine" else idx],
        send_sem=send_sems.at[k], recv_sem=recv_sems.at[k], device_id=peer, device_id_type=MESH)
        for k, (peer, idx) in enumerate(peers)]


def exchange_start(src, scatter, *, name):
    shape = (N_DEV,) + src.shape[-2:]

    def body(src_ref, land_ref, send_sems, recv_sems, src_thru, land_thru, token):
        for cp in _peer_copies(src_ref, land_ref, send_sems, recv_sems, scatter, "mine"):
            cp.start()
        token[...] = jnp.zeros_like(token)

    sems = pltpu.SemaphoreType.DMA((N_PEERS,))
    return pl.pallas_call(
        body, name=name,
        out_shape=(sems, sems, pltpu.HBM(src.shape, src.dtype), pltpu.HBM(shape, src.dtype), jax.ShapeDtypeStruct((8, LANES), F32)),
        in_specs=(HBM_SPEC, HBM_SPEC), out_specs=(SEM_SPEC, SEM_SPEC, HBM_SPEC, HBM_SPEC, VMEM_SPEC),
        input_output_aliases={0: 2, 1: 3},
        compiler_params=pltpu.CompilerParams(has_side_effects=pltpu.SideEffectType.DATAFLOW_SIDE_EFFECTING),
    )(pltpu.with_memory_space_constraint(src, pltpu.HBM), pltpu.with_memory_space_constraint(lax.empty(shape, src.dtype), pltpu.HBM))


def exchange_wait(started, after, scatter, *, name):
    send_sems, recv_sems, src_thru, land_thru, _ = started

    def body(src_ref, land_ref, send_sems, recv_sems, after_ref, src_dead, got_ref):
        for cp in _peer_copies(src_ref, land_ref, send_sems, recv_sems, scatter, "theirs"):
            cp.wait_send()
            cp.wait_recv()

    return pl.pallas_call(
        body, name=name, out_shape=(pltpu.HBM(src_thru.shape, src_thru.dtype), pltpu.HBM(land_thru.shape, land_thru.dtype)),
        in_specs=(HBM_SPEC, HBM_SPEC, SEM_SPEC, SEM_SPEC, ANY), out_specs=(HBM_SPEC, HBM_SPEC), input_output_aliases={0: 0, 1: 1},
        compiler_params=pltpu.CompilerParams(has_side_effects=pltpu.SideEffectType.DATAFLOW_SIDE_EFFECTING),
    )(src_thru, land_thru, send_sems, recv_sems, after)[1]


def sum_blocks(blocks, *, name):
    n, R, C = blocks.shape
    tr = _largest_tile(R, 512, 16)

    def body(b_ref, o_ref):
        acc = b_ref[0].astype(F32)
        for d in range(1, n):
            acc = acc + b_ref[d].astype(F32)
        o_ref[...] = acc

    return pl.pallas_call(
        body, name=name, grid=(R // tr,),
        in_specs=[pl.BlockSpec((n, tr, C), lambda i: (0, i, 0))], out_specs=pl.BlockSpec((tr, C), lambda i: (i, 0)),
        out_shape=jax.ShapeDtypeStruct((R, C), F32), compiler_params=_params("parallel"),
    )(blocks)


WEIGHTS = ("norm_mix_g", "w_in", "b_forget", "lru_conv_w", "lru_conv_b", "lru_w_a", "lru_b_a", "lru_w_x", "lru_b_x", "lru_lambda",
           "w_out", "norm_cross_g", "norm_mem_g", "w_cq", "w_ck", "w_cv", "w_co", "norm_ffn_g", "w_up", "ffn_conv_w", "ffn_conv_b",
           "w_down", "rel_bias", "final_norm_g")
LARGE = ("w_in", "w_out", "w_cq", "w_ck", "w_cv", "w_co", "w_up", "w_down")
COLUMN_SPLIT_SMALL = ("lru_conv_w", "ffn_conv_w")
PACK = (("qkv", 128, 2304), ("aux", 128, 640), ("out", 128, 1024), ("cq", 128, 256), ("ckv", 128, 512), ("coT", 128, 256),
        ("upT", 704, 1024), ("down", 352, 1024))
PACK_W = 1024


def _pack_rows(parts):
    return jnp.concatenate([p.reshape(-1, PACK_W) for p in parts], axis=0)


def _pad_rows(flat, mult=8 * LANES):
    n = flat.shape[0]
    return jnp.pad(flat, (0, (-n) % mult)).reshape(-1, LANES)


def kernel(x, mem, norm_mix_g, w_in, b_forget, lru_conv_w, lru_conv_b, lru_w_a, lru_b_a, lru_w_x, lru_b_x, lru_lambda, w_out, norm_cross_g, norm_mem_g, w_cq, w_ck, w_cv, w_co, norm_ffn_g, w_up, ffn_conv_w, ffn_conv_b, w_down, rel_bias, final_norm_g, loss_target, m_norm_mix_g, m_w_in, m_b_forget, m_lru_conv_w, m_lru_conv_b, m_lru_w_a, m_lru_b_a, m_lru_w_x, m_lru_b_x, m_lru_lambda, m_w_out, m_norm_cross_g, m_norm_mem_g, m_w_cq, m_w_ck, m_w_cv, m_w_co, m_norm_ffn_g, m_w_up, m_ffn_conv_w, m_ffn_conv_b, m_w_down, m_rel_bias, m_final_norm_g, v_norm_mix_g, v_w_in, v_b_forget, v_lru_conv_w, v_lru_conv_b, v_lru_w_a, v_lru_b_a, v_lru_w_x, v_lru_b_x, v_lru_lambda, v_w_out, v_norm_cross_g, v_norm_mem_g, v_w_cq, v_w_ck, v_w_cv, v_w_co, v_norm_ffn_g, v_w_up, v_ffn_conv_w, v_ffn_conv_b, v_w_down, v_rel_bias, v_final_norm_g):
    w = dict(norm_mix_g=norm_mix_g, w_in=w_in, b_forget=b_forget, lru_conv_w=lru_conv_w, lru_conv_b=lru_conv_b, lru_w_a=lru_w_a,
             lru_b_a=lru_b_a, lru_w_x=lru_w_x, lru_b_x=lru_b_x, lru_lambda=lru_lambda, w_out=w_out, norm_cross_g=norm_cross_g,
             norm_mem_g=norm_mem_g, w_cq=w_cq, w_ck=w_ck, w_cv=w_cv, w_co=w_co, norm_ffn_g=norm_ffn_g, w_up=w_up,
             ffn_conv_w=ffn_conv_w, ffn_conv_b=ffn_conv_b, w_down=w_down, rel_bias=rel_bias, final_norm_g=final_norm_g)
    m = dict(norm_mix_g=m_norm_mix_g, w_in=m_w_in, b_forget=m_b_forget, lru_conv_w=m_lru_conv_w, lru_conv_b=m_lru_conv_b,
             lru_w_a=m_lru_w_a, lru_b_a=m_lru_b_a, lru_w_x=m_lru_w_x, lru_b_x=m_lru_b_x, lru_lambda=m_lru_lambda, w_out=m_w_out,
             norm_cross_g=m_norm_cross_g, norm_mem_g=m_norm_mem_g, w_cq=m_w_cq, w_ck=m_w_ck, w_cv=m_w_cv, w_co=m_w_co,
             norm_ffn_g=m_norm_ffn_g, w_up=m_w_up, ffn_conv_w=m_ffn_conv_w, ffn_conv_b=m_ffn_conv_b, w_down=m_w_down,
             rel_bias=m_rel_bias, final_norm_g=m_final_norm_g)
    v = dict(norm_mix_g=v_norm_mix_g, w_in=v_w_in, b_forget=v_b_forget, lru_conv_w=v_lru_conv_w, lru_conv_b=v_lru_conv_b,
             lru_w_a=v_lru_w_a, lru_b_a=v_lru_b_a, lru_w_x=v_lru_w_x, lru_b_x=v_lru_b_x, lru_lambda=v_lru_lambda, w_out=v_w_out,
             norm_cross_g=v_norm_cross_g, norm_mem_g=v_norm_mem_g, w_cq=v_w_cq, w_ck=v_w_ck, w_cv=v_w_cv, w_co=v_w_co,
             norm_ffn_g=v_norm_ffn_g, w_up=v_w_up, ffn_conv_w=v_ffn_conv_w, ffn_conv_b=v_ffn_conv_b, w_down=v_w_down,
             rel_bias=v_rel_bias, final_norm_g=v_final_norm_g)
    me = 4 * lax.axis_index("x") + 2 * lax.axis_index("y") + lax.axis_index("c")

    conv_shard = jnp.concatenate([w[n].reshape(-1) for n in COLUMN_SPLIT_SMALL])
    conv_all, conv_gathered = allgather_small(_pad_rows(conv_shard), name="gather_conv")
    conv_all = conv_all.reshape(N_DEV, -1)
    full = dict(w)
    off = 0
    for n in COLUMN_SPLIT_SMALL:
        d, k, c = w[n].shape
        blocks = conv_all[:, off:off + d * k * c].reshape(N_DEV, d, k, c)
        full[n] = blocks.transpose(1, 2, 0, 3).reshape(d, k, N_DEV * c)
        off += d * k * c

    IN, MID, FFN = PACK[:2], PACK[2:6], PACK[6:]
    REST = MID + FFN

    def packed_shard(l, group):
        canon = canonical_weights(*[w[n][l] for n in LARGE])
        return _pack_rows([canon[k].astype(BF16) for k, _, _ in group])

    def unpack_weights(packed, group):
        W, row = {}, 0
        for k, r, c in group:
            n_rows = r * c // PACK_W
            W[k] = packed[:, row:row + n_rows].reshape(N_DEV * r, c)
            row += n_rows
        if "upT" in W:
            upT = W.pop("upT")
            W["up_u"], W["up_g"] = upT[:D_FF], upT[D_FF:]
        return W

    def packed_grads(gW, group):
        g = dict(gW)
        if "up_u" in g:
            g["upT"] = jnp.concatenate([g.pop("up_u"), g.pop("up_g")], axis=0)
        return jnp.concatenate([g[k].reshape(N_DEV, r * c // PACK_W, PACK_W) for k, r, c in group], axis=1)

    def unpack_grads(shard_sum, group):
        g, row = {}, 0
        for k, r, c in group:
            n_rows = r * c // PACK_W
            g[k] = shard_sum[row:row + n_rows].reshape(r, c)
            row += n_rows
        return g

    def own_block_in(landed, block):
        return lax.dynamic_update_slice(landed, block[None], (me, 0, 0))

    def gathered_weights(copies, shard, after, group, name):
        return unpack_weights(own_block_in(exchange_wait(copies, after, False, name=name), shard), group)

    def scattered_sum(src, copies, after, tag):
        landed = exchange_wait(copies, after, True, name=f"{tag}_wait")
        mine = lax.dynamic_index_in_dim(src, me, axis=0, keepdims=False)
        return sum_blocks(own_block_in(landed, mine), name=f"{tag}_sum")

    last = DEPTH - 1
    in0, gathered = allgather_hbm(packed_shard(0, IN) + conv_gathered[0, 0].astype(BF16), me, name="gather_weights")
    rest0_shard = packed_shard(0, REST) + gathered[0, 0].astype(BF16)
    gather_rest0 = exchange_start(rest0_shard, False, name="gather_rest0_start")
    last_shard = packed_shard(last, PACK) + gather_rest0[4][0, 0].astype(BF16)
    gather_last = exchange_start(last_shard, False, name="gather_last_start")
    started = gather_last[4][0, 0]
    layer_weights = {}

    def weights_of(l, h):
        if l == 0:
            W = unpack_weights(in0, IN)
            W["rest"] = lambda after: gathered_weights(gather_rest0, rest0_shard, after, REST, "gather_rest0_wait")
        else:
            assert l == last
            W = gathered_weights(gather_last, last_shard, h, PACK, "gather_last_wait")
        layer_weights[l] = W
        return W

    in_flight = {}

    def scatter(key, g_all, name):
        in_flight[key] = (g_all, exchange_start(g_all, True, name=name))
        return in_flight[key][1][4][0, 0].astype(BF16)

    def grads_done(l, gW):
        if l == last:
            W0 = layer_weights[0]
            W0["down"] = W0["down"] + scatter("last", packed_grads(gW, PACK), "grads_last_start")

    def ffn0_grads_done(gW, W, s):
        W["coT"] = W["coT"] + scatter("ffn0", packed_grads({k: gW[k] for k in ("up_u", "up_g", "down")}, FFN), "grads_ffn0_start")

    def mid0_grads_done(gW, W, s):
        s["cumc"] = s["cumc"] + scatter("mid0", packed_grads({k: gW[k] for k, _, _ in MID}, MID), "grads_mid0_start").astype(F32)

    Ps = [small_params(full, l) for l in range(DEPTH)]
    Ps[0]["norm_mix_g"] = Ps[0]["norm_mix_g"] + started
    loss, grad_x, gWs, gPs, d_rel, d_final = local_step(
        x, mem, loss_target, weights_of, Ps, rel_bias, final_norm_g.reshape(1, -1), grads_done,
        lambda l: {"ffn": ffn0_grads_done, "mid": mid0_grads_done} if l == 0 else None)

    shard_grads = {last: unpack_grads(scattered_sum(*in_flight["last"], grad_x, "grads_last"), PACK)}
    shard_grads[0] = unpack_grads(scattered_sum(*in_flight["ffn0"], grad_x, "grads_ffn0"), FFN)
    shard_grads[0].update(unpack_grads(scattered_sum(*in_flight["mid0"], grad_x, "grads_mid0"), MID))

    g_all = packed_grads({k: gWs[0][k] for k, _, _ in IN}, IN)
    rows = g_all.shape[1]
    got = pair_exchange(g_all, name="grads_pair_exchange")
    own = lax.dynamic_index_in_dim(g_all.reshape(N_CHIPS, 2, rows, PACK_W), lax.axis_index("c"), axis=1, keepdims=False)
    pair = sum_cast([own.reshape(-1, PACK_W), got.reshape(-1, PACK_W)], GRAD_WIRE, name="grads_pair_sum").reshape(N_CHIPS, rows, PACK_W)
    from_x, from_y, from_xy = chip_exchange(pair, name="grads_chip_exchange")
    mine = lax.dynamic_index_in_dim(pair, 2 * lax.axis_index("x") + lax.axis_index("y"), axis=0, keepdims=False)
    shard_grads[0].update(unpack_grads(sum_cast([mine, from_x, from_y, from_xy], F32, name="grads_chip_sum"), IN))

    grads = {}
    per_layer = [native_grads(shard_grads[l]) for l in range(DEPTH)]
    for i, n in enumerate(LARGE):
        grads[n] = jnp.stack([per_layer[l][i] for l in range(DEPTH)])

    small_names = [n for n in WEIGHTS if n not in LARGE and n not in ("rel_bias", "final_norm_g")]
    pieces = [gPs[l][n].reshape(-1) for n in small_names for l in range(DEPTH)] + [d_rel.reshape(-1), d_final.reshape(-1), loss[0, :1]]
    sizes = [p.shape[0] for p in pieces]
    _, total = allgather_small(_pad_rows(jnp.concatenate(pieces)), name="allreduce_small", reduce=True)
    total = total.reshape(-1)
    off, it = 0, iter(sizes)
    for n in small_names:
        per = []
        for l in range(DEPTH):
            sz = next(it)
            per.append(total[off:off + sz])
            off += sz
        full_shape = (DEPTH,) + full[n].shape[1:]
        gfull = jnp.stack(per).reshape(full_shape)
        if n in COLUMN_SPLIT_SMALL:
            c = w[n].shape[-1]
            gfull = lax.dynamic_slice_in_dim(gfull, me * c, c, axis=gfull.ndim - 1)
        grads[n] = gfull
    grads["rel_bias"] = total[off:off + rel_bias.size].reshape(rel_bias.shape)
    off += rel_bias.size
    grads["final_norm_g"] = total[off:off + D_MODEL]
    off += D_MODEL
    loss_out = total[off]

    delta, new_m, new_v = {}, {}, {}
    for n in LARGE:
        shape = w[n].shape
        two_d = lambda a: a.reshape(-1, shape[-1])
        d_, m_, v_ = adamw(two_d(w[n]), two_d(grads[n]), two_d(m[n]), two_d(v[n]), name=f"adamw_{n}")
        delta[n], new_m[n], new_v[n] = d_.reshape(shape), m_.reshape(shape), v_.reshape(shape)
    small_all = [n for n in WEIGHTS if n not in LARGE]
    two_d = lambda a: a.reshape(-1, a.shape[-1])
    d_, m_, v_ = adamw_many(*[[two_d(src[n]) for n in small_all] for src in (w, grads, m, v)], name="adamw_small")
    for i, n in enumerate(small_all):
        delta[n], new_m[n], new_v[n] = (a[i].reshape(w[n].shape) for a in (d_, m_, v_))

    return (loss_out, grad_x, *[grads[n] for n in WEIGHTS], *[delta[n] for n in WEIGHTS], *[new_m[n] for n in WEIGHTS],
            *[new_v[n] for n in WEIGHTS])
```

```python
import functools
import math

import numpy as np
import jax
import jax.numpy as jnp
from jax import lax
from jax.experimental import pallas as pl
from jax.experimental.pallas import tpu as pltpu

F32 = jnp.float32
BF16 = jnp.bfloat16
MESH = pl.DeviceIdType.MESH

N_DEV = 8
D_MODEL = 1024
SEQ = 2048
DEPTH = 2
HEAD_DIM = 64
N_HEADS = 4
GROUP_W = N_HEADS * HEAD_DIM
D_FF = 2816
N_MEM = 256
NUM_BUCKETS = 32
MAX_DISTANCE = 2048
BLOCK = 128
DILATIONS = (1, 4, 16)
EPS = 1e-6
LRU_C = 8.0
Q_SCALE = HEAD_DIM ** -0.5
AUX_W = 640
LRU_HALF_W = 128
LRU_HALVES = GROUP_W // LRU_HALF_W
ADAM_LR, ADAM_B1, ADAM_B2, ADAM_EPS, ADAM_WD, ADAM_STEP = 0.001, 0.9, 0.999, 1e-08, 0.01, 10

VMEM_LIMIT_V7X = 48 * 1024 * 1024


def _params(*sem):
    return pltpu.CompilerParams(dimension_semantics=sem if sem else None, vmem_limit_bytes=VMEM_LIMIT_V7X)


def _pick(n, cands):
    for c in cands:
        if n % c == 0:
            return c
    return n


def _largest_tile(n, cap, align):
    best = None
    for t in range(align, min(n, cap) + 1, align):
        if n % t == 0:
            best = t
    return n if best is None else best


def matmul(a, b, *, name, trans_a=False, trans_b=False, out_dtype=F32, residual=None):
    (K, M) = a.shape if trans_a else a.shape[::-1]
    (N, Kb) = b.shape if trans_b else b.shape[::-1]
    assert K == Kb, (a.shape, b.shape)
    tm = _largest_tile(M, 1408 if trans_a else (1024 if K <= 1024 else 512), 128)
    tn = _largest_tile(N, 1408, 128)
    tk = _largest_tile(K, 1024 if trans_a else 2816, 128)
    nk = K // tk
    a_spec = pl.BlockSpec((tk, tm), lambda i, j, k: (k, i)) if trans_a else pl.BlockSpec((tm, tk), lambda i, j, k: (i, k))
    b_spec = pl.BlockSpec((tn, tk), lambda i, j, k: (j, k)) if trans_b else pl.BlockSpec((tk, tn), lambda i, j, k: (k, j))
    o_spec = pl.BlockSpec((tm, tn), lambda i, j, k: (i, j))
    dims = (((0 if trans_a else 1,), (1 if trans_b else 0,)), ((), ()))
    has_res = residual is not None

    def body(*refs):
        a_ref, b_ref = refs[0], refs[1]
        r_ref = refs[2] if has_res else None
        part = lax.dot_general(a_ref[...].astype(BF16), b_ref[...].astype(BF16), dims, preferred_element_type=F32)
        if nk == 1:
            if has_res:
                part = part + r_ref[...].astype(F32)
            refs[-1][...] = part.astype(out_dtype)
            return
        o_ref, acc_ref = refs[-2], refs[-1]
        k = pl.program_id(2)

        @pl.when(k == 0)
        def _():
            acc_ref[...] = part

        @pl.when(k > 0)
        def _():
            acc_ref[...] += part

        @pl.when(k == nk - 1)
        def _():
            r = acc_ref[...]
            if has_res:
                r = r + r_ref[...].astype(F32)
            o_ref[...] = r.astype(out_dtype)

    ops = (a, b) + ((residual,) if has_res else ())
    return pl.pallas_call(
        body, name=name, grid=(M // tm, N // tn, nk),
        in_specs=[a_spec, b_spec] + ([o_spec] if has_res else []),
        out_specs=o_spec, out_shape=jax.ShapeDtypeStruct((M, N), out_dtype),
        scratch_shapes=[pltpu.VMEM((tm, tn), F32)] if nk > 1 else [],
        compiler_params=_params("parallel", "parallel", "arbitrary"),
    )(*ops)


def rmsnorm_fwd(x, g, *, name):
    R, D = x.shape
    tr = _pick(R, (512, 256))

    def body(x_ref, g_ref, o_ref):
        xv = x_ref[...]
        r = lax.rsqrt(jnp.mean(xv * xv, axis=-1, keepdims=True) + EPS)
        o_ref[...] = (xv * r * g_ref[...]).astype(BF16)

    return pl.pallas_call(
        body, name=name, grid=(R // tr,),
        in_specs=[pl.BlockSpec((tr, D), lambda i: (i, 0)), pl.BlockSpec((1, D), lambda i: (0, 0))],
        out_specs=pl.BlockSpec((tr, D), lambda i: (i, 0)), out_shape=jax.ShapeDtypeStruct((R, D), BF16),
        compiler_params=_params("parallel"),
    )(x, g)


def rmsnorm_bwd(x, g, dh, dres, *, name):
    R, D = x.shape
    tr = _pick(R, (512, 256))
    has_res = dres is not None

    def body(*refs):
        x_ref, g_ref, dh_ref = refs[:3]
        dx_ref, dg_ref = refs[-2], refs[-1]
        xv = x_ref[...]
        r = lax.rsqrt(jnp.mean(xv * xv, axis=-1, keepdims=True) + EPS)
        n = xv * r
        dhv = dh_ref[...]
        dn = dhv * g_ref[...]
        dx = r * (dn - n * jnp.mean(dn * n, axis=-1, keepdims=True))
        if has_res:
            dx = dx + refs[3][...]
        dx_ref[...] = dx
        part = jnp.sum(dhv * n, axis=0, keepdims=True)

        @pl.when(pl.program_id(0) == 0)
        def _():
            dg_ref[...] = part

        @pl.when(pl.program_id(0) > 0)
        def _():
            dg_ref[...] += part

    row = pl.BlockSpec((tr, D), lambda i: (i, 0))
    vec = pl.BlockSpec((1, D), lambda i: (0, 0))
    ops = (x, g, dh) + ((dres,) if has_res else ())
    return pl.pallas_call(
        body, name=name, grid=(R // tr,),
        in_specs=[row, vec, row] + ([row] if has_res else []),
        out_specs=[row, vec],
        out_shape=[jax.ShapeDtypeStruct((R, D), F32), jax.ShapeDtypeStruct((1, D), F32)],
        compiler_params=_params("arbitrary"),
    )(*ops)


_SQRT_HALF = 0.7071067811865476
_INV_SQRT_2PI = 0.3989422804014327


def _normal_cdf_pdf(x):
    ax = jnp.abs(x) * _SQRT_HALF
    t = 1.0 / (1.0 + 0.3275911 * ax)
    poly = t * (0.254829592 + t * (-0.284496736 + t * (1.421413741 + t * (-1.453152027 + t * 1.061405429))))
    e = jnp.exp(-0.5 * x * x)
    half_tail = 0.5 * poly * e
    return jnp.where(x < 0, half_tail, 1.0 - half_tail), e


def _gelu_cdf(x):
    return _normal_cdf_pdf(x)[0]


def _gelu_and_grad(x):
    cdf, e = _normal_cdf_pdf(x)
    return x * cdf, cdf + x * _INV_SQRT_2PI * e


def _shift_down(main, halo, first, shifts):
    halo = jnp.where(first, 0.0, halo)
    ext = jnp.concatenate([halo, main], axis=0)
    return [pltpu.roll(ext, s, 0)[8:] for s in shifts]


def _conv3(main, halo, first, w, b):
    m1, m2 = _shift_down(main, halo, first, (1, 2))
    return ((b + w[0:1] * m2) + w[1:2] * m1) + w[2:3] * main, m1, m2


def glu_fwd(hu, hg, wu, wg, bu, bg, *, name):
    T, F = hu.shape
    tm, tf = 512, _largest_tile(F, 704, 128)
    hb = tm // 8
    blocks_per_example = SEQ // tm

    def body(hu_ref, hg_ref, hau_ref, hag_ref, wu_ref, wg_ref, bu_ref, bg_ref, o_ref):
        first = pl.program_id(0) % blocks_per_example == 0
        up, _, _ = _conv3(hu_ref[...], hau_ref[...], first, wu_ref[...], bu_ref[...])
        gate, _, _ = _conv3(hg_ref[...], hag_ref[...], first, wg_ref[...], bg_ref[...])
        o_ref[...] = (gate * _gelu_cdf(gate) * up).astype(BF16)

    main = pl.BlockSpec((tm, tf), lambda i, j: (i, j))
    halo = pl.BlockSpec((8, tf), lambda i, j: (jnp.maximum(i * hb - 1, 0), j))
    w3 = pl.BlockSpec((3, tf), lambda i, j: (0, j))
    b1 = pl.BlockSpec((1, tf), lambda i, j: (0, j))
    return pl.pallas_call(
        body, name=name, grid=(T // tm, F // tf),
        in_specs=[main, main, halo, halo, w3, w3, b1, b1],
        out_specs=main, out_shape=jax.ShapeDtypeStruct((T, F), BF16),
        compiler_params=_params("parallel", "parallel"),
    )(hu, hg, hu, hg, wu, wg, bu, bg)


def glu_bwd(hu, hg, dact, wu, wg, bu, bg, *, name):
    T, F = hu.shape
    tm, tf = 512, _largest_tile(F, 704, 128)
    hb = tm // 8
    blocks_per_example = SEQ // tm
    n_halo_blocks = T // 8
    n_ext = tm + 8

    def body(hu_ref, hg_ref, hau_ref, hag_ref, hnu_ref, hng_ref, da_ref, dan_ref, wu_ref, wg_ref, bu_ref, bg_ref,
             du_ref, dg_ref, dwu_ref, dwg_ref, dbu_ref, dbg_ref):
        i = pl.program_id(1)
        first = i % blocks_per_example == 0
        last = i % blocks_per_example == blocks_per_example - 1
        wu, wg = wu_ref[...], wg_ref[...]

        def conv_ext(main_ref, prev_ref, next_ref, w, b):
            ext = jnp.concatenate([jnp.where(first, 0.0, prev_ref[...]), main_ref[...], next_ref[...]], axis=0)
            x0, x1, x2 = ext[8:], pltpu.roll(ext, 1, 0)[8:], pltpu.roll(ext, 2, 0)[8:]
            return ((b + w[0:1] * x2) + w[1:2] * x1) + w[2:3] * x0, x0, x1, x2

        up, xu, u1, u2 = conv_ext(hu_ref, hau_ref, hnu_ref, wu, bu_ref[...])
        gate, xg, g1, g2 = conv_ext(hg_ref, hag_ref, hng_ref, wg, bg_ref[...])
        act, dact_dgate = _gelu_and_grad(gate)
        da = jnp.concatenate([da_ref[...], jnp.where(last, 0.0, dan_ref[...])], axis=0)
        dup = da * act
        dgate = da * up * dact_dgate

        def conv_t(d, w):
            return (w[2:3] * d[:tm] + w[1:2] * pltpu.roll(d, n_ext - 1, 0)[:tm] + w[0:1] * pltpu.roll(d, n_ext - 2, 0)[:tm]).astype(BF16)

        du_ref[...] = conv_t(dup, wu)
        dg_ref[...] = conv_t(dgate, wg)

        def sums(d, x0, x1, x2):
            s = lambda v: jnp.sum(v[:tm], axis=0, keepdims=True)
            return jnp.concatenate([s(d * x2), s(d * x1), s(d * x0)], axis=0), s(d)

        pwu, pbu = sums(dup, xu, u1, u2)
        pwg, pbg = sums(dgate, xg, g1, g2)

        @pl.when(i == 0)
        def _():
            dwu_ref[...] = pwu
            dwg_ref[...] = pwg
            dbu_ref[...] = pbu
            dbg_ref[...] = pbg

        @pl.when(i > 0)
        def _():
            dwu_ref[...] += pwu
            dwg_ref[...] += pwg
            dbu_ref[...] += pbu
            dbg_ref[...] += pbg

    main = pl.BlockSpec((tm, tf), lambda j, i: (i, j))
    before = pl.BlockSpec((8, tf), lambda j, i: (jnp.maximum(i * hb - 1, 0), j))
    after = pl.BlockSpec((8, tf), lambda j, i: (jnp.minimum((i + 1) * hb, n_halo_blocks - 1), j))
    w3 = pl.BlockSpec((3, tf), lambda j, i: (0, j))
    b1 = pl.BlockSpec((1, tf), lambda j, i: (0, j))
    sd = jax.ShapeDtypeStruct
    return pl.pallas_call(
        body, name=name, grid=(F // tf, T // tm),
        in_specs=[main, main, before, before, after, after, main, after, w3, w3, b1, b1],
        out_specs=[main, main, w3, w3, b1, b1],
        out_shape=[sd((T, F), BF16), sd((T, F), BF16), sd((3, F), F32), sd((3, F), F32), sd((1, F), F32), sd((1, F), F32)],
        compiler_params=_params("parallel", "arbitrary"),
    )(hu, hg, hu, hg, hu, hg, dact, dact, wu, wg, bu, bg)


def loss_head(x, g, target, *, name):
    T, D = x.shape
    tr = 256

    def body(x_ref, g_ref, t_ref, loss_ref, dx_ref, dg_ref):
        xv = x_ref[...]
        gv = g_ref[...]
        r = lax.rsqrt(jnp.mean(xv * xv, axis=-1, keepdims=True) + EPS)
        n = xv * r
        err = n * gv - t_ref[...]
        part_loss = jnp.zeros((1, 128), F32) + 0.5 * jnp.sum(jnp.mean(err * err, axis=-1, keepdims=True))
        dy = err * (1.0 / D)
        dn = dy * gv
        dx_ref[...] = r * (dn - n * jnp.mean(dn * n, axis=-1, keepdims=True))
        part_g = jnp.sum(dy * n, axis=0, keepdims=True)

        @pl.when(pl.program_id(0) == 0)
        def _():
            loss_ref[...] = part_loss
            dg_ref[...] = part_g

        @pl.when(pl.program_id(0) > 0)
        def _():
            loss_ref[...] += part_loss
            dg_ref[...] += part_g

    row = pl.BlockSpec((tr, D), lambda i: (i, 0))
    vec = pl.BlockSpec((1, D), lambda i: (0, 0))
    sd = jax.ShapeDtypeStruct
    return pl.pallas_call(
        body, name=name, grid=(T // tr,),
        in_specs=[row, vec, row],
        out_specs=[pl.BlockSpec((1, 128), lambda i: (0, 0)), row, vec],
        out_shape=[sd((1, 128), F32), sd((T, D), F32), sd((1, D), F32)],
        compiler_params=_params("arbitrary"),
    )(x, g, target)


def adamw(w, g, m, v, *, name):
    R, C = w.shape
    tr = _pick(R, (256, 128, 64, 32, 16, 8))

    def body(w_ref, g_ref, m_ref, v_ref, d_ref, nm_ref, nv_ref):
        gv = g_ref[...]
        mn = ADAM_B1 * m_ref[...] + (1.0 - ADAM_B1) * gv
        vn = ADAM_B2 * v_ref[...] + (1.0 - ADAM_B2) * (gv * gv)
        m_hat = mn / (1.0 - ADAM_B1 ** ADAM_STEP)
        v_hat = vn / (1.0 - ADAM_B2 ** ADAM_STEP)
        d_ref[...] = -ADAM_LR * (m_hat / (jnp.sqrt(v_hat) + ADAM_EPS) + ADAM_WD * w_ref[...])
        nm_ref[...] = mn
        nv_ref[...] = vn

    blk = pl.BlockSpec((tr, C), lambda i: (i, 0))
    sd = jax.ShapeDtypeStruct((R, C), F32)
    return pl.pallas_call(
        body, name=name, grid=(R // tr,), in_specs=[blk] * 4, out_specs=[blk] * 3, out_shape=[sd] * 3,
        compiler_params=_params("parallel"),
    )(w, g, m, v)


def adamw_many(ws, gs, ms, vs, *, name):
    n = len(ws)

    def body(*refs):
        ins, outs = refs[:4 * n], refs[4 * n:]
        for i in range(n):
            w_ref, g_ref, m_ref, v_ref = ins[i], ins[n + i], ins[2 * n + i], ins[3 * n + i]
            gv = g_ref[...]
            mn = ADAM_B1 * m_ref[...] + (1.0 - ADAM_B1) * gv
            vn = ADAM_B2 * v_ref[...] + (1.0 - ADAM_B2) * (gv * gv)
            m_hat = mn / (1.0 - ADAM_B1 ** ADAM_STEP)
            v_hat = vn / (1.0 - ADAM_B2 ** ADAM_STEP)
            outs[i][...] = -ADAM_LR * (m_hat / (jnp.sqrt(v_hat) + ADAM_EPS) + ADAM_WD * w_ref[...])
            outs[n + i][...] = mn
            outs[2 * n + i][...] = vn

    vm = pl.BlockSpec(memory_space=pltpu.VMEM)
    shapes = [jax.ShapeDtypeStruct(w.shape, F32) for w in ws]
    res = pl.pallas_call(
        body, name=name, in_specs=[vm] * (4 * n), out_specs=[vm] * (3 * n), out_shape=shapes * 3, compiler_params=_params(),
    )(*ws, *gs, *ms, *vs)
    return res[:n], res[n:2 * n], res[2 * n:]


def _softplus(x):
    return jnp.maximum(x, 0.0) + jnp.log(1.0 + jnp.exp(-jnp.abs(x)))


def _lru_gates(x, cw, cb, wa, ba, wx, bx, lam):
    S = x.shape[0]
    row = lax.broadcasted_iota(jnp.int32, (S, 1), 0)

    def back(s):
        return jnp.where(row >= s, pltpu.roll(x, s, 0), 0.0)

    xc = (((cb + cw[0:1] * back(3)) + cw[1:2] * back(2)) + cw[2:3] * back(1)) + cw[3:4] * x
    xb = xc.astype(BF16)
    r = jax.nn.sigmoid(jnp.dot(xb, wa, preferred_element_type=F32) + ba)
    ig = jax.nn.sigmoid(jnp.dot(xb, wx, preferred_element_type=F32) + bx)
    sp = _softplus(-lam)
    la = -LRU_C * r * sp
    a = jnp.exp(la)
    y = 2.0 * la
    one_minus_a2 = jnp.where(y > -0.05, -y * (1.0 + y * (0.5 + y * (1.0 / 6.0 + y * (1.0 / 24.0)))), 1.0 - jnp.exp(y))
    mm = jnp.sqrt(one_minus_a2)
    return xc, xb, r, ig, sp, a, mm


SCAN_UNROLL = 4


def _scan8(a, b, reverse):
    row = lax.broadcasted_iota(jnp.int32, (8, 1), 0)
    for k in (1, 2, 4):
        inside = row < 8 - k if reverse else row >= k
        shift = 8 - k if reverse else k
        a_n = jnp.where(inside, pltpu.roll(a, shift, 0), 1.0)
        b_n = jnp.where(inside, pltpu.roll(b, shift, 0), 0.0)
        b = a * b_n + b
        a = a * a_n
    return a, b


def lru_fwd(aux, cw, cb, wa, ba, wx, bx, lam, *, name):
    T = aux.shape[0]
    S, C = SEQ, LRU_HALF_W

    def body(x_ref, g_ref, cw_ref, cb_ref, wa_ref, ba_ref, wx_ref, bx_ref, lam_ref, o_ref, h_ref, a_s, u_s):
        xc, _, r, ig, sp, a, mm = _lru_gates(x_ref[...], cw_ref[...], cb_ref[...], wa_ref[...], ba_ref[...],
                                             wx_ref[...], bx_ref[...], lam_ref[...])
        a_s[...] = a
        u_s[...] = mm * (ig * xc)

        def group(i, h):
            for j in range(SCAN_UNROLL):
                base = pl.multiple_of((i * SCAN_UNROLL + j) * 8, 8)
                A, Bv = _scan8(a_s[pl.ds(base, 8), :], u_s[pl.ds(base, 8), :], reverse=False)
                H = A * h + Bv
                h_ref[pl.ds(base, 8), :] = H
                h = H[7:8]
            return h

        lax.fori_loop(0, S // 8 // SCAN_UNROLL, group, jnp.zeros((1, C), F32))
        gate = g_ref[...]
        o_ref[...] = (h_ref[...] * (gate * _gelu_cdf(gate))).astype(BF16)

    blk = lambda col: pl.BlockSpec((S, C), lambda c, b: (b, col + c))
    par = lambda rows: pl.BlockSpec((rows, C), lambda c, b: (0, c))
    sq = pl.BlockSpec((None, C, C), lambda c, b: (c, 0, 0))
    sd = jax.ShapeDtypeStruct
    W = LRU_HALVES * C
    return pl.pallas_call(
        body, name=name, grid=(LRU_HALVES, T // S),
        in_specs=[blk(0), blk(LRU_HALVES), par(4), par(1), sq, par(1), sq, par(1), par(1)],
        out_specs=[blk(0), blk(0)], out_shape=[sd((T, W), BF16), sd((T, W), F32)],
        scratch_shapes=[pltpu.VMEM((S, C), F32), pltpu.VMEM((S, C), F32)],
        compiler_params=_params("parallel", "parallel"),
    )(aux, aux, cw, cb, wa, ba, wx, bx, lam)


def lru_bwd(aux, h, dmixed, cw, cb, wa, ba, wx, bx, lam, *, name):
    T = aux.shape[0]
    S, C = SEQ, LRU_HALF_W

    def body(x_ref, g_ref, h_ref, do_ref, cw_ref, cb_ref, wa_ref, ba_ref, wx_ref, bx_ref, lam_ref,
             dx_ref, dgate_ref, dcw_ref, dcb_ref, dwa_ref, dba_ref, dwx_ref, dbx_ref, dlam_ref, a_s, d_s):
        x = x_ref[...]
        cw = cw_ref[...]
        lam = lam_ref[...]
        xc, xb, r, ig, sp, a, mm = _lru_gates(x, cw, cb_ref[...], wa_ref[...], ba_ref[...], wx_ref[...], bx_ref[...], lam)
        gate = g_ref[...]
        gl, dgl = _gelu_and_grad(gate)
        dout = do_ref[...]
        hv = h_ref[...]
        dgate_ref[...] = dout * hv * dgl
        a_s[...] = a
        d_s[...] = dout * gl

        last_row = lax.broadcasted_iota(jnp.int32, (8, 1), 0) == 7

        def group(i, c):
            for j in range(SCAN_UNROLL):
                base = pl.multiple_of((S // 8 - 1 - (i * SCAN_UNROLL + j)) * 8, 8)
                a8 = a_s[pl.ds(base, 8), :]
                d8 = d_s[pl.ds(base, 8), :]
                A, Bv = _scan8(a8, a8 * d8, reverse=True)
                Cv = A * c + Bv
                d_s[pl.ds(base, 8), :] = d8 + jnp.where(last_row, c, pltpu.roll(Cv, 7, 0))
                c = Cv[0:1]
            return c

        lax.fori_loop(0, S // 8 // SCAN_UNROLL, group, jnp.zeros((1, C), F32))
        row = lax.broadcasted_iota(jnp.int32, (S, 1), 0)
        dht = d_s[...]
        h_prev = jnp.where(row >= 1, pltpu.roll(hv, 1, 0), 0.0)
        da = dht * h_prev
        gx = ig * xc
        dmm = dht * gx
        dig = dht * mm * xc
        dxc = dht * mm * ig
        dla = da * a - dmm * (a * a) / mm
        dr = dla * (-LRU_C * sp)
        dsp = jnp.sum(dla * (-LRU_C * r), axis=0, keepdims=True)
        dlam = dsp * (-jax.nn.sigmoid(-lam))
        dpa = dr * r * (1.0 - r)
        dpx = dig * ig * (1.0 - ig)
        dpa_b, dpx_b = dpa.astype(BF16), dpx.astype(BF16)
        nt = (((1,), (1,)), ((), ()))
        tn = (((0,), (0,)), ((), ()))
        dxc = dxc + lax.dot_general(dpa_b, wa_ref[...], nt, preferred_element_type=F32) \
                  + lax.dot_general(dpx_b, wx_ref[...], nt, preferred_element_type=F32)
        dwa = lax.dot_general(xb, dpa_b, tn, preferred_element_type=F32)
        dwx = lax.dot_general(xb, dpx_b, tn, preferred_element_type=F32)

        def fwd(v, s):
            return jnp.where(row < S - s, pltpu.roll(v, S - s, 0), 0.0)

        def back(v, s):
            return jnp.where(row >= s, pltpu.roll(v, s, 0), 0.0)

        dx_ref[...] = cw[3:4] * dxc + cw[2:3] * fwd(dxc, 1) + cw[1:2] * fwd(dxc, 2) + cw[0:1] * fwd(dxc, 3)
        s0 = lambda v: jnp.sum(v, axis=0, keepdims=True)
        dcw = jnp.concatenate([s0(dxc * back(x, 3)), s0(dxc * back(x, 2)), s0(dxc * back(x, 1)), s0(dxc * x)], axis=0)
        parts = ((dcw_ref, dcw), (dcb_ref, s0(dxc)), (dwa_ref, dwa), (dba_ref, s0(dpa)), (dwx_ref, dwx),
                 (dbx_ref, s0(dpx)), (dlam_ref, dlam))

        @pl.when(pl.program_id(1) == 0)
        def _():
            for ref, val in parts:
                ref[...] = val

        @pl.when(pl.program_id(1) > 0)
        def _():
            for ref, val in parts:
                ref[...] += val

    blk = lambda col: pl.BlockSpec((S, C), lambda c, b: (b, col + c))
    par = lambda rows: pl.BlockSpec((rows, C), lambda c, b: (0, c))
    sq = pl.BlockSpec((None, C, C), lambda c, b: (c, 0, 0))
    sd = jax.ShapeDtypeStruct
    W = LRU_HALVES * C
    vec = sd((1, W), F32)
    return pl.pallas_call(
        body, name=name, grid=(LRU_HALVES, T // S),
        in_specs=[blk(0), blk(LRU_HALVES), blk(0), blk(3 * LRU_HALVES), par(4), par(1), sq, par(1), sq, par(1), par(1)],
        out_specs=[blk(0), blk(0), par(4), par(1), sq, par(1), sq, par(1), par(1)],
        out_shape=[sd((T, W), F32), sd((T, W), F32), sd((4, W), F32), vec, sd((LRU_HALVES, C, C), F32), vec,
                   sd((LRU_HALVES, C, C), F32), vec, vec],
        scratch_shapes=[pltpu.VMEM((S, C), F32), pltpu.VMEM((S, C), F32)],
        compiler_params=_params("parallel", "arbitrary"),
    )(aux, aux, h, dmixed, cw, cb, wa, ba, wx, bx, lam)


_NT = (((1,), (1,)), ((), ()))
_TN = (((0,), (0,)), ((), ()))


def _dot(a, b, dims=None):
    if dims is None:
        return jnp.dot(a, b, preferred_element_type=F32)
    return lax.dot_general(a, b, dims, preferred_element_type=F32)


def _hs(h):
    return slice(h * HEAD_DIM, (h + 1) * HEAD_DIM)


def cross_fwd(q, kv, *, name):
    T = q.shape[0]
    tq = 512

    def body(q_ref, kv_ref, o_ref):
        for h in range(N_HEADS):
            qh = q_ref[:, _hs(h)] * Q_SCALE
            k = kv_ref[:, _hs(h)]
            v = kv_ref[:, GROUP_W + h * HEAD_DIM:GROUP_W + (h + 1) * HEAD_DIM]
            s = _dot(qh, k, _NT)
            p = jnp.exp(s - jnp.max(s, axis=-1, keepdims=True))
            p = p / jnp.sum(p, axis=-1, keepdims=True)
            o_ref[:, _hs(h)] = _dot(p.astype(BF16), v).astype(BF16)

    per = SEQ // tq
    return pl.pallas_call(
        body, name=name, grid=(T // tq,),
        in_specs=[pl.BlockSpec((tq, GROUP_W), lambda i: (i, 0)), pl.BlockSpec((N_MEM, 2 * GROUP_W), lambda i: (i // per, 0))],
        out_specs=pl.BlockSpec((tq, GROUP_W), lambda i: (i, 0)), out_shape=jax.ShapeDtypeStruct((T, GROUP_W), BF16),
        compiler_params=_params("parallel"),
    )(q, kv)


def cross_bwd(q, kv, do, *, name):
    T = q.shape[0]
    tq = 512
    per = SEQ // tq

    def body(q_ref, kv_ref, do_ref, dq_ref, dkv_ref):
        first = pl.program_id(0) % per == 0
        for h in range(N_HEADS):
            vs = slice(GROUP_W + h * HEAD_DIM, GROUP_W + (h + 1) * HEAD_DIM)
            qh = q_ref[:, _hs(h)] * Q_SCALE
            k = kv_ref[:, _hs(h)]
            v = kv_ref[:, vs]
            doh = do_ref[:, _hs(h)].astype(BF16)
            s = _dot(qh, k, _NT)
            p = jnp.exp(s - jnp.max(s, axis=-1, keepdims=True))
            p = p / jnp.sum(p, axis=-1, keepdims=True)
            dp = _dot(doh, v, _NT)
            ds = (p * (dp - jnp.sum(p * dp, axis=-1, keepdims=True))).astype(BF16)
            dq_ref[:, _hs(h)] = (_dot(ds, k) * Q_SCALE).astype(BF16)
            dk = _dot(ds, qh, _TN)
            dv = _dot(p.astype(BF16), doh, _TN)

            @pl.when(first)
            def _():
                dkv_ref[:, _hs(h)] = dk
                dkv_ref[:, vs] = dv

            @pl.when(jnp.logical_not(first))
            def _():
                dkv_ref[:, _hs(h)] += dk
                dkv_ref[:, vs] += dv

    qb = pl.BlockSpec((tq, GROUP_W), lambda i: (i, 0))
    kvb = pl.BlockSpec((N_MEM, 2 * GROUP_W), lambda i: (i // per, 0))
    sd = jax.ShapeDtypeStruct
    return pl.pallas_call(
        body, name=name, grid=(T // tq,),
        in_specs=[qb, kvb, qb], out_specs=[qb, kvb],
        out_shape=[sd((T, GROUP_W), BF16), sd(kv.shape, F32)],
        compiler_params=_params("arbitrary"),
    )(q, kv, do)


NB = SEQ // BLOCK
NEG = -1e30
HEADS = tuple(range(N_HEADS))


def _blk(i):
    return pl.ds(pl.multiple_of(i * BLOCK, BLOCK), BLOCK)


def _qkv_specs(first_col):
    return [pl.BlockSpec((SEQ, GROUP_W), lambda b, c=first_col + j: (b, c)) for j in range(3)]


LANES = 128
CUM_BLK = 256


def col_to_row(c):
    b = c.shape[0] // SEQ
    return c.reshape(b, SEQ, LANES)[:, :, :8].transpose(0, 2, 1).reshape(b * 8, SEQ)


def row_to_col(r):
    b = r.shape[0] // 8
    c = r.reshape(b, 8, SEQ).transpose(0, 2, 1)
    return jnp.pad(c, ((0, 0), (0, 0), (0, LANES - 8))).reshape(b * SEQ, LANES)


def fox_prep(aux, bf, *, name):
    T = aux.shape[0]

    def body(f_ref, b_ref, o_ref):
        row = lax.broadcasted_iota(jnp.int32, (CUM_BLK, CUM_BLK), 0)
        col = lax.broadcasted_iota(jnp.int32, (CUM_BLK, CUM_BLK), 1)
        upto = (col <= row).astype(BF16)
        carry = jnp.zeros((1, LANES), F32)
        for n in range(SEQ // CUM_BLK):
            rows = slice(n * CUM_BLK, (n + 1) * CUM_BLK)
            logf = -_softplus(-(f_ref[rows, :] + b_ref[...]))
            hi = logf.astype(BF16)
            lo = (logf - hi.astype(F32)).astype(BF16)
            cum = _dot(upto, hi) + _dot(upto, lo) + carry
            o_ref[rows, :] = cum
            carry = cum[CUM_BLK - 1:CUM_BLK]

    return pl.pallas_call(
        body, name=name, grid=(T // SEQ,),
        in_specs=[pl.BlockSpec((SEQ, LANES), lambda b: (b, 4)), pl.BlockSpec((1, LANES), lambda b: (0, 0))],
        out_specs=pl.BlockSpec((SEQ, LANES), lambda b: (b, 0)), out_shape=jax.ShapeDtypeStruct((T, LANES), F32),
        compiler_params=_params("parallel"),
    )(aux, bf)


def fox_prep_bwd(aux, bf, dcum, *, name):
    T = aux.shape[0]

    def body(f_ref, b_ref, d_ref, df_ref, db_ref):
        row = lax.broadcasted_iota(jnp.int32, (CUM_BLK, CUM_BLK), 0)
        col = lax.broadcasted_iota(jnp.int32, (CUM_BLK, CUM_BLK), 1)
        onward = (col >= row).astype(BF16)
        carry = jnp.zeros((1, LANES), F32)
        tot = jnp.zeros((1, LANES), F32)
        for n in range(SEQ // CUM_BLK - 1, -1, -1):
            rows = slice(n * CUM_BLK, (n + 1) * CUM_BLK)
            d = d_ref[rows, :]
            hi = d.astype(BF16)
            lo = (d - hi.astype(F32)).astype(BF16)
            dlogf = _dot(onward, hi) + _dot(onward, lo) + carry
            carry = dlogf[0:1]
            df = dlogf * jax.nn.sigmoid(-(f_ref[rows, :] + b_ref[...]))
            df_ref[rows, :] = df
            tot = tot + jnp.sum(df, axis=0, keepdims=True)

        @pl.when(pl.program_id(0) == 0)
        def _():
            db_ref[...] = tot

        @pl.when(pl.program_id(0) > 0)
        def _():
            db_ref[...] += tot

    blk = pl.BlockSpec((SEQ, LANES), lambda b: (b, 0))
    vec = pl.BlockSpec((1, LANES), lambda b: (0, 0))
    sd = jax.ShapeDtypeStruct
    return pl.pallas_call(
        body, name=name, grid=(T // SEQ,),
        in_specs=[pl.BlockSpec((SEQ, LANES), lambda b: (b, 4)), vec, blk],
        out_specs=[blk, vec], out_shape=[sd((T, LANES), F32), sd((1, LANES), F32)],
        compiler_params=_params("arbitrary"),
    )(aux, bf, dcum)


CHUNK = 256
WIDE = N_HEADS * CHUNK
NCH = SEQ // CHUNK


def _seg(h):
    return slice(h * CHUNK, (h + 1) * CHUNK)


def _chunk_rows(c):
    return pl.ds(pl.multiple_of(c * CHUNK, CHUNK), CHUNK)


def _wide_consts():
    r = lax.broadcasted_iota(jnp.int32, (WIDE, GROUP_W), 0)
    f = lax.broadcasted_iota(jnp.int32, (WIDE, GROUP_W), 1)
    bd = (r // CHUNK) == (f // HEAD_DIM)
    row = lax.broadcasted_iota(jnp.int32, (BLOCK, WIDE), 0)
    key = lax.broadcasted_iota(jnp.int32, (BLOCK, WIDE), 1) % CHUNK
    return bd, row, key


def _block_diag(x, bd):
    return jnp.where(bd, jnp.concatenate([x] * N_HEADS, axis=0), jnp.zeros((), x.dtype))


def _fold_heads(w, bd):
    w = jnp.where(bd, w, 0.0)
    return (w[0:CHUNK] + w[CHUNK:2 * CHUNK]) + (w[2 * CHUNK:3 * CHUNK] + w[3 * CHUNK:])


def _widen(cols):
    return jnp.concatenate([jnp.broadcast_to(c, (BLOCK, CHUNK)) for c in cols], axis=1)


def _head_rowsums(w):
    return [jnp.sum(w[:, _seg(h)], axis=1, keepdims=True) for h in HEADS]


def _tri_wide(x, tri):
    hi = x.astype(BF16)
    lo = (x - hi.astype(F32)).astype(BF16)
    y = _dot(jnp.concatenate([hi[:, _seg(h)] for h in HEADS] + [lo[:, _seg(h)] for h in HEADS], axis=0), tri)
    return jnp.concatenate([y[h * BLOCK:(h + 1) * BLOCK] + y[(N_HEADS + h) * BLOCK:(N_HEADS + h + 1) * BLOCK] for h in HEADS], axis=1)


def _feature_widen(cols):
    return jnp.concatenate([jnp.broadcast_to(c, (BLOCK, HEAD_DIM)) for c in cols], axis=1)


def _loop_by_two(n, index, body, carry):
    odd = n % 2
    carry = lax.fori_loop(0, odd, lambda _, cr: body(index(0), cr), carry)
    return lax.fori_loop(0, n // 2, lambda t, cr: body(index(odd + 2 * t + 1), body(index(odd + 2 * t), cr)), carry)


def _sbw_scores(q, kbd, later):
    z = _dot(q, kbd, _NT)
    lk = -_softplus(z)
    return z + lk, lk, _tri_wide(lk, later)


def _sbw_tile(q, kbd, mask, later, csum):
    z = _dot(q, kbd, _NT)
    lk = -_softplus(z)
    if mask is not None:
        lk = jnp.where(mask, lk, 0.0)
    e = z + lk
    att = jnp.exp(e + _tri_wide(lk, later) + csum)
    if mask is not None:
        att = jnp.where(mask, att, 0.0)
    return att, e, lk


def sbw_fwd(qkv, *, name):
    T = qkv.shape[0]

    def body(q_ref, k_ref, v_ref, o_ref):
        bd, row, key = _wide_consts()
        r2 = lax.broadcasted_iota(jnp.int32, (CHUNK, CHUNK), 0)
        c2 = lax.broadcasted_iota(jnp.int32, (CHUNK, CHUNK), 1)
        later = (r2 > c2).astype(BF16)

        def qblock(i, _):
            q = q_ref[_blk(i), :] * Q_SCALE
            cd = i // 2
            strict = key < row + BLOCK * (i % 2)

            def tile(c, mask, carry):
                acc, csum = carry
                att, _, lk = _sbw_tile(q, _block_diag(k_ref[_chunk_rows(c), :], bd), mask, later, csum)
                acc = acc + _dot(att.astype(BF16), _block_diag(v_ref[_chunk_rows(c), :], bd))
                return acc, csum + _widen(_head_rowsums(lk))

            def two_tiles(c1, carry):
                acc, csum = carry
                e1, lk1, t1 = _sbw_scores(q, _block_diag(k_ref[_chunk_rows(c1), :], bd), later)
                e2, lk2, t2 = _sbw_scores(q, _block_diag(k_ref[_chunk_rows(c1 - 1), :], bd), later)
                att1 = jnp.exp(e1 + t1 + csum)
                csum = csum + _widen(_head_rowsums(lk1))
                att2 = jnp.exp(e2 + t2 + csum)
                csum = csum + _widen(_head_rowsums(lk2))
                acc = acc + _dot(att1.astype(BF16), _block_diag(v_ref[_chunk_rows(c1), :], bd))
                acc = acc + _dot(att2.astype(BF16), _block_diag(v_ref[_chunk_rows(c1 - 1), :], bd))
                return acc, csum

            carry = tile(cd, strict, (jnp.zeros((BLOCK, GROUP_W), F32), jnp.zeros((BLOCK, WIDE), F32)))
            odd = cd % 2
            carry = lax.fori_loop(0, odd, lambda n, cr: tile(cd - 1, None, cr), carry)
            acc, _ = lax.fori_loop(0, cd // 2, lambda n, cr: two_tiles(cd - 1 - odd - 2 * n, cr), carry)
            o_ref[_blk(i), :] = acc.astype(BF16)
            return 0

        lax.fori_loop(0, NB, qblock, 0)

    return pl.pallas_call(
        body, name=name, grid=(T // SEQ,), in_specs=_qkv_specs(0),
        out_specs=pl.BlockSpec((SEQ, GROUP_W), lambda b: (b, 0)), out_shape=jax.ShapeDtypeStruct((T, GROUP_W), BF16),
        compiler_params=_params("parallel"),
    )(qkv, qkv, qkv)


def sbw_bwd(qkv, dmixed, *, name):
    T = qkv.shape[0]

    def body(q_ref, k_ref, v_ref, do_ref, dq_ref, dk_ref, dv_ref, att_s, sg_s):
        bd, row, key = _wide_consts()
        r2 = lax.broadcasted_iota(jnp.int32, (CHUNK, CHUNK), 0)
        c2 = lax.broadcasted_iota(jnp.int32, (CHUNK, CHUNK), 1)
        later = (r2 > c2).astype(BF16)
        earlier = (r2 < c2).astype(BF16)
        dk_ref[...] = jnp.zeros_like(dk_ref)
        dv_ref[...] = jnp.zeros_like(dv_ref)

        def qblock(i, _):
            q = q_ref[_blk(i), :] * Q_SCALE
            do = do_ref[_blk(i), :].astype(BF16)
            cd = i // 2
            strict = key < row + BLOCK * (i % 2)

            def recompute(c, mask, csum):
                att, e, lk = _sbw_tile(q, _block_diag(k_ref[_chunk_rows(c), :], bd), mask, later, csum)
                sg = jnp.exp(e)
                att_s[c] = att
                sg_s[c] = sg if mask is None else jnp.where(mask, sg, 0.0)
                return csum + _widen(_head_rowsums(lk))

            def recompute_two(c1, csum):
                e1, lk1, t1 = _sbw_scores(q, _block_diag(k_ref[_chunk_rows(c1), :], bd), later)
                e2, lk2, t2 = _sbw_scores(q, _block_diag(k_ref[_chunk_rows(c1 - 1), :], bd), later)
                sg_s[c1] = jnp.exp(e1)
                sg_s[c1 - 1] = jnp.exp(e2)
                att_s[c1] = jnp.exp(e1 + t1 + csum)
                csum = csum + _widen(_head_rowsums(lk1))
                att_s[c1 - 1] = jnp.exp(e2 + t2 + csum)
                return csum + _widen(_head_rowsums(lk2))

            csum = recompute(cd, strict, jnp.zeros((BLOCK, WIDE), F32))
            odd = cd % 2
            csum = lax.fori_loop(0, odd, lambda n, cs: recompute(cd - 1, None, cs), csum)
            lax.fori_loop(0, cd // 2, lambda n, cs: recompute_two(cd - 1 - odd - 2 * n, cs), csum)

            def tile(c, carry):
                dq, pre = carry
                kbd = _block_diag(k_ref[_chunk_rows(c), :], bd)
                vbd = _block_diag(v_ref[_chunk_rows(c), :], bd)
                att = att_s[c]
                ds = _dot(do, vbd, _NT) * att
                dlk = ds + _tri_wide(ds, earlier) + pre
                dz = (ds - dlk * sg_s[c]).astype(BF16)
                dk_ref[_chunk_rows(c), :] += _fold_heads(_dot(dz, q, _TN), bd)
                dv_ref[_chunk_rows(c), :] += _fold_heads(_dot(att.astype(BF16), do, _TN), bd)
                return dq + _dot(dz, kbd), pre + _widen(_head_rowsums(ds))

            def two_tiles(c1, carry):
                dq, pre = carry
                c2 = c1 + 1
                kbd1, kbd2 = _block_diag(k_ref[_chunk_rows(c1), :], bd), _block_diag(k_ref[_chunk_rows(c2), :], bd)
                att1, att2 = att_s[c1], att_s[c2]
                ds1 = _dot(do, _block_diag(v_ref[_chunk_rows(c1), :], bd), _NT) * att1
                ds2 = _dot(do, _block_diag(v_ref[_chunk_rows(c2), :], bd), _NT) * att2
                tri1, tri2 = _tri_wide(ds1, earlier), _tri_wide(ds2, earlier)
                dv_ref[_chunk_rows(c1), :] += _fold_heads(_dot(att1.astype(BF16), do, _TN), bd)
                dv_ref[_chunk_rows(c2), :] += _fold_heads(_dot(att2.astype(BF16), do, _TN), bd)
                dz1 = (ds1 - (ds1 + tri1 + pre) * sg_s[c1]).astype(BF16)
                pre = pre + _widen(_head_rowsums(ds1))
                dz2 = (ds2 - (ds2 + tri2 + pre) * sg_s[c2]).astype(BF16)
                pre = pre + _widen(_head_rowsums(ds2))
                dk_ref[_chunk_rows(c1), :] += _fold_heads(_dot(dz1, q, _TN), bd)
                dk_ref[_chunk_rows(c2), :] += _fold_heads(_dot(dz2, q, _TN), bd)
                return dq + _dot(dz1, kbd1) + _dot(dz2, kbd2), pre

            n_tiles = cd + 1
            odd = n_tiles % 2
            carry = (jnp.zeros((BLOCK, GROUP_W), F32), jnp.zeros((BLOCK, WIDE), F32))
            carry = lax.fori_loop(0, odd, lambda n, cr: tile(0, cr), carry)
            dq, _ = lax.fori_loop(0, n_tiles // 2, lambda n, cr: two_tiles(odd + 2 * n, cr), carry)
            dq_ref[_blk(i), :] = dq * Q_SCALE
            return 0

        lax.fori_loop(0, NB, qblock, 0)

    out = pl.BlockSpec((SEQ, GROUP_W), lambda b: (b, 0))
    sd = jax.ShapeDtypeStruct((T, GROUP_W), F32)
    return pl.pallas_call(
        body, name=name, grid=(T // SEQ,), in_specs=_qkv_specs(0) + [out],
        out_specs=[out] * 3, out_shape=[sd] * 3,
        scratch_shapes=[pltpu.VMEM((NCH, BLOCK, WIDE), F32), pltpu.VMEM((NCH, BLOCK, WIDE), F32)],
        compiler_params=_params("parallel"),
    )(qkv, qkv, qkv, dmixed)


def _foxw_logits(q, kbd, cq, cr_ref, c, mask):
    ck = jnp.concatenate([cr_ref[h:h + 1, _chunk_rows(c)] for h in HEADS], axis=1)
    z = _dot(q, kbd, _NT) + cq - ck
    return z if mask is None else jnp.where(mask, z, NEG)


def foxw_fwd(qkv, cumc, cumr, *, name):
    T = qkv.shape[0]

    def body(q_ref, k_ref, v_ref, cc_ref, cr_ref, o_ref, o32_ref, lse_ref, z_s):
        bd, row, key = _wide_consts()
        lse_ref[...] = jnp.zeros_like(lse_ref)

        def qblock(i, _):
            q = q_ref[_blk(i), :] * Q_SCALE
            cq = _widen([cc_ref[_blk(i), h:h + 1] for h in HEADS])
            cd = i // 2
            causal = key <= row + BLOCK * (i % 2)

            def logits(c, mask, ms):
                z = _foxw_logits(q, _block_diag(k_ref[_chunk_rows(c), :], bd), cq, cr_ref, c, mask)
                z_s[c] = z
                return tuple(jnp.maximum(ms[h], jnp.max(z[:, _seg(h)], axis=1, keepdims=True)) for h in HEADS)

            ms = logits(cd, causal, (jnp.full((BLOCK, 1), NEG, F32),) * N_HEADS)
            ms = _loop_by_two(cd, lambda n: n, lambda c, m: logits(c, None, m), ms)
            m_wide = _widen(ms)

            def values(c, carry):
                acc, l = carry
                p = jnp.exp(z_s[c] - m_wide)
                return acc + _dot(p.astype(BF16), _block_diag(v_ref[_chunk_rows(c), :], bd)), l + _widen(_head_rowsums(p))

            acc, l = _loop_by_two(cd + 1, lambda n: n, values, (jnp.zeros((BLOCK, GROUP_W), F32), jnp.zeros((BLOCK, WIDE), F32)))
            ls = [l[:, h * CHUNK:h * CHUNK + 1] for h in HEADS]
            o = acc / _feature_widen(ls)
            o_ref[_blk(i), :] = o.astype(BF16)
            o32_ref[_blk(i), :] = o
            for h in HEADS:
                lse_ref[_blk(i), h:h + 1] = ms[h] + jnp.log(ls[h])
            return 0

        lax.fori_loop(0, NB, qblock, 0)

    out = pl.BlockSpec((SEQ, GROUP_W), lambda b: (b, 0))
    colb = pl.BlockSpec((SEQ, LANES), lambda b: (b, 0))
    sd = jax.ShapeDtypeStruct
    return pl.pallas_call(
        body, name=name, grid=(T // SEQ,),
        in_specs=_qkv_specs(3) + [colb, pl.BlockSpec((8, SEQ), lambda b: (b, 0))],
        out_specs=[out, out, colb], out_shape=[sd((T, GROUP_W), BF16), sd((T, GROUP_W), F32), sd((T, LANES), F32)],
        scratch_shapes=[pltpu.VMEM((NCH, BLOCK, WIDE), F32)],
        compiler_params=_params("parallel"),
    )(qkv, qkv, qkv, cumc, cumr)


def foxw_bwd(qkv, cumc, cumr, lse, o32, dmixed, *, name):
    T = qkv.shape[0]

    def body(q_ref, k_ref, v_ref, cc_ref, cr_ref, lse_ref, o_ref, do_ref, dq_ref, dk_ref, dv_ref, dcc_ref, dcr_ref):
        bd, row, key = _wide_consts()
        dk_ref[...] = jnp.zeros_like(dk_ref)
        dv_ref[...] = jnp.zeros_like(dv_ref)
        dcc_ref[...] = jnp.zeros_like(dcc_ref)
        dcr_ref[...] = jnp.zeros_like(dcr_ref)

        def qblock(i, _):
            q = q_ref[_blk(i), :] * Q_SCALE
            do = do_ref[_blk(i), :].astype(BF16)
            prod = do.astype(F32) * o_ref[_blk(i), :]
            delta = _widen([jnp.sum(prod[:, _hs(h)], axis=1, keepdims=True) for h in HEADS])
            cq = _widen([cc_ref[_blk(i), h:h + 1] for h in HEADS])
            lse_w = _widen([lse_ref[_blk(i), h:h + 1] for h in HEADS])
            cd = i // 2
            causal = key <= row + BLOCK * (i % 2)

            def tile(c, mask, carry):
                dq, dcq = carry
                kbd = _block_diag(k_ref[_chunk_rows(c), :], bd)
                vbd = _block_diag(v_ref[_chunk_rows(c), :], bd)
                p = jnp.exp(_foxw_logits(q, kbd, cq, cr_ref, c, mask) - lse_w)
                ds = p * (_dot(do, vbd, _NT) - delta)
                dsb = ds.astype(BF16)
                dk_ref[_chunk_rows(c), :] += _fold_heads(_dot(dsb, q, _TN), bd)
                dv_ref[_chunk_rows(c), :] += _fold_heads(_dot(p.astype(BF16), do, _TN), bd)
                for h in HEADS:
                    dcr_ref[h:h + 1, _chunk_rows(c)] -= jnp.sum(ds[:, _seg(h)], axis=0, keepdims=True)
                return dq + _dot(dsb, kbd), dcq + _widen(_head_rowsums(ds))

            def two_tiles(c1, carry):
                dq, dcq = carry
                cs = (c1, c1 + 1)
                kbds = [_block_diag(k_ref[_chunk_rows(c), :], bd) for c in cs]
                vbds = [_block_diag(v_ref[_chunk_rows(c), :], bd) for c in cs]
                ps = [jnp.exp(_foxw_logits(q, kbds[j], cq, cr_ref, cs[j], None) - lse_w) for j in range(2)]
                dss = [ps[j] * (_dot(do, vbds[j], _NT) - delta) for j in range(2)]
                dsbs = [d.astype(BF16) for d in dss]
                for j, c in enumerate(cs):
                    dk_ref[_chunk_rows(c), :] += _fold_heads(_dot(dsbs[j], q, _TN), bd)
                    dv_ref[_chunk_rows(c), :] += _fold_heads(_dot(ps[j].astype(BF16), do, _TN), bd)
                    for h in HEADS:
                        dcr_ref[h:h + 1, _chunk_rows(c)] -= jnp.sum(dss[j][:, _seg(h)], axis=0, keepdims=True)
                dq = dq + _dot(dsbs[0], kbds[0]) + _dot(dsbs[1], kbds[1])
                return dq, dcq + _widen(_head_rowsums(dss[0])) + _widen(_head_rowsums(dss[1]))

            carry = tile(cd, causal, (jnp.zeros((BLOCK, GROUP_W), F32), jnp.zeros((BLOCK, WIDE), F32)))
            odd = cd % 2
            carry = lax.fori_loop(0, odd, lambda n, cr: tile(0, None, cr), carry)
            dq, dcq = lax.fori_loop(0, cd // 2, lambda n, cr: two_tiles(odd + 2 * n, cr), carry)
            dq_ref[_blk(i), :] = dq * Q_SCALE
            for h in HEADS:
                dcc_ref[_blk(i), h:h + 1] = dcq[:, h * CHUNK:h * CHUNK + 1]
            return 0

        lax.fori_loop(0, NB, qblock, 0)

    out = pl.BlockSpec((SEQ, GROUP_W), lambda b: (b, 0))
    colb = pl.BlockSpec((SEQ, LANES), lambda b: (b, 0))
    rowb = pl.BlockSpec((8, SEQ), lambda b: (b, 0))
    sd = jax.ShapeDtypeStruct
    big = sd((T, GROUP_W), F32)
    return pl.pallas_call(
        body, name=name, grid=(T // SEQ,),
        in_specs=_qkv_specs(3) + [colb, rowb, colb, out, pl.BlockSpec((SEQ, GROUP_W), lambda b: (b, 1))],
        out_specs=[out, out, out, colb, rowb],
        out_shape=[big, big, big, sd((T, LANES), F32), sd((T // SEQ * 8, SEQ), F32)],
        compiler_params=_params("parallel"),
    )(qkv, qkv, qkv, cumc, cumr, lse, o32, dmixed)


BAND = 2 * BLOCK


def _t5_bucket_np(dist):
    n = np.maximum(dist, 0)
    max_exact = NUM_BUCKETS // 2
    nf = np.maximum(n, 1).astype(np.float32)
    large = max_exact + (np.log(nf / np.float32(max_exact)) / np.float32(math.log(MAX_DISTANCE / max_exact))
                         * np.float32(NUM_BUCKETS - max_exact)).astype(np.int32)
    large = np.minimum(large, NUM_BUCKETS - 1)
    return np.where(n < max_exact, n, large).astype(np.int32)


def _band_buckets():
    qi = np.arange(BLOCK)[:, None]
    ki = np.arange(BAND)[None, :]
    delta = np.clip(qi - ki + BLOCK, 0, BLOCK)
    return np.stack([_t5_bucket_np(delta * d) for d in DILATIONS])


def relbias_expand(rel, *, name):
    buckets = jnp.asarray(_band_buckets())
    n_pat = len(DILATIONS)

    def body(rel_ref, bk_ref, o_ref):
        for p in range(n_pat):
            bk = bk_ref[p]
            for h in range(N_HEADS):
                acc = jnp.zeros((BLOCK, BAND), F32)
                for b in range(NUM_BUCKETS):
                    acc = jnp.where(bk == b, rel_ref[b, h], acc)
                o_ref[p * N_HEADS + h] = acc

    return pl.pallas_call(
        body, name=name,
        in_specs=[pl.BlockSpec(memory_space=pltpu.SMEM), pl.BlockSpec(memory_space=pltpu.VMEM)],
        out_specs=pl.BlockSpec(memory_space=pltpu.VMEM),
        out_shape=jax.ShapeDtypeStruct((n_pat * N_HEADS, BLOCK, BAND), F32),
        compiler_params=_params(),
    )(rel, buckets)


def relbias_reduce(ds_all, *, name):
    buckets = jnp.asarray(_band_buckets())
    n_pat = len(DILATIONS)

    def body(ds_ref, bk_ref, o_ref):
        for b in range(NUM_BUCKETS):
            for h in range(N_HEADS):
                tot = jnp.float32(0.0)
                for p in range(n_pat):
                    tot = tot + jnp.sum(jnp.where(bk_ref[p] == b, ds_ref[p * N_HEADS + h], 0.0))
                o_ref[b, h] = tot

    return pl.pallas_call(
        body, name=name,
        in_specs=[pl.BlockSpec(memory_space=pltpu.VMEM), pl.BlockSpec(memory_space=pltpu.VMEM)],
        out_specs=pl.BlockSpec(memory_space=pltpu.SMEM),
        out_shape=jax.ShapeDtypeStruct((NUM_BUCKETS, N_HEADS), F32),
        compiler_params=_params(),
    )(ds_all, buckets)


def _band_valid_wide(first, row, key):
    inside = jnp.logical_and(key >= row, key <= row + BLOCK)
    return jnp.logical_and(inside, jnp.logical_or(jnp.logical_not(first), key >= BLOCK))


QKV_BLOCKS = 9


def _band_in_specs(d, pattern, has_prev):
    rows = BLOCK * d
    cur = lambda c: pl.BlockSpec((rows, GROUP_W), lambda tb, r: (tb, c))
    prev = lambda c: pl.BlockSpec((rows, GROUP_W), lambda tb, r: (jnp.maximum(tb - 1, 0), c))
    bias = pl.BlockSpec((N_HEADS, BLOCK, BAND), lambda tb, r: (pattern, 0, 0))
    return [cur(6), cur(7), cur(8)] + ([prev(7), prev(8)] if has_prev else []) + [bias]


def _classes_per_step(d):
    return 2 if d > 1 else 1


def _step_classes(d):
    n = _classes_per_step(d)
    return [pl.program_id(1) * n + j for j in range(n)]


def _class_rows(d, cls):
    return pl.ds(cls, BLOCK, stride=d) if d > 1 else pl.ds(0, BLOCK)


def _halves_scratch(rows, n):
    return [pltpu.VMEM((2, rows, LANES), F32)] * n


def _stage(refs, scratch):
    @pl.when(pl.program_id(1) == 0)
    def _():
        for src, dst in zip(refs, scratch):
            dst[0] = src[:, :LANES].astype(F32)
            dst[1] = src[:, LANES:].astype(F32)


def _take_class(s, d, cls):
    rows = _class_rows(d, cls)
    return jnp.concatenate([s.at[0][rows, :], s.at[1][rows, :]], axis=1)


def _put_class(s, d, cls, x):
    rows = _class_rows(d, cls)
    s.at[0][rows, :] = x[:, :LANES]
    s.at[1][rows, :] = x[:, LANES:]


def _flush(scratch, refs, d):
    @pl.when(pl.program_id(1) == d // _classes_per_step(d) - 1)
    def _():
        for s, o in zip(scratch, refs):
            o[...] = jnp.concatenate([s[0], s[1]], axis=1)


def _band_operands(scratch, d, cls, has_prev):
    take = lambda s: _take_class(s, d, cls).astype(BF16)
    q = (_take_class(scratch[0], d, cls) * Q_SCALE).astype(BF16)
    if has_prev:
        k = jnp.concatenate([take(scratch[3]), take(scratch[1])], axis=0)
        v = jnp.concatenate([take(scratch[4]), take(scratch[2])], axis=0)
    else:
        k = jnp.concatenate([jnp.zeros((BLOCK, GROUP_W), BF16), take(scratch[1])], axis=0)
        v = jnp.concatenate([jnp.zeros((BLOCK, GROUP_W), BF16), take(scratch[2])], axis=0)
    return q, k, v


def _lane_columns(cols):
    lane = lax.broadcasted_iota(jnp.int32, (BLOCK, LANES), 1)
    out = jnp.zeros((BLOCK, LANES), F32)
    for h, c in enumerate(cols):
        out = jnp.where(lane == h, c, out)
    return out


def band_fwd(qkv, bias, pattern, *, name):
    T = qkv.shape[0]
    d = DILATIONS[pattern]
    rows_per_block = BLOCK * d
    seq_blocks = SEQ // rows_per_block
    has_prev = seq_blocks > 1
    n_in = 5 if has_prev else 3

    def body(*refs):
        ins, b_ref, o_ref, lse_ref = refs[:n_in], refs[n_in], refs[n_in + 1], refs[n_in + 2]
        staged, o_s = refs[n_in + 3:2 * n_in + 3], refs[2 * n_in + 3]
        bd, row, key = _wide_consts()
        valid = _band_valid_wide(pl.program_id(0) % seq_blocks == 0, row, key)
        _stage(ins, staged)
        bias_w = jnp.concatenate([b_ref[h] for h in HEADS], axis=1)
        for cls in _step_classes(d):
            q, k, v = _band_operands(staged, d, cls, has_prev)
            kbd, vbd = _block_diag(k, bd), _block_diag(v, bd)
            sc = jnp.where(valid, _dot(q, kbd, _NT) + bias_w, NEG)
            ms = [jnp.max(sc[:, _seg(h)], axis=1, keepdims=True) for h in HEADS]
            p = jnp.exp(sc - _widen(ms))
            ls = _head_rowsums(p)
            _put_class(o_s, d, cls, _dot(p.astype(BF16), vbd) / _feature_widen(ls))
            lse_ref[_class_rows(d, cls), :] = _lane_columns([ms[h] + jnp.log(ls[h]) for h in HEADS])
        _flush([o_s], [o_ref], d)

    sd = jax.ShapeDtypeStruct
    return pl.pallas_call(
        body, name=name, grid=(T // rows_per_block, d // _classes_per_step(d)), in_specs=_band_in_specs(d, pattern, has_prev),
        out_specs=[pl.BlockSpec((rows_per_block, GROUP_W), lambda tb, r: (tb, 0)),
                   pl.BlockSpec((rows_per_block, LANES), lambda tb, r: (tb, 0))],
        out_shape=[sd((T, GROUP_W), F32), sd((T, LANES), F32)],
        scratch_shapes=_halves_scratch(rows_per_block, n_in + 1),
        compiler_params=_params("parallel", "arbitrary"),
    )(*([qkv] * n_in), bias)


def band_bwd(qkv, bias, lse, do, dlse, pattern, *, name):
    T = qkv.shape[0]
    d = DILATIONS[pattern]
    rows_per_block = BLOCK * d
    seq_blocks = SEQ // rows_per_block
    has_prev = seq_blocks > 1
    n_in = 5 if has_prev else 3
    n_out = 5 if has_prev else 3

    def body(*refs):
        ins, b_ref, lse_ref, do_ref, dlse_ref = refs[:n_in], refs[n_in], refs[n_in + 1], refs[n_in + 2], refs[n_in + 3]
        outs = refs[n_in + 4:n_in + 4 + n_out]
        ds_ref = refs[n_in + 4 + n_out]
        scratch = refs[n_in + 5 + n_out:]
        staged, do_s, out_s = scratch[:n_in], scratch[n_in], scratch[n_in + 1:]
        first_step = jnp.logical_and(pl.program_id(0) == 0, pl.program_id(1) == 0)
        bd, row, key = _wide_consts()
        valid = _band_valid_wide(pl.program_id(0) % seq_blocks == 0, row, key)
        _stage(list(ins) + [do_ref], list(staged) + [do_s])
        bias_w = jnp.concatenate([b_ref[h] for h in HEADS], axis=1)
        ds = None
        for cls in _step_classes(d):
            q, k, v = _band_operands(staged, d, cls, has_prev)
            kbd, vbd = _block_diag(k, bd), _block_diag(v, bd)
            rows = _class_rows(d, cls)
            do = _take_class(do_s, d, cls).astype(BF16)
            lse_t, dlse_t = lse_ref[rows, :], dlse_ref[rows, :]
            lse_w = _widen([lse_t[:, h:h + 1] for h in HEADS])
            dlse_w = _widen([dlse_t[:, h:h + 1] for h in HEADS])
            p = jnp.where(valid, jnp.exp(_dot(q, kbd, _NT) + bias_w - lse_w), 0.0)
            dp = _dot(do, vbd, _NT)
            ds_c = p * (dp - _widen(_head_rowsums(p * dp)) + dlse_w)
            dsb, pb = ds_c.astype(BF16), p.astype(BF16)
            _put_class(out_s[0], d, cls, _dot(dsb, kbd) * Q_SCALE)
            dk = _fold_heads(_dot(dsb, q, _TN), bd)
            dv = _fold_heads(_dot(pb, do, _TN), bd)
            _put_class(out_s[1], d, cls, dk[BLOCK:])
            _put_class(out_s[2], d, cls, dv[BLOCK:])
            if has_prev:
                _put_class(out_s[3], d, cls, dk[:BLOCK])
                _put_class(out_s[4], d, cls, dv[:BLOCK])
            ds = ds_c if ds is None else ds + ds_c
        _flush(out_s, outs, d)

        @pl.when(first_step)
        def _():
            for h in HEADS:
                ds_ref[h] = ds[:, _seg(h)]

        @pl.when(jnp.logical_not(first_step))
        def _():
            for h in HEADS:
                ds_ref[h] += ds[:, _seg(h)]

    big = pl.BlockSpec((rows_per_block, GROUP_W), lambda tb, r: (tb, 0))
    colb = pl.BlockSpec((rows_per_block, LANES), lambda tb, r: (tb, 0))
    sd = jax.ShapeDtypeStruct
    return pl.pallas_call(
        body, name=name, grid=(T // rows_per_block, d // _classes_per_step(d)),
        in_specs=_band_in_specs(d, pattern, has_prev) + [colb, big, colb],
        out_specs=[big] * n_out + [pl.BlockSpec((N_HEADS, BLOCK, BAND), lambda tb, r: (0, 0, 0))],
        out_shape=[sd((T, GROUP_W), F32)] * n_out + [sd((N_HEADS, BLOCK, BAND), F32)],
        scratch_shapes=_halves_scratch(rows_per_block, n_in + 1 + n_out),
        compiler_params=_params("arbitrary", "arbitrary"),
    )(*([qkv] * n_in), bias, lse, do, dlse)


def shift_add(cur, prev, d, *, name):
    rows = BLOCK * d
    nb = cur.shape[0] // rows

    def body(c_ref, p_ref, o_ref):
        keep = (pl.program_id(0) < nb - 1).astype(F32)
        o_ref[...] = c_ref[...] + keep * p_ref[...]

    blk = pl.BlockSpec((rows, GROUP_W), lambda tb: (tb, 0))
    nxt = pl.BlockSpec((rows, GROUP_W), lambda tb: (jnp.minimum(tb + 1, nb - 1), 0))
    return pl.pallas_call(
        body, name=name, grid=(nb,), in_specs=[blk, nxt], out_specs=blk,
        out_shape=jax.ShapeDtypeStruct(cur.shape, F32), compiler_params=_params("parallel"),
    )(cur, prev)


def _pattern_weights(lse_refs, h):
    ls = [r[:, h:h + 1] for r in lse_refs]
    mx = functools.reduce(jnp.maximum, ls)
    es = [jnp.exp(l - mx) for l in ls]
    tot = functools.reduce(lambda a, b: a + b, es)
    return [e / tot for e in es]


def dil_combine_fwd(outs, *, name):
    T = outs[0][0].shape[0]
    n = len(outs)
    tm = 512

    def body(*refs):
        o_refs, l_refs, out_ref = refs[:n], refs[n:2 * n], refs[2 * n]
        for h in range(N_HEADS):
            w = _pattern_weights(l_refs, h)
            acc = w[0] * o_refs[0][:, _hs(h)]
            for p in range(1, n):
                acc = acc + w[p] * o_refs[p][:, _hs(h)]
            out_ref[:, _hs(h)] = acc.astype(BF16)

    big = pl.BlockSpec((tm, GROUP_W), lambda i: (i, 0))
    colb = pl.BlockSpec((tm, LANES), lambda i: (i, 0))
    return pl.pallas_call(
        body, name=name, grid=(T // tm,), in_specs=[big] * n + [colb] * n,
        out_specs=big, out_shape=jax.ShapeDtypeStruct((T, GROUP_W), BF16),
        compiler_params=_params("parallel"),
    )(*[o for o, _ in outs], *[l for _, l in outs])


def dil_combine_bwd(outs, dmixed, *, name):
    T = outs[0][0].shape[0]
    n = len(outs)
    tm = 512

    def body(*refs):
        o_refs, l_refs, do_ref = refs[:n], refs[n:2 * n], refs[2 * n]
        do_refs, dl_refs = refs[2 * n + 1:3 * n + 1], refs[3 * n + 1:]
        for r in dl_refs:
            r[...] = jnp.zeros_like(r)
        for h in range(N_HEADS):
            w = _pattern_weights(l_refs, h)
            do = do_ref[:, _hs(h)]
            dw = [jnp.sum(do * o_refs[p][:, _hs(h)], axis=1, keepdims=True) for p in range(n)]
            mean = functools.reduce(lambda a, b: a + b, [w[p] * dw[p] for p in range(n)])
            for p in range(n):
                do_refs[p][:, _hs(h)] = w[p] * do
                dl_refs[p][:, h:h + 1] = w[p] * (dw[p] - mean)

    big = pl.BlockSpec((tm, GROUP_W), lambda i: (i, 0))
    colb = pl.BlockSpec((tm, LANES), lambda i: (i, 0))
    sd = jax.ShapeDtypeStruct
    res = pl.pallas_call(
        body, name=name, grid=(T // tm,),
        in_specs=[big] * n + [colb] * n + [pl.BlockSpec((tm, GROUP_W), lambda i: (i, 2))],
        out_specs=[big] * n + [colb] * n, out_shape=[sd((T, GROUP_W), F32)] * n + [sd((T, LANES), F32)] * n,
        compiler_params=_params("parallel"),
    )(*[o for o, _ in outs], *[l for _, l in outs], dmixed)
    return list(zip(res[:n], res[n:]))


def dilated_fwd(qkv, bias, tag):
    return [band_fwd(qkv, bias, p, name=f"{tag}_band_fwd{p}") for p in range(len(DILATIONS))]


def dilated_bwd(qkv, bias, outs, dmixed, tag):
    grads = dil_combine_bwd(outs, dmixed, name=f"{tag}_combine_bwd")
    parts, ds_all = [], []
    for p, d in enumerate(DILATIONS):
        (_, lse), (do, dlse) = outs[p], grads[p]
        res = band_bwd(qkv, bias, lse, do, dlse, p, name=f"{tag}_band_bwd{p}")
        dq, dk, dv, ds = res[0], res[1], res[2], res[-1]
        if len(res) > 4:
            dk = shift_add(dk, res[3], d, name=f"{tag}_dk{p}")
            dv = shift_add(dv, res[4], d, name=f"{tag}_dv{p}")
        parts.append([dq, dk, dv])
        ds_all.append(ds)
    return parts, jnp.concatenate(ds_all, axis=0)


def assemble_dqkv(d_sb, d_fox, d_dil, *, name):
    T = d_sb[0].shape[0]
    tr = 512
    n_pat = len(d_dil)
    flat = list(d_sb) + list(d_fox) + [a for part in d_dil for a in part]

    def body(*refs):
        o_ref = refs[-1]
        for j in range(6):
            o_ref[:, j * GROUP_W:(j + 1) * GROUP_W] = refs[j][...].astype(BF16)
        for j in range(3):
            acc = refs[6 + j][...]
            for p in range(1, n_pat):
                acc = acc + refs[6 + 3 * p + j][...]
            o_ref[:, (6 + j) * GROUP_W:(7 + j) * GROUP_W] = acc.astype(BF16)

    blk = pl.BlockSpec((tr, GROUP_W), lambda i: (i, 0))
    return pl.pallas_call(
        body, name=name, grid=(T // tr,), in_specs=[blk] * len(flat),
        out_specs=pl.BlockSpec((tr, QKV_BLOCKS * GROUP_W), lambda i: (i, 0)),
        out_shape=jax.ShapeDtypeStruct((T, QKV_BLOCKS * GROUP_W), BF16), compiler_params=_params("parallel"),
    )(*flat)


def sum_cast(arrs, dtype, *, name):
    R, C = arrs[0].shape
    tr = _largest_tile(R, 512, 16)
    n = len(arrs)

    def body(*refs):
        acc = refs[0][...].astype(F32)
        for r in refs[1:n]:
            acc = acc + r[...].astype(F32)
        refs[n][...] = acc.astype(dtype)

    blk = pl.BlockSpec((tr, C), lambda i: (i, 0))
    return pl.pallas_call(
        body, name=name, grid=(R // tr,), in_specs=[blk] * n, out_specs=blk, out_shape=jax.ShapeDtypeStruct((R, C), dtype),
        compiler_params=_params("parallel"),
    )(*arrs)


GRAD_WIRE = BF16


def _block_diag_halves(w):
    z = jnp.zeros((HEAD_DIM, HEAD_DIM), w.dtype)
    half = lambda a, b: jnp.concatenate([jnp.concatenate([a, z], axis=1), jnp.concatenate([z, b], axis=1)], axis=0)
    return jnp.stack([half(w[0], w[1]), half(w[2], w[3])]).astype(BF16)


def _diag_blocks(d):
    h = HEAD_DIM
    return jnp.stack([d[0, :h, :h], d[0, h:, h:], d[1, :h, :h], d[1, h:, h:]])


def layer_fwd(x, mem2d, W, P, bias, tag):
    s = {}
    s["x"] = x
    h1 = rmsnorm_fwd(x, P["norm_mix_g"], name=f"{tag}_norm_mix")
    qkv = matmul(h1, W["qkv"], out_dtype=BF16, name=f"{tag}_qkv")
    aux = matmul(h1, W["aux"], name=f"{tag}_aux")
    o_sb = sbw_fwd(qkv, name=f"{tag}_sb_fwd")
    cumc = fox_prep(aux, P["bf"], name=f"{tag}_fox_prep")
    cumr = col_to_row(cumc)
    o_fox, o_fox32, lse_fox = foxw_fwd(qkv, cumc, cumr, name=f"{tag}_fox_fwd")
    dil = dilated_fwd(qkv, bias, tag)
    o_dil = dil_combine_fwd(dil, name=f"{tag}_dil_combine")
    o_lru, h_lru = lru_fwd(aux, P["lru_conv_w"], P["lru_conv_b"], P["wa"], P["lru_b_a"], P["wx"], P["lru_b_x"],
                           P["lru_lambda"], name=f"{tag}_lru_fwd")
    mixed = jnp.concatenate([o_sb, o_fox, o_dil, o_lru], axis=1)
    if "rest" in W:
        W.update(W.pop("rest")(mixed))
    x1 = matmul(mixed, W["out"], residual=x, name=f"{tag}_out")
    hq = rmsnorm_fwd(x1, P["norm_cross_g"], name=f"{tag}_norm_cross")
    qc = matmul(hq, W["cq"], out_dtype=BF16, name=f"{tag}_cq")
    memn = rmsnorm_fwd(mem2d, P["norm_mem_g"], name=f"{tag}_norm_mem")
    kv = matmul(memn, W["ckv"], out_dtype=BF16, name=f"{tag}_ckv")
    oc = cross_fwd(qc, kv, name=f"{tag}_cross_fwd")
    x2 = matmul(oc, W["coT"], trans_b=True, residual=x1, name=f"{tag}_co")
    h2 = rmsnorm_fwd(x2, P["norm_ffn_g"], name=f"{tag}_norm_ffn")
    hu = matmul(h2, W["up_u"], trans_b=True, name=f"{tag}_up_u")
    hg = matmul(h2, W["up_g"], trans_b=True, name=f"{tag}_up_g")
    act = glu_fwd(hu, hg, P["wu"], P["wg"], P["bu"], P["bg"], name=f"{tag}_glu_fwd")
    x3 = matmul(act, W["down"], residual=x2, name=f"{tag}_down")
    s.update(h1=h1, qkv=qkv, aux=aux, cumc=cumc, cumr=cumr, lse_fox=lse_fox, o_fox32=o_fox32, dil=dil, h_lru=h_lru, mixed=mixed,
             x1=x1, hq=hq, qc=qc, memn=memn, kv=kv, oc=oc, x2=x2, h2=h2, hu=hu, hg=hg, act=act)
    return x3, s


def layer_bwd(dx3, mem2d, W, P, bias, s, tag, hooks=None):
    mm = functools.partial(matmul, out_dtype=GRAD_WIRE, trans_a=True)
    gW, gP = {}, {}
    hooks = hooks or {}
    dact = matmul(dx3, W["down"], trans_b=True, name=f"{tag}_d_act")
    gW["down"] = mm(s["act"], dx3, name=f"{tag}_g_down")
    dhu, dhg, dwu, dwg, dbu, dbg = glu_bwd(s["hu"], s["hg"], dact, P["wu"], P["wg"], P["bu"], P["bg"], name=f"{tag}_glu_bwd")
    gP["ffn_conv_w"] = jnp.concatenate([dwu, dwg], axis=1)
    gP["ffn_conv_b"] = jnp.concatenate([dbu, dbg], axis=1)
    dh2 = matmul(dhu, W["up_u"], name=f"{tag}_d_h2u")
    dh2 = matmul(dhg, W["up_g"], residual=dh2, name=f"{tag}_d_h2g")
    gW["up_u"] = mm(dhu, s["h2"], name=f"{tag}_g_up_u")
    gW["up_g"] = mm(dhg, s["h2"], name=f"{tag}_g_up_g")
    dx2, gP["norm_ffn_g"] = rmsnorm_bwd(s["x2"], P["norm_ffn_g"], dh2, dx3, name=f"{tag}_norm_ffn_bwd")
    if "ffn" in hooks:
        hooks["ffn"](gW, W, s)
    doc = matmul(dx2, W["coT"], name=f"{tag}_d_oc")
    gW["coT"] = mm(dx2, s["oc"], name=f"{tag}_g_co")
    dqc, dkv = cross_bwd(s["qc"], s["kv"], doc, name=f"{tag}_cross_bwd")
    dhq = matmul(dqc, W["cq"], trans_b=True, name=f"{tag}_d_hq")
    gW["cq"] = mm(s["hq"], dqc, name=f"{tag}_g_cq")
    dmemn = matmul(dkv, W["ckv"], trans_b=True, name=f"{tag}_d_memn")
    gW["ckv"] = mm(s["memn"], dkv, name=f"{tag}_g_ckv")
    _, gP["norm_mem_g"] = rmsnorm_bwd(mem2d, P["norm_mem_g"], dmemn, None, name=f"{tag}_norm_mem_bwd")
    dx1, gP["norm_cross_g"] = rmsnorm_bwd(s["x1"], P["norm_cross_g"], dhq, dx2, name=f"{tag}_norm_cross_bwd")
    dmixed = matmul(dx1, W["out"], trans_b=True, name=f"{tag}_d_mixed")
    gW["out"] = mm(s["mixed"], dx1, name=f"{tag}_g_out")
    if "mid" in hooks:
        hooks["mid"](gW, W, s)
    qkv, aux = s["qkv"], s["aux"]
    d_sb = sbw_bwd(qkv, dmixed, name=f"{tag}_sb_bwd")
    dfq, dfk, dfv, dcc, dcr = foxw_bwd(qkv, s["cumc"], s["cumr"], s["lse_fox"], s["o_fox32"], dmixed, name=f"{tag}_fox_bwd")
    dcum = sum_cast([dcc, row_to_col(dcr)], F32, name=f"{tag}_dcum")
    df, dbf = fox_prep_bwd(aux, P["bf"], dcum, name=f"{tag}_fox_prep_bwd")
    gP["b_forget"] = dbf[0, :N_HEADS]
    d_dil, ds_band = dilated_bwd(qkv, bias, s["dil"], dmixed, tag)
    dlx, dlg, dcw, dcb, dwa, dba, dwx, dbx, dlam = lru_bwd(
        aux, s["h_lru"], dmixed, P["lru_conv_w"], P["lru_conv_b"], P["wa"], P["lru_b_a"], P["wx"], P["lru_b_x"],
        P["lru_lambda"], name=f"{tag}_lru_bwd")
    gP.update(lru_conv_w=dcw, lru_conv_b=dcb, lru_w_a=_diag_blocks(dwa), lru_b_a=dba, lru_w_x=_diag_blocks(dwx),
              lru_b_x=dbx, lru_lambda=dlam)
    dqkv = assemble_dqkv(d_sb, [dfq, dfk, dfv], d_dil, name=f"{tag}_dqkv")
    daux = jnp.concatenate([dlx, dlg, df], axis=1)
    dh1 = matmul(dqkv, W["qkv"], trans_b=True, name=f"{tag}_d_h1a")
    dh1 = matmul(daux, W["aux"], trans_b=True, residual=dh1, name=f"{tag}_d_h1b")
    gW["qkv"] = mm(s["h1"], dqkv, name=f"{tag}_g_qkv")
    gW["aux"] = mm(s["h1"], daux, name=f"{tag}_g_aux")
    dx, gP["norm_mix_g"] = rmsnorm_bwd(s["x"], P["norm_mix_g"], dh1, dx1, name=f"{tag}_norm_mix_bwd")
    return dx, gW, gP, ds_band


def local_step(x, mem, target, weights_of, Ps, rel_bias, final_norm_g, grads_done=None, bwd_hooks=None):
    B = x.shape[0]
    x2d = x.reshape(B * SEQ, D_MODEL)
    mem2d = mem.reshape(B * N_MEM, D_MODEL)
    bias = relbias_expand(rel_bias, name="relbias_expand")
    saved, Ws = [], []
    h = x2d
    for l in range(DEPTH):
        Ws.append(weights_of(l, h))
        h, s = layer_fwd(h, mem2d, Ws[l], Ps[l], bias, f"l{l}")
        saved.append(s)
    loss, dh, d_final = loss_head(h, final_norm_g, target.reshape(B * SEQ, D_MODEL), name="loss_head")
    gWs, gPs, ds_bands = [None] * DEPTH, [None] * DEPTH, []
    for l in range(DEPTH - 1, -1, -1):
        hooks = None if bwd_hooks is None else bwd_hooks(l)
        dh, gWs[l], gPs[l], ds = layer_bwd(dh, mem2d, Ws[l], Ps[l], bias, saved[l], f"l{l}", hooks)
        if grads_done is not None:
            grads_done(l, gWs[l])
        ds_bands.append(ds)
    d_rel = relbias_reduce(sum_cast([d.reshape(-1, BAND) for d in ds_bands], F32, name="ds_band_sum").reshape(-1, BLOCK, BAND),
                           name="relbias_reduce")
    return loss, dh.reshape(B, SEQ, D_MODEL), gWs, gPs, d_rel, d_final


def small_params(p, l):
    row = lambda name: p[name][l].reshape(1, -1)
    ffn_w, ffn_b = p["ffn_conv_w"][l], row("ffn_conv_b")
    return dict(
        norm_mix_g=row("norm_mix_g"), norm_cross_g=row("norm_cross_g"), norm_mem_g=row("norm_mem_g"), norm_ffn_g=row("norm_ffn_g"),
        bf=jnp.pad(row("b_forget"), ((0, 0), (0, LANES - N_HEADS))),
        lru_conv_w=p["lru_conv_w"][l], lru_conv_b=row("lru_conv_b"), wa=_block_diag_halves(p["lru_w_a"][l]), lru_b_a=row("lru_b_a"),
        wx=_block_diag_halves(p["lru_w_x"][l]), lru_b_x=row("lru_b_x"), lru_lambda=row("lru_lambda"),
        wu=ffn_w[:, :D_FF], wg=ffn_w[:, D_FF:], bu=ffn_b[:, :D_FF], bg=ffn_b[:, D_FF:])


def canonical_weights(w_in, w_out, w_cq, w_ck, w_cv, w_co, w_up, w_down):
    sb_fox, fox_f, rest = w_in[:, :6 * GROUP_W], w_in[:, 6 * GROUP_W:6 * GROUP_W + N_HEADS], w_in[:, 6 * GROUP_W + N_HEADS:]
    dil, lru = rest[:, :3 * GROUP_W], rest[:, 3 * GROUP_W:]
    pad = jnp.zeros((w_in.shape[0], AUX_W - 2 * GROUP_W - N_HEADS), w_in.dtype)
    return dict(qkv=jnp.concatenate([sb_fox, dil], axis=1), aux=jnp.concatenate([lru, fox_f, pad], axis=1), out=w_out,
                cq=w_cq, ckv=jnp.concatenate([w_ck, w_cv], axis=1), coT=w_co.T, upT=w_up.T, down=w_down)


def native_grads(g):
    qkv, aux = g["qkv"], g["aux"]
    a, b = 6 * GROUP_W, 6 * GROUP_W + N_HEADS
    w_in = jnp.zeros((qkv.shape[0], b + 5 * GROUP_W), qkv.dtype)
    w_in = w_in.at[:, :a].set(qkv[:, :a]).at[:, a:b].set(aux[:, 2 * GROUP_W:2 * GROUP_W + N_HEADS])
    w_in = w_in.at[:, b:b + 3 * GROUP_W].set(qkv[:, a:]).at[:, b + 3 * GROUP_W:].set(aux[:, :2 * GROUP_W])
    return (w_in, g["out"], g["cq"], g["ckv"][:, :GROUP_W], g["ckv"][:, GROUP_W:], g["coT"].T) + native_ffn_grads(g)


def native_ffn_grads(g):
    return (g["upT"].T, g["down"])


ANY = pl.BlockSpec(memory_space=pl.ANY)
VMEM_SPEC = pl.BlockSpec(memory_space=pltpu.VMEM)


def _place():
    x, y, c = lax.axis_index("x"), lax.axis_index("y"), lax.axis_index("c")
    other_chips = [(1 - x, y), (x, 1 - y), (1 - x, 1 - y)]
    return x, y, c, other_chips


def _gather_body(x_ref, out_ref, send_sems, recv_sems, local_sem):
    x, y, c, chips = _place()
    me, sibling = (x, y, c), (x, y, 1 - c)

    def slot(px, py, pc):
        return out_ref.at[4 * px + 2 * py + pc]

    def copy(k, block, to, src=None):
        return pltpu.make_async_remote_copy(
            src_ref=slot(*block) if src is None else src, dst_ref=slot(*block),
            send_sem=send_sems.at[k], recv_sem=recv_sems.at[k], device_id=to, device_id_type=MESH)

    if local_sem is not None:
        mine = pltpu.make_async_copy(x_ref, slot(*me), local_sem)
        mine.start()
    first = [copy(0, me, sibling, src=x_ref)]
    first += [copy(1 + j, me, (*chip, c), src=x_ref) for j, chip in enumerate(chips)]
    for cp in first:
        cp.start()
    passed = [copy(4 + j, (*chip, c), sibling) for j, chip in enumerate(chips)]
    for j, chip in enumerate(chips):
        copy(1 + j, (*chip, c), me).wait_recv()
        passed[j].start()
    copy(0, sibling, me).wait_recv()
    for j, chip in enumerate(chips):
        copy(4 + j, (*chip, 1 - c), me).wait_recv()
    for cp in first + passed:
        cp.wait_send()
    if local_sem is not None:
        mine.wait()


_GATHER_SEMS = [pltpu.SemaphoreType.DMA((7,)), pltpu.SemaphoreType.DMA((7,)), pltpu.SemaphoreType.DMA]


def allgather_hbm(shard, me, *, name):
    def body(x_ref, out_ref, done_ref, send_sems, recv_sems):
        _gather_body(x_ref, out_ref, send_sems, recv_sems, None)
        done_ref[...] = jnp.zeros_like(done_ref)

    others, done = pl.pallas_call(
        body, name=name, in_specs=[ANY], out_specs=[ANY, VMEM_SPEC],
        out_shape=[jax.ShapeDtypeStruct((N_DEV,) + shard.shape, shard.dtype), jax.ShapeDtypeStruct((8, LANES), F32)],
        scratch_shapes=_GATHER_SEMS[:2],
    )(shard)
    return lax.dynamic_update_slice(others, shard[None], (me, 0, 0)), done


def allgather_small(x, *, name, reduce=False):
    def body(x_ref, out_ref, second_ref, *sems):
        _gather_body(x_ref, out_ref, *sems)
        if reduce:
            acc = out_ref[0]
            for d in range(1, N_DEV):
                acc = acc + out_ref[d]
            second_ref[...] = acc
        else:
            second_ref[...] = jnp.zeros_like(second_ref)

    sd = jax.ShapeDtypeStruct
    return pl.pallas_call(
        body, name=name, in_specs=[VMEM_SPEC], out_specs=[VMEM_SPEC, VMEM_SPEC],
        out_shape=[sd((N_DEV,) + x.shape, x.dtype), sd(x.shape if reduce else (8, LANES), x.dtype)],
        scratch_shapes=_GATHER_SEMS, compiler_params=pltpu.CompilerParams(vmem_limit_bytes=VMEM_LIMIT_V7X),
    )(x)


N_CHIPS = 4


def pair_exchange(g, *, name):
    _, R, C = g.shape

    def body(g_ref, recv_ref, send_sems, recv_sems):
        x, y, c, _ = _place()
        sibling = (x, y, 1 - c)
        remote = [pltpu.make_async_remote_copy(
            src_ref=g_ref.at[2 * q + (1 - c)], dst_ref=recv_ref.at[q], send_sem=send_sems.at[q], recv_sem=recv_sems.at[q],
            device_id=sibling, device_id_type=MESH) for q in range(N_CHIPS)]
        for cp in remote:
            cp.start()
        for cp in remote:
            cp.wait_recv()
        for cp in remote:
            cp.wait_send()

    return pl.pallas_call(
        body, name=name, in_specs=[ANY], out_specs=ANY, out_shape=jax.ShapeDtypeStruct((N_CHIPS, R, C), g.dtype),
        scratch_shapes=[pltpu.SemaphoreType.DMA((N_CHIPS,))] * 2,
    )(g)


def chip_exchange(s, *, name):
    _, R, C = s.shape

    def body(s_ref, o0, o1, o2, send_sems, recv_sems):
        x, y, c, chips = _place()
        outs = (o0, o1, o2)
        copies = [pltpu.make_async_remote_copy(
            src_ref=s_ref.at[2 * cx + cy], dst_ref=outs[j], send_sem=send_sems.at[j], recv_sem=recv_sems.at[j],
            device_id=(cx, cy, c), device_id_type=MESH) for j, (cx, cy) in enumerate(chips)]
        for cp in copies:
            cp.start()
        for cp in copies:
            cp.wait_recv()
        for cp in copies:
            cp.wait_send()

    sd = jax.ShapeDtypeStruct((R, C), s.dtype)
    return pl.pallas_call(
        body, name=name, in_specs=[ANY], out_specs=[ANY] * 3, out_shape=[sd] * 3,
        scratch_shapes=[pltpu.SemaphoreType.DMA((3,)), pltpu.SemaphoreType.DMA((3,))],
    )(s)


HBM_SPEC = pl.BlockSpec(memory_space=pltpu.HBM)
SEM_SPEC = pl.BlockSpec(memory_space=pltpu.SEMAPHORE)
N_PEERS = N_DEV - 1


def _peers():
    x, y, c = lax.axis_index("x"), lax.axis_index("y"), lax.axis_index("c")
    flip = lambda v, bit: 1 - v if bit else v
    out = []
    for k in range(1, N_DEV):
        px, py, pc = flip(x, (k >> 2) & 1), flip(y, (k >> 1) & 1), flip(c, k & 1)
        out.append(((px, py, pc), 4 * px + 2 * py + pc))
    return out, 4 * x + 2 * y + c


def _peer_copies(src_ref, land_ref, send_sems, recv_sems, scatter, landing):
    peers, me = _peers()
    return [pltpu.make_async_remote_copy(
        src_ref=src_ref.at[idx] if scatter else src_ref, dst_ref=land_ref.at[me if landing == "mine" else idx],
        send_sem=send_sems.at[k], recv_sem=recv_sems.at[k], device_id=peer, device_id_type=MESH)
        for k, (peer, idx) in enumerate(peers)]


def exchange_start(src, scatter, *, name):
    shape = (N_DEV,) + src.shape[-2:]

    def body(src_ref, land_ref, send_sems, recv_sems, src_thru, land_thru, token):
        for cp in _peer_copies(src_ref, land_ref, send_sems, recv_sems, scatter, "mine"):
            cp.start()
        token[...] = jnp.zeros_like(token)

    sems = pltpu.SemaphoreType.DMA((N_PEERS,))
    return pl.pallas_call(
        body, name=name,
        out_shape=(sems, sems, pltpu.HBM(src.shape, src.dtype), pltpu.HBM(shape, src.dtype), jax.ShapeDtypeStruct((8, LANES), F32)),
        in_specs=(HBM_SPEC, HBM_SPEC), out_specs=(SEM_SPEC, SEM_SPEC, HBM_SPEC, HBM_SPEC, VMEM_SPEC),
        input_output_aliases={0: 2, 1: 3},
        compiler_params=pltpu.CompilerParams(has_side_effects=pltpu.SideEffectType.DATAFLOW_SIDE_EFFECTING),
    )(pltpu.with_memory_space_constraint(src, pltpu.HBM), pltpu.with_memory_space_constraint(lax.empty(shape, src.dtype), pltpu.HBM))


def exchange_wait(started, after, scatter, *, name):
    send_sems, recv_sems, src_thru, land_thru, _ = started

    def body(src_ref, land_ref, send_sems, recv_sems, after_ref, src_dead, got_ref):
        for cp in _peer_copies(src_ref, land_ref, send_sems, recv_sems, scatter, "theirs"):
            cp.wait_send()
            cp.wait_recv()

    return pl.pallas_call(
        body, name=name, out_shape=(pltpu.HBM(src_thru.shape, src_thru.dtype), pltpu.HBM(land_thru.shape, land_thru.dtype)),
        in_specs=(HBM_SPEC, HBM_SPEC, SEM_SPEC, SEM_SPEC, ANY), out_specs=(HBM_SPEC, HBM_SPEC), input_output_aliases={0: 0, 1: 1},
        compiler_params=pltpu.CompilerParams(has_side_effects=pltpu.SideEffectType.DATAFLOW_SIDE_EFFECTING),
    )(src_thru, land_thru, send_sems, recv_sems, after)[1]


def sum_blocks(blocks, *, name):
    n, R, C = blocks.shape
    tr = _largest_tile(R, 512, 16)

    def body(b_ref, o_ref):
        acc = b_ref[0].astype(F32)
        for d in range(1, n):
            acc = acc + b_ref[d].astype(F32)
        o_ref[...] = acc

    return pl.pallas_call(
        body, name=name, grid=(R // tr,),
        in_specs=[pl.BlockSpec((n, tr, C), lambda i: (0, i, 0))], out_specs=pl.BlockSpec((tr, C), lambda i: (i, 0)),
        out_shape=jax.ShapeDtypeStruct((R, C), F32), compiler_params=_params("parallel"),
    )(blocks)


WEIGHTS = ("norm_mix_g", "w_in", "b_forget", "lru_conv_w", "lru_conv_b", "lru_w_a", "lru_b_a", "lru_w_x", "lru_b_x", "lru_lambda",
           "w_out", "norm_cross_g", "norm_mem_g", "w_cq", "w_ck", "w_cv", "w_co", "norm_ffn_g", "w_up", "ffn_conv_w", "ffn_conv_b",
           "w_down", "rel_bias", "final_norm_g")
LARGE = ("w_in", "w_out", "w_cq", "w_ck", "w_cv", "w_co", "w_up", "w_down")
COLUMN_SPLIT_SMALL = ("lru_conv_w", "ffn_conv_w")
PACK = (("qkv", 128, 2304), ("aux", 128, 640), ("out", 128, 1024), ("cq", 128, 256), ("ckv", 128, 512), ("coT", 128, 256),
        ("upT", 704, 1024), ("down", 352, 1024))
PACK_W = 1024


def _pack_rows(parts):
    return jnp.concatenate([p.reshape(-1, PACK_W) for p in parts], axis=0)


def _pad_rows(flat, mult=8 * LANES):
    n = flat.shape[0]
    return jnp.pad(flat, (0, (-n) % mult)).reshape(-1, LANES)


def kernel(x, mem, norm_mix_g, w_in, b_forget, lru_conv_w, lru_conv_b, lru_w_a, lru_b_a, lru_w_x, lru_b_x, lru_lambda, w_out, norm_cross_g, norm_mem_g, w_cq, w_ck, w_cv, w_co, norm_ffn_g, w_up, ffn_conv_w, ffn_conv_b, w_down, rel_bias, final_norm_g, loss_target, m_norm_mix_g, m_w_in, m_b_forget, m_lru_conv_w, m_lru_conv_b, m_lru_w_a, m_lru_b_a, m_lru_w_x, m_lru_b_x, m_lru_lambda, m_w_out, m_norm_cross_g, m_norm_mem_g, m_w_cq, m_w_ck, m_w_cv, m_w_co, m_norm_ffn_g, m_w_up, m_ffn_conv_w, m_ffn_conv_b, m_w_down, m_rel_bias, m_final_norm_g, v_norm_mix_g, v_w_in, v_b_forget, v_lru_conv_w, v_lru_conv_b, v_lru_w_a, v_lru_b_a, v_lru_w_x, v_lru_b_x, v_lru_lambda, v_w_out, v_norm_cross_g, v_norm_mem_g, v_w_cq, v_w_ck, v_w_cv, v_w_co, v_norm_ffn_g, v_w_up, v_ffn_conv_w, v_ffn_conv_b, v_w_down, v_rel_bias, v_final_norm_g):
    w = dict(norm_mix_g=norm_mix_g, w_in=w_in, b_forget=b_forget, lru_conv_w=lru_conv_w, lru_conv_b=lru_conv_b, lru_w_a=lru_w_a,
             lru_b_a=lru_b_a, lru_w_x=lru_w_x, lru_b_x=lru_b_x, lru_lambda=lru_lambda, w_out=w_out, norm_cross_g=norm_cross_g,
             norm_mem_g=norm_mem_g, w_cq=w_cq, w_ck=w_ck, w_cv=w_cv, w_co=w_co, norm_ffn_g=norm_ffn_g, w_up=w_up,
             ffn_conv_w=ffn_conv_w, ffn_conv_b=ffn_conv_b, w_down=w_down, rel_bias=rel_bias, final_norm_g=final_norm_g)
    m = dict(norm_mix_g=m_norm_mix_g, w_in=m_w_in, b_forget=m_b_forget, lru_conv_w=m_lru_conv_w, lru_conv_b=m_lru_conv_b,
             lru_w_a=m_lru_w_a, lru_b_a=m_lru_b_a, lru_w_x=m_lru_w_x, lru_b_x=m_lru_b_x, lru_lambda=m_lru_lambda, w_out=m_w_out,
             norm_cross_g=m_norm_cross_g, norm_mem_g=m_norm_mem_g, w_cq=m_w_cq, w_ck=m_w_ck, w_cv=m_w_cv, w_co=m_w_co,
             norm_ffn_g=m_norm_ffn_g, w_up=m_w_up, ffn_conv_w=m_ffn_conv_w, ffn_conv_b=m_ffn_conv_b, w_down=m_w_down,
             rel_bias=m_rel_bias, final_norm_g=m_final_norm_g)
    v = dict(norm_mix_g=v_norm_mix_g, w_in=v_w_in, b_forget=v_b_forget, lru_conv_w=v_lru_conv_w, lru_conv_b=v_lru_conv_b,
             lru_w_a=v_lru_w_a, lru_b_a=v_lru_b_a, lru_w_x=v_lru_w_x, lru_b_x=v_lru_b_x, lru_lambda=v_lru_lambda, w_out=v_w_out,
             norm_cross_g=v_norm_cross_g, norm_mem_g=v_norm_mem_g, w_cq=v_w_cq, w_ck=v_w_ck, w_cv=v_w_cv, w_co=v_w_co,
             norm_ffn_g=v_norm_ffn_g, w_up=v_w_up, ffn_conv_w=v_ffn_conv_w, ffn_conv_b=v_ffn_conv_b, w_down=v_w_down,
             rel_bias=v_rel_bias, final_norm_g=v_final_norm_g)
    me = 4 * lax.axis_index("x") + 2 * lax.axis_index("y") + lax.axis_index("c")

    conv_shard = jnp.concatenate([w[n].reshape(-1) for n in COLUMN_SPLIT_SMALL])
    conv_all, conv_gathered = allgather_small(_pad_rows(conv_shard), name="gather_conv")
    conv_all = conv_all.reshape(N_DEV, -1)
    full = dict(w)
    off = 0
    for n in COLUMN_SPLIT_SMALL:
        d, k, c = w[n].shape
        blocks = conv_all[:, off:off + d * k * c].reshape(N_DEV, d, k, c)
        full[n] = blocks.transpose(1, 2, 0, 3).reshape(d, k, N_DEV * c)
        off += d * k * c

    IN, MID, FFN = PACK[:2], PACK[2:6], PACK[6:]
    REST = MID + FFN

    def packed_shard(l, group):
        canon = canonical_weights(*[w[n][l] for n in LARGE])
        return _pack_rows([canon[k].astype(BF16) for k, _, _ in group])

    def unpack_weights(packed, group):
        W, row = {}, 0
        for k, r, c in group:
            n_rows = r * c // PACK_W
            W[k] = packed[:, row:row + n_rows].reshape(N_DEV * r, c)
            row += n_rows
        if "upT" in W:
            upT = W.pop("upT")
            W["up_u"], W["up_g"] = upT[:D_FF], upT[D_FF:]
        return W

    def packed_grads(gW, group):
        g = dict(gW)
        if "up_u" in g:
            g["upT"] = jnp.concatenate([g.pop("up_u"), g.pop("up_g")], axis=0)
        return jnp.concatenate([g[k].reshape(N_DEV, r * c // PACK_W, PACK_W) for k, r, c in group], axis=1)

    def unpack_grads(shard_sum, group):
        g, row = {}, 0
        for k, r, c in group:
            n_rows = r * c // PACK_W
            g[k] = shard_sum[row:row + n_rows].reshape(r, c)
            row += n_rows
        return g

    def own_block_in(landed, block):
        return lax.dynamic_update_slice(landed, block[None], (me, 0, 0))

    def gathered_weights(copies, shard, after, group, name):
        return unpack_weights(own_block_in(exchange_wait(copies, after, False, name=name), shard), group)

    def scattered_sum(src, copies, after, tag):
        landed = exchange_wait(copies, after, True, name=f"{tag}_wait")
        mine = lax.dynamic_index_in_dim(src, me, axis=0, keepdims=False)
        return sum_blocks(own_block_in(landed, mine), name=f"{tag}_sum")

    last = DEPTH - 1
    in0, gathered = allgather_hbm(packed_shard(0, IN) + conv_gathered[0, 0].astype(BF16), me, name="gather_weights")
    rest0_shard = packed_shard(0, REST) + gathered[0, 0].astype(BF16)
    gather_rest0 = exchange_start(rest0_shard, False, name="gather_rest0_start")
    last_shard = packed_shard(last, PACK) + gather_rest0[4][0, 0].astype(BF16)
    gather_last = exchange_start(last_shard, False, name="gather_last_start")
    started = gather_last[4][0, 0]
    layer_weights = {}

    def weights_of(l, h):
        if l == 0:
            W = unpack_weights(in0, IN)
            W["rest"] = lambda after: gathered_weights(gather_rest0, rest0_shard, after, REST, "gather_rest0_wait")
        else:
            assert l == last
            W = gathered_weights(gather_last, last_shard, h, PACK, "gather_last_wait")
        layer_weights[l] = W
        return W

    in_flight = {}

    def scatter(key, g_all, name):
        in_flight[key] = (g_all, exchange_start(g_all, True, name=name))
        return in_flight[key][1][4][0, 0].astype(BF16)

    def grads_done(l, gW):
        if l == last:
            W0 = layer_weights[0]
            W0["down"] = W0["down"] + scatter("last", packed_grads(gW, PACK), "grads_last_start")

    def ffn0_grads_done(gW, W, s):
        W["coT"] = W["coT"] + scatter("ffn0", packed_grads({k: gW[k] for k in ("up_u", "up_g", "down")}, FFN), "grads_ffn0_start")

    def mid0_grads_done(gW, W, s):
        s["cumc"] = s["cumc"] + scatter("mid0", packed_grads({k: gW[k] for k, _, _ in MID}, MID), "grads_mid0_start").astype(F32)

    Ps = [small_params(full, l) for l in range(DEPTH)]
    Ps[0]["norm_mix_g"] = Ps[0]["norm_mix_g"] + started
    loss, grad_x, gWs, gPs, d_rel, d_final = local_step(
        x, mem, loss_target, weights_of, Ps, rel_bias, final_norm_g.reshape(1, -1), grads_done,
        lambda l: {"ffn": ffn0_grads_done, "mid": mid0_grads_done} if l == 0 else None)

    shard_grads = {last: unpack_grads(scattered_sum(*in_flight["last"], grad_x, "grads_last"), PACK)}
    shard_grads[0] = unpack_grads(scattered_sum(*in_flight["ffn0"], grad_x, "grads_ffn0"), FFN)
    shard_grads[0].update(unpack_grads(scattered_sum(*in_flight["mid0"], grad_x, "grads_mid0"), MID))

    g_all = packed_grads({k: gWs[0][k] for k, _, _ in IN}, IN)
    rows = g_all.shape[1]
    got = pair_exchange(g_all, name="grads_pair_exchange")
    own = lax.dynamic_index_in_dim(g_all.reshape(N_CHIPS, 2, rows, PACK_W), lax.axis_index("c"), axis=1, keepdims=False)
    pair = sum_cast([own.reshape(-1, PACK_W), got.reshape(-1, PACK_W)], GRAD_WIRE, name="grads_pair_sum").reshape(N_CHIPS, rows, PACK_W)
    from_x, from_y, from_xy = chip_exchange(pair, name="grads_chip_exchange")
    mine = lax.dynamic_index_in_dim(pair, 2 * lax.axis_index("x") + lax.axis_index("y"), axis=0, keepdims=False)
    shard_grads[0].update(unpack_grads(sum_cast([mine, from_x, from_y, from_xy], F32, name="grads_chip_sum"), IN))

    grads = {}
    per_layer = [native_grads(shard_grads[l]) for l in range(DEPTH)]
    for i, n in enumerate(LARGE):
        grads[n] = jnp.stack([per_layer[l][i] for l in range(DEPTH)])

    small_names = [n for n in WEIGHTS if n not in LARGE and n not in ("rel_bias", "final_norm_g")]
    pieces = [gPs[l][n].reshape(-1) for n in small_names for l in range(DEPTH)] + [d_rel.reshape(-1), d_final.reshape(-1), loss[0, :1]]
    sizes = [p.shape[0] for p in pieces]
    _, total = allgather_small(_pad_rows(jnp.concatenate(pieces)), name="allreduce_small", reduce=True)
    total = total.reshape(-1)
    off, it = 0, iter(sizes)
    for n in small_names:
        per = []
        for l in range(DEPTH):
            sz = next(it)
            per.append(total[off:off + sz])
            off += sz
        full_shape = (DEPTH,) + full[n].shape[1:]
        gfull = jnp.stack(per).reshape(full_shape)
        if n in COLUMN_SPLIT_SMALL:
            c = w[n].shape[-1]
            gfull = lax.dynamic_slice_in_dim(gfull, me * c, c, axis=gfull.ndim - 1)
        grads[n] = gfull
    grads["rel_bias"] = total[off:off + rel_bias.size].reshape(rel_bias.shape)
    off += rel_bias.size
    grads["final_norm_g"] = total[off:off + D_MODEL]
    off += D_MODEL
    loss_out = total[off]

    delta, new_m, new_v = {}, {}, {}
    for n in LARGE:
        shape = w[n].shape
        two_d = lambda a: a.reshape(-1, shape[-1])
        d_, m_, v_ = adamw(two_d(w[n]), two_d(grads[n]), two_d(m[n]), two_d(v[n]), name=f"adamw_{n}")
        delta[n], new_m[n], new_v[n] = d_.reshape(shape), m_.reshape(shape), v_.reshape(shape)
    small_all = [n for n in WEIGHTS if n not in LARGE]
    two_d = lambda a: a.reshape(-1, a.shape[-1])
    d_, m_, v_ = adamw_many(*[[two_d(src[n]) for n in small_all] for src in (w, grads, m, v)], name="adamw_small")
    for i, n in enumerate(small_all):
        delta[n], new_m[n], new_v[n] = (a[i].reshape(w[n].shape) for a in (d_, m_, v_))

    return (loss_out, grad_x, *[grads[n] for n in WEIGHTS], *[delta[n] for n in WEIGHTS], *[new_m[n] for n in WEIGHTS],
            *[new_v[n] for n in WEIGHTS])
```

```python
import functools
import math

import numpy as np
import jax
import jax.numpy as jnp
from jax import lax
from jax.experimental import pallas as pl
from jax.experimental.pallas import tpu as pltpu

F32 = jnp.float32
BF16 = jnp.bfloat16
MESH = pl.DeviceIdType.MESH

N_DEV = 8
D_MODEL = 1024
SEQ = 2048
DEPTH = 2
HEAD_DIM = 64
N_HEADS = 4
GROUP_W = N_HEADS * HEAD_DIM
D_FF = 2816
N_MEM = 256
NUM_BUCKETS = 32
MAX_DISTANCE = 2048
BLOCK = 128
DILATIONS = (1, 4, 16)
EPS = 1e-6
LRU_C = 8.0
Q_SCALE = HEAD_DIM ** -0.5
AUX_W = 640
LRU_HALF_W = 128
LRU_HALVES = GROUP_W // LRU_HALF_W
ADAM_LR, ADAM_B1, ADAM_B2, ADAM_EPS, ADAM_WD, ADAM_STEP = 0.001, 0.9, 0.999, 1e-08, 0.01, 10

VMEM_LIMIT_V7X = 48 * 1024 * 1024


def _params(*sem):
    return pltpu.CompilerParams(dimension_semantics=sem if sem else None, vmem_limit_bytes=VMEM_LIMIT_V7X)


def _pick(n, cands):
    for c in cands:
        if n % c == 0:
            return c
    return n


def _largest_tile(n, cap, align):
    best = None
    for t in range(align, min(n, cap) + 1, align):
        if n % t == 0:
            best = t
    return n if best is None else best


def matmul(a, b, *, name, trans_a=False, trans_b=False, out_dtype=F32, residual=None):
    (K, M) = a.shape if trans_a else a.shape[::-1]
    (N, Kb) = b.shape if trans_b else b.shape[::-1]
    assert K == Kb, (a.shape, b.shape)
    tm = _largest_tile(M, 1408 if trans_a else (1024 if K <= 1024 else 512), 128)
    tn = _largest_tile(N, 1408, 128)
    tk = _largest_tile(K, 1024 if trans_a else 2816, 128)
    nk = K // tk
    a_spec = pl.BlockSpec((tk, tm), lambda i, j, k: (k, i)) if trans_a else pl.BlockSpec((tm, tk), lambda i, j, k: (i, k))
    b_spec = pl.BlockSpec((tn, tk), lambda i, j, k: (j, k)) if trans_b else pl.BlockSpec((tk, tn), lambda i, j, k: (k, j))
    o_spec = pl.BlockSpec((tm, tn), lambda i, j, k: (i, j))
    dims = (((0 if trans_a else 1,), (1 if trans_b else 0,)), ((), ()))
    has_res = residual is not None

    def body(*refs):
        a_ref, b_ref = refs[0], refs[1]
        r_ref = refs[2] if has_res else None
        part = lax.dot_general(a_ref[...].astype(BF16), b_ref[...].astype(BF16), dims, preferred_element_type=F32)
        if nk == 1:
            if has_res:
                part = part + r_ref[...].astype(F32)
            refs[-1][...] = part.astype(out_dtype)
            return
        o_ref, acc_ref = refs[-2], refs[-1]
        k = pl.program_id(2)

        @pl.when(k == 0)
        def _():
            acc_ref[...] = part

        @pl.when(k > 0)
        def _():
            acc_ref[...] += part

        @pl.when(k == nk - 1)
        def _():
            r = acc_ref[...]
            if has_res:
                r = r + r_ref[...].astype(F32)
            o_ref[...] = r.astype(out_dtype)

    ops = (a, b) + ((residual,) if has_res else ())
    return pl.pallas_call(
        body, name=name, grid=(M // tm, N // tn, nk),
        in_specs=[a_spec, b_spec] + ([o_spec] if has_res else []),
        out_specs=o_spec, out_shape=jax.ShapeDtypeStruct((M, N), out_dtype),
        scratch_shapes=[pltpu.VMEM((tm, tn), F32)] if nk > 1 else [],
        compiler_params=_params("parallel", "parallel", "arbitrary"),
    )(*ops)


def rmsnorm_fwd(x, g, *, name):
    R, D = x.shape
    tr = _pick(R, (512, 256))

    def body(x_ref, g_ref, o_ref):
        xv = x_ref[...]
        r = lax.rsqrt(jnp.mean(xv * xv, axis=-1, keepdims=True) + EPS)
        o_ref[...] = (xv * r * g_ref[...]).astype(BF16)

    return pl.pallas_call(
        body, name=name, grid=(R // tr,),
        in_specs=[pl.BlockSpec((tr, D), lambda i: (i, 0)), pl.BlockSpec((1, D), lambda i: (0, 0))],
        out_specs=pl.BlockSpec((tr, D), lambda i: (i, 0)), out_shape=jax.ShapeDtypeStruct((R, D), BF16),
        compiler_params=_params("parallel"),
    )(x, g)


def rmsnorm_bwd(x, g, dh, dres, *, name):
    R, D = x.shape
    tr = _pick(R, (512, 256))
    has_res = dres is not None

    def body(*refs):
        x_ref, g_ref, dh_ref = refs[:3]
        dx_ref, dg_ref = refs[-2], refs[-1]
        xv = x_ref[...]
        r = lax.rsqrt(jnp.mean(xv * xv, axis=-1, keepdims=True) + EPS)
        n = xv * r
        dhv = dh_ref[...]
        dn = dhv * g_ref[...]
        dx = r * (dn - n * jnp.mean(dn * n, axis=-1, keepdims=True))
        if has_res:
            dx = dx + refs[3][...]
        dx_ref[...] = dx
        part = jnp.sum(dhv * n, axis=0, keepdims=True)

        @pl.when(pl.program_id(0) == 0)
        def _():
            dg_ref[...] = part

        @pl.when(pl.program_id(0) > 0)
        def _():
            dg_ref[...] += part

    row = pl.BlockSpec((tr, D), lambda i: (i, 0))
    vec = pl.BlockSpec((1, D), lambda i: (0, 0))
    ops = (x, g, dh) + ((dres,) if has_res else ())
    return pl.pallas_call(
        body, name=name, grid=(R // tr,),
        in_specs=[row, vec, row] + ([row] if has_res else []),
        out_specs=[row, vec],
        out_shape=[jax.ShapeDtypeStruct((R, D), F32), jax.ShapeDtypeStruct((1, D), F32)],
        compiler_params=_params("arbitrary"),
    )(*ops)


_SQRT_HALF = 0.7071067811865476
_INV_SQRT_2PI = 0.3989422804014327


def _normal_cdf_pdf(x):
    ax = jnp.abs(x) * _SQRT_HALF
    t = 1.0 / (1.0 + 0.3275911 * ax)
    poly = t * (0.254829592 + t * (-0.284496736 + t * (1.421413741 + t * (-1.453152027 + t * 1.061405429))))
    e = jnp.exp(-0.5 * x * x)
    half_tail = 0.5 * poly * e
    return jnp.where(x < 0, half_tail, 1.0 - half_tail), e


def _gelu_cdf(x):
    return _normal_cdf_pdf(x)[0]


def _gelu_and_grad(x):
    cdf, e = _normal_cdf_pdf(x)
    return x * cdf, cdf + x * _INV_SQRT_2PI * e


def _shift_down(main, halo, first, shifts):
    halo = jnp.where(first, 0.0, halo)
    ext = jnp.concatenate([halo, main], axis=0)
    return [pltpu.roll(ext, s, 0)[8:] for s in shifts]


def _conv3(main, halo, first, w, b):
    m1, m2 = _shift_down(main, halo, first, (1, 2))
    return ((b + w[0:1] * m2) + w[1:2] * m1) + w[2:3] * main, m1, m2


def glu_fwd(hu, hg, wu, wg, bu, bg, *, name):
    T, F = hu.shape
    tm, tf = 512, _largest_tile(F, 704, 128)
    hb = tm // 8
    blocks_per_example = SEQ // tm

    def body(hu_ref, hg_ref, hau_ref, hag_ref, wu_ref, wg_ref, bu_ref, bg_ref, o_ref):
        first = pl.program_id(0) % blocks_per_example == 0
        up, _, _ = _conv3(hu_ref[...], hau_ref[...], first, wu_ref[...], bu_ref[...])
        gate, _, _ = _conv3(hg_ref[...], hag_ref[...], first, wg_ref[...], bg_ref[...])
        o_ref[...] = (gate * _gelu_cdf(gate) * up).astype(BF16)

    main = pl.BlockSpec((tm, tf), lambda i, j: (i, j))
    halo = pl.BlockSpec((8, tf), lambda i, j: (jnp.maximum(i * hb - 1, 0), j))
    w3 = pl.BlockSpec((3, tf), lambda i, j: (0, j))
    b1 = pl.BlockSpec((1, tf), lambda i, j: (0, j))
    return pl.pallas_call(
        body, name=name, grid=(T // tm, F // tf),
        in_specs=[main, main, halo, halo, w3, w3, b1, b1],
        out_specs=main, out_shape=jax.ShapeDtypeStruct((T, F), BF16),
        compiler_params=_params("parallel", "parallel"),
    )(hu, hg, hu, hg, wu, wg, bu, bg)


def glu_bwd(hu, hg, dact, wu, wg, bu, bg, *, name):
    T, F = hu.shape
    tm, tf = 512, _largest_tile(F, 704, 128)
    hb = tm // 8
    blocks_per_example = SEQ // tm
    n_halo_blocks = T // 8
    n_ext = tm + 8

    def body(hu_ref, hg_ref, hau_ref, hag_ref, hnu_ref, hng_ref, da_ref, dan_ref, wu_ref, wg_ref, bu_ref, bg_ref,
             du_ref, dg_ref, dwu_ref, dwg_ref, dbu_ref, dbg_ref):
        i = pl.program_id(1)
        first = i % blocks_per_example == 0
        last = i % blocks_per_example == blocks_per_example - 1
        wu, wg = wu_ref[...], wg_ref[...]

        def conv_ext(main_ref, prev_ref, next_ref, w, b):
            ext = jnp.concatenate([jnp.where(first, 0.0, prev_ref[...]), main_ref[...], next_ref[...]], axis=0)
            x0, x1, x2 = ext[8:], pltpu.roll(ext, 1, 0)[8:], pltpu.roll(ext, 2, 0)[8:]
            return ((b + w[0:1] * x2) + w[1:2] * x1) + w[2:3] * x0, x0, x1, x2

        up, xu, u1, u2 = conv_ext(hu_ref, hau_ref, hnu_ref, wu, bu_ref[...])
        gate, xg, g1, g2 = conv_ext(hg_ref, hag_ref, hng_ref, wg, bg_ref[...])
        act, dact_dgate = _gelu_and_grad(gate)
        da = jnp.concatenate([da_ref[...], jnp.where(last, 0.0, dan_ref[...])], axis=0)
        dup = da * act
        dgate = da * up * dact_dgate

        def conv_t(d, w):
            return (w[2:3] * d[:tm] + w[1:2] * pltpu.roll(d, n_ext - 1, 0)[:tm] + w[0:1] * pltpu.roll(d, n_ext - 2, 0)[:tm]).astype(BF16)

        du_ref[...] = conv_t(dup, wu)
        dg_ref[...] = conv_t(dgate, wg)

        def sums(d, x0, x1, x2):
            s = lambda v: jnp.sum(v[:tm], axis=0, keepdims=True)
            return jnp.concatenate([s(d * x2), s(d * x1), s(d * x0)], axis=0), s(d)

        pwu, pbu = sums(dup, xu, u1, u2)
        pwg, pbg = sums(dgate, xg, g1, g2)

        @pl.when(i == 0)
        def _():
            dwu_ref[...] = pwu
            dwg_ref[...] = pwg
            dbu_ref[...] = pbu
            dbg_ref[...] = pbg

        @pl.when(i > 0)
        def _():
            dwu_ref[...] += pwu
            dwg_ref[...] += pwg
            dbu_ref[...] += pbu
            dbg_ref[...] += pbg

    main = pl.BlockSpec((tm, tf), lambda j, i: (i, j))
    before = pl.BlockSpec((8, tf), lambda j, i: (jnp.maximum(i * hb - 1, 0), j))
    after = pl.BlockSpec((8, tf), lambda j, i: (jnp.minimum((i + 1) * hb, n_halo_blocks - 1), j))
    w3 = pl.BlockSpec((3, tf), lambda j, i: (0, j))
    b1 = pl.BlockSpec((1, tf), lambda j, i: (0, j))
    sd = jax.ShapeDtypeStruct
    return pl.pallas_call(
        body, name=name, grid=(F // tf, T // tm),
        in_specs=[main, main, before, before, after, after, main, after, w3, w3, b1, b1],
        out_specs=[main, main, w3, w3, b1, b1],
        out_shape=[sd((T, F), BF16), sd((T, F), BF16), sd((3, F), F32), sd((3, F), F32), sd((1, F), F32), sd((1, F), F32)],
        compiler_params=_params("parallel", "arbitrary"),
    )(hu, hg, hu, hg, hu, hg, dact, dact, wu, wg, bu, bg)


def loss_head(x, g, target, *, name):
    T, D = x.shape
    tr = 256

    def body(x_ref, g_ref, t_ref, loss_ref, dx_ref, dg_ref):
        xv = x_ref[...]
        gv = g_ref[...]
        r = lax.rsqrt(jnp.mean(xv * xv, axis=-1, keepdims=True) + EPS)
        n = xv * r
        err = n * gv - t_ref[...]
        part_loss = jnp.zeros((1, 128), F32) + 0.5 * jnp.sum(jnp.mean(err * err, axis=-1, keepdims=True))
        dy = err * (1.0 / D)
        dn = dy * gv
        dx_ref[...] = r * (dn - n * jnp.mean(dn * n, axis=-1, keepdims=True))
        part_g = jnp.sum(dy * n, axis=0, keepdims=True)

        @pl.when(pl.program_id(0) == 0)
        def _():
            loss_ref[...] = part_loss
            dg_ref[...] = part_g

        @pl.when(pl.program_id(0) > 0)
        def _():
            loss_ref[...] += part_loss
            dg_ref[...] += part_g

    row = pl.BlockSpec((tr, D), lambda i: (i, 0))
    vec = pl.BlockSpec((1, D), lambda i: (0, 0))
    sd = jax.ShapeDtypeStruct
    return pl.pallas_call(
        body, name=name, grid=(T // tr,),
        in_specs=[row, vec, row],
        out_specs=[pl.BlockSpec((1, 128), lambda i: (0, 0)), row, vec],
        out_shape=[sd((1, 128), F32), sd((T, D), F32), sd((1, D), F32)],
        compiler_params=_params("arbitrary"),
    )(x, g, target)


def adamw(w, g, m, v, *, name):
    R, C = w.shape
    tr = _pick(R, (256, 128, 64, 32, 16, 8))

    def body(w_ref, g_ref, m_ref, v_ref, d_ref, nm_ref, nv_ref):
        gv = g_ref[...]
        mn = ADAM_B1 * m_ref[...] + (1.0 - ADAM_B1) * gv
        vn = ADAM_B2 * v_ref[...] + (1.0 - ADAM_B2) * (gv * gv)
        m_hat = mn / (1.0 - ADAM_B1 ** ADAM_STEP)
        v_hat = vn / (1.0 - ADAM_B2 ** ADAM_STEP)
        d_ref[...] = -ADAM_LR * (m_hat / (jnp.sqrt(v_hat) + ADAM_EPS) + ADAM_WD * w_ref[...])
        nm_ref[...] = mn
        nv_ref[...] = vn

    blk = pl.BlockSpec((tr, C), lambda i: (i, 0))
    sd = jax.ShapeDtypeStruct((R, C), F32)
    return pl.pallas_call(
        body, name=name, grid=(R // tr,), in_specs=[blk] * 4, out_specs=[blk] * 3, out_shape=[sd] * 3,
        compiler_params=_params("parallel"),
    )(w, g, m, v)


def adamw_many(ws, gs, ms, vs, *, name):
    n = len(ws)

    def body(*refs):
        ins, outs = refs[:4 * n], refs[4 * n:]
        for i in range(n):
            w_ref, g_ref, m_ref, v_ref = ins[i], ins[n + i], ins[2 * n + i], ins[3 * n + i]
            gv = g_ref[...]
            mn = ADAM_B1 * m_ref[...] + (1.0 - ADAM_B1) * gv
            vn = ADAM_B2 * v_ref[...] + (1.0 - ADAM_B2) * (gv * gv)
            m_hat = mn / (1.0 - ADAM_B1 ** ADAM_STEP)
            v_hat = vn / (1.0 - ADAM_B2 ** ADAM_STEP)
            outs[i][...] = -ADAM_LR * (m_hat / (jnp.sqrt(v_hat) + ADAM_EPS) + ADAM_WD * w_ref[...])
            outs[n + i][...] = mn
            outs[2 * n + i][...] = vn

    vm = pl.BlockSpec(memory_space=pltpu.VMEM)
    shapes = [jax.ShapeDtypeStruct(w.shape, F32) for w in ws]
    res = pl.pallas_call(
        body, name=name, in_specs=[vm] * (4 * n), out_specs=[vm] * (3 * n), out_shape=shapes * 3, compiler_params=_params(),
    )(*ws, *gs, *ms, *vs)
    return res[:n], res[n:2 * n], res[2 * n:]


def _softplus(x):
    return jnp.maximum(x, 0.0) + jnp.log(1.0 + jnp.exp(-jnp.abs(x)))


def _lru_gates(x, cw, cb, wa, ba, wx, bx, lam):
    S = x.shape[0]
    row = lax.broadcasted_iota(jnp.int32, (S, 1), 0)

    def back(s):
        return jnp.where(row >= s, pltpu.roll(x, s, 0), 0.0)

    xc = (((cb + cw[0:1] * back(3)) + cw[1:2] * back(2)) + cw[2:3] * back(1)) + cw[3:4] * x
    xb = xc.astype(BF16)
    r = jax.nn.sigmoid(jnp.dot(xb, wa, preferred_element_type=F32) + ba)
    ig = jax.nn.sigmoid(jnp.dot(xb, wx, preferred_element_type=F32) + bx)
    sp = _softplus(-lam)
    la = -LRU_C * r * sp
    a = jnp.exp(la)
    y = 2.0 * la
    one_minus_a2 = jnp.where(y > -0.05, -y * (1.0 + y * (0.5 + y * (1.0 / 6.0 + y * (1.0 / 24.0)))), 1.0 - jnp.exp(y))
    mm = jnp.sqrt(one_minus_a2)
    return xc, xb, r, ig, sp, a, mm


SCAN_UNROLL = 4


def _scan8(a, b, reverse):
    row = lax.broadcasted_iota(jnp.int32, (8, 1), 0)
    for k in (1, 2, 4):
        inside = row < 8 - k if reverse else row >= k
        shift = 8 - k if reverse else k
        a_n = jnp.where(inside, pltpu.roll(a, shift, 0), 1.0)
        b_n = jnp.where(inside, pltpu.roll(b, shift, 0), 0.0)
        b = a * b_n + b
        a = a * a_n
    return a, b


def lru_fwd(aux, cw, cb, wa, ba, wx, bx, lam, *, name):
    T = aux.shape[0]
    S, C = SEQ, LRU_HALF_W

    def body(x_ref, g_ref, cw_ref, cb_ref, wa_ref, ba_ref, wx_ref, bx_ref, lam_ref, o_ref, h_ref, a_s, u_s):
        xc, _, r, ig, sp, a, mm = _lru_gates(x_ref[...], cw_ref[...], cb_ref[...], wa_ref[...], ba_ref[...],
                                             wx_ref[...], bx_ref[...], lam_ref[...])
        a_s[...] = a
        u_s[...] = mm * (ig * xc)

        def group(i, h):
            for j in range(SCAN_UNROLL):
                base = pl.multiple_of((i * SCAN_UNROLL + j) * 8, 8)
                A, Bv = _scan8(a_s[pl.ds(base, 8), :], u_s[pl.ds(base, 8), :], reverse=False)
                H = A * h + Bv
                h_ref[pl.ds(base, 8), :] = H
                h = H[7:8]
            return h

        lax.fori_loop(0, S // 8 // SCAN_UNROLL, group, jnp.zeros((1, C), F32))
        gate = g_ref[...]
        o_ref[...] = (h_ref[...] * (gate * _gelu_cdf(gate))).astype(BF16)

    blk = lambda col: pl.BlockSpec((S, C), lambda c, b: (b, col + c))
    par = lambda rows: pl.BlockSpec((rows, C), lambda c, b: (0, c))
    sq = pl.BlockSpec((None, C, C), lambda c, b: (c, 0, 0))
    sd = jax.ShapeDtypeStruct
    W = LRU_HALVES * C
    return pl.pallas_call(
        body, name=name, grid=(LRU_HALVES, T // S),
        in_specs=[blk(0), blk(LRU_HALVES), par(4), par(1), sq, par(1), sq, par(1), par(1)],
        out_specs=[blk(0), blk(0)], out_shape=[sd((T, W), BF16), sd((T, W), F32)],
        scratch_shapes=[pltpu.VMEM((S, C), F32), pltpu.VMEM((S, C), F32)],
        compiler_params=_params("parallel", "parallel"),
    )(aux, aux, cw, cb, wa, ba, wx, bx, lam)


def lru_bwd(aux, h, dmixed, cw, cb, wa, ba, wx, bx, lam, *, name):
    T = aux.shape[0]
    S, C = SEQ, LRU_HALF_W

    def body(x_ref, g_ref, h_ref, do_ref, cw_ref, cb_ref, wa_ref, ba_ref, wx_ref, bx_ref, lam_ref,
             dx_ref, dgate_ref, dcw_ref, dcb_ref, dwa_ref, dba_ref, dwx_ref, dbx_ref, dlam_ref, a_s, d_s):
        x = x_ref[...]
        cw = cw_ref[...]
        lam = lam_ref[...]
        xc, xb, r, ig, sp, a, mm = _lru_gates(x, cw, cb_ref[...], wa_ref[...], ba_ref[...], wx_ref[...], bx_ref[...], lam)
        gate = g_ref[...]
        gl, dgl = _gelu_and_grad(gate)
        dout = do_ref[...]
        hv = h_ref[...]
        dgate_ref[...] = dout * hv * dgl
        a_s[...] = a
        d_s[...] = dout * gl

        last_row = lax.broadcasted_iota(jnp.int32, (8, 1), 0) == 7

        def group(i, c):
            for j in range(SCAN_UNROLL):
                base = pl.multiple_of((S // 8 - 1 - (i * SCAN_UNROLL + j)) * 8, 8)
                a8 = a_s[pl.ds(base, 8), :]
                d8 = d_s[pl.ds(base, 8), :]
                A, Bv = _scan8(a8, a8 * d8, reverse=True)
                Cv = A * c + Bv
                d_s[pl.ds(base, 8), :] = d8 + jnp.where(last_row, c, pltpu.roll(Cv, 7, 0))
                c = Cv[0:1]
            return c

        lax.fori_loop(0, S // 8 // SCAN_UNROLL, group, jnp.zeros((1, C), F32))
        row = lax.broadcasted_iota(jnp.int32, (S, 1), 0)
        dht = d_s[...]
        h_prev = jnp.where(row >= 1, pltpu.roll(hv, 1, 0), 0.0)
        da = dht * h_prev
        gx = ig * xc
        dmm = dht * gx
        dig = dht * mm * xc
        dxc = dht * mm * ig
        dla = da * a - dmm * (a * a) / mm
        dr = dla * (-LRU_C * sp)
        dsp = jnp.sum(dla * (-LRU_C * r), axis=0, keepdims=True)
        dlam = dsp * (-jax.nn.sigmoid(-lam))
        dpa = dr * r * (1.0 - r)
        dpx = dig * ig * (1.0 - ig)
        dpa_b, dpx_b = dpa.astype(BF16), dpx.astype(BF16)
        nt = (((1,), (1,)), ((), ()))
        tn = (((0,), (0,)), ((), ()))
        dxc = dxc + lax.dot_general(dpa_b, wa_ref[...], nt, preferred_element_type=F32) \
                  + lax.dot_general(dpx_b, wx_ref[...], nt, preferred_element_type=F32)
        dwa = lax.dot_general(xb, dpa_b, tn, preferred_element_type=F32)
        dwx = lax.dot_general(xb, dpx_b, tn, preferred_element_type=F32)

        def fwd(v, s):
            return jnp.where(row < S - s, pltpu.roll(v, S - s, 0), 0.0)

        def back(v, s):
            return jnp.where(row >= s, pltpu.roll(v, s, 0), 0.0)

        dx_ref[...] = cw[3:4] * dxc + cw[2:3] * fwd(dxc, 1) + cw[1:2] * fwd(dxc, 2) + cw[0:1] * fwd(dxc, 3)
        s0 = lambda v: jnp.sum(v, axis=0, keepdims=True)
        dcw = jnp.concatenate([s0(dxc * back(x, 3)), s0(dxc * back(x, 2)), s0(dxc * back(x, 1)), s0(dxc * x)], axis=0)
        parts = ((dcw_ref, dcw), (dcb_ref, s0(dxc)), (dwa_ref, dwa), (dba_ref, s0(dpa)), (dwx_ref, dwx),
                 (dbx_ref, s0(dpx)), (dlam_ref, dlam))

        @pl.when(pl.program_id(1) == 0)
        def _():
            for ref, val in parts:
                ref[...] = val

        @pl.when(pl.program_id(1) > 0)
        def _():
            for ref, val in parts:
                ref[...] += val

    blk = lambda col: pl.BlockSpec((S, C), lambda c, b: (b, col + c))
    par = lambda rows: pl.BlockSpec((rows, C), lambda c, b: (0, c))
    sq = pl.BlockSpec((None, C, C), lambda c, b: (c, 0, 0))
    sd = jax.ShapeDtypeStruct
    W = LRU_HALVES * C
    vec = sd((1, W), F32)
    return pl.pallas_call(
        body, name=name, grid=(LRU_HALVES, T // S),
        in_specs=[blk(0), blk(LRU_HALVES), blk(0), blk(3 * LRU_HALVES), par(4), par(1), sq, par(1), sq, par(1), par(1)],
        out_specs=[blk(0), blk(0), par(4), par(1), sq, par(1), sq, par(1), par(1)],
        out_shape=[sd((T, W), F32), sd((T, W), F32), sd((4, W), F32), vec, sd((LRU_HALVES, C, C), F32), vec,
                   sd((LRU_HALVES, C, C), F32), vec, vec],
        scratch_shapes=[pltpu.VMEM((S, C), F32), pltpu.VMEM((S, C), F32)],
        compiler_params=_params("parallel", "arbitrary"),
    )(aux, aux, h, dmixed, cw, cb, wa, ba, wx, bx, lam)


_NT = (((1,), (1,)), ((), ()))
_TN = (((0,), (0,)), ((), ()))


def _dot(a, b, dims=None):
    if dims is None:
        return jnp.dot(a, b, preferred_element_type=F32)
    return lax.dot_general(a, b, dims, preferred_element_type=F32)


def _hs(h):
    return slice(h * HEAD_DIM, (h + 1) * HEAD_DIM)


def cross_fwd(q, kv, *, name):
    T = q.shape[0]
    tq = 512

    def body(q_ref, kv_ref, o_ref):
        for h in range(N_HEADS):
            qh = q_ref[:, _hs(h)] * Q_SCALE
            k = kv_ref[:, _hs(h)]
            v = kv_ref[:, GROUP_W + h * HEAD_DIM:GROUP_W + (h + 1) * HEAD_DIM]
            s = _dot(qh, k, _NT)
            p = jnp.exp(s - jnp.max(s, axis=-1, keepdims=True))
            p = p / jnp.sum(p, axis=-1, keepdims=True)
            o_ref[:, _hs(h)] = _dot(p.astype(BF16), v).astype(BF16)

    per = SEQ // tq
    return pl.pallas_call(
        body, name=name, grid=(T // tq,),
        in_specs=[pl.BlockSpec((tq, GROUP_W), lambda i: (i, 0)), pl.BlockSpec((N_MEM, 2 * GROUP_W), lambda i: (i // per, 0))],
        out_specs=pl.BlockSpec((tq, GROUP_W), lambda i: (i, 0)), out_shape=jax.ShapeDtypeStruct((T, GROUP_W), BF16),
        compiler_params=_params("parallel"),
    )(q, kv)


def cross_bwd(q, kv, do, *, name):
    T = q.shape[0]
    tq = 512
    per = SEQ // tq

    def body(q_ref, kv_ref, do_ref, dq_ref, dkv_ref):
        first = pl.program_id(0) % per == 0
        for h in range(N_HEADS):
            vs = slice(GROUP_W + h * HEAD_DIM, GROUP_W + (h + 1) * HEAD_DIM)
            qh = q_ref[:, _hs(h)] * Q_SCALE
            k = kv_ref[:, _hs(h)]
            v = kv_ref[:, vs]
            doh = do_ref[:, _hs(h)].astype(BF16)
            s = _dot(qh, k, _NT)
            p = jnp.exp(s - jnp.max(s, axis=-1, keepdims=True))
            p = p / jnp.sum(p, axis=-1, keepdims=True)
            dp = _dot(doh, v, _NT)
            ds = (p * (dp - jnp.sum(p * dp, axis=-1, keepdims=True))).astype(BF16)
            dq_ref[:, _hs(h)] = (_dot(ds, k) * Q_SCALE).astype(BF16)
            dk = _dot(ds, qh, _TN)
            dv = _dot(p.astype(BF16), doh, _TN)

            @pl.when(first)
            def _():
                dkv_ref[:, _hs(h)] = dk
                dkv_ref[:, vs] = dv

            @pl.when(jnp.logical_not(first))
            def _():
                dkv_ref[:, _hs(h)] += dk
                dkv_ref[:, vs] += dv

    qb = pl.BlockSpec((tq, GROUP_W), lambda i: (i, 0))
    kvb = pl.BlockSpec((N_MEM, 2 * GROUP_W), lambda i: (i // per, 0))
    sd = jax.ShapeDtypeStruct
    return pl.pallas_call(
        body, name=name, grid=(T // tq,),
        in_specs=[qb, kvb, qb], out_specs=[qb, kvb],
        out_shape=[sd((T, GROUP_W), BF16), sd(kv.shape, F32)],
        compiler_params=_params("arbitrary"),
    )(q, kv, do)


NB = SEQ // BLOCK
NEG = -1e30
HEADS = tuple(range(N_HEADS))


def _blk(i):
    return pl.ds(pl.multiple_of(i * BLOCK, BLOCK), BLOCK)


def _qkv_specs(first_col):
    return [pl.BlockSpec((SEQ, GROUP_W), lambda b, c=first_col + j: (b, c)) for j in range(3)]


LANES = 128
CUM_BLK = 256


def col_to_row(c):
    b = c.shape[0] // SEQ
    return c.reshape(b, SEQ, LANES)[:, :, :8].transpose(0, 2, 1).reshape(b * 8, SEQ)


def row_to_col(r):
    b = r.shape[0] // 8
    c = r.reshape(b, 8, SEQ).transpose(0, 2, 1)
    return jnp.pad(c, ((0, 0), (0, 0), (0, LANES - 8))).reshape(b * SEQ, LANES)


def fox_prep(aux, bf, *, name):
    T = aux.shape[0]

    def body(f_ref, b_ref, o_ref):
        row = lax.broadcasted_iota(jnp.int32, (CUM_BLK, CUM_BLK), 0)
        col = lax.broadcasted_iota(jnp.int32, (CUM_BLK, CUM_BLK), 1)
        upto = (col <= row).astype(BF16)
        carry = jnp.zeros((1, LANES), F32)
        for n in range(SEQ // CUM_BLK):
            rows = slice(n * CUM_BLK, (n + 1) * CUM_BLK)
            logf = -_softplus(-(f_ref[rows, :] + b_ref[...]))
            hi = logf.astype(BF16)
            lo = (logf - hi.astype(F32)).astype(BF16)
            cum = _dot(upto, hi) + _dot(upto, lo) + carry
            o_ref[rows, :] = cum
            carry = cum[CUM_BLK - 1:CUM_BLK]

    return pl.pallas_call(
        body, name=name, grid=(T // SEQ,),
        in_specs=[pl.BlockSpec((SEQ, LANES), lambda b: (b, 4)), pl.BlockSpec((1, LANES), lambda b: (0, 0))],
        out_specs=pl.BlockSpec((SEQ, LANES), lambda b: (b, 0)), out_shape=jax.ShapeDtypeStruct((T, LANES), F32),
        compiler_params=_params("parallel"),
    )(aux, bf)


def fox_prep_bwd(aux, bf, dcum, *, name):
    T = aux.shape[0]

    def body(f_ref, b_ref, d_ref, df_ref, db_ref):
        row = lax.broadcasted_iota(jnp.int32, (CUM_BLK, CUM_BLK), 0)
        col = lax.broadcasted_iota(jnp.int32, (CUM_BLK, CUM_BLK), 1)
        onward = (col >= row).astype(BF16)
        carry = jnp.zeros((1, LANES), F32)
        tot = jnp.zeros((1, LANES), F32)
        for n in range(SEQ // CUM_BLK - 1, -1, -1):
            rows = slice(n * CUM_BLK, (n + 1) * CUM_BLK)
            d = d_ref[rows, :]
            hi = d.astype(BF16)
            lo = (d - hi.astype(F32)).astype(BF16)
            dlogf = _dot(onward, hi) + _dot(onward, lo) + carry
            carry = dlogf[0:1]
            df = dlogf * jax.nn.sigmoid(-(f_ref[rows, :] + b_ref[...]))
            df_ref[rows, :] = df
            tot = tot + jnp.sum(df, axis=0, keepdims=True)

        @pl.when(pl.program_id(0) == 0)
        def _():
            db_ref[...] = tot

        @pl.when(pl.program_id(0) > 0)
        def _():
            db_ref[...] += tot

    blk = pl.BlockSpec((SEQ, LANES), lambda b: (b, 0))
    vec = pl.BlockSpec((1, LANES), lambda b: (0, 0))
    sd = jax.ShapeDtypeStruct
    return pl.pallas_call(
        body, name=name, grid=(T // SEQ,),
        in_specs=[pl.BlockSpec((SEQ, LANES), lambda b: (b, 4)), vec, blk],
        out_specs=[blk, vec], out_shape=[sd((T, LANES), F32), sd((1, LANES), F32)],
        compiler_params=_params("arbitrary"),
    )(aux, bf, dcum)


CHUNK = 256
WIDE = N_HEADS * CHUNK
NCH = SEQ // CHUNK


def _seg(h):
    return slice(h * CHUNK, (h + 1) * CHUNK)


def _chunk_rows(c):
    return pl.ds(pl.multiple_of(c * CHUNK, CHUNK), CHUNK)


def _wide_consts():
    r = lax.broadcasted_iota(jnp.int32, (WIDE, GROUP_W), 0)
    f = lax.broadcasted_iota(jnp.int32, (WIDE, GROUP_W), 1)
    bd = (r // CHUNK) == (f // HEAD_DIM)
    row = lax.broadcasted_iota(jnp.int32, (BLOCK, WIDE), 0)
    key = lax.broadcasted_iota(jnp.int32, (BLOCK, WIDE), 1) % CHUNK
    return bd, row, key


def _block_diag(x, bd):
    return jnp.where(bd, jnp.concatenate([x] * N_HEADS, axis=0), jnp.zeros((), x.dtype))


def _fold_heads(w, bd):
    w = jnp.where(bd, w, 0.0)
    return (w[0:CHUNK] + w[CHUNK:2 * CHUNK]) + (w[2 * CHUNK:3 * CHUNK] + w[3 * CHUNK:])


def _widen(cols):
    return jnp.concatenate([jnp.broadcast_to(c, (BLOCK, CHUNK)) for c in cols], axis=1)


def _head_rowsums(w):
    return [jnp.sum(w[:, _seg(h)], axis=1, keepdims=True) for h in HEADS]


def _tri_wide(x, tri):
    hi = x.astype(BF16)
    lo = (x - hi.astype(F32)).astype(BF16)
    y = _dot(jnp.concatenate([hi[:, _seg(h)] for h in HEADS] + [lo[:, _seg(h)] for h in HEADS], axis=0), tri)
    return jnp.concatenate([y[h * BLOCK:(h + 1) * BLOCK] + y[(N_HEADS + h) * BLOCK:(N_HEADS + h + 1) * BLOCK] for h in HEADS], axis=1)


def _feature_widen(cols):
    return jnp.concatenate([jnp.broadcast_to(c, (BLOCK, HEAD_DIM)) for c in cols], axis=1)


def _loop_by_two(n, index, body, carry):
    odd = n % 2
    carry = lax.fori_loop(0, odd, lambda _, cr: body(index(0), cr), carry)
    return lax.fori_loop(0, n // 2, lambda t, cr: body(index(odd + 2 * t + 1), body(index(odd + 2 * t), cr)), carry)


def _sbw_scores(q, kbd, later):
    z = _dot(q, kbd, _NT)
    lk = -_softplus(z)
    return z + lk, lk, _tri_wide(lk, later)


def _sbw_tile(q, kbd, mask, later, csum):
    z = _dot(q, kbd, _NT)
    lk = -_softplus(z)
    if mask is not None:
        lk = jnp.where(mask, lk, 0.0)
    e = z + lk
    att = jnp.exp(e + _tri_wide(lk, later) + csum)
    if mask is not None:
        att = jnp.where(mask, att, 0.0)
    return att, e, lk


def sbw_fwd(qkv, *, name):
    T = qkv.shape[0]

    def body(q_ref, k_ref, v_ref, o_ref):
        bd, row, key = _wide_consts()
        r2 = lax.broadcasted_iota(jnp.int32, (CHUNK, CHUNK), 0)
        c2 = lax.broadcasted_iota(jnp.int32, (CHUNK, CHUNK), 1)
        later = (r2 > c2).astype(BF16)

        def qblock(i, _):
            q = q_ref[_blk(i), :] * Q_SCALE
            cd = i // 2
            strict = key < row + BLOCK * (i % 2)

            def tile(c, mask, carry):
                acc, csum = carry
                att, _, lk = _sbw_tile(q, _block_diag(k_ref[_chunk_rows(c), :], bd), mask, later, csum)
                acc = acc + _dot(att.astype(BF16), _block_diag(v_ref[_chunk_rows(c), :], bd))
                return acc, csum + _widen(_head_rowsums(lk))

            def two_tiles(c1, carry):
                acc, csum = carry
                e1, lk1, t1 = _sbw_scores(q, _block_diag(k_ref[_chunk_rows(c1), :], bd), later)
                e2, lk2, t2 = _sbw_scores(q, _block_diag(k_ref[_chunk_rows(c1 - 1), :], bd), later)
                att1 = jnp.exp(e1 + t1 + csum)
                csum = csum + _widen(_head_rowsums(lk1))
                att2 = jnp.exp(e2 + t2 + csum)
                csum = csum + _widen(_head_rowsums(lk2))
                acc = acc + _dot(att1.astype(BF16), _block_diag(v_ref[_chunk_rows(c1), :], bd))
                acc = acc + _dot(att2.astype(BF16), _block_diag(v_ref[_chunk_rows(c1 - 1), :], bd))
                return acc, csum

            carry = tile(cd, strict, (jnp.zeros((BLOCK, GROUP_W), F32), jnp.zeros((BLOCK, WIDE), F32)))
            odd = cd % 2
            carry = lax.fori_loop(0, odd, lambda n, cr: tile(cd - 1, None, cr), carry)
            acc, _ = lax.fori_loop(0, cd // 2, lambda n, cr: two_tiles(cd - 1 - odd - 2 * n, cr), carry)
            o_ref[_blk(i), :] = acc.astype(BF16)
            return 0

        lax.fori_loop(0, NB, qblock, 0)

    return pl.pallas_call(
        body, name=name, grid=(T // SEQ,), in_specs=_qkv_specs(0),
        out_specs=pl.BlockSpec((SEQ, GROUP_W), lambda b: (b, 0)), out_shape=jax.ShapeDtypeStruct((T, GROUP_W), BF16),
        compiler_params=_params("parallel"),
    )(qkv, qkv, qkv)


def sbw_bwd(qkv, dmixed, *, name):
    T = qkv.shape[0]

    def body(q_ref, k_ref, v_ref, do_ref, dq_ref, dk_ref, dv_ref, att_s, sg_s):
        bd, row, key = _wide_consts()
        r2 = lax.broadcasted_iota(jnp.int32, (CHUNK, CHUNK), 0)
        c2 = lax.broadcasted_iota(jnp.int32, (CHUNK, CHUNK), 1)
        later = (r2 > c2).astype(BF16)
        earlier = (r2 < c2).astype(BF16)
        dk_ref[...] = jnp.zeros_like(dk_ref)
        dv_ref[...] = jnp.zeros_like(dv_ref)

        def qblock(i, _):
            q = q_ref[_blk(i), :] * Q_SCALE
            do = do_ref[_blk(i), :].astype(BF16)
            cd = i // 2
            strict = key < row + BLOCK * (i % 2)

            def recompute(c, mask, csum):
                att, e, lk = _sbw_tile(q, _block_diag(k_ref[_chunk_rows(c), :], bd), mask, later, csum)
                sg = jnp.exp(e)
                att_s[c] = att
                sg_s[c] = sg if mask is None else jnp.where(mask, sg, 0.0)
                return csum + _widen(_head_rowsums(lk))

            def recompute_two(c1, csum):
                e1, lk1, t1 = _sbw_scores(q, _block_diag(k_ref[_chunk_rows(c1), :], bd), later)
                e2, lk2, t2 = _sbw_scores(q, _block_diag(k_ref[_chunk_rows(c1 - 1), :], bd), later)
                sg_s[c1] = jnp.exp(e1)
                sg_s[c1 - 1] = jnp.exp(e2)
                att_s[c1] = jnp.exp(e1 + t1 + csum)
                csum = csum + _widen(_head_rowsums(lk1))
                att_s[c1 - 1] = jnp.exp(e2 + t2 + csum)
                return csum + _widen(_head_rowsums(lk2))

            csum = recompute(cd, strict, jnp.zeros((BLOCK, WIDE), F32))
            odd = cd % 2
            csum = lax.fori_loop(0, odd, lambda n, cs: recompute(cd - 1, None, cs), csum)
            lax.fori_loop(0, cd // 2, lambda n, cs: recompute_two(cd - 1 - odd - 2 * n, cs), csum)

            def tile(c, carry):
                dq, pre = carry
                kbd = _block_diag(k_ref[_chunk_rows(c), :], bd)
                vbd = _block_diag(v_ref[_chunk_rows(c), :], bd)
                att = att_s[c]
                ds = _dot(do, vbd, _NT) * att
                dlk = ds + _tri_wide(ds, earlier) + pre
                dz = (ds - dlk * sg_s[c]).astype(BF16)
                dk_ref[_chunk_rows(c), :] += _fold_heads(_dot(dz, q, _TN), bd)
                dv_ref[_chunk_rows(c), :] += _fold_heads(_dot(att.astype(BF16), do, _TN), bd)
                return dq + _dot(dz, kbd), pre + _widen(_head_rowsums(ds))

            def two_tiles(c1, carry):
                dq, pre = carry
                c2 = c1 + 1
                kbd1, kbd2 = _block_diag(k_ref[_chunk_rows(c1), :], bd), _block_diag(k_ref[_chunk_rows(c2), :], bd)
                att1, att2 = att_s[c1], att_s[c2]
                ds1 = _dot(do, _block_diag(v_ref[_chunk_rows(c1), :], bd), _NT) * att1
                ds2 = _dot(do, _block_diag(v_ref[_chunk_rows(c2), :], bd), _NT) * att2
                tri1, tri2 = _tri_wide(ds1, earlier), _tri_wide(ds2, earlier)
                dv_ref[_chunk_rows(c1), :] += _fold_heads(_dot(att1.astype(BF16), do, _TN), bd)
                dv_ref[_chunk_rows(c2), :] += _fold_heads(_dot(att2.astype(BF16), do, _TN), bd)
                dz1 = (ds1 - (ds1 + tri1 + pre) * sg_s[c1]).astype(BF16)
                pre = pre + _widen(_head_rowsums(ds1))
                dz2 = (ds2 - (ds2 + tri2 + pre) * sg_s[c2]).astype(BF16)
                pre = pre + _widen(_head_rowsums(ds2))
                dk_ref[_chunk_rows(c1), :] += _fold_heads(_dot(dz1, q, _TN), bd)
                dk_ref[_chunk_rows(c2), :] += _fold_heads(_dot(dz2, q, _TN), bd)
                return dq + _dot(dz1, kbd1) + _dot(dz2, kbd2), pre

            n_tiles = cd + 1
            odd = n_tiles % 2
            carry = (jnp.zeros((BLOCK, GROUP_W), F32), jnp.zeros((BLOCK, WIDE), F32))
            carry = lax.fori_loop(0, odd, lambda n, cr: tile(0, cr), carry)
            dq, _ = lax.fori_loop(0, n_tiles // 2, lambda n, cr: two_tiles(odd + 2 * n, cr), carry)
            dq_ref[_blk(i), :] = dq * Q_SCALE
            return 0

        lax.fori_loop(0, NB, qblock, 0)

    out = pl.BlockSpec((SEQ, GROUP_W), lambda b: (b, 0))
    sd = jax.ShapeDtypeStruct((T, GROUP_W), F32)
    return pl.pallas_call(
        body, name=name, grid=(T // SEQ,), in_specs=_qkv_specs(0) + [out],
        out_specs=[out] * 3, out_shape=[sd] * 3,
        scratch_shapes=[pltpu.VMEM((NCH, BLOCK, WIDE), F32), pltpu.VMEM((NCH, BLOCK, WIDE), F32)],
        compiler_params=_params("parallel"),
    )(qkv, qkv, qkv, dmixed)


def _foxw_logits(q, kbd, cq, cr_ref, c, mask):
    ck = jnp.concatenate([cr_ref[h:h + 1, _chunk_rows(c)] for h in HEADS], axis=1)
    z = _dot(q, kbd, _NT) + cq - ck
    return z if mask is None else jnp.where(mask, z, NEG)


def foxw_fwd(qkv, cumc, cumr, *, name):
    T = qkv.shape[0]

    def body(q_ref, k_ref, v_ref, cc_ref, cr_ref, o_ref, o32_ref, lse_ref, z_s):
        bd, row, key = _wide_consts()
        lse_ref[...] = jnp.zeros_like(lse_ref)

        def qblock(i, _):
            q = q_ref[_blk(i), :] * Q_SCALE
            cq = _widen([cc_ref[_blk(i), h:h + 1] for h in HEADS])
            cd = i // 2
            causal = key <= row + BLOCK * (i % 2)

            def logits(c, mask, ms):
                z = _foxw_logits(q, _block_diag(k_ref[_chunk_rows(c), :], bd), cq, cr_ref, c, mask)
                z_s[c] = z
                return tuple(jnp.maximum(ms[h], jnp.max(z[:, _seg(h)], axis=1, keepdims=True)) for h in HEADS)

            ms = logits(cd, causal, (jnp.full((BLOCK, 1), NEG, F32),) * N_HEADS)
            ms = _loop_by_two(cd, lambda n: n, lambda c, m: logits(c, None, m), ms)
            m_wide = _widen(ms)

            def values(c, carry):
                acc, l = carry
                p = jnp.exp(z_s[c] - m_wide)
                return acc + _dot(p.astype(BF16), _block_diag(v_ref[_chunk_rows(c), :], bd)), l + _widen(_head_rowsums(p))

            acc, l = _loop_by_two(cd + 1, lambda n: n, values, (jnp.zeros((BLOCK, GROUP_W), F32), jnp.zeros((BLOCK, WIDE), F32)))
            ls = [l[:, h * CHUNK:h * CHUNK + 1] for h in HEADS]
            o = acc / _feature_widen(ls)
            o_ref[_blk(i), :] = o.astype(BF16)
            o32_ref[_blk(i), :] = o
            for h in HEADS:
                lse_ref[_blk(i), h:h + 1] = ms[h] + jnp.log(ls[h])
            return 0

        lax.fori_loop(0, NB, qblock, 0)

    out = pl.BlockSpec((SEQ, GROUP_W), lambda b: (b, 0))
    colb = pl.BlockSpec((SEQ, LANES), lambda b: (b, 0))
    sd = jax.ShapeDtypeStruct
    return pl.pallas_call(
        body, name=name, grid=(T // SEQ,),
        in_specs=_qkv_specs(3) + [colb, pl.BlockSpec((8, SEQ), lambda b: (b, 0))],
        out_specs=[out, out, colb], out_shape=[sd((T, GROUP_W), BF16), sd((T, GROUP_W), F32), sd((T, LANES), F32)],
        scratch_shapes=[pltpu.VMEM((NCH, BLOCK, WIDE), F32)],
        compiler_params=_params("parallel"),
    )(qkv, qkv, qkv, cumc, cumr)


def foxw_bwd(qkv, cumc, cumr, lse, o32, dmixed, *, name):
    T = qkv.shape[0]

    def body(q_ref, k_ref, v_ref, cc_ref, cr_ref, lse_ref, o_ref, do_ref, dq_ref, dk_ref, dv_ref, dcc_ref, dcr_ref):
        bd, row, key = _wide_consts()
        dk_ref[...] = jnp.zeros_like(dk_ref)
        dv_ref[...] = jnp.zeros_like(dv_ref)
        dcc_ref[...] = jnp.zeros_like(dcc_ref)
        dcr_ref[...] = jnp.zeros_like(dcr_ref)

        def qblock(i, _):
            q = q_ref[_blk(i), :] * Q_SCALE
            do = do_ref[_blk(i), :].astype(BF16)
            prod = do.astype(F32) * o_ref[_blk(i), :]
            delta = _widen([jnp.sum(prod[:, _hs(h)], axis=1, keepdims=True) for h in HEADS])
            cq = _widen([cc_ref[_blk(i), h:h + 1] for h in HEADS])
            lse_w = _widen([lse_ref[_blk(i), h:h + 1] for h in HEADS])
            cd = i // 2
            causal = key <= row + BLOCK * (i % 2)

            def tile(c, mask, carry):
                dq, dcq = carry
                kbd = _block_diag(k_ref[_chunk_rows(c), :], bd)
                vbd = _block_diag(v_ref[_chunk_rows(c), :], bd)
                p = jnp.exp(_foxw_logits(q, kbd, cq, cr_ref, c, mask) - lse_w)
                ds = p * (_dot(do, vbd, _NT) - delta)
                dsb = ds.astype(BF16)
                dk_ref[_chunk_rows(c), :] += _fold_heads(_dot(dsb, q, _TN), bd)
                dv_ref[_chunk_rows(c), :] += _fold_heads(_dot(p.astype(BF16), do, _TN), bd)
                for h in HEADS:
                    dcr_ref[h:h + 1, _chunk_rows(c)] -= jnp.sum(ds[:, _seg(h)], axis=0, keepdims=True)
                return dq + _dot(dsb, kbd), dcq + _widen(_head_rowsums(ds))

            def two_tiles(c1, carry):
                dq, dcq = carry
                cs = (c1, c1 + 1)
                kbds = [_block_diag(k_ref[_chunk_rows(c), :], bd) for c in cs]
                vbds = [_block_diag(v_ref[_chunk_rows(c), :], bd) for c in cs]
                ps = [jnp.exp(_foxw_logits(q, kbds[j], cq, cr_ref, cs[j], None) - lse_w) for j in range(2)]
                dss = [ps[j] * (_dot(do, vbds[j], _NT) - delta) for j in range(2)]
                dsbs = [d.astype(BF16) for d in dss]
                for j, c in enumerate(cs):
                    dk_ref[_chunk_rows(c), :] += _fold_heads(_dot(dsbs[j], q, _TN), bd)
                    dv_ref[_chunk_rows(c), :] += _fold_heads(_dot(ps[j].astype(BF16), do, _TN), bd)
                    for h in HEADS:
                        dcr_ref[h:h + 1, _chunk_rows(c)] -= jnp.sum(dss[j][:, _seg(h)], axis=0, keepdims=True)
                dq = dq + _dot(dsbs[0], kbds[0]) + _dot(dsbs[1], kbds[1])
                return dq, dcq + _widen(_head_rowsums(dss[0])) + _widen(_head_rowsums(dss[1]))

            carry = tile(cd, causal, (jnp.zeros((BLOCK, GROUP_W), F32), jnp.zeros((BLOCK, WIDE), F32)))
            odd = cd % 2
            carry = lax.fori_loop(0, odd, lambda n, cr: tile(0, None, cr), carry)
            dq, dcq = lax.fori_loop(0, cd // 2, lambda n, cr: two_tiles(odd + 2 * n, cr), carry)
            dq_ref[_blk(i), :] = dq * Q_SCALE
            for h in HEADS:
                dcc_ref[_blk(i), h:h + 1] = dcq[:, h * CHUNK:h * CHUNK + 1]
            return 0

        lax.fori_loop(0, NB, qblock, 0)

    out = pl.BlockSpec((SEQ, GROUP_W), lambda b: (b, 0))
    colb = pl.BlockSpec((SEQ, LANES), lambda b: (b, 0))
    rowb = pl.BlockSpec((8, SEQ), lambda b: (b, 0))
    sd = jax.ShapeDtypeStruct
    big = sd((T, GROUP_W), F32)
    return pl.pallas_call(
        body, name=name, grid=(T // SEQ,),
        in_specs=_qkv_specs(3) + [colb, rowb, colb, out, pl.BlockSpec((SEQ, GROUP_W), lambda b: (b, 1))],
        out_specs=[out, out, out, colb, rowb],
        out_shape=[big, big, big, sd((T, LANES), F32), sd((T // SEQ * 8, SEQ), F32)],
        compiler_params=_params("parallel"),
    )(qkv, qkv, qkv, cumc, cumr, lse, o32, dmixed)


BAND = 2 * BLOCK


def _t5_bucket_np(dist):
    n = np.maximum(dist, 0)
    max_exact = NUM_BUCKETS // 2
    nf = np.maximum(n, 1).astype(np.float32)
    large = max_exact + (np.log(nf / np.float32(max_exact)) / np.float32(math.log(MAX_DISTANCE / max_exact))
                         * np.float32(NUM_BUCKETS - max_exact)).astype(np.int32)
    large = np.minimum(large, NUM_BUCKETS - 1)
    return np.where(n < max_exact, n, large).astype(np.int32)


def _band_buckets():
    qi = np.arange(BLOCK)[:, None]
    ki = np.arange(BAND)[None, :]
    delta = np.clip(qi - ki + BLOCK, 0, BLOCK)
    return np.stack([_t5_bucket_np(delta * d) for d in DILATIONS])


def relbias_expand(rel, *, name):
    buckets = jnp.asarray(_band_buckets())
    n_pat = len(DILATIONS)

    def body(rel_ref, bk_ref, o_ref):
        for p in range(n_pat):
            bk = bk_ref[p]
            for h in range(N_HEADS):
                acc = jnp.zeros((BLOCK, BAND), F32)
                for b in range(NUM_BUCKETS):
                    acc = jnp.where(bk == b, rel_ref[b, h], acc)
                o_ref[p * N_HEADS + h] = acc

    return pl.pallas_call(
        body, name=name,
        in_specs=[pl.BlockSpec(memory_space=pltpu.SMEM), pl.BlockSpec(memory_space=pltpu.VMEM)],
        out_specs=pl.BlockSpec(memory_space=pltpu.VMEM),
        out_shape=jax.ShapeDtypeStruct((n_pat * N_HEADS, BLOCK, BAND), F32),
        compiler_params=_params(),
    )(rel, buckets)


def relbias_reduce(ds_all, *, name):
    buckets = jnp.asarray(_band_buckets())
    n_pat = len(DILATIONS)

    def body(ds_ref, bk_ref, o_ref):
        for b in range(NUM_BUCKETS):
            for h in range(N_HEADS):
                tot = jnp.float32(0.0)
                for p in range(n_pat):
                    tot = tot + jnp.sum(jnp.where(bk_ref[p] == b, ds_ref[p * N_HEADS + h], 0.0))
                o_ref[b, h] = tot

    return pl.pallas_call(
        body, name=name,
        in_specs=[pl.BlockSpec(memory_space=pltpu.VMEM), pl.BlockSpec(memory_space=pltpu.VMEM)],
        out_specs=pl.BlockSpec(memory_space=pltpu.SMEM),
        out_shape=jax.ShapeDtypeStruct((NUM_BUCKETS, N_HEADS), F32),
        compiler_params=_params(),
    )(ds_all, buckets)


def _band_valid_wide(first, row, key):
    inside = jnp.logical_and(key >= row, key <= row + BLOCK)
    return jnp.logical_and(inside, jnp.logical_or(jnp.logical_not(first), key >= BLOCK))


QKV_BLOCKS = 9


def _band_in_specs(d, pattern, has_prev):
    rows = BLOCK * d
    cur = lambda c: pl.BlockSpec((rows, GROUP_W), lambda tb, r: (tb, c))
    prev = lambda c: pl.BlockSpec((rows, GROUP_W), lambda tb, r: (jnp.maximum(tb - 1, 0), c))
    bias = pl.BlockSpec((N_HEADS, BLOCK, BAND), lambda tb, r: (pattern, 0, 0))
    return [cur(6), cur(7), cur(8)] + ([prev(7), prev(8)] if has_prev else []) + [bias]


def _classes_per_step(d):
    return min(d, 4)


def _step_classes(d):
    n = _classes_per_step(d)
    return [pl.program_id(1) * n + j for j in range(n)]


def _class_rows(d, cls):
    return pl.ds(cls, BLOCK, stride=d) if d > 1 else pl.ds(0, BLOCK)


def _halves_scratch(rows, n):
    return [pltpu.VMEM((2, rows, LANES), F32)] * n


def _stage(refs, scratch):
    @pl.when(pl.program_id(1) == 0)
    def _():
        for src, dst in zip(refs, scratch):
            dst[0] = src[:, :LANES].astype(F32)
            dst[1] = src[:, LANES:].astype(F32)


def _take_class(s, d, cls):
    rows = _class_rows(d, cls)
    return jnp.concatenate([s.at[0][rows, :], s.at[1][rows, :]], axis=1)


def _put_class(s, d, cls, x):
    rows = _class_rows(d, cls)
    s.at[0][rows, :] = x[:, :LANES]
    s.at[1][rows, :] = x[:, LANES:]


def _flush(scratch, refs, d):
    @pl.when(pl.program_id(1) == d // _classes_per_step(d) - 1)
    def _():
        for s, o in zip(scratch, refs):
            o[...] = jnp.concatenate([s[0], s[1]], axis=1)


def _band_operands(scratch, d, cls, has_prev):
    take = lambda s: _take_class(s, d, cls).astype(BF16)
    q = (_take_class(scratch[0], d, cls) * Q_SCALE).astype(BF16)
    if has_prev:
        k = jnp.concatenate([take(scratch[3]), take(scratch[1])], axis=0)
        v = jnp.concatenate([take(scratch[4]), take(scratch[2])], axis=0)
    else:
        k = jnp.concatenate([jnp.zeros((BLOCK, GROUP_W), BF16), take(scratch[1])], axis=0)
        v = jnp.concatenate([jnp.zeros((BLOCK, GROUP_W), BF16), take(scratch[2])], axis=0)
    return q, k, v


def _lane_columns(cols):
    lane = lax.broadcasted_iota(jnp.int32, (BLOCK, LANES), 1)
    out = jnp.zeros((BLOCK, LANES), F32)
    for h, c in enumerate(cols):
        out = jnp.where(lane == h, c, out)
    return out


def band_fwd(qkv, bias, pattern, *, name):
    T = qkv.shape[0]
    d = DILATIONS[pattern]
    rows_per_block = BLOCK * d
    seq_blocks = SEQ // rows_per_block
    has_prev = seq_blocks > 1
    n_in = 5 if has_prev else 3

    def body(*refs):
        ins, b_ref, o_ref, lse_ref = refs[:n_in], refs[n_in], refs[n_in + 1], refs[n_in + 2]
        staged, o_s = refs[n_in + 3:2 * n_in + 3], refs[2 * n_in + 3]
        bd, row, key = _wide_consts()
        valid = _band_valid_wide(pl.program_id(0) % seq_blocks == 0, row, key)
        _stage(ins, staged)
        bias_w = jnp.concatenate([b_ref[h] for h in HEADS], axis=1)
        for cls in _step_classes(d):
            q, k, v = _band_operands(staged, d, cls, has_prev)
            kbd, vbd = _block_diag(k, bd), _block_diag(v, bd)
            sc = jnp.where(valid, _dot(q, kbd, _NT) + bias_w, NEG)
            ms = [jnp.max(sc[:, _seg(h)], axis=1, keepdims=True) for h in HEADS]
            p = jnp.exp(sc - _widen(ms))
            ls = _head_rowsums(p)
            _put_class(o_s, d, cls, _dot(p.astype(BF16), vbd) / _feature_widen(ls))
            lse_ref[_class_rows(d, cls), :] = _lane_columns([ms[h] + jnp.log(ls[h]) for h in HEADS])
        _flush([o_s], [o_ref], d)

    sd = jax.ShapeDtypeStruct
    return pl.pallas_call(
        body, name=name, grid=(T // rows_per_block, d // _classes_per_step(d)), in_specs=_band_in_specs(d, pattern, has_prev),
        out_specs=[pl.BlockSpec((rows_per_block, GROUP_W), lambda tb, r: (tb, 0)),
                   pl.BlockSpec((rows_per_block, LANES), lambda tb, r: (tb, 0))],
        out_shape=[sd((T, GROUP_W), F32), sd((T, LANES), F32)],
        scratch_shapes=_halves_scratch(rows_per_block, n_in + 1),
        compiler_params=_params("parallel", "arbitrary"),
    )(*([qkv] * n_in), bias)


def band_bwd(qkv, bias, lse, do, dlse, pattern, *, name):
    T = qkv.shape[0]
    d = DILATIONS[pattern]
    rows_per_block = BLOCK * d
    seq_blocks = SEQ // rows_per_block
    has_prev = seq_blocks > 1
    n_in = 5 if has_prev else 3
    n_out = 5 if has_prev else 3

    def body(*refs):
        ins, b_ref, lse_ref, do_ref, dlse_ref = refs[:n_in], refs[n_in], refs[n_in + 1], refs[n_in + 2], refs[n_in + 3]
        outs = refs[n_in + 4:n_in + 4 + n_out]
        ds_ref = refs[n_in + 4 + n_out]
        scratch = refs[n_in + 5 + n_out:]
        staged, do_s, out_s = scratch[:n_in], scratch[n_in], scratch[n_in + 1:]
        first_step = jnp.logical_and(pl.program_id(0) == 0, pl.program_id(1) == 0)
        bd, row, key = _wide_consts()
        valid = _band_valid_wide(pl.program_id(0) % seq_blocks == 0, row, key)
        _stage(list(ins) + [do_ref], list(staged) + [do_s])
        bias_w = jnp.concatenate([b_ref[h] for h in HEADS], axis=1)
        ds = None
        for cls in _step_classes(d):
            q, k, v = _band_operands(staged, d, cls, has_prev)
            kbd, vbd = _block_diag(k, bd), _block_diag(v, bd)
            rows = _class_rows(d, cls)
            do = _take_class(do_s, d, cls).astype(BF16)
            lse_t, dlse_t = lse_ref[rows, :], dlse_ref[rows, :]
            lse_w = _widen([lse_t[:, h:h + 1] for h in HEADS])
            dlse_w = _widen([dlse_t[:, h:h + 1] for h in HEADS])
            p = jnp.where(valid, jnp.exp(_dot(q, kbd, _NT) + bias_w - lse_w), 0.0)
            dp = _dot(do, vbd, _NT)
            ds_c = p * (dp - _widen(_head_rowsums(p * dp)) + dlse_w)
            dsb, pb = ds_c.astype(BF16), p.astype(BF16)
            _put_class(out_s[0], d, cls, _dot(dsb, kbd) * Q_SCALE)
            dk = _fold_heads(_dot(dsb, q, _TN), bd)
            dv = _fold_heads(_dot(pb, do, _TN), bd)
            _put_class(out_s[1], d, cls, dk[BLOCK:])
            _put_class(out_s[2], d, cls, dv[BLOCK:])
            if has_prev:
                _put_class(out_s[3], d, cls, dk[:BLOCK])
                _put_class(out_s[4], d, cls, dv[:BLOCK])
            ds = ds_c if ds is None else ds + ds_c
        _flush(out_s, outs, d)

        @pl.when(first_step)
        def _():
            for h in HEADS:
                ds_ref[h] = ds[:, _seg(h)]

        @pl.when(jnp.logical_not(first_step))
        def _():
            for h in HEADS:
                ds_ref[h] += ds[:, _seg(h)]

    big = pl.BlockSpec((rows_per_block, GROUP_W), lambda tb, r: (tb, 0))
    colb = pl.BlockSpec((rows_per_block, LANES), lambda tb, r: (tb, 0))
    sd = jax.ShapeDtypeStruct
    return pl.pallas_call(
        body, name=name, grid=(T // rows_per_block, d // _classes_per_step(d)),
        in_specs=_band_in_specs(d, pattern, has_prev) + [colb, big, colb],
        out_specs=[big] * n_out + [pl.BlockSpec((N_HEADS, BLOCK, BAND), lambda tb, r: (0, 0, 0))],
        out_shape=[sd((T, GROUP_W), F32)] * n_out + [sd((N_HEADS, BLOCK, BAND), F32)],
        scratch_shapes=_halves_scratch(rows_per_block, n_in + 1 + n_out),
        compiler_params=_params("arbitrary", "arbitrary"),
    )(*([qkv] * n_in), bias, lse, do, dlse)


def shift_add(cur, prev, d, *, name):
    rows = BLOCK * d
    nb = cur.shape[0] // rows

    def body(c_ref, p_ref, o_ref):
        keep = (pl.program_id(0) < nb - 1).astype(F32)
        o_ref[...] = c_ref[...] + keep * p_ref[...]

    blk = pl.BlockSpec((rows, GROUP_W), lambda tb: (tb, 0))
    nxt = pl.BlockSpec((rows, GROUP_W), lambda tb: (jnp.minimum(tb + 1, nb - 1), 0))
    return pl.pallas_call(
        body, name=name, grid=(nb,), in_specs=[blk, nxt], out_specs=blk,
        out_shape=jax.ShapeDtypeStruct(cur.shape, F32), compiler_params=_params("parallel"),
    )(cur, prev)


def _pattern_weights(lse_refs, h):
    ls = [r[:, h:h + 1] for r in lse_refs]
    mx = functools.reduce(jnp.maximum, ls)
    es = [jnp.exp(l - mx) for l in ls]
    tot = functools.reduce(lambda a, b: a + b, es)
    return [e / tot for e in es]


def dil_combine_fwd(outs, *, name):
    T = outs[0][0].shape[0]
    n = len(outs)
    tm = 512

    def body(*refs):
        o_refs, l_refs, out_ref = refs[:n], refs[n:2 * n], refs[2 * n]
        for h in range(N_HEADS):
            w = _pattern_weights(l_refs, h)
            acc = w[0] * o_refs[0][:, _hs(h)]
            for p in range(1, n):
                acc = acc + w[p] * o_refs[p][:, _hs(h)]
            out_ref[:, _hs(h)] = acc.astype(BF16)

    big = pl.BlockSpec((tm, GROUP_W), lambda i: (i, 0))
    colb = pl.BlockSpec((tm, LANES), lambda i: (i, 0))
    return pl.pallas_call(
        body, name=name, grid=(T // tm,), in_specs=[big] * n + [colb] * n,
        out_specs=big, out_shape=jax.ShapeDtypeStruct((T, GROUP_W), BF16),
        compiler_params=_params("parallel"),
    )(*[o for o, _ in outs], *[l for _, l in outs])


def dil_combine_bwd(outs, dmixed, *, name):
    T = outs[0][0].shape[0]
    n = len(outs)
    tm = 512

    def body(*refs):
        o_refs, l_refs, do_ref = refs[:n], refs[n:2 * n], refs[2 * n]
        do_refs, dl_refs = refs[2 * n + 1:3 * n + 1], refs[3 * n + 1:]
        for r in dl_refs:
            r[...] = jnp.zeros_like(r)
        for h in range(N_HEADS):
            w = _pattern_weights(l_refs, h)
            do = do_ref[:, _hs(h)]
            dw = [jnp.sum(do * o_refs[p][:, _hs(h)], axis=1, keepdims=True) for p in range(n)]
            mean = functools.reduce(lambda a, b: a + b, [w[p] * dw[p] for p in range(n)])
            for p in range(n):
                do_refs[p][:, _hs(h)] = w[p] * do
                dl_refs[p][:, h:h + 1] = w[p] * (dw[p] - mean)

    big = pl.BlockSpec((tm, GROUP_W), lambda i: (i, 0))
    colb = pl.BlockSpec((tm, LANES), lambda i: (i, 0))
    sd = jax.ShapeDtypeStruct
    res = pl.pallas_call(
        body, name=name, grid=(T // tm,),
        in_specs=[big] * n + [colb] * n + [pl.BlockSpec((tm, GROUP_W), lambda i: (i, 2))],
        out_specs=[big] * n + [colb] * n, out_shape=[sd((T, GROUP_W), F32)] * n + [sd((T, LANES), F32)] * n,
        compiler_params=_params("parallel"),
    )(*[o for o, _ in outs], *[l for _, l in outs], dmixed)
    return list(zip(res[:n], res[n:]))


def dilated_fwd(qkv, bias, tag):
    return [band_fwd(qkv, bias, p, name=f"{tag}_band_fwd{p}") for p in range(len(DILATIONS))]


def dilated_bwd(qkv, bias, outs, dmixed, tag):
    grads = dil_combine_bwd(outs, dmixed, name=f"{tag}_combine_bwd")
    parts, ds_all = [], []
    for p, d in enumerate(DILATIONS):
        (_, lse), (do, dlse) = outs[p], grads[p]
        res = band_bwd(qkv, bias, lse, do, dlse, p, name=f"{tag}_band_bwd{p}")
        dq, dk, dv, ds = res[0], res[1], res[2], res[-1]
        if len(res) > 4:
            dk = shift_add(dk, res[3], d, name=f"{tag}_dk{p}")
            dv = shift_add(dv, res[4], d, name=f"{tag}_dv{p}")
        parts.append([dq, dk, dv])
        ds_all.append(ds)
    return parts, jnp.concatenate(ds_all, axis=0)


def assemble_dqkv(d_sb, d_fox, d_dil, *, name):
    T = d_sb[0].shape[0]
    tr = 512
    n_pat = len(d_dil)
    flat = list(d_sb) + list(d_fox) + [a for part in d_dil for a in part]

    def body(*refs):
        o_ref = refs[-1]
        for j in range(6):
            o_ref[:, j * GROUP_W:(j + 1) * GROUP_W] = refs[j][...].astype(BF16)
        for j in range(3):
            acc = refs[6 + j][...]
            for p in range(1, n_pat):
                acc = acc + refs[6 + 3 * p + j][...]
            o_ref[:, (6 + j) * GROUP_W:(7 + j) * GROUP_W] = acc.astype(BF16)

    blk = pl.BlockSpec((tr, GROUP_W), lambda i: (i, 0))
    return pl.pallas_call(
        body, name=name, grid=(T // tr,), in_specs=[blk] * len(flat),
        out_specs=pl.BlockSpec((tr, QKV_BLOCKS * GROUP_W), lambda i: (i, 0)),
        out_shape=jax.ShapeDtypeStruct((T, QKV_BLOCKS * GROUP_W), BF16), compiler_params=_params("parallel"),
    )(*flat)


def sum_cast(arrs, dtype, *, name):
    R, C = arrs[0].shape
    tr = _largest_tile(R, 512, 16)
    n = len(arrs)

    def body(*refs):
        acc = refs[0][...].astype(F32)
        for r in refs[1:n]:
            acc = acc + r[...].astype(F32)
        refs[n][...] = acc.astype(dtype)

    blk = pl.BlockSpec((tr, C), lambda i: (i, 0))
    return pl.pallas_call(
        body, name=name, grid=(R // tr,), in_specs=[blk] * n, out_specs=blk, out_shape=jax.ShapeDtypeStruct((R, C), dtype),
        compiler_params=_params("parallel"),
    )(*arrs)


GRAD_WIRE = BF16


def _block_diag_halves(w):
    z = jnp.zeros((HEAD_DIM, HEAD_DIM), w.dtype)
    half = lambda a, b: jnp.concatenate([jnp.concatenate([a, z], axis=1), jnp.concatenate([z, b], axis=1)], axis=0)
    return jnp.stack([half(w[0], w[1]), half(w[2], w[3])]).astype(BF16)


def _diag_blocks(d):
    h = HEAD_DIM
    return jnp.stack([d[0, :h, :h], d[0, h:, h:], d[1, :h, :h], d[1, h:, h:]])


def layer_fwd(x, mem2d, W, P, bias, tag):
    s = {}
    s["x"] = x
    h1 = rmsnorm_fwd(x, P["norm_mix_g"], name=f"{tag}_norm_mix")
    qkv = matmul(h1, W["qkv"], out_dtype=BF16, name=f"{tag}_qkv")
    aux = matmul(h1, W["aux"], name=f"{tag}_aux")
    o_sb = sbw_fwd(qkv, name=f"{tag}_sb_fwd")
    cumc = fox_prep(aux, P["bf"], name=f"{tag}_fox_prep")
    cumr = col_to_row(cumc)
    o_fox, o_fox32, lse_fox = foxw_fwd(qkv, cumc, cumr, name=f"{tag}_fox_fwd")
    dil = dilated_fwd(qkv, bias, tag)
    o_dil = dil_combine_fwd(dil, name=f"{tag}_dil_combine")
    o_lru, h_lru = lru_fwd(aux, P["lru_conv_w"], P["lru_conv_b"], P["wa"], P["lru_b_a"], P["wx"], P["lru_b_x"],
                           P["lru_lambda"], name=f"{tag}_lru_fwd")
    mixed = jnp.concatenate([o_sb, o_fox, o_dil, o_lru], axis=1)
    if "rest" in W:
        W.update(W.pop("rest")(mixed))
    x1 = matmul(mixed, W["out"], residual=x, name=f"{tag}_out")
    hq = rmsnorm_fwd(x1, P["norm_cross_g"], name=f"{tag}_norm_cross")
    qc = matmul(hq, W["cq"], out_dtype=BF16, name=f"{tag}_cq")
    memn = rmsnorm_fwd(mem2d, P["norm_mem_g"], name=f"{tag}_norm_mem")
    kv = matmul(memn, W["ckv"], out_dtype=BF16, name=f"{tag}_ckv")
    oc = cross_fwd(qc, kv, name=f"{tag}_cross_fwd")
    x2 = matmul(oc, W["coT"], trans_b=True, residual=x1, name=f"{tag}_co")
    h2 = rmsnorm_fwd(x2, P["norm_ffn_g"], name=f"{tag}_norm_ffn")
    hu = matmul(h2, W["up_u"], trans_b=True, name=f"{tag}_up_u")
    hg = matmul(h2, W["up_g"], trans_b=True, name=f"{tag}_up_g")
    act = glu_fwd(hu, hg, P["wu"], P["wg"], P["bu"], P["bg"], name=f"{tag}_glu_fwd")
    x3 = matmul(act, W["down"], residual=x2, name=f"{tag}_down")
    s.update(h1=h1, qkv=qkv, aux=aux, cumc=cumc, cumr=cumr, lse_fox=lse_fox, o_fox32=o_fox32, dil=dil, h_lru=h_lru, mixed=mixed,
             x1=x1, hq=hq, qc=qc, memn=memn, kv=kv, oc=oc, x2=x2, h2=h2, hu=hu, hg=hg, act=act)
    return x3, s


def layer_bwd(dx3, mem2d, W, P, bias, s, tag, hooks=None):
    mm = functools.partial(matmul, out_dtype=GRAD_WIRE, trans_a=True)
    gW, gP = {}, {}
    hooks = hooks or {}
    dact = matmul(dx3, W["down"], trans_b=True, name=f"{tag}_d_act")
    gW["down"] = mm(s["act"], dx3, name=f"{tag}_g_down")
    dhu, dhg, dwu, dwg, dbu, dbg = glu_bwd(s["hu"], s["hg"], dact, P["wu"], P["wg"], P["bu"], P["bg"], name=f"{tag}_glu_bwd")
    gP["ffn_conv_w"] = jnp.concatenate([dwu, dwg], axis=1)
    gP["ffn_conv_b"] = jnp.concatenate([dbu, dbg], axis=1)
    dh2 = matmul(dhu, W["up_u"], name=f"{tag}_d_h2u")
    dh2 = matmul(dhg, W["up_g"], residual=dh2, name=f"{tag}_d_h2g")
    gW["up_u"] = mm(dhu, s["h2"], name=f"{tag}_g_up_u")
    gW["up_g"] = mm(dhg, s["h2"], name=f"{tag}_g_up_g")
    dx2, gP["norm_ffn_g"] = rmsnorm_bwd(s["x2"], P["norm_ffn_g"], dh2, dx3, name=f"{tag}_norm_ffn_bwd")
    if "ffn" in hooks:
        hooks["ffn"](gW, W, s)
    doc = matmul(dx2, W["coT"], name=f"{tag}_d_oc")
    gW["coT"] = mm(dx2, s["oc"], name=f"{tag}_g_co")
    dqc, dkv = cross_bwd(s["qc"], s["kv"], doc, name=f"{tag}_cross_bwd")
    dhq = matmul(dqc, W["cq"], trans_b=True, name=f"{tag}_d_hq")
    gW["cq"] = mm(s["hq"], dqc, name=f"{tag}_g_cq")
    dmemn = matmul(dkv, W["ckv"], trans_b=True, name=f"{tag}_d_memn")
    gW["ckv"] = mm(s["memn"], dkv, name=f"{tag}_g_ckv")
    _, gP["norm_mem_g"] = rmsnorm_bwd(mem2d, P["norm_mem_g"], dmemn, None, name=f"{tag}_norm_mem_bwd")
    dx1, gP["norm_cross_g"] = rmsnorm_bwd(s["x1"], P["norm_cross_g"], dhq, dx2, name=f"{tag}_norm_cross_bwd")
    dmixed = matmul(dx1, W["out"], trans_b=True, name=f"{tag}_d_mixed")
    gW["out"] = mm(s["mixed"], dx1, name=f"{tag}_g_out")
    if "mid" in hooks:
        hooks["mid"](gW, W, s)
    qkv, aux = s["qkv"], s["aux"]
    d_sb = sbw_bwd(qkv, dmixed, name=f"{tag}_sb_bwd")
    dfq, dfk, dfv, dcc, dcr = foxw_bwd(qkv, s["cumc"], s["cumr"], s["lse_fox"], s["o_fox32"], dmixed, name=f"{tag}_fox_bwd")
    dcum = sum_cast([dcc, row_to_col(dcr)], F32, name=f"{tag}_dcum")
    df, dbf = fox_prep_bwd(aux, P["bf"], dcum, name=f"{tag}_fox_prep_bwd")
    gP["b_forget"] = dbf[0, :N_HEADS]
    d_dil, ds_band = dilated_bwd(qkv, bias, s["dil"], dmixed, tag)
    dlx, dlg, dcw, dcb, dwa, dba, dwx, dbx, dlam = lru_bwd(
        aux, s["h_lru"], dmixed, P["lru_conv_w"], P["lru_conv_b"], P["wa"], P["lru_b_a"], P["wx"], P["lru_b_x"],
        P["lru_lambda"], name=f"{tag}_lru_bwd")
    gP.update(lru_conv_w=dcw, lru_conv_b=dcb, lru_w_a=_diag_blocks(dwa), lru_b_a=dba, lru_w_x=_diag_blocks(dwx),
              lru_b_x=dbx, lru_lambda=dlam)
    dqkv = assemble_dqkv(d_sb, [dfq, dfk, dfv], d_dil, name=f"{tag}_dqkv")
    daux = jnp.concatenate([dlx, dlg, df], axis=1)
    dh1 = matmul(dqkv, W["qkv"], trans_b=True, name=f"{tag}_d_h1a")
    dh1 = matmul(daux, W["aux"], trans_b=True, residual=dh1, name=f"{tag}_d_h1b")
    gW["qkv"] = mm(s["h1"], dqkv, name=f"{tag}_g_qkv")
    gW["aux"] = mm(s["h1"], daux, name=f"{tag}_g_aux")
    dx, gP["norm_mix_g"] = rmsnorm_bwd(s["x"], P["norm_mix_g"], dh1, dx1, name=f"{tag}_norm_mix_bwd")
    return dx, gW, gP, ds_band


def local_step(x, mem, target, weights_of, Ps, rel_bias, final_norm_g, grads_done=None, bwd_hooks=None):
    B = x.shape[0]
    x2d = x.reshape(B * SEQ, D_MODEL)
    mem2d = mem.reshape(B * N_MEM, D_MODEL)
    bias = relbias_expand(rel_bias, name="relbias_expand")
    saved, Ws = [], []
    h = x2d
    for l in range(DEPTH):
        Ws.append(weights_of(l, h))
        h, s = layer_fwd(h, mem2d, Ws[l], Ps[l], bias, f"l{l}")
        saved.append(s)
    loss, dh, d_final = loss_head(h, final_norm_g, target.reshape(B * SEQ, D_MODEL), name="loss_head")
    gWs, gPs, ds_bands = [None] * DEPTH, [None] * DEPTH, []
    for l in range(DEPTH - 1, -1, -1):
        hooks = None if bwd_hooks is None else bwd_hooks(l)
        dh, gWs[l], gPs[l], ds = layer_bwd(dh, mem2d, Ws[l], Ps[l], bias, saved[l], f"l{l}", hooks)
        if grads_done is not None:
            grads_done(l, gWs[l])
        ds_bands.append(ds)
    d_rel = relbias_reduce(sum_cast([d.reshape(-1, BAND) for d in ds_bands], F32, name="ds_band_sum").reshape(-1, BLOCK, BAND),
                           name="relbias_reduce")
    return loss, dh.reshape(B, SEQ, D_MODEL), gWs, gPs, d_rel, d_final


def small_params(p, l):
    row = lambda name: p[name][l].reshape(1, -1)
    ffn_w, ffn_b = p["ffn_conv_w"][l], row("ffn_conv_b")
    return dict(
        norm_mix_g=row("norm_mix_g"), norm_cross_g=row("norm_cross_g"), norm_mem_g=row("norm_mem_g"), norm_ffn_g=row("norm_ffn_g"),
        bf=jnp.pad(row("b_forget"), ((0, 0), (0, LANES - N_HEADS))),
        lru_conv_w=p["lru_conv_w"][l], lru_conv_b=row("lru_conv_b"), wa=_block_diag_halves(p["lru_w_a"][l]), lru_b_a=row("lru_b_a"),
        wx=_block_diag_halves(p["lru_w_x"][l]), lru_b_x=row("lru_b_x"), lru_lambda=row("lru_lambda"),
        wu=ffn_w[:, :D_FF], wg=ffn_w[:, D_FF:], bu=ffn_b[:, :D_FF], bg=ffn_b[:, D_FF:])


def canonical_weights(w_in, w_out, w_cq, w_ck, w_cv, w_co, w_up, w_down):
    sb_fox, fox_f, rest = w_in[:, :6 * GROUP_W], w_in[:, 6 * GROUP_W:6 * GROUP_W + N_HEADS], w_in[:, 6 * GROUP_W + N_HEADS:]
    dil, lru = rest[:, :3 * GROUP_W], rest[:, 3 * GROUP_W:]
    pad = jnp.zeros((w_in.shape[0], AUX_W - 2 * GROUP_W - N_HEADS), w_in.dtype)
    return dict(qkv=jnp.concatenate([sb_fox, dil], axis=1), aux=jnp.concatenate([lru, fox_f, pad], axis=1), out=w_out,
                cq=w_cq, ckv=jnp.concatenate([w_ck, w_cv], axis=1), coT=w_co.T, upT=w_up.T, down=w_down)


def native_grads(g):
    qkv, aux = g["qkv"], g["aux"]
    a, b = 6 * GROUP_W, 6 * GROUP_W + N_HEADS
    w_in = jnp.zeros((qkv.shape[0], b + 5 * GROUP_W), qkv.dtype)
    w_in = w_in.at[:, :a].set(qkv[:, :a]).at[:, a:b].set(aux[:, 2 * GROUP_W:2 * GROUP_W + N_HEADS])
    w_in = w_in.at[:, b:b + 3 * GROUP_W].set(qkv[:, a:]).at[:, b + 3 * GROUP_W:].set(aux[:, :2 * GROUP_W])
    return (w_in, g["out"], g["cq"], g["ckv"][:, :GROUP_W], g["ckv"][:, GROUP_W:], g["coT"].T) + native_ffn_grads(g)


def native_ffn_grads(g):
    return (g["upT"].T, g["down"])


ANY = pl.BlockSpec(memory_space=pl.ANY)
VMEM_SPEC = pl.BlockSpec(memory_space=pltpu.VMEM)


def _place():
    x, y, c = lax.axis_index("x"), lax.axis_index("y"), lax.axis_index("c")
    other_chips = [(1 - x, y), (x, 1 - y), (1 - x, 1 - y)]
    return x, y, c, other_chips


def _gather_body(x_ref, out_ref, send_sems, recv_sems, local_sem):
    x, y, c, chips = _place()
    me, sibling = (x, y, c), (x, y, 1 - c)

    def slot(px, py, pc):
        return out_ref.at[4 * px + 2 * py + pc]

    def copy(k, block, to, src=None):
        return pltpu.make_async_remote_copy(
            src_ref=slot(*block) if src is None else src, dst_ref=slot(*block),
            send_sem=send_sems.at[k], recv_sem=recv_sems.at[k], device_id=to, device_id_type=MESH)

    if local_sem is not None:
        mine = pltpu.make_async_copy(x_ref, slot(*me), local_sem)
        mine.start()
    first = [copy(0, me, sibling, src=x_ref)]
    first += [copy(1 + j, me, (*chip, c), src=x_ref) for j, chip in enumerate(chips)]
    for cp in first:
        cp.start()
    passed = [copy(4 + j, (*chip, c), sibling) for j, chip in enumerate(chips)]
    for j, chip in enumerate(chips):
        copy(1 + j, (*chip, c), me).wait_recv()
        passed[j].start()
    copy(0, sibling, me).wait_recv()
    for j, chip in enumerate(chips):
        copy(4 + j, (*chip, 1 - c), me).wait_recv()
    for cp in first + passed:
        cp.wait_send()
    if local_sem is not None:
        mine.wait()


_GATHER_SEMS = [pltpu.SemaphoreType.DMA((7,)), pltpu.SemaphoreType.DMA((7,)), pltpu.SemaphoreType.DMA]


def allgather_hbm(shard, me, *, name):
    def body(x_ref, out_ref, done_ref, send_sems, recv_sems):
        _gather_body(x_ref, out_ref, send_sems, recv_sems, None)
        done_ref[...] = jnp.zeros_like(done_ref)

    others, done = pl.pallas_call(
        body, name=name, in_specs=[ANY], out_specs=[ANY, VMEM_SPEC],
        out_shape=[jax.ShapeDtypeStruct((N_DEV,) + shard.shape, shard.dtype), jax.ShapeDtypeStruct((8, LANES), F32)],
        scratch_shapes=_GATHER_SEMS[:2],
    )(shard)
    return lax.dynamic_update_slice(others, shard[None], (me, 0, 0)), done


def allgather_small(x, *, name, reduce=False):
    def body(x_ref, out_ref, second_ref, *sems):
        _gather_body(x_ref, out_ref, *sems)
        if reduce:
            acc = out_ref[0]
            for d in range(1, N_DEV):
                acc = acc + out_ref[d]
            second_ref[...] = acc
        else:
            second_ref[...] = jnp.zeros_like(second_ref)

    sd = jax.ShapeDtypeStruct
    return pl.pallas_call(
        body, name=name, in_specs=[VMEM_SPEC], out_specs=[VMEM_SPEC, VMEM_SPEC],
        out_shape=[sd((N_DEV,) + x.shape, x.dtype), sd(x.shape if reduce else (8, LANES), x.dtype)],
        scratch_shapes=_GATHER_SEMS, compiler_params=pltpu.CompilerParams(vmem_limit_bytes=VMEM_LIMIT_V7X),
    )(x)


N_CHIPS = 4


def pair_exchange(g, *, name):
    _, R, C = g.shape

    def body(g_ref, recv_ref, send_sems, recv_sems):
        x, y, c, _ = _place()
        sibling = (x, y, 1 - c)
        remote = [pltpu.make_async_remote_copy(
            src_ref=g_ref.at[2 * q + (1 - c)], dst_ref=recv_ref.at[q], send_sem=send_sems.at[q], recv_sem=recv_sems.at[q],
            device_id=sibling, device_id_type=MESH) for q in range(N_CHIPS)]
        for cp in remote:
            cp.start()
        for cp in remote:
            cp.wait_recv()
        for cp in remote:
            cp.wait_send()

    return pl.pallas_call(
        body, name=name, in_specs=[ANY], out_specs=ANY, out_shape=jax.ShapeDtypeStruct((N_CHIPS, R, C), g.dtype),
        scratch_shapes=[pltpu.SemaphoreType.DMA((N_CHIPS,))] * 2,
    )(g)


def chip_exchange(s, *, name):
    _, R, C = s.shape

    def body(s_ref, o0, o1, o2, send_sems, recv_sems):
        x, y, c, chips = _place()
        outs = (o0, o1, o2)
        copies = [pltpu.make_async_remote_copy(
            src_ref=s_ref.at[2 * cx + cy], dst_ref=outs[j], send_sem=send_sems.at[j], recv_sem=recv_sems.at[j],
            device_id=(cx, cy, c), device_id_type=MESH) for j, (cx, cy) in enumerate(chips)]
        for cp in copies:
            cp.start()
        for cp in copies:
            cp.wait_recv()
        for cp in copies:
            cp.wait_send()

    sd = jax.ShapeDtypeStruct((R, C), s.dtype)
    return pl.pallas_call(
        body, name=name, in_specs=[ANY], out_specs=[ANY] * 3, out_shape=[sd] * 3,
        scratch_shapes=[pltpu.SemaphoreType.DMA((3,)), pltpu.SemaphoreType.DMA((3,))],
    )(s)


HBM_SPEC = pl.BlockSpec(memory_space=pltpu.HBM)
SEM_SPEC = pl.BlockSpec(memory_space=pltpu.SEMAPHORE)
N_PEERS = N_DEV - 1


def _peers():
    x, y, c = lax.axis_index("x"), lax.axis_index("y"), lax.axis_index("c")
    flip = lambda v, bit: 1 - v if bit else v
    out = []
    for k in range(1, N_DEV):
        px, py, pc = flip(x, (k >> 2) & 1), flip(y, (k >> 1) & 1), flip(c, k & 1)
        out.append(((px, py, pc), 4 * px + 2 * py + pc))
    return out, 4 * x + 2 * y + c


def _peer_copies(src_ref, land_ref, send_sems, recv_sems, scatter, landing):
    peers, me = _peers()
    return [pltpu.make_async_remote_copy(
        src_ref=src_ref.at[idx] if scatter else src_ref, dst_ref=land_ref.at[me if landing == "mine" else idx],
        send_sem=send_sems.at[k], recv_sem=recv_sems.at[k], device_id=peer, device_id_type=MESH)
        for k, (peer, idx) in enumerate(peers)]


def exchange_start(src, scatter, *, name):
    shape = (N_DEV,) + src.shape[-2:]

    def body(src_ref, land_ref, send_sems, recv_sems, src_thru, land_thru, token):
        for cp in _peer_copies(src_ref, land_ref, send_sems, recv_sems, scatter, "mine"):
            cp.start()
        token[...] = jnp.zeros_like(token)

    sems = pltpu.SemaphoreType.DMA((N_PEERS,))
    return pl.pallas_call(
        body, name=name,
        out_shape=(sems, sems, pltpu.HBM(src.shape, src.dtype), pltpu.HBM(shape, src.dtype), jax.ShapeDtypeStruct((8, LANES), F32)),
        in_specs=(HBM_SPEC, HBM_SPEC), out_specs=(SEM_SPEC, SEM_SPEC, HBM_SPEC, HBM_SPEC, VMEM_SPEC),
        input_output_aliases={0: 2, 1: 3},
        compiler_params=pltpu.CompilerParams(has_side_effects=pltpu.SideEffectType.DATAFLOW_SIDE_EFFECTING),
    )(pltpu.with_memory_space_constraint(src, pltpu.HBM), pltpu.with_memory_space_constraint(lax.empty(shape, src.dtype), pltpu.HBM))


def exchange_wait(started, after, scatter, *, name):
    send_sems, recv_sems, src_thru, land_thru, _ = started

    def body(src_ref, land_ref, send_sems, recv_sems, after_ref, src_dead, got_ref):
        for cp in _peer_copies(src_ref, land_ref, send_sems, recv_sems, scatter, "theirs"):
            cp.wait_send()
            cp.wait_recv()

    return pl.pallas_call(
        body, name=name, out_shape=(pltpu.HBM(src_thru.shape, src_thru.dtype), pltpu.HBM(land_thru.shape, land_thru.dtype)),
        in_specs=(HBM_SPEC, HBM_SPEC, SEM_SPEC, SEM_SPEC, ANY), out_specs=(HBM_SPEC, HBM_SPEC), input_output_aliases={0: 0, 1: 1},
        compiler_params=pltpu.CompilerParams(has_side_effects=pltpu.SideEffectType.DATAFLOW_SIDE_EFFECTING),
    )(src_thru, land_thru, send_sems, recv_sems, after)[1]


def sum_blocks(blocks, *, name):
    n, R, C = blocks.shape
    tr = _largest_tile(R, 512, 16)

    def body(b_ref, o_ref):
        acc = b_ref[0].astype(F32)
        for d in range(1, n):
            acc = acc + b_ref[d].astype(F32)
        o_ref[...] = acc

    return pl.pallas_call(
        body, name=name, grid=(R // tr,),
        in_specs=[pl.BlockSpec((n, tr, C), lambda i: (0, i, 0))], out_specs=pl.BlockSpec((tr, C), lambda i: (i, 0)),
        out_shape=jax.ShapeDtypeStruct((R, C), F32), compiler_params=_params("parallel"),
    )(blocks)


WEIGHTS = ("norm_mix_g", "w_in", "b_forget", "lru_conv_w", "lru_conv_b", "lru_w_a", "lru_b_a", "lru_w_x", "lru_b_x", "lru_lambda",
           "w_out", "norm_cross_g", "norm_mem_g", "w_cq", "w_ck", "w_cv", "w_co", "norm_ffn_g", "w_up", "ffn_conv_w", "ffn_conv_b",
           "w_down", "rel_bias", "final_norm_g")
LARGE = ("w_in", "w_out", "w_cq", "w_ck", "w_cv", "w_co", "w_up", "w_down")
COLUMN_SPLIT_SMALL = ("lru_conv_w", "ffn_conv_w")
PACK = (("qkv", 128, 2304), ("aux", 128, 640), ("out", 128, 1024), ("cq", 128, 256), ("ckv", 128, 512), ("coT", 128, 256),
        ("upT", 704, 1024), ("down", 352, 1024))
PACK_W = 1024


def _pack_rows(parts):
    return jnp.concatenate([p.reshape(-1, PACK_W) for p in parts], axis=0)


def _pad_rows(flat, mult=8 * LANES):
    n = flat.shape[0]
    return jnp.pad(flat, (0, (-n) % mult)).reshape(-1, LANES)


def kernel(x, mem, norm_mix_g, w_in, b_forget, lru_conv_w, lru_conv_b, lru_w_a, lru_b_a, lru_w_x, lru_b_x, lru_lambda, w_out, norm_cross_g, norm_mem_g, w_cq, w_ck, w_cv, w_co, norm_ffn_g, w_up, ffn_conv_w, ffn_conv_b, w_down, rel_bias, final_norm_g, loss_target, m_norm_mix_g, m_w_in, m_b_forget, m_lru_conv_w, m_lru_conv_b, m_lru_w_a, m_lru_b_a, m_lru_w_x, m_lru_b_x, m_lru_lambda, m_w_out, m_norm_cross_g, m_norm_mem_g, m_w_cq, m_w_ck, m_w_cv, m_w_co, m_norm_ffn_g, m_w_up, m_ffn_conv_w, m_ffn_conv_b, m_w_down, m_rel_bias, m_final_norm_g, v_norm_mix_g, v_w_in, v_b_forget, v_lru_conv_w, v_lru_conv_b, v_lru_w_a, v_lru_b_a, v_lru_w_x, v_lru_b_x, v_lru_lambda, v_w_out, v_norm_cross_g, v_norm_mem_g, v_w_cq, v_w_ck, v_w_cv, v_w_co, v_norm_ffn_g, v_w_up, v_ffn_conv_w, v_ffn_conv_b, v_w_down, v_rel_bias, v_final_norm_g):
    w = dict(norm_mix_g=norm_mix_g, w_in=w_in, b_forget=b_forget, lru_conv_w=lru_conv_w, lru_conv_b=lru_conv_b, lru_w_a=lru_w_a,
             lru_b_a=lru_b_a, lru_w_x=lru_w_x, lru_b_x=lru_b_x, lru_lambda=lru_lambda, w_out=w_out, norm_cross_g=norm_cross_g,
             norm_mem_g=norm_mem_g, w_cq=w_cq, w_ck=w_ck, w_cv=w_cv, w_co=w_co, norm_ffn_g=norm_ffn_g, w_up=w_up,
             ffn_conv_w=ffn_conv_w, ffn_conv_b=ffn_conv_b, w_down=w_down, rel_bias=rel_bias, final_norm_g=final_norm_g)
    m = dict(norm_mix_g=m_norm_mix_g, w_in=m_w_in, b_forget=m_b_forget, lru_conv_w=m_lru_conv_w, lru_conv_b=m_lru_conv_b,
             lru_w_a=m_lru_w_a, lru_b_a=m_lru_b_a, lru_w_x=m_lru_w_x, lru_b_x=m_lru_b_x, lru_lambda=m_lru_lambda, w_out=m_w_out,
             norm_cross_g=m_norm_cross_g, norm_mem_g=m_norm_mem_g, w_cq=m_w_cq, w_ck=m_w_ck, w_cv=m_w_cv, w_co=m_w_co,
             norm_ffn_g=m_norm_ffn_g, w_up=m_w_up, ffn_conv_w=m_ffn_conv_w, ffn_conv_b=m_ffn_conv_b, w_down=m_w_down,
             rel_bias=m_rel_bias, final_norm_g=m_final_norm_g)
    v = dict(norm_mix_g=v_norm_mix_g, w_in=v_w_in, b_forget=v_b_forget, lru_conv_w=v_lru_conv_w, lru_conv_b=v_lru_conv_b,
             lru_w_a=v_lru_w_a, lru_b_a=v_lru_b_a, lru_w_x=v_lru_w_x, lru_b_x=v_lru_b_x, lru_lambda=v_lru_lambda, w_out=v_w_out,
             norm_cross_g=v_norm_cross_g, norm_mem_g=v_norm_mem_g, w_cq=v_w_cq, w_ck=v_w_ck, w_cv=v_w_cv, w_co=v_w_co,
             norm_ffn_g=v_norm_ffn_g, w_up=v_w_up, ffn_conv_w=v_ffn_conv_w, ffn_conv_b=v_ffn_conv_b, w_down=v_w_down,
             rel_bias=v_rel_bias, final_norm_g=v_final_norm_g)
    me = 4 * lax.axis_index("x") + 2 * lax.axis_index("y") + lax.axis_index("c")

    conv_shard = jnp.concatenate([w[n].reshape(-1) for n in COLUMN_SPLIT_SMALL])
    conv_all, conv_gathered = allgather_small(_pad_rows(conv_shard), name="gather_conv")
    conv_all = conv_all.reshape(N_DEV, -1)
    full = dict(w)
    off = 0
    for n in COLUMN_SPLIT_SMALL:
        d, k, c = w[n].shape
        blocks = conv_all[:, off:off + d * k * c].reshape(N_DEV, d, k, c)
        full[n] = blocks.transpose(1, 2, 0, 3).reshape(d, k, N_DEV * c)
        off += d * k * c

    IN, MID, FFN = PACK[:2], PACK[2:6], PACK[6:]
    REST = MID + FFN

    def packed_shard(l, group):
        canon = canonical_weights(*[w[n][l] for n in LARGE])
        return _pack_rows([canon[k].astype(BF16) for k, _, _ in group])

    def unpack_weights(packed, group):
        W, row = {}, 0
        half = N_DEV // 2
        for k, r, c in group:
            n_rows = r * c // PACK_W
            rows = packed[:, row:row + n_rows]
            if k == "upT":
                W["up_u"], W["up_g"] = rows[:half].reshape(half * r, c), rows[half:].reshape(half * r, c)
            else:
                W[k] = rows.reshape(N_DEV * r, c)
            row += n_rows
        return W

    def packed_grads(gW, group):
        g = dict(gW)
        if "up_u" in g:
            g["upT"] = jnp.concatenate([g.pop("up_u"), g.pop("up_g")], axis=0)
        return jnp.concatenate([g[k].reshape(N_DEV, r * c // PACK_W, PACK_W) for k, r, c in group], axis=1)

    def unpack_grads(shard_sum, group):
        g, row = {}, 0
        for k, r, c in group:
            n_rows = r * c // PACK_W
            g[k] = shard_sum[row:row + n_rows].reshape(r, c)
            row += n_rows
        return g

    def own_block_in(landed, block):
        return lax.dynamic_update_slice(landed, block[None], (me, 0, 0))

    def gathered_weights(copies, shard, after, group, name):
        return unpack_weights(own_block_in(exchange_wait(copies, after, False, name=name), shard), group)

    def scattered_sum(src, copies, after, tag):
        landed = exchange_wait(copies, after, True, name=f"{tag}_wait")
        mine = lax.dynamic_index_in_dim(src, me, axis=0, keepdims=False)
        return sum_blocks(own_block_in(landed, mine), name=f"{tag}_sum")

    last = DEPTH - 1
    in0, gathered = allgather_hbm(packed_shard(0, IN) + conv_gathered[0, 0].astype(BF16), me, name="gather_weights")
    rest0_shard = packed_shard(0, REST) + gathered[0, 0].astype(BF16)
    gather_rest0 = exchange_start(rest0_shard, False, name="gather_rest0_start")
    last_shard = packed_shard(last, PACK) + gather_rest0[4][0, 0].astype(BF16)
    gather_last = exchange_start(last_shard, False, name="gather_last_start")
    started = gather_last[4][0, 0]
    layer_weights = {}

    def weights_of(l, h):
        if l == 0:
            W = unpack_weights(in0, IN)
            W["rest"] = lambda after: gathered_weights(gather_rest0, rest0_shard, after, REST, "gather_rest0_wait")
        else:
            assert l == last
            W = gathered_weights(gather_last, last_shard, h, PACK, "gather_last_wait")
        layer_weights[l] = W
        return W

    in_flight = {}

    def scatter(key, g_all, name):
        in_flight[key] = (g_all, exchange_start(g_all, True, name=name))
        return in_flight[key][1][4][0, 0].astype(BF16)

    def grads_done(l, gW):
        if l == last:
            W0 = layer_weights[0]
            W0["down"] = W0["down"] + scatter("last", packed_grads(gW, PACK), "grads_last_start")

    def ffn0_grads_done(gW, W, s):
        W["coT"] = W["coT"] + scatter("ffn0", packed_grads({k: gW[k] for k in ("up_u", "up_g", "down")}, FFN), "grads_ffn0_start")

    def mid0_grads_done(gW, W, s):
        s["cumc"] = s["cumc"] + scatter("mid0", packed_grads({k: gW[k] for k, _, _ in MID}, MID), "grads_mid0_start").astype(F32)

    Ps = [small_params(full, l) for l in range(DEPTH)]
    Ps[0]["norm_mix_g"] = Ps[0]["norm_mix_g"] + started
    loss, grad_x, gWs, gPs, d_rel, d_final = local_step(
        x, mem, loss_target, weights_of, Ps, rel_bias, final_norm_g.reshape(1, -1), grads_done,
        lambda l: {"ffn": ffn0_grads_done, "mid": mid0_grads_done} if l == 0 else None)

    shard_grads = {last: unpack_grads(scattered_sum(*in_flight["last"], grad_x, "grads_last"), PACK)}
    shard_grads[0] = unpack_grads(scattered_sum(*in_flight["ffn0"], grad_x, "grads_ffn0"), FFN)
    shard_grads[0].update(unpack_grads(scattered_sum(*in_flight["mid0"], grad_x, "grads_mid0"), MID))

    g_all = packed_grads({k: gWs[0][k] for k, _, _ in IN}, IN)
    rows = g_all.shape[1]
    got = pair_exchange(g_all, name="grads_pair_exchange")
    own = lax.dynamic_index_in_dim(g_all.reshape(N_CHIPS, 2, rows, PACK_W), lax.axis_index("c"), axis=1, keepdims=False)
    pair = sum_cast([own.reshape(-1, PACK_W), got.reshape(-1, PACK_W)], GRAD_WIRE, name="grads_pair_sum").reshape(N_CHIPS, rows, PACK_W)
    from_x, from_y, from_xy = chip_exchange(pair, name="grads_chip_exchange")
    mine = lax.dynamic_index_in_dim(pair, 2 * lax.axis_index("x") + lax.axis_index("y"), axis=0, keepdims=False)
    shard_grads[0].update(unpack_grads(sum_cast([mine, from_x, from_y, from_xy], F32, name="grads_chip_sum"), IN))

    grads = {}
    per_layer = [native_grads(shard_grads[l]) for l in range(DEPTH)]
    for i, n in enumerate(LARGE):
        grads[n] = jnp.stack([per_layer[l][i] for l in range(DEPTH)])

    small_names = [n for n in WEIGHTS if n not in LARGE and n not in ("rel_bias", "final_norm_g")]
    pieces = [gPs[l][n].reshape(-1) for n in small_names for l in range(DEPTH)] + [d_rel.reshape(-1), d_final.reshape(-1), loss[0, :1]]
    sizes = [p.shape[0] for p in pieces]
    _, total = allgather_small(_pad_rows(jnp.concatenate(pieces)), name="allreduce_small", reduce=True)
    total = total.reshape(-1)
    off, it = 0, iter(sizes)
    for n in small_names:
        per = []
        for l in range(DEPTH):
            sz = next(it)
            per.append(total[off:off + sz])
            off += sz
        full_shape = (DEPTH,) + full[n].shape[1:]
        gfull = jnp.stack(per).reshape(full_shape)
        if n in COLUMN_SPLIT_SMALL:
            c = w[n].shape[-1]
            gfull = lax.dynamic_slice_in_dim(gfull, me * c, c, axis=gfull.ndim - 1)
        grads[n] = gfull
    grads["rel_bias"] = total[off:off + rel_bias.size].reshape(rel_bias.shape)
    off += rel_bias.size
    grads["final_norm_g"] = total[off:off + D_MODEL]
    off += D_MODEL
    loss_out = total[off]

    delta, new_m, new_v = {}, {}, {}
    for n in LARGE:
        shape = w[n].shape
        two_d = lambda a: a.reshape(-1, shape[-1])
        d_, m_, v_ = adamw(two_d(w[n]), two_d(grads[n]), two_d(m[n]), two_d(v[n]), name=f"adamw_{n}")
        delta[n], new_m[n], new_v[n] = d_.reshape(shape), m_.reshape(shape), v_.reshape(shape)
    small_all = [n for n in WEIGHTS if n not in LARGE]
    two_d = lambda a: a.reshape(-1, a.shape[-1])
    d_, m_, v_ = adamw_many(*[[two_d(src[n]) for n in small_all] for src in (w, grads, m, v)], name="adamw_small")
    for i, n in enumerate(small_all):
        delta[n], new_m[n], new_v[n] = (a[i].reshape(w[n].shape) for a in (d_, m_, v_))

    return (loss_out, grad_x, *[grads[n] for n in WEIGHTS], *[delta[n] for n in WEIGHTS], *[new_m[n] for n in WEIGHTS],
            *[new_v[n] for n in WEIGHTS])
```

```python
import functools
import math

import numpy as np
import jax
import jax.numpy as jnp
from jax import lax
from jax.experimental import pallas as pl
from jax.experimental.pallas import tpu as pltpu

F32 = jnp.float32
BF16 = jnp.bfloat16
MESH = pl.DeviceIdType.MESH

N_DEV = 8
D_MODEL = 1024
SEQ = 2048
DEPTH = 2
HEAD_DIM = 64
N_HEADS = 4
GROUP_W = N_HEADS * HEAD_DIM
D_FF = 2816
N_MEM = 256
NUM_BUCKETS = 32
MAX_DISTANCE = 2048
BLOCK = 128
DILATIONS = (1, 4, 16)
EPS = 1e-6
LRU_C = 8.0
Q_SCALE = HEAD_DIM ** -0.5
AUX_W = 640
LRU_HALF_W = 128
LRU_HALVES = GROUP_W // LRU_HALF_W
ADAM_LR, ADAM_B1, ADAM_B2, ADAM_EPS, ADAM_WD, ADAM_STEP = 0.001, 0.9, 0.999, 1e-08, 0.01, 10

VMEM_LIMIT_V7X = 48 * 1024 * 1024


def _params(*sem):
    return pltpu.CompilerParams(dimension_semantics=sem if sem else None, vmem_limit_bytes=VMEM_LIMIT_V7X)


def _pick(n, cands):
    for c in cands:
        if n % c == 0:
            return c
    return n


def _largest_tile(n, cap, align):
    best = None
    for t in range(align, min(n, cap) + 1, align):
        if n % t == 0:
            best = t
    return n if best is None else best


def matmul(a, b, *, name, trans_a=False, trans_b=False, out_dtype=F32, residual=None):
    (K, M) = a.shape if trans_a else a.shape[::-1]
    (N, Kb) = b.shape if trans_b else b.shape[::-1]
    assert K == Kb, (a.shape, b.shape)
    tm = _largest_tile(M, 1408 if trans_a else (1024 if K <= 1024 else 512), 128)
    tn = _largest_tile(N, 1408, 128)
    tk = _largest_tile(K, 1024 if trans_a else 2816, 128)
    nk = K // tk
    a_spec = pl.BlockSpec((tk, tm), lambda i, j, k: (k, i)) if trans_a else pl.BlockSpec((tm, tk), lambda i, j, k: (i, k))
    b_spec = pl.BlockSpec((tn, tk), lambda i, j, k: (j, k)) if trans_b else pl.BlockSpec((tk, tn), lambda i, j, k: (k, j))
    o_spec = pl.BlockSpec((tm, tn), lambda i, j, k: (i, j))
    dims = (((0 if trans_a else 1,), (1 if trans_b else 0,)), ((), ()))
    has_res = residual is not None

    def body(*refs):
        a_ref, b_ref = refs[0], refs[1]
        r_ref = refs[2] if has_res else None
        part = lax.dot_general(a_ref[...].astype(BF16), b_ref[...].astype(BF16), dims, preferred_element_type=F32)
        if nk == 1:
            if has_res:
                part = part + r_ref[...].astype(F32)
            refs[-1][...] = part.astype(out_dtype)
            return
        o_ref, acc_ref = refs[-2], refs[-1]
        k = pl.program_id(2)

        @pl.when(k == 0)
        def _():
            acc_ref[...] = part

        @pl.when(k > 0)
        def _():
            acc_ref[...] += part

        @pl.when(k == nk - 1)
        def _():
            r = acc_ref[...]
            if has_res:
                r = r + r_ref[...].astype(F32)
            o_ref[...] = r.astype(out_dtype)

    ops = (a, b) + ((residual,) if has_res else ())
    return pl.pallas_call(
        body, name=name, grid=(M // tm, N // tn, nk),
        in_specs=[a_spec, b_spec] + ([o_spec] if has_res else []),
        out_specs=o_spec, out_shape=jax.ShapeDtypeStruct((M, N), out_dtype),
        scratch_shapes=[pltpu.VMEM((tm, tn), F32)] if nk > 1 else [],
        compiler_params=_params("parallel", "parallel", "arbitrary"),
    )(*ops)


def rmsnorm_fwd(x, g, *, name):
    R, D = x.shape
    tr = _pick(R, (512, 256))

    def body(x_ref, g_ref, o_ref):
        xv = x_ref[...]
        r = lax.rsqrt(jnp.mean(xv * xv, axis=-1, keepdims=True) + EPS)
        o_ref[...] = (xv * r * g_ref[...]).astype(BF16)

    return pl.pallas_call(
        body, name=name, grid=(R // tr,),
        in_specs=[pl.BlockSpec((tr, D), lambda i: (i, 0)), pl.BlockSpec((1, D), lambda i: (0, 0))],
        out_specs=pl.BlockSpec((tr, D), lambda i: (i, 0)), out_shape=jax.ShapeDtypeStruct((R, D), BF16),
        compiler_params=_params("parallel"),
    )(x, g)


def rmsnorm_bwd(x, g, dh, dres, *, name):
    R, D = x.shape
    tr = _pick(R, (512, 256))
    has_res = dres is not None

    def body(*refs):
        x_ref, g_ref, dh_ref = refs[:3]
        dx_ref, dg_ref = refs[-2], refs[-1]
        xv = x_ref[...]
        r = lax.rsqrt(jnp.mean(xv * xv, axis=-1, keepdims=True) + EPS)
        n = xv * r
        dhv = dh_ref[...]
        dn = dhv * g_ref[...]
        dx = r * (dn - n * jnp.mean(dn * n, axis=-1, keepdims=True))
        if has_res:
            dx = dx + refs[3][...]
        dx_ref[...] = dx
        part = jnp.sum(dhv * n, axis=0, keepdims=True)

        @pl.when(pl.program_id(0) == 0)
        def _():
            dg_ref[...] = part

        @pl.when(pl.program_id(0) > 0)
        def _():
            dg_ref[...] += part

    row = pl.BlockSpec((tr, D), lambda i: (i, 0))
    vec = pl.BlockSpec((1, D), lambda i: (0, 0))
    ops = (x, g, dh) + ((dres,) if has_res else ())
    return pl.pallas_call(
        body, name=name, grid=(R // tr,),
        in_specs=[row, vec, row] + ([row] if has_res else []),
        out_specs=[row, vec],
        out_shape=[jax.ShapeDtypeStruct((R, D), F32), jax.ShapeDtypeStruct((1, D), F32)],
        compiler_params=_params("arbitrary"),
    )(*ops)


_SQRT_HALF = 0.7071067811865476
_INV_SQRT_2PI = 0.3989422804014327


def _normal_cdf_pdf(x):
    ax = jnp.abs(x) * _SQRT_HALF
    t = 1.0 / (1.0 + 0.3275911 * ax)
    poly = t * (0.254829592 + t * (-0.284496736 + t * (1.421413741 + t * (-1.453152027 + t * 1.061405429))))
    e = jnp.exp(-0.5 * x * x)
    half_tail = 0.5 * poly * e
    return jnp.where(x < 0, half_tail, 1.0 - half_tail), e


def _gelu_cdf(x):
    return _normal_cdf_pdf(x)[0]


def _gelu_and_grad(x):
    cdf, e = _normal_cdf_pdf(x)
    return x * cdf, cdf + x * _INV_SQRT_2PI * e


def _shift_down(main, halo, first, shifts):
    halo = jnp.where(first, 0.0, halo)
    ext = jnp.concatenate([halo, main], axis=0)
    return [pltpu.roll(ext, s, 0)[8:] for s in shifts]


def _conv3(main, halo, first, w, b):
    m1, m2 = _shift_down(main, halo, first, (1, 2))
    return ((b + w[0:1] * m2) + w[1:2] * m1) + w[2:3] * main, m1, m2


def glu_fwd(hu, hg, wu, wg, bu, bg, *, name):
    T, F = hu.shape
    tm, tf = 512, _largest_tile(F, 704, 128)
    hb = tm // 8
    blocks_per_example = SEQ // tm

    def body(hu_ref, hg_ref, hau_ref, hag_ref, wu_ref, wg_ref, bu_ref, bg_ref, o_ref):
        first = pl.program_id(0) % blocks_per_example == 0
        up, _, _ = _conv3(hu_ref[...], hau_ref[...], first, wu_ref[...], bu_ref[...])
        gate, _, _ = _conv3(hg_ref[...], hag_ref[...], first, wg_ref[...], bg_ref[...])
        o_ref[...] = (gate * _gelu_cdf(gate) * up).astype(BF16)

    main = pl.BlockSpec((tm, tf), lambda i, j: (i, j))
    halo = pl.BlockSpec((8, tf), lambda i, j: (jnp.maximum(i * hb - 1, 0), j))
    w3 = pl.BlockSpec((3, tf), lambda i, j: (0, j))
    b1 = pl.BlockSpec((1, tf), lambda i, j: (0, j))
    return pl.pallas_call(
        body, name=name, grid=(T // tm, F // tf),
        in_specs=[main, main, halo, halo, w3, w3, b1, b1],
        out_specs=main, out_shape=jax.ShapeDtypeStruct((T, F), BF16),
        compiler_params=_params("parallel", "parallel"),
    )(hu, hg, hu, hg, wu, wg, bu, bg)


def glu_bwd(hu, hg, dact, wu, wg, bu, bg, *, name):
    T, F = hu.shape
    tm, tf = 512, _largest_tile(F, 704, 128)
    hb = tm // 8
    blocks_per_example = SEQ // tm
    n_halo_blocks = T // 8
    n_ext = tm + 8

    def body(hu_ref, hg_ref, hau_ref, hag_ref, hnu_ref, hng_ref, da_ref, dan_ref, wu_ref, wg_ref, bu_ref, bg_ref,
             du_ref, dg_ref, dwu_ref, dwg_ref, dbu_ref, dbg_ref):
        i = pl.program_id(1)
        first = i % blocks_per_example == 0
        last = i % blocks_per_example == blocks_per_example - 1
        wu, wg = wu_ref[...], wg_ref[...]

        def conv_ext(main_ref, prev_ref, next_ref, w, b):
            ext = jnp.concatenate([jnp.where(first, 0.0, prev_ref[...]), main_ref[...], next_ref[...]], axis=0)
            x0, x1, x2 = ext[8:], pltpu.roll(ext, 1, 0)[8:], pltpu.roll(ext, 2, 0)[8:]
            return ((b + w[0:1] * x2) + w[1:2] * x1) + w[2:3] * x0, x0, x1, x2

        up, xu, u1, u2 = conv_ext(hu_ref, hau_ref, hnu_ref, wu, bu_ref[...])
        gate, xg, g1, g2 = conv_ext(hg_ref, hag_ref, hng_ref, wg, bg_ref[...])
        act, dact_dgate = _gelu_and_grad(gate)
        da = jnp.concatenate([da_ref[...], jnp.where(last, 0.0, dan_ref[...])], axis=0)
        dup = da * act
        dgate = da * up * dact_dgate

        def conv_t(d, w):
            return (w[2:3] * d[:tm] + w[1:2] * pltpu.roll(d, n_ext - 1, 0)[:tm] + w[0:1] * pltpu.roll(d, n_ext - 2, 0)[:tm]).astype(BF16)

        du_ref[...] = conv_t(dup, wu)
        dg_ref[...] = conv_t(dgate, wg)

        def sums(d, x0, x1, x2):
            s = lambda v: jnp.sum(v[:tm], axis=0, keepdims=True)
            return jnp.concatenate([s(d * x2), s(d * x1), s(d * x0)], axis=0), s(d)

        pwu, pbu = sums(dup, xu, u1, u2)
        pwg, pbg = sums(dgate, xg, g1, g2)

        @pl.when(i == 0)
        def _():
            dwu_ref[...] = pwu
            dwg_ref[...] = pwg
            dbu_ref[...] = pbu
            dbg_ref[...] = pbg

        @pl.when(i > 0)
        def _():
            dwu_ref[...] += pwu
            dwg_ref[...] += pwg
            dbu_ref[...] += pbu
            dbg_ref[...] += pbg

    main = pl.BlockSpec((tm, tf), lambda j, i: (i, j))
    before = pl.BlockSpec((8, tf), lambda j, i: (jnp.maximum(i * hb - 1, 0), j))
    after = pl.BlockSpec((8, tf), lambda j, i: (jnp.minimum((i + 1) * hb, n_halo_blocks - 1), j))
    w3 = pl.BlockSpec((3, tf), lambda j, i: (0, j))
    b1 = pl.BlockSpec((1, tf), lambda j, i: (0, j))
    sd = jax.ShapeDtypeStruct
    return pl.pallas_call(
        body, name=name, grid=(F // tf, T // tm),
        in_specs=[main, main, before, before, after, after, main, after, w3, w3, b1, b1],
        out_specs=[main, main, w3, w3, b1, b1],
        out_shape=[sd((T, F), BF16), sd((T, F), BF16), sd((3, F), F32), sd((3, F), F32), sd((1, F), F32), sd((1, F), F32)],
        compiler_params=_params("parallel", "arbitrary"),
    )(hu, hg, hu, hg, hu, hg, dact, dact, wu, wg, bu, bg)


def loss_head(x, g, target, *, name):
    T, D = x.shape
    tr = 256

    def body(x_ref, g_ref, t_ref, loss_ref, dx_ref, dg_ref):
        xv = x_ref[...]
        gv = g_ref[...]
        r = lax.rsqrt(jnp.mean(xv * xv, axis=-1, keepdims=True) + EPS)
        n = xv * r
        err = n * gv - t_ref[...]
        part_loss = jnp.zeros((1, 128), F32) + 0.5 * jnp.sum(jnp.mean(err * err, axis=-1, keepdims=True))
        dy = err * (1.0 / D)
        dn = dy * gv
        dx_ref[...] = r * (dn - n * jnp.mean(dn * n, axis=-1, keepdims=True))
        part_g = jnp.sum(dy * n, axis=0, keepdims=True)

        @pl.when(pl.program_id(0) == 0)
        def _():
            loss_ref[...] = part_loss
            dg_ref[...] = part_g

        @pl.when(pl.program_id(0) > 0)
        def _():
            loss_ref[...] += part_loss
            dg_ref[...] += part_g

    row = pl.BlockSpec((tr, D), lambda i: (i, 0))
    vec = pl.BlockSpec((1, D), lambda i: (0, 0))
    sd = jax.ShapeDtypeStruct
    return pl.pallas_call(
        body, name=name, grid=(T // tr,),
        in_specs=[row, vec, row],
        out_specs=[pl.BlockSpec((1, 128), lambda i: (0, 0)), row, vec],
        out_shape=[sd((1, 128), F32), sd((T, D), F32), sd((1, D), F32)],
        compiler_params=_params("arbitrary"),
    )(x, g, target)


def adamw(w, g, m, v, *, name):
    R, C = w.shape
    tr = _pick(R, (256, 128, 64, 32, 16, 8))

    def body(w_ref, g_ref, m_ref, v_ref, d_ref, nm_ref, nv_ref):
        gv = g_ref[...]
        mn = ADAM_B1 * m_ref[...] + (1.0 - ADAM_B1) * gv
        vn = ADAM_B2 * v_ref[...] + (1.0 - ADAM_B2) * (gv * gv)
        m_hat = mn / (1.0 - ADAM_B1 ** ADAM_STEP)
        v_hat = vn / (1.0 - ADAM_B2 ** ADAM_STEP)
        d_ref[...] = -ADAM_LR * (m_hat / (jnp.sqrt(v_hat) + ADAM_EPS) + ADAM_WD * w_ref[...])
        nm_ref[...] = mn
        nv_ref[...] = vn

    blk = pl.BlockSpec((tr, C), lambda i: (i, 0))
    sd = jax.ShapeDtypeStruct((R, C), F32)
    return pl.pallas_call(
        body, name=name, grid=(R // tr,), in_specs=[blk] * 4, out_specs=[blk] * 3, out_shape=[sd] * 3,
        compiler_params=_params("parallel"),
    )(w, g, m, v)


def adamw_many(ws, gs, ms, vs, *, name):
    n = len(ws)

    def body(*refs):
        ins, outs = refs[:4 * n], refs[4 * n:]
        for i in range(n):
            w_ref, g_ref, m_ref, v_ref = ins[i], ins[n + i], ins[2 * n + i], ins[3 * n + i]
            gv = g_ref[...]
            mn = ADAM_B1 * m_ref[...] + (1.0 - ADAM_B1) * gv
            vn = ADAM_B2 * v_ref[...] + (1.0 - ADAM_B2) * (gv * gv)
            m_hat = mn / (1.0 - ADAM_B1 ** ADAM_STEP)
            v_hat = vn / (1.0 - ADAM_B2 ** ADAM_STEP)
            outs[i][...] = -ADAM_LR * (m_hat / (jnp.sqrt(v_hat) + ADAM_EPS) + ADAM_WD * w_ref[...])
            outs[n + i][...] = mn
            outs[2 * n + i][...] = vn

    vm = pl.BlockSpec(memory_space=pltpu.VMEM)
    shapes = [jax.ShapeDtypeStruct(w.shape, F32) for w in ws]
    res = pl.pallas_call(
        body, name=name, in_specs=[vm] * (4 * n), out_specs=[vm] * (3 * n), out_shape=shapes * 3, compiler_params=_params(),
    )(*ws, *gs, *ms, *vs)
    return res[:n], res[n:2 * n], res[2 * n:]


def _softplus(x):
    return jnp.maximum(x, 0.0) + jnp.log(1.0 + jnp.exp(-jnp.abs(x)))


def _lru_gates(x, cw, cb, wa, ba, wx, bx, lam):
    S = x.shape[0]
    row = lax.broadcasted_iota(jnp.int32, (S, 1), 0)

    def back(s):
        return jnp.where(row >= s, pltpu.roll(x, s, 0), 0.0)

    xc = (((cb + cw[0:1] * back(3)) + cw[1:2] * back(2)) + cw[2:3] * back(1)) + cw[3:4] * x
    xb = xc.astype(BF16)
    r = jax.nn.sigmoid(jnp.dot(xb, wa, preferred_element_type=F32) + ba)
    ig = jax.nn.sigmoid(jnp.dot(xb, wx, preferred_element_type=F32) + bx)
    sp = _softplus(-lam)
    la = -LRU_C * r * sp
    a = jnp.exp(la)
    y = 2.0 * la
    one_minus_a2 = jnp.where(y > -0.05, -y * (1.0 + y * (0.5 + y * (1.0 / 6.0 + y * (1.0 / 24.0)))), 1.0 - jnp.exp(y))
    mm = jnp.sqrt(one_minus_a2)
    return xc, xb, r, ig, sp, a, mm


SCAN_UNROLL = 4


def _scan8(a, b, reverse):
    row = lax.broadcasted_iota(jnp.int32, (8, 1), 0)
    for k in (1, 2, 4):
        inside = row < 8 - k if reverse else row >= k
        shift = 8 - k if reverse else k
        a_n = jnp.where(inside, pltpu.roll(a, shift, 0), 1.0)
        b_n = jnp.where(inside, pltpu.roll(b, shift, 0), 0.0)
        b = a * b_n + b
        a = a * a_n
    return a, b


def lru_fwd(aux, cw, cb, wa, ba, wx, bx, lam, *, name):
    T = aux.shape[0]
    S, C = SEQ, LRU_HALF_W

    def body(x_ref, g_ref, cw_ref, cb_ref, wa_ref, ba_ref, wx_ref, bx_ref, lam_ref, o_ref, h_ref, a_s, u_s):
        xc, _, r, ig, sp, a, mm = _lru_gates(x_ref[...], cw_ref[...], cb_ref[...], wa_ref[...], ba_ref[...],
                                             wx_ref[...], bx_ref[...], lam_ref[...])
        a_s[...] = a
        u_s[...] = mm * (ig * xc)

        def group(i, h):
            for j in range(SCAN_UNROLL):
                base = pl.multiple_of((i * SCAN_UNROLL + j) * 8, 8)
                A, Bv = _scan8(a_s[pl.ds(base, 8), :], u_s[pl.ds(base, 8), :], reverse=False)
                H = A * h + Bv
                h_ref[pl.ds(base, 8), :] = H
                h = H[7:8]
            return h

        lax.fori_loop(0, S // 8 // SCAN_UNROLL, group, jnp.zeros((1, C), F32))
        gate = g_ref[...]
        o_ref[...] = (h_ref[...] * (gate * _gelu_cdf(gate))).astype(BF16)

    blk = lambda col: pl.BlockSpec((S, C), lambda c, b: (b, col + c))
    par = lambda rows: pl.BlockSpec((rows, C), lambda c, b: (0, c))
    sq = pl.BlockSpec((None, C, C), lambda c, b: (c, 0, 0))
    sd = jax.ShapeDtypeStruct
    W = LRU_HALVES * C
    return pl.pallas_call(
        body, name=name, grid=(LRU_HALVES, T // S),
        in_specs=[blk(0), blk(LRU_HALVES), par(4), par(1), sq, par(1), sq, par(1), par(1)],
        out_specs=[blk(0), blk(0)], out_shape=[sd((T, W), BF16), sd((T, W), F32)],
        scratch_shapes=[pltpu.VMEM((S, C), F32), pltpu.VMEM((S, C), F32)],
        compiler_params=_params("parallel", "parallel"),
    )(aux, aux, cw, cb, wa, ba, wx, bx, lam)


def lru_bwd(aux, h, dmixed, cw, cb, wa, ba, wx, bx, lam, *, name):
    T = aux.shape[0]
    S, C = SEQ, LRU_HALF_W

    def body(x_ref, g_ref, h_ref, do_ref, cw_ref, cb_ref, wa_ref, ba_ref, wx_ref, bx_ref, lam_ref,
             dx_ref, dgate_ref, dcw_ref, dcb_ref, dwa_ref, dba_ref, dwx_ref, dbx_ref, dlam_ref, a_s, d_s):
        x = x_ref[...]
        cw = cw_ref[...]
        lam = lam_ref[...]
        xc, xb, r, ig, sp, a, mm = _lru_gates(x, cw, cb_ref[...], wa_ref[...], ba_ref[...], wx_ref[...], bx_ref[...], lam)
        gate = g_ref[...]
        gl, dgl = _gelu_and_grad(gate)
        dout = do_ref[...]
        hv = h_ref[...]
        dgate_ref[...] = dout * hv * dgl
        a_s[...] = a
        d_s[...] = dout * gl

        last_row = lax.broadcasted_iota(jnp.int32, (8, 1), 0) == 7

        def group(i, c):
            for j in range(SCAN_UNROLL):
                base = pl.multiple_of((S // 8 - 1 - (i * SCAN_UNROLL + j)) * 8, 8)
                a8 = a_s[pl.ds(base, 8), :]
                d8 = d_s[pl.ds(base, 8), :]
                A, Bv = _scan8(a8, a8 * d8, reverse=True)
                Cv = A * c + Bv
                d_s[pl.ds(base, 8), :] = d8 + jnp.where(last_row, c, pltpu.roll(Cv, 7, 0))
                c = Cv[0:1]
            return c

        lax.fori_loop(0, S // 8 // SCAN_UNROLL, group, jnp.zeros((1, C), F32))
        row = lax.broadcasted_iota(jnp.int32, (S, 1), 0)
        dht = d_s[...]
        h_prev = jnp.where(row >= 1, pltpu.roll(hv, 1, 0), 0.0)
        da = dht * h_prev
        gx = ig * xc
        dmm = dht * gx
        dig = dht * mm * xc
        dxc = dht * mm * ig
        dla = da * a - dmm * (a * a) / mm
        dr = dla * (-LRU_C * sp)
        dsp = jnp.sum(dla * (-LRU_C * r), axis=0, keepdims=True)
        dlam = dsp * (-jax.nn.sigmoid(-lam))
        dpa = dr * r * (1.0 - r)
        dpx = dig * ig * (1.0 - ig)
        dpa_b, dpx_b = dpa.astype(BF16), dpx.astype(BF16)
        nt = (((1,), (1,)), ((), ()))
        tn = (((0,), (0,)), ((), ()))
        dxc = dxc + lax.dot_general(dpa_b, wa_ref[...], nt, preferred_element_type=F32) \
                  + lax.dot_general(dpx_b, wx_ref[...], nt, preferred_element_type=F32)
        dwa = lax.dot_general(xb, dpa_b, tn, preferred_element_type=F32)
        dwx = lax.dot_general(xb, dpx_b, tn, preferred_element_type=F32)

        def fwd(v, s):
            return jnp.where(row < S - s, pltpu.roll(v, S - s, 0), 0.0)

        def back(v, s):
            return jnp.where(row >= s, pltpu.roll(v, s, 0), 0.0)

        dx_ref[...] = cw[3:4] * dxc + cw[2:3] * fwd(dxc, 1) + cw[1:2] * fwd(dxc, 2) + cw[0:1] * fwd(dxc, 3)
        s0 = lambda v: jnp.sum(v, axis=0, keepdims=True)
        dcw = jnp.concatenate([s0(dxc * back(x, 3)), s0(dxc * back(x, 2)), s0(dxc * back(x, 1)), s0(dxc * x)], axis=0)
        parts = ((dcw_ref, dcw), (dcb_ref, s0(dxc)), (dwa_ref, dwa), (dba_ref, s0(dpa)), (dwx_ref, dwx),
                 (dbx_ref, s0(dpx)), (dlam_ref, dlam))

        @pl.when(pl.program_id(1) == 0)
        def _():
            for ref, val in parts:
                ref[...] = val

        @pl.when(pl.program_id(1) > 0)
        def _():
            for ref, val in parts:
                ref[...] += val

    blk = lambda col: pl.BlockSpec((S, C), lambda c, b: (b, col + c))
    par = lambda rows: pl.BlockSpec((rows, C), lambda c, b: (0, c))
    sq = pl.BlockSpec((None, C, C), lambda c, b: (c, 0, 0))
    sd = jax.ShapeDtypeStruct
    W = LRU_HALVES * C
    vec = sd((1, W), F32)
    return pl.pallas_call(
        body, name=name, grid=(LRU_HALVES, T // S),
        in_specs=[blk(0), blk(LRU_HALVES), blk(0), blk(3 * LRU_HALVES), par(4), par(1), sq, par(1), sq, par(1), par(1)],
        out_specs=[blk(0), blk(0), par(4), par(1), sq, par(1), sq, par(1), par(1)],
        out_shape=[sd((T, W), F32), sd((T, W), F32), sd((4, W), F32), vec, sd((LRU_HALVES, C, C), F32), vec,
                   sd((LRU_HALVES, C, C), F32), vec, vec],
        scratch_shapes=[pltpu.VMEM((S, C), F32), pltpu.VMEM((S, C), F32)],
        compiler_params=_params("parallel", "arbitrary"),
    )(aux, aux, h, dmixed, cw, cb, wa, ba, wx, bx, lam)


_NT = (((1,), (1,)), ((), ()))
_TN = (((0,), (0,)), ((), ()))


def _dot(a, b, dims=None):
    if dims is None:
        return jnp.dot(a, b, preferred_element_type=F32)
    return lax.dot_general(a, b, dims, preferred_element_type=F32)


def _hs(h):
    return slice(h * HEAD_DIM, (h + 1) * HEAD_DIM)


def cross_fwd(q, kv, *, name):
    T = q.shape[0]
    tq = 512

    def body(q_ref, kv_ref, o_ref):
        for h in range(N_HEADS):
            qh = q_ref[:, _hs(h)] * Q_SCALE
            k = kv_ref[:, _hs(h)]
            v = kv_ref[:, GROUP_W + h * HEAD_DIM:GROUP_W + (h + 1) * HEAD_DIM]
            s = _dot(qh, k, _NT)
            p = jnp.exp(s - jnp.max(s, axis=-1, keepdims=True))
            p = p / jnp.sum(p, axis=-1, keepdims=True)
            o_ref[:, _hs(h)] = _dot(p.astype(BF16), v).astype(BF16)

    per = SEQ // tq
    return pl.pallas_call(
        body, name=name, grid=(T // tq,),
        in_specs=[pl.BlockSpec((tq, GROUP_W), lambda i: (i, 0)), pl.BlockSpec((N_MEM, 2 * GROUP_W), lambda i: (i // per, 0))],
        out_specs=pl.BlockSpec((tq, GROUP_W), lambda i: (i, 0)), out_shape=jax.ShapeDtypeStruct((T, GROUP_W), BF16),
        compiler_params=_params("parallel"),
    )(q, kv)


def cross_bwd(q, kv, do, *, name):
    T = q.shape[0]
    tq = 512
    per = SEQ // tq

    def body(q_ref, kv_ref, do_ref, dq_ref, dkv_ref):
        first = pl.program_id(0) % per == 0
        for h in range(N_HEADS):
            vs = slice(GROUP_W + h * HEAD_DIM, GROUP_W + (h + 1) * HEAD_DIM)
            qh = q_ref[:, _hs(h)] * Q_SCALE
            k = kv_ref[:, _hs(h)]
            v = kv_ref[:, vs]
            doh = do_ref[:, _hs(h)].astype(BF16)
            s = _dot(qh, k, _NT)
            p = jnp.exp(s - jnp.max(s, axis=-1, keepdims=True))
            p = p / jnp.sum(p, axis=-1, keepdims=True)
            dp = _dot(doh, v, _NT)
            ds = (p * (dp - jnp.sum(p * dp, axis=-1, keepdims=True))).astype(BF16)
            dq_ref[:, _hs(h)] = (_dot(ds, k) * Q_SCALE).astype(BF16)
            dk = _dot(ds, qh, _TN)
            dv = _dot(p.astype(BF16), doh, _TN)

            @pl.when(first)
            def _():
                dkv_ref[:, _hs(h)] = dk
                dkv_ref[:, vs] = dv

            @pl.when(jnp.logical_not(first))
            def _():
                dkv_ref[:, _hs(h)] += dk
                dkv_ref[:, vs] += dv

    qb = pl.BlockSpec((tq, GROUP_W), lambda i: (i, 0))
    kvb = pl.BlockSpec((N_MEM, 2 * GROUP_W), lambda i: (i // per, 0))
    sd = jax.ShapeDtypeStruct
    return pl.pallas_call(
        body, name=name, grid=(T // tq,),
        in_specs=[qb, kvb, qb], out_specs=[qb, kvb],
        out_shape=[sd((T, GROUP_W), BF16), sd(kv.shape, F32)],
        compiler_params=_params("arbitrary"),
    )(q, kv, do)


NB = SEQ // BLOCK
NEG = -1e30
HEADS = tuple(range(N_HEADS))


def _blk(i):
    return pl.ds(pl.multiple_of(i * BLOCK, BLOCK), BLOCK)


def _qkv_specs(first_col):
    return [pl.BlockSpec((SEQ, GROUP_W), lambda b, c=first_col + j: (b, c)) for j in range(3)]


LANES = 128
CUM_BLK = 256


def col_to_row(c):
    b = c.shape[0] // SEQ
    return c.reshape(b, SEQ, LANES)[:, :, :8].transpose(0, 2, 1).reshape(b * 8, SEQ)


def row_to_col(r):
    b = r.shape[0] // 8
    c = r.reshape(b, 8, SEQ).transpose(0, 2, 1)
    return jnp.pad(c, ((0, 0), (0, 0), (0, LANES - 8))).reshape(b * SEQ, LANES)


def fox_prep(aux, bf, *, name):
    T = aux.shape[0]

    def body(f_ref, b_ref, o_ref):
        row = lax.broadcasted_iota(jnp.int32, (CUM_BLK, CUM_BLK), 0)
        col = lax.broadcasted_iota(jnp.int32, (CUM_BLK, CUM_BLK), 1)
        upto = (col <= row).astype(BF16)
        carry = jnp.zeros((1, LANES), F32)
        for n in range(SEQ // CUM_BLK):
            rows = slice(n * CUM_BLK, (n + 1) * CUM_BLK)
            logf = -_softplus(-(f_ref[rows, :] + b_ref[...]))
            hi = logf.astype(BF16)
            lo = (logf - hi.astype(F32)).astype(BF16)
            cum = _dot(upto, hi) + _dot(upto, lo) + carry
            o_ref[rows, :] = cum
            carry = cum[CUM_BLK - 1:CUM_BLK]

    return pl.pallas_call(
        body, name=name, grid=(T // SEQ,),
        in_specs=[pl.BlockSpec((SEQ, LANES), lambda b: (b, 4)), pl.BlockSpec((1, LANES), lambda b: (0, 0))],
        out_specs=pl.BlockSpec((SEQ, LANES), lambda b: (b, 0)), out_shape=jax.ShapeDtypeStruct((T, LANES), F32),
        compiler_params=_params("parallel"),
    )(aux, bf)


def fox_prep_bwd(aux, bf, dcum, *, name):
    T = aux.shape[0]

    def body(f_ref, b_ref, d_ref, df_ref, db_ref):
        row = lax.broadcasted_iota(jnp.int32, (CUM_BLK, CUM_BLK), 0)
        col = lax.broadcasted_iota(jnp.int32, (CUM_BLK, CUM_BLK), 1)
        onward = (col >= row).astype(BF16)
        carry = jnp.zeros((1, LANES), F32)
        tot = jnp.zeros((1, LANES), F32)
        for n in range(SEQ // CUM_BLK - 1, -1, -1):
            rows = slice(n * CUM_BLK, (n + 1) * CUM_BLK)
            d = d_ref[rows, :]
            hi = d.astype(BF16)
            lo = (d - hi.astype(F32)).astype(BF16)
            dlogf = _dot(onward, hi) + _dot(onward, lo) + carry
            carry = dlogf[0:1]
            df = dlogf * jax.nn.sigmoid(-(f_ref[rows, :] + b_ref[...]))
            df_ref[rows, :] = df
            tot = tot + jnp.sum(df, axis=0, keepdims=True)

        @pl.when(pl.program_id(0) == 0)
        def _():
            db_ref[...] = tot

        @pl.when(pl.program_id(0) > 0)
        def _():
            db_ref[...] += tot

    blk = pl.BlockSpec((SEQ, LANES), lambda b: (b, 0))
    vec = pl.BlockSpec((1, LANES), lambda b: (0, 0))
    sd = jax.ShapeDtypeStruct
    return pl.pallas_call(
        body, name=name, grid=(T // SEQ,),
        in_specs=[pl.BlockSpec((SEQ, LANES), lambda b: (b, 4)), vec, blk],
        out_specs=[blk, vec], out_shape=[sd((T, LANES), F32), sd((1, LANES), F32)],
        compiler_params=_params("arbitrary"),
    )(aux, bf, dcum)


CHUNK = 256
WIDE = N_HEADS * CHUNK
NCH = SEQ // CHUNK


def _seg(h):
    return slice(h * CHUNK, (h + 1) * CHUNK)


def _chunk_rows(c):
    return pl.ds(pl.multiple_of(c * CHUNK, CHUNK), CHUNK)


def _wide_consts():
    r = lax.broadcasted_iota(jnp.int32, (WIDE, GROUP_W), 0)
    f = lax.broadcasted_iota(jnp.int32, (WIDE, GROUP_W), 1)
    bd = (r // CHUNK) == (f // HEAD_DIM)
    row = lax.broadcasted_iota(jnp.int32, (BLOCK, WIDE), 0)
    key = lax.broadcasted_iota(jnp.int32, (BLOCK, WIDE), 1) % CHUNK
    return bd, row, key


def _block_diag(x, bd):
    return jnp.where(bd, jnp.concatenate([x] * N_HEADS, axis=0), jnp.zeros((), x.dtype))


def _fold_heads(w, bd):
    w = jnp.where(bd, w, 0.0)
    return (w[0:CHUNK] + w[CHUNK:2 * CHUNK]) + (w[2 * CHUNK:3 * CHUNK] + w[3 * CHUNK:])


def _widen(cols):
    return jnp.concatenate([jnp.broadcast_to(c, (BLOCK, CHUNK)) for c in cols], axis=1)


def _head_rowsums(w):
    return [jnp.sum(w[:, _seg(h)], axis=1, keepdims=True) for h in HEADS]


def _tri_wide(x, tri):
    hi = x.astype(BF16)
    lo = (x - hi.astype(F32)).astype(BF16)
    y = _dot(jnp.concatenate([hi[:, _seg(h)] for h in HEADS] + [lo[:, _seg(h)] for h in HEADS], axis=0), tri)
    return jnp.concatenate([y[h * BLOCK:(h + 1) * BLOCK] + y[(N_HEADS + h) * BLOCK:(N_HEADS + h + 1) * BLOCK] for h in HEADS], axis=1)


def _feature_widen(cols):
    return jnp.concatenate([jnp.broadcast_to(c, (BLOCK, HEAD_DIM)) for c in cols], axis=1)


def _loop_by_two(n, index, body, carry):
    odd = n % 2
    carry = lax.fori_loop(0, odd, lambda _, cr: body(index(0), cr), carry)
    return lax.fori_loop(0, n // 2, lambda t, cr: body(index(odd + 2 * t + 1), body(index(odd + 2 * t), cr)), carry)


def _sbw_scores(q, kbd, later):
    z = _dot(q, kbd, _NT)
    lk = -_softplus(z)
    return z + lk, lk, _tri_wide(lk, later)


def _sbw_tile(q, kbd, mask, later, csum):
    z = _dot(q, kbd, _NT)
    lk = -_softplus(z)
    if mask is not None:
        lk = jnp.where(mask, lk, 0.0)
    e = z + lk
    att = jnp.exp(e + _tri_wide(lk, later) + csum)
    if mask is not None:
        att = jnp.where(mask, att, 0.0)
    return att, e, lk


def sbw_fwd(qkv, *, name):
    T = qkv.shape[0]

    def body(q_ref, k_ref, v_ref, o_ref):
        bd, row, key = _wide_consts()
        r2 = lax.broadcasted_iota(jnp.int32, (CHUNK, CHUNK), 0)
        c2 = lax.broadcasted_iota(jnp.int32, (CHUNK, CHUNK), 1)
        later = (r2 > c2).astype(BF16)

        def qblock(i, _):
            q = q_ref[_blk(i), :] * Q_SCALE
            cd = i // 2
            strict = key < row + BLOCK * (i % 2)

            def tile(c, mask, carry):
                acc, csum = carry
                att, _, lk = _sbw_tile(q, _block_diag(k_ref[_chunk_rows(c), :], bd), mask, later, csum)
                acc = acc + _dot(att.astype(BF16), _block_diag(v_ref[_chunk_rows(c), :], bd))
                return acc, csum + _widen(_head_rowsums(lk))

            def two_tiles(c1, carry):
                acc, csum = carry
                e1, lk1, t1 = _sbw_scores(q, _block_diag(k_ref[_chunk_rows(c1), :], bd), later)
                e2, lk2, t2 = _sbw_scores(q, _block_diag(k_ref[_chunk_rows(c1 - 1), :], bd), later)
                att1 = jnp.exp(e1 + t1 + csum)
                csum = csum + _widen(_head_rowsums(lk1))
                att2 = jnp.exp(e2 + t2 + csum)
                csum = csum + _widen(_head_rowsums(lk2))
                acc = acc + _dot(att1.astype(BF16), _block_diag(v_ref[_chunk_rows(c1), :], bd))
                acc = acc + _dot(att2.astype(BF16), _block_diag(v_ref[_chunk_rows(c1 - 1), :], bd))
                return acc, csum

            carry = tile(cd, strict, (jnp.zeros((BLOCK, GROUP_W), F32), jnp.zeros((BLOCK, WIDE), F32)))
            odd = cd % 2
            carry = lax.fori_loop(0, odd, lambda n, cr: tile(cd - 1, None, cr), carry)
            acc, _ = lax.fori_loop(0, cd // 2, lambda n, cr: two_tiles(cd - 1 - odd - 2 * n, cr), carry)
            o_ref[_blk(i), :] = acc.astype(BF16)
            return 0

        lax.fori_loop(0, NB, qblock, 0)

    return pl.pallas_call(
        body, name=name, grid=(T // SEQ,), in_specs=_qkv_specs(0),
        out_specs=pl.BlockSpec((SEQ, GROUP_W), lambda b: (b, 0)), out_shape=jax.ShapeDtypeStruct((T, GROUP_W), BF16),
        compiler_params=_params("parallel"),
    )(qkv, qkv, qkv)


def sbw_bwd(qkv, dmixed, *, name):
    T = qkv.shape[0]

    def body(q_ref, k_ref, v_ref, do_ref, dq_ref, dk_ref, dv_ref, att_s, sg_s):
        bd, row, key = _wide_consts()
        r2 = lax.broadcasted_iota(jnp.int32, (CHUNK, CHUNK), 0)
        c2 = lax.broadcasted_iota(jnp.int32, (CHUNK, CHUNK), 1)
        later = (r2 > c2).astype(BF16)
        earlier = (r2 < c2).astype(BF16)
        dk_ref[...] = jnp.zeros_like(dk_ref)
        dv_ref[...] = jnp.zeros_like(dv_ref)

        def qblock(i, _):
            q = q_ref[_blk(i), :] * Q_SCALE
            do = do_ref[_blk(i), :].astype(BF16)
            cd = i // 2
            strict = key < row + BLOCK * (i % 2)

            def recompute(c, mask, csum):
                att, e, lk = _sbw_tile(q, _block_diag(k_ref[_chunk_rows(c), :], bd), mask, later, csum)
                sg = jnp.exp(e)
                att_s[c] = att
                sg_s[c] = sg if mask is None else jnp.where(mask, sg, 0.0)
                return csum + _widen(_head_rowsums(lk))

            def recompute_two(c1, csum):
                e1, lk1, t1 = _sbw_scores(q, _block_diag(k_ref[_chunk_rows(c1), :], bd), later)
                e2, lk2, t2 = _sbw_scores(q, _block_diag(k_ref[_chunk_rows(c1 - 1), :], bd), later)
                sg_s[c1] = jnp.exp(e1)
                sg_s[c1 - 1] = jnp.exp(e2)
                att_s[c1] = jnp.exp(e1 + t1 + csum)
                csum = csum + _widen(_head_rowsums(lk1))
                att_s[c1 - 1] = jnp.exp(e2 + t2 + csum)
                return csum + _widen(_head_rowsums(lk2))

            csum = recompute(cd, strict, jnp.zeros((BLOCK, WIDE), F32))
            odd = cd % 2
            csum = lax.fori_loop(0, odd, lambda n, cs: recompute(cd - 1, None, cs), csum)
            lax.fori_loop(0, cd // 2, lambda n, cs: recompute_two(cd - 1 - odd - 2 * n, cs), csum)

            def tile(c, carry):
                dq, pre = carry
                kbd = _block_diag(k_ref[_chunk_rows(c), :], bd)
                vbd = _block_diag(v_ref[_chunk_rows(c), :], bd)
                att = att_s[c]
                ds = _dot(do, vbd, _NT) * att
                dlk = ds + _tri_wide(ds, earlier) + pre
                dz = (ds - dlk * sg_s[c]).astype(BF16)
                dk_ref[_chunk_rows(c), :] += _fold_heads(_dot(dz, q, _TN), bd)
                dv_ref[_chunk_rows(c), :] += _fold_heads(_dot(att.astype(BF16), do, _TN), bd)
                return dq + _dot(dz, kbd), pre + _widen(_head_rowsums(ds))

            def two_tiles(c1, carry):
                dq, pre = carry
                c2 = c1 + 1
                kbd1, kbd2 = _block_diag(k_ref[_chunk_rows(c1), :], bd), _block_diag(k_ref[_chunk_rows(c2), :], bd)
                att1, att2 = att_s[c1], att_s[c2]
                ds1 = _dot(do, _block_diag(v_ref[_chunk_rows(c1), :], bd), _NT) * att1
                ds2 = _dot(do, _block_diag(v_ref[_chunk_rows(c2), :], bd), _NT) * att2
                tri1, tri2 = _tri_wide(ds1, earlier), _tri_wide(ds2, earlier)
                dv_ref[_chunk_rows(c1), :] += _fold_heads(_dot(att1.astype(BF16), do, _TN), bd)
                dv_ref[_chunk_rows(c2), :] += _fold_heads(_dot(att2.astype(BF16), do, _TN), bd)
                dz1 = (ds1 - (ds1 + tri1 + pre) * sg_s[c1]).astype(BF16)
                pre = pre + _widen(_head_rowsums(ds1))
                dz2 = (ds2 - (ds2 + tri2 + pre) * sg_s[c2]).astype(BF16)
                pre = pre + _widen(_head_rowsums(ds2))
                dk_ref[_chunk_rows(c1), :] += _fold_heads(_dot(dz1, q, _TN), bd)
                dk_ref[_chunk_rows(c2), :] += _fold_heads(_dot(dz2, q, _TN), bd)
                return dq + _dot(dz1, kbd1) + _dot(dz2, kbd2), pre

            n_tiles = cd + 1
            odd = n_tiles % 2
            carry = (jnp.zeros((BLOCK, GROUP_W), F32), jnp.zeros((BLOCK, WIDE), F32))
            carry = lax.fori_loop(0, odd, lambda n, cr: tile(0, cr), carry)
            dq, _ = lax.fori_loop(0, n_tiles // 2, lambda n, cr: two_tiles(odd + 2 * n, cr), carry)
            dq_ref[_blk(i), :] = dq * Q_SCALE
            return 0

        lax.fori_loop(0, NB, qblock, 0)

    out = pl.BlockSpec((SEQ, GROUP_W), lambda b: (b, 0))
    sd = jax.ShapeDtypeStruct((T, GROUP_W), F32)
    return pl.pallas_call(
        body, name=name, grid=(T // SEQ,), in_specs=_qkv_specs(0) + [out],
        out_specs=[out] * 3, out_shape=[sd] * 3,
        scratch_shapes=[pltpu.VMEM((NCH, BLOCK, WIDE), F32), pltpu.VMEM((NCH, BLOCK, WIDE), F32)],
        compiler_params=_params("parallel"),
    )(qkv, qkv, qkv, dmixed)


def _foxw_logits(q, kbd, cq, cr_ref, c, mask):
    ck = jnp.concatenate([cr_ref[h:h + 1, _chunk_rows(c)] for h in HEADS], axis=1)
    z = _dot(q, kbd, _NT) + cq - ck
    return z if mask is None else jnp.where(mask, z, NEG)


def foxw_fwd(qkv, cumc, cumr, *, name):
    T = qkv.shape[0]

    def body(q_ref, k_ref, v_ref, cc_ref, cr_ref, o_ref, o32_ref, lse_ref, z_s):
        bd, row, key = _wide_consts()
        lse_ref[...] = jnp.zeros_like(lse_ref)

        def qblock(i, _):
            q = q_ref[_blk(i), :] * Q_SCALE
            cq = _widen([cc_ref[_blk(i), h:h + 1] for h in HEADS])
            cd = i // 2
            causal = key <= row + BLOCK * (i % 2)

            def logits(c, mask, ms):
                z = _foxw_logits(q, _block_diag(k_ref[_chunk_rows(c), :], bd), cq, cr_ref, c, mask)
                z_s[c] = z
                return tuple(jnp.maximum(ms[h], jnp.max(z[:, _seg(h)], axis=1, keepdims=True)) for h in HEADS)

            ms = logits(cd, causal, (jnp.full((BLOCK, 1), NEG, F32),) * N_HEADS)
            ms = _loop_by_two(cd, lambda n: n, lambda c, m: logits(c, None, m), ms)
            m_wide = _widen(ms)

            def values(c, carry):
                acc, l = carry
                p = jnp.exp(z_s[c] - m_wide)
                return acc + _dot(p.astype(BF16), _block_diag(v_ref[_chunk_rows(c), :], bd)), l + _widen(_head_rowsums(p))

            acc, l = _loop_by_two(cd + 1, lambda n: n, values, (jnp.zeros((BLOCK, GROUP_W), F32), jnp.zeros((BLOCK, WIDE), F32)))
            ls = [l[:, h * CHUNK:h * CHUNK + 1] for h in HEADS]
            o = acc / _feature_widen(ls)
            o_ref[_blk(i), :] = o.astype(BF16)
            o32_ref[_blk(i), :] = o
            for h in HEADS:
                lse_ref[_blk(i), h:h + 1] = ms[h] + jnp.log(ls[h])
            return 0

        lax.fori_loop(0, NB, qblock, 0)

    out = pl.BlockSpec((SEQ, GROUP_W), lambda b: (b, 0))
    colb = pl.BlockSpec((SEQ, LANES), lambda b: (b, 0))
    sd = jax.ShapeDtypeStruct
    return pl.pallas_call(
        body, name=name, grid=(T // SEQ,),
        in_specs=_qkv_specs(3) + [colb, pl.BlockSpec((8, SEQ), lambda b: (b, 0))],
        out_specs=[out, out, colb], out_shape=[sd((T, GROUP_W), BF16), sd((T, GROUP_W), F32), sd((T, LANES), F32)],
        scratch_shapes=[pltpu.VMEM((NCH, BLOCK, WIDE), F32)],
        compiler_params=_params("parallel"),
    )(qkv, qkv, qkv, cumc, cumr)


def foxw_bwd(qkv, cumc, cumr, lse, o32, dmixed, *, name):
    T = qkv.shape[0]

    def body(q_ref, k_ref, v_ref, cc_ref, cr_ref, lse_ref, o_ref, do_ref, dq_ref, dk_ref, dv_ref, dcc_ref, dcr_ref):
        bd, row, key = _wide_consts()
        dk_ref[...] = jnp.zeros_like(dk_ref)
        dv_ref[...] = jnp.zeros_like(dv_ref)
        dcc_ref[...] = jnp.zeros_like(dcc_ref)
        dcr_ref[...] = jnp.zeros_like(dcr_ref)

        def qblock(i, _):
            q = q_ref[_blk(i), :] * Q_SCALE
            do = do_ref[_blk(i), :].astype(BF16)
            prod = do.astype(F32) * o_ref[_blk(i), :]
            delta = _widen([jnp.sum(prod[:, _hs(h)], axis=1, keepdims=True) for h in HEADS])
            cq = _widen([cc_ref[_blk(i), h:h + 1] for h in HEADS])
            lse_w = _widen([lse_ref[_blk(i), h:h + 1] for h in HEADS])
            cd = i // 2
            causal = key <= row + BLOCK * (i % 2)

            def tile(c, mask, carry):
                dq, dcq = carry
                kbd = _block_diag(k_ref[_chunk_rows(c), :], bd)
                vbd = _block_diag(v_ref[_chunk_rows(c), :], bd)
                p = jnp.exp(_foxw_logits(q, kbd, cq, cr_ref, c, mask) - lse_w)
                ds = p * (_dot(do, vbd, _NT) - delta)
                dsb = ds.astype(BF16)
                dk_ref[_chunk_rows(c), :] += _fold_heads(_dot(dsb, q, _TN), bd)
                dv_ref[_chunk_rows(c), :] += _fold_heads(_dot(p.astype(BF16), do, _TN), bd)
                for h in HEADS:
                    dcr_ref[h:h + 1, _chunk_rows(c)] -= jnp.sum(ds[:, _seg(h)], axis=0, keepdims=True)
                return dq + _dot(dsb, kbd), dcq + _widen(_head_rowsums(ds))

            def two_tiles(c1, carry):
                dq, dcq = carry
                cs = (c1, c1 + 1)
                kbds = [_block_diag(k_ref[_chunk_rows(c), :], bd) for c in cs]
                vbds = [_block_diag(v_ref[_chunk_rows(c), :], bd) for c in cs]
                ps = [jnp.exp(_foxw_logits(q, kbds[j], cq, cr_ref, cs[j], None) - lse_w) for j in range(2)]
                dss = [ps[j] * (_dot(do, vbds[j], _NT) - delta) for j in range(2)]
                dsbs = [d.astype(BF16) for d in dss]
                for j, c in enumerate(cs):
                    dk_ref[_chunk_rows(c), :] += _fold_heads(_dot(dsbs[j], q, _TN), bd)
                    dv_ref[_chunk_rows(c), :] += _fold_heads(_dot(ps[j].astype(BF16), do, _TN), bd)
                    for h in HEADS:
                        dcr_ref[h:h + 1, _chunk_rows(c)] -= jnp.sum(dss[j][:, _seg(h)], axis=0, keepdims=True)
                dq = dq + _dot(dsbs[0], kbds[0]) + _dot(dsbs[1], kbds[1])
                return dq, dcq + _widen(_head_rowsums(dss[0])) + _widen(_head_rowsums(dss[1]))

            carry = tile(cd, causal, (jnp.zeros((BLOCK, GROUP_W), F32), jnp.zeros((BLOCK, WIDE), F32)))
            odd = cd % 2
            carry = lax.fori_loop(0, odd, lambda n, cr: tile(0, None, cr), carry)
            dq, dcq = lax.fori_loop(0, cd // 2, lambda n, cr: two_tiles(odd + 2 * n, cr), carry)
            dq_ref[_blk(i), :] = dq * Q_SCALE
            for h in HEADS:
                dcc_ref[_blk(i), h:h + 1] = dcq[:, h * CHUNK:h * CHUNK + 1]
            return 0

        lax.fori_loop(0, NB, qblock, 0)

    out = pl.BlockSpec((SEQ, GROUP_W), lambda b: (b, 0))
    colb = pl.BlockSpec((SEQ, LANES), lambda b: (b, 0))
    rowb = pl.BlockSpec((8, SEQ), lambda b: (b, 0))
    sd = jax.ShapeDtypeStruct
    big = sd((T, GROUP_W), F32)
    return pl.pallas_call(
        body, name=name, grid=(T // SEQ,),
        in_specs=_qkv_specs(3) + [colb, rowb, colb, out, pl.BlockSpec((SEQ, GROUP_W), lambda b: (b, 1))],
        out_specs=[out, out, out, colb, rowb],
        out_shape=[big, big, big, sd((T, LANES), F32), sd((T // SEQ * 8, SEQ), F32)],
        compiler_params=_params("parallel"),
    )(qkv, qkv, qkv, cumc, cumr, lse, o32, dmixed)


BAND = 2 * BLOCK


def _t5_bucket_np(dist):
    n = np.maximum(dist, 0)
    max_exact = NUM_BUCKETS // 2
    nf = np.maximum(n, 1).astype(np.float32)
    large = max_exact + (np.log(nf / np.float32(max_exact)) / np.float32(math.log(MAX_DISTANCE / max_exact))
                         * np.float32(NUM_BUCKETS - max_exact)).astype(np.int32)
    large = np.minimum(large, NUM_BUCKETS - 1)
    return np.where(n < max_exact, n, large).astype(np.int32)


def _band_buckets():
    qi = np.arange(BLOCK)[:, None]
    ki = np.arange(BAND)[None, :]
    delta = np.clip(qi - ki + BLOCK, 0, BLOCK)
    return np.stack([_t5_bucket_np(delta * d) for d in DILATIONS])


def relbias_expand(rel, *, name):
    buckets = jnp.asarray(_band_buckets())
    n_pat = len(DILATIONS)

    def body(rel_ref, bk_ref, o_ref):
        for p in range(n_pat):
            bk = bk_ref[p]
            for h in range(N_HEADS):
                acc = jnp.zeros((BLOCK, BAND), F32)
                for b in range(NUM_BUCKETS):
                    acc = jnp.where(bk == b, rel_ref[b, h], acc)
                o_ref[p * N_HEADS + h] = acc

    return pl.pallas_call(
        body, name=name,
        in_specs=[pl.BlockSpec(memory_space=pltpu.SMEM), pl.BlockSpec(memory_space=pltpu.VMEM)],
        out_specs=pl.BlockSpec(memory_space=pltpu.VMEM),
        out_shape=jax.ShapeDtypeStruct((n_pat * N_HEADS, BLOCK, BAND), F32),
        compiler_params=_params(),
    )(rel, buckets)


def relbias_reduce(ds_all, *, name):
    buckets = jnp.asarray(_band_buckets())
    n_pat = len(DILATIONS)

    def body(ds_ref, bk_ref, o_ref):
        for b in range(NUM_BUCKETS):
            for h in range(N_HEADS):
                tot = jnp.float32(0.0)
                for p in range(n_pat):
                    tot = tot + jnp.sum(jnp.where(bk_ref[p] == b, ds_ref[p * N_HEADS + h], 0.0))
                o_ref[b, h] = tot

    return pl.pallas_call(
        body, name=name,
        in_specs=[pl.BlockSpec(memory_space=pltpu.VMEM), pl.BlockSpec(memory_space=pltpu.VMEM)],
        out_specs=pl.BlockSpec(memory_space=pltpu.SMEM),
        out_shape=jax.ShapeDtypeStruct((NUM_BUCKETS, N_HEADS), F32),
        compiler_params=_params(),
    )(ds_all, buckets)


def _band_valid_wide(first, row, key):
    inside = jnp.logical_and(key >= row, key <= row + BLOCK)
    return jnp.logical_and(inside, jnp.logical_or(jnp.logical_not(first), key >= BLOCK))


QKV_BLOCKS = 9


def _band_in_specs(d, pattern, has_prev):
    rows = BLOCK * d
    cur = lambda c: pl.BlockSpec((rows, GROUP_W), lambda tb, r: (tb, c))
    prev = lambda c: pl.BlockSpec((rows, GROUP_W), lambda tb, r: (jnp.maximum(tb - 1, 0), c))
    bias = pl.BlockSpec((N_HEADS, BLOCK, BAND), lambda tb, r: (pattern, 0, 0))
    return [cur(6), cur(7), cur(8)] + ([prev(7), prev(8)] if has_prev else []) + [bias]


def _classes_per_step(d):
    return min(d, 4)


def _step_classes(d):
    n = _classes_per_step(d)
    return [pl.program_id(1) * n + j for j in range(n)]


def _class_rows(d, cls):
    return pl.ds(cls, BLOCK, stride=d) if d > 1 else pl.ds(0, BLOCK)


def _halves_scratch(rows, n):
    return [pltpu.VMEM((2, rows, LANES), F32)] * n


def _stage(refs, scratch):
    @pl.when(pl.program_id(1) == 0)
    def _():
        for src, dst in zip(refs, scratch):
            dst[0] = src[:, :LANES].astype(F32)
            dst[1] = src[:, LANES:].astype(F32)


def _take_class(s, d, cls):
    rows = _class_rows(d, cls)
    return jnp.concatenate([s.at[0][rows, :], s.at[1][rows, :]], axis=1)


def _put_class(s, d, cls, x):
    rows = _class_rows(d, cls)
    s.at[0][rows, :] = x[:, :LANES]
    s.at[1][rows, :] = x[:, LANES:]


def _flush(scratch, refs, d):
    @pl.when(pl.program_id(1) == d // _classes_per_step(d) - 1)
    def _():
        for s, o in zip(scratch, refs):
            o[...] = jnp.concatenate([s[0], s[1]], axis=1)


def _band_operands(scratch, d, cls, has_prev):
    take = lambda s: _take_class(s, d, cls).astype(BF16)
    q = (_take_class(scratch[0], d, cls) * Q_SCALE).astype(BF16)
    if has_prev:
        k = jnp.concatenate([take(scratch[3]), take(scratch[1])], axis=0)
        v = jnp.concatenate([take(scratch[4]), take(scratch[2])], axis=0)
    else:
        k = jnp.concatenate([jnp.zeros((BLOCK, GROUP_W), BF16), take(scratch[1])], axis=0)
        v = jnp.concatenate([jnp.zeros((BLOCK, GROUP_W), BF16), take(scratch[2])], axis=0)
    return q, k, v


def _lane_columns(cols):
    lane = lax.broadcasted_iota(jnp.int32, (BLOCK, LANES), 1)
    out = jnp.zeros((BLOCK, LANES), F32)
    for h, c in enumerate(cols):
        out = jnp.where(lane == h, c, out)
    return out


def band_fwd(qkv, bias, pattern, *, name):
    T = qkv.shape[0]
    d = DILATIONS[pattern]
    rows_per_block = BLOCK * d
    seq_blocks = SEQ // rows_per_block
    has_prev = seq_blocks > 1
    n_in = 5 if has_prev else 3

    def body(*refs):
        ins, b_ref, o_ref, lse_ref = refs[:n_in], refs[n_in], refs[n_in + 1], refs[n_in + 2]
        staged, o_s = refs[n_in + 3:2 * n_in + 3], refs[2 * n_in + 3]
        bd, row, key = _wide_consts()
        valid = _band_valid_wide(pl.program_id(0) % seq_blocks == 0, row, key)
        _stage(ins, staged)
        bias_w = jnp.concatenate([b_ref[h] for h in HEADS], axis=1)
        for cls in _step_classes(d):
            q, k, v = _band_operands(staged, d, cls, has_prev)
            kbd, vbd = _block_diag(k, bd), _block_diag(v, bd)
            sc = jnp.where(valid, _dot(q, kbd, _NT) + bias_w, NEG)
            ms = [jnp.max(sc[:, _seg(h)], axis=1, keepdims=True) for h in HEADS]
            p = jnp.exp(sc - _widen(ms))
            ls = _head_rowsums(p)
            _put_class(o_s, d, cls, _dot(p.astype(BF16), vbd) / _feature_widen(ls))
            lse_ref[_class_rows(d, cls), :] = _lane_columns([ms[h] + jnp.log(ls[h]) for h in HEADS])
        _flush([o_s], [o_ref], d)

    sd = jax.ShapeDtypeStruct
    return pl.pallas_call(
        body, name=name, grid=(T // rows_per_block, d // _classes_per_step(d)), in_specs=_band_in_specs(d, pattern, has_prev),
        out_specs=[pl.BlockSpec((rows_per_block, GROUP_W), lambda tb, r: (tb, 0)),
                   pl.BlockSpec((rows_per_block, LANES), lambda tb, r: (tb, 0))],
        out_shape=[sd((T, GROUP_W), F32), sd((T, LANES), F32)],
        scratch_shapes=_halves_scratch(rows_per_block, n_in + 1),
        compiler_params=_params("parallel", "arbitrary"),
    )(*([qkv] * n_in), bias)


def band_bwd(qkv, bias, lse, do, dlse, pattern, *, name):
    T = qkv.shape[0]
    d = DILATIONS[pattern]
    rows_per_block = BLOCK * d
    seq_blocks = SEQ // rows_per_block
    has_prev = seq_blocks > 1
    n_in = 5 if has_prev else 3
    n_out = 5 if has_prev else 3

    def body(*refs):
        ins, b_ref, lse_ref, do_ref, dlse_ref = refs[:n_in], refs[n_in], refs[n_in + 1], refs[n_in + 2], refs[n_in + 3]
        outs = refs[n_in + 4:n_in + 4 + n_out]
        ds_ref = refs[n_in + 4 + n_out]
        scratch = refs[n_in + 5 + n_out:]
        staged, do_s, out_s = scratch[:n_in], scratch[n_in], scratch[n_in + 1:]
        first_step = jnp.logical_and(pl.program_id(0) == 0, pl.program_id(1) == 0)
        bd, row, key = _wide_consts()
        valid = _band_valid_wide(pl.program_id(0) % seq_blocks == 0, row, key)
        _stage(list(ins) + [do_ref], list(staged) + [do_s])
        bias_w = jnp.concatenate([b_ref[h] for h in HEADS], axis=1)
        ds = None
        for cls in _step_classes(d):
            q, k, v = _band_operands(staged, d, cls, has_prev)
            kbd, vbd = _block_diag(k, bd), _block_diag(v, bd)
            rows = _class_rows(d, cls)
            do = _take_class(do_s, d, cls).astype(BF16)
            lse_t, dlse_t = lse_ref[rows, :], dlse_ref[rows, :]
            lse_w = _widen([lse_t[:, h:h + 1] for h in HEADS])
            dlse_w = _widen([dlse_t[:, h:h + 1] for h in HEADS])
            p = jnp.where(valid, jnp.exp(_dot(q, kbd, _NT) + bias_w - lse_w), 0.0)
            dp = _dot(do, vbd, _NT)
            ds_c = p * (dp - _widen(_head_rowsums(p * dp)) + dlse_w)
            dsb, pb = ds_c.astype(BF16), p.astype(BF16)
            _put_class(out_s[0], d, cls, _dot(dsb, kbd) * Q_SCALE)
            dk = _fold_heads(_dot(dsb, q, _TN), bd)
            dv = _fold_heads(_dot(pb, do, _TN), bd)
            _put_class(out_s[1], d, cls, dk[BLOCK:])
            _put_class(out_s[2], d, cls, dv[BLOCK:])
            if has_prev:
                _put_class(out_s[3], d, cls, dk[:BLOCK])
                _put_class(out_s[4], d, cls, dv[:BLOCK])
            ds = ds_c if ds is None else ds + ds_c
        _flush(out_s, outs, d)

        @pl.when(first_step)
        def _():
            for h in HEADS:
                ds_ref[h] = ds[:, _seg(h)]

        @pl.when(jnp.logical_not(first_step))
        def _():
            for h in HEADS:
                ds_ref[h] += ds[:, _seg(h)]

    big = pl.BlockSpec((rows_per_block, GROUP_W), lambda tb, r: (tb, 0))
    colb = pl.BlockSpec((rows_per_block, LANES), lambda tb, r: (tb, 0))
    sd = jax.ShapeDtypeStruct
    return pl.pallas_call(
        body, name=name, grid=(T // rows_per_block, d // _classes_per_step(d)),
        in_specs=_band_in_specs(d, pattern, has_prev) + [colb, big, colb],
        out_specs=[big] * n_out + [pl.BlockSpec((N_HEADS, BLOCK, BAND), lambda tb, r: (0, 0, 0))],
        out_shape=[sd((T, GROUP_W), F32)] * n_out + [sd((N_HEADS, BLOCK, BAND), F32)],
        scratch_shapes=_halves_scratch(rows_per_block, n_in + 1 + n_out),
        compiler_params=_params("arbitrary", "arbitrary"),
    )(*([qkv] * n_in), bias, lse, do, dlse)


def _pattern_weights(lse_refs, h):
    ls = [r[:, h:h + 1] for r in lse_refs]
    mx = functools.reduce(jnp.maximum, ls)
    es = [jnp.exp(l - mx) for l in ls]
    tot = functools.reduce(lambda a, b: a + b, es)
    return [e / tot for e in es]


def dil_combine_fwd(outs, *, name):
    T = outs[0][0].shape[0]
    n = len(outs)
    tm = 512

    def body(*refs):
        o_refs, l_refs, out_ref = refs[:n], refs[n:2 * n], refs[2 * n]
        for h in range(N_HEADS):
            w = _pattern_weights(l_refs, h)
            acc = w[0] * o_refs[0][:, _hs(h)]
            for p in range(1, n):
                acc = acc + w[p] * o_refs[p][:, _hs(h)]
            out_ref[:, _hs(h)] = acc.astype(BF16)

    big = pl.BlockSpec((tm, GROUP_W), lambda i: (i, 0))
    colb = pl.BlockSpec((tm, LANES), lambda i: (i, 0))
    return pl.pallas_call(
        body, name=name, grid=(T // tm,), in_specs=[big] * n + [colb] * n,
        out_specs=big, out_shape=jax.ShapeDtypeStruct((T, GROUP_W), BF16),
        compiler_params=_params("parallel"),
    )(*[o for o, _ in outs], *[l for _, l in outs])


def dil_combine_bwd(outs, dmixed, *, name):
    T = outs[0][0].shape[0]
    n = len(outs)
    tm = 512

    def body(*refs):
        o_refs, l_refs, do_ref = refs[:n], refs[n:2 * n], refs[2 * n]
        do_refs, dl_refs = refs[2 * n + 1:3 * n + 1], refs[3 * n + 1:]
        for r in dl_refs:
            r[...] = jnp.zeros_like(r)
        for h in range(N_HEADS):
            w = _pattern_weights(l_refs, h)
            do = do_ref[:, _hs(h)]
            dw = [jnp.sum(do * o_refs[p][:, _hs(h)], axis=1, keepdims=True) for p in range(n)]
            mean = functools.reduce(lambda a, b: a + b, [w[p] * dw[p] for p in range(n)])
            for p in range(n):
                do_refs[p][:, _hs(h)] = w[p] * do
                dl_refs[p][:, h:h + 1] = w[p] * (dw[p] - mean)

    big = pl.BlockSpec((tm, GROUP_W), lambda i: (i, 0))
    colb = pl.BlockSpec((tm, LANES), lambda i: (i, 0))
    sd = jax.ShapeDtypeStruct
    res = pl.pallas_call(
        body, name=name, grid=(T // tm,),
        in_specs=[big] * n + [colb] * n + [pl.BlockSpec((tm, GROUP_W), lambda i: (i, 2))],
        out_specs=[big] * n + [colb] * n, out_shape=[sd((T, GROUP_W), F32)] * n + [sd((T, LANES), F32)] * n,
        compiler_params=_params("parallel"),
    )(*[o for o, _ in outs], *[l for _, l in outs], dmixed)
    return list(zip(res[:n], res[n:]))


def dilated_fwd(qkv, bias, tag):
    return [band_fwd(qkv, bias, p, name=f"{tag}_band_fwd{p}") for p in range(len(DILATIONS))]


def dilated_bwd(qkv, bias, outs, dmixed, tag):
    grads = dil_combine_bwd(outs, dmixed, name=f"{tag}_combine_bwd")
    parts, ds_all = [], []
    for p, d in enumerate(DILATIONS):
        (_, lse), (do, dlse) = outs[p], grads[p]
        res = band_bwd(qkv, bias, lse, do, dlse, p, name=f"{tag}_band_bwd{p}")
        parts.append((list(res[:-1]), d))
        ds_all.append(res[-1])
    return parts, jnp.concatenate(ds_all, axis=0)


def assemble_dqkv(d_sb, d_fox, d_dil, *, name):
    T = d_sb[0].shape[0]
    nb = T // BLOCK
    flat = list(d_sb) + list(d_fox)
    specs = [pl.BlockSpec((BLOCK, GROUP_W), lambda i: (i, 0))] * 6
    layout = []
    for arrs, shift in d_dil:
        layout.append((len(flat), len(arrs) > 3, shift))
        flat += arrs
        specs += [pl.BlockSpec((BLOCK, GROUP_W), lambda i: (i, 0))] * 3
        if len(arrs) > 3:
            specs += [pl.BlockSpec((BLOCK, GROUP_W), lambda i, s=shift: (jnp.minimum(i + s, nb - 1), 0))] * 2

    def body(*refs):
        o_ref = refs[-1]
        i = pl.program_id(0)
        for j in range(6):
            o_ref[:, j * GROUP_W:(j + 1) * GROUP_W] = refs[j][...].astype(BF16)
        acc = [None, None, None]
        for first, has_prev, shift in layout:
            for j in range(3):
                v = refs[first + j][...]
                if has_prev and j > 0:
                    v = v + (i + shift < nb).astype(F32) * refs[first + 2 + j][...]
                acc[j] = v if acc[j] is None else acc[j] + v
        for j in range(3):
            o_ref[:, (6 + j) * GROUP_W:(7 + j) * GROUP_W] = acc[j].astype(BF16)

    return pl.pallas_call(
        body, name=name, grid=(nb,), in_specs=specs,
        out_specs=pl.BlockSpec((BLOCK, QKV_BLOCKS * GROUP_W), lambda i: (i, 0)),
        out_shape=jax.ShapeDtypeStruct((T, QKV_BLOCKS * GROUP_W), BF16), compiler_params=_params("parallel"),
    )(*flat)


def sum_cast(arrs, dtype, *, name):
    R, C = arrs[0].shape
    tr = _largest_tile(R, 512, 16)
    n = len(arrs)

    def body(*refs):
        acc = refs[0][...].astype(F32)
        for r in refs[1:n]:
            acc = acc + r[...].astype(F32)
        refs[n][...] = acc.astype(dtype)

    blk = pl.BlockSpec((tr, C), lambda i: (i, 0))
    return pl.pallas_call(
        body, name=name, grid=(R // tr,), in_specs=[blk] * n, out_specs=blk, out_shape=jax.ShapeDtypeStruct((R, C), dtype),
        compiler_params=_params("parallel"),
    )(*arrs)


GRAD_WIRE = BF16


def _block_diag_halves(w):
    z = jnp.zeros((HEAD_DIM, HEAD_DIM), w.dtype)
    half = lambda a, b: jnp.concatenate([jnp.concatenate([a, z], axis=1), jnp.concatenate([z, b], axis=1)], axis=0)
    return jnp.stack([half(w[0], w[1]), half(w[2], w[3])]).astype(BF16)


def _diag_blocks(d):
    h = HEAD_DIM
    return jnp.stack([d[0, :h, :h], d[0, h:, h:], d[1, :h, :h], d[1, h:, h:]])


def layer_fwd(x, mem2d, W, P, bias, tag):
    s = {}
    s["x"] = x
    h1 = rmsnorm_fwd(x, P["norm_mix_g"], name=f"{tag}_norm_mix")
    qkv = matmul(h1, W["qkv"], out_dtype=BF16, name=f"{tag}_qkv")
    aux = matmul(h1, W["aux"], name=f"{tag}_aux")
    o_sb = sbw_fwd(qkv, name=f"{tag}_sb_fwd")
    cumc = fox_prep(aux, P["bf"], name=f"{tag}_fox_prep")
    cumr = col_to_row(cumc)
    o_fox, o_fox32, lse_fox = foxw_fwd(qkv, cumc, cumr, name=f"{tag}_fox_fwd")
    dil = dilated_fwd(qkv, bias, tag)
    o_dil = dil_combine_fwd(dil, name=f"{tag}_dil_combine")
    o_lru, h_lru = lru_fwd(aux, P["lru_conv_w"], P["lru_conv_b"], P["wa"], P["lru_b_a"], P["wx"], P["lru_b_x"],
                           P["lru_lambda"], name=f"{tag}_lru_fwd")
    mixed = jnp.concatenate([o_sb, o_fox, o_dil, o_lru], axis=1)
    if "rest" in W:
        W.update(W.pop("rest")(mixed))
    x1 = matmul(mixed, W["out"], residual=x, name=f"{tag}_out")
    hq = rmsnorm_fwd(x1, P["norm_cross_g"], name=f"{tag}_norm_cross")
    qc = matmul(hq, W["cq"], out_dtype=BF16, name=f"{tag}_cq")
    memn = rmsnorm_fwd(mem2d, P["norm_mem_g"], name=f"{tag}_norm_mem")
    kv = matmul(memn, W["ckv"], out_dtype=BF16, name=f"{tag}_ckv")
    oc = cross_fwd(qc, kv, name=f"{tag}_cross_fwd")
    x2 = matmul(oc, W["coT"], trans_b=True, residual=x1, name=f"{tag}_co")
    h2 = rmsnorm_fwd(x2, P["norm_ffn_g"], name=f"{tag}_norm_ffn")
    hu = matmul(h2, W["up_u"], trans_b=True, name=f"{tag}_up_u")
    hg = matmul(h2, W["up_g"], trans_b=True, name=f"{tag}_up_g")
    act = glu_fwd(hu, hg, P["wu"], P["wg"], P["bu"], P["bg"], name=f"{tag}_glu_fwd")
    x3 = matmul(act, W["down"], residual=x2, name=f"{tag}_down")
    s.update(h1=h1, qkv=qkv, aux=aux, cumc=cumc, cumr=cumr, lse_fox=lse_fox, o_fox32=o_fox32, dil=dil, h_lru=h_lru, mixed=mixed,
             x1=x1, hq=hq, qc=qc, memn=memn, kv=kv, oc=oc, x2=x2, h2=h2, hu=hu, hg=hg, act=act)
    return x3, s


def layer_bwd(dx3, mem2d, W, P, bias, s, tag, hooks=None):
    mm = functools.partial(matmul, out_dtype=GRAD_WIRE, trans_a=True)
    gW, gP = {}, {}
    hooks = hooks or {}
    dact = matmul(dx3, W["down"], trans_b=True, name=f"{tag}_d_act")
    gW["down"] = mm(s["act"], dx3, name=f"{tag}_g_down")
    dhu, dhg, dwu, dwg, dbu, dbg = glu_bwd(s["hu"], s["hg"], dact, P["wu"], P["wg"], P["bu"], P["bg"], name=f"{tag}_glu_bwd")
    gP["ffn_conv_w"] = jnp.concatenate([dwu, dwg], axis=1)
    gP["ffn_conv_b"] = jnp.concatenate([dbu, dbg], axis=1)
    dh2 = matmul(dhu, W["up_u"], name=f"{tag}_d_h2u")
    dh2 = matmul(dhg, W["up_g"], residual=dh2, name=f"{tag}_d_h2g")
    gW["up_u"] = mm(dhu, s["h2"], name=f"{tag}_g_up_u")
    gW["up_g"] = mm(dhg, s["h2"], name=f"{tag}_g_up_g")
    dx2, gP["norm_ffn_g"] = rmsnorm_bwd(s["x2"], P["norm_ffn_g"], dh2, dx3, name=f"{tag}_norm_ffn_bwd")
    if "ffn" in hooks:
        hooks["ffn"](gW, W, s)
    doc = matmul(dx2, W["coT"], name=f"{tag}_d_oc")
    gW["coT"] = mm(dx2, s["oc"], name=f"{tag}_g_co")
    dqc, dkv = cross_bwd(s["qc"], s["kv"], doc, name=f"{tag}_cross_bwd")
    dhq = matmul(dqc, W["cq"], trans_b=True, name=f"{tag}_d_hq")
    gW["cq"] = mm(s["hq"], dqc, name=f"{tag}_g_cq")
    dmemn = matmul(dkv, W["ckv"], trans_b=True, name=f"{tag}_d_memn")
    gW["ckv"] = mm(s["memn"], dkv, name=f"{tag}_g_ckv")
    _, gP["norm_mem_g"] = rmsnorm_bwd(mem2d, P["norm_mem_g"], dmemn, None, name=f"{tag}_norm_mem_bwd")
    dx1, gP["norm_cross_g"] = rmsnorm_bwd(s["x1"], P["norm_cross_g"], dhq, dx2, name=f"{tag}_norm_cross_bwd")
    dmixed = matmul(dx1, W["out"], trans_b=True, name=f"{tag}_d_mixed")
    gW["out"] = mm(s["mixed"], dx1, name=f"{tag}_g_out")
    if "mid" in hooks:
        hooks["mid"](gW, W, s)
    qkv, aux = s["qkv"], s["aux"]
    d_sb = sbw_bwd(qkv, dmixed, name=f"{tag}_sb_bwd")
    dfq, dfk, dfv, dcc, dcr = foxw_bwd(qkv, s["cumc"], s["cumr"], s["lse_fox"], s["o_fox32"], dmixed, name=f"{tag}_fox_bwd")
    dcum = sum_cast([dcc, row_to_col(dcr)], F32, name=f"{tag}_dcum")
    df, dbf = fox_prep_bwd(aux, P["bf"], dcum, name=f"{tag}_fox_prep_bwd")
    gP["b_forget"] = dbf[0, :N_HEADS]
    d_dil, ds_band = dilated_bwd(qkv, bias, s["dil"], dmixed, tag)
    dlx, dlg, dcw, dcb, dwa, dba, dwx, dbx, dlam = lru_bwd(
        aux, s["h_lru"], dmixed, P["lru_conv_w"], P["lru_conv_b"], P["wa"], P["lru_b_a"], P["wx"], P["lru_b_x"],
        P["lru_lambda"], name=f"{tag}_lru_bwd")
    gP.update(lru_conv_w=dcw, lru_conv_b=dcb, lru_w_a=_diag_blocks(dwa), lru_b_a=dba, lru_w_x=_diag_blocks(dwx),
              lru_b_x=dbx, lru_lambda=dlam)
    dqkv = assemble_dqkv(d_sb, [dfq, dfk, dfv], d_dil, name=f"{tag}_dqkv")
    daux = jnp.concatenate([dlx, dlg, df], axis=1)
    dh1 = matmul(dqkv, W["qkv"], trans_b=True, name=f"{tag}_d_h1a")
    dh1 = matmul(daux, W["aux"], trans_b=True, residual=dh1, name=f"{tag}_d_h1b")
    gW["qkv"] = mm(s["h1"], dqkv, name=f"{tag}_g_qkv")
    gW["aux"] = mm(s["h1"], daux, name=f"{tag}_g_aux")
    dx, gP["norm_mix_g"] = rmsnorm_bwd(s["x"], P["norm_mix_g"], dh1, dx1, name=f"{tag}_norm_mix_bwd")
    return dx, gW, gP, ds_band


def local_step(x, mem, target, weights_of, Ps, rel_bias, final_norm_g, grads_done=None, bwd_hooks=None):
    B = x.shape[0]
    x2d = x.reshape(B * SEQ, D_MODEL)
    mem2d = mem.reshape(B * N_MEM, D_MODEL)
    bias = relbias_expand(rel_bias, name="relbias_expand")
    saved, Ws = [], []
    h = x2d
    for l in range(DEPTH):
        Ws.append(weights_of(l, h))
        h, s = layer_fwd(h, mem2d, Ws[l], Ps[l], bias, f"l{l}")
        saved.append(s)
    loss, dh, d_final = loss_head(h, final_norm_g, target.reshape(B * SEQ, D_MODEL), name="loss_head")
    gWs, gPs, ds_bands = [None] * DEPTH, [None] * DEPTH, []
    for l in range(DEPTH - 1, -1, -1):
        hooks = None if bwd_hooks is None else bwd_hooks(l)
        dh, gWs[l], gPs[l], ds = layer_bwd(dh, mem2d, Ws[l], Ps[l], bias, saved[l], f"l{l}", hooks)
        if grads_done is not None:
            grads_done(l, gWs[l])
        ds_bands.append(ds)
    d_rel = relbias_reduce(sum_cast([d.reshape(-1, BAND) for d in ds_bands], F32, name="ds_band_sum").reshape(-1, BLOCK, BAND),
                           name="relbias_reduce")
    return loss, dh.reshape(B, SEQ, D_MODEL), gWs, gPs, d_rel, d_final


def small_params(p, l):
    row = lambda name: p[name][l].reshape(1, -1)
    ffn_w, ffn_b = p["ffn_conv_w"][l], row("ffn_conv_b")
    return dict(
        norm_mix_g=row("norm_mix_g"), norm_cross_g=row("norm_cross_g"), norm_mem_g=row("norm_mem_g"), norm_ffn_g=row("norm_ffn_g"),
        bf=jnp.pad(row("b_forget"), ((0, 0), (0, LANES - N_HEADS))),
        lru_conv_w=p["lru_conv_w"][l], lru_conv_b=row("lru_conv_b"), wa=_block_diag_halves(p["lru_w_a"][l]), lru_b_a=row("lru_b_a"),
        wx=_block_diag_halves(p["lru_w_x"][l]), lru_b_x=row("lru_b_x"), lru_lambda=row("lru_lambda"),
        wu=ffn_w[:, :D_FF], wg=ffn_w[:, D_FF:], bu=ffn_b[:, :D_FF], bg=ffn_b[:, D_FF:])


def canonical_weights(w_in, w_out, w_cq, w_ck, w_cv, w_co, w_up, w_down):
    sb_fox, fox_f, rest = w_in[:, :6 * GROUP_W], w_in[:, 6 * GROUP_W:6 * GROUP_W + N_HEADS], w_in[:, 6 * GROUP_W + N_HEADS:]
    dil, lru = rest[:, :3 * GROUP_W], rest[:, 3 * GROUP_W:]
    pad = jnp.zeros((w_in.shape[0], AUX_W - 2 * GROUP_W - N_HEADS), w_in.dtype)
    return dict(qkv=jnp.concatenate([sb_fox, dil], axis=1), aux=jnp.concatenate([lru, fox_f, pad], axis=1), out=w_out,
                cq=w_cq, ckv=jnp.concatenate([w_ck, w_cv], axis=1), coT=w_co.T, upT=w_up.T, down=w_down)


def native_grads(g):
    qkv, aux = g["qkv"], g["aux"]
    a, b = 6 * GROUP_W, 6 * GROUP_W + N_HEADS
    w_in = jnp.zeros((qkv.shape[0], b + 5 * GROUP_W), qkv.dtype)
    w_in = w_in.at[:, :a].set(qkv[:, :a]).at[:, a:b].set(aux[:, 2 * GROUP_W:2 * GROUP_W + N_HEADS])
    w_in = w_in.at[:, b:b + 3 * GROUP_W].set(qkv[:, a:]).at[:, b + 3 * GROUP_W:].set(aux[:, :2 * GROUP_W])
    return (w_in, g["out"], g["cq"], g["ckv"][:, :GROUP_W], g["ckv"][:, GROUP_W:], g["coT"].T) + native_ffn_grads(g)


def native_ffn_grads(g):
    return (g["upT"].T, g["down"])


ANY = pl.BlockSpec(memory_space=pl.ANY)
VMEM_SPEC = pl.BlockSpec(memory_space=pltpu.VMEM)


def _place():
    x, y, c = lax.axis_index("x"), lax.axis_index("y"), lax.axis_index("c")
    other_chips = [(1 - x, y), (x, 1 - y), (1 - x, 1 - y)]
    return x, y, c, other_chips


def _gather_body(x_ref, out_ref, send_sems, recv_sems, local_sem):
    x, y, c, chips = _place()
    me, sibling = (x, y, c), (x, y, 1 - c)

    def slot(px, py, pc):
        return out_ref.at[4 * px + 2 * py + pc]

    def copy(k, block, to, src=None):
        return pltpu.make_async_remote_copy(
            src_ref=slot(*block) if src is None else src, dst_ref=slot(*block),
            send_sem=send_sems.at[k], recv_sem=recv_sems.at[k], device_id=to, device_id_type=MESH)

    if local_sem is not None:
        mine = pltpu.make_async_copy(x_ref, slot(*me), local_sem)
        mine.start()
    first = [copy(0, me, sibling, src=x_ref)]
    first += [copy(1 + j, me, (*chip, c), src=x_ref) for j, chip in enumerate(chips)]
    for cp in first:
        cp.start()
    passed = [copy(4 + j, (*chip, c), sibling) for j, chip in enumerate(chips)]
    for j, chip in enumerate(chips):
        copy(1 + j, (*chip, c), me).wait_recv()
        passed[j].start()
    copy(0, sibling, me).wait_recv()
    for j, chip in enumerate(chips):
        copy(4 + j, (*chip, 1 - c), me).wait_recv()
    for cp in first + passed:
        cp.wait_send()
    if local_sem is not None:
        mine.wait()


_GATHER_SEMS = [pltpu.SemaphoreType.DMA((7,)), pltpu.SemaphoreType.DMA((7,)), pltpu.SemaphoreType.DMA]


def allgather_hbm(shard, me, *, name):
    def body(x_ref, out_ref, done_ref, send_sems, recv_sems):
        _gather_body(x_ref, out_ref, send_sems, recv_sems, None)
        done_ref[...] = jnp.zeros_like(done_ref)

    others, done = pl.pallas_call(
        body, name=name, in_specs=[ANY], out_specs=[ANY, VMEM_SPEC],
        out_shape=[jax.ShapeDtypeStruct((N_DEV,) + shard.shape, shard.dtype), jax.ShapeDtypeStruct((8, LANES), F32)],
        scratch_shapes=_GATHER_SEMS[:2],
    )(shard)
    return lax.dynamic_update_slice(others, shard[None], (me, 0, 0)), done


def allgather_small(x, *, name, reduce=False):
    def body(x_ref, out_ref, second_ref, *sems):
        _gather_body(x_ref, out_ref, *sems)
        if reduce:
            acc = out_ref[0]
            for d in range(1, N_DEV):
                acc = acc + out_ref[d]
            second_ref[...] = acc
        else:
            second_ref[...] = jnp.zeros_like(second_ref)

    sd = jax.ShapeDtypeStruct
    return pl.pallas_call(
        body, name=name, in_specs=[VMEM_SPEC], out_specs=[VMEM_SPEC, VMEM_SPEC],
        out_shape=[sd((N_DEV,) + x.shape, x.dtype), sd(x.shape if reduce else (8, LANES), x.dtype)],
        scratch_shapes=_GATHER_SEMS, compiler_params=pltpu.CompilerParams(vmem_limit_bytes=VMEM_LIMIT_V7X),
    )(x)


N_CHIPS = 4


def pair_exchange(g, *, name):
    _, R, C = g.shape

    def body(g_ref, recv_ref, send_sems, recv_sems):
        x, y, c, _ = _place()
        sibling = (x, y, 1 - c)
        remote = [pltpu.make_async_remote_copy(
            src_ref=g_ref.at[2 * q + (1 - c)], dst_ref=recv_ref.at[q], send_sem=send_sems.at[q], recv_sem=recv_sems.at[q],
            device_id=sibling, device_id_type=MESH) for q in range(N_CHIPS)]
        for cp in remote:
            cp.start()
        for cp in remote:
            cp.wait_recv()
        for cp in remote:
            cp.wait_send()

    return pl.pallas_call(
        body, name=name, in_specs=[ANY], out_specs=ANY, out_shape=jax.ShapeDtypeStruct((N_CHIPS, R, C), g.dtype),
        scratch_shapes=[pltpu.SemaphoreType.DMA((N_CHIPS,))] * 2,
    )(g)


def chip_exchange(s, *, name):
    _, R, C = s.shape

    def body(s_ref, o0, o1, o2, send_sems, recv_sems):
        x, y, c, chips = _place()
        outs = (o0, o1, o2)
        copies = [pltpu.make_async_remote_copy(
            src_ref=s_ref.at[2 * cx + cy], dst_ref=outs[j], send_sem=send_sems.at[j], recv_sem=recv_sems.at[j],
            device_id=(cx, cy, c), device_id_type=MESH) for j, (cx, cy) in enumerate(chips)]
        for cp in copies:
            cp.start()
        for cp in copies:
            cp.wait_recv()
        for cp in copies:
            cp.wait_send()

    sd = jax.ShapeDtypeStruct((R, C), s.dtype)
    return pl.pallas_call(
        body, name=name, in_specs=[ANY], out_specs=[ANY] * 3, out_shape=[sd] * 3,
        scratch_shapes=[pltpu.SemaphoreType.DMA((3,)), pltpu.SemaphoreType.DMA((3,))],
    )(s)


HBM_SPEC = pl.BlockSpec(memory_space=pltpu.HBM)
SEM_SPEC = pl.BlockSpec(memory_space=pltpu.SEMAPHORE)
N_PEERS = N_DEV - 1


def _peers():
    x, y, c = lax.axis_index("x"), lax.axis_index("y"), lax.axis_index("c")
    flip = lambda v, bit: 1 - v if bit else v
    out = []
    for k in range(1, N_DEV):
        px, py, pc = flip(x, (k >> 2) & 1), flip(y, (k >> 1) & 1), flip(c, k & 1)
        out.append(((px, py, pc), 4 * px + 2 * py + pc))
    return out, 4 * x + 2 * y + c


def _peer_copies(src_ref, land_ref, send_sems, recv_sems, scatter, landing):
    peers, me = _peers()
    return [pltpu.make_async_remote_copy(
        src_ref=src_ref.at[idx] if scatter else src_ref, dst_ref=land_ref.at[me if landing == "mine" else idx],
        send_sem=send_sems.at[k], recv_sem=recv_sems.at[k], device_id=peer, device_id_type=MESH)
        for k, (peer, idx) in enumerate(peers)]


def exchange_start(src, scatter, *, name):
    shape = (N_DEV,) + src.shape[-2:]

    def body(src_ref, land_ref, send_sems, recv_sems, src_thru, land_thru, token):
        for cp in _peer_copies(src_ref, land_ref, send_sems, recv_sems, scatter, "mine"):
            cp.start()
        token[...] = jnp.zeros_like(token)

    sems = pltpu.SemaphoreType.DMA((N_PEERS,))
    return pl.pallas_call(
        body, name=name,
        out_shape=(sems, sems, pltpu.HBM(src.shape, src.dtype), pltpu.HBM(shape, src.dtype), jax.ShapeDtypeStruct((8, LANES), F32)),
        in_specs=(HBM_SPEC, HBM_SPEC), out_specs=(SEM_SPEC, SEM_SPEC, HBM_SPEC, HBM_SPEC, VMEM_SPEC),
        input_output_aliases={0: 2, 1: 3},
        compiler_params=pltpu.CompilerParams(has_side_effects=pltpu.SideEffectType.DATAFLOW_SIDE_EFFECTING),
    )(pltpu.with_memory_space_constraint(src, pltpu.HBM), pltpu.with_memory_space_constraint(lax.empty(shape, src.dtype), pltpu.HBM))


def exchange_wait(started, after, scatter, *, name):
    send_sems, recv_sems, src_thru, land_thru, _ = started

    def body(src_ref, land_ref, send_sems, recv_sems, after_ref, src_dead, got_ref):
        for cp in _peer_copies(src_ref, land_ref, send_sems, recv_sems, scatter, "theirs"):
            cp.wait_send()
            cp.wait_recv()

    return pl.pallas_call(
        body, name=name, out_shape=(pltpu.HBM(src_thru.shape, src_thru.dtype), pltpu.HBM(land_thru.shape, land_thru.dtype)),
        in_specs=(HBM_SPEC, HBM_SPEC, SEM_SPEC, SEM_SPEC, ANY), out_specs=(HBM_SPEC, HBM_SPEC), input_output_aliases={0: 0, 1: 1},
        compiler_params=pltpu.CompilerParams(has_side_effects=pltpu.SideEffectType.DATAFLOW_SIDE_EFFECTING),
    )(src_thru, land_thru, send_sems, recv_sems, after)[1]


def sum_blocks(blocks, *, name):
    n, R, C = blocks.shape
    tr = _largest_tile(R, 512, 16)

    def body(b_ref, o_ref):
        acc = b_ref[0].astype(F32)
        for d in range(1, n):
            acc = acc + b_ref[d].astype(F32)
        o_ref[...] = acc

    return pl.pallas_call(
        body, name=name, grid=(R // tr,),
        in_specs=[pl.BlockSpec((n, tr, C), lambda i: (0, i, 0))], out_specs=pl.BlockSpec((tr, C), lambda i: (i, 0)),
        out_shape=jax.ShapeDtypeStruct((R, C), F32), compiler_params=_params("parallel"),
    )(blocks)


WEIGHTS = ("norm_mix_g", "w_in", "b_forget", "lru_conv_w", "lru_conv_b", "lru_w_a", "lru_b_a", "lru_w_x", "lru_b_x", "lru_lambda",
           "w_out", "norm_cross_g", "norm_mem_g", "w_cq", "w_ck", "w_cv", "w_co", "norm_ffn_g", "w_up", "ffn_conv_w", "ffn_conv_b",
           "w_down", "rel_bias", "final_norm_g")
LARGE = ("w_in", "w_out", "w_cq", "w_ck", "w_cv", "w_co", "w_up", "w_down")
COLUMN_SPLIT_SMALL = ("lru_conv_w", "ffn_conv_w")
PACK = (("qkv", 128, 2304), ("aux", 128, 640), ("out", 128, 1024), ("cq", 128, 256), ("ckv", 128, 512), ("coT", 128, 256),
        ("upT", 704, 1024), ("down", 352, 1024))
PACK_W = 1024


def _pack_rows(parts):
    return jnp.concatenate([p.reshape(-1, PACK_W) for p in parts], axis=0)


def _pad_rows(flat, mult=8 * LANES):
    n = flat.shape[0]
    return jnp.pad(flat, (0, (-n) % mult)).reshape(-1, LANES)


def kernel(x, mem, norm_mix_g, w_in, b_forget, lru_conv_w, lru_conv_b, lru_w_a, lru_b_a, lru_w_x, lru_b_x, lru_lambda, w_out, norm_cross_g, norm_mem_g, w_cq, w_ck, w_cv, w_co, norm_ffn_g, w_up, ffn_conv_w, ffn_conv_b, w_down, rel_bias, final_norm_g, loss_target, m_norm_mix_g, m_w_in, m_b_forget, m_lru_conv_w, m_lru_conv_b, m_lru_w_a, m_lru_b_a, m_lru_w_x, m_lru_b_x, m_lru_lambda, m_w_out, m_norm_cross_g, m_norm_mem_g, m_w_cq, m_w_ck, m_w_cv, m_w_co, m_norm_ffn_g, m_w_up, m_ffn_conv_w, m_ffn_conv_b, m_w_down, m_rel_bias, m_final_norm_g, v_norm_mix_g, v_w_in, v_b_forget, v_lru_conv_w, v_lru_conv_b, v_lru_w_a, v_lru_b_a, v_lru_w_x, v_lru_b_x, v_lru_lambda, v_w_out, v_norm_cross_g, v_norm_mem_g, v_w_cq, v_w_ck, v_w_cv, v_w_co, v_norm_ffn_g, v_w_up, v_ffn_conv_w, v_ffn_conv_b, v_w_down, v_rel_bias, v_final_norm_g):
    w = dict(norm_mix_g=norm_mix_g, w_in=w_in, b_forget=b_forget, lru_conv_w=lru_conv_w, lru_conv_b=lru_conv_b, lru_w_a=lru_w_a,
             lru_b_a=lru_b_a, lru_w_x=lru_w_x, lru_b_x=lru_b_x, lru_lambda=lru_lambda, w_out=w_out, norm_cross_g=norm_cross_g,
             norm_mem_g=norm_mem_g, w_cq=w_cq, w_ck=w_ck, w_cv=w_cv, w_co=w_co, norm_ffn_g=norm_ffn_g, w_up=w_up,
             ffn_conv_w=ffn_conv_w, ffn_conv_b=ffn_conv_b, w_down=w_down, rel_bias=rel_bias, final_norm_g=final_norm_g)
    m = dict(norm_mix_g=m_norm_mix_g, w_in=m_w_in, b_forget=m_b_forget, lru_conv_w=m_lru_conv_w, lru_conv_b=m_lru_conv_b,
             lru_w_a=m_lru_w_a, lru_b_a=m_lru_b_a, lru_w_x=m_lru_w_x, lru_b_x=m_lru_b_x, lru_lambda=m_lru_lambda, w_out=m_w_out,
             norm_cross_g=m_norm_cross_g, norm_mem_g=m_norm_mem_g, w_cq=m_w_cq, w_ck=m_w_ck, w_cv=m_w_cv, w_co=m_w_co,
             norm_ffn_g=m_norm_ffn_g, w_up=m_w_up, ffn_conv_w=m_ffn_conv_w, ffn_conv_b=m_ffn_conv_b, w_down=m_w_down,
             rel_bias=m_rel_bias, final_norm_g=m_final_norm_g)
    v = dict(norm_mix_g=v_norm_mix_g, w_in=v_w_in, b_forget=v_b_forget, lru_conv_w=v_lru_conv_w, lru_conv_b=v_lru_conv_b,
             lru_w_a=v_lru_w_a, lru_b_a=v_lru_b_a, lru_w_x=v_lru_w_x, lru_b_x=v_lru_b_x, lru_lambda=v_lru_lambda, w_out=v_w_out,
             norm_cross_g=v_norm_cross_g, norm_mem_g=v_norm_mem_g, w_cq=v_w_cq, w_ck=v_w_ck, w_cv=v_w_cv, w_co=v_w_co,
             norm_ffn_g=v_norm_ffn_g, w_up=v_w_up, ffn_conv_w=v_ffn_conv_w, ffn_conv_b=v_ffn_conv_b, w_down=v_w_down,
             rel_bias=v_rel_bias, final_norm_g=v_final_norm_g)
    me = 4 * lax.axis_index("x") + 2 * lax.axis_index("y") + lax.axis_index("c")

    conv_shard = jnp.concatenate([w[n].reshape(-1) for n in COLUMN_SPLIT_SMALL])
    conv_all, conv_gathered = allgather_small(_pad_rows(conv_shard), name="gather_conv")
    conv_all = conv_all.reshape(N_DEV, -1)
    full = dict(w)
    off = 0
    for n in COLUMN_SPLIT_SMALL:
        d, k, c = w[n].shape
        blocks = conv_all[:, off:off + d * k * c].reshape(N_DEV, d, k, c)
        full[n] = blocks.transpose(1, 2, 0, 3).reshape(d, k, N_DEV * c)
        off += d * k * c

    IN, MID, FFN = PACK[:2], PACK[2:6], PACK[6:]
    REST = MID + FFN

    def packed_shard(l, group):
        canon = canonical_weights(*[w[n][l] for n in LARGE])
        return _pack_rows([canon[k].astype(BF16) for k, _, _ in group])

    def unpack_weights(packed, group):
        W, row = {}, 0
        half = N_DEV // 2
        for k, r, c in group:
            n_rows = r * c // PACK_W
            rows = packed[:, row:row + n_rows]
            if k == "upT":
                W["up_u"], W["up_g"] = rows[:half].reshape(half * r, c), rows[half:].reshape(half * r, c)
            else:
                W[k] = rows.reshape(N_DEV * r, c)
            row += n_rows
        return W

    def packed_grads(gW, group):
        g = dict(gW)
        if "up_u" in g:
            g["upT"] = jnp.concatenate([g.pop("up_u"), g.pop("up_g")], axis=0)
        return jnp.concatenate([g[k].reshape(N_DEV, r * c // PACK_W, PACK_W) for k, r, c in group], axis=1)

    def unpack_grads(shard_sum, group):
        g, row = {}, 0
        for k, r, c in group:
            n_rows = r * c // PACK_W
            g[k] = shard_sum[row:row + n_rows].reshape(r, c)
            row += n_rows
        return g

    def own_block_in(landed, block):
        return lax.dynamic_update_slice(landed, block[None], (me, 0, 0))

    def gathered_weights(copies, shard, after, group, name):
        return unpack_weights(own_block_in(exchange_wait(copies, after, False, name=name), shard), group)

    def scattered_sum(src, copies, after, tag):
        landed = exchange_wait(copies, after, True, name=f"{tag}_wait")
        mine = lax.dynamic_index_in_dim(src, me, axis=0, keepdims=False)
        return sum_blocks(own_block_in(landed, mine), name=f"{tag}_sum")

    last = DEPTH - 1
    in0, gathered = allgather_hbm(packed_shard(0, IN) + conv_gathered[0, 0].astype(BF16), me, name="gather_weights")
    rest0_shard = packed_shard(0, REST) + gathered[0, 0].astype(BF16)
    gather_rest0 = exchange_start(rest0_shard, False, name="gather_rest0_start")
    last_shard = packed_shard(last, PACK) + gather_rest0[4][0, 0].astype(BF16)
    gather_last = exchange_start(last_shard, False, name="gather_last_start")
    started = gather_last[4][0, 0]
    layer_weights = {}

    def weights_of(l, h):
        if l == 0:
            W = unpack_weights(in0, IN)
            W["rest"] = lambda after: gathered_weights(gather_rest0, rest0_shard, after, REST, "gather_rest0_wait")
        else:
            assert l == last
            W = gathered_weights(gather_last, last_shard, h, PACK, "gather_last_wait")
        layer_weights[l] = W
        return W

    in_flight = {}

    def scatter(key, g_all, name):
        in_flight[key] = (g_all, exchange_start(g_all, True, name=name))
        return in_flight[key][1][4][0, 0].astype(BF16)

    def grads_done(l, gW):
        if l == last:
            W0 = layer_weights[0]
            W0["down"] = W0["down"] + scatter("last", packed_grads(gW, PACK), "grads_last_start")

    def ffn0_grads_done(gW, W, s):
        W["coT"] = W["coT"] + scatter("ffn0", packed_grads({k: gW[k] for k in ("up_u", "up_g", "down")}, FFN), "grads_ffn0_start")

    def mid0_grads_done(gW, W, s):
        s["cumc"] = s["cumc"] + scatter("mid0", packed_grads({k: gW[k] for k, _, _ in MID}, MID), "grads_mid0_start").astype(F32)

    Ps = [small_params(full, l) for l in range(DEPTH)]
    Ps[0]["norm_mix_g"] = Ps[0]["norm_mix_g"] + started
    loss, grad_x, gWs, gPs, d_rel, d_final = local_step(
        x, mem, loss_target, weights_of, Ps, rel_bias, final_norm_g.reshape(1, -1), grads_done,
        lambda l: {"ffn": ffn0_grads_done, "mid": mid0_grads_done} if l == 0 else None)

    shard_grads = {last: unpack_grads(scattered_sum(*in_flight["last"], grad_x, "grads_last"), PACK)}
    shard_grads[0] = unpack_grads(scattered_sum(*in_flight["ffn0"], grad_x, "grads_ffn0"), FFN)
    shard_grads[0].update(unpack_grads(scattered_sum(*in_flight["mid0"], grad_x, "grads_mid0"), MID))

    g_all = packed_grads({k: gWs[0][k] for k, _, _ in IN}, IN)
    rows = g_all.shape[1]
    got = pair_exchange(g_all, name="grads_pair_exchange")
    own = lax.dynamic_index_in_dim(g_all.reshape(N_CHIPS, 2, rows, PACK_W), lax.axis_index("c"), axis=1, keepdims=False)
    pair = sum_cast([own.reshape(-1, PACK_W), got.reshape(-1, PACK_W)], GRAD_WIRE, name="grads_pair_sum").reshape(N_CHIPS, rows, PACK_W)
    from_x, from_y, from_xy = chip_exchange(pair, name="grads_chip_exchange")
    mine = lax.dynamic_index_in_dim(pair, 2 * lax.axis_index("x") + lax.axis_index("y"), axis=0, keepdims=False)
    shard_grads[0].update(unpack_grads(sum_cast([mine, from_x, from_y, from_xy], F32, name="grads_chip_sum"), IN))

    grads = {}
    per_layer = [native_grads(shard_grads[l]) for l in range(DEPTH)]
    for i, n in enumerate(LARGE):
        grads[n] = jnp.stack([per_layer[l][i] for l in range(DEPTH)])

    small_names = [n for n in WEIGHTS if n not in LARGE and n not in ("rel_bias", "final_norm_g")]
    pieces = [gPs[l][n].reshape(-1) for n in small_names for l in range(DEPTH)] + [d_rel.reshape(-1), d_final.reshape(-1), loss[0, :1]]
    sizes = [p.shape[0] for p in pieces]
    _, total = allgather_small(_pad_rows(jnp.concatenate(pieces)), name="allreduce_small", reduce=True)
    total = total.reshape(-1)
    off, it = 0, iter(sizes)
    for n in small_names:
        per = []
        for l in range(DEPTH):
            sz = next(it)
            per.append(total[off:off + sz])
            off += sz
        full_shape = (DEPTH,) + full[n].shape[1:]
        gfull = jnp.stack(per).reshape(full_shape)
        if n in COLUMN_SPLIT_SMALL:
            c = w[n].shape[-1]
            gfull = lax.dynamic_slice_in_dim(gfull, me * c, c, axis=gfull.ndim - 1)
        grads[n] = gfull
    grads["rel_bias"] = total[off:off + rel_bias.size].reshape(rel_bias.shape)
    off += rel_bias.size
    grads["final_norm_g"] = total[off:off + D_MODEL]
    off += D_MODEL
    loss_out = total[off]

    delta, new_m, new_v = {}, {}, {}
    for n in LARGE:
        shape = w[n].shape
        two_d = lambda a: a.reshape(-1, shape[-1])
        d_, m_, v_ = adamw(two_d(w[n]), two_d(grads[n]), two_d(m[n]), two_d(v[n]), name=f"adamw_{n}")
        delta[n], new_m[n], new_v[n] = d_.reshape(shape), m_.reshape(shape), v_.reshape(shape)
    small_all = [n for n in WEIGHTS if n not in LARGE]
    two_d = lambda a: a.reshape(-1, a.shape[-1])
    d_, m_, v_ = adamw_many(*[[two_d(src[n]) for n in small_all] for src in (w, grads, m, v)], name="adamw_small")
    for i, n in enumerate(small_all):
        delta[n], new_m[n], new_v[n] = (a[i].reshape(w[n].shape) for a in (d_, m_, v_))

    return (loss_out, grad_x, *[grads[n] for n in WEIGHTS], *[delta[n] for n in WEIGHTS], *[new_m[n] for n in WEIGHTS],
            *[new_v[n] for n in WEIGHTS])
```

```python
import functools
import math

import numpy as np
import jax
import jax.numpy as jnp
from jax import lax
from jax.experimental import pallas as pl
from jax.experimental.pallas import tpu as pltpu

F32 = jnp.float32
BF16 = jnp.bfloat16
MESH = pl.DeviceIdType.MESH

N_DEV = 8
D_MODEL = 1024
SEQ = 2048
DEPTH = 2
HEAD_DIM = 64
N_HEADS = 4
GROUP_W = N_HEADS * HEAD_DIM
D_FF = 2816
N_MEM = 256
NUM_BUCKETS = 32
MAX_DISTANCE = 2048
BLOCK = 128
DILATIONS = (1, 4, 16)
EPS = 1e-6
LRU_C = 8.0
Q_SCALE = HEAD_DIM ** -0.5
AUX_W = 640
LRU_HALF_W = 128
LRU_HALVES = GROUP_W // LRU_HALF_W
ADAM_LR, ADAM_B1, ADAM_B2, ADAM_EPS, ADAM_WD, ADAM_STEP = 0.001, 0.9, 0.999, 1e-08, 0.01, 10

VMEM_LIMIT_V7X = 48 * 1024 * 1024


def _params(*sem):
    return pltpu.CompilerParams(dimension_semantics=sem if sem else None, vmem_limit_bytes=VMEM_LIMIT_V7X)


def _pick(n, cands):
    for c in cands:
        if n % c == 0:
            return c
    return n


def _largest_tile(n, cap, align):
    best = None
    for t in range(align, min(n, cap) + 1, align):
        if n % t == 0:
            best = t
    return n if best is None else best


def matmul(a, b, *, name, trans_a=False, trans_b=False, out_dtype=F32, residual=None):
    (K, M) = a.shape if trans_a else a.shape[::-1]
    (N, Kb) = b.shape if trans_b else b.shape[::-1]
    assert K == Kb, (a.shape, b.shape)
    tm = _largest_tile(M, 1408 if trans_a else 1024, 128)
    tn = _largest_tile(N, 1408 if (trans_a or K <= 1024) else 512, 128)
    tk = _largest_tile(K, 1024 if trans_a else 2816, 128)
    nk = K // tk
    a_spec = pl.BlockSpec((tk, tm), lambda i, j, k: (k, i)) if trans_a else pl.BlockSpec((tm, tk), lambda i, j, k: (i, k))
    b_spec = pl.BlockSpec((tn, tk), lambda i, j, k: (j, k)) if trans_b else pl.BlockSpec((tk, tn), lambda i, j, k: (k, j))
    o_spec = pl.BlockSpec((tm, tn), lambda i, j, k: (i, j))
    dims = (((0 if trans_a else 1,), (1 if trans_b else 0,)), ((), ()))
    has_res = residual is not None

    def body(*refs):
        a_ref, b_ref = refs[0], refs[1]
        r_ref = refs[2] if has_res else None
        part = lax.dot_general(a_ref[...].astype(BF16), b_ref[...].astype(BF16), dims, preferred_element_type=F32)
        if nk == 1:
            if has_res:
                part = part + r_ref[...].astype(F32)
            refs[-1][...] = part.astype(out_dtype)
            return
        o_ref, acc_ref = refs[-2], refs[-1]
        k = pl.program_id(2)

        @pl.when(k == 0)
        def _():
            acc_ref[...] = part

        @pl.when(k > 0)
        def _():
            acc_ref[...] += part

        @pl.when(k == nk - 1)
        def _():
            r = acc_ref[...]
            if has_res:
                r = r + r_ref[...].astype(F32)
            o_ref[...] = r.astype(out_dtype)

    ops = (a, b) + ((residual,) if has_res else ())
    return pl.pallas_call(
        body, name=name, grid=(M // tm, N // tn, nk),
        in_specs=[a_spec, b_spec] + ([o_spec] if has_res else []),
        out_specs=o_spec, out_shape=jax.ShapeDtypeStruct((M, N), out_dtype),
        scratch_shapes=[pltpu.VMEM((tm, tn), F32)] if nk > 1 else [],
        compiler_params=_params("parallel", "parallel", "arbitrary"),
    )(*ops)


def rmsnorm_fwd(x, g, *, name):
    R, D = x.shape
    tr = _pick(R, (512, 256))

    def body(x_ref, g_ref, o_ref):
        xv = x_ref[...]
        r = lax.rsqrt(jnp.mean(xv * xv, axis=-1, keepdims=True) + EPS)
        o_ref[...] = (xv * r * g_ref[...]).astype(BF16)

    return pl.pallas_call(
        body, name=name, grid=(R // tr,),
        in_specs=[pl.BlockSpec((tr, D), lambda i: (i, 0)), pl.BlockSpec((1, D), lambda i: (0, 0))],
        out_specs=pl.BlockSpec((tr, D), lambda i: (i, 0)), out_shape=jax.ShapeDtypeStruct((R, D), BF16),
        compiler_params=_params("parallel"),
    )(x, g)


def rmsnorm_bwd(x, g, dh, dres, *, name):
    R, D = x.shape
    tr = _pick(R, (512, 256))
    has_res = dres is not None

    def body(*refs):
        x_ref, g_ref, dh_ref = refs[:3]
        dx_ref, dg_ref = refs[-2], refs[-1]
        xv = x_ref[...]
        r = lax.rsqrt(jnp.mean(xv * xv, axis=-1, keepdims=True) + EPS)
        n = xv * r
        dhv = dh_ref[...]
        dn = dhv * g_ref[...]
        dx = r * (dn - n * jnp.mean(dn * n, axis=-1, keepdims=True))
        if has_res:
            dx = dx + refs[3][...]
        dx_ref[...] = dx
        part = jnp.sum(dhv * n, axis=0, keepdims=True)

        @pl.when(pl.program_id(0) == 0)
        def _():
            dg_ref[...] = part

        @pl.when(pl.program_id(0) > 0)
        def _():
            dg_ref[...] += part

    row = pl.BlockSpec((tr, D), lambda i: (i, 0))
    vec = pl.BlockSpec((1, D), lambda i: (0, 0))
    ops = (x, g, dh) + ((dres,) if has_res else ())
    return pl.pallas_call(
        body, name=name, grid=(R // tr,),
        in_specs=[row, vec, row] + ([row] if has_res else []),
        out_specs=[row, vec],
        out_shape=[jax.ShapeDtypeStruct((R, D), F32), jax.ShapeDtypeStruct((1, D), F32)],
        compiler_params=_params("arbitrary"),
    )(*ops)


_SQRT_HALF = 0.7071067811865476
_INV_SQRT_2PI = 0.3989422804014327


def _normal_cdf_pdf(x):
    ax = jnp.abs(x) * _SQRT_HALF
    t = 1.0 / (1.0 + 0.3275911 * ax)
    poly = t * (0.254829592 + t * (-0.284496736 + t * (1.421413741 + t * (-1.453152027 + t * 1.061405429))))
    e = jnp.exp(-0.5 * x * x)
    half_tail = 0.5 * poly * e
    return jnp.where(x < 0, half_tail, 1.0 - half_tail), e


def _gelu_cdf(x):
    return _normal_cdf_pdf(x)[0]


def _gelu_and_grad(x):
    cdf, e = _normal_cdf_pdf(x)
    return x * cdf, cdf + x * _INV_SQRT_2PI * e


def _shift_down(main, halo, first, shifts):
    halo = jnp.where(first, 0.0, halo)
    ext = jnp.concatenate([halo, main], axis=0)
    return [pltpu.roll(ext, s, 0)[8:] for s in shifts]


def _conv3(main, halo, first, w, b):
    m1, m2 = _shift_down(main, halo, first, (1, 2))
    return ((b + w[0:1] * m2) + w[1:2] * m1) + w[2:3] * main, m1, m2


def glu_fwd(hu, hg, wu, wg, bu, bg, *, name):
    T, F = hu.shape
    tm, tf = 512, _largest_tile(F, 704, 128)
    hb = tm // 8
    blocks_per_example = SEQ // tm

    def body(hu_ref, hg_ref, hau_ref, hag_ref, wu_ref, wg_ref, bu_ref, bg_ref, o_ref):
        first = pl.program_id(0) % blocks_per_example == 0
        up, _, _ = _conv3(hu_ref[...], hau_ref[...], first, wu_ref[...], bu_ref[...])
        gate, _, _ = _conv3(hg_ref[...], hag_ref[...], first, wg_ref[...], bg_ref[...])
        o_ref[...] = (gate * _gelu_cdf(gate) * up).astype(BF16)

    main = pl.BlockSpec((tm, tf), lambda i, j: (i, j))
    halo = pl.BlockSpec((8, tf), lambda i, j: (jnp.maximum(i * hb - 1, 0), j))
    w3 = pl.BlockSpec((3, tf), lambda i, j: (0, j))
    b1 = pl.BlockSpec((1, tf), lambda i, j: (0, j))
    return pl.pallas_call(
        body, name=name, grid=(T // tm, F // tf),
        in_specs=[main, main, halo, halo, w3, w3, b1, b1],
        out_specs=main, out_shape=jax.ShapeDtypeStruct((T, F), BF16),
        compiler_params=_params("parallel", "parallel"),
    )(hu, hg, hu, hg, wu, wg, bu, bg)


def glu_bwd(hu, hg, dact, wu, wg, bu, bg, *, name):
    T, F = hu.shape
    tm, tf = 512, _largest_tile(F, 704, 128)
    hb = tm // 8
    blocks_per_example = SEQ // tm
    n_halo_blocks = T // 8
    n_ext = tm + 8

    def body(hu_ref, hg_ref, hau_ref, hag_ref, hnu_ref, hng_ref, da_ref, dan_ref, wu_ref, wg_ref, bu_ref, bg_ref,
             du_ref, dg_ref, dwu_ref, dwg_ref, dbu_ref, dbg_ref):
        i = pl.program_id(1)
        first = i % blocks_per_example == 0
        last = i % blocks_per_example == blocks_per_example - 1
        wu, wg = wu_ref[...], wg_ref[...]

        def conv_ext(main_ref, prev_ref, next_ref, w, b):
            ext = jnp.concatenate([jnp.where(first, 0.0, prev_ref[...]), main_ref[...], next_ref[...]], axis=0)
            x0, x1, x2 = ext[8:], pltpu.roll(ext, 1, 0)[8:], pltpu.roll(ext, 2, 0)[8:]
            return ((b + w[0:1] * x2) + w[1:2] * x1) + w[2:3] * x0, x0, x1, x2

        up, xu, u1, u2 = conv_ext(hu_ref, hau_ref, hnu_ref, wu, bu_ref[...])
        gate, xg, g1, g2 = conv_ext(hg_ref, hag_ref, hng_ref, wg, bg_ref[...])
        act, dact_dgate = _gelu_and_grad(gate)
        da = jnp.concatenate([da_ref[...], jnp.where(last, 0.0, dan_ref[...])], axis=0)
        dup = da * act
        dgate = da * up * dact_dgate

        def conv_t(d, w):
            return (w[2:3] * d[:tm] + w[1:2] * pltpu.roll(d, n_ext - 1, 0)[:tm] + w[0:1] * pltpu.roll(d, n_ext - 2, 0)[:tm]).astype(BF16)

        du_ref[...] = conv_t(dup, wu)
        dg_ref[...] = conv_t(dgate, wg)

        def sums(d, x0, x1, x2):
            s = lambda v: jnp.sum(v[:tm], axis=0, keepdims=True)
            return jnp.concatenate([s(d * x2), s(d * x1), s(d * x0)], axis=0), s(d)

        pwu, pbu = sums(dup, xu, u1, u2)
        pwg, pbg = sums(dgate, xg, g1, g2)

        @pl.when(i == 0)
        def _():
            dwu_ref[...] = pwu
            dwg_ref[...] = pwg
            dbu_ref[...] = pbu
            dbg_ref[...] = pbg

        @pl.when(i > 0)
        def _():
            dwu_ref[...] += pwu
            dwg_ref[...] += pwg
            dbu_ref[...] += pbu
            dbg_ref[...] += pbg

    main = pl.BlockSpec((tm, tf), lambda j, i: (i, j))
    before = pl.BlockSpec((8, tf), lambda j, i: (jnp.maximum(i * hb - 1, 0), j))
    after = pl.BlockSpec((8, tf), lambda j, i: (jnp.minimum((i + 1) * hb, n_halo_blocks - 1), j))
    w3 = pl.BlockSpec((3, tf), lambda j, i: (0, j))
    b1 = pl.BlockSpec((1, tf), lambda j, i: (0, j))
    sd = jax.ShapeDtypeStruct
    return pl.pallas_call(
        body, name=name, grid=(F // tf, T // tm),
        in_specs=[main, main, before, before, after, after, main, after, w3, w3, b1, b1],
        out_specs=[main, main, w3, w3, b1, b1],
        out_shape=[sd((T, F), BF16), sd((T, F), BF16), sd((3, F), F32), sd((3, F), F32), sd((1, F), F32), sd((1, F), F32)],
        compiler_params=_params("parallel", "arbitrary"),
    )(hu, hg, hu, hg, hu, hg, dact, dact, wu, wg, bu, bg)


def loss_head(x, g, target, *, name):
    T, D = x.shape
    tr = 256

    def body(x_ref, g_ref, t_ref, loss_ref, dx_ref, dg_ref):
        xv = x_ref[...]
        gv = g_ref[...]
        r = lax.rsqrt(jnp.mean(xv * xv, axis=-1, keepdims=True) + EPS)
        n = xv * r
        err = n * gv - t_ref[...]
        part_loss = jnp.zeros((1, 128), F32) + 0.5 * jnp.sum(jnp.mean(err * err, axis=-1, keepdims=True))
        dy = err * (1.0 / D)
        dn = dy * gv
        dx_ref[...] = r * (dn - n * jnp.mean(dn * n, axis=-1, keepdims=True))
        part_g = jnp.sum(dy * n, axis=0, keepdims=True)

        @pl.when(pl.program_id(0) == 0)
        def _():
            loss_ref[...] = part_loss
            dg_ref[...] = part_g

        @pl.when(pl.program_id(0) > 0)
        def _():
            loss_ref[...] += part_loss
            dg_ref[...] += part_g

    row = pl.BlockSpec((tr, D), lambda i: (i, 0))
    vec = pl.BlockSpec((1, D), lambda i: (0, 0))
    sd = jax.ShapeDtypeStruct
    return pl.pallas_call(
        body, name=name, grid=(T // tr,),
        in_specs=[row, vec, row],
        out_specs=[pl.BlockSpec((1, 128), lambda i: (0, 0)), row, vec],
        out_shape=[sd((1, 128), F32), sd((T, D), F32), sd((1, D), F32)],
        compiler_params=_params("arbitrary"),
    )(x, g, target)


def adamw(w, g, m, v, *, name):
    R, C = w.shape
    tr = _pick(R, (256, 128, 64, 32, 16, 8))

    def body(w_ref, g_ref, m_ref, v_ref, d_ref, nm_ref, nv_ref):
        gv = g_ref[...]
        mn = ADAM_B1 * m_ref[...] + (1.0 - ADAM_B1) * gv
        vn = ADAM_B2 * v_ref[...] + (1.0 - ADAM_B2) * (gv * gv)
        m_hat = mn / (1.0 - ADAM_B1 ** ADAM_STEP)
        v_hat = vn / (1.0 - ADAM_B2 ** ADAM_STEP)
        d_ref[...] = -ADAM_LR * (m_hat / (jnp.sqrt(v_hat) + ADAM_EPS) + ADAM_WD * w_ref[...])
        nm_ref[...] = mn
        nv_ref[...] = vn

    blk = pl.BlockSpec((tr, C), lambda i: (i, 0))
    sd = jax.ShapeDtypeStruct((R, C), F32)
    return pl.pallas_call(
        body, name=name, grid=(R // tr,), in_specs=[blk] * 4, out_specs=[blk] * 3, out_shape=[sd] * 3,
        compiler_params=_params("parallel"),
    )(w, g, m, v)


def adamw_many(ws, gs, ms, vs, *, name):
    n = len(ws)

    def body(*refs):
        ins, outs = refs[:4 * n], refs[4 * n:]
        for i in range(n):
            w_ref, g_ref, m_ref, v_ref = ins[i], ins[n + i], ins[2 * n + i], ins[3 * n + i]
            gv = g_ref[...]
            mn = ADAM_B1 * m_ref[...] + (1.0 - ADAM_B1) * gv
            vn = ADAM_B2 * v_ref[...] + (1.0 - ADAM_B2) * (gv * gv)
            m_hat = mn / (1.0 - ADAM_B1 ** ADAM_STEP)
            v_hat = vn / (1.0 - ADAM_B2 ** ADAM_STEP)
            outs[i][...] = -ADAM_LR * (m_hat / (jnp.sqrt(v_hat) + ADAM_EPS) + ADAM_WD * w_ref[...])
            outs[n + i][...] = mn
            outs[2 * n + i][...] = vn

    vm = pl.BlockSpec(memory_space=pltpu.VMEM)
    shapes = [jax.ShapeDtypeStruct(w.shape, F32) for w in ws]
    res = pl.pallas_call(
        body, name=name, in_specs=[vm] * (4 * n), out_specs=[vm] * (3 * n), out_shape=shapes * 3, compiler_params=_params(),
    )(*ws, *gs, *ms, *vs)
    return res[:n], res[n:2 * n], res[2 * n:]


def _softplus(x):
    return jnp.maximum(x, 0.0) + jnp.log(1.0 + jnp.exp(-jnp.abs(x)))


def _lru_gates(x, cw, cb, wa, ba, wx, bx, lam):
    S = x.shape[0]
    row = lax.broadcasted_iota(jnp.int32, (S, 1), 0)

    def back(s):
        return jnp.where(row >= s, pltpu.roll(x, s, 0), 0.0)

    xc = (((cb + cw[0:1] * back(3)) + cw[1:2] * back(2)) + cw[2:3] * back(1)) + cw[3:4] * x
    xb = xc.astype(BF16)
    r = jax.nn.sigmoid(jnp.dot(xb, wa, preferred_element_type=F32) + ba)
    ig = jax.nn.sigmoid(jnp.dot(xb, wx, preferred_element_type=F32) + bx)
    sp = _softplus(-lam)
    la = -LRU_C * r * sp
    a = jnp.exp(la)
    y = 2.0 * la
    one_minus_a2 = jnp.where(y > -0.05, -y * (1.0 + y * (0.5 + y * (1.0 / 6.0 + y * (1.0 / 24.0)))), 1.0 - jnp.exp(y))
    mm = jnp.sqrt(one_minus_a2)
    return xc, xb, r, ig, sp, a, mm


SCAN_UNROLL = 4


def _scan8(a, b, reverse):
    row = lax.broadcasted_iota(jnp.int32, (8, 1), 0)
    for k in (1, 2, 4):
        inside = row < 8 - k if reverse else row >= k
        shift = 8 - k if reverse else k
        a_n = jnp.where(inside, pltpu.roll(a, shift, 0), 1.0)
        b_n = jnp.where(inside, pltpu.roll(b, shift, 0), 0.0)
        b = a * b_n + b
        a = a * a_n
    return a, b


def lru_fwd(aux, cw, cb, wa, ba, wx, bx, lam, *, name):
    T = aux.shape[0]
    S, C = SEQ, LRU_HALF_W

    def body(x_ref, g_ref, cw_ref, cb_ref, wa_ref, ba_ref, wx_ref, bx_ref, lam_ref, o_ref, h_ref, a_s, u_s):
        xc, _, r, ig, sp, a, mm = _lru_gates(x_ref[...], cw_ref[...], cb_ref[...], wa_ref[...], ba_ref[...],
                                             wx_ref[...], bx_ref[...], lam_ref[...])
        a_s[...] = a
        u_s[...] = mm * (ig * xc)

        def group(i, h):
            for j in range(SCAN_UNROLL):
                base = pl.multiple_of((i * SCAN_UNROLL + j) * 8, 8)
                A, Bv = _scan8(a_s[pl.ds(base, 8), :], u_s[pl.ds(base, 8), :], reverse=False)
                H = A * h + Bv
                h_ref[pl.ds(base, 8), :] = H
                h = H[7:8]
            return h

        lax.fori_loop(0, S // 8 // SCAN_UNROLL, group, jnp.zeros((1, C), F32))
        gate = g_ref[...]
        o_ref[...] = (h_ref[...] * (gate * _gelu_cdf(gate))).astype(BF16)

    blk = lambda col: pl.BlockSpec((S, C), lambda c, b: (b, col + c))
    par = lambda rows: pl.BlockSpec((rows, C), lambda c, b: (0, c))
    sq = pl.BlockSpec((None, C, C), lambda c, b: (c, 0, 0))
    sd = jax.ShapeDtypeStruct
    W = LRU_HALVES * C
    return pl.pallas_call(
        body, name=name, grid=(LRU_HALVES, T // S),
        in_specs=[blk(0), blk(LRU_HALVES), par(4), par(1), sq, par(1), sq, par(1), par(1)],
        out_specs=[blk(0), blk(0)], out_shape=[sd((T, W), BF16), sd((T, W), F32)],
        scratch_shapes=[pltpu.VMEM((S, C), F32), pltpu.VMEM((S, C), F32)],
        compiler_params=_params("parallel", "parallel"),
    )(aux, aux, cw, cb, wa, ba, wx, bx, lam)


def lru_bwd(aux, h, dmixed, cw, cb, wa, ba, wx, bx, lam, *, name):
    T = aux.shape[0]
    S, C = SEQ, LRU_HALF_W

    def body(x_ref, g_ref, h_ref, do_ref, cw_ref, cb_ref, wa_ref, ba_ref, wx_ref, bx_ref, lam_ref,
             dx_ref, dgate_ref, dcw_ref, dcb_ref, dwa_ref, dba_ref, dwx_ref, dbx_ref, dlam_ref, a_s, d_s):
        x = x_ref[...]
        cw = cw_ref[...]
        lam = lam_ref[...]
        xc, xb, r, ig, sp, a, mm = _lru_gates(x, cw, cb_ref[...], wa_ref[...], ba_ref[...], wx_ref[...], bx_ref[...], lam)
        gate = g_ref[...]
        gl, dgl = _gelu_and_grad(gate)
        dout = do_ref[...]
        hv = h_ref[...]
        dgate_ref[...] = dout * hv * dgl
        a_s[...] = a
        d_s[...] = dout * gl

        last_row = lax.broadcasted_iota(jnp.int32, (8, 1), 0) == 7

        def group(i, c):
            for j in range(SCAN_UNROLL):
                base = pl.multiple_of((S // 8 - 1 - (i * SCAN_UNROLL + j)) * 8, 8)
                a8 = a_s[pl.ds(base, 8), :]
                d8 = d_s[pl.ds(base, 8), :]
                A, Bv = _scan8(a8, a8 * d8, reverse=True)
                Cv = A * c + Bv
                d_s[pl.ds(base, 8), :] = d8 + jnp.where(last_row, c, pltpu.roll(Cv, 7, 0))
                c = Cv[0:1]
            return c

        lax.fori_loop(0, S // 8 // SCAN_UNROLL, group, jnp.zeros((1, C), F32))
        row = lax.broadcasted_iota(jnp.int32, (S, 1), 0)
        dht = d_s[...]
        h_prev = jnp.where(row >= 1, pltpu.roll(hv, 1, 0), 0.0)
        da = dht * h_prev
        gx = ig * xc
        dmm = dht * gx
        dig = dht * mm * xc
        dxc = dht * mm * ig
        dla = da * a - dmm * (a * a) / mm
        dr = dla * (-LRU_C * sp)
        dsp = jnp.sum(dla * (-LRU_C * r), axis=0, keepdims=True)
        dlam = dsp * (-jax.nn.sigmoid(-lam))
        dpa = dr * r * (1.0 - r)
        dpx = dig * ig * (1.0 - ig)
        dpa_b, dpx_b = dpa.astype(BF16), dpx.astype(BF16)
        nt = (((1,), (1,)), ((), ()))
        tn = (((0,), (0,)), ((), ()))
        dxc = dxc + lax.dot_general(dpa_b, wa_ref[...], nt, preferred_element_type=F32) \
                  + lax.dot_general(dpx_b, wx_ref[...], nt, preferred_element_type=F32)
        dwa = lax.dot_general(xb, dpa_b, tn, preferred_element_type=F32)
        dwx = lax.dot_general(xb, dpx_b, tn, preferred_element_type=F32)

        def fwd(v, s):
            return jnp.where(row < S - s, pltpu.roll(v, S - s, 0), 0.0)

        def back(v, s):
            return jnp.where(row >= s, pltpu.roll(v, s, 0), 0.0)

        dx_ref[...] = cw[3:4] * dxc + cw[2:3] * fwd(dxc, 1) + cw[1:2] * fwd(dxc, 2) + cw[0:1] * fwd(dxc, 3)
        s0 = lambda v: jnp.sum(v, axis=0, keepdims=True)
        dcw = jnp.concatenate([s0(dxc * back(x, 3)), s0(dxc * back(x, 2)), s0(dxc * back(x, 1)), s0(dxc * x)], axis=0)
        parts = ((dcw_ref, dcw), (dcb_ref, s0(dxc)), (dwa_ref, dwa), (dba_ref, s0(dpa)), (dwx_ref, dwx),
                 (dbx_ref, s0(dpx)), (dlam_ref, dlam))

        @pl.when(pl.program_id(1) == 0)
        def _():
            for ref, val in parts:
                ref[...] = val

        @pl.when(pl.program_id(1) > 0)
        def _():
            for ref, val in parts:
                ref[...] += val

    blk = lambda col: pl.BlockSpec((S, C), lambda c, b: (b, col + c))
    par = lambda rows: pl.BlockSpec((rows, C), lambda c, b: (0, c))
    sq = pl.BlockSpec((None, C, C), lambda c, b: (c, 0, 0))
    sd = jax.ShapeDtypeStruct
    W = LRU_HALVES * C
    vec = sd((1, W), F32)
    return pl.pallas_call(
        body, name=name, grid=(LRU_HALVES, T // S),
        in_specs=[blk(0), blk(LRU_HALVES), blk(0), blk(3 * LRU_HALVES), par(4), par(1), sq, par(1), sq, par(1), par(1)],
        out_specs=[blk(0), blk(0), par(4), par(1), sq, par(1), sq, par(1), par(1)],
        out_shape=[sd((T, W), F32), sd((T, W), F32), sd((4, W), F32), vec, sd((LRU_HALVES, C, C), F32), vec,
                   sd((LRU_HALVES, C, C), F32), vec, vec],
        scratch_shapes=[pltpu.VMEM((S, C), F32), pltpu.VMEM((S, C), F32)],
        compiler_params=_params("parallel", "arbitrary"),
    )(aux, aux, h, dmixed, cw, cb, wa, ba, wx, bx, lam)


_NT = (((1,), (1,)), ((), ()))
_TN = (((0,), (0,)), ((), ()))


def _dot(a, b, dims=None):
    if dims is None:
        return jnp.dot(a, b, preferred_element_type=F32)
    return lax.dot_general(a, b, dims, preferred_element_type=F32)


def _hs(h):
    return slice(h * HEAD_DIM, (h + 1) * HEAD_DIM)


def cross_fwd(q, kv, *, name):
    T = q.shape[0]
    tq = 512

    def body(q_ref, kv_ref, o_ref):
        for h in range(N_HEADS):
            qh = q_ref[:, _hs(h)] * Q_SCALE
            k = kv_ref[:, _hs(h)]
            v = kv_ref[:, GROUP_W + h * HEAD_DIM:GROUP_W + (h + 1) * HEAD_DIM]
            s = _dot(qh, k, _NT)
            p = jnp.exp(s - jnp.max(s, axis=-1, keepdims=True))
            p = p / jnp.sum(p, axis=-1, keepdims=True)
            o_ref[:, _hs(h)] = _dot(p.astype(BF16), v).astype(BF16)

    per = SEQ // tq
    return pl.pallas_call(
        body, name=name, grid=(T // tq,),
        in_specs=[pl.BlockSpec((tq, GROUP_W), lambda i: (i, 0)), pl.BlockSpec((N_MEM, 2 * GROUP_W), lambda i: (i // per, 0))],
        out_specs=pl.BlockSpec((tq, GROUP_W), lambda i: (i, 0)), out_shape=jax.ShapeDtypeStruct((T, GROUP_W), BF16),
        compiler_params=_params("parallel"),
    )(q, kv)


def cross_bwd(q, kv, do, *, name):
    T = q.shape[0]
    tq = 512
    per = SEQ // tq

    def body(q_ref, kv_ref, do_ref, dq_ref, dkv_ref):
        first = pl.program_id(0) % per == 0
        for h in range(N_HEADS):
            vs = slice(GROUP_W + h * HEAD_DIM, GROUP_W + (h + 1) * HEAD_DIM)
            qh = q_ref[:, _hs(h)] * Q_SCALE
            k = kv_ref[:, _hs(h)]
            v = kv_ref[:, vs]
            doh = do_ref[:, _hs(h)].astype(BF16)
            s = _dot(qh, k, _NT)
            p = jnp.exp(s - jnp.max(s, axis=-1, keepdims=True))
            p = p / jnp.sum(p, axis=-1, keepdims=True)
            dp = _dot(doh, v, _NT)
            ds = (p * (dp - jnp.sum(p * dp, axis=-1, keepdims=True))).astype(BF16)
            dq_ref[:, _hs(h)] = (_dot(ds, k) * Q_SCALE).astype(BF16)
            dk = _dot(ds, qh, _TN)
            dv = _dot(p.astype(BF16), doh, _TN)

            @pl.when(first)
            def _():
                dkv_ref[:, _hs(h)] = dk
                dkv_ref[:, vs] = dv

            @pl.when(jnp.logical_not(first))
            def _():
                dkv_ref[:, _hs(h)] += dk
                dkv_ref[:, vs] += dv

    qb = pl.BlockSpec((tq, GROUP_W), lambda i: (i, 0))
    kvb = pl.BlockSpec((N_MEM, 2 * GROUP_W), lambda i: (i // per, 0))
    sd = jax.ShapeDtypeStruct
    return pl.pallas_call(
        body, name=name, grid=(T // tq,),
        in_specs=[qb, kvb, qb], out_specs=[qb, kvb],
        out_shape=[sd((T, GROUP_W), BF16), sd(kv.shape, F32)],
        compiler_params=_params("arbitrary"),
    )(q, kv, do)


NB = SEQ // BLOCK
NEG = -1e30
HEADS = tuple(range(N_HEADS))


def _blk(i):
    return pl.ds(pl.multiple_of(i * BLOCK, BLOCK), BLOCK)


def _qkv_specs(first_col):
    return [pl.BlockSpec((SEQ, GROUP_W), lambda b, c=first_col + j: (b, c)) for j in range(3)]


LANES = 128
CUM_BLK = 256


def col_to_row(c):
    b = c.shape[0] // SEQ
    return c.reshape(b, SEQ, LANES)[:, :, :8].transpose(0, 2, 1).reshape(b * 8, SEQ)


def row_to_col(r):
    b = r.shape[0] // 8
    c = r.reshape(b, 8, SEQ).transpose(0, 2, 1)
    return jnp.pad(c, ((0, 0), (0, 0), (0, LANES - 8))).reshape(b * SEQ, LANES)


def fox_prep(aux, bf, *, name):
    T = aux.shape[0]

    def body(f_ref, b_ref, o_ref):
        row = lax.broadcasted_iota(jnp.int32, (CUM_BLK, CUM_BLK), 0)
        col = lax.broadcasted_iota(jnp.int32, (CUM_BLK, CUM_BLK), 1)
        upto = (col <= row).astype(BF16)
        carry = jnp.zeros((1, LANES), F32)
        for n in range(SEQ // CUM_BLK):
            rows = slice(n * CUM_BLK, (n + 1) * CUM_BLK)
            logf = -_softplus(-(f_ref[rows, :] + b_ref[...]))
            hi = logf.astype(BF16)
            lo = (logf - hi.astype(F32)).astype(BF16)
            cum = _dot(upto, hi) + _dot(upto, lo) + carry
            o_ref[rows, :] = cum
            carry = cum[CUM_BLK - 1:CUM_BLK]

    return pl.pallas_call(
        body, name=name, grid=(T // SEQ,),
        in_specs=[pl.BlockSpec((SEQ, LANES), lambda b: (b, 4)), pl.BlockSpec((1, LANES), lambda b: (0, 0))],
        out_specs=pl.BlockSpec((SEQ, LANES), lambda b: (b, 0)), out_shape=jax.ShapeDtypeStruct((T, LANES), F32),
        compiler_params=_params("parallel"),
    )(aux, bf)


def fox_prep_bwd(aux, bf, dcum, *, name):
    T = aux.shape[0]

    def body(f_ref, b_ref, d_ref, df_ref, db_ref):
        row = lax.broadcasted_iota(jnp.int32, (CUM_BLK, CUM_BLK), 0)
        col = lax.broadcasted_iota(jnp.int32, (CUM_BLK, CUM_BLK), 1)
        onward = (col >= row).astype(BF16)
        carry = jnp.zeros((1, LANES), F32)
        tot = jnp.zeros((1, LANES), F32)
        for n in range(SEQ // CUM_BLK - 1, -1, -1):
            rows = slice(n * CUM_BLK, (n + 1) * CUM_BLK)
            d = d_ref[rows, :]
            hi = d.astype(BF16)
            lo = (d - hi.astype(F32)).astype(BF16)
            dlogf = _dot(onward, hi) + _dot(onward, lo) + carry
            carry = dlogf[0:1]
            df = dlogf * jax.nn.sigmoid(-(f_ref[rows, :] + b_ref[...]))
            df_ref[rows, :] = df
            tot = tot + jnp.sum(df, axis=0, keepdims=True)

        @pl.when(pl.program_id(0) == 0)
        def _():
            db_ref[...] = tot

        @pl.when(pl.program_id(0) > 0)
        def _():
            db_ref[...] += tot

    blk = pl.BlockSpec((SEQ, LANES), lambda b: (b, 0))
    vec = pl.BlockSpec((1, LANES), lambda b: (0, 0))
    sd = jax.ShapeDtypeStruct
    return pl.pallas_call(
        body, name=name, grid=(T // SEQ,),
        in_specs=[pl.BlockSpec((SEQ, LANES), lambda b: (b, 4)), vec, blk],
        out_specs=[blk, vec], out_shape=[sd((T, LANES), F32), sd((1, LANES), F32)],
        compiler_params=_params("arbitrary"),
    )(aux, bf, dcum)


CHUNK = 256
WIDE = N_HEADS * CHUNK
NCH = SEQ // CHUNK


def _seg(h):
    return slice(h * CHUNK, (h + 1) * CHUNK)


def _chunk_rows(c):
    return pl.ds(pl.multiple_of(c * CHUNK, CHUNK), CHUNK)


def _wide_consts():
    r = lax.broadcasted_iota(jnp.int32, (WIDE, GROUP_W), 0)
    f = lax.broadcasted_iota(jnp.int32, (WIDE, GROUP_W), 1)
    bd = (r // CHUNK) == (f // HEAD_DIM)
    row = lax.broadcasted_iota(jnp.int32, (BLOCK, WIDE), 0)
    key = lax.broadcasted_iota(jnp.int32, (BLOCK, WIDE), 1) % CHUNK
    return bd, row, key


def _block_diag(x, bd):
    return jnp.where(bd, jnp.concatenate([x] * N_HEADS, axis=0), jnp.zeros((), x.dtype))


def _fold_heads(w, bd):
    w = jnp.where(bd, w, 0.0)
    return (w[0:CHUNK] + w[CHUNK:2 * CHUNK]) + (w[2 * CHUNK:3 * CHUNK] + w[3 * CHUNK:])


def _widen(cols):
    return jnp.concatenate([jnp.broadcast_to(c, (BLOCK, CHUNK)) for c in cols], axis=1)


def _head_rowsums(w):
    return [jnp.sum(w[:, _seg(h)], axis=1, keepdims=True) for h in HEADS]


def _tri_wide(x, tri):
    hi = x.astype(BF16)
    lo = (x - hi.astype(F32)).astype(BF16)
    y = _dot(jnp.concatenate([hi[:, _seg(h)] for h in HEADS] + [lo[:, _seg(h)] for h in HEADS], axis=0), tri)
    return jnp.concatenate([y[h * BLOCK:(h + 1) * BLOCK] + y[(N_HEADS + h) * BLOCK:(N_HEADS + h + 1) * BLOCK] for h in HEADS], axis=1)


def _feature_widen(cols):
    return jnp.concatenate([jnp.broadcast_to(c, (BLOCK, HEAD_DIM)) for c in cols], axis=1)


def _loop_by_two(n, index, body, carry):
    odd = n % 2
    carry = lax.fori_loop(0, odd, lambda _, cr: body(index(0), cr), carry)
    return lax.fori_loop(0, n // 2, lambda t, cr: body(index(odd + 2 * t + 1), body(index(odd + 2 * t), cr)), carry)


def _sbw_scores(q, kbd, later):
    z = _dot(q, kbd, _NT)
    lk = -_softplus(z)
    return z + lk, lk, _tri_wide(lk, later)


def _sbw_tile(q, kbd, mask, later, csum):
    z = _dot(q, kbd, _NT)
    lk = -_softplus(z)
    if mask is not None:
        lk = jnp.where(mask, lk, 0.0)
    e = z + lk
    att = jnp.exp(e + _tri_wide(lk, later) + csum)
    if mask is not None:
        att = jnp.where(mask, att, 0.0)
    return att, e, lk


def sbw_fwd(qkv, *, name):
    T = qkv.shape[0]

    def body(q_ref, k_ref, v_ref, o_ref):
        bd, row, key = _wide_consts()
        r2 = lax.broadcasted_iota(jnp.int32, (CHUNK, CHUNK), 0)
        c2 = lax.broadcasted_iota(jnp.int32, (CHUNK, CHUNK), 1)
        later = (r2 > c2).astype(BF16)

        def qblock(i, _):
            q = q_ref[_blk(i), :] * Q_SCALE
            cd = i // 2
            strict = key < row + BLOCK * (i % 2)

            def tile(c, mask, carry):
                acc, csum = carry
                att, _, lk = _sbw_tile(q, _block_diag(k_ref[_chunk_rows(c), :], bd), mask, later, csum)
                acc = acc + _dot(att.astype(BF16), _block_diag(v_ref[_chunk_rows(c), :], bd))
                return acc, csum + _widen(_head_rowsums(lk))

            def two_tiles(c1, carry):
                acc, csum = carry
                e1, lk1, t1 = _sbw_scores(q, _block_diag(k_ref[_chunk_rows(c1), :], bd), later)
                e2, lk2, t2 = _sbw_scores(q, _block_diag(k_ref[_chunk_rows(c1 - 1), :], bd), later)
                att1 = jnp.exp(e1 + t1 + csum)
                csum = csum + _widen(_head_rowsums(lk1))
                att2 = jnp.exp(e2 + t2 + csum)
                csum = csum + _widen(_head_rowsums(lk2))
                acc = acc + _dot(att1.astype(BF16), _block_diag(v_ref[_chunk_rows(c1), :], bd))
                acc = acc + _dot(att2.astype(BF16), _block_diag(v_ref[_chunk_rows(c1 - 1), :], bd))
                return acc, csum

            carry = tile(cd, strict, (jnp.zeros((BLOCK, GROUP_W), F32), jnp.zeros((BLOCK, WIDE), F32)))
            odd = cd % 2
            carry = lax.fori_loop(0, odd, lambda n, cr: tile(cd - 1, None, cr), carry)
            acc, _ = lax.fori_loop(0, cd // 2, lambda n, cr: two_tiles(cd - 1 - odd - 2 * n, cr), carry)
            o_ref[_blk(i), :] = acc.astype(BF16)
            return 0

        lax.fori_loop(0, NB, qblock, 0)

    return pl.pallas_call(
        body, name=name, grid=(T // SEQ,), in_specs=_qkv_specs(0),
        out_specs=pl.BlockSpec((SEQ, GROUP_W), lambda b: (b, 0)), out_shape=jax.ShapeDtypeStruct((T, GROUP_W), BF16),
        compiler_params=_params("parallel"),
    )(qkv, qkv, qkv)


def sbw_bwd(qkv, dmixed, *, name):
    T = qkv.shape[0]

    def body(q_ref, k_ref, v_ref, do_ref, dq_ref, dk_ref, dv_ref, att_s, sg_s):
        bd, row, key = _wide_consts()
        r2 = lax.broadcasted_iota(jnp.int32, (CHUNK, CHUNK), 0)
        c2 = lax.broadcasted_iota(jnp.int32, (CHUNK, CHUNK), 1)
        later = (r2 > c2).astype(BF16)
        earlier = (r2 < c2).astype(BF16)
        dk_ref[...] = jnp.zeros_like(dk_ref)
        dv_ref[...] = jnp.zeros_like(dv_ref)

        def qblock(i, _):
            q = q_ref[_blk(i), :] * Q_SCALE
            do = do_ref[_blk(i), :].astype(BF16)
            cd = i // 2
            strict = key < row + BLOCK * (i % 2)

            def recompute(c, mask, csum):
                att, e, lk = _sbw_tile(q, _block_diag(k_ref[_chunk_rows(c), :], bd), mask, later, csum)
                sg = jnp.exp(e)
                att_s[c] = att
                sg_s[c] = sg if mask is None else jnp.where(mask, sg, 0.0)
                return csum + _widen(_head_rowsums(lk))

            def recompute_two(c1, csum):
                e1, lk1, t1 = _sbw_scores(q, _block_diag(k_ref[_chunk_rows(c1), :], bd), later)
                e2, lk2, t2 = _sbw_scores(q, _block_diag(k_ref[_chunk_rows(c1 - 1), :], bd), later)
                sg_s[c1] = jnp.exp(e1)
                sg_s[c1 - 1] = jnp.exp(e2)
                att_s[c1] = jnp.exp(e1 + t1 + csum)
                csum = csum + _widen(_head_rowsums(lk1))
                att_s[c1 - 1] = jnp.exp(e2 + t2 + csum)
                return csum + _widen(_head_rowsums(lk2))

            csum = recompute(cd, strict, jnp.zeros((BLOCK, WIDE), F32))
            odd = cd % 2
            csum = lax.fori_loop(0, odd, lambda n, cs: recompute(cd - 1, None, cs), csum)
            lax.fori_loop(0, cd // 2, lambda n, cs: recompute_two(cd - 1 - odd - 2 * n, cs), csum)

            def tile(c, carry):
                dq, pre = carry
                kbd = _block_diag(k_ref[_chunk_rows(c), :], bd)
                vbd = _block_diag(v_ref[_chunk_rows(c), :], bd)
                att = att_s[c]
                ds = _dot(do, vbd, _NT) * att
                dlk = ds + _tri_wide(ds, earlier) + pre
                dz = (ds - dlk * sg_s[c]).astype(BF16)
                dk_ref[_chunk_rows(c), :] += _fold_heads(_dot(dz, q, _TN), bd)
                dv_ref[_chunk_rows(c), :] += _fold_heads(_dot(att.astype(BF16), do, _TN), bd)
                return dq + _dot(dz, kbd), pre + _widen(_head_rowsums(ds))

            def two_tiles(c1, carry):
                dq, pre = carry
                c2 = c1 + 1
                kbd1, kbd2 = _block_diag(k_ref[_chunk_rows(c1), :], bd), _block_diag(k_ref[_chunk_rows(c2), :], bd)
                att1, att2 = att_s[c1], att_s[c2]
                ds1 = _dot(do, _block_diag(v_ref[_chunk_rows(c1), :], bd), _NT) * att1
                ds2 = _dot(do, _block_diag(v_ref[_chunk_rows(c2), :], bd), _NT) * att2
                tri1, tri2 = _tri_wide(ds1, earlier), _tri_wide(ds2, earlier)
                dv_ref[_chunk_rows(c1), :] += _fold_heads(_dot(att1.astype(BF16), do, _TN), bd)
                dv_ref[_chunk_rows(c2), :] += _fold_heads(_dot(att2.astype(BF16), do, _TN), bd)
                dz1 = (ds1 - (ds1 + tri1 + pre) * sg_s[c1]).astype(BF16)
                pre = pre + _widen(_head_rowsums(ds1))
                dz2 = (ds2 - (ds2 + tri2 + pre) * sg_s[c2]).astype(BF16)
                pre = pre + _widen(_head_rowsums(ds2))
                dk_ref[_chunk_rows(c1), :] += _fold_heads(_dot(dz1, q, _TN), bd)
                dk_ref[_chunk_rows(c2), :] += _fold_heads(_dot(dz2, q, _TN), bd)
                return dq + _dot(dz1, kbd1) + _dot(dz2, kbd2), pre

            n_tiles = cd + 1
            odd = n_tiles % 2
            carry = (jnp.zeros((BLOCK, GROUP_W), F32), jnp.zeros((BLOCK, WIDE), F32))
            carry = lax.fori_loop(0, odd, lambda n, cr: tile(0, cr), carry)
            dq, _ = lax.fori_loop(0, n_tiles // 2, lambda n, cr: two_tiles(odd + 2 * n, cr), carry)
            dq_ref[_blk(i), :] = dq * Q_SCALE
            return 0

        lax.fori_loop(0, NB, qblock, 0)

    out = pl.BlockSpec((SEQ, GROUP_W), lambda b: (b, 0))
    sd = jax.ShapeDtypeStruct((T, GROUP_W), F32)
    return pl.pallas_call(
        body, name=name, grid=(T // SEQ,), in_specs=_qkv_specs(0) + [out],
        out_specs=[out] * 3, out_shape=[sd] * 3,
        scratch_shapes=[pltpu.VMEM((NCH, BLOCK, WIDE), F32), pltpu.VMEM((NCH, BLOCK, WIDE), F32)],
        compiler_params=_params("parallel"),
    )(qkv, qkv, qkv, dmixed)


def _foxw_logits(q, kbd, cq, cr_ref, c, mask):
    ck = jnp.concatenate([cr_ref[h:h + 1, _chunk_rows(c)] for h in HEADS], axis=1)
    z = _dot(q, kbd, _NT) + cq - ck
    return z if mask is None else jnp.where(mask, z, NEG)


def foxw_fwd(qkv, cumc, cumr, *, name):
    T = qkv.shape[0]

    def body(q_ref, k_ref, v_ref, cc_ref, cr_ref, o_ref, o32_ref, lse_ref, z_s):
        bd, row, key = _wide_consts()
        lse_ref[...] = jnp.zeros_like(lse_ref)

        def qblock(i, _):
            q = q_ref[_blk(i), :] * Q_SCALE
            cq = _widen([cc_ref[_blk(i), h:h + 1] for h in HEADS])
            cd = i // 2
            causal = key <= row + BLOCK * (i % 2)

            def logits(c, mask, ms):
                z = _foxw_logits(q, _block_diag(k_ref[_chunk_rows(c), :], bd), cq, cr_ref, c, mask)
                z_s[c] = z
                return tuple(jnp.maximum(ms[h], jnp.max(z[:, _seg(h)], axis=1, keepdims=True)) for h in HEADS)

            ms = logits(cd, causal, (jnp.full((BLOCK, 1), NEG, F32),) * N_HEADS)
            ms = _loop_by_two(cd, lambda n: n, lambda c, m: logits(c, None, m), ms)
            m_wide = _widen(ms)

            def values(c, carry):
                acc, l = carry
                p = jnp.exp(z_s[c] - m_wide)
                return acc + _dot(p.astype(BF16), _block_diag(v_ref[_chunk_rows(c), :], bd)), l + _widen(_head_rowsums(p))

            acc, l = _loop_by_two(cd + 1, lambda n: n, values, (jnp.zeros((BLOCK, GROUP_W), F32), jnp.zeros((BLOCK, WIDE), F32)))
            ls = [l[:, h * CHUNK:h * CHUNK + 1] for h in HEADS]
            o = acc / _feature_widen(ls)
            o_ref[_blk(i), :] = o.astype(BF16)
            o32_ref[_blk(i), :] = o
            for h in HEADS:
                lse_ref[_blk(i), h:h + 1] = ms[h] + jnp.log(ls[h])
            return 0

        lax.fori_loop(0, NB, qblock, 0)

    out = pl.BlockSpec((SEQ, GROUP_W), lambda b: (b, 0))
    colb = pl.BlockSpec((SEQ, LANES), lambda b: (b, 0))
    sd = jax.ShapeDtypeStruct
    return pl.pallas_call(
        body, name=name, grid=(T // SEQ,),
        in_specs=_qkv_specs(3) + [colb, pl.BlockSpec((8, SEQ), lambda b: (b, 0))],
        out_specs=[out, out, colb], out_shape=[sd((T, GROUP_W), BF16), sd((T, GROUP_W), F32), sd((T, LANES), F32)],
        scratch_shapes=[pltpu.VMEM((NCH, BLOCK, WIDE), F32)],
        compiler_params=_params("parallel"),
    )(qkv, qkv, qkv, cumc, cumr)


def foxw_bwd(qkv, cumc, cumr, lse, o32, dmixed, *, name):
    T = qkv.shape[0]

    def body(q_ref, k_ref, v_ref, cc_ref, cr_ref, lse_ref, o_ref, do_ref, dq_ref, dk_ref, dv_ref, dcc_ref, dcr_ref):
        bd, row, key = _wide_consts()
        dk_ref[...] = jnp.zeros_like(dk_ref)
        dv_ref[...] = jnp.zeros_like(dv_ref)
        dcc_ref[...] = jnp.zeros_like(dcc_ref)
        dcr_ref[...] = jnp.zeros_like(dcr_ref)

        def qblock(i, _):
            q = q_ref[_blk(i), :] * Q_SCALE
            do = do_ref[_blk(i), :].astype(BF16)
            prod = do.astype(F32) * o_ref[_blk(i), :]
            delta = _widen([jnp.sum(prod[:, _hs(h)], axis=1, keepdims=True) for h in HEADS])
            cq = _widen([cc_ref[_blk(i), h:h + 1] for h in HEADS])
            lse_w = _widen([lse_ref[_blk(i), h:h + 1] for h in HEADS])
            cd = i // 2
            causal = key <= row + BLOCK * (i % 2)

            def tile(c, mask, carry):
                dq, dcq = carry
                kbd = _block_diag(k_ref[_chunk_rows(c), :], bd)
                vbd = _block_diag(v_ref[_chunk_rows(c), :], bd)
                p = jnp.exp(_foxw_logits(q, kbd, cq, cr_ref, c, mask) - lse_w)
                ds = p * (_dot(do, vbd, _NT) - delta)
                dsb = ds.astype(BF16)
                dk_ref[_chunk_rows(c), :] += _fold_heads(_dot(dsb, q, _TN), bd)
                dv_ref[_chunk_rows(c), :] += _fold_heads(_dot(p.astype(BF16), do, _TN), bd)
                for h in HEADS:
                    dcr_ref[h:h + 1, _chunk_rows(c)] -= jnp.sum(ds[:, _seg(h)], axis=0, keepdims=True)
                return dq + _dot(dsb, kbd), dcq + _widen(_head_rowsums(ds))

            def two_tiles(c1, carry):
                dq, dcq = carry
                cs = (c1, c1 + 1)
                kbds = [_block_diag(k_ref[_chunk_rows(c), :], bd) for c in cs]
                vbds = [_block_diag(v_ref[_chunk_rows(c), :], bd) for c in cs]
                ps = [jnp.exp(_foxw_logits(q, kbds[j], cq, cr_ref, cs[j], None) - lse_w) for j in range(2)]
                dss = [ps[j] * (_dot(do, vbds[j], _NT) - delta) for j in range(2)]
                dsbs = [d.astype(BF16) for d in dss]
                for j, c in enumerate(cs):
                    dk_ref[_chunk_rows(c), :] += _fold_heads(_dot(dsbs[j], q, _TN), bd)
                    dv_ref[_chunk_rows(c), :] += _fold_heads(_dot(ps[j].astype(BF16), do, _TN), bd)
                    for h in HEADS:
                        dcr_ref[h:h + 1, _chunk_rows(c)] -= jnp.sum(dss[j][:, _seg(h)], axis=0, keepdims=True)
                dq = dq + _dot(dsbs[0], kbds[0]) + _dot(dsbs[1], kbds[1])
                return dq, dcq + _widen(_head_rowsums(dss[0])) + _widen(_head_rowsums(dss[1]))

            carry = tile(cd, causal, (jnp.zeros((BLOCK, GROUP_W), F32), jnp.zeros((BLOCK, WIDE), F32)))
            odd = cd % 2
            carry = lax.fori_loop(0, odd, lambda n, cr: tile(0, None, cr), carry)
            dq, dcq = lax.fori_loop(0, cd // 2, lambda n, cr: two_tiles(odd + 2 * n, cr), carry)
            dq_ref[_blk(i), :] = dq * Q_SCALE
            for h in HEADS:
                dcc_ref[_blk(i), h:h + 1] = dcq[:, h * CHUNK:h * CHUNK + 1]
            return 0

        lax.fori_loop(0, NB, qblock, 0)

    out = pl.BlockSpec((SEQ, GROUP_W), lambda b: (b, 0))
    colb = pl.BlockSpec((SEQ, LANES), lambda b: (b, 0))
    rowb = pl.BlockSpec((8, SEQ), lambda b: (b, 0))
    sd = jax.ShapeDtypeStruct
    big = sd((T, GROUP_W), F32)
    return pl.pallas_call(
        body, name=name, grid=(T // SEQ,),
        in_specs=_qkv_specs(3) + [colb, rowb, colb, out, pl.BlockSpec((SEQ, GROUP_W), lambda b: (b, 1))],
        out_specs=[out, out, out, colb, rowb],
        out_shape=[big, big, big, sd((T, LANES), F32), sd((T // SEQ * 8, SEQ), F32)],
        compiler_params=_params("parallel"),
    )(qkv, qkv, qkv, cumc, cumr, lse, o32, dmixed)


BAND = 2 * BLOCK


def _t5_bucket_np(dist):
    n = np.maximum(dist, 0)
    max_exact = NUM_BUCKETS // 2
    nf = np.maximum(n, 1).astype(np.float32)
    large = max_exact + (np.log(nf / np.float32(max_exact)) / np.float32(math.log(MAX_DISTANCE / max_exact))
                         * np.float32(NUM_BUCKETS - max_exact)).astype(np.int32)
    large = np.minimum(large, NUM_BUCKETS - 1)
    return np.where(n < max_exact, n, large).astype(np.int32)


def _band_buckets():
    qi = np.arange(BLOCK)[:, None]
    ki = np.arange(BAND)[None, :]
    delta = np.clip(qi - ki + BLOCK, 0, BLOCK)
    return np.stack([_t5_bucket_np(delta * d) for d in DILATIONS])


def relbias_expand(rel, *, name):
    buckets = jnp.asarray(_band_buckets())
    n_pat = len(DILATIONS)

    def body(rel_ref, bk_ref, o_ref):
        for p in range(n_pat):
            bk = bk_ref[p]
            for h in range(N_HEADS):
                acc = jnp.zeros((BLOCK, BAND), F32)
                for b in range(NUM_BUCKETS):
                    acc = jnp.where(bk == b, rel_ref[b, h], acc)
                o_ref[p * N_HEADS + h] = acc

    return pl.pallas_call(
        body, name=name,
        in_specs=[pl.BlockSpec(memory_space=pltpu.SMEM), pl.BlockSpec(memory_space=pltpu.VMEM)],
        out_specs=pl.BlockSpec(memory_space=pltpu.VMEM),
        out_shape=jax.ShapeDtypeStruct((n_pat * N_HEADS, BLOCK, BAND), F32),
        compiler_params=_params(),
    )(rel, buckets)


def relbias_reduce(ds_all, *, name):
    buckets = jnp.asarray(_band_buckets())
    n_pat = len(DILATIONS)

    def body(ds_ref, bk_ref, o_ref):
        for b in range(NUM_BUCKETS):
            for h in range(N_HEADS):
                tot = jnp.float32(0.0)
                for p in range(n_pat):
                    tot = tot + jnp.sum(jnp.where(bk_ref[p] == b, ds_ref[p * N_HEADS + h], 0.0))
                o_ref[b, h] = tot

    return pl.pallas_call(
        body, name=name,
        in_specs=[pl.BlockSpec(memory_space=pltpu.VMEM), pl.BlockSpec(memory_space=pltpu.VMEM)],
        out_specs=pl.BlockSpec(memory_space=pltpu.SMEM),
        out_shape=jax.ShapeDtypeStruct((NUM_BUCKETS, N_HEADS), F32),
        compiler_params=_params(),
    )(ds_all, buckets)


def _band_valid_wide(first, row, key):
    inside = jnp.logical_and(key >= row, key <= row + BLOCK)
    return jnp.logical_and(inside, jnp.logical_or(jnp.logical_not(first), key >= BLOCK))


QKV_BLOCKS = 9


def _band_in_specs(d, pattern, has_prev):
    rows = BLOCK * d
    cur = lambda c: pl.BlockSpec((rows, GROUP_W), lambda tb, r: (tb, c))
    prev = lambda c: pl.BlockSpec((rows, GROUP_W), lambda tb, r: (jnp.maximum(tb - 1, 0), c))
    bias = pl.BlockSpec((N_HEADS, BLOCK, BAND), lambda tb, r: (pattern, 0, 0))
    return [cur(6), cur(7), cur(8)] + ([prev(7), prev(8)] if has_prev else []) + [bias]


def _classes_per_step(d):
    return min(d, 4)


def _step_classes(d):
    n = _classes_per_step(d)
    return [pl.program_id(1) * n + j for j in range(n)]


def _class_rows(d, cls):
    return pl.ds(cls, BLOCK, stride=d) if d > 1 else pl.ds(0, BLOCK)


def _halves_scratch(rows, n):
    return [pltpu.VMEM((2, rows, LANES), F32)] * n


def _stage(refs, scratch):
    @pl.when(pl.program_id(1) == 0)
    def _():
        for src, dst in zip(refs, scratch):
            dst[0] = src[:, :LANES].astype(F32)
            dst[1] = src[:, LANES:].astype(F32)


def _take_class(s, d, cls):
    rows = _class_rows(d, cls)
    return jnp.concatenate([s.at[0][rows, :], s.at[1][rows, :]], axis=1)


def _put_class(s, d, cls, x):
    rows = _class_rows(d, cls)
    s.at[0][rows, :] = x[:, :LANES]
    s.at[1][rows, :] = x[:, LANES:]


def _flush(scratch, refs, d):
    @pl.when(pl.program_id(1) == d // _classes_per_step(d) - 1)
    def _():
        for s, o in zip(scratch, refs):
            o[...] = jnp.concatenate([s[0], s[1]], axis=1)


def _band_operands(scratch, d, cls, has_prev):
    take = lambda s: _take_class(s, d, cls).astype(BF16)
    q = (_take_class(scratch[0], d, cls) * Q_SCALE).astype(BF16)
    if has_prev:
        k = jnp.concatenate([take(scratch[3]), take(scratch[1])], axis=0)
        v = jnp.concatenate([take(scratch[4]), take(scratch[2])], axis=0)
    else:
        k = jnp.concatenate([jnp.zeros((BLOCK, GROUP_W), BF16), take(scratch[1])], axis=0)
        v = jnp.concatenate([jnp.zeros((BLOCK, GROUP_W), BF16), take(scratch[2])], axis=0)
    return q, k, v


def _lane_columns(cols):
    lane = lax.broadcasted_iota(jnp.int32, (BLOCK, LANES), 1)
    out = jnp.zeros((BLOCK, LANES), F32)
    for h, c in enumerate(cols):
        out = jnp.where(lane == h, c, out)
    return out


def band_fwd(qkv, bias, pattern, *, name):
    T = qkv.shape[0]
    d = DILATIONS[pattern]
    rows_per_block = BLOCK * d
    seq_blocks = SEQ // rows_per_block
    has_prev = seq_blocks > 1
    n_in = 5 if has_prev else 3

    def body(*refs):
        ins, b_ref, o_ref, lse_ref = refs[:n_in], refs[n_in], refs[n_in + 1], refs[n_in + 2]
        staged, o_s = refs[n_in + 3:2 * n_in + 3], refs[2 * n_in + 3]
        bd, row, key = _wide_consts()
        valid = _band_valid_wide(pl.program_id(0) % seq_blocks == 0, row, key)
        _stage(ins, staged)
        bias_w = jnp.concatenate([b_ref[h] for h in HEADS], axis=1)
        for cls in _step_classes(d):
            q, k, v = _band_operands(staged, d, cls, has_prev)
            kbd, vbd = _block_diag(k, bd), _block_diag(v, bd)
            sc = jnp.where(valid, _dot(q, kbd, _NT) + bias_w, NEG)
            ms = [jnp.max(sc[:, _seg(h)], axis=1, keepdims=True) for h in HEADS]
            p = jnp.exp(sc - _widen(ms))
            ls = _head_rowsums(p)
            _put_class(o_s, d, cls, _dot(p.astype(BF16), vbd) / _feature_widen(ls))
            lse_ref[_class_rows(d, cls), :] = _lane_columns([ms[h] + jnp.log(ls[h]) for h in HEADS])
        _flush([o_s], [o_ref], d)

    sd = jax.ShapeDtypeStruct
    return pl.pallas_call(
        body, name=name, grid=(T // rows_per_block, d // _classes_per_step(d)), in_specs=_band_in_specs(d, pattern, has_prev),
        out_specs=[pl.BlockSpec((rows_per_block, GROUP_W), lambda tb, r: (tb, 0)),
                   pl.BlockSpec((rows_per_block, LANES), lambda tb, r: (tb, 0))],
        out_shape=[sd((T, GROUP_W), F32), sd((T, LANES), F32)],
        scratch_shapes=_halves_scratch(rows_per_block, n_in + 1),
        compiler_params=_params("parallel", "arbitrary"),
    )(*([qkv] * n_in), bias)


def band_bwd(qkv, bias, lse, do, dlse, pattern, *, name):
    T = qkv.shape[0]
    d = DILATIONS[pattern]
    rows_per_block = BLOCK * d
    seq_blocks = SEQ // rows_per_block
    has_prev = seq_blocks > 1
    n_in = 5 if has_prev else 3
    n_out = 5 if has_prev else 3

    def body(*refs):
        ins, b_ref, lse_ref, do_ref, dlse_ref = refs[:n_in], refs[n_in], refs[n_in + 1], refs[n_in + 2], refs[n_in + 3]
        outs = refs[n_in + 4:n_in + 4 + n_out]
        ds_ref = refs[n_in + 4 + n_out]
        scratch = refs[n_in + 5 + n_out:]
        staged, do_s, out_s = scratch[:n_in], scratch[n_in], scratch[n_in + 1:]
        first_step = jnp.logical_and(pl.program_id(0) == 0, pl.program_id(1) == 0)
        bd, row, key = _wide_consts()
        valid = _band_valid_wide(pl.program_id(0) % seq_blocks == 0, row, key)
        _stage(list(ins) + [do_ref], list(staged) + [do_s])
        bias_w = jnp.concatenate([b_ref[h] for h in HEADS], axis=1)
        ds = None
        for cls in _step_classes(d):
            q, k, v = _band_operands(staged, d, cls, has_prev)
            kbd, vbd = _block_diag(k, bd), _block_diag(v, bd)
            rows = _class_rows(d, cls)
            do = _take_class(do_s, d, cls).astype(BF16)
            lse_t, dlse_t = lse_ref[rows, :], dlse_ref[rows, :]
            lse_w = _widen([lse_t[:, h:h + 1] for h in HEADS])
            dlse_w = _widen([dlse_t[:, h:h + 1] for h in HEADS])
            p = jnp.where(valid, jnp.exp(_dot(q, kbd, _NT) + bias_w - lse_w), 0.0)
            dp = _dot(do, vbd, _NT)
            ds_c = p * (dp - _widen(_head_rowsums(p * dp)) + dlse_w)
            dsb, pb = ds_c.astype(BF16), p.astype(BF16)
            _put_class(out_s[0], d, cls, _dot(dsb, kbd) * Q_SCALE)
            dk = _fold_heads(_dot(dsb, q, _TN), bd)
            dv = _fold_heads(_dot(pb, do, _TN), bd)
            _put_class(out_s[1], d, cls, dk[BLOCK:])
            _put_class(out_s[2], d, cls, dv[BLOCK:])
            if has_prev:
                _put_class(out_s[3], d, cls, dk[:BLOCK])
                _put_class(out_s[4], d, cls, dv[:BLOCK])
            ds = ds_c if ds is None else ds + ds_c
        _flush(out_s, outs, d)

        @pl.when(first_step)
        def _():
            for h in HEADS:
                ds_ref[h] = ds[:, _seg(h)]

        @pl.when(jnp.logical_not(first_step))
        def _():
            for h in HEADS:
                ds_ref[h] += ds[:, _seg(h)]

    big = pl.BlockSpec((rows_per_block, GROUP_W), lambda tb, r: (tb, 0))
    colb = pl.BlockSpec((rows_per_block, LANES), lambda tb, r: (tb, 0))
    sd = jax.ShapeDtypeStruct
    return pl.pallas_call(
        body, name=name, grid=(T // rows_per_block, d // _classes_per_step(d)),
        in_specs=_band_in_specs(d, pattern, has_prev) + [colb, big, colb],
        out_specs=[big] * n_out + [pl.BlockSpec((N_HEADS, BLOCK, BAND), lambda tb, r: (0, 0, 0))],
        out_shape=[sd((T, GROUP_W), F32)] * n_out + [sd((N_HEADS, BLOCK, BAND), F32)],
        scratch_shapes=_halves_scratch(rows_per_block, n_in + 1 + n_out),
        compiler_params=_params("arbitrary", "arbitrary"),
    )(*([qkv] * n_in), bias, lse, do, dlse)


def _pattern_weights(lse_refs, h):
    ls = [r[:, h:h + 1] for r in lse_refs]
    mx = functools.reduce(jnp.maximum, ls)
    es = [jnp.exp(l - mx) for l in ls]
    tot = functools.reduce(lambda a, b: a + b, es)
    return [e / tot for e in es]


def dil_combine_fwd(outs, *, name):
    T = outs[0][0].shape[0]
    n = len(outs)
    tm = 512

    def body(*refs):
        o_refs, l_refs, out_ref = refs[:n], refs[n:2 * n], refs[2 * n]
        for h in range(N_HEADS):
            w = _pattern_weights(l_refs, h)
            acc = w[0] * o_refs[0][:, _hs(h)]
            for p in range(1, n):
                acc = acc + w[p] * o_refs[p][:, _hs(h)]
            out_ref[:, _hs(h)] = acc.astype(BF16)

    big = pl.BlockSpec((tm, GROUP_W), lambda i: (i, 0))
    colb = pl.BlockSpec((tm, LANES), lambda i: (i, 0))
    return pl.pallas_call(
        body, name=name, grid=(T // tm,), in_specs=[big] * n + [colb] * n,
        out_specs=big, out_shape=jax.ShapeDtypeStruct((T, GROUP_W), BF16),
        compiler_params=_params("parallel"),
    )(*[o for o, _ in outs], *[l for _, l in outs])


def dil_combine_bwd(outs, dmixed, *, name):
    T = outs[0][0].shape[0]
    n = len(outs)
    tm = 512

    def body(*refs):
        o_refs, l_refs, do_ref = refs[:n], refs[n:2 * n], refs[2 * n]
        do_refs, dl_refs = refs[2 * n + 1:3 * n + 1], refs[3 * n + 1:]
        for r in dl_refs:
            r[...] = jnp.zeros_like(r)
        for h in range(N_HEADS):
            w = _pattern_weights(l_refs, h)
            do = do_ref[:, _hs(h)]
            dw = [jnp.sum(do * o_refs[p][:, _hs(h)], axis=1, keepdims=True) for p in range(n)]
            mean = functools.reduce(lambda a, b: a + b, [w[p] * dw[p] for p in range(n)])
            for p in range(n):
                do_refs[p][:, _hs(h)] = w[p] * do
                dl_refs[p][:, h:h + 1] = w[p] * (dw[p] - mean)

    big = pl.BlockSpec((tm, GROUP_W), lambda i: (i, 0))
    colb = pl.BlockSpec((tm, LANES), lambda i: (i, 0))
    sd = jax.ShapeDtypeStruct
    res = pl.pallas_call(
        body, name=name, grid=(T // tm,),
        in_specs=[big] * n + [colb] * n + [pl.BlockSpec((tm, GROUP_W), lambda i: (i, 2))],
        out_specs=[big] * n + [colb] * n, out_shape=[sd((T, GROUP_W), F32)] * n + [sd((T, LANES), F32)] * n,
        compiler_params=_params("parallel"),
    )(*[o for o, _ in outs], *[l for _, l in outs], dmixed)
    return list(zip(res[:n], res[n:]))


def dilated_fwd(qkv, bias, tag):
    return [band_fwd(qkv, bias, p, name=f"{tag}_band_fwd{p}") for p in range(len(DILATIONS))]


def dilated_bwd(qkv, bias, outs, dmixed, tag):
    grads = dil_combine_bwd(outs, dmixed, name=f"{tag}_combine_bwd")
    parts, ds_all = [], []
    for p, d in enumerate(DILATIONS):
        (_, lse), (do, dlse) = outs[p], grads[p]
        res = band_bwd(qkv, bias, lse, do, dlse, p, name=f"{tag}_band_bwd{p}")
        parts.append((list(res[:-1]), d))
        ds_all.append(res[-1])
    return parts, jnp.concatenate(ds_all, axis=0)


def assemble_dqkv(d_sb, d_fox, d_dil, *, name):
    T = d_sb[0].shape[0]
    nb = T // BLOCK
    flat = list(d_sb) + list(d_fox)
    specs = [pl.BlockSpec((BLOCK, GROUP_W), lambda i: (i, 0))] * 6
    layout = []
    for arrs, shift in d_dil:
        layout.append((len(flat), len(arrs) > 3, shift))
        flat += arrs
        specs += [pl.BlockSpec((BLOCK, GROUP_W), lambda i: (i, 0))] * 3
        if len(arrs) > 3:
            specs += [pl.BlockSpec((BLOCK, GROUP_W), lambda i, s=shift: (jnp.minimum(i + s, nb - 1), 0))] * 2

    def body(*refs):
        o_ref = refs[-1]
        i = pl.program_id(0)
        for j in range(6):
            o_ref[:, j * GROUP_W:(j + 1) * GROUP_W] = refs[j][...].astype(BF16)
        acc = [None, None, None]
        for first, has_prev, shift in layout:
            for j in range(3):
                v = refs[first + j][...]
                if has_prev and j > 0:
                    v = v + (i + shift < nb).astype(F32) * refs[first + 2 + j][...]
                acc[j] = v if acc[j] is None else acc[j] + v
        for j in range(3):
            o_ref[:, (6 + j) * GROUP_W:(7 + j) * GROUP_W] = acc[j].astype(BF16)

    return pl.pallas_call(
        body, name=name, grid=(nb,), in_specs=specs,
        out_specs=pl.BlockSpec((BLOCK, QKV_BLOCKS * GROUP_W), lambda i: (i, 0)),
        out_shape=jax.ShapeDtypeStruct((T, QKV_BLOCKS * GROUP_W), BF16), compiler_params=_params("parallel"),
    )(*flat)


def sum_cast(arrs, dtype, *, name):
    R, C = arrs[0].shape
    tr = _largest_tile(R, 512, 16)
    n = len(arrs)

    def body(*refs):
        acc = refs[0][...].astype(F32)
        for r in refs[1:n]:
            acc = acc + r[...].astype(F32)
        refs[n][...] = acc.astype(dtype)

    blk = pl.BlockSpec((tr, C), lambda i: (i, 0))
    return pl.pallas_call(
        body, name=name, grid=(R // tr,), in_specs=[blk] * n, out_specs=blk, out_shape=jax.ShapeDtypeStruct((R, C), dtype),
        compiler_params=_params("parallel"),
    )(*arrs)


GRAD_WIRE = BF16


def _block_diag_halves(w):
    z = jnp.zeros((HEAD_DIM, HEAD_DIM), w.dtype)
    half = lambda a, b: jnp.concatenate([jnp.concatenate([a, z], axis=1), jnp.concatenate([z, b], axis=1)], axis=0)
    return jnp.stack([half(w[0], w[1]), half(w[2], w[3])]).astype(BF16)


def _diag_blocks(d):
    h = HEAD_DIM
    return jnp.stack([d[0, :h, :h], d[0, h:, h:], d[1, :h, :h], d[1, h:, h:]])


def layer_fwd(x, mem2d, W, P, bias, tag):
    s = {}
    s["x"] = x
    h1 = rmsnorm_fwd(x, P["norm_mix_g"], name=f"{tag}_norm_mix")
    qkv = matmul(h1, W["qkv"], out_dtype=BF16, name=f"{tag}_qkv")
    aux = matmul(h1, W["aux"], name=f"{tag}_aux")
    o_sb = sbw_fwd(qkv, name=f"{tag}_sb_fwd")
    cumc = fox_prep(aux, P["bf"], name=f"{tag}_fox_prep")
    cumr = col_to_row(cumc)
    o_fox, o_fox32, lse_fox = foxw_fwd(qkv, cumc, cumr, name=f"{tag}_fox_fwd")
    dil = dilated_fwd(qkv, bias, tag)
    o_dil = dil_combine_fwd(dil, name=f"{tag}_dil_combine")
    o_lru, h_lru = lru_fwd(aux, P["lru_conv_w"], P["lru_conv_b"], P["wa"], P["lru_b_a"], P["wx"], P["lru_b_x"],
                           P["lru_lambda"], name=f"{tag}_lru_fwd")
    mixed = jnp.concatenate([o_sb, o_fox, o_dil, o_lru], axis=1)
    if "rest" in W:
        W.update(W.pop("rest")(mixed))
    x1 = matmul(mixed, W["out"], residual=x, name=f"{tag}_out")
    hq = rmsnorm_fwd(x1, P["norm_cross_g"], name=f"{tag}_norm_cross")
    qc = matmul(hq, W["cq"], out_dtype=BF16, name=f"{tag}_cq")
    memn = rmsnorm_fwd(mem2d, P["norm_mem_g"], name=f"{tag}_norm_mem")
    kv = matmul(memn, W["ckv"], out_dtype=BF16, name=f"{tag}_ckv")
    oc = cross_fwd(qc, kv, name=f"{tag}_cross_fwd")
    x2 = matmul(oc, W["coT"], trans_b=True, residual=x1, name=f"{tag}_co")
    h2 = rmsnorm_fwd(x2, P["norm_ffn_g"], name=f"{tag}_norm_ffn")
    hu = matmul(h2, W["up_u"], trans_b=True, name=f"{tag}_up_u")
    hg = matmul(h2, W["up_g"], trans_b=True, name=f"{tag}_up_g")
    act = glu_fwd(hu, hg, P["wu"], P["wg"], P["bu"], P["bg"], name=f"{tag}_glu_fwd")
    x3 = matmul(act, W["down"], residual=x2, name=f"{tag}_down")
    s.update(h1=h1, qkv=qkv, aux=aux, cumc=cumc, cumr=cumr, lse_fox=lse_fox, o_fox32=o_fox32, dil=dil, h_lru=h_lru, mixed=mixed,
             x1=x1, hq=hq, qc=qc, memn=memn, kv=kv, oc=oc, x2=x2, h2=h2, hu=hu, hg=hg, act=act)
    return x3, s


def layer_bwd(dx3, mem2d, W, P, bias, s, tag, hooks=None):
    mm = functools.partial(matmul, out_dtype=GRAD_WIRE, trans_a=True)
    gW, gP = {}, {}
    hooks = hooks or {}
    dact = matmul(dx3, W["down"], trans_b=True, name=f"{tag}_d_act")
    gW["down"] = mm(s["act"], dx3, name=f"{tag}_g_down")
    dhu, dhg, dwu, dwg, dbu, dbg = glu_bwd(s["hu"], s["hg"], dact, P["wu"], P["wg"], P["bu"], P["bg"], name=f"{tag}_glu_bwd")
    gP["ffn_conv_w"] = jnp.concatenate([dwu, dwg], axis=1)
    gP["ffn_conv_b"] = jnp.concatenate([dbu, dbg], axis=1)
    dh2 = matmul(dhu, W["up_u"], name=f"{tag}_d_h2u")
    dh2 = matmul(dhg, W["up_g"], residual=dh2, name=f"{tag}_d_h2g")
    gW["up_u"] = mm(dhu, s["h2"], name=f"{tag}_g_up_u")
    gW["up_g"] = mm(dhg, s["h2"], name=f"{tag}_g_up_g")
    dx2, gP["norm_ffn_g"] = rmsnorm_bwd(s["x2"], P["norm_ffn_g"], dh2, dx3, name=f"{tag}_norm_ffn_bwd")
    if "ffn" in hooks:
        hooks["ffn"](gW, W, s)
    doc = matmul(dx2, W["coT"], name=f"{tag}_d_oc")
    gW["coT"] = mm(dx2, s["oc"], name=f"{tag}_g_co")
    dqc, dkv = cross_bwd(s["qc"], s["kv"], doc, name=f"{tag}_cross_bwd")
    dhq = matmul(dqc, W["cq"], trans_b=True, name=f"{tag}_d_hq")
    gW["cq"] = mm(s["hq"], dqc, name=f"{tag}_g_cq")
    dmemn = matmul(dkv, W["ckv"], trans_b=True, name=f"{tag}_d_memn")
    gW["ckv"] = mm(s["memn"], dkv, name=f"{tag}_g_ckv")
    _, gP["norm_mem_g"] = rmsnorm_bwd(mem2d, P["norm_mem_g"], dmemn, None, name=f"{tag}_norm_mem_bwd")
    dx1, gP["norm_cross_g"] = rmsnorm_bwd(s["x1"], P["norm_cross_g"], dhq, dx2, name=f"{tag}_norm_cross_bwd")
    dmixed = matmul(dx1, W["out"], trans_b=True, name=f"{tag}_d_mixed")
    gW["out"] = mm(s["mixed"], dx1, name=f"{tag}_g_out")
    if "mid" in hooks:
        hooks["mid"](gW, W, s)
    qkv, aux = s["qkv"], s["aux"]
    d_sb = sbw_bwd(qkv, dmixed, name=f"{tag}_sb_bwd")
    dfq, dfk, dfv, dcc, dcr = foxw_bwd(qkv, s["cumc"], s["cumr"], s["lse_fox"], s["o_fox32"], dmixed, name=f"{tag}_fox_bwd")
    dcum = sum_cast([dcc, row_to_col(dcr)], F32, name=f"{tag}_dcum")
    df, dbf = fox_prep_bwd(aux, P["bf"], dcum, name=f"{tag}_fox_prep_bwd")
    gP["b_forget"] = dbf[0, :N_HEADS]
    d_dil, ds_band = dilated_bwd(qkv, bias, s["dil"], dmixed, tag)
    dlx, dlg, dcw, dcb, dwa, dba, dwx, dbx, dlam = lru_bwd(
        aux, s["h_lru"], dmixed, P["lru_conv_w"], P["lru_conv_b"], P["wa"], P["lru_b_a"], P["wx"], P["lru_b_x"],
        P["lru_lambda"], name=f"{tag}_lru_bwd")
    gP.update(lru_conv_w=dcw, lru_conv_b=dcb, lru_w_a=_diag_blocks(dwa), lru_b_a=dba, lru_w_x=_diag_blocks(dwx),
              lru_b_x=dbx, lru_lambda=dlam)
    dqkv = assemble_dqkv(d_sb, [dfq, dfk, dfv], d_dil, name=f"{tag}_dqkv")
    daux = jnp.concatenate([dlx, dlg, df], axis=1)
    dh1 = matmul(dqkv, W["qkv"], trans_b=True, name=f"{tag}_d_h1a")
    dh1 = matmul(daux, W["aux"], trans_b=True, residual=dh1, name=f"{tag}_d_h1b")
    gW["qkv"] = mm(s["h1"], dqkv, name=f"{tag}_g_qkv")
    gW["aux"] = mm(s["h1"], daux, name=f"{tag}_g_aux")
    dx, gP["norm_mix_g"] = rmsnorm_bwd(s["x"], P["norm_mix_g"], dh1, dx1, name=f"{tag}_norm_mix_bwd")
    return dx, gW, gP, ds_band


def local_step(x, mem, target, weights_of, Ps, rel_bias, final_norm_g, grads_done=None, bwd_hooks=None):
    B = x.shape[0]
    x2d = x.reshape(B * SEQ, D_MODEL)
    mem2d = mem.reshape(B * N_MEM, D_MODEL)
    bias = relbias_expand(rel_bias, name="relbias_expand")
    saved, Ws = [], []
    h = x2d
    for l in range(DEPTH):
        Ws.append(weights_of(l, h))
        h, s = layer_fwd(h, mem2d, Ws[l], Ps[l], bias, f"l{l}")
        saved.append(s)
    loss, dh, d_final = loss_head(h, final_norm_g, target.reshape(B * SEQ, D_MODEL), name="loss_head")
    gWs, gPs, ds_bands = [None] * DEPTH, [None] * DEPTH, []
    for l in range(DEPTH - 1, -1, -1):
        hooks = None if bwd_hooks is None else bwd_hooks(l)
        dh, gWs[l], gPs[l], ds = layer_bwd(dh, mem2d, Ws[l], Ps[l], bias, saved[l], f"l{l}", hooks)
        if grads_done is not None:
            grads_done(l, gWs[l])
        ds_bands.append(ds)
    d_rel = relbias_reduce(sum_cast([d.reshape(-1, BAND) for d in ds_bands], F32, name="ds_band_sum").reshape(-1, BLOCK, BAND),
                           name="relbias_reduce")
    return loss, dh.reshape(B, SEQ, D_MODEL), gWs, gPs, d_rel, d_final


def small_params(p, l):
    row = lambda name: p[name][l].reshape(1, -1)
    ffn_w, ffn_b = p["ffn_conv_w"][l], row("ffn_conv_b")
    return dict(
        norm_mix_g=row("norm_mix_g"), norm_cross_g=row("norm_cross_g"), norm_mem_g=row("norm_mem_g"), norm_ffn_g=row("norm_ffn_g"),
        bf=jnp.pad(row("b_forget"), ((0, 0), (0, LANES - N_HEADS))),
        lru_conv_w=p["lru_conv_w"][l], lru_conv_b=row("lru_conv_b"), wa=_block_diag_halves(p["lru_w_a"][l]), lru_b_a=row("lru_b_a"),
        wx=_block_diag_halves(p["lru_w_x"][l]), lru_b_x=row("lru_b_x"), lru_lambda=row("lru_lambda"),
        wu=ffn_w[:, :D_FF], wg=ffn_w[:, D_FF:], bu=ffn_b[:, :D_FF], bg=ffn_b[:, D_FF:])


def canonical_weights(w_in, w_out, w_cq, w_ck, w_cv, w_co, w_up, w_down):
    sb_fox, fox_f, rest = w_in[:, :6 * GROUP_W], w_in[:, 6 * GROUP_W:6 * GROUP_W + N_HEADS], w_in[:, 6 * GROUP_W + N_HEADS:]
    dil, lru = rest[:, :3 * GROUP_W], rest[:, 3 * GROUP_W:]
    pad = jnp.zeros((w_in.shape[0], AUX_W - 2 * GROUP_W - N_HEADS), w_in.dtype)
    return dict(qkv=jnp.concatenate([sb_fox, dil], axis=1), aux=jnp.concatenate([lru, fox_f, pad], axis=1), out=w_out,
                cq=w_cq, ckv=jnp.concatenate([w_ck, w_cv], axis=1), coT=w_co.T, upT=w_up.T, down=w_down)


def native_grads(g):
    qkv, aux = g["qkv"], g["aux"]
    a, b = 6 * GROUP_W, 6 * GROUP_W + N_HEADS
    w_in = jnp.zeros((qkv.shape[0], b + 5 * GROUP_W), qkv.dtype)
    w_in = w_in.at[:, :a].set(qkv[:, :a]).at[:, a:b].set(aux[:, 2 * GROUP_W:2 * GROUP_W + N_HEADS])
    w_in = w_in.at[:, b:b + 3 * GROUP_W].set(qkv[:, a:]).at[:, b + 3 * GROUP_W:].set(aux[:, :2 * GROUP_W])
    return (w_in, g["out"], g["cq"], g["ckv"][:, :GROUP_W], g["ckv"][:, GROUP_W:], g["coT"].T) + native_ffn_grads(g)


def native_ffn_grads(g):
    return (g["upT"].T, g["down"])


ANY = pl.BlockSpec(memory_space=pl.ANY)
VMEM_SPEC = pl.BlockSpec(memory_space=pltpu.VMEM)


def _place():
    x, y, c = lax.axis_index("x"), lax.axis_index("y"), lax.axis_index("c")
    other_chips = [(1 - x, y), (x, 1 - y), (1 - x, 1 - y)]
    return x, y, c, other_chips


def _gather_body(x_ref, out_ref, send_sems, recv_sems, local_sem):
    x, y, c, chips = _place()
    me, sibling = (x, y, c), (x, y, 1 - c)

    def slot(px, py, pc):
        return out_ref.at[4 * px + 2 * py + pc]

    def copy(k, block, to, src=None):
        return pltpu.make_async_remote_copy(
            src_ref=slot(*block) if src is None else src, dst_ref=slot(*block),
            send_sem=send_sems.at[k], recv_sem=recv_sems.at[k], device_id=to, device_id_type=MESH)

    if local_sem is not None:
        mine = pltpu.make_async_copy(x_ref, slot(*me), local_sem)
        mine.start()
    first = [copy(0, me, sibling, src=x_ref)]
    first += [copy(1 + j, me, (*chip, c), src=x_ref) for j, chip in enumerate(chips)]
    for cp in first:
        cp.start()
    passed = [copy(4 + j, (*chip, c), sibling) for j, chip in enumerate(chips)]
    for j, chip in enumerate(chips):
        copy(1 + j, (*chip, c), me).wait_recv()
        passed[j].start()
    copy(0, sibling, me).wait_recv()
    for j, chip in enumerate(chips):
        copy(4 + j, (*chip, 1 - c), me).wait_recv()
    for cp in first + passed:
        cp.wait_send()
    if local_sem is not None:
        mine.wait()


_GATHER_SEMS = [pltpu.SemaphoreType.DMA((7,)), pltpu.SemaphoreType.DMA((7,)), pltpu.SemaphoreType.DMA]


def allgather_hbm(shard, me, *, name):
    def body(x_ref, out_ref, done_ref, send_sems, recv_sems):
        _gather_body(x_ref, out_ref, send_sems, recv_sems, None)
        done_ref[...] = jnp.zeros_like(done_ref)

    others, done = pl.pallas_call(
        body, name=name, in_specs=[ANY], out_specs=[ANY, VMEM_SPEC],
        out_shape=[jax.ShapeDtypeStruct((N_DEV,) + shard.shape, shard.dtype), jax.ShapeDtypeStruct((8, LANES), F32)],
        scratch_shapes=_GATHER_SEMS[:2],
    )(shard)
    return lax.dynamic_update_slice(others, shard[None], (me, 0, 0)), done


def allgather_small(x, *, name, reduce=False):
    def body(x_ref, out_ref, second_ref, *sems):
        _gather_body(x_ref, out_ref, *sems)
        if reduce:
            acc = out_ref[0]
            for d in range(1, N_DEV):
                acc = acc + out_ref[d]
            second_ref[...] = acc
        else:
            second_ref[...] = jnp.zeros_like(second_ref)

    sd = jax.ShapeDtypeStruct
    return pl.pallas_call(
        body, name=name, in_specs=[VMEM_SPEC], out_specs=[VMEM_SPEC, VMEM_SPEC],
        out_shape=[sd((N_DEV,) + x.shape, x.dtype), sd(x.shape if reduce else (8, LANES), x.dtype)],
        scratch_shapes=_GATHER_SEMS, compiler_params=pltpu.CompilerParams(vmem_limit_bytes=VMEM_LIMIT_V7X),
    )(x)


N_CHIPS = 4


def pair_exchange(g, *, name):
    _, R, C = g.shape

    def body(g_ref, recv_ref, send_sems, recv_sems):
        x, y, c, _ = _place()
        sibling = (x, y, 1 - c)
        remote = [pltpu.make_async_remote_copy(
            src_ref=g_ref.at[2 * q + (1 - c)], dst_ref=recv_ref.at[q], send_sem=send_sems.at[q], recv_sem=recv_sems.at[q],
            device_id=sibling, device_id_type=MESH) for q in range(N_CHIPS)]
        for cp in remote:
            cp.start()
        for cp in remote:
            cp.wait_recv()
        for cp in remote:
            cp.wait_send()

    return pl.pallas_call(
        body, name=name, in_specs=[ANY], out_specs=ANY, out_shape=jax.ShapeDtypeStruct((N_CHIPS, R, C), g.dtype),
        scratch_shapes=[pltpu.SemaphoreType.DMA((N_CHIPS,))] * 2,
    )(g)


def chip_exchange(s, *, name):
    _, R, C = s.shape

    def body(s_ref, o0, o1, o2, send_sems, recv_sems):
        x, y, c, chips = _place()
        outs = (o0, o1, o2)
        copies = [pltpu.make_async_remote_copy(
            src_ref=s_ref.at[2 * cx + cy], dst_ref=outs[j], send_sem=send_sems.at[j], recv_sem=recv_sems.at[j],
            device_id=(cx, cy, c), device_id_type=MESH) for j, (cx, cy) in enumerate(chips)]
        for cp in copies:
            cp.start()
        for cp in copies:
            cp.wait_recv()
        for cp in copies:
            cp.wait_send()

    sd = jax.ShapeDtypeStruct((R, C), s.dtype)
    return pl.pallas_call(
        body, name=name, in_specs=[ANY], out_specs=[ANY] * 3, out_shape=[sd] * 3,
        scratch_shapes=[pltpu.SemaphoreType.DMA((3,)), pltpu.SemaphoreType.DMA((3,))],
    )(s)


HBM_SPEC = pl.BlockSpec(memory_space=pltpu.HBM)
SEM_SPEC = pl.BlockSpec(memory_space=pltpu.SEMAPHORE)
N_PEERS = N_DEV - 1


def _peers():
    x, y, c = lax.axis_index("x"), lax.axis_index("y"), lax.axis_index("c")
    flip = lambda v, bit: 1 - v if bit else v
    out = []
    for k in range(1, N_DEV):
        px, py, pc = flip(x, (k >> 2) & 1), flip(y, (k >> 1) & 1), flip(c, k & 1)
        out.append(((px, py, pc), 4 * px + 2 * py + pc))
    return out, 4 * x + 2 * y + c


def _peer_copies(src_ref, land_ref, send_sems, recv_sems, scatter, landing):
    peers, me = _peers()
    return [pltpu.make_async_remote_copy(
        src_ref=src_ref.at[idx] if scatter else src_ref, dst_ref=land_ref.at[me if landing == "mine" else idx],
        send_sem=send_sems.at[k], recv_sem=recv_sems.at[k], device_id=peer, device_id_type=MESH)
        for k, (peer, idx) in enumerate(peers)]


def exchange_start(src, scatter, *, name):
    shape = (N_DEV,) + src.shape[-2:]

    def body(src_ref, land_ref, send_sems, recv_sems, src_thru, land_thru, token):
        for cp in _peer_copies(src_ref, land_ref, send_sems, recv_sems, scatter, "mine"):
            cp.start()
        token[...] = jnp.zeros_like(token)

    sems = pltpu.SemaphoreType.DMA((N_PEERS,))
    return pl.pallas_call(
        body, name=name,
        out_shape=(sems, sems, pltpu.HBM(src.shape, src.dtype), pltpu.HBM(shape, src.dtype), jax.ShapeDtypeStruct((8, LANES), F32)),
        in_specs=(HBM_SPEC, HBM_SPEC), out_specs=(SEM_SPEC, SEM_SPEC, HBM_SPEC, HBM_SPEC, VMEM_SPEC),
        input_output_aliases={0: 2, 1: 3},
        compiler_params=pltpu.CompilerParams(has_side_effects=pltpu.SideEffectType.DATAFLOW_SIDE_EFFECTING),
    )(pltpu.with_memory_space_constraint(src, pltpu.HBM), pltpu.with_memory_space_constraint(lax.empty(shape, src.dtype), pltpu.HBM))


def exchange_wait(started, after, scatter, *, name):
    send_sems, recv_sems, src_thru, land_thru, _ = started

    def body(src_ref, land_ref, send_sems, recv_sems, after_ref, src_dead, got_ref):
        for cp in _peer_copies(src_ref, land_ref, send_sems, recv_sems, scatter, "theirs"):
            cp.wait_send()
            cp.wait_recv()

    return pl.pallas_call(
        body, name=name, out_shape=(pltpu.HBM(src_thru.shape, src_thru.dtype), pltpu.HBM(land_thru.shape, land_thru.dtype)),
        in_specs=(HBM_SPEC, HBM_SPEC, SEM_SPEC, SEM_SPEC, ANY), out_specs=(HBM_SPEC, HBM_SPEC), input_output_aliases={0: 0, 1: 1},
        compiler_params=pltpu.CompilerParams(has_side_effects=pltpu.SideEffectType.DATAFLOW_SIDE_EFFECTING),
    )(src_thru, land_thru, send_sems, recv_sems, after)[1]


def sum_blocks(blocks, *, name):
    n, R, C = blocks.shape
    tr = _largest_tile(R, 512, 16)

    def body(b_ref, o_ref):
        acc = b_ref[0].astype(F32)
        for d in range(1, n):
            acc = acc + b_ref[d].astype(F32)
        o_ref[...] = acc

    return pl.pallas_call(
        body, name=name, grid=(R // tr,),
        in_specs=[pl.BlockSpec((n, tr, C), lambda i: (0, i, 0))], out_specs=pl.BlockSpec((tr, C), lambda i: (i, 0)),
        out_shape=jax.ShapeDtypeStruct((R, C), F32), compiler_params=_params("parallel"),
    )(blocks)


WEIGHTS = ("norm_mix_g", "w_in", "b_forget", "lru_conv_w", "lru_conv_b", "lru_w_a", "lru_b_a", "lru_w_x", "lru_b_x", "lru_lambda",
           "w_out", "norm_cross_g", "norm_mem_g", "w_cq", "w_ck", "w_cv", "w_co", "norm_ffn_g", "w_up", "ffn_conv_w", "ffn_conv_b",
           "w_down", "rel_bias", "final_norm_g")
LARGE = ("w_in", "w_out", "w_cq", "w_ck", "w_cv", "w_co", "w_up", "w_down")
COLUMN_SPLIT_SMALL = ("lru_conv_w", "ffn_conv_w")
PACK = (("qkv", 128, 2304), ("aux", 128, 640), ("out", 128, 1024), ("cq", 128, 256), ("ckv", 128, 512), ("coT", 128, 256),
        ("upT", 704, 1024), ("down", 352, 1024))
PACK_W = 1024


def _pack_rows(parts):
    return jnp.concatenate([p.reshape(-1, PACK_W) for p in parts], axis=0)


def _pad_rows(flat, mult=8 * LANES):
    n = flat.shape[0]
    return jnp.pad(flat, (0, (-n) % mult)).reshape(-1, LANES)


def kernel(x, mem, norm_mix_g, w_in, b_forget, lru_conv_w, lru_conv_b, lru_w_a, lru_b_a, lru_w_x, lru_b_x, lru_lambda, w_out, norm_cross_g, norm_mem_g, w_cq, w_ck, w_cv, w_co, norm_ffn_g, w_up, ffn_conv_w, ffn_conv_b, w_down, rel_bias, final_norm_g, loss_target, m_norm_mix_g, m_w_in, m_b_forget, m_lru_conv_w, m_lru_conv_b, m_lru_w_a, m_lru_b_a, m_lru_w_x, m_lru_b_x, m_lru_lambda, m_w_out, m_norm_cross_g, m_norm_mem_g, m_w_cq, m_w_ck, m_w_cv, m_w_co, m_norm_ffn_g, m_w_up, m_ffn_conv_w, m_ffn_conv_b, m_w_down, m_rel_bias, m_final_norm_g, v_norm_mix_g, v_w_in, v_b_forget, v_lru_conv_w, v_lru_conv_b, v_lru_w_a, v_lru_b_a, v_lru_w_x, v_lru_b_x, v_lru_lambda, v_w_out, v_norm_cross_g, v_norm_mem_g, v_w_cq, v_w_ck, v_w_cv, v_w_co, v_norm_ffn_g, v_w_up, v_ffn_conv_w, v_ffn_conv_b, v_w_down, v_rel_bias, v_final_norm_g):
    w = dict(norm_mix_g=norm_mix_g, w_in=w_in, b_forget=b_forget, lru_conv_w=lru_conv_w, lru_conv_b=lru_conv_b, lru_w_a=lru_w_a,
             lru_b_a=lru_b_a, lru_w_x=lru_w_x, lru_b_x=lru_b_x, lru_lambda=lru_lambda, w_out=w_out, norm_cross_g=norm_cross_g,
             norm_mem_g=norm_mem_g, w_cq=w_cq, w_ck=w_ck, w_cv=w_cv, w_co=w_co, norm_ffn_g=norm_ffn_g, w_up=w_up,
             ffn_conv_w=ffn_conv_w, ffn_conv_b=ffn_conv_b, w_down=w_down, rel_bias=rel_bias, final_norm_g=final_norm_g)
    m = dict(norm_mix_g=m_norm_mix_g, w_in=m_w_in, b_forget=m_b_forget, lru_conv_w=m_lru_conv_w, lru_conv_b=m_lru_conv_b,
             lru_w_a=m_lru_w_a, lru_b_a=m_lru_b_a, lru_w_x=m_lru_w_x, lru_b_x=m_lru_b_x, lru_lambda=m_lru_lambda, w_out=m_w_out,
             norm_cross_g=m_norm_cross_g, norm_mem_g=m_norm_mem_g, w_cq=m_w_cq, w_ck=m_w_ck, w_cv=m_w_cv, w_co=m_w_co,
             norm_ffn_g=m_norm_ffn_g, w_up=m_w_up, ffn_conv_w=m_ffn_conv_w, ffn_conv_b=m_ffn_conv_b, w_down=m_w_down,
             rel_bias=m_rel_bias, final_norm_g=m_final_norm_g)
    v = dict(norm_mix_g=v_norm_mix_g, w_in=v_w_in, b_forget=v_b_forget, lru_conv_w=v_lru_conv_w, lru_conv_b=v_lru_conv_b,
             lru_w_a=v_lru_w_a, lru_b_a=v_lru_b_a, lru_w_x=v_lru_w_x, lru_b_x=v_lru_b_x, lru_lambda=v_lru_lambda, w_out=v_w_out,
             norm_cross_g=v_norm_cross_g, norm_mem_g=v_norm_mem_g, w_cq=v_w_cq, w_ck=v_w_ck, w_cv=v_w_cv, w_co=v_w_co,
             norm_ffn_g=v_norm_ffn_g, w_up=v_w_up, ffn_conv_w=v_ffn_conv_w, ffn_conv_b=v_ffn_conv_b, w_down=v_w_down,
             rel_bias=v_rel_bias, final_norm_g=v_final_norm_g)
    me = 4 * lax.axis_index("x") + 2 * lax.axis_index("y") + lax.axis_index("c")

    conv_shard = jnp.concatenate([w[n].reshape(-1) for n in COLUMN_SPLIT_SMALL])
    conv_all, conv_gathered = allgather_small(_pad_rows(conv_shard), name="gather_conv")
    conv_all = conv_all.reshape(N_DEV, -1)
    full = dict(w)
    off = 0
    for n in COLUMN_SPLIT_SMALL:
        d, k, c = w[n].shape
        blocks = conv_all[:, off:off + d * k * c].reshape(N_DEV, d, k, c)
        full[n] = blocks.transpose(1, 2, 0, 3).reshape(d, k, N_DEV * c)
        off += d * k * c

    IN, MID, FFN = PACK[:2], PACK[2:6], PACK[6:]
    REST = MID + FFN

    def packed_shard(l, group):
        canon = canonical_weights(*[w[n][l] for n in LARGE])
        return _pack_rows([canon[k].astype(BF16) for k, _, _ in group])

    def unpack_weights(packed, group):
        W, row = {}, 0
        half = N_DEV // 2
        for k, r, c in group:
            n_rows = r * c // PACK_W
            rows = packed[:, row:row + n_rows]
            if k == "upT":
                W["up_u"], W["up_g"] = rows[:half].reshape(half * r, c), rows[half:].reshape(half * r, c)
            else:
                W[k] = rows.reshape(N_DEV * r, c)
            row += n_rows
        return W

    def packed_grads(gW, group):
        g = dict(gW)
        if "up_u" in g:
            g["upT"] = jnp.concatenate([g.pop("up_u"), g.pop("up_g")], axis=0)
        return jnp.concatenate([g[k].reshape(N_DEV, r * c // PACK_W, PACK_W) for k, r, c in group], axis=1)

    def unpack_grads(shard_sum, group):
        g, row = {}, 0
        for k, r, c in group:
            n_rows = r * c // PACK_W
            g[k] = shard_sum[row:row + n_rows].reshape(r, c)
            row += n_rows
        return g

    def own_block_in(landed, block):
        return lax.dynamic_update_slice(landed, block[None], (me, 0, 0))

    def gathered_weights(copies, shard, after, group, name):
        return unpack_weights(own_block_in(exchange_wait(copies, after, False, name=name), shard), group)

    def scattered_sum(src, copies, after, tag):
        landed = exchange_wait(copies, after, True, name=f"{tag}_wait")
        mine = lax.dynamic_index_in_dim(src, me, axis=0, keepdims=False)
        return sum_blocks(own_block_in(landed, mine), name=f"{tag}_sum")

    last = DEPTH - 1
    in0, gathered = allgather_hbm(packed_shard(0, IN) + conv_gathered[0, 0].astype(BF16), me, name="gather_weights")
    rest0_shard = packed_shard(0, REST) + gathered[0, 0].astype(BF16)
    gather_rest0 = exchange_start(rest0_shard, False, name="gather_rest0_start")
    last_shard = packed_shard(last, PACK) + gather_rest0[4][0, 0].astype(BF16)
    gather_last = exchange_start(last_shard, False, name="gather_last_start")
    started = gather_last[4][0, 0]
    layer_weights = {}

    def weights_of(l, h):
        if l == 0:
            W = unpack_weights(in0, IN)
            W["rest"] = lambda after: gathered_weights(gather_rest0, rest0_shard, after, REST, "gather_rest0_wait")
        else:
            assert l == last
            W = gathered_weights(gather_last, last_shard, h, PACK, "gather_last_wait")
        layer_weights[l] = W
        return W

    in_flight = {}

    def scatter(key, g_all, name):
        in_flight[key] = (g_all, exchange_start(g_all, True, name=name))
        return in_flight[key][1][4][0, 0].astype(BF16)

    def grads_done(l, gW):
        if l == last:
            W0 = layer_weights[0]
            W0["down"] = W0["down"] + scatter("last", packed_grads(gW, PACK), "grads_last_start")

    def ffn0_grads_done(gW, W, s):
        W["coT"] = W["coT"] + scatter("ffn0", packed_grads({k: gW[k] for k in ("up_u", "up_g", "down")}, FFN), "grads_ffn0_start")

    def mid0_grads_done(gW, W, s):
        s["cumc"] = s["cumc"] + scatter("mid0", packed_grads({k: gW[k] for k, _, _ in MID}, MID), "grads_mid0_start").astype(F32)

    Ps = [small_params(full, l) for l in range(DEPTH)]
    Ps[0]["norm_mix_g"] = Ps[0]["norm_mix_g"] + started
    loss, grad_x, gWs, gPs, d_rel, d_final = local_step(
        x, mem, loss_target, weights_of, Ps, rel_bias, final_norm_g.reshape(1, -1), grads_done,
        lambda l: {"ffn": ffn0_grads_done, "mid": mid0_grads_done} if l == 0 else None)

    shard_grads = {last: unpack_grads(scattered_sum(*in_flight["last"], grad_x, "grads_last"), PACK)}
    shard_grads[0] = unpack_grads(scattered_sum(*in_flight["ffn0"], grad_x, "grads_ffn0"), FFN)
    shard_grads[0].update(unpack_grads(scattered_sum(*in_flight["mid0"], grad_x, "grads_mid0"), MID))

    g_all = packed_grads({k: gWs[0][k] for k, _, _ in IN}, IN)
    rows = g_all.shape[1]
    got = pair_exchange(g_all, name="grads_pair_exchange")
    own = lax.dynamic_index_in_dim(g_all.reshape(N_CHIPS, 2, rows, PACK_W), lax.axis_index("c"), axis=1, keepdims=False)
    pair = sum_cast([own.reshape(-1, PACK_W), got.reshape(-1, PACK_W)], GRAD_WIRE, name="grads_pair_sum").reshape(N_CHIPS, rows, PACK_W)
    from_x, from_y, from_xy = chip_exchange(pair, name="grads_chip_exchange")
    mine = lax.dynamic_index_in_dim(pair, 2 * lax.axis_index("x") + lax.axis_index("y"), axis=0, keepdims=False)
    shard_grads[0].update(unpack_grads(sum_cast([mine, from_x, from_y, from_xy], F32, name="grads_chip_sum"), IN))

    grads = {}
    per_layer = [native_grads(shard_grads[l]) for l in range(DEPTH)]
    for i, n in enumerate(LARGE):
        grads[n] = jnp.stack([per_layer[l][i] for l in range(DEPTH)])

    small_names = [n for n in WEIGHTS if n not in LARGE and n not in ("rel_bias", "final_norm_g")]
    pieces = [gPs[l][n].reshape(-1) for n in small_names for l in range(DEPTH)] + [d_rel.reshape(-1), d_final.reshape(-1), loss[0, :1]]
    sizes = [p.shape[0] for p in pieces]
    _, total = allgather_small(_pad_rows(jnp.concatenate(pieces)), name="allreduce_small", reduce=True)
    total = total.reshape(-1)
    off, it = 0, iter(sizes)
    for n in small_names:
        per = []
        for l in range(DEPTH):
            sz = next(it)
            per.append(total[off:off + sz])
            off += sz
        full_shape = (DEPTH,) + full[n].shape[1:]
        gfull = jnp.stack(per).reshape(full_shape)
        if n in COLUMN_SPLIT_SMALL:
            c = w[n].shape[-1]
            gfull = lax.dynamic_slice_in_dim(gfull, me * c, c, axis=gfull.ndim - 1)
        grads[n] = gfull
    grads["rel_bias"] = total[off:off + rel_bias.size].reshape(rel_bias.shape)
    off += rel_bias.size
    grads["final_norm_g"] = total[off:off + D_MODEL]
    off += D_MODEL
    loss_out = total[off]

    delta, new_m, new_v = {}, {}, {}
    for n in LARGE:
        shape = w[n].shape
        two_d = lambda a: a.reshape(-1, shape[-1])
        d_, m_, v_ = adamw(two_d(w[n]), two_d(grads[n]), two_d(m[n]), two_d(v[n]), name=f"adamw_{n}")
        delta[n], new_m[n], new_v[n] = d_.reshape(shape), m_.reshape(shape), v_.reshape(shape)
    small_all = [n for n in WEIGHTS if n not in LARGE]
    two_d = lambda a: a.reshape(-1, a.shape[-1])
    d_, m_, v_ = adamw_many(*[[two_d(src[n]) for n in small_all] for src in (w, grads, m, v)], name="adamw_small")
    for i, n in enumerate(small_all):
        delta[n], new_m[n], new_v[n] = (a[i].reshape(w[n].shape) for a in (d_, m_, v_))

    return (loss_out, grad_x, *[grads[n] for n in WEIGHTS], *[delta[n] for n in WEIGHTS], *[new_m[n] for n in WEIGHTS],
            *[new_v[n] for n in WEIGHTS])
```

```python
import functools
import math

import numpy as np
import jax
import jax.numpy as jnp
from jax import lax
from jax.experimental import pallas as pl
from jax.experimental.pallas import tpu as pltpu

F32 = jnp.float32
BF16 = jnp.bfloat16
MESH = pl.DeviceIdType.MESH

N_DEV = 8
D_MODEL = 1024
SEQ = 2048
DEPTH = 2
HEAD_DIM = 64
N_HEADS = 4
GROUP_W = N_HEADS * HEAD_DIM
D_FF = 2816
N_MEM = 256
NUM_BUCKETS = 32
MAX_DISTANCE = 2048
BLOCK = 128
DILATIONS = (1, 4, 16)
EPS = 1e-6
LRU_C = 8.0
Q_SCALE = HEAD_DIM ** -0.5
AUX_W = 640
LRU_HALF_W = 128
LRU_HALVES = GROUP_W // LRU_HALF_W
ADAM_LR, ADAM_B1, ADAM_B2, ADAM_EPS, ADAM_WD, ADAM_STEP = 0.001, 0.9, 0.999, 1e-08, 0.01, 10

VMEM_LIMIT_V7X = 48 * 1024 * 1024


def _params(*sem):
    return pltpu.CompilerParams(dimension_semantics=sem if sem else None, vmem_limit_bytes=VMEM_LIMIT_V7X)


def _pick(n, cands):
    for c in cands:
        if n % c == 0:
            return c
    return n


def _largest_tile(n, cap, align):
    best = None
    for t in range(align, min(n, cap) + 1, align):
        if n % t == 0:
            best = t
    return n if best is None else best


def matmul(a, b, *, name, trans_a=False, trans_b=False, out_dtype=F32, residual=None):
    (K, M) = a.shape if trans_a else a.shape[::-1]
    (N, Kb) = b.shape if trans_b else b.shape[::-1]
    assert K == Kb, (a.shape, b.shape)
    tm = _largest_tile(M, 1408 if trans_a else (1024 if K <= 1024 else 512), 128)
    tn = _largest_tile(N, 1408, 128)
    tk = _largest_tile(K, 1024 if trans_a else 2816, 128)
    nk = K // tk
    a_spec = pl.BlockSpec((tk, tm), lambda i, j, k: (k, i)) if trans_a else pl.BlockSpec((tm, tk), lambda i, j, k: (i, k))
    b_spec = pl.BlockSpec((tn, tk), lambda i, j, k: (j, k)) if trans_b else pl.BlockSpec((tk, tn), lambda i, j, k: (k, j))
    o_spec = pl.BlockSpec((tm, tn), lambda i, j, k: (i, j))
    dims = (((0 if trans_a else 1,), (1 if trans_b else 0,)), ((), ()))
    has_res = residual is not None

    def body(*refs):
        a_ref, b_ref = refs[0], refs[1]
        r_ref = refs[2] if has_res else None
        part = lax.dot_general(a_ref[...].astype(BF16), b_ref[...].astype(BF16), dims, preferred_element_type=F32)
        if nk == 1:
            if has_res:
                part = part + r_ref[...].astype(F32)
            refs[-1][...] = part.astype(out_dtype)
            return
        o_ref, acc_ref = refs[-2], refs[-1]
        k = pl.program_id(2)

        @pl.when(k == 0)
        def _():
            acc_ref[...] = part

        @pl.when(k > 0)
        def _():
            acc_ref[...] += part

        @pl.when(k == nk - 1)
        def _():
            r = acc_ref[...]
            if has_res:
                r = r + r_ref[...].astype(F32)
            o_ref[...] = r.astype(out_dtype)

    ops = (a, b) + ((residual,) if has_res else ())
    return pl.pallas_call(
        body, name=name, grid=(M // tm, N // tn, nk),
        in_specs=[a_spec, b_spec] + ([o_spec] if has_res else []),
        out_specs=o_spec, out_shape=jax.ShapeDtypeStruct((M, N), out_dtype),
        scratch_shapes=[pltpu.VMEM((tm, tn), F32)] if nk > 1 else [],
        compiler_params=_params("parallel", "parallel", "arbitrary"),
    )(*ops)


def rmsnorm_fwd(x, g, *, name):
    R, D = x.shape
    tr = _pick(R, (512, 256))

    def body(x_ref, g_ref, o_ref):
        xv = x_ref[...]
        r = lax.rsqrt(jnp.mean(xv * xv, axis=-1, keepdims=True) + EPS)
        o_ref[...] = (xv * r * g_ref[...]).astype(BF16)

    return pl.pallas_call(
        body, name=name, grid=(R // tr,),
        in_specs=[pl.BlockSpec((tr, D), lambda i: (i, 0)), pl.BlockSpec((1, D), lambda i: (0, 0))],
        out_specs=pl.BlockSpec((tr, D), lambda i: (i, 0)), out_shape=jax.ShapeDtypeStruct((R, D), BF16),
        compiler_params=_params("parallel"),
    )(x, g)


def rmsnorm_bwd(x, g, dh, dres, *, name):
    R, D = x.shape
    tr = _pick(R, (512, 256))
    has_res = dres is not None

    def body(*refs):
        x_ref, g_ref, dh_ref = refs[:3]
        dx_ref, dg_ref = refs[-2], refs[-1]
        xv = x_ref[...]
        r = lax.rsqrt(jnp.mean(xv * xv, axis=-1, keepdims=True) + EPS)
        n = xv * r
        dhv = dh_ref[...]
        dn = dhv * g_ref[...]
        dx = r * (dn - n * jnp.mean(dn * n, axis=-1, keepdims=True))
        if has_res:
            dx = dx + refs[3][...]
        dx_ref[...] = dx
        part = jnp.sum(dhv * n, axis=0, keepdims=True)

        @pl.when(pl.program_id(0) == 0)
        def _():
            dg_ref[...] = part

        @pl.when(pl.program_id(0) > 0)
        def _():
            dg_ref[...] += part

    row = pl.BlockSpec((tr, D), lambda i: (i, 0))
    vec = pl.BlockSpec((1, D), lambda i: (0, 0))
    ops = (x, g, dh) + ((dres,) if has_res else ())
    return pl.pallas_call(
        body, name=name, grid=(R // tr,),
        in_specs=[row, vec, row] + ([row] if has_res else []),
        out_specs=[row, vec],
        out_shape=[jax.ShapeDtypeStruct((R, D), F32), jax.ShapeDtypeStruct((1, D), F32)],
        compiler_params=_params("arbitrary"),
    )(*ops)


_SQRT_HALF = 0.7071067811865476
_INV_SQRT_2PI = 0.3989422804014327


def _normal_cdf_pdf(x):
    ax = jnp.abs(x) * _SQRT_HALF
    t = 1.0 / (1.0 + 0.3275911 * ax)
    poly = t * (0.254829592 + t * (-0.284496736 + t * (1.421413741 + t * (-1.453152027 + t * 1.061405429))))
    e = jnp.exp(-0.5 * x * x)
    half_tail = 0.5 * poly * e
    return jnp.where(x < 0, half_tail, 1.0 - half_tail), e


def _gelu_cdf(x):
    return _normal_cdf_pdf(x)[0]


def _gelu_and_grad(x):
    cdf, e = _normal_cdf_pdf(x)
    return x * cdf, cdf + x * _INV_SQRT_2PI * e


def _shift_down(main, halo, first, shifts):
    halo = jnp.where(first, 0.0, halo)
    ext = jnp.concatenate([halo, main], axis=0)
    return [pltpu.roll(ext, s, 0)[8:] for s in shifts]


def _conv3(main, halo, first, w, b):
    m1, m2 = _shift_down(main, halo, first, (1, 2))
    return ((b + w[0:1] * m2) + w[1:2] * m1) + w[2:3] * main, m1, m2


def glu_fwd(hu, hg, wu, wg, bu, bg, *, name):
    T, F = hu.shape
    tm, tf = 512, _largest_tile(F, 704, 128)
    hb = tm // 8
    blocks_per_example = SEQ // tm

    def body(hu_ref, hg_ref, hau_ref, hag_ref, wu_ref, wg_ref, bu_ref, bg_ref, o_ref):
        first = pl.program_id(0) % blocks_per_example == 0
        up, _, _ = _conv3(hu_ref[...], hau_ref[...], first, wu_ref[...], bu_ref[...])
        gate, _, _ = _conv3(hg_ref[...], hag_ref[...], first, wg_ref[...], bg_ref[...])
        o_ref[...] = (gate * _gelu_cdf(gate) * up).astype(BF16)

    main = pl.BlockSpec((tm, tf), lambda i, j: (i, j))
    halo = pl.BlockSpec((8, tf), lambda i, j: (jnp.maximum(i * hb - 1, 0), j))
    w3 = pl.BlockSpec((3, tf), lambda i, j: (0, j))
    b1 = pl.BlockSpec((1, tf), lambda i, j: (0, j))
    return pl.pallas_call(
        body, name=name, grid=(T // tm, F // tf),
        in_specs=[main, main, halo, halo, w3, w3, b1, b1],
        out_specs=main, out_shape=jax.ShapeDtypeStruct((T, F), BF16),
        compiler_params=_params("parallel", "parallel"),
    )(hu, hg, hu, hg, wu, wg, bu, bg)


def glu_bwd(hu, hg, dact, wu, wg, bu, bg, *, name):
    T, F = hu.shape
    tm, tf = 512, _largest_tile(F, 704, 128)
    hb = tm // 8
    blocks_per_example = SEQ // tm
    n_halo_blocks = T // 8
    n_ext = tm + 8

    def body(hu_ref, hg_ref, hau_ref, hag_ref, hnu_ref, hng_ref, da_ref, dan_ref, wu_ref, wg_ref, bu_ref, bg_ref,
             du_ref, dg_ref, dwu_ref, dwg_ref, dbu_ref, dbg_ref):
        i = pl.program_id(1)
        first = i % blocks_per_example == 0
        last = i % blocks_per_example == blocks_per_example - 1
        wu, wg = wu_ref[...], wg_ref[...]

        def conv_ext(main_ref, prev_ref, next_ref, w, b):
            ext = jnp.concatenate([jnp.where(first, 0.0, prev_ref[...]), main_ref[...], next_ref[...]], axis=0)
            x0, x1, x2 = ext[8:], pltpu.roll(ext, 1, 0)[8:], pltpu.roll(ext, 2, 0)[8:]
            return ((b + w[0:1] * x2) + w[1:2] * x1) + w[2:3] * x0, x0, x1, x2

        up, xu, u1, u2 = conv_ext(hu_ref, hau_ref, hnu_ref, wu, bu_ref[...])
        gate, xg, g1, g2 = conv_ext(hg_ref, hag_ref, hng_ref, wg, bg_ref[...])
        act, dact_dgate = _gelu_and_grad(gate)
        da = jnp.concatenate([da_ref[...], jnp.where(last, 0.0, dan_ref[...])], axis=0)
        dup = da * act
        dgate = da * up * dact_dgate

        def conv_t(d, w):
            return (w[2:3] * d[:tm] + w[1:2] * pltpu.roll(d, n_ext - 1, 0)[:tm] + w[0:1] * pltpu.roll(d, n_ext - 2, 0)[:tm]).astype(BF16)

        du_ref[...] = conv_t(dup, wu)
        dg_ref[...] = conv_t(dgate, wg)

        def sums(d, x0, x1, x2):
            s = lambda v: jnp.sum(v[:tm], axis=0, keepdims=True)
            return jnp.concatenate([s(d * x2), s(d * x1), s(d * x0)], axis=0), s(d)

        pwu, pbu = sums(dup, xu, u1, u2)
        pwg, pbg = sums(dgate, xg, g1, g2)

        @pl.when(i == 0)
        def _():
            dwu_ref[...] = pwu
            dwg_ref[...] = pwg
            dbu_ref[...] = pbu
            dbg_ref[...] = pbg

        @pl.when(i > 0)
        def _():
            dwu_ref[...] += pwu
            dwg_ref[...] += pwg
            dbu_ref[...] += pbu
            dbg_ref[...] += pbg

    main = pl.BlockSpec((tm, tf), lambda j, i: (i, j))
    before = pl.BlockSpec((8, tf), lambda j, i: (jnp.maximum(i * hb - 1, 0), j))
    after = pl.BlockSpec((8, tf), lambda j, i: (jnp.minimum((i + 1) * hb, n_halo_blocks - 1), j))
    w3 = pl.BlockSpec((3, tf), lambda j, i: (0, j))
    b1 = pl.BlockSpec((1, tf), lambda j, i: (0, j))
    sd = jax.ShapeDtypeStruct
    return pl.pallas_call(
        body, name=name, grid=(F // tf, T // tm),
        in_specs=[main, main, before, before, after, after, main, after, w3, w3, b1, b1],
        out_specs=[main, main, w3, w3, b1, b1],
        out_shape=[sd((T, F), BF16), sd((T, F), BF16), sd((3, F), F32), sd((3, F), F32), sd((1, F), F32), sd((1, F), F32)],
        compiler_params=_params("parallel", "arbitrary"),
    )(hu, hg, hu, hg, hu, hg, dact, dact, wu, wg, bu, bg)


def loss_head(x, g, target, *, name):
    T, D = x.shape
    tr = 256

    def body(x_ref, g_ref, t_ref, loss_ref, dx_ref, dg_ref):
        xv = x_ref[...]
        gv = g_ref[...]
        r = lax.rsqrt(jnp.mean(xv * xv, axis=-1, keepdims=True) + EPS)
        n = xv * r
        err = n * gv - t_ref[...]
        part_loss = jnp.zeros((1, 128), F32) + 0.5 * jnp.sum(jnp.mean(err * err, axis=-1, keepdims=True))
        dy = err * (1.0 / D)
        dn = dy * gv
        dx_ref[...] = r * (dn - n * jnp.mean(dn * n, axis=-1, keepdims=True))
        part_g = jnp.sum(dy * n, axis=0, keepdims=True)

        @pl.when(pl.program_id(0) == 0)
        def _():
            loss_ref[...] = part_loss
            dg_ref[...] = part_g

        @pl.when(pl.program_id(0) > 0)
        def _():
            loss_ref[...] += part_loss
            dg_ref[...] += part_g

    row = pl.BlockSpec((tr, D), lambda i: (i, 0))
    vec = pl.BlockSpec((1, D), lambda i: (0, 0))
    sd = jax.ShapeDtypeStruct
    return pl.pallas_call(
        body, name=name, grid=(T // tr,),
        in_specs=[row, vec, row],
        out_specs=[pl.BlockSpec((1, 128), lambda i: (0, 0)), row, vec],
        out_shape=[sd((1, 128), F32), sd((T, D), F32), sd((1, D), F32)],
        compiler_params=_params("arbitrary"),
    )(x, g, target)


def adamw(w, g, m, v, *, name):
    R, C = w.shape
    tr = _pick(R, (256, 128, 64, 32, 16, 8))

    def body(w_ref, g_ref, m_ref, v_ref, d_ref, nm_ref, nv_ref):
        gv = g_ref[...]
        mn = ADAM_B1 * m_ref[...] + (1.0 - ADAM_B1) * gv
        vn = ADAM_B2 * v_ref[...] + (1.0 - ADAM_B2) * (gv * gv)
        m_hat = mn / (1.0 - ADAM_B1 ** ADAM_STEP)
        v_hat = vn / (1.0 - ADAM_B2 ** ADAM_STEP)
        d_ref[...] = -ADAM_LR * (m_hat / (jnp.sqrt(v_hat) + ADAM_EPS) + ADAM_WD * w_ref[...])
        nm_ref[...] = mn
        nv_ref[...] = vn

    blk = pl.BlockSpec((tr, C), lambda i: (i, 0))
    sd = jax.ShapeDtypeStruct((R, C), F32)
    return pl.pallas_call(
        body, name=name, grid=(R // tr,), in_specs=[blk] * 4, out_specs=[blk] * 3, out_shape=[sd] * 3,
        compiler_params=_params("parallel"),
    )(w, g, m, v)


def adamw_many(ws, gs, ms, vs, *, name):
    n = len(ws)

    def body(*refs):
        ins, outs = refs[:4 * n], refs[4 * n:]
        for i in range(n):
            w_ref, g_ref, m_ref, v_ref = ins[i], ins[n + i], ins[2 * n + i], ins[3 * n + i]
            gv = g_ref[...]
            mn = ADAM_B1 * m_ref[...] + (1.0 - ADAM_B1) * gv
            vn = ADAM_B2 * v_ref[...] + (1.0 - ADAM_B2) * (gv * gv)
            m_hat = mn / (1.0 - ADAM_B1 ** ADAM_STEP)
            v_hat = vn / (1.0 - ADAM_B2 ** ADAM_STEP)
            outs[i][...] = -ADAM_LR * (m_hat / (jnp.sqrt(v_hat) + ADAM_EPS) + ADAM_WD * w_ref[...])
            outs[n + i][...] = mn
            outs[2 * n + i][...] = vn

    vm = pl.BlockSpec(memory_space=pltpu.VMEM)
    shapes = [jax.ShapeDtypeStruct(w.shape, F32) for w in ws]
    res = pl.pallas_call(
        body, name=name, in_specs=[vm] * (4 * n), out_specs=[vm] * (3 * n), out_shape=shapes * 3, compiler_params=_params(),
    )(*ws, *gs, *ms, *vs)
    return res[:n], res[n:2 * n], res[2 * n:]


def _softplus(x):
    return jnp.maximum(x, 0.0) + jnp.log(1.0 + jnp.exp(-jnp.abs(x)))


def _lru_gates(x, cw, cb, wa, ba, wx, bx, lam):
    S = x.shape[0]
    row = lax.broadcasted_iota(jnp.int32, (S, 1), 0)

    def back(s):
        return jnp.where(row >= s, pltpu.roll(x, s, 0), 0.0)

    xc = (((cb + cw[0:1] * back(3)) + cw[1:2] * back(2)) + cw[2:3] * back(1)) + cw[3:4] * x
    xb = xc.astype(BF16)
    r = jax.nn.sigmoid(jnp.dot(xb, wa, preferred_element_type=F32) + ba)
    ig = jax.nn.sigmoid(jnp.dot(xb, wx, preferred_element_type=F32) + bx)
    sp = _softplus(-lam)
    la = -LRU_C * r * sp
    a = jnp.exp(la)
    y = 2.0 * la
    one_minus_a2 = jnp.where(y > -0.05, -y * (1.0 + y * (0.5 + y * (1.0 / 6.0 + y * (1.0 / 24.0)))), 1.0 - jnp.exp(y))
    mm = jnp.sqrt(one_minus_a2)
    return xc, xb, r, ig, sp, a, mm


SCAN_UNROLL = 4


def _scan8(a, b, reverse):
    row = lax.broadcasted_iota(jnp.int32, (8, 1), 0)
    for k in (1, 2, 4):
        inside = row < 8 - k if reverse else row >= k
        shift = 8 - k if reverse else k
        a_n = jnp.where(inside, pltpu.roll(a, shift, 0), 1.0)
        b_n = jnp.where(inside, pltpu.roll(b, shift, 0), 0.0)
        b = a * b_n + b
        a = a * a_n
    return a, b


def lru_fwd(aux, cw, cb, wa, ba, wx, bx, lam, *, name):
    T = aux.shape[0]
    S, C = SEQ, LRU_HALF_W

    def body(x_ref, g_ref, cw_ref, cb_ref, wa_ref, ba_ref, wx_ref, bx_ref, lam_ref, o_ref, h_ref, a_s, u_s):
        xc, _, r, ig, sp, a, mm = _lru_gates(x_ref[...], cw_ref[...], cb_ref[...], wa_ref[...], ba_ref[...],
                                             wx_ref[...], bx_ref[...], lam_ref[...])
        a_s[...] = a
        u_s[...] = mm * (ig * xc)

        def group(i, h):
            for j in range(SCAN_UNROLL):
                base = pl.multiple_of((i * SCAN_UNROLL + j) * 8, 8)
                A, Bv = _scan8(a_s[pl.ds(base, 8), :], u_s[pl.ds(base, 8), :], reverse=False)
                H = A * h + Bv
                h_ref[pl.ds(base, 8), :] = H
                h = H[7:8]
            return h

        lax.fori_loop(0, S // 8 // SCAN_UNROLL, group, jnp.zeros((1, C), F32))
        gate = g_ref[...]
        o_ref[...] = (h_ref[...] * (gate * _gelu_cdf(gate))).astype(BF16)

    blk = lambda col: pl.BlockSpec((S, C), lambda c, b: (b, col + c))
    par = lambda rows: pl.BlockSpec((rows, C), lambda c, b: (0, c))
    sq = pl.BlockSpec((None, C, C), lambda c, b: (c, 0, 0))
    sd = jax.ShapeDtypeStruct
    W = LRU_HALVES * C
    return pl.pallas_call(
        body, name=name, grid=(LRU_HALVES, T // S),
        in_specs=[blk(0), blk(LRU_HALVES), par(4), par(1), sq, par(1), sq, par(1), par(1)],
        out_specs=[blk(0), blk(0)], out_shape=[sd((T, W), BF16), sd((T, W), F32)],
        scratch_shapes=[pltpu.VMEM((S, C), F32), pltpu.VMEM((S, C), F32)],
        compiler_params=_params("parallel", "parallel"),
    )(aux, aux, cw, cb, wa, ba, wx, bx, lam)


def lru_bwd(aux, h, dmixed, cw, cb, wa, ba, wx, bx, lam, *, name):
    T = aux.shape[0]
    S, C = SEQ, LRU_HALF_W

    def body(x_ref, g_ref, h_ref, do_ref, cw_ref, cb_ref, wa_ref, ba_ref, wx_ref, bx_ref, lam_ref,
             dx_ref, dgate_ref, dcw_ref, dcb_ref, dwa_ref, dba_ref, dwx_ref, dbx_ref, dlam_ref, a_s, d_s):
        x = x_ref[...]
        cw = cw_ref[...]
        lam = lam_ref[...]
        xc, xb, r, ig, sp, a, mm = _lru_gates(x, cw, cb_ref[...], wa_ref[...], ba_ref[...], wx_ref[...], bx_ref[...], lam)
        gate = g_ref[...]
        gl, dgl = _gelu_and_grad(gate)
        dout = do_ref[...]
        hv = h_ref[...]
        dgate_ref[...] = dout * hv * dgl
        a_s[...] = a
        d_s[...] = dout * gl

        last_row = lax.broadcasted_iota(jnp.int32, (8, 1), 0) == 7

        def group(i, c):
            for j in range(SCAN_UNROLL):
                base = pl.multiple_of((S // 8 - 1 - (i * SCAN_UNROLL + j)) * 8, 8)
                a8 = a_s[pl.ds(base, 8), :]
                d8 = d_s[pl.ds(base, 8), :]
                A, Bv = _scan8(a8, a8 * d8, reverse=True)
                Cv = A * c + Bv
                d_s[pl.ds(base, 8), :] = d8 + jnp.where(last_row, c, pltpu.roll(Cv, 7, 0))
                c = Cv[0:1]
            return c

        lax.fori_loop(0, S // 8 // SCAN_UNROLL, group, jnp.zeros((1, C), F32))
        row = lax.broadcasted_iota(jnp.int32, (S, 1), 0)
        dht = d_s[...]
        h_prev = jnp.where(row >= 1, pltpu.roll(hv, 1, 0), 0.0)
        da = dht * h_prev
        gx = ig * xc
        dmm = dht * gx
        dig = dht * mm * xc
        dxc = dht * mm * ig
        dla = da * a - dmm * (a * a) / mm
        dr = dla * (-LRU_C * sp)
        dsp = jnp.sum(dla * (-LRU_C * r), axis=0, keepdims=True)
        dlam = dsp * (-jax.nn.sigmoid(-lam))
        dpa = dr * r * (1.0 - r)
        dpx = dig * ig * (1.0 - ig)
        dpa_b, dpx_b = dpa.astype(BF16), dpx.astype(BF16)
        nt = (((1,), (1,)), ((), ()))
        tn = (((0,), (0,)), ((), ()))
        dxc = dxc + lax.dot_general(dpa_b, wa_ref[...], nt, preferred_element_type=F32) \
                  + lax.dot_general(dpx_b, wx_ref[...], nt, preferred_element_type=F32)
        dwa = lax.dot_general(xb, dpa_b, tn, preferred_element_type=F32)
        dwx = lax.dot_general(xb, dpx_b, tn, preferred_element_type=F32)

        def fwd(v, s):
            return jnp.where(row < S - s, pltpu.roll(v, S - s, 0), 0.0)

        def back(v, s):
            return jnp.where(row >= s, pltpu.roll(v, s, 0), 0.0)

        dx_ref[...] = cw[3:4] * dxc + cw[2:3] * fwd(dxc, 1) + cw[1:2] * fwd(dxc, 2) + cw[0:1] * fwd(dxc, 3)
        s0 = lambda v: jnp.sum(v, axis=0, keepdims=True)
        dcw = jnp.concatenate([s0(dxc * back(x, 3)), s0(dxc * back(x, 2)), s0(dxc * back(x, 1)), s0(dxc * x)], axis=0)
        parts = ((dcw_ref, dcw), (dcb_ref, s0(dxc)), (dwa_ref, dwa), (dba_ref, s0(dpa)), (dwx_ref, dwx),
                 (dbx_ref, s0(dpx)), (dlam_ref, dlam))

        @pl.when(pl.program_id(1) == 0)
        def _():
            for ref, val in parts:
                ref[...] = val

        @pl.when(pl.program_id(1) > 0)
        def _():
            for ref, val in parts:
                ref[...] += val

    blk = lambda col: pl.BlockSpec((S, C), lambda c, b: (b, col + c))
    par = lambda rows: pl.BlockSpec((rows, C), lambda c, b: (0, c))
    sq = pl.BlockSpec((None, C, C), lambda c, b: (c, 0, 0))
    sd = jax.ShapeDtypeStruct
    W = LRU_HALVES * C
    vec = sd((1, W), F32)
    return pl.pallas_call(
        body, name=name, grid=(LRU_HALVES, T // S),
        in_specs=[blk(0), blk(LRU_HALVES), blk(0), blk(3 * LRU_HALVES), par(4), par(1), sq, par(1), sq, par(1), par(1)],
        out_specs=[blk(0), blk(0), par(4), par(1), sq, par(1), sq, par(1), par(1)],
        out_shape=[sd((T, W), F32), sd((T, W), F32), sd((4, W), F32), vec, sd((LRU_HALVES, C, C), F32), vec,
                   sd((LRU_HALVES, C, C), F32), vec, vec],
        scratch_shapes=[pltpu.VMEM((S, C), F32), pltpu.VMEM((S, C), F32)],
        compiler_params=_params("parallel", "arbitrary"),
    )(aux, aux, h, dmixed, cw, cb, wa, ba, wx, bx, lam)


_NT = (((1,), (1,)), ((), ()))
_TN = (((0,), (0,)), ((), ()))


def _dot(a, b, dims=None):
    if dims is None:
        return jnp.dot(a, b, preferred_element_type=F32)
    return lax.dot_general(a, b, dims, preferred_element_type=F32)


def _hs(h):
    return slice(h * HEAD_DIM, (h + 1) * HEAD_DIM)


def cross_fwd(q, kv, *, name):
    T = q.shape[0]
    tq = 512

    def body(q_ref, kv_ref, o_ref):
        for h in range(N_HEADS):
            qh = q_ref[:, _hs(h)] * Q_SCALE
            k = kv_ref[:, _hs(h)]
            v = kv_ref[:, GROUP_W + h * HEAD_DIM:GROUP_W + (h + 1) * HEAD_DIM]
            s = _dot(qh, k, _NT)
            p = jnp.exp(s - jnp.max(s, axis=-1, keepdims=True))
            p = p / jnp.sum(p, axis=-1, keepdims=True)
            o_ref[:, _hs(h)] = _dot(p.astype(BF16), v).astype(BF16)

    per = SEQ // tq
    return pl.pallas_call(
        body, name=name, grid=(T // tq,),
        in_specs=[pl.BlockSpec((tq, GROUP_W), lambda i: (i, 0)), pl.BlockSpec((N_MEM, 2 * GROUP_W), lambda i: (i // per, 0))],
        out_specs=pl.BlockSpec((tq, GROUP_W), lambda i: (i, 0)), out_shape=jax.ShapeDtypeStruct((T, GROUP_W), BF16),
        compiler_params=_params("parallel"),
    )(q, kv)


def cross_bwd(q, kv, do, *, name):
    T = q.shape[0]
    tq = 512
    per = SEQ // tq

    def body(q_ref, kv_ref, do_ref, dq_ref, dkv_ref):
        first = pl.program_id(0) % per == 0
        for h in range(N_HEADS):
            vs = slice(GROUP_W + h * HEAD_DIM, GROUP_W + (h + 1) * HEAD_DIM)
            qh = q_ref[:, _hs(h)] * Q_SCALE
            k = kv_ref[:, _hs(h)]
            v = kv_ref[:, vs]
            doh = do_ref[:, _hs(h)].astype(BF16)
            s = _dot(qh, k, _NT)
            p = jnp.exp(s - jnp.max(s, axis=-1, keepdims=True))
            p = p / jnp.sum(p, axis=-1, keepdims=True)
            dp = _dot(doh, v, _NT)
            ds = (p * (dp - jnp.sum(p * dp, axis=-1, keepdims=True))).astype(BF16)
            dq_ref[:, _hs(h)] = (_dot(ds, k) * Q_SCALE).astype(BF16)
            dk = _dot(ds, qh, _TN)
            dv = _dot(p.astype(BF16), doh, _TN)

            @pl.when(first)
            def _():
                dkv_ref[:, _hs(h)] = dk
                dkv_ref[:, vs] = dv

            @pl.when(jnp.logical_not(first))
            def _():
                dkv_ref[:, _hs(h)] += dk
                dkv_ref[:, vs] += dv

    qb = pl.BlockSpec((tq, GROUP_W), lambda i: (i, 0))
    kvb = pl.BlockSpec((N_MEM, 2 * GROUP_W), lambda i: (i // per, 0))
    sd = jax.ShapeDtypeStruct
    return pl.pallas_call(
        body, name=name, grid=(T // tq,),
        in_specs=[qb, kvb, qb], out_specs=[qb, kvb],
        out_shape=[sd((T, GROUP_W), BF16), sd(kv.shape, F32)],
        compiler_params=_params("arbitrary"),
    )(q, kv, do)


NB = SEQ // BLOCK
NEG = -1e30
HEADS = tuple(range(N_HEADS))


def _blk(i):
    return pl.ds(pl.multiple_of(i * BLOCK, BLOCK), BLOCK)


def _qkv_specs(first_col):
    return [pl.BlockSpec((SEQ, GROUP_W), lambda b, c=first_col + j: (b, c)) for j in range(3)]


LANES = 128
CUM_BLK = 256


def col_to_row(c):
    b = c.shape[0] // SEQ
    return c.reshape(b, SEQ, LANES)[:, :, :8].transpose(0, 2, 1).reshape(b * 8, SEQ)


def row_to_col(r):
    b = r.shape[0] // 8
    c = r.reshape(b, 8, SEQ).transpose(0, 2, 1)
    return jnp.pad(c, ((0, 0), (0, 0), (0, LANES - 8))).reshape(b * SEQ, LANES)


def fox_prep(aux, bf, *, name):
    T = aux.shape[0]

    def body(f_ref, b_ref, o_ref):
        row = lax.broadcasted_iota(jnp.int32, (CUM_BLK, CUM_BLK), 0)
        col = lax.broadcasted_iota(jnp.int32, (CUM_BLK, CUM_BLK), 1)
        upto = (col <= row).astype(BF16)
        carry = jnp.zeros((1, LANES), F32)
        for n in range(SEQ // CUM_BLK):
            rows = slice(n * CUM_BLK, (n + 1) * CUM_BLK)
            logf = -_softplus(-(f_ref[rows, :] + b_ref[...]))
            hi = logf.astype(BF16)
            lo = (logf - hi.astype(F32)).astype(BF16)
            cum = _dot(upto, hi) + _dot(upto, lo) + carry
            o_ref[rows, :] = cum
            carry = cum[CUM_BLK - 1:CUM_BLK]

    return pl.pallas_call(
        body, name=name, grid=(T // SEQ,),
        in_specs=[pl.BlockSpec((SEQ, LANES), lambda b: (b, 4)), pl.BlockSpec((1, LANES), lambda b: (0, 0))],
        out_specs=pl.BlockSpec((SEQ, LANES), lambda b: (b, 0)), out_shape=jax.ShapeDtypeStruct((T, LANES), F32),
        compiler_params=_params("parallel"),
    )(aux, bf)


def fox_prep_bwd(aux, bf, dcum_query, dcum_key, *, name):
    T = aux.shape[0]

    def body(f_ref, b_ref, dq_ref, dk_ref, df_ref, db_ref):
        row = lax.broadcasted_iota(jnp.int32, (CUM_BLK, CUM_BLK), 0)
        col = lax.broadcasted_iota(jnp.int32, (CUM_BLK, CUM_BLK), 1)
        onward = (col >= row).astype(BF16)
        carry = jnp.zeros((1, LANES), F32)
        tot = jnp.zeros((1, LANES), F32)
        for n in range(SEQ // CUM_BLK - 1, -1, -1):
            rows = slice(n * CUM_BLK, (n + 1) * CUM_BLK)
            d = dq_ref[rows, :] + dk_ref[rows, :]
            hi = d.astype(BF16)
            lo = (d - hi.astype(F32)).astype(BF16)
            dlogf = _dot(onward, hi) + _dot(onward, lo) + carry
            carry = dlogf[0:1]
            df = dlogf * jax.nn.sigmoid(-(f_ref[rows, :] + b_ref[...]))
            df_ref[rows, :] = df
            tot = tot + jnp.sum(df, axis=0, keepdims=True)

        @pl.when(pl.program_id(0) == 0)
        def _():
            db_ref[...] = tot

        @pl.when(pl.program_id(0) > 0)
        def _():
            db_ref[...] += tot

    blk = pl.BlockSpec((SEQ, LANES), lambda b: (b, 0))
    vec = pl.BlockSpec((1, LANES), lambda b: (0, 0))
    sd = jax.ShapeDtypeStruct
    return pl.pallas_call(
        body, name=name, grid=(T // SEQ,),
        in_specs=[pl.BlockSpec((SEQ, LANES), lambda b: (b, 4)), vec, blk, blk],
        out_specs=[blk, vec], out_shape=[sd((T, LANES), F32), sd((1, LANES), F32)],
        compiler_params=_params("arbitrary"),
    )(aux, bf, dcum_query, dcum_key)


CHUNK = 256
WIDE = N_HEADS * CHUNK
NCH = SEQ // CHUNK


def _seg(h):
    return slice(h * CHUNK, (h + 1) * CHUNK)


def _chunk_rows(c):
    return pl.ds(pl.multiple_of(c * CHUNK, CHUNK), CHUNK)


def _wide_consts():
    r = lax.broadcasted_iota(jnp.int32, (WIDE, GROUP_W), 0)
    f = lax.broadcasted_iota(jnp.int32, (WIDE, GROUP_W), 1)
    bd = (r // CHUNK) == (f // HEAD_DIM)
    row = lax.broadcasted_iota(jnp.int32, (BLOCK, WIDE), 0)
    key = lax.broadcasted_iota(jnp.int32, (BLOCK, WIDE), 1) % CHUNK
    return bd, row, key


def _block_diag(x, bd):
    return jnp.where(bd, jnp.concatenate([x] * N_HEADS, axis=0), jnp.zeros((), x.dtype))


def _fold_heads(w, bd):
    w = jnp.where(bd, w, 0.0)
    return (w[0:CHUNK] + w[CHUNK:2 * CHUNK]) + (w[2 * CHUNK:3 * CHUNK] + w[3 * CHUNK:])


def _widen(cols):
    return jnp.concatenate([jnp.broadcast_to(c, (BLOCK, CHUNK)) for c in cols], axis=1)


def _head_rowsums(w):
    return [jnp.sum(w[:, _seg(h)], axis=1, keepdims=True) for h in HEADS]


def _tri_wide(x, tri):
    hi = x.astype(BF16)
    lo = (x - hi.astype(F32)).astype(BF16)
    y = _dot(jnp.concatenate([hi[:, _seg(h)] for h in HEADS] + [lo[:, _seg(h)] for h in HEADS], axis=0), tri)
    return jnp.concatenate([y[h * BLOCK:(h + 1) * BLOCK] + y[(N_HEADS + h) * BLOCK:(N_HEADS + h + 1) * BLOCK] for h in HEADS], axis=1)


def _feature_widen(cols):
    return jnp.concatenate([jnp.broadcast_to(c, (BLOCK, HEAD_DIM)) for c in cols], axis=1)


def _loop_by_two(n, index, body, carry):
    odd = n % 2
    carry = lax.fori_loop(0, odd, lambda _, cr: body(index(0), cr), carry)
    return lax.fori_loop(0, n // 2, lambda t, cr: body(index(odd + 2 * t + 1), body(index(odd + 2 * t), cr)), carry)


def _sbw_scores(q, kbd, later):
    z = _dot(q, kbd, _NT)
    lk = -_softplus(z)
    return z + lk, lk, _tri_wide(lk, later)


def _sbw_tile(q, kbd, mask, later, csum):
    z = _dot(q, kbd, _NT)
    lk = -_softplus(z)
    if mask is not None:
        lk = jnp.where(mask, lk, 0.0)
    e = z + lk
    att = jnp.exp(e + _tri_wide(lk, later) + csum)
    if mask is not None:
        att = jnp.where(mask, att, 0.0)
    return att, e, lk


def sbw_fwd(qkv, *, name):
    T = qkv.shape[0]

    def body(q_ref, k_ref, v_ref, o_ref):
        bd, row, key = _wide_consts()
        r2 = lax.broadcasted_iota(jnp.int32, (CHUNK, CHUNK), 0)
        c2 = lax.broadcasted_iota(jnp.int32, (CHUNK, CHUNK), 1)
        later = (r2 > c2).astype(BF16)

        def qblock(i, _):
            q = q_ref[_blk(i), :] * Q_SCALE
            cd = i // 2
            strict = key < row + BLOCK * (i % 2)

            def tile(c, mask, carry):
                acc, csum = carry
                att, _, lk = _sbw_tile(q, _block_diag(k_ref[_chunk_rows(c), :], bd), mask, later, csum)
                acc = acc + _dot(att.astype(BF16), _block_diag(v_ref[_chunk_rows(c), :], bd))
                return acc, csum + _widen(_head_rowsums(lk))

            def two_tiles(c1, carry):
                acc, csum = carry
                e1, lk1, t1 = _sbw_scores(q, _block_diag(k_ref[_chunk_rows(c1), :], bd), later)
                e2, lk2, t2 = _sbw_scores(q, _block_diag(k_ref[_chunk_rows(c1 - 1), :], bd), later)
                att1 = jnp.exp(e1 + t1 + csum)
                csum = csum + _widen(_head_rowsums(lk1))
                att2 = jnp.exp(e2 + t2 + csum)
                csum = csum + _widen(_head_rowsums(lk2))
                acc = acc + _dot(att1.astype(BF16), _block_diag(v_ref[_chunk_rows(c1), :], bd))
                acc = acc + _dot(att2.astype(BF16), _block_diag(v_ref[_chunk_rows(c1 - 1), :], bd))
                return acc, csum

            carry = tile(cd, strict, (jnp.zeros((BLOCK, GROUP_W), F32), jnp.zeros((BLOCK, WIDE), F32)))
            odd = cd % 2
            carry = lax.fori_loop(0, odd, lambda n, cr: tile(cd - 1, None, cr), carry)
            acc, _ = lax.fori_loop(0, cd // 2, lambda n, cr: two_tiles(cd - 1 - odd - 2 * n, cr), carry)
            o_ref[_blk(i), :] = acc.astype(BF16)
            return 0

        lax.fori_loop(0, NB, qblock, 0)

    return pl.pallas_call(
        body, name=name, grid=(T // SEQ,), in_specs=_qkv_specs(0),
        out_specs=pl.BlockSpec((SEQ, GROUP_W), lambda b: (b, 0)), out_shape=jax.ShapeDtypeStruct((T, GROUP_W), BF16),
        compiler_params=_params("parallel"),
    )(qkv, qkv, qkv)


def sbw_bwd(qkv, dmixed, *, name):
    T = qkv.shape[0]

    def body(q_ref, k_ref, v_ref, do_ref, dq_ref, dk_ref, dv_ref, att_s, sg_s):
        bd, row, key = _wide_consts()
        r2 = lax.broadcasted_iota(jnp.int32, (CHUNK, CHUNK), 0)
        c2 = lax.broadcasted_iota(jnp.int32, (CHUNK, CHUNK), 1)
        later = (r2 > c2).astype(BF16)
        earlier = (r2 < c2).astype(BF16)
        dk_ref[...] = jnp.zeros_like(dk_ref)
        dv_ref[...] = jnp.zeros_like(dv_ref)

        def qblock(i, _):
            q = q_ref[_blk(i), :] * Q_SCALE
            do = do_ref[_blk(i), :].astype(BF16)
            cd = i // 2
            strict = key < row + BLOCK * (i % 2)

            def recompute(c, mask, csum):
                att, e, lk = _sbw_tile(q, _block_diag(k_ref[_chunk_rows(c), :], bd), mask, later, csum)
                sg = jnp.exp(e)
                att_s[c] = att
                sg_s[c] = sg if mask is None else jnp.where(mask, sg, 0.0)
                return csum + _widen(_head_rowsums(lk))

            def recompute_two(c1, csum):
                e1, lk1, t1 = _sbw_scores(q, _block_diag(k_ref[_chunk_rows(c1), :], bd), later)
                e2, lk2, t2 = _sbw_scores(q, _block_diag(k_ref[_chunk_rows(c1 - 1), :], bd), later)
                sg_s[c1] = jnp.exp(e1)
                sg_s[c1 - 1] = jnp.exp(e2)
                att_s[c1] = jnp.exp(e1 + t1 + csum)
                csum = csum + _widen(_head_rowsums(lk1))
                att_s[c1 - 1] = jnp.exp(e2 + t2 + csum)
                return csum + _widen(_head_rowsums(lk2))

            csum = recompute(cd, strict, jnp.zeros((BLOCK, WIDE), F32))
            odd = cd % 2
            csum = lax.fori_loop(0, odd, lambda n, cs: recompute(cd - 1, None, cs), csum)
            lax.fori_loop(0, cd // 2, lambda n, cs: recompute_two(cd - 1 - odd - 2 * n, cs), csum)

            def tile(c, carry):
                dq, pre = carry
                kbd = _block_diag(k_ref[_chunk_rows(c), :], bd)
                vbd = _block_diag(v_ref[_chunk_rows(c), :], bd)
                att = att_s[c]
                ds = _dot(do, vbd, _NT) * att
                dlk = ds + _tri_wide(ds, earlier) + pre
                dz = (ds - dlk * sg_s[c]).astype(BF16)
                dk_ref[_chunk_rows(c), :] += _fold_heads(_dot(dz, q, _TN), bd)
                dv_ref[_chunk_rows(c), :] += _fold_heads(_dot(att.astype(BF16), do, _TN), bd)
                return dq + _dot(dz, kbd), pre + _widen(_head_rowsums(ds))

            def two_tiles(c1, carry):
                dq, pre = carry
                c2 = c1 + 1
                kbd1, kbd2 = _block_diag(k_ref[_chunk_rows(c1), :], bd), _block_diag(k_ref[_chunk_rows(c2), :], bd)
                att1, att2 = att_s[c1], att_s[c2]
                ds1 = _dot(do, _block_diag(v_ref[_chunk_rows(c1), :], bd), _NT) * att1
                ds2 = _dot(do, _block_diag(v_ref[_chunk_rows(c2), :], bd), _NT) * att2
                tri1, tri2 = _tri_wide(ds1, earlier), _tri_wide(ds2, earlier)
                dv_ref[_chunk_rows(c1), :] += _fold_heads(_dot(att1.astype(BF16), do, _TN), bd)
                dv_ref[_chunk_rows(c2), :] += _fold_heads(_dot(att2.astype(BF16), do, _TN), bd)
                dz1 = (ds1 - (ds1 + tri1 + pre) * sg_s[c1]).astype(BF16)
                pre = pre + _widen(_head_rowsums(ds1))
                dz2 = (ds2 - (ds2 + tri2 + pre) * sg_s[c2]).astype(BF16)
                pre = pre + _widen(_head_rowsums(ds2))
                dk_ref[_chunk_rows(c1), :] += _fold_heads(_dot(dz1, q, _TN), bd)
                dk_ref[_chunk_rows(c2), :] += _fold_heads(_dot(dz2, q, _TN), bd)
                return dq + _dot(dz1, kbd1) + _dot(dz2, kbd2), pre

            n_tiles = cd + 1
            odd = n_tiles % 2
            carry = (jnp.zeros((BLOCK, GROUP_W), F32), jnp.zeros((BLOCK, WIDE), F32))
            carry = lax.fori_loop(0, odd, lambda n, cr: tile(0, cr), carry)
            dq, _ = lax.fori_loop(0, n_tiles // 2, lambda n, cr: two_tiles(odd + 2 * n, cr), carry)
            dq_ref[_blk(i), :] = dq * Q_SCALE
            return 0

        lax.fori_loop(0, NB, qblock, 0)

    out = pl.BlockSpec((SEQ, GROUP_W), lambda b: (b, 0))
    sd = jax.ShapeDtypeStruct((T, GROUP_W), F32)
    return pl.pallas_call(
        body, name=name, grid=(T // SEQ,), in_specs=_qkv_specs(0) + [out],
        out_specs=[out] * 3, out_shape=[sd] * 3,
        scratch_shapes=[pltpu.VMEM((NCH, BLOCK, WIDE), F32), pltpu.VMEM((NCH, BLOCK, WIDE), F32)],
        compiler_params=_params("parallel"),
    )(qkv, qkv, qkv, dmixed)


def _foxw_logits(q, kbd, cq, cr_ref, c, mask):
    ck = jnp.concatenate([cr_ref[h:h + 1, _chunk_rows(c)] for h in HEADS], axis=1)
    z = _dot(q, kbd, _NT) + cq - ck
    return z if mask is None else jnp.where(mask, z, NEG)


def foxw_fwd(qkv, cumc, cumr, *, name):
    T = qkv.shape[0]

    def body(q_ref, k_ref, v_ref, cc_ref, cr_ref, o_ref, o32_ref, lse_ref, z_s):
        bd, row, key = _wide_consts()
        lse_ref[...] = jnp.zeros_like(lse_ref)

        def qblock(i, _):
            q = q_ref[_blk(i), :] * Q_SCALE
            cq = _widen([cc_ref[_blk(i), h:h + 1] for h in HEADS])
            cd = i // 2
            causal = key <= row + BLOCK * (i % 2)

            def logits(c, mask, ms):
                z = _foxw_logits(q, _block_diag(k_ref[_chunk_rows(c), :], bd), cq, cr_ref, c, mask)
                z_s[c] = z
                return tuple(jnp.maximum(ms[h], jnp.max(z[:, _seg(h)], axis=1, keepdims=True)) for h in HEADS)

            ms = logits(cd, causal, (jnp.full((BLOCK, 1), NEG, F32),) * N_HEADS)
            ms = _loop_by_two(cd, lambda n: n, lambda c, m: logits(c, None, m), ms)
            m_wide = _widen(ms)

            def values(c, carry):
                acc, l = carry
                p = jnp.exp(z_s[c] - m_wide)
                return acc + _dot(p.astype(BF16), _block_diag(v_ref[_chunk_rows(c), :], bd)), l + _widen(_head_rowsums(p))

            acc, l = _loop_by_two(cd + 1, lambda n: n, values, (jnp.zeros((BLOCK, GROUP_W), F32), jnp.zeros((BLOCK, WIDE), F32)))
            ls = [l[:, h * CHUNK:h * CHUNK + 1] for h in HEADS]
            o = acc / _feature_widen(ls)
            o_ref[_blk(i), :] = o.astype(BF16)
            o32_ref[_blk(i), :] = o
            for h in HEADS:
                lse_ref[_blk(i), h:h + 1] = ms[h] + jnp.log(ls[h])
            return 0

        lax.fori_loop(0, NB, qblock, 0)

    out = pl.BlockSpec((SEQ, GROUP_W), lambda b: (b, 0))
    colb = pl.BlockSpec((SEQ, LANES), lambda b: (b, 0))
    sd = jax.ShapeDtypeStruct
    return pl.pallas_call(
        body, name=name, grid=(T // SEQ,),
        in_specs=_qkv_specs(3) + [colb, pl.BlockSpec((8, SEQ), lambda b: (b, 0))],
        out_specs=[out, out, colb], out_shape=[sd((T, GROUP_W), BF16), sd((T, GROUP_W), F32), sd((T, LANES), F32)],
        scratch_shapes=[pltpu.VMEM((NCH, BLOCK, WIDE), F32)],
        compiler_params=_params("parallel"),
    )(qkv, qkv, qkv, cumc, cumr)


def foxw_bwd(qkv, cumc, cumr, lse, o32, dmixed, *, name):
    T = qkv.shape[0]

    def body(q_ref, k_ref, v_ref, cc_ref, cr_ref, lse_ref, o_ref, do_ref, dq_ref, dk_ref, dv_ref, dcc_ref, dcr_ref):
        bd, row, key = _wide_consts()
        dk_ref[...] = jnp.zeros_like(dk_ref)
        dv_ref[...] = jnp.zeros_like(dv_ref)
        dcc_ref[...] = jnp.zeros_like(dcc_ref)
        dcr_ref[...] = jnp.zeros_like(dcr_ref)

        def qblock(i, _):
            q = q_ref[_blk(i), :] * Q_SCALE
            do = do_ref[_blk(i), :].astype(BF16)
            prod = do.astype(F32) * o_ref[_blk(i), :]
            delta = _widen([jnp.sum(prod[:, _hs(h)], axis=1, keepdims=True) for h in HEADS])
            cq = _widen([cc_ref[_blk(i), h:h + 1] for h in HEADS])
            lse_w = _widen([lse_ref[_blk(i), h:h + 1] for h in HEADS])
            cd = i // 2
            causal = key <= row + BLOCK * (i % 2)

            def tile(c, mask, carry):
                dq, dcq = carry
                kbd = _block_diag(k_ref[_chunk_rows(c), :], bd)
                vbd = _block_diag(v_ref[_chunk_rows(c), :], bd)
                p = jnp.exp(_foxw_logits(q, kbd, cq, cr_ref, c, mask) - lse_w)
                ds = p * (_dot(do, vbd, _NT) - delta)
                dsb = ds.astype(BF16)
                dk_ref[_chunk_rows(c), :] += _fold_heads(_dot(dsb, q, _TN), bd)
                dv_ref[_chunk_rows(c), :] += _fold_heads(_dot(p.astype(BF16), do, _TN), bd)
                for h in HEADS:
                    dcr_ref[h:h + 1, _chunk_rows(c)] -= jnp.sum(ds[:, _seg(h)], axis=0, keepdims=True)
                return dq + _dot(dsb, kbd), dcq + _widen(_head_rowsums(ds))

            def two_tiles(c1, carry):
                dq, dcq = carry
                cs = (c1, c1 + 1)
                kbds = [_block_diag(k_ref[_chunk_rows(c), :], bd) for c in cs]
                vbds = [_block_diag(v_ref[_chunk_rows(c), :], bd) for c in cs]
                ps = [jnp.exp(_foxw_logits(q, kbds[j], cq, cr_ref, cs[j], None) - lse_w) for j in range(2)]
                dss = [ps[j] * (_dot(do, vbds[j], _NT) - delta) for j in range(2)]
                dsbs = [d.astype(BF16) for d in dss]
                for j, c in enumerate(cs):
                    dk_ref[_chunk_rows(c), :] += _fold_heads(_dot(dsbs[j], q, _TN), bd)
                    dv_ref[_chunk_rows(c), :] += _fold_heads(_dot(ps[j].astype(BF16), do, _TN), bd)
                    for h in HEADS:
                        dcr_ref[h:h + 1, _chunk_rows(c)] -= jnp.sum(dss[j][:, _seg(h)], axis=0, keepdims=True)
                dq = dq + _dot(dsbs[0], kbds[0]) + _dot(dsbs[1], kbds[1])
                return dq, dcq + _widen(_head_rowsums(dss[0])) + _widen(_head_rowsums(dss[1]))

            carry = tile(cd, causal, (jnp.zeros((BLOCK, GROUP_W), F32), jnp.zeros((BLOCK, WIDE), F32)))
            odd = cd % 2
            carry = lax.fori_loop(0, odd, lambda n, cr: tile(0, None, cr), carry)
            dq, dcq = lax.fori_loop(0, cd // 2, lambda n, cr: two_tiles(odd + 2 * n, cr), carry)
            dq_ref[_blk(i), :] = dq * Q_SCALE
            for h in HEADS:
                dcc_ref[_blk(i), h:h + 1] = dcq[:, h * CHUNK:h * CHUNK + 1]
            return 0

        lax.fori_loop(0, NB, qblock, 0)

    out = pl.BlockSpec((SEQ, GROUP_W), lambda b: (b, 0))
    colb = pl.BlockSpec((SEQ, LANES), lambda b: (b, 0))
    rowb = pl.BlockSpec((8, SEQ), lambda b: (b, 0))
    sd = jax.ShapeDtypeStruct
    big = sd((T, GROUP_W), F32)
    return pl.pallas_call(
        body, name=name, grid=(T // SEQ,),
        in_specs=_qkv_specs(3) + [colb, rowb, colb, out, pl.BlockSpec((SEQ, GROUP_W), lambda b: (b, 1))],
        out_specs=[out, out, out, colb, rowb],
        out_shape=[big, big, big, sd((T, LANES), F32), sd((T // SEQ * 8, SEQ), F32)],
        compiler_params=_params("parallel"),
    )(qkv, qkv, qkv, cumc, cumr, lse, o32, dmixed)


BAND = 2 * BLOCK


def _t5_bucket_np(dist):
    n = np.maximum(dist, 0)
    max_exact = NUM_BUCKETS // 2
    nf = np.maximum(n, 1).astype(np.float32)
    large = max_exact + (np.log(nf / np.float32(max_exact)) / np.float32(math.log(MAX_DISTANCE / max_exact))
                         * np.float32(NUM_BUCKETS - max_exact)).astype(np.int32)
    large = np.minimum(large, NUM_BUCKETS - 1)
    return np.where(n < max_exact, n, large).astype(np.int32)


def _band_buckets():
    qi = np.arange(BLOCK)[:, None]
    ki = np.arange(BAND)[None, :]
    delta = np.clip(qi - ki + BLOCK, 0, BLOCK)
    return np.stack([_t5_bucket_np(delta * d) for d in DILATIONS])


def relbias_expand(rel, *, name):
    buckets = jnp.asarray(_band_buckets())
    n_pat = len(DILATIONS)

    def body(rel_ref, bk_ref, o_ref):
        for p in range(n_pat):
            bk = bk_ref[p]
            for h in range(N_HEADS):
                acc = jnp.zeros((BLOCK, BAND), F32)
                for b in range(NUM_BUCKETS):
                    acc = jnp.where(bk == b, rel_ref[b, h], acc)
                o_ref[p * N_HEADS + h] = acc

    return pl.pallas_call(
        body, name=name,
        in_specs=[pl.BlockSpec(memory_space=pltpu.SMEM), pl.BlockSpec(memory_space=pltpu.VMEM)],
        out_specs=pl.BlockSpec(memory_space=pltpu.VMEM),
        out_shape=jax.ShapeDtypeStruct((n_pat * N_HEADS, BLOCK, BAND), F32),
        compiler_params=_params(),
    )(rel, buckets)


def relbias_reduce(ds_all, *, name):
    buckets = jnp.asarray(_band_buckets())
    n_pat = len(DILATIONS)

    def body(ds_ref, bk_ref, o_ref):
        for b in range(NUM_BUCKETS):
            for h in range(N_HEADS):
                tot = jnp.float32(0.0)
                for p in range(n_pat):
                    tot = tot + jnp.sum(jnp.where(bk_ref[p] == b, ds_ref[p * N_HEADS + h], 0.0))
                o_ref[b, h] = tot

    return pl.pallas_call(
        body, name=name,
        in_specs=[pl.BlockSpec(memory_space=pltpu.VMEM), pl.BlockSpec(memory_space=pltpu.VMEM)],
        out_specs=pl.BlockSpec(memory_space=pltpu.SMEM),
        out_shape=jax.ShapeDtypeStruct((NUM_BUCKETS, N_HEADS), F32),
        compiler_params=_params(),
    )(ds_all, buckets)


def _band_valid_wide(first, row, key):
    inside = jnp.logical_and(key >= row, key <= row + BLOCK)
    return jnp.logical_and(inside, jnp.logical_or(jnp.logical_not(first), key >= BLOCK))


QKV_BLOCKS = 9


def _band_in_specs(d, pattern, has_prev):
    rows = BLOCK * d
    cur = lambda c: pl.BlockSpec((rows, GROUP_W), lambda tb, r: (tb, c))
    prev = lambda c: pl.BlockSpec((rows, GROUP_W), lambda tb, r: (jnp.maximum(tb - 1, 0), c))
    bias = pl.BlockSpec((N_HEADS, BLOCK, BAND), lambda tb, r: (pattern, 0, 0))
    return [cur(6), cur(7), cur(8)] + ([prev(7), prev(8)] if has_prev else []) + [bias]


def _classes_per_step(d):
    return min(d, 4)


def _step_classes(d):
    n = _classes_per_step(d)
    return [pl.program_id(1) * n + j for j in range(n)]


def _class_rows(d, cls):
    return pl.ds(cls, BLOCK, stride=d) if d > 1 else pl.ds(0, BLOCK)


def _halves_scratch(rows, n):
    return [pltpu.VMEM((2, rows, LANES), F32)] * n


def _stage(refs, scratch):
    @pl.when(pl.program_id(1) == 0)
    def _():
        for src, dst in zip(refs, scratch):
            dst[0] = src[:, :LANES].astype(F32)
            dst[1] = src[:, LANES:].astype(F32)


def _take_class(s, d, cls):
    rows = _class_rows(d, cls)
    return jnp.concatenate([s.at[0][rows, :], s.at[1][rows, :]], axis=1)


def _put_class(s, d, cls, x):
    rows = _class_rows(d, cls)
    s.at[0][rows, :] = x[:, :LANES]
    s.at[1][rows, :] = x[:, LANES:]


def _flush(scratch, refs, d):
    @pl.when(pl.program_id(1) == d // _classes_per_step(d) - 1)
    def _():
        for s, o in zip(scratch, refs):
            o[...] = jnp.concatenate([s[0], s[1]], axis=1)


def _band_operands(scratch, d, cls, has_prev):
    take = lambda s: _take_class(s, d, cls).astype(BF16)
    q = (_take_class(scratch[0], d, cls) * Q_SCALE).astype(BF16)
    if has_prev:
        k = jnp.concatenate([take(scratch[3]), take(scratch[1])], axis=0)
        v = jnp.concatenate([take(scratch[4]), take(scratch[2])], axis=0)
    else:
        k = jnp.concatenate([jnp.zeros((BLOCK, GROUP_W), BF16), take(scratch[1])], axis=0)
        v = jnp.concatenate([jnp.zeros((BLOCK, GROUP_W), BF16), take(scratch[2])], axis=0)
    return q, k, v


def _lane_columns(cols):
    lane = lax.broadcasted_iota(jnp.int32, (BLOCK, LANES), 1)
    out = jnp.zeros((BLOCK, LANES), F32)
    for h, c in enumerate(cols):
        out = jnp.where(lane == h, c, out)
    return out


def band_fwd(qkv, bias, pattern, *, name):
    T = qkv.shape[0]
    d = DILATIONS[pattern]
    rows_per_block = BLOCK * d
    seq_blocks = SEQ // rows_per_block
    has_prev = seq_blocks > 1
    n_in = 5 if has_prev else 3

    def body(*refs):
        ins, b_ref, o_ref, lse_ref = refs[:n_in], refs[n_in], refs[n_in + 1], refs[n_in + 2]
        staged, o_s = refs[n_in + 3:2 * n_in + 3], refs[2 * n_in + 3]
        bd, row, key = _wide_consts()
        valid = _band_valid_wide(pl.program_id(0) % seq_blocks == 0, row, key)
        _stage(ins, staged)
        bias_w = jnp.concatenate([b_ref[h] for h in HEADS], axis=1)
        for cls in _step_classes(d):
            q, k, v = _band_operands(staged, d, cls, has_prev)
            kbd, vbd = _block_diag(k, bd), _block_diag(v, bd)
            sc = jnp.where(valid, _dot(q, kbd, _NT) + bias_w, NEG)
            ms = [jnp.max(sc[:, _seg(h)], axis=1, keepdims=True) for h in HEADS]
            p = jnp.exp(sc - _widen(ms))
            ls = _head_rowsums(p)
            _put_class(o_s, d, cls, _dot(p.astype(BF16), vbd) / _feature_widen(ls))
            lse_ref[_class_rows(d, cls), :] = _lane_columns([ms[h] + jnp.log(ls[h]) for h in HEADS])
        _flush([o_s], [o_ref], d)

    sd = jax.ShapeDtypeStruct
    return pl.pallas_call(
        body, name=name, grid=(T // rows_per_block, d // _classes_per_step(d)), in_specs=_band_in_specs(d, pattern, has_prev),
        out_specs=[pl.BlockSpec((rows_per_block, GROUP_W), lambda tb, r: (tb, 0)),
                   pl.BlockSpec((rows_per_block, LANES), lambda tb, r: (tb, 0))],
        out_shape=[sd((T, GROUP_W), F32), sd((T, LANES), F32)],
        scratch_shapes=_halves_scratch(rows_per_block, n_in + 1),
        compiler_params=_params("parallel", "arbitrary"),
    )(*([qkv] * n_in), bias)


def band_bwd(qkv, bias, lse, do, dlse, pattern, *, name):
    T = qkv.shape[0]
    d = DILATIONS[pattern]
    rows_per_block = BLOCK * d
    seq_blocks = SEQ // rows_per_block
    has_prev = seq_blocks > 1
    n_in = 5 if has_prev else 3
    n_out = 5 if has_prev else 3

    def body(*refs):
        ins, b_ref, lse_ref, do_ref, dlse_ref = refs[:n_in], refs[n_in], refs[n_in + 1], refs[n_in + 2], refs[n_in + 3]
        outs = refs[n_in + 4:n_in + 4 + n_out]
        ds_ref = refs[n_in + 4 + n_out]
        scratch = refs[n_in + 5 + n_out:]
        staged, do_s, out_s = scratch[:n_in], scratch[n_in], scratch[n_in + 1:]
        first_step = jnp.logical_and(pl.program_id(0) == 0, pl.program_id(1) == 0)
        bd, row, key = _wide_consts()
        valid = _band_valid_wide(pl.program_id(0) % seq_blocks == 0, row, key)
        _stage(list(ins) + [do_ref], list(staged) + [do_s])
        bias_w = jnp.concatenate([b_ref[h] for h in HEADS], axis=1)
        ds = None
        for cls in _step_classes(d):
            q, k, v = _band_operands(staged, d, cls, has_prev)
            kbd, vbd = _block_diag(k, bd), _block_diag(v, bd)
            rows = _class_rows(d, cls)
            do = _take_class(do_s, d, cls).astype(BF16)
            lse_t, dlse_t = lse_ref[rows, :], dlse_ref[rows, :]
            lse_w = _widen([lse_t[:, h:h + 1] for h in HEADS])
            dlse_w = _widen([dlse_t[:, h:h + 1] for h in HEADS])
            p = jnp.where(valid, jnp.exp(_dot(q, kbd, _NT) + bias_w - lse_w), 0.0)
            dp = _dot(do, vbd, _NT)
            ds_c = p * (dp - _widen(_head_rowsums(p * dp)) + dlse_w)
            dsb, pb = ds_c.astype(BF16), p.astype(BF16)
            _put_class(out_s[0], d, cls, _dot(dsb, kbd) * Q_SCALE)
            dk = _fold_heads(_dot(dsb, q, _TN), bd)
            dv = _fold_heads(_dot(pb, do, _TN), bd)
            _put_class(out_s[1], d, cls, dk[BLOCK:])
            _put_class(out_s[2], d, cls, dv[BLOCK:])
            if has_prev:
                _put_class(out_s[3], d, cls, dk[:BLOCK])
                _put_class(out_s[4], d, cls, dv[:BLOCK])
            ds = ds_c if ds is None else ds + ds_c
        _flush(out_s, outs, d)

        @pl.when(first_step)
        def _():
            for h in HEADS:
                ds_ref[h] = ds[:, _seg(h)]

        @pl.when(jnp.logical_not(first_step))
        def _():
            for h in HEADS:
                ds_ref[h] += ds[:, _seg(h)]

    big = pl.BlockSpec((rows_per_block, GROUP_W), lambda tb, r: (tb, 0))
    colb = pl.BlockSpec((rows_per_block, LANES), lambda tb, r: (tb, 0))
    sd = jax.ShapeDtypeStruct
    return pl.pallas_call(
        body, name=name, grid=(T // rows_per_block, d // _classes_per_step(d)),
        in_specs=_band_in_specs(d, pattern, has_prev) + [colb, big, colb],
        out_specs=[big] * n_out + [pl.BlockSpec((N_HEADS, BLOCK, BAND), lambda tb, r: (0, 0, 0))],
        out_shape=[sd((T, GROUP_W), F32)] * n_out + [sd((N_HEADS, BLOCK, BAND), F32)],
        scratch_shapes=_halves_scratch(rows_per_block, n_in + 1 + n_out),
        compiler_params=_params("arbitrary", "arbitrary"),
    )(*([qkv] * n_in), bias, lse, do, dlse)


def _pattern_weights(lse_refs, h):
    ls = [r[:, h:h + 1] for r in lse_refs]
    mx = functools.reduce(jnp.maximum, ls)
    es = [jnp.exp(l - mx) for l in ls]
    tot = functools.reduce(lambda a, b: a + b, es)
    return [e / tot for e in es]


def dil_combine_fwd(outs, *, name):
    T = outs[0][0].shape[0]
    n = len(outs)
    tm = 512

    def body(*refs):
        o_refs, l_refs, out_ref = refs[:n], refs[n:2 * n], refs[2 * n]
        for h in range(N_HEADS):
            w = _pattern_weights(l_refs, h)
            acc = w[0] * o_refs[0][:, _hs(h)]
            for p in range(1, n):
                acc = acc + w[p] * o_refs[p][:, _hs(h)]
            out_ref[:, _hs(h)] = acc.astype(BF16)

    big = pl.BlockSpec((tm, GROUP_W), lambda i: (i, 0))
    colb = pl.BlockSpec((tm, LANES), lambda i: (i, 0))
    return pl.pallas_call(
        body, name=name, grid=(T // tm,), in_specs=[big] * n + [colb] * n,
        out_specs=big, out_shape=jax.ShapeDtypeStruct((T, GROUP_W), BF16),
        compiler_params=_params("parallel"),
    )(*[o for o, _ in outs], *[l for _, l in outs])


def dil_combine_bwd(outs, dmixed, *, name):
    T = outs[0][0].shape[0]
    n = len(outs)
    tm = 512

    def body(*refs):
        o_refs, l_refs, do_ref = refs[:n], refs[n:2 * n], refs[2 * n]
        do_refs, dl_refs = refs[2 * n + 1:3 * n + 1], refs[3 * n + 1:]
        for r in dl_refs:
            r[...] = jnp.zeros_like(r)
        for h in range(N_HEADS):
            w = _pattern_weights(l_refs, h)
            do = do_ref[:, _hs(h)]
            dw = [jnp.sum(do * o_refs[p][:, _hs(h)], axis=1, keepdims=True) for p in range(n)]
            mean = functools.reduce(lambda a, b: a + b, [w[p] * dw[p] for p in range(n)])
            for p in range(n):
                do_refs[p][:, _hs(h)] = w[p] * do
                dl_refs[p][:, h:h + 1] = w[p] * (dw[p] - mean)

    big = pl.BlockSpec((tm, GROUP_W), lambda i: (i, 0))
    colb = pl.BlockSpec((tm, LANES), lambda i: (i, 0))
    sd = jax.ShapeDtypeStruct
    res = pl.pallas_call(
        body, name=name, grid=(T // tm,),
        in_specs=[big] * n + [colb] * n + [pl.BlockSpec((tm, GROUP_W), lambda i: (i, 2))],
        out_specs=[big] * n + [colb] * n, out_shape=[sd((T, GROUP_W), F32)] * n + [sd((T, LANES), F32)] * n,
        compiler_params=_params("parallel"),
    )(*[o for o, _ in outs], *[l for _, l in outs], dmixed)
    return list(zip(res[:n], res[n:]))


def dilated_fwd(qkv, bias, tag):
    return [band_fwd(qkv, bias, p, name=f"{tag}_band_fwd{p}") for p in range(len(DILATIONS))]


def dilated_bwd(qkv, bias, outs, dmixed, tag):
    grads = dil_combine_bwd(outs, dmixed, name=f"{tag}_combine_bwd")
    parts, ds_all = [], []
    for p, d in enumerate(DILATIONS):
        (_, lse), (do, dlse) = outs[p], grads[p]
        res = band_bwd(qkv, bias, lse, do, dlse, p, name=f"{tag}_band_bwd{p}")
        parts.append((list(res[:-1]), d))
        ds_all.append(res[-1])
    return parts, jnp.concatenate(ds_all, axis=0)


def assemble_dqkv(d_sb, d_fox, d_dil, *, name):
    T = d_sb[0].shape[0]
    nb = T // BLOCK
    flat = list(d_sb) + list(d_fox)
    specs = [pl.BlockSpec((BLOCK, GROUP_W), lambda i: (i, 0))] * 6
    layout = []
    for arrs, shift in d_dil:
        layout.append((len(flat), len(arrs) > 3, shift))
        flat += arrs
        specs += [pl.BlockSpec((BLOCK, GROUP_W), lambda i: (i, 0))] * 3
        if len(arrs) > 3:
            specs += [pl.BlockSpec((BLOCK, GROUP_W), lambda i, s=shift: (jnp.minimum(i + s, nb - 1), 0))] * 2

    def body(*refs):
        o_ref = refs[-1]
        i = pl.program_id(0)
        for j in range(6):
            o_ref[:, j * GROUP_W:(j + 1) * GROUP_W] = refs[j][...].astype(BF16)
        acc = [None, None, None]
        for first, has_prev, shift in layout:
            for j in range(3):
                v = refs[first + j][...]
                if has_prev and j > 0:
                    v = v + (i + shift < nb).astype(F32) * refs[first + 2 + j][...]
                acc[j] = v if acc[j] is None else acc[j] + v
        for j in range(3):
            o_ref[:, (6 + j) * GROUP_W:(7 + j) * GROUP_W] = acc[j].astype(BF16)

    return pl.pallas_call(
        body, name=name, grid=(nb,), in_specs=specs,
        out_specs=pl.BlockSpec((BLOCK, QKV_BLOCKS * GROUP_W), lambda i: (i, 0)),
        out_shape=jax.ShapeDtypeStruct((T, QKV_BLOCKS * GROUP_W), BF16), compiler_params=_params("parallel"),
    )(*flat)


def sum_cast(arrs, dtype, *, name):
    R, C = arrs[0].shape
    tr = _largest_tile(R, 512, 16)
    n = len(arrs)

    def body(*refs):
        acc = refs[0][...].astype(F32)
        for r in refs[1:n]:
            acc = acc + r[...].astype(F32)
        refs[n][...] = acc.astype(dtype)

    blk = pl.BlockSpec((tr, C), lambda i: (i, 0))
    return pl.pallas_call(
        body, name=name, grid=(R // tr,), in_specs=[blk] * n, out_specs=blk, out_shape=jax.ShapeDtypeStruct((R, C), dtype),
        compiler_params=_params("parallel"),
    )(*arrs)


GRAD_WIRE = BF16


def _block_diag_halves(w):
    z = jnp.zeros((HEAD_DIM, HEAD_DIM), w.dtype)
    half = lambda a, b: jnp.concatenate([jnp.concatenate([a, z], axis=1), jnp.concatenate([z, b], axis=1)], axis=0)
    return jnp.stack([half(w[0], w[1]), half(w[2], w[3])]).astype(BF16)


def _diag_blocks(d):
    h = HEAD_DIM
    return jnp.stack([d[0, :h, :h], d[0, h:, h:], d[1, :h, :h], d[1, h:, h:]])


def layer_fwd(x, mem2d, W, P, bias, tag):
    s = {}
    s["x"] = x
    h1 = rmsnorm_fwd(x, P["norm_mix_g"], name=f"{tag}_norm_mix")
    qkv = matmul(h1, W["qkv"], out_dtype=BF16, name=f"{tag}_qkv")
    aux = matmul(h1, W["aux"], name=f"{tag}_aux")
    o_sb = sbw_fwd(qkv, name=f"{tag}_sb_fwd")
    cumc = fox_prep(aux, P["bf"], name=f"{tag}_fox_prep")
    cumr = col_to_row(cumc)
    o_fox, o_fox32, lse_fox = foxw_fwd(qkv, cumc, cumr, name=f"{tag}_fox_fwd")
    dil = dilated_fwd(qkv, bias, tag)
    o_dil = dil_combine_fwd(dil, name=f"{tag}_dil_combine")
    o_lru, h_lru = lru_fwd(aux, P["lru_conv_w"], P["lru_conv_b"], P["wa"], P["lru_b_a"], P["wx"], P["lru_b_x"],
                           P["lru_lambda"], name=f"{tag}_lru_fwd")
    mixed = jnp.concatenate([o_sb, o_fox, o_dil, o_lru], axis=1)
    if "rest" in W:
        W.update(W.pop("rest")(mixed))
    x1 = matmul(mixed, W["out"], residual=x, name=f"{tag}_out")
    hq = rmsnorm_fwd(x1, P["norm_cross_g"], name=f"{tag}_norm_cross")
    qc = matmul(hq, W["cq"], out_dtype=BF16, name=f"{tag}_cq")
    memn = rmsnorm_fwd(mem2d, P["norm_mem_g"], name=f"{tag}_norm_mem")
    kv = matmul(memn, W["ckv"], out_dtype=BF16, name=f"{tag}_ckv")
    oc = cross_fwd(qc, kv, name=f"{tag}_cross_fwd")
    x2 = matmul(oc, W["coT"], trans_b=True, residual=x1, name=f"{tag}_co")
    h2 = rmsnorm_fwd(x2, P["norm_ffn_g"], name=f"{tag}_norm_ffn")
    hu = matmul(h2, W["up_u"], trans_b=True, name=f"{tag}_up_u")
    hg = matmul(h2, W["up_g"], trans_b=True, name=f"{tag}_up_g")
    act = glu_fwd(hu, hg, P["wu"], P["wg"], P["bu"], P["bg"], name=f"{tag}_glu_fwd")
    x3 = matmul(act, W["down"], residual=x2, name=f"{tag}_down")
    s.update(h1=h1, qkv=qkv, aux=aux, cumc=cumc, cumr=cumr, lse_fox=lse_fox, o_fox32=o_fox32, dil=dil, h_lru=h_lru, mixed=mixed,
             x1=x1, hq=hq, qc=qc, memn=memn, kv=kv, oc=oc, x2=x2, h2=h2, hu=hu, hg=hg, act=act)
    return x3, s


def layer_bwd(dx3, mem2d, W, P, bias, s, tag, hooks=None):
    mm = functools.partial(matmul, out_dtype=GRAD_WIRE, trans_a=True)
    gW, gP = {}, {}
    hooks = hooks or {}
    dact = matmul(dx3, W["down"], trans_b=True, name=f"{tag}_d_act")
    gW["down"] = mm(s["act"], dx3, name=f"{tag}_g_down")
    dhu, dhg, dwu, dwg, dbu, dbg = glu_bwd(s["hu"], s["hg"], dact, P["wu"], P["wg"], P["bu"], P["bg"], name=f"{tag}_glu_bwd")
    gP["ffn_conv_w"] = jnp.concatenate([dwu, dwg], axis=1)
    gP["ffn_conv_b"] = jnp.concatenate([dbu, dbg], axis=1)
    dh2 = matmul(dhu, W["up_u"], name=f"{tag}_d_h2u")
    dh2 = matmul(dhg, W["up_g"], residual=dh2, name=f"{tag}_d_h2g")
    gW["up_u"] = mm(dhu, s["h2"], name=f"{tag}_g_up_u")
    gW["up_g"] = mm(dhg, s["h2"], name=f"{tag}_g_up_g")
    dx2, gP["norm_ffn_g"] = rmsnorm_bwd(s["x2"], P["norm_ffn_g"], dh2, dx3, name=f"{tag}_norm_ffn_bwd")
    if "ffn" in hooks:
        hooks["ffn"](gW, W, s)
    doc = matmul(dx2, W["coT"], name=f"{tag}_d_oc")
    gW["coT"] = mm(dx2, s["oc"], name=f"{tag}_g_co")
    dqc, dkv = cross_bwd(s["qc"], s["kv"], doc, name=f"{tag}_cross_bwd")
    dhq = matmul(dqc, W["cq"], trans_b=True, name=f"{tag}_d_hq")
    gW["cq"] = mm(s["hq"], dqc, name=f"{tag}_g_cq")
    dmemn = matmul(dkv, W["ckv"], trans_b=True, name=f"{tag}_d_memn")
    gW["ckv"] = mm(s["memn"], dkv, name=f"{tag}_g_ckv")
    _, gP["norm_mem_g"] = rmsnorm_bwd(mem2d, P["norm_mem_g"], dmemn, None, name=f"{tag}_norm_mem_bwd")
    dx1, gP["norm_cross_g"] = rmsnorm_bwd(s["x1"], P["norm_cross_g"], dhq, dx2, name=f"{tag}_norm_cross_bwd")
    dmixed = matmul(dx1, W["out"], trans_b=True, name=f"{tag}_d_mixed")
    gW["out"] = mm(s["mixed"], dx1, name=f"{tag}_g_out")
    if "mid" in hooks:
        hooks["mid"](gW, W, s)
    qkv, aux = s["qkv"], s["aux"]
    d_sb = sbw_bwd(qkv, dmixed, name=f"{tag}_sb_bwd")
    dfq, dfk, dfv, dcc, dcr = foxw_bwd(qkv, s["cumc"], s["cumr"], s["lse_fox"], s["o_fox32"], dmixed, name=f"{tag}_fox_bwd")
    df, dbf = fox_prep_bwd(aux, P["bf"], dcc, row_to_col(dcr), name=f"{tag}_fox_prep_bwd")
    gP["b_forget"] = dbf[0, :N_HEADS]
    d_dil, ds_band = dilated_bwd(qkv, bias, s["dil"], dmixed, tag)
    dlx, dlg, dcw, dcb, dwa, dba, dwx, dbx, dlam = lru_bwd(
        aux, s["h_lru"], dmixed, P["lru_conv_w"], P["lru_conv_b"], P["wa"], P["lru_b_a"], P["wx"], P["lru_b_x"],
        P["lru_lambda"], name=f"{tag}_lru_bwd")
    gP.update(lru_conv_w=dcw, lru_conv_b=dcb, lru_w_a=_diag_blocks(dwa), lru_b_a=dba, lru_w_x=_diag_blocks(dwx),
              lru_b_x=dbx, lru_lambda=dlam)
    dqkv = assemble_dqkv(d_sb, [dfq, dfk, dfv], d_dil, name=f"{tag}_dqkv")
    daux = jnp.concatenate([dlx, dlg, df], axis=1)
    dh1 = matmul(dqkv, W["qkv"], trans_b=True, name=f"{tag}_d_h1a")
    dh1 = matmul(daux, W["aux"], trans_b=True, residual=dh1, name=f"{tag}_d_h1b")
    gW["qkv"] = mm(s["h1"], dqkv, name=f"{tag}_g_qkv")
    gW["aux"] = mm(s["h1"], daux, name=f"{tag}_g_aux")
    dx, gP["norm_mix_g"] = rmsnorm_bwd(s["x"], P["norm_mix_g"], dh1, dx1, name=f"{tag}_norm_mix_bwd")
    return dx, gW, gP, ds_band


def local_step(x, mem, target, weights_of, Ps, rel_bias, final_norm_g, grads_done=None, bwd_hooks=None):
    B = x.shape[0]
    x2d = x.reshape(B * SEQ, D_MODEL)
    mem2d = mem.reshape(B * N_MEM, D_MODEL)
    bias = relbias_expand(rel_bias, name="relbias_expand")
    saved, Ws = [], []
    h = x2d
    for l in range(DEPTH):
        Ws.append(weights_of(l, h))
        h, s = layer_fwd(h, mem2d, Ws[l], Ps[l], bias, f"l{l}")
        saved.append(s)
    loss, dh, d_final = loss_head(h, final_norm_g, target.reshape(B * SEQ, D_MODEL), name="loss_head")
    gWs, gPs, ds_bands = [None] * DEPTH, [None] * DEPTH, []
    for l in range(DEPTH - 1, -1, -1):
        hooks = None if bwd_hooks is None else bwd_hooks(l)
        dh, gWs[l], gPs[l], ds = layer_bwd(dh, mem2d, Ws[l], Ps[l], bias, saved[l], f"l{l}", hooks)
        if grads_done is not None:
            grads_done(l, gWs[l])
        ds_bands.append(ds)
    d_rel = relbias_reduce(sum_cast([d.reshape(-1, BAND) for d in ds_bands], F32, name="ds_band_sum").reshape(-1, BLOCK, BAND),
                           name="relbias_reduce")
    return loss, dh.reshape(B, SEQ, D_MODEL), gWs, gPs, d_rel, d_final


def small_params(p, l):
    row = lambda name: p[name][l].reshape(1, -1)
    ffn_w, ffn_b = p["ffn_conv_w"][l], row("ffn_conv_b")
    return dict(
        norm_mix_g=row("norm_mix_g"), norm_cross_g=row("norm_cross_g"), norm_mem_g=row("norm_mem_g"), norm_ffn_g=row("norm_ffn_g"),
        bf=jnp.pad(row("b_forget"), ((0, 0), (0, LANES - N_HEADS))),
        lru_conv_w=p["lru_conv_w"][l], lru_conv_b=row("lru_conv_b"), wa=_block_diag_halves(p["lru_w_a"][l]), lru_b_a=row("lru_b_a"),
        wx=_block_diag_halves(p["lru_w_x"][l]), lru_b_x=row("lru_b_x"), lru_lambda=row("lru_lambda"),
        wu=ffn_w[:, :D_FF], wg=ffn_w[:, D_FF:], bu=ffn_b[:, :D_FF], bg=ffn_b[:, D_FF:])


def canonical_weights(w_in, w_out, w_cq, w_ck, w_cv, w_co, w_up, w_down):
    sb_fox, fox_f, rest = w_in[:, :6 * GROUP_W], w_in[:, 6 * GROUP_W:6 * GROUP_W + N_HEADS], w_in[:, 6 * GROUP_W + N_HEADS:]
    dil, lru = rest[:, :3 * GROUP_W], rest[:, 3 * GROUP_W:]
    pad = jnp.zeros((w_in.shape[0], AUX_W - 2 * GROUP_W - N_HEADS), w_in.dtype)
    return dict(qkv=jnp.concatenate([sb_fox, dil], axis=1), aux=jnp.concatenate([lru, fox_f, pad], axis=1), out=w_out,
                cq=w_cq, ckv=jnp.concatenate([w_ck, w_cv], axis=1), coT=w_co.T, upT=w_up.T, down=w_down)


def native_grads(g):
    qkv, aux = g["qkv"], g["aux"]
    a, b = 6 * GROUP_W, 6 * GROUP_W + N_HEADS
    w_in = jnp.zeros((qkv.shape[0], b + 5 * GROUP_W), qkv.dtype)
    w_in = w_in.at[:, :a].set(qkv[:, :a]).at[:, a:b].set(aux[:, 2 * GROUP_W:2 * GROUP_W + N_HEADS])
    w_in = w_in.at[:, b:b + 3 * GROUP_W].set(qkv[:, a:]).at[:, b + 3 * GROUP_W:].set(aux[:, :2 * GROUP_W])
    return (w_in, g["out"], g["cq"], g["ckv"][:, :GROUP_W], g["ckv"][:, GROUP_W:], g["coT"].T) + native_ffn_grads(g)


def native_ffn_grads(g):
    return (g["upT"].T, g["down"])


ANY = pl.BlockSpec(memory_space=pl.ANY)
VMEM_SPEC = pl.BlockSpec(memory_space=pltpu.VMEM)


def _place():
    x, y, c = lax.axis_index("x"), lax.axis_index("y"), lax.axis_index("c")
    other_chips = [(1 - x, y), (x, 1 - y), (1 - x, 1 - y)]
    return x, y, c, other_chips


def _gather_body(x_ref, out_ref, send_sems, recv_sems, local_sem):
    x, y, c, chips = _place()
    me, sibling = (x, y, c), (x, y, 1 - c)

    def slot(px, py, pc):
        return out_ref.at[4 * px + 2 * py + pc]

    def copy(k, block, to, src=None):
        return pltpu.make_async_remote_copy(
            src_ref=slot(*block) if src is None else src, dst_ref=slot(*block),
            send_sem=send_sems.at[k], recv_sem=recv_sems.at[k], device_id=to, device_id_type=MESH)

    if local_sem is not None:
        mine = pltpu.make_async_copy(x_ref, slot(*me), local_sem)
        mine.start()
    first = [copy(0, me, sibling, src=x_ref)]
    first += [copy(1 + j, me, (*chip, c), src=x_ref) for j, chip in enumerate(chips)]
    for cp in first:
        cp.start()
    passed = [copy(4 + j, (*chip, c), sibling) for j, chip in enumerate(chips)]
    for j, chip in enumerate(chips):
        copy(1 + j, (*chip, c), me).wait_recv()
        passed[j].start()
    copy(0, sibling, me).wait_recv()
    for j, chip in enumerate(chips):
        copy(4 + j, (*chip, 1 - c), me).wait_recv()
    for cp in first + passed:
        cp.wait_send()
    if local_sem is not None:
        mine.wait()


_GATHER_SEMS = [pltpu.SemaphoreType.DMA((7,)), pltpu.SemaphoreType.DMA((7,)), pltpu.SemaphoreType.DMA]


def allgather_hbm(shard, me, *, name):
    def body(x_ref, out_ref, done_ref, send_sems, recv_sems):
        _gather_body(x_ref, out_ref, send_sems, recv_sems, None)
        done_ref[...] = jnp.zeros_like(done_ref)

    others, done = pl.pallas_call(
        body, name=name, in_specs=[ANY], out_specs=[ANY, VMEM_SPEC],
        out_shape=[jax.ShapeDtypeStruct((N_DEV,) + shard.shape, shard.dtype), jax.ShapeDtypeStruct((8, LANES), F32)],
        scratch_shapes=_GATHER_SEMS[:2],
    )(shard)
    return lax.dynamic_update_slice(others, shard[None], (me, 0, 0)), done


def allgather_small(x, *, name, reduce=False):
    def body(x_ref, out_ref, second_ref, *sems):
        _gather_body(x_ref, out_ref, *sems)
        if reduce:
            acc = out_ref[0]
            for d in range(1, N_DEV):
                acc = acc + out_ref[d]
            second_ref[...] = acc
        else:
            second_ref[...] = jnp.zeros_like(second_ref)

    sd = jax.ShapeDtypeStruct
    return pl.pallas_call(
        body, name=name, in_specs=[VMEM_SPEC], out_specs=[VMEM_SPEC, VMEM_SPEC],
        out_shape=[sd((N_DEV,) + x.shape, x.dtype), sd(x.shape if reduce else (8, LANES), x.dtype)],
        scratch_shapes=_GATHER_SEMS, compiler_params=pltpu.CompilerParams(vmem_limit_bytes=VMEM_LIMIT_V7X),
    )(x)


N_CHIPS = 4


def pair_exchange(g, *, name):
    _, R, C = g.shape

    def body(g_ref, recv_ref, send_sems, recv_sems):
        x, y, c, _ = _place()
        sibling = (x, y, 1 - c)
        remote = [pltpu.make_async_remote_copy(
            src_ref=g_ref.at[2 * q + (1 - c)], dst_ref=recv_ref.at[q], send_sem=send_sems.at[q], recv_sem=recv_sems.at[q],
            device_id=sibling, device_id_type=MESH) for q in range(N_CHIPS)]
        for cp in remote:
            cp.start()
        for cp in remote:
            cp.wait_recv()
        for cp in remote:
            cp.wait_send()

    return pl.pallas_call(
        body, name=name, in_specs=[ANY], out_specs=ANY, out_shape=jax.ShapeDtypeStruct((N_CHIPS, R, C), g.dtype),
        scratch_shapes=[pltpu.SemaphoreType.DMA((N_CHIPS,))] * 2,
    )(g)


def chip_exchange(s, *, name):
    _, R, C = s.shape

    def body(s_ref, o0, o1, o2, send_sems, recv_sems):
        x, y, c, chips = _place()
        outs = (o0, o1, o2)
        copies = [pltpu.make_async_remote_copy(
            src_ref=s_ref.at[2 * cx + cy], dst_ref=outs[j], send_sem=send_sems.at[j], recv_sem=recv_sems.at[j],
            device_id=(cx, cy, c), device_id_type=MESH) for j, (cx, cy) in enumerate(chips)]
        for cp in copies:
            cp.start()
        for cp in copies:
            cp.wait_recv()
        for cp in copies:
            cp.wait_send()

    sd = jax.ShapeDtypeStruct((R, C), s.dtype)
    return pl.pallas_call(
        body, name=name, in_specs=[ANY], out_specs=[ANY] * 3, out_shape=[sd] * 3,
        scratch_shapes=[pltpu.SemaphoreType.DMA((3,)), pltpu.SemaphoreType.DMA((3,))],
    )(s)


HBM_SPEC = pl.BlockSpec(memory_space=pltpu.HBM)
SEM_SPEC = pl.BlockSpec(memory_space=pltpu.SEMAPHORE)
N_PEERS = N_DEV - 1


def _peers():
    x, y, c = lax.axis_index("x"), lax.axis_index("y"), lax.axis_index("c")
    flip = lambda v, bit: 1 - v if bit else v
    out = []
    for k in range(1, N_DEV):
        px, py, pc = flip(x, (k >> 2) & 1), flip(y, (k >> 1) & 1), flip(c, k & 1)
        out.append(((px, py, pc), 4 * px + 2 * py + pc))
    return out, 4 * x + 2 * y + c


def _peer_copies(src_ref, land_ref, send_sems, recv_sems, scatter, landing):
    peers, me = _peers()
    return [pltpu.make_async_remote_copy(
        src_ref=src_ref.at[idx] if scatter else src_ref, dst_ref=land_ref.at[me if landing == "mine" else idx],
        send_sem=send_sems.at[k], recv_sem=recv_sems.at[k], device_id=peer, device_id_type=MESH)
        for k, (peer, idx) in enumerate(peers)]


def exchange_start(src, scatter, *, name):
    shape = (N_DEV,) + src.shape[-2:]

    def body(src_ref, land_ref, send_sems, recv_sems, src_thru, land_thru, token):
        for cp in _peer_copies(src_ref, land_ref, send_sems, recv_sems, scatter, "mine"):
            cp.start()
        token[...] = jnp.zeros_like(token)

    sems = pltpu.SemaphoreType.DMA((N_PEERS,))
    return pl.pallas_call(
        body, name=name,
        out_shape=(sems, sems, pltpu.HBM(src.shape, src.dtype), pltpu.HBM(shape, src.dtype), jax.ShapeDtypeStruct((8, LANES), F32)),
        in_specs=(HBM_SPEC, HBM_SPEC), out_specs=(SEM_SPEC, SEM_SPEC, HBM_SPEC, HBM_SPEC, VMEM_SPEC),
        input_output_aliases={0: 2, 1: 3},
        compiler_params=pltpu.CompilerParams(has_side_effects=pltpu.SideEffectType.DATAFLOW_SIDE_EFFECTING),
    )(pltpu.with_memory_space_constraint(src, pltpu.HBM), pltpu.with_memory_space_constraint(lax.empty(shape, src.dtype), pltpu.HBM))


def exchange_wait(started, after, scatter, *, name):
    send_sems, recv_sems, src_thru, land_thru, _ = started

    def body(src_ref, land_ref, send_sems, recv_sems, after_ref, src_dead, got_ref):
        for cp in _peer_copies(src_ref, land_ref, send_sems, recv_sems, scatter, "theirs"):
            cp.wait_send()
            cp.wait_recv()

    return pl.pallas_call(
        body, name=name, out_shape=(pltpu.HBM(src_thru.shape, src_thru.dtype), pltpu.HBM(land_thru.shape, land_thru.dtype)),
        in_specs=(HBM_SPEC, HBM_SPEC, SEM_SPEC, SEM_SPEC, ANY), out_specs=(HBM_SPEC, HBM_SPEC), input_output_aliases={0: 0, 1: 1},
        compiler_params=pltpu.CompilerParams(has_side_effects=pltpu.SideEffectType.DATAFLOW_SIDE_EFFECTING),
    )(src_thru, land_thru, send_sems, recv_sems, after)[1]


def sum_blocks(blocks, *, name):
    n, R, C = blocks.shape
    tr = _largest_tile(R, 512, 16)

    def body(b_ref, o_ref):
        acc = b_ref[0].astype(F32)
        for d in range(1, n):
            acc = acc + b_ref[d].astype(F32)
        o_ref[...] = acc

    return pl.pallas_call(
        body, name=name, grid=(R // tr,),
        in_specs=[pl.BlockSpec((n, tr, C), lambda i: (0, i, 0))], out_specs=pl.BlockSpec((tr, C), lambda i: (i, 0)),
        out_shape=jax.ShapeDtypeStruct((R, C), F32), compiler_params=_params("parallel"),
    )(blocks)


WEIGHTS = ("norm_mix_g", "w_in", "b_forget", "lru_conv_w", "lru_conv_b", "lru_w_a", "lru_b_a", "lru_w_x", "lru_b_x", "lru_lambda",
           "w_out", "norm_cross_g", "norm_mem_g", "w_cq", "w_ck", "w_cv", "w_co", "norm_ffn_g", "w_up", "ffn_conv_w", "ffn_conv_b",
           "w_down", "rel_bias", "final_norm_g")
LARGE = ("w_in", "w_out", "w_cq", "w_ck", "w_cv", "w_co", "w_up", "w_down")
COLUMN_SPLIT_SMALL = ("lru_conv_w", "ffn_conv_w")
PACK = (("qkv", 128, 2304), ("aux", 128, 640), ("out", 128, 1024), ("cq", 128, 256), ("ckv", 128, 512), ("coT", 128, 256),
        ("upT", 704, 1024), ("down", 352, 1024))
PACK_W = 1024


def _pack_rows(parts):
    return jnp.concatenate([p.reshape(-1, PACK_W) for p in parts], axis=0)


def _pad_rows(flat, mult=8 * LANES):
    n = flat.shape[0]
    return jnp.pad(flat, (0, (-n) % mult)).reshape(-1, LANES)


def kernel(x, mem, norm_mix_g, w_in, b_forget, lru_conv_w, lru_conv_b, lru_w_a, lru_b_a, lru_w_x, lru_b_x, lru_lambda, w_out, norm_cross_g, norm_mem_g, w_cq, w_ck, w_cv, w_co, norm_ffn_g, w_up, ffn_conv_w, ffn_conv_b, w_down, rel_bias, final_norm_g, loss_target, m_norm_mix_g, m_w_in, m_b_forget, m_lru_conv_w, m_lru_conv_b, m_lru_w_a, m_lru_b_a, m_lru_w_x, m_lru_b_x, m_lru_lambda, m_w_out, m_norm_cross_g, m_norm_mem_g, m_w_cq, m_w_ck, m_w_cv, m_w_co, m_norm_ffn_g, m_w_up, m_ffn_conv_w, m_ffn_conv_b, m_w_down, m_rel_bias, m_final_norm_g, v_norm_mix_g, v_w_in, v_b_forget, v_lru_conv_w, v_lru_conv_b, v_lru_w_a, v_lru_b_a, v_lru_w_x, v_lru_b_x, v_lru_lambda, v_w_out, v_norm_cross_g, v_norm_mem_g, v_w_cq, v_w_ck, v_w_cv, v_w_co, v_norm_ffn_g, v_w_up, v_ffn_conv_w, v_ffn_conv_b, v_w_down, v_rel_bias, v_final_norm_g):
    w = dict(norm_mix_g=norm_mix_g, w_in=w_in, b_forget=b_forget, lru_conv_w=lru_conv_w, lru_conv_b=lru_conv_b, lru_w_a=lru_w_a,
             lru_b_a=lru_b_a, lru_w_x=lru_w_x, lru_b_x=lru_b_x, lru_lambda=lru_lambda, w_out=w_out, norm_cross_g=norm_cross_g,
             norm_mem_g=norm_mem_g, w_cq=w_cq, w_ck=w_ck, w_cv=w_cv, w_co=w_co, norm_ffn_g=norm_ffn_g, w_up=w_up,
             ffn_conv_w=ffn_conv_w, ffn_conv_b=ffn_conv_b, w_down=w_down, rel_bias=rel_bias, final_norm_g=final_norm_g)
    m = dict(norm_mix_g=m_norm_mix_g, w_in=m_w_in, b_forget=m_b_forget, lru_conv_w=m_lru_conv_w, lru_conv_b=m_lru_conv_b,
             lru_w_a=m_lru_w_a, lru_b_a=m_lru_b_a, lru_w_x=m_lru_w_x, lru_b_x=m_lru_b_x, lru_lambda=m_lru_lambda, w_out=m_w_out,
             norm_cross_g=m_norm_cross_g, norm_mem_g=m_norm_mem_g, w_cq=m_w_cq, w_ck=m_w_ck, w_cv=m_w_cv, w_co=m_w_co,
             norm_ffn_g=m_norm_ffn_g, w_up=m_w_up, ffn_conv_w=m_ffn_conv_w, ffn_conv_b=m_ffn_conv_b, w_down=m_w_down,
             rel_bias=m_rel_bias, final_norm_g=m_final_norm_g)
    v = dict(norm_mix_g=v_norm_mix_g, w_in=v_w_in, b_forget=v_b_forget, lru_conv_w=v_lru_conv_w, lru_conv_b=v_lru_conv_b,
             lru_w_a=v_lru_w_a, lru_b_a=v_lru_b_a, lru_w_x=v_lru_w_x, lru_b_x=v_lru_b_x, lru_lambda=v_lru_lambda, w_out=v_w_out,
             norm_cross_g=v_norm_cross_g, norm_mem_g=v_norm_mem_g, w_cq=v_w_cq, w_ck=v_w_ck, w_cv=v_w_cv, w_co=v_w_co,
             norm_ffn_g=v_norm_ffn_g, w_up=v_w_up, ffn_conv_w=v_ffn_conv_w, ffn_conv_b=v_ffn_conv_b, w_down=v_w_down,
             rel_bias=v_rel_bias, final_norm_g=v_final_norm_g)
    me = 4 * lax.axis_index("x") + 2 * lax.axis_index("y") + lax.axis_index("c")

    conv_shard = jnp.concatenate([w[n].reshape(-1) for n in COLUMN_SPLIT_SMALL])
    conv_all, conv_gathered = allgather_small(_pad_rows(conv_shard), name="gather_conv")
    conv_all = conv_all.reshape(N_DEV, -1)
    full = dict(w)
    off = 0
    for n in COLUMN_SPLIT_SMALL:
        d, k, c = w[n].shape
        blocks = conv_all[:, off:off + d * k * c].reshape(N_DEV, d, k, c)
        full[n] = blocks.transpose(1, 2, 0, 3).reshape(d, k, N_DEV * c)
        off += d * k * c

    IN, MID, FFN = PACK[:2], PACK[2:6], PACK[6:]
    REST = MID + FFN

    def packed_shard(l, group):
        canon = canonical_weights(*[w[n][l] for n in LARGE])
        return _pack_rows([canon[k].astype(BF16) for k, _, _ in group])

    def unpack_weights(packed, group):
        W, row = {}, 0
        half = N_DEV // 2
        for k, r, c in group:
            n_rows = r * c // PACK_W
            rows = packed[:, row:row + n_rows]
            if k == "upT":
                W["up_u"], W["up_g"] = rows[:half].reshape(half * r, c), rows[half:].reshape(half * r, c)
            else:
                W[k] = rows.reshape(N_DEV * r, c)
            row += n_rows
        return W

    def packed_grads(gW, group):
        g = dict(gW)
        if "up_u" in g:
            g["upT"] = jnp.concatenate([g.pop("up_u"), g.pop("up_g")], axis=0)
        return jnp.concatenate([g[k].reshape(N_DEV, r * c // PACK_W, PACK_W) for k, r, c in group], axis=1)

    def unpack_grads(shard_sum, group):
        g, row = {}, 0
        for k, r, c in group:
            n_rows = r * c // PACK_W
            g[k] = shard_sum[row:row + n_rows].reshape(r, c)
            row += n_rows
        return g

    def own_block_in(landed, block):
        return lax.dynamic_update_slice(landed, block[None], (me, 0, 0))

    def gathered_weights(copies, shard, after, group, name):
        return unpack_weights(own_block_in(exchange_wait(copies, after, False, name=name), shard), group)

    def scattered_sum(src, copies, after, tag):
        landed = exchange_wait(copies, after, True, name=f"{tag}_wait")
        mine = lax.dynamic_index_in_dim(src, me, axis=0, keepdims=False)
        return sum_blocks(own_block_in(landed, mine), name=f"{tag}_sum")

    last = DEPTH - 1
    in0, gathered = allgather_hbm(packed_shard(0, IN) + conv_gathered[0, 0].astype(BF16), me, name="gather_weights")
    rest0_shard = packed_shard(0, REST) + gathered[0, 0].astype(BF16)
    gather_rest0 = exchange_start(rest0_shard, False, name="gather_rest0_start")
    last_shard = packed_shard(last, PACK) + gather_rest0[4][0, 0].astype(BF16)
    gather_last = exchange_start(last_shard, False, name="gather_last_start")
    started = gather_last[4][0, 0]
    layer_weights = {}

    def weights_of(l, h):
        if l == 0:
            W = unpack_weights(in0, IN)
            W["rest"] = lambda after: gathered_weights(gather_rest0, rest0_shard, after, REST, "gather_rest0_wait")
        else:
            assert l == last
            W = gathered_weights(gather_last, last_shard, h, PACK, "gather_last_wait")
        layer_weights[l] = W
        return W

    in_flight = {}

    def scatter(key, g_all, name):
        in_flight[key] = (g_all, exchange_start(g_all, True, name=name))
        return in_flight[key][1][4][0, 0].astype(BF16)

    def grads_done(l, gW):
        if l == last:
            W0 = layer_weights[0]
            W0["down"] = W0["down"] + scatter("last", packed_grads(gW, PACK), "grads_last_start")

    def ffn0_grads_done(gW, W, s):
        W["coT"] = W["coT"] + scatter("ffn0", packed_grads({k: gW[k] for k in ("up_u", "up_g", "down")}, FFN), "grads_ffn0_start")

    def mid0_grads_done(gW, W, s):
        s["cumc"] = s["cumc"] + scatter("mid0", packed_grads({k: gW[k] for k, _, _ in MID}, MID), "grads_mid0_start").astype(F32)

    Ps = [small_params(full, l) for l in range(DEPTH)]
    Ps[0]["norm_mix_g"] = Ps[0]["norm_mix_g"] + started
    loss, grad_x, gWs, gPs, d_rel, d_final = local_step(
        x, mem, loss_target, weights_of, Ps, rel_bias, final_norm_g.reshape(1, -1), grads_done,
        lambda l: {"ffn": ffn0_grads_done, "mid": mid0_grads_done} if l == 0 else None)

    shard_grads = {last: unpack_grads(scattered_sum(*in_flight["last"], grad_x, "grads_last"), PACK)}
    shard_grads[0] = unpack_grads(scattered_sum(*in_flight["ffn0"], grad_x, "grads_ffn0"), FFN)
    shard_grads[0].update(unpack_grads(scattered_sum(*in_flight["mid0"], grad_x, "grads_mid0"), MID))

    g_all = packed_grads({k: gWs[0][k] for k, _, _ in IN}, IN)
    rows = g_all.shape[1]
    got = pair_exchange(g_all, name="grads_pair_exchange")
    own = lax.dynamic_index_in_dim(g_all.reshape(N_CHIPS, 2, rows, PACK_W), lax.axis_index("c"), axis=1, keepdims=False)
    pair = sum_cast([own.reshape(-1, PACK_W), got.reshape(-1, PACK_W)], GRAD_WIRE, name="grads_pair_sum").reshape(N_CHIPS, rows, PACK_W)
    from_x, from_y, from_xy = chip_exchange(pair, name="grads_chip_exchange")
    mine = lax.dynamic_index_in_dim(pair, 2 * lax.axis_index("x") + lax.axis_index("y"), axis=0, keepdims=False)
    shard_grads[0].update(unpack_grads(sum_cast([mine, from_x, from_y, from_xy], F32, name="grads_chip_sum"), IN))

    grads = {}
    per_layer = [native_grads(shard_grads[l]) for l in range(DEPTH)]
    for i, n in enumerate(LARGE):
        grads[n] = jnp.stack([per_layer[l][i] for l in range(DEPTH)])

    small_names = [n for n in WEIGHTS if n not in LARGE and n not in ("rel_bias", "final_norm_g")]
    pieces = [gPs[l][n].reshape(-1) for n in small_names for l in range(DEPTH)] + [d_rel.reshape(-1), d_final.reshape(-1), loss[0, :1]]
    sizes = [p.shape[0] for p in pieces]
    _, total = allgather_small(_pad_rows(jnp.concatenate(pieces)), name="allreduce_small", reduce=True)
    total = total.reshape(-1)
    off, it = 0, iter(sizes)
    for n in small_names:
        per = []
        for l in range(DEPTH):
            sz = next(it)
            per.append(total[off:off + sz])
            off += sz
        full_shape = (DEPTH,) + full[n].shape[1:]
        gfull = jnp.stack(per).reshape(full_shape)
        if n in COLUMN_SPLIT_SMALL:
            c = w[n].shape[-1]
            gfull = lax.dynamic_slice_in_dim(gfull, me * c, c, axis=gfull.ndim - 1)
        grads[n] = gfull
    grads["rel_bias"] = total[off:off + rel_bias.size].reshape(rel_bias.shape)
    off += rel_bias.size
    grads["final_norm_g"] = total[off:off + D_MODEL]
    off += D_MODEL
    loss_out = total[off]

    delta, new_m, new_v = {}, {}, {}
    for n in LARGE:
        shape = w[n].shape
        two_d = lambda a: a.reshape(-1, shape[-1])
        d_, m_, v_ = adamw(two_d(w[n]), two_d(grads[n]), two_d(m[n]), two_d(v[n]), name=f"adamw_{n}")
        delta[n], new_m[n], new_v[n] = d_.reshape(shape), m_.reshape(shape), v_.reshape(shape)
    small_all = [n for n in WEIGHTS if n not in LARGE]
    two_d = lambda a: a.reshape(-1, a.shape[-1])
    d_, m_, v_ = adamw_many(*[[two_d(src[n]) for n in small_all] for src in (w, grads, m, v)], name="adamw_small")
    for i, n in enumerate(small_all):
        delta[n], new_m[n], new_v[n] = (a[i].reshape(w[n].shape) for a in (d_, m_, v_))

    return (loss_out, grad_x, *[grads[n] for n in WEIGHTS], *[delta[n] for n in WEIGHTS], *[new_m[n] for n in WEIGHTS],
            *[new_v[n] for n in WEIGHTS])
```
